```python
import jax, jax.numpy as jnp
from jax import lax
import numpy as np

D_MODEL = 1024
BATCH = 16
SEQ = 2048
DEPTH = 1

D_MIX = D_MODEL
D_RWKV = D_MIX // 2
D_CONV = D_MIX - D_RWKV
HEAD_SIZE = 64
N_RWKV_HEADS = D_RWKV // HEAD_SIZE
CONV_GROUPS = 8
CONV_WIDTH = 3
D_DECAY_LORA = 64
D_AAA_LORA = 64
D_GATE_LORA = 160
D_FF = -(-(8 * D_MODEL) // (3 * 256)) * 256
D_SHIFTED = 3 * D_RWKV + D_DECAY_LORA + D_AAA_LORA + D_GATE_LORA
D_IN = D_SHIFTED + 3 * D_CONV
LOG_DECAY_SCALE = 0.606531
RMS_EPS = 1e-6
GN_EPS = 64e-5
NORM_EPS = 1e-12

kernel_name = "hybrid_rwkv7_shortconv_encoder_block"


def _rmsnorm(x, w):
    x32 = x.astype(jnp.float32)
    y = x32 * lax.rsqrt(jnp.mean(x32 * x32, axis=-1, keepdims=True) + RMS_EPS)
    return (y * w.astype(jnp.float32)).astype(x.dtype)


def _token_shift(p, mu):
    zeros = jnp.zeros_like(p[:, :1])
    prev = jnp.concatenate([zeros, p[:, :-1]], axis=1)
    nxt = jnp.concatenate([p[:, 1:], zeros], axis=1)
    return p + mu * (0.5 * (prev + nxt) - p)


def _to_heads(t):
    b, s, _ = t.shape
    return t.reshape(b, s, N_RWKV_HEADS, HEAD_SIZE)


def _wkv7_scan(r, w, k, v, kk, a, reverse):
    b = r.shape[0]
    xs = tuple(jnp.swapaxes(t, 0, 1) for t in (r, w, k, v, kk, a))

    def step(state, inp):
        r_t, w_t, k_t, v_t, kk_t, a_t = inp
        sa = jnp.einsum('bhvk,bhk->bhv', state, kk_t)
        state = (state * w_t[:, :, None, :]
                 - sa[..., None] * (kk_t * a_t)[:, :, None, :]
                 + v_t[..., None] * k_t[:, :, None, :])
        y_t = jnp.einsum('bhvk,bhk->bhv', state, r_t)
        return state, y_t

    s0 = jnp.zeros((b, N_RWKV_HEADS, HEAD_SIZE, HEAD_SIZE), jnp.float32)
    _, ys = lax.scan(step, s0, xs, reverse=reverse)
    return jnp.swapaxes(ys, 0, 1)


def _rwkv7_direction(r, k, v, kk, xw, xa, w_up, w0, a_up, a0, k_a, r_k, reverse):
    logit_w = w0 + jnp.tanh(xw) @ w_up.astype(jnp.float32)
    w = jnp.exp(-LOG_DECAY_SCALE * jax.nn.sigmoid(logit_w))
    a = jax.nn.sigmoid(a0 + xa @ a_up.astype(jnp.float32))
    kd = k * (1.0 + (a - 1.0) * k_a)
    rh, kh, vh = _to_heads(r), _to_heads(kd), _to_heads(v)
    y = _wkv7_scan(rh, _to_heads(w), kh, vh, kk, _to_heads(a), reverse)
    bonus = jnp.sum(rh * kh * r_k, axis=-1, keepdims=True) * vh
    return y, bonus


def _rwkv7_mixer(r, k, v, xw, xa, xg, w_up_f, w0_f, w_up_b, w0_b, a_up_f, a0_f,
                 a_up_b, a0_b, g_up, k_k, k_a_f, k_a_b, r_k_f, r_k_b, gn_w, gn_b):
    out_dtype = r.dtype
    r, k, v, xw, xa, xg = (t.astype(jnp.float32) for t in (r, k, v, xw, xa, xg))
    kk = _to_heads(k * k_k)
    kk = kk / jnp.maximum(jnp.linalg.norm(kk, axis=-1, keepdims=True), NORM_EPS)
    y_f, bonus_f = _rwkv7_direction(r, k, v, kk, xw, xa, w_up_f, w0_f, a_up_f, a0_f,
                                    k_a_f, r_k_f, reverse=False)
    y_b, bonus_b = _rwkv7_direction(r, k, v, kk, xw, xa, w_up_b, w0_b, a_up_b, a0_b,
                                    k_a_b, r_k_b, reverse=True)
    y = y_f + y_b
    mean = jnp.mean(y, axis=-1, keepdims=True)
    var = jnp.mean(jnp.square(y - mean), axis=-1, keepdims=True)
    y = (y - mean) * lax.rsqrt(var + GN_EPS)
    y = y * gn_w.reshape(N_RWKV_HEADS, HEAD_SIZE) + gn_b.reshape(N_RWKV_HEADS, HEAD_SIZE)
    y = y + bonus_f + bonus_b
    g = jax.nn.sigmoid(xg) @ g_up.astype(jnp.float32)
    b, s = r.shape[:2]
    return (y.reshape(b, s, D_RWKV) * g).astype(out_dtype)


def _short_conv(pc, conv_w):
    gate_b, gate_c, h = jnp.split(pc, [D_CONV, 2 * D_CONV], axis=-1)
    u = gate_c * h
    u = lax.conv_general_dilated(
        u, conv_w.reshape(CONV_WIDTH, 1, D_CONV).astype(u.dtype),
        window_strides=(1,), padding=[((CONV_WIDTH - 1) // 2, (CONV_WIDTH - 1) // 2)],
        dimension_numbers=('NWC', 'WIO', 'NWC'), feature_group_count=D_CONV)
    return gate_b * u


def _swiglu(h, w_gate, w_up, w_down):
    return (jax.nn.silu(h @ w_gate) * (h @ w_up)) @ w_down


def _fwd_setup_inputs(seed: int = 0) -> dict:
    key = jax.random.key(seed)
    ks = jax.random.split(key, 32)

    def nrm(k, shape, scale):
        return jax.random.normal(k, shape, jnp.float32) * scale

    L = DEPTH
    return {
        "x": nrm(ks[0], (BATCH, SEQ, D_MODEL), 1.0),
        "norm1_w": 1.0 + nrm(ks[1], (L, D_MODEL), 0.02),
        "w_in": nrm(ks[2], (L, D_MODEL, D_IN), D_MODEL ** -0.5),
        "mu_shift": jax.random.uniform(ks[3], (L, D_SHIFTED), jnp.float32),
        "w_up_f": nrm(ks[4], (L, D_DECAY_LORA, D_RWKV), D_DECAY_LORA ** -0.5),
        "w0_f": nrm(ks[5], (L, D_RWKV), 1.0),
        "w_up_b": nrm(ks[6], (L, D_DECAY_LORA, D_RWKV), D_DECAY_LORA ** -0.5),
        "w0_b": nrm(ks[7], (L, D_RWKV), 1.0),
        "a_up_f": nrm(ks[8], (L, D_AAA_LORA, D_RWKV), D_AAA_LORA ** -0.5),
        "a0_f": nrm(ks[9], (L, D_RWKV), 0.5),
        "a_up_b": nrm(ks[10], (L, D_AAA_LORA, D_RWKV), D_AAA_LORA ** -0.5),
        "a0_b": nrm(ks[11], (L, D_RWKV), 0.5),
        "g_up": nrm(ks[12], (L, D_GATE_LORA, D_RWKV), D_GATE_LORA ** -0.5),
        "k_k": 0.85 + nrm(ks[13], (L, D_RWKV), 0.05),
        "k_a_f": 1.0 + nrm(ks[14], (L, D_RWKV), 0.05),
        "k_a_b": 1.0 + nrm(ks[15], (L, D_RWKV), 0.05),
        "r_k_f": nrm(ks[16], (L, N_RWKV_HEADS, HEAD_SIZE), 0.1),
        "r_k_b": nrm(ks[17], (L, N_RWKV_HEADS, HEAD_SIZE), 0.1),
        "gn_w": 1.0 + nrm(ks[18], (L, D_RWKV), 0.02),
        "gn_b": nrm(ks[19], (L, D_RWKV), 0.02),
        "conv_w": nrm(ks[20], (L, CONV_WIDTH, D_CONV), CONV_WIDTH ** -0.5),
        "w_out": nrm(ks[21], (L, D_MIX, D_MODEL), D_MIX ** -0.5),
        "norm2_w": 1.0 + nrm(ks[22], (L, D_MODEL), 0.02),
        "w_gate": nrm(ks[23], (L, D_MODEL, D_FF), D_MODEL ** -0.5),
        "w_up": nrm(ks[24], (L, D_MODEL, D_FF), D_MODEL ** -0.5),
        "w_down": nrm(ks[25], (L, D_FF, D_MODEL), D_FF ** -0.5),
        "norm_f_w": 1.0 + nrm(ks[26], (D_MODEL,), 0.02),
    }


def _fwd_reference(x, norm1_w, w_in, mu_shift, w_up_f, w0_f, w_up_b, w0_b, a_up_f, a0_f,
              a_up_b, a0_b, g_up, k_k, k_a_f, k_a_b, r_k_f, r_k_b, gn_w, gn_b, conv_w,
              w_out, norm2_w, w_gate, w_up, w_down, norm_f_w):
    split_pts = [D_RWKV, 2 * D_RWKV, 3 * D_RWKV, 3 * D_RWKV + D_DECAY_LORA,
                 3 * D_RWKV + D_DECAY_LORA + D_AAA_LORA]
    for l in range(DEPTH):
        h = _rmsnorm(x, norm1_w[l])
        p = h @ w_in[l]
        ps, pc = p[..., :D_SHIFTED], p[..., D_SHIFTED:]
        ps = _token_shift(ps, mu_shift[l])
        r, k, v, xw, xa, xg = jnp.split(ps, split_pts, axis=-1)
        o_rwkv = _rwkv7_mixer(r, k, v, xw, xa, xg, w_up_f[l], w0_f[l], w_up_b[l], w0_b[l],
                              a_up_f[l], a0_f[l], a_up_b[l], a0_b[l], g_up[l], k_k[l],
                              k_a_f[l], k_a_b[l], r_k_f[l], r_k_b[l], gn_w[l], gn_b[l])
        o_conv = _short_conv(pc, conv_w[l])
        x = x + jnp.concatenate([o_rwkv, o_conv], axis=-1) @ w_out[l]
        x = x + _swiglu(_rmsnorm(x, norm2_w[l]), w_gate[l], w_up[l], w_down[l])
    return _rmsnorm(x, norm_f_w)


import jax as _jax
import jax.numpy as _jnp

TWIN_FORMAT = 'train_step'
FWD_PARAMS = ['x', 'norm1_w', 'w_in', 'mu_shift', 'w_up_f', 'w0_f', 'w_up_b', 'w0_b', 'a_up_f', 'a0_f', 'a_up_b', 'a0_b', 'g_up', 'k_k', 'k_a_f', 'k_a_b', 'r_k_f', 'r_k_b', 'gn_w', 'gn_b', 'conv_w', 'w_out', 'norm2_w', 'w_gate', 'w_up', 'w_down', 'norm_f_w']
TWIN_WEIGHTS = ['norm1_w', 'w_in', 'mu_shift', 'w_up_f', 'w0_f', 'w_up_b', 'w0_b', 'a_up_f', 'a0_f', 'a_up_b', 'a0_b', 'g_up', 'k_k', 'k_a_f', 'k_a_b', 'r_k_f', 'r_k_b', 'gn_w', 'gn_b', 'conv_w', 'w_out', 'norm2_w', 'w_gate', 'w_up', 'w_down', 'norm_f_w']
TWIN_DIFF_INPUT = 'x'
TWIN_INPUTS = ['x', 'norm1_w', 'w_in', 'mu_shift', 'w_up_f', 'w0_f', 'w_up_b', 'w0_b', 'a_up_f', 'a0_f', 'a_up_b', 'a0_b', 'g_up', 'k_k', 'k_a_f', 'k_a_b', 'r_k_f', 'r_k_b', 'gn_w', 'gn_b', 'conv_w', 'w_out', 'norm2_w', 'w_gate', 'w_up', 'w_down', 'norm_f_w', 'loss_target', 'm_norm1_w', 'm_w_in', 'm_mu_shift', 'm_w_up_f', 'm_w0_f', 'm_w_up_b', 'm_w0_b', 'm_a_up_f', 'm_a0_f', 'm_a_up_b', 'm_a0_b', 'm_g_up', 'm_k_k', 'm_k_a_f', 'm_k_a_b', 'm_r_k_f', 'm_r_k_b', 'm_gn_w', 'm_gn_b', 'm_conv_w', 'm_w_out', 'm_norm2_w', 'm_w_gate', 'm_w_up', 'm_w_down', 'm_norm_f_w', 'v_norm1_w', 'v_w_in', 'v_mu_shift', 'v_w_up_f', 'v_w0_f', 'v_w_up_b', 'v_w0_b', 'v_a_up_f', 'v_a0_f', 'v_a_up_b', 'v_a0_b', 'v_g_up', 'v_k_k', 'v_k_a_f', 'v_k_a_b', 'v_r_k_f', 'v_r_k_b', 'v_gn_w', 'v_gn_b', 'v_conv_w', 'v_w_out', 'v_norm2_w', 'v_w_gate', 'v_w_up', 'v_w_down', 'v_norm_f_w']
TWIN_OUTPUTS = ['loss', 'grad_x', 'grad_norm1_w', 'grad_w_in', 'grad_mu_shift', 'grad_w_up_f', 'grad_w0_f', 'grad_w_up_b', 'grad_w0_b', 'grad_a_up_f', 'grad_a0_f', 'grad_a_up_b', 'grad_a0_b', 'grad_g_up', 'grad_k_k', 'grad_k_a_f', 'grad_k_a_b', 'grad_r_k_f', 'grad_r_k_b', 'grad_gn_w', 'grad_gn_b', 'grad_conv_w', 'grad_w_out', 'grad_norm2_w', 'grad_w_gate', 'grad_w_up', 'grad_w_down', 'grad_norm_f_w', 'delta_norm1_w', 'delta_w_in', 'delta_mu_shift', 'delta_w_up_f', 'delta_w0_f', 'delta_w_up_b', 'delta_w0_b', 'delta_a_up_f', 'delta_a0_f', 'delta_a_up_b', 'delta_a0_b', 'delta_g_up', 'delta_k_k', 'delta_k_a_f', 'delta_k_a_b', 'delta_r_k_f', 'delta_r_k_b', 'delta_gn_w', 'delta_gn_b', 'delta_conv_w', 'delta_w_out', 'delta_norm2_w', 'delta_w_gate', 'delta_w_up', 'delta_w_down', 'delta_norm_f_w', 'new_m_norm1_w', 'new_m_w_in', 'new_m_mu_shift', 'new_m_w_up_f', 'new_m_w0_f', 'new_m_w_up_b', 'new_m_w0_b', 'new_m_a_up_f', 'new_m_a0_f', 'new_m_a_up_b', 'new_m_a0_b', 'new_m_g_up', 'new_m_k_k', 'new_m_k_a_f', 'new_m_k_a_b', 'new_m_r_k_f', 'new_m_r_k_b', 'new_m_gn_w', 'new_m_gn_b', 'new_m_conv_w', 'new_m_w_out', 'new_m_norm2_w', 'new_m_w_gate', 'new_m_w_up', 'new_m_w_down', 'new_m_norm_f_w', 'new_v_norm1_w', 'new_v_w_in', 'new_v_mu_shift', 'new_v_w_up_f', 'new_v_w0_f', 'new_v_w_up_b', 'new_v_w0_b', 'new_v_a_up_f', 'new_v_a0_f', 'new_v_a_up_b', 'new_v_a0_b', 'new_v_g_up', 'new_v_k_k', 'new_v_k_a_f', 'new_v_k_a_b', 'new_v_r_k_f', 'new_v_r_k_b', 'new_v_gn_w', 'new_v_gn_b', 'new_v_conv_w', 'new_v_w_out', 'new_v_norm2_w', 'new_v_w_gate', 'new_v_w_up', 'new_v_w_down', 'new_v_norm_f_w']
TWIN_LEAF_KINDS = {'loss': 'loss', 'grad_x': 'grad_x', 'grad_norm1_w': 'grad_w', 'grad_w_in': 'grad_w', 'grad_mu_shift': 'grad_w', 'grad_w_up_f': 'grad_w', 'grad_w0_f': 'grad_w', 'grad_w_up_b': 'grad_w', 'grad_w0_b': 'grad_w', 'grad_a_up_f': 'grad_w', 'grad_a0_f': 'grad_w', 'grad_a_up_b': 'grad_w', 'grad_a0_b': 'grad_w', 'grad_g_up': 'grad_w', 'grad_k_k': 'grad_w', 'grad_k_a_f': 'grad_w', 'grad_k_a_b': 'grad_w', 'grad_r_k_f': 'grad_w', 'grad_r_k_b': 'grad_w', 'grad_gn_w': 'grad_w', 'grad_gn_b': 'grad_w', 'grad_conv_w': 'grad_w', 'grad_w_out': 'grad_w', 'grad_norm2_w': 'grad_w', 'grad_w_gate': 'grad_w', 'grad_w_up': 'grad_w', 'grad_w_down': 'grad_w', 'grad_norm_f_w': 'grad_w', 'delta_norm1_w': 'delta_w', 'delta_w_in': 'delta_w', 'delta_mu_shift': 'delta_w', 'delta_w_up_f': 'delta_w', 'delta_w0_f': 'delta_w', 'delta_w_up_b': 'delta_w', 'delta_w0_b': 'delta_w', 'delta_a_up_f': 'delta_w', 'delta_a0_f': 'delta_w', 'delta_a_up_b': 'delta_w', 'delta_a0_b': 'delta_w', 'delta_g_up': 'delta_w', 'delta_k_k': 'delta_w', 'delta_k_a_f': 'delta_w', 'delta_k_a_b': 'delta_w', 'delta_r_k_f': 'delta_w', 'delta_r_k_b': 'delta_w', 'delta_gn_w': 'delta_w', 'delta_gn_b': 'delta_w', 'delta_conv_w': 'delta_w', 'delta_w_out': 'delta_w', 'delta_norm2_w': 'delta_w', 'delta_w_gate': 'delta_w', 'delta_w_up': 'delta_w', 'delta_w_down': 'delta_w', 'delta_norm_f_w': 'delta_w', 'new_m_norm1_w': 'new_m', 'new_m_w_in': 'new_m', 'new_m_mu_shift': 'new_m', 'new_m_w_up_f': 'new_m', 'new_m_w0_f': 'new_m', 'new_m_w_up_b': 'new_m', 'new_m_w0_b': 'new_m', 'new_m_a_up_f': 'new_m', 'new_m_a0_f': 'new_m', 'new_m_a_up_b': 'new_m', 'new_m_a0_b': 'new_m', 'new_m_g_up': 'new_m', 'new_m_k_k': 'new_m', 'new_m_k_a_f': 'new_m', 'new_m_k_a_b': 'new_m', 'new_m_r_k_f': 'new_m', 'new_m_r_k_b': 'new_m', 'new_m_gn_w': 'new_m', 'new_m_gn_b': 'new_m', 'new_m_conv_w': 'new_m', 'new_m_w_out': 'new_m', 'new_m_norm2_w': 'new_m', 'new_m_w_gate': 'new_m', 'new_m_w_up': 'new_m', 'new_m_w_down': 'new_m', 'new_m_norm_f_w': 'new_m', 'new_v_norm1_w': 'new_v', 'new_v_w_in': 'new_v', 'new_v_mu_shift': 'new_v', 'new_v_w_up_f': 'new_v', 'new_v_w0_f': 'new_v', 'new_v_w_up_b': 'new_v', 'new_v_w0_b': 'new_v', 'new_v_a_up_f': 'new_v', 'new_v_a0_f': 'new_v', 'new_v_a_up_b': 'new_v', 'new_v_a0_b': 'new_v', 'new_v_g_up': 'new_v', 'new_v_k_k': 'new_v', 'new_v_k_a_f': 'new_v', 'new_v_k_a_b': 'new_v', 'new_v_r_k_f': 'new_v', 'new_v_r_k_b': 'new_v', 'new_v_gn_w': 'new_v', 'new_v_gn_b': 'new_v', 'new_v_conv_w': 'new_v', 'new_v_w_out': 'new_v', 'new_v_norm2_w': 'new_v', 'new_v_w_gate': 'new_v', 'new_v_w_up': 'new_v', 'new_v_w_down': 'new_v', 'new_v_norm_f_w': 'new_v'}


def _forward(args):
    return _fwd_reference(*[args[k] for k in FWD_PARAMS])


def _output_shape():
    out = _jax.eval_shape(lambda: _forward(_fwd_setup_inputs(0)))
    return out.shape, out.dtype

N_MICROBATCH = 1
ADAM_LR = 0.001
ADAM_B1 = 0.9
ADAM_B2 = 0.999
ADAM_EPS = 1e-08
ADAM_WD = 0.01
ADAM_STEP = 10
PER_EXAMPLE_BATCH_AXIS = {'x': 0, 'loss_target': 0}
SHARED_INPUTS = []
_WEIGHT_DTYPES = {'norm1_w': _jnp.float32, 'w_in': _jnp.float32, 'mu_shift': _jnp.float32, 'w_up_f': _jnp.float32, 'w0_f': _jnp.float32, 'w_up_b': _jnp.float32, 'w0_b': _jnp.float32, 'a_up_f': _jnp.float32, 'a0_f': _jnp.float32, 'a_up_b': _jnp.float32, 'a0_b': _jnp.float32, 'g_up': _jnp.float32, 'k_k': _jnp.float32, 'k_a_f': _jnp.float32, 'k_a_b': _jnp.float32, 'r_k_f': _jnp.float32, 'r_k_b': _jnp.float32, 'gn_w': _jnp.float32, 'gn_b': _jnp.float32, 'conv_w': _jnp.float32, 'w_out': _jnp.float32, 'norm2_w': _jnp.float32, 'w_gate': _jnp.float32, 'w_up': _jnp.float32, 'w_down': _jnp.float32, 'norm_f_w': _jnp.float32}
MOMENT_SCALE = {'norm1_w': 2.172700e-01, 'w_in': 1.189979e-01, 'mu_shift': 1.472170e-01, 'w_up_f': 6.079919e-03, 'w0_f': 2.507475e-02, 'w_up_b': 6.440523e-03, 'w0_b': 2.391323e-02, 'a_up_f': 1.664463e-02, 'a0_f': 2.453522e-02, 'a_up_b': 1.679868e-02, 'a0_b': 2.455795e-02, 'g_up': 8.433233e-02, 'k_k': 2.713494e-02, 'k_a_f': 6.817366e-02, 'k_a_b': 6.855244e-02, 'r_k_f': 1.249533e-01, 'r_k_b': 1.268901e-01, 'gn_w': 9.452427e-02, 'gn_b': 8.466934e-02, 'conv_w': 1.645493e-01, 'w_out': 1.193200e-01, 'norm2_w': 1.075734e-01, 'w_gate': 4.681836e-02, 'w_up': 4.533389e-02, 'w_down': 7.528998e-02, 'norm_f_w': 3.192850e+01}


def _to_microbatches(a, axis):
    t = _jnp.moveaxis(a, axis, 0)
    t = t.reshape((N_MICROBATCH, t.shape[0] // N_MICROBATCH) + t.shape[1:])
    return _jnp.moveaxis(t, 1, axis + 1)


def setup_inputs(seed: int = 0) -> dict:
    inp = _fwd_setup_inputs(seed)
    key = _jax.random.fold_in(_jax.random.key(seed), 7919)
    shape, _ = _output_shape()
    out = dict(inp)
    out["loss_target"] = _jax.random.normal(_jax.random.fold_in(key, 0), shape, _jnp.float32)
    for i, name in enumerate(TWIN_WEIGHTS):
        w = inp[name].astype(_jnp.float32)
        if MOMENT_SCALE is None:
            s = _jnp.sqrt(_jnp.mean(_jnp.square(w)) + 1e-30)
        else:
            s = MOMENT_SCALE[name]
        km, kv = _jax.random.split(_jax.random.fold_in(key, i + 1))
        out[name] = w
        out["m_" + name] = s * _jax.random.normal(km, w.shape, _jnp.float32)
        out["v_" + name] = (s * s) * _jax.random.uniform(kv, w.shape, _jnp.float32, 0.5, 1.5)
    if N_MICROBATCH > 1:
        for name, axis in PER_EXAMPLE_BATCH_AXIS.items():
            out[name] = _to_microbatches(out[name], axis)
    return {'x': out['x'], 'norm1_w': out['norm1_w'], 'w_in': out['w_in'], 'mu_shift': out['mu_shift'], 'w_up_f': out['w_up_f'], 'w0_f': out['w0_f'], 'w_up_b': out['w_up_b'], 'w0_b': out['w0_b'], 'a_up_f': out['a_up_f'], 'a0_f': out['a0_f'], 'a_up_b': out['a_up_b'], 'a0_b': out['a0_b'], 'g_up': out['g_up'], 'k_k': out['k_k'], 'k_a_f': out['k_a_f'], 'k_a_b': out['k_a_b'], 'r_k_f': out['r_k_f'], 'r_k_b': out['r_k_b'], 'gn_w': out['gn_w'], 'gn_b': out['gn_b'], 'conv_w': out['conv_w'], 'w_out': out['w_out'], 'norm2_w': out['norm2_w'], 'w_gate': out['w_gate'], 'w_up': out['w_up'], 'w_down': out['w_down'], 'norm_f_w': out['norm_f_w'], 'loss_target': out['loss_target'], 'm_norm1_w': out['m_norm1_w'], 'm_w_in': out['m_w_in'], 'm_mu_shift': out['m_mu_shift'], 'm_w_up_f': out['m_w_up_f'], 'm_w0_f': out['m_w0_f'], 'm_w_up_b': out['m_w_up_b'], 'm_w0_b': out['m_w0_b'], 'm_a_up_f': out['m_a_up_f'], 'm_a0_f': out['m_a0_f'], 'm_a_up_b': out['m_a_up_b'], 'm_a0_b': out['m_a0_b'], 'm_g_up': out['m_g_up'], 'm_k_k': out['m_k_k'], 'm_k_a_f': out['m_k_a_f'], 'm_k_a_b': out['m_k_a_b'], 'm_r_k_f': out['m_r_k_f'], 'm_r_k_b': out['m_r_k_b'], 'm_gn_w': out['m_gn_w'], 'm_gn_b': out['m_gn_b'], 'm_conv_w': out['m_conv_w'], 'm_w_out': out['m_w_out'], 'm_norm2_w': out['m_norm2_w'], 'm_w_gate': out['m_w_gate'], 'm_w_up': out['m_w_up'], 'm_w_down': out['m_w_down'], 'm_norm_f_w': out['m_norm_f_w'], 'v_norm1_w': out['v_norm1_w'], 'v_w_in': out['v_w_in'], 'v_mu_shift': out['v_mu_shift'], 'v_w_up_f': out['v_w_up_f'], 'v_w0_f': out['v_w0_f'], 'v_w_up_b': out['v_w_up_b'], 'v_w0_b': out['v_w0_b'], 'v_a_up_f': out['v_a_up_f'], 'v_a0_f': out['v_a0_f'], 'v_a_up_b': out['v_a_up_b'], 'v_a0_b': out['v_a0_b'], 'v_g_up': out['v_g_up'], 'v_k_k': out['v_k_k'], 'v_k_a_f': out['v_k_a_f'], 'v_k_a_b': out['v_k_a_b'], 'v_r_k_f': out['v_r_k_f'], 'v_r_k_b': out['v_r_k_b'], 'v_gn_w': out['v_gn_w'], 'v_gn_b': out['v_gn_b'], 'v_conv_w': out['v_conv_w'], 'v_w_out': out['v_w_out'], 'v_norm2_w': out['v_norm2_w'], 'v_w_gate': out['v_w_gate'], 'v_w_up': out['v_w_up'], 'v_w_down': out['v_w_down'], 'v_norm_f_w': out['v_norm_f_w']}


def _loss(weights, diff, rest, loss_target):
    with _jax.named_scope("forward"):
        args = {**rest, TWIN_DIFF_INPUT: diff, **{k: w.astype(_WEIGHT_DTYPES[k]) for k, w in weights.items()}}
        y = _forward(args)
    with _jax.named_scope("loss_head"):
        err = _jnp.square(y.astype(_jnp.float32) - loss_target)
        return 0.5 * _jnp.sum(_jnp.mean(err, axis=-1)) if err.ndim else 0.5 * err


def _adamw(w, g, m, v):
    m = ADAM_B1 * m + (1.0 - ADAM_B1) * g
    v = ADAM_B2 * v + (1.0 - ADAM_B2) * _jnp.square(g)
    m_hat = m / (1.0 - ADAM_B1 ** ADAM_STEP)
    v_hat = v / (1.0 - ADAM_B2 ** ADAM_STEP)
    delta = -ADAM_LR * (m_hat / (_jnp.sqrt(v_hat) + ADAM_EPS) + ADAM_WD * w)
    return delta, m, v


def reference(x, norm1_w, w_in, mu_shift, w_up_f, w0_f, w_up_b, w0_b, a_up_f, a0_f, a_up_b, a0_b, g_up, k_k, k_a_f, k_a_b, r_k_f, r_k_b, gn_w, gn_b, conv_w, w_out, norm2_w, w_gate, w_up, w_down, norm_f_w, loss_target, m_norm1_w, m_w_in, m_mu_shift, m_w_up_f, m_w0_f, m_w_up_b, m_w0_b, m_a_up_f, m_a0_f, m_a_up_b, m_a0_b, m_g_up, m_k_k, m_k_a_f, m_k_a_b, m_r_k_f, m_r_k_b, m_gn_w, m_gn_b, m_conv_w, m_w_out, m_norm2_w, m_w_gate, m_w_up, m_w_down, m_norm_f_w, v_norm1_w, v_w_in, v_mu_shift, v_w_up_f, v_w0_f, v_w_up_b, v_w0_b, v_a_up_f, v_a0_f, v_a_up_b, v_a0_b, v_g_up, v_k_k, v_k_a_f, v_k_a_b, v_r_k_f, v_r_k_b, v_gn_w, v_gn_b, v_conv_w, v_w_out, v_norm2_w, v_w_gate, v_w_up, v_w_down, v_norm_f_w):
    given = dict(x=x, norm1_w=norm1_w, w_in=w_in, mu_shift=mu_shift, w_up_f=w_up_f, w0_f=w0_f, w_up_b=w_up_b, w0_b=w0_b, a_up_f=a_up_f, a0_f=a0_f, a_up_b=a_up_b, a0_b=a0_b, g_up=g_up, k_k=k_k, k_a_f=k_a_f, k_a_b=k_a_b, r_k_f=r_k_f, r_k_b=r_k_b, gn_w=gn_w, gn_b=gn_b, conv_w=conv_w, w_out=w_out, norm2_w=norm2_w, w_gate=w_gate, w_up=w_up, w_down=w_down, norm_f_w=norm_f_w, loss_target=loss_target, m_norm1_w=m_norm1_w, m_w_in=m_w_in, m_mu_shift=m_mu_shift, m_w_up_f=m_w_up_f, m_w0_f=m_w0_f, m_w_up_b=m_w_up_b, m_w0_b=m_w0_b, m_a_up_f=m_a_up_f, m_a0_f=m_a0_f, m_a_up_b=m_a_up_b, m_a0_b=m_a0_b, m_g_up=m_g_up, m_k_k=m_k_k, m_k_a_f=m_k_a_f, m_k_a_b=m_k_a_b, m_r_k_f=m_r_k_f, m_r_k_b=m_r_k_b, m_gn_w=m_gn_w, m_gn_b=m_gn_b, m_conv_w=m_conv_w, m_w_out=m_w_out, m_norm2_w=m_norm2_w, m_w_gate=m_w_gate, m_w_up=m_w_up, m_w_down=m_w_down, m_norm_f_w=m_norm_f_w, v_norm1_w=v_norm1_w, v_w_in=v_w_in, v_mu_shift=v_mu_shift, v_w_up_f=v_w_up_f, v_w0_f=v_w0_f, v_w_up_b=v_w_up_b, v_w0_b=v_w0_b, v_a_up_f=v_a_up_f, v_a0_f=v_a0_f, v_a_up_b=v_a_up_b, v_a0_b=v_a0_b, v_g_up=v_g_up, v_k_k=v_k_k, v_k_a_f=v_k_a_f, v_k_a_b=v_k_a_b, v_r_k_f=v_r_k_f, v_r_k_b=v_r_k_b, v_gn_w=v_gn_w, v_gn_b=v_gn_b, v_conv_w=v_conv_w, v_w_out=v_w_out, v_norm2_w=v_norm2_w, v_w_gate=v_w_gate, v_w_up=v_w_up, v_w_down=v_w_down, v_norm_f_w=v_norm_f_w)
    weights = {n: given[n] for n in TWIN_WEIGHTS}
    shared = {n: given[n] for n in SHARED_INPUTS}
    per_example = {n: given[n] for n in ['x']}
    grad_fn = _jax.value_and_grad(_loss, argnums=(0, 1))

    def one_microbatch(ex, loss_target):
        ex = dict(ex)
        diff = ex.pop(TWIN_DIFF_INPUT)
        return grad_fn(weights, diff, {**shared, **ex}, loss_target)

    if N_MICROBATCH == 1:
        loss, (grad_w, grad_x) = one_microbatch(per_example, given["loss_target"])
    else:
        def body(carry, xs):
            loss_sum, grad_sum = carry
            l_k, (gw_k, gx_k) = one_microbatch(xs[0], xs[1])
            with _jax.named_scope("update"):
                return (loss_sum + l_k, _jax.tree.map(_jnp.add, grad_sum, gw_k)), gx_k

        init = (_jnp.zeros((), _jnp.float32), _jax.tree.map(_jnp.zeros_like, weights))
        (loss, grad_w), grad_x = _jax.lax.scan(body, init, (per_example, given["loss_target"]))
    with _jax.named_scope("update"):
        delta_w, new_m, new_v = {}, {}, {}
        for n in TWIN_WEIGHTS:
            delta_w[n], new_m[n], new_v[n] = _adamw(weights[n], grad_w[n], given["m_" + n], given["v_" + n])
    return (loss, grad_x, *[grad_w[n] for n in TWIN_WEIGHTS], *[delta_w[n] for n in TWIN_WEIGHTS],
            *[new_m[n] for n in TWIN_WEIGHTS], *[new_v[n] for n in TWIN_WEIGHTS])
```

```python
import functools

import jax
import jax.numpy as jnp
from jax import lax
from jax.experimental import pallas as pl
from jax.experimental.pallas import tpu as pltpu

F32 = jnp.float32
BF16 = jnp.bfloat16
MESH = pl.DeviceIdType.MESH

D_MODEL = 1024
D_RWKV = 512
HEAD = 64
N_PAIR = D_RWKV // (2 * HEAD)
D_LORA = 64
D_GATE = 160
D_GATE_PAD = 384
D_FF = 2816
D_SHIFT = 1824
D_SHIFT_PAD = 2048
D_CONV3 = 1536
LOG_DECAY_SCALE = 0.606531
RMS_EPS = 1e-6
GN_EPS = 64e-5
NORM_EPS = 1e-12
ADAM_LR, ADAM_B1, ADAM_B2, ADAM_EPS, ADAM_WD, ADAM_STEP = 0.001, 0.9, 0.999, 1e-08, 0.01, 10

N_SHARD = 4
N_DEV = 8
V7X_VMEM_LIMIT = 48 * 1024 * 1024
SCAN_CHUNK = 64
GROUP = 8

_PACK_ROWS = (("w_in", 840), ("w_out", 256), ("w_gate", 704), ("w_up", 704), ("w_down", 704),
              ("w_up_f", 8), ("w_up_b", 8), ("a_up_f", 8), ("a_up_b", 8), ("g_up", 20), ("conv_w", 1))
PACK_R = 3264
PACK_H = PACK_R // 2
SMALL_ROWS = 24


def _tile(n, cap, mult=128):
    best = None
    t = mult
    while t <= min(n, cap):
        if n % t == 0:
            best = t
        t += mult
    return best or n


def _cp(*sem):
    return pltpu.CompilerParams(dimension_semantics=sem or None, vmem_limit_bytes=V7X_VMEM_LIMIT)


def _sds(shape, dtype=F32):
    return jax.ShapeDtypeStruct(shape, dtype)


def _matmul(a, b, *, mode, name, out_dtype=F32, add=None):
    if mode == "tn":
        r, m = a.shape
        n = b.shape[1]
        tm, tn, tk = _tile(m, 512), _tile(n, 1536), _tile(r, 512, 8)
        nk = r // tk
        a_spec = pl.BlockSpec((tk, tm), lambda i, j, k: (k, i))
        b_spec = pl.BlockSpec((tk, tn), lambda i, j, k: (k, j))
        dims = (((0,), (0,)), ((), ()))
    else:
        m, kdim = a.shape
        n = b.shape[1] if mode == "nn" else b.shape[0]
        tm, tn, nk = _tile(m, 512, 8), _tile(n, 1536), 1
        a_spec = pl.BlockSpec((tm, kdim), lambda i, j, k: (i, 0))
        if mode == "nn":
            b_spec = pl.BlockSpec((kdim, tn), lambda i, j, k: (0, j))
            dims = (((1,), (0,)), ((), ()))
        else:
            b_spec = pl.BlockSpec((tn, kdim), lambda i, j, k: (j, 0))
            dims = (((1,), (1,)), ((), ()))
    has_add = add is not None

    def body(*refs):
        a_ref, b_ref = refs[0], refs[1]
        add_ref = refs[2] if has_add else None
        o_ref = refs[3] if has_add else refs[2]
        part = lax.dot_general(a_ref[...].astype(BF16), b_ref[...].astype(BF16), dims,
                               preferred_element_type=F32)
        if nk == 1:
            if has_add:
                part = part + add_ref[...]
            o_ref[...] = part.astype(out_dtype)
        else:
            acc_ref = refs[-1]
            k = pl.program_id(2)

            @pl.when(k == 0)
            def _():
                acc_ref[...] = jnp.zeros_like(acc_ref)

            acc_ref[...] += part

            @pl.when(k == nk - 1)
            def _():
                res = acc_ref[...]
                if has_add:
                    res = res + add_ref[...]
                o_ref[...] = res.astype(out_dtype)

    o_spec = pl.BlockSpec((tm, tn), lambda i, j, k: (i, j))
    in_specs = [a_spec, b_spec] + ([o_spec] if has_add else [])
    args = (a, b) + ((add,) if has_add else ())
    return pl.pallas_call(
        body, name=name, grid=(m // tm, n // tn, nk), in_specs=in_specs, out_specs=o_spec,
        out_shape=_sds((m, n), out_dtype),
        scratch_shapes=[pltpu.VMEM((tm, tn), F32)] if nk > 1 else [],
        compiler_params=_cp("parallel", "parallel", "arbitrary"),
    )(*args)


def _row(tm, width):
    return pl.BlockSpec((tm, width), lambda i: (i, 0))


def _fixed(shape):
    return pl.BlockSpec(shape, lambda i: tuple(0 for _ in shape))


def _rmsnorm_fwd(x, w, tm, name):
    t, d = x.shape

    def body(x_ref, w_ref, o_ref):
        xv = x_ref[...]
        rstd = lax.rsqrt(jnp.mean(xv * xv, axis=-1, keepdims=True) + RMS_EPS)
        o_ref[...] = (xv * rstd * w_ref[...]).astype(BF16)

    return pl.pallas_call(
        body, name=name, grid=(t // tm,), in_specs=[_row(tm, d), _fixed((1, d))], out_specs=_row(tm, d),
        out_shape=_sds((t, d), BF16), compiler_params=_cp("parallel"))(x, w)


def _rms_bwd_math(xv, wv, dyv):
    rstd = lax.rsqrt(jnp.mean(xv * xv, axis=-1, keepdims=True) + RMS_EPS)
    xhat = xv * rstd
    gv = dyv * wv
    dx = rstd * (gv - xhat * jnp.mean(gv * xhat, axis=-1, keepdims=True))
    return dx, jnp.sum(dyv * xhat, axis=0, keepdims=True)


def _rmsnorm_bwd(x, w, dy, dres, tm, name):
    t, d = x.shape

    def body(x_ref, w_ref, dy_ref, dres_ref, dx_ref, dw_ref):
        dx, dw = _rms_bwd_math(x_ref[...], w_ref[...], dy_ref[...])
        dx_ref[...] = dres_ref[...] + dx

        @pl.when(pl.program_id(0) == 0)
        def _():
            dw_ref[...] = jnp.zeros_like(dw_ref)

        dw_ref[...] += dw

    return pl.pallas_call(
        body, name=name, grid=(t // tm,),
        in_specs=[_row(tm, d), _fixed((1, d)), _row(tm, d), _row(tm, d)],
        out_specs=[_row(tm, d), _fixed((1, d))],
        out_shape=[_sds((t, d)), _sds((1, d))], compiler_params=_cp("arbitrary"))(x, w, dy, dres)


def _loss_head(x, w, target, tm):
    t, d = x.shape

    def body(x_ref, w_ref, t_ref, dx_ref, dw_ref, loss_ref):
        xv, wv = x_ref[...], w_ref[...]
        rstd = lax.rsqrt(jnp.mean(xv * xv, axis=-1, keepdims=True) + RMS_EPS)
        err = xv * rstd * wv - t_ref[...]
        dx, dw = _rms_bwd_math(xv, wv, err * (1.0 / d))
        dx_ref[...] = dx

        @pl.when(pl.program_id(0) == 0)
        def _():
            dw_ref[...] = jnp.zeros_like(dw_ref)
            loss_ref[...] = jnp.zeros_like(loss_ref)

        dw_ref[...] += dw
        loss_ref[...] += 0.5 * jnp.sum(jnp.mean(err * err, axis=-1, keepdims=True), axis=0, keepdims=True)

    return pl.pallas_call(
        body, name="loss_head", grid=(t // tm,),
        in_specs=[_row(tm, d), _fixed((1, d)), _row(tm, d)],
        out_specs=[_row(tm, d), _fixed((1, d)), _fixed((1, 1))],
        out_shape=[_sds((t, d)), _sds((1, d)), _sds((1, 1))], compiler_params=_cp("arbitrary"))(x, w, target)


def _swiglu_fwd(gate, up, tm):
    t, f = gate.shape

    def body(g_ref, u_ref, o_ref):
        gv = g_ref[...]
        o_ref[...] = (gv * jax.nn.sigmoid(gv) * u_ref[...]).astype(BF16)

    return pl.pallas_call(
        body, name="swiglu_fwd", grid=(t // tm,), in_specs=[_row(tm, f), _row(tm, f)], out_specs=_row(tm, f),
        out_shape=_sds((t, f), BF16), compiler_params=_cp("parallel"))(gate, up)


def _swiglu_bwd(gate, up, dact, tm):
    t, f = gate.shape

    def body(g_ref, u_ref, d_ref, dg_ref, du_ref):
        gv, uv, dv = g_ref[...], u_ref[...], d_ref[...]
        sg = jax.nn.sigmoid(gv)
        du_ref[...] = (dv * gv * sg).astype(BF16)
        dg_ref[...] = (dv * uv * (sg * (1.0 + gv * (1.0 - sg)))).astype(BF16)

    return pl.pallas_call(
        body, name="swiglu_bwd", grid=(t // tm,), in_specs=[_row(tm, f)] * 3, out_specs=[_row(tm, f)] * 2,
        out_shape=[_sds((t, f), BF16)] * 2, compiler_params=_cp("parallel"))(gate, up, dact)


def _halo_specs(tm, width, rows_total):
    per = tm // GROUP
    last = rows_total // GROUP - 1
    prev = pl.BlockSpec((GROUP, width), lambda i: (jnp.maximum(i * per - 1, 0), 0))
    nxt = pl.BlockSpec((GROUP, width), lambda i: (jnp.minimum((i + 1) * per, last), 0))
    return prev, nxt


def _edge_flags(tm, seq):
    i = pl.program_id(0)
    has_prev = jnp.where((i * tm) % seq == 0, 0.0, 1.0).astype(F32)
    has_next = jnp.where(((i + 1) * tm) % seq == 0, 0.0, 1.0).astype(F32)
    return has_prev, has_next


def _shifted(xv, prev_row, next_row):
    tm = xv.shape[0]
    row = lax.broadcasted_iota(jnp.int32, xv.shape, 0)
    down = jnp.where(row == 0, prev_row, pltpu.roll(xv, 1, axis=0))
    up = jnp.where(row == tm - 1, next_row, pltpu.roll(xv, tm - 1, axis=0))
    return down, up


def _shift_fwd(p, mu, tm, seq):
    t, w = p.shape
    prev_spec, next_spec = _halo_specs(tm, w, t)

    def body(p_ref, hp_ref, hn_ref, mu_ref, o_ref):
        has_prev, has_next = _edge_flags(tm, seq)
        xv = p_ref[...]
        down, up = _shifted(xv, hp_ref[GROUP - 1:GROUP, :] * has_prev, hn_ref[0:1, :] * has_next)
        o_ref[...] = xv + mu_ref[0:1, :] * (0.5 * (down + up) - xv)

    return pl.pallas_call(
        body, name="shift_fwd", grid=(t // tm,),
        in_specs=[_row(tm, w), prev_spec, next_spec, _fixed((GROUP, w))], out_specs=_row(tm, w),
        out_shape=_sds((t, w)), compiler_params=_cp("parallel"))(p, p, p, mu)


def _shift_bwd(q, p, mu, tm, seq):
    t, w = p.shape
    prev_spec, next_spec = _halo_specs(tm, w, t)

    def body(q_ref, qp_ref, qn_ref, p_ref, pp_ref, pn_ref, mu_ref, dp_ref, dmu_ref):
        has_prev, has_next = _edge_flags(tm, seq)
        muv = mu_ref[0:1, :]
        qv = q_ref[...]
        mq = muv * qv
        mq_down, mq_up = _shifted(mq, muv * qp_ref[GROUP - 1:GROUP, :] * has_prev,
                                  muv * qn_ref[0:1, :] * has_next)
        dp_ref[...] = (qv - mq + 0.5 * (mq_down + mq_up)).astype(BF16)
        pv = p_ref[...]
        p_down, p_up = _shifted(pv, pp_ref[GROUP - 1:GROUP, :] * has_prev, pn_ref[0:1, :] * has_next)

        @pl.when(pl.program_id(0) == 0)
        def _():
            dmu_ref[...] = jnp.zeros_like(dmu_ref)

        dmu_ref[...] += jnp.sum(qv * (0.5 * (p_down + p_up) - pv), axis=0, keepdims=True)

    return pl.pallas_call(
        body, name="shift_bwd", grid=(t // tm,),
        in_specs=[_row(tm, w), prev_spec, next_spec, _row(tm, w), prev_spec, next_spec, _fixed((GROUP, w))],
        out_specs=[_row(tm, w), _fixed((1, w))],
        out_shape=[_sds((t, w), BF16), _sds((1, w))], compiler_params=_cp("arbitrary"))(q, q, q, p, p, p, mu)


@jax.custom_vjp
def _bdot(a, b):
    return jnp.dot(a.astype(BF16), b.astype(BF16), preferred_element_type=F32)


def _bdot_fwd(a, b):
    return _bdot(a, b), (a, b)


def _bdot_bwd(res, g):
    a, b = res
    gb = g.astype(BF16)
    da = lax.dot_general(gb, b.astype(BF16), (((1,), (1,)), ((), ())), preferred_element_type=F32)
    db = lax.dot_general(a.astype(BF16), gb, (((0,), (0,)), ((), ())), preferred_element_type=F32)
    return da, db


_bdot.defvjp(_bdot_fwd, _bdot_bwd)


def _seg_raw(x, ones_blocks):
    hi = x.astype(BF16)
    lo = (x - hi.astype(F32)).astype(BF16)
    return (jnp.dot(hi, ones_blocks, preferred_element_type=F32)
            + jnp.dot(lo, ones_blocks, preferred_element_type=F32))


@jax.custom_vjp
def _seg(x, ones_blocks):
    return _seg_raw(x, ones_blocks)


def _seg_fwd(x, ones_blocks):
    return _seg_raw(x, ones_blocks), ones_blocks


def _seg_bwd(ones_blocks, g):
    return _seg_raw(g, ones_blocks), jnp.zeros_like(ones_blocks)


_seg.defvjp(_seg_fwd, _seg_bwd)


def _head_ones():
    h = jnp.arange(D_RWKV) // HEAD
    return (h[:, None] == h[None, :]).astype(BF16)


def _prep_math(ps, k_k, w0_f, a0_f, k_a_f, w0_b, a0_b, k_a_b, wup_f, aup_f, wup_b, aup_b, gup, ones_blocks):
    r = ps[:, 0:512]
    k = ps[:, 512:1024]
    v = ps[:, 1024:1536]
    xwa = ps[:, 1536:1664]
    xg = ps[:, 1664:D_SHIFT_PAD]
    kk_raw = k * k_k
    norm = jnp.sqrt(_seg(kk_raw * kk_raw, ones_blocks))
    kk = kk_raw / jnp.maximum(norm, NORM_EPS)
    t_xwa = jnp.tanh(xwa)
    outs = [r, v, kk]
    for w0, a0, k_a, wup, aup in ((w0_f, a0_f, k_a_f, wup_f, aup_f), (w0_b, a0_b, k_a_b, wup_b, aup_b)):
        decay = jnp.exp(-LOG_DECAY_SCALE * jax.nn.sigmoid(w0 + _bdot(t_xwa, wup)))
        rate = jax.nn.sigmoid(a0 + _bdot(xwa, aup))
        outs += [decay, k * (1.0 + (rate - 1.0) * k_a), kk * rate]
    outs.append(_bdot(jax.nn.sigmoid(xg), gup))
    return tuple(outs)


def _prep_args(tm, ps_ref, pv_ref, mat_refs, ones_ref):
    vecs = [jnp.broadcast_to(pv_ref[j:j + 1, :], (tm, D_RWKV)) for j in range(7)]
    return [ps_ref[...]] + vecs + [m[...] for m in mat_refs] + [ones_ref[...]]


_PREP_MAT_SHAPES = ((128, D_RWKV),) * 4 + ((D_GATE_PAD, D_RWKV),)


def _prep_fwd(ps, pvec, mats, ones_blocks, tm):
    t = ps.shape[0]

    def body(ps_ref, pv_ref, m0, m1, m2, m3, m4, ones_ref, *out_refs):
        outs = _prep_math(*_prep_args(tm, ps_ref, pv_ref, (m0, m1, m2, m3, m4), ones_ref))
        for o_ref, val in zip(out_refs, outs[2:]):
            o_ref[...] = val

    return pl.pallas_call(
        body, name="prep_fwd", grid=(t // tm,),
        in_specs=[_row(tm, D_SHIFT_PAD), _fixed((8, D_RWKV))] + [_fixed(s) for s in _PREP_MAT_SHAPES]
        + [_fixed((D_RWKV, D_RWKV))],
        out_specs=[_row(tm, D_RWKV)] * 8, out_shape=[_sds((t, D_RWKV))] * 8,
        compiler_params=_cp("parallel"))(ps, pvec, *mats, ones_blocks)


def _prep_bwd(ps, pvec, mats, ones_blocks, cts, tm):
    t = ps.shape[0]
    counts = [len(c) for c in cts]
    flat = [a for c in cts for a in c]

    def body(ps_ref, pv_ref, m0, m1, m2, m3, m4, ones_ref, *refs):
        ct_refs = refs[:len(flat)]
        q_ref, dpv_ref = refs[len(flat)], refs[len(flat) + 1]
        dmat_refs = refs[len(flat) + 2:]
        args = _prep_args(tm, ps_ref, pv_ref, (m0, m1, m2, m3, m4), ones_ref)
        _, vjp = jax.vjp(lambda *a: _prep_math(*a, args[-1]), *args[:-1])
        ct_vals, pos = [], 0
        for n in counts:
            val = ct_refs[pos][...]
            for extra in ct_refs[pos + 1:pos + n]:
                val = val + extra[...]
            ct_vals.append(val)
            pos += n
        grads = vjp(tuple(ct_vals))
        q_ref[...] = grads[0]

        @pl.when(pl.program_id(0) == 0)
        def _():
            dpv_ref[...] = jnp.zeros_like(dpv_ref)
            for d_ref in dmat_refs:
                d_ref[...] = jnp.zeros_like(d_ref)

        for j in range(7):
            dpv_ref[j:j + 1, :] += jnp.sum(grads[1 + j], axis=0, keepdims=True)
        for d_ref, gm in zip(dmat_refs, grads[8:13]):
            d_ref[...] += gm

    return pl.pallas_call(
        body, name="prep_bwd", grid=(t // tm,),
        in_specs=[_row(tm, D_SHIFT_PAD), _fixed((8, D_RWKV))] + [_fixed(s) for s in _PREP_MAT_SHAPES]
        + [_fixed((D_RWKV, D_RWKV))] + [_row(tm, D_RWKV)] * len(flat),
        out_specs=[_row(tm, D_SHIFT_PAD), _fixed((8, D_RWKV))] + [_fixed(s) for s in _PREP_MAT_SHAPES],
        out_shape=[_sds((t, D_SHIFT_PAD)), _sds((8, D_RWKV))] + [_sds(s) for s in _PREP_MAT_SHAPES],
        compiler_params=_cp("arbitrary"))(ps, pvec, *mats, ones_blocks, *flat)


def _wkv_step(state, kk_row, w_row, b_row, kd_row, v_col):
    sa = jnp.sum(state * kk_row, axis=1, keepdims=True)
    return state * w_row - sa * b_row + v_col * kd_row


def _halves(x8):
    return x8[:, 0:HEAD], x8[:, HEAD:2 * HEAD]


def _scan_specs(n_chunks, col_blocks, fwd_chunk, bwd_chunk):
    def spec(chunk_of, col):
        return pl.BlockSpec((SCAN_CHUNK, 2 * HEAD), lambda b, p, g: (b * n_chunks + chunk_of(g), col + p))
    return [spec(fwd_chunk, c) for c in col_blocks] + [spec(bwd_chunk, c) for c in col_blocks]


def _scan_fwd(ps, kk, dirs, batch, seq):
    t = batch * seq
    n = seq // SCAN_CHUNK
    groups = SCAN_CHUNK // GROUP
    up = lambda g: g
    down = lambda g: n - 1 - g
    col_blocks = (0, 8, 0, 0, 0, 0)

    def one_group(refs, off, states, order):
        r_ref, v_ref, kk_ref, w_ref, kd_ref, b_ref = refs
        rows = pl.ds(off, GROUP)
        v_t = jnp.transpose(v_ref[rows, :])
        r8, kk8, w8, kd8, b8 = (_halves(ref[rows, :]) for ref in (r_ref, kk_ref, w_ref, kd_ref, b_ref))
        lane = lax.broadcasted_iota(jnp.int32, (HEAD, 2 * HEAD), 1)
        new_states, y_cols = [], []
        for j in range(2):
            state = states[j]
            y_t = jnp.zeros((HEAD, 2 * HEAD), F32)
            for i in order:
                sl = slice(i, i + 1)
                state = _wkv_step(state, kk8[j][sl], w8[j][sl], b8[j][sl], kd8[j][sl],
                                  v_t[HEAD * j:HEAD * (j + 1), sl])
                y_t = jnp.where(lane == i, jnp.sum(state * r8[j][sl], axis=1, keepdims=True), y_t)
            new_states.append(state)
            y_cols.append(y_t)
        return new_states, jnp.transpose(jnp.concatenate(y_cols, axis=0))[0:GROUP, :]

    def body(*refs):
        f_refs, b_refs = refs[0:6], refs[6:12]
        yf_ref, yb_ref, ckf_ref, ckb_ref, st_ref = refs[12:17]

        @pl.when(pl.program_id(2) == 0)
        def _():
            st_ref[...] = jnp.zeros_like(st_ref)

        ckf_ref[...] = st_ref[0:2]
        ckb_ref[...] = st_ref[2:4]

        def group(gi, carry):
            off_f = pl.multiple_of(gi * GROUP, GROUP)
            off_b = pl.multiple_of((groups - 1 - gi) * GROUP, GROUP)
            sf, y8f = one_group(f_refs, off_f, carry[0:2], range(GROUP))
            sb, y8b = one_group(b_refs, off_b, carry[2:4], range(GROUP - 1, -1, -1))
            yf_ref[pl.ds(off_f, GROUP), :] = y8f
            yb_ref[pl.ds(off_b, GROUP), :] = y8b
            return tuple(sf) + tuple(sb)

        final = lax.fori_loop(0, groups, group, tuple(st_ref[c] for c in range(4)))
        for c in range(4):
            st_ref[c] = final[c]

    y_spec_f = pl.BlockSpec((SCAN_CHUNK, 2 * HEAD), lambda b, p, g: (b * n + up(g), p))
    y_spec_b = pl.BlockSpec((SCAN_CHUNK, 2 * HEAD), lambda b, p, g: (b * n + down(g), p))
    ck_shape = (batch, N_PAIR, n, 2, HEAD, HEAD)
    ck_spec_f = pl.BlockSpec((None, None, None, 2, HEAD, HEAD), lambda b, p, g: (b, p, up(g), 0, 0, 0))
    ck_spec_b = pl.BlockSpec((None, None, None, 2, HEAD, HEAD), lambda b, p, g: (b, p, down(g), 0, 0, 0))
    (wf, kdf, bf), (wb, kdb, bb) = dirs
    return pl.pallas_call(
        body, name="wkv_fwd", grid=(batch, N_PAIR, n),
        in_specs=_scan_specs(n, col_blocks, up, down),
        out_specs=[y_spec_f, y_spec_b, ck_spec_f, ck_spec_b],
        out_shape=[_sds((t, D_RWKV)), _sds((t, D_RWKV)), _sds(ck_shape), _sds(ck_shape)],
        scratch_shapes=[pltpu.VMEM((4, HEAD, HEAD), F32)],
        compiler_params=_cp("parallel", "parallel", "arbitrary"),
    )(ps, ps, kk, wf, kdf, bf, ps, ps, kk, wb, kdb, bb)


def _scan_bwd(ps, kk, dirs, dy, ck_f, ck_b, batch, seq):
    t = batch * seq
    n = seq // SCAN_CHUNK
    groups = SCAN_CHUNK // GROUP
    fwd_chunk = lambda g: n - 1 - g
    bwd_chunk = lambda g: g
    col_blocks = (0, 8, 0, 0, 0, 0, 0)

    def recompute(refs, ck_ref, hist_ref, chain0, ascending):
        _, v_ref, kk_ref, w_ref, kd_ref, b_ref, _ = refs

        def group(gi, states):
            blk = gi if ascending else groups - 1 - gi
            rows = pl.ds(pl.multiple_of(blk * GROUP, GROUP), GROUP)
            v_t = jnp.transpose(v_ref[rows, :])
            kk8, w8, kd8, b8 = (_halves(ref[rows, :]) for ref in (kk_ref, w_ref, kd_ref, b_ref))
            out = []
            for j in range(2):
                state = states[j]
                for step in range(GROUP):
                    i = step if ascending else GROUP - 1 - step
                    sl = slice(i, i + 1)
                    hist_ref[chain0 + j, gi * GROUP + step] = state
                    state = _wkv_step(state, kk8[j][sl], w8[j][sl], b8[j][sl], kd8[j][sl],
                                      v_t[HEAD * j:HEAD * (j + 1), sl])
                out.append(state)
            return tuple(out)

        last = lax.fori_loop(0, groups, group, (ck_ref[0], ck_ref[1]))
        for j in range(2):
            hist_ref[chain0 + j, SCAN_CHUNK] = last[j]

    def undo_group(refs, out_refs, hist_ref, chain0, gi, d_states, ascending):
        r_ref, v_ref, kk_ref, w_ref, kd_ref, b_ref, dy_ref = refs
        blk = groups - 1 - gi if ascending else gi
        rows = pl.ds(pl.multiple_of(blk * GROUP, GROUP), GROUP)
        v_t = jnp.transpose(v_ref[rows, :])
        dy_t = jnp.transpose(dy_ref[rows, :])
        r8, kk8, w8, kd8, b8 = (_halves(ref[rows, :]) for ref in (r_ref, kk_ref, w_ref, kd_ref, b_ref))
        lane = lax.broadcasted_iota(jnp.int32, (HEAD, 2 * HEAD), 1)
        sub = lax.broadcasted_iota(jnp.int32, (GROUP, HEAD), 0)
        new_d, row_acc, dv_cols = [], [], []
        for j in range(2):
            d_state = d_states[j]
            acc = [jnp.zeros((GROUP, HEAD), F32) for _ in range(5)]
            dv_t = jnp.zeros((HEAD, 2 * HEAD), F32)
            for step in range(GROUP):
                i = GROUP - 1 - step if ascending else step
                q = (groups - 1 - gi) * GROUP + i if ascending else SCAN_CHUNK - 1 - (gi * GROUP + i)
                sl = slice(i, i + 1)
                before = hist_ref[chain0 + j, q]
                after = hist_ref[chain0 + j, q + 1]
                v_col = v_t[HEAD * j:HEAD * (j + 1), sl]
                dy_col = dy_t[HEAD * j:HEAD * (j + 1), sl]
                d_state = d_state + dy_col * r8[j][sl]
                sa = jnp.sum(before * kk8[j][sl], axis=1, keepdims=True)
                d_sa = -jnp.sum(d_state * b8[j][sl], axis=1, keepdims=True)
                rows_out = (
                    jnp.sum(after * dy_col, axis=0, keepdims=True),
                    jnp.sum(d_state * before, axis=0, keepdims=True),
                    jnp.sum(d_state * v_col, axis=0, keepdims=True),
                    jnp.sum(before * d_sa, axis=0, keepdims=True),
                    -jnp.sum(d_state * sa, axis=0, keepdims=True),
                )
                acc = [jnp.where(sub == i, val, a) for val, a in zip(rows_out, acc)]
                dv_t = jnp.where(lane == i, jnp.sum(d_state * kd8[j][sl], axis=1, keepdims=True), dv_t)
                d_state = d_state * w8[j][sl] + d_sa * kk8[j][sl]
            new_d.append(d_state)
            row_acc.append(acc)
            dv_cols.append(dv_t)
        dr_ref, dw_ref, dkd_ref, dv_ref, dkk_ref, db_ref = out_refs
        for o_ref, idx in ((dr_ref, 0), (dw_ref, 1), (dkd_ref, 2), (dkk_ref, 3), (db_ref, 4)):
            o_ref[rows, :] = jnp.concatenate([row_acc[0][idx], row_acc[1][idx]], axis=1)
        dv_ref[rows, :] = jnp.transpose(jnp.concatenate(dv_cols, axis=0))[0:GROUP, :]
        return new_d

    def body(*refs):
        f_refs, b_refs = refs[0:7], refs[7:14]
        ckf_ref, ckb_ref = refs[14:16]
        f_out, b_out = refs[16:22], refs[22:28]
        dst_ref, hist_ref = refs[28:30]

        @pl.when(pl.program_id(2) == 0)
        def _():
            dst_ref[...] = jnp.zeros_like(dst_ref)

        recompute(f_refs, ckf_ref, hist_ref, 0, True)
        recompute(b_refs, ckb_ref, hist_ref, 2, False)

        def group(gi, carry):
            df = undo_group(f_refs, f_out, hist_ref, 0, gi, carry[0:2], True)
            db = undo_group(b_refs, b_out, hist_ref, 2, gi, carry[2:4], False)
            return tuple(df) + tuple(db)

        final = lax.fori_loop(0, groups, group, tuple(dst_ref[c] for c in range(4)))
        for c in range(4):
            dst_ref[c] = final[c]

    blk = (SCAN_CHUNK, 2 * HEAD)
    out_f = pl.BlockSpec(blk, lambda b, p, g: (b * n + fwd_chunk(g), p))
    out_b = pl.BlockSpec(blk, lambda b, p, g: (b * n + bwd_chunk(g), p))
    ck_spec_f = pl.BlockSpec((None, None, None, 2, HEAD, HEAD), lambda b, p, g: (b, p, fwd_chunk(g), 0, 0, 0))
    ck_spec_b = pl.BlockSpec((None, None, None, 2, HEAD, HEAD), lambda b, p, g: (b, p, bwd_chunk(g), 0, 0, 0))
    (wf, kdf, bf), (wb, kdb, bb) = dirs
    outs = pl.pallas_call(
        body, name="wkv_bwd", grid=(batch, N_PAIR, n),
        in_specs=_scan_specs(n, col_blocks, fwd_chunk, bwd_chunk) + [ck_spec_f, ck_spec_b],
        out_specs=[out_f] * 6 + [out_b] * 6,
        out_shape=[_sds((t, D_RWKV))] * 12,
        scratch_shapes=[pltpu.VMEM((4, HEAD, HEAD), F32), pltpu.VMEM((4, SCAN_CHUNK + 1, HEAD, HEAD), F32)],
        compiler_params=_cp("parallel", "parallel", "arbitrary"),
    )(ps, ps, kk, wf, kdf, bf, dy, ps, ps, kk, wb, kdb, bb, dy, ck_f, ck_b)
    return outs[0:6], outs[6:12]


def _post_math(y, r, kd_f, kd_b, v, gate, gn_w, gn_b, rk_f, rk_b, ones_blocks):
    mean = _seg(y, ones_blocks) * (1.0 / HEAD)
    yc = y - mean
    var = _seg(yc * yc, ones_blocks) * (1.0 / HEAD)
    yn = yc * lax.rsqrt(var + GN_EPS) * gn_w + gn_b
    bonus = _seg(r * kd_f * rk_f, ones_blocks) * v + _seg(r * kd_b * rk_b, ones_blocks) * v
    return (yn + bonus) * gate


def _conv_parts(pc, halo_prev, halo_next, has_prev, has_next):
    gate_b, gate_c, hid = pc[:, 0:512], pc[:, 512:1024], pc[:, 1024:1536]
    u = gate_c * hid
    u_prev_row = halo_prev[GROUP - 1:GROUP, 512:1024] * halo_prev[GROUP - 1:GROUP, 1024:1536] * has_prev
    u_next_row = halo_next[0:1, 512:1024] * halo_next[0:1, 1024:1536] * has_next
    u_down, u_up = _shifted(u, u_prev_row, u_next_row)
    return gate_b, gate_c, hid, u, u_down, u_up


def _post_specs(tm, t):
    pc_prev, pc_next = _halo_specs(tm, D_CONV3, t)
    col = lambda c: pl.BlockSpec((tm, D_RWKV), lambda i: (i, c))
    return ([col(0), col(0), col(0), col(0), col(0), col(2), col(0), _row(tm, D_CONV3), pc_prev, pc_next,
             _fixed((8, D_RWKV)), _fixed((D_RWKV, D_RWKV))])


def _post_fwd(y_f, y_b, ps, kd_f, kd_b, gate, pc, qvec, ones_blocks, tm, seq):
    t = ps.shape[0]

    def body(yf_ref, yb_ref, r_ref, kdf_ref, kdb_ref, v_ref, g_ref, pc_ref, hp_ref, hn_ref, qv_ref, ones_ref,
             o_ref):
        has_prev, has_next = _edge_flags(tm, seq)
        vec = [jnp.broadcast_to(qv_ref[j:j + 1, :], (tm, D_RWKV)) for j in range(7)]
        o_ref[:, 0:D_RWKV] = _post_math(yf_ref[...] + yb_ref[...], r_ref[...], kdf_ref[...], kdb_ref[...],
                                        v_ref[...], g_ref[...], vec[0], vec[1], vec[2], vec[3],
                                        ones_ref[...]).astype(BF16)
        gate_b, _, _, u, u_down, u_up = _conv_parts(pc_ref[...], hp_ref[...], hn_ref[...], has_prev, has_next)
        o_ref[:, D_RWKV:2 * D_RWKV] = (gate_b * (vec[4] * u_down + vec[5] * u + vec[6] * u_up)).astype(BF16)

    return pl.pallas_call(
        body, name="post_fwd", grid=(t // tm,), in_specs=_post_specs(tm, t), out_specs=_row(tm, D_MODEL),
        out_shape=_sds((t, D_MODEL), BF16), compiler_params=_cp("parallel"),
    )(y_f, y_b, ps, kd_f, kd_b, ps, gate, pc, pc, pc, qvec, ones_blocks)


def _post_bwd(d_out, y_f, y_b, ps, kd_f, kd_b, gate, pc, qvec, ones_blocks, tm, seq):
    t = ps.shape[0]
    do_prev, do_next = _halo_specs(tm, D_MODEL, t)

    def body(do_ref, dop_ref, don_ref, yf_ref, yb_ref, r_ref, kdf_ref, kdb_ref, v_ref, g_ref, pc_ref, hp_ref,
             hn_ref, qv_ref, ones_ref, dy_ref, dr_ref, dkdf_ref, dkdb_ref, dv_ref, dg_ref, dpc_ref, dqv_ref):
        has_prev, has_next = _edge_flags(tm, seq)
        vec = [jnp.broadcast_to(qv_ref[j:j + 1, :], (tm, D_RWKV)) for j in range(7)]
        ones_v = ones_ref[...]
        args = (yf_ref[...] + yb_ref[...], r_ref[...], kdf_ref[...], kdb_ref[...], v_ref[...], g_ref[...],
                vec[0], vec[1], vec[2], vec[3])
        _, vjp = jax.vjp(lambda *a: _post_math(*a, ones_v), *args)
        grads = vjp(do_ref[:, 0:D_RWKV])
        for o_ref, gval in zip((dy_ref, dr_ref, dkdf_ref, dkdb_ref, dv_ref, dg_ref), grads[0:6]):
            o_ref[...] = gval

        hp, hn = hp_ref[...], hn_ref[...]
        gate_b, gate_c, hid, u, u_down, u_up = _conv_parts(pc_ref[...], hp, hn, has_prev, has_next)
        d_oc = do_ref[:, D_RWKV:2 * D_RWKV]
        d_cu = d_oc * gate_b
        d_cu_prev = dop_ref[GROUP - 1:GROUP, D_RWKV:2 * D_RWKV] * hp[GROUP - 1:GROUP, 0:512] * has_prev
        d_cu_next = don_ref[0:1, D_RWKV:2 * D_RWKV] * hn[0:1, 0:512] * has_next
        d_cu_down, d_cu_up = _shifted(d_cu, d_cu_prev, d_cu_next)
        d_u = vec[5] * d_cu + vec[4] * d_cu_up + vec[6] * d_cu_down
        dpc_ref[:, 0:512] = (d_oc * (vec[4] * u_down + vec[5] * u + vec[6] * u_up)).astype(BF16)
        dpc_ref[:, 512:1024] = (d_u * hid).astype(BF16)
        dpc_ref[:, 1024:1536] = (d_u * gate_c).astype(BF16)

        @pl.when(pl.program_id(0) == 0)
        def _():
            dqv_ref[...] = jnp.zeros_like(dqv_ref)

        vec_grads = list(grads[6:10]) + [d_cu * u_down, d_cu * u, d_cu * u_up]
        for j, gval in enumerate(vec_grads):
            dqv_ref[j:j + 1, :] += jnp.sum(gval, axis=0, keepdims=True)

    return pl.pallas_call(
        body, name="post_bwd", grid=(t // tm,),
        in_specs=[_row(tm, D_MODEL), do_prev, do_next] + _post_specs(tm, t),
        out_specs=[_row(tm, D_RWKV)] * 6 + [_row(tm, D_CONV3), _fixed((8, D_RWKV))],
        out_shape=[_sds((t, D_RWKV))] * 6 + [_sds((t, D_CONV3), BF16), _sds((8, D_RWKV))],
        compiler_params=_cp("arbitrary"),
    )(d_out, d_out, d_out, y_f, y_b, ps, kd_f, kd_b, ps, gate, pc, pc, pc, qvec, ones_blocks)


def _adamw(w, g, m, v, name):
    r, c = w.shape
    tr = _tile(r, 256, 8)
    spec = pl.BlockSpec((tr, c), lambda i: (i, 0))

    def body(w_ref, g_ref, m_ref, v_ref, d_ref, nm_ref, nv_ref):
        gv = g_ref[...]
        m2 = ADAM_B1 * m_ref[...] + (1.0 - ADAM_B1) * gv
        v2 = ADAM_B2 * v_ref[...] + (1.0 - ADAM_B2) * (gv * gv)
        m_hat = m2 / (1.0 - ADAM_B1 ** ADAM_STEP)
        v_hat = v2 / (1.0 - ADAM_B2 ** ADAM_STEP)
        d_ref[...] = -ADAM_LR * (m_hat / (jnp.sqrt(v_hat) + ADAM_EPS) + ADAM_WD * w_ref[...])
        nm_ref[...] = m2
        nv_ref[...] = v2

    return pl.pallas_call(
        body, name=name, grid=(r // tr,), in_specs=[spec] * 4, out_specs=[spec] * 3,
        out_shape=[_sds((r, c))] * 3, compiler_params=_cp("parallel"))(w, g, m, v)


def _add_n(parts, name):
    r, c = parts[0].shape
    tr = _tile(r, 408, 8)
    spec = pl.BlockSpec((tr, c), lambda i: (i, 0))

    def body(*refs):
        acc = refs[0][...]
        for ref in refs[1:-1]:
            acc = acc + ref[...]
        refs[-1][...] = acc

    return pl.pallas_call(
        body, name=name, grid=(r // tr,), in_specs=[spec] * len(parts), out_specs=spec,
        out_shape=_sds((r, c)), compiler_params=_cp("parallel"))(*parts)


_ANY = pl.BlockSpec(memory_space=pl.ANY)


def _place():
    return lax.axis_index("x"), lax.axis_index("y"), lax.axis_index("c")


def _other_chips(x, y):
    return [(1 - x, y), (x, 1 - y), (1 - x, 1 - y)]


def _remote(src, dst, send_sems, recv_sems, k, to):
    return pltpu.make_async_remote_copy(src_ref=src, dst_ref=dst, send_sem=send_sems.at[k],
                                        recv_sem=recv_sems.at[k], device_id=to, device_id_type=MESH)


def _gather_weights(pack):
    rows, width = pack.shape
    half = rows // 2

    def body(x_ref, out_ref, send_sems, recv_sems, local_sem):
        x, y, c = _place()
        sibling = (x, y, 1 - c)
        chips = _other_chips(x, y)

        def block(chip, part):
            return out_ref.at[2 * chip[0] + chip[1], pl.ds(part * half, half), :]

        mine = pltpu.make_async_copy(x_ref, out_ref.at[2 * x + y], local_sem)
        mine.start()
        first = [_remote(x_ref.at[pl.ds(c * half, half), :], block((x, y), c), send_sems, recv_sems, j, (*chip, c))
                 for j, chip in enumerate(chips)]
        for cp in first:
            cp.start()
        passed = [_remote(block(chip, c), block(chip, c), send_sems, recv_sems, 3 + j, sibling)
                  for j, chip in enumerate(chips)]
        for j, chip in enumerate(chips):
            _remote(block(chip, c), block(chip, c), send_sems, recv_sems, j, sibling).wait_recv()
            passed[j].start()
        for j, chip in enumerate(chips):
            _remote(block(chip, 1 - c), block(chip, 1 - c), send_sems, recv_sems, 3 + j, sibling).wait_recv()
        for cp in first + passed:
            cp.wait_send()
        mine.wait()

    return pl.pallas_call(
        body, name="gather_weights", in_specs=[_ANY], out_specs=_ANY,
        out_shape=_sds((N_SHARD, rows, width), pack.dtype),
        scratch_shapes=[pltpu.SemaphoreType.DMA((6,)), pltpu.SemaphoreType.DMA((6,)), pltpu.SemaphoreType.DMA],
    )(pack)


def _swap_with_sibling(block, name):
    def body(x_ref, out_ref, send_sems, recv_sems):
        x, y, c = _place()
        cp = _remote(x_ref, out_ref, send_sems, recv_sems, 0, (x, y, 1 - c))
        cp.start()
        cp.wait()

    return pl.pallas_call(
        body, name=name, in_specs=[_ANY], out_specs=_ANY, out_shape=_sds(block.shape, block.dtype),
        scratch_shapes=[pltpu.SemaphoreType.DMA((1,)), pltpu.SemaphoreType.DMA((1,))],
    )(block)


def _exchange_quarters(parts):
    _, rows, width = parts.shape

    def body(x_ref, out_ref, send_sems, recv_sems):
        x, y, c = _place()
        copies = [_remote(x_ref.at[2 * chip[0] + chip[1]], out_ref.at[j], send_sems, recv_sems, j, (*chip, c))
                  for j, chip in enumerate(_other_chips(x, y))]
        for cp in copies:
            cp.start()
        for cp in copies:
            cp.wait()

    return pl.pallas_call(
        body, name="exchange_quarters", in_specs=[_ANY], out_specs=_ANY,
        out_shape=_sds((3, rows, width), parts.dtype),
        scratch_shapes=[pltpu.SemaphoreType.DMA((3,)), pltpu.SemaphoreType.DMA((3,))],
    )(parts)


def _join_halves(mine):
    half, width = mine.shape

    def body(x_ref, out_ref, send_sems, recv_sems, local_sem):
        x, y, c = _place()
        own = pltpu.make_async_copy(x_ref, out_ref.at[pl.ds(c * half, half), :], local_sem)
        own.start()
        cp = _remote(x_ref, out_ref.at[pl.ds(c * half, half), :], send_sems, recv_sems, 0, (x, y, 1 - c))
        cp.start()
        cp.wait_send()
        _remote(x_ref, out_ref.at[pl.ds((1 - c) * half, half), :], send_sems, recv_sems, 0, (x, y, 1 - c)).wait_recv()
        own.wait()

    return pl.pallas_call(
        body, name="join_halves", in_specs=[_ANY], out_specs=_ANY, out_shape=_sds((2 * half, width), mine.dtype),
        scratch_shapes=[pltpu.SemaphoreType.DMA((1,)), pltpu.SemaphoreType.DMA((1,)), pltpu.SemaphoreType.DMA],
    )(mine)


def _allreduce_small(vec):
    rows, width = vec.shape
    vmem = pl.BlockSpec(memory_space=pltpu.VMEM)

    def body(x_ref, o_ref, buf_ref, send_sems, recv_sems):
        x, y, c = _place()
        me = 4 * x + 2 * y + c
        buf_ref[me] = x_ref[...]
        copies = []
        for k in range(1, N_DEV):
            peer = (x ^ ((k >> 2) & 1), y ^ ((k >> 1) & 1), c ^ (k & 1))
            copies.append(_remote(x_ref, buf_ref.at[me], send_sems, recv_sems, k - 1, peer))
        for cp in copies:
            cp.start()
        for k in range(1, N_DEV):
            _remote(x_ref, buf_ref.at[me ^ k], send_sems, recv_sems, k - 1, (x, y, c)).wait_recv()
        for cp in copies:
            cp.wait_send()
        total = buf_ref[0]
        for d in range(1, N_DEV):
            total = total + buf_ref[d]
        o_ref[...] = total

    return pl.pallas_call(
        body, name="allreduce_small", in_specs=[vmem], out_specs=vmem, out_shape=_sds((rows, width)),
        scratch_shapes=[pltpu.VMEM((N_DEV, rows, width), F32), pltpu.SemaphoreType.DMA((N_DEV - 1,)),
                        pltpu.SemaphoreType.DMA((N_DEV - 1,))],
    )(vec)


def _rows1024(a):
    return a.reshape(-1, 1024)


def _pad_rows(a, rows):
    return jnp.concatenate([a, jnp.zeros((rows - a.shape[0], a.shape[1]), a.dtype)], axis=0)


def _pack_weight_shards(w):
    conv_bits = lax.bitcast_convert_type(w["conv_w"], BF16).reshape(1, -1)
    conv_row = jnp.concatenate([conv_bits, jnp.zeros((1, 1024 - conv_bits.shape[1]), BF16)], axis=1)
    parts = [_rows1024(w[name].astype(BF16)) for name, _ in _PACK_ROWS[:-1]] + [conv_row]
    return _pad_rows(jnp.concatenate(parts, axis=0), PACK_R)


def _unpack_gathered(gathered):
    out, row = {}, 0
    for name, n in _PACK_ROWS:
        out[name] = gathered[:, row:row + n]
        row += n
    cols = lambda a, k: jnp.concatenate([a[s].reshape(k, -1) for s in range(N_SHARD)], axis=1)
    conv = lax.bitcast_convert_type(out["conv_w"][:, 0, :768].reshape(N_SHARD, 3, 128, 2), F32)
    return dict(
        w_in=cols(out["w_in"], D_MODEL), w_out=out["w_out"].reshape(D_MODEL, D_MODEL),
        w_gate=cols(out["w_gate"], D_MODEL), w_up=cols(out["w_up"], D_MODEL),
        w_down=out["w_down"].reshape(D_FF, D_MODEL),
        w_up_f=cols(out["w_up_f"], D_LORA), w_up_b=cols(out["w_up_b"], D_LORA),
        a_up_f=cols(out["a_up_f"], D_LORA), a_up_b=cols(out["a_up_b"], D_LORA),
        g_up=cols(out["g_up"], D_GATE), conv_w=jnp.concatenate([conv[s] for s in range(N_SHARD)], axis=1))


def _pack_grads(g):
    col_split = lambda a, s: a[:, s * (a.shape[1] // N_SHARD):(s + 1) * (a.shape[1] // N_SHARD)]
    row_split = lambda a, s: a[s * (a.shape[0] // N_SHARD):(s + 1) * (a.shape[0] // N_SHARD)]
    slots = []
    for s in range(N_SHARD):
        conv = col_split(g["conv_w"], s).reshape(1, -1)
        parts = [_rows1024(col_split(g["w_in"], s)), row_split(g["w_out"], s), _rows1024(col_split(g["w_gate"], s)),
                 _rows1024(col_split(g["w_up"], s)), row_split(g["w_down"], s)]
        parts += [_rows1024(col_split(g[name], s)) for name in ("w_up_f", "w_up_b", "a_up_f", "a_up_b", "g_up")]
        parts.append(jnp.concatenate([conv, jnp.zeros((1, 1024 - conv.shape[1]), F32)], axis=1))
        slots.append(_pad_rows(jnp.concatenate(parts, axis=0), PACK_R))
    return jnp.stack(slots)


def _unpack_grad_shard(pack):
    shapes = dict(w_in=(D_MODEL, 840), w_out=(256, D_MODEL), w_gate=(D_MODEL, 704), w_up=(D_MODEL, 704),
                  w_down=(704, D_MODEL), w_up_f=(D_LORA, 128), w_up_b=(D_LORA, 128), a_up_f=(D_LORA, 128),
                  a_up_b=(D_LORA, 128), g_up=(D_GATE, 128))
    out, row = {}, 0
    for name, n in _PACK_ROWS[:-1]:
        out[name] = pack[row:row + n].reshape(shapes[name])
        row += n
    out["conv_w"] = pack[row, :384].reshape(3, 128)
    return out


_SMALL_LAYOUT = (("norm1_w", 1024), ("mu_shift", D_SHIFT), ("w0_f", 512), ("w0_b", 512), ("a0_f", 512),
                 ("a0_b", 512), ("k_k", 512), ("k_a_f", 512), ("k_a_b", 512), ("r_k_f", 512), ("r_k_b", 512),
                 ("gn_w", 512), ("gn_b", 512), ("norm2_w", 1024), ("norm_f_w", 1024), ("loss", 1))


def _pack_small(vals):
    rows = []
    for name, n in _SMALL_LAYOUT:
        flat = vals[name].reshape(-1)
        n_rows = -(-n // 1024)
        rows.append(jnp.concatenate([flat, jnp.zeros((n_rows * 1024 - n,), F32)]).reshape(n_rows, 1024))
    return _pad_rows(jnp.concatenate(rows, axis=0), SMALL_ROWS)


def _unpack_small(pack):
    out, row = {}, 0
    for name, n in _SMALL_LAYOUT:
        n_rows = -(-n // 1024)
        out[name] = pack[row:row + n_rows].reshape(-1)[:n]
        row += n_rows
    return out


_WEIGHTS = ("norm1_w", "w_in", "mu_shift", "w_up_f", "w0_f", "w_up_b", "w0_b", "a_up_f", "a0_f", "a_up_b", "a0_b",
            "g_up", "k_k", "k_a_f", "k_a_b", "r_k_f", "r_k_b", "gn_w", "gn_b", "conv_w", "w_out", "norm2_w",
            "w_gate", "w_up", "w_down", "norm_f_w")


def _train_step(x, loss_target, w, m, v):
    batch, seq, _ = x.shape
    t = batch * seq
    tm = _tile(seq, 256, 8)
    xs = x.reshape(t, D_MODEL)
    target = loss_target.reshape(t, D_MODEL)
    vec = lambda name: w[name].reshape(1, -1)

    local = {name: w[name][0] for name, _ in _PACK_ROWS}
    full = _unpack_gathered(_gather_weights(_pack_weight_shards(local)))
    w_in = full["w_in"]
    w_shift = jnp.concatenate([w_in[:, :D_SHIFT], jnp.zeros((D_MODEL, D_SHIFT_PAD - D_SHIFT), BF16)], axis=1)
    w_conv = w_in[:, D_SHIFT:]
    zeros_lora = jnp.zeros((D_LORA, D_RWKV), F32)
    lora = lambda name: full[name].astype(F32)
    mats = (jnp.concatenate([lora("w_up_f"), zeros_lora]), jnp.concatenate([zeros_lora, lora("a_up_f")]),
            jnp.concatenate([lora("w_up_b"), zeros_lora]), jnp.concatenate([zeros_lora, lora("a_up_b")]),
            jnp.concatenate([lora("g_up"), jnp.zeros((D_GATE_PAD - D_GATE, D_RWKV), F32)]))
    mu = jnp.concatenate([vec("mu_shift"), jnp.zeros((1, D_SHIFT_PAD - D_SHIFT), F32)], axis=1)
    mu = jnp.broadcast_to(mu, (GROUP, D_SHIFT_PAD))
    zero_row = jnp.zeros((1, D_RWKV), F32)
    pvec = jnp.concatenate([vec("k_k"), vec("w0_f"), vec("a0_f"), vec("k_a_f"), vec("w0_b"), vec("a0_b"),
                            vec("k_a_b"), zero_row], axis=0)
    qvec = jnp.concatenate([vec("gn_w"), vec("gn_b"), vec("r_k_f"), vec("r_k_b"), full["conv_w"], zero_row], axis=0)
    ones_blocks = _head_ones()

    h1 = _rmsnorm_fwd(xs, vec("norm1_w"), tm, "norm1_fwd")
    p_shift = _matmul(h1, w_shift, mode="nn", name="in_proj_shift")
    pc = _matmul(h1, w_conv, mode="nn", name="in_proj_conv")
    ps = _shift_fwd(p_shift, mu, tm, seq)
    kk, w_f, kd_f, b_f, w_b, kd_b, b_b, gate = _prep_fwd(ps, pvec, mats, ones_blocks, tm)
    dirs = ((w_f, kd_f, b_f), (w_b, kd_b, b_b))
    y_f, y_b, ck_f, ck_b = _scan_fwd(ps, kk, dirs, batch, seq)
    mixed = _post_fwd(y_f, y_b, ps, kd_f, kd_b, gate, pc, qvec, ones_blocks, tm, seq)
    x1 = _matmul(mixed, full["w_out"], mode="nn", name="out_proj", add=xs)
    h2 = _rmsnorm_fwd(x1, vec("norm2_w"), tm, "norm2_fwd")
    ff_gate = _matmul(h2, full["w_gate"], mode="nn", name="ffn_gate")
    ff_up = _matmul(h2, full["w_up"], mode="nn", name="ffn_up")
    act = _swiglu_fwd(ff_gate, ff_up, tm)
    x2 = _matmul(act, full["w_down"], mode="nn", name="ffn_down", add=x1)
    d_x2, d_norm_f, loss_part = _loss_head(x2, w["norm_f_w"].reshape(1, -1), target, tm)

    g = {}
    d_act = _matmul(d_x2, full["w_down"], mode="nt", name="ffn_down_dx")
    g["w_down"] = _matmul(act, d_x2, mode="tn", name="ffn_down_dw")
    d_gate, d_up = _swiglu_bwd(ff_gate, ff_up, d_act, tm)
    d_h2 = _matmul(d_gate, full["w_gate"], mode="nt", name="ffn_gate_dx")
    d_h2 = _matmul(d_up, full["w_up"], mode="nt", name="ffn_up_dx", add=d_h2)
    g["w_gate"] = _matmul(h2, d_gate, mode="tn", name="ffn_gate_dw")
    g["w_up"] = _matmul(h2, d_up, mode="tn", name="ffn_up_dw")
    d_x1, d_norm2 = _rmsnorm_bwd(x1, vec("norm2_w"), d_h2, d_x2, tm, "norm2_bwd")
    d_mixed = _matmul(d_x1, full["w_out"], mode="nt", name="out_proj_dx")
    g["w_out"] = _matmul(mixed, d_x1, mode="tn", name="out_proj_dw")
    dy, dr_o, dkdf_o, dkdb_o, dv_o, d_gatev, d_pc, d_qvec = _post_bwd(
        d_mixed, y_f, y_b, ps, kd_f, kd_b, gate, pc, qvec, ones_blocks, tm, seq)
    (dr_f, dw_f, dkd_f, dv_f, dkk_f, db_f), (dr_b, dw_b, dkd_b, dv_b, dkk_b, db_b) = _scan_bwd(
        ps, kk, dirs, dy, ck_f, ck_b, batch, seq)
    cts = [[dr_f, dr_b, dr_o], [dv_f, dv_b, dv_o], [dkk_f, dkk_b], [dw_f], [dkd_f, dkdf_o], [db_f],
           [dw_b], [dkd_b, dkdb_o], [db_b], [d_gatev]]
    q, d_pvec, d_m0, d_m1, d_m2, d_m3, d_m4 = _prep_bwd(ps, pvec, mats, ones_blocks, cts, tm)
    d_pshift, d_mu = _shift_bwd(q, p_shift, mu, tm, seq)
    d_h1 = _matmul(d_pshift, w_shift, mode="nt", name="in_proj_shift_dx")
    d_h1 = _matmul(d_pc, w_conv, mode="nt", name="in_proj_conv_dx", add=d_h1)
    d_w_shift = _matmul(h1, d_pshift, mode="tn", name="in_proj_shift_dw")
    d_w_conv = _matmul(h1, d_pc, mode="tn", name="in_proj_conv_dw")
    g["w_in"] = jnp.concatenate([d_w_shift[:, :D_SHIFT], d_w_conv], axis=1)
    d_x, d_norm1 = _rmsnorm_bwd(xs, vec("norm1_w"), d_h1, d_x1, tm, "norm1_bwd")
    g["w_up_f"], g["a_up_f"] = d_m0[:D_LORA], d_m1[D_LORA:]
    g["w_up_b"], g["a_up_b"] = d_m2[:D_LORA], d_m3[D_LORA:]
    g["g_up"] = d_m4[:D_GATE]
    g["conv_w"] = d_qvec[4:7]

    c = lax.axis_index("c")
    chip = 2 * lax.axis_index("x") + lax.axis_index("y")
    packed = _pack_grads(g)
    keep = lax.dynamic_slice_in_dim(packed, c * PACK_H, PACK_H, axis=1)
    give = lax.dynamic_slice_in_dim(packed, (1 - c) * PACK_H, PACK_H, axis=1)
    got = _swap_with_sibling(give, "swap_halves")
    chip_sum = _add_n([keep.reshape(-1, 1024), got.reshape(-1, 1024)], "add_halves").reshape(N_SHARD, PACK_H, 1024)
    others = _exchange_quarters(chip_sum)
    own = lax.dynamic_index_in_dim(chip_sum, chip, axis=0, keepdims=False)
    eighth = _add_n([own, others[0], others[1], others[2]], "add_quarters")
    grads = _unpack_grad_shard(_join_halves(eighth))

    small = dict(norm1_w=d_norm1, mu_shift=d_mu[:, :D_SHIFT], w0_f=d_pvec[1], w0_b=d_pvec[4], a0_f=d_pvec[2],
                 a0_b=d_pvec[5], k_k=d_pvec[0], k_a_f=d_pvec[3], k_a_b=d_pvec[6], r_k_f=d_qvec[2], r_k_b=d_qvec[3],
                 gn_w=d_qvec[0], gn_b=d_qvec[1], norm2_w=d_norm2, norm_f_w=d_norm_f, loss=loss_part)
    reduced = _unpack_small(_allreduce_small(_pack_small(small)))
    loss = reduced.pop("loss")[0]
    grads.update(reduced)

    outs = {}
    for name in _WEIGHTS:
        shape = w[name].shape
        as2d = (1, shape[0]) if len(shape) == 1 else (-1, shape[-1]) if name not in ("r_k_f", "r_k_b") else (1, -1)
        grad = grads[name].reshape(shape)
        delta, new_m, new_v = _adamw(w[name].reshape(as2d), grad.reshape(as2d), m[name].reshape(as2d),
                                     v[name].reshape(as2d), "adamw_" + name)
        outs[name] = (grad, delta.reshape(shape), new_m.reshape(shape), new_v.reshape(shape))
    d_x = d_x.reshape(batch, seq, D_MODEL)
    return (loss, d_x) + tuple(outs[name][k] for k in range(4) for name in _WEIGHTS)


def kernel(x, norm1_w, w_in, mu_shift, w_up_f, w0_f, w_up_b, w0_b, a_up_f, a0_f, a_up_b, a0_b, g_up, k_k, k_a_f, k_a_b, r_k_f, r_k_b, gn_w, gn_b, conv_w, w_out, norm2_w, w_gate, w_up, w_down, norm_f_w, loss_target, m_norm1_w, m_w_in, m_mu_shift, m_w_up_f, m_w0_f, m_w_up_b, m_w0_b, m_a_up_f, m_a0_f, m_a_up_b, m_a0_b, m_g_up, m_k_k, m_k_a_f, m_k_a_b, m_r_k_f, m_r_k_b, m_gn_w, m_gn_b, m_conv_w, m_w_out, m_norm2_w, m_w_gate, m_w_up, m_w_down, m_norm_f_w, v_norm1_w, v_w_in, v_mu_shift, v_w_up_f, v_w0_f, v_w_up_b, v_w0_b, v_a_up_f, v_a0_f, v_a_up_b, v_a0_b, v_g_up, v_k_k, v_k_a_f, v_k_a_b, v_r_k_f, v_r_k_b, v_gn_w, v_gn_b, v_conv_w, v_w_out, v_norm2_w, v_w_gate, v_w_up, v_w_down, v_norm_f_w):
    args = locals()
    w = {name: args[name] for name in _WEIGHTS}
    m = {name: args["m_" + name] for name in _WEIGHTS}
    v = {name: args["v_" + name] for name in _WEIGHTS}
    return _train_step(x, loss_target, w, m, v)
```

```python
import functools

import jax
import jax.numpy as jnp
from jax import lax
from jax.experimental import pallas as pl
from jax.experimental.pallas import tpu as pltpu

F32 = jnp.float32
BF16 = jnp.bfloat16
MESH = pl.DeviceIdType.MESH

D_MODEL = 1024
D_RWKV = 512
HEAD = 64
N_PAIR = D_RWKV // (2 * HEAD)
D_LORA = 64
D_GATE = 160
D_GATE_PAD = 384
D_FF = 2816
D_SHIFT = 1824
D_SHIFT_PAD = 2048
D_CONV3 = 1536
LOG_DECAY_SCALE = 0.606531
RMS_EPS = 1e-6
GN_EPS = 64e-5
NORM_EPS = 1e-12
ADAM_LR, ADAM_B1, ADAM_B2, ADAM_EPS, ADAM_WD, ADAM_STEP = 0.001, 0.9, 0.999, 1e-08, 0.01, 10

N_SHARD = 4
N_DEV = 8
V7X_VMEM_LIMIT = 48 * 1024 * 1024
SCAN_CHUNK = 64
GROUP = 8

_PACK_ROWS = (("w_in", 840), ("w_out", 256), ("w_gate", 704), ("w_up", 704), ("w_down", 704),
              ("w_up_f", 8), ("w_up_b", 8), ("a_up_f", 8), ("a_up_b", 8), ("g_up", 20), ("conv_w", 1))
PACK_R = 3264
PACK_H = PACK_R // 2
SMALL_ROWS = 24


def _tile(n, cap, mult=128):
    best = None
    t = mult
    while t <= min(n, cap):
        if n % t == 0:
            best = t
        t += mult
    return best or n


def _cp(*sem):
    return pltpu.CompilerParams(dimension_semantics=sem or None, vmem_limit_bytes=V7X_VMEM_LIMIT)


def _sds(shape, dtype=F32):
    return jax.ShapeDtypeStruct(shape, dtype)


def _matmul(a, b, *, mode, name, out_dtype=F32, add=None):
    if mode == "tn":
        r, m = a.shape
        n = b.shape[1]
        tm, tn, tk = _tile(m, 512), _tile(n, 1536), _tile(r, 512, 8)
        nk = r // tk
        a_spec = pl.BlockSpec((tk, tm), lambda i, j, k: (k, i))
        b_spec = pl.BlockSpec((tk, tn), lambda i, j, k: (k, j))
        dims = (((0,), (0,)), ((), ()))
    else:
        m, kdim = a.shape
        n = b.shape[1] if mode == "nn" else b.shape[0]
        tm, tn, nk = _tile(m, 512, 8), _tile(n, 1536), 1
        a_spec = pl.BlockSpec((tm, kdim), lambda i, j, k: (i, 0))
        if mode == "nn":
            b_spec = pl.BlockSpec((kdim, tn), lambda i, j, k: (0, j))
            dims = (((1,), (0,)), ((), ()))
        else:
            b_spec = pl.BlockSpec((tn, kdim), lambda i, j, k: (j, 0))
            dims = (((1,), (1,)), ((), ()))
    has_add = add is not None

    def body(*refs):
        a_ref, b_ref = refs[0], refs[1]
        add_ref = refs[2] if has_add else None
        o_ref = refs[3] if has_add else refs[2]
        part = lax.dot_general(a_ref[...].astype(BF16), b_ref[...].astype(BF16), dims,
                               preferred_element_type=F32)
        if nk == 1:
            if has_add:
                part = part + add_ref[...]
            o_ref[...] = part.astype(out_dtype)
        else:
            acc_ref = refs[-1]
            k = pl.program_id(2)

            @pl.when(k == 0)
            def _():
                acc_ref[...] = jnp.zeros_like(acc_ref)

            acc_ref[...] += part

            @pl.when(k == nk - 1)
            def _():
                res = acc_ref[...]
                if has_add:
                    res = res + add_ref[...]
                o_ref[...] = res.astype(out_dtype)

    o_spec = pl.BlockSpec((tm, tn), lambda i, j, k: (i, j))
    in_specs = [a_spec, b_spec] + ([o_spec] if has_add else [])
    args = (a, b) + ((add,) if has_add else ())
    return pl.pallas_call(
        body, name=name, grid=(m // tm, n // tn, nk), in_specs=in_specs, out_specs=o_spec,
        out_shape=_sds((m, n), out_dtype),
        scratch_shapes=[pltpu.VMEM((tm, tn), F32)] if nk > 1 else [],
        compiler_params=_cp("parallel", "parallel", "arbitrary"),
    )(*args)


def _row(tm, width):
    return pl.BlockSpec((tm, width), lambda i: (i, 0))


def _fixed(shape):
    return pl.BlockSpec(shape, lambda i: tuple(0 for _ in shape))


def _rmsnorm_fwd(x, w, tm, name):
    t, d = x.shape

    def body(x_ref, w_ref, o_ref):
        xv = x_ref[...]
        rstd = lax.rsqrt(jnp.mean(xv * xv, axis=-1, keepdims=True) + RMS_EPS)
        o_ref[...] = (xv * rstd * w_ref[...]).astype(BF16)

    return pl.pallas_call(
        body, name=name, grid=(t // tm,), in_specs=[_row(tm, d), _fixed((1, d))], out_specs=_row(tm, d),
        out_shape=_sds((t, d), BF16), compiler_params=_cp("parallel"))(x, w)


def _rms_bwd_math(xv, wv, dyv):
    rstd = lax.rsqrt(jnp.mean(xv * xv, axis=-1, keepdims=True) + RMS_EPS)
    xhat = xv * rstd
    gv = dyv * wv
    dx = rstd * (gv - xhat * jnp.mean(gv * xhat, axis=-1, keepdims=True))
    return dx, jnp.sum(dyv * xhat, axis=0, keepdims=True)


def _rmsnorm_bwd(x, w, dy, dres, tm, name):
    t, d = x.shape

    def body(x_ref, w_ref, dy_ref, dres_ref, dx_ref, dw_ref):
        dx, dw = _rms_bwd_math(x_ref[...], w_ref[...], dy_ref[...])
        dx_ref[...] = dres_ref[...] + dx

        @pl.when(pl.program_id(0) == 0)
        def _():
            dw_ref[...] = jnp.zeros_like(dw_ref)

        dw_ref[...] += dw

    return pl.pallas_call(
        body, name=name, grid=(t // tm,),
        in_specs=[_row(tm, d), _fixed((1, d)), _row(tm, d), _row(tm, d)],
        out_specs=[_row(tm, d), _fixed((1, d))],
        out_shape=[_sds((t, d)), _sds((1, d))], compiler_params=_cp("arbitrary"))(x, w, dy, dres)


def _loss_head(x, w, target, tm):
    t, d = x.shape

    def body(x_ref, w_ref, t_ref, dx_ref, dw_ref, loss_ref):
        xv, wv = x_ref[...], w_ref[...]
        rstd = lax.rsqrt(jnp.mean(xv * xv, axis=-1, keepdims=True) + RMS_EPS)
        err = xv * rstd * wv - t_ref[...]
        dx, dw = _rms_bwd_math(xv, wv, err * (1.0 / d))
        dx_ref[...] = dx

        @pl.when(pl.program_id(0) == 0)
        def _():
            dw_ref[...] = jnp.zeros_like(dw_ref)
            loss_ref[...] = jnp.zeros_like(loss_ref)

        dw_ref[...] += dw
        loss_ref[...] += 0.5 * jnp.sum(jnp.mean(err * err, axis=-1, keepdims=True), axis=0, keepdims=True)

    return pl.pallas_call(
        body, name="loss_head", grid=(t // tm,),
        in_specs=[_row(tm, d), _fixed((1, d)), _row(tm, d)],
        out_specs=[_row(tm, d), _fixed((1, d)), _fixed((1, 1))],
        out_shape=[_sds((t, d)), _sds((1, d)), _sds((1, 1))], compiler_params=_cp("arbitrary"))(x, w, target)


def _swiglu_fwd(gate, up, tm):
    t, f = gate.shape

    def body(g_ref, u_ref, o_ref):
        gv = g_ref[...]
        o_ref[...] = (gv * jax.nn.sigmoid(gv) * u_ref[...]).astype(BF16)

    return pl.pallas_call(
        body, name="swiglu_fwd", grid=(t // tm,), in_specs=[_row(tm, f), _row(tm, f)], out_specs=_row(tm, f),
        out_shape=_sds((t, f), BF16), compiler_params=_cp("parallel"))(gate, up)


def _swiglu_bwd(gate, up, dact, tm):
    t, f = gate.shape

    def body(g_ref, u_ref, d_ref, dg_ref, du_ref):
        gv, uv, dv = g_ref[...], u_ref[...], d_ref[...]
        sg = jax.nn.sigmoid(gv)
        du_ref[...] = (dv * gv * sg).astype(BF16)
        dg_ref[...] = (dv * uv * (sg * (1.0 + gv * (1.0 - sg)))).astype(BF16)

    return pl.pallas_call(
        body, name="swiglu_bwd", grid=(t // tm,), in_specs=[_row(tm, f)] * 3, out_specs=[_row(tm, f)] * 2,
        out_shape=[_sds((t, f), BF16)] * 2, compiler_params=_cp("parallel"))(gate, up, dact)


def _halo_specs(tm, width, rows_total):
    per = tm // GROUP
    last = rows_total // GROUP - 1
    prev = pl.BlockSpec((GROUP, width), lambda i: (jnp.maximum(i * per - 1, 0), 0))
    nxt = pl.BlockSpec((GROUP, width), lambda i: (jnp.minimum((i + 1) * per, last), 0))
    return prev, nxt


def _edge_flags(tm, seq):
    i = pl.program_id(0)
    has_prev = jnp.where((i * tm) % seq == 0, 0.0, 1.0).astype(F32)
    has_next = jnp.where(((i + 1) * tm) % seq == 0, 0.0, 1.0).astype(F32)
    return has_prev, has_next


def _shifted(xv, prev_row, next_row):
    tm = xv.shape[0]
    row = lax.broadcasted_iota(jnp.int32, xv.shape, 0)
    down = jnp.where(row == 0, prev_row, pltpu.roll(xv, 1, axis=0))
    up = jnp.where(row == tm - 1, next_row, pltpu.roll(xv, tm - 1, axis=0))
    return down, up


def _shift_fwd(p, mu, tm, seq):
    t, w = p.shape
    prev_spec, next_spec = _halo_specs(tm, w, t)

    def body(p_ref, hp_ref, hn_ref, mu_ref, o_ref):
        has_prev, has_next = _edge_flags(tm, seq)
        xv = p_ref[...]
        down, up = _shifted(xv, hp_ref[GROUP - 1:GROUP, :] * has_prev, hn_ref[0:1, :] * has_next)
        o_ref[...] = xv + mu_ref[0:1, :] * (0.5 * (down + up) - xv)

    return pl.pallas_call(
        body, name="shift_fwd", grid=(t // tm,),
        in_specs=[_row(tm, w), prev_spec, next_spec, _fixed((GROUP, w))], out_specs=_row(tm, w),
        out_shape=_sds((t, w)), compiler_params=_cp("parallel"))(p, p, p, mu)


def _shift_bwd(q, p, mu, tm, seq):
    t, w = p.shape
    prev_spec, next_spec = _halo_specs(tm, w, t)

    def body(q_ref, qp_ref, qn_ref, p_ref, pp_ref, pn_ref, mu_ref, dp_ref, dmu_ref):
        has_prev, has_next = _edge_flags(tm, seq)
        muv = mu_ref[0:1, :]
        qv = q_ref[...]
        mq = muv * qv
        mq_down, mq_up = _shifted(mq, muv * qp_ref[GROUP - 1:GROUP, :] * has_prev,
                                  muv * qn_ref[0:1, :] * has_next)
        dp_ref[...] = (qv - mq + 0.5 * (mq_down + mq_up)).astype(BF16)
        pv = p_ref[...]
        p_down, p_up = _shifted(pv, pp_ref[GROUP - 1:GROUP, :] * has_prev, pn_ref[0:1, :] * has_next)

        @pl.when(pl.program_id(0) == 0)
        def _():
            dmu_ref[...] = jnp.zeros_like(dmu_ref)

        dmu_ref[...] += jnp.sum(qv * (0.5 * (p_down + p_up) - pv), axis=0, keepdims=True)

    return pl.pallas_call(
        body, name="shift_bwd", grid=(t // tm,),
        in_specs=[_row(tm, w), prev_spec, next_spec, _row(tm, w), prev_spec, next_spec, _fixed((GROUP, w))],
        out_specs=[_row(tm, w), _fixed((1, w))],
        out_shape=[_sds((t, w), BF16), _sds((1, w))], compiler_params=_cp("arbitrary"))(q, q, q, p, p, p, mu)


@jax.custom_vjp
def _bdot(a, b):
    return jnp.dot(a.astype(BF16), b.astype(BF16), preferred_element_type=F32)


def _bdot_fwd(a, b):
    return _bdot(a, b), (a, b)


def _bdot_bwd(res, g):
    a, b = res
    gb = g.astype(BF16)
    da = lax.dot_general(gb, b.astype(BF16), (((1,), (1,)), ((), ())), preferred_element_type=F32)
    db = lax.dot_general(a.astype(BF16), gb, (((0,), (0,)), ((), ())), preferred_element_type=F32)
    return da, db


_bdot.defvjp(_bdot_fwd, _bdot_bwd)


def _seg_raw(x, ones_blocks):
    hi = x.astype(BF16)
    lo = (x - hi.astype(F32)).astype(BF16)
    return (jnp.dot(hi, ones_blocks, preferred_element_type=F32)
            + jnp.dot(lo, ones_blocks, preferred_element_type=F32))


@jax.custom_vjp
def _seg(x, ones_blocks):
    return _seg_raw(x, ones_blocks)


def _seg_fwd(x, ones_blocks):
    return _seg_raw(x, ones_blocks), ones_blocks


def _seg_bwd(ones_blocks, g):
    return _seg_raw(g, ones_blocks), jnp.zeros_like(ones_blocks)


_seg.defvjp(_seg_fwd, _seg_bwd)


def _head_ones():
    h = jnp.arange(D_RWKV) // HEAD
    return (h[:, None] == h[None, :]).astype(BF16)


def _prep_math(ps, k_k, w0_f, a0_f, k_a_f, w0_b, a0_b, k_a_b, wup_f, aup_f, wup_b, aup_b, gup, ones_blocks):
    r = ps[:, 0:512]
    k = ps[:, 512:1024]
    v = ps[:, 1024:1536]
    xwa = ps[:, 1536:1664]
    xg = ps[:, 1664:D_SHIFT_PAD]
    kk_raw = k * k_k
    norm = jnp.sqrt(_seg(kk_raw * kk_raw, ones_blocks))
    kk = kk_raw / jnp.maximum(norm, NORM_EPS)
    t_xwa = jnp.tanh(xwa)
    outs = [r, v, kk]
    for w0, a0, k_a, wup, aup in ((w0_f, a0_f, k_a_f, wup_f, aup_f), (w0_b, a0_b, k_a_b, wup_b, aup_b)):
        decay = jnp.exp(-LOG_DECAY_SCALE * jax.nn.sigmoid(w0 + _bdot(t_xwa, wup)))
        rate = jax.nn.sigmoid(a0 + _bdot(xwa, aup))
        outs += [decay, k * (1.0 + (rate - 1.0) * k_a), kk * rate]
    outs.append(_bdot(jax.nn.sigmoid(xg), gup))
    return tuple(outs)


def _prep_args(tm, ps_ref, pv_ref, mat_refs, ones_ref):
    vecs = [jnp.broadcast_to(pv_ref[j:j + 1, :], (tm, D_RWKV)) for j in range(7)]
    return [ps_ref[...]] + vecs + [m[...] for m in mat_refs] + [ones_ref[...]]


_PREP_MAT_SHAPES = ((128, D_RWKV),) * 4 + ((D_GATE_PAD, D_RWKV),)


def _prep_fwd(ps, pvec, mats, ones_blocks, tm):
    t = ps.shape[0]

    def body(ps_ref, pv_ref, m0, m1, m2, m3, m4, ones_ref, *out_refs):
        outs = _prep_math(*_prep_args(tm, ps_ref, pv_ref, (m0, m1, m2, m3, m4), ones_ref))
        for o_ref, val in zip(out_refs, outs[2:]):
            o_ref[...] = val

    return pl.pallas_call(
        body, name="prep_fwd", grid=(t // tm,),
        in_specs=[_row(tm, D_SHIFT_PAD), _fixed((8, D_RWKV))] + [_fixed(s) for s in _PREP_MAT_SHAPES]
        + [_fixed((D_RWKV, D_RWKV))],
        out_specs=[_row(tm, D_RWKV)] * 8, out_shape=[_sds((t, D_RWKV))] * 8,
        compiler_params=_cp("parallel"))(ps, pvec, *mats, ones_blocks)


def _prep_bwd(ps, pvec, mats, ones_blocks, cts, tm):
    t = ps.shape[0]
    counts = [len(c) for c in cts]
    flat = [a for c in cts for a in c]

    def body(ps_ref, pv_ref, m0, m1, m2, m3, m4, ones_ref, *refs):
        ct_refs = refs[:len(flat)]
        q_ref, dpv_ref = refs[len(flat)], refs[len(flat) + 1]
        dmat_refs = refs[len(flat) + 2:]
        args = _prep_args(tm, ps_ref, pv_ref, (m0, m1, m2, m3, m4), ones_ref)
        _, vjp = jax.vjp(lambda *a: _prep_math(*a, args[-1]), *args[:-1])
        ct_vals, pos = [], 0
        for n in counts:
            val = ct_refs[pos][...]
            for extra in ct_refs[pos + 1:pos + n]:
                val = val + extra[...]
            ct_vals.append(val)
            pos += n
        grads = vjp(tuple(ct_vals))
        q_ref[...] = grads[0]

        @pl.when(pl.program_id(0) == 0)
        def _():
            dpv_ref[...] = jnp.zeros_like(dpv_ref)
            for d_ref in dmat_refs:
                d_ref[...] = jnp.zeros_like(d_ref)

        for j in range(7):
            dpv_ref[j:j + 1, :] += jnp.sum(grads[1 + j], axis=0, keepdims=True)
        for d_ref, gm in zip(dmat_refs, grads[8:13]):
            d_ref[...] += gm

    return pl.pallas_call(
        body, name="prep_bwd", grid=(t // tm,),
        in_specs=[_row(tm, D_SHIFT_PAD), _fixed((8, D_RWKV))] + [_fixed(s) for s in _PREP_MAT_SHAPES]
        + [_fixed((D_RWKV, D_RWKV))] + [_row(tm, D_RWKV)] * len(flat),
        out_specs=[_row(tm, D_SHIFT_PAD), _fixed((8, D_RWKV))] + [_fixed(s) for s in _PREP_MAT_SHAPES],
        out_shape=[_sds((t, D_SHIFT_PAD)), _sds((8, D_RWKV))] + [_sds(s) for s in _PREP_MAT_SHAPES],
        compiler_params=_cp("arbitrary"))(ps, pvec, *mats, ones_blocks, *flat)


def _pair_ones():
    h = jnp.arange(2 * HEAD) // HEAD
    return (h[:, None] == h[None, :]).astype(BF16)


def _seg_lanes(x, ones_pair):
    hi = x.astype(BF16)
    lo = (x - hi.astype(F32)).astype(BF16)
    return (jnp.dot(hi, ones_pair, preferred_element_type=F32)
            + jnp.dot(lo, ones_pair, preferred_element_type=F32))


def _diag_mask():
    lane = lax.broadcasted_iota(jnp.int32, (HEAD, 2 * HEAD), 1)
    sub = lax.broadcasted_iota(jnp.int32, (HEAD, 2 * HEAD), 0)
    return jnp.where((lane & (HEAD - 1)) == sub, 1.0, 0.0).astype(F32)


def _to_col(row, dmask, ones_pair):
    return _seg_lanes(dmask * row, ones_pair)


def _to_row(cols, dmask):
    return jnp.sum(cols * dmask, axis=0, keepdims=True)


def _wkv_step(state, kk_row, w_row, b_row, kd_row, v_col, ones_pair):
    sa = _seg_lanes(state * kk_row, ones_pair)
    return state * w_row - sa * b_row + v_col * kd_row


def _seg_many(xs, ones_pair):
    parts = []
    for x in xs:
        hi = x.astype(BF16)
        parts += [hi, (x - hi.astype(F32)).astype(BF16)]
    res = jnp.dot(jnp.concatenate(parts, axis=0), ones_pair, preferred_element_type=F32)
    return [res[2 * HEAD * c:2 * HEAD * c + HEAD] + res[2 * HEAD * c + HEAD:2 * HEAD * (c + 1)]
            for c in range(len(xs))]


N_CHAIN = 2 * N_PAIR


def _chain(c):
    d, p = divmod(c, N_PAIR)
    return d, slice(2 * HEAD * p, 2 * HEAD * (p + 1))


def _scan_specs(n_chunks, col_blocks, fwd_chunk, bwd_chunk):
    def spec(chunk_of, col):
        return pl.BlockSpec((SCAN_CHUNK, D_RWKV), lambda b, g: (b * n_chunks + chunk_of(g), col))
    return [spec(fwd_chunk, c) for c in col_blocks] + [spec(bwd_chunk, c) for c in col_blocks]


def _scan_fwd(ps, kk, dirs, batch, seq):
    t = batch * seq
    n = seq // SCAN_CHUNK
    groups = SCAN_CHUNK // GROUP
    up = lambda g: g
    down = lambda g: n - 1 - g
    col_blocks = (0, 2, 0, 0, 0, 0)

    def body(*refs):
        dir_refs = (refs[0:6], refs[6:12])
        ones_ref = refs[12]
        y_refs, ck_refs, st_ref = refs[13:15], refs[15:17], refs[17]

        @pl.when(pl.program_id(1) == 0)
        def _():
            st_ref[...] = jnp.zeros_like(st_ref)

        for d in range(2):
            ck_refs[d][...] = st_ref[N_PAIR * d:N_PAIR * (d + 1)]
        ones_pair = ones_ref[...]
        dmask = _diag_mask()
        sub8 = lax.broadcasted_iota(jnp.int32, (GROUP, 2 * HEAD), 0)

        def group(gi, carry):
            off = (pl.multiple_of(gi * GROUP, GROUP), pl.multiple_of((groups - 1 - gi) * GROUP, GROUP))
            loaded = [tuple(ref[pl.ds(off[d], GROUP), :] for ref in dir_refs[d]) for d in range(2)]
            states = list(carry)
            y_acc = [jnp.zeros((GROUP, 2 * HEAD), F32) for _ in range(N_CHAIN)]
            for step in range(GROUP):
                rows, idx = [], []
                for c in range(N_CHAIN):
                    d, lanes = _chain(c)
                    i = step if d == 0 else GROUP - 1 - step
                    idx.append(i)
                    rows.append(tuple(x8[i:i + 1, lanes] for x8 in loaded[d]))
                v_cols = _seg_many([dmask * rows[c][1] for c in range(N_CHAIN)], ones_pair)
                sas = _seg_many([states[c] * rows[c][2] for c in range(N_CHAIN)], ones_pair)
                for c in range(N_CHAIN):
                    _, _, _, w_row, kd_row, b_row = rows[c]
                    states[c] = states[c] * w_row - sas[c] * b_row + v_cols[c] * kd_row
                ys = _seg_many([states[c] * rows[c][0] for c in range(N_CHAIN)], ones_pair)
                for c in range(N_CHAIN):
                    y_acc[c] = jnp.where(sub8 == idx[c], _to_row(ys[c], dmask), y_acc[c])
            for c in range(N_CHAIN):
                d, lanes = _chain(c)
                y_refs[d][pl.ds(off[d], GROUP), lanes] = y_acc[c]
            return tuple(states)

        final = lax.fori_loop(0, groups, group, tuple(st_ref[c] for c in range(N_CHAIN)))
        for c in range(N_CHAIN):
            st_ref[c] = final[c]

    y_spec_f = pl.BlockSpec((SCAN_CHUNK, D_RWKV), lambda b, g: (b * n + up(g), 0))
    y_spec_b = pl.BlockSpec((SCAN_CHUNK, D_RWKV), lambda b, g: (b * n + down(g), 0))
    ck_shape = (batch, n, N_PAIR, HEAD, 2 * HEAD)
    ck_spec_f = pl.BlockSpec((None, None, N_PAIR, HEAD, 2 * HEAD), lambda b, g: (b, up(g), 0, 0, 0))
    ck_spec_b = pl.BlockSpec((None, None, N_PAIR, HEAD, 2 * HEAD), lambda b, g: (b, down(g), 0, 0, 0))
    ones_spec = pl.BlockSpec((2 * HEAD, 2 * HEAD), lambda b, g: (0, 0))
    (wf, kdf, bf), (wb, kdb, bb) = dirs
    return pl.pallas_call(
        body, name="wkv_fwd", grid=(batch, n),
        in_specs=_scan_specs(n, col_blocks, up, down) + [ones_spec],
        out_specs=[y_spec_f, y_spec_b, ck_spec_f, ck_spec_b],
        out_shape=[_sds((t, D_RWKV)), _sds((t, D_RWKV)), _sds(ck_shape), _sds(ck_shape)],
        scratch_shapes=[pltpu.VMEM((N_CHAIN, HEAD, 2 * HEAD), F32)],
        compiler_params=_cp("parallel", "arbitrary"),
    )(ps, ps, kk, wf, kdf, bf, ps, ps, kk, wb, kdb, bb, _pair_ones())


def _scan_bwd(ps, kk, dirs, dy, ck_f, ck_b, batch, seq):
    t = batch * seq
    n = seq // SCAN_CHUNK
    groups = SCAN_CHUNK // GROUP
    fwd_chunk = lambda g: n - 1 - g
    bwd_chunk = lambda g: g
    col_blocks = (0, 2, 0, 0, 0, 0, 0)

    def recompute(dir_refs, ck_refs, hist_ref, ones_pair, dmask):
        def group(gi, states):
            states = list(states)
            loaded = []
            for d in range(2):
                blk = gi if d == 0 else groups - 1 - gi
                rows = pl.ds(pl.multiple_of(blk * GROUP, GROUP), GROUP)
                loaded.append(tuple(ref[rows, :] for ref in dir_refs[d][1:6]))
            for step in range(GROUP):
                rows = []
                for c in range(N_CHAIN):
                    d, lanes = _chain(c)
                    i = step if d == 0 else GROUP - 1 - step
                    rows.append(tuple(x8[i:i + 1, lanes] for x8 in loaded[d]))
                    hist_ref[c, gi * GROUP + step] = states[c]
                v_cols = _seg_many([dmask * rows[c][0] for c in range(N_CHAIN)], ones_pair)
                sas = _seg_many([states[c] * rows[c][1] for c in range(N_CHAIN)], ones_pair)
                for c in range(N_CHAIN):
                    _, _, w_row, kd_row, b_row = rows[c]
                    states[c] = states[c] * w_row - sas[c] * b_row + v_cols[c] * kd_row
            return tuple(states)

        first = tuple(ck_refs[d][p] for d in range(2) for p in range(N_PAIR))
        last = lax.fori_loop(0, groups, group, first)
        for c in range(N_CHAIN):
            hist_ref[c, SCAN_CHUNK] = last[c]

    def undo_group(dir_refs, out_refs, hist_ref, gi, d_states, ones_pair, dmask, sub8):
        d_states = list(d_states)
        loaded, blocks = [], []
        for d in range(2):
            blk = groups - 1 - gi if d == 0 else gi
            blocks.append(pl.ds(pl.multiple_of(blk * GROUP, GROUP), GROUP))
            loaded.append(tuple(ref[blocks[d], :] for ref in dir_refs[d]))
        acc = [[jnp.zeros((GROUP, 2 * HEAD), F32) for _ in range(6)] for _ in range(N_CHAIN)]
        for step in range(GROUP):
            rows, idx, before, after = [], [], [], []
            for c in range(N_CHAIN):
                d, lanes = _chain(c)
                i = GROUP - 1 - step if d == 0 else step
                q = (groups - 1 - gi) * GROUP + i if d == 0 else SCAN_CHUNK - 1 - (gi * GROUP + i)
                idx.append(i)
                rows.append(tuple(x8[i:i + 1, lanes] for x8 in loaded[d]))
                before.append(hist_ref[c, q])
                after.append(hist_ref[c, q + 1])
            cols = _seg_many([dmask * rows[c][j] for c in range(N_CHAIN) for j in (1, 6)], ones_pair)
            v_cols, dy_cols = cols[0::2], cols[1::2]
            d_now = [d_states[c] + dy_cols[c] * rows[c][0] for c in range(N_CHAIN)]
            sums = _seg_many([x for c in range(N_CHAIN)
                              for x in (before[c] * rows[c][2], d_now[c] * rows[c][5], d_now[c] * rows[c][4])],
                             ones_pair)
            for c in range(N_CHAIN):
                sa, d_sa, dv_cols = sums[3 * c], -sums[3 * c + 1], sums[3 * c + 2]
                rows_out = (
                    jnp.sum(after[c] * dy_cols[c], axis=0, keepdims=True),
                    jnp.sum(d_now[c] * before[c], axis=0, keepdims=True),
                    jnp.sum(d_now[c] * v_cols[c], axis=0, keepdims=True),
                    _to_row(dv_cols, dmask),
                    jnp.sum(before[c] * d_sa, axis=0, keepdims=True),
                    -jnp.sum(d_now[c] * sa, axis=0, keepdims=True),
                )
                acc[c] = [jnp.where(sub8 == idx[c], val, a) for val, a in zip(rows_out, acc[c])]
                d_states[c] = d_now[c] * rows[c][3] + d_sa * rows[c][2]
        for c in range(N_CHAIN):
            d, lanes = _chain(c)
            for o_ref, val in zip(out_refs[d], acc[c]):
                o_ref[blocks[d], lanes] = val
        return tuple(d_states)

    def body(*refs):
        dir_refs = (refs[0:7], refs[7:14])
        ck_refs, ones_ref = refs[14:16], refs[16]
        out_refs = (refs[17:23], refs[23:29])
        dst_ref, hist_ref = refs[29:31]

        @pl.when(pl.program_id(1) == 0)
        def _():
            dst_ref[...] = jnp.zeros_like(dst_ref)

        ones_pair = ones_ref[...]
        dmask = _diag_mask()
        sub8 = lax.broadcasted_iota(jnp.int32, (GROUP, 2 * HEAD), 0)
        recompute(dir_refs, ck_refs, hist_ref, ones_pair, dmask)

        def group(gi, carry):
            return undo_group(dir_refs, out_refs, hist_ref, gi, carry, ones_pair, dmask, sub8)

        final = lax.fori_loop(0, groups, group, tuple(dst_ref[c] for c in range(N_CHAIN)))
        for c in range(N_CHAIN):
            dst_ref[c] = final[c]

    blk = (SCAN_CHUNK, D_RWKV)
    out_f = pl.BlockSpec(blk, lambda b, g: (b * n + fwd_chunk(g), 0))
    out_b = pl.BlockSpec(blk, lambda b, g: (b * n + bwd_chunk(g), 0))
    ck_spec_f = pl.BlockSpec((None, None, N_PAIR, HEAD, 2 * HEAD), lambda b, g: (b, fwd_chunk(g), 0, 0, 0))
    ck_spec_b = pl.BlockSpec((None, None, N_PAIR, HEAD, 2 * HEAD), lambda b, g: (b, bwd_chunk(g), 0, 0, 0))
    ones_spec = pl.BlockSpec((2 * HEAD, 2 * HEAD), lambda b, g: (0, 0))
    (wf, kdf, bf), (wb, kdb, bb) = dirs
    outs = pl.pallas_call(
        body, name="wkv_bwd", grid=(batch, n),
        in_specs=_scan_specs(n, col_blocks, fwd_chunk, bwd_chunk) + [ck_spec_f, ck_spec_b, ones_spec],
        out_specs=[out_f] * 6 + [out_b] * 6,
        out_shape=[_sds((t, D_RWKV))] * 12,
        scratch_shapes=[pltpu.VMEM((N_CHAIN, HEAD, 2 * HEAD), F32),
                        pltpu.VMEM((N_CHAIN, SCAN_CHUNK + 1, HEAD, 2 * HEAD), F32)],
        compiler_params=_cp("parallel", "arbitrary"),
    )(ps, ps, kk, wf, kdf, bf, dy, ps, ps, kk, wb, kdb, bb, dy, ck_f, ck_b, _pair_ones())
    return outs[0:6], outs[6:12]


def _post_math(y, r, kd_f, kd_b, v, gate, gn_w, gn_b, rk_f, rk_b, ones_blocks):
    mean = _seg(y, ones_blocks) * (1.0 / HEAD)
    yc = y - mean
    var = _seg(yc * yc, ones_blocks) * (1.0 / HEAD)
    yn = yc * lax.rsqrt(var + GN_EPS) * gn_w + gn_b
    bonus = _seg(r * kd_f * rk_f, ones_blocks) * v + _seg(r * kd_b * rk_b, ones_blocks) * v
    return (yn + bonus) * gate


def _conv_parts(pc, halo_prev, halo_next, has_prev, has_next):
    gate_b, gate_c, hid = pc[:, 0:512], pc[:, 512:1024], pc[:, 1024:1536]
    u = gate_c * hid
    u_prev_row = halo_prev[GROUP - 1:GROUP, 512:1024] * halo_prev[GROUP - 1:GROUP, 1024:1536] * has_prev
    u_next_row = halo_next[0:1, 512:1024] * halo_next[0:1, 1024:1536] * has_next
    u_down, u_up = _shifted(u, u_prev_row, u_next_row)
    return gate_b, gate_c, hid, u, u_down, u_up


def _post_specs(tm, t):
    pc_prev, pc_next = _halo_specs(tm, D_CONV3, t)
    col = lambda c: pl.BlockSpec((tm, D_RWKV), lambda i: (i, c))
    return ([col(0), col(0), col(0), col(0), col(0), col(2), col(0), _row(tm, D_CONV3), pc_prev, pc_next,
             _fixed((8, D_RWKV)), _fixed((D_RWKV, D_RWKV))])


def _post_fwd(y_f, y_b, ps, kd_f, kd_b, gate, pc, qvec, ones_blocks, tm, seq):
    t = ps.shape[0]

    def body(yf_ref, yb_ref, r_ref, kdf_ref, kdb_ref, v_ref, g_ref, pc_ref, hp_ref, hn_ref, qv_ref, ones_ref,
             o_ref):
        has_prev, has_next = _edge_flags(tm, seq)
        vec = [jnp.broadcast_to(qv_ref[j:j + 1, :], (tm, D_RWKV)) for j in range(7)]
        o_ref[:, 0:D_RWKV] = _post_math(yf_ref[...] + yb_ref[...], r_ref[...], kdf_ref[...], kdb_ref[...],
                                        v_ref[...], g_ref[...], vec[0], vec[1], vec[2], vec[3],
                                        ones_ref[...]).astype(BF16)
        gate_b, _, _, u, u_down, u_up = _conv_parts(pc_ref[...], hp_ref[...], hn_ref[...], has_prev, has_next)
        o_ref[:, D_RWKV:2 * D_RWKV] = (gate_b * (vec[4] * u_down + vec[5] * u + vec[6] * u_up)).astype(BF16)

    return pl.pallas_call(
        body, name="post_fwd", grid=(t // tm,), in_specs=_post_specs(tm, t), out_specs=_row(tm, D_MODEL),
        out_shape=_sds((t, D_MODEL), BF16), compiler_params=_cp("parallel"),
    )(y_f, y_b, ps, kd_f, kd_b, ps, gate, pc, pc, pc, qvec, ones_blocks)


def _post_bwd(d_out, y_f, y_b, ps, kd_f, kd_b, gate, pc, qvec, ones_blocks, tm, seq):
    t = ps.shape[0]
    do_prev, do_next = _halo_specs(tm, D_MODEL, t)

    def body(do_ref, dop_ref, don_ref, yf_ref, yb_ref, r_ref, kdf_ref, kdb_ref, v_ref, g_ref, pc_ref, hp_ref,
             hn_ref, qv_ref, ones_ref, dy_ref, dr_ref, dkdf_ref, dkdb_ref, dv_ref, dg_ref, dpc_ref, dqv_ref):
        has_prev, has_next = _edge_flags(tm, seq)
        vec = [jnp.broadcast_to(qv_ref[j:j + 1, :], (tm, D_RWKV)) for j in range(7)]
        ones_v = ones_ref[...]
        args = (yf_ref[...] + yb_ref[...], r_ref[...], kdf_ref[...], kdb_ref[...], v_ref[...], g_ref[...],
                vec[0], vec[1], vec[2], vec[3])
        _, vjp = jax.vjp(lambda *a: _post_math(*a, ones_v), *args)
        grads = vjp(do_ref[:, 0:D_RWKV])
        for o_ref, gval in zip((dy_ref, dr_ref, dkdf_ref, dkdb_ref, dv_ref, dg_ref), grads[0:6]):
            o_ref[...] = gval

        hp, hn = hp_ref[...], hn_ref[...]
        gate_b, gate_c, hid, u, u_down, u_up = _conv_parts(pc_ref[...], hp, hn, has_prev, has_next)
        d_oc = do_ref[:, D_RWKV:2 * D_RWKV]
        d_cu = d_oc * gate_b
        d_cu_prev = dop_ref[GROUP - 1:GROUP, D_RWKV:2 * D_RWKV] * hp[GROUP - 1:GROUP, 0:512] * has_prev
        d_cu_next = don_ref[0:1, D_RWKV:2 * D_RWKV] * hn[0:1, 0:512] * has_next
        d_cu_down, d_cu_up = _shifted(d_cu, d_cu_prev, d_cu_next)
        d_u = vec[5] * d_cu + vec[4] * d_cu_up + vec[6] * d_cu_down
        dpc_ref[:, 0:512] = (d_oc * (vec[4] * u_down + vec[5] * u + vec[6] * u_up)).astype(BF16)
        dpc_ref[:, 512:1024] = (d_u * hid).astype(BF16)
        dpc_ref[:, 1024:1536] = (d_u * gate_c).astype(BF16)

        @pl.when(pl.program_id(0) == 0)
        def _():
            dqv_ref[...] = jnp.zeros_like(dqv_ref)

        vec_grads = list(grads[6:10]) + [d_cu * u_down, d_cu * u, d_cu * u_up]
        for j, gval in enumerate(vec_grads):
            dqv_ref[j:j + 1, :] += jnp.sum(gval, axis=0, keepdims=True)

    return pl.pallas_call(
        body, name="post_bwd", grid=(t // tm,),
        in_specs=[_row(tm, D_MODEL), do_prev, do_next] + _post_specs(tm, t),
        out_specs=[_row(tm, D_RWKV)] * 6 + [_row(tm, D_CONV3), _fixed((8, D_RWKV))],
        out_shape=[_sds((t, D_RWKV))] * 6 + [_sds((t, D_CONV3), BF16), _sds((8, D_RWKV))],
        compiler_params=_cp("arbitrary"),
    )(d_out, d_out, d_out, y_f, y_b, ps, kd_f, kd_b, ps, gate, pc, pc, pc, qvec, ones_blocks)


def _adamw(w, g, m, v, name):
    r, c = w.shape
    tr = _tile(r, 256, 8)
    spec = pl.BlockSpec((tr, c), lambda i: (i, 0))

    def body(w_ref, g_ref, m_ref, v_ref, d_ref, nm_ref, nv_ref):
        gv = g_ref[...]
        m2 = ADAM_B1 * m_ref[...] + (1.0 - ADAM_B1) * gv
        v2 = ADAM_B2 * v_ref[...] + (1.0 - ADAM_B2) * (gv * gv)
        m_hat = m2 / (1.0 - ADAM_B1 ** ADAM_STEP)
        v_hat = v2 / (1.0 - ADAM_B2 ** ADAM_STEP)
        d_ref[...] = -ADAM_LR * (m_hat / (jnp.sqrt(v_hat) + ADAM_EPS) + ADAM_WD * w_ref[...])
        nm_ref[...] = m2
        nv_ref[...] = v2

    return pl.pallas_call(
        body, name=name, grid=(r // tr,), in_specs=[spec] * 4, out_specs=[spec] * 3,
        out_shape=[_sds((r, c))] * 3, compiler_params=_cp("parallel"))(w, g, m, v)


def _add_n(parts, name):
    r, c = parts[0].shape
    tr = _tile(r, 408, 8)
    spec = pl.BlockSpec((tr, c), lambda i: (i, 0))

    def body(*refs):
        acc = refs[0][...]
        for ref in refs[1:-1]:
            acc = acc + ref[...]
        refs[-1][...] = acc

    return pl.pallas_call(
        body, name=name, grid=(r // tr,), in_specs=[spec] * len(parts), out_specs=spec,
        out_shape=_sds((r, c)), compiler_params=_cp("parallel"))(*parts)


_ANY = pl.BlockSpec(memory_space=pl.ANY)


def _place():
    return lax.axis_index("x"), lax.axis_index("y"), lax.axis_index("c")


def _other_chips(x, y):
    return [(1 - x, y), (x, 1 - y), (1 - x, 1 - y)]


def _remote(src, dst, send_sems, recv_sems, k, to):
    return pltpu.make_async_remote_copy(src_ref=src, dst_ref=dst, send_sem=send_sems.at[k],
                                        recv_sem=recv_sems.at[k], device_id=to, device_id_type=MESH)


def _gather_weights(pack):
    rows, width = pack.shape
    half = rows // 2

    def body(x_ref, out_ref, send_sems, recv_sems, local_sem):
        x, y, c = _place()
        sibling = (x, y, 1 - c)
        chips = _other_chips(x, y)

        def block(chip, part):
            return out_ref.at[2 * chip[0] + chip[1], pl.ds(part * half, half), :]

        mine = pltpu.make_async_copy(x_ref, out_ref.at[2 * x + y], local_sem)
        mine.start()
        first = [_remote(x_ref.at[pl.ds(c * half, half), :], block((x, y), c), send_sems, recv_sems, j, (*chip, c))
                 for j, chip in enumerate(chips)]
        for cp in first:
            cp.start()
        passed = [_remote(block(chip, c), block(chip, c), send_sems, recv_sems, 3 + j, sibling)
                  for j, chip in enumerate(chips)]
        for j, chip in enumerate(chips):
            _remote(block(chip, c), block(chip, c), send_sems, recv_sems, j, sibling).wait_recv()
            passed[j].start()
        for j, chip in enumerate(chips):
            _remote(block(chip, 1 - c), block(chip, 1 - c), send_sems, recv_sems, 3 + j, sibling).wait_recv()
        for cp in first + passed:
            cp.wait_send()
        mine.wait()

    return pl.pallas_call(
        body, name="gather_weights", in_specs=[_ANY], out_specs=_ANY,
        out_shape=_sds((N_SHARD, rows, width), pack.dtype),
        scratch_shapes=[pltpu.SemaphoreType.DMA((6,)), pltpu.SemaphoreType.DMA((6,)), pltpu.SemaphoreType.DMA],
    )(pack)


def _swap_with_sibling(block, name):
    def body(x_ref, out_ref, send_sems, recv_sems):
        x, y, c = _place()
        cp = _remote(x_ref, out_ref, send_sems, recv_sems, 0, (x, y, 1 - c))
        cp.start()
        cp.wait()

    return pl.pallas_call(
        body, name=name, in_specs=[_ANY], out_specs=_ANY, out_shape=_sds(block.shape, block.dtype),
        scratch_shapes=[pltpu.SemaphoreType.DMA((1,)), pltpu.SemaphoreType.DMA((1,))],
    )(block)


def _exchange_quarters(parts):
    _, rows, width = parts.shape

    def body(x_ref, out_ref, send_sems, recv_sems):
        x, y, c = _place()
        copies = [_remote(x_ref.at[2 * chip[0] + chip[1]], out_ref.at[j], send_sems, recv_sems, j, (*chip, c))
                  for j, chip in enumerate(_other_chips(x, y))]
        for cp in copies:
            cp.start()
        for cp in copies:
            cp.wait()

    return pl.pallas_call(
        body, name="exchange_quarters", in_specs=[_ANY], out_specs=_ANY,
        out_shape=_sds((3, rows, width), parts.dtype),
        scratch_shapes=[pltpu.SemaphoreType.DMA((3,)), pltpu.SemaphoreType.DMA((3,))],
    )(parts)


def _join_halves(mine):
    half, width = mine.shape

    def body(x_ref, out_ref, send_sems, recv_sems, local_sem):
        x, y, c = _place()
        own = pltpu.make_async_copy(x_ref, out_ref.at[pl.ds(c * half, half), :], local_sem)
        own.start()
        cp = _remote(x_ref, out_ref.at[pl.ds(c * half, half), :], send_sems, recv_sems, 0, (x, y, 1 - c))
        cp.start()
        cp.wait_send()
        _remote(x_ref, out_ref.at[pl.ds((1 - c) * half, half), :], send_sems, recv_sems, 0, (x, y, 1 - c)).wait_recv()
        own.wait()

    return pl.pallas_call(
        body, name="join_halves", in_specs=[_ANY], out_specs=_ANY, out_shape=_sds((2 * half, width), mine.dtype),
        scratch_shapes=[pltpu.SemaphoreType.DMA((1,)), pltpu.SemaphoreType.DMA((1,)), pltpu.SemaphoreType.DMA],
    )(mine)


def _allreduce_small(vec):
    rows, width = vec.shape
    vmem = pl.BlockSpec(memory_space=pltpu.VMEM)

    def body(x_ref, o_ref, buf_ref, send_sems, recv_sems):
        x, y, c = _place()
        me = 4 * x + 2 * y + c
        buf_ref[me] = x_ref[...]
        copies = []
        for k in range(1, N_DEV):
            peer = (x ^ ((k >> 2) & 1), y ^ ((k >> 1) & 1), c ^ (k & 1))
            copies.append(_remote(x_ref, buf_ref.at[me], send_sems, recv_sems, k - 1, peer))
        for cp in copies:
            cp.start()
        for k in range(1, N_DEV):
            _remote(x_ref, buf_ref.at[me ^ k], send_sems, recv_sems, k - 1, (x, y, c)).wait_recv()
        for cp in copies:
            cp.wait_send()
        total = buf_ref[0]
        for d in range(1, N_DEV):
            total = total + buf_ref[d]
        o_ref[...] = total

    return pl.pallas_call(
        body, name="allreduce_small", in_specs=[vmem], out_specs=vmem, out_shape=_sds((rows, width)),
        scratch_shapes=[pltpu.VMEM((N_DEV, rows, width), F32), pltpu.SemaphoreType.DMA((N_DEV - 1,)),
                        pltpu.SemaphoreType.DMA((N_DEV - 1,))],
    )(vec)


def _rows1024(a):
    return a.reshape(-1, 1024)


def _pad_rows(a, rows):
    return jnp.concatenate([a, jnp.zeros((rows - a.shape[0], a.shape[1]), a.dtype)], axis=0)


def _pack_weight_shards(w):
    conv_bits = lax.bitcast_convert_type(w["conv_w"], BF16).reshape(1, -1)
    conv_row = jnp.concatenate([conv_bits, jnp.zeros((1, 1024 - conv_bits.shape[1]), BF16)], axis=1)
    parts = [_rows1024(w[name].astype(BF16)) for name, _ in _PACK_ROWS[:-1]] + [conv_row]
    return _pad_rows(jnp.concatenate(parts, axis=0), PACK_R)


def _unpack_gathered(gathered):
    out, row = {}, 0
    for name, n in _PACK_ROWS:
        out[name] = gathered[:, row:row + n]
        row += n
    cols = lambda a, k: jnp.concatenate([a[s].reshape(k, -1) for s in range(N_SHARD)], axis=1)
    conv = lax.bitcast_convert_type(out["conv_w"][:, 0, :768].reshape(N_SHARD, 3, 128, 2), F32)
    return dict(
        w_in=cols(out["w_in"], D_MODEL), w_out=out["w_out"].reshape(D_MODEL, D_MODEL),
        w_gate=cols(out["w_gate"], D_MODEL), w_up=cols(out["w_up"], D_MODEL),
        w_down=out["w_down"].reshape(D_FF, D_MODEL),
        w_up_f=cols(out["w_up_f"], D_LORA), w_up_b=cols(out["w_up_b"], D_LORA),
        a_up_f=cols(out["a_up_f"], D_LORA), a_up_b=cols(out["a_up_b"], D_LORA),
        g_up=cols(out["g_up"], D_GATE), conv_w=jnp.concatenate([conv[s] for s in range(N_SHARD)], axis=1))


def _pack_grads(g):
    col_split = lambda a, s: a[:, s * (a.shape[1] // N_SHARD):(s + 1) * (a.shape[1] // N_SHARD)]
    row_split = lambda a, s: a[s * (a.shape[0] // N_SHARD):(s + 1) * (a.shape[0] // N_SHARD)]
    slots = []
    for s in range(N_SHARD):
        conv = col_split(g["conv_w"], s).reshape(1, -1)
        parts = [_rows1024(col_split(g["w_in"], s)), row_split(g["w_out"], s), _rows1024(col_split(g["w_gate"], s)),
                 _rows1024(col_split(g["w_up"], s)), row_split(g["w_down"], s)]
        parts += [_rows1024(col_split(g[name], s)) for name in ("w_up_f", "w_up_b", "a_up_f", "a_up_b", "g_up")]
        parts.append(jnp.concatenate([conv, jnp.zeros((1, 1024 - conv.shape[1]), F32)], axis=1))
        slots.append(_pad_rows(jnp.concatenate(parts, axis=0), PACK_R))
    return jnp.stack(slots)


def _unpack_grad_shard(pack):
    shapes = dict(w_in=(D_MODEL, 840), w_out=(256, D_MODEL), w_gate=(D_MODEL, 704), w_up=(D_MODEL, 704),
                  w_down=(704, D_MODEL), w_up_f=(D_LORA, 128), w_up_b=(D_LORA, 128), a_up_f=(D_LORA, 128),
                  a_up_b=(D_LORA, 128), g_up=(D_GATE, 128))
    out, row = {}, 0
    for name, n in _PACK_ROWS[:-1]:
        out[name] = pack[row:row + n].reshape(shapes[name])
        row += n
    out["conv_w"] = pack[row, :384].reshape(3, 128)
    return out


_SMALL_LAYOUT = (("norm1_w", 1024), ("mu_shift", D_SHIFT), ("w0_f", 512), ("w0_b", 512), ("a0_f", 512),
                 ("a0_b", 512), ("k_k", 512), ("k_a_f", 512), ("k_a_b", 512), ("r_k_f", 512), ("r_k_b", 512),
                 ("gn_w", 512), ("gn_b", 512), ("norm2_w", 1024), ("norm_f_w", 1024), ("loss", 1))


def _pack_small(vals):
    rows = []
    for name, n in _SMALL_LAYOUT:
        flat = vals[name].reshape(-1)
        n_rows = -(-n // 1024)
        rows.append(jnp.concatenate([flat, jnp.zeros((n_rows * 1024 - n,), F32)]).reshape(n_rows, 1024))
    return _pad_rows(jnp.concatenate(rows, axis=0), SMALL_ROWS)


def _unpack_small(pack):
    out, row = {}, 0
    for name, n in _SMALL_LAYOUT:
        n_rows = -(-n // 1024)
        out[name] = pack[row:row + n_rows].reshape(-1)[:n]
        row += n_rows
    return out


_WEIGHTS = ("norm1_w", "w_in", "mu_shift", "w_up_f", "w0_f", "w_up_b", "w0_b", "a_up_f", "a0_f", "a_up_b", "a0_b",
            "g_up", "k_k", "k_a_f", "k_a_b", "r_k_f", "r_k_b", "gn_w", "gn_b", "conv_w", "w_out", "norm2_w",
            "w_gate", "w_up", "w_down", "norm_f_w")


def _train_step(x, loss_target, w, m, v):
    batch, seq, _ = x.shape
    t = batch * seq
    tm = _tile(seq, 256, 8)
    xs = x.reshape(t, D_MODEL)
    target = loss_target.reshape(t, D_MODEL)
    vec = lambda name: w[name].reshape(1, -1)

    local = {name: w[name][0] for name, _ in _PACK_ROWS}
    full = _unpack_gathered(_gather_weights(_pack_weight_shards(local)))
    w_in = full["w_in"]
    w_shift = jnp.concatenate([w_in[:, :D_SHIFT], jnp.zeros((D_MODEL, D_SHIFT_PAD - D_SHIFT), BF16)], axis=1)
    w_conv = w_in[:, D_SHIFT:]
    zeros_lora = jnp.zeros((D_LORA, D_RWKV), F32)
    lora = lambda name: full[name].astype(F32)
    mats = (jnp.concatenate([lora("w_up_f"), zeros_lora]), jnp.concatenate([zeros_lora, lora("a_up_f")]),
            jnp.concatenate([lora("w_up_b"), zeros_lora]), jnp.concatenate([zeros_lora, lora("a_up_b")]),
            jnp.concatenate([lora("g_up"), jnp.zeros((D_GATE_PAD - D_GATE, D_RWKV), F32)]))
    mu = jnp.concatenate([vec("mu_shift"), jnp.zeros((1, D_SHIFT_PAD - D_SHIFT), F32)], axis=1)
    mu = jnp.broadcast_to(mu, (GROUP, D_SHIFT_PAD))
    zero_row = jnp.zeros((1, D_RWKV), F32)
    pvec = jnp.concatenate([vec("k_k"), vec("w0_f"), vec("a0_f"), vec("k_a_f"), vec("w0_b"), vec("a0_b"),
                            vec("k_a_b"), zero_row], axis=0)
    qvec = jnp.concatenate([vec("gn_w"), vec("gn_b"), vec("r_k_f"), vec("r_k_b"), full["conv_w"], zero_row], axis=0)
    ones_blocks = _head_ones()

    h1 = _rmsnorm_fwd(xs, vec("norm1_w"), tm, "norm1_fwd")
    p_shift = _matmul(h1, w_shift, mode="nn", name="in_proj_shift")
    pc = _matmul(h1, w_conv, mode="nn", name="in_proj_conv")
    ps = _shift_fwd(p_shift, mu, tm, seq)
    kk, w_f, kd_f, b_f, w_b, kd_b, b_b, gate = _prep_fwd(ps, pvec, mats, ones_blocks, tm)
    dirs = ((w_f, kd_f, b_f), (w_b, kd_b, b_b))
    y_f, y_b, ck_f, ck_b = _scan_fwd(ps, kk, dirs, batch, seq)
    mixed = _post_fwd(y_f, y_b, ps, kd_f, kd_b, gate, pc, qvec, ones_blocks, tm, seq)
    x1 = _matmul(mixed, full["w_out"], mode="nn", name="out_proj", add=xs)
    h2 = _rmsnorm_fwd(x1, vec("norm2_w"), tm, "norm2_fwd")
    ff_gate = _matmul(h2, full["w_gate"], mode="nn", name="ffn_gate")
    ff_up = _matmul(h2, full["w_up"], mode="nn", name="ffn_up")
    act = _swiglu_fwd(ff_gate, ff_up, tm)
    x2 = _matmul(act, full["w_down"], mode="nn", name="ffn_down", add=x1)
    d_x2, d_norm_f, loss_part = _loss_head(x2, w["norm_f_w"].reshape(1, -1), target, tm)

    g = {}
    d_act = _matmul(d_x2, full["w_down"], mode="nt", name="ffn_down_dx")
    g["w_down"] = _matmul(act, d_x2, mode="tn", name="ffn_down_dw")
    d_gate, d_up = _swiglu_bwd(ff_gate, ff_up, d_act, tm)
    d_h2 = _matmul(d_gate, full["w_gate"], mode="nt", name="ffn_gate_dx")
    d_h2 = _matmul(d_up, full["w_up"], mode="nt", name="ffn_up_dx", add=d_h2)
    g["w_gate"] = _matmul(h2, d_gate, mode="tn", name="ffn_gate_dw")
    g["w_up"] = _matmul(h2, d_up, mode="tn", name="ffn_up_dw")
    d_x1, d_norm2 = _rmsnorm_bwd(x1, vec("norm2_w"), d_h2, d_x2, tm, "norm2_bwd")
    d_mixed = _matmul(d_x1, full["w_out"], mode="nt", name="out_proj_dx")
    g["w_out"] = _matmul(mixed, d_x1, mode="tn", name="out_proj_dw")
    dy, dr_o, dkdf_o, dkdb_o, dv_o, d_gatev, d_pc, d_qvec = _post_bwd(
        d_mixed, y_f, y_b, ps, kd_f, kd_b, gate, pc, qvec, ones_blocks, tm, seq)
    (dr_f, dw_f, dkd_f, dv_f, dkk_f, db_f), (dr_b, dw_b, dkd_b, dv_b, dkk_b, db_b) = _scan_bwd(
        ps, kk, dirs, dy, ck_f, ck_b, batch, seq)
    cts = [[dr_f, dr_b, dr_o], [dv_f, dv_b, dv_o], [dkk_f, dkk_b], [dw_f], [dkd_f, dkdf_o], [db_f],
           [dw_b], [dkd_b, dkdb_o], [db_b], [d_gatev]]
    q, d_pvec, d_m0, d_m1, d_m2, d_m3, d_m4 = _prep_bwd(ps, pvec, mats, ones_blocks, cts, tm)
    d_pshift, d_mu = _shift_bwd(q, p_shift, mu, tm, seq)
    d_h1 = _matmul(d_pshift, w_shift, mode="nt", name="in_proj_shift_dx")
    d_h1 = _matmul(d_pc, w_conv, mode="nt", name="in_proj_conv_dx", add=d_h1)
    d_w_shift = _matmul(h1, d_pshift, mode="tn", name="in_proj_shift_dw")
    d_w_conv = _matmul(h1, d_pc, mode="tn", name="in_proj_conv_dw")
    g["w_in"] = jnp.concatenate([d_w_shift[:, :D_SHIFT], d_w_conv], axis=1)
    d_x, d_norm1 = _rmsnorm_bwd(xs, vec("norm1_w"), d_h1, d_x1, tm, "norm1_bwd")
    g["w_up_f"], g["a_up_f"] = d_m0[:D_LORA], d_m1[D_LORA:]
    g["w_up_b"], g["a_up_b"] = d_m2[:D_LORA], d_m3[D_LORA:]
    g["g_up"] = d_m4[:D_GATE]
    g["conv_w"] = d_qvec[4:7]

    c = lax.axis_index("c")
    chip = 2 * lax.axis_index("x") + lax.axis_index("y")
    packed = _pack_grads(g)
    keep = lax.dynamic_slice_in_dim(packed, c * PACK_H, PACK_H, axis=1)
    give = lax.dynamic_slice_in_dim(packed, (1 - c) * PACK_H, PACK_H, axis=1)
    got = _swap_with_sibling(give, "swap_halves")
    chip_sum = _add_n([keep.reshape(-1, 1024), got.reshape(-1, 1024)], "add_halves").reshape(N_SHARD, PACK_H, 1024)
    others = _exchange_quarters(chip_sum)
    own = lax.dynamic_index_in_dim(chip_sum, chip, axis=0, keepdims=False)
    eighth = _add_n([own, others[0], others[1], others[2]], "add_quarters")
    grads = _unpack_grad_shard(_join_halves(eighth))

    small = dict(norm1_w=d_norm1, mu_shift=d_mu[:, :D_SHIFT], w0_f=d_pvec[1], w0_b=d_pvec[4], a0_f=d_pvec[2],
                 a0_b=d_pvec[5], k_k=d_pvec[0], k_a_f=d_pvec[3], k_a_b=d_pvec[6], r_k_f=d_qvec[2], r_k_b=d_qvec[3],
                 gn_w=d_qvec[0], gn_b=d_qvec[1], norm2_w=d_norm2, norm_f_w=d_norm_f, loss=loss_part)
    reduced = _unpack_small(_allreduce_small(_pack_small(small)))
    loss = reduced.pop("loss")[0]
    grads.update(reduced)

    outs = {}
    for name in _WEIGHTS:
        shape = w[name].shape
        as2d = (1, shape[0]) if len(shape) == 1 else (-1, shape[-1]) if name not in ("r_k_f", "r_k_b") else (1, -1)
        grad = grads[name].reshape(shape)
        delta, new_m, new_v = _adamw(w[name].reshape(as2d), grad.reshape(as2d), m[name].reshape(as2d),
                                     v[name].reshape(as2d), "adamw_" + name)
        outs[name] = (grad, delta.reshape(shape), new_m.reshape(shape), new_v.reshape(shape))
    d_x = d_x.reshape(batch, seq, D_MODEL)
    return (loss, d_x) + tuple(outs[name][k] for k in range(4) for name in _WEIGHTS)


def kernel(x, norm1_w, w_in, mu_shift, w_up_f, w0_f, w_up_b, w0_b, a_up_f, a0_f, a_up_b, a0_b, g_up, k_k, k_a_f, k_a_b, r_k_f, r_k_b, gn_w, gn_b, conv_w, w_out, norm2_w, w_gate, w_up, w_down, norm_f_w, loss_target, m_norm1_w, m_w_in, m_mu_shift, m_w_up_f, m_w0_f, m_w_up_b, m_w0_b, m_a_up_f, m_a0_f, m_a_up_b, m_a0_b, m_g_up, m_k_k, m_k_a_f, m_k_a_b, m_r_k_f, m_r_k_b, m_gn_w, m_gn_b, m_conv_w, m_w_out, m_norm2_w, m_w_gate, m_w_up, m_w_down, m_norm_f_w, v_norm1_w, v_w_in, v_mu_shift, v_w_up_f, v_w0_f, v_w_up_b, v_w0_b, v_a_up_f, v_a0_f, v_a_up_b, v_a0_b, v_g_up, v_k_k, v_k_a_f, v_k_a_b, v_r_k_f, v_r_k_b, v_gn_w, v_gn_b, v_conv_w, v_w_out, v_norm2_w, v_w_gate, v_w_up, v_w_down, v_norm_f_w):
    args = locals()
    w = {name: args[name] for name in _WEIGHTS}
    m = {name: args["m_" + name] for name in _WEIGHTS}
    v = {name: args["v_" + name] for name in _WEIGHTS}
    return _train_step(x, loss_target, w, m, v)
```

```python
import functools

import jax
import jax.numpy as jnp
from jax import lax
from jax.experimental import pallas as pl
from jax.experimental.pallas import tpu as pltpu

F32 = jnp.float32
BF16 = jnp.bfloat16
MESH = pl.DeviceIdType.MESH

D_MODEL = 1024
D_RWKV = 512
HEAD = 64
N_PAIR = D_RWKV // (2 * HEAD)
D_LORA = 64
D_GATE = 160
D_GATE_PAD = 384
D_FF = 2816
D_SHIFT = 1824
D_SHIFT_PAD = 2048
D_CONV3 = 1536
LOG_DECAY_SCALE = 0.606531
RMS_EPS = 1e-6
GN_EPS = 64e-5
NORM_EPS = 1e-12
ADAM_LR, ADAM_B1, ADAM_B2, ADAM_EPS, ADAM_WD, ADAM_STEP = 0.001, 0.9, 0.999, 1e-08, 0.01, 10

N_SHARD = 4
N_DEV = 8
V7X_VMEM_LIMIT = 48 * 1024 * 1024
SCAN_CHUNK = 64
GROUP = 8

_PACK_ROWS = (("w_in", 840), ("w_out", 256), ("w_gate", 704), ("w_up", 704), ("w_down", 704),
              ("w_up_f", 8), ("w_up_b", 8), ("a_up_f", 8), ("a_up_b", 8), ("g_up", 20), ("conv_w", 1))
PACK_R = 3264
PACK_H = PACK_R // 2
SMALL_ROWS = 24


def _tile(n, cap, mult=128):
    best = None
    t = mult
    while t <= min(n, cap):
        if n % t == 0:
            best = t
        t += mult
    return best or n


def _cp(*sem):
    return pltpu.CompilerParams(dimension_semantics=sem or None, vmem_limit_bytes=V7X_VMEM_LIMIT)


def _sds(shape, dtype=F32):
    return jax.ShapeDtypeStruct(shape, dtype)


def _matmul(a, b, *, mode, name, out_dtype=F32, add=None):
    if mode == "tn":
        r, m = a.shape
        n = b.shape[1]
        tm, tn, tk = _tile(m, 512), _tile(n, 1536), _tile(r, 512, 8)
        nk = r // tk
        a_spec = pl.BlockSpec((tk, tm), lambda i, j, k: (k, i))
        b_spec = pl.BlockSpec((tk, tn), lambda i, j, k: (k, j))
        dims = (((0,), (0,)), ((), ()))
    else:
        m, kdim = a.shape
        n = b.shape[1] if mode == "nn" else b.shape[0]
        tm, tn, nk = _tile(m, 512, 8), _tile(n, 1536), 1
        a_spec = pl.BlockSpec((tm, kdim), lambda i, j, k: (i, 0))
        if mode == "nn":
            b_spec = pl.BlockSpec((kdim, tn), lambda i, j, k: (0, j))
            dims = (((1,), (0,)), ((), ()))
        else:
            b_spec = pl.BlockSpec((tn, kdim), lambda i, j, k: (j, 0))
            dims = (((1,), (1,)), ((), ()))
    has_add = add is not None

    def body(*refs):
        a_ref, b_ref = refs[0], refs[1]
        add_ref = refs[2] if has_add else None
        o_ref = refs[3] if has_add else refs[2]
        part = lax.dot_general(a_ref[...].astype(BF16), b_ref[...].astype(BF16), dims,
                               preferred_element_type=F32)
        if nk == 1:
            if has_add:
                part = part + add_ref[...]
            o_ref[...] = part.astype(out_dtype)
        else:
            acc_ref = refs[-1]
            k = pl.program_id(2)

            @pl.when(k == 0)
            def _():
                acc_ref[...] = jnp.zeros_like(acc_ref)

            acc_ref[...] += part

            @pl.when(k == nk - 1)
            def _():
                res = acc_ref[...]
                if has_add:
                    res = res + add_ref[...]
                o_ref[...] = res.astype(out_dtype)

    o_spec = pl.BlockSpec((tm, tn), lambda i, j, k: (i, j))
    in_specs = [a_spec, b_spec] + ([o_spec] if has_add else [])
    args = (a, b) + ((add,) if has_add else ())
    return pl.pallas_call(
        body, name=name, grid=(m // tm, n // tn, nk), in_specs=in_specs, out_specs=o_spec,
        out_shape=_sds((m, n), out_dtype),
        scratch_shapes=[pltpu.VMEM((tm, tn), F32)] if nk > 1 else [],
        compiler_params=_cp("parallel", "parallel", "arbitrary"),
    )(*args)


def _row(tm, width):
    return pl.BlockSpec((tm, width), lambda i: (i, 0))


def _fixed(shape):
    return pl.BlockSpec(shape, lambda i: tuple(0 for _ in shape))


def _rmsnorm_fwd(x, w, tm, name):
    t, d = x.shape

    def body(x_ref, w_ref, o_ref):
        xv = x_ref[...]
        rstd = lax.rsqrt(jnp.mean(xv * xv, axis=-1, keepdims=True) + RMS_EPS)
        o_ref[...] = (xv * rstd * w_ref[...]).astype(BF16)

    return pl.pallas_call(
        body, name=name, grid=(t // tm,), in_specs=[_row(tm, d), _fixed((1, d))], out_specs=_row(tm, d),
        out_shape=_sds((t, d), BF16), compiler_params=_cp("parallel"))(x, w)


def _rms_bwd_math(xv, wv, dyv):
    rstd = lax.rsqrt(jnp.mean(xv * xv, axis=-1, keepdims=True) + RMS_EPS)
    xhat = xv * rstd
    gv = dyv * wv
    dx = rstd * (gv - xhat * jnp.mean(gv * xhat, axis=-1, keepdims=True))
    return dx, jnp.sum(dyv * xhat, axis=0, keepdims=True)


def _rmsnorm_bwd(x, w, dy, dres, tm, name):
    t, d = x.shape

    def body(x_ref, w_ref, dy_ref, dres_ref, dx_ref, dw_ref):
        dx, dw = _rms_bwd_math(x_ref[...], w_ref[...], dy_ref[...])
        dx_ref[...] = dres_ref[...] + dx

        @pl.when(pl.program_id(0) == 0)
        def _():
            dw_ref[...] = jnp.zeros_like(dw_ref)

        dw_ref[...] += dw

    return pl.pallas_call(
        body, name=name, grid=(t // tm,),
        in_specs=[_row(tm, d), _fixed((1, d)), _row(tm, d), _row(tm, d)],
        out_specs=[_row(tm, d), _fixed((1, d))],
        out_shape=[_sds((t, d)), _sds((1, d))], compiler_params=_cp("arbitrary"))(x, w, dy, dres)


def _loss_head(x, w, target, tm):
    t, d = x.shape

    def body(x_ref, w_ref, t_ref, dx_ref, dw_ref, loss_ref):
        xv, wv = x_ref[...], w_ref[...]
        rstd = lax.rsqrt(jnp.mean(xv * xv, axis=-1, keepdims=True) + RMS_EPS)
        err = xv * rstd * wv - t_ref[...]
        dx, dw = _rms_bwd_math(xv, wv, err * (1.0 / d))
        dx_ref[...] = dx

        @pl.when(pl.program_id(0) == 0)
        def _():
            dw_ref[...] = jnp.zeros_like(dw_ref)
            loss_ref[...] = jnp.zeros_like(loss_ref)

        dw_ref[...] += dw
        loss_ref[...] += 0.5 * jnp.sum(jnp.mean(err * err, axis=-1, keepdims=True), axis=0, keepdims=True)

    return pl.pallas_call(
        body, name="loss_head", grid=(t // tm,),
        in_specs=[_row(tm, d), _fixed((1, d)), _row(tm, d)],
        out_specs=[_row(tm, d), _fixed((1, d)), _fixed((1, 1))],
        out_shape=[_sds((t, d)), _sds((1, d)), _sds((1, 1))], compiler_params=_cp("arbitrary"))(x, w, target)


def _swiglu_fwd(gate, up, tm):
    t, f = gate.shape

    def body(g_ref, u_ref, o_ref):
        gv = g_ref[...]
        o_ref[...] = (gv * jax.nn.sigmoid(gv) * u_ref[...]).astype(BF16)

    return pl.pallas_call(
        body, name="swiglu_fwd", grid=(t // tm,), in_specs=[_row(tm, f), _row(tm, f)], out_specs=_row(tm, f),
        out_shape=_sds((t, f), BF16), compiler_params=_cp("parallel"))(gate, up)


def _swiglu_bwd(gate, up, dact, tm):
    t, f = gate.shape

    def body(g_ref, u_ref, d_ref, dg_ref, du_ref):
        gv, uv, dv = g_ref[...], u_ref[...], d_ref[...]
        sg = jax.nn.sigmoid(gv)
        du_ref[...] = (dv * gv * sg).astype(BF16)
        dg_ref[...] = (dv * uv * (sg * (1.0 + gv * (1.0 - sg)))).astype(BF16)

    return pl.pallas_call(
        body, name="swiglu_bwd", grid=(t // tm,), in_specs=[_row(tm, f)] * 3, out_specs=[_row(tm, f)] * 2,
        out_shape=[_sds((t, f), BF16)] * 2, compiler_params=_cp("parallel"))(gate, up, dact)


def _halo_specs(tm, width, rows_total):
    per = tm // GROUP
    last = rows_total // GROUP - 1
    prev = pl.BlockSpec((GROUP, width), lambda i: (jnp.maximum(i * per - 1, 0), 0))
    nxt = pl.BlockSpec((GROUP, width), lambda i: (jnp.minimum((i + 1) * per, last), 0))
    return prev, nxt


def _edge_flags(tm, seq):
    i = pl.program_id(0)
    has_prev = jnp.where((i * tm) % seq == 0, 0.0, 1.0).astype(F32)
    has_next = jnp.where(((i + 1) * tm) % seq == 0, 0.0, 1.0).astype(F32)
    return has_prev, has_next


def _shifted(xv, prev_row, next_row):
    tm = xv.shape[0]
    row = lax.broadcasted_iota(jnp.int32, xv.shape, 0)
    down = jnp.where(row == 0, prev_row, pltpu.roll(xv, 1, axis=0))
    up = jnp.where(row == tm - 1, next_row, pltpu.roll(xv, tm - 1, axis=0))
    return down, up


def _shift_fwd(p, mu, tm, seq):
    t, w = p.shape
    prev_spec, next_spec = _halo_specs(tm, w, t)

    def body(p_ref, hp_ref, hn_ref, mu_ref, o_ref):
        has_prev, has_next = _edge_flags(tm, seq)
        xv = p_ref[...]
        down, up = _shifted(xv, hp_ref[GROUP - 1:GROUP, :] * has_prev, hn_ref[0:1, :] * has_next)
        o_ref[...] = xv + mu_ref[0:1, :] * (0.5 * (down + up) - xv)

    return pl.pallas_call(
        body, name="shift_fwd", grid=(t // tm,),
        in_specs=[_row(tm, w), prev_spec, next_spec, _fixed((GROUP, w))], out_specs=_row(tm, w),
        out_shape=_sds((t, w)), compiler_params=_cp("parallel"))(p, p, p, mu)


def _shift_bwd(q, p, mu, tm, seq):
    t, w = p.shape
    prev_spec, next_spec = _halo_specs(tm, w, t)

    def body(q_ref, qp_ref, qn_ref, p_ref, pp_ref, pn_ref, mu_ref, dp_ref, dmu_ref):
        has_prev, has_next = _edge_flags(tm, seq)
        muv = mu_ref[0:1, :]
        qv = q_ref[...]
        mq = muv * qv
        mq_down, mq_up = _shifted(mq, muv * qp_ref[GROUP - 1:GROUP, :] * has_prev,
                                  muv * qn_ref[0:1, :] * has_next)
        dp_ref[...] = (qv - mq + 0.5 * (mq_down + mq_up)).astype(BF16)
        pv = p_ref[...]
        p_down, p_up = _shifted(pv, pp_ref[GROUP - 1:GROUP, :] * has_prev, pn_ref[0:1, :] * has_next)

        @pl.when(pl.program_id(0) == 0)
        def _():
            dmu_ref[...] = jnp.zeros_like(dmu_ref)

        dmu_ref[...] += jnp.sum(qv * (0.5 * (p_down + p_up) - pv), axis=0, keepdims=True)

    return pl.pallas_call(
        body, name="shift_bwd", grid=(t // tm,),
        in_specs=[_row(tm, w), prev_spec, next_spec, _row(tm, w), prev_spec, next_spec, _fixed((GROUP, w))],
        out_specs=[_row(tm, w), _fixed((1, w))],
        out_shape=[_sds((t, w), BF16), _sds((1, w))], compiler_params=_cp("arbitrary"))(q, q, q, p, p, p, mu)


@jax.custom_vjp
def _bdot(a, b):
    return jnp.dot(a.astype(BF16), b.astype(BF16), preferred_element_type=F32)


def _bdot_fwd(a, b):
    return _bdot(a, b), (a, b)


def _bdot_bwd(res, g):
    a, b = res
    gb = g.astype(BF16)
    da = lax.dot_general(gb, b.astype(BF16), (((1,), (1,)), ((), ())), preferred_element_type=F32)
    db = lax.dot_general(a.astype(BF16), gb, (((0,), (0,)), ((), ())), preferred_element_type=F32)
    return da, db


_bdot.defvjp(_bdot_fwd, _bdot_bwd)


def _seg_raw(x, ones_blocks):
    hi = x.astype(BF16)
    lo = (x - hi.astype(F32)).astype(BF16)
    return (jnp.dot(hi, ones_blocks, preferred_element_type=F32)
            + jnp.dot(lo, ones_blocks, preferred_element_type=F32))


@jax.custom_vjp
def _seg(x, ones_blocks):
    return _seg_raw(x, ones_blocks)


def _seg_fwd(x, ones_blocks):
    return _seg_raw(x, ones_blocks), ones_blocks


def _seg_bwd(ones_blocks, g):
    return _seg_raw(g, ones_blocks), jnp.zeros_like(ones_blocks)


_seg.defvjp(_seg_fwd, _seg_bwd)


def _head_ones():
    h = jnp.arange(D_RWKV) // HEAD
    return (h[:, None] == h[None, :]).astype(BF16)


def _prep_math(ps, k_k, w0_f, a0_f, k_a_f, w0_b, a0_b, k_a_b, wup_f, aup_f, wup_b, aup_b, gup, ones_blocks):
    r = ps[:, 0:512]
    k = ps[:, 512:1024]
    v = ps[:, 1024:1536]
    xwa = ps[:, 1536:1664]
    xg = ps[:, 1664:D_SHIFT_PAD]
    kk_raw = k * k_k
    norm = jnp.sqrt(_seg(kk_raw * kk_raw, ones_blocks))
    kk = kk_raw / jnp.maximum(norm, NORM_EPS)
    t_xwa = jnp.tanh(xwa)
    outs = [r, v, kk]
    for w0, a0, k_a, wup, aup in ((w0_f, a0_f, k_a_f, wup_f, aup_f), (w0_b, a0_b, k_a_b, wup_b, aup_b)):
        decay = jnp.exp(-LOG_DECAY_SCALE * jax.nn.sigmoid(w0 + _bdot(t_xwa, wup)))
        rate = jax.nn.sigmoid(a0 + _bdot(xwa, aup))
        outs += [decay, k * (1.0 + (rate - 1.0) * k_a), kk * rate]
    outs.append(_bdot(jax.nn.sigmoid(xg), gup))
    return tuple(outs)


def _prep_args(tm, ps_ref, pv_ref, mat_refs, ones_ref):
    vecs = [jnp.broadcast_to(pv_ref[j:j + 1, :], (tm, D_RWKV)) for j in range(7)]
    return [ps_ref[...]] + vecs + [m[...] for m in mat_refs] + [ones_ref[...]]


_PREP_MAT_SHAPES = ((128, D_RWKV),) * 4 + ((D_GATE_PAD, D_RWKV),)


def _prep_fwd(ps, pvec, mats, ones_blocks, tm):
    t = ps.shape[0]

    def body(ps_ref, pv_ref, m0, m1, m2, m3, m4, ones_ref, *out_refs):
        outs = _prep_math(*_prep_args(tm, ps_ref, pv_ref, (m0, m1, m2, m3, m4), ones_ref))
        for o_ref, val in zip(out_refs, outs[2:]):
            o_ref[...] = val

    return pl.pallas_call(
        body, name="prep_fwd", grid=(t // tm,),
        in_specs=[_row(tm, D_SHIFT_PAD), _fixed((8, D_RWKV))] + [_fixed(s) for s in _PREP_MAT_SHAPES]
        + [_fixed((D_RWKV, D_RWKV))],
        out_specs=[_row(tm, D_RWKV)] * 8, out_shape=[_sds((t, D_RWKV))] * 8,
        compiler_params=_cp("parallel"))(ps, pvec, *mats, ones_blocks)


def _prep_bwd(ps, pvec, mats, ones_blocks, cts, tm):
    t = ps.shape[0]
    counts = [len(c) for c in cts]
    flat = [a for c in cts for a in c]

    def body(ps_ref, pv_ref, m0, m1, m2, m3, m4, ones_ref, *refs):
        ct_refs = refs[:len(flat)]
        q_ref, dpv_ref = refs[len(flat)], refs[len(flat) + 1]
        dmat_refs = refs[len(flat) + 2:]
        args = _prep_args(tm, ps_ref, pv_ref, (m0, m1, m2, m3, m4), ones_ref)
        _, vjp = jax.vjp(lambda *a: _prep_math(*a, args[-1]), *args[:-1])
        ct_vals, pos = [], 0
        for n in counts:
            val = ct_refs[pos][...]
            for extra in ct_refs[pos + 1:pos + n]:
                val = val + extra[...]
            ct_vals.append(val)
            pos += n
        grads = vjp(tuple(ct_vals))
        q_ref[...] = grads[0]

        @pl.when(pl.program_id(0) == 0)
        def _():
            dpv_ref[...] = jnp.zeros_like(dpv_ref)
            for d_ref in dmat_refs:
                d_ref[...] = jnp.zeros_like(d_ref)

        for j in range(7):
            dpv_ref[j:j + 1, :] += jnp.sum(grads[1 + j], axis=0, keepdims=True)
        for d_ref, gm in zip(dmat_refs, grads[8:13]):
            d_ref[...] += gm

    return pl.pallas_call(
        body, name="prep_bwd", grid=(t // tm,),
        in_specs=[_row(tm, D_SHIFT_PAD), _fixed((8, D_RWKV))] + [_fixed(s) for s in _PREP_MAT_SHAPES]
        + [_fixed((D_RWKV, D_RWKV))] + [_row(tm, D_RWKV)] * len(flat),
        out_specs=[_row(tm, D_SHIFT_PAD), _fixed((8, D_RWKV))] + [_fixed(s) for s in _PREP_MAT_SHAPES],
        out_shape=[_sds((t, D_SHIFT_PAD)), _sds((8, D_RWKV))] + [_sds(s) for s in _PREP_MAT_SHAPES],
        compiler_params=_cp("arbitrary"))(ps, pvec, *mats, ones_blocks, *flat)


def _pair_ones():
    h = jnp.arange(2 * HEAD) // HEAD
    return (h[:, None] == h[None, :]).astype(BF16)


def _diag_mask():
    lane = lax.broadcasted_iota(jnp.int32, (HEAD, 2 * HEAD), 1)
    sub = lax.broadcasted_iota(jnp.int32, (HEAD, 2 * HEAD), 0)
    return jnp.where((lane & (HEAD - 1)) == sub, 1.0, 0.0).astype(F32)


def _to_row(cols, dmask):
    return jnp.sum(cols * dmask, axis=0, keepdims=True)


def _seg_many(exact, rounded, ones_pair):
    parts = []
    for x in exact:
        hi = x.astype(BF16)
        parts += [hi, (x - hi.astype(F32)).astype(BF16)]
    parts += [x.astype(BF16) for x in rounded]
    res = jnp.dot(jnp.concatenate(parts, axis=0), ones_pair, preferred_element_type=F32)
    n2 = 2 * len(exact)
    out_exact = [res[HEAD * (2 * c):HEAD * (2 * c + 1)] + res[HEAD * (2 * c + 1):HEAD * (2 * c + 2)]
                 for c in range(len(exact))]
    out_rounded = [res[HEAD * (n2 + c):HEAD * (n2 + c + 1)] for c in range(len(rounded))]
    return out_exact, out_rounded


N_CHAIN = 2 * N_PAIR


def _chain(c):
    d, p = divmod(c, N_PAIR)
    return d, slice(2 * HEAD * p, 2 * HEAD * (p + 1))


def _scan_specs(n_chunks, col_blocks, fwd_chunk, bwd_chunk):
    def spec(chunk_of, col):
        return pl.BlockSpec((SCAN_CHUNK, D_RWKV), lambda b, g: (b * n_chunks + chunk_of(g), col))
    return [spec(fwd_chunk, c) for c in col_blocks] + [spec(bwd_chunk, c) for c in col_blocks]


def _scan_fwd(ps, kk, dirs, batch, seq):
    t = batch * seq
    n = seq // SCAN_CHUNK
    groups = SCAN_CHUNK // GROUP
    up = lambda g: g
    down = lambda g: n - 1 - g
    col_blocks = (0, 2, 0, 0, 0, 0)

    def body(*refs):
        dir_refs = (refs[0:6], refs[6:12])
        ones_ref = refs[12]
        y_refs, ck_refs, st_ref = refs[13:15], refs[15:17], refs[17]

        @pl.when(pl.program_id(1) == 0)
        def _():
            st_ref[...] = jnp.zeros_like(st_ref)

        for d in range(2):
            ck_refs[d][...] = st_ref[N_PAIR * d:N_PAIR * (d + 1)]
        ones_pair = ones_ref[...]
        dmask = _diag_mask()
        sub8 = lax.broadcasted_iota(jnp.int32, (GROUP, 2 * HEAD), 0)

        def group(gi, carry):
            off = (pl.multiple_of(gi * GROUP, GROUP), pl.multiple_of((groups - 1 - gi) * GROUP, GROUP))
            loaded = [tuple(ref[pl.ds(off[d], GROUP), :] for ref in dir_refs[d]) for d in range(2)]
            states = list(carry)
            y_acc = [jnp.zeros((GROUP, 2 * HEAD), F32) for _ in range(N_CHAIN)]
            for step in range(GROUP):
                rows, idx = [], []
                for c in range(N_CHAIN):
                    d, lanes = _chain(c)
                    i = step if d == 0 else GROUP - 1 - step
                    idx.append(i)
                    rows.append(tuple(x8[i:i + 1, lanes] for x8 in loaded[d]))
                sas, v_cols = _seg_many([states[c] * rows[c][2] for c in range(N_CHAIN)],
                                        [dmask * rows[c][1] for c in range(N_CHAIN)], ones_pair)
                for c in range(N_CHAIN):
                    _, _, _, w_row, kd_row, b_row = rows[c]
                    states[c] = states[c] * w_row - sas[c] * b_row + v_cols[c] * kd_row
                _, ys = _seg_many([], [states[c] * rows[c][0] for c in range(N_CHAIN)], ones_pair)
                for c in range(N_CHAIN):
                    y_acc[c] = jnp.where(sub8 == idx[c], _to_row(ys[c], dmask), y_acc[c])
            for c in range(N_CHAIN):
                d, lanes = _chain(c)
                y_refs[d][pl.ds(off[d], GROUP), lanes] = y_acc[c]
            return tuple(states)

        final = lax.fori_loop(0, groups, group, tuple(st_ref[c] for c in range(N_CHAIN)))
        for c in range(N_CHAIN):
            st_ref[c] = final[c]

    y_spec_f = pl.BlockSpec((SCAN_CHUNK, D_RWKV), lambda b, g: (b * n + up(g), 0))
    y_spec_b = pl.BlockSpec((SCAN_CHUNK, D_RWKV), lambda b, g: (b * n + down(g), 0))
    ck_shape = (batch, n, N_PAIR, HEAD, 2 * HEAD)
    ck_spec_f = pl.BlockSpec((None, None, N_PAIR, HEAD, 2 * HEAD), lambda b, g: (b, up(g), 0, 0, 0))
    ck_spec_b = pl.BlockSpec((None, None, N_PAIR, HEAD, 2 * HEAD), lambda b, g: (b, down(g), 0, 0, 0))
    ones_spec = pl.BlockSpec((2 * HEAD, 2 * HEAD), lambda b, g: (0, 0))
    (wf, kdf, bf), (wb, kdb, bb) = dirs
    return pl.pallas_call(
        body, name="wkv_fwd", grid=(batch, n),
        in_specs=_scan_specs(n, col_blocks, up, down) + [ones_spec],
        out_specs=[y_spec_f, y_spec_b, ck_spec_f, ck_spec_b],
        out_shape=[_sds((t, D_RWKV)), _sds((t, D_RWKV)), _sds(ck_shape), _sds(ck_shape)],
        scratch_shapes=[pltpu.VMEM((N_CHAIN, HEAD, 2 * HEAD), F32)],
        compiler_params=_cp("parallel", "arbitrary"),
    )(ps, ps, kk, wf, kdf, bf, ps, ps, kk, wb, kdb, bb, _pair_ones())


def _scan_bwd(ps, kk, dirs, dy, ck_f, ck_b, batch, seq):
    t = batch * seq
    n = seq // SCAN_CHUNK
    groups = SCAN_CHUNK // GROUP
    fwd_chunk = lambda g: n - 1 - g
    bwd_chunk = lambda g: g
    col_blocks = (0, 2, 0, 0, 0, 0, 0)

    def recompute(dir_refs, ck_refs, hist_ref, ones_pair, dmask):
        def group(gi, states):
            states = list(states)
            loaded = []
            for d in range(2):
                blk = gi if d == 0 else groups - 1 - gi
                rows = pl.ds(pl.multiple_of(blk * GROUP, GROUP), GROUP)
                loaded.append(tuple(ref[rows, :] for ref in dir_refs[d][1:6]))
            for step in range(GROUP):
                rows = []
                for c in range(N_CHAIN):
                    d, lanes = _chain(c)
                    i = step if d == 0 else GROUP - 1 - step
                    rows.append(tuple(x8[i:i + 1, lanes] for x8 in loaded[d]))
                    hist_ref[c, gi * GROUP + step] = states[c]
                sas, v_cols = _seg_many([states[c] * rows[c][1] for c in range(N_CHAIN)],
                                        [dmask * rows[c][0] for c in range(N_CHAIN)], ones_pair)
                for c in range(N_CHAIN):
                    _, _, w_row, kd_row, b_row = rows[c]
                    states[c] = states[c] * w_row - sas[c] * b_row + v_cols[c] * kd_row
            return tuple(states)

        first = tuple(ck_refs[d][p] for d in range(2) for p in range(N_PAIR))
        last = lax.fori_loop(0, groups, group, first)
        for c in range(N_CHAIN):
            hist_ref[c, SCAN_CHUNK] = last[c]

    def undo_group(dir_refs, out_refs, hist_ref, gi, d_states, ones_pair, dmask, sub8):
        d_states = list(d_states)
        loaded, blocks = [], []
        for d in range(2):
            blk = groups - 1 - gi if d == 0 else gi
            blocks.append(pl.ds(pl.multiple_of(blk * GROUP, GROUP), GROUP))
            loaded.append(tuple(ref[blocks[d], :] for ref in dir_refs[d]))
        acc = [[jnp.zeros((GROUP, 2 * HEAD), F32) for _ in range(6)] for _ in range(N_CHAIN)]
        for step in range(GROUP):
            rows, idx, before, after = [], [], [], []
            for c in range(N_CHAIN):
                d, lanes = _chain(c)
                i = GROUP - 1 - step if d == 0 else step
                q = (groups - 1 - gi) * GROUP + i if d == 0 else SCAN_CHUNK - 1 - (gi * GROUP + i)
                idx.append(i)
                rows.append(tuple(x8[i:i + 1, lanes] for x8 in loaded[d]))
                before.append(hist_ref[c, q])
                after.append(hist_ref[c, q + 1])
            _, cols = _seg_many([], [dmask * rows[c][j] for c in range(N_CHAIN) for j in (1, 6)], ones_pair)
            v_cols, dy_cols = cols[0::2], cols[1::2]
            d_now = [d_states[c] + dy_cols[c] * rows[c][0] for c in range(N_CHAIN)]
            d_sas, others = _seg_many(
                [d_now[c] * rows[c][5] for c in range(N_CHAIN)],
                [x for c in range(N_CHAIN) for x in (before[c] * rows[c][2], d_now[c] * rows[c][4])], ones_pair)
            for c in range(N_CHAIN):
                sa, d_sa, dv_cols = others[2 * c], -d_sas[c], others[2 * c + 1]
                rows_out = (
                    jnp.sum(after[c] * dy_cols[c], axis=0, keepdims=True),
                    jnp.sum(d_now[c] * before[c], axis=0, keepdims=True),
                    jnp.sum(d_now[c] * v_cols[c], axis=0, keepdims=True),
                    _to_row(dv_cols, dmask),
                    jnp.sum(before[c] * d_sa, axis=0, keepdims=True),
                    -jnp.sum(d_now[c] * sa, axis=0, keepdims=True),
                )
                acc[c] = [jnp.where(sub8 == idx[c], val, a) for val, a in zip(rows_out, acc[c])]
                d_states[c] = d_now[c] * rows[c][3] + d_sa * rows[c][2]
        for c in range(N_CHAIN):
            d, lanes = _chain(c)
            for o_ref, val in zip(out_refs[d], acc[c]):
                o_ref[blocks[d], lanes] = val
        return tuple(d_states)

    def body(*refs):
        dir_refs = (refs[0:7], refs[7:14])
        ck_refs, ones_ref = refs[14:16], refs[16]
        out_refs = (refs[17:23], refs[23:29])
        dst_ref, hist_ref = refs[29:31]

        @pl.when(pl.program_id(1) == 0)
        def _():
            dst_ref[...] = jnp.zeros_like(dst_ref)

        ones_pair = ones_ref[...]
        dmask = _diag_mask()
        sub8 = lax.broadcasted_iota(jnp.int32, (GROUP, 2 * HEAD), 0)
        recompute(dir_refs, ck_refs, hist_ref, ones_pair, dmask)

        def group(gi, carry):
            return undo_group(dir_refs, out_refs, hist_ref, gi, carry, ones_pair, dmask, sub8)

        final = lax.fori_loop(0, groups, group, tuple(dst_ref[c] for c in range(N_CHAIN)))
        for c in range(N_CHAIN):
            dst_ref[c] = final[c]

    blk = (SCAN_CHUNK, D_RWKV)
    out_f = pl.BlockSpec(blk, lambda b, g: (b * n + fwd_chunk(g), 0))
    out_b = pl.BlockSpec(blk, lambda b, g: (b * n + bwd_chunk(g), 0))
    ck_spec_f = pl.BlockSpec((None, None, N_PAIR, HEAD, 2 * HEAD), lambda b, g: (b, fwd_chunk(g), 0, 0, 0))
    ck_spec_b = pl.BlockSpec((None, None, N_PAIR, HEAD, 2 * HEAD), lambda b, g: (b, bwd_chunk(g), 0, 0, 0))
    ones_spec = pl.BlockSpec((2 * HEAD, 2 * HEAD), lambda b, g: (0, 0))
    (wf, kdf, bf), (wb, kdb, bb) = dirs
    outs = pl.pallas_call(
        body, name="wkv_bwd", grid=(batch, n),
        in_specs=_scan_specs(n, col_blocks, fwd_chunk, bwd_chunk) + [ck_spec_f, ck_spec_b, ones_spec],
        out_specs=[out_f] * 6 + [out_b] * 6,
        out_shape=[_sds((t, D_RWKV))] * 12,
        scratch_shapes=[pltpu.VMEM((N_CHAIN, HEAD, 2 * HEAD), F32),
                        pltpu.VMEM((N_CHAIN, SCAN_CHUNK + 1, HEAD, 2 * HEAD), F32)],
        compiler_params=_cp("parallel", "arbitrary"),
    )(ps, ps, kk, wf, kdf, bf, dy, ps, ps, kk, wb, kdb, bb, dy, ck_f, ck_b, _pair_ones())
    return outs[0:6], outs[6:12]


def _post_math(y, r, kd_f, kd_b, v, gate, gn_w, gn_b, rk_f, rk_b, ones_blocks):
    mean = _seg(y, ones_blocks) * (1.0 / HEAD)
    yc = y - mean
    var = _seg(yc * yc, ones_blocks) * (1.0 / HEAD)
    yn = yc * lax.rsqrt(var + GN_EPS) * gn_w + gn_b
    bonus = _seg(r * kd_f * rk_f, ones_blocks) * v + _seg(r * kd_b * rk_b, ones_blocks) * v
    return (yn + bonus) * gate


def _conv_parts(pc, halo_prev, halo_next, has_prev, has_next):
    gate_b, gate_c, hid = pc[:, 0:512], pc[:, 512:1024], pc[:, 1024:1536]
    u = gate_c * hid
    u_prev_row = halo_prev[GROUP - 1:GROUP, 512:1024] * halo_prev[GROUP - 1:GROUP, 1024:1536] * has_prev
    u_next_row = halo_next[0:1, 512:1024] * halo_next[0:1, 1024:1536] * has_next
    u_down, u_up = _shifted(u, u_prev_row, u_next_row)
    return gate_b, gate_c, hid, u, u_down, u_up


def _post_specs(tm, t):
    pc_prev, pc_next = _halo_specs(tm, D_CONV3, t)
    col = lambda c: pl.BlockSpec((tm, D_RWKV), lambda i: (i, c))
    return ([col(0), col(0), col(0), col(0), col(0), col(2), col(0), _row(tm, D_CONV3), pc_prev, pc_next,
             _fixed((8, D_RWKV)), _fixed((D_RWKV, D_RWKV))])


def _post_fwd(y_f, y_b, ps, kd_f, kd_b, gate, pc, qvec, ones_blocks, tm, seq):
    t = ps.shape[0]

    def body(yf_ref, yb_ref, r_ref, kdf_ref, kdb_ref, v_ref, g_ref, pc_ref, hp_ref, hn_ref, qv_ref, ones_ref,
             o_ref):
        has_prev, has_next = _edge_flags(tm, seq)
        vec = [jnp.broadcast_to(qv_ref[j:j + 1, :], (tm, D_RWKV)) for j in range(7)]
        o_ref[:, 0:D_RWKV] = _post_math(yf_ref[...] + yb_ref[...], r_ref[...], kdf_ref[...], kdb_ref[...],
                                        v_ref[...], g_ref[...], vec[0], vec[1], vec[2], vec[3],
                                        ones_ref[...]).astype(BF16)
        gate_b, _, _, u, u_down, u_up = _conv_parts(pc_ref[...], hp_ref[...], hn_ref[...], has_prev, has_next)
        o_ref[:, D_RWKV:2 * D_RWKV] = (gate_b * (vec[4] * u_down + vec[5] * u + vec[6] * u_up)).astype(BF16)

    return pl.pallas_call(
        body, name="post_fwd", grid=(t // tm,), in_specs=_post_specs(tm, t), out_specs=_row(tm, D_MODEL),
        out_shape=_sds((t, D_MODEL), BF16), compiler_params=_cp("parallel"),
    )(y_f, y_b, ps, kd_f, kd_b, ps, gate, pc, pc, pc, qvec, ones_blocks)


def _post_bwd(d_out, y_f, y_b, ps, kd_f, kd_b, gate, pc, qvec, ones_blocks, tm, seq):
    t = ps.shape[0]
    do_prev, do_next = _halo_specs(tm, D_MODEL, t)

    def body(do_ref, dop_ref, don_ref, yf_ref, yb_ref, r_ref, kdf_ref, kdb_ref, v_ref, g_ref, pc_ref, hp_ref,
             hn_ref, qv_ref, ones_ref, dy_ref, dr_ref, dkdf_ref, dkdb_ref, dv_ref, dg_ref, dpc_ref, dqv_ref):
        has_prev, has_next = _edge_flags(tm, seq)
        vec = [jnp.broadcast_to(qv_ref[j:j + 1, :], (tm, D_RWKV)) for j in range(7)]
        ones_v = ones_ref[...]
        args = (yf_ref[...] + yb_ref[...], r_ref[...], kdf_ref[...], kdb_ref[...], v_ref[...], g_ref[...],
                vec[0], vec[1], vec[2], vec[3])
        _, vjp = jax.vjp(lambda *a: _post_math(*a, ones_v), *args)
        grads = vjp(do_ref[:, 0:D_RWKV])
        for o_ref, gval in zip((dy_ref, dr_ref, dkdf_ref, dkdb_ref, dv_ref, dg_ref), grads[0:6]):
            o_ref[...] = gval

        hp, hn = hp_ref[...], hn_ref[...]
        gate_b, gate_c, hid, u, u_down, u_up = _conv_parts(pc_ref[...], hp, hn, has_prev, has_next)
        d_oc = do_ref[:, D_RWKV:2 * D_RWKV]
        d_cu = d_oc * gate_b
        d_cu_prev = dop_ref[GROUP - 1:GROUP, D_RWKV:2 * D_RWKV] * hp[GROUP - 1:GROUP, 0:512] * has_prev
        d_cu_next = don_ref[0:1, D_RWKV:2 * D_RWKV] * hn[0:1, 0:512] * has_next
        d_cu_down, d_cu_up = _shifted(d_cu, d_cu_prev, d_cu_next)
        d_u = vec[5] * d_cu + vec[4] * d_cu_up + vec[6] * d_cu_down
        dpc_ref[:, 0:512] = (d_oc * (vec[4] * u_down + vec[5] * u + vec[6] * u_up)).astype(BF16)
        dpc_ref[:, 512:1024] = (d_u * hid).astype(BF16)
        dpc_ref[:, 1024:1536] = (d_u * gate_c).astype(BF16)

        @pl.when(pl.program_id(0) == 0)
        def _():
            dqv_ref[...] = jnp.zeros_like(dqv_ref)

        vec_grads = list(grads[6:10]) + [d_cu * u_down, d_cu * u, d_cu * u_up]
        for j, gval in enumerate(vec_grads):
            dqv_ref[j:j + 1, :] += jnp.sum(gval, axis=0, keepdims=True)

    return pl.pallas_call(
        body, name="post_bwd", grid=(t // tm,),
        in_specs=[_row(tm, D_MODEL), do_prev, do_next] + _post_specs(tm, t),
        out_specs=[_row(tm, D_RWKV)] * 6 + [_row(tm, D_CONV3), _fixed((8, D_RWKV))],
        out_shape=[_sds((t, D_RWKV))] * 6 + [_sds((t, D_CONV3), BF16), _sds((8, D_RWKV))],
        compiler_params=_cp("arbitrary"),
    )(d_out, d_out, d_out, y_f, y_b, ps, kd_f, kd_b, ps, gate, pc, pc, pc, qvec, ones_blocks)


def _adamw(w, g, m, v, name):
    r, c = w.shape
    tr = _tile(r, 256, 8)
    spec = pl.BlockSpec((tr, c), lambda i: (i, 0))

    def body(w_ref, g_ref, m_ref, v_ref, d_ref, nm_ref, nv_ref):
        gv = g_ref[...]
        m2 = ADAM_B1 * m_ref[...] + (1.0 - ADAM_B1) * gv
        v2 = ADAM_B2 * v_ref[...] + (1.0 - ADAM_B2) * (gv * gv)
        m_hat = m2 / (1.0 - ADAM_B1 ** ADAM_STEP)
        v_hat = v2 / (1.0 - ADAM_B2 ** ADAM_STEP)
        d_ref[...] = -ADAM_LR * (m_hat / (jnp.sqrt(v_hat) + ADAM_EPS) + ADAM_WD * w_ref[...])
        nm_ref[...] = m2
        nv_ref[...] = v2

    return pl.pallas_call(
        body, name=name, grid=(r // tr,), in_specs=[spec] * 4, out_specs=[spec] * 3,
        out_shape=[_sds((r, c))] * 3, compiler_params=_cp("parallel"))(w, g, m, v)


def _add_n(parts, name):
    r, c = parts[0].shape
    tr = _tile(r, 408, 8)
    spec = pl.BlockSpec((tr, c), lambda i: (i, 0))

    def body(*refs):
        acc = refs[0][...]
        for ref in refs[1:-1]:
            acc = acc + ref[...]
        refs[-1][...] = acc

    return pl.pallas_call(
        body, name=name, grid=(r // tr,), in_specs=[spec] * len(parts), out_specs=spec,
        out_shape=_sds((r, c)), compiler_params=_cp("parallel"))(*parts)


_ANY = pl.BlockSpec(memory_space=pl.ANY)


def _place():
    return lax.axis_index("x"), lax.axis_index("y"), lax.axis_index("c")


def _other_chips(x, y):
    return [(1 - x, y), (x, 1 - y), (1 - x, 1 - y)]


def _remote(src, dst, send_sems, recv_sems, k, to):
    return pltpu.make_async_remote_copy(src_ref=src, dst_ref=dst, send_sem=send_sems.at[k],
                                        recv_sem=recv_sems.at[k], device_id=to, device_id_type=MESH)


def _gather_weights(pack):
    rows, width = pack.shape
    half = rows // 2

    def body(x_ref, out_ref, send_sems, recv_sems):
        x, y, c = _place()
        sibling = (x, y, 1 - c)
        chips = _other_chips(x, y)

        def block(chip, part):
            return out_ref.at[2 * chip[0] + chip[1], pl.ds(part * half, half), :]

        first = [_remote(x_ref.at[pl.ds(c * half, half), :], block((x, y), c), send_sems, recv_sems, j, (*chip, c))
                 for j, chip in enumerate(chips)]
        for cp in first:
            cp.start()
        passed = [_remote(block(chip, c), block(chip, c), send_sems, recv_sems, 3 + j, sibling)
                  for j, chip in enumerate(chips)]
        for j, chip in enumerate(chips):
            _remote(block(chip, c), block(chip, c), send_sems, recv_sems, j, sibling).wait_recv()
            passed[j].start()
        for j, chip in enumerate(chips):
            _remote(block(chip, 1 - c), block(chip, 1 - c), send_sems, recv_sems, 3 + j, sibling).wait_recv()
        for cp in first + passed:
            cp.wait_send()

    return pl.pallas_call(
        body, name="gather_weights", in_specs=[_ANY], out_specs=_ANY,
        out_shape=_sds((N_SHARD, rows, width), pack.dtype),
        scratch_shapes=[pltpu.SemaphoreType.DMA((6,)), pltpu.SemaphoreType.DMA((6,))],
    )(pack)


def _swap_with_sibling(block, name):
    def body(x_ref, out_ref, send_sems, recv_sems):
        x, y, c = _place()
        cp = _remote(x_ref, out_ref, send_sems, recv_sems, 0, (x, y, 1 - c))
        cp.start()
        cp.wait()

    return pl.pallas_call(
        body, name=name, in_specs=[_ANY], out_specs=_ANY, out_shape=_sds(block.shape, block.dtype),
        scratch_shapes=[pltpu.SemaphoreType.DMA((1,)), pltpu.SemaphoreType.DMA((1,))],
    )(block)


def _exchange_quarters(parts):
    _, rows, width = parts.shape

    def body(x_ref, out_ref, send_sems, recv_sems):
        x, y, c = _place()
        copies = [_remote(x_ref.at[2 * chip[0] + chip[1]], out_ref.at[j], send_sems, recv_sems, j, (*chip, c))
                  for j, chip in enumerate(_other_chips(x, y))]
        for cp in copies:
            cp.start()
        for cp in copies:
            cp.wait()

    return pl.pallas_call(
        body, name="exchange_quarters", in_specs=[_ANY], out_specs=_ANY,
        out_shape=_sds((3, rows, width), parts.dtype),
        scratch_shapes=[pltpu.SemaphoreType.DMA((3,)), pltpu.SemaphoreType.DMA((3,))],
    )(parts)


def _allreduce_small(vec):
    rows, width = vec.shape
    vmem = pl.BlockSpec(memory_space=pltpu.VMEM)

    def body(x_ref, o_ref, buf_ref, send_sems, recv_sems):
        x, y, c = _place()
        me = 4 * x + 2 * y + c
        buf_ref[me] = x_ref[...]
        copies = []
        for k in range(1, N_DEV):
            peer = (x ^ ((k >> 2) & 1), y ^ ((k >> 1) & 1), c ^ (k & 1))
            copies.append(_remote(x_ref, buf_ref.at[me], send_sems, recv_sems, k - 1, peer))
        for cp in copies:
            cp.start()
        for k in range(1, N_DEV):
            _remote(x_ref, buf_ref.at[me ^ k], send_sems, recv_sems, k - 1, (x, y, c)).wait_recv()
        for cp in copies:
            cp.wait_send()
        total = buf_ref[0]
        for d in range(1, N_DEV):
            total = total + buf_ref[d]
        o_ref[...] = total

    return pl.pallas_call(
        body, name="allreduce_small", in_specs=[vmem], out_specs=vmem, out_shape=_sds((rows, width)),
        scratch_shapes=[pltpu.VMEM((N_DEV, rows, width), F32), pltpu.SemaphoreType.DMA((N_DEV - 1,)),
                        pltpu.SemaphoreType.DMA((N_DEV - 1,))],
    )(vec)


def _rows1024(a):
    return a.reshape(-1, 1024)


def _pad_rows(a, rows):
    return jnp.concatenate([a, jnp.zeros((rows - a.shape[0], a.shape[1]), a.dtype)], axis=0)


def _pack_weight_shards(w):
    conv_bits = lax.bitcast_convert_type(w["conv_w"], BF16).reshape(1, -1)
    conv_row = jnp.concatenate([conv_bits, jnp.zeros((1, 1024 - conv_bits.shape[1]), BF16)], axis=1)
    parts = [_rows1024(w[name].astype(BF16)) for name, _ in _PACK_ROWS[:-1]] + [conv_row]
    return _pad_rows(jnp.concatenate(parts, axis=0), PACK_R)


def _unpack_gathered(gathered):
    out, row = {}, 0
    for name, n in _PACK_ROWS:
        out[name] = gathered[:, row:row + n]
        row += n
    cols = lambda a, k: jnp.concatenate([a[s].reshape(k, -1) for s in range(N_SHARD)], axis=1)
    conv = lax.bitcast_convert_type(out["conv_w"][:, 0, :768].reshape(N_SHARD, 3, 128, 2), F32)
    return dict(
        w_in=cols(out["w_in"], D_MODEL), w_out=out["w_out"].reshape(D_MODEL, D_MODEL),
        w_gate=cols(out["w_gate"], D_MODEL), w_up=cols(out["w_up"], D_MODEL),
        w_down=out["w_down"].reshape(D_FF, D_MODEL),
        w_up_f=cols(out["w_up_f"], D_LORA), w_up_b=cols(out["w_up_b"], D_LORA),
        a_up_f=cols(out["a_up_f"], D_LORA), a_up_b=cols(out["a_up_b"], D_LORA),
        g_up=cols(out["g_up"], D_GATE), conv_w=jnp.concatenate([conv[s] for s in range(N_SHARD)], axis=1))


def _pack_grads(g):
    col_split = lambda a, s: a[:, s * (a.shape[1] // N_SHARD):(s + 1) * (a.shape[1] // N_SHARD)]
    row_split = lambda a, s: a[s * (a.shape[0] // N_SHARD):(s + 1) * (a.shape[0] // N_SHARD)]
    slots = []
    for s in range(N_SHARD):
        conv = col_split(g["conv_w"], s).reshape(1, -1)
        parts = [_rows1024(col_split(g["w_in"], s)), row_split(g["w_out"], s), _rows1024(col_split(g["w_gate"], s)),
                 _rows1024(col_split(g["w_up"], s)), row_split(g["w_down"], s)]
        parts += [_rows1024(col_split(g[name], s)) for name in ("w_up_f", "w_up_b", "a_up_f", "a_up_b", "g_up")]
        parts.append(jnp.concatenate([conv, jnp.zeros((1, 1024 - conv.shape[1]), F32)], axis=1))
        slots.append(_pad_rows(jnp.concatenate(parts, axis=0), PACK_R))
    return jnp.stack(slots)


def _unpack_grad_shard(pack):
    shapes = dict(w_in=(D_MODEL, 840), w_out=(256, D_MODEL), w_gate=(D_MODEL, 704), w_up=(D_MODEL, 704),
                  w_down=(704, D_MODEL), w_up_f=(D_LORA, 128), w_up_b=(D_LORA, 128), a_up_f=(D_LORA, 128),
                  a_up_b=(D_LORA, 128), g_up=(D_GATE, 128))
    out, row = {}, 0
    for name, n in _PACK_ROWS[:-1]:
        out[name] = pack[row:row + n].reshape(shapes[name])
        row += n
    out["conv_w"] = pack[row, :384].reshape(3, 128)
    return out


_SMALL_LAYOUT = (("norm1_w", 1024), ("mu_shift", D_SHIFT), ("w0_f", 512), ("w0_b", 512), ("a0_f", 512),
                 ("a0_b", 512), ("k_k", 512), ("k_a_f", 512), ("k_a_b", 512), ("r_k_f", 512), ("r_k_b", 512),
                 ("gn_w", 512), ("gn_b", 512), ("norm2_w", 1024), ("norm_f_w", 1024), ("loss", 1))


def _pack_small(vals):
    rows = []
    for name, n in _SMALL_LAYOUT:
        flat = vals[name].reshape(-1)
        n_rows = -(-n // 1024)
        rows.append(jnp.concatenate([flat, jnp.zeros((n_rows * 1024 - n,), F32)]).reshape(n_rows, 1024))
    return _pad_rows(jnp.concatenate(rows, axis=0), SMALL_ROWS)


def _unpack_small(pack):
    out, row = {}, 0
    for name, n in _SMALL_LAYOUT:
        n_rows = -(-n // 1024)
        out[name] = pack[row:row + n_rows].reshape(-1)[:n]
        row += n_rows
    return out


_WEIGHTS = ("norm1_w", "w_in", "mu_shift", "w_up_f", "w0_f", "w_up_b", "w0_b", "a_up_f", "a0_f", "a_up_b", "a0_b",
            "g_up", "k_k", "k_a_f", "k_a_b", "r_k_f", "r_k_b", "gn_w", "gn_b", "conv_w", "w_out", "norm2_w",
            "w_gate", "w_up", "w_down", "norm_f_w")


def _train_step(x, loss_target, w, m, v):
    batch, seq, _ = x.shape
    t = batch * seq
    tm = _tile(seq, 256, 8)
    xs = x.reshape(t, D_MODEL)
    target = loss_target.reshape(t, D_MODEL)
    vec = lambda name: w[name].reshape(1, -1)

    local = {name: w[name][0] for name, _ in _PACK_ROWS}
    c = lax.axis_index("c")
    chip = 2 * lax.axis_index("x") + lax.axis_index("y")
    pack = _pack_weight_shards(local)
    full = _unpack_gathered(lax.dynamic_update_slice(_gather_weights(pack), pack[None], (chip, 0, 0)))
    w_in = full["w_in"]
    w_shift = jnp.concatenate([w_in[:, :D_SHIFT], jnp.zeros((D_MODEL, D_SHIFT_PAD - D_SHIFT), BF16)], axis=1)
    w_conv = w_in[:, D_SHIFT:]
    zeros_lora = jnp.zeros((D_LORA, D_RWKV), F32)
    lora = lambda name: full[name].astype(F32)
    mats = (jnp.concatenate([lora("w_up_f"), zeros_lora]), jnp.concatenate([zeros_lora, lora("a_up_f")]),
            jnp.concatenate([lora("w_up_b"), zeros_lora]), jnp.concatenate([zeros_lora, lora("a_up_b")]),
            jnp.concatenate([lora("g_up"), jnp.zeros((D_GATE_PAD - D_GATE, D_RWKV), F32)]))
    mu = jnp.concatenate([vec("mu_shift"), jnp.zeros((1, D_SHIFT_PAD - D_SHIFT), F32)], axis=1)
    mu = jnp.broadcast_to(mu, (GROUP, D_SHIFT_PAD))
    zero_row = jnp.zeros((1, D_RWKV), F32)
    pvec = jnp.concatenate([vec("k_k"), vec("w0_f"), vec("a0_f"), vec("k_a_f"), vec("w0_b"), vec("a0_b"),
                            vec("k_a_b"), zero_row], axis=0)
    qvec = jnp.concatenate([vec("gn_w"), vec("gn_b"), vec("r_k_f"), vec("r_k_b"), full["conv_w"], zero_row], axis=0)
    ones_blocks = _head_ones()

    h1 = _rmsnorm_fwd(xs, vec("norm1_w"), tm, "norm1_fwd")
    p_shift = _matmul(h1, w_shift, mode="nn", name="in_proj_shift")
    pc = _matmul(h1, w_conv, mode="nn", name="in_proj_conv")
    ps = _shift_fwd(p_shift, mu, tm, seq)
    kk, w_f, kd_f, b_f, w_b, kd_b, b_b, gate = _prep_fwd(ps, pvec, mats, ones_blocks, tm)
    dirs = ((w_f, kd_f, b_f), (w_b, kd_b, b_b))
    y_f, y_b, ck_f, ck_b = _scan_fwd(ps, kk, dirs, batch, seq)
    mixed = _post_fwd(y_f, y_b, ps, kd_f, kd_b, gate, pc, qvec, ones_blocks, tm, seq)
    x1 = _matmul(mixed, full["w_out"], mode="nn", name="out_proj", add=xs)
    h2 = _rmsnorm_fwd(x1, vec("norm2_w"), tm, "norm2_fwd")
    ff_gate = _matmul(h2, full["w_gate"], mode="nn", name="ffn_gate")
    ff_up = _matmul(h2, full["w_up"], mode="nn", name="ffn_up")
    act = _swiglu_fwd(ff_gate, ff_up, tm)
    x2 = _matmul(act, full["w_down"], mode="nn", name="ffn_down", add=x1)
    d_x2, d_norm_f, loss_part = _loss_head(x2, w["norm_f_w"].reshape(1, -1), target, tm)

    g = {}
    d_act = _matmul(d_x2, full["w_down"], mode="nt", name="ffn_down_dx")
    g["w_down"] = _matmul(d_x2, act, mode="tn", name="ffn_down_dw").T
    d_gate, d_up = _swiglu_bwd(ff_gate, ff_up, d_act, tm)
    d_h2 = _matmul(d_gate, full["w_gate"], mode="nt", name="ffn_gate_dx")
    d_h2 = _matmul(d_up, full["w_up"], mode="nt", name="ffn_up_dx", add=d_h2)
    g["w_gate"] = _matmul(h2, d_gate, mode="tn", name="ffn_gate_dw")
    g["w_up"] = _matmul(h2, d_up, mode="tn", name="ffn_up_dw")
    d_x1, d_norm2 = _rmsnorm_bwd(x1, vec("norm2_w"), d_h2, d_x2, tm, "norm2_bwd")
    d_mixed = _matmul(d_x1, full["w_out"], mode="nt", name="out_proj_dx")
    g["w_out"] = _matmul(mixed, d_x1, mode="tn", name="out_proj_dw")
    dy, dr_o, dkdf_o, dkdb_o, dv_o, d_gatev, d_pc, d_qvec = _post_bwd(
        d_mixed, y_f, y_b, ps, kd_f, kd_b, gate, pc, qvec, ones_blocks, tm, seq)
    (dr_f, dw_f, dkd_f, dv_f, dkk_f, db_f), (dr_b, dw_b, dkd_b, dv_b, dkk_b, db_b) = _scan_bwd(
        ps, kk, dirs, dy, ck_f, ck_b, batch, seq)
    cts = [[dr_f, dr_b, dr_o], [dv_f, dv_b, dv_o], [dkk_f, dkk_b], [dw_f], [dkd_f, dkdf_o], [db_f],
           [dw_b], [dkd_b, dkdb_o], [db_b], [d_gatev]]
    q, d_pvec, d_m0, d_m1, d_m2, d_m3, d_m4 = _prep_bwd(ps, pvec, mats, ones_blocks, cts, tm)
    d_pshift, d_mu = _shift_bwd(q, p_shift, mu, tm, seq)
    d_h1 = _matmul(d_pshift, w_shift, mode="nt", name="in_proj_shift_dx")
    d_h1 = _matmul(d_pc, w_conv, mode="nt", name="in_proj_conv_dx", add=d_h1)
    d_w_shift = _matmul(h1, d_pshift, mode="tn", name="in_proj_shift_dw")
    d_w_conv = _matmul(h1, d_pc, mode="tn", name="in_proj_conv_dw")
    g["w_in"] = jnp.concatenate([d_w_shift[:, :D_SHIFT], d_w_conv], axis=1)
    d_x, d_norm1 = _rmsnorm_bwd(xs, vec("norm1_w"), d_h1, d_x1, tm, "norm1_bwd")
    g["w_up_f"], g["a_up_f"] = d_m0[:D_LORA], d_m1[D_LORA:]
    g["w_up_b"], g["a_up_b"] = d_m2[:D_LORA], d_m3[D_LORA:]
    g["g_up"] = d_m4[:D_GATE]
    g["conv_w"] = d_qvec[4:7]

    packed = _pack_grads(g)
    keep = lax.dynamic_slice_in_dim(packed, c * PACK_H, PACK_H, axis=1)
    give = lax.dynamic_slice_in_dim(packed, (1 - c) * PACK_H, PACK_H, axis=1)
    got = _swap_with_sibling(give, "swap_halves")
    chip_sum = _add_n([keep.reshape(-1, 1024), got.reshape(-1, 1024)], "add_halves").reshape(N_SHARD, PACK_H, 1024)
    others = _exchange_quarters(chip_sum)
    own = lax.dynamic_index_in_dim(chip_sum, chip, axis=0, keepdims=False)
    eighth = _add_n([own, others[0], others[1], others[2]], "add_quarters")
    other_eighth = _swap_with_sibling(eighth, "swap_eighths")
    grads = _unpack_grad_shard(jnp.concatenate([jnp.where(c == 0, eighth, other_eighth),
                                                jnp.where(c == 0, other_eighth, eighth)], axis=0))

    small = dict(norm1_w=d_norm1, mu_shift=d_mu[:, :D_SHIFT], w0_f=d_pvec[1], w0_b=d_pvec[4], a0_f=d_pvec[2],
                 a0_b=d_pvec[5], k_k=d_pvec[0], k_a_f=d_pvec[3], k_a_b=d_pvec[6], r_k_f=d_qvec[2], r_k_b=d_qvec[3],
                 gn_w=d_qvec[0], gn_b=d_qvec[1], norm2_w=d_norm2, norm_f_w=d_norm_f, loss=loss_part)
    reduced = _unpack_small(_allreduce_small(_pack_small(small)))
    loss = reduced.pop("loss")[0]
    grads.update(reduced)

    outs = {}
    for name in _WEIGHTS:
        shape = w[name].shape
        as2d = (1, shape[0]) if len(shape) == 1 else (-1, shape[-1]) if name not in ("r_k_f", "r_k_b") else (1, -1)
        grad = grads[name].reshape(shape)
        delta, new_m, new_v = _adamw(w[name].reshape(as2d), grad.reshape(as2d), m[name].reshape(as2d),
                                     v[name].reshape(as2d), "adamw_" + name)
        outs[name] = (grad, delta.reshape(shape), new_m.reshape(shape), new_v.reshape(shape))
    d_x = d_x.reshape(batch, seq, D_MODEL)
    return (loss, d_x) + tuple(outs[name][k] for k in range(4) for name in _WEIGHTS)


def kernel(x, norm1_w, w_in, mu_shift, w_up_f, w0_f, w_up_b, w0_b, a_up_f, a0_f, a_up_b, a0_b, g_up, k_k, k_a_f, k_a_b, r_k_f, r_k_b, gn_w, gn_b, conv_w, w_out, norm2_w, w_gate, w_up, w_down, norm_f_w, loss_target, m_norm1_w, m_w_in, m_mu_shift, m_w_up_f, m_w0_f, m_w_up_b, m_w0_b, m_a_up_f, m_a0_f, m_a_up_b, m_a0_b, m_g_up, m_k_k, m_k_a_f, m_k_a_b, m_r_k_f, m_r_k_b, m_gn_w, m_gn_b, m_conv_w, m_w_out, m_norm2_w, m_w_gate, m_w_up, m_w_down, m_norm_f_w, v_norm1_w, v_w_in, v_mu_shift, v_w_up_f, v_w0_f, v_w_up_b, v_w0_b, v_a_up_f, v_a0_f, v_a_up_b, v_a0_b, v_g_up, v_k_k, v_k_a_f, v_k_a_b, v_r_k_f, v_r_k_b, v_gn_w, v_gn_b, v_conv_w, v_w_out, v_norm2_w, v_w_gate, v_w_up, v_w_down, v_norm_f_w):
    args = locals()
    w = {name: args[name] for name in _WEIGHTS}
    m = {name: args["m_" + name] for name in _WEIGHTS}
    v = {name: args["v_" + name] for name in _WEIGHTS}
    return _train_step(x, loss_target, w, m, v)
```

```python
import functools

import jax
import jax.numpy as jnp
from jax import lax
from jax.experimental import pallas as pl
from jax.experimental.pallas import tpu as pltpu

F32 = jnp.float32
BF16 = jnp.bfloat16
MESH = pl.DeviceIdType.MESH

D_MODEL = 1024
D_RWKV = 512
HEAD = 64
N_PAIR = D_RWKV // (2 * HEAD)
D_LORA = 64
D_GATE = 160
D_GATE_PAD = 384
D_FF = 2816
D_SHIFT = 1824
D_SHIFT_PAD = 2048
D_CONV3 = 1536
LOG_DECAY_SCALE = 0.606531
RMS_EPS = 1e-6
GN_EPS = 64e-5
NORM_EPS = 1e-12
ADAM_LR, ADAM_B1, ADAM_B2, ADAM_EPS, ADAM_WD, ADAM_STEP = 0.001, 0.9, 0.999, 1e-08, 0.01, 10

N_SHARD = 4
N_DEV = 8
V7X_VMEM_LIMIT = 48 * 1024 * 1024
SCAN_CHUNK = 64
GROUP = 8

_PACK_ROWS = (("w_in", 840), ("w_out", 256), ("w_gate", 704), ("w_up", 704), ("w_down", 704),
              ("w_up_f", 8), ("w_up_b", 8), ("a_up_f", 8), ("a_up_b", 8), ("g_up", 20), ("conv_w", 1))
PACK_R = 3264
PACK_H = PACK_R // 2
SMALL_ROWS = 24


def _tile(n, cap, mult=128):
    best = None
    t = mult
    while t <= min(n, cap):
        if n % t == 0:
            best = t
        t += mult
    return best or n


def _cp(*sem):
    return pltpu.CompilerParams(dimension_semantics=sem or None, vmem_limit_bytes=V7X_VMEM_LIMIT)


def _sds(shape, dtype=F32):
    return jax.ShapeDtypeStruct(shape, dtype)


def _matmul(a, b, *, mode, name, out_dtype=F32, add=None):
    if mode == "tn":
        r, m = a.shape
        n = b.shape[1]
        tm, tn, tk = _tile(m, 512), _tile(n, 1536), _tile(r, 512, 8)
        nk = r // tk
        a_spec = pl.BlockSpec((tk, tm), lambda i, j, k: (k, i))
        b_spec = pl.BlockSpec((tk, tn), lambda i, j, k: (k, j))
        dims = (((0,), (0,)), ((), ()))
    else:
        m, kdim = a.shape
        n = b.shape[1] if mode == "nn" else b.shape[0]
        tm, tn, nk = _tile(m, 512, 8), _tile(n, 1536), 1
        a_spec = pl.BlockSpec((tm, kdim), lambda i, j, k: (i, 0))
        if mode == "nn":
            b_spec = pl.BlockSpec((kdim, tn), lambda i, j, k: (0, j))
            dims = (((1,), (0,)), ((), ()))
        else:
            b_spec = pl.BlockSpec((tn, kdim), lambda i, j, k: (j, 0))
            dims = (((1,), (1,)), ((), ()))
    has_add = add is not None

    def body(*refs):
        a_ref, b_ref = refs[0], refs[1]
        add_ref = refs[2] if has_add else None
        o_ref = refs[3] if has_add else refs[2]
        part = lax.dot_general(a_ref[...].astype(BF16), b_ref[...].astype(BF16), dims,
                               preferred_element_type=F32)
        if nk == 1:
            if has_add:
                part = part + add_ref[...]
            o_ref[...] = part.astype(out_dtype)
        else:
            acc_ref = refs[-1]
            k = pl.program_id(2)

            @pl.when(k == 0)
            def _():
                acc_ref[...] = jnp.zeros_like(acc_ref)

            acc_ref[...] += part

            @pl.when(k == nk - 1)
            def _():
                res = acc_ref[...]
                if has_add:
                    res = res + add_ref[...]
                o_ref[...] = res.astype(out_dtype)

    o_spec = pl.BlockSpec((tm, tn), lambda i, j, k: (i, j))
    in_specs = [a_spec, b_spec] + ([o_spec] if has_add else [])
    args = (a, b) + ((add,) if has_add else ())
    return pl.pallas_call(
        body, name=name, grid=(m // tm, n // tn, nk), in_specs=in_specs, out_specs=o_spec,
        out_shape=_sds((m, n), out_dtype),
        scratch_shapes=[pltpu.VMEM((tm, tn), F32)] if nk > 1 else [],
        compiler_params=_cp("parallel", "parallel", "arbitrary"),
    )(*args)


def _row(tm, width):
    return pl.BlockSpec((tm, width), lambda i: (i, 0))


def _fixed(shape):
    return pl.BlockSpec(shape, lambda i: tuple(0 for _ in shape))


def _rmsnorm_fwd(x, w, tm, name):
    t, d = x.shape

    def body(x_ref, w_ref, o_ref):
        xv = x_ref[...]
        rstd = lax.rsqrt(jnp.mean(xv * xv, axis=-1, keepdims=True) + RMS_EPS)
        o_ref[...] = (xv * rstd * w_ref[...]).astype(BF16)

    return pl.pallas_call(
        body, name=name, grid=(t // tm,), in_specs=[_row(tm, d), _fixed((1, d))], out_specs=_row(tm, d),
        out_shape=_sds((t, d), BF16), compiler_params=_cp("parallel"))(x, w)


def _rms_bwd_math(xv, wv, dyv):
    rstd = lax.rsqrt(jnp.mean(xv * xv, axis=-1, keepdims=True) + RMS_EPS)
    xhat = xv * rstd
    gv = dyv * wv
    dx = rstd * (gv - xhat * jnp.mean(gv * xhat, axis=-1, keepdims=True))
    return dx, jnp.sum(dyv * xhat, axis=0, keepdims=True)


def _rmsnorm_bwd(x, w, dy, dres, tm, name):
    t, d = x.shape

    def body(x_ref, w_ref, dy_ref, dres_ref, dx_ref, dw_ref):
        dx, dw = _rms_bwd_math(x_ref[...], w_ref[...], dy_ref[...])
        dx_ref[...] = dres_ref[...] + dx

        @pl.when(pl.program_id(0) == 0)
        def _():
            dw_ref[...] = jnp.zeros_like(dw_ref)

        dw_ref[...] += dw

    return pl.pallas_call(
        body, name=name, grid=(t // tm,),
        in_specs=[_row(tm, d), _fixed((1, d)), _row(tm, d), _row(tm, d)],
        out_specs=[_row(tm, d), _fixed((1, d))],
        out_shape=[_sds((t, d)), _sds((1, d))], compiler_params=_cp("arbitrary"))(x, w, dy, dres)


def _loss_head(x, w, target, tm):
    t, d = x.shape

    def body(x_ref, w_ref, t_ref, dx_ref, dw_ref, loss_ref):
        xv, wv = x_ref[...], w_ref[...]
        rstd = lax.rsqrt(jnp.mean(xv * xv, axis=-1, keepdims=True) + RMS_EPS)
        err = xv * rstd * wv - t_ref[...]
        dx, dw = _rms_bwd_math(xv, wv, err * (1.0 / d))
        dx_ref[...] = dx

        @pl.when(pl.program_id(0) == 0)
        def _():
            dw_ref[...] = jnp.zeros_like(dw_ref)
            loss_ref[...] = jnp.zeros_like(loss_ref)

        dw_ref[...] += dw
        loss_ref[...] += 0.5 * jnp.sum(jnp.mean(err * err, axis=-1, keepdims=True), axis=0, keepdims=True)

    return pl.pallas_call(
        body, name="loss_head", grid=(t // tm,),
        in_specs=[_row(tm, d), _fixed((1, d)), _row(tm, d)],
        out_specs=[_row(tm, d), _fixed((1, d)), _fixed((1, 1))],
        out_shape=[_sds((t, d)), _sds((1, d)), _sds((1, 1))], compiler_params=_cp("arbitrary"))(x, w, target)


def _swiglu_fwd(gate, up, tm):
    t, f = gate.shape

    def body(g_ref, u_ref, o_ref):
        gv = g_ref[...]
        o_ref[...] = (gv * jax.nn.sigmoid(gv) * u_ref[...]).astype(BF16)

    return pl.pallas_call(
        body, name="swiglu_fwd", grid=(t // tm,), in_specs=[_row(tm, f), _row(tm, f)], out_specs=_row(tm, f),
        out_shape=_sds((t, f), BF16), compiler_params=_cp("parallel"))(gate, up)


def _swiglu_bwd(gate, up, dact, tm):
    t, f = gate.shape

    def body(g_ref, u_ref, d_ref, dg_ref, du_ref):
        gv, uv, dv = g_ref[...], u_ref[...], d_ref[...]
        sg = jax.nn.sigmoid(gv)
        du_ref[...] = (dv * gv * sg).astype(BF16)
        dg_ref[...] = (dv * uv * (sg * (1.0 + gv * (1.0 - sg)))).astype(BF16)

    return pl.pallas_call(
        body, name="swiglu_bwd", grid=(t // tm,), in_specs=[_row(tm, f)] * 3, out_specs=[_row(tm, f)] * 2,
        out_shape=[_sds((t, f), BF16)] * 2, compiler_params=_cp("parallel"))(gate, up, dact)


def _halo_specs(tm, width, rows_total):
    per = tm // GROUP
    last = rows_total // GROUP - 1
    prev = pl.BlockSpec((GROUP, width), lambda i: (jnp.maximum(i * per - 1, 0), 0))
    nxt = pl.BlockSpec((GROUP, width), lambda i: (jnp.minimum((i + 1) * per, last), 0))
    return prev, nxt


def _edge_flags(tm, seq):
    i = pl.program_id(0)
    has_prev = jnp.where((i * tm) % seq == 0, 0.0, 1.0).astype(F32)
    has_next = jnp.where(((i + 1) * tm) % seq == 0, 0.0, 1.0).astype(F32)
    return has_prev, has_next


def _shifted(xv, prev_row, next_row):
    tm = xv.shape[0]
    row = lax.broadcasted_iota(jnp.int32, xv.shape, 0)
    down = jnp.where(row == 0, prev_row, pltpu.roll(xv, 1, axis=0))
    up = jnp.where(row == tm - 1, next_row, pltpu.roll(xv, tm - 1, axis=0))
    return down, up


def _shift_fwd(p, mu, tm, seq):
    t, w = p.shape
    prev_spec, next_spec = _halo_specs(tm, w, t)

    def body(p_ref, hp_ref, hn_ref, mu_ref, o_ref):
        has_prev, has_next = _edge_flags(tm, seq)
        xv = p_ref[...]
        down, up = _shifted(xv, hp_ref[GROUP - 1:GROUP, :] * has_prev, hn_ref[0:1, :] * has_next)
        o_ref[...] = xv + mu_ref[0:1, :] * (0.5 * (down + up) - xv)

    return pl.pallas_call(
        body, name="shift_fwd", grid=(t // tm,),
        in_specs=[_row(tm, w), prev_spec, next_spec, _fixed((GROUP, w))], out_specs=_row(tm, w),
        out_shape=_sds((t, w)), compiler_params=_cp("parallel"))(p, p, p, mu)


def _shift_bwd(q, p, mu, tm, seq):
    t, w = p.shape
    prev_spec, next_spec = _halo_specs(tm, w, t)

    def body(q_ref, qp_ref, qn_ref, p_ref, pp_ref, pn_ref, mu_ref, dp_ref, dmu_ref):
        has_prev, has_next = _edge_flags(tm, seq)
        muv = mu_ref[0:1, :]
        qv = q_ref[...]
        mq = muv * qv
        mq_down, mq_up = _shifted(mq, muv * qp_ref[GROUP - 1:GROUP, :] * has_prev,
                                  muv * qn_ref[0:1, :] * has_next)
        dp_ref[...] = (qv - mq + 0.5 * (mq_down + mq_up)).astype(BF16)
        pv = p_ref[...]
        p_down, p_up = _shifted(pv, pp_ref[GROUP - 1:GROUP, :] * has_prev, pn_ref[0:1, :] * has_next)

        @pl.when(pl.program_id(0) == 0)
        def _():
            dmu_ref[...] = jnp.zeros_like(dmu_ref)

        dmu_ref[...] += jnp.sum(qv * (0.5 * (p_down + p_up) - pv), axis=0, keepdims=True)

    return pl.pallas_call(
        body, name="shift_bwd", grid=(t // tm,),
        in_specs=[_row(tm, w), prev_spec, next_spec, _row(tm, w), prev_spec, next_spec, _fixed((GROUP, w))],
        out_specs=[_row(tm, w), _fixed((1, w))],
        out_shape=[_sds((t, w), BF16), _sds((1, w))], compiler_params=_cp("arbitrary"))(q, q, q, p, p, p, mu)


@jax.custom_vjp
def _bdot(a, b):
    return jnp.dot(a.astype(BF16), b.astype(BF16), preferred_element_type=F32)


def _bdot_fwd(a, b):
    return _bdot(a, b), (a, b)


def _bdot_bwd(res, g):
    a, b = res
    gb = g.astype(BF16)
    da = lax.dot_general(gb, b.astype(BF16), (((1,), (1,)), ((), ())), preferred_element_type=F32)
    db = lax.dot_general(a.astype(BF16), gb, (((0,), (0,)), ((), ())), preferred_element_type=F32)
    return da, db


_bdot.defvjp(_bdot_fwd, _bdot_bwd)


def _seg_raw(x, ones_blocks):
    hi = x.astype(BF16)
    lo = (x - hi.astype(F32)).astype(BF16)
    return (jnp.dot(hi, ones_blocks, preferred_element_type=F32)
            + jnp.dot(lo, ones_blocks, preferred_element_type=F32))


@jax.custom_vjp
def _seg(x, ones_blocks):
    return _seg_raw(x, ones_blocks)


def _seg_fwd(x, ones_blocks):
    return _seg_raw(x, ones_blocks), ones_blocks


def _seg_bwd(ones_blocks, g):
    return _seg_raw(g, ones_blocks), jnp.zeros_like(ones_blocks)


_seg.defvjp(_seg_fwd, _seg_bwd)


def _head_ones():
    h = jnp.arange(D_RWKV) // HEAD
    return (h[:, None] == h[None, :]).astype(BF16)


def _prep_math(ps, k_k, w0_f, a0_f, k_a_f, w0_b, a0_b, k_a_b, wup_f, aup_f, wup_b, aup_b, gup, ones_blocks):
    r = ps[:, 0:512]
    k = ps[:, 512:1024]
    v = ps[:, 1024:1536]
    xwa = ps[:, 1536:1664]
    xg = ps[:, 1664:D_SHIFT_PAD]
    kk_raw = k * k_k
    norm = jnp.sqrt(_seg(kk_raw * kk_raw, ones_blocks))
    kk = kk_raw / jnp.maximum(norm, NORM_EPS)
    t_xwa = jnp.tanh(xwa)
    outs = [r, v, kk]
    for w0, a0, k_a, wup, aup in ((w0_f, a0_f, k_a_f, wup_f, aup_f), (w0_b, a0_b, k_a_b, wup_b, aup_b)):
        decay = jnp.exp(-LOG_DECAY_SCALE * jax.nn.sigmoid(w0 + _bdot(t_xwa, wup)))
        rate = jax.nn.sigmoid(a0 + _bdot(xwa, aup))
        outs += [decay, k * (1.0 + (rate - 1.0) * k_a), kk * rate]
    outs.append(_bdot(jax.nn.sigmoid(xg), gup))
    return tuple(outs)


def _prep_args(tm, ps_ref, pv_ref, mat_refs, ones_ref):
    vecs = [jnp.broadcast_to(pv_ref[j:j + 1, :], (tm, D_RWKV)) for j in range(7)]
    return [ps_ref[...]] + vecs + [m[...] for m in mat_refs] + [ones_ref[...]]


_PREP_MAT_SHAPES = ((128, D_RWKV),) * 4 + ((D_GATE_PAD, D_RWKV),)


def _prep_fwd(ps, pvec, mats, ones_blocks, tm):
    t = ps.shape[0]

    def body(ps_ref, pv_ref, m0, m1, m2, m3, m4, ones_ref, *out_refs):
        outs = _prep_math(*_prep_args(tm, ps_ref, pv_ref, (m0, m1, m2, m3, m4), ones_ref))
        for o_ref, val in zip(out_refs, outs[2:]):
            o_ref[...] = val

    return pl.pallas_call(
        body, name="prep_fwd", grid=(t // tm,),
        in_specs=[_row(tm, D_SHIFT_PAD), _fixed((8, D_RWKV))] + [_fixed(s) for s in _PREP_MAT_SHAPES]
        + [_fixed((D_RWKV, D_RWKV))],
        out_specs=[_row(tm, D_RWKV)] * 8, out_shape=[_sds((t, D_RWKV))] * 8,
        compiler_params=_cp("parallel"))(ps, pvec, *mats, ones_blocks)


def _prep_bwd(ps, pvec, mats, ones_blocks, cts, tm):
    t = ps.shape[0]
    counts = [len(c) for c in cts]
    flat = [a for c in cts for a in c]

    def body(ps_ref, pv_ref, m0, m1, m2, m3, m4, ones_ref, *refs):
        ct_refs = refs[:len(flat)]
        q_ref, dpv_ref = refs[len(flat)], refs[len(flat) + 1]
        dmat_refs = refs[len(flat) + 2:]
        args = _prep_args(tm, ps_ref, pv_ref, (m0, m1, m2, m3, m4), ones_ref)
        _, vjp = jax.vjp(lambda *a: _prep_math(*a, args[-1]), *args[:-1])
        ct_vals, pos = [], 0
        for n in counts:
            val = ct_refs[pos][...]
            for extra in ct_refs[pos + 1:pos + n]:
                val = val + extra[...]
            ct_vals.append(val)
            pos += n
        grads = vjp(tuple(ct_vals))
        q_ref[...] = grads[0]

        @pl.when(pl.program_id(0) == 0)
        def _():
            dpv_ref[...] = jnp.zeros_like(dpv_ref)
            for d_ref in dmat_refs:
                d_ref[...] = jnp.zeros_like(d_ref)

        for j in range(7):
            dpv_ref[j:j + 1, :] += jnp.sum(grads[1 + j], axis=0, keepdims=True)
        for d_ref, gm in zip(dmat_refs, grads[8:13]):
            d_ref[...] += gm

    return pl.pallas_call(
        body, name="prep_bwd", grid=(t // tm,),
        in_specs=[_row(tm, D_SHIFT_PAD), _fixed((8, D_RWKV))] + [_fixed(s) for s in _PREP_MAT_SHAPES]
        + [_fixed((D_RWKV, D_RWKV))] + [_row(tm, D_RWKV)] * len(flat),
        out_specs=[_row(tm, D_SHIFT_PAD), _fixed((8, D_RWKV))] + [_fixed(s) for s in _PREP_MAT_SHAPES],
        out_shape=[_sds((t, D_SHIFT_PAD)), _sds((8, D_RWKV))] + [_sds(s) for s in _PREP_MAT_SHAPES],
        compiler_params=_cp("arbitrary"))(ps, pvec, *mats, ones_blocks, *flat)


def _pair_ones():
    h = jnp.arange(2 * HEAD) // HEAD
    return (h[:, None] == h[None, :]).astype(BF16)


def _diag_mask():
    lane = lax.broadcasted_iota(jnp.int32, (HEAD, 2 * HEAD), 1)
    sub = lax.broadcasted_iota(jnp.int32, (HEAD, 2 * HEAD), 0)
    return jnp.where((lane & (HEAD - 1)) == sub, 1.0, 0.0).astype(F32)


def _to_row(cols, dmask):
    return jnp.sum(cols * dmask, axis=0, keepdims=True)


def _seg_many(exact, rounded, ones_pair):
    parts = []
    for x in exact:
        hi = x.astype(BF16)
        parts += [hi, (x - hi.astype(F32)).astype(BF16)]
    parts += [x.astype(BF16) for x in rounded]
    res = jnp.dot(jnp.concatenate(parts, axis=0), ones_pair, preferred_element_type=F32)
    n2 = 2 * len(exact)
    out_exact = [res[HEAD * (2 * c):HEAD * (2 * c + 1)] + res[HEAD * (2 * c + 1):HEAD * (2 * c + 2)]
                 for c in range(len(exact))]
    out_rounded = [res[HEAD * (n2 + c):HEAD * (n2 + c + 1)] for c in range(len(rounded))]
    return out_exact, out_rounded


N_CHAIN = 2 * N_PAIR


def _chain(c):
    d, p = divmod(c, N_PAIR)
    return d, slice(2 * HEAD * p, 2 * HEAD * (p + 1))


def _scan_specs(n_chunks, col_blocks, fwd_chunk, bwd_chunk):
    def spec(chunk_of, col):
        return pl.BlockSpec((SCAN_CHUNK, D_RWKV), lambda b, g: (b * n_chunks + chunk_of(g), col))
    return [spec(fwd_chunk, c) for c in col_blocks] + [spec(bwd_chunk, c) for c in col_blocks]


def _scan_fwd(ps, kk, dirs, batch, seq):
    t = batch * seq
    n = seq // SCAN_CHUNK
    groups = SCAN_CHUNK // GROUP
    up = lambda g: g
    down = lambda g: n - 1 - g
    col_blocks = (0, 2, 0, 0, 0, 0)

    def body(*refs):
        dir_refs = (refs[0:6], refs[6:12])
        ones_ref = refs[12]
        y_refs, ck_refs, st_ref = refs[13:15], refs[15:17], refs[17]

        @pl.when(pl.program_id(1) == 0)
        def _():
            st_ref[...] = jnp.zeros_like(st_ref)

        for d in range(2):
            ck_refs[d][...] = st_ref[N_PAIR * d:N_PAIR * (d + 1)]
        ones_pair = ones_ref[...]
        dmask = _diag_mask()
        sub8 = lax.broadcasted_iota(jnp.int32, (GROUP, 2 * HEAD), 0)

        def group(gi, carry):
            off = (pl.multiple_of(gi * GROUP, GROUP), pl.multiple_of((groups - 1 - gi) * GROUP, GROUP))
            loaded = [tuple(ref[pl.ds(off[d], GROUP), :] for ref in dir_refs[d]) for d in range(2)]
            states = list(carry)
            y_acc = [jnp.zeros((GROUP, 2 * HEAD), F32) for _ in range(N_CHAIN)]
            for step in range(GROUP):
                rows, idx = [], []
                for c in range(N_CHAIN):
                    d, lanes = _chain(c)
                    i = step if d == 0 else GROUP - 1 - step
                    idx.append(i)
                    rows.append(tuple(x8[i:i + 1, lanes] for x8 in loaded[d]))
                _, v_cols = _seg_many([], [dmask * rows[c][1] for c in range(N_CHAIN)], ones_pair)
                sas, _ = _seg_many([states[c] * rows[c][2] for c in range(N_CHAIN)], [], ones_pair)
                for c in range(N_CHAIN):
                    _, _, _, w_row, kd_row, b_row = rows[c]
                    states[c] = states[c] * w_row - sas[c] * b_row + v_cols[c] * kd_row
                _, ys = _seg_many([], [states[c] * rows[c][0] for c in range(N_CHAIN)], ones_pair)
                for c in range(N_CHAIN):
                    y_acc[c] = jnp.where(sub8 == idx[c], _to_row(ys[c], dmask), y_acc[c])
            for c in range(N_CHAIN):
                d, lanes = _chain(c)
                y_refs[d][pl.ds(off[d], GROUP), lanes] = y_acc[c]
            return tuple(states)

        final = lax.fori_loop(0, groups, group, tuple(st_ref[c] for c in range(N_CHAIN)))
        for c in range(N_CHAIN):
            st_ref[c] = final[c]

    y_spec_f = pl.BlockSpec((SCAN_CHUNK, D_RWKV), lambda b, g: (b * n + up(g), 0))
    y_spec_b = pl.BlockSpec((SCAN_CHUNK, D_RWKV), lambda b, g: (b * n + down(g), 0))
    ck_shape = (batch, n, N_PAIR, HEAD, 2 * HEAD)
    ck_spec_f = pl.BlockSpec((None, None, N_PAIR, HEAD, 2 * HEAD), lambda b, g: (b, up(g), 0, 0, 0))
    ck_spec_b = pl.BlockSpec((None, None, N_PAIR, HEAD, 2 * HEAD), lambda b, g: (b, down(g), 0, 0, 0))
    ones_spec = pl.BlockSpec((2 * HEAD, 2 * HEAD), lambda b, g: (0, 0))
    (wf, kdf, bf), (wb, kdb, bb) = dirs
    return pl.pallas_call(
        body, name="wkv_fwd", grid=(batch, n),
        in_specs=_scan_specs(n, col_blocks, up, down) + [ones_spec],
        out_specs=[y_spec_f, y_spec_b, ck_spec_f, ck_spec_b],
        out_shape=[_sds((t, D_RWKV)), _sds((t, D_RWKV)), _sds(ck_shape), _sds(ck_shape)],
        scratch_shapes=[pltpu.VMEM((N_CHAIN, HEAD, 2 * HEAD), F32)],
        compiler_params=_cp("parallel", "arbitrary"),
    )(ps, ps, kk, wf, kdf, bf, ps, ps, kk, wb, kdb, bb, _pair_ones())


def _scan_bwd(ps, kk, dirs, dy, ck_f, ck_b, batch, seq):
    t = batch * seq
    n = seq // SCAN_CHUNK
    groups = SCAN_CHUNK // GROUP
    fwd_chunk = lambda g: n - 1 - g
    bwd_chunk = lambda g: g
    col_blocks = (0, 2, 0, 0, 0, 0, 0)

    def recompute(dir_refs, ck_refs, hist_ref, ones_pair, dmask):
        def group(gi, states):
            states = list(states)
            loaded = []
            for d in range(2):
                blk = gi if d == 0 else groups - 1 - gi
                rows = pl.ds(pl.multiple_of(blk * GROUP, GROUP), GROUP)
                loaded.append(tuple(ref[rows, :] for ref in dir_refs[d][1:6]))
            for step in range(GROUP):
                rows = []
                for c in range(N_CHAIN):
                    d, lanes = _chain(c)
                    i = step if d == 0 else GROUP - 1 - step
                    rows.append(tuple(x8[i:i + 1, lanes] for x8 in loaded[d]))
                    hist_ref[c, gi * GROUP + step] = states[c]
                _, v_cols = _seg_many([], [dmask * rows[c][0] for c in range(N_CHAIN)], ones_pair)
                sas, _ = _seg_many([states[c] * rows[c][1] for c in range(N_CHAIN)], [], ones_pair)
                for c in range(N_CHAIN):
                    _, _, w_row, kd_row, b_row = rows[c]
                    states[c] = states[c] * w_row - sas[c] * b_row + v_cols[c] * kd_row
            return tuple(states)

        first = tuple(ck_refs[d][p] for d in range(2) for p in range(N_PAIR))
        last = lax.fori_loop(0, groups, group, first)
        for c in range(N_CHAIN):
            hist_ref[c, SCAN_CHUNK] = last[c]

    def undo_group(dir_refs, out_refs, hist_ref, gi, d_states, ones_pair, dmask, sub8):
        d_states = list(d_states)
        loaded, blocks = [], []
        for d in range(2):
            blk = groups - 1 - gi if d == 0 else gi
            blocks.append(pl.ds(pl.multiple_of(blk * GROUP, GROUP), GROUP))
            loaded.append(tuple(ref[blocks[d], :] for ref in dir_refs[d]))
        acc = [[jnp.zeros((GROUP, 2 * HEAD), F32) for _ in range(6)] for _ in range(N_CHAIN)]
        for step in range(GROUP):
            rows, idx, before, after = [], [], [], []
            for c in range(N_CHAIN):
                d, lanes = _chain(c)
                i = GROUP - 1 - step if d == 0 else step
                q = (groups - 1 - gi) * GROUP + i if d == 0 else SCAN_CHUNK - 1 - (gi * GROUP + i)
                idx.append(i)
                rows.append(tuple(x8[i:i + 1, lanes] for x8 in loaded[d]))
                before.append(hist_ref[c, q])
                after.append(hist_ref[c, q + 1])
            _, cols = _seg_many([], [dmask * rows[c][j] for c in range(N_CHAIN) for j in (1, 6)], ones_pair)
            v_cols, dy_cols = cols[0::2], cols[1::2]
            d_now = [d_states[c] + dy_cols[c] * rows[c][0] for c in range(N_CHAIN)]
            d_sas, _ = _seg_many([d_now[c] * rows[c][5] for c in range(N_CHAIN)], [], ones_pair)
            _, others = _seg_many(
                [], [x for c in range(N_CHAIN) for x in (before[c] * rows[c][2], d_now[c] * rows[c][4])], ones_pair)
            for c in range(N_CHAIN):
                sa, d_sa, dv_cols = others[2 * c], -d_sas[c], others[2 * c + 1]
                rows_out = (
                    jnp.sum(after[c] * dy_cols[c], axis=0, keepdims=True),
                    jnp.sum(d_now[c] * before[c], axis=0, keepdims=True),
                    jnp.sum(d_now[c] * v_cols[c], axis=0, keepdims=True),
                    _to_row(dv_cols, dmask),
                    jnp.sum(before[c] * d_sa, axis=0, keepdims=True),
                    -jnp.sum(d_now[c] * sa, axis=0, keepdims=True),
                )
                acc[c] = [jnp.where(sub8 == idx[c], val, a) for val, a in zip(rows_out, acc[c])]
                d_states[c] = d_now[c] * rows[c][3] + d_sa * rows[c][2]
        for c in range(N_CHAIN):
            d, lanes = _chain(c)
            for o_ref, val in zip(out_refs[d], acc[c]):
                o_ref[blocks[d], lanes] = val
        return tuple(d_states)

    def body(*refs):
        dir_refs = (refs[0:7], refs[7:14])
        ck_refs, ones_ref = refs[14:16], refs[16]
        out_refs = (refs[17:23], refs[23:29])
        dst_ref, hist_ref = refs[29:31]

        @pl.when(pl.program_id(1) == 0)
        def _():
            dst_ref[...] = jnp.zeros_like(dst_ref)

        ones_pair = ones_ref[...]
        dmask = _diag_mask()
        sub8 = lax.broadcasted_iota(jnp.int32, (GROUP, 2 * HEAD), 0)
        recompute(dir_refs, ck_refs, hist_ref, ones_pair, dmask)

        def group(gi, carry):
            return undo_group(dir_refs, out_refs, hist_ref, gi, carry, ones_pair, dmask, sub8)

        final = lax.fori_loop(0, groups, group, tuple(dst_ref[c] for c in range(N_CHAIN)))
        for c in range(N_CHAIN):
            dst_ref[c] = final[c]

    blk = (SCAN_CHUNK, D_RWKV)
    out_f = pl.BlockSpec(blk, lambda b, g: (b * n + fwd_chunk(g), 0))
    out_b = pl.BlockSpec(blk, lambda b, g: (b * n + bwd_chunk(g), 0))
    ck_spec_f = pl.BlockSpec((None, None, N_PAIR, HEAD, 2 * HEAD), lambda b, g: (b, fwd_chunk(g), 0, 0, 0))
    ck_spec_b = pl.BlockSpec((None, None, N_PAIR, HEAD, 2 * HEAD), lambda b, g: (b, bwd_chunk(g), 0, 0, 0))
    ones_spec = pl.BlockSpec((2 * HEAD, 2 * HEAD), lambda b, g: (0, 0))
    (wf, kdf, bf), (wb, kdb, bb) = dirs
    outs = pl.pallas_call(
        body, name="wkv_bwd", grid=(batch, n),
        in_specs=_scan_specs(n, col_blocks, fwd_chunk, bwd_chunk) + [ck_spec_f, ck_spec_b, ones_spec],
        out_specs=[out_f] * 6 + [out_b] * 6,
        out_shape=[_sds((t, D_RWKV))] * 12,
        scratch_shapes=[pltpu.VMEM((N_CHAIN, HEAD, 2 * HEAD), F32),
                        pltpu.VMEM((N_CHAIN, SCAN_CHUNK + 1, HEAD, 2 * HEAD), F32)],
        compiler_params=_cp("parallel", "arbitrary"),
    )(ps, ps, kk, wf, kdf, bf, dy, ps, ps, kk, wb, kdb, bb, dy, ck_f, ck_b, _pair_ones())
    return outs[0:6], outs[6:12]


def _post_math(y, r, kd_f, kd_b, v, gate, gn_w, gn_b, rk_f, rk_b, ones_blocks):
    mean = _seg(y, ones_blocks) * (1.0 / HEAD)
    yc = y - mean
    var = _seg(yc * yc, ones_blocks) * (1.0 / HEAD)
    yn = yc * lax.rsqrt(var + GN_EPS) * gn_w + gn_b
    bonus = _seg(r * kd_f * rk_f, ones_blocks) * v + _seg(r * kd_b * rk_b, ones_blocks) * v
    return (yn + bonus) * gate


def _conv_parts(pc, halo_prev, halo_next, has_prev, has_next):
    gate_b, gate_c, hid = pc[:, 0:512], pc[:, 512:1024], pc[:, 1024:1536]
    u = gate_c * hid
    u_prev_row = halo_prev[GROUP - 1:GROUP, 512:1024] * halo_prev[GROUP - 1:GROUP, 1024:1536] * has_prev
    u_next_row = halo_next[0:1, 512:1024] * halo_next[0:1, 1024:1536] * has_next
    u_down, u_up = _shifted(u, u_prev_row, u_next_row)
    return gate_b, gate_c, hid, u, u_down, u_up


def _post_specs(tm, t):
    pc_prev, pc_next = _halo_specs(tm, D_CONV3, t)
    col = lambda c: pl.BlockSpec((tm, D_RWKV), lambda i: (i, c))
    return ([col(0), col(0), col(0), col(0), col(0), col(2), col(0), _row(tm, D_CONV3), pc_prev, pc_next,
             _fixed((8, D_RWKV)), _fixed((D_RWKV, D_RWKV))])


def _post_fwd(y_f, y_b, ps, kd_f, kd_b, gate, pc, qvec, ones_blocks, tm, seq):
    t = ps.shape[0]

    def body(yf_ref, yb_ref, r_ref, kdf_ref, kdb_ref, v_ref, g_ref, pc_ref, hp_ref, hn_ref, qv_ref, ones_ref,
             o_ref):
        has_prev, has_next = _edge_flags(tm, seq)
        vec = [jnp.broadcast_to(qv_ref[j:j + 1, :], (tm, D_RWKV)) for j in range(7)]
        o_ref[:, 0:D_RWKV] = _post_math(yf_ref[...] + yb_ref[...], r_ref[...], kdf_ref[...], kdb_ref[...],
                                        v_ref[...], g_ref[...], vec[0], vec[1], vec[2], vec[3],
                                        ones_ref[...]).astype(BF16)
        gate_b, _, _, u, u_down, u_up = _conv_parts(pc_ref[...], hp_ref[...], hn_ref[...], has_prev, has_next)
        o_ref[:, D_RWKV:2 * D_RWKV] = (gate_b * (vec[4] * u_down + vec[5] * u + vec[6] * u_up)).astype(BF16)

    return pl.pallas_call(
        body, name="post_fwd", grid=(t // tm,), in_specs=_post_specs(tm, t), out_specs=_row(tm, D_MODEL),
        out_shape=_sds((t, D_MODEL), BF16), compiler_params=_cp("parallel"),
    )(y_f, y_b, ps, kd_f, kd_b, ps, gate, pc, pc, pc, qvec, ones_blocks)


def _post_bwd(d_out, y_f, y_b, ps, kd_f, kd_b, gate, pc, qvec, ones_blocks, tm, seq):
    t = ps.shape[0]
    do_prev, do_next = _halo_specs(tm, D_MODEL, t)

    def body(do_ref, dop_ref, don_ref, yf_ref, yb_ref, r_ref, kdf_ref, kdb_ref, v_ref, g_ref, pc_ref, hp_ref,
             hn_ref, qv_ref, ones_ref, dy_ref, dr_ref, dkdf_ref, dkdb_ref, dv_ref, dg_ref, dpc_ref, dqv_ref):
        has_prev, has_next = _edge_flags(tm, seq)
        vec = [jnp.broadcast_to(qv_ref[j:j + 1, :], (tm, D_RWKV)) for j in range(7)]
        ones_v = ones_ref[...]
        args = (yf_ref[...] + yb_ref[...], r_ref[...], kdf_ref[...], kdb_ref[...], v_ref[...], g_ref[...],
                vec[0], vec[1], vec[2], vec[3])
        _, vjp = jax.vjp(lambda *a: _post_math(*a, ones_v), *args)
        grads = vjp(do_ref[:, 0:D_RWKV])
        for o_ref, gval in zip((dy_ref, dr_ref, dkdf_ref, dkdb_ref, dv_ref, dg_ref), grads[0:6]):
            o_ref[...] = gval

        hp, hn = hp_ref[...], hn_ref[...]
        gate_b, gate_c, hid, u, u_down, u_up = _conv_parts(pc_ref[...], hp, hn, has_prev, has_next)
        d_oc = do_ref[:, D_RWKV:2 * D_RWKV]
        d_cu = d_oc * gate_b
        d_cu_prev = dop_ref[GROUP - 1:GROUP, D_RWKV:2 * D_RWKV] * hp[GROUP - 1:GROUP, 0:512] * has_prev
        d_cu_next = don_ref[0:1, D_RWKV:2 * D_RWKV] * hn[0:1, 0:512] * has_next
        d_cu_down, d_cu_up = _shifted(d_cu, d_cu_prev, d_cu_next)
        d_u = vec[5] * d_cu + vec[4] * d_cu_up + vec[6] * d_cu_down
        dpc_ref[:, 0:512] = (d_oc * (vec[4] * u_down + vec[5] * u + vec[6] * u_up)).astype(BF16)
        dpc_ref[:, 512:1024] = (d_u * hid).astype(BF16)
        dpc_ref[:, 1024:1536] = (d_u * gate_c).astype(BF16)

        @pl.when(pl.program_id(0) == 0)
        def _():
            dqv_ref[...] = jnp.zeros_like(dqv_ref)

        vec_grads = list(grads[6:10]) + [d_cu * u_down, d_cu * u, d_cu * u_up]
        for j, gval in enumerate(vec_grads):
            dqv_ref[j:j + 1, :] += jnp.sum(gval, axis=0, keepdims=True)

    return pl.pallas_call(
        body, name="post_bwd", grid=(t // tm,),
        in_specs=[_row(tm, D_MODEL), do_prev, do_next] + _post_specs(tm, t),
        out_specs=[_row(tm, D_RWKV)] * 6 + [_row(tm, D_CONV3), _fixed((8, D_RWKV))],
        out_shape=[_sds((t, D_RWKV))] * 6 + [_sds((t, D_CONV3), BF16), _sds((8, D_RWKV))],
        compiler_params=_cp("arbitrary"),
    )(d_out, d_out, d_out, y_f, y_b, ps, kd_f, kd_b, ps, gate, pc, pc, pc, qvec, ones_blocks)


def _adamw(w, g, m, v, name):
    r, c = w.shape
    tr = _tile(r, 256, 8)
    spec = pl.BlockSpec((tr, c), lambda i: (i, 0))

    def body(w_ref, g_ref, m_ref, v_ref, d_ref, nm_ref, nv_ref):
        gv = g_ref[...]
        m2 = ADAM_B1 * m_ref[...] + (1.0 - ADAM_B1) * gv
        v2 = ADAM_B2 * v_ref[...] + (1.0 - ADAM_B2) * (gv * gv)
        m_hat = m2 / (1.0 - ADAM_B1 ** ADAM_STEP)
        v_hat = v2 / (1.0 - ADAM_B2 ** ADAM_STEP)
        d_ref[...] = -ADAM_LR * (m_hat / (jnp.sqrt(v_hat) + ADAM_EPS) + ADAM_WD * w_ref[...])
        nm_ref[...] = m2
        nv_ref[...] = v2

    return pl.pallas_call(
        body, name=name, grid=(r // tr,), in_specs=[spec] * 4, out_specs=[spec] * 3,
        out_shape=[_sds((r, c))] * 3, compiler_params=_cp("parallel"))(w, g, m, v)


def _add_n(parts, name):
    r, c = parts[0].shape
    tr = _tile(r, 408, 8)
    spec = pl.BlockSpec((tr, c), lambda i: (i, 0))

    def body(*refs):
        acc = refs[0][...]
        for ref in refs[1:-1]:
            acc = acc + ref[...]
        refs[-1][...] = acc

    return pl.pallas_call(
        body, name=name, grid=(r // tr,), in_specs=[spec] * len(parts), out_specs=spec,
        out_shape=_sds((r, c)), compiler_params=_cp("parallel"))(*parts)


_ANY = pl.BlockSpec(memory_space=pl.ANY)


def _place():
    return lax.axis_index("x"), lax.axis_index("y"), lax.axis_index("c")


def _other_chips(x, y):
    return [(1 - x, y), (x, 1 - y), (1 - x, 1 - y)]


def _remote(src, dst, send_sems, recv_sems, k, to):
    return pltpu.make_async_remote_copy(src_ref=src, dst_ref=dst, send_sem=send_sems.at[k],
                                        recv_sem=recv_sems.at[k], device_id=to, device_id_type=MESH)


def _gather_weights(pack):
    rows, width = pack.shape
    half = rows // 2

    def body(x_ref, out_ref, send_sems, recv_sems):
        x, y, c = _place()
        sibling = (x, y, 1 - c)
        chips = _other_chips(x, y)

        def block(chip, part):
            return out_ref.at[2 * chip[0] + chip[1], pl.ds(part * half, half), :]

        first = [_remote(x_ref.at[pl.ds(c * half, half), :], block((x, y), c), send_sems, recv_sems, j, (*chip, c))
                 for j, chip in enumerate(chips)]
        for cp in first:
            cp.start()
        passed = [_remote(block(chip, c), block(chip, c), send_sems, recv_sems, 3 + j, sibling)
                  for j, chip in enumerate(chips)]
        for j, chip in enumerate(chips):
            _remote(block(chip, c), block(chip, c), send_sems, recv_sems, j, sibling).wait_recv()
            passed[j].start()
        for j, chip in enumerate(chips):
            _remote(block(chip, 1 - c), block(chip, 1 - c), send_sems, recv_sems, 3 + j, sibling).wait_recv()
        for cp in first + passed:
            cp.wait_send()

    return pl.pallas_call(
        body, name="gather_weights", in_specs=[_ANY], out_specs=_ANY,
        out_shape=_sds((N_SHARD, rows, width), pack.dtype),
        scratch_shapes=[pltpu.SemaphoreType.DMA((6,)), pltpu.SemaphoreType.DMA((6,))],
    )(pack)


def _swap_with_sibling(block, name):
    def body(x_ref, out_ref, send_sems, recv_sems):
        x, y, c = _place()
        cp = _remote(x_ref, out_ref, send_sems, recv_sems, 0, (x, y, 1 - c))
        cp.start()
        cp.wait()

    return pl.pallas_call(
        body, name=name, in_specs=[_ANY], out_specs=_ANY, out_shape=_sds(block.shape, block.dtype),
        scratch_shapes=[pltpu.SemaphoreType.DMA((1,)), pltpu.SemaphoreType.DMA((1,))],
    )(block)


def _exchange_quarters(parts):
    _, rows, width = parts.shape

    def body(x_ref, out_ref, send_sems, recv_sems):
        x, y, c = _place()
        copies = [_remote(x_ref.at[2 * chip[0] + chip[1]], out_ref.at[j], send_sems, recv_sems, j, (*chip, c))
                  for j, chip in enumerate(_other_chips(x, y))]
        for cp in copies:
            cp.start()
        for cp in copies:
            cp.wait()

    return pl.pallas_call(
        body, name="exchange_quarters", in_specs=[_ANY], out_specs=_ANY,
        out_shape=_sds((3, rows, width), parts.dtype),
        scratch_shapes=[pltpu.SemaphoreType.DMA((3,)), pltpu.SemaphoreType.DMA((3,))],
    )(parts)


def _allreduce_small(vec):
    rows, width = vec.shape
    vmem = pl.BlockSpec(memory_space=pltpu.VMEM)

    def body(x_ref, o_ref, buf_ref, send_sems, recv_sems):
        x, y, c = _place()
        me = 4 * x + 2 * y + c
        buf_ref[me] = x_ref[...]
        copies = []
        for k in range(1, N_DEV):
            peer = (x ^ ((k >> 2) & 1), y ^ ((k >> 1) & 1), c ^ (k & 1))
            copies.append(_remote(x_ref, buf_ref.at[me], send_sems, recv_sems, k - 1, peer))
        for cp in copies:
            cp.start()
        for k in range(1, N_DEV):
            _remote(x_ref, buf_ref.at[me ^ k], send_sems, recv_sems, k - 1, (x, y, c)).wait_recv()
        for cp in copies:
            cp.wait_send()
        total = buf_ref[0]
        for d in range(1, N_DEV):
            total = total + buf_ref[d]
        o_ref[...] = total

    return pl.pallas_call(
        body, name="allreduce_small", in_specs=[vmem], out_specs=vmem, out_shape=_sds((rows, width)),
        scratch_shapes=[pltpu.VMEM((N_DEV, rows, width), F32), pltpu.SemaphoreType.DMA((N_DEV - 1,)),
                        pltpu.SemaphoreType.DMA((N_DEV - 1,))],
    )(vec)


def _rows1024(a):
    return a.reshape(-1, 1024)


def _pad_rows(a, rows):
    return jnp.concatenate([a, jnp.zeros((rows - a.shape[0], a.shape[1]), a.dtype)], axis=0)


def _pack_weight_shards(w):
    conv_bits = lax.bitcast_convert_type(w["conv_w"], BF16).reshape(1, -1)
    conv_row = jnp.concatenate([conv_bits, jnp.zeros((1, 1024 - conv_bits.shape[1]), BF16)], axis=1)
    parts = [_rows1024(w[name].astype(BF16)) for name, _ in _PACK_ROWS[:-1]] + [conv_row]
    return _pad_rows(jnp.concatenate(parts, axis=0), PACK_R)


def _unpack_gathered(gathered):
    out, row = {}, 0
    for name, n in _PACK_ROWS:
        out[name] = gathered[:, row:row + n]
        row += n
    cols = lambda a, k: jnp.concatenate([a[s].reshape(k, -1) for s in range(N_SHARD)], axis=1)
    conv = lax.bitcast_convert_type(out["conv_w"][:, 0, :768].reshape(N_SHARD, 3, 128, 2), F32)
    return dict(
        w_in=cols(out["w_in"], D_MODEL), w_out=out["w_out"].reshape(D_MODEL, D_MODEL),
        w_gate=cols(out["w_gate"], D_MODEL), w_up=cols(out["w_up"], D_MODEL),
        w_down=out["w_down"].reshape(D_FF, D_MODEL),
        w_up_f=cols(out["w_up_f"], D_LORA), w_up_b=cols(out["w_up_b"], D_LORA),
        a_up_f=cols(out["a_up_f"], D_LORA), a_up_b=cols(out["a_up_b"], D_LORA),
        g_up=cols(out["g_up"], D_GATE), conv_w=jnp.concatenate([conv[s] for s in range(N_SHARD)], axis=1))


def _pack_grads(g):
    col_split = lambda a, s: a[:, s * (a.shape[1] // N_SHARD):(s + 1) * (a.shape[1] // N_SHARD)]
    row_split = lambda a, s: a[s * (a.shape[0] // N_SHARD):(s + 1) * (a.shape[0] // N_SHARD)]
    slots = []
    for s in range(N_SHARD):
        conv = col_split(g["conv_w"], s).reshape(1, -1)
        parts = [_rows1024(col_split(g["w_in"], s)), row_split(g["w_out"], s), _rows1024(col_split(g["w_gate"], s)),
                 _rows1024(col_split(g["w_up"], s)), row_split(g["w_down"], s)]
        parts += [_rows1024(col_split(g[name], s)) for name in ("w_up_f", "w_up_b", "a_up_f", "a_up_b", "g_up")]
        parts.append(jnp.concatenate([conv, jnp.zeros((1, 1024 - conv.shape[1]), F32)], axis=1))
        slots.append(_pad_rows(jnp.concatenate(parts, axis=0), PACK_R))
    return jnp.stack(slots)


def _unpack_grad_shard(pack):
    shapes = dict(w_in=(D_MODEL, 840), w_out=(256, D_MODEL), w_gate=(D_MODEL, 704), w_up=(D_MODEL, 704),
                  w_down=(704, D_MODEL), w_up_f=(D_LORA, 128), w_up_b=(D_LORA, 128), a_up_f=(D_LORA, 128),
                  a_up_b=(D_LORA, 128), g_up=(D_GATE, 128))
    out, row = {}, 0
    for name, n in _PACK_ROWS[:-1]:
        out[name] = pack[row:row + n].reshape(shapes[name])
        row += n
    out["conv_w"] = pack[row, :384].reshape(3, 128)
    return out


_SMALL_LAYOUT = (("norm1_w", 1024), ("mu_shift", D_SHIFT), ("w0_f", 512), ("w0_b", 512), ("a0_f", 512),
                 ("a0_b", 512), ("k_k", 512), ("k_a_f", 512), ("k_a_b", 512), ("r_k_f", 512), ("r_k_b", 512),
                 ("gn_w", 512), ("gn_b", 512), ("norm2_w", 1024), ("norm_f_w", 1024), ("loss", 1))


def _pack_small(vals):
    rows = []
    for name, n in _SMALL_LAYOUT:
        flat = vals[name].reshape(-1)
        n_rows = -(-n // 1024)
        rows.append(jnp.concatenate([flat, jnp.zeros((n_rows * 1024 - n,), F32)]).reshape(n_rows, 1024))
    return _pad_rows(jnp.concatenate(rows, axis=0), SMALL_ROWS)


def _unpack_small(pack):
    out, row = {}, 0
    for name, n in _SMALL_LAYOUT:
        n_rows = -(-n // 1024)
        out[name] = pack[row:row + n_rows].reshape(-1)[:n]
        row += n_rows
    return out


_WEIGHTS = ("norm1_w", "w_in", "mu_shift", "w_up_f", "w0_f", "w_up_b", "w0_b", "a_up_f", "a0_f", "a_up_b", "a0_b",
            "g_up", "k_k", "k_a_f", "k_a_b", "r_k_f", "r_k_b", "gn_w", "gn_b", "conv_w", "w_out", "norm2_w",
            "w_gate", "w_up", "w_down", "norm_f_w")


def _train_step(x, loss_target, w, m, v):
    batch, seq, _ = x.shape
    t = batch * seq
    tm = _tile(seq, 256, 8)
    xs = x.reshape(t, D_MODEL)
    target = loss_target.reshape(t, D_MODEL)
    vec = lambda name: w[name].reshape(1, -1)

    local = {name: w[name][0] for name, _ in _PACK_ROWS}
    c = lax.axis_index("c")
    chip = 2 * lax.axis_index("x") + lax.axis_index("y")
    pack = _pack_weight_shards(local)
    full = _unpack_gathered(lax.dynamic_update_slice(_gather_weights(pack), pack[None], (chip, 0, 0)))
    w_in = full["w_in"]
    w_shift = jnp.concatenate([w_in[:, :D_SHIFT], jnp.zeros((D_MODEL, D_SHIFT_PAD - D_SHIFT), BF16)], axis=1)
    w_conv = w_in[:, D_SHIFT:]
    zeros_lora = jnp.zeros((D_LORA, D_RWKV), F32)
    lora = lambda name: full[name].astype(F32)
    mats = (jnp.concatenate([lora("w_up_f"), zeros_lora]), jnp.concatenate([zeros_lora, lora("a_up_f")]),
            jnp.concatenate([lora("w_up_b"), zeros_lora]), jnp.concatenate([zeros_lora, lora("a_up_b")]),
            jnp.concatenate([lora("g_up"), jnp.zeros((D_GATE_PAD - D_GATE, D_RWKV), F32)]))
    mu = jnp.concatenate([vec("mu_shift"), jnp.zeros((1, D_SHIFT_PAD - D_SHIFT), F32)], axis=1)
    mu = jnp.broadcast_to(mu, (GROUP, D_SHIFT_PAD))
    zero_row = jnp.zeros((1, D_RWKV), F32)
    pvec = jnp.concatenate([vec("k_k"), vec("w0_f"), vec("a0_f"), vec("k_a_f"), vec("w0_b"), vec("a0_b"),
                            vec("k_a_b"), zero_row], axis=0)
    qvec = jnp.concatenate([vec("gn_w"), vec("gn_b"), vec("r_k_f"), vec("r_k_b"), full["conv_w"], zero_row], axis=0)
    ones_blocks = _head_ones()

    h1 = _rmsnorm_fwd(xs, vec("norm1_w"), tm, "norm1_fwd")
    p_shift = _matmul(h1, w_shift, mode="nn", name="in_proj_shift")
    pc = _matmul(h1, w_conv, mode="nn", name="in_proj_conv")
    ps = _shift_fwd(p_shift, mu, tm, seq)
    kk, w_f, kd_f, b_f, w_b, kd_b, b_b, gate = _prep_fwd(ps, pvec, mats, ones_blocks, tm)
    dirs = ((w_f, kd_f, b_f), (w_b, kd_b, b_b))
    y_f, y_b, ck_f, ck_b = _scan_fwd(ps, kk, dirs, batch, seq)
    mixed = _post_fwd(y_f, y_b, ps, kd_f, kd_b, gate, pc, qvec, ones_blocks, tm, seq)
    x1 = _matmul(mixed, full["w_out"], mode="nn", name="out_proj", add=xs)
    h2 = _rmsnorm_fwd(x1, vec("norm2_w"), tm, "norm2_fwd")
    ff_gate = _matmul(h2, full["w_gate"], mode="nn", name="ffn_gate")
    ff_up = _matmul(h2, full["w_up"], mode="nn", name="ffn_up")
    act = _swiglu_fwd(ff_gate, ff_up, tm)
    x2 = _matmul(act, full["w_down"], mode="nn", name="ffn_down", add=x1)
    d_x2, d_norm_f, loss_part = _loss_head(x2, w["norm_f_w"].reshape(1, -1), target, tm)

    g = {}
    d_act = _matmul(d_x2, full["w_down"], mode="nt", name="ffn_down_dx")
    g["w_down"] = _matmul(d_x2, act, mode="tn", name="ffn_down_dw").T
    d_gate, d_up = _swiglu_bwd(ff_gate, ff_up, d_act, tm)
    d_h2 = _matmul(d_gate, full["w_gate"], mode="nt", name="ffn_gate_dx")
    d_h2 = _matmul(d_up, full["w_up"], mode="nt", name="ffn_up_dx", add=d_h2)
    g["w_gate"] = _matmul(h2, d_gate, mode="tn", name="ffn_gate_dw")
    g["w_up"] = _matmul(h2, d_up, mode="tn", name="ffn_up_dw")
    d_x1, d_norm2 = _rmsnorm_bwd(x1, vec("norm2_w"), d_h2, d_x2, tm, "norm2_bwd")
    d_mixed = _matmul(d_x1, full["w_out"], mode="nt", name="out_proj_dx")
    g["w_out"] = _matmul(mixed, d_x1, mode="tn", name="out_proj_dw")
    dy, dr_o, dkdf_o, dkdb_o, dv_o, d_gatev, d_pc, d_qvec = _post_bwd(
        d_mixed, y_f, y_b, ps, kd_f, kd_b, gate, pc, qvec, ones_blocks, tm, seq)
    (dr_f, dw_f, dkd_f, dv_f, dkk_f, db_f), (dr_b, dw_b, dkd_b, dv_b, dkk_b, db_b) = _scan_bwd(
        ps, kk, dirs, dy, ck_f, ck_b, batch, seq)
    cts = [[dr_f, dr_b, dr_o], [dv_f, dv_b, dv_o], [dkk_f, dkk_b], [dw_f], [dkd_f, dkdf_o], [db_f],
           [dw_b], [dkd_b, dkdb_o], [db_b], [d_gatev]]
    q, d_pvec, d_m0, d_m1, d_m2, d_m3, d_m4 = _prep_bwd(ps, pvec, mats, ones_blocks, cts, tm)
    d_pshift, d_mu = _shift_bwd(q, p_shift, mu, tm, seq)
    d_h1 = _matmul(d_pshift, w_shift, mode="nt", name="in_proj_shift_dx")
    d_h1 = _matmul(d_pc, w_conv, mode="nt", name="in_proj_conv_dx", add=d_h1)
    d_w_shift = _matmul(h1, d_pshift, mode="tn", name="in_proj_shift_dw")
    d_w_conv = _matmul(h1, d_pc, mode="tn", name="in_proj_conv_dw")
    g["w_in"] = jnp.concatenate([d_w_shift[:, :D_SHIFT], d_w_conv], axis=1)
    d_x, d_norm1 = _rmsnorm_bwd(xs, vec("norm1_w"), d_h1, d_x1, tm, "norm1_bwd")
    g["w_up_f"], g["a_up_f"] = d_m0[:D_LORA], d_m1[D_LORA:]
    g["w_up_b"], g["a_up_b"] = d_m2[:D_LORA], d_m3[D_LORA:]
    g["g_up"] = d_m4[:D_GATE]
    g["conv_w"] = d_qvec[4:7]

    packed = _pack_grads(g)
    keep = lax.dynamic_slice_in_dim(packed, c * PACK_H, PACK_H, axis=1)
    give = lax.dynamic_slice_in_dim(packed, (1 - c) * PACK_H, PACK_H, axis=1)
    got = _swap_with_sibling(give, "swap_halves")
    chip_sum = _add_n([keep.reshape(-1, 1024), got.reshape(-1, 1024)], "add_halves").reshape(N_SHARD, PACK_H, 1024)
    others = _exchange_quarters(chip_sum)
    own = lax.dynamic_index_in_dim(chip_sum, chip, axis=0, keepdims=False)
    eighth = _add_n([own, others[0], others[1], others[2]], "add_quarters")
    other_eighth = _swap_with_sibling(eighth, "swap_eighths")
    grads = _unpack_grad_shard(jnp.concatenate([jnp.where(c == 0, eighth, other_eighth),
                                                jnp.where(c == 0, other_eighth, eighth)], axis=0))

    small = dict(norm1_w=d_norm1, mu_shift=d_mu[:, :D_SHIFT], w0_f=d_pvec[1], w0_b=d_pvec[4], a0_f=d_pvec[2],
                 a0_b=d_pvec[5], k_k=d_pvec[0], k_a_f=d_pvec[3], k_a_b=d_pvec[6], r_k_f=d_qvec[2], r_k_b=d_qvec[3],
                 gn_w=d_qvec[0], gn_b=d_qvec[1], norm2_w=d_norm2, norm_f_w=d_norm_f, loss=loss_part)
    reduced = _unpack_small(_allreduce_small(_pack_small(small)))
    loss = reduced.pop("loss")[0]
    grads.update(reduced)

    outs = {}
    for name in _WEIGHTS:
        shape = w[name].shape
        as2d = (1, shape[0]) if len(shape) == 1 else (-1, shape[-1]) if name not in ("r_k_f", "r_k_b") else (1, -1)
        grad = grads[name].reshape(shape)
        delta, new_m, new_v = _adamw(w[name].reshape(as2d), grad.reshape(as2d), m[name].reshape(as2d),
                                     v[name].reshape(as2d), "adamw_" + name)
        outs[name] = (grad, delta.reshape(shape), new_m.reshape(shape), new_v.reshape(shape))
    d_x = d_x.reshape(batch, seq, D_MODEL)
    return (loss, d_x) + tuple(outs[name][k] for k in range(4) for name in _WEIGHTS)


def kernel(x, norm1_w, w_in, mu_shift, w_up_f, w0_f, w_up_b, w0_b, a_up_f, a0_f, a_up_b, a0_b, g_up, k_k, k_a_f, k_a_b, r_k_f, r_k_b, gn_w, gn_b, conv_w, w_out, norm2_w, w_gate, w_up, w_down, norm_f_w, loss_target, m_norm1_w, m_w_in, m_mu_shift, m_w_up_f, m_w0_f, m_w_up_b, m_w0_b, m_a_up_f, m_a0_f, m_a_up_b, m_a0_b, m_g_up, m_k_k, m_k_a_f, m_k_a_b, m_r_k_f, m_r_k_b, m_gn_w, m_gn_b, m_conv_w, m_w_out, m_norm2_w, m_w_gate, m_w_up, m_w_down, m_norm_f_w, v_norm1_w, v_w_in, v_mu_shift, v_w_up_f, v_w0_f, v_w_up_b, v_w0_b, v_a_up_f, v_a0_f, v_a_up_b, v_a0_b, v_g_up, v_k_k, v_k_a_f, v_k_a_b, v_r_k_f, v_r_k_b, v_gn_w, v_gn_b, v_conv_w, v_w_out, v_norm2_w, v_w_gate, v_w_up, v_w_down, v_norm_f_w):
    args = locals()
    w = {name: args[name] for name in _WEIGHTS}
    m = {name: args["m_" + name] for name in _WEIGHTS}
    v = {name: args["v_" + name] for name in _WEIGHTS}
    return _train_step(x, loss_target, w, m, v)
```

```python
import functools

import jax
import jax.numpy as jnp
from jax import lax
from jax.experimental import pallas as pl
from jax.experimental.pallas import tpu as pltpu

F32 = jnp.float32
BF16 = jnp.bfloat16
MESH = pl.DeviceIdType.MESH

D_MODEL = 1024
D_RWKV = 512
HEAD = 64
N_PAIR = D_RWKV // (2 * HEAD)
D_LORA = 64
D_GATE = 160
D_GATE_PAD = 384
D_FF = 2816
D_SHIFT = 1824
D_SHIFT_PAD = 2048
D_CONV3 = 1536
LOG_DECAY_SCALE = 0.606531
RMS_EPS = 1e-6
GN_EPS = 64e-5
NORM_EPS = 1e-12
ADAM_LR, ADAM_B1, ADAM_B2, ADAM_EPS, ADAM_WD, ADAM_STEP = 0.001, 0.9, 0.999, 1e-08, 0.01, 10

N_SHARD = 4
N_DEV = 8
V7X_VMEM_LIMIT = 48 * 1024 * 1024
SCAN_CHUNK = 32
GROUP = 8

_PACK_ROWS = (("w_in", 840), ("w_out", 256), ("w_gate", 704), ("w_up", 704), ("w_down", 704),
              ("w_up_f", 8), ("w_up_b", 8), ("a_up_f", 8), ("a_up_b", 8), ("g_up", 20), ("conv_w", 1))
PACK_R = 3264
PACK_H = PACK_R // 2
SMALL_ROWS = 24


def _tile(n, cap, mult=128):
    best = None
    t = mult
    while t <= min(n, cap):
        if n % t == 0:
            best = t
        t += mult
    return best or n


def _cp(*sem):
    return pltpu.CompilerParams(dimension_semantics=sem or None, vmem_limit_bytes=V7X_VMEM_LIMIT)


def _sds(shape, dtype=F32):
    return jax.ShapeDtypeStruct(shape, dtype)


def _matmul(a, b, *, mode, name, out_dtype=F32, add=None):
    if mode == "tn":
        r, m = a.shape
        n = b.shape[1]
        tm, tn, tk = _tile(m, 512), _tile(n, 1536), _tile(r, 512, 8)
        nk = r // tk
        a_spec = pl.BlockSpec((tk, tm), lambda i, j, k: (k, i))
        b_spec = pl.BlockSpec((tk, tn), lambda i, j, k: (k, j))
        dims = (((0,), (0,)), ((), ()))
    else:
        m, kdim = a.shape
        n = b.shape[1] if mode == "nn" else b.shape[0]
        tm, tn, nk = _tile(m, 512, 8), _tile(n, 1536), 1
        a_spec = pl.BlockSpec((tm, kdim), lambda i, j, k: (i, 0))
        if mode == "nn":
            b_spec = pl.BlockSpec((kdim, tn), lambda i, j, k: (0, j))
            dims = (((1,), (0,)), ((), ()))
        else:
            b_spec = pl.BlockSpec((tn, kdim), lambda i, j, k: (j, 0))
            dims = (((1,), (1,)), ((), ()))
    has_add = add is not None

    def body(*refs):
        a_ref, b_ref = refs[0], refs[1]
        add_ref = refs[2] if has_add else None
        o_ref = refs[3] if has_add else refs[2]
        part = lax.dot_general(a_ref[...].astype(BF16), b_ref[...].astype(BF16), dims,
                               preferred_element_type=F32)
        if nk == 1:
            if has_add:
                part = part + add_ref[...]
            o_ref[...] = part.astype(out_dtype)
        else:
            acc_ref = refs[-1]
            k = pl.program_id(2)

            @pl.when(k == 0)
            def _():
                acc_ref[...] = jnp.zeros_like(acc_ref)

            acc_ref[...] += part

            @pl.when(k == nk - 1)
            def _():
                res = acc_ref[...]
                if has_add:
                    res = res + add_ref[...]
                o_ref[...] = res.astype(out_dtype)

    o_spec = pl.BlockSpec((tm, tn), lambda i, j, k: (i, j))
    in_specs = [a_spec, b_spec] + ([o_spec] if has_add else [])
    args = (a, b) + ((add,) if has_add else ())
    return pl.pallas_call(
        body, name=name, grid=(m // tm, n // tn, nk), in_specs=in_specs, out_specs=o_spec,
        out_shape=_sds((m, n), out_dtype),
        scratch_shapes=[pltpu.VMEM((tm, tn), F32)] if nk > 1 else [],
        compiler_params=_cp("parallel", "parallel", "arbitrary"),
    )(*args)


def _row(tm, width):
    return pl.BlockSpec((tm, width), lambda i: (i, 0))


def _fixed(shape):
    return pl.BlockSpec(shape, lambda i: tuple(0 for _ in shape))


def _rmsnorm_fwd(x, w, tm, name):
    t, d = x.shape

    def body(x_ref, w_ref, o_ref):
        xv = x_ref[...]
        rstd = lax.rsqrt(jnp.mean(xv * xv, axis=-1, keepdims=True) + RMS_EPS)
        o_ref[...] = (xv * rstd * w_ref[...]).astype(BF16)

    return pl.pallas_call(
        body, name=name, grid=(t // tm,), in_specs=[_row(tm, d), _fixed((1, d))], out_specs=_row(tm, d),
        out_shape=_sds((t, d), BF16), compiler_params=_cp("parallel"))(x, w)


def _rms_bwd_math(xv, wv, dyv):
    rstd = lax.rsqrt(jnp.mean(xv * xv, axis=-1, keepdims=True) + RMS_EPS)
    xhat = xv * rstd
    gv = dyv * wv
    dx = rstd * (gv - xhat * jnp.mean(gv * xhat, axis=-1, keepdims=True))
    return dx, jnp.sum(dyv * xhat, axis=0, keepdims=True)


def _rmsnorm_bwd(x, w, dy, dres, tm, name):
    t, d = x.shape

    def body(x_ref, w_ref, dy_ref, dres_ref, dx_ref, dw_ref):
        dx, dw = _rms_bwd_math(x_ref[...], w_ref[...], dy_ref[...])
        dx_ref[...] = dres_ref[...] + dx

        @pl.when(pl.program_id(0) == 0)
        def _():
            dw_ref[...] = jnp.zeros_like(dw_ref)

        dw_ref[...] += dw

    return pl.pallas_call(
        body, name=name, grid=(t // tm,),
        in_specs=[_row(tm, d), _fixed((1, d)), _row(tm, d), _row(tm, d)],
        out_specs=[_row(tm, d), _fixed((1, d))],
        out_shape=[_sds((t, d)), _sds((1, d))], compiler_params=_cp("arbitrary"))(x, w, dy, dres)


def _loss_head(x, w, target, tm):
    t, d = x.shape

    def body(x_ref, w_ref, t_ref, dx_ref, dw_ref, loss_ref):
        xv, wv = x_ref[...], w_ref[...]
        rstd = lax.rsqrt(jnp.mean(xv * xv, axis=-1, keepdims=True) + RMS_EPS)
        err = xv * rstd * wv - t_ref[...]
        dx, dw = _rms_bwd_math(xv, wv, err * (1.0 / d))
        dx_ref[...] = dx

        @pl.when(pl.program_id(0) == 0)
        def _():
            dw_ref[...] = jnp.zeros_like(dw_ref)
            loss_ref[...] = jnp.zeros_like(loss_ref)

        dw_ref[...] += dw
        loss_ref[...] += 0.5 * jnp.sum(jnp.mean(err * err, axis=-1, keepdims=True), axis=0, keepdims=True)

    return pl.pallas_call(
        body, name="loss_head", grid=(t // tm,),
        in_specs=[_row(tm, d), _fixed((1, d)), _row(tm, d)],
        out_specs=[_row(tm, d), _fixed((1, d)), _fixed((1, 1))],
        out_shape=[_sds((t, d)), _sds((1, d)), _sds((1, 1))], compiler_params=_cp("arbitrary"))(x, w, target)


def _swiglu_fwd(gate, up, tm):
    t, f = gate.shape

    def body(g_ref, u_ref, o_ref):
        gv = g_ref[...]
        o_ref[...] = (gv * jax.nn.sigmoid(gv) * u_ref[...]).astype(BF16)

    return pl.pallas_call(
        body, name="swiglu_fwd", grid=(t // tm,), in_specs=[_row(tm, f), _row(tm, f)], out_specs=_row(tm, f),
        out_shape=_sds((t, f), BF16), compiler_params=_cp("parallel"))(gate, up)


def _swiglu_bwd(gate, up, dact, tm):
    t, f = gate.shape

    def body(g_ref, u_ref, d_ref, dg_ref, du_ref):
        gv, uv, dv = g_ref[...], u_ref[...], d_ref[...]
        sg = jax.nn.sigmoid(gv)
        du_ref[...] = (dv * gv * sg).astype(BF16)
        dg_ref[...] = (dv * uv * (sg * (1.0 + gv * (1.0 - sg)))).astype(BF16)

    return pl.pallas_call(
        body, name="swiglu_bwd", grid=(t // tm,), in_specs=[_row(tm, f)] * 3, out_specs=[_row(tm, f)] * 2,
        out_shape=[_sds((t, f), BF16)] * 2, compiler_params=_cp("parallel"))(gate, up, dact)


def _halo_specs(tm, width, rows_total):
    per = tm // GROUP
    last = rows_total // GROUP - 1
    prev = pl.BlockSpec((GROUP, width), lambda i: (jnp.maximum(i * per - 1, 0), 0))
    nxt = pl.BlockSpec((GROUP, width), lambda i: (jnp.minimum((i + 1) * per, last), 0))
    return prev, nxt


def _edge_flags(tm, seq):
    i = pl.program_id(0)
    has_prev = jnp.where((i * tm) % seq == 0, 0.0, 1.0).astype(F32)
    has_next = jnp.where(((i + 1) * tm) % seq == 0, 0.0, 1.0).astype(F32)
    return has_prev, has_next


def _shifted(xv, prev_row, next_row):
    tm = xv.shape[0]
    row = lax.broadcasted_iota(jnp.int32, xv.shape, 0)
    down = jnp.where(row == 0, prev_row, pltpu.roll(xv, 1, axis=0))
    up = jnp.where(row == tm - 1, next_row, pltpu.roll(xv, tm - 1, axis=0))
    return down, up


def _shift_fwd(p, mu, tm, seq):
    t, w = p.shape
    prev_spec, next_spec = _halo_specs(tm, w, t)

    def body(p_ref, hp_ref, hn_ref, mu_ref, o_ref):
        has_prev, has_next = _edge_flags(tm, seq)
        xv = p_ref[...]
        down, up = _shifted(xv, hp_ref[GROUP - 1:GROUP, :] * has_prev, hn_ref[0:1, :] * has_next)
        o_ref[...] = xv + mu_ref[0:1, :] * (0.5 * (down + up) - xv)

    return pl.pallas_call(
        body, name="shift_fwd", grid=(t // tm,),
        in_specs=[_row(tm, w), prev_spec, next_spec, _fixed((GROUP, w))], out_specs=_row(tm, w),
        out_shape=_sds((t, w)), compiler_params=_cp("parallel"))(p, p, p, mu)


def _shift_bwd(q, p, mu, tm, seq):
    t, w = p.shape
    prev_spec, next_spec = _halo_specs(tm, w, t)

    def body(q_ref, qp_ref, qn_ref, p_ref, pp_ref, pn_ref, mu_ref, dp_ref, dmu_ref):
        has_prev, has_next = _edge_flags(tm, seq)
        muv = mu_ref[0:1, :]
        qv = q_ref[...]
        mq = muv * qv
        mq_down, mq_up = _shifted(mq, muv * qp_ref[GROUP - 1:GROUP, :] * has_prev,
                                  muv * qn_ref[0:1, :] * has_next)
        dp_ref[...] = (qv - mq + 0.5 * (mq_down + mq_up)).astype(BF16)
        pv = p_ref[...]
        p_down, p_up = _shifted(pv, pp_ref[GROUP - 1:GROUP, :] * has_prev, pn_ref[0:1, :] * has_next)

        @pl.when(pl.program_id(0) == 0)
        def _():
            dmu_ref[...] = jnp.zeros_like(dmu_ref)

        dmu_ref[...] += jnp.sum(qv * (0.5 * (p_down + p_up) - pv), axis=0, keepdims=True)

    return pl.pallas_call(
        body, name="shift_bwd", grid=(t // tm,),
        in_specs=[_row(tm, w), prev_spec, next_spec, _row(tm, w), prev_spec, next_spec, _fixed((GROUP, w))],
        out_specs=[_row(tm, w), _fixed((1, w))],
        out_shape=[_sds((t, w), BF16), _sds((1, w))], compiler_params=_cp("arbitrary"))(q, q, q, p, p, p, mu)


@jax.custom_vjp
def _bdot(a, b):
    return jnp.dot(a.astype(BF16), b.astype(BF16), preferred_element_type=F32)


def _bdot_fwd(a, b):
    return _bdot(a, b), (a, b)


def _bdot_bwd(res, g):
    a, b = res
    gb = g.astype(BF16)
    da = lax.dot_general(gb, b.astype(BF16), (((1,), (1,)), ((), ())), preferred_element_type=F32)
    db = lax.dot_general(a.astype(BF16), gb, (((0,), (0,)), ((), ())), preferred_element_type=F32)
    return da, db


_bdot.defvjp(_bdot_fwd, _bdot_bwd)


def _seg_raw(x, ones_blocks):
    hi = x.astype(BF16)
    lo = (x - hi.astype(F32)).astype(BF16)
    return (jnp.dot(hi, ones_blocks, preferred_element_type=F32)
            + jnp.dot(lo, ones_blocks, preferred_element_type=F32))


@jax.custom_vjp
def _seg(x, ones_blocks):
    return _seg_raw(x, ones_blocks)


def _seg_fwd(x, ones_blocks):
    return _seg_raw(x, ones_blocks), ones_blocks


def _seg_bwd(ones_blocks, g):
    return _seg_raw(g, ones_blocks), jnp.zeros_like(ones_blocks)


_seg.defvjp(_seg_fwd, _seg_bwd)


def _head_ones():
    h = jnp.arange(D_RWKV) // HEAD
    return (h[:, None] == h[None, :]).astype(BF16)


def _prep_math(ps, k_k, w0_f, a0_f, k_a_f, w0_b, a0_b, k_a_b, wup_f, aup_f, wup_b, aup_b, gup, ones_blocks):
    r = ps[:, 0:512]
    k = ps[:, 512:1024]
    v = ps[:, 1024:1536]
    xwa = ps[:, 1536:1664]
    xg = ps[:, 1664:D_SHIFT_PAD]
    kk_raw = k * k_k
    norm = jnp.sqrt(_seg(kk_raw * kk_raw, ones_blocks))
    kk = kk_raw / jnp.maximum(norm, NORM_EPS)
    t_xwa = jnp.tanh(xwa)
    outs = [r, v, kk]
    for w0, a0, k_a, wup, aup in ((w0_f, a0_f, k_a_f, wup_f, aup_f), (w0_b, a0_b, k_a_b, wup_b, aup_b)):
        decay = jnp.exp(-LOG_DECAY_SCALE * jax.nn.sigmoid(w0 + _bdot(t_xwa, wup)))
        rate = jax.nn.sigmoid(a0 + _bdot(xwa, aup))
        outs += [decay, k * (1.0 + (rate - 1.0) * k_a), kk * rate]
    outs.append(_bdot(jax.nn.sigmoid(xg), gup))
    return tuple(outs)


def _prep_args(tm, ps_ref, pv_ref, mat_refs, ones_ref):
    vecs = [jnp.broadcast_to(pv_ref[j:j + 1, :], (tm, D_RWKV)) for j in range(7)]
    return [ps_ref[...]] + vecs + [m[...] for m in mat_refs] + [ones_ref[...]]


_PREP_MAT_SHAPES = ((128, D_RWKV),) * 4 + ((D_GATE_PAD, D_RWKV),)


def _prep_fwd(ps, pvec, mats, ones_blocks, tm):
    t = ps.shape[0]

    def body(ps_ref, pv_ref, m0, m1, m2, m3, m4, ones_ref, *out_refs):
        outs = _prep_math(*_prep_args(tm, ps_ref, pv_ref, (m0, m1, m2, m3, m4), ones_ref))
        for o_ref, val in zip(out_refs, outs[2:]):
            o_ref[...] = val

    return pl.pallas_call(
        body, name="prep_fwd", grid=(t // tm,),
        in_specs=[_row(tm, D_SHIFT_PAD), _fixed((8, D_RWKV))] + [_fixed(s) for s in _PREP_MAT_SHAPES]
        + [_fixed((D_RWKV, D_RWKV))],
        out_specs=[_row(tm, D_RWKV)] * 8, out_shape=[_sds((t, D_RWKV))] * 8,
        compiler_params=_cp("parallel"))(ps, pvec, *mats, ones_blocks)


def _prep_bwd(ps, pvec, mats, ones_blocks, cts, tm):
    t = ps.shape[0]
    counts = [len(c) for c in cts]
    flat = [a for c in cts for a in c]

    def body(ps_ref, pv_ref, m0, m1, m2, m3, m4, ones_ref, *refs):
        ct_refs = refs[:len(flat)]
        q_ref, dpv_ref = refs[len(flat)], refs[len(flat) + 1]
        dmat_refs = refs[len(flat) + 2:]
        args = _prep_args(tm, ps_ref, pv_ref, (m0, m1, m2, m3, m4), ones_ref)
        _, vjp = jax.vjp(lambda *a: _prep_math(*a, args[-1]), *args[:-1])
        ct_vals, pos = [], 0
        for n in counts:
            val = ct_refs[pos][...]
            for extra in ct_refs[pos + 1:pos + n]:
                val = val + extra[...]
            ct_vals.append(val)
            pos += n
        grads = vjp(tuple(ct_vals))
        q_ref[...] = grads[0]

        @pl.when(pl.program_id(0) == 0)
        def _():
            dpv_ref[...] = jnp.zeros_like(dpv_ref)
            for d_ref in dmat_refs:
                d_ref[...] = jnp.zeros_like(d_ref)

        for j in range(7):
            dpv_ref[j:j + 1, :] += jnp.sum(grads[1 + j], axis=0, keepdims=True)
        for d_ref, gm in zip(dmat_refs, grads[8:13]):
            d_ref[...] += gm

    return pl.pallas_call(
        body, name="prep_bwd", grid=(t // tm,),
        in_specs=[_row(tm, D_SHIFT_PAD), _fixed((8, D_RWKV))] + [_fixed(s) for s in _PREP_MAT_SHAPES]
        + [_fixed((D_RWKV, D_RWKV))] + [_row(tm, D_RWKV)] * len(flat),
        out_specs=[_row(tm, D_SHIFT_PAD), _fixed((8, D_RWKV))] + [_fixed(s) for s in _PREP_MAT_SHAPES],
        out_shape=[_sds((t, D_SHIFT_PAD)), _sds((8, D_RWKV))] + [_sds(s) for s in _PREP_MAT_SHAPES],
        compiler_params=_cp("arbitrary"))(ps, pvec, *mats, ones_blocks, *flat)


def _pair_ones():
    h = jnp.arange(2 * HEAD) // HEAD
    return (h[:, None] == h[None, :]).astype(BF16)


def _diag_mask():
    lane = lax.broadcasted_iota(jnp.int32, (HEAD, 2 * HEAD), 1)
    sub = lax.broadcasted_iota(jnp.int32, (HEAD, 2 * HEAD), 0)
    return jnp.where((lane & (HEAD - 1)) == sub, 1.0, 0.0).astype(F32)


def _to_row(cols, dmask):
    return jnp.sum(cols * dmask, axis=0, keepdims=True)


def _seg_many(exact, rounded, ones_pair):
    parts = []
    for x in exact:
        hi = x.astype(BF16)
        parts += [hi, (x - hi.astype(F32)).astype(BF16)]
    parts += [x.astype(BF16) for x in rounded]
    res = jnp.dot(jnp.concatenate(parts, axis=0), ones_pair, preferred_element_type=F32)
    n2 = 2 * len(exact)
    out_exact = [res[HEAD * (2 * c):HEAD * (2 * c + 1)] + res[HEAD * (2 * c + 1):HEAD * (2 * c + 2)]
                 for c in range(len(exact))]
    out_rounded = [res[HEAD * (n2 + c):HEAD * (n2 + c + 1)] for c in range(len(rounded))]
    return out_exact, out_rounded


N_CHAIN = 2 * N_PAIR


def _chain(c):
    d, p = divmod(c, N_PAIR)
    return d, slice(2 * HEAD * p, 2 * HEAD * (p + 1))


def _scan_specs(n_chunks, col_blocks, fwd_chunk, bwd_chunk):
    def spec(chunk_of, col):
        return pl.BlockSpec((SCAN_CHUNK, D_RWKV), lambda b, g: (b * n_chunks + chunk_of(g), col))
    return [spec(fwd_chunk, c) for c in col_blocks] + [spec(bwd_chunk, c) for c in col_blocks]


def _scan_fwd(ps, kk, dirs, batch, seq):
    t = batch * seq
    n = seq // SCAN_CHUNK
    groups = SCAN_CHUNK // GROUP
    up = lambda g: g
    down = lambda g: n - 1 - g
    col_blocks = (0, 2, 0, 0, 0, 0)

    def body(*refs):
        dir_refs = (refs[0:6], refs[6:12])
        ones_ref = refs[12]
        y_refs, hist_refs, st_ref = refs[13:15], refs[15:17], refs[17]

        @pl.when(pl.program_id(1) == 0)
        def _():
            st_ref[...] = jnp.zeros_like(st_ref)

        ones_pair = ones_ref[...]
        dmask = _diag_mask()
        sub8 = lax.broadcasted_iota(jnp.int32, (GROUP, 2 * HEAD), 0)

        def group(gi, carry):
            off = (pl.multiple_of(gi * GROUP, GROUP), pl.multiple_of((groups - 1 - gi) * GROUP, GROUP))
            loaded = [tuple(ref[pl.ds(off[d], GROUP), :] for ref in dir_refs[d]) for d in range(2)]
            states = list(carry)
            y_acc = [jnp.zeros((GROUP, 2 * HEAD), F32) for _ in range(N_CHAIN)]
            for step in range(GROUP):
                rows, idx = [], []
                for c in range(N_CHAIN):
                    d, lanes = _chain(c)
                    i = step if d == 0 else GROUP - 1 - step
                    idx.append(i)
                    rows.append(tuple(x8[i:i + 1, lanes] for x8 in loaded[d]))
                    hist_refs[d][c % N_PAIR, gi * GROUP + step] = states[c]
                _, v_cols = _seg_many([], [dmask * rows[c][1] for c in range(N_CHAIN)], ones_pair)
                sas, _ = _seg_many([states[c] * rows[c][2] for c in range(N_CHAIN)], [], ones_pair)
                for c in range(N_CHAIN):
                    _, _, _, w_row, kd_row, b_row = rows[c]
                    states[c] = states[c] * w_row - sas[c] * b_row + v_cols[c] * kd_row
                _, ys = _seg_many([], [states[c] * rows[c][0] for c in range(N_CHAIN)], ones_pair)
                for c in range(N_CHAIN):
                    y_acc[c] = jnp.where(sub8 == idx[c], _to_row(ys[c], dmask), y_acc[c])
            for c in range(N_CHAIN):
                d, lanes = _chain(c)
                y_refs[d][pl.ds(off[d], GROUP), lanes] = y_acc[c]
            return tuple(states)

        final = lax.fori_loop(0, groups, group, tuple(st_ref[c] for c in range(N_CHAIN)))
        for c in range(N_CHAIN):
            st_ref[c] = final[c]
            hist_refs[c // N_PAIR][c % N_PAIR, SCAN_CHUNK] = final[c]

    y_spec_f = pl.BlockSpec((SCAN_CHUNK, D_RWKV), lambda b, g: (b * n + up(g), 0))
    y_spec_b = pl.BlockSpec((SCAN_CHUNK, D_RWKV), lambda b, g: (b * n + down(g), 0))
    hist_shape = (batch, n, N_PAIR, SCAN_CHUNK + 1, HEAD, 2 * HEAD)
    hist_block = (None, None, N_PAIR, SCAN_CHUNK + 1, HEAD, 2 * HEAD)
    hist_spec_f = pl.BlockSpec(hist_block, lambda b, g: (b, up(g), 0, 0, 0, 0))
    hist_spec_b = pl.BlockSpec(hist_block, lambda b, g: (b, down(g), 0, 0, 0, 0))
    ones_spec = pl.BlockSpec((2 * HEAD, 2 * HEAD), lambda b, g: (0, 0))
    (wf, kdf, bf), (wb, kdb, bb) = dirs
    return pl.pallas_call(
        body, name="wkv_fwd", grid=(batch, n),
        in_specs=_scan_specs(n, col_blocks, up, down) + [ones_spec],
        out_specs=[y_spec_f, y_spec_b, hist_spec_f, hist_spec_b],
        out_shape=[_sds((t, D_RWKV)), _sds((t, D_RWKV)), _sds(hist_shape), _sds(hist_shape)],
        scratch_shapes=[pltpu.VMEM((N_CHAIN, HEAD, 2 * HEAD), F32)],
        compiler_params=_cp("parallel", "arbitrary"),
    )(ps, ps, kk, wf, kdf, bf, ps, ps, kk, wb, kdb, bb, _pair_ones())


def _scan_bwd(ps, kk, dirs, dy, hist_f, hist_b, batch, seq):
    t = batch * seq
    n = seq // SCAN_CHUNK
    groups = SCAN_CHUNK // GROUP
    fwd_chunk = lambda g: n - 1 - g
    bwd_chunk = lambda g: g
    col_blocks = (0, 2, 0, 0, 0, 0, 0)

    def undo_group(dir_refs, out_refs, hist_refs, gi, d_states, ones_pair, dmask, sub8):
        d_states = list(d_states)
        loaded, blocks = [], []
        for d in range(2):
            blk = groups - 1 - gi if d == 0 else gi
            blocks.append(pl.ds(pl.multiple_of(blk * GROUP, GROUP), GROUP))
            loaded.append(tuple(ref[blocks[d], :] for ref in dir_refs[d]))
        acc = [[jnp.zeros((GROUP, 2 * HEAD), F32) for _ in range(6)] for _ in range(N_CHAIN)]
        for step in range(GROUP):
            rows, idx, before, after = [], [], [], []
            for c in range(N_CHAIN):
                d, lanes = _chain(c)
                i = GROUP - 1 - step if d == 0 else step
                q = (groups - 1 - gi) * GROUP + i if d == 0 else SCAN_CHUNK - 1 - (gi * GROUP + i)
                idx.append(i)
                rows.append(tuple(x8[i:i + 1, lanes] for x8 in loaded[d]))
                before.append(hist_refs[d][c % N_PAIR, q])
                after.append(hist_refs[d][c % N_PAIR, q + 1])
            _, cols = _seg_many([], [dmask * rows[c][j] for c in range(N_CHAIN) for j in (1, 6)], ones_pair)
            v_cols, dy_cols = cols[0::2], cols[1::2]
            d_now = [d_states[c] + dy_cols[c] * rows[c][0] for c in range(N_CHAIN)]
            d_sas, _ = _seg_many([d_now[c] * rows[c][5] for c in range(N_CHAIN)], [], ones_pair)
            _, others = _seg_many(
                [], [x for c in range(N_CHAIN) for x in (before[c] * rows[c][2], d_now[c] * rows[c][4])], ones_pair)
            for c in range(N_CHAIN):
                sa, d_sa, dv_cols = others[2 * c], -d_sas[c], others[2 * c + 1]
                rows_out = (
                    jnp.sum(after[c] * dy_cols[c], axis=0, keepdims=True),
                    jnp.sum(d_now[c] * before[c], axis=0, keepdims=True),
                    jnp.sum(d_now[c] * v_cols[c], axis=0, keepdims=True),
                    _to_row(dv_cols, dmask),
                    jnp.sum(before[c] * d_sa, axis=0, keepdims=True),
                    -jnp.sum(d_now[c] * sa, axis=0, keepdims=True),
                )
                acc[c] = [jnp.where(sub8 == idx[c], val, a) for val, a in zip(rows_out, acc[c])]
                d_states[c] = d_now[c] * rows[c][3] + d_sa * rows[c][2]
        for c in range(N_CHAIN):
            d, lanes = _chain(c)
            for o_ref, val in zip(out_refs[d], acc[c]):
                o_ref[blocks[d], lanes] = val
        return tuple(d_states)

    def body(*refs):
        dir_refs = (refs[0:7], refs[7:14])
        hist_refs, ones_ref = refs[14:16], refs[16]
        out_refs = (refs[17:23], refs[23:29])
        dst_ref = refs[29]

        @pl.when(pl.program_id(1) == 0)
        def _():
            dst_ref[...] = jnp.zeros_like(dst_ref)

        ones_pair = ones_ref[...]
        dmask = _diag_mask()
        sub8 = lax.broadcasted_iota(jnp.int32, (GROUP, 2 * HEAD), 0)

        def group(gi, carry):
            return undo_group(dir_refs, out_refs, hist_refs, gi, carry, ones_pair, dmask, sub8)

        final = lax.fori_loop(0, groups, group, tuple(dst_ref[c] for c in range(N_CHAIN)))
        for c in range(N_CHAIN):
            dst_ref[c] = final[c]

    blk = (SCAN_CHUNK, D_RWKV)
    out_f = pl.BlockSpec(blk, lambda b, g: (b * n + fwd_chunk(g), 0))
    out_b = pl.BlockSpec(blk, lambda b, g: (b * n + bwd_chunk(g), 0))
    hist_block = (None, None, N_PAIR, SCAN_CHUNK + 1, HEAD, 2 * HEAD)
    hist_spec_f = pl.BlockSpec(hist_block, lambda b, g: (b, fwd_chunk(g), 0, 0, 0, 0))
    hist_spec_b = pl.BlockSpec(hist_block, lambda b, g: (b, bwd_chunk(g), 0, 0, 0, 0))
    ones_spec = pl.BlockSpec((2 * HEAD, 2 * HEAD), lambda b, g: (0, 0))
    (wf, kdf, bf), (wb, kdb, bb) = dirs
    outs = pl.pallas_call(
        body, name="wkv_bwd", grid=(batch, n),
        in_specs=_scan_specs(n, col_blocks, fwd_chunk, bwd_chunk) + [hist_spec_f, hist_spec_b, ones_spec],
        out_specs=[out_f] * 6 + [out_b] * 6,
        out_shape=[_sds((t, D_RWKV))] * 12,
        scratch_shapes=[pltpu.VMEM((N_CHAIN, HEAD, 2 * HEAD), F32)],
        compiler_params=_cp("parallel", "arbitrary"),
    )(ps, ps, kk, wf, kdf, bf, dy, ps, ps, kk, wb, kdb, bb, dy, hist_f, hist_b, _pair_ones())
    return outs[0:6], outs[6:12]


def _post_math(y, r, kd_f, kd_b, v, gate, gn_w, gn_b, rk_f, rk_b, ones_blocks):
    mean = _seg(y, ones_blocks) * (1.0 / HEAD)
    yc = y - mean
    var = _seg(yc * yc, ones_blocks) * (1.0 / HEAD)
    yn = yc * lax.rsqrt(var + GN_EPS) * gn_w + gn_b
    bonus = _seg(r * kd_f * rk_f, ones_blocks) * v + _seg(r * kd_b * rk_b, ones_blocks) * v
    return (yn + bonus) * gate


def _conv_parts(pc, halo_prev, halo_next, has_prev, has_next):
    gate_b, gate_c, hid = pc[:, 0:512], pc[:, 512:1024], pc[:, 1024:1536]
    u = gate_c * hid
    u_prev_row = halo_prev[GROUP - 1:GROUP, 512:1024] * halo_prev[GROUP - 1:GROUP, 1024:1536] * has_prev
    u_next_row = halo_next[0:1, 512:1024] * halo_next[0:1, 1024:1536] * has_next
    u_down, u_up = _shifted(u, u_prev_row, u_next_row)
    return gate_b, gate_c, hid, u, u_down, u_up


def _post_specs(tm, t):
    pc_prev, pc_next = _halo_specs(tm, D_CONV3, t)
    col = lambda c: pl.BlockSpec((tm, D_RWKV), lambda i: (i, c))
    return ([col(0), col(0), col(0), col(0), col(0), col(2), col(0), _row(tm, D_CONV3), pc_prev, pc_next,
             _fixed((8, D_RWKV)), _fixed((D_RWKV, D_RWKV))])


def _post_fwd(y_f, y_b, ps, kd_f, kd_b, gate, pc, qvec, ones_blocks, tm, seq):
    t = ps.shape[0]

    def body(yf_ref, yb_ref, r_ref, kdf_ref, kdb_ref, v_ref, g_ref, pc_ref, hp_ref, hn_ref, qv_ref, ones_ref,
             o_ref):
        has_prev, has_next = _edge_flags(tm, seq)
        vec = [jnp.broadcast_to(qv_ref[j:j + 1, :], (tm, D_RWKV)) for j in range(7)]
        o_ref[:, 0:D_RWKV] = _post_math(yf_ref[...] + yb_ref[...], r_ref[...], kdf_ref[...], kdb_ref[...],
                                        v_ref[...], g_ref[...], vec[0], vec[1], vec[2], vec[3],
                                        ones_ref[...]).astype(BF16)
        gate_b, _, _, u, u_down, u_up = _conv_parts(pc_ref[...], hp_ref[...], hn_ref[...], has_prev, has_next)
        o_ref[:, D_RWKV:2 * D_RWKV] = (gate_b * (vec[4] * u_down + vec[5] * u + vec[6] * u_up)).astype(BF16)

    return pl.pallas_call(
        body, name="post_fwd", grid=(t // tm,), in_specs=_post_specs(tm, t), out_specs=_row(tm, D_MODEL),
        out_shape=_sds((t, D_MODEL), BF16), compiler_params=_cp("parallel"),
    )(y_f, y_b, ps, kd_f, kd_b, ps, gate, pc, pc, pc, qvec, ones_blocks)


def _post_bwd(d_out, y_f, y_b, ps, kd_f, kd_b, gate, pc, qvec, ones_blocks, tm, seq):
    t = ps.shape[0]
    do_prev, do_next = _halo_specs(tm, D_MODEL, t)

    def body(do_ref, dop_ref, don_ref, yf_ref, yb_ref, r_ref, kdf_ref, kdb_ref, v_ref, g_ref, pc_ref, hp_ref,
             hn_ref, qv_ref, ones_ref, dy_ref, dr_ref, dkdf_ref, dkdb_ref, dv_ref, dg_ref, dpc_ref, dqv_ref):
        has_prev, has_next = _edge_flags(tm, seq)
        vec = [jnp.broadcast_to(qv_ref[j:j + 1, :], (tm, D_RWKV)) for j in range(7)]
        ones_v = ones_ref[...]
        args = (yf_ref[...] + yb_ref[...], r_ref[...], kdf_ref[...], kdb_ref[...], v_ref[...], g_ref[...],
                vec[0], vec[1], vec[2], vec[3])
        _, vjp = jax.vjp(lambda *a: _post_math(*a, ones_v), *args)
        grads = vjp(do_ref[:, 0:D_RWKV])
        for o_ref, gval in zip((dy_ref, dr_ref, dkdf_ref, dkdb_ref, dv_ref, dg_ref), grads[0:6]):
            o_ref[...] = gval

        hp, hn = hp_ref[...], hn_ref[...]
        gate_b, gate_c, hid, u, u_down, u_up = _conv_parts(pc_ref[...], hp, hn, has_prev, has_next)
        d_oc = do_ref[:, D_RWKV:2 * D_RWKV]
        d_cu = d_oc * gate_b
        d_cu_prev = dop_ref[GROUP - 1:GROUP, D_RWKV:2 * D_RWKV] * hp[GROUP - 1:GROUP, 0:512] * has_prev
        d_cu_next = don_ref[0:1, D_RWKV:2 * D_RWKV] * hn[0:1, 0:512] * has_next
        d_cu_down, d_cu_up = _shifted(d_cu, d_cu_prev, d_cu_next)
        d_u = vec[5] * d_cu + vec[4] * d_cu_up + vec[6] * d_cu_down
        dpc_ref[:, 0:512] = (d_oc * (vec[4] * u_down + vec[5] * u + vec[6] * u_up)).astype(BF16)
        dpc_ref[:, 512:1024] = (d_u * hid).astype(BF16)
        dpc_ref[:, 1024:1536] = (d_u * gate_c).astype(BF16)

        @pl.when(pl.program_id(0) == 0)
        def _():
            dqv_ref[...] = jnp.zeros_like(dqv_ref)

        vec_grads = list(grads[6:10]) + [d_cu * u_down, d_cu * u, d_cu * u_up]
        for j, gval in enumerate(vec_grads):
            dqv_ref[j:j + 1, :] += jnp.sum(gval, axis=0, keepdims=True)

    return pl.pallas_call(
        body, name="post_bwd", grid=(t // tm,),
        in_specs=[_row(tm, D_MODEL), do_prev, do_next] + _post_specs(tm, t),
        out_specs=[_row(tm, D_RWKV)] * 6 + [_row(tm, D_CONV3), _fixed((8, D_RWKV))],
        out_shape=[_sds((t, D_RWKV))] * 6 + [_sds((t, D_CONV3), BF16), _sds((8, D_RWKV))],
        compiler_params=_cp("arbitrary"),
    )(d_out, d_out, d_out, y_f, y_b, ps, kd_f, kd_b, ps, gate, pc, pc, pc, qvec, ones_blocks)


def _adamw(w, g, m, v, name):
    r, c = w.shape
    tr = _tile(r, 256, 8)
    spec = pl.BlockSpec((tr, c), lambda i: (i, 0))

    def body(w_ref, g_ref, m_ref, v_ref, d_ref, nm_ref, nv_ref):
        gv = g_ref[...]
        m2 = ADAM_B1 * m_ref[...] + (1.0 - ADAM_B1) * gv
        v2 = ADAM_B2 * v_ref[...] + (1.0 - ADAM_B2) * (gv * gv)
        m_hat = m2 / (1.0 - ADAM_B1 ** ADAM_STEP)
        v_hat = v2 / (1.0 - ADAM_B2 ** ADAM_STEP)
        d_ref[...] = -ADAM_LR * (m_hat / (jnp.sqrt(v_hat) + ADAM_EPS) + ADAM_WD * w_ref[...])
        nm_ref[...] = m2
        nv_ref[...] = v2

    return pl.pallas_call(
        body, name=name, grid=(r // tr,), in_specs=[spec] * 4, out_specs=[spec] * 3,
        out_shape=[_sds((r, c))] * 3, compiler_params=_cp("parallel"))(w, g, m, v)


def _add_n(parts, name):
    r, c = parts[0].shape
    tr = _tile(r, 408, 8)
    spec = pl.BlockSpec((tr, c), lambda i: (i, 0))

    def body(*refs):
        acc = refs[0][...]
        for ref in refs[1:-1]:
            acc = acc + ref[...]
        refs[-1][...] = acc

    return pl.pallas_call(
        body, name=name, grid=(r // tr,), in_specs=[spec] * len(parts), out_specs=spec,
        out_shape=_sds((r, c)), compiler_params=_cp("parallel"))(*parts)


_ANY = pl.BlockSpec(memory_space=pl.ANY)


def _place():
    return lax.axis_index("x"), lax.axis_index("y"), lax.axis_index("c")


def _other_chips(x, y):
    return [(1 - x, y), (x, 1 - y), (1 - x, 1 - y)]


def _remote(src, dst, send_sems, recv_sems, k, to):
    return pltpu.make_async_remote_copy(src_ref=src, dst_ref=dst, send_sem=send_sems.at[k],
                                        recv_sem=recv_sems.at[k], device_id=to, device_id_type=MESH)


def _gather_weights(pack):
    rows, width = pack.shape
    half = rows // 2

    def body(x_ref, out_ref, send_sems, recv_sems):
        x, y, c = _place()
        sibling = (x, y, 1 - c)
        chips = _other_chips(x, y)

        def block(chip, part):
            return out_ref.at[2 * chip[0] + chip[1], pl.ds(part * half, half), :]

        first = [_remote(x_ref.at[pl.ds(c * half, half), :], block((x, y), c), send_sems, recv_sems, j, (*chip, c))
                 for j, chip in enumerate(chips)]
        for cp in first:
            cp.start()
        passed = [_remote(block(chip, c), block(chip, c), send_sems, recv_sems, 3 + j, sibling)
                  for j, chip in enumerate(chips)]
        for j, chip in enumerate(chips):
            _remote(block(chip, c), block(chip, c), send_sems, recv_sems, j, sibling).wait_recv()
            passed[j].start()
        for j, chip in enumerate(chips):
            _remote(block(chip, 1 - c), block(chip, 1 - c), send_sems, recv_sems, 3 + j, sibling).wait_recv()
        for cp in first + passed:
            cp.wait_send()

    return pl.pallas_call(
        body, name="gather_weights", in_specs=[_ANY], out_specs=_ANY,
        out_shape=_sds((N_SHARD, rows, width), pack.dtype),
        scratch_shapes=[pltpu.SemaphoreType.DMA((6,)), pltpu.SemaphoreType.DMA((6,))],
    )(pack)


def _swap_with_sibling(block, name):
    def body(x_ref, out_ref, send_sems, recv_sems):
        x, y, c = _place()
        cp = _remote(x_ref, out_ref, send_sems, recv_sems, 0, (x, y, 1 - c))
        cp.start()
        cp.wait()

    return pl.pallas_call(
        body, name=name, in_specs=[_ANY], out_specs=_ANY, out_shape=_sds(block.shape, block.dtype),
        scratch_shapes=[pltpu.SemaphoreType.DMA((1,)), pltpu.SemaphoreType.DMA((1,))],
    )(block)


def _exchange_quarters(parts):
    _, rows, width = parts.shape

    def body(x_ref, out_ref, send_sems, recv_sems):
        x, y, c = _place()
        copies = [_remote(x_ref.at[2 * chip[0] + chip[1]], out_ref.at[j], send_sems, recv_sems, j, (*chip, c))
                  for j, chip in enumerate(_other_chips(x, y))]
        for cp in copies:
            cp.start()
        for cp in copies:
            cp.wait()

    return pl.pallas_call(
        body, name="exchange_quarters", in_specs=[_ANY], out_specs=_ANY,
        out_shape=_sds((3, rows, width), parts.dtype),
        scratch_shapes=[pltpu.SemaphoreType.DMA((3,)), pltpu.SemaphoreType.DMA((3,))],
    )(parts)


def _allreduce_small(vec):
    rows, width = vec.shape
    vmem = pl.BlockSpec(memory_space=pltpu.VMEM)

    def body(x_ref, o_ref, buf_ref, send_sems, recv_sems):
        x, y, c = _place()
        me = 4 * x + 2 * y + c
        buf_ref[me] = x_ref[...]
        copies = []
        for k in range(1, N_DEV):
            peer = (x ^ ((k >> 2) & 1), y ^ ((k >> 1) & 1), c ^ (k & 1))
            copies.append(_remote(x_ref, buf_ref.at[me], send_sems, recv_sems, k - 1, peer))
        for cp in copies:
            cp.start()
        for k in range(1, N_DEV):
            _remote(x_ref, buf_ref.at[me ^ k], send_sems, recv_sems, k - 1, (x, y, c)).wait_recv()
        for cp in copies:
            cp.wait_send()
        total = buf_ref[0]
        for d in range(1, N_DEV):
            total = total + buf_ref[d]
        o_ref[...] = total

    return pl.pallas_call(
        body, name="allreduce_small", in_specs=[vmem], out_specs=vmem, out_shape=_sds((rows, width)),
        scratch_shapes=[pltpu.VMEM((N_DEV, rows, width), F32), pltpu.SemaphoreType.DMA((N_DEV - 1,)),
                        pltpu.SemaphoreType.DMA((N_DEV - 1,))],
    )(vec)


def _rows1024(a):
    return a.reshape(-1, 1024)


def _pad_rows(a, rows):
    return jnp.concatenate([a, jnp.zeros((rows - a.shape[0], a.shape[1]), a.dtype)], axis=0)


def _pack_weight_shards(w):
    conv_bits = lax.bitcast_convert_type(w["conv_w"], BF16).reshape(1, -1)
    conv_row = jnp.concatenate([conv_bits, jnp.zeros((1, 1024 - conv_bits.shape[1]), BF16)], axis=1)
    parts = [_rows1024(w[name].astype(BF16)) for name, _ in _PACK_ROWS[:-1]] + [conv_row]
    return _pad_rows(jnp.concatenate(parts, axis=0), PACK_R)


def _unpack_gathered(gathered):
    out, row = {}, 0
    for name, n in _PACK_ROWS:
        out[name] = gathered[:, row:row + n]
        row += n
    cols = lambda a, k: jnp.concatenate([a[s].reshape(k, -1) for s in range(N_SHARD)], axis=1)
    conv = lax.bitcast_convert_type(out["conv_w"][:, 0, :768].reshape(N_SHARD, 3, 128, 2), F32)
    return dict(
        w_in=cols(out["w_in"], D_MODEL), w_out=out["w_out"].reshape(D_MODEL, D_MODEL),
        w_gate=cols(out["w_gate"], D_MODEL), w_up=cols(out["w_up"], D_MODEL),
        w_down=out["w_down"].reshape(D_FF, D_MODEL),
        w_up_f=cols(out["w_up_f"], D_LORA), w_up_b=cols(out["w_up_b"], D_LORA),
        a_up_f=cols(out["a_up_f"], D_LORA), a_up_b=cols(out["a_up_b"], D_LORA),
        g_up=cols(out["g_up"], D_GATE), conv_w=jnp.concatenate([conv[s] for s in range(N_SHARD)], axis=1))


def _pack_grads(g):
    col_split = lambda a, s: a[:, s * (a.shape[1] // N_SHARD):(s + 1) * (a.shape[1] // N_SHARD)]
    row_split = lambda a, s: a[s * (a.shape[0] // N_SHARD):(s + 1) * (a.shape[0] // N_SHARD)]
    slots = []
    for s in range(N_SHARD):
        conv = col_split(g["conv_w"], s).reshape(1, -1)
        parts = [_rows1024(col_split(g["w_in"], s)), row_split(g["w_out"], s), _rows1024(col_split(g["w_gate"], s)),
                 _rows1024(col_split(g["w_up"], s)), row_split(g["w_down"], s)]
        parts += [_rows1024(col_split(g[name], s)) for name in ("w_up_f", "w_up_b", "a_up_f", "a_up_b", "g_up")]
        parts.append(jnp.concatenate([conv, jnp.zeros((1, 1024 - conv.shape[1]), F32)], axis=1))
        slots.append(_pad_rows(jnp.concatenate(parts, axis=0), PACK_R))
    return jnp.stack(slots)


def _unpack_grad_shard(pack):
    shapes = dict(w_in=(D_MODEL, 840), w_out=(256, D_MODEL), w_gate=(D_MODEL, 704), w_up=(D_MODEL, 704),
                  w_down=(704, D_MODEL), w_up_f=(D_LORA, 128), w_up_b=(D_LORA, 128), a_up_f=(D_LORA, 128),
                  a_up_b=(D_LORA, 128), g_up=(D_GATE, 128))
    out, row = {}, 0
    for name, n in _PACK_ROWS[:-1]:
        out[name] = pack[row:row + n].reshape(shapes[name])
        row += n
    out["conv_w"] = pack[row, :384].reshape(3, 128)
    return out


_SMALL_LAYOUT = (("norm1_w", 1024), ("mu_shift", D_SHIFT), ("w0_f", 512), ("w0_b", 512), ("a0_f", 512),
                 ("a0_b", 512), ("k_k", 512), ("k_a_f", 512), ("k_a_b", 512), ("r_k_f", 512), ("r_k_b", 512),
                 ("gn_w", 512), ("gn_b", 512), ("norm2_w", 1024), ("norm_f_w", 1024), ("loss", 1))


def _pack_small(vals):
    rows = []
    for name, n in _SMALL_LAYOUT:
        flat = vals[name].reshape(-1)
        n_rows = -(-n // 1024)
        rows.append(jnp.concatenate([flat, jnp.zeros((n_rows * 1024 - n,), F32)]).reshape(n_rows, 1024))
    return _pad_rows(jnp.concatenate(rows, axis=0), SMALL_ROWS)


def _unpack_small(pack):
    out, row = {}, 0
    for name, n in _SMALL_LAYOUT:
        n_rows = -(-n // 1024)
        out[name] = pack[row:row + n_rows].reshape(-1)[:n]
        row += n_rows
    return out


_WEIGHTS = ("norm1_w", "w_in", "mu_shift", "w_up_f", "w0_f", "w_up_b", "w0_b", "a_up_f", "a0_f", "a_up_b", "a0_b",
            "g_up", "k_k", "k_a_f", "k_a_b", "r_k_f", "r_k_b", "gn_w", "gn_b", "conv_w", "w_out", "norm2_w",
            "w_gate", "w_up", "w_down", "norm_f_w")


def _train_step(x, loss_target, w, m, v):
    batch, seq, _ = x.shape
    t = batch * seq
    tm = _tile(seq, 256, 8)
    xs = x.reshape(t, D_MODEL)
    target = loss_target.reshape(t, D_MODEL)
    vec = lambda name: w[name].reshape(1, -1)

    local = {name: w[name][0] for name, _ in _PACK_ROWS}
    c = lax.axis_index("c")
    chip = 2 * lax.axis_index("x") + lax.axis_index("y")
    pack = _pack_weight_shards(local)
    full = _unpack_gathered(lax.dynamic_update_slice(_gather_weights(pack), pack[None], (chip, 0, 0)))
    w_in = full["w_in"]
    w_shift = jnp.concatenate([w_in[:, :D_SHIFT], jnp.zeros((D_MODEL, D_SHIFT_PAD - D_SHIFT), BF16)], axis=1)
    w_conv = w_in[:, D_SHIFT:]
    zeros_lora = jnp.zeros((D_LORA, D_RWKV), F32)
    lora = lambda name: full[name].astype(F32)
    mats = (jnp.concatenate([lora("w_up_f"), zeros_lora]), jnp.concatenate([zeros_lora, lora("a_up_f")]),
            jnp.concatenate([lora("w_up_b"), zeros_lora]), jnp.concatenate([zeros_lora, lora("a_up_b")]),
            jnp.concatenate([lora("g_up"), jnp.zeros((D_GATE_PAD - D_GATE, D_RWKV), F32)]))
    mu = jnp.concatenate([vec("mu_shift"), jnp.zeros((1, D_SHIFT_PAD - D_SHIFT), F32)], axis=1)
    mu = jnp.broadcast_to(mu, (GROUP, D_SHIFT_PAD))
    zero_row = jnp.zeros((1, D_RWKV), F32)
    pvec = jnp.concatenate([vec("k_k"), vec("w0_f"), vec("a0_f"), vec("k_a_f"), vec("w0_b"), vec("a0_b"),
                            vec("k_a_b"), zero_row], axis=0)
    qvec = jnp.concatenate([vec("gn_w"), vec("gn_b"), vec("r_k_f"), vec("r_k_b"), full["conv_w"], zero_row], axis=0)
    ones_blocks = _head_ones()

    h1 = _rmsnorm_fwd(xs, vec("norm1_w"), tm, "norm1_fwd")
    p_shift = _matmul(h1, w_shift, mode="nn", name="in_proj_shift")
    pc = _matmul(h1, w_conv, mode="nn", name="in_proj_conv")
    ps = _shift_fwd(p_shift, mu, tm, seq)
    kk, w_f, kd_f, b_f, w_b, kd_b, b_b, gate = _prep_fwd(ps, pvec, mats, ones_blocks, tm)
    dirs = ((w_f, kd_f, b_f), (w_b, kd_b, b_b))
    y_f, y_b, hist_f, hist_b = _scan_fwd(ps, kk, dirs, batch, seq)
    mixed = _post_fwd(y_f, y_b, ps, kd_f, kd_b, gate, pc, qvec, ones_blocks, tm, seq)
    x1 = _matmul(mixed, full["w_out"], mode="nn", name="out_proj", add=xs)
    h2 = _rmsnorm_fwd(x1, vec("norm2_w"), tm, "norm2_fwd")
    ff_gate = _matmul(h2, full["w_gate"], mode="nn", name="ffn_gate")
    ff_up = _matmul(h2, full["w_up"], mode="nn", name="ffn_up")
    act = _swiglu_fwd(ff_gate, ff_up, tm)
    x2 = _matmul(act, full["w_down"], mode="nn", name="ffn_down", add=x1)
    d_x2, d_norm_f, loss_part = _loss_head(x2, w["norm_f_w"].reshape(1, -1), target, tm)

    g = {}
    d_act = _matmul(d_x2, full["w_down"], mode="nt", name="ffn_down_dx")
    g["w_down"] = _matmul(d_x2, act, mode="tn", name="ffn_down_dw").T
    d_gate, d_up = _swiglu_bwd(ff_gate, ff_up, d_act, tm)
    d_h2 = _matmul(d_gate, full["w_gate"], mode="nt", name="ffn_gate_dx")
    d_h2 = _matmul(d_up, full["w_up"], mode="nt", name="ffn_up_dx", add=d_h2)
    g["w_gate"] = _matmul(h2, d_gate, mode="tn", name="ffn_gate_dw")
    g["w_up"] = _matmul(h2, d_up, mode="tn", name="ffn_up_dw")
    d_x1, d_norm2 = _rmsnorm_bwd(x1, vec("norm2_w"), d_h2, d_x2, tm, "norm2_bwd")
    d_mixed = _matmul(d_x1, full["w_out"], mode="nt", name="out_proj_dx")
    g["w_out"] = _matmul(mixed, d_x1, mode="tn", name="out_proj_dw")
    dy, dr_o, dkdf_o, dkdb_o, dv_o, d_gatev, d_pc, d_qvec = _post_bwd(
        d_mixed, y_f, y_b, ps, kd_f, kd_b, gate, pc, qvec, ones_blocks, tm, seq)
    (dr_f, dw_f, dkd_f, dv_f, dkk_f, db_f), (dr_b, dw_b, dkd_b, dv_b, dkk_b, db_b) = _scan_bwd(
        ps, kk, dirs, dy, hist_f, hist_b, batch, seq)
    cts = [[dr_f, dr_b, dr_o], [dv_f, dv_b, dv_o], [dkk_f, dkk_b], [dw_f], [dkd_f, dkdf_o], [db_f],
           [dw_b], [dkd_b, dkdb_o], [db_b], [d_gatev]]
    q, d_pvec, d_m0, d_m1, d_m2, d_m3, d_m4 = _prep_bwd(ps, pvec, mats, ones_blocks, cts, tm)
    d_pshift, d_mu = _shift_bwd(q, p_shift, mu, tm, seq)
    d_h1 = _matmul(d_pshift, w_shift, mode="nt", name="in_proj_shift_dx")
    d_h1 = _matmul(d_pc, w_conv, mode="nt", name="in_proj_conv_dx", add=d_h1)
    d_w_shift = _matmul(h1, d_pshift, mode="tn", name="in_proj_shift_dw")
    d_w_conv = _matmul(h1, d_pc, mode="tn", name="in_proj_conv_dw")
    g["w_in"] = jnp.concatenate([d_w_shift[:, :D_SHIFT], d_w_conv], axis=1)
    d_x, d_norm1 = _rmsnorm_bwd(xs, vec("norm1_w"), d_h1, d_x1, tm, "norm1_bwd")
    g["w_up_f"], g["a_up_f"] = d_m0[:D_LORA], d_m1[D_LORA:]
    g["w_up_b"], g["a_up_b"] = d_m2[:D_LORA], d_m3[D_LORA:]
    g["g_up"] = d_m4[:D_GATE]
    g["conv_w"] = d_qvec[4:7]

    packed = _pack_grads(g)
    keep = lax.dynamic_slice_in_dim(packed, c * PACK_H, PACK_H, axis=1)
    give = lax.dynamic_slice_in_dim(packed, (1 - c) * PACK_H, PACK_H, axis=1)
    got = _swap_with_sibling(give, "swap_halves")
    chip_sum = _add_n([keep.reshape(-1, 1024), got.reshape(-1, 1024)], "add_halves").reshape(N_SHARD, PACK_H, 1024)
    others = _exchange_quarters(chip_sum)
    own = lax.dynamic_index_in_dim(chip_sum, chip, axis=0, keepdims=False)
    eighth = _add_n([own, others[0], others[1], others[2]], "add_quarters")
    other_eighth = _swap_with_sibling(eighth, "swap_eighths")
    grads = _unpack_grad_shard(jnp.concatenate([jnp.where(c == 0, eighth, other_eighth),
                                                jnp.where(c == 0, other_eighth, eighth)], axis=0))

    small = dict(norm1_w=d_norm1, mu_shift=d_mu[:, :D_SHIFT], w0_f=d_pvec[1], w0_b=d_pvec[4], a0_f=d_pvec[2],
                 a0_b=d_pvec[5], k_k=d_pvec[0], k_a_f=d_pvec[3], k_a_b=d_pvec[6], r_k_f=d_qvec[2], r_k_b=d_qvec[3],
                 gn_w=d_qvec[0], gn_b=d_qvec[1], norm2_w=d_norm2, norm_f_w=d_norm_f, loss=loss_part)
    reduced = _unpack_small(_allreduce_small(_pack_small(small)))
    loss = reduced.pop("loss")[0]
    grads.update(reduced)

    outs = {}
    for name in _WEIGHTS:
        shape = w[name].shape
        as2d = (1, shape[0]) if len(shape) == 1 else (-1, shape[-1]) if name not in ("r_k_f", "r_k_b") else (1, -1)
        grad = grads[name].reshape(shape)
        delta, new_m, new_v = _adamw(w[name].reshape(as2d), grad.reshape(as2d), m[name].reshape(as2d),
                                     v[name].reshape(as2d), "adamw_" + name)
        outs[name] = (grad, delta.reshape(shape), new_m.reshape(shape), new_v.reshape(shape))
    d_x = d_x.reshape(batch, seq, D_MODEL)
    return (loss, d_x) + tuple(outs[name][k] for k in range(4) for name in _WEIGHTS)


def kernel(x, norm1_w, w_in, mu_shift, w_up_f, w0_f, w_up_b, w0_b, a_up_f, a0_f, a_up_b, a0_b, g_up, k_k, k_a_f, k_a_b, r_k_f, r_k_b, gn_w, gn_b, conv_w, w_out, norm2_w, w_gate, w_up, w_down, norm_f_w, loss_target, m_norm1_w, m_w_in, m_mu_shift, m_w_up_f, m_w0_f, m_w_up_b, m_w0_b, m_a_up_f, m_a0_f, m_a_up_b, m_a0_b, m_g_up, m_k_k, m_k_a_f, m_k_a_b, m_r_k_f, m_r_k_b, m_gn_w, m_gn_b, m_conv_w, m_w_out, m_norm2_w, m_w_gate, m_w_up, m_w_down, m_norm_f_w, v_norm1_w, v_w_in, v_mu_shift, v_w_up_f, v_w0_f, v_w_up_b, v_w0_b, v_a_up_f, v_a0_f, v_a_up_b, v_a0_b, v_g_up, v_k_k, v_k_a_f, v_k_a_b, v_r_k_f, v_r_k_b, v_gn_w, v_gn_b, v_conv_w, v_w_out, v_norm2_w, v_w_gate, v_w_up, v_w_down, v_norm_f_w):
    args = locals()
    w = {name: args[name] for name in _WEIGHTS}
    m = {name: args["m_" + name] for name in _WEIGHTS}
    v = {name: args["v_" + name] for name in _WEIGHTS}
    return _train_step(x, loss_target, w, m, v)
```

```python
import functools

import jax
import jax.numpy as jnp
from jax import lax
from jax.experimental import pallas as pl
from jax.experimental.pallas import tpu as pltpu

F32 = jnp.float32
BF16 = jnp.bfloat16
MESH = pl.DeviceIdType.MESH

D_MODEL = 1024
D_RWKV = 512
HEAD = 64
N_PAIR = D_RWKV // (2 * HEAD)
D_LORA = 64
D_GATE = 160
D_GATE_PAD = 384
D_FF = 2816
D_SHIFT = 1824
D_SHIFT_PAD = 2048
D_CONV3 = 1536
LOG_DECAY_SCALE = 0.606531
RMS_EPS = 1e-6
GN_EPS = 64e-5
NORM_EPS = 1e-12
ADAM_LR, ADAM_B1, ADAM_B2, ADAM_EPS, ADAM_WD, ADAM_STEP = 0.001, 0.9, 0.999, 1e-08, 0.01, 10

N_SHARD = 4
N_DEV = 8
V7X_VMEM_LIMIT = 48 * 1024 * 1024
SCAN_CHUNK = 32
GROUP = 8

_PACK_ROWS = (("w_in", 840), ("w_out", 256), ("w_gate", 704), ("w_up", 704), ("w_down", 704),
              ("w_up_f", 8), ("w_up_b", 8), ("a_up_f", 8), ("a_up_b", 8), ("g_up", 20), ("conv_w", 1))
PACK_R = 3264
PACK_H = PACK_R // 2
SMALL_ROWS = 24


def _tile(n, cap, mult=128):
    best = None
    t = mult
    while t <= min(n, cap):
        if n % t == 0:
            best = t
        t += mult
    return best or n


def _cp(*sem):
    return pltpu.CompilerParams(dimension_semantics=sem or None, vmem_limit_bytes=V7X_VMEM_LIMIT)


def _sds(shape, dtype=F32):
    return jax.ShapeDtypeStruct(shape, dtype)


def _matmul(a, b, *, mode, name, out_dtype=F32, add=None):
    if mode == "tn":
        r, m = a.shape
        n = b.shape[1]
        tm, tn, tk = _tile(m, 512), _tile(n, 1536), _tile(r, 512, 8)
        nk = r // tk
        a_spec = pl.BlockSpec((tk, tm), lambda i, j, k: (k, i))
        b_spec = pl.BlockSpec((tk, tn), lambda i, j, k: (k, j))
        dims = (((0,), (0,)), ((), ()))
    else:
        m, kdim = a.shape
        n = b.shape[1] if mode == "nn" else b.shape[0]
        tm, tn, nk = _tile(m, 512, 8), _tile(n, 1536), 1
        a_spec = pl.BlockSpec((tm, kdim), lambda i, j, k: (i, 0))
        if mode == "nn":
            b_spec = pl.BlockSpec((kdim, tn), lambda i, j, k: (0, j))
            dims = (((1,), (0,)), ((), ()))
        else:
            b_spec = pl.BlockSpec((tn, kdim), lambda i, j, k: (j, 0))
            dims = (((1,), (1,)), ((), ()))
    has_add = add is not None

    def body(*refs):
        a_ref, b_ref = refs[0], refs[1]
        add_ref = refs[2] if has_add else None
        o_ref = refs[3] if has_add else refs[2]
        part = lax.dot_general(a_ref[...].astype(BF16), b_ref[...].astype(BF16), dims,
                               preferred_element_type=F32)
        if nk == 1:
            if has_add:
                part = part + add_ref[...]
            o_ref[...] = part.astype(out_dtype)
        else:
            acc_ref = refs[-1]
            k = pl.program_id(2)

            @pl.when(k == 0)
            def _():
                acc_ref[...] = jnp.zeros_like(acc_ref)

            acc_ref[...] += part

            @pl.when(k == nk - 1)
            def _():
                res = acc_ref[...]
                if has_add:
                    res = res + add_ref[...]
                o_ref[...] = res.astype(out_dtype)

    o_spec = pl.BlockSpec((tm, tn), lambda i, j, k: (i, j))
    in_specs = [a_spec, b_spec] + ([o_spec] if has_add else [])
    args = (a, b) + ((add,) if has_add else ())
    return pl.pallas_call(
        body, name=name, grid=(m // tm, n // tn, nk), in_specs=in_specs, out_specs=o_spec,
        out_shape=_sds((m, n), out_dtype),
        scratch_shapes=[pltpu.VMEM((tm, tn), F32)] if nk > 1 else [],
        compiler_params=_cp("parallel", "parallel", "arbitrary"),
    )(*args)


def _row(tm, width):
    return pl.BlockSpec((tm, width), lambda i: (i, 0))


def _fixed(shape):
    return pl.BlockSpec(shape, lambda i: tuple(0 for _ in shape))


def _rmsnorm_fwd(x, w, tm, name):
    t, d = x.shape

    def body(x_ref, w_ref, o_ref):
        xv = x_ref[...]
        rstd = lax.rsqrt(jnp.mean(xv * xv, axis=-1, keepdims=True) + RMS_EPS)
        o_ref[...] = (xv * rstd * w_ref[...]).astype(BF16)

    return pl.pallas_call(
        body, name=name, grid=(t // tm,), in_specs=[_row(tm, d), _fixed((1, d))], out_specs=_row(tm, d),
        out_shape=_sds((t, d), BF16), compiler_params=_cp("parallel"))(x, w)


def _rms_bwd_math(xv, wv, dyv):
    rstd = lax.rsqrt(jnp.mean(xv * xv, axis=-1, keepdims=True) + RMS_EPS)
    xhat = xv * rstd
    gv = dyv * wv
    dx = rstd * (gv - xhat * jnp.mean(gv * xhat, axis=-1, keepdims=True))
    return dx, jnp.sum(dyv * xhat, axis=0, keepdims=True)


def _rmsnorm_bwd(x, w, dy, dres, tm, name):
    t, d = x.shape

    def body(x_ref, w_ref, dy_ref, dres_ref, dx_ref, dw_ref):
        dx, dw = _rms_bwd_math(x_ref[...], w_ref[...], dy_ref[...])
        dx_ref[...] = dres_ref[...] + dx

        @pl.when(pl.program_id(0) == 0)
        def _():
            dw_ref[...] = jnp.zeros_like(dw_ref)

        dw_ref[...] += dw

    return pl.pallas_call(
        body, name=name, grid=(t // tm,),
        in_specs=[_row(tm, d), _fixed((1, d)), _row(tm, d), _row(tm, d)],
        out_specs=[_row(tm, d), _fixed((1, d))],
        out_shape=[_sds((t, d)), _sds((1, d))], compiler_params=_cp("arbitrary"))(x, w, dy, dres)


def _loss_head(x, w, target, tm):
    t, d = x.shape

    def body(x_ref, w_ref, t_ref, dx_ref, dw_ref, loss_ref):
        xv, wv = x_ref[...], w_ref[...]
        rstd = lax.rsqrt(jnp.mean(xv * xv, axis=-1, keepdims=True) + RMS_EPS)
        err = xv * rstd * wv - t_ref[...]
        dx, dw = _rms_bwd_math(xv, wv, err * (1.0 / d))
        dx_ref[...] = dx

        @pl.when(pl.program_id(0) == 0)
        def _():
            dw_ref[...] = jnp.zeros_like(dw_ref)
            loss_ref[...] = jnp.zeros_like(loss_ref)

        dw_ref[...] += dw
        loss_ref[...] += 0.5 * jnp.sum(jnp.mean(err * err, axis=-1, keepdims=True), axis=0, keepdims=True)

    return pl.pallas_call(
        body, name="loss_head", grid=(t // tm,),
        in_specs=[_row(tm, d), _fixed((1, d)), _row(tm, d)],
        out_specs=[_row(tm, d), _fixed((1, d)), _fixed((1, 1))],
        out_shape=[_sds((t, d)), _sds((1, d)), _sds((1, 1))], compiler_params=_cp("arbitrary"))(x, w, target)


def _swiglu_fwd(gate, up, tm):
    t, f = gate.shape

    def body(g_ref, u_ref, o_ref):
        gv = g_ref[...]
        o_ref[...] = (gv * jax.nn.sigmoid(gv) * u_ref[...]).astype(BF16)

    return pl.pallas_call(
        body, name="swiglu_fwd", grid=(t // tm,), in_specs=[_row(tm, f), _row(tm, f)], out_specs=_row(tm, f),
        out_shape=_sds((t, f), BF16), compiler_params=_cp("parallel"))(gate, up)


def _swiglu_bwd(gate, up, dact, tm):
    t, f = gate.shape

    def body(g_ref, u_ref, d_ref, dg_ref, du_ref):
        gv, uv, dv = g_ref[...], u_ref[...], d_ref[...]
        sg = jax.nn.sigmoid(gv)
        du_ref[...] = (dv * gv * sg).astype(BF16)
        dg_ref[...] = (dv * uv * (sg * (1.0 + gv * (1.0 - sg)))).astype(BF16)

    return pl.pallas_call(
        body, name="swiglu_bwd", grid=(t // tm,), in_specs=[_row(tm, f)] * 3, out_specs=[_row(tm, f)] * 2,
        out_shape=[_sds((t, f), BF16)] * 2, compiler_params=_cp("parallel"))(gate, up, dact)


def _halo_specs(tm, width, rows_total):
    per = tm // GROUP
    last = rows_total // GROUP - 1
    prev = pl.BlockSpec((GROUP, width), lambda i: (jnp.maximum(i * per - 1, 0), 0))
    nxt = pl.BlockSpec((GROUP, width), lambda i: (jnp.minimum((i + 1) * per, last), 0))
    return prev, nxt


def _edge_flags(tm, seq):
    i = pl.program_id(0)
    has_prev = jnp.where((i * tm) % seq == 0, 0.0, 1.0).astype(F32)
    has_next = jnp.where(((i + 1) * tm) % seq == 0, 0.0, 1.0).astype(F32)
    return has_prev, has_next


def _shifted(xv, prev_row, next_row):
    tm = xv.shape[0]
    row = lax.broadcasted_iota(jnp.int32, xv.shape, 0)
    down = jnp.where(row == 0, prev_row, pltpu.roll(xv, 1, axis=0))
    up = jnp.where(row == tm - 1, next_row, pltpu.roll(xv, tm - 1, axis=0))
    return down, up


def _shift_fwd(p, mu, tm, seq):
    t, w = p.shape
    prev_spec, next_spec = _halo_specs(tm, w, t)

    def body(p_ref, hp_ref, hn_ref, mu_ref, o_ref):
        has_prev, has_next = _edge_flags(tm, seq)
        xv = p_ref[...]
        down, up = _shifted(xv, hp_ref[GROUP - 1:GROUP, :] * has_prev, hn_ref[0:1, :] * has_next)
        o_ref[...] = xv + mu_ref[0:1, :] * (0.5 * (down + up) - xv)

    return pl.pallas_call(
        body, name="shift_fwd", grid=(t // tm,),
        in_specs=[_row(tm, w), prev_spec, next_spec, _fixed((GROUP, w))], out_specs=_row(tm, w),
        out_shape=_sds((t, w)), compiler_params=_cp("parallel"))(p, p, p, mu)


def _shift_bwd(q, p, mu, tm, seq):
    t, w = p.shape
    prev_spec, next_spec = _halo_specs(tm, w, t)

    def body(q_ref, qp_ref, qn_ref, p_ref, pp_ref, pn_ref, mu_ref, dp_ref, dmu_ref):
        has_prev, has_next = _edge_flags(tm, seq)
        muv = mu_ref[0:1, :]
        qv = q_ref[...]
        mq = muv * qv
        mq_down, mq_up = _shifted(mq, muv * qp_ref[GROUP - 1:GROUP, :] * has_prev,
                                  muv * qn_ref[0:1, :] * has_next)
        dp_ref[...] = (qv - mq + 0.5 * (mq_down + mq_up)).astype(BF16)
        pv = p_ref[...]
        p_down, p_up = _shifted(pv, pp_ref[GROUP - 1:GROUP, :] * has_prev, pn_ref[0:1, :] * has_next)

        @pl.when(pl.program_id(0) == 0)
        def _():
            dmu_ref[...] = jnp.zeros_like(dmu_ref)

        dmu_ref[...] += jnp.sum(qv * (0.5 * (p_down + p_up) - pv), axis=0, keepdims=True)

    return pl.pallas_call(
        body, name="shift_bwd", grid=(t // tm,),
        in_specs=[_row(tm, w), prev_spec, next_spec, _row(tm, w), prev_spec, next_spec, _fixed((GROUP, w))],
        out_specs=[_row(tm, w), _fixed((1, w))],
        out_shape=[_sds((t, w), BF16), _sds((1, w))], compiler_params=_cp("arbitrary"))(q, q, q, p, p, p, mu)


@jax.custom_vjp
def _bdot(a, b):
    return jnp.dot(a.astype(BF16), b.astype(BF16), preferred_element_type=F32)


def _bdot_fwd(a, b):
    return _bdot(a, b), (a, b)


def _bdot_bwd(res, g):
    a, b = res
    gb = g.astype(BF16)
    da = lax.dot_general(gb, b.astype(BF16), (((1,), (1,)), ((), ())), preferred_element_type=F32)
    db = lax.dot_general(a.astype(BF16), gb, (((0,), (0,)), ((), ())), preferred_element_type=F32)
    return da, db


_bdot.defvjp(_bdot_fwd, _bdot_bwd)


def _seg_raw(x, ones_blocks):
    hi = x.astype(BF16)
    lo = (x - hi.astype(F32)).astype(BF16)
    return (jnp.dot(hi, ones_blocks, preferred_element_type=F32)
            + jnp.dot(lo, ones_blocks, preferred_element_type=F32))


@jax.custom_vjp
def _seg(x, ones_blocks):
    return _seg_raw(x, ones_blocks)


def _seg_fwd(x, ones_blocks):
    return _seg_raw(x, ones_blocks), ones_blocks


def _seg_bwd(ones_blocks, g):
    return _seg_raw(g, ones_blocks), jnp.zeros_like(ones_blocks)


_seg.defvjp(_seg_fwd, _seg_bwd)


def _head_ones():
    h = jnp.arange(D_RWKV) // HEAD
    return (h[:, None] == h[None, :]).astype(BF16)


def _prep_math(ps, k_k, w0_f, a0_f, k_a_f, w0_b, a0_b, k_a_b, wup_f, aup_f, wup_b, aup_b, gup, ones_blocks):
    r = ps[:, 0:512]
    k = ps[:, 512:1024]
    v = ps[:, 1024:1536]
    xwa = ps[:, 1536:1664]
    xg = ps[:, 1664:D_SHIFT_PAD]
    kk_raw = k * k_k
    norm = jnp.sqrt(_seg(kk_raw * kk_raw, ones_blocks))
    kk = kk_raw / jnp.maximum(norm, NORM_EPS)
    t_xwa = jnp.tanh(xwa)
    outs = [r, v, kk]
    for w0, a0, k_a, wup, aup in ((w0_f, a0_f, k_a_f, wup_f, aup_f), (w0_b, a0_b, k_a_b, wup_b, aup_b)):
        decay = jnp.exp(-LOG_DECAY_SCALE * jax.nn.sigmoid(w0 + _bdot(t_xwa, wup)))
        rate = jax.nn.sigmoid(a0 + _bdot(xwa, aup))
        outs += [decay, k * (1.0 + (rate - 1.0) * k_a), kk * rate]
    outs.append(_bdot(jax.nn.sigmoid(xg), gup))
    return tuple(outs)


def _prep_args(tm, ps_ref, pv_ref, mat_refs, ones_ref):
    vecs = [jnp.broadcast_to(pv_ref[j:j + 1, :], (tm, D_RWKV)) for j in range(7)]
    return [ps_ref[...]] + vecs + [m[...] for m in mat_refs] + [ones_ref[...]]


_PREP_MAT_SHAPES = ((128, D_RWKV),) * 4 + ((D_GATE_PAD, D_RWKV),)


def _prep_fwd(ps, pvec, mats, ones_blocks, tm):
    t = ps.shape[0]

    def body(ps_ref, pv_ref, m0, m1, m2, m3, m4, ones_ref, *out_refs):
        outs = _prep_math(*_prep_args(tm, ps_ref, pv_ref, (m0, m1, m2, m3, m4), ones_ref))
        for o_ref, val in zip(out_refs, outs[2:]):
            o_ref[...] = val

    return pl.pallas_call(
        body, name="prep_fwd", grid=(t // tm,),
        in_specs=[_row(tm, D_SHIFT_PAD), _fixed((8, D_RWKV))] + [_fixed(s) for s in _PREP_MAT_SHAPES]
        + [_fixed((D_RWKV, D_RWKV))],
        out_specs=[_row(tm, D_RWKV)] * 8, out_shape=[_sds((t, D_RWKV))] * 8,
        compiler_params=_cp("parallel"))(ps, pvec, *mats, ones_blocks)


def _prep_bwd(ps, pvec, mats, ones_blocks, cts, tm):
    t = ps.shape[0]
    counts = [len(c) for c in cts]
    flat = [a for c in cts for a in c]

    def body(ps_ref, pv_ref, m0, m1, m2, m3, m4, ones_ref, *refs):
        ct_refs = refs[:len(flat)]
        q_ref, dpv_ref = refs[len(flat)], refs[len(flat) + 1]
        dmat_refs = refs[len(flat) + 2:]
        args = _prep_args(tm, ps_ref, pv_ref, (m0, m1, m2, m3, m4), ones_ref)
        _, vjp = jax.vjp(lambda *a: _prep_math(*a, args[-1]), *args[:-1])
        ct_vals, pos = [], 0
        for n in counts:
            val = ct_refs[pos][...]
            for extra in ct_refs[pos + 1:pos + n]:
                val = val + extra[...]
            ct_vals.append(val)
            pos += n
        grads = vjp(tuple(ct_vals))
        q_ref[...] = grads[0]

        @pl.when(pl.program_id(0) == 0)
        def _():
            dpv_ref[...] = jnp.zeros_like(dpv_ref)
            for d_ref in dmat_refs:
                d_ref[...] = jnp.zeros_like(d_ref)

        for j in range(7):
            dpv_ref[j:j + 1, :] += jnp.sum(grads[1 + j], axis=0, keepdims=True)
        for d_ref, gm in zip(dmat_refs, grads[8:13]):
            d_ref[...] += gm

    return pl.pallas_call(
        body, name="prep_bwd", grid=(t // tm,),
        in_specs=[_row(tm, D_SHIFT_PAD), _fixed((8, D_RWKV))] + [_fixed(s) for s in _PREP_MAT_SHAPES]
        + [_fixed((D_RWKV, D_RWKV))] + [_row(tm, D_RWKV)] * len(flat),
        out_specs=[_row(tm, D_SHIFT_PAD), _fixed((8, D_RWKV))] + [_fixed(s) for s in _PREP_MAT_SHAPES],
        out_shape=[_sds((t, D_SHIFT_PAD)), _sds((8, D_RWKV))] + [_sds(s) for s in _PREP_MAT_SHAPES],
        compiler_params=_cp("arbitrary"))(ps, pvec, *mats, ones_blocks, *flat)


def _pair_ones():
    h = jnp.arange(2 * HEAD) // HEAD
    block = (h[:, None] == h[None, :]).astype(BF16)
    return jnp.concatenate([block, block], axis=0)


def _diag_mask():
    lane = lax.broadcasted_iota(jnp.int32, (HEAD, 2 * HEAD), 1)
    sub = lax.broadcasted_iota(jnp.int32, (HEAD, 2 * HEAD), 0)
    return jnp.where((lane & (HEAD - 1)) == sub, 1.0, 0.0).astype(F32)


def _to_row(cols, dmask):
    return jnp.sum(cols * dmask, axis=0, keepdims=True)


def _seg_many(exact, rounded, ones_pair):
    out_exact, out_rounded = [], []
    if exact:
        parts = []
        for x in exact:
            hi = x.astype(BF16)
            parts.append(jnp.concatenate([hi, (x - hi.astype(F32)).astype(BF16)], axis=1))
        res = jnp.dot(jnp.concatenate(parts, axis=0), ones_pair, preferred_element_type=F32)
        out_exact = [res[HEAD * c:HEAD * (c + 1)] for c in range(len(exact))]
    if rounded:
        res = jnp.dot(jnp.concatenate([x.astype(BF16) for x in rounded], axis=0), ones_pair[0:2 * HEAD],
                      preferred_element_type=F32)
        out_rounded = [res[HEAD * c:HEAD * (c + 1)] for c in range(len(rounded))]
    return out_exact, out_rounded


N_CHAIN = 2 * N_PAIR


def _chain(c):
    d, p = divmod(c, N_PAIR)
    return d, slice(2 * HEAD * p, 2 * HEAD * (p + 1))


def _scan_specs(n_chunks, col_blocks, fwd_chunk, bwd_chunk):
    def spec(chunk_of, col):
        return pl.BlockSpec((SCAN_CHUNK, D_RWKV), lambda b, g: (b * n_chunks + chunk_of(g), col))
    return [spec(fwd_chunk, c) for c in col_blocks] + [spec(bwd_chunk, c) for c in col_blocks]


def _scan_fwd(ps, kk, dirs, batch, seq):
    t = batch * seq
    n = seq // SCAN_CHUNK
    groups = SCAN_CHUNK // GROUP
    up = lambda g: g
    down = lambda g: n - 1 - g
    col_blocks = (0, 2, 0, 0, 0, 0)

    def body(*refs):
        dir_refs = (refs[0:6], refs[6:12])
        ones_ref = refs[12]
        y_refs, hist_refs, st_ref = refs[13:15], refs[15:17], refs[17]

        @pl.when(pl.program_id(1) == 0)
        def _():
            st_ref[...] = jnp.zeros_like(st_ref)

        ones_pair = ones_ref[...]
        dmask = _diag_mask()
        sub8 = lax.broadcasted_iota(jnp.int32, (GROUP, 2 * HEAD), 0)

        def group(gi, carry):
            off = (pl.multiple_of(gi * GROUP, GROUP), pl.multiple_of((groups - 1 - gi) * GROUP, GROUP))
            loaded = [tuple(ref[pl.ds(off[d], GROUP), :] for ref in dir_refs[d]) for d in range(2)]
            states = list(carry)
            y_acc = [jnp.zeros((GROUP, 2 * HEAD), F32) for _ in range(N_CHAIN)]
            for step in range(GROUP):
                rows, idx = [], []
                for c in range(N_CHAIN):
                    d, lanes = _chain(c)
                    i = step if d == 0 else GROUP - 1 - step
                    idx.append(i)
                    rows.append(tuple(x8[i:i + 1, lanes] for x8 in loaded[d]))
                    hist_refs[d][c % N_PAIR, gi * GROUP + step] = states[c]
                _, v_cols = _seg_many([], [dmask * rows[c][1] for c in range(N_CHAIN)], ones_pair)
                sas, _ = _seg_many([states[c] * rows[c][2] for c in range(N_CHAIN)], [], ones_pair)
                for c in range(N_CHAIN):
                    _, _, _, w_row, kd_row, b_row = rows[c]
                    states[c] = states[c] * w_row - sas[c] * b_row + v_cols[c] * kd_row
                _, ys = _seg_many([], [states[c] * rows[c][0] for c in range(N_CHAIN)], ones_pair)
                for c in range(N_CHAIN):
                    y_acc[c] = jnp.where(sub8 == idx[c], _to_row(ys[c], dmask), y_acc[c])
            for c in range(N_CHAIN):
                d, lanes = _chain(c)
                y_refs[d][pl.ds(off[d], GROUP), lanes] = y_acc[c]
            return tuple(states)

        final = lax.fori_loop(0, groups, group, tuple(st_ref[c] for c in range(N_CHAIN)))
        for c in range(N_CHAIN):
            st_ref[c] = final[c]
            hist_refs[c // N_PAIR][c % N_PAIR, SCAN_CHUNK] = final[c]

    y_spec_f = pl.BlockSpec((SCAN_CHUNK, D_RWKV), lambda b, g: (b * n + up(g), 0))
    y_spec_b = pl.BlockSpec((SCAN_CHUNK, D_RWKV), lambda b, g: (b * n + down(g), 0))
    hist_shape = (batch, n, N_PAIR, SCAN_CHUNK + 1, HEAD, 2 * HEAD)
    hist_block = (None, None, N_PAIR, SCAN_CHUNK + 1, HEAD, 2 * HEAD)
    hist_spec_f = pl.BlockSpec(hist_block, lambda b, g: (b, up(g), 0, 0, 0, 0))
    hist_spec_b = pl.BlockSpec(hist_block, lambda b, g: (b, down(g), 0, 0, 0, 0))
    ones_spec = pl.BlockSpec((4 * HEAD, 2 * HEAD), lambda b, g: (0, 0))
    (wf, kdf, bf), (wb, kdb, bb) = dirs
    return pl.pallas_call(
        body, name="wkv_fwd", grid=(batch, n),
        in_specs=_scan_specs(n, col_blocks, up, down) + [ones_spec],
        out_specs=[y_spec_f, y_spec_b, hist_spec_f, hist_spec_b],
        out_shape=[_sds((t, D_RWKV)), _sds((t, D_RWKV)), _sds(hist_shape), _sds(hist_shape)],
        scratch_shapes=[pltpu.VMEM((N_CHAIN, HEAD, 2 * HEAD), F32)],
        compiler_params=_cp("parallel", "arbitrary"),
    )(ps, ps, kk, wf, kdf, bf, ps, ps, kk, wb, kdb, bb, _pair_ones())


def _scan_bwd(ps, kk, dirs, dy, hist_f, hist_b, batch, seq):
    t = batch * seq
    n = seq // SCAN_CHUNK
    groups = SCAN_CHUNK // GROUP
    fwd_chunk = lambda g: n - 1 - g
    bwd_chunk = lambda g: g
    col_blocks = (0, 2, 0, 0, 0, 0, 0)

    def undo_group(dir_refs, out_refs, hist_refs, gi, d_states, ones_pair, dmask, sub8):
        d_states = list(d_states)
        loaded, blocks = [], []
        for d in range(2):
            blk = groups - 1 - gi if d == 0 else gi
            blocks.append(pl.ds(pl.multiple_of(blk * GROUP, GROUP), GROUP))
            loaded.append(tuple(ref[blocks[d], :] for ref in dir_refs[d]))
        acc = [[jnp.zeros((GROUP, 2 * HEAD), F32) for _ in range(6)] for _ in range(N_CHAIN)]
        for step in range(GROUP):
            rows, idx, before, after = [], [], [], []
            for c in range(N_CHAIN):
                d, lanes = _chain(c)
                i = GROUP - 1 - step if d == 0 else step
                q = (groups - 1 - gi) * GROUP + i if d == 0 else SCAN_CHUNK - 1 - (gi * GROUP + i)
                idx.append(i)
                rows.append(tuple(x8[i:i + 1, lanes] for x8 in loaded[d]))
                before.append(hist_refs[d][c % N_PAIR, q])
                after.append(hist_refs[d][c % N_PAIR, q + 1])
            _, cols = _seg_many([], [dmask * rows[c][j] for c in range(N_CHAIN) for j in (1, 6)], ones_pair)
            v_cols, dy_cols = cols[0::2], cols[1::2]
            d_now = [d_states[c] + dy_cols[c] * rows[c][0] for c in range(N_CHAIN)]
            d_sas, _ = _seg_many([d_now[c] * rows[c][5] for c in range(N_CHAIN)], [], ones_pair)
            _, others = _seg_many(
                [], [x for c in range(N_CHAIN) for x in (before[c] * rows[c][2], d_now[c] * rows[c][4])], ones_pair)
            for c in range(N_CHAIN):
                sa, d_sa, dv_cols = others[2 * c], -d_sas[c], others[2 * c + 1]
                rows_out = (
                    jnp.sum(after[c] * dy_cols[c], axis=0, keepdims=True),
                    jnp.sum(d_now[c] * before[c], axis=0, keepdims=True),
                    jnp.sum(d_now[c] * v_cols[c], axis=0, keepdims=True),
                    _to_row(dv_cols, dmask),
                    jnp.sum(before[c] * d_sa, axis=0, keepdims=True),
                    -jnp.sum(d_now[c] * sa, axis=0, keepdims=True),
                )
                acc[c] = [jnp.where(sub8 == idx[c], val, a) for val, a in zip(rows_out, acc[c])]
                d_states[c] = d_now[c] * rows[c][3] + d_sa * rows[c][2]
        for c in range(N_CHAIN):
            d, lanes = _chain(c)
            for o_ref, val in zip(out_refs[d], acc[c]):
                o_ref[blocks[d], lanes] = val
        return tuple(d_states)

    def body(*refs):
        dir_refs = (refs[0:7], refs[7:14])
        hist_refs, ones_ref = refs[14:16], refs[16]
        out_refs = (refs[17:23], refs[23:29])
        dst_ref = refs[29]

        @pl.when(pl.program_id(1) == 0)
        def _():
            dst_ref[...] = jnp.zeros_like(dst_ref)

        ones_pair = ones_ref[...]
        dmask = _diag_mask()
        sub8 = lax.broadcasted_iota(jnp.int32, (GROUP, 2 * HEAD), 0)

        def group(gi, carry):
            return undo_group(dir_refs, out_refs, hist_refs, gi, carry, ones_pair, dmask, sub8)

        final = lax.fori_loop(0, groups, group, tuple(dst_ref[c] for c in range(N_CHAIN)))
        for c in range(N_CHAIN):
            dst_ref[c] = final[c]

    blk = (SCAN_CHUNK, D_RWKV)
    out_f = pl.BlockSpec(blk, lambda b, g: (b * n + fwd_chunk(g), 0))
    out_b = pl.BlockSpec(blk, lambda b, g: (b * n + bwd_chunk(g), 0))
    hist_block = (None, None, N_PAIR, SCAN_CHUNK + 1, HEAD, 2 * HEAD)
    hist_spec_f = pl.BlockSpec(hist_block, lambda b, g: (b, fwd_chunk(g), 0, 0, 0, 0))
    hist_spec_b = pl.BlockSpec(hist_block, lambda b, g: (b, bwd_chunk(g), 0, 0, 0, 0))
    ones_spec = pl.BlockSpec((4 * HEAD, 2 * HEAD), lambda b, g: (0, 0))
    (wf, kdf, bf), (wb, kdb, bb) = dirs
    outs = pl.pallas_call(
        body, name="wkv_bwd", grid=(batch, n),
        in_specs=_scan_specs(n, col_blocks, fwd_chunk, bwd_chunk) + [hist_spec_f, hist_spec_b, ones_spec],
        out_specs=[out_f] * 6 + [out_b] * 6,
        out_shape=[_sds((t, D_RWKV))] * 12,
        scratch_shapes=[pltpu.VMEM((N_CHAIN, HEAD, 2 * HEAD), F32)],
        compiler_params=_cp("parallel", "arbitrary"),
    )(ps, ps, kk, wf, kdf, bf, dy, ps, ps, kk, wb, kdb, bb, dy, hist_f, hist_b, _pair_ones())
    return outs[0:6], outs[6:12]


def _post_math(y, r, kd_f, kd_b, v, gate, gn_w, gn_b, rk_f, rk_b, ones_blocks):
    mean = _seg(y, ones_blocks) * (1.0 / HEAD)
    yc = y - mean
    var = _seg(yc * yc, ones_blocks) * (1.0 / HEAD)
    yn = yc * lax.rsqrt(var + GN_EPS) * gn_w + gn_b
    bonus = _seg(r * kd_f * rk_f, ones_blocks) * v + _seg(r * kd_b * rk_b, ones_blocks) * v
    return (yn + bonus) * gate


def _conv_parts(pc, halo_prev, halo_next, has_prev, has_next):
    gate_b, gate_c, hid = pc[:, 0:512], pc[:, 512:1024], pc[:, 1024:1536]
    u = gate_c * hid
    u_prev_row = halo_prev[GROUP - 1:GROUP, 512:1024] * halo_prev[GROUP - 1:GROUP, 1024:1536] * has_prev
    u_next_row = halo_next[0:1, 512:1024] * halo_next[0:1, 1024:1536] * has_next
    u_down, u_up = _shifted(u, u_prev_row, u_next_row)
    return gate_b, gate_c, hid, u, u_down, u_up


def _post_specs(tm, t):
    pc_prev, pc_next = _halo_specs(tm, D_CONV3, t)
    col = lambda c: pl.BlockSpec((tm, D_RWKV), lambda i: (i, c))
    return ([col(0), col(0), col(0), col(0), col(0), col(2), col(0), _row(tm, D_CONV3), pc_prev, pc_next,
             _fixed((8, D_RWKV)), _fixed((D_RWKV, D_RWKV))])


def _post_fwd(y_f, y_b, ps, kd_f, kd_b, gate, pc, qvec, ones_blocks, tm, seq):
    t = ps.shape[0]

    def body(yf_ref, yb_ref, r_ref, kdf_ref, kdb_ref, v_ref, g_ref, pc_ref, hp_ref, hn_ref, qv_ref, ones_ref,
             o_ref):
        has_prev, has_next = _edge_flags(tm, seq)
        vec = [jnp.broadcast_to(qv_ref[j:j + 1, :], (tm, D_RWKV)) for j in range(7)]
        o_ref[:, 0:D_RWKV] = _post_math(yf_ref[...] + yb_ref[...], r_ref[...], kdf_ref[...], kdb_ref[...],
                                        v_ref[...], g_ref[...], vec[0], vec[1], vec[2], vec[3],
                                        ones_ref[...]).astype(BF16)
        gate_b, _, _, u, u_down, u_up = _conv_parts(pc_ref[...], hp_ref[...], hn_ref[...], has_prev, has_next)
        o_ref[:, D_RWKV:2 * D_RWKV] = (gate_b * (vec[4] * u_down + vec[5] * u + vec[6] * u_up)).astype(BF16)

    return pl.pallas_call(
        body, name="post_fwd", grid=(t // tm,), in_specs=_post_specs(tm, t), out_specs=_row(tm, D_MODEL),
        out_shape=_sds((t, D_MODEL), BF16), compiler_params=_cp("parallel"),
    )(y_f, y_b, ps, kd_f, kd_b, ps, gate, pc, pc, pc, qvec, ones_blocks)


def _post_bwd(d_out, y_f, y_b, ps, kd_f, kd_b, gate, pc, qvec, ones_blocks, tm, seq):
    t = ps.shape[0]
    do_prev, do_next = _halo_specs(tm, D_MODEL, t)

    def body(do_ref, dop_ref, don_ref, yf_ref, yb_ref, r_ref, kdf_ref, kdb_ref, v_ref, g_ref, pc_ref, hp_ref,
             hn_ref, qv_ref, ones_ref, dy_ref, dr_ref, dkdf_ref, dkdb_ref, dv_ref, dg_ref, dpc_ref, dqv_ref):
        has_prev, has_next = _edge_flags(tm, seq)
        vec = [jnp.broadcast_to(qv_ref[j:j + 1, :], (tm, D_RWKV)) for j in range(7)]
        ones_v = ones_ref[...]
        args = (yf_ref[...] + yb_ref[...], r_ref[...], kdf_ref[...], kdb_ref[...], v_ref[...], g_ref[...],
                vec[0], vec[1], vec[2], vec[3])
        _, vjp = jax.vjp(lambda *a: _post_math(*a, ones_v), *args)
        grads = vjp(do_ref[:, 0:D_RWKV])
        for o_ref, gval in zip((dy_ref, dr_ref, dkdf_ref, dkdb_ref, dv_ref, dg_ref), grads[0:6]):
            o_ref[...] = gval

        hp, hn = hp_ref[...], hn_ref[...]
        gate_b, gate_c, hid, u, u_down, u_up = _conv_parts(pc_ref[...], hp, hn, has_prev, has_next)
        d_oc = do_ref[:, D_RWKV:2 * D_RWKV]
        d_cu = d_oc * gate_b
        d_cu_prev = dop_ref[GROUP - 1:GROUP, D_RWKV:2 * D_RWKV] * hp[GROUP - 1:GROUP, 0:512] * has_prev
        d_cu_next = don_ref[0:1, D_RWKV:2 * D_RWKV] * hn[0:1, 0:512] * has_next
        d_cu_down, d_cu_up = _shifted(d_cu, d_cu_prev, d_cu_next)
        d_u = vec[5] * d_cu + vec[4] * d_cu_up + vec[6] * d_cu_down
        dpc_ref[:, 0:512] = (d_oc * (vec[4] * u_down + vec[5] * u + vec[6] * u_up)).astype(BF16)
        dpc_ref[:, 512:1024] = (d_u * hid).astype(BF16)
        dpc_ref[:, 1024:1536] = (d_u * gate_c).astype(BF16)

        @pl.when(pl.program_id(0) == 0)
        def _():
            dqv_ref[...] = jnp.zeros_like(dqv_ref)

        vec_grads = list(grads[6:10]) + [d_cu * u_down, d_cu * u, d_cu * u_up]
        for j, gval in enumerate(vec_grads):
            dqv_ref[j:j + 1, :] += jnp.sum(gval, axis=0, keepdims=True)

    return pl.pallas_call(
        body, name="post_bwd", grid=(t // tm,),
        in_specs=[_row(tm, D_MODEL), do_prev, do_next] + _post_specs(tm, t),
        out_specs=[_row(tm, D_RWKV)] * 6 + [_row(tm, D_CONV3), _fixed((8, D_RWKV))],
        out_shape=[_sds((t, D_RWKV))] * 6 + [_sds((t, D_CONV3), BF16), _sds((8, D_RWKV))],
        compiler_params=_cp("arbitrary"),
    )(d_out, d_out, d_out, y_f, y_b, ps, kd_f, kd_b, ps, gate, pc, pc, pc, qvec, ones_blocks)


def _adamw(w, g, m, v, name):
    r, c = w.shape[-2:]
    tr = _tile(r, 256, 8)
    if w.ndim == 3:
        spec = pl.BlockSpec((None, tr, c), lambda i: (0, i, 0))
    else:
        spec = pl.BlockSpec((tr, c), lambda i: (i, 0))

    def body(w_ref, g_ref, m_ref, v_ref, d_ref, nm_ref, nv_ref):
        gv = g_ref[...]
        m2 = ADAM_B1 * m_ref[...] + (1.0 - ADAM_B1) * gv
        v2 = ADAM_B2 * v_ref[...] + (1.0 - ADAM_B2) * (gv * gv)
        m_hat = m2 / (1.0 - ADAM_B1 ** ADAM_STEP)
        v_hat = v2 / (1.0 - ADAM_B2 ** ADAM_STEP)
        d_ref[...] = -ADAM_LR * (m_hat / (jnp.sqrt(v_hat) + ADAM_EPS) + ADAM_WD * w_ref[...])
        nm_ref[...] = m2
        nv_ref[...] = v2

    return pl.pallas_call(
        body, name=name, grid=(r // tr,), in_specs=[spec] * 4, out_specs=[spec] * 3,
        out_shape=[_sds(w.shape)] * 3, compiler_params=_cp("parallel"))(w, g, m, v)


def _add_n(parts, name, out_dtypes=(F32,)):
    r, c = parts[0].shape
    tr = _tile(r, 408, 16)
    spec = pl.BlockSpec((tr, c), lambda i: (i, 0))
    n_in = len(parts)

    def body(*refs):
        acc = refs[0][...].astype(F32)
        for ref in refs[1:n_in]:
            acc = acc + ref[...].astype(F32)
        for o_ref in refs[n_in:]:
            o_ref[...] = acc.astype(o_ref.dtype)

    outs = pl.pallas_call(
        body, name=name, grid=(r // tr,), in_specs=[spec] * n_in, out_specs=[spec] * len(out_dtypes),
        out_shape=[_sds((r, c), dt) for dt in out_dtypes], compiler_params=_cp("parallel"))(*parts)
    return outs[0] if len(out_dtypes) == 1 else outs


_ANY = pl.BlockSpec(memory_space=pl.ANY)


def _place():
    return lax.axis_index("x"), lax.axis_index("y"), lax.axis_index("c")


def _other_chips(x, y):
    return [(1 - x, y), (x, 1 - y), (1 - x, 1 - y)]


def _remote(src, dst, send_sems, recv_sems, k, to):
    return pltpu.make_async_remote_copy(src_ref=src, dst_ref=dst, send_sem=send_sems.at[k],
                                        recv_sem=recv_sems.at[k], device_id=to, device_id_type=MESH)


def _gather_weights(pack):
    rows, width = pack.shape
    half = rows // 2

    def body(x_ref, out_ref, send_sems, recv_sems):
        x, y, c = _place()
        sibling = (x, y, 1 - c)
        chips = _other_chips(x, y)

        def block(chip, part):
            return out_ref.at[2 * chip[0] + chip[1], pl.ds(part * half, half), :]

        first = [_remote(x_ref.at[pl.ds(c * half, half), :], block((x, y), c), send_sems, recv_sems, j, (*chip, c))
                 for j, chip in enumerate(chips)]
        for cp in first:
            cp.start()
        passed = [_remote(block(chip, c), block(chip, c), send_sems, recv_sems, 3 + j, sibling)
                  for j, chip in enumerate(chips)]
        for j, chip in enumerate(chips):
            _remote(block(chip, c), block(chip, c), send_sems, recv_sems, j, sibling).wait_recv()
            passed[j].start()
        for j, chip in enumerate(chips):
            _remote(block(chip, 1 - c), block(chip, 1 - c), send_sems, recv_sems, 3 + j, sibling).wait_recv()
        for cp in first + passed:
            cp.wait_send()

    return pl.pallas_call(
        body, name="gather_weights", in_specs=[_ANY], out_specs=_ANY,
        out_shape=_sds((N_SHARD, rows, width), pack.dtype),
        scratch_shapes=[pltpu.SemaphoreType.DMA((6,)), pltpu.SemaphoreType.DMA((6,))],
    )(pack)


def _swap_with_sibling(block, name):
    def body(x_ref, out_ref, send_sems, recv_sems):
        x, y, c = _place()
        cp = _remote(x_ref, out_ref, send_sems, recv_sems, 0, (x, y, 1 - c))
        cp.start()
        cp.wait()

    return pl.pallas_call(
        body, name=name, in_specs=[_ANY], out_specs=_ANY, out_shape=_sds(block.shape, block.dtype),
        scratch_shapes=[pltpu.SemaphoreType.DMA((1,)), pltpu.SemaphoreType.DMA((1,))],
    )(block)


def _exchange_quarters(parts):
    _, rows, width = parts.shape

    def body(x_ref, out_ref, send_sems, recv_sems):
        x, y, c = _place()
        copies = [_remote(x_ref.at[2 * chip[0] + chip[1]], out_ref.at[j], send_sems, recv_sems, j, (*chip, c))
                  for j, chip in enumerate(_other_chips(x, y))]
        for cp in copies:
            cp.start()
        for cp in copies:
            cp.wait()

    return pl.pallas_call(
        body, name="exchange_quarters", in_specs=[_ANY], out_specs=_ANY,
        out_shape=_sds((3, rows, width), parts.dtype),
        scratch_shapes=[pltpu.SemaphoreType.DMA((3,)), pltpu.SemaphoreType.DMA((3,))],
    )(parts)


def _allreduce_small(vec):
    rows, width = vec.shape
    vmem = pl.BlockSpec(memory_space=pltpu.VMEM)

    def body(x_ref, o_ref, buf_ref, send_sems, recv_sems):
        x, y, c = _place()
        me = 4 * x + 2 * y + c
        buf_ref[me] = x_ref[...]
        copies = []
        for k in range(1, N_DEV):
            peer = (x ^ ((k >> 2) & 1), y ^ ((k >> 1) & 1), c ^ (k & 1))
            copies.append(_remote(x_ref, buf_ref.at[me], send_sems, recv_sems, k - 1, peer))
        for cp in copies:
            cp.start()
        for k in range(1, N_DEV):
            _remote(x_ref, buf_ref.at[me ^ k], send_sems, recv_sems, k - 1, (x, y, c)).wait_recv()
        for cp in copies:
            cp.wait_send()
        total = buf_ref[0]
        for d in range(1, N_DEV):
            total = total + buf_ref[d]
        o_ref[...] = total

    return pl.pallas_call(
        body, name="allreduce_small", in_specs=[vmem], out_specs=vmem, out_shape=_sds((rows, width)),
        scratch_shapes=[pltpu.VMEM((N_DEV, rows, width), F32), pltpu.SemaphoreType.DMA((N_DEV - 1,)),
                        pltpu.SemaphoreType.DMA((N_DEV - 1,))],
    )(vec)


def _rows1024(a):
    return a.reshape(-1, 1024)


def _pad_rows(a, rows):
    return jnp.concatenate([a, jnp.zeros((rows - a.shape[0], a.shape[1]), a.dtype)], axis=0)


def _pack_weight_shards(w):
    conv_bits = lax.bitcast_convert_type(w["conv_w"], BF16).reshape(1, -1)
    conv_row = jnp.concatenate([conv_bits, jnp.zeros((1, 1024 - conv_bits.shape[1]), BF16)], axis=1)
    parts = [_rows1024(w[name].astype(BF16)) for name, _ in _PACK_ROWS[:-1]] + [conv_row]
    return _pad_rows(jnp.concatenate(parts, axis=0), PACK_R)


def _unpack_gathered(gathered):
    out, row = {}, 0
    for name, n in _PACK_ROWS:
        out[name] = gathered[:, row:row + n]
        row += n
    cols = lambda a, k: jnp.concatenate([a[s].reshape(k, -1) for s in range(N_SHARD)], axis=1)
    conv = lax.bitcast_convert_type(out["conv_w"][:, 0, :768].reshape(N_SHARD, 3, 128, 2), F32)
    return dict(
        w_in=cols(out["w_in"], D_MODEL), w_out=out["w_out"].reshape(D_MODEL, D_MODEL),
        w_gate=cols(out["w_gate"], D_MODEL), w_up=cols(out["w_up"], D_MODEL),
        w_down=out["w_down"].reshape(D_FF, D_MODEL),
        w_up_f=cols(out["w_up_f"], D_LORA), w_up_b=cols(out["w_up_b"], D_LORA),
        a_up_f=cols(out["a_up_f"], D_LORA), a_up_b=cols(out["a_up_b"], D_LORA),
        g_up=cols(out["g_up"], D_GATE), conv_w=jnp.concatenate([conv[s] for s in range(N_SHARD)], axis=1))


def _pack_grads(g):
    col_split = lambda a, s: a[:, s * (a.shape[1] // N_SHARD):(s + 1) * (a.shape[1] // N_SHARD)]
    row_split = lambda a, s: a[s * (a.shape[0] // N_SHARD):(s + 1) * (a.shape[0] // N_SHARD)]
    slots = []
    for s in range(N_SHARD):
        conv = col_split(g["conv_w"], s).reshape(1, -1)
        parts = [_rows1024(col_split(g["w_in"], s)), row_split(g["w_out"], s), _rows1024(col_split(g["w_gate"], s)),
                 _rows1024(col_split(g["w_up"], s)), row_split(g["w_down"], s)]
        parts += [_rows1024(col_split(g[name], s)) for name in ("w_up_f", "w_up_b", "a_up_f", "a_up_b", "g_up")]
        parts.append(jnp.concatenate([conv, jnp.zeros((1, 1024 - conv.shape[1]), F32)], axis=1))
        slots.append(_pad_rows(jnp.concatenate(parts, axis=0), PACK_R))
    return jnp.stack(slots)


def _unpack_grad_shard(pack):
    shapes = dict(w_in=(D_MODEL, 840), w_out=(256, D_MODEL), w_gate=(D_MODEL, 704), w_up=(D_MODEL, 704),
                  w_down=(704, D_MODEL), w_up_f=(D_LORA, 128), w_up_b=(D_LORA, 128), a_up_f=(D_LORA, 128),
                  a_up_b=(D_LORA, 128), g_up=(D_GATE, 128))
    out, row = {}, 0
    for name, n in _PACK_ROWS[:-1]:
        out[name] = pack[row:row + n].reshape(shapes[name])
        row += n
    out["conv_w"] = pack[row, :384].reshape(3, 128)
    return out


_SMALL_LAYOUT = (("norm1_w", 1024), ("mu_shift", D_SHIFT), ("w0_f", 512), ("w0_b", 512), ("a0_f", 512),
                 ("a0_b", 512), ("k_k", 512), ("k_a_f", 512), ("k_a_b", 512), ("r_k_f", 512), ("r_k_b", 512),
                 ("gn_w", 512), ("gn_b", 512), ("norm2_w", 1024), ("norm_f_w", 1024), ("loss", 1))


def _pack_small(vals):
    rows = []
    for name, n in _SMALL_LAYOUT:
        flat = vals[name].reshape(-1)
        n_rows = -(-n // 1024)
        rows.append(jnp.concatenate([flat, jnp.zeros((n_rows * 1024 - n,), F32)]).reshape(n_rows, 1024))
    return _pad_rows(jnp.concatenate(rows, axis=0), SMALL_ROWS)


def _unpack_small(pack):
    out, row = {}, 0
    for name, n in _SMALL_LAYOUT:
        n_rows = -(-n // 1024)
        out[name] = pack[row:row + n_rows].reshape(-1)[:n]
        row += n_rows
    return out


_WEIGHTS = ("norm1_w", "w_in", "mu_shift", "w_up_f", "w0_f", "w_up_b", "w0_b", "a_up_f", "a0_f", "a_up_b", "a0_b",
            "g_up", "k_k", "k_a_f", "k_a_b", "r_k_f", "r_k_b", "gn_w", "gn_b", "conv_w", "w_out", "norm2_w",
            "w_gate", "w_up", "w_down", "norm_f_w")


def _train_step(x, loss_target, w, m, v):
    batch, seq, _ = x.shape
    t = batch * seq
    tm = _tile(seq, 256, 8)
    xs = x.reshape(t, D_MODEL)
    target = loss_target.reshape(t, D_MODEL)
    vec = lambda name: w[name].reshape(1, -1)

    local = {name: w[name][0] for name, _ in _PACK_ROWS}
    c = lax.axis_index("c")
    chip = 2 * lax.axis_index("x") + lax.axis_index("y")
    pack = _pack_weight_shards(local)
    full = _unpack_gathered(lax.dynamic_update_slice(_gather_weights(pack), pack[None], (chip, 0, 0)))
    w_in = full["w_in"]
    w_shift = jnp.concatenate([w_in[:, :D_SHIFT], jnp.zeros((D_MODEL, D_SHIFT_PAD - D_SHIFT), BF16)], axis=1)
    w_conv = w_in[:, D_SHIFT:]
    zeros_lora = jnp.zeros((D_LORA, D_RWKV), F32)
    lora = lambda name: full[name].astype(F32)
    mats = (jnp.concatenate([lora("w_up_f"), zeros_lora]), jnp.concatenate([zeros_lora, lora("a_up_f")]),
            jnp.concatenate([lora("w_up_b"), zeros_lora]), jnp.concatenate([zeros_lora, lora("a_up_b")]),
            jnp.concatenate([lora("g_up"), jnp.zeros((D_GATE_PAD - D_GATE, D_RWKV), F32)]))
    mu = jnp.concatenate([vec("mu_shift"), jnp.zeros((1, D_SHIFT_PAD - D_SHIFT), F32)], axis=1)
    mu = jnp.broadcast_to(mu, (GROUP, D_SHIFT_PAD))
    zero_row = jnp.zeros((1, D_RWKV), F32)
    pvec = jnp.concatenate([vec("k_k"), vec("w0_f"), vec("a0_f"), vec("k_a_f"), vec("w0_b"), vec("a0_b"),
                            vec("k_a_b"), zero_row], axis=0)
    qvec = jnp.concatenate([vec("gn_w"), vec("gn_b"), vec("r_k_f"), vec("r_k_b"), full["conv_w"], zero_row], axis=0)
    ones_blocks = _head_ones()

    h1 = _rmsnorm_fwd(xs, vec("norm1_w"), tm, "norm1_fwd")
    p_shift = _matmul(h1, w_shift, mode="nn", name="in_proj_shift")
    pc = _matmul(h1, w_conv, mode="nn", name="in_proj_conv")
    ps = _shift_fwd(p_shift, mu, tm, seq)
    kk, w_f, kd_f, b_f, w_b, kd_b, b_b, gate = _prep_fwd(ps, pvec, mats, ones_blocks, tm)
    dirs = ((w_f, kd_f, b_f), (w_b, kd_b, b_b))
    y_f, y_b, hist_f, hist_b = _scan_fwd(ps, kk, dirs, batch, seq)
    mixed = _post_fwd(y_f, y_b, ps, kd_f, kd_b, gate, pc, qvec, ones_blocks, tm, seq)
    x1 = _matmul(mixed, full["w_out"], mode="nn", name="out_proj", add=xs)
    h2 = _rmsnorm_fwd(x1, vec("norm2_w"), tm, "norm2_fwd")
    ff_gate = _matmul(h2, full["w_gate"], mode="nn", name="ffn_gate")
    ff_up = _matmul(h2, full["w_up"], mode="nn", name="ffn_up")
    act = _swiglu_fwd(ff_gate, ff_up, tm)
    x2 = _matmul(act, full["w_down"], mode="nn", name="ffn_down", add=x1)
    d_x2, d_norm_f, loss_part = _loss_head(x2, w["norm_f_w"].reshape(1, -1), target, tm)

    g = {}
    d_act = _matmul(d_x2, full["w_down"], mode="nt", name="ffn_down_dx")
    g["w_down"] = _matmul(d_x2, act, mode="tn", name="ffn_down_dw").T
    d_gate, d_up = _swiglu_bwd(ff_gate, ff_up, d_act, tm)
    d_h2 = _matmul(d_gate, full["w_gate"], mode="nt", name="ffn_gate_dx")
    d_h2 = _matmul(d_up, full["w_up"], mode="nt", name="ffn_up_dx", add=d_h2)
    g["w_gate"] = _matmul(h2, d_gate, mode="tn", name="ffn_gate_dw")
    g["w_up"] = _matmul(h2, d_up, mode="tn", name="ffn_up_dw")
    d_x1, d_norm2 = _rmsnorm_bwd(x1, vec("norm2_w"), d_h2, d_x2, tm, "norm2_bwd")
    d_mixed = _matmul(d_x1, full["w_out"], mode="nt", name="out_proj_dx")
    g["w_out"] = _matmul(mixed, d_x1, mode="tn", name="out_proj_dw")
    dy, dr_o, dkdf_o, dkdb_o, dv_o, d_gatev, d_pc, d_qvec = _post_bwd(
        d_mixed, y_f, y_b, ps, kd_f, kd_b, gate, pc, qvec, ones_blocks, tm, seq)
    (dr_f, dw_f, dkd_f, dv_f, dkk_f, db_f), (dr_b, dw_b, dkd_b, dv_b, dkk_b, db_b) = _scan_bwd(
        ps, kk, dirs, dy, hist_f, hist_b, batch, seq)
    cts = [[dr_f, dr_b, dr_o], [dv_f, dv_b, dv_o], [dkk_f, dkk_b], [dw_f], [dkd_f, dkdf_o], [db_f],
           [dw_b], [dkd_b, dkdb_o], [db_b], [d_gatev]]
    q, d_pvec, d_m0, d_m1, d_m2, d_m3, d_m4 = _prep_bwd(ps, pvec, mats, ones_blocks, cts, tm)
    d_pshift, d_mu = _shift_bwd(q, p_shift, mu, tm, seq)
    d_h1 = _matmul(d_pshift, w_shift, mode="nt", name="in_proj_shift_dx")
    d_h1 = _matmul(d_pc, w_conv, mode="nt", name="in_proj_conv_dx", add=d_h1)
    d_w_shift = _matmul(h1, d_pshift, mode="tn", name="in_proj_shift_dw")
    d_w_conv = _matmul(h1, d_pc, mode="tn", name="in_proj_conv_dw")
    g["w_in"] = jnp.concatenate([d_w_shift[:, :D_SHIFT], d_w_conv], axis=1)
    d_x, d_norm1 = _rmsnorm_bwd(xs, vec("norm1_w"), d_h1, d_x1, tm, "norm1_bwd")
    g["w_up_f"], g["a_up_f"] = d_m0[:D_LORA], d_m1[D_LORA:]
    g["w_up_b"], g["a_up_b"] = d_m2[:D_LORA], d_m3[D_LORA:]
    g["g_up"] = d_m4[:D_GATE]
    g["conv_w"] = d_qvec[4:7]

    packed = _pack_grads(g)
    keep = lax.dynamic_slice_in_dim(packed, c * PACK_H, PACK_H, axis=1)
    give = lax.dynamic_slice_in_dim(packed, (1 - c) * PACK_H, PACK_H, axis=1)
    got = _swap_with_sibling(give, "swap_halves")
    chip_sum, chip_sum_bf16 = _add_n([keep.reshape(-1, 1024), got.reshape(-1, 1024)], "add_halves", (F32, BF16))
    others = _exchange_quarters(chip_sum_bf16.reshape(N_SHARD, PACK_H, 1024))
    own = lax.dynamic_index_in_dim(chip_sum.reshape(N_SHARD, PACK_H, 1024), chip, axis=0, keepdims=False)
    eighth = _add_n([own, others[0], others[1], others[2]], "add_quarters")
    other_eighth = _swap_with_sibling(eighth, "swap_eighths")
    grads = _unpack_grad_shard(jnp.concatenate([jnp.where(c == 0, eighth, other_eighth),
                                                jnp.where(c == 0, other_eighth, eighth)], axis=0))

    small = dict(norm1_w=d_norm1, mu_shift=d_mu[:, :D_SHIFT], w0_f=d_pvec[1], w0_b=d_pvec[4], a0_f=d_pvec[2],
                 a0_b=d_pvec[5], k_k=d_pvec[0], k_a_f=d_pvec[3], k_a_b=d_pvec[6], r_k_f=d_qvec[2], r_k_b=d_qvec[3],
                 gn_w=d_qvec[0], gn_b=d_qvec[1], norm2_w=d_norm2, norm_f_w=d_norm_f, loss=loss_part)
    reduced = _unpack_small(_allreduce_small(_pack_small(small)))
    loss = reduced.pop("loss")[0]
    grads.update(reduced)

    outs = {}
    for name in _WEIGHTS:
        shape = w[name].shape
        as2d = (1, shape[0]) if len(shape) == 1 else shape
        grad = grads[name].reshape(shape)
        delta, new_m, new_v = _adamw(w[name].reshape(as2d), grad.reshape(as2d), m[name].reshape(as2d),
                                     v[name].reshape(as2d), "adamw_" + name)
        outs[name] = (grad, delta.reshape(shape), new_m.reshape(shape), new_v.reshape(shape))
    d_x = d_x.reshape(batch, seq, D_MODEL)
    return (loss, d_x) + tuple(outs[name][k] for k in range(4) for name in _WEIGHTS)


def kernel(x, norm1_w, w_in, mu_shift, w_up_f, w0_f, w_up_b, w0_b, a_up_f, a0_f, a_up_b, a0_b, g_up, k_k, k_a_f, k_a_b, r_k_f, r_k_b, gn_w, gn_b, conv_w, w_out, norm2_w, w_gate, w_up, w_down, norm_f_w, loss_target, m_norm1_w, m_w_in, m_mu_shift, m_w_up_f, m_w0_f, m_w_up_b, m_w0_b, m_a_up_f, m_a0_f, m_a_up_b, m_a0_b, m_g_up, m_k_k, m_k_a_f, m_k_a_b, m_r_k_f, m_r_k_b, m_gn_w, m_gn_b, m_conv_w, m_w_out, m_norm2_w, m_w_gate, m_w_up, m_w_down, m_norm_f_w, v_norm1_w, v_w_in, v_mu_shift, v_w_up_f, v_w0_f, v_w_up_b, v_w0_b, v_a_up_f, v_a0_f, v_a_up_b, v_a0_b, v_g_up, v_k_k, v_k_a_f, v_k_a_b, v_r_k_f, v_r_k_b, v_gn_w, v_gn_b, v_conv_w, v_w_out, v_norm2_w, v_w_gate, v_w_up, v_w_down, v_norm_f_w):
    args = locals()
    w = {name: args[name] for name in _WEIGHTS}
    m = {name: args["m_" + name] for name in _WEIGHTS}
    v = {name: args["v_" + name] for name in _WEIGHTS}
    return _train_step(x, loss_target, w, m, v)
```

```python
import functools

import jax
import jax.numpy as jnp
from jax import lax
from jax.experimental import pallas as pl
from jax.experimental.pallas import tpu as pltpu

F32 = jnp.float32
BF16 = jnp.bfloat16
MESH = pl.DeviceIdType.MESH

D_MODEL = 1024
D_RWKV = 512
HEAD = 64
N_PAIR = D_RWKV // (2 * HEAD)
D_LORA = 64
D_GATE = 160
D_GATE_PAD = 384
D_FF = 2816
D_SHIFT = 1824
D_SHIFT_PAD = 2048
D_CONV3 = 1536
LOG_DECAY_SCALE = 0.606531
RMS_EPS = 1e-6
GN_EPS = 64e-5
NORM_EPS = 1e-12
ADAM_LR, ADAM_B1, ADAM_B2, ADAM_EPS, ADAM_WD, ADAM_STEP = 0.001, 0.9, 0.999, 1e-08, 0.01, 10

N_SHARD = 4
N_DEV = 8
V7X_VMEM_LIMIT = 48 * 1024 * 1024
SCAN_CHUNK = 32
GROUP = 8

_PACK_ROWS = (("w_in", 840), ("w_out", 256), ("w_gate", 704), ("w_up", 704), ("w_down", 704),
              ("w_up_f", 8), ("w_up_b", 8), ("a_up_f", 8), ("a_up_b", 8), ("g_up", 20), ("conv_w", 1))
PACK_R = 3264
PACK_H = PACK_R // 2
SMALL_ROWS = 24


def _tile(n, cap, mult=128):
    best = None
    t = mult
    while t <= min(n, cap):
        if n % t == 0:
            best = t
        t += mult
    return best or n


def _cp(*sem):
    return pltpu.CompilerParams(dimension_semantics=sem or None, vmem_limit_bytes=V7X_VMEM_LIMIT)


def _sds(shape, dtype=F32):
    return jax.ShapeDtypeStruct(shape, dtype)


def _matmul(a, b, *, mode, name, out_dtype=F32, add=None):
    if mode == "tn":
        r, m = a.shape
        n = b.shape[1]
        tm, tn, tk = _tile(m, 512), _tile(n, 1536), _tile(r, 512, 8)
        nk = r // tk
        a_spec = pl.BlockSpec((tk, tm), lambda i, j, k: (k, i))
        b_spec = pl.BlockSpec((tk, tn), lambda i, j, k: (k, j))
        dims = (((0,), (0,)), ((), ()))
    else:
        m, kdim = a.shape
        n = b.shape[1] if mode == "nn" else b.shape[0]
        tm, tn, nk = _tile(m, 512, 8), _tile(n, 1536), 1
        a_spec = pl.BlockSpec((tm, kdim), lambda i, j, k: (i, 0))
        if mode == "nn":
            b_spec = pl.BlockSpec((kdim, tn), lambda i, j, k: (0, j))
            dims = (((1,), (0,)), ((), ()))
        else:
            b_spec = pl.BlockSpec((tn, kdim), lambda i, j, k: (j, 0))
            dims = (((1,), (1,)), ((), ()))
    has_add = add is not None

    def body(*refs):
        a_ref, b_ref = refs[0], refs[1]
        add_ref = refs[2] if has_add else None
        o_ref = refs[3] if has_add else refs[2]
        part = lax.dot_general(a_ref[...].astype(BF16), b_ref[...].astype(BF16), dims,
                               preferred_element_type=F32)
        if nk == 1:
            if has_add:
                part = part + add_ref[...]
            o_ref[...] = part.astype(out_dtype)
        else:
            acc_ref = refs[-1]
            k = pl.program_id(2)

            @pl.when(k == 0)
            def _():
                acc_ref[...] = jnp.zeros_like(acc_ref)

            acc_ref[...] += part

            @pl.when(k == nk - 1)
            def _():
                res = acc_ref[...]
                if has_add:
                    res = res + add_ref[...]
                o_ref[...] = res.astype(out_dtype)

    o_spec = pl.BlockSpec((tm, tn), lambda i, j, k: (i, j))
    in_specs = [a_spec, b_spec] + ([o_spec] if has_add else [])
    args = (a, b) + ((add,) if has_add else ())
    return pl.pallas_call(
        body, name=name, grid=(m // tm, n // tn, nk), in_specs=in_specs, out_specs=o_spec,
        out_shape=_sds((m, n), out_dtype),
        scratch_shapes=[pltpu.VMEM((tm, tn), F32)] if nk > 1 else [],
        compiler_params=_cp("parallel", "parallel", "arbitrary"),
    )(*args)


def _row(tm, width):
    return pl.BlockSpec((tm, width), lambda i: (i, 0))


def _fixed(shape):
    return pl.BlockSpec(shape, lambda i: tuple(0 for _ in shape))


def _rmsnorm_fwd(x, w, tm, name):
    t, d = x.shape

    def body(x_ref, w_ref, o_ref):
        xv = x_ref[...]
        rstd = lax.rsqrt(jnp.mean(xv * xv, axis=-1, keepdims=True) + RMS_EPS)
        o_ref[...] = (xv * rstd * w_ref[...]).astype(BF16)

    return pl.pallas_call(
        body, name=name, grid=(t // tm,), in_specs=[_row(tm, d), _fixed((1, d))], out_specs=_row(tm, d),
        out_shape=_sds((t, d), BF16), compiler_params=_cp("parallel"))(x, w)


def _rms_bwd_math(xv, wv, dyv):
    rstd = lax.rsqrt(jnp.mean(xv * xv, axis=-1, keepdims=True) + RMS_EPS)
    xhat = xv * rstd
    gv = dyv * wv
    dx = rstd * (gv - xhat * jnp.mean(gv * xhat, axis=-1, keepdims=True))
    return dx, jnp.sum(dyv * xhat, axis=0, keepdims=True)


def _rmsnorm_bwd(x, w, dy, dres, tm, name):
    t, d = x.shape

    def body(x_ref, w_ref, dy_ref, dres_ref, dx_ref, dw_ref):
        dx, dw = _rms_bwd_math(x_ref[...], w_ref[...], dy_ref[...])
        dx_ref[...] = dres_ref[...] + dx

        @pl.when(pl.program_id(0) == 0)
        def _():
            dw_ref[...] = jnp.zeros_like(dw_ref)

        dw_ref[...] += dw

    return pl.pallas_call(
        body, name=name, grid=(t // tm,),
        in_specs=[_row(tm, d), _fixed((1, d)), _row(tm, d), _row(tm, d)],
        out_specs=[_row(tm, d), _fixed((1, d))],
        out_shape=[_sds((t, d)), _sds((1, d))], compiler_params=_cp("arbitrary"))(x, w, dy, dres)


def _loss_head(x, w, target, tm):
    t, d = x.shape

    def body(x_ref, w_ref, t_ref, dx_ref, dw_ref, loss_ref):
        xv, wv = x_ref[...], w_ref[...]
        rstd = lax.rsqrt(jnp.mean(xv * xv, axis=-1, keepdims=True) + RMS_EPS)
        err = xv * rstd * wv - t_ref[...]
        dx, dw = _rms_bwd_math(xv, wv, err * (1.0 / d))
        dx_ref[...] = dx

        @pl.when(pl.program_id(0) == 0)
        def _():
            dw_ref[...] = jnp.zeros_like(dw_ref)
            loss_ref[...] = jnp.zeros_like(loss_ref)

        dw_ref[...] += dw
        loss_ref[...] += 0.5 * jnp.sum(jnp.mean(err * err, axis=-1, keepdims=True), axis=0, keepdims=True)

    return pl.pallas_call(
        body, name="loss_head", grid=(t // tm,),
        in_specs=[_row(tm, d), _fixed((1, d)), _row(tm, d)],
        out_specs=[_row(tm, d), _fixed((1, d)), _fixed((1, 1))],
        out_shape=[_sds((t, d)), _sds((1, d)), _sds((1, 1))], compiler_params=_cp("arbitrary"))(x, w, target)


def _swiglu_fwd(gate, up, tm):
    t, f = gate.shape

    def body(g_ref, u_ref, o_ref):
        gv = g_ref[...]
        o_ref[...] = (gv * jax.nn.sigmoid(gv) * u_ref[...]).astype(BF16)

    return pl.pallas_call(
        body, name="swiglu_fwd", grid=(t // tm,), in_specs=[_row(tm, f), _row(tm, f)], out_specs=_row(tm, f),
        out_shape=_sds((t, f), BF16), compiler_params=_cp("parallel"))(gate, up)


def _swiglu_bwd(gate, up, dact, tm):
    t, f = gate.shape

    def body(g_ref, u_ref, d_ref, dg_ref, du_ref):
        gv, uv, dv = g_ref[...], u_ref[...], d_ref[...]
        sg = jax.nn.sigmoid(gv)
        du_ref[...] = (dv * gv * sg).astype(BF16)
        dg_ref[...] = (dv * uv * (sg * (1.0 + gv * (1.0 - sg)))).astype(BF16)

    return pl.pallas_call(
        body, name="swiglu_bwd", grid=(t // tm,), in_specs=[_row(tm, f)] * 3, out_specs=[_row(tm, f)] * 2,
        out_shape=[_sds((t, f), BF16)] * 2, compiler_params=_cp("parallel"))(gate, up, dact)


def _halo_specs(tm, width, rows_total):
    per = tm // GROUP
    last = rows_total // GROUP - 1
    prev = pl.BlockSpec((GROUP, width), lambda i: (jnp.maximum(i * per - 1, 0), 0))
    nxt = pl.BlockSpec((GROUP, width), lambda i: (jnp.minimum((i + 1) * per, last), 0))
    return prev, nxt


def _edge_flags(tm, seq):
    i = pl.program_id(0)
    has_prev = jnp.where((i * tm) % seq == 0, 0.0, 1.0).astype(F32)
    has_next = jnp.where(((i + 1) * tm) % seq == 0, 0.0, 1.0).astype(F32)
    return has_prev, has_next


def _shifted(xv, prev_row, next_row):
    tm = xv.shape[0]
    row = lax.broadcasted_iota(jnp.int32, xv.shape, 0)
    down = jnp.where(row == 0, prev_row, pltpu.roll(xv, 1, axis=0))
    up = jnp.where(row == tm - 1, next_row, pltpu.roll(xv, tm - 1, axis=0))
    return down, up


def _shift_fwd(p, mu, tm, seq):
    t, w = p.shape
    prev_spec, next_spec = _halo_specs(tm, w, t)

    def body(p_ref, hp_ref, hn_ref, mu_ref, o_ref):
        has_prev, has_next = _edge_flags(tm, seq)
        xv = p_ref[...]
        down, up = _shifted(xv, hp_ref[GROUP - 1:GROUP, :] * has_prev, hn_ref[0:1, :] * has_next)
        o_ref[...] = xv + mu_ref[0:1, :] * (0.5 * (down + up) - xv)

    return pl.pallas_call(
        body, name="shift_fwd", grid=(t // tm,),
        in_specs=[_row(tm, w), prev_spec, next_spec, _fixed((GROUP, w))], out_specs=_row(tm, w),
        out_shape=_sds((t, w)), compiler_params=_cp("parallel"))(p, p, p, mu)


def _shift_bwd(q, p, mu, tm, seq):
    t, w = p.shape
    prev_spec, next_spec = _halo_specs(tm, w, t)

    def body(q_ref, qp_ref, qn_ref, p_ref, pp_ref, pn_ref, mu_ref, dp_ref, dmu_ref):
        has_prev, has_next = _edge_flags(tm, seq)
        muv = mu_ref[0:1, :]
        qv = q_ref[...]
        mq = muv * qv
        mq_down, mq_up = _shifted(mq, muv * qp_ref[GROUP - 1:GROUP, :] * has_prev,
                                  muv * qn_ref[0:1, :] * has_next)
        dp_ref[...] = (qv - mq + 0.5 * (mq_down + mq_up)).astype(BF16)
        pv = p_ref[...]
        p_down, p_up = _shifted(pv, pp_ref[GROUP - 1:GROUP, :] * has_prev, pn_ref[0:1, :] * has_next)

        @pl.when(pl.program_id(0) == 0)
        def _():
            dmu_ref[...] = jnp.zeros_like(dmu_ref)

        dmu_ref[...] += jnp.sum(qv * (0.5 * (p_down + p_up) - pv), axis=0, keepdims=True)

    return pl.pallas_call(
        body, name="shift_bwd", grid=(t // tm,),
        in_specs=[_row(tm, w), prev_spec, next_spec, _row(tm, w), prev_spec, next_spec, _fixed((GROUP, w))],
        out_specs=[_row(tm, w), _fixed((1, w))],
        out_shape=[_sds((t, w), BF16), _sds((1, w))], compiler_params=_cp("arbitrary"))(q, q, q, p, p, p, mu)


@jax.custom_vjp
def _bdot(a, b):
    return jnp.dot(a.astype(BF16), b.astype(BF16), preferred_element_type=F32)


def _bdot_fwd(a, b):
    return _bdot(a, b), (a, b)


def _bdot_bwd(res, g):
    a, b = res
    gb = g.astype(BF16)
    da = lax.dot_general(gb, b.astype(BF16), (((1,), (1,)), ((), ())), preferred_element_type=F32)
    db = lax.dot_general(a.astype(BF16), gb, (((0,), (0,)), ((), ())), preferred_element_type=F32)
    return da, db


_bdot.defvjp(_bdot_fwd, _bdot_bwd)


def _seg_raw(x, ones_blocks):
    hi = x.astype(BF16)
    lo = (x - hi.astype(F32)).astype(BF16)
    return (jnp.dot(hi, ones_blocks, preferred_element_type=F32)
            + jnp.dot(lo, ones_blocks, preferred_element_type=F32))


@jax.custom_vjp
def _seg(x, ones_blocks):
    return _seg_raw(x, ones_blocks)


def _seg_fwd(x, ones_blocks):
    return _seg_raw(x, ones_blocks), ones_blocks


def _seg_bwd(ones_blocks, g):
    return _seg_raw(g, ones_blocks), jnp.zeros_like(ones_blocks)


_seg.defvjp(_seg_fwd, _seg_bwd)


def _head_ones():
    h = jnp.arange(D_RWKV) // HEAD
    return (h[:, None] == h[None, :]).astype(BF16)


def _prep_math(ps, k_k, w0_f, a0_f, k_a_f, w0_b, a0_b, k_a_b, wup_f, aup_f, wup_b, aup_b, gup, ones_blocks):
    r = ps[:, 0:512]
    k = ps[:, 512:1024]
    v = ps[:, 1024:1536]
    xwa = ps[:, 1536:1664]
    xg = ps[:, 1664:D_SHIFT_PAD]
    kk_raw = k * k_k
    norm = jnp.sqrt(_seg(kk_raw * kk_raw, ones_blocks))
    kk = kk_raw / jnp.maximum(norm, NORM_EPS)
    t_xwa = jnp.tanh(xwa)
    outs = [r, v, kk]
    for w0, a0, k_a, wup, aup in ((w0_f, a0_f, k_a_f, wup_f, aup_f), (w0_b, a0_b, k_a_b, wup_b, aup_b)):
        decay = jnp.exp(-LOG_DECAY_SCALE * jax.nn.sigmoid(w0 + _bdot(t_xwa, wup)))
        rate = jax.nn.sigmoid(a0 + _bdot(xwa, aup))
        outs += [decay, k * (1.0 + (rate - 1.0) * k_a), kk * rate]
    outs.append(_bdot(jax.nn.sigmoid(xg), gup))
    return tuple(outs)


def _prep_args(tm, ps_ref, pv_ref, mat_refs, ones_ref):
    vecs = [jnp.broadcast_to(pv_ref[j:j + 1, :], (tm, D_RWKV)) for j in range(7)]
    return [ps_ref[...]] + vecs + [m[...] for m in mat_refs] + [ones_ref[...]]


_PREP_MAT_SHAPES = ((128, D_RWKV),) * 4 + ((D_GATE_PAD, D_RWKV),)


def _prep_fwd(ps, pvec, mats, ones_blocks, tm):
    t = ps.shape[0]

    def body(ps_ref, pv_ref, m0, m1, m2, m3, m4, ones_ref, *out_refs):
        outs = _prep_math(*_prep_args(tm, ps_ref, pv_ref, (m0, m1, m2, m3, m4), ones_ref))
        for o_ref, val in zip(out_refs, outs[2:]):
            o_ref[...] = val

    return pl.pallas_call(
        body, name="prep_fwd", grid=(t // tm,),
        in_specs=[_row(tm, D_SHIFT_PAD), _fixed((8, D_RWKV))] + [_fixed(s) for s in _PREP_MAT_SHAPES]
        + [_fixed((D_RWKV, D_RWKV))],
        out_specs=[_row(tm, D_RWKV)] * 8, out_shape=[_sds((t, D_RWKV))] * 8,
        compiler_params=_cp("parallel"))(ps, pvec, *mats, ones_blocks)


def _prep_bwd(ps, pvec, mats, ones_blocks, cts, tm):
    t = ps.shape[0]
    counts = [len(c) for c in cts]
    flat = [a for c in cts for a in c]

    def body(ps_ref, pv_ref, m0, m1, m2, m3, m4, ones_ref, *refs):
        ct_refs = refs[:len(flat)]
        q_ref, dpv_ref = refs[len(flat)], refs[len(flat) + 1]
        dmat_refs = refs[len(flat) + 2:]
        args = _prep_args(tm, ps_ref, pv_ref, (m0, m1, m2, m3, m4), ones_ref)
        _, vjp = jax.vjp(lambda *a: _prep_math(*a, args[-1]), *args[:-1])
        ct_vals, pos = [], 0
        for n in counts:
            val = ct_refs[pos][...]
            for extra in ct_refs[pos + 1:pos + n]:
                val = val + extra[...]
            ct_vals.append(val)
            pos += n
        grads = vjp(tuple(ct_vals))
        q_ref[...] = grads[0]

        @pl.when(pl.program_id(0) == 0)
        def _():
            dpv_ref[...] = jnp.zeros_like(dpv_ref)
            for d_ref in dmat_refs:
                d_ref[...] = jnp.zeros_like(d_ref)

        for j in range(7):
            dpv_ref[j:j + 1, :] += jnp.sum(grads[1 + j], axis=0, keepdims=True)
        for d_ref, gm in zip(dmat_refs, grads[8:13]):
            d_ref[...] += gm

    return pl.pallas_call(
        body, name="prep_bwd", grid=(t // tm,),
        in_specs=[_row(tm, D_SHIFT_PAD), _fixed((8, D_RWKV))] + [_fixed(s) for s in _PREP_MAT_SHAPES]
        + [_fixed((D_RWKV, D_RWKV))] + [_row(tm, D_RWKV)] * len(flat),
        out_specs=[_row(tm, D_SHIFT_PAD), _fixed((8, D_RWKV))] + [_fixed(s) for s in _PREP_MAT_SHAPES],
        out_shape=[_sds((t, D_SHIFT_PAD)), _sds((8, D_RWKV))] + [_sds(s) for s in _PREP_MAT_SHAPES],
        compiler_params=_cp("arbitrary"))(ps, pvec, *mats, ones_blocks, *flat)


def _pair_ones():
    h = jnp.arange(2 * HEAD) // HEAD
    block = (h[:, None] == h[None, :]).astype(BF16)
    return jnp.concatenate([block, block], axis=0)


def _diag_mask():
    lane = lax.broadcasted_iota(jnp.int32, (HEAD, 2 * HEAD), 1)
    sub = lax.broadcasted_iota(jnp.int32, (HEAD, 2 * HEAD), 0)
    return jnp.where((lane & (HEAD - 1)) == sub, 1.0, 0.0).astype(F32)


def _to_row(cols, dmask):
    return jnp.sum(cols * dmask, axis=0, keepdims=True)


def _seg_many(exact, rounded, ones_pair):
    out_exact, out_rounded = [], []
    if exact:
        parts = []
        for x in exact:
            hi = x.astype(BF16)
            parts.append(jnp.concatenate([hi, (x - hi.astype(F32)).astype(BF16)], axis=1))
        res = jnp.dot(jnp.concatenate(parts, axis=0), ones_pair, preferred_element_type=F32)
        out_exact = [res[HEAD * c:HEAD * (c + 1)] for c in range(len(exact))]
    if rounded:
        res = jnp.dot(jnp.concatenate([x.astype(BF16) for x in rounded], axis=0), ones_pair[0:2 * HEAD],
                      preferred_element_type=F32)
        out_rounded = [res[HEAD * c:HEAD * (c + 1)] for c in range(len(rounded))]
    return out_exact, out_rounded


N_CHAIN = 2 * N_PAIR


def _chain(c):
    d, p = divmod(c, N_PAIR)
    return d, slice(2 * HEAD * p, 2 * HEAD * (p + 1))


def _scan_specs(n_chunks, col_blocks, fwd_chunk, bwd_chunk):
    def spec(chunk_of, col):
        return pl.BlockSpec((SCAN_CHUNK, D_RWKV), lambda b, g: (b * n_chunks + chunk_of(g), col))
    return [spec(fwd_chunk, c) for c in col_blocks] + [spec(bwd_chunk, c) for c in col_blocks]


def _scan_fwd(ps, kk, dirs, batch, seq):
    t = batch * seq
    n = seq // SCAN_CHUNK
    groups = SCAN_CHUNK // GROUP
    up = lambda g: g
    down = lambda g: n - 1 - g
    col_blocks = (0, 2, 0, 0, 0, 0)

    def body(*refs):
        dir_refs = (refs[0:6], refs[6:12])
        ones_ref = refs[12]
        y_refs, hist_refs, st_ref = refs[13:15], refs[15:17], refs[17]

        @pl.when(pl.program_id(1) == 0)
        def _():
            st_ref[...] = jnp.zeros_like(st_ref)

        ones_pair = ones_ref[...]
        dmask = _diag_mask()
        sub8 = lax.broadcasted_iota(jnp.int32, (GROUP, 2 * HEAD), 0)

        def group(gi, carry):
            off = (pl.multiple_of(gi * GROUP, GROUP), pl.multiple_of((groups - 1 - gi) * GROUP, GROUP))
            loaded = [tuple(ref[pl.ds(off[d], GROUP), :] for ref in dir_refs[d]) for d in range(2)]
            states = list(carry)
            y_acc = [jnp.zeros((GROUP, 2 * HEAD), F32) for _ in range(N_CHAIN)]
            for step in range(GROUP):
                rows, idx = [], []
                for c in range(N_CHAIN):
                    d, lanes = _chain(c)
                    i = step if d == 0 else GROUP - 1 - step
                    idx.append(i)
                    rows.append(tuple(x8[i:i + 1, lanes] for x8 in loaded[d]))
                    hist_refs[d][c % N_PAIR, gi * GROUP + step] = states[c]
                _, v_cols = _seg_many([], [dmask * rows[c][1] for c in range(N_CHAIN)], ones_pair)
                sas, _ = _seg_many([states[c] * rows[c][2] for c in range(N_CHAIN)], [], ones_pair)
                for c in range(N_CHAIN):
                    _, _, _, w_row, kd_row, b_row = rows[c]
                    states[c] = states[c] * w_row - sas[c] * b_row + v_cols[c] * kd_row
                _, ys = _seg_many([], [states[c] * rows[c][0] for c in range(N_CHAIN)], ones_pair)
                for c in range(N_CHAIN):
                    y_acc[c] = jnp.where(sub8 == idx[c], _to_row(ys[c], dmask), y_acc[c])
            for c in range(N_CHAIN):
                d, lanes = _chain(c)
                y_refs[d][pl.ds(off[d], GROUP), lanes] = y_acc[c]
            return tuple(states)

        final = lax.fori_loop(0, groups, group, tuple(st_ref[c] for c in range(N_CHAIN)))
        for c in range(N_CHAIN):
            st_ref[c] = final[c]
            hist_refs[c // N_PAIR][c % N_PAIR, SCAN_CHUNK] = final[c]

    y_spec_f = pl.BlockSpec((SCAN_CHUNK, D_RWKV), lambda b, g: (b * n + up(g), 0))
    y_spec_b = pl.BlockSpec((SCAN_CHUNK, D_RWKV), lambda b, g: (b * n + down(g), 0))
    hist_shape = (batch, n, N_PAIR, SCAN_CHUNK + 1, HEAD, 2 * HEAD)
    hist_block = (None, None, N_PAIR, SCAN_CHUNK + 1, HEAD, 2 * HEAD)
    hist_spec_f = pl.BlockSpec(hist_block, lambda b, g: (b, up(g), 0, 0, 0, 0))
    hist_spec_b = pl.BlockSpec(hist_block, lambda b, g: (b, down(g), 0, 0, 0, 0))
    ones_spec = pl.BlockSpec((4 * HEAD, 2 * HEAD), lambda b, g: (0, 0))
    (wf, kdf, bf), (wb, kdb, bb) = dirs
    return pl.pallas_call(
        body, name="wkv_fwd", grid=(batch, n),
        in_specs=_scan_specs(n, col_blocks, up, down) + [ones_spec],
        out_specs=[y_spec_f, y_spec_b, hist_spec_f, hist_spec_b],
        out_shape=[_sds((t, D_RWKV)), _sds((t, D_RWKV)), _sds(hist_shape), _sds(hist_shape)],
        scratch_shapes=[pltpu.VMEM((N_CHAIN, HEAD, 2 * HEAD), F32)],
        compiler_params=_cp("parallel", "arbitrary"),
    )(ps, ps, kk, wf, kdf, bf, ps, ps, kk, wb, kdb, bb, _pair_ones())


def _scan_bwd(ps, kk, dirs, dy, hist_f, hist_b, batch, seq):
    t = batch * seq
    n = seq // SCAN_CHUNK
    groups = SCAN_CHUNK // GROUP
    fwd_chunk = lambda g: n - 1 - g
    bwd_chunk = lambda g: g
    col_blocks = (0, 2, 0, 0, 0, 0, 0)

    def undo_group(dir_refs, out_refs, hist_refs, gi, d_states, ones_pair, dmask, sub8):
        d_states = list(d_states)
        loaded, blocks = [], []
        for d in range(2):
            blk = groups - 1 - gi if d == 0 else gi
            blocks.append(pl.ds(pl.multiple_of(blk * GROUP, GROUP), GROUP))
            loaded.append(tuple(ref[blocks[d], :] for ref in dir_refs[d]))
        acc = [[jnp.zeros((GROUP, 2 * HEAD), F32) for _ in range(6)] for _ in range(N_CHAIN)]
        for step in range(GROUP):
            rows, idx, before, after = [], [], [], []
            for c in range(N_CHAIN):
                d, lanes = _chain(c)
                i = GROUP - 1 - step if d == 0 else step
                q = (groups - 1 - gi) * GROUP + i if d == 0 else SCAN_CHUNK - 1 - (gi * GROUP + i)
                idx.append(i)
                rows.append(tuple(x8[i:i + 1, lanes] for x8 in loaded[d]))
                before.append(hist_refs[d][c % N_PAIR, q])
                after.append(hist_refs[d][c % N_PAIR, q + 1])
            _, cols = _seg_many([], [dmask * rows[c][j] for c in range(N_CHAIN) for j in (1, 6)], ones_pair)
            v_cols, dy_cols = cols[0::2], cols[1::2]
            d_now = [d_states[c] + dy_cols[c] * rows[c][0] for c in range(N_CHAIN)]
            d_sas, _ = _seg_many([d_now[c] * rows[c][5] for c in range(N_CHAIN)], [], ones_pair)
            _, others = _seg_many(
                [], [x for c in range(N_CHAIN) for x in (before[c] * rows[c][2], d_now[c] * rows[c][4])], ones_pair)
            for c in range(N_CHAIN):
                sa, d_sa, dv_cols = others[2 * c], -d_sas[c], others[2 * c + 1]
                rows_out = (
                    jnp.sum(after[c] * dy_cols[c], axis=0, keepdims=True),
                    jnp.sum(d_now[c] * before[c], axis=0, keepdims=True),
                    jnp.sum(d_now[c] * v_cols[c], axis=0, keepdims=True),
                    _to_row(dv_cols, dmask),
                    jnp.sum(before[c] * d_sa, axis=0, keepdims=True),
                    -jnp.sum(d_now[c] * sa, axis=0, keepdims=True),
                )
                acc[c] = [jnp.where(sub8 == idx[c], val, a) for val, a in zip(rows_out, acc[c])]
                d_states[c] = d_now[c] * rows[c][3] + d_sa * rows[c][2]
        for c in range(N_CHAIN):
            d, lanes = _chain(c)
            for o_ref, val in zip(out_refs[d], acc[c]):
                o_ref[blocks[d], lanes] = val
        return tuple(d_states)

    def body(*refs):
        dir_refs = (refs[0:7], refs[7:14])
        hist_refs, ones_ref = refs[14:16], refs[16]
        out_refs = (refs[17:23], refs[23:29])
        dst_ref = refs[29]

        @pl.when(pl.program_id(1) == 0)
        def _():
            dst_ref[...] = jnp.zeros_like(dst_ref)

        ones_pair = ones_ref[...]
        dmask = _diag_mask()
        sub8 = lax.broadcasted_iota(jnp.int32, (GROUP, 2 * HEAD), 0)

        def group(gi, carry):
            return undo_group(dir_refs, out_refs, hist_refs, gi, carry, ones_pair, dmask, sub8)

        final = lax.fori_loop(0, groups, group, tuple(dst_ref[c] for c in range(N_CHAIN)))
        for c in range(N_CHAIN):
            dst_ref[c] = final[c]

    blk = (SCAN_CHUNK, D_RWKV)
    out_f = pl.BlockSpec(blk, lambda b, g: (b * n + fwd_chunk(g), 0))
    out_b = pl.BlockSpec(blk, lambda b, g: (b * n + bwd_chunk(g), 0))
    hist_block = (None, None, N_PAIR, SCAN_CHUNK + 1, HEAD, 2 * HEAD)
    hist_spec_f = pl.BlockSpec(hist_block, lambda b, g: (b, fwd_chunk(g), 0, 0, 0, 0))
    hist_spec_b = pl.BlockSpec(hist_block, lambda b, g: (b, bwd_chunk(g), 0, 0, 0, 0))
    ones_spec = pl.BlockSpec((4 * HEAD, 2 * HEAD), lambda b, g: (0, 0))
    (wf, kdf, bf), (wb, kdb, bb) = dirs
    outs = pl.pallas_call(
        body, name="wkv_bwd", grid=(batch, n),
        in_specs=_scan_specs(n, col_blocks, fwd_chunk, bwd_chunk) + [hist_spec_f, hist_spec_b, ones_spec],
        out_specs=[out_f] * 6 + [out_b] * 6,
        out_shape=[_sds((t, D_RWKV))] * 12,
        scratch_shapes=[pltpu.VMEM((N_CHAIN, HEAD, 2 * HEAD), F32)],
        compiler_params=_cp("parallel", "arbitrary"),
    )(ps, ps, kk, wf, kdf, bf, dy, ps, ps, kk, wb, kdb, bb, dy, hist_f, hist_b, _pair_ones())
    return outs[0:6], outs[6:12]


def _post_math(y, r, kd_f, kd_b, v, gate, gn_w, gn_b, rk_f, rk_b, ones_blocks):
    mean = _seg(y, ones_blocks) * (1.0 / HEAD)
    yc = y - mean
    var = _seg(yc * yc, ones_blocks) * (1.0 / HEAD)
    yn = yc * lax.rsqrt(var + GN_EPS) * gn_w + gn_b
    bonus = _seg(r * kd_f * rk_f, ones_blocks) * v + _seg(r * kd_b * rk_b, ones_blocks) * v
    return (yn + bonus) * gate


def _conv_parts(pc, halo_prev, halo_next, has_prev, has_next):
    gate_b, gate_c, hid = pc[:, 0:512], pc[:, 512:1024], pc[:, 1024:1536]
    u = gate_c * hid
    u_prev_row = halo_prev[GROUP - 1:GROUP, 512:1024] * halo_prev[GROUP - 1:GROUP, 1024:1536] * has_prev
    u_next_row = halo_next[0:1, 512:1024] * halo_next[0:1, 1024:1536] * has_next
    u_down, u_up = _shifted(u, u_prev_row, u_next_row)
    return gate_b, gate_c, hid, u, u_down, u_up


def _post_specs(tm, t):
    pc_prev, pc_next = _halo_specs(tm, D_CONV3, t)
    col = lambda c: pl.BlockSpec((tm, D_RWKV), lambda i: (i, c))
    return ([col(0), col(0), col(0), col(0), col(0), col(2), col(0), _row(tm, D_CONV3), pc_prev, pc_next,
             _fixed((8, D_RWKV)), _fixed((D_RWKV, D_RWKV))])


def _post_fwd(y_f, y_b, ps, kd_f, kd_b, gate, pc, qvec, ones_blocks, tm, seq):
    t = ps.shape[0]

    def body(yf_ref, yb_ref, r_ref, kdf_ref, kdb_ref, v_ref, g_ref, pc_ref, hp_ref, hn_ref, qv_ref, ones_ref,
             o_ref):
        has_prev, has_next = _edge_flags(tm, seq)
        vec = [jnp.broadcast_to(qv_ref[j:j + 1, :], (tm, D_RWKV)) for j in range(7)]
        o_ref[:, 0:D_RWKV] = _post_math(yf_ref[...] + yb_ref[...], r_ref[...], kdf_ref[...], kdb_ref[...],
                                        v_ref[...], g_ref[...], vec[0], vec[1], vec[2], vec[3],
                                        ones_ref[...]).astype(BF16)
        gate_b, _, _, u, u_down, u_up = _conv_parts(pc_ref[...], hp_ref[...], hn_ref[...], has_prev, has_next)
        o_ref[:, D_RWKV:2 * D_RWKV] = (gate_b * (vec[4] * u_down + vec[5] * u + vec[6] * u_up)).astype(BF16)

    return pl.pallas_call(
        body, name="post_fwd", grid=(t // tm,), in_specs=_post_specs(tm, t), out_specs=_row(tm, D_MODEL),
        out_shape=_sds((t, D_MODEL), BF16), compiler_params=_cp("parallel"),
    )(y_f, y_b, ps, kd_f, kd_b, ps, gate, pc, pc, pc, qvec, ones_blocks)


def _post_bwd(d_out, y_f, y_b, ps, kd_f, kd_b, gate, pc, qvec, ones_blocks, tm, seq):
    t = ps.shape[0]
    do_prev, do_next = _halo_specs(tm, D_MODEL, t)

    def body(do_ref, dop_ref, don_ref, yf_ref, yb_ref, r_ref, kdf_ref, kdb_ref, v_ref, g_ref, pc_ref, hp_ref,
             hn_ref, qv_ref, ones_ref, dy_ref, dr_ref, dkdf_ref, dkdb_ref, dv_ref, dg_ref, dpc_ref, dqv_ref):
        has_prev, has_next = _edge_flags(tm, seq)
        vec = [jnp.broadcast_to(qv_ref[j:j + 1, :], (tm, D_RWKV)) for j in range(7)]
        ones_v = ones_ref[...]
        args = (yf_ref[...] + yb_ref[...], r_ref[...], kdf_ref[...], kdb_ref[...], v_ref[...], g_ref[...],
                vec[0], vec[1], vec[2], vec[3])
        _, vjp = jax.vjp(lambda *a: _post_math(*a, ones_v), *args)
        grads = vjp(do_ref[:, 0:D_RWKV])
        for o_ref, gval in zip((dy_ref, dr_ref, dkdf_ref, dkdb_ref, dv_ref, dg_ref), grads[0:6]):
            o_ref[...] = gval

        hp, hn = hp_ref[...], hn_ref[...]
        gate_b, gate_c, hid, u, u_down, u_up = _conv_parts(pc_ref[...], hp, hn, has_prev, has_next)
        d_oc = do_ref[:, D_RWKV:2 * D_RWKV]
        d_cu = d_oc * gate_b
        d_cu_prev = dop_ref[GROUP - 1:GROUP, D_RWKV:2 * D_RWKV] * hp[GROUP - 1:GROUP, 0:512] * has_prev
        d_cu_next = don_ref[0:1, D_RWKV:2 * D_RWKV] * hn[0:1, 0:512] * has_next
        d_cu_down, d_cu_up = _shifted(d_cu, d_cu_prev, d_cu_next)
        d_u = vec[5] * d_cu + vec[4] * d_cu_up + vec[6] * d_cu_down
        dpc_ref[:, 0:512] = (d_oc * (vec[4] * u_down + vec[5] * u + vec[6] * u_up)).astype(BF16)
        dpc_ref[:, 512:1024] = (d_u * hid).astype(BF16)
        dpc_ref[:, 1024:1536] = (d_u * gate_c).astype(BF16)

        @pl.when(pl.program_id(0) == 0)
        def _():
            dqv_ref[...] = jnp.zeros_like(dqv_ref)

        vec_grads = list(grads[6:10]) + [d_cu * u_down, d_cu * u, d_cu * u_up]
        for j, gval in enumerate(vec_grads):
            dqv_ref[j:j + 1, :] += jnp.sum(gval, axis=0, keepdims=True)

    return pl.pallas_call(
        body, name="post_bwd", grid=(t // tm,),
        in_specs=[_row(tm, D_MODEL), do_prev, do_next] + _post_specs(tm, t),
        out_specs=[_row(tm, D_RWKV)] * 6 + [_row(tm, D_CONV3), _fixed((8, D_RWKV))],
        out_shape=[_sds((t, D_RWKV))] * 6 + [_sds((t, D_CONV3), BF16), _sds((8, D_RWKV))],
        compiler_params=_cp("arbitrary"),
    )(d_out, d_out, d_out, y_f, y_b, ps, kd_f, kd_b, ps, gate, pc, pc, pc, qvec, ones_blocks)


def _adamw(w, g, m, v, name):
    r, c = w.shape[-2:]
    tr = _tile(r, 256, 8)
    if w.ndim == 3:
        spec = pl.BlockSpec((None, tr, c), lambda i: (0, i, 0))
    else:
        spec = pl.BlockSpec((tr, c), lambda i: (i, 0))

    def body(w_ref, g_ref, m_ref, v_ref, d_ref, nm_ref, nv_ref):
        gv = g_ref[...]
        m2 = ADAM_B1 * m_ref[...] + (1.0 - ADAM_B1) * gv
        v2 = ADAM_B2 * v_ref[...] + (1.0 - ADAM_B2) * (gv * gv)
        m_hat = m2 / (1.0 - ADAM_B1 ** ADAM_STEP)
        v_hat = v2 / (1.0 - ADAM_B2 ** ADAM_STEP)
        d_ref[...] = -ADAM_LR * (m_hat / (jnp.sqrt(v_hat) + ADAM_EPS) + ADAM_WD * w_ref[...])
        nm_ref[...] = m2
        nv_ref[...] = v2

    return pl.pallas_call(
        body, name=name, grid=(r // tr,), in_specs=[spec] * 4, out_specs=[spec] * 3,
        out_shape=[_sds(w.shape)] * 3, compiler_params=_cp("parallel"))(w, g, m, v)


_ANY = pl.BlockSpec(memory_space=pl.ANY)


def _place():
    return lax.axis_index("x"), lax.axis_index("y"), lax.axis_index("c")


def _other_chips(x, y):
    return [(1 - x, y), (x, 1 - y), (1 - x, 1 - y)]


def _remote(src, dst, send_sems, recv_sems, k, to):
    return pltpu.make_async_remote_copy(src_ref=src, dst_ref=dst, send_sem=send_sems.at[k],
                                        recv_sem=recv_sems.at[k], device_id=to, device_id_type=MESH)


def _gather_weights(pack):
    rows, width = pack.shape
    half = rows // 2

    def body(x_ref, out_ref, send_sems, recv_sems):
        x, y, c = _place()
        sibling = (x, y, 1 - c)
        chips = _other_chips(x, y)

        def block(chip, part):
            return out_ref.at[2 * chip[0] + chip[1], pl.ds(part * half, half), :]

        first = [_remote(x_ref.at[pl.ds(c * half, half), :], block((x, y), c), send_sems, recv_sems, j, (*chip, c))
                 for j, chip in enumerate(chips)]
        for cp in first:
            cp.start()
        passed = [_remote(block(chip, c), block(chip, c), send_sems, recv_sems, 3 + j, sibling)
                  for j, chip in enumerate(chips)]
        for j, chip in enumerate(chips):
            _remote(block(chip, c), block(chip, c), send_sems, recv_sems, j, sibling).wait_recv()
            passed[j].start()
        for j, chip in enumerate(chips):
            _remote(block(chip, 1 - c), block(chip, 1 - c), send_sems, recv_sems, 3 + j, sibling).wait_recv()
        for cp in first + passed:
            cp.wait_send()

    return pl.pallas_call(
        body, name="gather_weights", in_specs=[_ANY], out_specs=_ANY,
        out_shape=_sds((N_SHARD, rows, width), pack.dtype),
        scratch_shapes=[pltpu.SemaphoreType.DMA((6,)), pltpu.SemaphoreType.DMA((6,))],
    )(pack)


def _swap_with_sibling(block, name):
    def body(x_ref, out_ref, send_sems, recv_sems):
        x, y, c = _place()
        cp = _remote(x_ref, out_ref, send_sems, recv_sems, 0, (x, y, 1 - c))
        cp.start()
        cp.wait()

    return pl.pallas_call(
        body, name=name, in_specs=[_ANY], out_specs=_ANY, out_shape=_sds(block.shape, block.dtype),
        scratch_shapes=[pltpu.SemaphoreType.DMA((1,)), pltpu.SemaphoreType.DMA((1,))],
    )(block)


def _swap_other_half(packed):
    slots, rows, width = packed.shape
    half = rows // 2

    def body(x_ref, out_ref, send_sems, recv_sems):
        x, y, c = _place()
        cp = _remote(x_ref.at[:, pl.ds((1 - c) * half, half), :], out_ref, send_sems, recv_sems, 0, (x, y, 1 - c))
        cp.start()
        cp.wait()

    return pl.pallas_call(
        body, name="swap_halves", in_specs=[_ANY], out_specs=_ANY, out_shape=_sds((slots, half, width)),
        scratch_shapes=[pltpu.SemaphoreType.DMA((1,)), pltpu.SemaphoreType.DMA((1,))],
    )(packed)


def _add_halves(packed, got, c):
    slots, rows, width = packed.shape
    half = rows // 2
    tr = _tile(half, 408, 16)
    per = half // tr
    block = (None, tr, width)

    def body(c_ref, mine_ref, got_ref, sum_ref, sum16_ref):
        acc = mine_ref[...] + got_ref[...]
        sum_ref[...] = acc
        sum16_ref[...] = acc.astype(BF16)

    plain = pl.BlockSpec(block, lambda s, i, c_ref: (s, i, 0))
    grid_spec = pltpu.PrefetchScalarGridSpec(
        num_scalar_prefetch=1, grid=(slots, per),
        in_specs=[pl.BlockSpec(block, lambda s, i, c_ref: (s, c_ref[0] * per + i, 0)), plain],
        out_specs=[plain, plain])
    return pl.pallas_call(
        body, name="add_halves", grid_spec=grid_spec,
        out_shape=[_sds((slots, half, width)), _sds((slots, half, width), BF16)],
        compiler_params=_cp("parallel", "parallel"))(c.reshape(1).astype(jnp.int32), packed, got)


def _add_quarters(chip_sum, others, chip):
    _, rows, width = chip_sum.shape
    tr = _tile(rows, 408, 16)

    def body(chip_ref, own_ref, others_ref, o_ref):
        acc = own_ref[...]
        for j in range(3):
            acc = acc + others_ref[j].astype(F32)
        o_ref[...] = acc

    grid_spec = pltpu.PrefetchScalarGridSpec(
        num_scalar_prefetch=1, grid=(rows // tr,),
        in_specs=[pl.BlockSpec((None, tr, width), lambda i, chip_ref: (chip_ref[0], i, 0)),
                  pl.BlockSpec((3, tr, width), lambda i, chip_ref: (0, i, 0))],
        out_specs=pl.BlockSpec((tr, width), lambda i, chip_ref: (i, 0)))
    return pl.pallas_call(
        body, name="add_quarters", grid_spec=grid_spec, out_shape=_sds((rows, width)),
        compiler_params=_cp("parallel"))(chip.reshape(1).astype(jnp.int32), chip_sum, others)


def _exchange_quarters(parts):
    _, rows, width = parts.shape

    def body(x_ref, out_ref, send_sems, recv_sems):
        x, y, c = _place()
        copies = [_remote(x_ref.at[2 * chip[0] + chip[1]], out_ref.at[j], send_sems, recv_sems, j, (*chip, c))
                  for j, chip in enumerate(_other_chips(x, y))]
        for cp in copies:
            cp.start()
        for cp in copies:
            cp.wait()

    return pl.pallas_call(
        body, name="exchange_quarters", in_specs=[_ANY], out_specs=_ANY,
        out_shape=_sds((3, rows, width), parts.dtype),
        scratch_shapes=[pltpu.SemaphoreType.DMA((3,)), pltpu.SemaphoreType.DMA((3,))],
    )(parts)


def _allreduce_small(vec):
    rows, width = vec.shape
    vmem = pl.BlockSpec(memory_space=pltpu.VMEM)

    def body(x_ref, o_ref, buf_ref, send_sems, recv_sems):
        x, y, c = _place()
        me = 4 * x + 2 * y + c
        buf_ref[me] = x_ref[...]
        copies = []
        for k in range(1, N_DEV):
            peer = (x ^ ((k >> 2) & 1), y ^ ((k >> 1) & 1), c ^ (k & 1))
            copies.append(_remote(x_ref, buf_ref.at[me], send_sems, recv_sems, k - 1, peer))
        for cp in copies:
            cp.start()
        for k in range(1, N_DEV):
            _remote(x_ref, buf_ref.at[me ^ k], send_sems, recv_sems, k - 1, (x, y, c)).wait_recv()
        for cp in copies:
            cp.wait_send()
        total = buf_ref[0]
        for d in range(1, N_DEV):
            total = total + buf_ref[d]
        o_ref[...] = total

    return pl.pallas_call(
        body, name="allreduce_small", in_specs=[vmem], out_specs=vmem, out_shape=_sds((rows, width)),
        scratch_shapes=[pltpu.VMEM((N_DEV, rows, width), F32), pltpu.SemaphoreType.DMA((N_DEV - 1,)),
                        pltpu.SemaphoreType.DMA((N_DEV - 1,))],
    )(vec)


def _rows1024(a):
    return a.reshape(-1, 1024)


def _pad_rows(a, rows):
    return jnp.concatenate([a, jnp.zeros((rows - a.shape[0], a.shape[1]), a.dtype)], axis=0)


_TRANSPOSED = ("w_in", "w_gate", "w_up")
_SMALL_SHARDED = ("w_up_f", "w_up_b", "a_up_f", "a_up_b", "g_up")
_BIG_SHARDED = ("w_in", "w_out", "w_gate", "w_up", "w_down")


def _pack_weight_shards(w):
    conv_bits = lax.bitcast_convert_type(w["conv_w"], BF16).reshape(1, -1)
    conv_row = jnp.concatenate([conv_bits, jnp.zeros((1, 1024 - conv_bits.shape[1]), BF16)], axis=1)
    parts = []
    for name, _ in _PACK_ROWS[:-1]:
        a = w[name].astype(BF16)
        parts.append(a.T if name in _TRANSPOSED else _rows1024(a))
    return _pad_rows(jnp.concatenate(parts + [conv_row], axis=0), PACK_R)


def _unpack_gathered(gathered):
    out, row = {}, 0
    for name, n in _PACK_ROWS:
        out[name] = gathered[:, row:row + n]
        row += n
    cols = lambda a, k: jnp.concatenate([a[s].reshape(k, -1) for s in range(N_SHARD)], axis=1)
    conv = lax.bitcast_convert_type(out["conv_w"][:, 0, :768].reshape(N_SHARD, 3, 128, 2), F32)
    full = {name: out[name].reshape(-1, 1024).T for name in _TRANSPOSED}
    full.update(w_out=out["w_out"].reshape(D_MODEL, D_MODEL), w_down=out["w_down"].reshape(D_FF, D_MODEL),
                conv_w=jnp.concatenate([conv[s] for s in range(N_SHARD)], axis=1))
    full.update({name: cols(out[name], D_GATE if name == "g_up" else D_LORA) for name in _SMALL_SHARDED})
    return full


def _pack_grads(g):
    col_split = lambda a, s: a[:, s * (a.shape[1] // N_SHARD):(s + 1) * (a.shape[1] // N_SHARD)]
    row_split = lambda a, s: a[s * (a.shape[0] // N_SHARD):(s + 1) * (a.shape[0] // N_SHARD)]
    by_rows = {name: (g[name].T if name in _TRANSPOSED else g[name]) for name in _BIG_SHARDED}
    used = sum(n for _, n in _PACK_ROWS)
    parts = []
    for s in range(N_SHARD):
        conv = col_split(g["conv_w"], s).reshape(1, -1)
        parts += [row_split(by_rows[name], s) for name in _BIG_SHARDED]
        parts += [_rows1024(col_split(g[name], s)) for name in _SMALL_SHARDED]
        parts += [jnp.concatenate([conv, jnp.zeros((1, 1024 - conv.shape[1]), F32)], axis=1),
                  jnp.zeros((PACK_R - used, 1024), F32)]
    return jnp.concatenate(parts, axis=0).reshape(N_SHARD, PACK_R, 1024)


def _unpack_grad_shard(pack):
    small_shapes = {name: (D_GATE if name == "g_up" else D_LORA, 128) for name in _SMALL_SHARDED}
    out, row = {}, 0
    for name, n in _PACK_ROWS[:-1]:
        piece = pack[row:row + n]
        out[name] = piece.T if name in _TRANSPOSED else piece.reshape(small_shapes.get(name, piece.shape))
        row += n
    out["conv_w"] = pack[row, :384].reshape(3, 128)
    return out


_SMALL_LAYOUT = (("norm1_w", 1024), ("mu_shift", D_SHIFT), ("w0_f", 512), ("w0_b", 512), ("a0_f", 512),
                 ("a0_b", 512), ("k_k", 512), ("k_a_f", 512), ("k_a_b", 512), ("r_k_f", 512), ("r_k_b", 512),
                 ("gn_w", 512), ("gn_b", 512), ("norm2_w", 1024), ("norm_f_w", 1024), ("loss", 1))


def _pack_small(vals):
    rows = []
    for name, n in _SMALL_LAYOUT:
        flat = vals[name].reshape(-1)
        n_rows = -(-n // 1024)
        rows.append(jnp.concatenate([flat, jnp.zeros((n_rows * 1024 - n,), F32)]).reshape(n_rows, 1024))
    return _pad_rows(jnp.concatenate(rows, axis=0), SMALL_ROWS)


def _unpack_small(pack):
    out, row = {}, 0
    for name, n in _SMALL_LAYOUT:
        n_rows = -(-n // 1024)
        out[name] = pack[row:row + n_rows].reshape(-1)[:n]
        row += n_rows
    return out


_WEIGHTS = ("norm1_w", "w_in", "mu_shift", "w_up_f", "w0_f", "w_up_b", "w0_b", "a_up_f", "a0_f", "a_up_b", "a0_b",
            "g_up", "k_k", "k_a_f", "k_a_b", "r_k_f", "r_k_b", "gn_w", "gn_b", "conv_w", "w_out", "norm2_w",
            "w_gate", "w_up", "w_down", "norm_f_w")


def _train_step(x, loss_target, w, m, v):
    batch, seq, _ = x.shape
    t = batch * seq
    tm = _tile(seq, 256, 8)
    xs = x.reshape(t, D_MODEL)
    target = loss_target.reshape(t, D_MODEL)
    vec = lambda name: w[name].reshape(1, -1)

    local = {name: w[name][0] for name, _ in _PACK_ROWS}
    c = lax.axis_index("c")
    chip = 2 * lax.axis_index("x") + lax.axis_index("y")
    pack = _pack_weight_shards(local)
    full = _unpack_gathered(lax.dynamic_update_slice(_gather_weights(pack), pack[None], (chip, 0, 0)))
    w_in = full["w_in"]
    w_shift = jnp.concatenate([w_in[:, :D_SHIFT], jnp.zeros((D_MODEL, D_SHIFT_PAD - D_SHIFT), BF16)], axis=1)
    w_conv = w_in[:, D_SHIFT:]
    zeros_lora = jnp.zeros((D_LORA, D_RWKV), F32)
    lora = lambda name: full[name].astype(F32)
    mats = (jnp.concatenate([lora("w_up_f"), zeros_lora]), jnp.concatenate([zeros_lora, lora("a_up_f")]),
            jnp.concatenate([lora("w_up_b"), zeros_lora]), jnp.concatenate([zeros_lora, lora("a_up_b")]),
            jnp.concatenate([lora("g_up"), jnp.zeros((D_GATE_PAD - D_GATE, D_RWKV), F32)]))
    mu = jnp.concatenate([vec("mu_shift"), jnp.zeros((1, D_SHIFT_PAD - D_SHIFT), F32)], axis=1)
    mu = jnp.broadcast_to(mu, (GROUP, D_SHIFT_PAD))
    zero_row = jnp.zeros((1, D_RWKV), F32)
    pvec = jnp.concatenate([vec("k_k"), vec("w0_f"), vec("a0_f"), vec("k_a_f"), vec("w0_b"), vec("a0_b"),
                            vec("k_a_b"), zero_row], axis=0)
    qvec = jnp.concatenate([vec("gn_w"), vec("gn_b"), vec("r_k_f"), vec("r_k_b"), full["conv_w"], zero_row], axis=0)
    ones_blocks = _head_ones()

    h1 = _rmsnorm_fwd(xs, vec("norm1_w"), tm, "norm1_fwd")
    p_shift = _matmul(h1, w_shift, mode="nn", name="in_proj_shift")
    pc = _matmul(h1, w_conv, mode="nn", name="in_proj_conv")
    ps = _shift_fwd(p_shift, mu, tm, seq)
    kk, w_f, kd_f, b_f, w_b, kd_b, b_b, gate = _prep_fwd(ps, pvec, mats, ones_blocks, tm)
    dirs = ((w_f, kd_f, b_f), (w_b, kd_b, b_b))
    y_f, y_b, hist_f, hist_b = _scan_fwd(ps, kk, dirs, batch, seq)
    mixed = _post_fwd(y_f, y_b, ps, kd_f, kd_b, gate, pc, qvec, ones_blocks, tm, seq)
    x1 = _matmul(mixed, full["w_out"], mode="nn", name="out_proj", add=xs)
    h2 = _rmsnorm_fwd(x1, vec("norm2_w"), tm, "norm2_fwd")
    ff_gate = _matmul(h2, full["w_gate"], mode="nn", name="ffn_gate")
    ff_up = _matmul(h2, full["w_up"], mode="nn", name="ffn_up")
    act = _swiglu_fwd(ff_gate, ff_up, tm)
    x2 = _matmul(act, full["w_down"], mode="nn", name="ffn_down", add=x1)
    d_x2, d_norm_f, loss_part = _loss_head(x2, w["norm_f_w"].reshape(1, -1), target, tm)

    g = {}
    d_act = _matmul(d_x2, full["w_down"], mode="nt", name="ffn_down_dx")
    g["w_down"] = _matmul(d_x2, act, mode="tn", name="ffn_down_dw").T
    d_gate, d_up = _swiglu_bwd(ff_gate, ff_up, d_act, tm)
    d_h2 = _matmul(d_gate, full["w_gate"], mode="nt", name="ffn_gate_dx")
    d_h2 = _matmul(d_up, full["w_up"], mode="nt", name="ffn_up_dx", add=d_h2)
    g["w_gate"] = _matmul(h2, d_gate, mode="tn", name="ffn_gate_dw")
    g["w_up"] = _matmul(h2, d_up, mode="tn", name="ffn_up_dw")
    d_x1, d_norm2 = _rmsnorm_bwd(x1, vec("norm2_w"), d_h2, d_x2, tm, "norm2_bwd")
    d_mixed = _matmul(d_x1, full["w_out"], mode="nt", name="out_proj_dx")
    g["w_out"] = _matmul(mixed, d_x1, mode="tn", name="out_proj_dw")
    dy, dr_o, dkdf_o, dkdb_o, dv_o, d_gatev, d_pc, d_qvec = _post_bwd(
        d_mixed, y_f, y_b, ps, kd_f, kd_b, gate, pc, qvec, ones_blocks, tm, seq)
    (dr_f, dw_f, dkd_f, dv_f, dkk_f, db_f), (dr_b, dw_b, dkd_b, dv_b, dkk_b, db_b) = _scan_bwd(
        ps, kk, dirs, dy, hist_f, hist_b, batch, seq)
    cts = [[dr_f, dr_b, dr_o], [dv_f, dv_b, dv_o], [dkk_f, dkk_b], [dw_f], [dkd_f, dkdf_o], [db_f],
           [dw_b], [dkd_b, dkdb_o], [db_b], [d_gatev]]
    q, d_pvec, d_m0, d_m1, d_m2, d_m3, d_m4 = _prep_bwd(ps, pvec, mats, ones_blocks, cts, tm)
    d_pshift, d_mu = _shift_bwd(q, p_shift, mu, tm, seq)
    d_h1 = _matmul(d_pshift, w_shift, mode="nt", name="in_proj_shift_dx")
    d_h1 = _matmul(d_pc, w_conv, mode="nt", name="in_proj_conv_dx", add=d_h1)
    d_w_shift = _matmul(h1, d_pshift, mode="tn", name="in_proj_shift_dw")
    d_w_conv = _matmul(h1, d_pc, mode="tn", name="in_proj_conv_dw")
    g["w_in"] = jnp.concatenate([d_w_shift[:, :D_SHIFT], d_w_conv], axis=1)
    d_x, d_norm1 = _rmsnorm_bwd(xs, vec("norm1_w"), d_h1, d_x1, tm, "norm1_bwd")
    g["w_up_f"], g["a_up_f"] = d_m0[:D_LORA], d_m1[D_LORA:]
    g["w_up_b"], g["a_up_b"] = d_m2[:D_LORA], d_m3[D_LORA:]
    g["g_up"] = d_m4[:D_GATE]
    g["conv_w"] = d_qvec[4:7]

    packed = _pack_grads(g)
    chip_sum, chip_sum_bf16 = _add_halves(packed, _swap_other_half(packed), c)
    others = _exchange_quarters(chip_sum_bf16)
    eighth = _add_quarters(chip_sum, others, chip)
    other_eighth = _swap_with_sibling(eighth, "swap_eighths")
    grads = _unpack_grad_shard(jnp.concatenate([jnp.where(c == 0, eighth, other_eighth),
                                                jnp.where(c == 0, other_eighth, eighth)], axis=0))

    small = dict(norm1_w=d_norm1, mu_shift=d_mu[:, :D_SHIFT], w0_f=d_pvec[1], w0_b=d_pvec[4], a0_f=d_pvec[2],
                 a0_b=d_pvec[5], k_k=d_pvec[0], k_a_f=d_pvec[3], k_a_b=d_pvec[6], r_k_f=d_qvec[2], r_k_b=d_qvec[3],
                 gn_w=d_qvec[0], gn_b=d_qvec[1], norm2_w=d_norm2, norm_f_w=d_norm_f, loss=loss_part)
    reduced = _unpack_small(_allreduce_small(_pack_small(small)))
    loss = reduced.pop("loss")[0]
    grads.update(reduced)

    outs = {}
    for name in _WEIGHTS:
        shape = w[name].shape
        as2d = (1, shape[0]) if len(shape) == 1 else shape
        grad = grads[name].reshape(shape)
        delta, new_m, new_v = _adamw(w[name].reshape(as2d), grad.reshape(as2d), m[name].reshape(as2d),
                                     v[name].reshape(as2d), "adamw_" + name)
        outs[name] = (grad, delta.reshape(shape), new_m.reshape(shape), new_v.reshape(shape))
    d_x = d_x.reshape(batch, seq, D_MODEL)
    return (loss, d_x) + tuple(outs[name][k] for k in range(4) for name in _WEIGHTS)


def kernel(x, norm1_w, w_in, mu_shift, w_up_f, w0_f, w_up_b, w0_b, a_up_f, a0_f, a_up_b, a0_b, g_up, k_k, k_a_f, k_a_b, r_k_f, r_k_b, gn_w, gn_b, conv_w, w_out, norm2_w, w_gate, w_up, w_down, norm_f_w, loss_target, m_norm1_w, m_w_in, m_mu_shift, m_w_up_f, m_w0_f, m_w_up_b, m_w0_b, m_a_up_f, m_a0_f, m_a_up_b, m_a0_b, m_g_up, m_k_k, m_k_a_f, m_k_a_b, m_r_k_f, m_r_k_b, m_gn_w, m_gn_b, m_conv_w, m_w_out, m_norm2_w, m_w_gate, m_w_up, m_w_down, m_norm_f_w, v_norm1_w, v_w_in, v_mu_shift, v_w_up_f, v_w0_f, v_w_up_b, v_w0_b, v_a_up_f, v_a0_f, v_a_up_b, v_a0_b, v_g_up, v_k_k, v_k_a_f, v_k_a_b, v_r_k_f, v_r_k_b, v_gn_w, v_gn_b, v_conv_w, v_w_out, v_norm2_w, v_w_gate, v_w_up, v_w_down, v_norm_f_w):
    args = locals()
    w = {name: args[name] for name in _WEIGHTS}
    m = {name: args["m_" + name] for name in _WEIGHTS}
    v = {name: args["v_" + name] for name in _WEIGHTS}
    return _train_step(x, loss_target, w, m, v)
```

```python
import functools

import jax
import jax.numpy as jnp
from jax import lax
from jax.experimental import pallas as pl
from jax.experimental.pallas import tpu as pltpu

F32 = jnp.float32
BF16 = jnp.bfloat16
MESH = pl.DeviceIdType.MESH

D_MODEL = 1024
D_RWKV = 512
HEAD = 64
N_PAIR = D_RWKV // (2 * HEAD)
D_LORA = 64
D_GATE = 160
D_GATE_PAD = 384
D_FF = 2816
D_SHIFT = 1824
D_SHIFT_PAD = 2048
D_CONV3 = 1536
LOG_DECAY_SCALE = 0.606531
RMS_EPS = 1e-6
GN_EPS = 64e-5
NORM_EPS = 1e-12
ADAM_LR, ADAM_B1, ADAM_B2, ADAM_EPS, ADAM_WD, ADAM_STEP = 0.001, 0.9, 0.999, 1e-08, 0.01, 10

N_SHARD = 4
N_DEV = 8
V7X_VMEM_LIMIT = 48 * 1024 * 1024
SCAN_CHUNK = 32
GROUP = 8

_PACK_ROWS = (("w_in", 840), ("w_out", 256), ("w_gate", 704), ("w_up", 704), ("w_down", 704),
              ("w_up_f", 8), ("w_up_b", 8), ("a_up_f", 8), ("a_up_b", 8), ("g_up", 20), ("conv_w", 1))
PACK_R = 3264
PACK_H = PACK_R // 2
SMALL_ROWS = 24


def _tile(n, cap, mult=128):
    best = None
    t = mult
    while t <= min(n, cap):
        if n % t == 0:
            best = t
        t += mult
    return best or n


def _cp(*sem):
    return pltpu.CompilerParams(dimension_semantics=sem or None, vmem_limit_bytes=V7X_VMEM_LIMIT)


def _sds(shape, dtype=F32):
    return jax.ShapeDtypeStruct(shape, dtype)


def _matmul(a, b, *, mode, name, out_dtype=F32, add=None):
    if mode == "tn":
        r, m = a.shape
        n = b.shape[1]
        tm, tn, tk = _tile(m, 512), _tile(n, 1536), _tile(r, 512, 8)
        nk = r // tk
        a_spec = pl.BlockSpec((tk, tm), lambda i, j, k: (k, i))
        b_spec = pl.BlockSpec((tk, tn), lambda i, j, k: (k, j))
        dims = (((0,), (0,)), ((), ()))
    else:
        m, kdim = a.shape
        n = b.shape[1] if mode == "nn" else b.shape[0]
        tm, tn = _tile(m, 512, 8), _tile(n, 1536)
        tk = kdim if kdim <= 3584 else _tile(kdim, 1024)
        nk = kdim // tk
        a_spec = pl.BlockSpec((tm, tk), lambda i, j, k: (i, k))
        if mode == "nn":
            b_spec = pl.BlockSpec((tk, tn), lambda i, j, k: (k, j))
            dims = (((1,), (0,)), ((), ()))
        else:
            b_spec = pl.BlockSpec((tn, tk), lambda i, j, k: (j, k))
            dims = (((1,), (1,)), ((), ()))
    has_add = add is not None

    def body(*refs):
        a_ref, b_ref = refs[0], refs[1]
        add_ref = refs[2] if has_add else None
        o_ref = refs[3] if has_add else refs[2]
        part = lax.dot_general(a_ref[...].astype(BF16), b_ref[...].astype(BF16), dims,
                               preferred_element_type=F32)
        if nk == 1:
            if has_add:
                part = part + add_ref[...]
            o_ref[...] = part.astype(out_dtype)
        else:
            acc_ref = refs[-1]
            k = pl.program_id(2)

            @pl.when(k == 0)
            def _():
                acc_ref[...] = jnp.zeros_like(acc_ref)

            acc_ref[...] += part

            @pl.when(k == nk - 1)
            def _():
                res = acc_ref[...]
                if has_add:
                    res = res + add_ref[...]
                o_ref[...] = res.astype(out_dtype)

    o_spec = pl.BlockSpec((tm, tn), lambda i, j, k: (i, j))
    in_specs = [a_spec, b_spec] + ([o_spec] if has_add else [])
    args = (a, b) + ((add,) if has_add else ())
    return pl.pallas_call(
        body, name=name, grid=(m // tm, n // tn, nk), in_specs=in_specs, out_specs=o_spec,
        out_shape=_sds((m, n), out_dtype),
        scratch_shapes=[pltpu.VMEM((tm, tn), F32)] if nk > 1 else [],
        compiler_params=_cp("parallel", "parallel", "arbitrary"),
    )(*args)


def _row(tm, width):
    return pl.BlockSpec((tm, width), lambda i: (i, 0))


def _col(tm, height):
    return pl.BlockSpec((height, tm), lambda i: (0, i))


def _fixed(shape):
    return pl.BlockSpec(shape, lambda i: tuple(0 for _ in shape))


def _rmsnorm_fwd(x, w, tm, name):
    t, d = x.shape

    def body(x_ref, w_ref, o_ref, ot_ref):
        xv = x_ref[...]
        rstd = lax.rsqrt(jnp.mean(xv * xv, axis=-1, keepdims=True) + RMS_EPS)
        yv = xv * rstd * w_ref[...]
        o_ref[...] = yv.astype(BF16)
        ot_ref[...] = jnp.transpose(yv).astype(BF16)

    return pl.pallas_call(
        body, name=name, grid=(t // tm,), in_specs=[_row(tm, d), _fixed((1, d))],
        out_specs=[_row(tm, d), _col(tm, d)], out_shape=[_sds((t, d), BF16), _sds((d, t), BF16)],
        compiler_params=_cp("parallel"))(x, w)


def _rms_bwd_math(xv, wv, dyv):
    rstd = lax.rsqrt(jnp.mean(xv * xv, axis=-1, keepdims=True) + RMS_EPS)
    xhat = xv * rstd
    gv = dyv * wv
    dx = rstd * (gv - xhat * jnp.mean(gv * xhat, axis=-1, keepdims=True))
    return dx, jnp.sum(dyv * xhat, axis=0, keepdims=True)


def _rmsnorm_bwd(x, w, dy, dres, tm, name):
    t, d = x.shape

    def body(x_ref, w_ref, dy_ref, dres_ref, dx_ref, dw_ref):
        dx, dw = _rms_bwd_math(x_ref[...], w_ref[...], dy_ref[...])
        dx_ref[...] = dres_ref[...] + dx

        @pl.when(pl.program_id(0) == 0)
        def _():
            dw_ref[...] = jnp.zeros_like(dw_ref)

        dw_ref[...] += dw

    return pl.pallas_call(
        body, name=name, grid=(t // tm,),
        in_specs=[_row(tm, d), _fixed((1, d)), _row(tm, d), _row(tm, d)],
        out_specs=[_row(tm, d), _fixed((1, d))],
        out_shape=[_sds((t, d)), _sds((1, d))], compiler_params=_cp("arbitrary"))(x, w, dy, dres)


def _loss_head(x, w, target, tm):
    t, d = x.shape

    def body(x_ref, w_ref, t_ref, dx_ref, dw_ref, loss_ref):
        xv, wv = x_ref[...], w_ref[...]
        rstd = lax.rsqrt(jnp.mean(xv * xv, axis=-1, keepdims=True) + RMS_EPS)
        err = xv * rstd * wv - t_ref[...]
        dx, dw = _rms_bwd_math(xv, wv, err * (1.0 / d))
        dx_ref[...] = dx

        @pl.when(pl.program_id(0) == 0)
        def _():
            dw_ref[...] = jnp.zeros_like(dw_ref)
            loss_ref[...] = jnp.zeros_like(loss_ref)

        dw_ref[...] += dw
        loss_ref[...] += 0.5 * jnp.sum(jnp.mean(err * err, axis=-1, keepdims=True), axis=0, keepdims=True)

    return pl.pallas_call(
        body, name="loss_head", grid=(t // tm,),
        in_specs=[_row(tm, d), _fixed((1, d)), _row(tm, d)],
        out_specs=[_row(tm, d), _fixed((1, d)), _fixed((1, 1))],
        out_shape=[_sds((t, d)), _sds((1, d)), _sds((1, 1))], compiler_params=_cp("arbitrary"))(x, w, target)


def _swiglu_fwd(gate, up, tm):
    t, f = gate.shape

    def body(g_ref, u_ref, o_ref, ot_ref):
        gv = g_ref[...].astype(F32)
        act = gv * jax.nn.sigmoid(gv) * u_ref[...].astype(F32)
        o_ref[...] = act.astype(BF16)
        ot_ref[...] = jnp.transpose(act).astype(BF16)

    return pl.pallas_call(
        body, name="swiglu_fwd", grid=(t // tm,), in_specs=[_row(tm, f), _row(tm, f)],
        out_specs=[_row(tm, f), _col(tm, f)], out_shape=[_sds((t, f), BF16), _sds((f, t), BF16)],
        compiler_params=_cp("parallel"))(gate, up)


def _swiglu_bwd(gate, up, dact, tm):
    t, f = gate.shape

    def body(g_ref, u_ref, d_ref, dg_ref, du_ref):
        gv, uv, dv = g_ref[...].astype(F32), u_ref[...].astype(F32), d_ref[...].astype(F32)
        sg = jax.nn.sigmoid(gv)
        du_ref[...] = (dv * gv * sg).astype(BF16)
        dg_ref[...] = (dv * uv * (sg * (1.0 + gv * (1.0 - sg)))).astype(BF16)

    return pl.pallas_call(
        body, name="swiglu_bwd", grid=(t // tm,), in_specs=[_row(tm, f)] * 3, out_specs=[_row(tm, f)] * 2,
        out_shape=[_sds((t, f), BF16)] * 2, compiler_params=_cp("parallel"))(gate, up, dact)


def _halo_specs(tm, width, rows_total):
    per = tm // GROUP
    last = rows_total // GROUP - 1
    prev = pl.BlockSpec((GROUP, width), lambda i: (jnp.maximum(i * per - 1, 0), 0))
    nxt = pl.BlockSpec((GROUP, width), lambda i: (jnp.minimum((i + 1) * per, last), 0))
    return prev, nxt


def _edge_flags(tm, seq):
    i = pl.program_id(0)
    has_prev = jnp.where((i * tm) % seq == 0, 0.0, 1.0).astype(F32)
    has_next = jnp.where(((i + 1) * tm) % seq == 0, 0.0, 1.0).astype(F32)
    return has_prev, has_next


def _shifted(xv, prev_row, next_row):
    tm = xv.shape[0]
    row = lax.broadcasted_iota(jnp.int32, xv.shape, 0)
    down = jnp.where(row == 0, prev_row, pltpu.roll(xv, 1, axis=0))
    up = jnp.where(row == tm - 1, next_row, pltpu.roll(xv, tm - 1, axis=0))
    return down, up


def _shift_fwd(p, mu, tm, seq):
    t, w = p.shape
    prev_spec, next_spec = _halo_specs(tm, w, t)

    def body(p_ref, hp_ref, hn_ref, mu_ref, o_ref):
        has_prev, has_next = _edge_flags(tm, seq)
        xv = p_ref[...]
        down, up = _shifted(xv, hp_ref[GROUP - 1:GROUP, :] * has_prev, hn_ref[0:1, :] * has_next)
        o_ref[...] = xv + mu_ref[0:1, :] * (0.5 * (down + up) - xv)

    return pl.pallas_call(
        body, name="shift_fwd", grid=(t // tm,),
        in_specs=[_row(tm, w), prev_spec, next_spec, _fixed((GROUP, w))], out_specs=_row(tm, w),
        out_shape=_sds((t, w)), compiler_params=_cp("parallel"))(p, p, p, mu)


def _shift_bwd(q, p, mu, tm, seq):
    t, w = p.shape
    prev_spec, next_spec = _halo_specs(tm, w, t)

    def body(q_ref, qp_ref, qn_ref, p_ref, pp_ref, pn_ref, mu_ref, dp_ref, dmu_ref):
        has_prev, has_next = _edge_flags(tm, seq)
        muv = mu_ref[0:1, :]
        qv = q_ref[...]
        mq = muv * qv
        mq_down, mq_up = _shifted(mq, muv * qp_ref[GROUP - 1:GROUP, :] * has_prev,
                                  muv * qn_ref[0:1, :] * has_next)
        dp_ref[...] = (qv - mq + 0.5 * (mq_down + mq_up)).astype(BF16)
        pv = p_ref[...]
        p_down, p_up = _shifted(pv, pp_ref[GROUP - 1:GROUP, :] * has_prev, pn_ref[0:1, :] * has_next)

        @pl.when(pl.program_id(0) == 0)
        def _():
            dmu_ref[...] = jnp.zeros_like(dmu_ref)

        dmu_ref[...] += jnp.sum(qv * (0.5 * (p_down + p_up) - pv), axis=0, keepdims=True)

    return pl.pallas_call(
        body, name="shift_bwd", grid=(t // tm,),
        in_specs=[_row(tm, w), prev_spec, next_spec, _row(tm, w), prev_spec, next_spec, _fixed((GROUP, w))],
        out_specs=[_row(tm, w), _fixed((1, w))],
        out_shape=[_sds((t, w), BF16), _sds((1, w))], compiler_params=_cp("arbitrary"))(q, q, q, p, p, p, mu)


@jax.custom_vjp
def _bdot(a, b):
    return jnp.dot(a.astype(BF16), b.astype(BF16), preferred_element_type=F32)


def _bdot_fwd(a, b):
    return _bdot(a, b), (a, b)


def _bdot_bwd(res, g):
    a, b = res
    gb = g.astype(BF16)
    da = lax.dot_general(gb, b.astype(BF16), (((1,), (1,)), ((), ())), preferred_element_type=F32)
    db = lax.dot_general(a.astype(BF16), gb, (((0,), (0,)), ((), ())), preferred_element_type=F32)
    return da, db


_bdot.defvjp(_bdot_fwd, _bdot_bwd)


def _seg_raw(x, ones_blocks):
    hi = x.astype(BF16)
    lo = (x - hi.astype(F32)).astype(BF16)
    return (jnp.dot(hi, ones_blocks, preferred_element_type=F32)
            + jnp.dot(lo, ones_blocks, preferred_element_type=F32))


@jax.custom_vjp
def _seg(x, ones_blocks):
    return _seg_raw(x, ones_blocks)


def _seg_fwd(x, ones_blocks):
    return _seg_raw(x, ones_blocks), ones_blocks


def _seg_bwd(ones_blocks, g):
    return _seg_raw(g, ones_blocks), jnp.zeros_like(ones_blocks)


_seg.defvjp(_seg_fwd, _seg_bwd)


def _head_ones():
    h = jnp.arange(D_RWKV) // HEAD
    return (h[:, None] == h[None, :]).astype(BF16)


def _prep_math(ps, k_k, w0_f, a0_f, k_a_f, w0_b, a0_b, k_a_b, wup_f, aup_f, wup_b, aup_b, gup, ones_blocks):
    r = ps[:, 0:512]
    k = ps[:, 512:1024]
    v = ps[:, 1024:1536]
    xwa = ps[:, 1536:1664]
    xg = ps[:, 1664:D_SHIFT_PAD]
    kk_raw = k * k_k
    norm = jnp.sqrt(_seg(kk_raw * kk_raw, ones_blocks))
    kk = kk_raw / jnp.maximum(norm, NORM_EPS)
    t_xwa = jnp.tanh(xwa)
    outs = [r, v, kk]
    for w0, a0, k_a, wup, aup in ((w0_f, a0_f, k_a_f, wup_f, aup_f), (w0_b, a0_b, k_a_b, wup_b, aup_b)):
        decay = jnp.exp(-LOG_DECAY_SCALE * jax.nn.sigmoid(w0 + _bdot(t_xwa, wup)))
        rate = jax.nn.sigmoid(a0 + _bdot(xwa, aup))
        outs += [decay, k * (1.0 + (rate - 1.0) * k_a), kk * rate]
    outs.append(_bdot(jax.nn.sigmoid(xg), gup))
    return tuple(outs)


def _prep_args(tm, ps_ref, pv_ref, mat_refs, ones_ref):
    vecs = [jnp.broadcast_to(pv_ref[j:j + 1, :], (tm, D_RWKV)) for j in range(7)]
    return [ps_ref[...]] + vecs + [m[...] for m in mat_refs] + [ones_ref[...]]


_PREP_MAT_SHAPES = ((128, D_RWKV),) * 4 + ((D_GATE_PAD, D_RWKV),)


def _prep_fwd(ps, pvec, mats, ones_blocks, tm):
    t = ps.shape[0]

    def body(ps_ref, pv_ref, m0, m1, m2, m3, m4, ones_ref, *out_refs):
        outs = _prep_math(*_prep_args(tm, ps_ref, pv_ref, (m0, m1, m2, m3, m4), ones_ref))
        for o_ref, val in zip(out_refs, outs[2:]):
            o_ref[...] = val

    return pl.pallas_call(
        body, name="prep_fwd", grid=(t // tm,),
        in_specs=[_row(tm, D_SHIFT_PAD), _fixed((8, D_RWKV))] + [_fixed(s) for s in _PREP_MAT_SHAPES]
        + [_fixed((D_RWKV, D_RWKV))],
        out_specs=[_row(tm, D_RWKV)] * 8, out_shape=[_sds((t, D_RWKV))] * 8,
        compiler_params=_cp("parallel"))(ps, pvec, *mats, ones_blocks)


def _prep_bwd(ps, pvec, mats, ones_blocks, cts, tm):
    t = ps.shape[0]
    counts = [len(c) for c in cts]
    flat = [a for c in cts for a in c]

    def body(ps_ref, pv_ref, m0, m1, m2, m3, m4, ones_ref, *refs):
        ct_refs = refs[:len(flat)]
        q_ref, dpv_ref = refs[len(flat)], refs[len(flat) + 1]
        dmat_refs = refs[len(flat) + 2:]
        args = _prep_args(tm, ps_ref, pv_ref, (m0, m1, m2, m3, m4), ones_ref)
        _, vjp = jax.vjp(lambda *a: _prep_math(*a, args[-1]), *args[:-1])
        ct_vals, pos = [], 0
        for n in counts:
            val = ct_refs[pos][...]
            for extra in ct_refs[pos + 1:pos + n]:
                val = val + extra[...]
            ct_vals.append(val)
            pos += n
        grads = vjp(tuple(ct_vals))
        q_ref[...] = grads[0]

        @pl.when(pl.program_id(0) == 0)
        def _():
            dpv_ref[...] = jnp.zeros_like(dpv_ref)
            for d_ref in dmat_refs:
                d_ref[...] = jnp.zeros_like(d_ref)

        for j in range(7):
            dpv_ref[j:j + 1, :] += jnp.sum(grads[1 + j], axis=0, keepdims=True)
        for d_ref, gm in zip(dmat_refs, grads[8:13]):
            d_ref[...] += gm

    return pl.pallas_call(
        body, name="prep_bwd", grid=(t // tm,),
        in_specs=[_row(tm, D_SHIFT_PAD), _fixed((8, D_RWKV))] + [_fixed(s) for s in _PREP_MAT_SHAPES]
        + [_fixed((D_RWKV, D_RWKV))] + [_row(tm, D_RWKV)] * len(flat),
        out_specs=[_row(tm, D_SHIFT_PAD), _fixed((8, D_RWKV))] + [_fixed(s) for s in _PREP_MAT_SHAPES],
        out_shape=[_sds((t, D_SHIFT_PAD)), _sds((8, D_RWKV))] + [_sds(s) for s in _PREP_MAT_SHAPES],
        compiler_params=_cp("arbitrary"))(ps, pvec, *mats, ones_blocks, *flat)


def _pair_ones():
    h = jnp.arange(2 * HEAD) // HEAD
    block = (h[:, None] == h[None, :]).astype(BF16)
    return jnp.concatenate([block, block], axis=0)


def _diag_mask():
    lane = lax.broadcasted_iota(jnp.int32, (HEAD, 2 * HEAD), 1)
    sub = lax.broadcasted_iota(jnp.int32, (HEAD, 2 * HEAD), 0)
    return jnp.where((lane & (HEAD - 1)) == sub, 1.0, 0.0).astype(F32)


def _to_row(cols, dmask):
    return jnp.sum(cols * dmask, axis=0, keepdims=True)


def _seg_many(exact, rounded, ones_pair):
    out_exact, out_rounded = [], []
    if exact:
        parts = []
        for x in exact:
            hi = x.astype(BF16)
            parts.append(jnp.concatenate([hi, (x - hi.astype(F32)).astype(BF16)], axis=1))
        res = jnp.dot(jnp.concatenate(parts, axis=0), ones_pair, preferred_element_type=F32)
        out_exact = [res[HEAD * c:HEAD * (c + 1)] for c in range(len(exact))]
    if rounded:
        res = jnp.dot(jnp.concatenate([x.astype(BF16) for x in rounded], axis=0), ones_pair[0:2 * HEAD],
                      preferred_element_type=F32)
        out_rounded = [res[HEAD * c:HEAD * (c + 1)] for c in range(len(rounded))]
    return out_exact, out_rounded


N_CHAIN = 2 * N_PAIR


def _chain(c):
    d, p = divmod(c, N_PAIR)
    return d, slice(2 * HEAD * p, 2 * HEAD * (p + 1))


def _scan_specs(n_chunks, col_blocks, fwd_chunk, bwd_chunk):
    def spec(chunk_of, col):
        return pl.BlockSpec((SCAN_CHUNK, D_RWKV), lambda b, g: (b * n_chunks + chunk_of(g), col))
    return [spec(fwd_chunk, c) for c in col_blocks] + [spec(bwd_chunk, c) for c in col_blocks]


def _scan_fwd(ps, kk, dirs, batch, seq):
    t = batch * seq
    n = seq // SCAN_CHUNK
    groups = SCAN_CHUNK // GROUP
    up = lambda g: g
    down = lambda g: n - 1 - g
    col_blocks = (0, 2, 0, 0, 0, 0)

    def body(*refs):
        dir_refs = (refs[0:6], refs[6:12])
        ones_ref = refs[12]
        y_refs, hist_refs, st_ref = refs[13:15], refs[15:17], refs[17]

        @pl.when(pl.program_id(1) == 0)
        def _():
            st_ref[...] = jnp.zeros_like(st_ref)

        ones_pair = ones_ref[...]
        dmask = _diag_mask()
        sub8 = lax.broadcasted_iota(jnp.int32, (GROUP, 2 * HEAD), 0)

        def group(gi, carry):
            off = (pl.multiple_of(gi * GROUP, GROUP), pl.multiple_of((groups - 1 - gi) * GROUP, GROUP))
            loaded = [tuple(ref[pl.ds(off[d], GROUP), :] for ref in dir_refs[d]) for d in range(2)]
            states = list(carry)
            y_acc = [jnp.zeros((GROUP, 2 * HEAD), F32) for _ in range(N_CHAIN)]
            for step in range(GROUP):
                rows, idx = [], []
                for c in range(N_CHAIN):
                    d, lanes = _chain(c)
                    i = step if d == 0 else GROUP - 1 - step
                    idx.append(i)
                    rows.append(tuple(x8[i:i + 1, lanes] for x8 in loaded[d]))
                    hist_refs[d][c % N_PAIR, gi * GROUP + step] = states[c]
                _, v_cols = _seg_many([], [dmask * rows[c][1] for c in range(N_CHAIN)], ones_pair)
                sas, _ = _seg_many([states[c] * rows[c][2] for c in range(N_CHAIN)], [], ones_pair)
                for c in range(N_CHAIN):
                    _, _, _, w_row, kd_row, b_row = rows[c]
                    states[c] = states[c] * w_row - sas[c] * b_row + v_cols[c] * kd_row
                _, ys = _seg_many([], [states[c] * rows[c][0] for c in range(N_CHAIN)], ones_pair)
                for c in range(N_CHAIN):
                    y_acc[c] = jnp.where(sub8 == idx[c], _to_row(ys[c], dmask), y_acc[c])
            for c in range(N_CHAIN):
                d, lanes = _chain(c)
                y_refs[d][pl.ds(off[d], GROUP), lanes] = y_acc[c]
            return tuple(states)

        final = lax.fori_loop(0, groups, group, tuple(st_ref[c] for c in range(N_CHAIN)))
        for c in range(N_CHAIN):
            st_ref[c] = final[c]
            hist_refs[c // N_PAIR][c % N_PAIR, SCAN_CHUNK] = final[c]

    y_spec_f = pl.BlockSpec((SCAN_CHUNK, D_RWKV), lambda b, g: (b * n + up(g), 0))
    y_spec_b = pl.BlockSpec((SCAN_CHUNK, D_RWKV), lambda b, g: (b * n + down(g), 0))
    hist_shape = (batch, n, N_PAIR, SCAN_CHUNK + 1, HEAD, 2 * HEAD)
    hist_block = (None, None, N_PAIR, SCAN_CHUNK + 1, HEAD, 2 * HEAD)
    hist_spec_f = pl.BlockSpec(hist_block, lambda b, g: (b, up(g), 0, 0, 0, 0))
    hist_spec_b = pl.BlockSpec(hist_block, lambda b, g: (b, down(g), 0, 0, 0, 0))
    ones_spec = pl.BlockSpec((4 * HEAD, 2 * HEAD), lambda b, g: (0, 0))
    (wf, kdf, bf), (wb, kdb, bb) = dirs
    return pl.pallas_call(
        body, name="wkv_fwd", grid=(batch, n),
        in_specs=_scan_specs(n, col_blocks, up, down) + [ones_spec],
        out_specs=[y_spec_f, y_spec_b, hist_spec_f, hist_spec_b],
        out_shape=[_sds((t, D_RWKV)), _sds((t, D_RWKV)), _sds(hist_shape), _sds(hist_shape)],
        scratch_shapes=[pltpu.VMEM((N_CHAIN, HEAD, 2 * HEAD), F32)],
        compiler_params=_cp("parallel", "arbitrary"),
    )(ps, ps, kk, wf, kdf, bf, ps, ps, kk, wb, kdb, bb, _pair_ones())


def _scan_bwd(ps, kk, dirs, dy, hist_f, hist_b, batch, seq):
    t = batch * seq
    n = seq // SCAN_CHUNK
    groups = SCAN_CHUNK // GROUP
    fwd_chunk = lambda g: n - 1 - g
    bwd_chunk = lambda g: g
    col_blocks = (0, 2, 0, 0, 0, 0, 0)

    def undo_group(dir_refs, out_refs, hist_refs, gi, d_states, ones_pair, dmask, sub8):
        d_states = list(d_states)
        loaded, blocks = [], []
        for d in range(2):
            blk = groups - 1 - gi if d == 0 else gi
            blocks.append(pl.ds(pl.multiple_of(blk * GROUP, GROUP), GROUP))
            loaded.append(tuple(ref[blocks[d], :] for ref in dir_refs[d]))
        acc = [[jnp.zeros((GROUP, 2 * HEAD), F32) for _ in range(6)] for _ in range(N_CHAIN)]
        for step in range(GROUP):
            rows, idx, before, after = [], [], [], []
            for c in range(N_CHAIN):
                d, lanes = _chain(c)
                i = GROUP - 1 - step if d == 0 else step
                q = (groups - 1 - gi) * GROUP + i if d == 0 else SCAN_CHUNK - 1 - (gi * GROUP + i)
                idx.append(i)
                rows.append(tuple(x8[i:i + 1, lanes] for x8 in loaded[d]))
                before.append(hist_refs[d][c % N_PAIR, q])
                after.append(hist_refs[d][c % N_PAIR, q + 1])
            _, cols = _seg_many([], [dmask * rows[c][j] for c in range(N_CHAIN) for j in (1, 6)], ones_pair)
            v_cols, dy_cols = cols[0::2], cols[1::2]
            d_now = [d_states[c] + dy_cols[c] * rows[c][0] for c in range(N_CHAIN)]
            d_sas, _ = _seg_many([d_now[c] * rows[c][5] for c in range(N_CHAIN)], [], ones_pair)
            _, others = _seg_many(
                [], [x for c in range(N_CHAIN) for x in (before[c] * rows[c][2], d_now[c] * rows[c][4])], ones_pair)
            for c in range(N_CHAIN):
                sa, d_sa, dv_cols = others[2 * c], -d_sas[c], others[2 * c + 1]
                rows_out = (
                    jnp.sum(after[c] * dy_cols[c], axis=0, keepdims=True),
                    jnp.sum(d_now[c] * before[c], axis=0, keepdims=True),
                    jnp.sum(d_now[c] * v_cols[c], axis=0, keepdims=True),
                    _to_row(dv_cols, dmask),
                    jnp.sum(before[c] * d_sa, axis=0, keepdims=True),
                    -jnp.sum(d_now[c] * sa, axis=0, keepdims=True),
                )
                acc[c] = [jnp.where(sub8 == idx[c], val, a) for val, a in zip(rows_out, acc[c])]
                d_states[c] = d_now[c] * rows[c][3] + d_sa * rows[c][2]
        for c in range(N_CHAIN):
            d, lanes = _chain(c)
            for o_ref, val in zip(out_refs[d], acc[c]):
                o_ref[blocks[d], lanes] = val
        return tuple(d_states)

    def body(*refs):
        dir_refs = (refs[0:7], refs[7:14])
        hist_refs, ones_ref = refs[14:16], refs[16]
        out_refs = (refs[17:23], refs[23:29])
        dst_ref = refs[29]

        @pl.when(pl.program_id(1) == 0)
        def _():
            dst_ref[...] = jnp.zeros_like(dst_ref)

        ones_pair = ones_ref[...]
        dmask = _diag_mask()
        sub8 = lax.broadcasted_iota(jnp.int32, (GROUP, 2 * HEAD), 0)

        def group(gi, carry):
            return undo_group(dir_refs, out_refs, hist_refs, gi, carry, ones_pair, dmask, sub8)

        final = lax.fori_loop(0, groups, group, tuple(dst_ref[c] for c in range(N_CHAIN)))
        for c in range(N_CHAIN):
            dst_ref[c] = final[c]

    blk = (SCAN_CHUNK, D_RWKV)
    out_f = pl.BlockSpec(blk, lambda b, g: (b * n + fwd_chunk(g), 0))
    out_b = pl.BlockSpec(blk, lambda b, g: (b * n + bwd_chunk(g), 0))
    hist_block = (None, None, N_PAIR, SCAN_CHUNK + 1, HEAD, 2 * HEAD)
    hist_spec_f = pl.BlockSpec(hist_block, lambda b, g: (b, fwd_chunk(g), 0, 0, 0, 0))
    hist_spec_b = pl.BlockSpec(hist_block, lambda b, g: (b, bwd_chunk(g), 0, 0, 0, 0))
    ones_spec = pl.BlockSpec((4 * HEAD, 2 * HEAD), lambda b, g: (0, 0))
    (wf, kdf, bf), (wb, kdb, bb) = dirs
    outs = pl.pallas_call(
        body, name="wkv_bwd", grid=(batch, n),
        in_specs=_scan_specs(n, col_blocks, fwd_chunk, bwd_chunk) + [hist_spec_f, hist_spec_b, ones_spec],
        out_specs=[out_f] * 6 + [out_b] * 6,
        out_shape=[_sds((t, D_RWKV))] * 12,
        scratch_shapes=[pltpu.VMEM((N_CHAIN, HEAD, 2 * HEAD), F32)],
        compiler_params=_cp("parallel", "arbitrary"),
    )(ps, ps, kk, wf, kdf, bf, dy, ps, ps, kk, wb, kdb, bb, dy, hist_f, hist_b, _pair_ones())
    return outs[0:6], outs[6:12]


def _post_math(y, r, kd_f, kd_b, v, gate, gn_w, gn_b, rk_f, rk_b, ones_blocks):
    mean = _seg(y, ones_blocks) * (1.0 / HEAD)
    yc = y - mean
    var = _seg(yc * yc, ones_blocks) * (1.0 / HEAD)
    yn = yc * lax.rsqrt(var + GN_EPS) * gn_w + gn_b
    bonus = _seg(r * kd_f * rk_f, ones_blocks) * v + _seg(r * kd_b * rk_b, ones_blocks) * v
    return (yn + bonus) * gate


def _conv_parts(pc, halo_prev, halo_next, has_prev, has_next):
    gate_b, gate_c, hid = pc[:, 0:512], pc[:, 512:1024], pc[:, 1024:1536]
    u = gate_c * hid
    u_prev_row = halo_prev[GROUP - 1:GROUP, 512:1024] * halo_prev[GROUP - 1:GROUP, 1024:1536] * has_prev
    u_next_row = halo_next[0:1, 512:1024] * halo_next[0:1, 1024:1536] * has_next
    u_down, u_up = _shifted(u, u_prev_row, u_next_row)
    return gate_b, gate_c, hid, u, u_down, u_up


def _post_specs(tm, t):
    pc_prev, pc_next = _halo_specs(tm, D_CONV3, t)
    col = lambda c: pl.BlockSpec((tm, D_RWKV), lambda i: (i, c))
    return ([col(0), col(0), col(0), col(0), col(0), col(2), col(0), _row(tm, D_CONV3), pc_prev, pc_next,
             _fixed((8, D_RWKV)), _fixed((D_RWKV, D_RWKV))])


def _post_fwd(y_f, y_b, ps, kd_f, kd_b, gate, pc, qvec, ones_blocks, tm, seq):
    t = ps.shape[0]

    def body(yf_ref, yb_ref, r_ref, kdf_ref, kdb_ref, v_ref, g_ref, pc_ref, hp_ref, hn_ref, qv_ref, ones_ref,
             o_ref, ot_ref):
        has_prev, has_next = _edge_flags(tm, seq)
        vec = [jnp.broadcast_to(qv_ref[j:j + 1, :], (tm, D_RWKV)) for j in range(7)]
        o_rwkv = _post_math(yf_ref[...] + yb_ref[...], r_ref[...], kdf_ref[...], kdb_ref[...], v_ref[...],
                            g_ref[...], vec[0], vec[1], vec[2], vec[3], ones_ref[...])
        gate_b, _, _, u, u_down, u_up = _conv_parts(pc_ref[...], hp_ref[...], hn_ref[...], has_prev, has_next)
        o_conv = gate_b * (vec[4] * u_down + vec[5] * u + vec[6] * u_up)
        for half, val in enumerate((o_rwkv, o_conv)):
            o_ref[:, D_RWKV * half:D_RWKV * (half + 1)] = val.astype(BF16)
            ot_ref[D_RWKV * half:D_RWKV * (half + 1), :] = jnp.transpose(val).astype(BF16)

    return pl.pallas_call(
        body, name="post_fwd", grid=(t // tm,), in_specs=_post_specs(tm, t),
        out_specs=[_row(tm, D_MODEL), _col(tm, D_MODEL)],
        out_shape=[_sds((t, D_MODEL), BF16), _sds((D_MODEL, t), BF16)], compiler_params=_cp("parallel"),
    )(y_f, y_b, ps, kd_f, kd_b, ps, gate, pc, pc, pc, qvec, ones_blocks)


def _post_bwd(d_out, y_f, y_b, ps, kd_f, kd_b, gate, pc, qvec, ones_blocks, tm, seq):
    t = ps.shape[0]
    do_prev, do_next = _halo_specs(tm, D_MODEL, t)

    def body(do_ref, dop_ref, don_ref, yf_ref, yb_ref, r_ref, kdf_ref, kdb_ref, v_ref, g_ref, pc_ref, hp_ref,
             hn_ref, qv_ref, ones_ref, dy_ref, dr_ref, dkdf_ref, dkdb_ref, dv_ref, dg_ref, dpc_ref, dqv_ref):
        has_prev, has_next = _edge_flags(tm, seq)
        vec = [jnp.broadcast_to(qv_ref[j:j + 1, :], (tm, D_RWKV)) for j in range(7)]
        ones_v = ones_ref[...]
        args = (yf_ref[...] + yb_ref[...], r_ref[...], kdf_ref[...], kdb_ref[...], v_ref[...], g_ref[...],
                vec[0], vec[1], vec[2], vec[3])
        _, vjp = jax.vjp(lambda *a: _post_math(*a, ones_v), *args)
        grads = vjp(do_ref[:, 0:D_RWKV])
        for o_ref, gval in zip((dy_ref, dr_ref, dkdf_ref, dkdb_ref, dv_ref, dg_ref), grads[0:6]):
            o_ref[...] = gval

        hp, hn = hp_ref[...], hn_ref[...]
        gate_b, gate_c, hid, u, u_down, u_up = _conv_parts(pc_ref[...], hp, hn, has_prev, has_next)
        d_oc = do_ref[:, D_RWKV:2 * D_RWKV]
        d_cu = d_oc * gate_b
        d_cu_prev = dop_ref[GROUP - 1:GROUP, D_RWKV:2 * D_RWKV] * hp[GROUP - 1:GROUP, 0:512] * has_prev
        d_cu_next = don_ref[0:1, D_RWKV:2 * D_RWKV] * hn[0:1, 0:512] * has_next
        d_cu_down, d_cu_up = _shifted(d_cu, d_cu_prev, d_cu_next)
        d_u = vec[5] * d_cu + vec[4] * d_cu_up + vec[6] * d_cu_down
        dpc_ref[:, 0:512] = (d_oc * (vec[4] * u_down + vec[5] * u + vec[6] * u_up)).astype(BF16)
        dpc_ref[:, 512:1024] = (d_u * hid).astype(BF16)
        dpc_ref[:, 1024:1536] = (d_u * gate_c).astype(BF16)

        @pl.when(pl.program_id(0) == 0)
        def _():
            dqv_ref[...] = jnp.zeros_like(dqv_ref)

        vec_grads = list(grads[6:10]) + [d_cu * u_down, d_cu * u, d_cu * u_up]
        for j, gval in enumerate(vec_grads):
            dqv_ref[j:j + 1, :] += jnp.sum(gval, axis=0, keepdims=True)

    return pl.pallas_call(
        body, name="post_bwd", grid=(t // tm,),
        in_specs=[_row(tm, D_MODEL), do_prev, do_next] + _post_specs(tm, t),
        out_specs=[_row(tm, D_RWKV)] * 6 + [_row(tm, D_CONV3), _fixed((8, D_RWKV))],
        out_shape=[_sds((t, D_RWKV))] * 6 + [_sds((t, D_CONV3), BF16), _sds((8, D_RWKV))],
        compiler_params=_cp("arbitrary"),
    )(d_out, d_out, d_out, y_f, y_b, ps, kd_f, kd_b, ps, gate, pc, pc, pc, qvec, ones_blocks)


def _adamw(w, g, m, v, name):
    r, c = w.shape[-2:]
    tr = _tile(r, 256, 8)
    if w.ndim == 3:
        spec = pl.BlockSpec((None, tr, c), lambda i: (0, i, 0))
    else:
        spec = pl.BlockSpec((tr, c), lambda i: (i, 0))

    def body(w_ref, g_ref, m_ref, v_ref, d_ref, nm_ref, nv_ref):
        gv = g_ref[...]
        m2 = ADAM_B1 * m_ref[...] + (1.0 - ADAM_B1) * gv
        v2 = ADAM_B2 * v_ref[...] + (1.0 - ADAM_B2) * (gv * gv)
        m_hat = m2 / (1.0 - ADAM_B1 ** ADAM_STEP)
        v_hat = v2 / (1.0 - ADAM_B2 ** ADAM_STEP)
        d_ref[...] = -ADAM_LR * (m_hat / (jnp.sqrt(v_hat) + ADAM_EPS) + ADAM_WD * w_ref[...])
        nm_ref[...] = m2
        nv_ref[...] = v2

    return pl.pallas_call(
        body, name=name, grid=(r // tr,), in_specs=[spec] * 4, out_specs=[spec] * 3,
        out_shape=[_sds(w.shape)] * 3, compiler_params=_cp("parallel"))(w, g, m, v)


_ANY = pl.BlockSpec(memory_space=pl.ANY)


def _place():
    return lax.axis_index("x"), lax.axis_index("y"), lax.axis_index("c")


def _other_chips(x, y):
    return [(1 - x, y), (x, 1 - y), (1 - x, 1 - y)]


def _remote(src, dst, send_sems, recv_sems, k, to):
    return pltpu.make_async_remote_copy(src_ref=src, dst_ref=dst, send_sem=send_sems.at[k],
                                        recv_sem=recv_sems.at[k], device_id=to, device_id_type=MESH)


def _gather_weights(pack):
    rows, width = pack.shape
    half = rows // 2

    def body(x_ref, out_ref, send_sems, recv_sems):
        x, y, c = _place()
        sibling = (x, y, 1 - c)
        chips = _other_chips(x, y)

        def block(chip, part):
            return out_ref.at[2 * chip[0] + chip[1], pl.ds(part * half, half), :]

        first = [_remote(x_ref.at[pl.ds(c * half, half), :], block((x, y), c), send_sems, recv_sems, j, (*chip, c))
                 for j, chip in enumerate(chips)]
        for cp in first:
            cp.start()
        passed = [_remote(block(chip, c), block(chip, c), send_sems, recv_sems, 3 + j, sibling)
                  for j, chip in enumerate(chips)]
        for j, chip in enumerate(chips):
            _remote(block(chip, c), block(chip, c), send_sems, recv_sems, j, sibling).wait_recv()
            passed[j].start()
        for j, chip in enumerate(chips):
            _remote(block(chip, 1 - c), block(chip, 1 - c), send_sems, recv_sems, 3 + j, sibling).wait_recv()
        for cp in first + passed:
            cp.wait_send()

    return pl.pallas_call(
        body, name="gather_weights", in_specs=[_ANY], out_specs=_ANY,
        out_shape=_sds((N_SHARD, rows, width), pack.dtype),
        scratch_shapes=[pltpu.SemaphoreType.DMA((6,)), pltpu.SemaphoreType.DMA((6,))],
    )(pack)


def _swap_with_sibling(block, name):
    def body(x_ref, out_ref, send_sems, recv_sems):
        x, y, c = _place()
        cp = _remote(x_ref, out_ref, send_sems, recv_sems, 0, (x, y, 1 - c))
        cp.start()
        cp.wait()

    return pl.pallas_call(
        body, name=name, in_specs=[_ANY], out_specs=_ANY, out_shape=_sds(block.shape, block.dtype),
        scratch_shapes=[pltpu.SemaphoreType.DMA((1,)), pltpu.SemaphoreType.DMA((1,))],
    )(block)


def _swap_other_half(packed):
    slots, rows, width = packed.shape
    half = rows // 2

    def body(x_ref, out_ref, send_sems, recv_sems):
        x, y, c = _place()
        cp = _remote(x_ref.at[:, pl.ds((1 - c) * half, half), :], out_ref, send_sems, recv_sems, 0, (x, y, 1 - c))
        cp.start()
        cp.wait()

    return pl.pallas_call(
        body, name="swap_halves", in_specs=[_ANY], out_specs=_ANY, out_shape=_sds((slots, half, width)),
        scratch_shapes=[pltpu.SemaphoreType.DMA((1,)), pltpu.SemaphoreType.DMA((1,))],
    )(packed)


def _add_halves(packed, got, c):
    slots, rows, width = packed.shape
    half = rows // 2
    tr = _tile(half, 408, 16)
    per = half // tr
    block = (None, tr, width)

    def body(c_ref, mine_ref, got_ref, sum_ref, sum16_ref):
        acc = mine_ref[...] + got_ref[...]
        sum_ref[...] = acc
        sum16_ref[...] = acc.astype(BF16)

    plain = pl.BlockSpec(block, lambda s, i, c_ref: (s, i, 0))
    grid_spec = pltpu.PrefetchScalarGridSpec(
        num_scalar_prefetch=1, grid=(slots, per),
        in_specs=[pl.BlockSpec(block, lambda s, i, c_ref: (s, c_ref[0] * per + i, 0)), plain],
        out_specs=[plain, plain])
    return pl.pallas_call(
        body, name="add_halves", grid_spec=grid_spec,
        out_shape=[_sds((slots, half, width)), _sds((slots, half, width), BF16)],
        compiler_params=_cp("parallel", "parallel"))(c.reshape(1).astype(jnp.int32), packed, got)


def _add_quarters(chip_sum, others, chip):
    _, rows, width = chip_sum.shape
    tr = _tile(rows, 408, 16)

    def body(chip_ref, own_ref, others_ref, o_ref):
        acc = own_ref[...]
        for j in range(3):
            acc = acc + others_ref[j].astype(F32)
        o_ref[...] = acc

    grid_spec = pltpu.PrefetchScalarGridSpec(
        num_scalar_prefetch=1, grid=(rows // tr,),
        in_specs=[pl.BlockSpec((None, tr, width), lambda i, chip_ref: (chip_ref[0], i, 0)),
                  pl.BlockSpec((3, tr, width), lambda i, chip_ref: (0, i, 0))],
        out_specs=pl.BlockSpec((tr, width), lambda i, chip_ref: (i, 0)))
    return pl.pallas_call(
        body, name="add_quarters", grid_spec=grid_spec, out_shape=_sds((rows, width)),
        compiler_params=_cp("parallel"))(chip.reshape(1).astype(jnp.int32), chip_sum, others)


def _exchange_quarters(parts):
    _, rows, width = parts.shape

    def body(x_ref, out_ref, send_sems, recv_sems):
        x, y, c = _place()
        copies = [_remote(x_ref.at[2 * chip[0] + chip[1]], out_ref.at[j], send_sems, recv_sems, j, (*chip, c))
                  for j, chip in enumerate(_other_chips(x, y))]
        for cp in copies:
            cp.start()
        for cp in copies:
            cp.wait()

    return pl.pallas_call(
        body, name="exchange_quarters", in_specs=[_ANY], out_specs=_ANY,
        out_shape=_sds((3, rows, width), parts.dtype),
        scratch_shapes=[pltpu.SemaphoreType.DMA((3,)), pltpu.SemaphoreType.DMA((3,))],
    )(parts)


def _allreduce_small(vec):
    rows, width = vec.shape
    vmem = pl.BlockSpec(memory_space=pltpu.VMEM)

    def body(x_ref, o_ref, buf_ref, send_sems, recv_sems):
        x, y, c = _place()
        me = 4 * x + 2 * y + c
        buf_ref[me] = x_ref[...]
        copies = []
        for k in range(1, N_DEV):
            peer = (x ^ ((k >> 2) & 1), y ^ ((k >> 1) & 1), c ^ (k & 1))
            copies.append(_remote(x_ref, buf_ref.at[me], send_sems, recv_sems, k - 1, peer))
        for cp in copies:
            cp.start()
        for k in range(1, N_DEV):
            _remote(x_ref, buf_ref.at[me ^ k], send_sems, recv_sems, k - 1, (x, y, c)).wait_recv()
        for cp in copies:
            cp.wait_send()
        total = buf_ref[0]
        for d in range(1, N_DEV):
            total = total + buf_ref[d]
        o_ref[...] = total

    return pl.pallas_call(
        body, name="allreduce_small", in_specs=[vmem], out_specs=vmem, out_shape=_sds((rows, width)),
        scratch_shapes=[pltpu.VMEM((N_DEV, rows, width), F32), pltpu.SemaphoreType.DMA((N_DEV - 1,)),
                        pltpu.SemaphoreType.DMA((N_DEV - 1,))],
    )(vec)


def _rows1024(a):
    return a.reshape(-1, 1024)


def _pad_rows(a, rows):
    return jnp.concatenate([a, jnp.zeros((rows - a.shape[0], a.shape[1]), a.dtype)], axis=0)


_TRANSPOSED = ("w_in", "w_gate", "w_up")
_SMALL_SHARDED = ("w_up_f", "w_up_b", "a_up_f", "a_up_b", "g_up")
_BIG_SHARDED = ("w_in", "w_out", "w_gate", "w_up", "w_down")


def _pack_weight_shards(w):
    conv_bits = lax.bitcast_convert_type(w["conv_w"], BF16).reshape(1, -1)
    conv_row = jnp.concatenate([conv_bits, jnp.zeros((1, 1024 - conv_bits.shape[1]), BF16)], axis=1)
    parts = []
    for name, _ in _PACK_ROWS[:-1]:
        a = w[name].astype(BF16)
        parts.append(a.T if name in _TRANSPOSED else _rows1024(a))
    return _pad_rows(jnp.concatenate(parts + [conv_row], axis=0), PACK_R)


def _unpack_gathered(gathered):
    out, row = {}, 0
    for name, n in _PACK_ROWS:
        out[name] = gathered[:, row:row + n]
        row += n
    cols = lambda a, k: jnp.concatenate([a[s].reshape(k, -1) for s in range(N_SHARD)], axis=1)
    conv = lax.bitcast_convert_type(out["conv_w"][:, 0, :768].reshape(N_SHARD, 3, 128, 2), F32)
    full = {name: out[name].reshape(-1, 1024).T for name in _TRANSPOSED}
    full.update(w_out=out["w_out"].reshape(D_MODEL, D_MODEL), w_down=out["w_down"].reshape(D_FF, D_MODEL),
                conv_w=jnp.concatenate([conv[s] for s in range(N_SHARD)], axis=1))
    full.update({name: cols(out[name], D_GATE if name == "g_up" else D_LORA) for name in _SMALL_SHARDED})
    return full


def _pack_grads(g):
    col_split = lambda a, s: a[:, s * (a.shape[1] // N_SHARD):(s + 1) * (a.shape[1] // N_SHARD)]
    row_split = lambda a, s: a[s * (a.shape[0] // N_SHARD):(s + 1) * (a.shape[0] // N_SHARD)]
    by_rows = {name: (g[name].T if name in _TRANSPOSED else g[name]) for name in _BIG_SHARDED}
    used = sum(n for _, n in _PACK_ROWS)
    parts = []
    for s in range(N_SHARD):
        conv = col_split(g["conv_w"], s).reshape(1, -1)
        parts += [row_split(by_rows[name], s) for name in _BIG_SHARDED]
        parts += [_rows1024(col_split(g[name], s)) for name in _SMALL_SHARDED]
        parts += [jnp.concatenate([conv, jnp.zeros((1, 1024 - conv.shape[1]), F32)], axis=1),
                  jnp.zeros((PACK_R - used, 1024), F32)]
    return jnp.concatenate(parts, axis=0).reshape(N_SHARD, PACK_R, 1024)


def _unpack_grad_shard(pack):
    small_shapes = {name: (D_GATE if name == "g_up" else D_LORA, 128) for name in _SMALL_SHARDED}
    out, row = {}, 0
    for name, n in _PACK_ROWS[:-1]:
        piece = pack[row:row + n]
        out[name] = piece.T if name in _TRANSPOSED else piece.reshape(small_shapes.get(name, piece.shape))
        row += n
    out["conv_w"] = pack[row, :384].reshape(3, 128)
    return out


_SMALL_LAYOUT = (("norm1_w", 1024), ("mu_shift", D_SHIFT), ("w0_f", 512), ("w0_b", 512), ("a0_f", 512),
                 ("a0_b", 512), ("k_k", 512), ("k_a_f", 512), ("k_a_b", 512), ("r_k_f", 512), ("r_k_b", 512),
                 ("gn_w", 512), ("gn_b", 512), ("norm2_w", 1024), ("norm_f_w", 1024), ("loss", 1))


def _pack_small(vals):
    rows = []
    for name, n in _SMALL_LAYOUT:
        flat = vals[name].reshape(-1)
        n_rows = -(-n // 1024)
        rows.append(jnp.concatenate([flat, jnp.zeros((n_rows * 1024 - n,), F32)]).reshape(n_rows, 1024))
    return _pad_rows(jnp.concatenate(rows, axis=0), SMALL_ROWS)


def _unpack_small(pack):
    out, row = {}, 0
    for name, n in _SMALL_LAYOUT:
        n_rows = -(-n // 1024)
        out[name] = pack[row:row + n_rows].reshape(-1)[:n]
        row += n_rows
    return out


_WEIGHTS = ("norm1_w", "w_in", "mu_shift", "w_up_f", "w0_f", "w_up_b", "w0_b", "a_up_f", "a0_f", "a_up_b", "a0_b",
            "g_up", "k_k", "k_a_f", "k_a_b", "r_k_f", "r_k_b", "gn_w", "gn_b", "conv_w", "w_out", "norm2_w",
            "w_gate", "w_up", "w_down", "norm_f_w")


def _train_step(x, loss_target, w, m, v):
    batch, seq, _ = x.shape
    t = batch * seq
    tm = _tile(seq, 256, 8)
    xs = x.reshape(t, D_MODEL)
    target = loss_target.reshape(t, D_MODEL)
    vec = lambda name: w[name].reshape(1, -1)

    local = {name: w[name][0] for name, _ in _PACK_ROWS}
    c = lax.axis_index("c")
    chip = 2 * lax.axis_index("x") + lax.axis_index("y")
    pack = _pack_weight_shards(local)
    full = _unpack_gathered(lax.dynamic_update_slice(_gather_weights(pack), pack[None], (chip, 0, 0)))
    w_in = full["w_in"]
    w_shift = jnp.concatenate([w_in[:, :D_SHIFT], jnp.zeros((D_MODEL, D_SHIFT_PAD - D_SHIFT), BF16)], axis=1)
    w_conv = w_in[:, D_SHIFT:]
    zeros_lora = jnp.zeros((D_LORA, D_RWKV), F32)
    lora = lambda name: full[name].astype(F32)
    mats = (jnp.concatenate([lora("w_up_f"), zeros_lora]), jnp.concatenate([zeros_lora, lora("a_up_f")]),
            jnp.concatenate([lora("w_up_b"), zeros_lora]), jnp.concatenate([zeros_lora, lora("a_up_b")]),
            jnp.concatenate([lora("g_up"), jnp.zeros((D_GATE_PAD - D_GATE, D_RWKV), F32)]))
    mu = jnp.concatenate([vec("mu_shift"), jnp.zeros((1, D_SHIFT_PAD - D_SHIFT), F32)], axis=1)
    mu = jnp.broadcast_to(mu, (GROUP, D_SHIFT_PAD))
    zero_row = jnp.zeros((1, D_RWKV), F32)
    pvec = jnp.concatenate([vec("k_k"), vec("w0_f"), vec("a0_f"), vec("k_a_f"), vec("w0_b"), vec("a0_b"),
                            vec("k_a_b"), zero_row], axis=0)
    qvec = jnp.concatenate([vec("gn_w"), vec("gn_b"), vec("r_k_f"), vec("r_k_b"), full["conv_w"], zero_row], axis=0)
    ones_blocks = _head_ones()

    h1, h1_t = _rmsnorm_fwd(xs, vec("norm1_w"), tm, "norm1_fwd")
    p_shift = _matmul(h1, w_shift, mode="nn", name="in_proj_shift")
    pc = _matmul(h1, w_conv, mode="nn", name="in_proj_conv")
    ps = _shift_fwd(p_shift, mu, tm, seq)
    kk, w_f, kd_f, b_f, w_b, kd_b, b_b, gate = _prep_fwd(ps, pvec, mats, ones_blocks, tm)
    dirs = ((w_f, kd_f, b_f), (w_b, kd_b, b_b))
    y_f, y_b, hist_f, hist_b = _scan_fwd(ps, kk, dirs, batch, seq)
    mixed, mixed_t = _post_fwd(y_f, y_b, ps, kd_f, kd_b, gate, pc, qvec, ones_blocks, tm, seq)
    x1 = _matmul(mixed, full["w_out"], mode="nn", name="out_proj", add=xs)
    h2, h2_t = _rmsnorm_fwd(x1, vec("norm2_w"), tm, "norm2_fwd")
    ff_gate = _matmul(h2, full["w_gate"], mode="nn", name="ffn_gate", out_dtype=BF16)
    ff_up = _matmul(h2, full["w_up"], mode="nn", name="ffn_up", out_dtype=BF16)
    act, act_t = _swiglu_fwd(ff_gate, ff_up, tm)
    x2 = _matmul(act, full["w_down"], mode="nn", name="ffn_down", add=x1)
    d_x2, d_norm_f, loss_part = _loss_head(x2, w["norm_f_w"].reshape(1, -1), target, tm)

    g = {}
    d_act = _matmul(d_x2, full["w_down"], mode="nt", name="ffn_down_dx", out_dtype=BF16)
    g["w_down"] = _matmul(act_t, d_x2, mode="nn", name="ffn_down_dw")
    d_gate, d_up = _swiglu_bwd(ff_gate, ff_up, d_act, tm)
    d_h2 = _matmul(d_gate, full["w_gate"], mode="nt", name="ffn_gate_dx")
    d_h2 = _matmul(d_up, full["w_up"], mode="nt", name="ffn_up_dx", add=d_h2)
    g["w_gate"] = _matmul(h2_t, d_gate, mode="nn", name="ffn_gate_dw")
    g["w_up"] = _matmul(h2_t, d_up, mode="nn", name="ffn_up_dw")
    d_x1, d_norm2 = _rmsnorm_bwd(x1, vec("norm2_w"), d_h2, d_x2, tm, "norm2_bwd")
    d_mixed = _matmul(d_x1, full["w_out"], mode="nt", name="out_proj_dx")
    g["w_out"] = _matmul(mixed_t, d_x1, mode="nn", name="out_proj_dw")
    dy, dr_o, dkdf_o, dkdb_o, dv_o, d_gatev, d_pc, d_qvec = _post_bwd(
        d_mixed, y_f, y_b, ps, kd_f, kd_b, gate, pc, qvec, ones_blocks, tm, seq)
    (dr_f, dw_f, dkd_f, dv_f, dkk_f, db_f), (dr_b, dw_b, dkd_b, dv_b, dkk_b, db_b) = _scan_bwd(
        ps, kk, dirs, dy, hist_f, hist_b, batch, seq)
    cts = [[dr_f, dr_b, dr_o], [dv_f, dv_b, dv_o], [dkk_f, dkk_b], [dw_f], [dkd_f, dkdf_o], [db_f],
           [dw_b], [dkd_b, dkdb_o], [db_b], [d_gatev]]
    q, d_pvec, d_m0, d_m1, d_m2, d_m3, d_m4 = _prep_bwd(ps, pvec, mats, ones_blocks, cts, tm)
    d_pshift, d_mu = _shift_bwd(q, p_shift, mu, tm, seq)
    d_h1 = _matmul(d_pshift, w_shift, mode="nt", name="in_proj_shift_dx")
    d_h1 = _matmul(d_pc, w_conv, mode="nt", name="in_proj_conv_dx", add=d_h1)
    d_w_shift = _matmul(h1_t, d_pshift, mode="nn", name="in_proj_shift_dw")
    d_w_conv = _matmul(h1_t, d_pc, mode="nn", name="in_proj_conv_dw")
    g["w_in"] = jnp.concatenate([d_w_shift[:, :D_SHIFT], d_w_conv], axis=1)
    d_x, d_norm1 = _rmsnorm_bwd(xs, vec("norm1_w"), d_h1, d_x1, tm, "norm1_bwd")
    g["w_up_f"], g["a_up_f"] = d_m0[:D_LORA], d_m1[D_LORA:]
    g["w_up_b"], g["a_up_b"] = d_m2[:D_LORA], d_m3[D_LORA:]
    g["g_up"] = d_m4[:D_GATE]
    g["conv_w"] = d_qvec[4:7]

    packed = _pack_grads(g)
    chip_sum, chip_sum_bf16 = _add_halves(packed, _swap_other_half(packed), c)
    others = _exchange_quarters(chip_sum_bf16)
    eighth = _add_quarters(chip_sum, others, chip)
    other_eighth = _swap_with_sibling(eighth, "swap_eighths")
    grads = _unpack_grad_shard(jnp.concatenate([jnp.where(c == 0, eighth, other_eighth),
                                                jnp.where(c == 0, other_eighth, eighth)], axis=0))

    small = dict(norm1_w=d_norm1, mu_shift=d_mu[:, :D_SHIFT], w0_f=d_pvec[1], w0_b=d_pvec[4], a0_f=d_pvec[2],
                 a0_b=d_pvec[5], k_k=d_pvec[0], k_a_f=d_pvec[3], k_a_b=d_pvec[6], r_k_f=d_qvec[2], r_k_b=d_qvec[3],
                 gn_w=d_qvec[0], gn_b=d_qvec[1], norm2_w=d_norm2, norm_f_w=d_norm_f, loss=loss_part)
    reduced = _unpack_small(_allreduce_small(_pack_small(small)))
    loss = reduced.pop("loss")[0]
    grads.update(reduced)

    outs = {}
    for name in _WEIGHTS:
        shape = w[name].shape
        as2d = (1, shape[0]) if len(shape) == 1 else shape
        grad = grads[name].reshape(shape)
        delta, new_m, new_v = _adamw(w[name].reshape(as2d), grad.reshape(as2d), m[name].reshape(as2d),
                                     v[name].reshape(as2d), "adamw_" + name)
        outs[name] = (grad, delta.reshape(shape), new_m.reshape(shape), new_v.reshape(shape))
    d_x = d_x.reshape(batch, seq, D_MODEL)
    return (loss, d_x) + tuple(outs[name][k] for k in range(4) for name in _WEIGHTS)


def kernel(x, norm1_w, w_in, mu_shift, w_up_f, w0_f, w_up_b, w0_b, a_up_f, a0_f, a_up_b, a0_b, g_up, k_k, k_a_f, k_a_b, r_k_f, r_k_b, gn_w, gn_b, conv_w, w_out, norm2_w, w_gate, w_up, w_down, norm_f_w, loss_target, m_norm1_w, m_w_in, m_mu_shift, m_w_up_f, m_w0_f, m_w_up_b, m_w0_b, m_a_up_f, m_a0_f, m_a_up_b, m_a0_b, m_g_up, m_k_k, m_k_a_f, m_k_a_b, m_r_k_f, m_r_k_b, m_gn_w, m_gn_b, m_conv_w, m_w_out, m_norm2_w, m_w_gate, m_w_up, m_w_down, m_norm_f_w, v_norm1_w, v_w_in, v_mu_shift, v_w_up_f, v_w0_f, v_w_up_b, v_w0_b, v_a_up_f, v_a0_f, v_a_up_b, v_a0_b, v_g_up, v_k_k, v_k_a_f, v_k_a_b, v_r_k_f, v_r_k_b, v_gn_w, v_gn_b, v_conv_w, v_w_out, v_norm2_w, v_w_gate, v_w_up, v_w_down, v_norm_f_w):
    args = locals()
    w = {name: args[name] for name in _WEIGHTS}
    m = {name: args["m_" + name] for name in _WEIGHTS}
    v = {name: args["v_" + name] for name in _WEIGHTS}
    return _train_step(x, loss_target, w, m, v)
```

```python
import functools

import jax
import jax.numpy as jnp
from jax import lax
from jax.experimental import pallas as pl
from jax.experimental.pallas import tpu as pltpu

F32 = jnp.float32
BF16 = jnp.bfloat16
MESH = pl.DeviceIdType.MESH

D_MODEL = 1024
D_RWKV = 512
HEAD = 64
N_PAIR = D_RWKV // (2 * HEAD)
D_LORA = 64
D_GATE = 160
D_GATE_PAD = 384
D_FF = 2816
D_SHIFT = 1824
D_SHIFT_PAD = 2048
D_CONV3 = 1536
LOG_DECAY_SCALE = 0.606531
RMS_EPS = 1e-6
GN_EPS = 64e-5
NORM_EPS = 1e-12
ADAM_LR, ADAM_B1, ADAM_B2, ADAM_EPS, ADAM_WD, ADAM_STEP = 0.001, 0.9, 0.999, 1e-08, 0.01, 10

N_SHARD = 4
N_DEV = 8
V7X_VMEM_LIMIT = 48 * 1024 * 1024
SCAN_CHUNK = 32
GROUP = 8

_PACK_ROWS = (("w_in", 840), ("w_out", 256), ("w_gate", 704), ("w_up", 704), ("w_down", 704),
              ("w_up_f", 8), ("w_up_b", 8), ("a_up_f", 8), ("a_up_b", 8), ("g_up", 20), ("conv_w", 1))
PACK_R = 3264
PACK_H = PACK_R // 2
SMALL_ROWS = 24


def _tile(n, cap, mult=128):
    best = None
    t = mult
    while t <= min(n, cap):
        if n % t == 0:
            best = t
        t += mult
    return best or n


def _cp(*sem):
    return pltpu.CompilerParams(dimension_semantics=sem or None, vmem_limit_bytes=V7X_VMEM_LIMIT)


def _sds(shape, dtype=F32):
    return jax.ShapeDtypeStruct(shape, dtype)


def _matmul(a, b, *, mode, name, out_dtype=F32, add=None):
    m, kdim = a.shape
    n = b.shape[1] if mode == "nn" else b.shape[0]
    tm, tn = _tile(m, 512, 8), _tile(n, 1536)
    tk = kdim if kdim <= 3584 else _tile(kdim, 1024)
    nk = kdim // tk
    a_spec = pl.BlockSpec((tm, tk), lambda i, j, k: (i, k))
    if mode == "nn":
        b_spec = pl.BlockSpec((tk, tn), lambda i, j, k: (k, j))
        dims = (((1,), (0,)), ((), ()))
    else:
        b_spec = pl.BlockSpec((tn, tk), lambda i, j, k: (j, k))
        dims = (((1,), (1,)), ((), ()))
    has_add = add is not None

    def body(*refs):
        a_ref, b_ref = refs[0], refs[1]
        add_ref = refs[2] if has_add else None
        o_ref = refs[3] if has_add else refs[2]
        part = lax.dot_general(a_ref[...].astype(BF16), b_ref[...].astype(BF16), dims,
                               preferred_element_type=F32)
        if nk == 1:
            if has_add:
                part = part + add_ref[...]
            o_ref[...] = part.astype(out_dtype)
        else:
            acc_ref = refs[-1]
            k = pl.program_id(2)

            @pl.when(k == 0)
            def _():
                acc_ref[...] = jnp.zeros_like(acc_ref)

            acc_ref[...] += part

            @pl.when(k == nk - 1)
            def _():
                res = acc_ref[...]
                if has_add:
                    res = res + add_ref[...]
                o_ref[...] = res.astype(out_dtype)

    o_spec = pl.BlockSpec((tm, tn), lambda i, j, k: (i, j))
    in_specs = [a_spec, b_spec] + ([o_spec] if has_add else [])
    args = (a, b) + ((add,) if has_add else ())
    return pl.pallas_call(
        body, name=name, grid=(m // tm, n // tn, nk), in_specs=in_specs, out_specs=o_spec,
        out_shape=_sds((m, n), out_dtype),
        scratch_shapes=[pltpu.VMEM((tm, tn), F32)] if nk > 1 else [],
        compiler_params=_cp("parallel", "parallel", "arbitrary"),
    )(*args)


def _row(tm, width):
    return pl.BlockSpec((tm, width), lambda i: (i, 0))


def _col(tm, height):
    return pl.BlockSpec((height, tm), lambda i: (0, i))


def _fixed(shape):
    return pl.BlockSpec(shape, lambda i: tuple(0 for _ in shape))


def _rmsnorm_fwd(x, w, tm, name):
    t, d = x.shape

    def body(x_ref, w_ref, o_ref, ot_ref):
        xv = x_ref[...]
        rstd = lax.rsqrt(jnp.mean(xv * xv, axis=-1, keepdims=True) + RMS_EPS)
        yv = xv * rstd * w_ref[...]
        o_ref[...] = yv.astype(BF16)
        ot_ref[...] = jnp.transpose(yv).astype(BF16)

    return pl.pallas_call(
        body, name=name, grid=(t // tm,), in_specs=[_row(tm, d), _fixed((1, d))],
        out_specs=[_row(tm, d), _col(tm, d)], out_shape=[_sds((t, d), BF16), _sds((d, t), BF16)],
        compiler_params=_cp("parallel"))(x, w)


def _rms_bwd_math(xv, wv, dyv):
    rstd = lax.rsqrt(jnp.mean(xv * xv, axis=-1, keepdims=True) + RMS_EPS)
    xhat = xv * rstd
    gv = dyv * wv
    dx = rstd * (gv - xhat * jnp.mean(gv * xhat, axis=-1, keepdims=True))
    return dx, jnp.sum(dyv * xhat, axis=0, keepdims=True)


def _rmsnorm_bwd(x, w, dy, dres, tm, name):
    t, d = x.shape

    def body(x_ref, w_ref, dy_ref, dres_ref, dx_ref, dw_ref):
        dx, dw = _rms_bwd_math(x_ref[...], w_ref[...], dy_ref[...])
        dx_ref[...] = dres_ref[...] + dx

        @pl.when(pl.program_id(0) == 0)
        def _():
            dw_ref[...] = jnp.zeros_like(dw_ref)

        dw_ref[...] += dw

    return pl.pallas_call(
        body, name=name, grid=(t // tm,),
        in_specs=[_row(tm, d), _fixed((1, d)), _row(tm, d), _row(tm, d)],
        out_specs=[_row(tm, d), _fixed((1, d))],
        out_shape=[_sds((t, d)), _sds((1, d))], compiler_params=_cp("arbitrary"))(x, w, dy, dres)


def _loss_head(x, w, target, tm):
    t, d = x.shape

    def body(x_ref, w_ref, t_ref, dx_ref, dw_ref, loss_ref):
        xv, wv = x_ref[...], w_ref[...]
        rstd = lax.rsqrt(jnp.mean(xv * xv, axis=-1, keepdims=True) + RMS_EPS)
        err = xv * rstd * wv - t_ref[...]
        dx, dw = _rms_bwd_math(xv, wv, err * (1.0 / d))
        dx_ref[...] = dx

        @pl.when(pl.program_id(0) == 0)
        def _():
            dw_ref[...] = jnp.zeros_like(dw_ref)
            loss_ref[...] = jnp.zeros_like(loss_ref)

        dw_ref[...] += dw
        loss_ref[...] += 0.5 * jnp.sum(jnp.mean(err * err, axis=-1, keepdims=True), axis=0, keepdims=True)

    return pl.pallas_call(
        body, name="loss_head", grid=(t // tm,),
        in_specs=[_row(tm, d), _fixed((1, d)), _row(tm, d)],
        out_specs=[_row(tm, d), _fixed((1, d)), _fixed((1, 1))],
        out_shape=[_sds((t, d)), _sds((1, d)), _sds((1, 1))], compiler_params=_cp("arbitrary"))(x, w, target)


def _ffn_in(h, w_gate, w_up):
    t, d = h.shape
    f = w_gate.shape[1]
    tm, tn = _tile(t, 512, 8), _tile(f, 1536)

    def body(h_ref, wg_ref, wu_ref, g_ref, u_ref, a_ref, at_ref):
        hv = h_ref[...]
        gv = jnp.dot(hv, wg_ref[...], preferred_element_type=F32)
        uv = jnp.dot(hv, wu_ref[...], preferred_element_type=F32)
        act = gv * jax.nn.sigmoid(gv) * uv
        g_ref[...] = gv.astype(BF16)
        u_ref[...] = uv.astype(BF16)
        a_ref[...] = act.astype(BF16)
        at_ref[...] = jnp.transpose(act).astype(BF16)

    w_spec = pl.BlockSpec((d, tn), lambda i, j: (0, j))
    o_spec = pl.BlockSpec((tm, tn), lambda i, j: (i, j))
    return pl.pallas_call(
        body, name="ffn_in", grid=(t // tm, f // tn),
        in_specs=[pl.BlockSpec((tm, d), lambda i, j: (i, 0)), w_spec, w_spec],
        out_specs=[o_spec, o_spec, o_spec, pl.BlockSpec((tn, tm), lambda i, j: (j, i))],
        out_shape=[_sds((t, f), BF16)] * 3 + [_sds((f, t), BF16)],
        compiler_params=_cp("parallel", "parallel"))(h, w_gate, w_up)


def _ffn_in_bwd(d_out, w_down, gate, up):
    t, d = d_out.shape
    f = w_down.shape[0]
    tm, tn = _tile(t, 512, 8), _tile(f, 1536)

    def body(do_ref, w_ref, g_ref, u_ref, dg_ref, du_ref):
        dv = lax.dot_general(do_ref[...].astype(BF16), w_ref[...], (((1,), (1,)), ((), ())),
                             preferred_element_type=F32)
        gv, uv = g_ref[...].astype(F32), u_ref[...].astype(F32)
        sg = jax.nn.sigmoid(gv)
        du_ref[...] = (dv * gv * sg).astype(BF16)
        dg_ref[...] = (dv * uv * (sg * (1.0 + gv * (1.0 - sg)))).astype(BF16)

    tile = pl.BlockSpec((tm, tn), lambda i, j: (i, j))
    return pl.pallas_call(
        body, name="ffn_in_bwd", grid=(t // tm, f // tn),
        in_specs=[pl.BlockSpec((tm, d), lambda i, j: (i, 0)), pl.BlockSpec((tn, d), lambda i, j: (j, 0)), tile, tile],
        out_specs=[tile, tile], out_shape=[_sds((t, f), BF16)] * 2,
        compiler_params=_cp("parallel", "parallel"))(d_out, w_down, gate, up)


def _halo_specs(tm, width, rows_total):
    per = tm // GROUP
    last = rows_total // GROUP - 1
    prev = pl.BlockSpec((GROUP, width), lambda i: (jnp.maximum(i * per - 1, 0), 0))
    nxt = pl.BlockSpec((GROUP, width), lambda i: (jnp.minimum((i + 1) * per, last), 0))
    return prev, nxt


def _edge_flags(tm, seq):
    i = pl.program_id(0)
    has_prev = jnp.where((i * tm) % seq == 0, 0.0, 1.0).astype(F32)
    has_next = jnp.where(((i + 1) * tm) % seq == 0, 0.0, 1.0).astype(F32)
    return has_prev, has_next


def _shifted(xv, prev_row, next_row):
    tm = xv.shape[0]
    row = lax.broadcasted_iota(jnp.int32, xv.shape, 0)
    down = jnp.where(row == 0, prev_row, pltpu.roll(xv, 1, axis=0))
    up = jnp.where(row == tm - 1, next_row, pltpu.roll(xv, tm - 1, axis=0))
    return down, up


def _shift_fwd(p, mu, tm, seq):
    t, w = p.shape
    prev_spec, next_spec = _halo_specs(tm, w, t)

    def body(p_ref, hp_ref, hn_ref, mu_ref, o_ref):
        has_prev, has_next = _edge_flags(tm, seq)
        xv = p_ref[...]
        down, up = _shifted(xv, hp_ref[GROUP - 1:GROUP, :] * has_prev, hn_ref[0:1, :] * has_next)
        o_ref[...] = xv + mu_ref[0:1, :] * (0.5 * (down + up) - xv)

    return pl.pallas_call(
        body, name="shift_fwd", grid=(t // tm,),
        in_specs=[_row(tm, w), prev_spec, next_spec, _fixed((GROUP, w))], out_specs=_row(tm, w),
        out_shape=_sds((t, w)), compiler_params=_cp("parallel"))(p, p, p, mu)


def _shift_bwd(q, p, mu, tm, seq):
    t, w = p.shape
    prev_spec, next_spec = _halo_specs(tm, w, t)

    def body(q_ref, qp_ref, qn_ref, p_ref, pp_ref, pn_ref, mu_ref, dp_ref, dmu_ref):
        has_prev, has_next = _edge_flags(tm, seq)
        muv = mu_ref[0:1, :]
        qv = q_ref[...]
        mq = muv * qv
        mq_down, mq_up = _shifted(mq, muv * qp_ref[GROUP - 1:GROUP, :] * has_prev,
                                  muv * qn_ref[0:1, :] * has_next)
        dp_ref[...] = (qv - mq + 0.5 * (mq_down + mq_up)).astype(BF16)
        pv = p_ref[...]
        p_down, p_up = _shifted(pv, pp_ref[GROUP - 1:GROUP, :] * has_prev, pn_ref[0:1, :] * has_next)

        @pl.when(pl.program_id(0) == 0)
        def _():
            dmu_ref[...] = jnp.zeros_like(dmu_ref)

        dmu_ref[...] += jnp.sum(qv * (0.5 * (p_down + p_up) - pv), axis=0, keepdims=True)

    return pl.pallas_call(
        body, name="shift_bwd", grid=(t // tm,),
        in_specs=[_row(tm, w), prev_spec, next_spec, _row(tm, w), prev_spec, next_spec, _fixed((GROUP, w))],
        out_specs=[_row(tm, w), _fixed((1, w))],
        out_shape=[_sds((t, w), BF16), _sds((1, w))], compiler_params=_cp("arbitrary"))(q, q, q, p, p, p, mu)


@jax.custom_vjp
def _bdot(a, b):
    return jnp.dot(a.astype(BF16), b.astype(BF16), preferred_element_type=F32)


def _bdot_fwd(a, b):
    return _bdot(a, b), (a, b)


def _bdot_bwd(res, g):
    a, b = res
    gb = g.astype(BF16)
    da = lax.dot_general(gb, b.astype(BF16), (((1,), (1,)), ((), ())), preferred_element_type=F32)
    db = lax.dot_general(a.astype(BF16), gb, (((0,), (0,)), ((), ())), preferred_element_type=F32)
    return da, db


_bdot.defvjp(_bdot_fwd, _bdot_bwd)


def _seg_raw(x, ones_blocks):
    hi = x.astype(BF16)
    lo = (x - hi.astype(F32)).astype(BF16)
    return (jnp.dot(hi, ones_blocks, preferred_element_type=F32)
            + jnp.dot(lo, ones_blocks, preferred_element_type=F32))


@jax.custom_vjp
def _seg(x, ones_blocks):
    return _seg_raw(x, ones_blocks)


def _seg_fwd(x, ones_blocks):
    return _seg_raw(x, ones_blocks), ones_blocks


def _seg_bwd(ones_blocks, g):
    return _seg_raw(g, ones_blocks), jnp.zeros_like(ones_blocks)


_seg.defvjp(_seg_fwd, _seg_bwd)


def _head_ones():
    h = jnp.arange(D_RWKV) // HEAD
    return (h[:, None] == h[None, :]).astype(BF16)


def _prep_math(ps, k_k, w0_f, a0_f, k_a_f, w0_b, a0_b, k_a_b, wup_f, aup_f, wup_b, aup_b, gup, ones_blocks):
    r = ps[:, 0:512]
    k = ps[:, 512:1024]
    v = ps[:, 1024:1536]
    xwa = ps[:, 1536:1664]
    xg = ps[:, 1664:D_SHIFT_PAD]
    kk_raw = k * k_k
    norm = jnp.sqrt(_seg(kk_raw * kk_raw, ones_blocks))
    kk = kk_raw / jnp.maximum(norm, NORM_EPS)
    t_xwa = jnp.tanh(xwa)
    outs = [r, v, kk]
    for w0, a0, k_a, wup, aup in ((w0_f, a0_f, k_a_f, wup_f, aup_f), (w0_b, a0_b, k_a_b, wup_b, aup_b)):
        decay = jnp.exp(-LOG_DECAY_SCALE * jax.nn.sigmoid(w0 + _bdot(t_xwa, wup)))
        rate = jax.nn.sigmoid(a0 + _bdot(xwa, aup))
        outs += [decay, k * (1.0 + (rate - 1.0) * k_a), kk * rate]
    outs.append(_bdot(jax.nn.sigmoid(xg), gup))
    return tuple(outs)


def _prep_args(tm, ps_ref, pv_ref, mat_refs, ones_ref):
    vecs = [jnp.broadcast_to(pv_ref[j:j + 1, :], (tm, D_RWKV)) for j in range(7)]
    return [ps_ref[...]] + vecs + [m[...] for m in mat_refs] + [ones_ref[...]]


_PREP_MAT_SHAPES = ((128, D_RWKV),) * 4 + ((D_GATE_PAD, D_RWKV),)


def _prep_fwd(ps, pvec, mats, ones_blocks, tm):
    t = ps.shape[0]

    def body(ps_ref, pv_ref, m0, m1, m2, m3, m4, ones_ref, *out_refs):
        outs = _prep_math(*_prep_args(tm, ps_ref, pv_ref, (m0, m1, m2, m3, m4), ones_ref))
        for o_ref, val in zip(out_refs, outs[2:]):
            o_ref[...] = val

    return pl.pallas_call(
        body, name="prep_fwd", grid=(t // tm,),
        in_specs=[_row(tm, D_SHIFT_PAD), _fixed((8, D_RWKV))] + [_fixed(s) for s in _PREP_MAT_SHAPES]
        + [_fixed((D_RWKV, D_RWKV))],
        out_specs=[_row(tm, D_RWKV)] * 8, out_shape=[_sds((t, D_RWKV))] * 8,
        compiler_params=_cp("parallel"))(ps, pvec, *mats, ones_blocks)


def _prep_bwd(ps, pvec, mats, ones_blocks, cts, tm):
    t = ps.shape[0]
    counts = [len(c) for c in cts]
    flat = [a for c in cts for a in c]

    def body(ps_ref, pv_ref, m0, m1, m2, m3, m4, ones_ref, *refs):
        ct_refs = refs[:len(flat)]
        q_ref, dpv_ref = refs[len(flat)], refs[len(flat) + 1]
        dmat_refs = refs[len(flat) + 2:]
        args = _prep_args(tm, ps_ref, pv_ref, (m0, m1, m2, m3, m4), ones_ref)
        _, vjp = jax.vjp(lambda *a: _prep_math(*a, args[-1]), *args[:-1])
        ct_vals, pos = [], 0
        for n in counts:
            val = ct_refs[pos][...]
            for extra in ct_refs[pos + 1:pos + n]:
                val = val + extra[...]
            ct_vals.append(val)
            pos += n
        grads = vjp(tuple(ct_vals))
        q_ref[...] = grads[0]

        @pl.when(pl.program_id(0) == 0)
        def _():
            dpv_ref[...] = jnp.zeros_like(dpv_ref)
            for d_ref in dmat_refs:
                d_ref[...] = jnp.zeros_like(d_ref)

        for j in range(7):
            dpv_ref[j:j + 1, :] += jnp.sum(grads[1 + j], axis=0, keepdims=True)
        for d_ref, gm in zip(dmat_refs, grads[8:13]):
            d_ref[...] += gm

    return pl.pallas_call(
        body, name="prep_bwd", grid=(t // tm,),
        in_specs=[_row(tm, D_SHIFT_PAD), _fixed((8, D_RWKV))] + [_fixed(s) for s in _PREP_MAT_SHAPES]
        + [_fixed((D_RWKV, D_RWKV))] + [_row(tm, D_RWKV)] * len(flat),
        out_specs=[_row(tm, D_SHIFT_PAD), _fixed((8, D_RWKV))] + [_fixed(s) for s in _PREP_MAT_SHAPES],
        out_shape=[_sds((t, D_SHIFT_PAD)), _sds((8, D_RWKV))] + [_sds(s) for s in _PREP_MAT_SHAPES],
        compiler_params=_cp("arbitrary"))(ps, pvec, *mats, ones_blocks, *flat)


def _pair_ones():
    h = jnp.arange(2 * HEAD) // HEAD
    block = (h[:, None] == h[None, :]).astype(BF16)
    return jnp.concatenate([block, block], axis=0)


def _diag_mask():
    lane = lax.broadcasted_iota(jnp.int32, (HEAD, 2 * HEAD), 1)
    sub = lax.broadcasted_iota(jnp.int32, (HEAD, 2 * HEAD), 0)
    return jnp.where((lane & (HEAD - 1)) == sub, 1.0, 0.0).astype(F32)


def _to_row(cols, dmask):
    return jnp.sum(cols * dmask, axis=0, keepdims=True)


def _seg_many(exact, rounded, ones_pair):
    out_exact, out_rounded = [], []
    if exact:
        parts = []
        for x in exact:
            hi = x.astype(BF16)
            parts.append(jnp.concatenate([hi, (x - hi.astype(F32)).astype(BF16)], axis=1))
        res = jnp.dot(jnp.concatenate(parts, axis=0), ones_pair, preferred_element_type=F32)
        out_exact = [res[HEAD * c:HEAD * (c + 1)] for c in range(len(exact))]
    if rounded:
        res = jnp.dot(jnp.concatenate([x.astype(BF16) for x in rounded], axis=0), ones_pair[0:2 * HEAD],
                      preferred_element_type=F32)
        out_rounded = [res[HEAD * c:HEAD * (c + 1)] for c in range(len(rounded))]
    return out_exact, out_rounded


N_CHAIN = 2 * N_PAIR


def _chain(c):
    d, p = divmod(c, N_PAIR)
    return d, slice(2 * HEAD * p, 2 * HEAD * (p + 1))


def _scan_specs(n_chunks, col_blocks, fwd_chunk, bwd_chunk):
    def spec(chunk_of, col):
        return pl.BlockSpec((SCAN_CHUNK, D_RWKV), lambda b, g: (b * n_chunks + chunk_of(g), col))
    return [spec(fwd_chunk, c) for c in col_blocks] + [spec(bwd_chunk, c) for c in col_blocks]


def _scan_fwd(ps, kk, dirs, batch, seq):
    t = batch * seq
    n = seq // SCAN_CHUNK
    groups = SCAN_CHUNK // GROUP
    up = lambda g: g
    down = lambda g: n - 1 - g
    col_blocks = (0, 2, 0, 0, 0, 0)

    def body(*refs):
        dir_refs = (refs[0:6], refs[6:12])
        ones_ref = refs[12]
        y_refs, hist_refs, st_ref = refs[13:15], refs[15:17], refs[17]

        @pl.when(pl.program_id(1) == 0)
        def _():
            st_ref[...] = jnp.zeros_like(st_ref)

        ones_pair = ones_ref[...]
        dmask = _diag_mask()
        sub8 = lax.broadcasted_iota(jnp.int32, (GROUP, 2 * HEAD), 0)

        def group(gi, carry):
            off = (pl.multiple_of(gi * GROUP, GROUP), pl.multiple_of((groups - 1 - gi) * GROUP, GROUP))
            loaded = [tuple(ref[pl.ds(off[d], GROUP), :] for ref in dir_refs[d]) for d in range(2)]
            states = list(carry)
            y_acc = [jnp.zeros((GROUP, 2 * HEAD), F32) for _ in range(N_CHAIN)]
            for step in range(GROUP):
                rows, idx = [], []
                for c in range(N_CHAIN):
                    d, lanes = _chain(c)
                    i = step if d == 0 else GROUP - 1 - step
                    idx.append(i)
                    rows.append(tuple(x8[i:i + 1, lanes] for x8 in loaded[d]))
                    hist_refs[d][c % N_PAIR, gi * GROUP + step] = states[c]
                _, v_cols = _seg_many([], [dmask * rows[c][1] for c in range(N_CHAIN)], ones_pair)
                sas, _ = _seg_many([states[c] * rows[c][2] for c in range(N_CHAIN)], [], ones_pair)
                for c in range(N_CHAIN):
                    _, _, _, w_row, kd_row, b_row = rows[c]
                    states[c] = states[c] * w_row - sas[c] * b_row + v_cols[c] * kd_row
                _, ys = _seg_many([], [states[c] * rows[c][0] for c in range(N_CHAIN)], ones_pair)
                for c in range(N_CHAIN):
                    y_acc[c] = jnp.where(sub8 == idx[c], _to_row(ys[c], dmask), y_acc[c])
            for c in range(N_CHAIN):
                d, lanes = _chain(c)
                y_refs[d][pl.ds(off[d], GROUP), lanes] = y_acc[c]
            return tuple(states)

        final = lax.fori_loop(0, groups, group, tuple(st_ref[c] for c in range(N_CHAIN)))
        for c in range(N_CHAIN):
            st_ref[c] = final[c]
            hist_refs[c // N_PAIR][c % N_PAIR, SCAN_CHUNK] = final[c]

    y_spec_f = pl.BlockSpec((SCAN_CHUNK, D_RWKV), lambda b, g: (b * n + up(g), 0))
    y_spec_b = pl.BlockSpec((SCAN_CHUNK, D_RWKV), lambda b, g: (b * n + down(g), 0))
    hist_shape = (batch, n, N_PAIR, SCAN_CHUNK + 1, HEAD, 2 * HEAD)
    hist_block = (None, None, N_PAIR, SCAN_CHUNK + 1, HEAD, 2 * HEAD)
    hist_spec_f = pl.BlockSpec(hist_block, lambda b, g: (b, up(g), 0, 0, 0, 0))
    hist_spec_b = pl.BlockSpec(hist_block, lambda b, g: (b, down(g), 0, 0, 0, 0))
    ones_spec = pl.BlockSpec((4 * HEAD, 2 * HEAD), lambda b, g: (0, 0))
    (wf, kdf, bf), (wb, kdb, bb) = dirs
    return pl.pallas_call(
        body, name="wkv_fwd", grid=(batch, n),
        in_specs=_scan_specs(n, col_blocks, up, down) + [ones_spec],
        out_specs=[y_spec_f, y_spec_b, hist_spec_f, hist_spec_b],
        out_shape=[_sds((t, D_RWKV)), _sds((t, D_RWKV)), _sds(hist_shape), _sds(hist_shape)],
        scratch_shapes=[pltpu.VMEM((N_CHAIN, HEAD, 2 * HEAD), F32)],
        compiler_params=_cp("parallel", "arbitrary"),
    )(ps, ps, kk, wf, kdf, bf, ps, ps, kk, wb, kdb, bb, _pair_ones())


def _scan_bwd(ps, kk, dirs, dy, hist_f, hist_b, batch, seq):
    t = batch * seq
    n = seq // SCAN_CHUNK
    groups = SCAN_CHUNK // GROUP
    fwd_chunk = lambda g: n - 1 - g
    bwd_chunk = lambda g: g
    col_blocks = (0, 2, 0, 0, 0, 0, 0)

    def undo_group(dir_refs, out_refs, hist_refs, gi, d_states, ones_pair, dmask, sub8):
        d_states = list(d_states)
        loaded, blocks = [], []
        for d in range(2):
            blk = groups - 1 - gi if d == 0 else gi
            blocks.append(pl.ds(pl.multiple_of(blk * GROUP, GROUP), GROUP))
            loaded.append(tuple(ref[blocks[d], :] for ref in dir_refs[d]))
        acc = [[jnp.zeros((GROUP, 2 * HEAD), F32) for _ in range(6)] for _ in range(N_CHAIN)]
        for step in range(GROUP):
            rows, idx, before, after = [], [], [], []
            for c in range(N_CHAIN):
                d, lanes = _chain(c)
                i = GROUP - 1 - step if d == 0 else step
                q = (groups - 1 - gi) * GROUP + i if d == 0 else SCAN_CHUNK - 1 - (gi * GROUP + i)
                idx.append(i)
                rows.append(tuple(x8[i:i + 1, lanes] for x8 in loaded[d]))
                before.append(hist_refs[d][c % N_PAIR, q])
                after.append(hist_refs[d][c % N_PAIR, q + 1])
            _, cols = _seg_many([], [dmask * rows[c][j] for c in range(N_CHAIN) for j in (1, 6)], ones_pair)
            v_cols, dy_cols = cols[0::2], cols[1::2]
            d_now = [d_states[c] + dy_cols[c] * rows[c][0] for c in range(N_CHAIN)]
            d_sas, _ = _seg_many([d_now[c] * rows[c][5] for c in range(N_CHAIN)], [], ones_pair)
            _, others = _seg_many(
                [], [x for c in range(N_CHAIN) for x in (before[c] * rows[c][2], d_now[c] * rows[c][4])], ones_pair)
            for c in range(N_CHAIN):
                sa, d_sa, dv_cols = others[2 * c], -d_sas[c], others[2 * c + 1]
                rows_out = (
                    jnp.sum(after[c] * dy_cols[c], axis=0, keepdims=True),
                    jnp.sum(d_now[c] * before[c], axis=0, keepdims=True),
                    jnp.sum(d_now[c] * v_cols[c], axis=0, keepdims=True),
                    _to_row(dv_cols, dmask),
                    jnp.sum(before[c] * d_sa, axis=0, keepdims=True),
                    -jnp.sum(d_now[c] * sa, axis=0, keepdims=True),
                )
                acc[c] = [jnp.where(sub8 == idx[c], val, a) for val, a in zip(rows_out, acc[c])]
                d_states[c] = d_now[c] * rows[c][3] + d_sa * rows[c][2]
        for c in range(N_CHAIN):
            d, lanes = _chain(c)
            for o_ref, val in zip(out_refs[d], acc[c]):
                o_ref[blocks[d], lanes] = val
        return tuple(d_states)

    def body(*refs):
        dir_refs = (refs[0:7], refs[7:14])
        hist_refs, ones_ref = refs[14:16], refs[16]
        out_refs = (refs[17:23], refs[23:29])
        dst_ref = refs[29]

        @pl.when(pl.program_id(1) == 0)
        def _():
            dst_ref[...] = jnp.zeros_like(dst_ref)

        ones_pair = ones_ref[...]
        dmask = _diag_mask()
        sub8 = lax.broadcasted_iota(jnp.int32, (GROUP, 2 * HEAD), 0)

        def group(gi, carry):
            return undo_group(dir_refs, out_refs, hist_refs, gi, carry, ones_pair, dmask, sub8)

        final = lax.fori_loop(0, groups, group, tuple(dst_ref[c] for c in range(N_CHAIN)))
        for c in range(N_CHAIN):
            dst_ref[c] = final[c]

    blk = (SCAN_CHUNK, D_RWKV)
    out_f = pl.BlockSpec(blk, lambda b, g: (b * n + fwd_chunk(g), 0))
    out_b = pl.BlockSpec(blk, lambda b, g: (b * n + bwd_chunk(g), 0))
    hist_block = (None, None, N_PAIR, SCAN_CHUNK + 1, HEAD, 2 * HEAD)
    hist_spec_f = pl.BlockSpec(hist_block, lambda b, g: (b, fwd_chunk(g), 0, 0, 0, 0))
    hist_spec_b = pl.BlockSpec(hist_block, lambda b, g: (b, bwd_chunk(g), 0, 0, 0, 0))
    ones_spec = pl.BlockSpec((4 * HEAD, 2 * HEAD), lambda b, g: (0, 0))
    (wf, kdf, bf), (wb, kdb, bb) = dirs
    outs = pl.pallas_call(
        body, name="wkv_bwd", grid=(batch, n),
        in_specs=_scan_specs(n, col_blocks, fwd_chunk, bwd_chunk) + [hist_spec_f, hist_spec_b, ones_spec],
        out_specs=[out_f] * 6 + [out_b] * 6,
        out_shape=[_sds((t, D_RWKV))] * 12,
        scratch_shapes=[pltpu.VMEM((N_CHAIN, HEAD, 2 * HEAD), F32)],
        compiler_params=_cp("parallel", "arbitrary"),
    )(ps, ps, kk, wf, kdf, bf, dy, ps, ps, kk, wb, kdb, bb, dy, hist_f, hist_b, _pair_ones())
    return outs[0:6], outs[6:12]


def _post_math(y, r, kd_f, kd_b, v, gate, gn_w, gn_b, rk_f, rk_b, ones_blocks):
    mean = _seg(y, ones_blocks) * (1.0 / HEAD)
    yc = y - mean
    var = _seg(yc * yc, ones_blocks) * (1.0 / HEAD)
    yn = yc * lax.rsqrt(var + GN_EPS) * gn_w + gn_b
    bonus = _seg(r * kd_f * rk_f, ones_blocks) * v + _seg(r * kd_b * rk_b, ones_blocks) * v
    return (yn + bonus) * gate


def _conv_parts(pc, halo_prev, halo_next, has_prev, has_next):
    gate_b, gate_c, hid = pc[:, 0:512], pc[:, 512:1024], pc[:, 1024:1536]
    u = gate_c * hid
    u_prev_row = halo_prev[GROUP - 1:GROUP, 512:1024] * halo_prev[GROUP - 1:GROUP, 1024:1536] * has_prev
    u_next_row = halo_next[0:1, 512:1024] * halo_next[0:1, 1024:1536] * has_next
    u_down, u_up = _shifted(u, u_prev_row, u_next_row)
    return gate_b, gate_c, hid, u, u_down, u_up


def _post_specs(tm, t):
    pc_prev, pc_next = _halo_specs(tm, D_CONV3, t)
    col = lambda c: pl.BlockSpec((tm, D_RWKV), lambda i: (i, c))
    return ([col(0), col(0), col(0), col(0), col(0), col(2), col(0), _row(tm, D_CONV3), pc_prev, pc_next,
             _fixed((8, D_RWKV)), _fixed((D_RWKV, D_RWKV))])


def _post_fwd(y_f, y_b, ps, kd_f, kd_b, gate, pc, qvec, ones_blocks, tm, seq):
    t = ps.shape[0]

    def body(yf_ref, yb_ref, r_ref, kdf_ref, kdb_ref, v_ref, g_ref, pc_ref, hp_ref, hn_ref, qv_ref, ones_ref,
             o_ref, ot_ref):
        has_prev, has_next = _edge_flags(tm, seq)
        vec = [jnp.broadcast_to(qv_ref[j:j + 1, :], (tm, D_RWKV)) for j in range(7)]
        o_rwkv = _post_math(yf_ref[...] + yb_ref[...], r_ref[...], kdf_ref[...], kdb_ref[...], v_ref[...],
                            g_ref[...], vec[0], vec[1], vec[2], vec[3], ones_ref[...])
        gate_b, _, _, u, u_down, u_up = _conv_parts(pc_ref[...], hp_ref[...], hn_ref[...], has_prev, has_next)
        o_conv = gate_b * (vec[4] * u_down + vec[5] * u + vec[6] * u_up)
        for half, val in enumerate((o_rwkv, o_conv)):
            o_ref[:, D_RWKV * half:D_RWKV * (half + 1)] = val.astype(BF16)
            ot_ref[D_RWKV * half:D_RWKV * (half + 1), :] = jnp.transpose(val).astype(BF16)

    return pl.pallas_call(
        body, name="post_fwd", grid=(t // tm,), in_specs=_post_specs(tm, t),
        out_specs=[_row(tm, D_MODEL), _col(tm, D_MODEL)],
        out_shape=[_sds((t, D_MODEL), BF16), _sds((D_MODEL, t), BF16)], compiler_params=_cp("parallel"),
    )(y_f, y_b, ps, kd_f, kd_b, ps, gate, pc, pc, pc, qvec, ones_blocks)


def _post_bwd(d_out, y_f, y_b, ps, kd_f, kd_b, gate, pc, qvec, ones_blocks, tm, seq):
    t = ps.shape[0]
    do_prev, do_next = _halo_specs(tm, D_MODEL, t)

    def body(do_ref, dop_ref, don_ref, yf_ref, yb_ref, r_ref, kdf_ref, kdb_ref, v_ref, g_ref, pc_ref, hp_ref,
             hn_ref, qv_ref, ones_ref, dy_ref, dr_ref, dkdf_ref, dkdb_ref, dv_ref, dg_ref, dpc_ref, dqv_ref):
        has_prev, has_next = _edge_flags(tm, seq)
        vec = [jnp.broadcast_to(qv_ref[j:j + 1, :], (tm, D_RWKV)) for j in range(7)]
        ones_v = ones_ref[...]
        args = (yf_ref[...] + yb_ref[...], r_ref[...], kdf_ref[...], kdb_ref[...], v_ref[...], g_ref[...],
                vec[0], vec[1], vec[2], vec[3])
        _, vjp = jax.vjp(lambda *a: _post_math(*a, ones_v), *args)
        grads = vjp(do_ref[:, 0:D_RWKV])
        for o_ref, gval in zip((dy_ref, dr_ref, dkdf_ref, dkdb_ref, dv_ref, dg_ref), grads[0:6]):
            o_ref[...] = gval

        hp, hn = hp_ref[...], hn_ref[...]
        gate_b, gate_c, hid, u, u_down, u_up = _conv_parts(pc_ref[...], hp, hn, has_prev, has_next)
        d_oc = do_ref[:, D_RWKV:2 * D_RWKV]
        d_cu = d_oc * gate_b
        d_cu_prev = dop_ref[GROUP - 1:GROUP, D_RWKV:2 * D_RWKV] * hp[GROUP - 1:GROUP, 0:512] * has_prev
        d_cu_next = don_ref[0:1, D_RWKV:2 * D_RWKV] * hn[0:1, 0:512] * has_next
        d_cu_down, d_cu_up = _shifted(d_cu, d_cu_prev, d_cu_next)
        d_u = vec[5] * d_cu + vec[4] * d_cu_up + vec[6] * d_cu_down
        dpc_ref[:, 0:512] = (d_oc * (vec[4] * u_down + vec[5] * u + vec[6] * u_up)).astype(BF16)
        dpc_ref[:, 512:1024] = (d_u * hid).astype(BF16)
        dpc_ref[:, 1024:1536] = (d_u * gate_c).astype(BF16)

        @pl.when(pl.program_id(0) == 0)
        def _():
            dqv_ref[...] = jnp.zeros_like(dqv_ref)

        vec_grads = list(grads[6:10]) + [d_cu * u_down, d_cu * u, d_cu * u_up]
        for j, gval in enumerate(vec_grads):
            dqv_ref[j:j + 1, :] += jnp.sum(gval, axis=0, keepdims=True)

    return pl.pallas_call(
        body, name="post_bwd", grid=(t // tm,),
        in_specs=[_row(tm, D_MODEL), do_prev, do_next] + _post_specs(tm, t),
        out_specs=[_row(tm, D_RWKV)] * 6 + [_row(tm, D_CONV3), _fixed((8, D_RWKV))],
        out_shape=[_sds((t, D_RWKV))] * 6 + [_sds((t, D_CONV3), BF16), _sds((8, D_RWKV))],
        compiler_params=_cp("arbitrary"),
    )(d_out, d_out, d_out, y_f, y_b, ps, kd_f, kd_b, ps, gate, pc, pc, pc, qvec, ones_blocks)


def _adamw_math(wv, gv, mv, vv):
    m2 = ADAM_B1 * mv + (1.0 - ADAM_B1) * gv
    v2 = ADAM_B2 * vv + (1.0 - ADAM_B2) * (gv * gv)
    m_hat = m2 / (1.0 - ADAM_B1 ** ADAM_STEP)
    v_hat = v2 / (1.0 - ADAM_B2 ** ADAM_STEP)
    return -ADAM_LR * (m_hat / (jnp.sqrt(v_hat) + ADAM_EPS) + ADAM_WD * wv), m2, v2


def _adamw_small(items):
    n = len(items)

    def body(*refs):
        ins, outs = refs[:4 * n], refs[4 * n:]
        for k in range(n):
            w_ref, g_ref, m_ref, v_ref = ins[4 * k:4 * k + 4]
            for o_ref, val in zip(outs[3 * k:3 * k + 3], _adamw_math(w_ref[...], g_ref[...], m_ref[...], v_ref[...])):
                o_ref[...] = val

    flat = [a for item in items for a in item]
    outs = pl.pallas_call(
        body, name="adamw_small", out_shape=[_sds(item[0].shape) for item in items for _ in range(3)],
        compiler_params=_cp())(*flat)
    return [tuple(outs[3 * k:3 * k + 3]) for k in range(n)]


def _adamw(w, g, m, v, name):
    r, c = w.shape[-2:]
    tr = _tile(r, 256, 8)
    if w.ndim == 3:
        spec = pl.BlockSpec((None, tr, c), lambda i: (0, i, 0))
    else:
        spec = pl.BlockSpec((tr, c), lambda i: (i, 0))

    def body(w_ref, g_ref, m_ref, v_ref, d_ref, nm_ref, nv_ref):
        d_ref[...], nm_ref[...], nv_ref[...] = _adamw_math(w_ref[...], g_ref[...], m_ref[...], v_ref[...])

    return pl.pallas_call(
        body, name=name, grid=(r // tr,), in_specs=[spec] * 4, out_specs=[spec] * 3,
        out_shape=[_sds(w.shape)] * 3, compiler_params=_cp("parallel"))(w, g, m, v)


_ANY = pl.BlockSpec(memory_space=pl.ANY)


def _place():
    return lax.axis_index("x"), lax.axis_index("y"), lax.axis_index("c")


def _other_chips(x, y):
    return [(1 - x, y), (x, 1 - y), (1 - x, 1 - y)]


def _remote(src, dst, send_sems, recv_sems, k, to):
    return pltpu.make_async_remote_copy(src_ref=src, dst_ref=dst, send_sem=send_sems.at[k],
                                        recv_sem=recv_sems.at[k], device_id=to, device_id_type=MESH)


def _gather_weights(pack):
    rows, width = pack.shape
    half = rows // 2

    def body(x_ref, out_ref, send_sems, recv_sems):
        x, y, c = _place()
        sibling = (x, y, 1 - c)
        chips = _other_chips(x, y)

        def block(chip, part):
            return out_ref.at[2 * chip[0] + chip[1], pl.ds(part * half, half), :]

        first = [_remote(x_ref.at[pl.ds(c * half, half), :], block((x, y), c), send_sems, recv_sems, j, (*chip, c))
                 for j, chip in enumerate(chips)]
        for cp in first:
            cp.start()
        passed = [_remote(block(chip, c), block(chip, c), send_sems, recv_sems, 3 + j, sibling)
                  for j, chip in enumerate(chips)]
        for j, chip in enumerate(chips):
            _remote(block(chip, c), block(chip, c), send_sems, recv_sems, j, sibling).wait_recv()
            passed[j].start()
        for j, chip in enumerate(chips):
            _remote(block(chip, 1 - c), block(chip, 1 - c), send_sems, recv_sems, 3 + j, sibling).wait_recv()
        for cp in first + passed:
            cp.wait_send()

    return pl.pallas_call(
        body, name="gather_weights", in_specs=[_ANY], out_specs=_ANY,
        out_shape=_sds((N_SHARD, rows, width), pack.dtype),
        scratch_shapes=[pltpu.SemaphoreType.DMA((6,)), pltpu.SemaphoreType.DMA((6,))],
    )(pack)


def _swap_with_sibling(block, name):
    def body(x_ref, out_ref, send_sems, recv_sems):
        x, y, c = _place()
        cp = _remote(x_ref, out_ref, send_sems, recv_sems, 0, (x, y, 1 - c))
        cp.start()
        cp.wait()

    return pl.pallas_call(
        body, name=name, in_specs=[_ANY], out_specs=_ANY, out_shape=_sds(block.shape, block.dtype),
        scratch_shapes=[pltpu.SemaphoreType.DMA((1,)), pltpu.SemaphoreType.DMA((1,))],
    )(block)


def _swap_other_half(packed):
    slots, rows, width = packed.shape
    half = rows // 2

    def body(x_ref, out_ref, send_sems, recv_sems):
        x, y, c = _place()
        cp = _remote(x_ref.at[:, pl.ds((1 - c) * half, half), :], out_ref, send_sems, recv_sems, 0, (x, y, 1 - c))
        cp.start()
        cp.wait()

    return pl.pallas_call(
        body, name="swap_halves", in_specs=[_ANY], out_specs=_ANY, out_shape=_sds((slots, half, width)),
        scratch_shapes=[pltpu.SemaphoreType.DMA((1,)), pltpu.SemaphoreType.DMA((1,))],
    )(packed)


def _add_halves(packed, got, c):
    slots, rows, width = packed.shape
    half = rows // 2
    tr = _tile(half, 408, 16)
    per = half // tr
    block = (None, tr, width)

    def body(c_ref, mine_ref, got_ref, sum_ref, sum16_ref):
        acc = mine_ref[...] + got_ref[...]
        sum_ref[...] = acc
        sum16_ref[...] = acc.astype(BF16)

    plain = pl.BlockSpec(block, lambda s, i, c_ref: (s, i, 0))
    grid_spec = pltpu.PrefetchScalarGridSpec(
        num_scalar_prefetch=1, grid=(slots, per),
        in_specs=[pl.BlockSpec(block, lambda s, i, c_ref: (s, c_ref[0] * per + i, 0)), plain],
        out_specs=[plain, plain])
    return pl.pallas_call(
        body, name="add_halves", grid_spec=grid_spec,
        out_shape=[_sds((slots, half, width)), _sds((slots, half, width), BF16)],
        compiler_params=_cp("parallel", "parallel"))(c.reshape(1).astype(jnp.int32), packed, got)


def _add_quarters(chip_sum, others, chip):
    _, rows, width = chip_sum.shape
    tr = _tile(rows, 408, 16)

    def body(chip_ref, own_ref, others_ref, o_ref):
        acc = own_ref[...]
        for j in range(3):
            acc = acc + others_ref[j].astype(F32)
        o_ref[...] = acc

    grid_spec = pltpu.PrefetchScalarGridSpec(
        num_scalar_prefetch=1, grid=(rows // tr,),
        in_specs=[pl.BlockSpec((None, tr, width), lambda i, chip_ref: (chip_ref[0], i, 0)),
                  pl.BlockSpec((3, tr, width), lambda i, chip_ref: (0, i, 0))],
        out_specs=pl.BlockSpec((tr, width), lambda i, chip_ref: (i, 0)))
    return pl.pallas_call(
        body, name="add_quarters", grid_spec=grid_spec, out_shape=_sds((rows, width)),
        compiler_params=_cp("parallel"))(chip.reshape(1).astype(jnp.int32), chip_sum, others)


def _exchange_quarters(parts):
    _, rows, width = parts.shape

    def body(x_ref, out_ref, send_sems, recv_sems):
        x, y, c = _place()
        copies = [_remote(x_ref.at[2 * chip[0] + chip[1]], out_ref.at[j], send_sems, recv_sems, j, (*chip, c))
                  for j, chip in enumerate(_other_chips(x, y))]
        for cp in copies:
            cp.start()
        for cp in copies:
            cp.wait()

    return pl.pallas_call(
        body, name="exchange_quarters", in_specs=[_ANY], out_specs=_ANY,
        out_shape=_sds((3, rows, width), parts.dtype),
        scratch_shapes=[pltpu.SemaphoreType.DMA((3,)), pltpu.SemaphoreType.DMA((3,))],
    )(parts)


def _allreduce_small(vec):
    rows, width = vec.shape
    vmem = pl.BlockSpec(memory_space=pltpu.VMEM)

    def body(x_ref, o_ref, buf_ref, send_sems, recv_sems):
        x, y, c = _place()
        me = 4 * x + 2 * y + c
        buf_ref[me] = x_ref[...]
        copies = []
        for k in range(1, N_DEV):
            peer = (x ^ ((k >> 2) & 1), y ^ ((k >> 1) & 1), c ^ (k & 1))
            copies.append(_remote(x_ref, buf_ref.at[me], send_sems, recv_sems, k - 1, peer))
        for cp in copies:
            cp.start()
        for k in range(1, N_DEV):
            _remote(x_ref, buf_ref.at[me ^ k], send_sems, recv_sems, k - 1, (x, y, c)).wait_recv()
        for cp in copies:
            cp.wait_send()
        total = buf_ref[0]
        for d in range(1, N_DEV):
            total = total + buf_ref[d]
        o_ref[...] = total

    return pl.pallas_call(
        body, name="allreduce_small", in_specs=[vmem], out_specs=vmem, out_shape=_sds((rows, width)),
        scratch_shapes=[pltpu.VMEM((N_DEV, rows, width), F32), pltpu.SemaphoreType.DMA((N_DEV - 1,)),
                        pltpu.SemaphoreType.DMA((N_DEV - 1,))],
    )(vec)


def _rows1024(a):
    return a.reshape(-1, 1024)


def _pad_rows(a, rows):
    return jnp.concatenate([a, jnp.zeros((rows - a.shape[0], a.shape[1]), a.dtype)], axis=0)


_TRANSPOSED = ("w_in", "w_gate", "w_up")
_SMALL_SHARDED = ("w_up_f", "w_up_b", "a_up_f", "a_up_b", "g_up")
_BIG_SHARDED = ("w_in", "w_out", "w_gate", "w_up", "w_down")


def _pack_weight_shards(w):
    conv_bits = lax.bitcast_convert_type(w["conv_w"], BF16).reshape(1, -1)
    conv_row = jnp.concatenate([conv_bits, jnp.zeros((1, 1024 - conv_bits.shape[1]), BF16)], axis=1)
    parts = []
    for name, _ in _PACK_ROWS[:-1]:
        a = w[name].astype(BF16)
        parts.append(a.T if name in _TRANSPOSED else _rows1024(a))
    return _pad_rows(jnp.concatenate(parts + [conv_row], axis=0), PACK_R)


def _unpack_gathered(gathered):
    out, row = {}, 0
    for name, n in _PACK_ROWS:
        out[name] = gathered[:, row:row + n]
        row += n
    cols = lambda a, k: jnp.concatenate([a[s].reshape(k, -1) for s in range(N_SHARD)], axis=1)
    conv = lax.bitcast_convert_type(out["conv_w"][:, 0, :768].reshape(N_SHARD, 3, 128, 2), F32)
    full = {name: out[name].reshape(-1, 1024).T for name in _TRANSPOSED}
    full.update(w_out=out["w_out"].reshape(D_MODEL, D_MODEL), w_down=out["w_down"].reshape(D_FF, D_MODEL),
                conv_w=jnp.concatenate([conv[s] for s in range(N_SHARD)], axis=1))
    full.update({name: cols(out[name], D_GATE if name == "g_up" else D_LORA) for name in _SMALL_SHARDED})
    return full


def _pack_grads(g):
    col_split = lambda a, s: a[:, s * (a.shape[1] // N_SHARD):(s + 1) * (a.shape[1] // N_SHARD)]
    row_split = lambda a, s: a[s * (a.shape[0] // N_SHARD):(s + 1) * (a.shape[0] // N_SHARD)]
    by_rows = {name: (g[name].T if name in _TRANSPOSED else g[name]) for name in _BIG_SHARDED}
    used = sum(n for _, n in _PACK_ROWS)
    parts = []
    for s in range(N_SHARD):
        conv = col_split(g["conv_w"], s).reshape(1, -1)
        parts += [row_split(by_rows[name], s) for name in _BIG_SHARDED]
        parts += [_rows1024(col_split(g[name], s)) for name in _SMALL_SHARDED]
        parts += [jnp.concatenate([conv, jnp.zeros((1, 1024 - conv.shape[1]), F32)], axis=1),
                  jnp.zeros((PACK_R - used, 1024), F32)]
    return jnp.concatenate(parts, axis=0).reshape(N_SHARD, PACK_R, 1024)


def _unpack_grad_shard(pack):
    small_shapes = {name: (D_GATE if name == "g_up" else D_LORA, 128) for name in _SMALL_SHARDED}
    out, row = {}, 0
    for name, n in _PACK_ROWS[:-1]:
        piece = pack[row:row + n]
        out[name] = piece.T if name in _TRANSPOSED else piece.reshape(small_shapes.get(name, piece.shape))
        row += n
    out["conv_w"] = pack[row, :384].reshape(3, 128)
    return out


_SMALL_LAYOUT = (("norm1_w", 1024), ("mu_shift", D_SHIFT), ("w0_f", 512), ("w0_b", 512), ("a0_f", 512),
                 ("a0_b", 512), ("k_k", 512), ("k_a_f", 512), ("k_a_b", 512), ("r_k_f", 512), ("r_k_b", 512),
                 ("gn_w", 512), ("gn_b", 512), ("norm2_w", 1024), ("norm_f_w", 1024), ("loss", 1))


def _pack_small(vals):
    rows = []
    for name, n in _SMALL_LAYOUT:
        flat = vals[name].reshape(-1)
        n_rows = -(-n // 1024)
        rows.append(jnp.concatenate([flat, jnp.zeros((n_rows * 1024 - n,), F32)]).reshape(n_rows, 1024))
    return _pad_rows(jnp.concatenate(rows, axis=0), SMALL_ROWS)


def _unpack_small(pack):
    out, row = {}, 0
    for name, n in _SMALL_LAYOUT:
        n_rows = -(-n // 1024)
        out[name] = pack[row:row + n_rows].reshape(-1)[:n]
        row += n_rows
    return out


_WEIGHTS = ("norm1_w", "w_in", "mu_shift", "w_up_f", "w0_f", "w_up_b", "w0_b", "a_up_f", "a0_f", "a_up_b", "a0_b",
            "g_up", "k_k", "k_a_f", "k_a_b", "r_k_f", "r_k_b", "gn_w", "gn_b", "conv_w", "w_out", "norm2_w",
            "w_gate", "w_up", "w_down", "norm_f_w")


def _train_step(x, loss_target, w, m, v):
    batch, seq, _ = x.shape
    t = batch * seq
    tm = _tile(seq, 256, 8)
    xs = x.reshape(t, D_MODEL)
    target = loss_target.reshape(t, D_MODEL)
    vec = lambda name: w[name].reshape(1, -1)

    local = {name: w[name][0] for name, _ in _PACK_ROWS}
    c = lax.axis_index("c")
    chip = 2 * lax.axis_index("x") + lax.axis_index("y")
    pack = _pack_weight_shards(local)
    full = _unpack_gathered(lax.dynamic_update_slice(_gather_weights(pack), pack[None], (chip, 0, 0)))
    w_in = full["w_in"]
    w_shift = jnp.concatenate([w_in[:, :D_SHIFT], jnp.zeros((D_MODEL, D_SHIFT_PAD - D_SHIFT), BF16)], axis=1)
    w_conv = w_in[:, D_SHIFT:]
    zeros_lora = jnp.zeros((D_LORA, D_RWKV), F32)
    lora = lambda name: full[name].astype(F32)
    mats = (jnp.concatenate([lora("w_up_f"), zeros_lora]), jnp.concatenate([zeros_lora, lora("a_up_f")]),
            jnp.concatenate([lora("w_up_b"), zeros_lora]), jnp.concatenate([zeros_lora, lora("a_up_b")]),
            jnp.concatenate([lora("g_up"), jnp.zeros((D_GATE_PAD - D_GATE, D_RWKV), F32)]))
    mu = jnp.concatenate([vec("mu_shift"), jnp.zeros((1, D_SHIFT_PAD - D_SHIFT), F32)], axis=1)
    mu = jnp.broadcast_to(mu, (GROUP, D_SHIFT_PAD))
    zero_row = jnp.zeros((1, D_RWKV), F32)
    pvec = jnp.concatenate([vec("k_k"), vec("w0_f"), vec("a0_f"), vec("k_a_f"), vec("w0_b"), vec("a0_b"),
                            vec("k_a_b"), zero_row], axis=0)
    qvec = jnp.concatenate([vec("gn_w"), vec("gn_b"), vec("r_k_f"), vec("r_k_b"), full["conv_w"], zero_row], axis=0)
    ones_blocks = _head_ones()

    h1, h1_t = _rmsnorm_fwd(xs, vec("norm1_w"), tm, "norm1_fwd")
    p_shift = _matmul(h1, w_shift, mode="nn", name="in_proj_shift")
    pc = _matmul(h1, w_conv, mode="nn", name="in_proj_conv")
    ps = _shift_fwd(p_shift, mu, tm, seq)
    kk, w_f, kd_f, b_f, w_b, kd_b, b_b, gate = _prep_fwd(ps, pvec, mats, ones_blocks, tm)
    dirs = ((w_f, kd_f, b_f), (w_b, kd_b, b_b))
    y_f, y_b, hist_f, hist_b = _scan_fwd(ps, kk, dirs, batch, seq)
    mixed, mixed_t = _post_fwd(y_f, y_b, ps, kd_f, kd_b, gate, pc, qvec, ones_blocks, tm, seq)
    x1 = _matmul(mixed, full["w_out"], mode="nn", name="out_proj", add=xs)
    h2, h2_t = _rmsnorm_fwd(x1, vec("norm2_w"), tm, "norm2_fwd")
    ff_gate, ff_up, act, act_t = _ffn_in(h2, full["w_gate"], full["w_up"])
    x2 = _matmul(act, full["w_down"], mode="nn", name="ffn_down", add=x1)
    d_x2, d_norm_f, loss_part = _loss_head(x2, w["norm_f_w"].reshape(1, -1), target, tm)

    g = {}
    g["w_down"] = _matmul(act_t, d_x2, mode="nn", name="ffn_down_dw")
    d_gate, d_up = _ffn_in_bwd(d_x2, full["w_down"], ff_gate, ff_up)
    d_h2 = _matmul(d_gate, full["w_gate"], mode="nt", name="ffn_gate_dx")
    d_h2 = _matmul(d_up, full["w_up"], mode="nt", name="ffn_up_dx", add=d_h2)
    g["w_gate"] = _matmul(h2_t, d_gate, mode="nn", name="ffn_gate_dw")
    g["w_up"] = _matmul(h2_t, d_up, mode="nn", name="ffn_up_dw")
    d_x1, d_norm2 = _rmsnorm_bwd(x1, vec("norm2_w"), d_h2, d_x2, tm, "norm2_bwd")
    d_mixed = _matmul(d_x1, full["w_out"], mode="nt", name="out_proj_dx")
    g["w_out"] = _matmul(mixed_t, d_x1, mode="nn", name="out_proj_dw")
    dy, dr_o, dkdf_o, dkdb_o, dv_o, d_gatev, d_pc, d_qvec = _post_bwd(
        d_mixed, y_f, y_b, ps, kd_f, kd_b, gate, pc, qvec, ones_blocks, tm, seq)
    (dr_f, dw_f, dkd_f, dv_f, dkk_f, db_f), (dr_b, dw_b, dkd_b, dv_b, dkk_b, db_b) = _scan_bwd(
        ps, kk, dirs, dy, hist_f, hist_b, batch, seq)
    cts = [[dr_f, dr_b, dr_o], [dv_f, dv_b, dv_o], [dkk_f, dkk_b], [dw_f], [dkd_f, dkdf_o], [db_f],
           [dw_b], [dkd_b, dkdb_o], [db_b], [d_gatev]]
    q, d_pvec, d_m0, d_m1, d_m2, d_m3, d_m4 = _prep_bwd(ps, pvec, mats, ones_blocks, cts, tm)
    d_pshift, d_mu = _shift_bwd(q, p_shift, mu, tm, seq)
    d_h1 = _matmul(d_pshift, w_shift, mode="nt", name="in_proj_shift_dx")
    d_h1 = _matmul(d_pc, w_conv, mode="nt", name="in_proj_conv_dx", add=d_h1)
    d_w_shift = _matmul(h1_t, d_pshift, mode="nn", name="in_proj_shift_dw")
    d_w_conv = _matmul(h1_t, d_pc, mode="nn", name="in_proj_conv_dw")
    g["w_in"] = jnp.concatenate([d_w_shift[:, :D_SHIFT], d_w_conv], axis=1)
    d_x, d_norm1 = _rmsnorm_bwd(xs, vec("norm1_w"), d_h1, d_x1, tm, "norm1_bwd")
    g["w_up_f"], g["a_up_f"] = d_m0[:D_LORA], d_m1[D_LORA:]
    g["w_up_b"], g["a_up_b"] = d_m2[:D_LORA], d_m3[D_LORA:]
    g["g_up"] = d_m4[:D_GATE]
    g["conv_w"] = d_qvec[4:7]

    packed = _pack_grads(g)
    chip_sum, chip_sum_bf16 = _add_halves(packed, _swap_other_half(packed), c)
    others = _exchange_quarters(chip_sum_bf16)
    eighth = _add_quarters(chip_sum, others, chip)
    other_eighth = _swap_with_sibling(eighth, "swap_eighths")
    grads = _unpack_grad_shard(jnp.concatenate([jnp.where(c == 0, eighth, other_eighth),
                                                jnp.where(c == 0, other_eighth, eighth)], axis=0))

    small = dict(norm1_w=d_norm1, mu_shift=d_mu[:, :D_SHIFT], w0_f=d_pvec[1], w0_b=d_pvec[4], a0_f=d_pvec[2],
                 a0_b=d_pvec[5], k_k=d_pvec[0], k_a_f=d_pvec[3], k_a_b=d_pvec[6], r_k_f=d_qvec[2], r_k_b=d_qvec[3],
                 gn_w=d_qvec[0], gn_b=d_qvec[1], norm2_w=d_norm2, norm_f_w=d_norm_f, loss=loss_part)
    reduced = _unpack_small(_allreduce_small(_pack_small(small)))
    loss = reduced.pop("loss")[0]
    grads.update(reduced)

    outs = {}
    small = [name for name in _WEIGHTS if name not in _BIG_SHARDED]
    as2d = lambda name: (1, w[name].shape[0]) if w[name].ndim == 1 else w[name].shape
    operands = lambda name: tuple(a.reshape(as2d(name)) for a in (w[name], grads[name], m[name], v[name]))
    updates = dict(zip(small, _adamw_small([operands(name) for name in small])))
    for name in _BIG_SHARDED:
        updates[name] = _adamw(*operands(name), "adamw_" + name)
    for name in _WEIGHTS:
        shape = w[name].shape
        outs[name] = (grads[name].reshape(shape),) + tuple(a.reshape(shape) for a in updates[name])
    d_x = d_x.reshape(batch, seq, D_MODEL)
    return (loss, d_x) + tuple(outs[name][k] for k in range(4) for name in _WEIGHTS)


def kernel(x, norm1_w, w_in, mu_shift, w_up_f, w0_f, w_up_b, w0_b, a_up_f, a0_f, a_up_b, a0_b, g_up, k_k, k_a_f, k_a_b, r_k_f, r_k_b, gn_w, gn_b, conv_w, w_out, norm2_w, w_gate, w_up, w_down, norm_f_w, loss_target, m_norm1_w, m_w_in, m_mu_shift, m_w_up_f, m_w0_f, m_w_up_b, m_w0_b, m_a_up_f, m_a0_f, m_a_up_b, m_a0_b, m_g_up, m_k_k, m_k_a_f, m_k_a_b, m_r_k_f, m_r_k_b, m_gn_w, m_gn_b, m_conv_w, m_w_out, m_norm2_w, m_w_gate, m_w_up, m_w_down, m_norm_f_w, v_norm1_w, v_w_in, v_mu_shift, v_w_up_f, v_w0_f, v_w_up_b, v_w0_b, v_a_up_f, v_a0_f, v_a_up_b, v_a0_b, v_g_up, v_k_k, v_k_a_f, v_k_a_b, v_r_k_f, v_r_k_b, v_gn_w, v_gn_b, v_conv_w, v_w_out, v_norm2_w, v_w_gate, v_w_up, v_w_down, v_norm_f_w):
    args = locals()
    w = {name: args[name] for name in _WEIGHTS}
    m = {name: args["m_" + name] for name in _WEIGHTS}
    v = {name: args["v_" + name] for name in _WEIGHTS}
    return _train_step(x, loss_target, w, m, v)
```

```python
import functools

import jax
import jax.numpy as jnp
from jax import lax
from jax.experimental import pallas as pl
from jax.experimental.pallas import tpu as pltpu

F32 = jnp.float32
BF16 = jnp.bfloat16
MESH = pl.DeviceIdType.MESH

D_MODEL = 1024
D_RWKV = 512
HEAD = 64
N_PAIR = D_RWKV // (2 * HEAD)
D_LORA = 64
D_GATE = 160
D_GATE_PAD = 384
D_FF = 2816
D_SHIFT = 1824
D_SHIFT_PAD = 2048
D_CONV3 = 1536
LOG_DECAY_SCALE = 0.606531
RMS_EPS = 1e-6
GN_EPS = 64e-5
NORM_EPS = 1e-12
ADAM_LR, ADAM_B1, ADAM_B2, ADAM_EPS, ADAM_WD, ADAM_STEP = 0.001, 0.9, 0.999, 1e-08, 0.01, 10

N_SHARD = 4
N_DEV = 8
V7X_VMEM_LIMIT = 48 * 1024 * 1024
SCAN_CHUNK = 32
GROUP = 8

_PACK_ROWS = (("w_in", 840), ("w_out", 256), ("w_gate", 704), ("w_up", 704), ("w_down", 704),
              ("w_up_f", 8), ("w_up_b", 8), ("a_up_f", 8), ("a_up_b", 8), ("g_up", 20), ("conv_w", 1))
PACK_R = 3264
PACK_H = PACK_R // 2
SMALL_ROWS = 24


def _tile(n, cap, mult=128):
    best = None
    t = mult
    while t <= min(n, cap):
        if n % t == 0:
            best = t
        t += mult
    return best or n


def _cp(*sem):
    return pltpu.CompilerParams(dimension_semantics=sem or None, vmem_limit_bytes=V7X_VMEM_LIMIT)


def _sds(shape, dtype=F32):
    return jax.ShapeDtypeStruct(shape, dtype)


def _matmul(a, b, *, mode, name, out_dtype=F32, add=None):
    m, kdim = a.shape
    n = b.shape[1] if mode == "nn" else b.shape[0]
    tm, tn = _tile(m, 512, 8), _tile(n, 1536)
    tk = kdim if kdim <= 3584 else _tile(kdim, 1024)
    nk = kdim // tk
    a_spec = pl.BlockSpec((tm, tk), lambda i, j, k: (i, k))
    if mode == "nn":
        b_spec = pl.BlockSpec((tk, tn), lambda i, j, k: (k, j))
        dims = (((1,), (0,)), ((), ()))
    else:
        b_spec = pl.BlockSpec((tn, tk), lambda i, j, k: (j, k))
        dims = (((1,), (1,)), ((), ()))
    has_add = add is not None

    def body(*refs):
        a_ref, b_ref = refs[0], refs[1]
        add_ref = refs[2] if has_add else None
        o_ref = refs[3] if has_add else refs[2]
        part = lax.dot_general(a_ref[...].astype(BF16), b_ref[...].astype(BF16), dims,
                               preferred_element_type=F32)
        if nk == 1:
            if has_add:
                part = part + add_ref[...]
            o_ref[...] = part.astype(out_dtype)
        else:
            acc_ref = refs[-1]
            k = pl.program_id(2)

            @pl.when(k == 0)
            def _():
                acc_ref[...] = jnp.zeros_like(acc_ref)

            acc_ref[...] += part

            @pl.when(k == nk - 1)
            def _():
                res = acc_ref[...]
                if has_add:
                    res = res + add_ref[...]
                o_ref[...] = res.astype(out_dtype)

    o_spec = pl.BlockSpec((tm, tn), lambda i, j, k: (i, j))
    in_specs = [a_spec, b_spec] + ([o_spec] if has_add else [])
    args = (a, b) + ((add,) if has_add else ())
    return pl.pallas_call(
        body, name=name, grid=(m // tm, n // tn, nk), in_specs=in_specs, out_specs=o_spec,
        out_shape=_sds((m, n), out_dtype),
        scratch_shapes=[pltpu.VMEM((tm, tn), F32)] if nk > 1 else [],
        compiler_params=_cp("parallel", "parallel", "arbitrary"),
    )(*args)


def _row(tm, width):
    return pl.BlockSpec((tm, width), lambda i: (i, 0))


def _col(tm, height):
    return pl.BlockSpec((height, tm), lambda i: (0, i))


def _fixed(shape):
    return pl.BlockSpec(shape, lambda i: tuple(0 for _ in shape))


def _rmsnorm_fwd(x, w, tm, name):
    t, d = x.shape

    def body(x_ref, w_ref, o_ref, ot_ref):
        xv = x_ref[...]
        rstd = lax.rsqrt(jnp.mean(xv * xv, axis=-1, keepdims=True) + RMS_EPS)
        yv = xv * rstd * w_ref[...]
        o_ref[...] = yv.astype(BF16)
        ot_ref[...] = jnp.transpose(yv).astype(BF16)

    return pl.pallas_call(
        body, name=name, grid=(t // tm,), in_specs=[_row(tm, d), _fixed((1, d))],
        out_specs=[_row(tm, d), _col(tm, d)], out_shape=[_sds((t, d), BF16), _sds((d, t), BF16)],
        compiler_params=_cp("parallel"))(x, w)


def _rms_bwd_math(xv, wv, dyv):
    rstd = lax.rsqrt(jnp.mean(xv * xv, axis=-1, keepdims=True) + RMS_EPS)
    xhat = xv * rstd
    gv = dyv * wv
    dx = rstd * (gv - xhat * jnp.mean(gv * xhat, axis=-1, keepdims=True))
    return dx, jnp.sum(dyv * xhat, axis=0, keepdims=True)


def _rmsnorm_bwd(x, w, dy, dres, tm, name):
    t, d = x.shape

    def body(x_ref, w_ref, dy_ref, dres_ref, dx_ref, dw_ref):
        dx, dw = _rms_bwd_math(x_ref[...], w_ref[...], dy_ref[...])
        dx_ref[...] = dres_ref[...] + dx

        @pl.when(pl.program_id(0) == 0)
        def _():
            dw_ref[...] = jnp.zeros_like(dw_ref)

        dw_ref[...] += dw

    return pl.pallas_call(
        body, name=name, grid=(t // tm,),
        in_specs=[_row(tm, d), _fixed((1, d)), _row(tm, d), _row(tm, d)],
        out_specs=[_row(tm, d), _fixed((1, d))],
        out_shape=[_sds((t, d)), _sds((1, d))], compiler_params=_cp("arbitrary"))(x, w, dy, dres)


def _loss_head(x, w, target, tm):
    t, d = x.shape

    def body(x_ref, w_ref, t_ref, dx_ref, dw_ref, loss_ref):
        xv, wv = x_ref[...], w_ref[...]
        rstd = lax.rsqrt(jnp.mean(xv * xv, axis=-1, keepdims=True) + RMS_EPS)
        err = xv * rstd * wv - t_ref[...]
        dx, dw = _rms_bwd_math(xv, wv, err * (1.0 / d))
        dx_ref[...] = dx

        @pl.when(pl.program_id(0) == 0)
        def _():
            dw_ref[...] = jnp.zeros_like(dw_ref)
            loss_ref[...] = jnp.zeros_like(loss_ref)

        dw_ref[...] += dw
        loss_ref[...] += 0.5 * jnp.sum(jnp.mean(err * err, axis=-1, keepdims=True), axis=0, keepdims=True)

    return pl.pallas_call(
        body, name="loss_head", grid=(t // tm,),
        in_specs=[_row(tm, d), _fixed((1, d)), _row(tm, d)],
        out_specs=[_row(tm, d), _fixed((1, d)), _fixed((1, 1))],
        out_shape=[_sds((t, d)), _sds((1, d)), _sds((1, 1))], compiler_params=_cp("arbitrary"))(x, w, target)


def _ffn_in(h, w_gate, w_up):
    t, d = h.shape
    f = w_gate.shape[1]
    tm, tn = _tile(t, 512, 8), _tile(f, 1536)

    def body(h_ref, wg_ref, wu_ref, g_ref, u_ref, a_ref, at_ref):
        hv = h_ref[...]
        gv = jnp.dot(hv, wg_ref[...], preferred_element_type=F32)
        uv = jnp.dot(hv, wu_ref[...], preferred_element_type=F32)
        act = gv * jax.nn.sigmoid(gv) * uv
        g_ref[...] = gv.astype(BF16)
        u_ref[...] = uv.astype(BF16)
        a_ref[...] = act.astype(BF16)
        at_ref[...] = jnp.transpose(act).astype(BF16)

    w_spec = pl.BlockSpec((d, tn), lambda i, j: (0, j))
    o_spec = pl.BlockSpec((tm, tn), lambda i, j: (i, j))
    return pl.pallas_call(
        body, name="ffn_in", grid=(t // tm, f // tn),
        in_specs=[pl.BlockSpec((tm, d), lambda i, j: (i, 0)), w_spec, w_spec],
        out_specs=[o_spec, o_spec, o_spec, pl.BlockSpec((tn, tm), lambda i, j: (j, i))],
        out_shape=[_sds((t, f), BF16)] * 3 + [_sds((f, t), BF16)],
        compiler_params=_cp("parallel", "parallel"))(h, w_gate, w_up)


def _ffn_in_bwd(d_out, w_down, gate, up):
    t, d = d_out.shape
    f = w_down.shape[0]
    tm, tn = _tile(t, 512, 8), _tile(f, 1536)

    def body(do_ref, w_ref, g_ref, u_ref, dg_ref, du_ref):
        dv = lax.dot_general(do_ref[...].astype(BF16), w_ref[...], (((1,), (1,)), ((), ())),
                             preferred_element_type=F32)
        gv, uv = g_ref[...].astype(F32), u_ref[...].astype(F32)
        sg = jax.nn.sigmoid(gv)
        du_ref[...] = (dv * gv * sg).astype(BF16)
        dg_ref[...] = (dv * uv * (sg * (1.0 + gv * (1.0 - sg)))).astype(BF16)

    tile = pl.BlockSpec((tm, tn), lambda i, j: (i, j))
    return pl.pallas_call(
        body, name="ffn_in_bwd", grid=(t // tm, f // tn),
        in_specs=[pl.BlockSpec((tm, d), lambda i, j: (i, 0)), pl.BlockSpec((tn, d), lambda i, j: (j, 0)), tile, tile],
        out_specs=[tile, tile], out_shape=[_sds((t, f), BF16)] * 2,
        compiler_params=_cp("parallel", "parallel"))(d_out, w_down, gate, up)


def _halo_specs(tm, width, rows_total):
    per = tm // GROUP
    last = rows_total // GROUP - 1
    prev = pl.BlockSpec((GROUP, width), lambda i: (jnp.maximum(i * per - 1, 0), 0))
    nxt = pl.BlockSpec((GROUP, width), lambda i: (jnp.minimum((i + 1) * per, last), 0))
    return prev, nxt


def _edge_flags(tm, seq):
    i = pl.program_id(0)
    has_prev = jnp.where((i * tm) % seq == 0, 0.0, 1.0).astype(F32)
    has_next = jnp.where(((i + 1) * tm) % seq == 0, 0.0, 1.0).astype(F32)
    return has_prev, has_next


def _shifted(xv, prev_row, next_row):
    tm = xv.shape[0]
    row = lax.broadcasted_iota(jnp.int32, xv.shape, 0)
    down = jnp.where(row == 0, prev_row, pltpu.roll(xv, 1, axis=0))
    up = jnp.where(row == tm - 1, next_row, pltpu.roll(xv, tm - 1, axis=0))
    return down, up


def _shift_fwd(p, mu, tm, seq):
    t, w = p.shape
    prev_spec, next_spec = _halo_specs(tm, w, t)

    def body(p_ref, hp_ref, hn_ref, mu_ref, o_ref):
        has_prev, has_next = _edge_flags(tm, seq)
        xv = p_ref[...]
        down, up = _shifted(xv, hp_ref[GROUP - 1:GROUP, :] * has_prev, hn_ref[0:1, :] * has_next)
        o_ref[...] = xv + mu_ref[0:1, :] * (0.5 * (down + up) - xv)

    return pl.pallas_call(
        body, name="shift_fwd", grid=(t // tm,),
        in_specs=[_row(tm, w), prev_spec, next_spec, _fixed((GROUP, w))], out_specs=_row(tm, w),
        out_shape=_sds((t, w)), compiler_params=_cp("parallel"))(p, p, p, mu)


def _shift_bwd(q, p, mu, tm, seq):
    t, w = p.shape
    prev_spec, next_spec = _halo_specs(tm, w, t)

    def body(q_ref, qp_ref, qn_ref, p_ref, pp_ref, pn_ref, mu_ref, dp_ref, dmu_ref):
        has_prev, has_next = _edge_flags(tm, seq)
        muv = mu_ref[0:1, :]
        qv = q_ref[...]
        mq = muv * qv
        mq_down, mq_up = _shifted(mq, muv * qp_ref[GROUP - 1:GROUP, :] * has_prev,
                                  muv * qn_ref[0:1, :] * has_next)
        dp_ref[...] = (qv - mq + 0.5 * (mq_down + mq_up)).astype(BF16)
        pv = p_ref[...]
        p_down, p_up = _shifted(pv, pp_ref[GROUP - 1:GROUP, :] * has_prev, pn_ref[0:1, :] * has_next)

        @pl.when(pl.program_id(0) == 0)
        def _():
            dmu_ref[...] = jnp.zeros_like(dmu_ref)

        dmu_ref[...] += jnp.sum(qv * (0.5 * (p_down + p_up) - pv), axis=0, keepdims=True)

    return pl.pallas_call(
        body, name="shift_bwd", grid=(t // tm,),
        in_specs=[_row(tm, w), prev_spec, next_spec, _row(tm, w), prev_spec, next_spec, _fixed((GROUP, w))],
        out_specs=[_row(tm, w), _fixed((1, w))],
        out_shape=[_sds((t, w), BF16), _sds((1, w))], compiler_params=_cp("arbitrary"))(q, q, q, p, p, p, mu)


@jax.custom_vjp
def _bdot(a, b):
    return jnp.dot(a.astype(BF16), b.astype(BF16), preferred_element_type=F32)


def _bdot_fwd(a, b):
    return _bdot(a, b), (a, b)


def _bdot_bwd(res, g):
    a, b = res
    gb = g.astype(BF16)
    da = lax.dot_general(gb, b.astype(BF16), (((1,), (1,)), ((), ())), preferred_element_type=F32)
    db = lax.dot_general(a.astype(BF16), gb, (((0,), (0,)), ((), ())), preferred_element_type=F32)
    return da, db


_bdot.defvjp(_bdot_fwd, _bdot_bwd)


def _seg_raw(x, ones_blocks):
    hi = x.astype(BF16)
    lo = (x - hi.astype(F32)).astype(BF16)
    return (jnp.dot(hi, ones_blocks, preferred_element_type=F32)
            + jnp.dot(lo, ones_blocks, preferred_element_type=F32))


@jax.custom_vjp
def _seg(x, ones_blocks):
    return _seg_raw(x, ones_blocks)


def _seg_fwd(x, ones_blocks):
    return _seg_raw(x, ones_blocks), ones_blocks


def _seg_bwd(ones_blocks, g):
    return _seg_raw(g, ones_blocks), jnp.zeros_like(ones_blocks)


_seg.defvjp(_seg_fwd, _seg_bwd)


def _head_ones():
    h = jnp.arange(D_RWKV) // HEAD
    return (h[:, None] == h[None, :]).astype(BF16)


def _prep_math(ps, k_k, w0_f, a0_f, k_a_f, w0_b, a0_b, k_a_b, wup_f, aup_f, wup_b, aup_b, gup, ones_blocks):
    r = ps[:, 0:512]
    k = ps[:, 512:1024]
    v = ps[:, 1024:1536]
    xwa = ps[:, 1536:1664]
    xg = ps[:, 1664:D_SHIFT_PAD]
    kk_raw = k * k_k
    norm = jnp.sqrt(_seg(kk_raw * kk_raw, ones_blocks))
    kk = kk_raw / jnp.maximum(norm, NORM_EPS)
    t_xwa = jnp.tanh(xwa)
    outs = [r, v, kk]
    for w0, a0, k_a, wup, aup in ((w0_f, a0_f, k_a_f, wup_f, aup_f), (w0_b, a0_b, k_a_b, wup_b, aup_b)):
        decay = jnp.exp(-LOG_DECAY_SCALE * jax.nn.sigmoid(w0 + _bdot(t_xwa, wup)))
        rate = jax.nn.sigmoid(a0 + _bdot(xwa, aup))
        outs += [decay, k * (1.0 + (rate - 1.0) * k_a), kk * rate]
    outs.append(_bdot(jax.nn.sigmoid(xg), gup))
    return tuple(outs)


def _prep_args(tm, ps_ref, pv_ref, mat_refs, ones_ref):
    vecs = [jnp.broadcast_to(pv_ref[j:j + 1, :], (tm, D_RWKV)) for j in range(7)]
    return [ps_ref[...]] + vecs + [m[...] for m in mat_refs] + [ones_ref[...]]


_PREP_MAT_SHAPES = ((128, D_RWKV),) * 4 + ((D_GATE_PAD, D_RWKV),)


def _prep_fwd(ps, pvec, mats, ones_blocks, tm):
    t = ps.shape[0]

    def body(ps_ref, pv_ref, m0, m1, m2, m3, m4, ones_ref, *out_refs):
        outs = _prep_math(*_prep_args(tm, ps_ref, pv_ref, (m0, m1, m2, m3, m4), ones_ref))
        for o_ref, val in zip(out_refs, outs[2:]):
            o_ref[...] = val

    return pl.pallas_call(
        body, name="prep_fwd", grid=(t // tm,),
        in_specs=[_row(tm, D_SHIFT_PAD), _fixed((8, D_RWKV))] + [_fixed(s) for s in _PREP_MAT_SHAPES]
        + [_fixed((D_RWKV, D_RWKV))],
        out_specs=[_row(tm, D_RWKV)] * 8, out_shape=[_sds((t, D_RWKV))] * 8,
        compiler_params=_cp("parallel"))(ps, pvec, *mats, ones_blocks)


def _prep_bwd(ps, pvec, mats, ones_blocks, cts, tm):
    t = ps.shape[0]
    counts = [len(c) for c in cts]
    flat = [a for c in cts for a in c]

    def body(ps_ref, pv_ref, m0, m1, m2, m3, m4, ones_ref, *refs):
        ct_refs = refs[:len(flat)]
        q_ref, dpv_ref = refs[len(flat)], refs[len(flat) + 1]
        dmat_refs = refs[len(flat) + 2:]
        args = _prep_args(tm, ps_ref, pv_ref, (m0, m1, m2, m3, m4), ones_ref)
        _, vjp = jax.vjp(lambda *a: _prep_math(*a, args[-1]), *args[:-1])
        ct_vals, pos = [], 0
        for n in counts:
            val = ct_refs[pos][...]
            for extra in ct_refs[pos + 1:pos + n]:
                val = val + extra[...]
            ct_vals.append(val)
            pos += n
        grads = vjp(tuple(ct_vals))
        q_ref[...] = grads[0]

        @pl.when(pl.program_id(0) == 0)
        def _():
            dpv_ref[...] = jnp.zeros_like(dpv_ref)
            for d_ref in dmat_refs:
                d_ref[...] = jnp.zeros_like(d_ref)

        for j in range(7):
            dpv_ref[j:j + 1, :] += jnp.sum(grads[1 + j], axis=0, keepdims=True)
        for d_ref, gm in zip(dmat_refs, grads[8:13]):
            d_ref[...] += gm

    return pl.pallas_call(
        body, name="prep_bwd", grid=(t // tm,),
        in_specs=[_row(tm, D_SHIFT_PAD), _fixed((8, D_RWKV))] + [_fixed(s) for s in _PREP_MAT_SHAPES]
        + [_fixed((D_RWKV, D_RWKV))] + [_row(tm, D_RWKV)] * len(flat),
        out_specs=[_row(tm, D_SHIFT_PAD), _fixed((8, D_RWKV))] + [_fixed(s) for s in _PREP_MAT_SHAPES],
        out_shape=[_sds((t, D_SHIFT_PAD)), _sds((8, D_RWKV))] + [_sds(s) for s in _PREP_MAT_SHAPES],
        compiler_params=_cp("arbitrary"))(ps, pvec, *mats, ones_blocks, *flat)


def _pair_ones():
    h = jnp.arange(2 * HEAD) // HEAD
    block = (h[:, None] == h[None, :]).astype(BF16)
    return jnp.concatenate([block, block], axis=0)


def _diag_mask():
    lane = lax.broadcasted_iota(jnp.int32, (HEAD, 2 * HEAD), 1)
    sub = lax.broadcasted_iota(jnp.int32, (HEAD, 2 * HEAD), 0)
    return jnp.where((lane & (HEAD - 1)) == sub, 1.0, 0.0).astype(F32)


def _to_row(cols, dmask):
    return jnp.sum(cols * dmask, axis=0, keepdims=True)


def _seg_many(exact, rounded, ones_pair):
    out_exact, out_rounded = [], []
    if exact:
        parts = []
        for x in exact:
            hi = x.astype(BF16)
            parts.append(jnp.concatenate([hi, (x - hi.astype(F32)).astype(BF16)], axis=1))
        res = jnp.dot(jnp.concatenate(parts, axis=0), ones_pair, preferred_element_type=F32)
        out_exact = [res[HEAD * c:HEAD * (c + 1)] for c in range(len(exact))]
    if rounded:
        res = jnp.dot(jnp.concatenate([x.astype(BF16) for x in rounded], axis=0), ones_pair[0:2 * HEAD],
                      preferred_element_type=F32)
        out_rounded = [res[HEAD * c:HEAD * (c + 1)] for c in range(len(rounded))]
    return out_exact, out_rounded


N_CHAIN = 2 * N_PAIR


def _chain(c):
    d, p = divmod(c, N_PAIR)
    return d, slice(2 * HEAD * p, 2 * HEAD * (p + 1))


def _scan_specs(n_chunks, col_blocks, fwd_chunk, bwd_chunk):
    def spec(chunk_of, col):
        return pl.BlockSpec((SCAN_CHUNK, D_RWKV), lambda b, g: (b * n_chunks + chunk_of(g), col))
    return [spec(fwd_chunk, c) for c in col_blocks] + [spec(bwd_chunk, c) for c in col_blocks]


def _scan_fwd(ps, kk, dirs, batch, seq):
    t = batch * seq
    n = seq // SCAN_CHUNK
    groups = SCAN_CHUNK // GROUP
    up = lambda g: g
    down = lambda g: n - 1 - g
    col_blocks = (0, 2, 0, 0, 0, 0)

    def body(*refs):
        dir_refs = (refs[0:6], refs[6:12])
        ones_ref = refs[12]
        y_refs, hist_refs, st_ref = refs[13:15], refs[15:17], refs[17]

        @pl.when(pl.program_id(1) == 0)
        def _():
            st_ref[...] = jnp.zeros_like(st_ref)

        ones_pair = ones_ref[...]
        dmask = _diag_mask()
        dmask_b = dmask.astype(BF16)
        sub8 = lax.broadcasted_iota(jnp.int32, (GROUP, 2 * HEAD), 0)

        def group(gi, carry):
            off = (pl.multiple_of(gi * GROUP, GROUP), pl.multiple_of((groups - 1 - gi) * GROUP, GROUP))
            loaded = [tuple(ref[pl.ds(off[d], GROUP), :] for ref in dir_refs[d]) for d in range(2)]
            states = list(carry)
            y_acc = [jnp.zeros((GROUP, 2 * HEAD), F32) for _ in range(N_CHAIN)]
            for step in range(GROUP):
                rows, idx = [], []
                for c in range(N_CHAIN):
                    d, lanes = _chain(c)
                    i = step if d == 0 else GROUP - 1 - step
                    idx.append(i)
                    rows.append(tuple(x8[i:i + 1, lanes] for x8 in loaded[d]))
                    hist_refs[d][c % N_PAIR, gi * GROUP + step] = states[c]
                _, v_cols = _seg_many([], [dmask_b * rows[c][1].astype(BF16) for c in range(N_CHAIN)], ones_pair)
                sas, _ = _seg_many([states[c] * rows[c][2] for c in range(N_CHAIN)], [], ones_pair)
                for c in range(N_CHAIN):
                    _, _, _, w_row, kd_row, b_row = rows[c]
                    states[c] = states[c] * w_row - sas[c] * b_row + v_cols[c] * kd_row
                _, ys = _seg_many([], [states[c] * rows[c][0] for c in range(N_CHAIN)], ones_pair)
                for c in range(N_CHAIN):
                    y_acc[c] = jnp.where(sub8 == idx[c], _to_row(ys[c], dmask), y_acc[c])
            for c in range(N_CHAIN):
                d, lanes = _chain(c)
                y_refs[d][pl.ds(off[d], GROUP), lanes] = y_acc[c]
            return tuple(states)

        final = lax.fori_loop(0, groups, group, tuple(st_ref[c] for c in range(N_CHAIN)))
        for c in range(N_CHAIN):
            st_ref[c] = final[c]
            hist_refs[c // N_PAIR][c % N_PAIR, SCAN_CHUNK] = final[c]

    y_spec_f = pl.BlockSpec((SCAN_CHUNK, D_RWKV), lambda b, g: (b * n + up(g), 0))
    y_spec_b = pl.BlockSpec((SCAN_CHUNK, D_RWKV), lambda b, g: (b * n + down(g), 0))
    hist_shape = (batch, n, N_PAIR, SCAN_CHUNK + 1, HEAD, 2 * HEAD)
    hist_block = (None, None, N_PAIR, SCAN_CHUNK + 1, HEAD, 2 * HEAD)
    hist_spec_f = pl.BlockSpec(hist_block, lambda b, g: (b, up(g), 0, 0, 0, 0))
    hist_spec_b = pl.BlockSpec(hist_block, lambda b, g: (b, down(g), 0, 0, 0, 0))
    ones_spec = pl.BlockSpec((4 * HEAD, 2 * HEAD), lambda b, g: (0, 0))
    (wf, kdf, bf), (wb, kdb, bb) = dirs
    return pl.pallas_call(
        body, name="wkv_fwd", grid=(batch, n),
        in_specs=_scan_specs(n, col_blocks, up, down) + [ones_spec],
        out_specs=[y_spec_f, y_spec_b, hist_spec_f, hist_spec_b],
        out_shape=[_sds((t, D_RWKV)), _sds((t, D_RWKV)), _sds(hist_shape), _sds(hist_shape)],
        scratch_shapes=[pltpu.VMEM((N_CHAIN, HEAD, 2 * HEAD), F32)],
        compiler_params=_cp("parallel", "arbitrary"),
    )(ps, ps, kk, wf, kdf, bf, ps, ps, kk, wb, kdb, bb, _pair_ones())


def _scan_bwd(ps, kk, dirs, dy, hist_f, hist_b, batch, seq):
    t = batch * seq
    n = seq // SCAN_CHUNK
    groups = SCAN_CHUNK // GROUP
    fwd_chunk = lambda g: n - 1 - g
    bwd_chunk = lambda g: g
    col_blocks = (0, 2, 0, 0, 0, 0, 0)

    def undo_group(dir_refs, out_refs, hist_refs, gi, d_states, ones_pair, dmask, sub8):
        d_states = list(d_states)
        loaded, blocks = [], []
        for d in range(2):
            blk = groups - 1 - gi if d == 0 else gi
            blocks.append(pl.ds(pl.multiple_of(blk * GROUP, GROUP), GROUP))
            r8, v8, kk8, w8, kd8, b8, dy8 = (ref[blocks[d], :] for ref in dir_refs[d])
            loaded.append((r8, v8, kk8, w8, kd8, -b8, dy8))
        acc = [[jnp.zeros((GROUP, 2 * HEAD), F32) for _ in range(6)] for _ in range(N_CHAIN)]
        for step in range(GROUP):
            rows, idx, before, after = [], [], [], []
            for c in range(N_CHAIN):
                d, lanes = _chain(c)
                i = GROUP - 1 - step if d == 0 else step
                q = (groups - 1 - gi) * GROUP + i if d == 0 else SCAN_CHUNK - 1 - (gi * GROUP + i)
                idx.append(i)
                rows.append(tuple(x8[i:i + 1, lanes] for x8 in loaded[d]))
                before.append(hist_refs[d][c % N_PAIR, q])
                after.append(hist_refs[d][c % N_PAIR, q + 1])
            _, cols = _seg_many([], [dmask.astype(BF16) * rows[c][j].astype(BF16) for c in range(N_CHAIN) for j in (1, 6)],
                                ones_pair)
            v_cols, dy_cols = cols[0::2], cols[1::2]
            d_now = [d_states[c] + dy_cols[c] * rows[c][0] for c in range(N_CHAIN)]
            d_sas, _ = _seg_many([d_now[c] * rows[c][5] for c in range(N_CHAIN)], [], ones_pair)
            _, others = _seg_many(
                [], [x for c in range(N_CHAIN) for x in (before[c] * rows[c][2], d_now[c] * rows[c][4])], ones_pair)
            for c in range(N_CHAIN):
                sa, d_sa, dv_cols = others[2 * c], d_sas[c], others[2 * c + 1]
                rows_out = (
                    jnp.sum(after[c] * dy_cols[c], axis=0, keepdims=True),
                    jnp.sum(d_now[c] * before[c], axis=0, keepdims=True),
                    jnp.sum(d_now[c] * v_cols[c], axis=0, keepdims=True),
                    _to_row(dv_cols, dmask),
                    jnp.sum(before[c] * d_sa, axis=0, keepdims=True),
                    -jnp.sum(d_now[c] * sa, axis=0, keepdims=True),
                )
                acc[c] = [jnp.where(sub8 == idx[c], val, a) for val, a in zip(rows_out, acc[c])]
                d_states[c] = d_now[c] * rows[c][3] + d_sa * rows[c][2]
        for c in range(N_CHAIN):
            d, lanes = _chain(c)
            for o_ref, val in zip(out_refs[d], acc[c]):
                o_ref[blocks[d], lanes] = val
        return tuple(d_states)

    def body(*refs):
        dir_refs = (refs[0:7], refs[7:14])
        hist_refs, ones_ref = refs[14:16], refs[16]
        out_refs = (refs[17:23], refs[23:29])
        dst_ref = refs[29]

        @pl.when(pl.program_id(1) == 0)
        def _():
            dst_ref[...] = jnp.zeros_like(dst_ref)

        ones_pair = ones_ref[...]
        dmask = _diag_mask()
        sub8 = lax.broadcasted_iota(jnp.int32, (GROUP, 2 * HEAD), 0)

        def group(gi, carry):
            return undo_group(dir_refs, out_refs, hist_refs, gi, carry, ones_pair, dmask, sub8)

        final = lax.fori_loop(0, groups, group, tuple(dst_ref[c] for c in range(N_CHAIN)))
        for c in range(N_CHAIN):
            dst_ref[c] = final[c]

    blk = (SCAN_CHUNK, D_RWKV)
    out_f = pl.BlockSpec(blk, lambda b, g: (b * n + fwd_chunk(g), 0))
    out_b = pl.BlockSpec(blk, lambda b, g: (b * n + bwd_chunk(g), 0))
    hist_block = (None, None, N_PAIR, SCAN_CHUNK + 1, HEAD, 2 * HEAD)
    hist_spec_f = pl.BlockSpec(hist_block, lambda b, g: (b, fwd_chunk(g), 0, 0, 0, 0))
    hist_spec_b = pl.BlockSpec(hist_block, lambda b, g: (b, bwd_chunk(g), 0, 0, 0, 0))
    ones_spec = pl.BlockSpec((4 * HEAD, 2 * HEAD), lambda b, g: (0, 0))
    (wf, kdf, bf), (wb, kdb, bb) = dirs
    outs = pl.pallas_call(
        body, name="wkv_bwd", grid=(batch, n),
        in_specs=_scan_specs(n, col_blocks, fwd_chunk, bwd_chunk) + [hist_spec_f, hist_spec_b, ones_spec],
        out_specs=[out_f] * 6 + [out_b] * 6,
        out_shape=[_sds((t, D_RWKV))] * 12,
        scratch_shapes=[pltpu.VMEM((N_CHAIN, HEAD, 2 * HEAD), F32)],
        compiler_params=_cp("parallel", "arbitrary"),
    )(ps, ps, kk, wf, kdf, bf, dy, ps, ps, kk, wb, kdb, bb, dy, hist_f, hist_b, _pair_ones())
    return outs[0:6], outs[6:12]


def _post_math(y, r, kd_f, kd_b, v, gate, gn_w, gn_b, rk_f, rk_b, ones_blocks):
    mean = _seg(y, ones_blocks) * (1.0 / HEAD)
    yc = y - mean
    var = _seg(yc * yc, ones_blocks) * (1.0 / HEAD)
    yn = yc * lax.rsqrt(var + GN_EPS) * gn_w + gn_b
    bonus = _seg(r * kd_f * rk_f, ones_blocks) * v + _seg(r * kd_b * rk_b, ones_blocks) * v
    return (yn + bonus) * gate


def _conv_parts(pc, halo_prev, halo_next, has_prev, has_next):
    gate_b, gate_c, hid = pc[:, 0:512], pc[:, 512:1024], pc[:, 1024:1536]
    u = gate_c * hid
    u_prev_row = halo_prev[GROUP - 1:GROUP, 512:1024] * halo_prev[GROUP - 1:GROUP, 1024:1536] * has_prev
    u_next_row = halo_next[0:1, 512:1024] * halo_next[0:1, 1024:1536] * has_next
    u_down, u_up = _shifted(u, u_prev_row, u_next_row)
    return gate_b, gate_c, hid, u, u_down, u_up


def _post_specs(tm, t):
    pc_prev, pc_next = _halo_specs(tm, D_CONV3, t)
    col = lambda c: pl.BlockSpec((tm, D_RWKV), lambda i: (i, c))
    return ([col(0), col(0), col(0), col(0), col(0), col(2), col(0), _row(tm, D_CONV3), pc_prev, pc_next,
             _fixed((8, D_RWKV)), _fixed((D_RWKV, D_RWKV))])


def _post_fwd(y_f, y_b, ps, kd_f, kd_b, gate, pc, qvec, ones_blocks, tm, seq):
    t = ps.shape[0]

    def body(yf_ref, yb_ref, r_ref, kdf_ref, kdb_ref, v_ref, g_ref, pc_ref, hp_ref, hn_ref, qv_ref, ones_ref,
             o_ref, ot_ref):
        has_prev, has_next = _edge_flags(tm, seq)
        vec = [jnp.broadcast_to(qv_ref[j:j + 1, :], (tm, D_RWKV)) for j in range(7)]
        o_rwkv = _post_math(yf_ref[...] + yb_ref[...], r_ref[...], kdf_ref[...], kdb_ref[...], v_ref[...],
                            g_ref[...], vec[0], vec[1], vec[2], vec[3], ones_ref[...])
        gate_b, _, _, u, u_down, u_up = _conv_parts(pc_ref[...], hp_ref[...], hn_ref[...], has_prev, has_next)
        o_conv = gate_b * (vec[4] * u_down + vec[5] * u + vec[6] * u_up)
        for half, val in enumerate((o_rwkv, o_conv)):
            o_ref[:, D_RWKV * half:D_RWKV * (half + 1)] = val.astype(BF16)
            ot_ref[D_RWKV * half:D_RWKV * (half + 1), :] = jnp.transpose(val).astype(BF16)

    return pl.pallas_call(
        body, name="post_fwd", grid=(t // tm,), in_specs=_post_specs(tm, t),
        out_specs=[_row(tm, D_MODEL), _col(tm, D_MODEL)],
        out_shape=[_sds((t, D_MODEL), BF16), _sds((D_MODEL, t), BF16)], compiler_params=_cp("parallel"),
    )(y_f, y_b, ps, kd_f, kd_b, ps, gate, pc, pc, pc, qvec, ones_blocks)


def _post_bwd(d_out, y_f, y_b, ps, kd_f, kd_b, gate, pc, qvec, ones_blocks, tm, seq):
    t = ps.shape[0]
    do_prev, do_next = _halo_specs(tm, D_MODEL, t)

    def body(do_ref, dop_ref, don_ref, yf_ref, yb_ref, r_ref, kdf_ref, kdb_ref, v_ref, g_ref, pc_ref, hp_ref,
             hn_ref, qv_ref, ones_ref, dy_ref, dr_ref, dkdf_ref, dkdb_ref, dv_ref, dg_ref, dpc_ref, dqv_ref):
        has_prev, has_next = _edge_flags(tm, seq)
        vec = [jnp.broadcast_to(qv_ref[j:j + 1, :], (tm, D_RWKV)) for j in range(7)]
        ones_v = ones_ref[...]
        args = (yf_ref[...] + yb_ref[...], r_ref[...], kdf_ref[...], kdb_ref[...], v_ref[...], g_ref[...],
                vec[0], vec[1], vec[2], vec[3])
        _, vjp = jax.vjp(lambda *a: _post_math(*a, ones_v), *args)
        grads = vjp(do_ref[:, 0:D_RWKV])
        for o_ref, gval in zip((dy_ref, dr_ref, dkdf_ref, dkdb_ref, dv_ref, dg_ref), grads[0:6]):
            o_ref[...] = gval

        hp, hn = hp_ref[...], hn_ref[...]
        gate_b, gate_c, hid, u, u_down, u_up = _conv_parts(pc_ref[...], hp, hn, has_prev, has_next)
        d_oc = do_ref[:, D_RWKV:2 * D_RWKV]
        d_cu = d_oc * gate_b
        d_cu_prev = dop_ref[GROUP - 1:GROUP, D_RWKV:2 * D_RWKV] * hp[GROUP - 1:GROUP, 0:512] * has_prev
        d_cu_next = don_ref[0:1, D_RWKV:2 * D_RWKV] * hn[0:1, 0:512] * has_next
        d_cu_down, d_cu_up = _shifted(d_cu, d_cu_prev, d_cu_next)
        d_u = vec[5] * d_cu + vec[4] * d_cu_up + vec[6] * d_cu_down
        dpc_ref[:, 0:512] = (d_oc * (vec[4] * u_down + vec[5] * u + vec[6] * u_up)).astype(BF16)
        dpc_ref[:, 512:1024] = (d_u * hid).astype(BF16)
        dpc_ref[:, 1024:1536] = (d_u * gate_c).astype(BF16)

        @pl.when(pl.program_id(0) == 0)
        def _():
            dqv_ref[...] = jnp.zeros_like(dqv_ref)

        vec_grads = list(grads[6:10]) + [d_cu * u_down, d_cu * u, d_cu * u_up]
        for j, gval in enumerate(vec_grads):
            dqv_ref[j:j + 1, :] += jnp.sum(gval, axis=0, keepdims=True)

    return pl.pallas_call(
        body, name="post_bwd", grid=(t // tm,),
        in_specs=[_row(tm, D_MODEL), do_prev, do_next] + _post_specs(tm, t),
        out_specs=[_row(tm, D_RWKV)] * 6 + [_row(tm, D_CONV3), _fixed((8, D_RWKV))],
        out_shape=[_sds((t, D_RWKV))] * 6 + [_sds((t, D_CONV3), BF16), _sds((8, D_RWKV))],
        compiler_params=_cp("arbitrary"),
    )(d_out, d_out, d_out, y_f, y_b, ps, kd_f, kd_b, ps, gate, pc, pc, pc, qvec, ones_blocks)


def _adamw_math(wv, gv, mv, vv):
    m2 = ADAM_B1 * mv + (1.0 - ADAM_B1) * gv
    v2 = ADAM_B2 * vv + (1.0 - ADAM_B2) * (gv * gv)
    m_hat = m2 / (1.0 - ADAM_B1 ** ADAM_STEP)
    v_hat = v2 / (1.0 - ADAM_B2 ** ADAM_STEP)
    return -ADAM_LR * (m_hat / (jnp.sqrt(v_hat) + ADAM_EPS) + ADAM_WD * wv), m2, v2


def _adamw_small(items):
    n = len(items)

    def body(*refs):
        ins, outs = refs[:4 * n], refs[4 * n:]
        for k in range(n):
            w_ref, g_ref, m_ref, v_ref = ins[4 * k:4 * k + 4]
            for o_ref, val in zip(outs[3 * k:3 * k + 3], _adamw_math(w_ref[...], g_ref[...], m_ref[...], v_ref[...])):
                o_ref[...] = val

    flat = [a for item in items for a in item]
    outs = pl.pallas_call(
        body, name="adamw_small", out_shape=[_sds(item[0].shape) for item in items for _ in range(3)],
        compiler_params=_cp())(*flat)
    return [tuple(outs[3 * k:3 * k + 3]) for k in range(n)]


def _adamw(w, g, m, v, name):
    r, c = w.shape[-2:]
    tr = _tile(r, 256, 8)
    if w.ndim == 3:
        spec = pl.BlockSpec((None, tr, c), lambda i: (0, i, 0))
    else:
        spec = pl.BlockSpec((tr, c), lambda i: (i, 0))

    def body(w_ref, g_ref, m_ref, v_ref, d_ref, nm_ref, nv_ref):
        d_ref[...], nm_ref[...], nv_ref[...] = _adamw_math(w_ref[...], g_ref[...], m_ref[...], v_ref[...])

    return pl.pallas_call(
        body, name=name, grid=(r // tr,), in_specs=[spec] * 4, out_specs=[spec] * 3,
        out_shape=[_sds(w.shape)] * 3, compiler_params=_cp("parallel"))(w, g, m, v)


_ANY = pl.BlockSpec(memory_space=pl.ANY)


def _place():
    return lax.axis_index("x"), lax.axis_index("y"), lax.axis_index("c")


def _other_chips(x, y):
    return [(1 - x, y), (x, 1 - y), (1 - x, 1 - y)]


def _remote(src, dst, send_sems, recv_sems, k, to):
    return pltpu.make_async_remote_copy(src_ref=src, dst_ref=dst, send_sem=send_sems.at[k],
                                        recv_sem=recv_sems.at[k], device_id=to, device_id_type=MESH)


def _gather_weights(pack):
    rows, width = pack.shape
    half = rows // 2

    def body(x_ref, out_ref, send_sems, recv_sems):
        x, y, c = _place()
        sibling = (x, y, 1 - c)
        chips = _other_chips(x, y)

        def block(chip, part):
            return out_ref.at[2 * chip[0] + chip[1], pl.ds(part * half, half), :]

        first = [_remote(x_ref.at[pl.ds(c * half, half), :], block((x, y), c), send_sems, recv_sems, j, (*chip, c))
                 for j, chip in enumerate(chips)]
        for cp in first:
            cp.start()
        passed = [_remote(block(chip, c), block(chip, c), send_sems, recv_sems, 3 + j, sibling)
                  for j, chip in enumerate(chips)]
        for j, chip in enumerate(chips):
            _remote(block(chip, c), block(chip, c), send_sems, recv_sems, j, sibling).wait_recv()
            passed[j].start()
        for j, chip in enumerate(chips):
            _remote(block(chip, 1 - c), block(chip, 1 - c), send_sems, recv_sems, 3 + j, sibling).wait_recv()
        for cp in first + passed:
            cp.wait_send()

    return pl.pallas_call(
        body, name="gather_weights", in_specs=[_ANY], out_specs=_ANY,
        out_shape=_sds((N_SHARD, rows, width), pack.dtype),
        scratch_shapes=[pltpu.SemaphoreType.DMA((6,)), pltpu.SemaphoreType.DMA((6,))],
    )(pack)


def _swap_with_sibling(block, name):
    def body(x_ref, out_ref, send_sems, recv_sems):
        x, y, c = _place()
        cp = _remote(x_ref, out_ref, send_sems, recv_sems, 0, (x, y, 1 - c))
        cp.start()
        cp.wait()

    return pl.pallas_call(
        body, name=name, in_specs=[_ANY], out_specs=_ANY, out_shape=_sds(block.shape, block.dtype),
        scratch_shapes=[pltpu.SemaphoreType.DMA((1,)), pltpu.SemaphoreType.DMA((1,))],
    )(block)


def _swap_other_half(packed):
    slots, rows, width = packed.shape
    half = rows // 2

    def body(x_ref, out_ref, send_sems, recv_sems):
        x, y, c = _place()
        cp = _remote(x_ref.at[:, pl.ds((1 - c) * half, half), :], out_ref, send_sems, recv_sems, 0, (x, y, 1 - c))
        cp.start()
        cp.wait()

    return pl.pallas_call(
        body, name="swap_halves", in_specs=[_ANY], out_specs=_ANY, out_shape=_sds((slots, half, width)),
        scratch_shapes=[pltpu.SemaphoreType.DMA((1,)), pltpu.SemaphoreType.DMA((1,))],
    )(packed)


def _add_halves(packed, got, c):
    slots, rows, width = packed.shape
    half = rows // 2
    tr = _tile(half, 408, 16)
    per = half // tr
    block = (None, tr, width)

    def body(c_ref, mine_ref, got_ref, sum_ref, sum16_ref):
        acc = mine_ref[...] + got_ref[...]
        sum_ref[...] = acc
        sum16_ref[...] = acc.astype(BF16)

    plain = pl.BlockSpec(block, lambda s, i, c_ref: (s, i, 0))
    grid_spec = pltpu.PrefetchScalarGridSpec(
        num_scalar_prefetch=1, grid=(slots, per),
        in_specs=[pl.BlockSpec(block, lambda s, i, c_ref: (s, c_ref[0] * per + i, 0)), plain],
        out_specs=[plain, plain])
    return pl.pallas_call(
        body, name="add_halves", grid_spec=grid_spec,
        out_shape=[_sds((slots, half, width)), _sds((slots, half, width), BF16)],
        compiler_params=_cp("parallel", "parallel"))(c.reshape(1).astype(jnp.int32), packed, got)


def _add_quarters(chip_sum, others, chip):
    _, rows, width = chip_sum.shape
    tr = _tile(rows, 408, 16)

    def body(chip_ref, own_ref, others_ref, o_ref):
        acc = own_ref[...]
        for j in range(3):
            acc = acc + others_ref[j].astype(F32)
        o_ref[...] = acc

    grid_spec = pltpu.PrefetchScalarGridSpec(
        num_scalar_prefetch=1, grid=(rows // tr,),
        in_specs=[pl.BlockSpec((None, tr, width), lambda i, chip_ref: (chip_ref[0], i, 0)),
                  pl.BlockSpec((3, tr, width), lambda i, chip_ref: (0, i, 0))],
        out_specs=pl.BlockSpec((tr, width), lambda i, chip_ref: (i, 0)))
    return pl.pallas_call(
        body, name="add_quarters", grid_spec=grid_spec, out_shape=_sds((rows, width)),
        compiler_params=_cp("parallel"))(chip.reshape(1).astype(jnp.int32), chip_sum, others)


def _exchange_quarters(parts):
    _, rows, width = parts.shape

    def body(x_ref, out_ref, send_sems, recv_sems):
        x, y, c = _place()
        copies = [_remote(x_ref.at[2 * chip[0] + chip[1]], out_ref.at[j], send_sems, recv_sems, j, (*chip, c))
                  for j, chip in enumerate(_other_chips(x, y))]
        for cp in copies:
            cp.start()
        for cp in copies:
            cp.wait()

    return pl.pallas_call(
        body, name="exchange_quarters", in_specs=[_ANY], out_specs=_ANY,
        out_shape=_sds((3, rows, width), parts.dtype),
        scratch_shapes=[pltpu.SemaphoreType.DMA((3,)), pltpu.SemaphoreType.DMA((3,))],
    )(parts)


def _allreduce_small(vec):
    rows, width = vec.shape
    vmem = pl.BlockSpec(memory_space=pltpu.VMEM)

    def body(x_ref, o_ref, buf_ref, send_sems, recv_sems):
        x, y, c = _place()
        me = 4 * x + 2 * y + c
        buf_ref[me] = x_ref[...]
        copies = []
        for k in range(1, N_DEV):
            peer = (x ^ ((k >> 2) & 1), y ^ ((k >> 1) & 1), c ^ (k & 1))
            copies.append(_remote(x_ref, buf_ref.at[me], send_sems, recv_sems, k - 1, peer))
        for cp in copies:
            cp.start()
        for k in range(1, N_DEV):
            _remote(x_ref, buf_ref.at[me ^ k], send_sems, recv_sems, k - 1, (x, y, c)).wait_recv()
        for cp in copies:
            cp.wait_send()
        total = buf_ref[0]
        for d in range(1, N_DEV):
            total = total + buf_ref[d]
        o_ref[...] = total

    return pl.pallas_call(
        body, name="allreduce_small", in_specs=[vmem], out_specs=vmem, out_shape=_sds((rows, width)),
        scratch_shapes=[pltpu.VMEM((N_DEV, rows, width), F32), pltpu.SemaphoreType.DMA((N_DEV - 1,)),
                        pltpu.SemaphoreType.DMA((N_DEV - 1,))],
    )(vec)


def _rows1024(a):
    return a.reshape(-1, 1024)


def _pad_rows(a, rows):
    return jnp.concatenate([a, jnp.zeros((rows - a.shape[0], a.shape[1]), a.dtype)], axis=0)


_TRANSPOSED = ("w_in", "w_gate", "w_up")
_SMALL_SHARDED = ("w_up_f", "w_up_b", "a_up_f", "a_up_b", "g_up")
_BIG_SHARDED = ("w_in", "w_out", "w_gate", "w_up", "w_down")


def _pack_weight_shards(w):
    conv_bits = lax.bitcast_convert_type(w["conv_w"], BF16).reshape(1, -1)
    conv_row = jnp.concatenate([conv_bits, jnp.zeros((1, 1024 - conv_bits.shape[1]), BF16)], axis=1)
    parts = []
    for name, _ in _PACK_ROWS[:-1]:
        a = w[name].astype(BF16)
        parts.append(a.T if name in _TRANSPOSED else _rows1024(a))
    return _pad_rows(jnp.concatenate(parts + [conv_row], axis=0), PACK_R)


def _unpack_gathered(gathered):
    out, row = {}, 0
    for name, n in _PACK_ROWS:
        out[name] = gathered[:, row:row + n]
        row += n
    cols = lambda a, k: jnp.concatenate([a[s].reshape(k, -1) for s in range(N_SHARD)], axis=1)
    conv = lax.bitcast_convert_type(out["conv_w"][:, 0, :768].reshape(N_SHARD, 3, 128, 2), F32)
    full = {name: out[name].reshape(-1, 1024).T for name in _TRANSPOSED}
    full.update(w_out=out["w_out"].reshape(D_MODEL, D_MODEL), w_down=out["w_down"].reshape(D_FF, D_MODEL),
                conv_w=jnp.concatenate([conv[s] for s in range(N_SHARD)], axis=1))
    full.update({name: cols(out[name], D_GATE if name == "g_up" else D_LORA) for name in _SMALL_SHARDED})
    return full


def _pack_grads(g):
    col_split = lambda a, s: a[:, s * (a.shape[1] // N_SHARD):(s + 1) * (a.shape[1] // N_SHARD)]
    row_split = lambda a, s: a[s * (a.shape[0] // N_SHARD):(s + 1) * (a.shape[0] // N_SHARD)]
    by_rows = {name: (g[name].T if name in _TRANSPOSED else g[name]) for name in _BIG_SHARDED}
    used = sum(n for _, n in _PACK_ROWS)
    parts = []
    for s in range(N_SHARD):
        conv = col_split(g["conv_w"], s).reshape(1, -1)
        parts += [row_split(by_rows[name], s) for name in _BIG_SHARDED]
        parts += [_rows1024(col_split(g[name], s)) for name in _SMALL_SHARDED]
        parts += [jnp.concatenate([conv, jnp.zeros((1, 1024 - conv.shape[1]), F32)], axis=1),
                  jnp.zeros((PACK_R - used, 1024), F32)]
    return jnp.concatenate(parts, axis=0).reshape(N_SHARD, PACK_R, 1024)


def _unpack_grad_shard(pack):
    small_shapes = {name: (D_GATE if name == "g_up" else D_LORA, 128) for name in _SMALL_SHARDED}
    out, row = {}, 0
    for name, n in _PACK_ROWS[:-1]:
        piece = pack[row:row + n]
        out[name] = piece.T if name in _TRANSPOSED else piece.reshape(small_shapes.get(name, piece.shape))
        row += n
    out["conv_w"] = pack[row, :384].reshape(3, 128)
    return out


_SMALL_LAYOUT = (("norm1_w", 1024), ("mu_shift", D_SHIFT), ("w0_f", 512), ("w0_b", 512), ("a0_f", 512),
                 ("a0_b", 512), ("k_k", 512), ("k_a_f", 512), ("k_a_b", 512), ("r_k_f", 512), ("r_k_b", 512),
                 ("gn_w", 512), ("gn_b", 512), ("norm2_w", 1024), ("norm_f_w", 1024), ("loss", 1))


def _pack_small(vals):
    rows = []
    for name, n in _SMALL_LAYOUT:
        flat = vals[name].reshape(-1)
        n_rows = -(-n // 1024)
        rows.append(jnp.concatenate([flat, jnp.zeros((n_rows * 1024 - n,), F32)]).reshape(n_rows, 1024))
    return _pad_rows(jnp.concatenate(rows, axis=0), SMALL_ROWS)


def _unpack_small(pack):
    out, row = {}, 0
    for name, n in _SMALL_LAYOUT:
        n_rows = -(-n // 1024)
        out[name] = pack[row:row + n_rows].reshape(-1)[:n]
        row += n_rows
    return out


_WEIGHTS = ("norm1_w", "w_in", "mu_shift", "w_up_f", "w0_f", "w_up_b", "w0_b", "a_up_f", "a0_f", "a_up_b", "a0_b",
            "g_up", "k_k", "k_a_f", "k_a_b", "r_k_f", "r_k_b", "gn_w", "gn_b", "conv_w", "w_out", "norm2_w",
            "w_gate", "w_up", "w_down", "norm_f_w")


def _train_step(x, loss_target, w, m, v):
    batch, seq, _ = x.shape
    t = batch * seq
    tm = _tile(seq, 256, 8)
    xs = x.reshape(t, D_MODEL)
    target = loss_target.reshape(t, D_MODEL)
    vec = lambda name: w[name].reshape(1, -1)

    local = {name: w[name][0] for name, _ in _PACK_ROWS}
    c = lax.axis_index("c")
    chip = 2 * lax.axis_index("x") + lax.axis_index("y")
    pack = _pack_weight_shards(local)
    full = _unpack_gathered(lax.dynamic_update_slice(_gather_weights(pack), pack[None], (chip, 0, 0)))
    w_in = full["w_in"]
    w_shift = jnp.concatenate([w_in[:, :D_SHIFT], jnp.zeros((D_MODEL, D_SHIFT_PAD - D_SHIFT), BF16)], axis=1)
    w_conv = w_in[:, D_SHIFT:]
    zeros_lora = jnp.zeros((D_LORA, D_RWKV), F32)
    lora = lambda name: full[name].astype(F32)
    mats = (jnp.concatenate([lora("w_up_f"), zeros_lora]), jnp.concatenate([zeros_lora, lora("a_up_f")]),
            jnp.concatenate([lora("w_up_b"), zeros_lora]), jnp.concatenate([zeros_lora, lora("a_up_b")]),
            jnp.concatenate([lora("g_up"), jnp.zeros((D_GATE_PAD - D_GATE, D_RWKV), F32)]))
    mu = jnp.concatenate([vec("mu_shift"), jnp.zeros((1, D_SHIFT_PAD - D_SHIFT), F32)], axis=1)
    mu = jnp.broadcast_to(mu, (GROUP, D_SHIFT_PAD))
    zero_row = jnp.zeros((1, D_RWKV), F32)
    pvec = jnp.concatenate([vec("k_k"), vec("w0_f"), vec("a0_f"), vec("k_a_f"), vec("w0_b"), vec("a0_b"),
                            vec("k_a_b"), zero_row], axis=0)
    qvec = jnp.concatenate([vec("gn_w"), vec("gn_b"), vec("r_k_f"), vec("r_k_b"), full["conv_w"], zero_row], axis=0)
    ones_blocks = _head_ones()

    h1, h1_t = _rmsnorm_fwd(xs, vec("norm1_w"), tm, "norm1_fwd")
    p_shift = _matmul(h1, w_shift, mode="nn", name="in_proj_shift")
    pc = _matmul(h1, w_conv, mode="nn", name="in_proj_conv")
    ps = _shift_fwd(p_shift, mu, tm, seq)
    kk, w_f, kd_f, b_f, w_b, kd_b, b_b, gate = _prep_fwd(ps, pvec, mats, ones_blocks, tm)
    dirs = ((w_f, kd_f, b_f), (w_b, kd_b, b_b))
    y_f, y_b, hist_f, hist_b = _scan_fwd(ps, kk, dirs, batch, seq)
    mixed, mixed_t = _post_fwd(y_f, y_b, ps, kd_f, kd_b, gate, pc, qvec, ones_blocks, tm, seq)
    x1 = _matmul(mixed, full["w_out"], mode="nn", name="out_proj", add=xs)
    h2, h2_t = _rmsnorm_fwd(x1, vec("norm2_w"), tm, "norm2_fwd")
    ff_gate, ff_up, act, act_t = _ffn_in(h2, full["w_gate"], full["w_up"])
    x2 = _matmul(act, full["w_down"], mode="nn", name="ffn_down", add=x1)
    d_x2, d_norm_f, loss_part = _loss_head(x2, w["norm_f_w"].reshape(1, -1), target, tm)

    g = {}
    g["w_down"] = _matmul(act_t, d_x2, mode="nn", name="ffn_down_dw")
    d_gate, d_up = _ffn_in_bwd(d_x2, full["w_down"], ff_gate, ff_up)
    d_h2 = _matmul(d_gate, full["w_gate"], mode="nt", name="ffn_gate_dx")
    d_h2 = _matmul(d_up, full["w_up"], mode="nt", name="ffn_up_dx", add=d_h2)
    g["w_gate"] = _matmul(h2_t, d_gate, mode="nn", name="ffn_gate_dw")
    g["w_up"] = _matmul(h2_t, d_up, mode="nn", name="ffn_up_dw")
    d_x1, d_norm2 = _rmsnorm_bwd(x1, vec("norm2_w"), d_h2, d_x2, tm, "norm2_bwd")
    d_mixed = _matmul(d_x1, full["w_out"], mode="nt", name="out_proj_dx")
    g["w_out"] = _matmul(mixed_t, d_x1, mode="nn", name="out_proj_dw")
    dy, dr_o, dkdf_o, dkdb_o, dv_o, d_gatev, d_pc, d_qvec = _post_bwd(
        d_mixed, y_f, y_b, ps, kd_f, kd_b, gate, pc, qvec, ones_blocks, tm, seq)
    (dr_f, dw_f, dkd_f, dv_f, dkk_f, db_f), (dr_b, dw_b, dkd_b, dv_b, dkk_b, db_b) = _scan_bwd(
        ps, kk, dirs, dy, hist_f, hist_b, batch, seq)
    cts = [[dr_f, dr_b, dr_o], [dv_f, dv_b, dv_o], [dkk_f, dkk_b], [dw_f], [dkd_f, dkdf_o], [db_f],
           [dw_b], [dkd_b, dkdb_o], [db_b], [d_gatev]]
    q, d_pvec, d_m0, d_m1, d_m2, d_m3, d_m4 = _prep_bwd(ps, pvec, mats, ones_blocks, cts, tm)
    d_pshift, d_mu = _shift_bwd(q, p_shift, mu, tm, seq)
    d_h1 = _matmul(d_pshift, w_shift, mode="nt", name="in_proj_shift_dx")
    d_h1 = _matmul(d_pc, w_conv, mode="nt", name="in_proj_conv_dx", add=d_h1)
    d_w_shift = _matmul(h1_t, d_pshift, mode="nn", name="in_proj_shift_dw")
    d_w_conv = _matmul(h1_t, d_pc, mode="nn", name="in_proj_conv_dw")
    g["w_in"] = jnp.concatenate([d_w_shift[:, :D_SHIFT], d_w_conv], axis=1)
    d_x, d_norm1 = _rmsnorm_bwd(xs, vec("norm1_w"), d_h1, d_x1, tm, "norm1_bwd")
    g["w_up_f"], g["a_up_f"] = d_m0[:D_LORA], d_m1[D_LORA:]
    g["w_up_b"], g["a_up_b"] = d_m2[:D_LORA], d_m3[D_LORA:]
    g["g_up"] = d_m4[:D_GATE]
    g["conv_w"] = d_qvec[4:7]

    packed = _pack_grads(g)
    chip_sum, chip_sum_bf16 = _add_halves(packed, _swap_other_half(packed), c)
    others = _exchange_quarters(chip_sum_bf16)
    eighth = _add_quarters(chip_sum, others, chip)
    other_eighth = _swap_with_sibling(eighth, "swap_eighths")
    grads = _unpack_grad_shard(jnp.concatenate([jnp.where(c == 0, eighth, other_eighth),
                                                jnp.where(c == 0, other_eighth, eighth)], axis=0))

    small = dict(norm1_w=d_norm1, mu_shift=d_mu[:, :D_SHIFT], w0_f=d_pvec[1], w0_b=d_pvec[4], a0_f=d_pvec[2],
                 a0_b=d_pvec[5], k_k=d_pvec[0], k_a_f=d_pvec[3], k_a_b=d_pvec[6], r_k_f=d_qvec[2], r_k_b=d_qvec[3],
                 gn_w=d_qvec[0], gn_b=d_qvec[1], norm2_w=d_norm2, norm_f_w=d_norm_f, loss=loss_part)
    reduced = _unpack_small(_allreduce_small(_pack_small(small)))
    loss = reduced.pop("loss")[0]
    grads.update(reduced)

    outs = {}
    small = [name for name in _WEIGHTS if name not in _BIG_SHARDED]
    as2d = lambda name: (1, w[name].shape[0]) if w[name].ndim == 1 else w[name].shape
    operands = lambda name: tuple(a.reshape(as2d(name)) for a in (w[name], grads[name], m[name], v[name]))
    updates = dict(zip(small, _adamw_small([operands(name) for name in small])))
    for name in _BIG_SHARDED:
        updates[name] = _adamw(*operands(name), "adamw_" + name)
    for name in _WEIGHTS:
        shape = w[name].shape
        outs[name] = (grads[name].reshape(shape),) + tuple(a.reshape(shape) for a in updates[name])
    d_x = d_x.reshape(batch, seq, D_MODEL)
    return (loss, d_x) + tuple(outs[name][k] for k in range(4) for name in _WEIGHTS)


def kernel(x, norm1_w, w_in, mu_shift, w_up_f, w0_f, w_up_b, w0_b, a_up_f, a0_f, a_up_b, a0_b, g_up, k_k, k_a_f, k_a_b, r_k_f, r_k_b, gn_w, gn_b, conv_w, w_out, norm2_w, w_gate, w_up, w_down, norm_f_w, loss_target, m_norm1_w, m_w_in, m_mu_shift, m_w_up_f, m_w0_f, m_w_up_b, m_w0_b, m_a_up_f, m_a0_f, m_a_up_b, m_a0_b, m_g_up, m_k_k, m_k_a_f, m_k_a_b, m_r_k_f, m_r_k_b, m_gn_w, m_gn_b, m_conv_w, m_w_out, m_norm2_w, m_w_gate, m_w_up, m_w_down, m_norm_f_w, v_norm1_w, v_w_in, v_mu_shift, v_w_up_f, v_w0_f, v_w_up_b, v_w0_b, v_a_up_f, v_a0_f, v_a_up_b, v_a0_b, v_g_up, v_k_k, v_k_a_f, v_k_a_b, v_r_k_f, v_r_k_b, v_gn_w, v_gn_b, v_conv_w, v_w_out, v_norm2_w, v_w_gate, v_w_up, v_w_down, v_norm_f_w):
    args = locals()
    w = {name: args[name] for name in _WEIGHTS}
    m = {name: args["m_" + name] for name in _WEIGHTS}
    v = {name: args["v_" + name] for name in _WEIGHTS}
    return _train_step(x, loss_target, w, m, v)
```

```python
import functools

import jax
import jax.numpy as jnp
from jax import lax
from jax.experimental import pallas as pl
from jax.experimental.pallas import tpu as pltpu

F32 = jnp.float32
BF16 = jnp.bfloat16
MESH = pl.DeviceIdType.MESH

D_MODEL = 1024
D_RWKV = 512
HEAD = 64
N_PAIR = D_RWKV // (2 * HEAD)
D_LORA = 64
D_GATE = 160
D_GATE_PAD = 384
D_FF = 2816
D_SHIFT = 1824
D_SHIFT_PAD = 2048
D_CONV3 = 1536
LOG_DECAY_SCALE = 0.606531
RMS_EPS = 1e-6
GN_EPS = 64e-5
NORM_EPS = 1e-12
ADAM_LR, ADAM_B1, ADAM_B2, ADAM_EPS, ADAM_WD, ADAM_STEP = 0.001, 0.9, 0.999, 1e-08, 0.01, 10

N_SHARD = 4
N_DEV = 8
V7X_VMEM_LIMIT = 48 * 1024 * 1024
SCAN_CHUNK = 32
GROUP = 8

_PACK_ROWS = (("w_in", 840), ("w_out", 256), ("w_gate", 704), ("w_up", 704), ("w_down", 704),
              ("w_up_f", 8), ("w_up_b", 8), ("a_up_f", 8), ("a_up_b", 8), ("g_up", 20), ("conv_w", 1))
PACK_R = 3264
PACK_H = PACK_R // 2
_FFN_NAMES = ("w_gate", "w_up", "w_down")
_EARLY_ROWS = tuple(item for item in _PACK_ROWS if item[0] not in _FFN_NAMES)
_FFN_ROWS = tuple(item for item in _PACK_ROWS if item[0] in _FFN_NAMES)
EARLY_R = 1152
SMALL_ROWS = 24


def _tile(n, cap, mult=128):
    best = None
    t = mult
    while t <= min(n, cap):
        if n % t == 0:
            best = t
        t += mult
    return best or n


def _cp(*sem):
    return pltpu.CompilerParams(dimension_semantics=sem or None, vmem_limit_bytes=V7X_VMEM_LIMIT)


def _sds(shape, dtype=F32):
    return jax.ShapeDtypeStruct(shape, dtype)


def _matmul(a, b, *, mode, name, out_dtype=F32, add=None):
    m, kdim = a.shape
    n = b.shape[1] if mode == "nn" else b.shape[0]
    tm, tn = _tile(m, 512, 8), _tile(n, 1536)
    tk = kdim if kdim <= 3584 else _tile(kdim, 1024)
    nk = kdim // tk
    a_spec = pl.BlockSpec((tm, tk), lambda i, j, k: (i, k))
    if mode == "nn":
        b_spec = pl.BlockSpec((tk, tn), lambda i, j, k: (k, j))
        dims = (((1,), (0,)), ((), ()))
    else:
        b_spec = pl.BlockSpec((tn, tk), lambda i, j, k: (j, k))
        dims = (((1,), (1,)), ((), ()))
    has_add = add is not None

    def body(*refs):
        a_ref, b_ref = refs[0], refs[1]
        add_ref = refs[2] if has_add else None
        o_ref = refs[3] if has_add else refs[2]
        part = lax.dot_general(a_ref[...].astype(BF16), b_ref[...].astype(BF16), dims,
                               preferred_element_type=F32)
        if nk == 1:
            if has_add:
                part = part + add_ref[...]
            o_ref[...] = part.astype(out_dtype)
        else:
            acc_ref = refs[-1]
            k = pl.program_id(2)

            @pl.when(k == 0)
            def _():
                acc_ref[...] = jnp.zeros_like(acc_ref)

            acc_ref[...] += part

            @pl.when(k == nk - 1)
            def _():
                res = acc_ref[...]
                if has_add:
                    res = res + add_ref[...]
                o_ref[...] = res.astype(out_dtype)

    o_spec = pl.BlockSpec((tm, tn), lambda i, j, k: (i, j))
    in_specs = [a_spec, b_spec] + ([o_spec] if has_add else [])
    args = (a, b) + ((add,) if has_add else ())
    return pl.pallas_call(
        body, name=name, grid=(m // tm, n // tn, nk), in_specs=in_specs, out_specs=o_spec,
        out_shape=_sds((m, n), out_dtype),
        scratch_shapes=[pltpu.VMEM((tm, tn), F32)] if nk > 1 else [],
        compiler_params=_cp("parallel", "parallel", "arbitrary"),
    )(*args)


def _row(tm, width):
    return pl.BlockSpec((tm, width), lambda i: (i, 0))


def _col(tm, height):
    return pl.BlockSpec((height, tm), lambda i: (0, i))


def _fixed(shape):
    return pl.BlockSpec(shape, lambda i: tuple(0 for _ in shape))


def _rmsnorm_fwd(x, w, tm, name):
    t, d = x.shape

    def body(x_ref, w_ref, o_ref, ot_ref):
        xv = x_ref[...]
        rstd = lax.rsqrt(jnp.mean(xv * xv, axis=-1, keepdims=True) + RMS_EPS)
        yv = xv * rstd * w_ref[...]
        o_ref[...] = yv.astype(BF16)
        ot_ref[...] = jnp.transpose(yv).astype(BF16)

    return pl.pallas_call(
        body, name=name, grid=(t // tm,), in_specs=[_row(tm, d), _fixed((1, d))],
        out_specs=[_row(tm, d), _col(tm, d)], out_shape=[_sds((t, d), BF16), _sds((d, t), BF16)],
        compiler_params=_cp("parallel"))(x, w)


def _rms_bwd_math(xv, wv, dyv):
    rstd = lax.rsqrt(jnp.mean(xv * xv, axis=-1, keepdims=True) + RMS_EPS)
    xhat = xv * rstd
    gv = dyv * wv
    dx = rstd * (gv - xhat * jnp.mean(gv * xhat, axis=-1, keepdims=True))
    return dx, jnp.sum(dyv * xhat, axis=0, keepdims=True)


def _rmsnorm_bwd(x, w, dy, dres, tm, name):
    t, d = x.shape

    def body(x_ref, w_ref, dy_ref, dres_ref, dx_ref, dw_ref):
        dx, dw = _rms_bwd_math(x_ref[...], w_ref[...], dy_ref[...])
        dx_ref[...] = dres_ref[...] + dx

        @pl.when(pl.program_id(0) == 0)
        def _():
            dw_ref[...] = jnp.zeros_like(dw_ref)

        dw_ref[...] += dw

    return pl.pallas_call(
        body, name=name, grid=(t // tm,),
        in_specs=[_row(tm, d), _fixed((1, d)), _row(tm, d), _row(tm, d)],
        out_specs=[_row(tm, d), _fixed((1, d))],
        out_shape=[_sds((t, d)), _sds((1, d))], compiler_params=_cp("arbitrary"))(x, w, dy, dres)


def _loss_head(x, w, target, tm):
    t, d = x.shape

    def body(x_ref, w_ref, t_ref, dx_ref, dw_ref, loss_ref):
        xv, wv = x_ref[...], w_ref[...]
        rstd = lax.rsqrt(jnp.mean(xv * xv, axis=-1, keepdims=True) + RMS_EPS)
        err = xv * rstd * wv - t_ref[...]
        dx, dw = _rms_bwd_math(xv, wv, err * (1.0 / d))
        dx_ref[...] = dx

        @pl.when(pl.program_id(0) == 0)
        def _():
            dw_ref[...] = jnp.zeros_like(dw_ref)
            loss_ref[...] = jnp.zeros_like(loss_ref)

        dw_ref[...] += dw
        loss_ref[...] += 0.5 * jnp.sum(jnp.mean(err * err, axis=-1, keepdims=True), axis=0, keepdims=True)

    return pl.pallas_call(
        body, name="loss_head", grid=(t // tm,),
        in_specs=[_row(tm, d), _fixed((1, d)), _row(tm, d)],
        out_specs=[_row(tm, d), _fixed((1, d)), _fixed((1, 1))],
        out_shape=[_sds((t, d)), _sds((1, d)), _sds((1, 1))], compiler_params=_cp("arbitrary"))(x, w, target)


def _ffn_in(h, w_gate, w_up):
    t, d = h.shape
    f = w_gate.shape[1]
    tm, tn = _tile(t, 512, 8), _tile(f, 1536)

    def body(h_ref, wg_ref, wu_ref, g_ref, u_ref, a_ref, at_ref):
        hv = h_ref[...]
        gv = jnp.dot(hv, wg_ref[...], preferred_element_type=F32)
        uv = jnp.dot(hv, wu_ref[...], preferred_element_type=F32)
        act = gv * jax.nn.sigmoid(gv) * uv
        g_ref[...] = gv.astype(BF16)
        u_ref[...] = uv.astype(BF16)
        a_ref[...] = act.astype(BF16)
        at_ref[...] = jnp.transpose(act).astype(BF16)

    w_spec = pl.BlockSpec((d, tn), lambda i, j: (0, j))
    o_spec = pl.BlockSpec((tm, tn), lambda i, j: (i, j))
    return pl.pallas_call(
        body, name="ffn_in", grid=(t // tm, f // tn),
        in_specs=[pl.BlockSpec((tm, d), lambda i, j: (i, 0)), w_spec, w_spec],
        out_specs=[o_spec, o_spec, o_spec, pl.BlockSpec((tn, tm), lambda i, j: (j, i))],
        out_shape=[_sds((t, f), BF16)] * 3 + [_sds((f, t), BF16)],
        compiler_params=_cp("parallel", "parallel"))(h, w_gate, w_up)


def _ffn_in_bwd(d_out, w_down, gate, up):
    t, d = d_out.shape
    f = w_down.shape[0]
    tm, tn = _tile(t, 512, 8), _tile(f, 1536)

    def body(do_ref, w_ref, g_ref, u_ref, dg_ref, du_ref):
        dv = lax.dot_general(do_ref[...].astype(BF16), w_ref[...], (((1,), (1,)), ((), ())),
                             preferred_element_type=F32)
        gv, uv = g_ref[...].astype(F32), u_ref[...].astype(F32)
        sg = jax.nn.sigmoid(gv)
        du_ref[...] = (dv * gv * sg).astype(BF16)
        dg_ref[...] = (dv * uv * (sg * (1.0 + gv * (1.0 - sg)))).astype(BF16)

    tile = pl.BlockSpec((tm, tn), lambda i, j: (i, j))
    return pl.pallas_call(
        body, name="ffn_in_bwd", grid=(t // tm, f // tn),
        in_specs=[pl.BlockSpec((tm, d), lambda i, j: (i, 0)), pl.BlockSpec((tn, d), lambda i, j: (j, 0)), tile, tile],
        out_specs=[tile, tile], out_shape=[_sds((t, f), BF16)] * 2,
        compiler_params=_cp("parallel", "parallel"))(d_out, w_down, gate, up)


def _halo_specs(tm, width, rows_total):
    per = tm // GROUP
    last = rows_total // GROUP - 1
    prev = pl.BlockSpec((GROUP, width), lambda i: (jnp.maximum(i * per - 1, 0), 0))
    nxt = pl.BlockSpec((GROUP, width), lambda i: (jnp.minimum((i + 1) * per, last), 0))
    return prev, nxt


def _edge_flags(tm, seq):
    i = pl.program_id(0)
    has_prev = jnp.where((i * tm) % seq == 0, 0.0, 1.0).astype(F32)
    has_next = jnp.where(((i + 1) * tm) % seq == 0, 0.0, 1.0).astype(F32)
    return has_prev, has_next


def _shifted(xv, prev_row, next_row):
    tm = xv.shape[0]
    row = lax.broadcasted_iota(jnp.int32, xv.shape, 0)
    down = jnp.where(row == 0, prev_row, pltpu.roll(xv, 1, axis=0))
    up = jnp.where(row == tm - 1, next_row, pltpu.roll(xv, tm - 1, axis=0))
    return down, up


def _shift_fwd(p, mu, tm, seq):
    t, w = p.shape
    prev_spec, next_spec = _halo_specs(tm, w, t)

    def body(p_ref, hp_ref, hn_ref, mu_ref, o_ref):
        has_prev, has_next = _edge_flags(tm, seq)
        xv = p_ref[...]
        down, up = _shifted(xv, hp_ref[GROUP - 1:GROUP, :] * has_prev, hn_ref[0:1, :] * has_next)
        o_ref[...] = xv + mu_ref[0:1, :] * (0.5 * (down + up) - xv)

    return pl.pallas_call(
        body, name="shift_fwd", grid=(t // tm,),
        in_specs=[_row(tm, w), prev_spec, next_spec, _fixed((GROUP, w))], out_specs=_row(tm, w),
        out_shape=_sds((t, w)), compiler_params=_cp("parallel"))(p, p, p, mu)


def _shift_bwd(q, p, mu, tm, seq):
    t, w = p.shape
    prev_spec, next_spec = _halo_specs(tm, w, t)

    def body(q_ref, qp_ref, qn_ref, p_ref, pp_ref, pn_ref, mu_ref, dp_ref, dmu_ref):
        has_prev, has_next = _edge_flags(tm, seq)
        muv = mu_ref[0:1, :]
        qv = q_ref[...]
        mq = muv * qv
        mq_down, mq_up = _shifted(mq, muv * qp_ref[GROUP - 1:GROUP, :] * has_prev,
                                  muv * qn_ref[0:1, :] * has_next)
        dp_ref[...] = (qv - mq + 0.5 * (mq_down + mq_up)).astype(BF16)
        pv = p_ref[...]
        p_down, p_up = _shifted(pv, pp_ref[GROUP - 1:GROUP, :] * has_prev, pn_ref[0:1, :] * has_next)

        @pl.when(pl.program_id(0) == 0)
        def _():
            dmu_ref[...] = jnp.zeros_like(dmu_ref)

        dmu_ref[...] += jnp.sum(qv * (0.5 * (p_down + p_up) - pv), axis=0, keepdims=True)

    return pl.pallas_call(
        body, name="shift_bwd", grid=(t // tm,),
        in_specs=[_row(tm, w), prev_spec, next_spec, _row(tm, w), prev_spec, next_spec, _fixed((GROUP, w))],
        out_specs=[_row(tm, w), _fixed((1, w))],
        out_shape=[_sds((t, w), BF16), _sds((1, w))], compiler_params=_cp("arbitrary"))(q, q, q, p, p, p, mu)


@jax.custom_vjp
def _bdot(a, b):
    return jnp.dot(a.astype(BF16), b.astype(BF16), preferred_element_type=F32)


def _bdot_fwd(a, b):
    return _bdot(a, b), (a, b)


def _bdot_bwd(res, g):
    a, b = res
    gb = g.astype(BF16)
    da = lax.dot_general(gb, b.astype(BF16), (((1,), (1,)), ((), ())), preferred_element_type=F32)
    db = lax.dot_general(a.astype(BF16), gb, (((0,), (0,)), ((), ())), preferred_element_type=F32)
    return da, db


_bdot.defvjp(_bdot_fwd, _bdot_bwd)


def _seg_raw(x, ones_blocks):
    hi = x.astype(BF16)
    lo = (x - hi.astype(F32)).astype(BF16)
    return (jnp.dot(hi, ones_blocks, preferred_element_type=F32)
            + jnp.dot(lo, ones_blocks, preferred_element_type=F32))


@jax.custom_vjp
def _seg(x, ones_blocks):
    return _seg_raw(x, ones_blocks)


def _seg_fwd(x, ones_blocks):
    return _seg_raw(x, ones_blocks), ones_blocks


def _seg_bwd(ones_blocks, g):
    return _seg_raw(g, ones_blocks), jnp.zeros_like(ones_blocks)


_seg.defvjp(_seg_fwd, _seg_bwd)


def _head_ones():
    h = jnp.arange(D_RWKV) // HEAD
    return (h[:, None] == h[None, :]).astype(BF16)


def _prep_math(ps, k_k, w0_f, a0_f, k_a_f, w0_b, a0_b, k_a_b, wup_f, aup_f, wup_b, aup_b, gup, ones_blocks):
    r = ps[:, 0:512]
    k = ps[:, 512:1024]
    v = ps[:, 1024:1536]
    xwa = ps[:, 1536:1664]
    xg = ps[:, 1664:D_SHIFT_PAD]
    kk_raw = k * k_k
    norm = jnp.sqrt(_seg(kk_raw * kk_raw, ones_blocks))
    kk = kk_raw / jnp.maximum(norm, NORM_EPS)
    t_xwa = jnp.tanh(xwa)
    outs = [r, v, kk]
    for w0, a0, k_a, wup, aup in ((w0_f, a0_f, k_a_f, wup_f, aup_f), (w0_b, a0_b, k_a_b, wup_b, aup_b)):
        decay = jnp.exp(-LOG_DECAY_SCALE * jax.nn.sigmoid(w0 + _bdot(t_xwa, wup)))
        rate = jax.nn.sigmoid(a0 + _bdot(xwa, aup))
        outs += [decay, k * (1.0 + (rate - 1.0) * k_a), kk * rate]
    outs.append(_bdot(jax.nn.sigmoid(xg), gup))
    return tuple(outs)


def _prep_args(tm, ps_ref, pv_ref, mat_refs, ones_ref):
    vecs = [jnp.broadcast_to(pv_ref[j:j + 1, :], (tm, D_RWKV)) for j in range(7)]
    return [ps_ref[...]] + vecs + [m[...] for m in mat_refs] + [ones_ref[...]]


_PREP_MAT_SHAPES = ((128, D_RWKV),) * 4 + ((D_GATE_PAD, D_RWKV),)


def _prep_fwd(ps, pvec, mats, ones_blocks, tm):
    t = ps.shape[0]

    def body(ps_ref, pv_ref, m0, m1, m2, m3, m4, ones_ref, *out_refs):
        outs = _prep_math(*_prep_args(tm, ps_ref, pv_ref, (m0, m1, m2, m3, m4), ones_ref))
        for o_ref, val in zip(out_refs, outs[2:]):
            o_ref[...] = val

    return pl.pallas_call(
        body, name="prep_fwd", grid=(t // tm,),
        in_specs=[_row(tm, D_SHIFT_PAD), _fixed((8, D_RWKV))] + [_fixed(s) for s in _PREP_MAT_SHAPES]
        + [_fixed((D_RWKV, D_RWKV))],
        out_specs=[_row(tm, D_RWKV)] * 8, out_shape=[_sds((t, D_RWKV))] * 8,
        compiler_params=_cp("parallel"))(ps, pvec, *mats, ones_blocks)


def _prep_bwd(ps, pvec, mats, ones_blocks, cts, tm):
    t = ps.shape[0]
    counts = [len(c) for c in cts]
    flat = [a for c in cts for a in c]

    def body(ps_ref, pv_ref, m0, m1, m2, m3, m4, ones_ref, *refs):
        ct_refs = refs[:len(flat)]
        q_ref, dpv_ref = refs[len(flat)], refs[len(flat) + 1]
        dmat_refs = refs[len(flat) + 2:]
        args = _prep_args(tm, ps_ref, pv_ref, (m0, m1, m2, m3, m4), ones_ref)
        _, vjp = jax.vjp(lambda *a: _prep_math(*a, args[-1]), *args[:-1])
        ct_vals, pos = [], 0
        for n in counts:
            val = ct_refs[pos][...]
            for extra in ct_refs[pos + 1:pos + n]:
                val = val + extra[...]
            ct_vals.append(val)
            pos += n
        grads = vjp(tuple(ct_vals))
        q_ref[...] = grads[0]

        @pl.when(pl.program_id(0) == 0)
        def _():
            dpv_ref[...] = jnp.zeros_like(dpv_ref)
            for d_ref in dmat_refs:
                d_ref[...] = jnp.zeros_like(d_ref)

        for j in range(7):
            dpv_ref[j:j + 1, :] += jnp.sum(grads[1 + j], axis=0, keepdims=True)
        for d_ref, gm in zip(dmat_refs, grads[8:13]):
            d_ref[...] += gm

    return pl.pallas_call(
        body, name="prep_bwd", grid=(t // tm,),
        in_specs=[_row(tm, D_SHIFT_PAD), _fixed((8, D_RWKV))] + [_fixed(s) for s in _PREP_MAT_SHAPES]
        + [_fixed((D_RWKV, D_RWKV))] + [_row(tm, D_RWKV)] * len(flat),
        out_specs=[_row(tm, D_SHIFT_PAD), _fixed((8, D_RWKV))] + [_fixed(s) for s in _PREP_MAT_SHAPES],
        out_shape=[_sds((t, D_SHIFT_PAD)), _sds((8, D_RWKV))] + [_sds(s) for s in _PREP_MAT_SHAPES],
        compiler_params=_cp("arbitrary"))(ps, pvec, *mats, ones_blocks, *flat)


def _pair_ones():
    h = jnp.arange(2 * HEAD) // HEAD
    block = (h[:, None] == h[None, :]).astype(BF16)
    return jnp.concatenate([block, block], axis=0)


def _diag_mask():
    lane = lax.broadcasted_iota(jnp.int32, (HEAD, 2 * HEAD), 1)
    sub = lax.broadcasted_iota(jnp.int32, (HEAD, 2 * HEAD), 0)
    return jnp.where((lane & (HEAD - 1)) == sub, 1.0, 0.0).astype(F32)


def _to_row(cols, dmask):
    return jnp.sum(cols * dmask, axis=0, keepdims=True)


def _seg_many(exact, rounded, ones_pair):
    out_exact, out_rounded = [], []
    if exact:
        parts = []
        for x in exact:
            hi = x.astype(BF16)
            parts.append(jnp.concatenate([hi, (x - hi.astype(F32)).astype(BF16)], axis=1))
        res = jnp.dot(jnp.concatenate(parts, axis=0), ones_pair, preferred_element_type=F32)
        out_exact = [res[HEAD * c:HEAD * (c + 1)] for c in range(len(exact))]
    if rounded:
        res = jnp.dot(jnp.concatenate([x.astype(BF16) for x in rounded], axis=0), ones_pair[0:2 * HEAD],
                      preferred_element_type=F32)
        out_rounded = [res[HEAD * c:HEAD * (c + 1)] for c in range(len(rounded))]
    return out_exact, out_rounded


N_CHAIN = 2 * N_PAIR


def _chain(c):
    d, p = divmod(c, N_PAIR)
    return d, slice(2 * HEAD * p, 2 * HEAD * (p + 1))


def _scan_specs(n_chunks, col_blocks, fwd_chunk, bwd_chunk):
    def spec(chunk_of, col):
        return pl.BlockSpec((SCAN_CHUNK, D_RWKV), lambda b, g: (b * n_chunks + chunk_of(g), col))
    return [spec(fwd_chunk, c) for c in col_blocks] + [spec(bwd_chunk, c) for c in col_blocks]


def _scan_fwd(ps, kk, dirs, batch, seq):
    t = batch * seq
    n = seq // SCAN_CHUNK
    groups = SCAN_CHUNK // GROUP
    up = lambda g: g
    down = lambda g: n - 1 - g
    col_blocks = (0, 2, 0, 0, 0, 0)

    def body(*refs):
        dir_refs = (refs[0:6], refs[6:12])
        ones_ref = refs[12]
        y_refs, hist_refs, st_ref = refs[13:15], refs[15:17], refs[17]

        @pl.when(pl.program_id(1) == 0)
        def _():
            st_ref[...] = jnp.zeros_like(st_ref)

        ones_pair = ones_ref[...]
        dmask = _diag_mask()
        dmask_b = dmask.astype(BF16)
        sub8 = lax.broadcasted_iota(jnp.int32, (GROUP, 2 * HEAD), 0)

        def group(gi, carry):
            off = (pl.multiple_of(gi * GROUP, GROUP), pl.multiple_of((groups - 1 - gi) * GROUP, GROUP))
            loaded = [tuple(ref[pl.ds(off[d], GROUP), :] for ref in dir_refs[d]) for d in range(2)]
            states = list(carry)
            y_acc = [jnp.zeros((GROUP, 2 * HEAD), F32) for _ in range(N_CHAIN)]
            for step in range(GROUP):
                rows, idx = [], []
                for c in range(N_CHAIN):
                    d, lanes = _chain(c)
                    i = step if d == 0 else GROUP - 1 - step
                    idx.append(i)
                    rows.append(tuple(x8[i:i + 1, lanes] for x8 in loaded[d]))
                    hist_refs[d][c % N_PAIR, gi * GROUP + step] = states[c]
                _, v_cols = _seg_many([], [dmask_b * rows[c][1].astype(BF16) for c in range(N_CHAIN)], ones_pair)
                sas, _ = _seg_many([states[c] * rows[c][2] for c in range(N_CHAIN)], [], ones_pair)
                for c in range(N_CHAIN):
                    _, _, _, w_row, kd_row, b_row = rows[c]
                    states[c] = states[c] * w_row - sas[c] * b_row + v_cols[c] * kd_row
                _, ys = _seg_many([], [states[c] * rows[c][0] for c in range(N_CHAIN)], ones_pair)
                for c in range(N_CHAIN):
                    y_acc[c] = jnp.where(sub8 == idx[c], _to_row(ys[c], dmask), y_acc[c])
            for c in range(N_CHAIN):
                d, lanes = _chain(c)
                y_refs[d][pl.ds(off[d], GROUP), lanes] = y_acc[c]
            return tuple(states)

        final = lax.fori_loop(0, groups, group, tuple(st_ref[c] for c in range(N_CHAIN)))
        for c in range(N_CHAIN):
            st_ref[c] = final[c]
            hist_refs[c // N_PAIR][c % N_PAIR, SCAN_CHUNK] = final[c]

    y_spec_f = pl.BlockSpec((SCAN_CHUNK, D_RWKV), lambda b, g: (b * n + up(g), 0))
    y_spec_b = pl.BlockSpec((SCAN_CHUNK, D_RWKV), lambda b, g: (b * n + down(g), 0))
    hist_shape = (batch, n, N_PAIR, SCAN_CHUNK + 1, HEAD, 2 * HEAD)
    hist_block = (None, None, N_PAIR, SCAN_CHUNK + 1, HEAD, 2 * HEAD)
    hist_spec_f = pl.BlockSpec(hist_block, lambda b, g: (b, up(g), 0, 0, 0, 0))
    hist_spec_b = pl.BlockSpec(hist_block, lambda b, g: (b, down(g), 0, 0, 0, 0))
    ones_spec = pl.BlockSpec((4 * HEAD, 2 * HEAD), lambda b, g: (0, 0))
    (wf, kdf, bf), (wb, kdb, bb) = dirs
    return pl.pallas_call(
        body, name="wkv_fwd", grid=(batch, n),
        in_specs=_scan_specs(n, col_blocks, up, down) + [ones_spec],
        out_specs=[y_spec_f, y_spec_b, hist_spec_f, hist_spec_b],
        out_shape=[_sds((t, D_RWKV)), _sds((t, D_RWKV)), _sds(hist_shape), _sds(hist_shape)],
        scratch_shapes=[pltpu.VMEM((N_CHAIN, HEAD, 2 * HEAD), F32)],
        compiler_params=_cp("parallel", "arbitrary"),
    )(ps, ps, kk, wf, kdf, bf, ps, ps, kk, wb, kdb, bb, _pair_ones())


def _scan_bwd(ps, kk, dirs, dy, hist_f, hist_b, batch, seq):
    t = batch * seq
    n = seq // SCAN_CHUNK
    groups = SCAN_CHUNK // GROUP
    fwd_chunk = lambda g: n - 1 - g
    bwd_chunk = lambda g: g
    col_blocks = (0, 2, 0, 0, 0, 0, 0)

    def undo_group(dir_refs, out_refs, hist_refs, gi, d_states, ones_pair, dmask, sub8):
        d_states = list(d_states)
        loaded, blocks = [], []
        for d in range(2):
            blk = groups - 1 - gi if d == 0 else gi
            blocks.append(pl.ds(pl.multiple_of(blk * GROUP, GROUP), GROUP))
            r8, v8, kk8, w8, kd8, b8, dy8 = (ref[blocks[d], :] for ref in dir_refs[d])
            loaded.append((r8, v8, kk8, w8, kd8, -b8, dy8))
        acc = [[jnp.zeros((GROUP, 2 * HEAD), F32) for _ in range(6)] for _ in range(N_CHAIN)]
        for step in range(GROUP):
            rows, idx, before, after = [], [], [], []
            for c in range(N_CHAIN):
                d, lanes = _chain(c)
                i = GROUP - 1 - step if d == 0 else step
                q = (groups - 1 - gi) * GROUP + i if d == 0 else SCAN_CHUNK - 1 - (gi * GROUP + i)
                idx.append(i)
                rows.append(tuple(x8[i:i + 1, lanes] for x8 in loaded[d]))
                before.append(hist_refs[d][c % N_PAIR, q])
                after.append(hist_refs[d][c % N_PAIR, q + 1])
            _, cols = _seg_many([], [dmask.astype(BF16) * rows[c][j].astype(BF16) for c in range(N_CHAIN) for j in (1, 6)],
                                ones_pair)
            v_cols, dy_cols = cols[0::2], cols[1::2]
            d_now = [d_states[c] + dy_cols[c] * rows[c][0] for c in range(N_CHAIN)]
            d_sas, _ = _seg_many([d_now[c] * rows[c][5] for c in range(N_CHAIN)], [], ones_pair)
            _, others = _seg_many(
                [], [x for c in range(N_CHAIN) for x in (before[c] * rows[c][2], d_now[c] * rows[c][4])], ones_pair)
            for c in range(N_CHAIN):
                sa, d_sa, dv_cols = others[2 * c], d_sas[c], others[2 * c + 1]
                rows_out = (
                    jnp.sum(after[c] * dy_cols[c], axis=0, keepdims=True),
                    jnp.sum(d_now[c] * before[c], axis=0, keepdims=True),
                    jnp.sum(d_now[c] * v_cols[c], axis=0, keepdims=True),
                    _to_row(dv_cols, dmask),
                    jnp.sum(before[c] * d_sa, axis=0, keepdims=True),
                    -jnp.sum(d_now[c] * sa, axis=0, keepdims=True),
                )
                acc[c] = [jnp.where(sub8 == idx[c], val, a) for val, a in zip(rows_out, acc[c])]
                d_states[c] = d_now[c] * rows[c][3] + d_sa * rows[c][2]
        for c in range(N_CHAIN):
            d, lanes = _chain(c)
            for o_ref, val in zip(out_refs[d], acc[c]):
                o_ref[blocks[d], lanes] = val
        return tuple(d_states)

    def body(*refs):
        dir_refs = (refs[0:7], refs[7:14])
        hist_refs, ones_ref = refs[14:16], refs[16]
        out_refs = (refs[17:23], refs[23:29])
        dst_ref = refs[29]

        @pl.when(pl.program_id(1) == 0)
        def _():
            dst_ref[...] = jnp.zeros_like(dst_ref)

        ones_pair = ones_ref[...]
        dmask = _diag_mask()
        sub8 = lax.broadcasted_iota(jnp.int32, (GROUP, 2 * HEAD), 0)

        def group(gi, carry):
            return undo_group(dir_refs, out_refs, hist_refs, gi, carry, ones_pair, dmask, sub8)

        final = lax.fori_loop(0, groups, group, tuple(dst_ref[c] for c in range(N_CHAIN)))
        for c in range(N_CHAIN):
            dst_ref[c] = final[c]

    blk = (SCAN_CHUNK, D_RWKV)
    out_f = pl.BlockSpec(blk, lambda b, g: (b * n + fwd_chunk(g), 0))
    out_b = pl.BlockSpec(blk, lambda b, g: (b * n + bwd_chunk(g), 0))
    hist_block = (None, None, N_PAIR, SCAN_CHUNK + 1, HEAD, 2 * HEAD)
    hist_spec_f = pl.BlockSpec(hist_block, lambda b, g: (b, fwd_chunk(g), 0, 0, 0, 0))
    hist_spec_b = pl.BlockSpec(hist_block, lambda b, g: (b, bwd_chunk(g), 0, 0, 0, 0))
    ones_spec = pl.BlockSpec((4 * HEAD, 2 * HEAD), lambda b, g: (0, 0))
    (wf, kdf, bf), (wb, kdb, bb) = dirs
    outs = pl.pallas_call(
        body, name="wkv_bwd", grid=(batch, n),
        in_specs=_scan_specs(n, col_blocks, fwd_chunk, bwd_chunk) + [hist_spec_f, hist_spec_b, ones_spec],
        out_specs=[out_f] * 6 + [out_b] * 6,
        out_shape=[_sds((t, D_RWKV))] * 12,
        scratch_shapes=[pltpu.VMEM((N_CHAIN, HEAD, 2 * HEAD), F32)],
        compiler_params=_cp("parallel", "arbitrary"),
    )(ps, ps, kk, wf, kdf, bf, dy, ps, ps, kk, wb, kdb, bb, dy, hist_f, hist_b, _pair_ones())
    return outs[0:6], outs[6:12]


def _post_math(y, r, kd_f, kd_b, v, gate, gn_w, gn_b, rk_f, rk_b, ones_blocks):
    mean = _seg(y, ones_blocks) * (1.0 / HEAD)
    yc = y - mean
    var = _seg(yc * yc, ones_blocks) * (1.0 / HEAD)
    yn = yc * lax.rsqrt(var + GN_EPS) * gn_w + gn_b
    bonus = _seg(r * kd_f * rk_f, ones_blocks) * v + _seg(r * kd_b * rk_b, ones_blocks) * v
    return (yn + bonus) * gate


def _conv_parts(pc, halo_prev, halo_next, has_prev, has_next):
    gate_b, gate_c, hid = pc[:, 0:512], pc[:, 512:1024], pc[:, 1024:1536]
    u = gate_c * hid
    u_prev_row = halo_prev[GROUP - 1:GROUP, 512:1024] * halo_prev[GROUP - 1:GROUP, 1024:1536] * has_prev
    u_next_row = halo_next[0:1, 512:1024] * halo_next[0:1, 1024:1536] * has_next
    u_down, u_up = _shifted(u, u_prev_row, u_next_row)
    return gate_b, gate_c, hid, u, u_down, u_up


def _post_specs(tm, t):
    pc_prev, pc_next = _halo_specs(tm, D_CONV3, t)
    col = lambda c: pl.BlockSpec((tm, D_RWKV), lambda i: (i, c))
    return ([col(0), col(0), col(0), col(0), col(0), col(2), col(0), _row(tm, D_CONV3), pc_prev, pc_next,
             _fixed((8, D_RWKV)), _fixed((D_RWKV, D_RWKV))])


def _post_fwd(y_f, y_b, ps, kd_f, kd_b, gate, pc, qvec, ones_blocks, tm, seq):
    t = ps.shape[0]

    def body(yf_ref, yb_ref, r_ref, kdf_ref, kdb_ref, v_ref, g_ref, pc_ref, hp_ref, hn_ref, qv_ref, ones_ref,
             o_ref, ot_ref):
        has_prev, has_next = _edge_flags(tm, seq)
        vec = [jnp.broadcast_to(qv_ref[j:j + 1, :], (tm, D_RWKV)) for j in range(7)]
        o_rwkv = _post_math(yf_ref[...] + yb_ref[...], r_ref[...], kdf_ref[...], kdb_ref[...], v_ref[...],
                            g_ref[...], vec[0], vec[1], vec[2], vec[3], ones_ref[...])
        gate_b, _, _, u, u_down, u_up = _conv_parts(pc_ref[...], hp_ref[...], hn_ref[...], has_prev, has_next)
        o_conv = gate_b * (vec[4] * u_down + vec[5] * u + vec[6] * u_up)
        for half, val in enumerate((o_rwkv, o_conv)):
            o_ref[:, D_RWKV * half:D_RWKV * (half + 1)] = val.astype(BF16)
            ot_ref[D_RWKV * half:D_RWKV * (half + 1), :] = jnp.transpose(val).astype(BF16)

    return pl.pallas_call(
        body, name="post_fwd", grid=(t // tm,), in_specs=_post_specs(tm, t),
        out_specs=[_row(tm, D_MODEL), _col(tm, D_MODEL)],
        out_shape=[_sds((t, D_MODEL), BF16), _sds((D_MODEL, t), BF16)], compiler_params=_cp("parallel"),
    )(y_f, y_b, ps, kd_f, kd_b, ps, gate, pc, pc, pc, qvec, ones_blocks)


def _post_bwd(d_out, y_f, y_b, ps, kd_f, kd_b, gate, pc, qvec, ones_blocks, tm, seq):
    t = ps.shape[0]
    do_prev, do_next = _halo_specs(tm, D_MODEL, t)

    def body(do_ref, dop_ref, don_ref, yf_ref, yb_ref, r_ref, kdf_ref, kdb_ref, v_ref, g_ref, pc_ref, hp_ref,
             hn_ref, qv_ref, ones_ref, dy_ref, dr_ref, dkdf_ref, dkdb_ref, dv_ref, dg_ref, dpc_ref, dqv_ref):
        has_prev, has_next = _edge_flags(tm, seq)
        vec = [jnp.broadcast_to(qv_ref[j:j + 1, :], (tm, D_RWKV)) for j in range(7)]
        ones_v = ones_ref[...]
        args = (yf_ref[...] + yb_ref[...], r_ref[...], kdf_ref[...], kdb_ref[...], v_ref[...], g_ref[...],
                vec[0], vec[1], vec[2], vec[3])
        _, vjp = jax.vjp(lambda *a: _post_math(*a, ones_v), *args)
        grads = vjp(do_ref[:, 0:D_RWKV])
        for o_ref, gval in zip((dy_ref, dr_ref, dkdf_ref, dkdb_ref, dv_ref, dg_ref), grads[0:6]):
            o_ref[...] = gval

        hp, hn = hp_ref[...], hn_ref[...]
        gate_b, gate_c, hid, u, u_down, u_up = _conv_parts(pc_ref[...], hp, hn, has_prev, has_next)
        d_oc = do_ref[:, D_RWKV:2 * D_RWKV]
        d_cu = d_oc * gate_b
        d_cu_prev = dop_ref[GROUP - 1:GROUP, D_RWKV:2 * D_RWKV] * hp[GROUP - 1:GROUP, 0:512] * has_prev
        d_cu_next = don_ref[0:1, D_RWKV:2 * D_RWKV] * hn[0:1, 0:512] * has_next
        d_cu_down, d_cu_up = _shifted(d_cu, d_cu_prev, d_cu_next)
        d_u = vec[5] * d_cu + vec[4] * d_cu_up + vec[6] * d_cu_down
        dpc_ref[:, 0:512] = (d_oc * (vec[4] * u_down + vec[5] * u + vec[6] * u_up)).astype(BF16)
        dpc_ref[:, 512:1024] = (d_u * hid).astype(BF16)
        dpc_ref[:, 1024:1536] = (d_u * gate_c).astype(BF16)

        @pl.when(pl.program_id(0) == 0)
        def _():
            dqv_ref[...] = jnp.zeros_like(dqv_ref)

        vec_grads = list(grads[6:10]) + [d_cu * u_down, d_cu * u, d_cu * u_up]
        for j, gval in enumerate(vec_grads):
            dqv_ref[j:j + 1, :] += jnp.sum(gval, axis=0, keepdims=True)

    return pl.pallas_call(
        body, name="post_bwd", grid=(t // tm,),
        in_specs=[_row(tm, D_MODEL), do_prev, do_next] + _post_specs(tm, t),
        out_specs=[_row(tm, D_RWKV)] * 6 + [_row(tm, D_CONV3), _fixed((8, D_RWKV))],
        out_shape=[_sds((t, D_RWKV))] * 6 + [_sds((t, D_CONV3), BF16), _sds((8, D_RWKV))],
        compiler_params=_cp("arbitrary"),
    )(d_out, d_out, d_out, y_f, y_b, ps, kd_f, kd_b, ps, gate, pc, pc, pc, qvec, ones_blocks)


def _adamw_math(wv, gv, mv, vv):
    m2 = ADAM_B1 * mv + (1.0 - ADAM_B1) * gv
    v2 = ADAM_B2 * vv + (1.0 - ADAM_B2) * (gv * gv)
    m_hat = m2 / (1.0 - ADAM_B1 ** ADAM_STEP)
    v_hat = v2 / (1.0 - ADAM_B2 ** ADAM_STEP)
    return -ADAM_LR * (m_hat / (jnp.sqrt(v_hat) + ADAM_EPS) + ADAM_WD * wv), m2, v2


def _adamw_small(items):
    n = len(items)

    def body(*refs):
        ins, outs = refs[:4 * n], refs[4 * n:]
        for k in range(n):
            w_ref, g_ref, m_ref, v_ref = ins[4 * k:4 * k + 4]
            for o_ref, val in zip(outs[3 * k:3 * k + 3], _adamw_math(w_ref[...], g_ref[...], m_ref[...], v_ref[...])):
                o_ref[...] = val

    flat = [a for item in items for a in item]
    outs = pl.pallas_call(
        body, name="adamw_small", out_shape=[_sds(item[0].shape) for item in items for _ in range(3)],
        compiler_params=_cp())(*flat)
    return [tuple(outs[3 * k:3 * k + 3]) for k in range(n)]


def _adamw(w, g, m, v, name):
    r, c = w.shape[-2:]
    tr = _tile(r, 256, 8)
    if w.ndim == 3:
        spec = pl.BlockSpec((None, tr, c), lambda i: (0, i, 0))
    else:
        spec = pl.BlockSpec((tr, c), lambda i: (i, 0))

    def body(w_ref, g_ref, m_ref, v_ref, d_ref, nm_ref, nv_ref):
        d_ref[...], nm_ref[...], nv_ref[...] = _adamw_math(w_ref[...], g_ref[...], m_ref[...], v_ref[...])

    return pl.pallas_call(
        body, name=name, grid=(r // tr,), in_specs=[spec] * 4, out_specs=[spec] * 3,
        out_shape=[_sds(w.shape)] * 3, compiler_params=_cp("parallel"))(w, g, m, v)


_ANY = pl.BlockSpec(memory_space=pl.ANY)


def _place():
    return lax.axis_index("x"), lax.axis_index("y"), lax.axis_index("c")


def _other_chips(x, y):
    return [(1 - x, y), (x, 1 - y), (1 - x, 1 - y)]


def _remote(src, dst, send_sems, recv_sems, k, to):
    return pltpu.make_async_remote_copy(src_ref=src, dst_ref=dst, send_sem=send_sems.at[k],
                                        recv_sem=recv_sems.at[k], device_id=to, device_id_type=MESH)


def _gather_weights(pack):
    rows, width = pack.shape
    half = rows // 2

    def body(x_ref, out_ref, send_sems, recv_sems):
        x, y, c = _place()
        sibling = (x, y, 1 - c)
        chips = _other_chips(x, y)

        def block(chip, part):
            return out_ref.at[2 * chip[0] + chip[1], pl.ds(part * half, half), :]

        first = [_remote(x_ref.at[pl.ds(c * half, half), :], block((x, y), c), send_sems, recv_sems, j, (*chip, c))
                 for j, chip in enumerate(chips)]
        for cp in first:
            cp.start()
        passed = [_remote(block(chip, c), block(chip, c), send_sems, recv_sems, 3 + j, sibling)
                  for j, chip in enumerate(chips)]
        for j, chip in enumerate(chips):
            _remote(block(chip, c), block(chip, c), send_sems, recv_sems, j, sibling).wait_recv()
            passed[j].start()
        for j, chip in enumerate(chips):
            _remote(block(chip, 1 - c), block(chip, 1 - c), send_sems, recv_sems, 3 + j, sibling).wait_recv()
        for cp in first + passed:
            cp.wait_send()

    return pl.pallas_call(
        body, name="gather_weights", in_specs=[_ANY], out_specs=_ANY,
        out_shape=_sds((N_SHARD, rows, width), pack.dtype),
        scratch_shapes=[pltpu.SemaphoreType.DMA((6,)), pltpu.SemaphoreType.DMA((6,))],
    )(pack)


_HBM = pl.BlockSpec(memory_space=pltpu.HBM)
_SEMS = pl.BlockSpec(memory_space=pltpu.SEMAPHORE)
_DATAFLOW = pltpu.SideEffectType.DATAFLOW_SIDE_EFFECTING


def _fetch_start(pack):
    def body(x_ref, land_ref, send_sems, recv_sems, x_thru, land_thru, token):
        x, y, c = _place()
        for j, chip in enumerate(_other_chips(x, y)):
            _remote(x_ref, land_ref.at[2 * x + y], send_sems, recv_sems, j, (*chip, c)).start()
        token[...] = jnp.zeros_like(token)

    land = lax.empty((N_SHARD,) + pack.shape, pack.dtype)
    return pl.pallas_call(
        body, name="fetch_ffn_start",
        out_shape=(pltpu.SemaphoreType.DMA((3,)), pltpu.SemaphoreType.DMA((3,)), pltpu.HBM(pack.shape, pack.dtype),
                   pltpu.HBM(land.shape, land.dtype), _sds((8, 128))),
        in_specs=(_HBM, _HBM), out_specs=(_SEMS, _SEMS, _HBM, _HBM, pl.BlockSpec(memory_space=pltpu.VMEM)),
        input_output_aliases={0: 2, 1: 3}, compiler_params=pltpu.CompilerParams(has_side_effects=_DATAFLOW),
    )(pltpu.with_memory_space_constraint(pack, pltpu.HBM), pltpu.with_memory_space_constraint(land, pltpu.HBM))


def _fetch_wait(send_sems, recv_sems, pack_thru, land_thru, after):
    def body(x_ref, land_ref, send_sems, recv_sems, after_ref, x_dead, got_ref):
        x, y, c = _place()
        for j, chip in enumerate(_other_chips(x, y)):
            cp = _remote(x_ref, land_ref.at[2 * chip[0] + chip[1]], send_sems, recv_sems, j, (*chip, c))
            cp.wait_send()
            cp.wait_recv()

    return pl.pallas_call(
        body, name="fetch_ffn_wait",
        out_shape=(pltpu.HBM(pack_thru.shape, pack_thru.dtype), pltpu.HBM(land_thru.shape, land_thru.dtype)),
        in_specs=(_HBM, _HBM, _SEMS, _SEMS, _ANY), out_specs=(_HBM, _HBM), input_output_aliases={0: 0, 1: 1},
        compiler_params=pltpu.CompilerParams(has_side_effects=_DATAFLOW),
    )(pack_thru, land_thru, send_sems, recv_sems, after)[1]


def _swap_with_sibling(block, name):
    def body(x_ref, out_ref, send_sems, recv_sems):
        x, y, c = _place()
        cp = _remote(x_ref, out_ref, send_sems, recv_sems, 0, (x, y, 1 - c))
        cp.start()
        cp.wait()

    return pl.pallas_call(
        body, name=name, in_specs=[_ANY], out_specs=_ANY, out_shape=_sds(block.shape, block.dtype),
        scratch_shapes=[pltpu.SemaphoreType.DMA((1,)), pltpu.SemaphoreType.DMA((1,))],
    )(block)


def _swap_other_half(packed):
    slots, rows, width = packed.shape
    half = rows // 2

    def body(x_ref, out_ref, send_sems, recv_sems):
        x, y, c = _place()
        cp = _remote(x_ref.at[:, pl.ds((1 - c) * half, half), :], out_ref, send_sems, recv_sems, 0, (x, y, 1 - c))
        cp.start()
        cp.wait()

    return pl.pallas_call(
        body, name="swap_halves", in_specs=[_ANY], out_specs=_ANY, out_shape=_sds((slots, half, width)),
        scratch_shapes=[pltpu.SemaphoreType.DMA((1,)), pltpu.SemaphoreType.DMA((1,))],
    )(packed)


def _add_halves(packed, got, c):
    slots, rows, width = packed.shape
    half = rows // 2
    tr = _tile(half, 408, 16)
    per = half // tr
    block = (None, tr, width)

    def body(c_ref, mine_ref, got_ref, sum_ref, sum16_ref):
        acc = mine_ref[...] + got_ref[...]
        sum_ref[...] = acc
        sum16_ref[...] = acc.astype(BF16)

    plain = pl.BlockSpec(block, lambda s, i, c_ref: (s, i, 0))
    grid_spec = pltpu.PrefetchScalarGridSpec(
        num_scalar_prefetch=1, grid=(slots, per),
        in_specs=[pl.BlockSpec(block, lambda s, i, c_ref: (s, c_ref[0] * per + i, 0)), plain],
        out_specs=[plain, plain])
    return pl.pallas_call(
        body, name="add_halves", grid_spec=grid_spec,
        out_shape=[_sds((slots, half, width)), _sds((slots, half, width), BF16)],
        compiler_params=_cp("parallel", "parallel"))(c.reshape(1).astype(jnp.int32), packed, got)


def _add_quarters(chip_sum, others, chip):
    _, rows, width = chip_sum.shape
    tr = _tile(rows, 408, 16)

    def body(chip_ref, own_ref, others_ref, o_ref):
        acc = own_ref[...]
        for j in range(3):
            acc = acc + others_ref[j].astype(F32)
        o_ref[...] = acc

    grid_spec = pltpu.PrefetchScalarGridSpec(
        num_scalar_prefetch=1, grid=(rows // tr,),
        in_specs=[pl.BlockSpec((None, tr, width), lambda i, chip_ref: (chip_ref[0], i, 0)),
                  pl.BlockSpec((3, tr, width), lambda i, chip_ref: (0, i, 0))],
        out_specs=pl.BlockSpec((tr, width), lambda i, chip_ref: (i, 0)))
    return pl.pallas_call(
        body, name="add_quarters", grid_spec=grid_spec, out_shape=_sds((rows, width)),
        compiler_params=_cp("parallel"))(chip.reshape(1).astype(jnp.int32), chip_sum, others)


def _exchange_quarters(parts):
    _, rows, width = parts.shape

    def body(x_ref, out_ref, send_sems, recv_sems):
        x, y, c = _place()
        copies = [_remote(x_ref.at[2 * chip[0] + chip[1]], out_ref.at[j], send_sems, recv_sems, j, (*chip, c))
                  for j, chip in enumerate(_other_chips(x, y))]
        for cp in copies:
            cp.start()
        for cp in copies:
            cp.wait()

    return pl.pallas_call(
        body, name="exchange_quarters", in_specs=[_ANY], out_specs=_ANY,
        out_shape=_sds((3, rows, width), parts.dtype),
        scratch_shapes=[pltpu.SemaphoreType.DMA((3,)), pltpu.SemaphoreType.DMA((3,))],
    )(parts)


def _allreduce_small(vec):
    rows, width = vec.shape
    vmem = pl.BlockSpec(memory_space=pltpu.VMEM)

    def body(x_ref, o_ref, buf_ref, send_sems, recv_sems):
        x, y, c = _place()
        me = 4 * x + 2 * y + c
        buf_ref[me] = x_ref[...]
        copies = []
        for k in range(1, N_DEV):
            peer = (x ^ ((k >> 2) & 1), y ^ ((k >> 1) & 1), c ^ (k & 1))
            copies.append(_remote(x_ref, buf_ref.at[me], send_sems, recv_sems, k - 1, peer))
        for cp in copies:
            cp.start()
        for k in range(1, N_DEV):
            _remote(x_ref, buf_ref.at[me ^ k], send_sems, recv_sems, k - 1, (x, y, c)).wait_recv()
        for cp in copies:
            cp.wait_send()
        total = buf_ref[0]
        for d in range(1, N_DEV):
            total = total + buf_ref[d]
        o_ref[...] = total

    return pl.pallas_call(
        body, name="allreduce_small", in_specs=[vmem], out_specs=vmem, out_shape=_sds((rows, width)),
        scratch_shapes=[pltpu.VMEM((N_DEV, rows, width), F32), pltpu.SemaphoreType.DMA((N_DEV - 1,)),
                        pltpu.SemaphoreType.DMA((N_DEV - 1,))],
    )(vec)


def _rows1024(a):
    return a.reshape(-1, 1024)


def _pad_rows(a, rows):
    return jnp.concatenate([a, jnp.zeros((rows - a.shape[0], a.shape[1]), a.dtype)], axis=0)


_TRANSPOSED = ("w_in", "w_gate", "w_up")
_SMALL_SHARDED = ("w_up_f", "w_up_b", "a_up_f", "a_up_b", "g_up")
_BIG_SHARDED = ("w_in", "w_out", "w_gate", "w_up", "w_down")


def _pack_weight_shards(w):
    conv_bits = lax.bitcast_convert_type(w["conv_w"], BF16).reshape(1, -1)
    conv_row = jnp.concatenate([conv_bits, jnp.zeros((1, 1024 - conv_bits.shape[1]), BF16)], axis=1)

    def rows(name):
        a = w[name].astype(BF16)
        return a.T if name in _TRANSPOSED else _rows1024(a)

    early = _pad_rows(jnp.concatenate([rows(name) for name, _ in _EARLY_ROWS[:-1]] + [conv_row], axis=0), EARLY_R)
    return early, jnp.concatenate([rows(name) for name, _ in _FFN_ROWS], axis=0)


def _split_rows(gathered, layout):
    out, row = {}, 0
    for name, n in layout:
        out[name] = gathered[:, row:row + n]
        row += n
    return out


def _unpack_early(gathered):
    out = _split_rows(gathered, _EARLY_ROWS)
    cols = lambda a, k: jnp.concatenate([a[s].reshape(k, -1) for s in range(N_SHARD)], axis=1)
    conv = lax.bitcast_convert_type(out["conv_w"][:, 0, :768].reshape(N_SHARD, 3, 128, 2), F32)
    full = dict(w_in=out["w_in"].reshape(-1, 1024).T, w_out=out["w_out"].reshape(D_MODEL, D_MODEL),
                conv_w=jnp.concatenate([conv[s] for s in range(N_SHARD)], axis=1))
    full.update({name: cols(out[name], D_GATE if name == "g_up" else D_LORA) for name in _SMALL_SHARDED})
    return full


def _unpack_ffn(gathered):
    out = _split_rows(gathered, _FFN_ROWS)
    return dict(w_gate=out["w_gate"].reshape(-1, 1024).T, w_up=out["w_up"].reshape(-1, 1024).T,
                w_down=out["w_down"].reshape(D_FF, D_MODEL))


def _pack_grads(g):
    col_split = lambda a, s: a[:, s * (a.shape[1] // N_SHARD):(s + 1) * (a.shape[1] // N_SHARD)]
    row_split = lambda a, s: a[s * (a.shape[0] // N_SHARD):(s + 1) * (a.shape[0] // N_SHARD)]
    by_rows = {name: (g[name].T if name in _TRANSPOSED else g[name]) for name in _BIG_SHARDED}
    used = sum(n for _, n in _PACK_ROWS)
    parts = []
    for s in range(N_SHARD):
        conv = col_split(g["conv_w"], s).reshape(1, -1)
        parts += [row_split(by_rows[name], s) for name in _BIG_SHARDED]
        parts += [_rows1024(col_split(g[name], s)) for name in _SMALL_SHARDED]
        parts += [jnp.concatenate([conv, jnp.zeros((1, 1024 - conv.shape[1]), F32)], axis=1),
                  jnp.zeros((PACK_R - used, 1024), F32)]
    return jnp.concatenate(parts, axis=0).reshape(N_SHARD, PACK_R, 1024)


def _unpack_grad_shard(pack):
    small_shapes = {name: (D_GATE if name == "g_up" else D_LORA, 128) for name in _SMALL_SHARDED}
    out, row = {}, 0
    for name, n in _PACK_ROWS[:-1]:
        piece = pack[row:row + n]
        out[name] = piece.T if name in _TRANSPOSED else piece.reshape(small_shapes.get(name, piece.shape))
        row += n
    out["conv_w"] = pack[row, :384].reshape(3, 128)
    return out


_SMALL_LAYOUT = (("norm1_w", 1024), ("mu_shift", D_SHIFT), ("w0_f", 512), ("w0_b", 512), ("a0_f", 512),
                 ("a0_b", 512), ("k_k", 512), ("k_a_f", 512), ("k_a_b", 512), ("r_k_f", 512), ("r_k_b", 512),
                 ("gn_w", 512), ("gn_b", 512), ("norm2_w", 1024), ("norm_f_w", 1024), ("loss", 1))


def _pack_small(vals):
    rows = []
    for name, n in _SMALL_LAYOUT:
        flat = vals[name].reshape(-1)
        n_rows = -(-n // 1024)
        rows.append(jnp.concatenate([flat, jnp.zeros((n_rows * 1024 - n,), F32)]).reshape(n_rows, 1024))
    return _pad_rows(jnp.concatenate(rows, axis=0), SMALL_ROWS)


def _unpack_small(pack):
    out, row = {}, 0
    for name, n in _SMALL_LAYOUT:
        n_rows = -(-n // 1024)
        out[name] = pack[row:row + n_rows].reshape(-1)[:n]
        row += n_rows
    return out


_WEIGHTS = ("norm1_w", "w_in", "mu_shift", "w_up_f", "w0_f", "w_up_b", "w0_b", "a_up_f", "a0_f", "a_up_b", "a0_b",
            "g_up", "k_k", "k_a_f", "k_a_b", "r_k_f", "r_k_b", "gn_w", "gn_b", "conv_w", "w_out", "norm2_w",
            "w_gate", "w_up", "w_down", "norm_f_w")


def _train_step(x, loss_target, w, m, v):
    batch, seq, _ = x.shape
    t = batch * seq
    tm = _tile(seq, 256, 8)
    xs = x.reshape(t, D_MODEL)
    target = loss_target.reshape(t, D_MODEL)
    vec = lambda name: w[name].reshape(1, -1)

    local = {name: w[name][0] for name, _ in _PACK_ROWS}
    c = lax.axis_index("c")
    chip = 2 * lax.axis_index("x") + lax.axis_index("y")
    early, ffn_pack = _pack_weight_shards(local)
    early_all = lax.dynamic_update_slice(_gather_weights(early), early[None], (chip, 0, 0))
    early_all, ffn_pack = lax.optimization_barrier((early_all, ffn_pack))
    ffn_send, ffn_recv, ffn_pack, ffn_land, token = _fetch_start(ffn_pack)
    token, xs = lax.optimization_barrier((token, xs))
    full = _unpack_early(early_all)
    w_in = full["w_in"]
    w_shift = jnp.concatenate([w_in[:, :D_SHIFT], jnp.zeros((D_MODEL, D_SHIFT_PAD - D_SHIFT), BF16)], axis=1)
    w_conv = w_in[:, D_SHIFT:]
    zeros_lora = jnp.zeros((D_LORA, D_RWKV), F32)
    lora = lambda name: full[name].astype(F32)
    mats = (jnp.concatenate([lora("w_up_f"), zeros_lora]), jnp.concatenate([zeros_lora, lora("a_up_f")]),
            jnp.concatenate([lora("w_up_b"), zeros_lora]), jnp.concatenate([zeros_lora, lora("a_up_b")]),
            jnp.concatenate([lora("g_up"), jnp.zeros((D_GATE_PAD - D_GATE, D_RWKV), F32)]))
    mu = jnp.concatenate([vec("mu_shift"), jnp.zeros((1, D_SHIFT_PAD - D_SHIFT), F32)], axis=1)
    mu = jnp.broadcast_to(mu, (GROUP, D_SHIFT_PAD))
    zero_row = jnp.zeros((1, D_RWKV), F32)
    pvec = jnp.concatenate([vec("k_k"), vec("w0_f"), vec("a0_f"), vec("k_a_f"), vec("w0_b"), vec("a0_b"),
                            vec("k_a_b"), zero_row], axis=0)
    qvec = jnp.concatenate([vec("gn_w"), vec("gn_b"), vec("r_k_f"), vec("r_k_b"), full["conv_w"], zero_row], axis=0)
    ones_blocks = _head_ones()

    h1, h1_t = _rmsnorm_fwd(xs, vec("norm1_w"), tm, "norm1_fwd")
    p_shift = _matmul(h1, w_shift, mode="nn", name="in_proj_shift")
    pc = _matmul(h1, w_conv, mode="nn", name="in_proj_conv")
    ps = _shift_fwd(p_shift, mu, tm, seq)
    kk, w_f, kd_f, b_f, w_b, kd_b, b_b, gate = _prep_fwd(ps, pvec, mats, ones_blocks, tm)
    dirs = ((w_f, kd_f, b_f), (w_b, kd_b, b_b))
    y_f, y_b, hist_f, hist_b = _scan_fwd(ps, kk, dirs, batch, seq)
    mixed, mixed_t = _post_fwd(y_f, y_b, ps, kd_f, kd_b, gate, pc, qvec, ones_blocks, tm, seq)
    x1 = _matmul(mixed, full["w_out"], mode="nn", name="out_proj", add=xs)
    h2, h2_t = _rmsnorm_fwd(x1, vec("norm2_w"), tm, "norm2_fwd")
    ffn_all = _fetch_wait(ffn_send, ffn_recv, ffn_pack, ffn_land, h2)
    full.update(_unpack_ffn(lax.dynamic_update_slice(ffn_all, ffn_pack[None], (chip, 0, 0))))
    ff_gate, ff_up, act, act_t = _ffn_in(h2, full["w_gate"], full["w_up"])
    x2 = _matmul(act, full["w_down"], mode="nn", name="ffn_down", add=x1)
    d_x2, d_norm_f, loss_part = _loss_head(x2, w["norm_f_w"].reshape(1, -1), target, tm)

    g = {}
    g["w_down"] = _matmul(act_t, d_x2, mode="nn", name="ffn_down_dw")
    d_gate, d_up = _ffn_in_bwd(d_x2, full["w_down"], ff_gate, ff_up)
    d_h2 = _matmul(d_gate, full["w_gate"], mode="nt", name="ffn_gate_dx")
    d_h2 = _matmul(d_up, full["w_up"], mode="nt", name="ffn_up_dx", add=d_h2)
    g["w_gate"] = _matmul(h2_t, d_gate, mode="nn", name="ffn_gate_dw")
    g["w_up"] = _matmul(h2_t, d_up, mode="nn", name="ffn_up_dw")
    d_x1, d_norm2 = _rmsnorm_bwd(x1, vec("norm2_w"), d_h2, d_x2, tm, "norm2_bwd")
    d_mixed = _matmul(d_x1, full["w_out"], mode="nt", name="out_proj_dx")
    g["w_out"] = _matmul(mixed_t, d_x1, mode="nn", name="out_proj_dw")
    dy, dr_o, dkdf_o, dkdb_o, dv_o, d_gatev, d_pc, d_qvec = _post_bwd(
        d_mixed, y_f, y_b, ps, kd_f, kd_b, gate, pc, qvec, ones_blocks, tm, seq)
    (dr_f, dw_f, dkd_f, dv_f, dkk_f, db_f), (dr_b, dw_b, dkd_b, dv_b, dkk_b, db_b) = _scan_bwd(
        ps, kk, dirs, dy, hist_f, hist_b, batch, seq)
    cts = [[dr_f, dr_b, dr_o], [dv_f, dv_b, dv_o], [dkk_f, dkk_b], [dw_f], [dkd_f, dkdf_o], [db_f],
           [dw_b], [dkd_b, dkdb_o], [db_b], [d_gatev]]
    q, d_pvec, d_m0, d_m1, d_m2, d_m3, d_m4 = _prep_bwd(ps, pvec, mats, ones_blocks, cts, tm)
    d_pshift, d_mu = _shift_bwd(q, p_shift, mu, tm, seq)
    d_h1 = _matmul(d_pshift, w_shift, mode="nt", name="in_proj_shift_dx")
    d_h1 = _matmul(d_pc, w_conv, mode="nt", name="in_proj_conv_dx", add=d_h1)
    d_w_shift = _matmul(h1_t, d_pshift, mode="nn", name="in_proj_shift_dw")
    d_w_conv = _matmul(h1_t, d_pc, mode="nn", name="in_proj_conv_dw")
    g["w_in"] = jnp.concatenate([d_w_shift[:, :D_SHIFT], d_w_conv], axis=1)
    d_x, d_norm1 = _rmsnorm_bwd(xs, vec("norm1_w"), d_h1, d_x1, tm, "norm1_bwd")
    g["w_up_f"], g["a_up_f"] = d_m0[:D_LORA], d_m1[D_LORA:]
    g["w_up_b"], g["a_up_b"] = d_m2[:D_LORA], d_m3[D_LORA:]
    g["g_up"] = d_m4[:D_GATE]
    g["conv_w"] = d_qvec[4:7]

    packed = _pack_grads(g)
    chip_sum, chip_sum_bf16 = _add_halves(packed, _swap_other_half(packed), c)
    others = _exchange_quarters(chip_sum_bf16)
    eighth = _add_quarters(chip_sum, others, chip)
    other_eighth = _swap_with_sibling(eighth, "swap_eighths")
    grads = _unpack_grad_shard(jnp.concatenate([jnp.where(c == 0, eighth, other_eighth),
                                                jnp.where(c == 0, other_eighth, eighth)], axis=0))

    small = dict(norm1_w=d_norm1, mu_shift=d_mu[:, :D_SHIFT], w0_f=d_pvec[1], w0_b=d_pvec[4], a0_f=d_pvec[2],
                 a0_b=d_pvec[5], k_k=d_pvec[0], k_a_f=d_pvec[3], k_a_b=d_pvec[6], r_k_f=d_qvec[2], r_k_b=d_qvec[3],
                 gn_w=d_qvec[0], gn_b=d_qvec[1], norm2_w=d_norm2, norm_f_w=d_norm_f, loss=loss_part)
    reduced = _unpack_small(_allreduce_small(_pack_small(small)))
    loss = reduced.pop("loss")[0]
    grads.update(reduced)

    outs = {}
    small = [name for name in _WEIGHTS if name not in _BIG_SHARDED]
    as2d = lambda name: (1, w[name].shape[0]) if w[name].ndim == 1 else w[name].shape
    operands = lambda name: tuple(a.reshape(as2d(name)) for a in (w[name], grads[name], m[name], v[name]))
    updates = dict(zip(small, _adamw_small([operands(name) for name in small])))
    for name in _BIG_SHARDED:
        updates[name] = _adamw(*operands(name), "adamw_" + name)
    for name in _WEIGHTS:
        shape = w[name].shape
        outs[name] = (grads[name].reshape(shape),) + tuple(a.reshape(shape) for a in updates[name])
    d_x = d_x.reshape(batch, seq, D_MODEL)
    return (loss, d_x) + tuple(outs[name][k] for k in range(4) for name in _WEIGHTS)


def kernel(x, norm1_w, w_in, mu_shift, w_up_f, w0_f, w_up_b, w0_b, a_up_f, a0_f, a_up_b, a0_b, g_up, k_k, k_a_f, k_a_b, r_k_f, r_k_b, gn_w, gn_b, conv_w, w_out, norm2_w, w_gate, w_up, w_down, norm_f_w, loss_target, m_norm1_w, m_w_in, m_mu_shift, m_w_up_f, m_w0_f, m_w_up_b, m_w0_b, m_a_up_f, m_a0_f, m_a_up_b, m_a0_b, m_g_up, m_k_k, m_k_a_f, m_k_a_b, m_r_k_f, m_r_k_b, m_gn_w, m_gn_b, m_conv_w, m_w_out, m_norm2_w, m_w_gate, m_w_up, m_w_down, m_norm_f_w, v_norm1_w, v_w_in, v_mu_shift, v_w_up_f, v_w0_f, v_w_up_b, v_w0_b, v_a_up_f, v_a0_f, v_a_up_b, v_a0_b, v_g_up, v_k_k, v_k_a_f, v_k_a_b, v_r_k_f, v_r_k_b, v_gn_w, v_gn_b, v_conv_w, v_w_out, v_norm2_w, v_w_gate, v_w_up, v_w_down, v_norm_f_w):
    args = locals()
    w = {name: args[name] for name in _WEIGHTS}
    m = {name: args["m_" + name] for name in _WEIGHTS}
    v = {name: args["v_" + name] for name in _WEIGHTS}
    return _train_step(x, loss_target, w, m, v)
```

```python
import functools

import jax
import jax.numpy as jnp
from jax import lax
from jax.experimental import pallas as pl
from jax.experimental.pallas import tpu as pltpu

F32 = jnp.float32
BF16 = jnp.bfloat16
MESH = pl.DeviceIdType.MESH

D_MODEL = 1024
D_RWKV = 512
HEAD = 64
N_PAIR = D_RWKV // (2 * HEAD)
D_LORA = 64
D_GATE = 160
D_GATE_PAD = 384
D_FF = 2816
D_SHIFT = 1824
D_SHIFT_PAD = 2048
D_CONV3 = 1536
LOG_DECAY_SCALE = 0.606531
RMS_EPS = 1e-6
GN_EPS = 64e-5
NORM_EPS = 1e-12
ADAM_LR, ADAM_B1, ADAM_B2, ADAM_EPS, ADAM_WD, ADAM_STEP = 0.001, 0.9, 0.999, 1e-08, 0.01, 10

N_SHARD = 4
N_DEV = 8
V7X_VMEM_LIMIT = 48 * 1024 * 1024
SCAN_CHUNK = 32
GROUP = 8

_PACK_ROWS = (("w_in", 840), ("w_out", 256), ("w_gate", 704), ("w_up", 704), ("w_down", 704),
              ("w_up_f", 8), ("w_up_b", 8), ("a_up_f", 8), ("a_up_b", 8), ("g_up", 20), ("conv_w", 1))
PACK_R = 3264
PACK_H = PACK_R // 2
_FFN_NAMES = ("w_gate", "w_up", "w_down")
_EARLY_ROWS = tuple(item for item in _PACK_ROWS if item[0] not in _FFN_NAMES)
_FFN_ROWS = tuple(item for item in _PACK_ROWS if item[0] in _FFN_NAMES)
EARLY_R = 1152
SMALL_ROWS = 24


def _tile(n, cap, mult=128):
    best = None
    t = mult
    while t <= min(n, cap):
        if n % t == 0:
            best = t
        t += mult
    return best or n


def _cp(*sem):
    return pltpu.CompilerParams(dimension_semantics=sem or None, vmem_limit_bytes=V7X_VMEM_LIMIT)


def _sds(shape, dtype=F32):
    return jax.ShapeDtypeStruct(shape, dtype)


def _matmul(a, b, *, mode, name, out_dtype=F32, add=None):
    m, kdim = a.shape
    n = b.shape[1] if mode == "nn" else b.shape[0]
    tm, tn = _tile(m, 512, 8), _tile(n, 1536)
    tk = kdim if kdim <= 3584 else _tile(kdim, 1024)
    nk = kdim // tk
    a_spec = pl.BlockSpec((tm, tk), lambda i, j, k: (i, k))
    if mode == "nn":
        b_spec = pl.BlockSpec((tk, tn), lambda i, j, k: (k, j))
        dims = (((1,), (0,)), ((), ()))
    else:
        b_spec = pl.BlockSpec((tn, tk), lambda i, j, k: (j, k))
        dims = (((1,), (1,)), ((), ()))
    has_add = add is not None

    def body(*refs):
        a_ref, b_ref = refs[0], refs[1]
        add_ref = refs[2] if has_add else None
        o_ref = refs[3] if has_add else refs[2]
        part = lax.dot_general(a_ref[...].astype(BF16), b_ref[...].astype(BF16), dims,
                               preferred_element_type=F32)
        if nk == 1:
            if has_add:
                part = part + add_ref[...]
            o_ref[...] = part.astype(out_dtype)
        else:
            acc_ref = refs[-1]
            k = pl.program_id(2)

            @pl.when(k == 0)
            def _():
                acc_ref[...] = jnp.zeros_like(acc_ref)

            acc_ref[...] += part

            @pl.when(k == nk - 1)
            def _():
                res = acc_ref[...]
                if has_add:
                    res = res + add_ref[...]
                o_ref[...] = res.astype(out_dtype)

    o_spec = pl.BlockSpec((tm, tn), lambda i, j, k: (i, j))
    in_specs = [a_spec, b_spec] + ([o_spec] if has_add else [])
    args = (a, b) + ((add,) if has_add else ())
    return pl.pallas_call(
        body, name=name, grid=(m // tm, n // tn, nk), in_specs=in_specs, out_specs=o_spec,
        out_shape=_sds((m, n), out_dtype),
        scratch_shapes=[pltpu.VMEM((tm, tn), F32)] if nk > 1 else [],
        compiler_params=_cp("parallel", "parallel", "arbitrary"),
    )(*args)


def _row(tm, width):
    return pl.BlockSpec((tm, width), lambda i: (i, 0))


def _col(tm, height):
    return pl.BlockSpec((height, tm), lambda i: (0, i))


def _fixed(shape):
    return pl.BlockSpec(shape, lambda i: tuple(0 for _ in shape))


def _rmsnorm_fwd(x, w, tm, name):
    t, d = x.shape

    def body(x_ref, w_ref, o_ref, ot_ref):
        xv = x_ref[...]
        rstd = lax.rsqrt(jnp.mean(xv * xv, axis=-1, keepdims=True) + RMS_EPS)
        yv = xv * rstd * w_ref[...]
        o_ref[...] = yv.astype(BF16)
        ot_ref[...] = jnp.transpose(yv).astype(BF16)

    return pl.pallas_call(
        body, name=name, grid=(t // tm,), in_specs=[_row(tm, d), _fixed((1, d))],
        out_specs=[_row(tm, d), _col(tm, d)], out_shape=[_sds((t, d), BF16), _sds((d, t), BF16)],
        compiler_params=_cp("parallel"))(x, w)


def _rms_bwd_math(xv, wv, dyv):
    rstd = lax.rsqrt(jnp.mean(xv * xv, axis=-1, keepdims=True) + RMS_EPS)
    xhat = xv * rstd
    gv = dyv * wv
    dx = rstd * (gv - xhat * jnp.mean(gv * xhat, axis=-1, keepdims=True))
    return dx, jnp.sum(dyv * xhat, axis=0, keepdims=True)


def _rmsnorm_bwd(x, w, dy, dres, tm, name):
    t, d = x.shape

    def body(x_ref, w_ref, dy_ref, dres_ref, dx_ref, dw_ref):
        dx, dw = _rms_bwd_math(x_ref[...], w_ref[...], dy_ref[...])
        dx_ref[...] = dres_ref[...] + dx

        @pl.when(pl.program_id(0) == 0)
        def _():
            dw_ref[...] = jnp.zeros_like(dw_ref)

        dw_ref[...] += dw

    return pl.pallas_call(
        body, name=name, grid=(t // tm,),
        in_specs=[_row(tm, d), _fixed((1, d)), _row(tm, d), _row(tm, d)],
        out_specs=[_row(tm, d), _fixed((1, d))],
        out_shape=[_sds((t, d)), _sds((1, d))], compiler_params=_cp("arbitrary"))(x, w, dy, dres)


def _loss_head(x, w, target, tm):
    t, d = x.shape

    def body(x_ref, w_ref, t_ref, dx_ref, dw_ref, loss_ref):
        xv, wv = x_ref[...], w_ref[...]
        rstd = lax.rsqrt(jnp.mean(xv * xv, axis=-1, keepdims=True) + RMS_EPS)
        err = xv * rstd * wv - t_ref[...]
        dx, dw = _rms_bwd_math(xv, wv, err * (1.0 / d))
        dx_ref[...] = dx

        @pl.when(pl.program_id(0) == 0)
        def _():
            dw_ref[...] = jnp.zeros_like(dw_ref)
            loss_ref[...] = jnp.zeros_like(loss_ref)

        dw_ref[...] += dw
        loss_ref[...] += 0.5 * jnp.sum(jnp.mean(err * err, axis=-1, keepdims=True), axis=0, keepdims=True)

    return pl.pallas_call(
        body, name="loss_head", grid=(t // tm,),
        in_specs=[_row(tm, d), _fixed((1, d)), _row(tm, d)],
        out_specs=[_row(tm, d), _fixed((1, d)), _fixed((1, 1))],
        out_shape=[_sds((t, d)), _sds((1, d)), _sds((1, 1))], compiler_params=_cp("arbitrary"))(x, w, target)


def _ffn_in(h, w_gate, w_up):
    t, d = h.shape
    f = w_gate.shape[1]
    tm, tn = _tile(t, 512, 8), _tile(f, 1536)

    def body(h_ref, wg_ref, wu_ref, g_ref, u_ref, a_ref, at_ref):
        hv = h_ref[...]
        gv = jnp.dot(hv, wg_ref[...], preferred_element_type=F32)
        uv = jnp.dot(hv, wu_ref[...], preferred_element_type=F32)
        act = gv * jax.nn.sigmoid(gv) * uv
        g_ref[...] = gv.astype(BF16)
        u_ref[...] = uv.astype(BF16)
        a_ref[...] = act.astype(BF16)
        at_ref[...] = jnp.transpose(act).astype(BF16)

    w_spec = pl.BlockSpec((d, tn), lambda i, j: (0, j))
    o_spec = pl.BlockSpec((tm, tn), lambda i, j: (i, j))
    return pl.pallas_call(
        body, name="ffn_in", grid=(t // tm, f // tn),
        in_specs=[pl.BlockSpec((tm, d), lambda i, j: (i, 0)), w_spec, w_spec],
        out_specs=[o_spec, o_spec, o_spec, pl.BlockSpec((tn, tm), lambda i, j: (j, i))],
        out_shape=[_sds((t, f), BF16)] * 3 + [_sds((f, t), BF16)],
        compiler_params=_cp("parallel", "parallel"))(h, w_gate, w_up)


def _ffn_in_bwd(d_out, w_down, gate, up):
    t, d = d_out.shape
    f = w_down.shape[0]
    tm, tn = _tile(t, 512, 8), _tile(f, 1536)

    def body(do_ref, w_ref, g_ref, u_ref, dg_ref, du_ref):
        dv = lax.dot_general(do_ref[...].astype(BF16), w_ref[...], (((1,), (1,)), ((), ())),
                             preferred_element_type=F32)
        gv, uv = g_ref[...].astype(F32), u_ref[...].astype(F32)
        sg = jax.nn.sigmoid(gv)
        du_ref[...] = (dv * gv * sg).astype(BF16)
        dg_ref[...] = (dv * uv * (sg * (1.0 + gv * (1.0 - sg)))).astype(BF16)

    tile = pl.BlockSpec((tm, tn), lambda i, j: (i, j))
    return pl.pallas_call(
        body, name="ffn_in_bwd", grid=(t // tm, f // tn),
        in_specs=[pl.BlockSpec((tm, d), lambda i, j: (i, 0)), pl.BlockSpec((tn, d), lambda i, j: (j, 0)), tile, tile],
        out_specs=[tile, tile], out_shape=[_sds((t, f), BF16)] * 2,
        compiler_params=_cp("parallel", "parallel"))(d_out, w_down, gate, up)


def _halo_specs(tm, width, rows_total):
    per = tm // GROUP
    last = rows_total // GROUP - 1
    prev = pl.BlockSpec((GROUP, width), lambda i: (jnp.maximum(i * per - 1, 0), 0))
    nxt = pl.BlockSpec((GROUP, width), lambda i: (jnp.minimum((i + 1) * per, last), 0))
    return prev, nxt


def _edge_flags(tm, seq):
    i = pl.program_id(0)
    has_prev = jnp.where((i * tm) % seq == 0, 0.0, 1.0).astype(F32)
    has_next = jnp.where(((i + 1) * tm) % seq == 0, 0.0, 1.0).astype(F32)
    return has_prev, has_next


def _shifted(xv, prev_row, next_row):
    tm = xv.shape[0]
    row = lax.broadcasted_iota(jnp.int32, xv.shape, 0)
    down = jnp.where(row == 0, prev_row, pltpu.roll(xv, 1, axis=0))
    up = jnp.where(row == tm - 1, next_row, pltpu.roll(xv, tm - 1, axis=0))
    return down, up


def _shift_fwd(p, mu, tm, seq):
    t, w = p.shape
    prev_spec, next_spec = _halo_specs(tm, w, t)

    def body(p_ref, hp_ref, hn_ref, mu_ref, o_ref):
        has_prev, has_next = _edge_flags(tm, seq)
        xv = p_ref[...]
        down, up = _shifted(xv, hp_ref[GROUP - 1:GROUP, :] * has_prev, hn_ref[0:1, :] * has_next)
        o_ref[...] = xv + mu_ref[0:1, :] * (0.5 * (down + up) - xv)

    return pl.pallas_call(
        body, name="shift_fwd", grid=(t // tm,),
        in_specs=[_row(tm, w), prev_spec, next_spec, _fixed((GROUP, w))], out_specs=_row(tm, w),
        out_shape=_sds((t, w)), compiler_params=_cp("parallel"))(p, p, p, mu)


def _shift_bwd(q, p, mu, tm, seq):
    t, w = p.shape
    prev_spec, next_spec = _halo_specs(tm, w, t)

    def body(q_ref, qp_ref, qn_ref, p_ref, pp_ref, pn_ref, mu_ref, dp_ref, dmu_ref):
        has_prev, has_next = _edge_flags(tm, seq)
        muv = mu_ref[0:1, :]
        qv = q_ref[...]
        mq = muv * qv
        mq_down, mq_up = _shifted(mq, muv * qp_ref[GROUP - 1:GROUP, :] * has_prev,
                                  muv * qn_ref[0:1, :] * has_next)
        dp_ref[...] = (qv - mq + 0.5 * (mq_down + mq_up)).astype(BF16)
        pv = p_ref[...]
        p_down, p_up = _shifted(pv, pp_ref[GROUP - 1:GROUP, :] * has_prev, pn_ref[0:1, :] * has_next)

        @pl.when(pl.program_id(0) == 0)
        def _():
            dmu_ref[...] = jnp.zeros_like(dmu_ref)

        dmu_ref[...] += jnp.sum(qv * (0.5 * (p_down + p_up) - pv), axis=0, keepdims=True)

    return pl.pallas_call(
        body, name="shift_bwd", grid=(t // tm,),
        in_specs=[_row(tm, w), prev_spec, next_spec, _row(tm, w), prev_spec, next_spec, _fixed((GROUP, w))],
        out_specs=[_row(tm, w), _fixed((1, w))],
        out_shape=[_sds((t, w), BF16), _sds((1, w))], compiler_params=_cp("arbitrary"))(q, q, q, p, p, p, mu)


@jax.custom_vjp
def _bdot(a, b):
    return jnp.dot(a.astype(BF16), b.astype(BF16), preferred_element_type=F32)


def _bdot_fwd(a, b):
    return _bdot(a, b), (a, b)


def _bdot_bwd(res, g):
    a, b = res
    gb = g.astype(BF16)
    da = lax.dot_general(gb, b.astype(BF16), (((1,), (1,)), ((), ())), preferred_element_type=F32)
    db = lax.dot_general(a.astype(BF16), gb, (((0,), (0,)), ((), ())), preferred_element_type=F32)
    return da, db


_bdot.defvjp(_bdot_fwd, _bdot_bwd)


def _seg_raw(x, ones_blocks):
    hi = x.astype(BF16)
    lo = (x - hi.astype(F32)).astype(BF16)
    return (jnp.dot(hi, ones_blocks, preferred_element_type=F32)
            + jnp.dot(lo, ones_blocks, preferred_element_type=F32))


@jax.custom_vjp
def _seg(x, ones_blocks):
    return _seg_raw(x, ones_blocks)


def _seg_fwd(x, ones_blocks):
    return _seg_raw(x, ones_blocks), ones_blocks


def _seg_bwd(ones_blocks, g):
    return _seg_raw(g, ones_blocks), jnp.zeros_like(ones_blocks)


_seg.defvjp(_seg_fwd, _seg_bwd)


def _head_ones():
    h = jnp.arange(D_RWKV) // HEAD
    return (h[:, None] == h[None, :]).astype(BF16)


def _prep_math(ps, k_k, w0_f, a0_f, k_a_f, w0_b, a0_b, k_a_b, wup_f, aup_f, wup_b, aup_b, gup, ones_blocks):
    r = ps[:, 0:512]
    k = ps[:, 512:1024]
    v = ps[:, 1024:1536]
    xwa = ps[:, 1536:1664]
    xg = ps[:, 1664:D_SHIFT_PAD]
    kk_raw = k * k_k
    norm = jnp.sqrt(_seg(kk_raw * kk_raw, ones_blocks))
    kk = kk_raw / jnp.maximum(norm, NORM_EPS)
    t_xwa = jnp.tanh(xwa)
    outs = [r, v, kk]
    for w0, a0, k_a, wup, aup in ((w0_f, a0_f, k_a_f, wup_f, aup_f), (w0_b, a0_b, k_a_b, wup_b, aup_b)):
        decay = jnp.exp(-LOG_DECAY_SCALE * jax.nn.sigmoid(w0 + _bdot(t_xwa, wup)))
        rate = jax.nn.sigmoid(a0 + _bdot(xwa, aup))
        outs += [decay, k * (1.0 + (rate - 1.0) * k_a), kk * rate]
    outs.append(_bdot(jax.nn.sigmoid(xg), gup))
    return tuple(outs)


def _prep_args(tm, ps_ref, pv_ref, mat_refs, ones_ref):
    vecs = [jnp.broadcast_to(pv_ref[j:j + 1, :], (tm, D_RWKV)) for j in range(7)]
    return [ps_ref[...]] + vecs + [m[...] for m in mat_refs] + [ones_ref[...]]


_PREP_MAT_SHAPES = ((128, D_RWKV),) * 4 + ((D_GATE_PAD, D_RWKV),)


def _prep_fwd(ps, pvec, mats, ones_blocks, tm):
    t = ps.shape[0]

    def body(ps_ref, pv_ref, m0, m1, m2, m3, m4, ones_ref, *out_refs):
        outs = _prep_math(*_prep_args(tm, ps_ref, pv_ref, (m0, m1, m2, m3, m4), ones_ref))
        for o_ref, val in zip(out_refs, outs[2:]):
            o_ref[...] = val

    return pl.pallas_call(
        body, name="prep_fwd", grid=(t // tm,),
        in_specs=[_row(tm, D_SHIFT_PAD), _fixed((8, D_RWKV))] + [_fixed(s) for s in _PREP_MAT_SHAPES]
        + [_fixed((D_RWKV, D_RWKV))],
        out_specs=[_row(tm, D_RWKV)] * 8, out_shape=[_sds((t, D_RWKV))] * 8,
        compiler_params=_cp("parallel"))(ps, pvec, *mats, ones_blocks)


def _prep_bwd(ps, pvec, mats, ones_blocks, cts, tm):
    t = ps.shape[0]
    counts = [len(c) for c in cts]
    flat = [a for c in cts for a in c]

    def body(ps_ref, pv_ref, m0, m1, m2, m3, m4, ones_ref, *refs):
        ct_refs = refs[:len(flat)]
        q_ref, dpv_ref = refs[len(flat)], refs[len(flat) + 1]
        dmat_refs = refs[len(flat) + 2:]
        args = _prep_args(tm, ps_ref, pv_ref, (m0, m1, m2, m3, m4), ones_ref)
        _, vjp = jax.vjp(lambda *a: _prep_math(*a, args[-1]), *args[:-1])
        ct_vals, pos = [], 0
        for n in counts:
            val = ct_refs[pos][...]
            for extra in ct_refs[pos + 1:pos + n]:
                val = val + extra[...]
            ct_vals.append(val)
            pos += n
        grads = vjp(tuple(ct_vals))
        q_ref[...] = grads[0]

        @pl.when(pl.program_id(0) == 0)
        def _():
            dpv_ref[...] = jnp.zeros_like(dpv_ref)
            for d_ref in dmat_refs:
                d_ref[...] = jnp.zeros_like(d_ref)

        for j in range(7):
            dpv_ref[j:j + 1, :] += jnp.sum(grads[1 + j], axis=0, keepdims=True)
        for d_ref, gm in zip(dmat_refs, grads[8:13]):
            d_ref[...] += gm

    return pl.pallas_call(
        body, name="prep_bwd", grid=(t // tm,),
        in_specs=[_row(tm, D_SHIFT_PAD), _fixed((8, D_RWKV))] + [_fixed(s) for s in _PREP_MAT_SHAPES]
        + [_fixed((D_RWKV, D_RWKV))] + [_row(tm, D_RWKV)] * len(flat),
        out_specs=[_row(tm, D_SHIFT_PAD), _fixed((8, D_RWKV))] + [_fixed(s) for s in _PREP_MAT_SHAPES],
        out_shape=[_sds((t, D_SHIFT_PAD)), _sds((8, D_RWKV))] + [_sds(s) for s in _PREP_MAT_SHAPES],
        compiler_params=_cp("arbitrary"))(ps, pvec, *mats, ones_blocks, *flat)


def _pair_ones():
    h = jnp.arange(2 * HEAD) // HEAD
    block = (h[:, None] == h[None, :]).astype(BF16)
    return jnp.concatenate([block, block], axis=0)


def _diag_mask():
    lane = lax.broadcasted_iota(jnp.int32, (HEAD, 2 * HEAD), 1)
    sub = lax.broadcasted_iota(jnp.int32, (HEAD, 2 * HEAD), 0)
    return jnp.where((lane & (HEAD - 1)) == sub, 1.0, 0.0).astype(F32)


def _to_row(cols, dmask):
    return jnp.sum(cols * dmask, axis=0, keepdims=True)


def _seg_many(exact, rounded, ones_pair):
    out_exact, out_rounded = [], []
    if exact:
        parts = []
        for x in exact:
            hi = x.astype(BF16)
            parts.append(jnp.concatenate([hi, (x - hi.astype(F32)).astype(BF16)], axis=1))
        res = jnp.dot(jnp.concatenate(parts, axis=0), ones_pair, preferred_element_type=F32)
        out_exact = [res[HEAD * c:HEAD * (c + 1)] for c in range(len(exact))]
    if rounded:
        res = jnp.dot(jnp.concatenate([x.astype(BF16) for x in rounded], axis=0), ones_pair[0:2 * HEAD],
                      preferred_element_type=F32)
        out_rounded = [res[HEAD * c:HEAD * (c + 1)] for c in range(len(rounded))]
    return out_exact, out_rounded


N_CHAIN = 2 * N_PAIR


def _chain(c):
    d, p = divmod(c, N_PAIR)
    return d, slice(2 * HEAD * p, 2 * HEAD * (p + 1))


def _scan_specs(n_chunks, col_blocks, fwd_chunk, bwd_chunk):
    def spec(chunk_of, col):
        return pl.BlockSpec((SCAN_CHUNK, D_RWKV), lambda b, g: (b * n_chunks + chunk_of(g), col))
    return [spec(fwd_chunk, c) for c in col_blocks] + [spec(bwd_chunk, c) for c in col_blocks]


def _scan_fwd(ps, kk, dirs, batch, seq):
    t = batch * seq
    n = seq // SCAN_CHUNK
    groups = SCAN_CHUNK // GROUP
    up = lambda g: g
    down = lambda g: n - 1 - g
    col_blocks = (0, 2, 0, 0, 0, 0)

    def body(*refs):
        dir_refs = (refs[0:6], refs[6:12])
        ones_ref = refs[12]
        y_refs, hist_refs, st_ref = refs[13:15], refs[15:17], refs[17]

        @pl.when(pl.program_id(1) == 0)
        def _():
            st_ref[...] = jnp.zeros_like(st_ref)

        ones_pair = ones_ref[...]
        dmask = _diag_mask()
        dmask_b = dmask.astype(BF16)
        sub8 = lax.broadcasted_iota(jnp.int32, (GROUP, 2 * HEAD), 0)

        def group(gi, carry):
            off = (pl.multiple_of(gi * GROUP, GROUP), pl.multiple_of((groups - 1 - gi) * GROUP, GROUP))
            loaded = [tuple(ref[pl.ds(off[d], GROUP), :] for ref in dir_refs[d]) for d in range(2)]
            states = list(carry)
            y_acc = [jnp.zeros((GROUP, 2 * HEAD), F32) for _ in range(N_CHAIN)]
            for step in range(GROUP):
                rows, idx = [], []
                for c in range(N_CHAIN):
                    d, lanes = _chain(c)
                    i = step if d == 0 else GROUP - 1 - step
                    idx.append(i)
                    rows.append(tuple(x8[i:i + 1, lanes] for x8 in loaded[d]))
                    hist_refs[d][c % N_PAIR, gi * GROUP + step] = states[c]
                _, v_cols = _seg_many([], [dmask_b * rows[c][1].astype(BF16) for c in range(N_CHAIN)], ones_pair)
                sas, _ = _seg_many([states[c] * rows[c][2] for c in range(N_CHAIN)], [], ones_pair)
                for c in range(N_CHAIN):
                    _, _, _, w_row, kd_row, b_row = rows[c]
                    states[c] = states[c] * w_row - sas[c] * b_row + v_cols[c] * kd_row
                _, ys = _seg_many([], [states[c] * rows[c][0] for c in range(N_CHAIN)], ones_pair)
                for c in range(N_CHAIN):
                    y_acc[c] = jnp.where(sub8 == idx[c], _to_row(ys[c], dmask), y_acc[c])
            for c in range(N_CHAIN):
                d, lanes = _chain(c)
                y_refs[d][pl.ds(off[d], GROUP), lanes] = y_acc[c]
            return tuple(states)

        final = lax.fori_loop(0, groups, group, tuple(st_ref[c] for c in range(N_CHAIN)))
        for c in range(N_CHAIN):
            st_ref[c] = final[c]
            hist_refs[c // N_PAIR][c % N_PAIR, SCAN_CHUNK] = final[c]

    y_spec_f = pl.BlockSpec((SCAN_CHUNK, D_RWKV), lambda b, g: (b * n + up(g), 0))
    y_spec_b = pl.BlockSpec((SCAN_CHUNK, D_RWKV), lambda b, g: (b * n + down(g), 0))
    hist_shape = (batch, n, N_PAIR, SCAN_CHUNK + 1, HEAD, 2 * HEAD)
    hist_block = (None, None, N_PAIR, SCAN_CHUNK + 1, HEAD, 2 * HEAD)
    hist_spec_f = pl.BlockSpec(hist_block, lambda b, g: (b, up(g), 0, 0, 0, 0))
    hist_spec_b = pl.BlockSpec(hist_block, lambda b, g: (b, down(g), 0, 0, 0, 0))
    ones_spec = pl.BlockSpec((4 * HEAD, 2 * HEAD), lambda b, g: (0, 0))
    (wf, kdf, bf), (wb, kdb, bb) = dirs
    return pl.pallas_call(
        body, name="wkv_fwd", grid=(batch, n),
        in_specs=_scan_specs(n, col_blocks, up, down) + [ones_spec],
        out_specs=[y_spec_f, y_spec_b, hist_spec_f, hist_spec_b],
        out_shape=[_sds((t, D_RWKV)), _sds((t, D_RWKV)), _sds(hist_shape), _sds(hist_shape)],
        scratch_shapes=[pltpu.VMEM((N_CHAIN, HEAD, 2 * HEAD), F32)],
        compiler_params=_cp("parallel", "arbitrary"),
    )(ps, ps, kk, wf, kdf, bf, ps, ps, kk, wb, kdb, bb, _pair_ones())


def _scan_bwd(ps, kk, dirs, dy, hist_f, hist_b, batch, seq):
    t = batch * seq
    n = seq // SCAN_CHUNK
    groups = SCAN_CHUNK // GROUP
    fwd_chunk = lambda g: n - 1 - g
    bwd_chunk = lambda g: g
    col_blocks = (0, 2, 0, 0, 0, 0, 0)

    def undo_group(dir_refs, out_refs, hist_refs, gi, d_states, ones_pair, dmask, sub8):
        d_states = list(d_states)
        loaded, blocks = [], []
        for d in range(2):
            blk = groups - 1 - gi if d == 0 else gi
            blocks.append(pl.ds(pl.multiple_of(blk * GROUP, GROUP), GROUP))
            r8, v8, kk8, w8, kd8, b8, dy8 = (ref[blocks[d], :] for ref in dir_refs[d])
            loaded.append((r8, v8, kk8, w8, kd8, -b8, dy8))
        acc = [[jnp.zeros((GROUP, 2 * HEAD), F32) for _ in range(6)] for _ in range(N_CHAIN)]
        for step in range(GROUP):
            rows, idx, before, after = [], [], [], []
            for c in range(N_CHAIN):
                d, lanes = _chain(c)
                i = GROUP - 1 - step if d == 0 else step
                q = (groups - 1 - gi) * GROUP + i if d == 0 else SCAN_CHUNK - 1 - (gi * GROUP + i)
                idx.append(i)
                rows.append(tuple(x8[i:i + 1, lanes] for x8 in loaded[d]))
                before.append(hist_refs[d][c % N_PAIR, q])
                after.append(hist_refs[d][c % N_PAIR, q + 1])
            _, cols = _seg_many([], [dmask.astype(BF16) * rows[c][j].astype(BF16) for c in range(N_CHAIN) for j in (1, 6)],
                                ones_pair)
            v_cols, dy_cols = cols[0::2], cols[1::2]
            d_now = [d_states[c] + dy_cols[c] * rows[c][0] for c in range(N_CHAIN)]
            d_sas, _ = _seg_many([d_now[c] * rows[c][5] for c in range(N_CHAIN)], [], ones_pair)
            _, others = _seg_many(
                [], [x for c in range(N_CHAIN) for x in (before[c] * rows[c][2], d_now[c] * rows[c][4])], ones_pair)
            for c in range(N_CHAIN):
                sa, d_sa, dv_cols = others[2 * c], d_sas[c], others[2 * c + 1]
                rows_out = (
                    jnp.sum(after[c] * dy_cols[c], axis=0, keepdims=True),
                    jnp.sum(d_now[c] * before[c], axis=0, keepdims=True),
                    jnp.sum(d_now[c] * v_cols[c], axis=0, keepdims=True),
                    _to_row(dv_cols, dmask),
                    jnp.sum(before[c] * d_sa, axis=0, keepdims=True),
                    -jnp.sum(d_now[c] * sa, axis=0, keepdims=True),
                )
                acc[c] = [jnp.where(sub8 == idx[c], val, a) for val, a in zip(rows_out, acc[c])]
                d_states[c] = d_now[c] * rows[c][3] + d_sa * rows[c][2]
        for c in range(N_CHAIN):
            d, lanes = _chain(c)
            for o_ref, val in zip(out_refs[d], acc[c]):
                o_ref[blocks[d], lanes] = val
        return tuple(d_states)

    def body(*refs):
        dir_refs = (refs[0:7], refs[7:14])
        hist_refs, ones_ref = refs[14:16], refs[16]
        out_refs = (refs[17:23], refs[23:29])
        dst_ref = refs[29]

        @pl.when(pl.program_id(1) == 0)
        def _():
            dst_ref[...] = jnp.zeros_like(dst_ref)

        ones_pair = ones_ref[...]
        dmask = _diag_mask()
        sub8 = lax.broadcasted_iota(jnp.int32, (GROUP, 2 * HEAD), 0)

        def group(gi, carry):
            return undo_group(dir_refs, out_refs, hist_refs, gi, carry, ones_pair, dmask, sub8)

        final = lax.fori_loop(0, groups, group, tuple(dst_ref[c] for c in range(N_CHAIN)))
        for c in range(N_CHAIN):
            dst_ref[c] = final[c]

    blk = (SCAN_CHUNK, D_RWKV)
    out_f = pl.BlockSpec(blk, lambda b, g: (b * n + fwd_chunk(g), 0))
    out_b = pl.BlockSpec(blk, lambda b, g: (b * n + bwd_chunk(g), 0))
    hist_block = (None, None, N_PAIR, SCAN_CHUNK + 1, HEAD, 2 * HEAD)
    hist_spec_f = pl.BlockSpec(hist_block, lambda b, g: (b, fwd_chunk(g), 0, 0, 0, 0))
    hist_spec_b = pl.BlockSpec(hist_block, lambda b, g: (b, bwd_chunk(g), 0, 0, 0, 0))
    ones_spec = pl.BlockSpec((4 * HEAD, 2 * HEAD), lambda b, g: (0, 0))
    (wf, kdf, bf), (wb, kdb, bb) = dirs
    outs = pl.pallas_call(
        body, name="wkv_bwd", grid=(batch, n),
        in_specs=_scan_specs(n, col_blocks, fwd_chunk, bwd_chunk) + [hist_spec_f, hist_spec_b, ones_spec],
        out_specs=[out_f] * 6 + [out_b] * 6,
        out_shape=[_sds((t, D_RWKV))] * 12,
        scratch_shapes=[pltpu.VMEM((N_CHAIN, HEAD, 2 * HEAD), F32)],
        compiler_params=_cp("parallel", "arbitrary"),
    )(ps, ps, kk, wf, kdf, bf, dy, ps, ps, kk, wb, kdb, bb, dy, hist_f, hist_b, _pair_ones())
    return outs[0:6], outs[6:12]


def _post_math(y, r, kd_f, kd_b, v, gate, gn_w, gn_b, rk_f, rk_b, ones_blocks):
    mean = _seg(y, ones_blocks) * (1.0 / HEAD)
    yc = y - mean
    var = _seg(yc * yc, ones_blocks) * (1.0 / HEAD)
    yn = yc * lax.rsqrt(var + GN_EPS) * gn_w + gn_b
    bonus = _seg(r * kd_f * rk_f, ones_blocks) * v + _seg(r * kd_b * rk_b, ones_blocks) * v
    return (yn + bonus) * gate


def _conv_parts(pc, halo_prev, halo_next, has_prev, has_next):
    gate_b, gate_c, hid = pc[:, 0:512], pc[:, 512:1024], pc[:, 1024:1536]
    u = gate_c * hid
    u_prev_row = halo_prev[GROUP - 1:GROUP, 512:1024] * halo_prev[GROUP - 1:GROUP, 1024:1536] * has_prev
    u_next_row = halo_next[0:1, 512:1024] * halo_next[0:1, 1024:1536] * has_next
    u_down, u_up = _shifted(u, u_prev_row, u_next_row)
    return gate_b, gate_c, hid, u, u_down, u_up


def _post_specs(tm, t):
    pc_prev, pc_next = _halo_specs(tm, D_CONV3, t)
    col = lambda c: pl.BlockSpec((tm, D_RWKV), lambda i: (i, c))
    return ([col(0), col(0), col(0), col(0), col(0), col(2), col(0), _row(tm, D_CONV3), pc_prev, pc_next,
             _fixed((8, D_RWKV)), _fixed((D_RWKV, D_RWKV))])


def _post_fwd(y_f, y_b, ps, kd_f, kd_b, gate, pc, qvec, ones_blocks, tm, seq):
    t = ps.shape[0]

    def body(yf_ref, yb_ref, r_ref, kdf_ref, kdb_ref, v_ref, g_ref, pc_ref, hp_ref, hn_ref, qv_ref, ones_ref,
             o_ref, ot_ref):
        has_prev, has_next = _edge_flags(tm, seq)
        vec = [jnp.broadcast_to(qv_ref[j:j + 1, :], (tm, D_RWKV)) for j in range(7)]
        o_rwkv = _post_math(yf_ref[...] + yb_ref[...], r_ref[...], kdf_ref[...], kdb_ref[...], v_ref[...],
                            g_ref[...], vec[0], vec[1], vec[2], vec[3], ones_ref[...])
        gate_b, _, _, u, u_down, u_up = _conv_parts(pc_ref[...], hp_ref[...], hn_ref[...], has_prev, has_next)
        o_conv = gate_b * (vec[4] * u_down + vec[5] * u + vec[6] * u_up)
        for half, val in enumerate((o_rwkv, o_conv)):
            o_ref[:, D_RWKV * half:D_RWKV * (half + 1)] = val.astype(BF16)
            ot_ref[D_RWKV * half:D_RWKV * (half + 1), :] = jnp.transpose(val).astype(BF16)

    return pl.pallas_call(
        body, name="post_fwd", grid=(t // tm,), in_specs=_post_specs(tm, t),
        out_specs=[_row(tm, D_MODEL), _col(tm, D_MODEL)],
        out_shape=[_sds((t, D_MODEL), BF16), _sds((D_MODEL, t), BF16)], compiler_params=_cp("parallel"),
    )(y_f, y_b, ps, kd_f, kd_b, ps, gate, pc, pc, pc, qvec, ones_blocks)


def _post_bwd(d_out, y_f, y_b, ps, kd_f, kd_b, gate, pc, qvec, ones_blocks, tm, seq):
    t = ps.shape[0]
    do_prev, do_next = _halo_specs(tm, D_MODEL, t)

    def body(do_ref, dop_ref, don_ref, yf_ref, yb_ref, r_ref, kdf_ref, kdb_ref, v_ref, g_ref, pc_ref, hp_ref,
             hn_ref, qv_ref, ones_ref, dy_ref, dr_ref, dkdf_ref, dkdb_ref, dv_ref, dg_ref, dpc_ref, dqv_ref):
        has_prev, has_next = _edge_flags(tm, seq)
        vec = [jnp.broadcast_to(qv_ref[j:j + 1, :], (tm, D_RWKV)) for j in range(7)]
        ones_v = ones_ref[...]
        args = (yf_ref[...] + yb_ref[...], r_ref[...], kdf_ref[...], kdb_ref[...], v_ref[...], g_ref[...],
                vec[0], vec[1], vec[2], vec[3])
        _, vjp = jax.vjp(lambda *a: _post_math(*a, ones_v), *args)
        grads = vjp(do_ref[:, 0:D_RWKV])
        for o_ref, gval in zip((dy_ref, dr_ref, dkdf_ref, dkdb_ref, dv_ref, dg_ref), grads[0:6]):
            o_ref[...] = gval

        hp, hn = hp_ref[...], hn_ref[...]
        gate_b, gate_c, hid, u, u_down, u_up = _conv_parts(pc_ref[...], hp, hn, has_prev, has_next)
        d_oc = do_ref[:, D_RWKV:2 * D_RWKV]
        d_cu = d_oc * gate_b
        d_cu_prev = dop_ref[GROUP - 1:GROUP, D_RWKV:2 * D_RWKV] * hp[GROUP - 1:GROUP, 0:512] * has_prev
        d_cu_next = don_ref[0:1, D_RWKV:2 * D_RWKV] * hn[0:1, 0:512] * has_next
        d_cu_down, d_cu_up = _shifted(d_cu, d_cu_prev, d_cu_next)
        d_u = vec[5] * d_cu + vec[4] * d_cu_up + vec[6] * d_cu_down
        dpc_ref[:, 0:512] = (d_oc * (vec[4] * u_down + vec[5] * u + vec[6] * u_up)).astype(BF16)
        dpc_ref[:, 512:1024] = (d_u * hid).astype(BF16)
        dpc_ref[:, 1024:1536] = (d_u * gate_c).astype(BF16)

        @pl.when(pl.program_id(0) == 0)
        def _():
            dqv_ref[...] = jnp.zeros_like(dqv_ref)

        vec_grads = list(grads[6:10]) + [d_cu * u_down, d_cu * u, d_cu * u_up]
        for j, gval in enumerate(vec_grads):
            dqv_ref[j:j + 1, :] += jnp.sum(gval, axis=0, keepdims=True)

    return pl.pallas_call(
        body, name="post_bwd", grid=(t // tm,),
        in_specs=[_row(tm, D_MODEL), do_prev, do_next] + _post_specs(tm, t),
        out_specs=[_row(tm, D_RWKV)] * 6 + [_row(tm, D_CONV3), _fixed((8, D_RWKV))],
        out_shape=[_sds((t, D_RWKV))] * 6 + [_sds((t, D_CONV3), BF16), _sds((8, D_RWKV))],
        compiler_params=_cp("arbitrary"),
    )(d_out, d_out, d_out, y_f, y_b, ps, kd_f, kd_b, ps, gate, pc, pc, pc, qvec, ones_blocks)


def _adamw_math(wv, gv, mv, vv):
    m2 = ADAM_B1 * mv + (1.0 - ADAM_B1) * gv
    v2 = ADAM_B2 * vv + (1.0 - ADAM_B2) * (gv * gv)
    m_hat = m2 / (1.0 - ADAM_B1 ** ADAM_STEP)
    v_hat = v2 / (1.0 - ADAM_B2 ** ADAM_STEP)
    return -ADAM_LR * (m_hat / (jnp.sqrt(v_hat) + ADAM_EPS) + ADAM_WD * wv), m2, v2


def _adamw_small(items):
    n = len(items)

    def body(*refs):
        ins, outs = refs[:4 * n], refs[4 * n:]
        for k in range(n):
            w_ref, g_ref, m_ref, v_ref = ins[4 * k:4 * k + 4]
            for o_ref, val in zip(outs[3 * k:3 * k + 3], _adamw_math(w_ref[...], g_ref[...], m_ref[...], v_ref[...])):
                o_ref[...] = val

    flat = [a for item in items for a in item]
    outs = pl.pallas_call(
        body, name="adamw_small", out_shape=[_sds(item[0].shape) for item in items for _ in range(3)],
        compiler_params=_cp())(*flat)
    return [tuple(outs[3 * k:3 * k + 3]) for k in range(n)]


def _adamw(w, g, m, v, name):
    r, c = w.shape[-2:]
    tr = _tile(r, 256, 8)
    if w.ndim == 3:
        spec = pl.BlockSpec((None, tr, c), lambda i: (0, i, 0))
    else:
        spec = pl.BlockSpec((tr, c), lambda i: (i, 0))

    def body(w_ref, g_ref, m_ref, v_ref, d_ref, nm_ref, nv_ref):
        d_ref[...], nm_ref[...], nv_ref[...] = _adamw_math(w_ref[...], g_ref[...], m_ref[...], v_ref[...])

    return pl.pallas_call(
        body, name=name, grid=(r // tr,), in_specs=[spec] * 4, out_specs=[spec] * 3,
        out_shape=[_sds(w.shape)] * 3, compiler_params=_cp("parallel"))(w, g, m, v)


_ANY = pl.BlockSpec(memory_space=pl.ANY)


def _place():
    return lax.axis_index("x"), lax.axis_index("y"), lax.axis_index("c")


def _other_chips(x, y):
    return [(1 - x, y), (x, 1 - y), (1 - x, 1 - y)]


def _remote(src, dst, send_sems, recv_sems, k, to):
    return pltpu.make_async_remote_copy(src_ref=src, dst_ref=dst, send_sem=send_sems.at[k],
                                        recv_sem=recv_sems.at[k], device_id=to, device_id_type=MESH)


def _gather_weights(pack):
    rows, width = pack.shape
    half = rows // 2

    def body(x_ref, out_ref, send_sems, recv_sems):
        x, y, c = _place()
        sibling = (x, y, 1 - c)
        chips = _other_chips(x, y)

        def block(chip, part):
            return out_ref.at[2 * chip[0] + chip[1], pl.ds(part * half, half), :]

        first = [_remote(x_ref.at[pl.ds(c * half, half), :], block((x, y), c), send_sems, recv_sems, j, (*chip, c))
                 for j, chip in enumerate(chips)]
        for cp in first:
            cp.start()
        passed = [_remote(block(chip, c), block(chip, c), send_sems, recv_sems, 3 + j, sibling)
                  for j, chip in enumerate(chips)]
        for j, chip in enumerate(chips):
            _remote(block(chip, c), block(chip, c), send_sems, recv_sems, j, sibling).wait_recv()
            passed[j].start()
        for j, chip in enumerate(chips):
            _remote(block(chip, 1 - c), block(chip, 1 - c), send_sems, recv_sems, 3 + j, sibling).wait_recv()
        for cp in first + passed:
            cp.wait_send()

    return pl.pallas_call(
        body, name="gather_weights", in_specs=[_ANY], out_specs=_ANY,
        out_shape=_sds((N_SHARD, rows, width), pack.dtype),
        scratch_shapes=[pltpu.SemaphoreType.DMA((6,)), pltpu.SemaphoreType.DMA((6,))],
    )(pack)


_HBM = pl.BlockSpec(memory_space=pltpu.HBM)
_SEMS = pl.BlockSpec(memory_space=pltpu.SEMAPHORE)
_DATAFLOW = pltpu.SideEffectType.DATAFLOW_SIDE_EFFECTING


def _fetch_start(pack):
    def body(x_ref, land_ref, send_sems, recv_sems, x_thru, land_thru, token):
        x, y, c = _place()
        for j, chip in enumerate(_other_chips(x, y)):
            _remote(x_ref, land_ref.at[2 * x + y], send_sems, recv_sems, j, (*chip, c)).start()
        token[...] = jnp.zeros_like(token)

    land = lax.empty((N_SHARD,) + pack.shape, pack.dtype)
    return pl.pallas_call(
        body, name="fetch_ffn_start",
        out_shape=(pltpu.SemaphoreType.DMA((3,)), pltpu.SemaphoreType.DMA((3,)), pltpu.HBM(pack.shape, pack.dtype),
                   pltpu.HBM(land.shape, land.dtype), _sds((8, 128))),
        in_specs=(_HBM, _HBM), out_specs=(_SEMS, _SEMS, _HBM, _HBM, pl.BlockSpec(memory_space=pltpu.VMEM)),
        input_output_aliases={0: 2, 1: 3}, compiler_params=pltpu.CompilerParams(has_side_effects=_DATAFLOW),
    )(pltpu.with_memory_space_constraint(pack, pltpu.HBM), pltpu.with_memory_space_constraint(land, pltpu.HBM))


def _fetch_wait(send_sems, recv_sems, pack_thru, land_thru, after):
    def body(x_ref, land_ref, send_sems, recv_sems, after_ref, x_dead, got_ref):
        x, y, c = _place()
        for j, chip in enumerate(_other_chips(x, y)):
            cp = _remote(x_ref, land_ref.at[2 * chip[0] + chip[1]], send_sems, recv_sems, j, (*chip, c))
            cp.wait_send()
            cp.wait_recv()

    return pl.pallas_call(
        body, name="fetch_ffn_wait",
        out_shape=(pltpu.HBM(pack_thru.shape, pack_thru.dtype), pltpu.HBM(land_thru.shape, land_thru.dtype)),
        in_specs=(_HBM, _HBM, _SEMS, _SEMS, _ANY), out_specs=(_HBM, _HBM), input_output_aliases={0: 0, 1: 1},
        compiler_params=pltpu.CompilerParams(has_side_effects=_DATAFLOW),
    )(pack_thru, land_thru, send_sems, recv_sems, after)[1]


def _swap_with_sibling(block, name):
    def body(x_ref, out_ref, send_sems, recv_sems):
        x, y, c = _place()
        cp = _remote(x_ref, out_ref, send_sems, recv_sems, 0, (x, y, 1 - c))
        cp.start()
        cp.wait()

    return pl.pallas_call(
        body, name=name, in_specs=[_ANY], out_specs=_ANY, out_shape=_sds(block.shape, block.dtype),
        scratch_shapes=[pltpu.SemaphoreType.DMA((1,)), pltpu.SemaphoreType.DMA((1,))],
    )(block)


def _swap_other_half(packed, tag):
    slots, rows, width = packed.shape
    half = rows // 2

    def body(x_ref, out_ref, send_sems, recv_sems):
        x, y, c = _place()
        cp = _remote(x_ref.at[:, pl.ds((1 - c) * half, half), :], out_ref, send_sems, recv_sems, 0, (x, y, 1 - c))
        cp.start()
        cp.wait()

    return pl.pallas_call(
        body, name="swap_halves_" + tag, in_specs=[_ANY], out_specs=_ANY, out_shape=_sds((slots, half, width)),
        scratch_shapes=[pltpu.SemaphoreType.DMA((1,)), pltpu.SemaphoreType.DMA((1,))],
    )(packed)


def _add_halves(packed, got, c, tag):
    slots, rows, width = packed.shape
    half = rows // 2
    tr = _tile(half, 408, 16)
    per = half // tr
    block = (None, tr, width)

    def body(c_ref, mine_ref, got_ref, sum_ref, sum16_ref):
        acc = mine_ref[...] + got_ref[...]
        sum_ref[...] = acc
        sum16_ref[...] = acc.astype(BF16)

    plain = pl.BlockSpec(block, lambda s, i, c_ref: (s, i, 0))
    grid_spec = pltpu.PrefetchScalarGridSpec(
        num_scalar_prefetch=1, grid=(slots, per),
        in_specs=[pl.BlockSpec(block, lambda s, i, c_ref: (s, c_ref[0] * per + i, 0)), plain],
        out_specs=[plain, plain])
    return pl.pallas_call(
        body, name="add_halves_" + tag, grid_spec=grid_spec,
        out_shape=[_sds((slots, half, width)), _sds((slots, half, width), BF16)],
        compiler_params=_cp("parallel", "parallel"))(c.reshape(1).astype(jnp.int32), packed, got)


def _add_quarters(chip_sum, others, chip, tag):
    _, rows, width = chip_sum.shape
    tr = _tile(rows, 408, 16)

    def body(chip_ref, own_ref, others_ref, o_ref):
        acc = own_ref[...]
        for j in range(3):
            acc = acc + others_ref[j].astype(F32)
        o_ref[...] = acc

    grid_spec = pltpu.PrefetchScalarGridSpec(
        num_scalar_prefetch=1, grid=(rows // tr,),
        in_specs=[pl.BlockSpec((None, tr, width), lambda i, chip_ref: (chip_ref[0], i, 0)),
                  pl.BlockSpec((3, tr, width), lambda i, chip_ref: (0, i, 0))],
        out_specs=pl.BlockSpec((tr, width), lambda i, chip_ref: (i, 0)))
    return pl.pallas_call(
        body, name="add_quarters_" + tag, grid_spec=grid_spec, out_shape=_sds((rows, width)),
        compiler_params=_cp("parallel"))(chip.reshape(1).astype(jnp.int32), chip_sum, others)


def _exchange_start(parts):
    _, rows, width = parts.shape

    def body(x_ref, land_ref, send_sems, recv_sems, x_thru, land_thru, token):
        x, y, c = _place()
        for j, chip in enumerate(_other_chips(x, y)):
            _remote(x_ref.at[2 * chip[0] + chip[1]], land_ref.at[j], send_sems, recv_sems, j, (*chip, c)).start()
        token[...] = jnp.zeros_like(token)

    land = lax.empty((3, rows, width), parts.dtype)
    return pl.pallas_call(
        body, name="exchange_ffn_start",
        out_shape=(pltpu.SemaphoreType.DMA((3,)), pltpu.SemaphoreType.DMA((3,)), pltpu.HBM(parts.shape, parts.dtype),
                   pltpu.HBM(land.shape, land.dtype), _sds((8, 128))),
        in_specs=(_HBM, _HBM), out_specs=(_SEMS, _SEMS, _HBM, _HBM, pl.BlockSpec(memory_space=pltpu.VMEM)),
        input_output_aliases={0: 2, 1: 3}, compiler_params=pltpu.CompilerParams(has_side_effects=_DATAFLOW),
    )(pltpu.with_memory_space_constraint(parts, pltpu.HBM), pltpu.with_memory_space_constraint(land, pltpu.HBM))


def _exchange_wait(send_sems, recv_sems, parts_thru, land_thru, after):
    def body(x_ref, land_ref, send_sems, recv_sems, after_ref, x_dead, got_ref):
        x, y, c = _place()
        for j, chip in enumerate(_other_chips(x, y)):
            cp = _remote(x_ref.at[2 * chip[0] + chip[1]], land_ref.at[j], send_sems, recv_sems, j, (*chip, c))
            cp.wait_send()
            cp.wait_recv()

    return pl.pallas_call(
        body, name="exchange_ffn_wait",
        out_shape=(pltpu.HBM(parts_thru.shape, parts_thru.dtype), pltpu.HBM(land_thru.shape, land_thru.dtype)),
        in_specs=(_HBM, _HBM, _SEMS, _SEMS, _ANY), out_specs=(_HBM, _HBM), input_output_aliases={0: 0, 1: 1},
        compiler_params=pltpu.CompilerParams(has_side_effects=_DATAFLOW),
    )(parts_thru, land_thru, send_sems, recv_sems, after)[1]


def _exchange_quarters(parts):
    _, rows, width = parts.shape

    def body(x_ref, out_ref, send_sems, recv_sems):
        x, y, c = _place()
        copies = [_remote(x_ref.at[2 * chip[0] + chip[1]], out_ref.at[j], send_sems, recv_sems, j, (*chip, c))
                  for j, chip in enumerate(_other_chips(x, y))]
        for cp in copies:
            cp.start()
        for cp in copies:
            cp.wait()

    return pl.pallas_call(
        body, name="exchange_quarters", in_specs=[_ANY], out_specs=_ANY,
        out_shape=_sds((3, rows, width), parts.dtype),
        scratch_shapes=[pltpu.SemaphoreType.DMA((3,)), pltpu.SemaphoreType.DMA((3,))],
    )(parts)


def _allreduce_small(vec):
    rows, width = vec.shape
    vmem = pl.BlockSpec(memory_space=pltpu.VMEM)

    def body(x_ref, o_ref, buf_ref, send_sems, recv_sems):
        x, y, c = _place()
        me = 4 * x + 2 * y + c
        buf_ref[me] = x_ref[...]
        copies = []
        for k in range(1, N_DEV):
            peer = (x ^ ((k >> 2) & 1), y ^ ((k >> 1) & 1), c ^ (k & 1))
            copies.append(_remote(x_ref, buf_ref.at[me], send_sems, recv_sems, k - 1, peer))
        for cp in copies:
            cp.start()
        for k in range(1, N_DEV):
            _remote(x_ref, buf_ref.at[me ^ k], send_sems, recv_sems, k - 1, (x, y, c)).wait_recv()
        for cp in copies:
            cp.wait_send()
        total = buf_ref[0]
        for d in range(1, N_DEV):
            total = total + buf_ref[d]
        o_ref[...] = total

    return pl.pallas_call(
        body, name="allreduce_small", in_specs=[vmem], out_specs=vmem, out_shape=_sds((rows, width)),
        scratch_shapes=[pltpu.VMEM((N_DEV, rows, width), F32), pltpu.SemaphoreType.DMA((N_DEV - 1,)),
                        pltpu.SemaphoreType.DMA((N_DEV - 1,))],
    )(vec)


def _rows1024(a):
    return a.reshape(-1, 1024)


def _pad_rows(a, rows):
    return jnp.concatenate([a, jnp.zeros((rows - a.shape[0], a.shape[1]), a.dtype)], axis=0)


_TRANSPOSED = ("w_in", "w_gate", "w_up")
_SMALL_SHARDED = ("w_up_f", "w_up_b", "a_up_f", "a_up_b", "g_up")
_BIG_SHARDED = ("w_in", "w_out", "w_gate", "w_up", "w_down")


def _pack_weight_shards(w):
    conv_bits = lax.bitcast_convert_type(w["conv_w"], BF16).reshape(1, -1)
    conv_row = jnp.concatenate([conv_bits, jnp.zeros((1, 1024 - conv_bits.shape[1]), BF16)], axis=1)

    def rows(name):
        a = w[name].astype(BF16)
        return a.T if name in _TRANSPOSED else _rows1024(a)

    early = _pad_rows(jnp.concatenate([rows(name) for name, _ in _EARLY_ROWS[:-1]] + [conv_row], axis=0), EARLY_R)
    return early, jnp.concatenate([rows(name) for name, _ in _FFN_ROWS], axis=0)


def _split_rows(gathered, layout):
    out, row = {}, 0
    for name, n in layout:
        out[name] = gathered[:, row:row + n]
        row += n
    return out


def _unpack_early(gathered):
    out = _split_rows(gathered, _EARLY_ROWS)
    cols = lambda a, k: jnp.concatenate([a[s].reshape(k, -1) for s in range(N_SHARD)], axis=1)
    conv = lax.bitcast_convert_type(out["conv_w"][:, 0, :768].reshape(N_SHARD, 3, 128, 2), F32)
    full = dict(w_in=out["w_in"].reshape(-1, 1024).T, w_out=out["w_out"].reshape(D_MODEL, D_MODEL),
                conv_w=jnp.concatenate([conv[s] for s in range(N_SHARD)], axis=1))
    full.update({name: cols(out[name], D_GATE if name == "g_up" else D_LORA) for name in _SMALL_SHARDED})
    return full


def _unpack_ffn(gathered):
    out = _split_rows(gathered, _FFN_ROWS)
    return dict(w_gate=out["w_gate"].reshape(-1, 1024).T, w_up=out["w_up"].reshape(-1, 1024).T,
                w_down=out["w_down"].reshape(D_FF, D_MODEL))


def _pack_grads(g, layout, rows):
    col_split = lambda a, s: a[:, s * (a.shape[1] // N_SHARD):(s + 1) * (a.shape[1] // N_SHARD)]
    row_split = lambda a, s: a[s * (a.shape[0] // N_SHARD):(s + 1) * (a.shape[0] // N_SHARD)]
    by_rows = {name: (g[name].T if name in _TRANSPOSED else g[name]) for name, _ in layout if name in _BIG_SHARDED}
    used = sum(n for _, n in layout)
    parts = []
    for s in range(N_SHARD):
        for name, _ in layout:
            if name in _BIG_SHARDED:
                parts.append(row_split(by_rows[name], s))
            elif name in _SMALL_SHARDED:
                parts.append(_rows1024(col_split(g[name], s)))
            else:
                conv = col_split(g["conv_w"], s).reshape(1, -1)
                parts.append(jnp.concatenate([conv, jnp.zeros((1, 1024 - conv.shape[1]), F32)], axis=1))
        if rows > used:
            parts.append(jnp.zeros((rows - used, 1024), F32))
    return jnp.concatenate(parts, axis=0).reshape(N_SHARD, rows, 1024)


def _unpack_grad_shard(pack, layout):
    small_shapes = {name: (D_GATE if name == "g_up" else D_LORA, 128) for name in _SMALL_SHARDED}
    out, row = {}, 0
    for name, n in layout:
        piece = pack[row:row + n]
        if name == "conv_w":
            out[name] = piece[0, :384].reshape(3, 128)
        else:
            out[name] = piece.T if name in _TRANSPOSED else piece.reshape(small_shapes.get(name, piece.shape))
        row += n
    return out


_SMALL_LAYOUT = (("norm1_w", 1024), ("mu_shift", D_SHIFT), ("w0_f", 512), ("w0_b", 512), ("a0_f", 512),
                 ("a0_b", 512), ("k_k", 512), ("k_a_f", 512), ("k_a_b", 512), ("r_k_f", 512), ("r_k_b", 512),
                 ("gn_w", 512), ("gn_b", 512), ("norm2_w", 1024), ("norm_f_w", 1024), ("loss", 1))


def _pack_small(vals):
    rows = []
    for name, n in _SMALL_LAYOUT:
        flat = vals[name].reshape(-1)
        n_rows = -(-n // 1024)
        rows.append(jnp.concatenate([flat, jnp.zeros((n_rows * 1024 - n,), F32)]).reshape(n_rows, 1024))
    return _pad_rows(jnp.concatenate(rows, axis=0), SMALL_ROWS)


def _unpack_small(pack):
    out, row = {}, 0
    for name, n in _SMALL_LAYOUT:
        n_rows = -(-n // 1024)
        out[name] = pack[row:row + n_rows].reshape(-1)[:n]
        row += n_rows
    return out


_WEIGHTS = ("norm1_w", "w_in", "mu_shift", "w_up_f", "w0_f", "w_up_b", "w0_b", "a_up_f", "a0_f", "a_up_b", "a0_b",
            "g_up", "k_k", "k_a_f", "k_a_b", "r_k_f", "r_k_b", "gn_w", "gn_b", "conv_w", "w_out", "norm2_w",
            "w_gate", "w_up", "w_down", "norm_f_w")


def _train_step(x, loss_target, w, m, v):
    batch, seq, _ = x.shape
    t = batch * seq
    tm = _tile(seq, 256, 8)
    xs = x.reshape(t, D_MODEL)
    target = loss_target.reshape(t, D_MODEL)
    vec = lambda name: w[name].reshape(1, -1)

    local = {name: w[name][0] for name, _ in _PACK_ROWS}
    c = lax.axis_index("c")
    chip = 2 * lax.axis_index("x") + lax.axis_index("y")
    early, ffn_pack = _pack_weight_shards(local)
    early_all = lax.dynamic_update_slice(_gather_weights(early), early[None], (chip, 0, 0))
    early_all, ffn_pack = lax.optimization_barrier((early_all, ffn_pack))
    ffn_send, ffn_recv, ffn_pack, ffn_land, token = _fetch_start(ffn_pack)
    token, xs = lax.optimization_barrier((token, xs))
    full = _unpack_early(early_all)
    w_in = full["w_in"]
    w_shift = jnp.concatenate([w_in[:, :D_SHIFT], jnp.zeros((D_MODEL, D_SHIFT_PAD - D_SHIFT), BF16)], axis=1)
    w_conv = w_in[:, D_SHIFT:]
    zeros_lora = jnp.zeros((D_LORA, D_RWKV), F32)
    lora = lambda name: full[name].astype(F32)
    mats = (jnp.concatenate([lora("w_up_f"), zeros_lora]), jnp.concatenate([zeros_lora, lora("a_up_f")]),
            jnp.concatenate([lora("w_up_b"), zeros_lora]), jnp.concatenate([zeros_lora, lora("a_up_b")]),
            jnp.concatenate([lora("g_up"), jnp.zeros((D_GATE_PAD - D_GATE, D_RWKV), F32)]))
    mu = jnp.concatenate([vec("mu_shift"), jnp.zeros((1, D_SHIFT_PAD - D_SHIFT), F32)], axis=1)
    mu = jnp.broadcast_to(mu, (GROUP, D_SHIFT_PAD))
    zero_row = jnp.zeros((1, D_RWKV), F32)
    pvec = jnp.concatenate([vec("k_k"), vec("w0_f"), vec("a0_f"), vec("k_a_f"), vec("w0_b"), vec("a0_b"),
                            vec("k_a_b"), zero_row], axis=0)
    qvec = jnp.concatenate([vec("gn_w"), vec("gn_b"), vec("r_k_f"), vec("r_k_b"), full["conv_w"], zero_row], axis=0)
    ones_blocks = _head_ones()

    h1, h1_t = _rmsnorm_fwd(xs, vec("norm1_w"), tm, "norm1_fwd")
    p_shift = _matmul(h1, w_shift, mode="nn", name="in_proj_shift")
    pc = _matmul(h1, w_conv, mode="nn", name="in_proj_conv")
    ps = _shift_fwd(p_shift, mu, tm, seq)
    kk, w_f, kd_f, b_f, w_b, kd_b, b_b, gate = _prep_fwd(ps, pvec, mats, ones_blocks, tm)
    dirs = ((w_f, kd_f, b_f), (w_b, kd_b, b_b))
    y_f, y_b, hist_f, hist_b = _scan_fwd(ps, kk, dirs, batch, seq)
    mixed, mixed_t = _post_fwd(y_f, y_b, ps, kd_f, kd_b, gate, pc, qvec, ones_blocks, tm, seq)
    x1 = _matmul(mixed, full["w_out"], mode="nn", name="out_proj", add=xs)
    h2, h2_t = _rmsnorm_fwd(x1, vec("norm2_w"), tm, "norm2_fwd")
    ffn_all = _fetch_wait(ffn_send, ffn_recv, ffn_pack, ffn_land, h2)
    full.update(_unpack_ffn(lax.dynamic_update_slice(ffn_all, ffn_pack[None], (chip, 0, 0))))
    ff_gate, ff_up, act, act_t = _ffn_in(h2, full["w_gate"], full["w_up"])
    x2 = _matmul(act, full["w_down"], mode="nn", name="ffn_down", add=x1)
    d_x2, d_norm_f, loss_part = _loss_head(x2, w["norm_f_w"].reshape(1, -1), target, tm)

    g = {}
    g["w_down"] = _matmul(act_t, d_x2, mode="nn", name="ffn_down_dw")
    d_gate, d_up = _ffn_in_bwd(d_x2, full["w_down"], ff_gate, ff_up)
    d_h2 = _matmul(d_gate, full["w_gate"], mode="nt", name="ffn_gate_dx")
    d_h2 = _matmul(d_up, full["w_up"], mode="nt", name="ffn_up_dx", add=d_h2)
    g["w_gate"] = _matmul(h2_t, d_gate, mode="nn", name="ffn_gate_dw")
    g["w_up"] = _matmul(h2_t, d_up, mode="nn", name="ffn_up_dw")
    ffn_grads = _pack_grads(g, _FFN_ROWS, sum(n for _, n in _FFN_ROWS))
    ffn_sum, ffn_sum_bf16 = _add_halves(ffn_grads, _swap_other_half(ffn_grads, "ffn"), c, "ffn")
    ex_send, ex_recv, ffn_sum_bf16, ex_land, ex_token = _exchange_start(ffn_sum_bf16)
    ex_token, d_h2 = lax.optimization_barrier((ex_token, d_h2))
    d_x1, d_norm2 = _rmsnorm_bwd(x1, vec("norm2_w"), d_h2, d_x2, tm, "norm2_bwd")
    d_mixed = _matmul(d_x1, full["w_out"], mode="nt", name="out_proj_dx")
    g["w_out"] = _matmul(mixed_t, d_x1, mode="nn", name="out_proj_dw")
    dy, dr_o, dkdf_o, dkdb_o, dv_o, d_gatev, d_pc, d_qvec = _post_bwd(
        d_mixed, y_f, y_b, ps, kd_f, kd_b, gate, pc, qvec, ones_blocks, tm, seq)
    (dr_f, dw_f, dkd_f, dv_f, dkk_f, db_f), (dr_b, dw_b, dkd_b, dv_b, dkk_b, db_b) = _scan_bwd(
        ps, kk, dirs, dy, hist_f, hist_b, batch, seq)
    cts = [[dr_f, dr_b, dr_o], [dv_f, dv_b, dv_o], [dkk_f, dkk_b], [dw_f], [dkd_f, dkdf_o], [db_f],
           [dw_b], [dkd_b, dkdb_o], [db_b], [d_gatev]]
    q, d_pvec, d_m0, d_m1, d_m2, d_m3, d_m4 = _prep_bwd(ps, pvec, mats, ones_blocks, cts, tm)
    d_pshift, d_mu = _shift_bwd(q, p_shift, mu, tm, seq)
    d_h1 = _matmul(d_pshift, w_shift, mode="nt", name="in_proj_shift_dx")
    d_h1 = _matmul(d_pc, w_conv, mode="nt", name="in_proj_conv_dx", add=d_h1)
    d_w_shift = _matmul(h1_t, d_pshift, mode="nn", name="in_proj_shift_dw")
    d_w_conv = _matmul(h1_t, d_pc, mode="nn", name="in_proj_conv_dw")
    g["w_in"] = jnp.concatenate([d_w_shift[:, :D_SHIFT], d_w_conv], axis=1)
    d_x, d_norm1 = _rmsnorm_bwd(xs, vec("norm1_w"), d_h1, d_x1, tm, "norm1_bwd")
    g["w_up_f"], g["a_up_f"] = d_m0[:D_LORA], d_m1[D_LORA:]
    g["w_up_b"], g["a_up_b"] = d_m2[:D_LORA], d_m3[D_LORA:]
    g["g_up"] = d_m4[:D_GATE]
    g["conv_w"] = d_qvec[4:7]

    def finish(chip_sum, others, tag, layout):
        eighth = _add_quarters(chip_sum, others, chip, tag)
        other_eighth = _swap_with_sibling(eighth, "swap_eighths_" + tag)
        return _unpack_grad_shard(jnp.concatenate([jnp.where(c == 0, eighth, other_eighth),
                                                   jnp.where(c == 0, other_eighth, eighth)], axis=0), layout)

    packed = _pack_grads(g, _EARLY_ROWS, EARLY_R)
    chip_sum, chip_sum_bf16 = _add_halves(packed, _swap_other_half(packed, "mixer"), c, "mixer")
    grads = finish(chip_sum, _exchange_quarters(chip_sum_bf16), "mixer", _EARLY_ROWS)
    grads.update(finish(ffn_sum, _exchange_wait(ex_send, ex_recv, ffn_sum_bf16, ex_land, d_x), "ffn", _FFN_ROWS))

    small = dict(norm1_w=d_norm1, mu_shift=d_mu[:, :D_SHIFT], w0_f=d_pvec[1], w0_b=d_pvec[4], a0_f=d_pvec[2],
                 a0_b=d_pvec[5], k_k=d_pvec[0], k_a_f=d_pvec[3], k_a_b=d_pvec[6], r_k_f=d_qvec[2], r_k_b=d_qvec[3],
                 gn_w=d_qvec[0], gn_b=d_qvec[1], norm2_w=d_norm2, norm_f_w=d_norm_f, loss=loss_part)
    reduced = _unpack_small(_allreduce_small(_pack_small(small)))
    loss = reduced.pop("loss")[0]
    grads.update(reduced)

    outs = {}
    small = [name for name in _WEIGHTS if name not in _BIG_SHARDED]
    as2d = lambda name: (1, w[name].shape[0]) if w[name].ndim == 1 else w[name].shape
    operands = lambda name: tuple(a.reshape(as2d(name)) for a in (w[name], grads[name], m[name], v[name]))
    updates = dict(zip(small, _adamw_small([operands(name) for name in small])))
    for name in _BIG_SHARDED:
        updates[name] = _adamw(*operands(name), "adamw_" + name)
    for name in _WEIGHTS:
        shape = w[name].shape
        outs[name] = (grads[name].reshape(shape),) + tuple(a.reshape(shape) for a in updates[name])
    d_x = d_x.reshape(batch, seq, D_MODEL)
    return (loss, d_x) + tuple(outs[name][k] for k in range(4) for name in _WEIGHTS)


def kernel(x, norm1_w, w_in, mu_shift, w_up_f, w0_f, w_up_b, w0_b, a_up_f, a0_f, a_up_b, a0_b, g_up, k_k, k_a_f, k_a_b, r_k_f, r_k_b, gn_w, gn_b, conv_w, w_out, norm2_w, w_gate, w_up, w_down, norm_f_w, loss_target, m_norm1_w, m_w_in, m_mu_shift, m_w_up_f, m_w0_f, m_w_up_b, m_w0_b, m_a_up_f, m_a0_f, m_a_up_b, m_a0_b, m_g_up, m_k_k, m_k_a_f, m_k_a_b, m_r_k_f, m_r_k_b, m_gn_w, m_gn_b, m_conv_w, m_w_out, m_norm2_w, m_w_gate, m_w_up, m_w_down, m_norm_f_w, v_norm1_w, v_w_in, v_mu_shift, v_w_up_f, v_w0_f, v_w_up_b, v_w0_b, v_a_up_f, v_a0_f, v_a_up_b, v_a0_b, v_g_up, v_k_k, v_k_a_f, v_k_a_b, v_r_k_f, v_r_k_b, v_gn_w, v_gn_b, v_conv_w, v_w_out, v_norm2_w, v_w_gate, v_w_up, v_w_down, v_norm_f_w):
    args = locals()
    w = {name: args[name] for name in _WEIGHTS}
    m = {name: args["m_" + name] for name in _WEIGHTS}
    v = {name: args["v_" + name] for name in _WEIGHTS}
    return _train_step(x, loss_target, w, m, v)
```

```python
import functools

import jax
import jax.numpy as jnp
from jax import lax
from jax.experimental import pallas as pl
from jax.experimental.pallas import tpu as pltpu

F32 = jnp.float32
BF16 = jnp.bfloat16
MESH = pl.DeviceIdType.MESH

D_MODEL = 1024
D_RWKV = 512
HEAD = 64
N_PAIR = D_RWKV // (2 * HEAD)
D_LORA = 64
D_GATE = 160
D_GATE_PAD = 384
D_FF = 2816
D_SHIFT = 1824
D_SHIFT_PAD = 2048
D_CONV3 = 1536
LOG_DECAY_SCALE = 0.606531
RMS_EPS = 1e-6
GN_EPS = 64e-5
NORM_EPS = 1e-12
ADAM_LR, ADAM_B1, ADAM_B2, ADAM_EPS, ADAM_WD, ADAM_STEP = 0.001, 0.9, 0.999, 1e-08, 0.01, 10

N_SHARD = 4
N_DEV = 8
V7X_VMEM_LIMIT = 48 * 1024 * 1024
SCAN_CHUNK = 32
GROUP = 8

_PACK_ROWS = (("w_in", 840), ("w_out", 256), ("w_gate", 704), ("w_up", 704), ("w_down", 704),
              ("w_up_f", 8), ("w_up_b", 8), ("a_up_f", 8), ("a_up_b", 8), ("g_up", 20), ("conv_w", 1))
_FFN_NAMES = ("w_gate", "w_up", "w_down")
_EARLY_ROWS = tuple(item for item in _PACK_ROWS if item[0] not in _FFN_NAMES)
_FFN_ROWS = tuple(item for item in _PACK_ROWS if item[0] in _FFN_NAMES)
EARLY_R = 1152
SMALL_ROWS = 24


def _tile(n, cap, mult=128):
    best = None
    t = mult
    while t <= min(n, cap):
        if n % t == 0:
            best = t
        t += mult
    return best or n


def _cp(*sem):
    return pltpu.CompilerParams(dimension_semantics=sem or None, vmem_limit_bytes=V7X_VMEM_LIMIT)


def _sds(shape, dtype=F32):
    return jax.ShapeDtypeStruct(shape, dtype)


def _matmul(a, b, *, mode, name, out_dtype=F32, add=None):
    m, kdim = a.shape
    n = b.shape[1] if mode == "nn" else b.shape[0]
    tm, tn = _tile(m, 512, 8), _tile(n, 1536)
    tk = kdim if kdim <= 3584 else _tile(kdim, 1024)
    nk = kdim // tk
    a_spec = pl.BlockSpec((tm, tk), lambda i, j, k: (i, k))
    if mode == "nn":
        b_spec = pl.BlockSpec((tk, tn), lambda i, j, k: (k, j))
        dims = (((1,), (0,)), ((), ()))
    else:
        b_spec = pl.BlockSpec((tn, tk), lambda i, j, k: (j, k))
        dims = (((1,), (1,)), ((), ()))
    has_add = add is not None

    def body(*refs):
        a_ref, b_ref = refs[0], refs[1]
        add_ref = refs[2] if has_add else None
        o_ref = refs[3] if has_add else refs[2]
        part = lax.dot_general(a_ref[...].astype(BF16), b_ref[...].astype(BF16), dims,
                               preferred_element_type=F32)
        if nk == 1:
            if has_add:
                part = part + add_ref[...]
            o_ref[...] = part.astype(out_dtype)
        else:
            acc_ref = refs[-1]
            k = pl.program_id(2)

            @pl.when(k == 0)
            def _():
                acc_ref[...] = jnp.zeros_like(acc_ref)

            acc_ref[...] += part

            @pl.when(k == nk - 1)
            def _():
                res = acc_ref[...]
                if has_add:
                    res = res + add_ref[...]
                o_ref[...] = res.astype(out_dtype)

    o_spec = pl.BlockSpec((tm, tn), lambda i, j, k: (i, j))
    in_specs = [a_spec, b_spec] + ([o_spec] if has_add else [])
    args = (a, b) + ((add,) if has_add else ())
    return pl.pallas_call(
        body, name=name, grid=(m // tm, n // tn, nk), in_specs=in_specs, out_specs=o_spec,
        out_shape=_sds((m, n), out_dtype),
        scratch_shapes=[pltpu.VMEM((tm, tn), F32)] if nk > 1 else [],
        compiler_params=_cp("parallel", "parallel", "arbitrary"),
    )(*args)


def _row(tm, width):
    return pl.BlockSpec((tm, width), lambda i: (i, 0))


def _col(tm, height):
    return pl.BlockSpec((height, tm), lambda i: (0, i))


def _fixed(shape):
    return pl.BlockSpec(shape, lambda i: tuple(0 for _ in shape))


def _rmsnorm_fwd(x, w, tm, name):
    t, d = x.shape

    def body(x_ref, w_ref, o_ref, ot_ref):
        xv = x_ref[...]
        rstd = lax.rsqrt(jnp.mean(xv * xv, axis=-1, keepdims=True) + RMS_EPS)
        yv = xv * rstd * w_ref[...]
        o_ref[...] = yv.astype(BF16)
        ot_ref[...] = jnp.transpose(yv).astype(BF16)

    return pl.pallas_call(
        body, name=name, grid=(t // tm,), in_specs=[_row(tm, d), _fixed((1, d))],
        out_specs=[_row(tm, d), _col(tm, d)], out_shape=[_sds((t, d), BF16), _sds((d, t), BF16)],
        compiler_params=_cp("parallel"))(x, w)


def _rms_bwd_math(xv, wv, dyv):
    rstd = lax.rsqrt(jnp.mean(xv * xv, axis=-1, keepdims=True) + RMS_EPS)
    xhat = xv * rstd
    gv = dyv * wv
    dx = rstd * (gv - xhat * jnp.mean(gv * xhat, axis=-1, keepdims=True))
    return dx, jnp.sum(dyv * xhat, axis=0, keepdims=True)


def _rmsnorm_bwd(x, w, dy, dres, tm, name):
    t, d = x.shape

    def body(x_ref, w_ref, dy_ref, dres_ref, dx_ref, dw_ref):
        dx, dw = _rms_bwd_math(x_ref[...], w_ref[...], dy_ref[...])
        dx_ref[...] = dres_ref[...] + dx

        @pl.when(pl.program_id(0) == 0)
        def _():
            dw_ref[...] = jnp.zeros_like(dw_ref)

        dw_ref[...] += dw

    return pl.pallas_call(
        body, name=name, grid=(t // tm,),
        in_specs=[_row(tm, d), _fixed((1, d)), _row(tm, d), _row(tm, d)],
        out_specs=[_row(tm, d), _fixed((1, d))],
        out_shape=[_sds((t, d)), _sds((1, d))], compiler_params=_cp("arbitrary"))(x, w, dy, dres)


def _loss_head(x, w, target, tm):
    t, d = x.shape

    def body(x_ref, w_ref, t_ref, dx_ref, dw_ref, loss_ref):
        xv, wv = x_ref[...], w_ref[...]
        rstd = lax.rsqrt(jnp.mean(xv * xv, axis=-1, keepdims=True) + RMS_EPS)
        err = xv * rstd * wv - t_ref[...]
        dx, dw = _rms_bwd_math(xv, wv, err * (1.0 / d))
        dx_ref[...] = dx

        @pl.when(pl.program_id(0) == 0)
        def _():
            dw_ref[...] = jnp.zeros_like(dw_ref)
            loss_ref[...] = jnp.zeros_like(loss_ref)

        dw_ref[...] += dw
        loss_ref[...] += 0.5 * jnp.sum(jnp.mean(err * err, axis=-1, keepdims=True), axis=0, keepdims=True)

    return pl.pallas_call(
        body, name="loss_head", grid=(t // tm,),
        in_specs=[_row(tm, d), _fixed((1, d)), _row(tm, d)],
        out_specs=[_row(tm, d), _fixed((1, d)), _fixed((1, 1))],
        out_shape=[_sds((t, d)), _sds((1, d)), _sds((1, 1))], compiler_params=_cp("arbitrary"))(x, w, target)


def _ffn_in(h, w_gate, w_up):
    t, d = h.shape
    f = w_gate.shape[1]
    tm, tn = _tile(t, 512, 8), _tile(f, 1536)

    def body(h_ref, wg_ref, wu_ref, g_ref, u_ref, a_ref, at_ref):
        hv = h_ref[...]
        gv = jnp.dot(hv, wg_ref[...], preferred_element_type=F32)
        uv = jnp.dot(hv, wu_ref[...], preferred_element_type=F32)
        act = gv * jax.nn.sigmoid(gv) * uv
        g_ref[...] = gv.astype(BF16)
        u_ref[...] = uv.astype(BF16)
        a_ref[...] = act.astype(BF16)
        at_ref[...] = jnp.transpose(act).astype(BF16)

    w_spec = pl.BlockSpec((d, tn), lambda i, j: (0, j))
    o_spec = pl.BlockSpec((tm, tn), lambda i, j: (i, j))
    return pl.pallas_call(
        body, name="ffn_in", grid=(t // tm, f // tn),
        in_specs=[pl.BlockSpec((tm, d), lambda i, j: (i, 0)), w_spec, w_spec],
        out_specs=[o_spec, o_spec, o_spec, pl.BlockSpec((tn, tm), lambda i, j: (j, i))],
        out_shape=[_sds((t, f), BF16)] * 3 + [_sds((f, t), BF16)],
        compiler_params=_cp("parallel", "parallel"))(h, w_gate, w_up)


def _ffn_in_bwd(d_out, w_down, gate, up):
    t, d = d_out.shape
    f = w_down.shape[0]
    tm, tn = _tile(t, 512, 8), _tile(f, 1536)

    def body(do_ref, w_ref, g_ref, u_ref, dg_ref, du_ref):
        dv = lax.dot_general(do_ref[...].astype(BF16), w_ref[...], (((1,), (1,)), ((), ())),
                             preferred_element_type=F32)
        gv, uv = g_ref[...].astype(F32), u_ref[...].astype(F32)
        sg = jax.nn.sigmoid(gv)
        du_ref[...] = (dv * gv * sg).astype(BF16)
        dg_ref[...] = (dv * uv * (sg * (1.0 + gv * (1.0 - sg)))).astype(BF16)

    tile = pl.BlockSpec((tm, tn), lambda i, j: (i, j))
    return pl.pallas_call(
        body, name="ffn_in_bwd", grid=(t // tm, f // tn),
        in_specs=[pl.BlockSpec((tm, d), lambda i, j: (i, 0)), pl.BlockSpec((tn, d), lambda i, j: (j, 0)), tile, tile],
        out_specs=[tile, tile], out_shape=[_sds((t, f), BF16)] * 2,
        compiler_params=_cp("parallel", "parallel"))(d_out, w_down, gate, up)


def _halo_specs(tm, width, rows_total):
    per = tm // GROUP
    last = rows_total // GROUP - 1
    prev = pl.BlockSpec((GROUP, width), lambda i: (jnp.maximum(i * per - 1, 0), 0))
    nxt = pl.BlockSpec((GROUP, width), lambda i: (jnp.minimum((i + 1) * per, last), 0))
    return prev, nxt


def _edge_flags(tm, seq):
    i = pl.program_id(0)
    has_prev = jnp.where((i * tm) % seq == 0, 0.0, 1.0).astype(F32)
    has_next = jnp.where(((i + 1) * tm) % seq == 0, 0.0, 1.0).astype(F32)
    return has_prev, has_next


def _shifted(xv, prev_row, next_row):
    tm = xv.shape[0]
    row = lax.broadcasted_iota(jnp.int32, xv.shape, 0)
    down = jnp.where(row == 0, prev_row, pltpu.roll(xv, 1, axis=0))
    up = jnp.where(row == tm - 1, next_row, pltpu.roll(xv, tm - 1, axis=0))
    return down, up


def _shift_fwd(p, mu, tm, seq):
    t, w = p.shape
    prev_spec, next_spec = _halo_specs(tm, w, t)

    def body(p_ref, hp_ref, hn_ref, mu_ref, o_ref):
        has_prev, has_next = _edge_flags(tm, seq)
        xv = p_ref[...]
        down, up = _shifted(xv, hp_ref[GROUP - 1:GROUP, :] * has_prev, hn_ref[0:1, :] * has_next)
        o_ref[...] = xv + mu_ref[0:1, :] * (0.5 * (down + up) - xv)

    return pl.pallas_call(
        body, name="shift_fwd", grid=(t // tm,),
        in_specs=[_row(tm, w), prev_spec, next_spec, _fixed((GROUP, w))], out_specs=_row(tm, w),
        out_shape=_sds((t, w)), compiler_params=_cp("parallel"))(p, p, p, mu)


def _shift_bwd(q, p, mu, tm, seq):
    t, w = p.shape
    prev_spec, next_spec = _halo_specs(tm, w, t)

    def body(q_ref, qp_ref, qn_ref, p_ref, pp_ref, pn_ref, mu_ref, dp_ref, dmu_ref):
        has_prev, has_next = _edge_flags(tm, seq)
        muv = mu_ref[0:1, :]
        qv = q_ref[...]
        mq = muv * qv
        mq_down, mq_up = _shifted(mq, muv * qp_ref[GROUP - 1:GROUP, :] * has_prev,
                                  muv * qn_ref[0:1, :] * has_next)
        dp_ref[...] = (qv - mq + 0.5 * (mq_down + mq_up)).astype(BF16)
        pv = p_ref[...]
        p_down, p_up = _shifted(pv, pp_ref[GROUP - 1:GROUP, :] * has_prev, pn_ref[0:1, :] * has_next)

        @pl.when(pl.program_id(0) == 0)
        def _():
            dmu_ref[...] = jnp.zeros_like(dmu_ref)

        dmu_ref[...] += jnp.sum(qv * (0.5 * (p_down + p_up) - pv), axis=0, keepdims=True)

    return pl.pallas_call(
        body, name="shift_bwd", grid=(t // tm,),
        in_specs=[_row(tm, w), prev_spec, next_spec, _row(tm, w), prev_spec, next_spec, _fixed((GROUP, w))],
        out_specs=[_row(tm, w), _fixed((1, w))],
        out_shape=[_sds((t, w), BF16), _sds((1, w))], compiler_params=_cp("arbitrary"))(q, q, q, p, p, p, mu)


@jax.custom_vjp
def _bdot(a, b):
    return jnp.dot(a.astype(BF16), b.astype(BF16), preferred_element_type=F32)


def _bdot_fwd(a, b):
    return _bdot(a, b), (a, b)


def _bdot_bwd(res, g):
    a, b = res
    gb = g.astype(BF16)
    da = lax.dot_general(gb, b.astype(BF16), (((1,), (1,)), ((), ())), preferred_element_type=F32)
    db = lax.dot_general(a.astype(BF16), gb, (((0,), (0,)), ((), ())), preferred_element_type=F32)
    return da, db


_bdot.defvjp(_bdot_fwd, _bdot_bwd)


def _seg_raw(x, ones_blocks):
    hi = x.astype(BF16)
    lo = (x - hi.astype(F32)).astype(BF16)
    return (jnp.dot(hi, ones_blocks, preferred_element_type=F32)
            + jnp.dot(lo, ones_blocks, preferred_element_type=F32))


@jax.custom_vjp
def _seg(x, ones_blocks):
    return _seg_raw(x, ones_blocks)


def _seg_fwd(x, ones_blocks):
    return _seg_raw(x, ones_blocks), ones_blocks


def _seg_bwd(ones_blocks, g):
    return _seg_raw(g, ones_blocks), jnp.zeros_like(ones_blocks)


_seg.defvjp(_seg_fwd, _seg_bwd)


def _head_ones():
    h = jnp.arange(D_RWKV) // HEAD
    return (h[:, None] == h[None, :]).astype(BF16)


def _prep_math(ps, k_k, w0_f, a0_f, k_a_f, w0_b, a0_b, k_a_b, wup_f, aup_f, wup_b, aup_b, gup, ones_blocks):
    r = ps[:, 0:512]
    k = ps[:, 512:1024]
    v = ps[:, 1024:1536]
    xwa = ps[:, 1536:1664]
    xg = ps[:, 1664:D_SHIFT_PAD]
    kk_raw = k * k_k
    norm = jnp.sqrt(_seg(kk_raw * kk_raw, ones_blocks))
    kk = kk_raw / jnp.maximum(norm, NORM_EPS)
    t_xwa = jnp.tanh(xwa)
    outs = [r, v, kk]
    for w0, a0, k_a, wup, aup in ((w0_f, a0_f, k_a_f, wup_f, aup_f), (w0_b, a0_b, k_a_b, wup_b, aup_b)):
        decay = jnp.exp(-LOG_DECAY_SCALE * jax.nn.sigmoid(w0 + _bdot(t_xwa, wup)))
        rate = jax.nn.sigmoid(a0 + _bdot(xwa, aup))
        outs += [decay, k * (1.0 + (rate - 1.0) * k_a), kk * rate]
    outs.append(_bdot(jax.nn.sigmoid(xg), gup))
    return tuple(outs)


def _prep_args(tm, ps_ref, pv_ref, mat_refs, ones_ref):
    vecs = [jnp.broadcast_to(pv_ref[j:j + 1, :], (tm, D_RWKV)) for j in range(7)]
    return [ps_ref[...]] + vecs + [m[...] for m in mat_refs] + [ones_ref[...]]


_PREP_MAT_SHAPES = ((128, D_RWKV),) * 4 + ((D_GATE_PAD, D_RWKV),)


def _prep_fwd(ps, pvec, mats, ones_blocks, tm):
    t = ps.shape[0]

    def body(ps_ref, pv_ref, m0, m1, m2, m3, m4, ones_ref, *out_refs):
        outs = _prep_math(*_prep_args(tm, ps_ref, pv_ref, (m0, m1, m2, m3, m4), ones_ref))
        for o_ref, val in zip(out_refs, outs[2:]):
            o_ref[...] = val

    return pl.pallas_call(
        body, name="prep_fwd", grid=(t // tm,),
        in_specs=[_row(tm, D_SHIFT_PAD), _fixed((8, D_RWKV))] + [_fixed(s) for s in _PREP_MAT_SHAPES]
        + [_fixed((D_RWKV, D_RWKV))],
        out_specs=[_row(tm, D_RWKV)] * 8, out_shape=[_sds((t, D_RWKV))] * 8,
        compiler_params=_cp("parallel"))(ps, pvec, *mats, ones_blocks)


def _prep_bwd(ps, pvec, mats, ones_blocks, cts, tm):
    t = ps.shape[0]
    counts = [len(c) for c in cts]
    flat = [a for c in cts for a in c]

    def body(ps_ref, pv_ref, m0, m1, m2, m3, m4, ones_ref, *refs):
        ct_refs = refs[:len(flat)]
        q_ref, dpv_ref = refs[len(flat)], refs[len(flat) + 1]
        dmat_refs = refs[len(flat) + 2:]
        args = _prep_args(tm, ps_ref, pv_ref, (m0, m1, m2, m3, m4), ones_ref)
        _, vjp = jax.vjp(lambda *a: _prep_math(*a, args[-1]), *args[:-1])
        ct_vals, pos = [], 0
        for n in counts:
            val = ct_refs[pos][...]
            for extra in ct_refs[pos + 1:pos + n]:
                val = val + extra[...]
            ct_vals.append(val)
            pos += n
        grads = vjp(tuple(ct_vals))
        q_ref[...] = grads[0]

        @pl.when(pl.program_id(0) == 0)
        def _():
            dpv_ref[...] = jnp.zeros_like(dpv_ref)
            for d_ref in dmat_refs:
                d_ref[...] = jnp.zeros_like(d_ref)

        for j in range(7):
            dpv_ref[j:j + 1, :] += jnp.sum(grads[1 + j], axis=0, keepdims=True)
        for d_ref, gm in zip(dmat_refs, grads[8:13]):
            d_ref[...] += gm

    return pl.pallas_call(
        body, name="prep_bwd", grid=(t // tm,),
        in_specs=[_row(tm, D_SHIFT_PAD), _fixed((8, D_RWKV))] + [_fixed(s) for s in _PREP_MAT_SHAPES]
        + [_fixed((D_RWKV, D_RWKV))] + [_row(tm, D_RWKV)] * len(flat),
        out_specs=[_row(tm, D_SHIFT_PAD), _fixed((8, D_RWKV))] + [_fixed(s) for s in _PREP_MAT_SHAPES],
        out_shape=[_sds((t, D_SHIFT_PAD)), _sds((8, D_RWKV))] + [_sds(s) for s in _PREP_MAT_SHAPES],
        compiler_params=_cp("arbitrary"))(ps, pvec, *mats, ones_blocks, *flat)


def _pair_ones():
    h = jnp.arange(2 * HEAD) // HEAD
    block = (h[:, None] == h[None, :]).astype(BF16)
    return jnp.concatenate([block, block], axis=0)


def _diag_mask():
    lane = lax.broadcasted_iota(jnp.int32, (HEAD, 2 * HEAD), 1)
    sub = lax.broadcasted_iota(jnp.int32, (HEAD, 2 * HEAD), 0)
    return jnp.where((lane & (HEAD - 1)) == sub, 1.0, 0.0).astype(F32)


def _to_row(cols, dmask):
    return jnp.sum(cols * dmask, axis=0, keepdims=True)


def _seg_many(exact, rounded, ones_pair):
    out_exact, out_rounded = [], []
    if exact:
        parts = []
        for x in exact:
            hi = x.astype(BF16)
            parts.append(jnp.concatenate([hi, (x - hi.astype(F32)).astype(BF16)], axis=1))
        res = jnp.dot(jnp.concatenate(parts, axis=0), ones_pair, preferred_element_type=F32)
        out_exact = [res[HEAD * c:HEAD * (c + 1)] for c in range(len(exact))]
    if rounded:
        res = jnp.dot(jnp.concatenate([x.astype(BF16) for x in rounded], axis=0), ones_pair[0:2 * HEAD],
                      preferred_element_type=F32)
        out_rounded = [res[HEAD * c:HEAD * (c + 1)] for c in range(len(rounded))]
    return out_exact, out_rounded


N_CHAIN = 2 * N_PAIR


def _chain(c):
    d, p = divmod(c, N_PAIR)
    return d, slice(2 * HEAD * p, 2 * HEAD * (p + 1))


def _scan_specs(n_chunks, col_blocks, fwd_chunk, bwd_chunk):
    def spec(chunk_of, col):
        return pl.BlockSpec((SCAN_CHUNK, D_RWKV), lambda b, g: (b * n_chunks + chunk_of(g), col))
    return [spec(fwd_chunk, c) for c in col_blocks] + [spec(bwd_chunk, c) for c in col_blocks]


def _scan_fwd(ps, kk, dirs, batch, seq):
    t = batch * seq
    n = seq // SCAN_CHUNK
    groups = SCAN_CHUNK // GROUP
    up = lambda g: g
    down = lambda g: n - 1 - g
    col_blocks = (0, 2, 0, 0, 0, 0)

    def body(*refs):
        dir_refs = (refs[0:6], refs[6:12])
        ones_ref = refs[12]
        y_refs, hist_refs, st_ref = refs[13:15], refs[15:17], refs[17]

        @pl.when(pl.program_id(1) == 0)
        def _():
            st_ref[...] = jnp.zeros_like(st_ref)

        ones_pair = ones_ref[...]
        dmask = _diag_mask()
        dmask_b = dmask.astype(BF16)
        sub8 = lax.broadcasted_iota(jnp.int32, (GROUP, 2 * HEAD), 0)

        def group(gi, carry):
            off = (pl.multiple_of(gi * GROUP, GROUP), pl.multiple_of((groups - 1 - gi) * GROUP, GROUP))
            loaded = [tuple(ref[pl.ds(off[d], GROUP), :] for ref in dir_refs[d]) for d in range(2)]
            states = list(carry)
            y_acc = [jnp.zeros((GROUP, 2 * HEAD), F32) for _ in range(N_CHAIN)]
            for step in range(GROUP):
                rows, idx = [], []
                for c in range(N_CHAIN):
                    d, lanes = _chain(c)
                    i = step if d == 0 else GROUP - 1 - step
                    idx.append(i)
                    rows.append(tuple(x8[i:i + 1, lanes] for x8 in loaded[d]))
                    hist_refs[d][c % N_PAIR, gi * GROUP + step] = states[c]
                _, v_cols = _seg_many([], [dmask_b * rows[c][1].astype(BF16) for c in range(N_CHAIN)], ones_pair)
                sas, _ = _seg_many([states[c] * rows[c][2] for c in range(N_CHAIN)], [], ones_pair)
                for c in range(N_CHAIN):
                    _, _, _, w_row, kd_row, b_row = rows[c]
                    states[c] = states[c] * w_row - sas[c] * b_row + v_cols[c] * kd_row
                _, ys = _seg_many([], [states[c] * rows[c][0] for c in range(N_CHAIN)], ones_pair)
                for c in range(N_CHAIN):
                    y_acc[c] = jnp.where(sub8 == idx[c], _to_row(ys[c], dmask), y_acc[c])
            for c in range(N_CHAIN):
                d, lanes = _chain(c)
                y_refs[d][pl.ds(off[d], GROUP), lanes] = y_acc[c]
            return tuple(states)

        final = lax.fori_loop(0, groups, group, tuple(st_ref[c] for c in range(N_CHAIN)))
        for c in range(N_CHAIN):
            st_ref[c] = final[c]
            hist_refs[c // N_PAIR][c % N_PAIR, SCAN_CHUNK] = final[c]

    y_spec_f = pl.BlockSpec((SCAN_CHUNK, D_RWKV), lambda b, g: (b * n + up(g), 0))
    y_spec_b = pl.BlockSpec((SCAN_CHUNK, D_RWKV), lambda b, g: (b * n + down(g), 0))
    hist_shape = (batch, n, N_PAIR, SCAN_CHUNK + 1, HEAD, 2 * HEAD)
    hist_block = (None, None, N_PAIR, SCAN_CHUNK + 1, HEAD, 2 * HEAD)
    hist_spec_f = pl.BlockSpec(hist_block, lambda b, g: (b, up(g), 0, 0, 0, 0))
    hist_spec_b = pl.BlockSpec(hist_block, lambda b, g: (b, down(g), 0, 0, 0, 0))
    ones_spec = pl.BlockSpec((4 * HEAD, 2 * HEAD), lambda b, g: (0, 0))
    (wf, kdf, bf), (wb, kdb, bb) = dirs
    return pl.pallas_call(
        body, name="wkv_fwd", grid=(batch, n),
        in_specs=_scan_specs(n, col_blocks, up, down) + [ones_spec],
        out_specs=[y_spec_f, y_spec_b, hist_spec_f, hist_spec_b],
        out_shape=[_sds((t, D_RWKV)), _sds((t, D_RWKV)), _sds(hist_shape), _sds(hist_shape)],
        scratch_shapes=[pltpu.VMEM((N_CHAIN, HEAD, 2 * HEAD), F32)],
        compiler_params=_cp("parallel", "arbitrary"),
    )(ps, ps, kk, wf, kdf, bf, ps, ps, kk, wb, kdb, bb, _pair_ones())


def _scan_bwd(ps, kk, dirs, dy, hist_f, hist_b, batch, seq):
    t = batch * seq
    n = seq // SCAN_CHUNK
    groups = SCAN_CHUNK // GROUP
    fwd_chunk = lambda g: n - 1 - g
    bwd_chunk = lambda g: g
    col_blocks = (0, 2, 0, 0, 0, 0, 0)

    def undo_group(dir_refs, out_refs, hist_refs, gi, d_states, ones_pair, dmask, sub8):
        d_states = list(d_states)
        loaded, blocks = [], []
        for d in range(2):
            blk = groups - 1 - gi if d == 0 else gi
            blocks.append(pl.ds(pl.multiple_of(blk * GROUP, GROUP), GROUP))
            r8, v8, kk8, w8, kd8, b8, dy8 = (ref[blocks[d], :] for ref in dir_refs[d])
            loaded.append((r8, v8, kk8, w8, kd8, -b8, dy8))
        acc = [[jnp.zeros((GROUP, 2 * HEAD), F32) for _ in range(6)] for _ in range(N_CHAIN)]
        for step in range(GROUP):
            rows, idx, before, after = [], [], [], []
            for c in range(N_CHAIN):
                d, lanes = _chain(c)
                i = GROUP - 1 - step if d == 0 else step
                q = (groups - 1 - gi) * GROUP + i if d == 0 else SCAN_CHUNK - 1 - (gi * GROUP + i)
                idx.append(i)
                rows.append(tuple(x8[i:i + 1, lanes] for x8 in loaded[d]))
                before.append(hist_refs[d][c % N_PAIR, q])
                after.append(hist_refs[d][c % N_PAIR, q + 1])
            _, cols = _seg_many([], [dmask.astype(BF16) * rows[c][j].astype(BF16) for c in range(N_CHAIN) for j in (1, 6)],
                                ones_pair)
            v_cols, dy_cols = cols[0::2], cols[1::2]
            d_now = [d_states[c] + dy_cols[c] * rows[c][0] for c in range(N_CHAIN)]
            d_sas, _ = _seg_many([d_now[c] * rows[c][5] for c in range(N_CHAIN)], [], ones_pair)
            _, others = _seg_many(
                [], [x for c in range(N_CHAIN) for x in (before[c] * rows[c][2], d_now[c] * rows[c][4])], ones_pair)
            for c in range(N_CHAIN):
                sa, d_sa, dv_cols = others[2 * c], d_sas[c], others[2 * c + 1]
                rows_out = (
                    jnp.sum(after[c] * dy_cols[c], axis=0, keepdims=True),
                    jnp.sum(d_now[c] * before[c], axis=0, keepdims=True),
                    jnp.sum(d_now[c] * v_cols[c], axis=0, keepdims=True),
                    _to_row(dv_cols, dmask),
                    jnp.sum(before[c] * d_sa, axis=0, keepdims=True),
                    -jnp.sum(d_now[c] * sa, axis=0, keepdims=True),
                )
                acc[c] = [jnp.where(sub8 == idx[c], val, a) for val, a in zip(rows_out, acc[c])]
                d_states[c] = d_now[c] * rows[c][3] + d_sa * rows[c][2]
        for c in range(N_CHAIN):
            d, lanes = _chain(c)
            for o_ref, val in zip(out_refs[d], acc[c]):
                o_ref[blocks[d], lanes] = val
        return tuple(d_states)

    def body(*refs):
        dir_refs = (refs[0:7], refs[7:14])
        hist_refs, ones_ref = refs[14:16], refs[16]
        out_refs = (refs[17:23], refs[23:29])
        dst_ref = refs[29]

        @pl.when(pl.program_id(1) == 0)
        def _():
            dst_ref[...] = jnp.zeros_like(dst_ref)

        ones_pair = ones_ref[...]
        dmask = _diag_mask()
        sub8 = lax.broadcasted_iota(jnp.int32, (GROUP, 2 * HEAD), 0)

        def group(gi, carry):
            return undo_group(dir_refs, out_refs, hist_refs, gi, carry, ones_pair, dmask, sub8)

        final = lax.fori_loop(0, groups, group, tuple(dst_ref[c] for c in range(N_CHAIN)))
        for c in range(N_CHAIN):
            dst_ref[c] = final[c]

    blk = (SCAN_CHUNK, D_RWKV)
    out_f = pl.BlockSpec(blk, lambda b, g: (b * n + fwd_chunk(g), 0))
    out_b = pl.BlockSpec(blk, lambda b, g: (b * n + bwd_chunk(g), 0))
    hist_block = (None, None, N_PAIR, SCAN_CHUNK + 1, HEAD, 2 * HEAD)
    hist_spec_f = pl.BlockSpec(hist_block, lambda b, g: (b, fwd_chunk(g), 0, 0, 0, 0))
    hist_spec_b = pl.BlockSpec(hist_block, lambda b, g: (b, bwd_chunk(g), 0, 0, 0, 0))
    ones_spec = pl.BlockSpec((4 * HEAD, 2 * HEAD), lambda b, g: (0, 0))
    (wf, kdf, bf), (wb, kdb, bb) = dirs
    outs = pl.pallas_call(
        body, name="wkv_bwd", grid=(batch, n),
        in_specs=_scan_specs(n, col_blocks, fwd_chunk, bwd_chunk) + [hist_spec_f, hist_spec_b, ones_spec],
        out_specs=[out_f] * 6 + [out_b] * 6,
        out_shape=[_sds((t, D_RWKV))] * 12,
        scratch_shapes=[pltpu.VMEM((N_CHAIN, HEAD, 2 * HEAD), F32)],
        compiler_params=_cp("parallel", "arbitrary"),
    )(ps, ps, kk, wf, kdf, bf, dy, ps, ps, kk, wb, kdb, bb, dy, hist_f, hist_b, _pair_ones())
    return outs[0:6], outs[6:12]


def _post_math(y, r, kd_f, kd_b, v, gate, gn_w, gn_b, rk_f, rk_b, ones_blocks):
    mean = _seg(y, ones_blocks) * (1.0 / HEAD)
    yc = y - mean
    var = _seg(yc * yc, ones_blocks) * (1.0 / HEAD)
    yn = yc * lax.rsqrt(var + GN_EPS) * gn_w + gn_b
    bonus = _seg(r * kd_f * rk_f, ones_blocks) * v + _seg(r * kd_b * rk_b, ones_blocks) * v
    return (yn + bonus) * gate


def _conv_parts(pc, halo_prev, halo_next, has_prev, has_next):
    gate_b, gate_c, hid = pc[:, 0:512], pc[:, 512:1024], pc[:, 1024:1536]
    u = gate_c * hid
    u_prev_row = halo_prev[GROUP - 1:GROUP, 512:1024] * halo_prev[GROUP - 1:GROUP, 1024:1536] * has_prev
    u_next_row = halo_next[0:1, 512:1024] * halo_next[0:1, 1024:1536] * has_next
    u_down, u_up = _shifted(u, u_prev_row, u_next_row)
    return gate_b, gate_c, hid, u, u_down, u_up


def _post_specs(tm, t):
    pc_prev, pc_next = _halo_specs(tm, D_CONV3, t)
    col = lambda c: pl.BlockSpec((tm, D_RWKV), lambda i: (i, c))
    return ([col(0), col(0), col(0), col(0), col(0), col(2), col(0), _row(tm, D_CONV3), pc_prev, pc_next,
             _fixed((8, D_RWKV)), _fixed((D_RWKV, D_RWKV))])


def _post_fwd(y_f, y_b, ps, kd_f, kd_b, gate, pc, qvec, ones_blocks, tm, seq):
    t = ps.shape[0]

    def body(yf_ref, yb_ref, r_ref, kdf_ref, kdb_ref, v_ref, g_ref, pc_ref, hp_ref, hn_ref, qv_ref, ones_ref,
             o_ref, ot_ref):
        has_prev, has_next = _edge_flags(tm, seq)
        vec = [jnp.broadcast_to(qv_ref[j:j + 1, :], (tm, D_RWKV)) for j in range(7)]
        o_rwkv = _post_math(yf_ref[...] + yb_ref[...], r_ref[...], kdf_ref[...], kdb_ref[...], v_ref[...],
                            g_ref[...], vec[0], vec[1], vec[2], vec[3], ones_ref[...])
        gate_b, _, _, u, u_down, u_up = _conv_parts(pc_ref[...], hp_ref[...], hn_ref[...], has_prev, has_next)
        o_conv = gate_b * (vec[4] * u_down + vec[5] * u + vec[6] * u_up)
        for half, val in enumerate((o_rwkv, o_conv)):
            o_ref[:, D_RWKV * half:D_RWKV * (half + 1)] = val.astype(BF16)
            ot_ref[D_RWKV * half:D_RWKV * (half + 1), :] = jnp.transpose(val).astype(BF16)

    return pl.pallas_call(
        body, name="post_fwd", grid=(t // tm,), in_specs=_post_specs(tm, t),
        out_specs=[_row(tm, D_MODEL), _col(tm, D_MODEL)],
        out_shape=[_sds((t, D_MODEL), BF16), _sds((D_MODEL, t), BF16)], compiler_params=_cp("parallel"),
    )(y_f, y_b, ps, kd_f, kd_b, ps, gate, pc, pc, pc, qvec, ones_blocks)


def _post_bwd(d_out, y_f, y_b, ps, kd_f, kd_b, gate, pc, qvec, ones_blocks, tm, seq):
    t = ps.shape[0]
    do_prev, do_next = _halo_specs(tm, D_MODEL, t)

    def body(do_ref, dop_ref, don_ref, yf_ref, yb_ref, r_ref, kdf_ref, kdb_ref, v_ref, g_ref, pc_ref, hp_ref,
             hn_ref, qv_ref, ones_ref, dy_ref, dr_ref, dkdf_ref, dkdb_ref, dv_ref, dg_ref, dpc_ref, dqv_ref):
        has_prev, has_next = _edge_flags(tm, seq)
        vec = [jnp.broadcast_to(qv_ref[j:j + 1, :], (tm, D_RWKV)) for j in range(7)]
        ones_v = ones_ref[...]
        args = (yf_ref[...] + yb_ref[...], r_ref[...], kdf_ref[...], kdb_ref[...], v_ref[...], g_ref[...],
                vec[0], vec[1], vec[2], vec[3])
        _, vjp = jax.vjp(lambda *a: _post_math(*a, ones_v), *args)
        grads = vjp(do_ref[:, 0:D_RWKV])
        for o_ref, gval in zip((dy_ref, dr_ref, dkdf_ref, dkdb_ref, dv_ref, dg_ref), grads[0:6]):
            o_ref[...] = gval

        hp, hn = hp_ref[...], hn_ref[...]
        gate_b, gate_c, hid, u, u_down, u_up = _conv_parts(pc_ref[...], hp, hn, has_prev, has_next)
        d_oc = do_ref[:, D_RWKV:2 * D_RWKV]
        d_cu = d_oc * gate_b
        d_cu_prev = dop_ref[GROUP - 1:GROUP, D_RWKV:2 * D_RWKV] * hp[GROUP - 1:GROUP, 0:512] * has_prev
        d_cu_next = don_ref[0:1, D_RWKV:2 * D_RWKV] * hn[0:1, 0:512] * has_next
        d_cu_down, d_cu_up = _shifted(d_cu, d_cu_prev, d_cu_next)
        d_u = vec[5] * d_cu + vec[4] * d_cu_up + vec[6] * d_cu_down
        dpc_ref[:, 0:512] = (d_oc * (vec[4] * u_down + vec[5] * u + vec[6] * u_up)).astype(BF16)
        dpc_ref[:, 512:1024] = (d_u * hid).astype(BF16)
        dpc_ref[:, 1024:1536] = (d_u * gate_c).astype(BF16)

        @pl.when(pl.program_id(0) == 0)
        def _():
            dqv_ref[...] = jnp.zeros_like(dqv_ref)

        vec_grads = list(grads[6:10]) + [d_cu * u_down, d_cu * u, d_cu * u_up]
        for j, gval in enumerate(vec_grads):
            dqv_ref[j:j + 1, :] += jnp.sum(gval, axis=0, keepdims=True)

    return pl.pallas_call(
        body, name="post_bwd", grid=(t // tm,),
        in_specs=[_row(tm, D_MODEL), do_prev, do_next] + _post_specs(tm, t),
        out_specs=[_row(tm, D_RWKV)] * 6 + [_row(tm, D_CONV3), _fixed((8, D_RWKV))],
        out_shape=[_sds((t, D_RWKV))] * 6 + [_sds((t, D_CONV3), BF16), _sds((8, D_RWKV))],
        compiler_params=_cp("arbitrary"),
    )(d_out, d_out, d_out, y_f, y_b, ps, kd_f, kd_b, ps, gate, pc, pc, pc, qvec, ones_blocks)


def _adamw_math(wv, gv, mv, vv):
    m2 = ADAM_B1 * mv + (1.0 - ADAM_B1) * gv
    v2 = ADAM_B2 * vv + (1.0 - ADAM_B2) * (gv * gv)
    m_hat = m2 / (1.0 - ADAM_B1 ** ADAM_STEP)
    v_hat = v2 / (1.0 - ADAM_B2 ** ADAM_STEP)
    return -ADAM_LR * (m_hat / (jnp.sqrt(v_hat) + ADAM_EPS) + ADAM_WD * wv), m2, v2


def _adamw_small(items):
    n = len(items)

    def body(*refs):
        ins, outs = refs[:4 * n], refs[4 * n:]
        for k in range(n):
            w_ref, g_ref, m_ref, v_ref = ins[4 * k:4 * k + 4]
            for o_ref, val in zip(outs[3 * k:3 * k + 3], _adamw_math(w_ref[...], g_ref[...], m_ref[...], v_ref[...])):
                o_ref[...] = val

    flat = [a for item in items for a in item]
    outs = pl.pallas_call(
        body, name="adamw_small", out_shape=[_sds(item[0].shape) for item in items for _ in range(3)],
        compiler_params=_cp())(*flat)
    return [tuple(outs[3 * k:3 * k + 3]) for k in range(n)]


def _adamw(w, g, m, v, name):
    r, c = w.shape[-2:]
    tr = _tile(r, 256, 8)
    if w.ndim == 3:
        spec = pl.BlockSpec((None, tr, c), lambda i: (0, i, 0))
    else:
        spec = pl.BlockSpec((tr, c), lambda i: (i, 0))

    def body(w_ref, g_ref, m_ref, v_ref, d_ref, nm_ref, nv_ref):
        d_ref[...], nm_ref[...], nv_ref[...] = _adamw_math(w_ref[...], g_ref[...], m_ref[...], v_ref[...])

    return pl.pallas_call(
        body, name=name, grid=(r // tr,), in_specs=[spec] * 4, out_specs=[spec] * 3,
        out_shape=[_sds(w.shape)] * 3, compiler_params=_cp("parallel"))(w, g, m, v)


_ANY = pl.BlockSpec(memory_space=pl.ANY)


def _place():
    return lax.axis_index("x"), lax.axis_index("y"), lax.axis_index("c")


def _other_chips(x, y):
    return [(1 - x, y), (x, 1 - y), (1 - x, 1 - y)]


def _remote(src, dst, send_sems, recv_sems, k, to):
    return pltpu.make_async_remote_copy(src_ref=src, dst_ref=dst, send_sem=send_sems.at[k],
                                        recv_sem=recv_sems.at[k], device_id=to, device_id_type=MESH)


def _gather_weights(pack):
    rows, width = pack.shape
    half = rows // 2

    def body(x_ref, out_ref, send_sems, recv_sems):
        x, y, c = _place()
        sibling = (x, y, 1 - c)
        chips = _other_chips(x, y)

        def block(chip, part):
            return out_ref.at[2 * chip[0] + chip[1], pl.ds(part * half, half), :]

        first = [_remote(x_ref.at[pl.ds(c * half, half), :], block((x, y), c), send_sems, recv_sems, j, (*chip, c))
                 for j, chip in enumerate(chips)]
        for cp in first:
            cp.start()
        passed = [_remote(block(chip, c), block(chip, c), send_sems, recv_sems, 3 + j, sibling)
                  for j, chip in enumerate(chips)]
        for j, chip in enumerate(chips):
            _remote(block(chip, c), block(chip, c), send_sems, recv_sems, j, sibling).wait_recv()
            passed[j].start()
        for j, chip in enumerate(chips):
            _remote(block(chip, 1 - c), block(chip, 1 - c), send_sems, recv_sems, 3 + j, sibling).wait_recv()
        for cp in first + passed:
            cp.wait_send()

    return pl.pallas_call(
        body, name="gather_weights", in_specs=[_ANY], out_specs=_ANY,
        out_shape=_sds((N_SHARD, rows, width), pack.dtype),
        scratch_shapes=[pltpu.SemaphoreType.DMA((6,)), pltpu.SemaphoreType.DMA((6,))],
    )(pack)


_HBM = pl.BlockSpec(memory_space=pltpu.HBM)
_SEMS = pl.BlockSpec(memory_space=pltpu.SEMAPHORE)
_DATAFLOW = pltpu.SideEffectType.DATAFLOW_SIDE_EFFECTING


def _fetch_start(pack, after):
    def body(x_ref, land_ref, after_ref, send_sems, recv_sems, x_thru, land_thru, token):
        x, y, c = _place()
        for j, chip in enumerate(_other_chips(x, y)):
            _remote(x_ref, land_ref.at[2 * x + y], send_sems, recv_sems, j, (*chip, c)).start()
        token[...] = jnp.zeros_like(token)

    land = lax.empty((N_SHARD,) + pack.shape, pack.dtype)
    return pl.pallas_call(
        body, name="fetch_ffn_start",
        out_shape=(pltpu.SemaphoreType.DMA((3,)), pltpu.SemaphoreType.DMA((3,)), pltpu.HBM(pack.shape, pack.dtype),
                   pltpu.HBM(land.shape, land.dtype), _sds((8, 128))),
        in_specs=(_HBM, _HBM, _ANY), out_specs=(_SEMS, _SEMS, _HBM, _HBM, pl.BlockSpec(memory_space=pltpu.VMEM)),
        input_output_aliases={0: 2, 1: 3}, compiler_params=pltpu.CompilerParams(has_side_effects=_DATAFLOW),
    )(pltpu.with_memory_space_constraint(pack, pltpu.HBM), pltpu.with_memory_space_constraint(land, pltpu.HBM), after)


def _fetch_wait(send_sems, recv_sems, pack_thru, land_thru, after):
    def body(x_ref, land_ref, send_sems, recv_sems, after_ref, x_dead, got_ref):
        x, y, c = _place()
        for j, chip in enumerate(_other_chips(x, y)):
            cp = _remote(x_ref, land_ref.at[2 * chip[0] + chip[1]], send_sems, recv_sems, j, (*chip, c))
            cp.wait_send()
            cp.wait_recv()

    return pl.pallas_call(
        body, name="fetch_ffn_wait",
        out_shape=(pltpu.HBM(pack_thru.shape, pack_thru.dtype), pltpu.HBM(land_thru.shape, land_thru.dtype)),
        in_specs=(_HBM, _HBM, _SEMS, _SEMS, _ANY), out_specs=(_HBM, _HBM), input_output_aliases={0: 0, 1: 1},
        compiler_params=pltpu.CompilerParams(has_side_effects=_DATAFLOW),
    )(pack_thru, land_thru, send_sems, recv_sems, after)[1]


def _swap_with_sibling(block, name):
    def body(x_ref, out_ref, send_sems, recv_sems):
        x, y, c = _place()
        cp = _remote(x_ref, out_ref, send_sems, recv_sems, 0, (x, y, 1 - c))
        cp.start()
        cp.wait()

    return pl.pallas_call(
        body, name=name, in_specs=[_ANY], out_specs=_ANY, out_shape=_sds(block.shape, block.dtype),
        scratch_shapes=[pltpu.SemaphoreType.DMA((1,)), pltpu.SemaphoreType.DMA((1,))],
    )(block)


def _swap_other_half(packed, tag):
    slots, rows, width = packed.shape
    half = rows // 2

    def body(x_ref, out_ref, send_sems, recv_sems):
        x, y, c = _place()
        cp = _remote(x_ref.at[:, pl.ds((1 - c) * half, half), :], out_ref, send_sems, recv_sems, 0, (x, y, 1 - c))
        cp.start()
        cp.wait()

    return pl.pallas_call(
        body, name="swap_halves_" + tag, in_specs=[_ANY], out_specs=_ANY, out_shape=_sds((slots, half, width)),
        scratch_shapes=[pltpu.SemaphoreType.DMA((1,)), pltpu.SemaphoreType.DMA((1,))],
    )(packed)


def _add_halves(packed, got, c, tag):
    slots, rows, width = packed.shape
    half = rows // 2
    tr = _tile(half, 408, 16)
    per = half // tr
    block = (None, tr, width)

    def body(c_ref, mine_ref, got_ref, sum_ref, sum16_ref):
        acc = mine_ref[...] + got_ref[...]
        sum_ref[...] = acc
        sum16_ref[...] = acc.astype(BF16)

    plain = pl.BlockSpec(block, lambda s, i, c_ref: (s, i, 0))
    grid_spec = pltpu.PrefetchScalarGridSpec(
        num_scalar_prefetch=1, grid=(slots, per),
        in_specs=[pl.BlockSpec(block, lambda s, i, c_ref: (s, c_ref[0] * per + i, 0)), plain],
        out_specs=[plain, plain])
    return pl.pallas_call(
        body, name="add_halves_" + tag, grid_spec=grid_spec,
        out_shape=[_sds((slots, half, width)), _sds((slots, half, width), BF16)],
        compiler_params=_cp("parallel", "parallel"))(c.reshape(1).astype(jnp.int32), packed, got)


def _add_quarters(chip_sum, others, chip, tag):
    _, rows, width = chip_sum.shape
    tr = _tile(rows, 408, 16)

    def body(chip_ref, own_ref, others_ref, o_ref):
        acc = own_ref[...]
        for j in range(3):
            acc = acc + others_ref[j].astype(F32)
        o_ref[...] = acc

    grid_spec = pltpu.PrefetchScalarGridSpec(
        num_scalar_prefetch=1, grid=(rows // tr,),
        in_specs=[pl.BlockSpec((None, tr, width), lambda i, chip_ref: (chip_ref[0], i, 0)),
                  pl.BlockSpec((3, tr, width), lambda i, chip_ref: (0, i, 0))],
        out_specs=pl.BlockSpec((tr, width), lambda i, chip_ref: (i, 0)))
    return pl.pallas_call(
        body, name="add_quarters_" + tag, grid_spec=grid_spec, out_shape=_sds((rows, width)),
        compiler_params=_cp("parallel"))(chip.reshape(1).astype(jnp.int32), chip_sum, others)


def _exchange_start(parts):
    _, rows, width = parts.shape

    def body(x_ref, land_ref, send_sems, recv_sems, x_thru, land_thru, token):
        x, y, c = _place()
        for j, chip in enumerate(_other_chips(x, y)):
            _remote(x_ref.at[2 * chip[0] + chip[1]], land_ref.at[j], send_sems, recv_sems, j, (*chip, c)).start()
        token[...] = jnp.zeros_like(token)

    land = lax.empty((3, rows, width), parts.dtype)
    return pl.pallas_call(
        body, name="exchange_ffn_start",
        out_shape=(pltpu.SemaphoreType.DMA((3,)), pltpu.SemaphoreType.DMA((3,)), pltpu.HBM(parts.shape, parts.dtype),
                   pltpu.HBM(land.shape, land.dtype), _sds((8, 128))),
        in_specs=(_HBM, _HBM), out_specs=(_SEMS, _SEMS, _HBM, _HBM, pl.BlockSpec(memory_space=pltpu.VMEM)),
        input_output_aliases={0: 2, 1: 3}, compiler_params=pltpu.CompilerParams(has_side_effects=_DATAFLOW),
    )(pltpu.with_memory_space_constraint(parts, pltpu.HBM), pltpu.with_memory_space_constraint(land, pltpu.HBM))


def _exchange_wait(send_sems, recv_sems, parts_thru, land_thru, after):
    def body(x_ref, land_ref, send_sems, recv_sems, after_ref, x_dead, got_ref):
        x, y, c = _place()
        for j, chip in enumerate(_other_chips(x, y)):
            cp = _remote(x_ref.at[2 * chip[0] + chip[1]], land_ref.at[j], send_sems, recv_sems, j, (*chip, c))
            cp.wait_send()
            cp.wait_recv()

    return pl.pallas_call(
        body, name="exchange_ffn_wait",
        out_shape=(pltpu.HBM(parts_thru.shape, parts_thru.dtype), pltpu.HBM(land_thru.shape, land_thru.dtype)),
        in_specs=(_HBM, _HBM, _SEMS, _SEMS, _ANY), out_specs=(_HBM, _HBM), input_output_aliases={0: 0, 1: 1},
        compiler_params=pltpu.CompilerParams(has_side_effects=_DATAFLOW),
    )(parts_thru, land_thru, send_sems, recv_sems, after)[1]


def _exchange_quarters(parts):
    _, rows, width = parts.shape

    def body(x_ref, out_ref, send_sems, recv_sems):
        x, y, c = _place()
        copies = [_remote(x_ref.at[2 * chip[0] + chip[1]], out_ref.at[j], send_sems, recv_sems, j, (*chip, c))
                  for j, chip in enumerate(_other_chips(x, y))]
        for cp in copies:
            cp.start()
        for cp in copies:
            cp.wait()

    return pl.pallas_call(
        body, name="exchange_quarters", in_specs=[_ANY], out_specs=_ANY,
        out_shape=_sds((3, rows, width), parts.dtype),
        scratch_shapes=[pltpu.SemaphoreType.DMA((3,)), pltpu.SemaphoreType.DMA((3,))],
    )(parts)


def _allreduce_small(vec):
    rows, width = vec.shape
    vmem = pl.BlockSpec(memory_space=pltpu.VMEM)

    def body(x_ref, o_ref, buf_ref, send_sems, recv_sems):
        x, y, c = _place()
        me = 4 * x + 2 * y + c
        buf_ref[me] = x_ref[...]
        copies = []
        for k in range(1, N_DEV):
            peer = (x ^ ((k >> 2) & 1), y ^ ((k >> 1) & 1), c ^ (k & 1))
            copies.append(_remote(x_ref, buf_ref.at[me], send_sems, recv_sems, k - 1, peer))
        for cp in copies:
            cp.start()
        for k in range(1, N_DEV):
            _remote(x_ref, buf_ref.at[me ^ k], send_sems, recv_sems, k - 1, (x, y, c)).wait_recv()
        for cp in copies:
            cp.wait_send()
        total = buf_ref[0]
        for d in range(1, N_DEV):
            total = total + buf_ref[d]
        o_ref[...] = total

    return pl.pallas_call(
        body, name="allreduce_small", in_specs=[vmem], out_specs=vmem, out_shape=_sds((rows, width)),
        scratch_shapes=[pltpu.VMEM((N_DEV, rows, width), F32), pltpu.SemaphoreType.DMA((N_DEV - 1,)),
                        pltpu.SemaphoreType.DMA((N_DEV - 1,))],
    )(vec)


def _rows1024(a):
    return a.reshape(-1, 1024)


def _pad_rows(a, rows):
    return jnp.concatenate([a, jnp.zeros((rows - a.shape[0], a.shape[1]), a.dtype)], axis=0)


_TRANSPOSED = ("w_in", "w_gate", "w_up")
_SMALL_SHARDED = ("w_up_f", "w_up_b", "a_up_f", "a_up_b", "g_up")
_BIG_SHARDED = ("w_in", "w_out", "w_gate", "w_up", "w_down")


def _pack_weight_shards(w):
    conv_bits = lax.bitcast_convert_type(w["conv_w"], BF16).reshape(1, -1)
    conv_row = jnp.concatenate([conv_bits, jnp.zeros((1, 1024 - conv_bits.shape[1]), BF16)], axis=1)

    def rows(name):
        a = w[name].astype(BF16)
        return a.T if name in _TRANSPOSED else _rows1024(a)

    early = _pad_rows(jnp.concatenate([rows(name) for name, _ in _EARLY_ROWS[:-1]] + [conv_row], axis=0), EARLY_R)
    return early, jnp.concatenate([rows(name) for name, _ in _FFN_ROWS], axis=0)


def _split_rows(gathered, layout):
    out, row = {}, 0
    for name, n in layout:
        out[name] = gathered[:, row:row + n]
        row += n
    return out


def _unpack_early(gathered):
    out = _split_rows(gathered, _EARLY_ROWS)
    cols = lambda a, k: jnp.concatenate([a[s].reshape(k, -1) for s in range(N_SHARD)], axis=1)
    conv = lax.bitcast_convert_type(out["conv_w"][:, 0, :768].reshape(N_SHARD, 3, 128, 2), F32)
    full = dict(w_in=out["w_in"].reshape(-1, 1024).T, w_out=out["w_out"].reshape(D_MODEL, D_MODEL),
                conv_w=jnp.concatenate([conv[s] for s in range(N_SHARD)], axis=1))
    full.update({name: cols(out[name], D_GATE if name == "g_up" else D_LORA) for name in _SMALL_SHARDED})
    return full


def _unpack_ffn(gathered):
    out = _split_rows(gathered, _FFN_ROWS)
    return dict(w_gate=out["w_gate"].reshape(-1, 1024).T, w_up=out["w_up"].reshape(-1, 1024).T,
                w_down=out["w_down"].reshape(D_FF, D_MODEL))


def _pack_grads(g, layout, rows):
    col_split = lambda a, s: a[:, s * (a.shape[1] // N_SHARD):(s + 1) * (a.shape[1] // N_SHARD)]
    row_split = lambda a, s: a[s * (a.shape[0] // N_SHARD):(s + 1) * (a.shape[0] // N_SHARD)]
    by_rows = {name: (g[name].T if name in _TRANSPOSED else g[name]) for name, _ in layout if name in _BIG_SHARDED}
    used = sum(n for _, n in layout)
    parts = []
    for s in range(N_SHARD):
        for name, _ in layout:
            if name in _BIG_SHARDED:
                parts.append(row_split(by_rows[name], s))
            elif name in _SMALL_SHARDED:
                parts.append(_rows1024(col_split(g[name], s)))
            else:
                conv = col_split(g["conv_w"], s).reshape(1, -1)
                parts.append(jnp.concatenate([conv, jnp.zeros((1, 1024 - conv.shape[1]), F32)], axis=1))
        if rows > used:
            parts.append(jnp.zeros((rows - used, 1024), F32))
    return jnp.concatenate(parts, axis=0).reshape(N_SHARD, rows, 1024)


def _unpack_grad_shard(pack, layout):
    small_shapes = {name: (D_GATE if name == "g_up" else D_LORA, 128) for name in _SMALL_SHARDED}
    out, row = {}, 0
    for name, n in layout:
        piece = pack[row:row + n]
        if name == "conv_w":
            out[name] = piece[0, :384].reshape(3, 128)
        else:
            out[name] = piece.T if name in _TRANSPOSED else piece.reshape(small_shapes.get(name, piece.shape))
        row += n
    return out


_SMALL_LAYOUT = (("norm1_w", 1024), ("mu_shift", D_SHIFT), ("w0_f", 512), ("w0_b", 512), ("a0_f", 512),
                 ("a0_b", 512), ("k_k", 512), ("k_a_f", 512), ("k_a_b", 512), ("r_k_f", 512), ("r_k_b", 512),
                 ("gn_w", 512), ("gn_b", 512), ("norm2_w", 1024), ("norm_f_w", 1024), ("loss", 1))


def _pack_small(vals):
    rows = []
    for name, n in _SMALL_LAYOUT:
        flat = vals[name].reshape(-1)
        n_rows = -(-n // 1024)
        rows.append(jnp.concatenate([flat, jnp.zeros((n_rows * 1024 - n,), F32)]).reshape(n_rows, 1024))
    return _pad_rows(jnp.concatenate(rows, axis=0), SMALL_ROWS)


def _unpack_small(pack):
    out, row = {}, 0
    for name, n in _SMALL_LAYOUT:
        n_rows = -(-n // 1024)
        out[name] = pack[row:row + n_rows].reshape(-1)[:n]
        row += n_rows
    return out


_WEIGHTS = ("norm1_w", "w_in", "mu_shift", "w_up_f", "w0_f", "w_up_b", "w0_b", "a_up_f", "a0_f", "a_up_b", "a0_b",
            "g_up", "k_k", "k_a_f", "k_a_b", "r_k_f", "r_k_b", "gn_w", "gn_b", "conv_w", "w_out", "norm2_w",
            "w_gate", "w_up", "w_down", "norm_f_w")


def _train_step(x, loss_target, w, m, v):
    batch, seq, _ = x.shape
    t = batch * seq
    tm = _tile(seq, 256, 8)
    xs = x.reshape(t, D_MODEL)
    target = loss_target.reshape(t, D_MODEL)
    vec = lambda name: w[name].reshape(1, -1)

    local = {name: w[name][0] for name, _ in _PACK_ROWS}
    c = lax.axis_index("c")
    chip = 2 * lax.axis_index("x") + lax.axis_index("y")
    early, ffn_pack = _pack_weight_shards(local)
    early_all = lax.dynamic_update_slice(_gather_weights(early), early[None], (chip, 0, 0))
    ffn_send, ffn_recv, ffn_pack, ffn_land, token = _fetch_start(ffn_pack, early_all)
    full = _unpack_early(early_all)
    w_in = full["w_in"]
    w_shift = jnp.concatenate([w_in[:, :D_SHIFT], jnp.zeros((D_MODEL, D_SHIFT_PAD - D_SHIFT), BF16)], axis=1)
    w_conv = w_in[:, D_SHIFT:]
    zeros_lora = jnp.zeros((D_LORA, D_RWKV), F32)
    lora = lambda name: full[name].astype(F32)
    mats = (jnp.concatenate([lora("w_up_f"), zeros_lora]), jnp.concatenate([zeros_lora, lora("a_up_f")]),
            jnp.concatenate([lora("w_up_b"), zeros_lora]), jnp.concatenate([zeros_lora, lora("a_up_b")]),
            jnp.concatenate([lora("g_up"), jnp.zeros((D_GATE_PAD - D_GATE, D_RWKV), F32)]))
    mu = jnp.concatenate([vec("mu_shift"), jnp.zeros((1, D_SHIFT_PAD - D_SHIFT), F32)], axis=1)
    mu = jnp.broadcast_to(mu, (GROUP, D_SHIFT_PAD))
    zero_row = jnp.zeros((1, D_RWKV), F32)
    pvec = jnp.concatenate([vec("k_k"), vec("w0_f"), vec("a0_f"), vec("k_a_f"), vec("w0_b"), vec("a0_b"),
                            vec("k_a_b"), zero_row], axis=0)
    qvec = jnp.concatenate([vec("gn_w"), vec("gn_b"), vec("r_k_f"), vec("r_k_b"), full["conv_w"], zero_row], axis=0)
    ones_blocks = _head_ones()

    h1, h1_t = _rmsnorm_fwd(xs, vec("norm1_w") + token[0, 0], tm, "norm1_fwd")
    p_shift = _matmul(h1, w_shift, mode="nn", name="in_proj_shift")
    pc = _matmul(h1, w_conv, mode="nn", name="in_proj_conv")
    ps = _shift_fwd(p_shift, mu, tm, seq)
    kk, w_f, kd_f, b_f, w_b, kd_b, b_b, gate = _prep_fwd(ps, pvec, mats, ones_blocks, tm)
    dirs = ((w_f, kd_f, b_f), (w_b, kd_b, b_b))
    y_f, y_b, hist_f, hist_b = _scan_fwd(ps, kk, dirs, batch, seq)
    mixed, mixed_t = _post_fwd(y_f, y_b, ps, kd_f, kd_b, gate, pc, qvec, ones_blocks, tm, seq)
    x1 = _matmul(mixed, full["w_out"], mode="nn", name="out_proj", add=xs)
    h2, h2_t = _rmsnorm_fwd(x1, vec("norm2_w"), tm, "norm2_fwd")
    ffn_all = _fetch_wait(ffn_send, ffn_recv, ffn_pack, ffn_land, h2)
    full.update(_unpack_ffn(lax.dynamic_update_slice(ffn_all, ffn_pack[None], (chip, 0, 0))))
    ff_gate, ff_up, act, act_t = _ffn_in(h2, full["w_gate"], full["w_up"])
    x2 = _matmul(act, full["w_down"], mode="nn", name="ffn_down", add=x1)
    d_x2, d_norm_f, loss_part = _loss_head(x2, w["norm_f_w"].reshape(1, -1), target, tm)

    g = {}
    g["w_down"] = _matmul(act_t, d_x2, mode="nn", name="ffn_down_dw")
    d_gate, d_up = _ffn_in_bwd(d_x2, full["w_down"], ff_gate, ff_up)
    d_h2 = _matmul(d_gate, full["w_gate"], mode="nt", name="ffn_gate_dx")
    d_h2 = _matmul(d_up, full["w_up"], mode="nt", name="ffn_up_dx", add=d_h2)
    g["w_gate"] = _matmul(h2_t, d_gate, mode="nn", name="ffn_gate_dw")
    g["w_up"] = _matmul(h2_t, d_up, mode="nn", name="ffn_up_dw")
    ffn_grads = _pack_grads(g, _FFN_ROWS, sum(n for _, n in _FFN_ROWS))
    ffn_sum, ffn_sum_bf16 = _add_halves(ffn_grads, _swap_other_half(ffn_grads, "ffn"), c, "ffn")
    ex_send, ex_recv, ffn_sum_bf16, ex_land, ex_token = _exchange_start(ffn_sum_bf16)
    d_x1, d_norm2 = _rmsnorm_bwd(x1, vec("norm2_w") + ex_token[0, 0], d_h2, d_x2, tm, "norm2_bwd")
    d_mixed = _matmul(d_x1, full["w_out"], mode="nt", name="out_proj_dx")
    g["w_out"] = _matmul(mixed_t, d_x1, mode="nn", name="out_proj_dw")
    dy, dr_o, dkdf_o, dkdb_o, dv_o, d_gatev, d_pc, d_qvec = _post_bwd(
        d_mixed, y_f, y_b, ps, kd_f, kd_b, gate, pc, qvec, ones_blocks, tm, seq)
    (dr_f, dw_f, dkd_f, dv_f, dkk_f, db_f), (dr_b, dw_b, dkd_b, dv_b, dkk_b, db_b) = _scan_bwd(
        ps, kk, dirs, dy, hist_f, hist_b, batch, seq)
    cts = [[dr_f, dr_b, dr_o], [dv_f, dv_b, dv_o], [dkk_f, dkk_b], [dw_f], [dkd_f, dkdf_o], [db_f],
           [dw_b], [dkd_b, dkdb_o], [db_b], [d_gatev]]
    q, d_pvec, d_m0, d_m1, d_m2, d_m3, d_m4 = _prep_bwd(ps, pvec, mats, ones_blocks, cts, tm)
    d_pshift, d_mu = _shift_bwd(q, p_shift, mu, tm, seq)
    d_h1 = _matmul(d_pshift, w_shift, mode="nt", name="in_proj_shift_dx")
    d_h1 = _matmul(d_pc, w_conv, mode="nt", name="in_proj_conv_dx", add=d_h1)
    d_w_shift = _matmul(h1_t, d_pshift, mode="nn", name="in_proj_shift_dw")
    d_w_conv = _matmul(h1_t, d_pc, mode="nn", name="in_proj_conv_dw")
    g["w_in"] = jnp.concatenate([d_w_shift[:, :D_SHIFT], d_w_conv], axis=1)
    d_x, d_norm1 = _rmsnorm_bwd(xs, vec("norm1_w"), d_h1, d_x1, tm, "norm1_bwd")
    g["w_up_f"], g["a_up_f"] = d_m0[:D_LORA], d_m1[D_LORA:]
    g["w_up_b"], g["a_up_b"] = d_m2[:D_LORA], d_m3[D_LORA:]
    g["g_up"] = d_m4[:D_GATE]
    g["conv_w"] = d_qvec[4:7]

    def finish(chip_sum, others, tag, layout):
        eighth = _add_quarters(chip_sum, others, chip, tag)
        other_eighth = _swap_with_sibling(eighth, "swap_eighths_" + tag)
        return _unpack_grad_shard(jnp.concatenate([jnp.where(c == 0, eighth, other_eighth),
                                                   jnp.where(c == 0, other_eighth, eighth)], axis=0), layout)

    packed = _pack_grads(g, _EARLY_ROWS, EARLY_R)
    chip_sum, chip_sum_bf16 = _add_halves(packed, _swap_other_half(packed, "mixer"), c, "mixer")
    grads = finish(chip_sum, _exchange_quarters(chip_sum_bf16), "mixer", _EARLY_ROWS)
    grads.update(finish(ffn_sum, _exchange_wait(ex_send, ex_recv, ffn_sum_bf16, ex_land, d_x), "ffn", _FFN_ROWS))

    small = dict(norm1_w=d_norm1, mu_shift=d_mu[:, :D_SHIFT], w0_f=d_pvec[1], w0_b=d_pvec[4], a0_f=d_pvec[2],
                 a0_b=d_pvec[5], k_k=d_pvec[0], k_a_f=d_pvec[3], k_a_b=d_pvec[6], r_k_f=d_qvec[2], r_k_b=d_qvec[3],
                 gn_w=d_qvec[0], gn_b=d_qvec[1], norm2_w=d_norm2, norm_f_w=d_norm_f, loss=loss_part)
    reduced = _unpack_small(_allreduce_small(_pack_small(small)))
    loss = reduced.pop("loss")[0]
    grads.update(reduced)

    outs = {}
    small = [name for name in _WEIGHTS if name not in _BIG_SHARDED]
    as2d = lambda name: (1, w[name].shape[0]) if w[name].ndim == 1 else w[name].shape
    operands = lambda name: tuple(a.reshape(as2d(name)) for a in (w[name], grads[name], m[name], v[name]))
    updates = dict(zip(small, _adamw_small([operands(name) for name in small])))
    for name in _BIG_SHARDED:
        updates[name] = _adamw(*operands(name), "adamw_" + name)
    for name in _WEIGHTS:
        shape = w[name].shape
        outs[name] = (grads[name].reshape(shape),) + tuple(a.reshape(shape) for a in updates[name])
    d_x = d_x.reshape(batch, seq, D_MODEL)
    return (loss, d_x) + tuple(outs[name][k] for k in range(4) for name in _WEIGHTS)


def kernel(x, norm1_w, w_in, mu_shift, w_up_f, w0_f, w_up_b, w0_b, a_up_f, a0_f, a_up_b, a0_b, g_up, k_k, k_a_f, k_a_b, r_k_f, r_k_b, gn_w, gn_b, conv_w, w_out, norm2_w, w_gate, w_up, w_down, norm_f_w, loss_target, m_norm1_w, m_w_in, m_mu_shift, m_w_up_f, m_w0_f, m_w_up_b, m_w0_b, m_a_up_f, m_a0_f, m_a_up_b, m_a0_b, m_g_up, m_k_k, m_k_a_f, m_k_a_b, m_r_k_f, m_r_k_b, m_gn_w, m_gn_b, m_conv_w, m_w_out, m_norm2_w, m_w_gate, m_w_up, m_w_down, m_norm_f_w, v_norm1_w, v_w_in, v_mu_shift, v_w_up_f, v_w0_f, v_w_up_b, v_w0_b, v_a_up_f, v_a0_f, v_a_up_b, v_a0_b, v_g_up, v_k_k, v_k_a_f, v_k_a_b, v_r_k_f, v_r_k_b, v_gn_w, v_gn_b, v_conv_w, v_w_out, v_norm2_w, v_w_gate, v_w_up, v_w_down, v_norm_f_w):
    args = locals()
    w = {name: args[name] for name in _WEIGHTS}
    m = {name: args["m_" + name] for name in _WEIGHTS}
    v = {name: args["v_" + name] for name in _WEIGHTS}
    return _train_step(x, loss_target, w, m, v)
```

```python
import functools

import jax
import jax.numpy as jnp
from jax import lax
from jax.experimental import pallas as pl
from jax.experimental.pallas import tpu as pltpu

F32 = jnp.float32
BF16 = jnp.bfloat16
MESH = pl.DeviceIdType.MESH

D_MODEL = 1024
D_RWKV = 512
HEAD = 64
N_PAIR = D_RWKV // (2 * HEAD)
D_LORA = 64
D_GATE = 160
D_GATE_PAD = 384
D_FF = 2816
D_SHIFT = 1824
D_SHIFT_PAD = 2048
D_CONV3 = 1536
LOG_DECAY_SCALE = 0.606531
RMS_EPS = 1e-6
GN_EPS = 64e-5
NORM_EPS = 1e-12
ADAM_LR, ADAM_B1, ADAM_B2, ADAM_EPS, ADAM_WD, ADAM_STEP = 0.001, 0.9, 0.999, 1e-08, 0.01, 10

N_SHARD = 4
N_DEV = 8
V7X_VMEM_LIMIT = 48 * 1024 * 1024
SCAN_CHUNK = 32
GROUP = 8

_PACK_ROWS = (("w_in", 840), ("w_out", 256), ("w_gate", 704), ("w_up", 704), ("w_down", 704),
              ("w_up_f", 8), ("w_up_b", 8), ("a_up_f", 8), ("a_up_b", 8), ("g_up", 20), ("conv_w", 1))
_FFN_NAMES = ("w_gate", "w_up", "w_down")
_EARLY_ROWS = tuple(item for item in _PACK_ROWS if item[0] not in _FFN_NAMES)
_FFN_ROWS = tuple(item for item in _PACK_ROWS if item[0] in _FFN_NAMES)
EARLY_R = 1152
SMALL_ROWS = 24


def _tile(n, cap, mult=128):
    best = None
    t = mult
    while t <= min(n, cap):
        if n % t == 0:
            best = t
        t += mult
    return best or n


def _cp(*sem):
    return pltpu.CompilerParams(dimension_semantics=sem or None, vmem_limit_bytes=V7X_VMEM_LIMIT)


def _sds(shape, dtype=F32):
    return jax.ShapeDtypeStruct(shape, dtype)


def _matmul(a, b, *, mode, name, out_dtype=F32, add=None):
    m, kdim = a.shape
    n = b.shape[1] if mode == "nn" else b.shape[0]
    tm, tn = _tile(m, 512, 8), _tile(n, 1536)
    tk = kdim if kdim <= 3584 else _tile(kdim, 1024)
    nk = kdim // tk
    a_spec = pl.BlockSpec((tm, tk), lambda i, j, k: (i, k))
    if mode == "nn":
        b_spec = pl.BlockSpec((tk, tn), lambda i, j, k: (k, j))
        dims = (((1,), (0,)), ((), ()))
    else:
        b_spec = pl.BlockSpec((tn, tk), lambda i, j, k: (j, k))
        dims = (((1,), (1,)), ((), ()))
    has_add = add is not None

    def body(*refs):
        a_ref, b_ref = refs[0], refs[1]
        add_ref = refs[2] if has_add else None
        o_ref = refs[3] if has_add else refs[2]
        part = lax.dot_general(a_ref[...].astype(BF16), b_ref[...].astype(BF16), dims,
                               preferred_element_type=F32)
        if nk == 1:
            if has_add:
                part = part + add_ref[...]
            o_ref[...] = part.astype(out_dtype)
        else:
            acc_ref = refs[-1]
            k = pl.program_id(2)

            @pl.when(k == 0)
            def _():
                acc_ref[...] = jnp.zeros_like(acc_ref)

            acc_ref[...] += part

            @pl.when(k == nk - 1)
            def _():
                res = acc_ref[...]
                if has_add:
                    res = res + add_ref[...]
                o_ref[...] = res.astype(out_dtype)

    o_spec = pl.BlockSpec((tm, tn), lambda i, j, k: (i, j))
    in_specs = [a_spec, b_spec] + ([o_spec] if has_add else [])
    args = (a, b) + ((add,) if has_add else ())
    return pl.pallas_call(
        body, name=name, grid=(m // tm, n // tn, nk), in_specs=in_specs, out_specs=o_spec,
        out_shape=_sds((m, n), out_dtype),
        scratch_shapes=[pltpu.VMEM((tm, tn), F32)] if nk > 1 else [],
        compiler_params=_cp("parallel", "parallel", "arbitrary"),
    )(*args)


def _row(tm, width):
    return pl.BlockSpec((tm, width), lambda i: (i, 0))


def _col(tm, height):
    return pl.BlockSpec((height, tm), lambda i: (0, i))


def _fixed(shape):
    return pl.BlockSpec(shape, lambda i: tuple(0 for _ in shape))


def _rmsnorm_fwd(x, w, tm, name):
    t, d = x.shape

    def body(x_ref, w_ref, o_ref, ot_ref):
        xv = x_ref[...]
        rstd = lax.rsqrt(jnp.mean(xv * xv, axis=-1, keepdims=True) + RMS_EPS)
        yv = xv * rstd * w_ref[...]
        o_ref[...] = yv.astype(BF16)
        ot_ref[...] = jnp.transpose(yv).astype(BF16)

    return pl.pallas_call(
        body, name=name, grid=(t // tm,), in_specs=[_row(tm, d), _fixed((1, d))],
        out_specs=[_row(tm, d), _col(tm, d)], out_shape=[_sds((t, d), BF16), _sds((d, t), BF16)],
        compiler_params=_cp("parallel"))(x, w)


def _rms_bwd_math(xv, wv, dyv):
    rstd = lax.rsqrt(jnp.mean(xv * xv, axis=-1, keepdims=True) + RMS_EPS)
    xhat = xv * rstd
    gv = dyv * wv
    dx = rstd * (gv - xhat * jnp.mean(gv * xhat, axis=-1, keepdims=True))
    return dx, jnp.sum(dyv * xhat, axis=0, keepdims=True)


def _rmsnorm_bwd(x, w, dy, dres, tm, name):
    t, d = x.shape

    def body(x_ref, w_ref, dy_ref, dres_ref, dx_ref, dw_ref):
        dx, dw = _rms_bwd_math(x_ref[...], w_ref[...], dy_ref[...])
        dx_ref[...] = dres_ref[...] + dx

        @pl.when(pl.program_id(0) == 0)
        def _():
            dw_ref[...] = jnp.zeros_like(dw_ref)

        dw_ref[...] += dw

    return pl.pallas_call(
        body, name=name, grid=(t // tm,),
        in_specs=[_row(tm, d), _fixed((1, d)), _row(tm, d), _row(tm, d)],
        out_specs=[_row(tm, d), _fixed((1, d))],
        out_shape=[_sds((t, d)), _sds((1, d))], compiler_params=_cp("arbitrary"))(x, w, dy, dres)


def _loss_head(x, w, target, tm):
    t, d = x.shape

    def body(x_ref, w_ref, t_ref, dx_ref, dw_ref, loss_ref):
        xv, wv = x_ref[...], w_ref[...]
        rstd = lax.rsqrt(jnp.mean(xv * xv, axis=-1, keepdims=True) + RMS_EPS)
        err = xv * rstd * wv - t_ref[...]
        dx, dw = _rms_bwd_math(xv, wv, err * (1.0 / d))
        dx_ref[...] = dx

        @pl.when(pl.program_id(0) == 0)
        def _():
            dw_ref[...] = jnp.zeros_like(dw_ref)
            loss_ref[...] = jnp.zeros_like(loss_ref)

        dw_ref[...] += dw
        loss_ref[...] += 0.5 * jnp.sum(jnp.mean(err * err, axis=-1, keepdims=True), axis=0, keepdims=True)

    return pl.pallas_call(
        body, name="loss_head", grid=(t // tm,),
        in_specs=[_row(tm, d), _fixed((1, d)), _row(tm, d)],
        out_specs=[_row(tm, d), _fixed((1, d)), _fixed((1, 1))],
        out_shape=[_sds((t, d)), _sds((1, d)), _sds((1, 1))], compiler_params=_cp("arbitrary"))(x, w, target)


def _ffn_in(h, w_gate, w_up):
    t, d = h.shape
    f = w_gate.shape[1]
    tm, tn = _tile(t, 512, 8), _tile(f, 1536)

    def body(h_ref, wg_ref, wu_ref, g_ref, u_ref, a_ref, at_ref):
        hv = h_ref[...]
        gv = jnp.dot(hv, wg_ref[...], preferred_element_type=F32)
        uv = jnp.dot(hv, wu_ref[...], preferred_element_type=F32)
        act = gv * jax.nn.sigmoid(gv) * uv
        g_ref[...] = gv.astype(BF16)
        u_ref[...] = uv.astype(BF16)
        a_ref[...] = act.astype(BF16)
        at_ref[...] = jnp.transpose(act).astype(BF16)

    w_spec = pl.BlockSpec((d, tn), lambda i, j: (0, j))
    o_spec = pl.BlockSpec((tm, tn), lambda i, j: (i, j))
    return pl.pallas_call(
        body, name="ffn_in", grid=(t // tm, f // tn),
        in_specs=[pl.BlockSpec((tm, d), lambda i, j: (i, 0)), w_spec, w_spec],
        out_specs=[o_spec, o_spec, o_spec, pl.BlockSpec((tn, tm), lambda i, j: (j, i))],
        out_shape=[_sds((t, f), BF16)] * 3 + [_sds((f, t), BF16)],
        compiler_params=_cp("parallel", "parallel"))(h, w_gate, w_up)


def _ffn_in_bwd(d_out, w_down, gate, up):
    t, d = d_out.shape
    f = w_down.shape[0]
    tm, tn = _tile(t, 512, 8), _tile(f, 1536)

    def body(do_ref, w_ref, g_ref, u_ref, dg_ref, du_ref):
        dv = lax.dot_general(do_ref[...].astype(BF16), w_ref[...], (((1,), (1,)), ((), ())),
                             preferred_element_type=F32)
        gv, uv = g_ref[...].astype(F32), u_ref[...].astype(F32)
        sg = jax.nn.sigmoid(gv)
        du_ref[...] = (dv * gv * sg).astype(BF16)
        dg_ref[...] = (dv * uv * (sg * (1.0 + gv * (1.0 - sg)))).astype(BF16)

    tile = pl.BlockSpec((tm, tn), lambda i, j: (i, j))
    return pl.pallas_call(
        body, name="ffn_in_bwd", grid=(t // tm, f // tn),
        in_specs=[pl.BlockSpec((tm, d), lambda i, j: (i, 0)), pl.BlockSpec((tn, d), lambda i, j: (j, 0)), tile, tile],
        out_specs=[tile, tile], out_shape=[_sds((t, f), BF16)] * 2,
        compiler_params=_cp("parallel", "parallel"))(d_out, w_down, gate, up)


def _halo_specs(tm, width, rows_total):
    per = tm // GROUP
    last = rows_total // GROUP - 1
    prev = pl.BlockSpec((GROUP, width), lambda i: (jnp.maximum(i * per - 1, 0), 0))
    nxt = pl.BlockSpec((GROUP, width), lambda i: (jnp.minimum((i + 1) * per, last), 0))
    return prev, nxt


def _edge_flags(tm, seq):
    i = pl.program_id(0)
    has_prev = jnp.where((i * tm) % seq == 0, 0.0, 1.0).astype(F32)
    has_next = jnp.where(((i + 1) * tm) % seq == 0, 0.0, 1.0).astype(F32)
    return has_prev, has_next


def _shifted(xv, prev_row, next_row):
    tm = xv.shape[0]
    row = lax.broadcasted_iota(jnp.int32, xv.shape, 0)
    down = jnp.where(row == 0, prev_row, pltpu.roll(xv, 1, axis=0))
    up = jnp.where(row == tm - 1, next_row, pltpu.roll(xv, tm - 1, axis=0))
    return down, up


def _shift_fwd(p, mu, tm, seq):
    t, w = p.shape
    prev_spec, next_spec = _halo_specs(tm, w, t)

    def body(p_ref, hp_ref, hn_ref, mu_ref, o_ref):
        has_prev, has_next = _edge_flags(tm, seq)
        xv = p_ref[...]
        down, up = _shifted(xv, hp_ref[GROUP - 1:GROUP, :] * has_prev, hn_ref[0:1, :] * has_next)
        o_ref[...] = xv + mu_ref[0:1, :] * (0.5 * (down + up) - xv)

    return pl.pallas_call(
        body, name="shift_fwd", grid=(t // tm,),
        in_specs=[_row(tm, w), prev_spec, next_spec, _fixed((GROUP, w))], out_specs=_row(tm, w),
        out_shape=_sds((t, w)), compiler_params=_cp("parallel"))(p, p, p, mu)


def _shift_bwd(q, p, mu, tm, seq):
    t, w = p.shape
    prev_spec, next_spec = _halo_specs(tm, w, t)

    def body(q_ref, qp_ref, qn_ref, p_ref, pp_ref, pn_ref, mu_ref, dp_ref, dmu_ref):
        has_prev, has_next = _edge_flags(tm, seq)
        muv = mu_ref[0:1, :]
        qv = q_ref[...]
        mq = muv * qv
        mq_down, mq_up = _shifted(mq, muv * qp_ref[GROUP - 1:GROUP, :] * has_prev,
                                  muv * qn_ref[0:1, :] * has_next)
        dp_ref[...] = (qv - mq + 0.5 * (mq_down + mq_up)).astype(BF16)
        pv = p_ref[...]
        p_down, p_up = _shifted(pv, pp_ref[GROUP - 1:GROUP, :] * has_prev, pn_ref[0:1, :] * has_next)

        @pl.when(pl.program_id(0) == 0)
        def _():
            dmu_ref[...] = jnp.zeros_like(dmu_ref)

        dmu_ref[...] += jnp.sum(qv * (0.5 * (p_down + p_up) - pv), axis=0, keepdims=True)

    return pl.pallas_call(
        body, name="shift_bwd", grid=(t // tm,),
        in_specs=[_row(tm, w), prev_spec, next_spec, _row(tm, w), prev_spec, next_spec, _fixed((GROUP, w))],
        out_specs=[_row(tm, w), _fixed((1, w))],
        out_shape=[_sds((t, w), BF16), _sds((1, w))], compiler_params=_cp("arbitrary"))(q, q, q, p, p, p, mu)


@jax.custom_vjp
def _bdot(a, b):
    return jnp.dot(a.astype(BF16), b.astype(BF16), preferred_element_type=F32)


def _bdot_fwd(a, b):
    return _bdot(a, b), (a, b)


def _bdot_bwd(res, g):
    a, b = res
    gb = g.astype(BF16)
    da = lax.dot_general(gb, b.astype(BF16), (((1,), (1,)), ((), ())), preferred_element_type=F32)
    db = lax.dot_general(a.astype(BF16), gb, (((0,), (0,)), ((), ())), preferred_element_type=F32)
    return da, db


_bdot.defvjp(_bdot_fwd, _bdot_bwd)


def _seg_raw(x, ones_blocks):
    hi = x.astype(BF16)
    lo = (x - hi.astype(F32)).astype(BF16)
    return (jnp.dot(hi, ones_blocks, preferred_element_type=F32)
            + jnp.dot(lo, ones_blocks, preferred_element_type=F32))


@jax.custom_vjp
def _seg(x, ones_blocks):
    return _seg_raw(x, ones_blocks)


def _seg_fwd(x, ones_blocks):
    return _seg_raw(x, ones_blocks), ones_blocks


def _seg_bwd(ones_blocks, g):
    return _seg_raw(g, ones_blocks), jnp.zeros_like(ones_blocks)


_seg.defvjp(_seg_fwd, _seg_bwd)


def _head_ones():
    h = jnp.arange(D_RWKV) // HEAD
    return (h[:, None] == h[None, :]).astype(BF16)


def _prep_math(ps, k_k, w0_f, a0_f, k_a_f, w0_b, a0_b, k_a_b, wup_f, aup_f, wup_b, aup_b, gup, ones_blocks):
    r = ps[:, 0:512]
    k = ps[:, 512:1024]
    v = ps[:, 1024:1536]
    xwa = ps[:, 1536:1664]
    xg = ps[:, 1664:D_SHIFT_PAD]
    kk_raw = k * k_k
    norm = jnp.sqrt(_seg(kk_raw * kk_raw, ones_blocks))
    kk = kk_raw / jnp.maximum(norm, NORM_EPS)
    t_xwa = jnp.tanh(xwa)
    outs = [r, v, kk]
    for w0, a0, k_a, wup, aup in ((w0_f, a0_f, k_a_f, wup_f, aup_f), (w0_b, a0_b, k_a_b, wup_b, aup_b)):
        decay = jnp.exp(-LOG_DECAY_SCALE * jax.nn.sigmoid(w0 + _bdot(t_xwa, wup)))
        rate = jax.nn.sigmoid(a0 + _bdot(xwa, aup))
        outs += [decay, k * (1.0 + (rate - 1.0) * k_a), kk * rate]
    outs.append(_bdot(jax.nn.sigmoid(xg), gup))
    return tuple(outs)


def _prep_args(tm, ps_ref, pv_ref, mat_refs, ones_ref):
    vecs = [jnp.broadcast_to(pv_ref[j:j + 1, :], (tm, D_RWKV)) for j in range(7)]
    return [ps_ref[...]] + vecs + [m[...] for m in mat_refs] + [ones_ref[...]]


_PREP_MAT_SHAPES = ((128, D_RWKV),) * 4 + ((D_GATE_PAD, D_RWKV),)


def _prep_fwd(ps, pvec, mats, ones_blocks, tm):
    t = ps.shape[0]

    def body(ps_ref, pv_ref, m0, m1, m2, m3, m4, ones_ref, *out_refs):
        outs = _prep_math(*_prep_args(tm, ps_ref, pv_ref, (m0, m1, m2, m3, m4), ones_ref))
        for o_ref, val in zip(out_refs, outs[2:]):
            o_ref[...] = val

    return pl.pallas_call(
        body, name="prep_fwd", grid=(t // tm,),
        in_specs=[_row(tm, D_SHIFT_PAD), _fixed((8, D_RWKV))] + [_fixed(s) for s in _PREP_MAT_SHAPES]
        + [_fixed((D_RWKV, D_RWKV))],
        out_specs=[_row(tm, D_RWKV)] * 8, out_shape=[_sds((t, D_RWKV))] * 8,
        compiler_params=_cp("parallel"))(ps, pvec, *mats, ones_blocks)


def _prep_bwd(ps, pvec, mats, ones_blocks, cts, tm):
    t = ps.shape[0]
    counts = [len(c) for c in cts]
    flat = [a for c in cts for a in c]

    def body(ps_ref, pv_ref, m0, m1, m2, m3, m4, ones_ref, *refs):
        ct_refs = refs[:len(flat)]
        q_ref, dpv_ref = refs[len(flat)], refs[len(flat) + 1]
        dmat_refs = refs[len(flat) + 2:]
        args = _prep_args(tm, ps_ref, pv_ref, (m0, m1, m2, m3, m4), ones_ref)
        _, vjp = jax.vjp(lambda *a: _prep_math(*a, args[-1]), *args[:-1])
        ct_vals, pos = [], 0
        for n in counts:
            val = ct_refs[pos][...]
            for extra in ct_refs[pos + 1:pos + n]:
                val = val + extra[...]
            ct_vals.append(val)
            pos += n
        grads = vjp(tuple(ct_vals))
        q_ref[...] = grads[0]

        @pl.when(pl.program_id(0) == 0)
        def _():
            dpv_ref[...] = jnp.zeros_like(dpv_ref)
            for d_ref in dmat_refs:
                d_ref[...] = jnp.zeros_like(d_ref)

        for j in range(7):
            dpv_ref[j:j + 1, :] += jnp.sum(grads[1 + j], axis=0, keepdims=True)
        for d_ref, gm in zip(dmat_refs, grads[8:13]):
            d_ref[...] += gm

    return pl.pallas_call(
        body, name="prep_bwd", grid=(t // tm,),
        in_specs=[_row(tm, D_SHIFT_PAD), _fixed((8, D_RWKV))] + [_fixed(s) for s in _PREP_MAT_SHAPES]
        + [_fixed((D_RWKV, D_RWKV))] + [_row(tm, D_RWKV)] * len(flat),
        out_specs=[_row(tm, D_SHIFT_PAD), _fixed((8, D_RWKV))] + [_fixed(s) for s in _PREP_MAT_SHAPES],
        out_shape=[_sds((t, D_SHIFT_PAD)), _sds((8, D_RWKV))] + [_sds(s) for s in _PREP_MAT_SHAPES],
        compiler_params=_cp("arbitrary"))(ps, pvec, *mats, ones_blocks, *flat)


def _pair_ones():
    h = jnp.arange(2 * HEAD) // HEAD
    block = (h[:, None] == h[None, :]).astype(BF16)
    return jnp.concatenate([block, block], axis=0)


def _diag_mask():
    lane = lax.broadcasted_iota(jnp.int32, (HEAD, 2 * HEAD), 1)
    sub = lax.broadcasted_iota(jnp.int32, (HEAD, 2 * HEAD), 0)
    return jnp.where((lane & (HEAD - 1)) == sub, 1.0, 0.0).astype(F32)


def _to_row(cols, dmask):
    return jnp.sum(cols * dmask, axis=0, keepdims=True)


def _seg_many(exact, rounded, ones_pair):
    out_exact, out_rounded = [], []
    if exact:
        parts = []
        for x in exact:
            hi = x.astype(BF16)
            parts.append(jnp.concatenate([hi, (x - hi.astype(F32)).astype(BF16)], axis=1))
        res = jnp.dot(jnp.concatenate(parts, axis=0), ones_pair, preferred_element_type=F32)
        out_exact = [res[HEAD * c:HEAD * (c + 1)] for c in range(len(exact))]
    if rounded:
        res = jnp.dot(jnp.concatenate([x.astype(BF16) for x in rounded], axis=0), ones_pair[0:2 * HEAD],
                      preferred_element_type=F32)
        out_rounded = [res[HEAD * c:HEAD * (c + 1)] for c in range(len(rounded))]
    return out_exact, out_rounded


N_CHAIN = 2 * N_PAIR


def _chain(c):
    d, p = divmod(c, N_PAIR)
    return d, slice(2 * HEAD * p, 2 * HEAD * (p + 1))


def _scan_specs(n_chunks, col_blocks, fwd_chunk, bwd_chunk):
    def spec(chunk_of, col):
        return pl.BlockSpec((SCAN_CHUNK, D_RWKV), lambda b, g: (b * n_chunks + chunk_of(g), col))
    return [spec(fwd_chunk, c) for c in col_blocks] + [spec(bwd_chunk, c) for c in col_blocks]


def _scan_fwd(ps, kk, dirs, batch, seq):
    t = batch * seq
    n = seq // SCAN_CHUNK
    groups = SCAN_CHUNK // GROUP
    up = lambda g: g
    down = lambda g: n - 1 - g
    col_blocks = (0, 2, 0, 0, 0, 0)

    def body(*refs):
        dir_refs = (refs[0:6], refs[6:12])
        ones_ref = refs[12]
        y_refs, hist_refs, st_ref = refs[13:15], refs[15:17], refs[17]

        @pl.when(pl.program_id(1) == 0)
        def _():
            st_ref[...] = jnp.zeros_like(st_ref)

        ones_pair = ones_ref[...]
        dmask = _diag_mask()
        dmask_b = dmask.astype(BF16)
        sub8 = lax.broadcasted_iota(jnp.int32, (GROUP, 2 * HEAD), 0)

        def group(gi, carry):
            off = (pl.multiple_of(gi * GROUP, GROUP), pl.multiple_of((groups - 1 - gi) * GROUP, GROUP))
            loaded = [tuple(ref[pl.ds(off[d], GROUP), :] for ref in dir_refs[d]) for d in range(2)]
            states = list(carry)
            y_acc = [jnp.zeros((GROUP, 2 * HEAD), F32) for _ in range(N_CHAIN)]
            for step in range(GROUP):
                rows, idx = [], []
                for c in range(N_CHAIN):
                    d, lanes = _chain(c)
                    i = step if d == 0 else GROUP - 1 - step
                    idx.append(i)
                    rows.append(tuple(x8[i:i + 1, lanes] for x8 in loaded[d]))
                    hist_refs[d][c % N_PAIR, gi * GROUP + step] = states[c]
                _, v_cols = _seg_many([], [dmask_b * rows[c][1].astype(BF16) for c in range(N_CHAIN)], ones_pair)
                sas, _ = _seg_many([states[c] * rows[c][2] for c in range(N_CHAIN)], [], ones_pair)
                for c in range(N_CHAIN):
                    _, _, _, w_row, kd_row, b_row = rows[c]
                    states[c] = states[c] * w_row - sas[c] * b_row + v_cols[c] * kd_row
                _, ys = _seg_many([], [states[c] * rows[c][0] for c in range(N_CHAIN)], ones_pair)
                for c in range(N_CHAIN):
                    y_acc[c] = jnp.where(sub8 == idx[c], _to_row(ys[c], dmask), y_acc[c])
            for c in range(N_CHAIN):
                d, lanes = _chain(c)
                y_refs[d][pl.ds(off[d], GROUP), lanes] = y_acc[c]
            return tuple(states)

        final = lax.fori_loop(0, groups, group, tuple(st_ref[c] for c in range(N_CHAIN)))
        for c in range(N_CHAIN):
            st_ref[c] = final[c]
            hist_refs[c // N_PAIR][c % N_PAIR, SCAN_CHUNK] = final[c]

    y_spec_f = pl.BlockSpec((SCAN_CHUNK, D_RWKV), lambda b, g: (b * n + up(g), 0))
    y_spec_b = pl.BlockSpec((SCAN_CHUNK, D_RWKV), lambda b, g: (b * n + down(g), 0))
    hist_shape = (batch, n, N_PAIR, SCAN_CHUNK + 1, HEAD, 2 * HEAD)
    hist_block = (None, None, N_PAIR, SCAN_CHUNK + 1, HEAD, 2 * HEAD)
    hist_spec_f = pl.BlockSpec(hist_block, lambda b, g: (b, up(g), 0, 0, 0, 0))
    hist_spec_b = pl.BlockSpec(hist_block, lambda b, g: (b, down(g), 0, 0, 0, 0))
    ones_spec = pl.BlockSpec((4 * HEAD, 2 * HEAD), lambda b, g: (0, 0))
    (wf, kdf, bf), (wb, kdb, bb) = dirs
    return pl.pallas_call(
        body, name="wkv_fwd", grid=(batch, n),
        in_specs=_scan_specs(n, col_blocks, up, down) + [ones_spec],
        out_specs=[y_spec_f, y_spec_b, hist_spec_f, hist_spec_b],
        out_shape=[_sds((t, D_RWKV)), _sds((t, D_RWKV)), _sds(hist_shape), _sds(hist_shape)],
        scratch_shapes=[pltpu.VMEM((N_CHAIN, HEAD, 2 * HEAD), F32)],
        compiler_params=_cp("parallel", "arbitrary"),
    )(ps, ps, kk, wf, kdf, bf, ps, ps, kk, wb, kdb, bb, _pair_ones())


def _scan_bwd(ps, kk, dirs, dy, hist_f, hist_b, batch, seq):
    t = batch * seq
    n = seq // SCAN_CHUNK
    groups = SCAN_CHUNK // GROUP
    fwd_chunk = lambda g: n - 1 - g
    bwd_chunk = lambda g: g
    col_blocks = (0, 2, 0, 0, 0, 0, 0)

    def undo_group(dir_refs, out_refs, hist_refs, gi, d_states, ones_pair, dmask, sub8):
        d_states = list(d_states)
        loaded, blocks = [], []
        for d in range(2):
            blk = groups - 1 - gi if d == 0 else gi
            blocks.append(pl.ds(pl.multiple_of(blk * GROUP, GROUP), GROUP))
            r8, v8, kk8, w8, kd8, b8, dy8 = (ref[blocks[d], :] for ref in dir_refs[d])
            loaded.append((r8, v8, kk8, w8, kd8, -b8, dy8))
        acc = [[jnp.zeros((GROUP, 2 * HEAD), F32) for _ in range(6)] for _ in range(N_CHAIN)]
        for step in range(GROUP):
            rows, idx, before, after = [], [], [], []
            for c in range(N_CHAIN):
                d, lanes = _chain(c)
                i = GROUP - 1 - step if d == 0 else step
                q = (groups - 1 - gi) * GROUP + i if d == 0 else SCAN_CHUNK - 1 - (gi * GROUP + i)
                idx.append(i)
                rows.append(tuple(x8[i:i + 1, lanes] for x8 in loaded[d]))
                before.append(hist_refs[d][c % N_PAIR, q])
                after.append(hist_refs[d][c % N_PAIR, q + 1])
            _, cols = _seg_many([], [dmask.astype(BF16) * rows[c][j].astype(BF16) for c in range(N_CHAIN) for j in (1, 6)],
                                ones_pair)
            v_cols, dy_cols = cols[0::2], cols[1::2]
            d_now = [d_states[c] + dy_cols[c] * rows[c][0] for c in range(N_CHAIN)]
            d_sas, _ = _seg_many([d_now[c] * rows[c][5] for c in range(N_CHAIN)], [], ones_pair)
            _, others = _seg_many(
                [], [x for c in range(N_CHAIN) for x in (before[c] * rows[c][2], d_now[c] * rows[c][4])], ones_pair)
            for c in range(N_CHAIN):
                sa, d_sa, dv_cols = others[2 * c], d_sas[c], others[2 * c + 1]
                rows_out = (
                    jnp.sum(after[c] * dy_cols[c], axis=0, keepdims=True),
                    jnp.sum(d_now[c] * before[c], axis=0, keepdims=True),
                    jnp.sum(d_now[c] * v_cols[c], axis=0, keepdims=True),
                    _to_row(dv_cols, dmask),
                    jnp.sum(before[c] * d_sa, axis=0, keepdims=True),
                    -jnp.sum(d_now[c] * sa, axis=0, keepdims=True),
                )
                acc[c] = [jnp.where(sub8 == idx[c], val, a) for val, a in zip(rows_out, acc[c])]
                d_states[c] = d_now[c] * rows[c][3] + d_sa * rows[c][2]
        for c in range(N_CHAIN):
            d, lanes = _chain(c)
            for o_ref, val in zip(out_refs[d], acc[c]):
                o_ref[blocks[d], lanes] = val
        return tuple(d_states)

    def body(*refs):
        dir_refs = (refs[0:7], refs[7:14])
        hist_refs, ones_ref = refs[14:16], refs[16]
        out_refs = (refs[17:23], refs[23:29])
        dst_ref = refs[29]

        @pl.when(pl.program_id(1) == 0)
        def _():
            dst_ref[...] = jnp.zeros_like(dst_ref)

        ones_pair = ones_ref[...]
        dmask = _diag_mask()
        sub8 = lax.broadcasted_iota(jnp.int32, (GROUP, 2 * HEAD), 0)

        def group(gi, carry):
            return undo_group(dir_refs, out_refs, hist_refs, gi, carry, ones_pair, dmask, sub8)

        final = lax.fori_loop(0, groups, group, tuple(dst_ref[c] for c in range(N_CHAIN)))
        for c in range(N_CHAIN):
            dst_ref[c] = final[c]

    blk = (SCAN_CHUNK, D_RWKV)
    out_f = pl.BlockSpec(blk, lambda b, g: (b * n + fwd_chunk(g), 0))
    out_b = pl.BlockSpec(blk, lambda b, g: (b * n + bwd_chunk(g), 0))
    hist_block = (None, None, N_PAIR, SCAN_CHUNK + 1, HEAD, 2 * HEAD)
    hist_spec_f = pl.BlockSpec(hist_block, lambda b, g: (b, fwd_chunk(g), 0, 0, 0, 0))
    hist_spec_b = pl.BlockSpec(hist_block, lambda b, g: (b, bwd_chunk(g), 0, 0, 0, 0))
    ones_spec = pl.BlockSpec((4 * HEAD, 2 * HEAD), lambda b, g: (0, 0))
    (wf, kdf, bf), (wb, kdb, bb) = dirs
    outs = pl.pallas_call(
        body, name="wkv_bwd", grid=(batch, n),
        in_specs=_scan_specs(n, col_blocks, fwd_chunk, bwd_chunk) + [hist_spec_f, hist_spec_b, ones_spec],
        out_specs=[out_f] * 6 + [out_b] * 6,
        out_shape=[_sds((t, D_RWKV))] * 12,
        scratch_shapes=[pltpu.VMEM((N_CHAIN, HEAD, 2 * HEAD), F32)],
        compiler_params=_cp("parallel", "arbitrary"),
    )(ps, ps, kk, wf, kdf, bf, dy, ps, ps, kk, wb, kdb, bb, dy, hist_f, hist_b, _pair_ones())
    return outs[0:6], outs[6:12]


def _post_math(y, r, kd_f, kd_b, v, gate, gn_w, gn_b, rk_f, rk_b, ones_blocks):
    mean = _seg(y, ones_blocks) * (1.0 / HEAD)
    yc = y - mean
    var = _seg(yc * yc, ones_blocks) * (1.0 / HEAD)
    yn = yc * lax.rsqrt(var + GN_EPS) * gn_w + gn_b
    bonus = _seg(r * kd_f * rk_f, ones_blocks) * v + _seg(r * kd_b * rk_b, ones_blocks) * v
    return (yn + bonus) * gate


def _conv_parts(pc, halo_prev, halo_next, has_prev, has_next):
    gate_b, gate_c, hid = pc[:, 0:512], pc[:, 512:1024], pc[:, 1024:1536]
    u = gate_c * hid
    u_prev_row = halo_prev[GROUP - 1:GROUP, 512:1024] * halo_prev[GROUP - 1:GROUP, 1024:1536] * has_prev
    u_next_row = halo_next[0:1, 512:1024] * halo_next[0:1, 1024:1536] * has_next
    u_down, u_up = _shifted(u, u_prev_row, u_next_row)
    return gate_b, gate_c, hid, u, u_down, u_up


def _post_specs(tm, t):
    pc_prev, pc_next = _halo_specs(tm, D_CONV3, t)
    col = lambda c: pl.BlockSpec((tm, D_RWKV), lambda i: (i, c))
    return ([col(0), col(0), col(0), col(0), col(0), col(2), col(0), _row(tm, D_CONV3), pc_prev, pc_next,
             _fixed((8, D_RWKV)), _fixed((D_RWKV, D_RWKV))])


def _post_fwd(y_f, y_b, ps, kd_f, kd_b, gate, pc, qvec, ones_blocks, tm, seq):
    t = ps.shape[0]

    def body(yf_ref, yb_ref, r_ref, kdf_ref, kdb_ref, v_ref, g_ref, pc_ref, hp_ref, hn_ref, qv_ref, ones_ref,
             o_ref, ot_ref):
        has_prev, has_next = _edge_flags(tm, seq)
        vec = [jnp.broadcast_to(qv_ref[j:j + 1, :], (tm, D_RWKV)) for j in range(7)]
        o_rwkv = _post_math(yf_ref[...] + yb_ref[...], r_ref[...], kdf_ref[...], kdb_ref[...], v_ref[...],
                            g_ref[...], vec[0], vec[1], vec[2], vec[3], ones_ref[...])
        gate_b, _, _, u, u_down, u_up = _conv_parts(pc_ref[...], hp_ref[...], hn_ref[...], has_prev, has_next)
        o_conv = gate_b * (vec[4] * u_down + vec[5] * u + vec[6] * u_up)
        for half, val in enumerate((o_rwkv, o_conv)):
            o_ref[:, D_RWKV * half:D_RWKV * (half + 1)] = val.astype(BF16)
            ot_ref[D_RWKV * half:D_RWKV * (half + 1), :] = jnp.transpose(val).astype(BF16)

    return pl.pallas_call(
        body, name="post_fwd", grid=(t // tm,), in_specs=_post_specs(tm, t),
        out_specs=[_row(tm, D_MODEL), _col(tm, D_MODEL)],
        out_shape=[_sds((t, D_MODEL), BF16), _sds((D_MODEL, t), BF16)], compiler_params=_cp("parallel"),
    )(y_f, y_b, ps, kd_f, kd_b, ps, gate, pc, pc, pc, qvec, ones_blocks)


def _post_bwd(d_out, y_f, y_b, ps, kd_f, kd_b, gate, pc, qvec, ones_blocks, tm, seq):
    t = ps.shape[0]
    do_prev, do_next = _halo_specs(tm, D_MODEL, t)

    def body(do_ref, dop_ref, don_ref, yf_ref, yb_ref, r_ref, kdf_ref, kdb_ref, v_ref, g_ref, pc_ref, hp_ref,
             hn_ref, qv_ref, ones_ref, dy_ref, dr_ref, dkdf_ref, dkdb_ref, dv_ref, dg_ref, dpc_ref, dqv_ref):
        has_prev, has_next = _edge_flags(tm, seq)
        vec = [jnp.broadcast_to(qv_ref[j:j + 1, :], (tm, D_RWKV)) for j in range(7)]
        ones_v = ones_ref[...]
        args = (yf_ref[...] + yb_ref[...], r_ref[...], kdf_ref[...], kdb_ref[...], v_ref[...], g_ref[...],
                vec[0], vec[1], vec[2], vec[3])
        _, vjp = jax.vjp(lambda *a: _post_math(*a, ones_v), *args)
        grads = vjp(do_ref[:, 0:D_RWKV])
        for o_ref, gval in zip((dy_ref, dr_ref, dkdf_ref, dkdb_ref, dv_ref, dg_ref), grads[0:6]):
            o_ref[...] = gval

        hp, hn = hp_ref[...], hn_ref[...]
        gate_b, gate_c, hid, u, u_down, u_up = _conv_parts(pc_ref[...], hp, hn, has_prev, has_next)
        d_oc = do_ref[:, D_RWKV:2 * D_RWKV]
        d_cu = d_oc * gate_b
        d_cu_prev = dop_ref[GROUP - 1:GROUP, D_RWKV:2 * D_RWKV] * hp[GROUP - 1:GROUP, 0:512] * has_prev
        d_cu_next = don_ref[0:1, D_RWKV:2 * D_RWKV] * hn[0:1, 0:512] * has_next
        d_cu_down, d_cu_up = _shifted(d_cu, d_cu_prev, d_cu_next)
        d_u = vec[5] * d_cu + vec[4] * d_cu_up + vec[6] * d_cu_down
        dpc_ref[:, 0:512] = (d_oc * (vec[4] * u_down + vec[5] * u + vec[6] * u_up)).astype(BF16)
        dpc_ref[:, 512:1024] = (d_u * hid).astype(BF16)
        dpc_ref[:, 1024:1536] = (d_u * gate_c).astype(BF16)

        @pl.when(pl.program_id(0) == 0)
        def _():
            dqv_ref[...] = jnp.zeros_like(dqv_ref)

        vec_grads = list(grads[6:10]) + [d_cu * u_down, d_cu * u, d_cu * u_up]
        for j, gval in enumerate(vec_grads):
            dqv_ref[j:j + 1, :] += jnp.sum(gval, axis=0, keepdims=True)

    return pl.pallas_call(
        body, name="post_bwd", grid=(t // tm,),
        in_specs=[_row(tm, D_MODEL), do_prev, do_next] + _post_specs(tm, t),
        out_specs=[_row(tm, D_RWKV)] * 6 + [_row(tm, D_CONV3), _fixed((8, D_RWKV))],
        out_shape=[_sds((t, D_RWKV))] * 6 + [_sds((t, D_CONV3), BF16), _sds((8, D_RWKV))],
        compiler_params=_cp("arbitrary"),
    )(d_out, d_out, d_out, y_f, y_b, ps, kd_f, kd_b, ps, gate, pc, pc, pc, qvec, ones_blocks)


def _adamw_math(wv, gv, mv, vv):
    m2 = ADAM_B1 * mv + (1.0 - ADAM_B1) * gv
    v2 = ADAM_B2 * vv + (1.0 - ADAM_B2) * (gv * gv)
    m_hat = m2 / (1.0 - ADAM_B1 ** ADAM_STEP)
    v_hat = v2 / (1.0 - ADAM_B2 ** ADAM_STEP)
    return -ADAM_LR * (m_hat / (jnp.sqrt(v_hat) + ADAM_EPS) + ADAM_WD * wv), m2, v2


def _adamw_small(items):
    n = len(items)

    def body(*refs):
        ins, outs = refs[:4 * n], refs[4 * n:]
        for k in range(n):
            w_ref, g_ref, m_ref, v_ref = ins[4 * k:4 * k + 4]
            for o_ref, val in zip(outs[3 * k:3 * k + 3], _adamw_math(w_ref[...], g_ref[...], m_ref[...], v_ref[...])):
                o_ref[...] = val

    flat = [a for item in items for a in item]
    outs = pl.pallas_call(
        body, name="adamw_small", out_shape=[_sds(item[0].shape) for item in items for _ in range(3)],
        compiler_params=_cp())(*flat)
    return [tuple(outs[3 * k:3 * k + 3]) for k in range(n)]


def _adamw(w, g, m, v, name):
    r, c = w.shape[-2:]
    tr = _tile(r, 256, 8)
    if w.ndim == 3:
        spec = pl.BlockSpec((None, tr, c), lambda i: (0, i, 0))
    else:
        spec = pl.BlockSpec((tr, c), lambda i: (i, 0))

    def body(w_ref, g_ref, m_ref, v_ref, d_ref, nm_ref, nv_ref):
        d_ref[...], nm_ref[...], nv_ref[...] = _adamw_math(w_ref[...], g_ref[...], m_ref[...], v_ref[...])

    return pl.pallas_call(
        body, name=name, grid=(r // tr,), in_specs=[spec] * 4, out_specs=[spec] * 3,
        out_shape=[_sds(w.shape)] * 3, compiler_params=_cp("parallel"))(w, g, m, v)


_ANY = pl.BlockSpec(memory_space=pl.ANY)


def _place():
    return lax.axis_index("x"), lax.axis_index("y"), lax.axis_index("c")


def _other_chips(x, y):
    return [(1 - x, y), (x, 1 - y), (1 - x, 1 - y)]


def _remote(src, dst, send_sems, recv_sems, k, to):
    return pltpu.make_async_remote_copy(src_ref=src, dst_ref=dst, send_sem=send_sems.at[k],
                                        recv_sem=recv_sems.at[k], device_id=to, device_id_type=MESH)


def _gather_weights(pack):
    rows, width = pack.shape
    half = rows // 2

    def body(x_ref, out_ref, send_sems, recv_sems):
        x, y, c = _place()
        sibling = (x, y, 1 - c)
        chips = _other_chips(x, y)

        def block(chip, part):
            return out_ref.at[2 * chip[0] + chip[1], pl.ds(part * half, half), :]

        first = [_remote(x_ref.at[pl.ds(c * half, half), :], block((x, y), c), send_sems, recv_sems, j, (*chip, c))
                 for j, chip in enumerate(chips)]
        for cp in first:
            cp.start()
        passed = [_remote(block(chip, c), block(chip, c), send_sems, recv_sems, 3 + j, sibling)
                  for j, chip in enumerate(chips)]
        for j, chip in enumerate(chips):
            _remote(block(chip, c), block(chip, c), send_sems, recv_sems, j, sibling).wait_recv()
            passed[j].start()
        for j, chip in enumerate(chips):
            _remote(block(chip, 1 - c), block(chip, 1 - c), send_sems, recv_sems, 3 + j, sibling).wait_recv()
        for cp in first + passed:
            cp.wait_send()

    return pl.pallas_call(
        body, name="gather_weights", in_specs=[_ANY], out_specs=_ANY,
        out_shape=_sds((N_SHARD, rows, width), pack.dtype),
        scratch_shapes=[pltpu.SemaphoreType.DMA((6,)), pltpu.SemaphoreType.DMA((6,))],
    )(pack)


_HBM = pl.BlockSpec(memory_space=pltpu.HBM)
_SEMS = pl.BlockSpec(memory_space=pltpu.SEMAPHORE)
_DATAFLOW = pltpu.SideEffectType.DATAFLOW_SIDE_EFFECTING


def _fetch_start(pack, after):
    def body(x_ref, land_ref, after_ref, send_sems, recv_sems, x_thru, land_thru, token):
        x, y, c = _place()
        for j, chip in enumerate(_other_chips(x, y)):
            _remote(x_ref, land_ref.at[2 * x + y], send_sems, recv_sems, j, (*chip, c)).start()
        token[...] = jnp.zeros_like(token)

    land = lax.empty((N_SHARD,) + pack.shape, pack.dtype)
    return pl.pallas_call(
        body, name="fetch_ffn_start",
        out_shape=(pltpu.SemaphoreType.DMA((3,)), pltpu.SemaphoreType.DMA((3,)), pltpu.HBM(pack.shape, pack.dtype),
                   pltpu.HBM(land.shape, land.dtype), _sds((8, 128))),
        in_specs=(_HBM, _HBM, _ANY), out_specs=(_SEMS, _SEMS, _HBM, _HBM, pl.BlockSpec(memory_space=pltpu.VMEM)),
        input_output_aliases={0: 2, 1: 3}, compiler_params=pltpu.CompilerParams(has_side_effects=_DATAFLOW),
    )(pltpu.with_memory_space_constraint(pack, pltpu.HBM), pltpu.with_memory_space_constraint(land, pltpu.HBM), after)


def _fetch_wait(send_sems, recv_sems, pack_thru, land_thru, after):
    def body(x_ref, land_ref, send_sems, recv_sems, after_ref, x_dead, got_ref):
        x, y, c = _place()
        for j, chip in enumerate(_other_chips(x, y)):
            cp = _remote(x_ref, land_ref.at[2 * chip[0] + chip[1]], send_sems, recv_sems, j, (*chip, c))
            cp.wait_send()
            cp.wait_recv()

    return pl.pallas_call(
        body, name="fetch_ffn_wait",
        out_shape=(pltpu.HBM(pack_thru.shape, pack_thru.dtype), pltpu.HBM(land_thru.shape, land_thru.dtype)),
        in_specs=(_HBM, _HBM, _SEMS, _SEMS, _ANY), out_specs=(_HBM, _HBM), input_output_aliases={0: 0, 1: 1},
        compiler_params=pltpu.CompilerParams(has_side_effects=_DATAFLOW),
    )(pack_thru, land_thru, send_sems, recv_sems, after)[1]


def _swap_with_sibling(block, name):
    def body(x_ref, out_ref, send_sems, recv_sems):
        x, y, c = _place()
        cp = _remote(x_ref, out_ref, send_sems, recv_sems, 0, (x, y, 1 - c))
        cp.start()
        cp.wait()

    return pl.pallas_call(
        body, name=name, in_specs=[_ANY], out_specs=_ANY, out_shape=_sds(block.shape, block.dtype),
        scratch_shapes=[pltpu.SemaphoreType.DMA((1,)), pltpu.SemaphoreType.DMA((1,))],
    )(block)


def _swap_other_half(packed, tag):
    slots, rows, width = packed.shape
    half = rows // 2

    def body(x_ref, out_ref, send_sems, recv_sems):
        x, y, c = _place()
        cp = _remote(x_ref.at[:, pl.ds((1 - c) * half, half), :], out_ref, send_sems, recv_sems, 0, (x, y, 1 - c))
        cp.start()
        cp.wait()

    return pl.pallas_call(
        body, name="swap_halves_" + tag, in_specs=[_ANY], out_specs=_ANY, out_shape=_sds((slots, half, width)),
        scratch_shapes=[pltpu.SemaphoreType.DMA((1,)), pltpu.SemaphoreType.DMA((1,))],
    )(packed)


def _add_halves(packed, got, c, tag):
    slots, rows, width = packed.shape
    half = rows // 2
    tr = _tile(half, 408, 16)
    per = half // tr
    block = (None, tr, width)

    def body(c_ref, mine_ref, got_ref, sum_ref, sum16_ref):
        acc = mine_ref[...] + got_ref[...]
        sum_ref[...] = acc
        sum16_ref[...] = acc.astype(BF16)

    plain = pl.BlockSpec(block, lambda s, i, c_ref: (s, i, 0))
    grid_spec = pltpu.PrefetchScalarGridSpec(
        num_scalar_prefetch=1, grid=(slots, per),
        in_specs=[pl.BlockSpec(block, lambda s, i, c_ref: (s, c_ref[0] * per + i, 0)), plain],
        out_specs=[plain, plain])
    return pl.pallas_call(
        body, name="add_halves_" + tag, grid_spec=grid_spec,
        out_shape=[_sds((slots, half, width)), _sds((slots, half, width), BF16)],
        compiler_params=_cp("parallel", "parallel"))(c.reshape(1).astype(jnp.int32), packed, got)


def _add_quarters(chip_sum, others, chip, tag):
    _, rows, width = chip_sum.shape
    tr = _tile(rows, 408, 16)

    def body(chip_ref, own_ref, others_ref, o_ref):
        acc = own_ref[...]
        for j in range(3):
            acc = acc + others_ref[j].astype(F32)
        o_ref[...] = acc

    grid_spec = pltpu.PrefetchScalarGridSpec(
        num_scalar_prefetch=1, grid=(rows // tr,),
        in_specs=[pl.BlockSpec((None, tr, width), lambda i, chip_ref: (chip_ref[0], i, 0)),
                  pl.BlockSpec((3, tr, width), lambda i, chip_ref: (0, i, 0))],
        out_specs=pl.BlockSpec((tr, width), lambda i, chip_ref: (i, 0)))
    return pl.pallas_call(
        body, name="add_quarters_" + tag, grid_spec=grid_spec, out_shape=_sds((rows, width)),
        compiler_params=_cp("parallel"))(chip.reshape(1).astype(jnp.int32), chip_sum, others)


def _exchange_start(parts, tag):
    _, rows, width = parts.shape

    def body(x_ref, land_ref, send_sems, recv_sems, x_thru, land_thru, token):
        x, y, c = _place()
        for j, chip in enumerate(_other_chips(x, y)):
            _remote(x_ref.at[2 * chip[0] + chip[1]], land_ref.at[j], send_sems, recv_sems, j, (*chip, c)).start()
        token[...] = jnp.zeros_like(token)

    land = lax.empty((3, rows, width), parts.dtype)
    return pl.pallas_call(
        body, name="exchange_" + tag + "_start",
        out_shape=(pltpu.SemaphoreType.DMA((3,)), pltpu.SemaphoreType.DMA((3,)), pltpu.HBM(parts.shape, parts.dtype),
                   pltpu.HBM(land.shape, land.dtype), _sds((8, 128))),
        in_specs=(_HBM, _HBM), out_specs=(_SEMS, _SEMS, _HBM, _HBM, pl.BlockSpec(memory_space=pltpu.VMEM)),
        input_output_aliases={0: 2, 1: 3}, compiler_params=pltpu.CompilerParams(has_side_effects=_DATAFLOW),
    )(pltpu.with_memory_space_constraint(parts, pltpu.HBM), pltpu.with_memory_space_constraint(land, pltpu.HBM))


def _exchange_wait(send_sems, recv_sems, parts_thru, land_thru, after, tag):
    def body(x_ref, land_ref, send_sems, recv_sems, after_ref, x_dead, got_ref):
        x, y, c = _place()
        for j, chip in enumerate(_other_chips(x, y)):
            cp = _remote(x_ref.at[2 * chip[0] + chip[1]], land_ref.at[j], send_sems, recv_sems, j, (*chip, c))
            cp.wait_send()
            cp.wait_recv()

    return pl.pallas_call(
        body, name="exchange_" + tag + "_wait",
        out_shape=(pltpu.HBM(parts_thru.shape, parts_thru.dtype), pltpu.HBM(land_thru.shape, land_thru.dtype)),
        in_specs=(_HBM, _HBM, _SEMS, _SEMS, _ANY), out_specs=(_HBM, _HBM), input_output_aliases={0: 0, 1: 1},
        compiler_params=pltpu.CompilerParams(has_side_effects=_DATAFLOW),
    )(parts_thru, land_thru, send_sems, recv_sems, after)[1]


def _allreduce_small(vec):
    rows, width = vec.shape
    vmem = pl.BlockSpec(memory_space=pltpu.VMEM)

    def body(x_ref, o_ref, buf_ref, send_sems, recv_sems):
        x, y, c = _place()
        me = 4 * x + 2 * y + c
        buf_ref[me] = x_ref[...]
        copies = []
        for k in range(1, N_DEV):
            peer = (x ^ ((k >> 2) & 1), y ^ ((k >> 1) & 1), c ^ (k & 1))
            copies.append(_remote(x_ref, buf_ref.at[me], send_sems, recv_sems, k - 1, peer))
        for cp in copies:
            cp.start()
        for k in range(1, N_DEV):
            _remote(x_ref, buf_ref.at[me ^ k], send_sems, recv_sems, k - 1, (x, y, c)).wait_recv()
        for cp in copies:
            cp.wait_send()
        total = buf_ref[0]
        for d in range(1, N_DEV):
            total = total + buf_ref[d]
        o_ref[...] = total

    return pl.pallas_call(
        body, name="allreduce_small", in_specs=[vmem], out_specs=vmem, out_shape=_sds((rows, width)),
        scratch_shapes=[pltpu.VMEM((N_DEV, rows, width), F32), pltpu.SemaphoreType.DMA((N_DEV - 1,)),
                        pltpu.SemaphoreType.DMA((N_DEV - 1,))],
    )(vec)


def _rows1024(a):
    return a.reshape(-1, 1024)


def _pad_rows(a, rows):
    return jnp.concatenate([a, jnp.zeros((rows - a.shape[0], a.shape[1]), a.dtype)], axis=0)


_TRANSPOSED = ("w_in", "w_gate", "w_up")
_SMALL_SHARDED = ("w_up_f", "w_up_b", "a_up_f", "a_up_b", "g_up")
_BIG_SHARDED = ("w_in", "w_out", "w_gate", "w_up", "w_down")


def _pack_weight_shards(w):
    conv_bits = lax.bitcast_convert_type(w["conv_w"], BF16).reshape(1, -1)
    conv_row = jnp.concatenate([conv_bits, jnp.zeros((1, 1024 - conv_bits.shape[1]), BF16)], axis=1)

    def rows(name):
        a = w[name].astype(BF16)
        return a.T if name in _TRANSPOSED else _rows1024(a)

    early = _pad_rows(jnp.concatenate([rows(name) for name, _ in _EARLY_ROWS[:-1]] + [conv_row], axis=0), EARLY_R)
    return early, jnp.concatenate([rows(name) for name, _ in _FFN_ROWS], axis=0)


def _split_rows(gathered, layout):
    out, row = {}, 0
    for name, n in layout:
        out[name] = gathered[:, row:row + n]
        row += n
    return out


def _unpack_early(gathered):
    out = _split_rows(gathered, _EARLY_ROWS)
    cols = lambda a, k: jnp.concatenate([a[s].reshape(k, -1) for s in range(N_SHARD)], axis=1)
    conv = lax.bitcast_convert_type(out["conv_w"][:, 0, :768].reshape(N_SHARD, 3, 128, 2), F32)
    full = dict(w_in=out["w_in"].reshape(-1, 1024).T, w_out=out["w_out"].reshape(D_MODEL, D_MODEL),
                conv_w=jnp.concatenate([conv[s] for s in range(N_SHARD)], axis=1))
    full.update({name: cols(out[name], D_GATE if name == "g_up" else D_LORA) for name in _SMALL_SHARDED})
    return full


def _unpack_ffn(gathered):
    out = _split_rows(gathered, _FFN_ROWS)
    return dict(w_gate=out["w_gate"].reshape(-1, 1024).T, w_up=out["w_up"].reshape(-1, 1024).T,
                w_down=out["w_down"].reshape(D_FF, D_MODEL))


def _pack_grads(g, layout, rows):
    col_split = lambda a, s: a[:, s * (a.shape[1] // N_SHARD):(s + 1) * (a.shape[1] // N_SHARD)]
    row_split = lambda a, s: a[s * (a.shape[0] // N_SHARD):(s + 1) * (a.shape[0] // N_SHARD)]
    by_rows = {name: (g[name].T if name in _TRANSPOSED else g[name]) for name, _ in layout if name in _BIG_SHARDED}
    used = sum(n for _, n in layout)
    parts = []
    for s in range(N_SHARD):
        for name, _ in layout:
            if name in _BIG_SHARDED:
                parts.append(row_split(by_rows[name], s))
            elif name in _SMALL_SHARDED:
                parts.append(_rows1024(col_split(g[name], s)))
            else:
                conv = col_split(g["conv_w"], s).reshape(1, -1)
                parts.append(jnp.concatenate([conv, jnp.zeros((1, 1024 - conv.shape[1]), F32)], axis=1))
        if rows > used:
            parts.append(jnp.zeros((rows - used, 1024), F32))
    return jnp.concatenate(parts, axis=0).reshape(N_SHARD, rows, 1024)


def _unpack_grad_shard(pack, layout):
    small_shapes = {name: (D_GATE if name == "g_up" else D_LORA, 128) for name in _SMALL_SHARDED}
    out, row = {}, 0
    for name, n in layout:
        piece = pack[row:row + n]
        if name == "conv_w":
            out[name] = piece[0, :384].reshape(3, 128)
        else:
            out[name] = piece.T if name in _TRANSPOSED else piece.reshape(small_shapes.get(name, piece.shape))
        row += n
    return out


_SMALL_LAYOUT = (("norm1_w", 1024), ("mu_shift", D_SHIFT), ("w0_f", 512), ("w0_b", 512), ("a0_f", 512),
                 ("a0_b", 512), ("k_k", 512), ("k_a_f", 512), ("k_a_b", 512), ("r_k_f", 512), ("r_k_b", 512),
                 ("gn_w", 512), ("gn_b", 512), ("norm2_w", 1024), ("norm_f_w", 1024), ("loss", 1))


def _pack_small(vals):
    rows = []
    for name, n in _SMALL_LAYOUT:
        flat = vals[name].reshape(-1)
        n_rows = -(-n // 1024)
        rows.append(jnp.concatenate([flat, jnp.zeros((n_rows * 1024 - n,), F32)]).reshape(n_rows, 1024))
    return _pad_rows(jnp.concatenate(rows, axis=0), SMALL_ROWS)


def _unpack_small(pack):
    out, row = {}, 0
    for name, n in _SMALL_LAYOUT:
        n_rows = -(-n // 1024)
        out[name] = pack[row:row + n_rows].reshape(-1)[:n]
        row += n_rows
    return out


_WEIGHTS = ("norm1_w", "w_in", "mu_shift", "w_up_f", "w0_f", "w_up_b", "w0_b", "a_up_f", "a0_f", "a_up_b", "a0_b",
            "g_up", "k_k", "k_a_f", "k_a_b", "r_k_f", "r_k_b", "gn_w", "gn_b", "conv_w", "w_out", "norm2_w",
            "w_gate", "w_up", "w_down", "norm_f_w")


def _train_step(x, loss_target, w, m, v):
    batch, seq, _ = x.shape
    t = batch * seq
    tm = _tile(seq, 256, 8)
    xs = x.reshape(t, D_MODEL)
    target = loss_target.reshape(t, D_MODEL)
    vec = lambda name: w[name].reshape(1, -1)

    local = {name: w[name][0] for name, _ in _PACK_ROWS}
    c = lax.axis_index("c")
    chip = 2 * lax.axis_index("x") + lax.axis_index("y")
    early, ffn_pack = _pack_weight_shards(local)
    early_all = lax.dynamic_update_slice(_gather_weights(early), early[None], (chip, 0, 0))
    ffn_send, ffn_recv, ffn_pack, ffn_land, token = _fetch_start(ffn_pack, early_all)
    full = _unpack_early(early_all)
    w_in = full["w_in"]
    w_shift = jnp.concatenate([w_in[:, :D_SHIFT], jnp.zeros((D_MODEL, D_SHIFT_PAD - D_SHIFT), BF16)], axis=1)
    w_conv = w_in[:, D_SHIFT:]
    zeros_lora = jnp.zeros((D_LORA, D_RWKV), F32)
    lora = lambda name: full[name].astype(F32)
    mats = (jnp.concatenate([lora("w_up_f"), zeros_lora]), jnp.concatenate([zeros_lora, lora("a_up_f")]),
            jnp.concatenate([lora("w_up_b"), zeros_lora]), jnp.concatenate([zeros_lora, lora("a_up_b")]),
            jnp.concatenate([lora("g_up"), jnp.zeros((D_GATE_PAD - D_GATE, D_RWKV), F32)]))
    mu = jnp.concatenate([vec("mu_shift"), jnp.zeros((1, D_SHIFT_PAD - D_SHIFT), F32)], axis=1)
    mu = jnp.broadcast_to(mu, (GROUP, D_SHIFT_PAD))
    zero_row = jnp.zeros((1, D_RWKV), F32)
    pvec = jnp.concatenate([vec("k_k"), vec("w0_f"), vec("a0_f"), vec("k_a_f"), vec("w0_b"), vec("a0_b"),
                            vec("k_a_b"), zero_row], axis=0)
    qvec = jnp.concatenate([vec("gn_w"), vec("gn_b"), vec("r_k_f"), vec("r_k_b"), full["conv_w"], zero_row], axis=0)
    ones_blocks = _head_ones()

    h1, h1_t = _rmsnorm_fwd(xs, vec("norm1_w") + token[0, 0], tm, "norm1_fwd")
    p_shift = _matmul(h1, w_shift, mode="nn", name="in_proj_shift")
    pc = _matmul(h1, w_conv, mode="nn", name="in_proj_conv")
    ps = _shift_fwd(p_shift, mu, tm, seq)
    kk, w_f, kd_f, b_f, w_b, kd_b, b_b, gate = _prep_fwd(ps, pvec, mats, ones_blocks, tm)
    dirs = ((w_f, kd_f, b_f), (w_b, kd_b, b_b))
    y_f, y_b, hist_f, hist_b = _scan_fwd(ps, kk, dirs, batch, seq)
    mixed, mixed_t = _post_fwd(y_f, y_b, ps, kd_f, kd_b, gate, pc, qvec, ones_blocks, tm, seq)
    x1 = _matmul(mixed, full["w_out"], mode="nn", name="out_proj", add=xs)
    h2, h2_t = _rmsnorm_fwd(x1, vec("norm2_w"), tm, "norm2_fwd")
    ffn_all = _fetch_wait(ffn_send, ffn_recv, ffn_pack, ffn_land, h2)
    full.update(_unpack_ffn(lax.dynamic_update_slice(ffn_all, ffn_pack[None], (chip, 0, 0))))
    ff_gate, ff_up, act, act_t = _ffn_in(h2, full["w_gate"], full["w_up"])
    x2 = _matmul(act, full["w_down"], mode="nn", name="ffn_down", add=x1)
    d_x2, d_norm_f, loss_part = _loss_head(x2, w["norm_f_w"].reshape(1, -1), target, tm)

    g = {}
    g["w_down"] = _matmul(act_t, d_x2, mode="nn", name="ffn_down_dw")
    d_gate, d_up = _ffn_in_bwd(d_x2, full["w_down"], ff_gate, ff_up)
    d_h2 = _matmul(d_gate, full["w_gate"], mode="nt", name="ffn_gate_dx")
    d_h2 = _matmul(d_up, full["w_up"], mode="nt", name="ffn_up_dx", add=d_h2)
    g["w_gate"] = _matmul(h2_t, d_gate, mode="nn", name="ffn_gate_dw")
    g["w_up"] = _matmul(h2_t, d_up, mode="nn", name="ffn_up_dw")
    ffn_grads = _pack_grads(g, _FFN_ROWS, sum(n for _, n in _FFN_ROWS))
    ffn_sum, ffn_sum_bf16 = _add_halves(ffn_grads, _swap_other_half(ffn_grads, "ffn"), c, "ffn")
    ex_send, ex_recv, ffn_sum_bf16, ex_land, ex_token = _exchange_start(ffn_sum_bf16, "ffn")
    d_x1, d_norm2 = _rmsnorm_bwd(x1, vec("norm2_w") + ex_token[0, 0], d_h2, d_x2, tm, "norm2_bwd")
    d_mixed = _matmul(d_x1, full["w_out"], mode="nt", name="out_proj_dx")
    g["w_out"] = _matmul(mixed_t, d_x1, mode="nn", name="out_proj_dw")
    dy, dr_o, dkdf_o, dkdb_o, dv_o, d_gatev, d_pc, d_qvec = _post_bwd(
        d_mixed, y_f, y_b, ps, kd_f, kd_b, gate, pc, qvec, ones_blocks, tm, seq)
    (dr_f, dw_f, dkd_f, dv_f, dkk_f, db_f), (dr_b, dw_b, dkd_b, dv_b, dkk_b, db_b) = _scan_bwd(
        ps, kk, dirs, dy, hist_f, hist_b, batch, seq)
    cts = [[dr_f, dr_b, dr_o], [dv_f, dv_b, dv_o], [dkk_f, dkk_b], [dw_f], [dkd_f, dkdf_o], [db_f],
           [dw_b], [dkd_b, dkdb_o], [db_b], [d_gatev]]
    q, d_pvec, d_m0, d_m1, d_m2, d_m3, d_m4 = _prep_bwd(ps, pvec, mats, ones_blocks, cts, tm)
    d_pshift, d_mu = _shift_bwd(q, p_shift, mu, tm, seq)
    d_h1 = _matmul(d_pshift, w_shift, mode="nt", name="in_proj_shift_dx")
    d_h1 = _matmul(d_pc, w_conv, mode="nt", name="in_proj_conv_dx", add=d_h1)
    d_w_shift = _matmul(h1_t, d_pshift, mode="nn", name="in_proj_shift_dw")
    d_w_conv = _matmul(h1_t, d_pc, mode="nn", name="in_proj_conv_dw")
    g["w_in"] = jnp.concatenate([d_w_shift[:, :D_SHIFT], d_w_conv], axis=1)
    d_x, d_norm1 = _rmsnorm_bwd(xs, vec("norm1_w"), d_h1, d_x1, tm, "norm1_bwd")
    g["w_up_f"], g["a_up_f"] = d_m0[:D_LORA], d_m1[D_LORA:]
    g["w_up_b"], g["a_up_b"] = d_m2[:D_LORA], d_m3[D_LORA:]
    g["g_up"] = d_m4[:D_GATE]
    g["conv_w"] = d_qvec[4:7]

    def finish(chip_sum, others, tag, layout):
        eighth = _add_quarters(chip_sum, others, chip, tag)
        other_eighth = _swap_with_sibling(eighth, "swap_eighths_" + tag)
        return _unpack_grad_shard(jnp.concatenate([jnp.where(c == 0, eighth, other_eighth),
                                                   jnp.where(c == 0, other_eighth, eighth)], axis=0), layout)

    as2d = lambda name: (1, w[name].shape[0]) if w[name].ndim == 1 else w[name].shape
    operands = lambda name: tuple(a.reshape(as2d(name)) for a in (w[name], grads[name], m[name], v[name]))

    packed = _pack_grads(g, _EARLY_ROWS, EARLY_R)
    mix_sum, mix_sum_bf16 = _add_halves(packed, _swap_other_half(packed, "mixer"), c, "mixer")
    mx_send, mx_recv, mix_sum_bf16, mx_land, mx_token = _exchange_start(mix_sum_bf16, "mixer")
    grads = finish(ffn_sum, _exchange_wait(ex_send, ex_recv, ffn_sum_bf16, ex_land, mx_token, "ffn"), "ffn", _FFN_ROWS)
    updates = {name: _adamw(*operands(name), "adamw_" + name) for name in _FFN_NAMES}
    mix_others = _exchange_wait(mx_send, mx_recv, mix_sum_bf16, mx_land, updates["w_down"][2], "mixer")
    grads.update(finish(mix_sum, mix_others, "mixer", _EARLY_ROWS))

    small = dict(norm1_w=d_norm1, mu_shift=d_mu[:, :D_SHIFT], w0_f=d_pvec[1], w0_b=d_pvec[4], a0_f=d_pvec[2],
                 a0_b=d_pvec[5], k_k=d_pvec[0], k_a_f=d_pvec[3], k_a_b=d_pvec[6], r_k_f=d_qvec[2], r_k_b=d_qvec[3],
                 gn_w=d_qvec[0], gn_b=d_qvec[1], norm2_w=d_norm2, norm_f_w=d_norm_f, loss=loss_part)
    reduced = _unpack_small(_allreduce_small(_pack_small(small)))
    loss = reduced.pop("loss")[0]
    grads.update(reduced)

    outs = {}
    small = [name for name in _WEIGHTS if name not in _BIG_SHARDED]
    updates.update(zip(small, _adamw_small([operands(name) for name in small])))
    for name in ("w_in", "w_out"):
        updates[name] = _adamw(*operands(name), "adamw_" + name)
    for name in _WEIGHTS:
        shape = w[name].shape
        outs[name] = (grads[name].reshape(shape),) + tuple(a.reshape(shape) for a in updates[name])
    d_x = d_x.reshape(batch, seq, D_MODEL)
    return (loss, d_x) + tuple(outs[name][k] for k in range(4) for name in _WEIGHTS)


def kernel(x, norm1_w, w_in, mu_shift, w_up_f, w0_f, w_up_b, w0_b, a_up_f, a0_f, a_up_b, a0_b, g_up, k_k, k_a_f, k_a_b, r_k_f, r_k_b, gn_w, gn_b, conv_w, w_out, norm2_w, w_gate, w_up, w_down, norm_f_w, loss_target, m_norm1_w, m_w_in, m_mu_shift, m_w_up_f, m_w0_f, m_w_up_b, m_w0_b, m_a_up_f, m_a0_f, m_a_up_b, m_a0_b, m_g_up, m_k_k, m_k_a_f, m_k_a_b, m_r_k_f, m_r_k_b, m_gn_w, m_gn_b, m_conv_w, m_w_out, m_norm2_w, m_w_gate, m_w_up, m_w_down, m_norm_f_w, v_norm1_w, v_w_in, v_mu_shift, v_w_up_f, v_w0_f, v_w_up_b, v_w0_b, v_a_up_f, v_a0_f, v_a_up_b, v_a0_b, v_g_up, v_k_k, v_k_a_f, v_k_a_b, v_r_k_f, v_r_k_b, v_gn_w, v_gn_b, v_conv_w, v_w_out, v_norm2_w, v_w_gate, v_w_up, v_w_down, v_norm_f_w):
    args = locals()
    w = {name: args[name] for name in _WEIGHTS}
    m = {name: args["m_" + name] for name in _WEIGHTS}
    v = {name: args["v_" + name] for name in _WEIGHTS}
    return _train_step(x, loss_target, w, m, v)
```

```python
import functools

import jax
import jax.numpy as jnp
from jax import lax
from jax.experimental import pallas as pl
from jax.experimental.pallas import tpu as pltpu

F32 = jnp.float32
BF16 = jnp.bfloat16
MESH = pl.DeviceIdType.MESH

D_MODEL = 1024
D_RWKV = 512
HEAD = 64
N_PAIR = D_RWKV // (2 * HEAD)
D_LORA = 64
D_GATE = 160
D_GATE_PAD = 384
D_FF = 2816
D_SHIFT = 1824
D_SHIFT_PAD = 2048
D_CONV3 = 1536
LOG_DECAY_SCALE = 0.606531
RMS_EPS = 1e-6
GN_EPS = 64e-5
NORM_EPS = 1e-12
ADAM_LR, ADAM_B1, ADAM_B2, ADAM_EPS, ADAM_WD, ADAM_STEP = 0.001, 0.9, 0.999, 1e-08, 0.01, 10

N_SHARD = 4
N_DEV = 8
V7X_VMEM_LIMIT = 48 * 1024 * 1024
SCAN_CHUNK = 32
GROUP = 8

_PACK_ROWS = (("w_in", 840), ("w_out", 256), ("w_gate", 704), ("w_up", 704), ("w_down", 704),
              ("w_up_f", 8), ("w_up_b", 8), ("a_up_f", 8), ("a_up_b", 8), ("g_up", 20), ("conv_w", 1))
_FFN_NAMES = ("w_gate", "w_up", "w_down")
_EARLY_ROWS = tuple(item for item in _PACK_ROWS if item[0] not in _FFN_NAMES)
_FFN_ROWS = tuple(item for item in _PACK_ROWS if item[0] in _FFN_NAMES)
EARLY_R = 1152
SMALL_ROWS = 24


def _tile(n, cap, mult=128):
    best = None
    t = mult
    while t <= min(n, cap):
        if n % t == 0:
            best = t
        t += mult
    return best or n


def _cp(*sem):
    return pltpu.CompilerParams(dimension_semantics=sem or None, vmem_limit_bytes=V7X_VMEM_LIMIT)


def _sds(shape, dtype=F32):
    return jax.ShapeDtypeStruct(shape, dtype)


def _matmul(a, b, *, mode, name, out_dtype=F32, add=None):
    m, kdim = a.shape
    n = b.shape[1] if mode == "nn" else b.shape[0]
    tm, tn = _tile(m, 512, 8), _tile(n, 1536)
    tk = kdim if kdim <= 3584 else _tile(kdim, 1024)
    nk = kdim // tk
    a_spec = pl.BlockSpec((tm, tk), lambda i, j, k: (i, k))
    if mode == "nn":
        b_spec = pl.BlockSpec((tk, tn), lambda i, j, k: (k, j))
        dims = (((1,), (0,)), ((), ()))
    else:
        b_spec = pl.BlockSpec((tn, tk), lambda i, j, k: (j, k))
        dims = (((1,), (1,)), ((), ()))
    has_add = add is not None

    def body(*refs):
        a_ref, b_ref = refs[0], refs[1]
        add_ref = refs[2] if has_add else None
        o_ref = refs[3] if has_add else refs[2]
        part = lax.dot_general(a_ref[...].astype(BF16), b_ref[...].astype(BF16), dims,
                               preferred_element_type=F32)
        if nk == 1:
            if has_add:
                part = part + add_ref[...]
            o_ref[...] = part.astype(out_dtype)
        else:
            acc_ref = refs[-1]
            k = pl.program_id(2)

            @pl.when(k == 0)
            def _():
                acc_ref[...] = jnp.zeros_like(acc_ref)

            acc_ref[...] += part

            @pl.when(k == nk - 1)
            def _():
                res = acc_ref[...]
                if has_add:
                    res = res + add_ref[...]
                o_ref[...] = res.astype(out_dtype)

    o_spec = pl.BlockSpec((tm, tn), lambda i, j, k: (i, j))
    in_specs = [a_spec, b_spec] + ([o_spec] if has_add else [])
    args = (a, b) + ((add,) if has_add else ())
    return pl.pallas_call(
        body, name=name, grid=(m // tm, n // tn, nk), in_specs=in_specs, out_specs=o_spec,
        out_shape=_sds((m, n), out_dtype),
        scratch_shapes=[pltpu.VMEM((tm, tn), F32)] if nk > 1 else [],
        compiler_params=_cp("parallel", "parallel", "arbitrary"),
    )(*args)


def _row(tm, width):
    return pl.BlockSpec((tm, width), lambda i: (i, 0))


def _col(tm, height):
    return pl.BlockSpec((height, tm), lambda i: (0, i))


def _fixed(shape):
    return pl.BlockSpec(shape, lambda i: tuple(0 for _ in shape))


def _rmsnorm_fwd(x, w, tm, name):
    t, d = x.shape

    def body(x_ref, w_ref, o_ref, ot_ref):
        xv = x_ref[...]
        rstd = lax.rsqrt(jnp.mean(xv * xv, axis=-1, keepdims=True) + RMS_EPS)
        yv = xv * rstd * w_ref[...]
        o_ref[...] = yv.astype(BF16)
        ot_ref[...] = jnp.transpose(yv).astype(BF16)

    return pl.pallas_call(
        body, name=name, grid=(t // tm,), in_specs=[_row(tm, d), _fixed((1, d))],
        out_specs=[_row(tm, d), _col(tm, d)], out_shape=[_sds((t, d), BF16), _sds((d, t), BF16)],
        compiler_params=_cp("parallel"))(x, w)


def _rms_bwd_math(xv, wv, dyv):
    rstd = lax.rsqrt(jnp.mean(xv * xv, axis=-1, keepdims=True) + RMS_EPS)
    xhat = xv * rstd
    gv = dyv * wv
    dx = rstd * (gv - xhat * jnp.mean(gv * xhat, axis=-1, keepdims=True))
    return dx, jnp.sum(dyv * xhat, axis=0, keepdims=True)


def _rmsnorm_bwd(x, w, dy, dres, tm, name):
    t, d = x.shape

    def body(x_ref, w_ref, dy_ref, dres_ref, dx_ref, dw_ref):
        dx, dw = _rms_bwd_math(x_ref[...], w_ref[...], dy_ref[...])
        dx_ref[...] = dres_ref[...] + dx

        @pl.when(pl.program_id(0) == 0)
        def _():
            dw_ref[...] = jnp.zeros_like(dw_ref)

        dw_ref[...] += dw

    return pl.pallas_call(
        body, name=name, grid=(t // tm,),
        in_specs=[_row(tm, d), _fixed((1, d)), _row(tm, d), _row(tm, d)],
        out_specs=[_row(tm, d), _fixed((1, d))],
        out_shape=[_sds((t, d)), _sds((1, d))], compiler_params=_cp("arbitrary"))(x, w, dy, dres)


def _loss_head(x, w, target, tm):
    t, d = x.shape

    def body(x_ref, w_ref, t_ref, dx_ref, dw_ref, loss_ref):
        xv, wv = x_ref[...], w_ref[...]
        rstd = lax.rsqrt(jnp.mean(xv * xv, axis=-1, keepdims=True) + RMS_EPS)
        err = xv * rstd * wv - t_ref[...]
        dx, dw = _rms_bwd_math(xv, wv, err * (1.0 / d))
        dx_ref[...] = dx

        @pl.when(pl.program_id(0) == 0)
        def _():
            dw_ref[...] = jnp.zeros_like(dw_ref)
            loss_ref[...] = jnp.zeros_like(loss_ref)

        dw_ref[...] += dw
        loss_ref[...] += 0.5 * jnp.sum(jnp.mean(err * err, axis=-1, keepdims=True), axis=0, keepdims=True)

    return pl.pallas_call(
        body, name="loss_head", grid=(t // tm,),
        in_specs=[_row(tm, d), _fixed((1, d)), _row(tm, d)],
        out_specs=[_row(tm, d), _fixed((1, d)), _fixed((1, 1))],
        out_shape=[_sds((t, d)), _sds((1, d)), _sds((1, 1))], compiler_params=_cp("arbitrary"))(x, w, target)


def _ffn_in(h, w_gate, w_up):
    t, d = h.shape
    f = w_gate.shape[1]
    tm, tn = _tile(t, 512, 8), _tile(f, 1536)

    def body(h_ref, wg_ref, wu_ref, g_ref, u_ref, a_ref, at_ref):
        hv = h_ref[...]
        gv = jnp.dot(hv, wg_ref[...], preferred_element_type=F32)
        uv = jnp.dot(hv, wu_ref[...], preferred_element_type=F32)
        act = gv * jax.nn.sigmoid(gv) * uv
        g_ref[...] = gv.astype(BF16)
        u_ref[...] = uv.astype(BF16)
        a_ref[...] = act.astype(BF16)
        at_ref[...] = jnp.transpose(act).astype(BF16)

    w_spec = pl.BlockSpec((d, tn), lambda i, j: (0, j))
    o_spec = pl.BlockSpec((tm, tn), lambda i, j: (i, j))
    return pl.pallas_call(
        body, name="ffn_in", grid=(t // tm, f // tn),
        in_specs=[pl.BlockSpec((tm, d), lambda i, j: (i, 0)), w_spec, w_spec],
        out_specs=[o_spec, o_spec, o_spec, pl.BlockSpec((tn, tm), lambda i, j: (j, i))],
        out_shape=[_sds((t, f), BF16)] * 3 + [_sds((f, t), BF16)],
        compiler_params=_cp("parallel", "parallel"))(h, w_gate, w_up)


def _ffn_in_bwd(d_out, w_down, gate, up):
    t, d = d_out.shape
    f = w_down.shape[0]
    tm, tn = _tile(t, 512, 8), _tile(f, 1536)

    def body(do_ref, w_ref, g_ref, u_ref, dg_ref, du_ref):
        dv = lax.dot_general(do_ref[...].astype(BF16), w_ref[...], (((1,), (1,)), ((), ())),
                             preferred_element_type=F32)
        gv, uv = g_ref[...].astype(F32), u_ref[...].astype(F32)
        sg = jax.nn.sigmoid(gv)
        du_ref[...] = (dv * gv * sg).astype(BF16)
        dg_ref[...] = (dv * uv * (sg * (1.0 + gv * (1.0 - sg)))).astype(BF16)

    tile = pl.BlockSpec((tm, tn), lambda i, j: (i, j))
    return pl.pallas_call(
        body, name="ffn_in_bwd", grid=(t // tm, f // tn),
        in_specs=[pl.BlockSpec((tm, d), lambda i, j: (i, 0)), pl.BlockSpec((tn, d), lambda i, j: (j, 0)), tile, tile],
        out_specs=[tile, tile], out_shape=[_sds((t, f), BF16)] * 2,
        compiler_params=_cp("parallel", "parallel"))(d_out, w_down, gate, up)


def _halo_specs(tm, width, rows_total):
    per = tm // GROUP
    last = rows_total // GROUP - 1
    prev = pl.BlockSpec((GROUP, width), lambda i: (jnp.maximum(i * per - 1, 0), 0))
    nxt = pl.BlockSpec((GROUP, width), lambda i: (jnp.minimum((i + 1) * per, last), 0))
    return prev, nxt


def _edge_flags(tm, seq):
    i = pl.program_id(0)
    has_prev = jnp.where((i * tm) % seq == 0, 0.0, 1.0).astype(F32)
    has_next = jnp.where(((i + 1) * tm) % seq == 0, 0.0, 1.0).astype(F32)
    return has_prev, has_next


def _shifted(xv, prev_row, next_row):
    tm = xv.shape[0]
    row = lax.broadcasted_iota(jnp.int32, xv.shape, 0)
    down = jnp.where(row == 0, prev_row, pltpu.roll(xv, 1, axis=0))
    up = jnp.where(row == tm - 1, next_row, pltpu.roll(xv, tm - 1, axis=0))
    return down, up


def _shift_fwd(p, mu, tm, seq):
    t, w = p.shape
    prev_spec, next_spec = _halo_specs(tm, w, t)

    def body(p_ref, hp_ref, hn_ref, mu_ref, o_ref):
        has_prev, has_next = _edge_flags(tm, seq)
        xv = p_ref[...]
        down, up = _shifted(xv, hp_ref[GROUP - 1:GROUP, :] * has_prev, hn_ref[0:1, :] * has_next)
        o_ref[...] = xv + mu_ref[0:1, :] * (0.5 * (down + up) - xv)

    return pl.pallas_call(
        body, name="shift_fwd", grid=(t // tm,),
        in_specs=[_row(tm, w), prev_spec, next_spec, _fixed((GROUP, w))], out_specs=_row(tm, w),
        out_shape=_sds((t, w)), compiler_params=_cp("parallel"))(p, p, p, mu)


def _shift_bwd(q, p, mu, tm, seq):
    t, w = p.shape
    prev_spec, next_spec = _halo_specs(tm, w, t)

    def body(q_ref, qp_ref, qn_ref, p_ref, pp_ref, pn_ref, mu_ref, dp_ref, dmu_ref):
        has_prev, has_next = _edge_flags(tm, seq)
        muv = mu_ref[0:1, :]
        qv = q_ref[...]
        mq = muv * qv
        mq_down, mq_up = _shifted(mq, muv * qp_ref[GROUP - 1:GROUP, :] * has_prev,
                                  muv * qn_ref[0:1, :] * has_next)
        dp_ref[...] = (qv - mq + 0.5 * (mq_down + mq_up)).astype(BF16)
        pv = p_ref[...]
        p_down, p_up = _shifted(pv, pp_ref[GROUP - 1:GROUP, :] * has_prev, pn_ref[0:1, :] * has_next)

        @pl.when(pl.program_id(0) == 0)
        def _():
            dmu_ref[...] = jnp.zeros_like(dmu_ref)

        dmu_ref[...] += jnp.sum(qv * (0.5 * (p_down + p_up) - pv), axis=0, keepdims=True)

    return pl.pallas_call(
        body, name="shift_bwd", grid=(t // tm,),
        in_specs=[_row(tm, w), prev_spec, next_spec, _row(tm, w), prev_spec, next_spec, _fixed((GROUP, w))],
        out_specs=[_row(tm, w), _fixed((1, w))],
        out_shape=[_sds((t, w), BF16), _sds((1, w))], compiler_params=_cp("arbitrary"))(q, q, q, p, p, p, mu)


@jax.custom_vjp
def _bdot(a, b):
    return jnp.dot(a.astype(BF16), b.astype(BF16), preferred_element_type=F32)


def _bdot_fwd(a, b):
    return _bdot(a, b), (a, b)


def _bdot_bwd(res, g):
    a, b = res
    gb = g.astype(BF16)
    da = lax.dot_general(gb, b.astype(BF16), (((1,), (1,)), ((), ())), preferred_element_type=F32)
    db = lax.dot_general(a.astype(BF16), gb, (((0,), (0,)), ((), ())), preferred_element_type=F32)
    return da, db


_bdot.defvjp(_bdot_fwd, _bdot_bwd)


def _seg_raw(x, ones_blocks):
    hi = x.astype(BF16)
    lo = (x - hi.astype(F32)).astype(BF16)
    return (jnp.dot(hi, ones_blocks, preferred_element_type=F32)
            + jnp.dot(lo, ones_blocks, preferred_element_type=F32))


@jax.custom_vjp
def _seg(x, ones_blocks):
    return _seg_raw(x, ones_blocks)


def _seg_fwd(x, ones_blocks):
    return _seg_raw(x, ones_blocks), ones_blocks


def _seg_bwd(ones_blocks, g):
    return _seg_raw(g, ones_blocks), jnp.zeros_like(ones_blocks)


_seg.defvjp(_seg_fwd, _seg_bwd)


def _head_ones():
    h = jnp.arange(D_RWKV) // HEAD
    return (h[:, None] == h[None, :]).astype(BF16)


def _prep_math(ps, k_k, w0_f, a0_f, k_a_f, w0_b, a0_b, k_a_b, wup_f, aup_f, wup_b, aup_b, gup, ones_blocks):
    r = ps[:, 0:512]
    k = ps[:, 512:1024]
    v = ps[:, 1024:1536]
    xwa = ps[:, 1536:1664]
    xg = ps[:, 1664:D_SHIFT_PAD]
    kk_raw = k * k_k
    norm = jnp.sqrt(_seg(kk_raw * kk_raw, ones_blocks))
    kk = kk_raw / jnp.maximum(norm, NORM_EPS)
    t_xwa = jnp.tanh(xwa)
    outs = [r, v, kk]
    for w0, a0, k_a, wup, aup in ((w0_f, a0_f, k_a_f, wup_f, aup_f), (w0_b, a0_b, k_a_b, wup_b, aup_b)):
        decay = jnp.exp(-LOG_DECAY_SCALE * jax.nn.sigmoid(w0 + _bdot(t_xwa, wup)))
        rate = jax.nn.sigmoid(a0 + _bdot(xwa, aup))
        outs += [decay, k * (1.0 + (rate - 1.0) * k_a), kk * rate]
    outs.append(_bdot(jax.nn.sigmoid(xg), gup))
    return tuple(outs)


def _prep_args(tm, ps_ref, pv_ref, mat_refs, ones_ref):
    vecs = [jnp.broadcast_to(pv_ref[j:j + 1, :], (tm, D_RWKV)) for j in range(7)]
    return [ps_ref[...]] + vecs + [m[...] for m in mat_refs] + [ones_ref[...]]


_PREP_MAT_SHAPES = ((128, D_RWKV),) * 4 + ((D_GATE_PAD, D_RWKV),)


def _prep_fwd(ps, pvec, mats, ones_blocks, tm):
    t = ps.shape[0]

    def body(ps_ref, pv_ref, m0, m1, m2, m3, m4, ones_ref, *out_refs):
        outs = _prep_math(*_prep_args(tm, ps_ref, pv_ref, (m0, m1, m2, m3, m4), ones_ref))
        for o_ref, val in zip(out_refs, outs[2:]):
            o_ref[...] = val

    return pl.pallas_call(
        body, name="prep_fwd", grid=(t // tm,),
        in_specs=[_row(tm, D_SHIFT_PAD), _fixed((8, D_RWKV))] + [_fixed(s) for s in _PREP_MAT_SHAPES]
        + [_fixed((D_RWKV, D_RWKV))],
        out_specs=[_row(tm, D_RWKV)] * 8, out_shape=[_sds((t, D_RWKV))] * 8,
        compiler_params=_cp("parallel"))(ps, pvec, *mats, ones_blocks)


def _prep_bwd(ps, pvec, mats, ones_blocks, cts, tm):
    t = ps.shape[0]
    counts = [len(c) for c in cts]
    flat = [a for c in cts for a in c]

    def body(ps_ref, pv_ref, m0, m1, m2, m3, m4, ones_ref, *refs):
        ct_refs = refs[:len(flat)]
        q_ref, dpv_ref = refs[len(flat)], refs[len(flat) + 1]
        dmat_refs = refs[len(flat) + 2:]
        args = _prep_args(tm, ps_ref, pv_ref, (m0, m1, m2, m3, m4), ones_ref)
        _, vjp = jax.vjp(lambda *a: _prep_math(*a, args[-1]), *args[:-1])
        ct_vals, pos = [], 0
        for n in counts:
            val = ct_refs[pos][...]
            for extra in ct_refs[pos + 1:pos + n]:
                val = val + extra[...]
            ct_vals.append(val)
            pos += n
        grads = vjp(tuple(ct_vals))
        q_ref[...] = grads[0]

        @pl.when(pl.program_id(0) == 0)
        def _():
            dpv_ref[...] = jnp.zeros_like(dpv_ref)
            for d_ref in dmat_refs:
                d_ref[...] = jnp.zeros_like(d_ref)

        for j in range(7):
            dpv_ref[j:j + 1, :] += jnp.sum(grads[1 + j], axis=0, keepdims=True)
        for d_ref, gm in zip(dmat_refs, grads[8:13]):
            d_ref[...] += gm

    return pl.pallas_call(
        body, name="prep_bwd", grid=(t // tm,),
        in_specs=[_row(tm, D_SHIFT_PAD), _fixed((8, D_RWKV))] + [_fixed(s) for s in _PREP_MAT_SHAPES]
        + [_fixed((D_RWKV, D_RWKV))] + [_row(tm, D_RWKV)] * len(flat),
        out_specs=[_row(tm, D_SHIFT_PAD), _fixed((8, D_RWKV))] + [_fixed(s) for s in _PREP_MAT_SHAPES],
        out_shape=[_sds((t, D_SHIFT_PAD)), _sds((8, D_RWKV))] + [_sds(s) for s in _PREP_MAT_SHAPES],
        compiler_params=_cp("arbitrary"))(ps, pvec, *mats, ones_blocks, *flat)


def _pair_ones():
    h = jnp.arange(2 * HEAD) // HEAD
    block = (h[:, None] == h[None, :]).astype(BF16)
    return jnp.concatenate([block, block], axis=0)


def _diag_mask():
    lane = lax.broadcasted_iota(jnp.int32, (HEAD, 2 * HEAD), 1)
    sub = lax.broadcasted_iota(jnp.int32, (HEAD, 2 * HEAD), 0)
    return jnp.where((lane & (HEAD - 1)) == sub, 1.0, 0.0).astype(F32)


def _to_row(cols, dmask):
    return jnp.sum(cols * dmask, axis=0, keepdims=True)


def _seg_many(exact, rounded, ones_pair):
    out_exact, out_rounded = [], []
    if exact:
        parts = []
        for x in exact:
            hi = x.astype(BF16)
            parts.append(jnp.concatenate([hi, (x - hi.astype(F32)).astype(BF16)], axis=1))
        res = jnp.dot(jnp.concatenate(parts, axis=0), ones_pair, preferred_element_type=F32)
        out_exact = [res[HEAD * c:HEAD * (c + 1)] for c in range(len(exact))]
    if rounded:
        res = jnp.dot(jnp.concatenate([x.astype(BF16) for x in rounded], axis=0), ones_pair[0:2 * HEAD],
                      preferred_element_type=F32)
        out_rounded = [res[HEAD * c:HEAD * (c + 1)] for c in range(len(rounded))]
    return out_exact, out_rounded


N_CHAIN = 2 * N_PAIR


def _chain(c):
    d, p = divmod(c, N_PAIR)
    return d, slice(2 * HEAD * p, 2 * HEAD * (p + 1))


def _scan_specs(n_chunks, col_blocks, fwd_chunk, bwd_chunk):
    def spec(chunk_of, col):
        return pl.BlockSpec((SCAN_CHUNK, D_RWKV), lambda b, g: (b * n_chunks + chunk_of(g), col))
    return [spec(fwd_chunk, c) for c in col_blocks] + [spec(bwd_chunk, c) for c in col_blocks]


def _scan_fwd(ps, kk, dirs, batch, seq):
    t = batch * seq
    n = seq // SCAN_CHUNK
    groups = SCAN_CHUNK // GROUP
    up = lambda g: g
    down = lambda g: n - 1 - g
    col_blocks = (0, 2, 0, 0, 0, 0)

    def body(*refs):
        dir_refs = (refs[0:6], refs[6:12])
        ones_ref = refs[12]
        y_refs, hist_refs, st_ref = refs[13:15], refs[15:17], refs[17]

        @pl.when(pl.program_id(1) == 0)
        def _():
            st_ref[...] = jnp.zeros_like(st_ref)

        ones_pair = ones_ref[...]
        dmask = _diag_mask()
        dmask_b = dmask.astype(BF16)
        sub8 = lax.broadcasted_iota(jnp.int32, (GROUP, 2 * HEAD), 0)

        def group(gi, carry):
            off = (pl.multiple_of(gi * GROUP, GROUP), pl.multiple_of((groups - 1 - gi) * GROUP, GROUP))
            loaded = [tuple(ref[pl.ds(off[d], GROUP), :] for ref in dir_refs[d]) for d in range(2)]
            states = list(carry)
            y_acc = [jnp.zeros((GROUP, 2 * HEAD), F32) for _ in range(N_CHAIN)]
            for step in range(GROUP):
                rows, idx = [], []
                for c in range(N_CHAIN):
                    d, lanes = _chain(c)
                    i = step if d == 0 else GROUP - 1 - step
                    idx.append(i)
                    rows.append(tuple(x8[i:i + 1, lanes] for x8 in loaded[d]))
                    hist_refs[d][c % N_PAIR, gi * GROUP + step] = states[c]
                _, v_cols = _seg_many([], [dmask_b * rows[c][1].astype(BF16) for c in range(N_CHAIN)], ones_pair)
                sas, _ = _seg_many([states[c] * rows[c][2] for c in range(N_CHAIN)], [], ones_pair)
                for c in range(N_CHAIN):
                    _, _, _, w_row, kd_row, b_row = rows[c]
                    states[c] = states[c] * w_row - sas[c] * b_row + v_cols[c] * kd_row
                _, ys = _seg_many([], [states[c] * rows[c][0] for c in range(N_CHAIN)], ones_pair)
                for c in range(N_CHAIN):
                    y_acc[c] = jnp.where(sub8 == idx[c], _to_row(ys[c], dmask), y_acc[c])
            for c in range(N_CHAIN):
                d, lanes = _chain(c)
                y_refs[d][pl.ds(off[d], GROUP), lanes] = y_acc[c]
            return tuple(states)

        final = lax.fori_loop(0, groups, group, tuple(st_ref[c] for c in range(N_CHAIN)))
        for c in range(N_CHAIN):
            st_ref[c] = final[c]
            hist_refs[c // N_PAIR][c % N_PAIR, SCAN_CHUNK] = final[c]

    y_spec_f = pl.BlockSpec((SCAN_CHUNK, D_RWKV), lambda b, g: (b * n + up(g), 0))
    y_spec_b = pl.BlockSpec((SCAN_CHUNK, D_RWKV), lambda b, g: (b * n + down(g), 0))
    hist_shape = (batch, n, N_PAIR, SCAN_CHUNK + 1, HEAD, 2 * HEAD)
    hist_block = (None, None, N_PAIR, SCAN_CHUNK + 1, HEAD, 2 * HEAD)
    hist_spec_f = pl.BlockSpec(hist_block, lambda b, g: (b, up(g), 0, 0, 0, 0))
    hist_spec_b = pl.BlockSpec(hist_block, lambda b, g: (b, down(g), 0, 0, 0, 0))
    ones_spec = pl.BlockSpec((4 * HEAD, 2 * HEAD), lambda b, g: (0, 0))
    (wf, kdf, bf), (wb, kdb, bb) = dirs
    return pl.pallas_call(
        body, name="wkv_fwd", grid=(batch, n),
        in_specs=_scan_specs(n, col_blocks, up, down) + [ones_spec],
        out_specs=[y_spec_f, y_spec_b, hist_spec_f, hist_spec_b],
        out_shape=[_sds((t, D_RWKV)), _sds((t, D_RWKV)), _sds(hist_shape), _sds(hist_shape)],
        scratch_shapes=[pltpu.VMEM((N_CHAIN, HEAD, 2 * HEAD), F32)],
        compiler_params=_cp("parallel", "arbitrary"),
    )(ps, ps, kk, wf, kdf, bf, ps, ps, kk, wb, kdb, bb, _pair_ones())


def _scan_bwd(ps, kk, dirs, dy, hist_f, hist_b, batch, seq):
    t = batch * seq
    n = seq // SCAN_CHUNK
    groups = SCAN_CHUNK // GROUP
    fwd_chunk = lambda g: n - 1 - g
    bwd_chunk = lambda g: g
    col_blocks = (0, 2, 0, 0, 0, 0, 0)

    def undo_group(dir_refs, out_refs, hist_refs, gi, d_states, ones_pair, dmask, sub8):
        d_states = list(d_states)
        loaded, blocks = [], []
        for d in range(2):
            blk = groups - 1 - gi if d == 0 else gi
            blocks.append(pl.ds(pl.multiple_of(blk * GROUP, GROUP), GROUP))
            r8, v8, kk8, w8, kd8, b8, dy8 = (ref[blocks[d], :] for ref in dir_refs[d])
            loaded.append((r8, v8, kk8, w8, kd8, -b8, dy8))
        acc = [[jnp.zeros((GROUP, 2 * HEAD), F32) for _ in range(6)] for _ in range(N_CHAIN)]
        for step in range(GROUP):
            rows, idx, before, after = [], [], [], []
            for c in range(N_CHAIN):
                d, lanes = _chain(c)
                i = GROUP - 1 - step if d == 0 else step
                q = (groups - 1 - gi) * GROUP + i if d == 0 else SCAN_CHUNK - 1 - (gi * GROUP + i)
                idx.append(i)
                rows.append(tuple(x8[i:i + 1, lanes] for x8 in loaded[d]))
                before.append(hist_refs[d][c % N_PAIR, q])
                after.append(hist_refs[d][c % N_PAIR, q + 1])
            _, cols = _seg_many([], [dmask.astype(BF16) * rows[c][j].astype(BF16) for c in range(N_CHAIN) for j in (1, 6)],
                                ones_pair)
            v_cols, dy_cols = cols[0::2], cols[1::2]
            d_now = [d_states[c] + dy_cols[c] * rows[c][0] for c in range(N_CHAIN)]
            d_sas, _ = _seg_many([d_now[c] * rows[c][5] for c in range(N_CHAIN)], [], ones_pair)
            _, others = _seg_many(
                [], [x for c in range(N_CHAIN) for x in (before[c] * rows[c][2], d_now[c] * rows[c][4])], ones_pair)
            for c in range(N_CHAIN):
                sa, d_sa, dv_cols = others[2 * c], d_sas[c], others[2 * c + 1]
                rows_out = (
                    jnp.sum(after[c] * dy_cols[c], axis=0, keepdims=True),
                    jnp.sum(d_now[c] * before[c], axis=0, keepdims=True),
                    jnp.sum(d_now[c] * v_cols[c], axis=0, keepdims=True),
                    _to_row(dv_cols, dmask),
                    jnp.sum(before[c] * d_sa, axis=0, keepdims=True),
                    -jnp.sum(d_now[c] * sa, axis=0, keepdims=True),
                )
                acc[c] = [jnp.where(sub8 == idx[c], val, a) for val, a in zip(rows_out, acc[c])]
                d_states[c] = d_now[c] * rows[c][3] + d_sa * rows[c][2]
        for c in range(N_CHAIN):
            d, lanes = _chain(c)
            for o_ref, val in zip(out_refs[d], acc[c]):
                o_ref[blocks[d], lanes] = val
        return tuple(d_states)

    def body(*refs):
        dir_refs = (refs[0:7], refs[7:14])
        hist_refs, ones_ref = refs[14:16], refs[16]
        out_refs = (refs[17:23], refs[23:29])
        dst_ref = refs[29]

        @pl.when(pl.program_id(1) == 0)
        def _():
            dst_ref[...] = jnp.zeros_like(dst_ref)

        ones_pair = ones_ref[...]
        dmask = _diag_mask()
        sub8 = lax.broadcasted_iota(jnp.int32, (GROUP, 2 * HEAD), 0)

        def group(gi, carry):
            return undo_group(dir_refs, out_refs, hist_refs, gi, carry, ones_pair, dmask, sub8)

        final = lax.fori_loop(0, groups, group, tuple(dst_ref[c] for c in range(N_CHAIN)))
        for c in range(N_CHAIN):
            dst_ref[c] = final[c]

    blk = (SCAN_CHUNK, D_RWKV)
    out_f = pl.BlockSpec(blk, lambda b, g: (b * n + fwd_chunk(g), 0))
    out_b = pl.BlockSpec(blk, lambda b, g: (b * n + bwd_chunk(g), 0))
    hist_block = (None, None, N_PAIR, SCAN_CHUNK + 1, HEAD, 2 * HEAD)
    hist_spec_f = pl.BlockSpec(hist_block, lambda b, g: (b, fwd_chunk(g), 0, 0, 0, 0))
    hist_spec_b = pl.BlockSpec(hist_block, lambda b, g: (b, bwd_chunk(g), 0, 0, 0, 0))
    ones_spec = pl.BlockSpec((4 * HEAD, 2 * HEAD), lambda b, g: (0, 0))
    (wf, kdf, bf), (wb, kdb, bb) = dirs
    outs = pl.pallas_call(
        body, name="wkv_bwd", grid=(batch, n),
        in_specs=_scan_specs(n, col_blocks, fwd_chunk, bwd_chunk) + [hist_spec_f, hist_spec_b, ones_spec],
        out_specs=[out_f] * 6 + [out_b] * 6,
        out_shape=[_sds((t, D_RWKV))] * 12,
        scratch_shapes=[pltpu.VMEM((N_CHAIN, HEAD, 2 * HEAD), F32)],
        compiler_params=_cp("parallel", "arbitrary"),
    )(ps, ps, kk, wf, kdf, bf, dy, ps, ps, kk, wb, kdb, bb, dy, hist_f, hist_b, _pair_ones())
    return outs[0:6], outs[6:12]


def _post_math(y, r, kd_f, kd_b, v, gate, gn_w, gn_b, rk_f, rk_b, ones_blocks):
    mean = _seg(y, ones_blocks) * (1.0 / HEAD)
    yc = y - mean
    var = _seg(yc * yc, ones_blocks) * (1.0 / HEAD)
    yn = yc * lax.rsqrt(var + GN_EPS) * gn_w + gn_b
    bonus = _seg(r * kd_f * rk_f, ones_blocks) * v + _seg(r * kd_b * rk_b, ones_blocks) * v
    return (yn + bonus) * gate


def _conv_parts(pc, halo_prev, halo_next, has_prev, has_next):
    gate_b, gate_c, hid = pc[:, 0:512], pc[:, 512:1024], pc[:, 1024:1536]
    u = gate_c * hid
    u_prev_row = halo_prev[GROUP - 1:GROUP, 512:1024] * halo_prev[GROUP - 1:GROUP, 1024:1536] * has_prev
    u_next_row = halo_next[0:1, 512:1024] * halo_next[0:1, 1024:1536] * has_next
    u_down, u_up = _shifted(u, u_prev_row, u_next_row)
    return gate_b, gate_c, hid, u, u_down, u_up


def _post_specs(tm, t):
    pc_prev, pc_next = _halo_specs(tm, D_CONV3, t)
    col = lambda c: pl.BlockSpec((tm, D_RWKV), lambda i: (i, c))
    return ([col(0), col(0), col(0), col(0), col(0), col(2), col(0), _row(tm, D_CONV3), pc_prev, pc_next,
             _fixed((8, D_RWKV)), _fixed((D_RWKV, D_RWKV))])


def _post_fwd(y_f, y_b, ps, kd_f, kd_b, gate, pc, qvec, ones_blocks, tm, seq):
    t = ps.shape[0]

    def body(yf_ref, yb_ref, r_ref, kdf_ref, kdb_ref, v_ref, g_ref, pc_ref, hp_ref, hn_ref, qv_ref, ones_ref,
             o_ref, ot_ref):
        has_prev, has_next = _edge_flags(tm, seq)
        vec = [jnp.broadcast_to(qv_ref[j:j + 1, :], (tm, D_RWKV)) for j in range(7)]
        o_rwkv = _post_math(yf_ref[...] + yb_ref[...], r_ref[...], kdf_ref[...], kdb_ref[...], v_ref[...],
                            g_ref[...], vec[0], vec[1], vec[2], vec[3], ones_ref[...])
        gate_b, _, _, u, u_down, u_up = _conv_parts(pc_ref[...], hp_ref[...], hn_ref[...], has_prev, has_next)
        o_conv = gate_b * (vec[4] * u_down + vec[5] * u + vec[6] * u_up)
        for half, val in enumerate((o_rwkv, o_conv)):
            o_ref[:, D_RWKV * half:D_RWKV * (half + 1)] = val.astype(BF16)
            ot_ref[D_RWKV * half:D_RWKV * (half + 1), :] = jnp.transpose(val).astype(BF16)

    return pl.pallas_call(
        body, name="post_fwd", grid=(t // tm,), in_specs=_post_specs(tm, t),
        out_specs=[_row(tm, D_MODEL), _col(tm, D_MODEL)],
        out_shape=[_sds((t, D_MODEL), BF16), _sds((D_MODEL, t), BF16)], compiler_params=_cp("parallel"),
    )(y_f, y_b, ps, kd_f, kd_b, ps, gate, pc, pc, pc, qvec, ones_blocks)


def _post_bwd(d_out, y_f, y_b, ps, kd_f, kd_b, gate, pc, qvec, ones_blocks, tm, seq):
    t = ps.shape[0]
    do_prev, do_next = _halo_specs(tm, D_MODEL, t)

    def body(do_ref, dop_ref, don_ref, yf_ref, yb_ref, r_ref, kdf_ref, kdb_ref, v_ref, g_ref, pc_ref, hp_ref,
             hn_ref, qv_ref, ones_ref, dy_ref, dr_ref, dkdf_ref, dkdb_ref, dv_ref, dg_ref, dpc_ref, dqv_ref):
        has_prev, has_next = _edge_flags(tm, seq)
        vec = [jnp.broadcast_to(qv_ref[j:j + 1, :], (tm, D_RWKV)) for j in range(7)]
        ones_v = ones_ref[...]
        args = (yf_ref[...] + yb_ref[...], r_ref[...], kdf_ref[...], kdb_ref[...], v_ref[...], g_ref[...],
                vec[0], vec[1], vec[2], vec[3])
        _, vjp = jax.vjp(lambda *a: _post_math(*a, ones_v), *args)
        grads = vjp(do_ref[:, 0:D_RWKV])
        for o_ref, gval in zip((dy_ref, dr_ref, dkdf_ref, dkdb_ref, dv_ref, dg_ref), grads[0:6]):
            o_ref[...] = gval

        hp, hn = hp_ref[...], hn_ref[...]
        gate_b, gate_c, hid, u, u_down, u_up = _conv_parts(pc_ref[...], hp, hn, has_prev, has_next)
        d_oc = do_ref[:, D_RWKV:2 * D_RWKV]
        d_cu = d_oc * gate_b
        d_cu_prev = dop_ref[GROUP - 1:GROUP, D_RWKV:2 * D_RWKV] * hp[GROUP - 1:GROUP, 0:512] * has_prev
        d_cu_next = don_ref[0:1, D_RWKV:2 * D_RWKV] * hn[0:1, 0:512] * has_next
        d_cu_down, d_cu_up = _shifted(d_cu, d_cu_prev, d_cu_next)
        d_u = vec[5] * d_cu + vec[4] * d_cu_up + vec[6] * d_cu_down
        dpc_ref[:, 0:512] = (d_oc * (vec[4] * u_down + vec[5] * u + vec[6] * u_up)).astype(BF16)
        dpc_ref[:, 512:1024] = (d_u * hid).astype(BF16)
        dpc_ref[:, 1024:1536] = (d_u * gate_c).astype(BF16)

        @pl.when(pl.program_id(0) == 0)
        def _():
            dqv_ref[...] = jnp.zeros_like(dqv_ref)

        vec_grads = list(grads[6:10]) + [d_cu * u_down, d_cu * u, d_cu * u_up]
        for j, gval in enumerate(vec_grads):
            dqv_ref[j:j + 1, :] += jnp.sum(gval, axis=0, keepdims=True)

    return pl.pallas_call(
        body, name="post_bwd", grid=(t // tm,),
        in_specs=[_row(tm, D_MODEL), do_prev, do_next] + _post_specs(tm, t),
        out_specs=[_row(tm, D_RWKV)] * 6 + [_row(tm, D_CONV3), _fixed((8, D_RWKV))],
        out_shape=[_sds((t, D_RWKV))] * 6 + [_sds((t, D_CONV3), BF16), _sds((8, D_RWKV))],
        compiler_params=_cp("arbitrary"),
    )(d_out, d_out, d_out, y_f, y_b, ps, kd_f, kd_b, ps, gate, pc, pc, pc, qvec, ones_blocks)


def _adamw_math(wv, gv, mv, vv):
    m2 = ADAM_B1 * mv + (1.0 - ADAM_B1) * gv
    v2 = ADAM_B2 * vv + (1.0 - ADAM_B2) * (gv * gv)
    m_hat = m2 / (1.0 - ADAM_B1 ** ADAM_STEP)
    v_hat = v2 / (1.0 - ADAM_B2 ** ADAM_STEP)
    return -ADAM_LR * (m_hat / (jnp.sqrt(v_hat) + ADAM_EPS) + ADAM_WD * wv), m2, v2


def _adamw_small(items):
    n = len(items)

    def body(*refs):
        ins, outs = refs[:4 * n], refs[4 * n:]
        for k in range(n):
            w_ref, g_ref, m_ref, v_ref = ins[4 * k:4 * k + 4]
            for o_ref, val in zip(outs[3 * k:3 * k + 3], _adamw_math(w_ref[...], g_ref[...], m_ref[...], v_ref[...])):
                o_ref[...] = val

    flat = [a for item in items for a in item]
    outs = pl.pallas_call(
        body, name="adamw_small", out_shape=[_sds(item[0].shape) for item in items for _ in range(3)],
        compiler_params=_cp())(*flat)
    return [tuple(outs[3 * k:3 * k + 3]) for k in range(n)]


def _adamw(w, g, m, v, name):
    r, c = w.shape[-2:]
    tr = _tile(r, 256, 8)
    if w.ndim == 3:
        spec = pl.BlockSpec((None, tr, c), lambda i: (0, i, 0))
    else:
        spec = pl.BlockSpec((tr, c), lambda i: (i, 0))

    def body(w_ref, g_ref, m_ref, v_ref, d_ref, nm_ref, nv_ref):
        d_ref[...], nm_ref[...], nv_ref[...] = _adamw_math(w_ref[...], g_ref[...], m_ref[...], v_ref[...])

    return pl.pallas_call(
        body, name=name, grid=(r // tr,), in_specs=[spec] * 4, out_specs=[spec] * 3,
        out_shape=[_sds(w.shape)] * 3, compiler_params=_cp("parallel"))(w, g, m, v)


_ANY = pl.BlockSpec(memory_space=pl.ANY)


def _place():
    return lax.axis_index("x"), lax.axis_index("y"), lax.axis_index("c")


def _other_chips(x, y):
    return [(1 - x, y), (x, 1 - y), (1 - x, 1 - y)]


def _remote(src, dst, send_sems, recv_sems, k, to):
    return pltpu.make_async_remote_copy(src_ref=src, dst_ref=dst, send_sem=send_sems.at[k],
                                        recv_sem=recv_sems.at[k], device_id=to, device_id_type=MESH)


def _gather_weights(pack):
    rows, width = pack.shape
    half = rows // 2

    def body(x_ref, out_ref, send_sems, recv_sems):
        x, y, c = _place()
        sibling = (x, y, 1 - c)
        chips = _other_chips(x, y)

        def block(chip, part):
            return out_ref.at[2 * chip[0] + chip[1], pl.ds(part * half, half), :]

        first = [_remote(x_ref.at[pl.ds(c * half, half), :], block((x, y), c), send_sems, recv_sems, j, (*chip, c))
                 for j, chip in enumerate(chips)]
        for cp in first:
            cp.start()
        passed = [_remote(block(chip, c), block(chip, c), send_sems, recv_sems, 3 + j, sibling)
                  for j, chip in enumerate(chips)]
        for j, chip in enumerate(chips):
            _remote(block(chip, c), block(chip, c), send_sems, recv_sems, j, sibling).wait_recv()
            passed[j].start()
        for j, chip in enumerate(chips):
            _remote(block(chip, 1 - c), block(chip, 1 - c), send_sems, recv_sems, 3 + j, sibling).wait_recv()
        for cp in first + passed:
            cp.wait_send()

    return pl.pallas_call(
        body, name="gather_weights", in_specs=[_ANY], out_specs=_ANY,
        out_shape=_sds((N_SHARD, rows, width), pack.dtype),
        scratch_shapes=[pltpu.SemaphoreType.DMA((6,)), pltpu.SemaphoreType.DMA((6,))],
    )(pack)


_HBM = pl.BlockSpec(memory_space=pltpu.HBM)
_SEMS = pl.BlockSpec(memory_space=pltpu.SEMAPHORE)
_DATAFLOW = pltpu.SideEffectType.DATAFLOW_SIDE_EFFECTING


def _fetch_start(pack, after):
    def body(x_ref, land_ref, after_ref, send_sems, recv_sems, x_thru, land_thru, token):
        x, y, c = _place()
        for j, chip in enumerate(_other_chips(x, y)):
            _remote(x_ref, land_ref.at[2 * x + y], send_sems, recv_sems, j, (*chip, c)).start()
        token[...] = jnp.zeros_like(token)

    land = lax.empty((N_SHARD,) + pack.shape, pack.dtype)
    return pl.pallas_call(
        body, name="fetch_ffn_start",
        out_shape=(pltpu.SemaphoreType.DMA((3,)), pltpu.SemaphoreType.DMA((3,)), pltpu.HBM(pack.shape, pack.dtype),
                   pltpu.HBM(land.shape, land.dtype), _sds((8, 128))),
        in_specs=(_HBM, _HBM, _ANY), out_specs=(_SEMS, _SEMS, _HBM, _HBM, pl.BlockSpec(memory_space=pltpu.VMEM)),
        input_output_aliases={0: 2, 1: 3}, compiler_params=pltpu.CompilerParams(has_side_effects=_DATAFLOW),
    )(pltpu.with_memory_space_constraint(pack, pltpu.HBM), pltpu.with_memory_space_constraint(land, pltpu.HBM), after)


def _fetch_wait(send_sems, recv_sems, pack_thru, land_thru, after):
    def body(x_ref, land_ref, send_sems, recv_sems, after_ref, x_dead, got_ref):
        x, y, c = _place()
        for j, chip in enumerate(_other_chips(x, y)):
            cp = _remote(x_ref, land_ref.at[2 * chip[0] + chip[1]], send_sems, recv_sems, j, (*chip, c))
            cp.wait_send()
            cp.wait_recv()

    return pl.pallas_call(
        body, name="fetch_ffn_wait",
        out_shape=(pltpu.HBM(pack_thru.shape, pack_thru.dtype), pltpu.HBM(land_thru.shape, land_thru.dtype)),
        in_specs=(_HBM, _HBM, _SEMS, _SEMS, _ANY), out_specs=(_HBM, _HBM), input_output_aliases={0: 0, 1: 1},
        compiler_params=pltpu.CompilerParams(has_side_effects=_DATAFLOW),
    )(pack_thru, land_thru, send_sems, recv_sems, after)[1]


def _swap_with_sibling(block, name):
    def body(x_ref, out_ref, send_sems, recv_sems):
        x, y, c = _place()
        cp = _remote(x_ref, out_ref, send_sems, recv_sems, 0, (x, y, 1 - c))
        cp.start()
        cp.wait()

    return pl.pallas_call(
        body, name=name, in_specs=[_ANY], out_specs=_ANY, out_shape=_sds(block.shape, block.dtype),
        scratch_shapes=[pltpu.SemaphoreType.DMA((1,)), pltpu.SemaphoreType.DMA((1,))],
    )(block)


def _swap_other_half(packed, tag):
    slots, rows, width = packed.shape
    half = rows // 2

    def body(x_ref, out_ref, send_sems, recv_sems):
        x, y, c = _place()
        cp = _remote(x_ref.at[:, pl.ds((1 - c) * half, half), :], out_ref, send_sems, recv_sems, 0, (x, y, 1 - c))
        cp.start()
        cp.wait()

    return pl.pallas_call(
        body, name="swap_halves_" + tag, in_specs=[_ANY], out_specs=_ANY, out_shape=_sds((slots, half, width)),
        scratch_shapes=[pltpu.SemaphoreType.DMA((1,)), pltpu.SemaphoreType.DMA((1,))],
    )(packed)


def _swap_half_start(packed):
    slots, rows, width = packed.shape
    half = rows // 2

    def body(x_ref, land_ref, send_sems, recv_sems, x_thru, land_thru, token):
        x, y, c = _place()
        _remote(x_ref.at[:, pl.ds((1 - c) * half, half), :], land_ref, send_sems, recv_sems, 0, (x, y, 1 - c)).start()
        token[...] = jnp.zeros_like(token)

    land = lax.empty((slots, half, width), packed.dtype)
    return pl.pallas_call(
        body, name="swap_halves_ffn_start",
        out_shape=(pltpu.SemaphoreType.DMA((1,)), pltpu.SemaphoreType.DMA((1,)), pltpu.HBM(packed.shape, packed.dtype),
                   pltpu.HBM(land.shape, land.dtype), _sds((8, 128))),
        in_specs=(_HBM, _HBM), out_specs=(_SEMS, _SEMS, _HBM, _HBM, pl.BlockSpec(memory_space=pltpu.VMEM)),
        input_output_aliases={0: 2, 1: 3}, compiler_params=pltpu.CompilerParams(has_side_effects=_DATAFLOW),
    )(pltpu.with_memory_space_constraint(packed, pltpu.HBM), pltpu.with_memory_space_constraint(land, pltpu.HBM))


def _swap_half_wait(send_sems, recv_sems, packed_thru, land_thru, after):
    half = land_thru.shape[1]

    def body(x_ref, land_ref, send_sems, recv_sems, after_ref, x_out, got_ref):
        x, y, c = _place()
        cp = _remote(x_ref.at[:, pl.ds((1 - c) * half, half), :], land_ref, send_sems, recv_sems, 0, (x, y, 1 - c))
        cp.wait_send()
        cp.wait_recv()

    return pl.pallas_call(
        body, name="swap_halves_ffn_wait",
        out_shape=(pltpu.HBM(packed_thru.shape, packed_thru.dtype), pltpu.HBM(land_thru.shape, land_thru.dtype)),
        in_specs=(_HBM, _HBM, _SEMS, _SEMS, _ANY), out_specs=(_HBM, _HBM), input_output_aliases={0: 0, 1: 1},
        compiler_params=pltpu.CompilerParams(has_side_effects=_DATAFLOW),
    )(packed_thru, land_thru, send_sems, recv_sems, after)


def _add_halves(packed, got, c, tag):
    slots, rows, width = packed.shape
    half = rows // 2
    tr = _tile(half, 408, 16)
    per = half // tr
    block = (None, tr, width)

    def body(c_ref, mine_ref, got_ref, sum_ref, sum16_ref):
        acc = mine_ref[...] + got_ref[...]
        sum_ref[...] = acc
        sum16_ref[...] = acc.astype(BF16)

    plain = pl.BlockSpec(block, lambda s, i, c_ref: (s, i, 0))
    grid_spec = pltpu.PrefetchScalarGridSpec(
        num_scalar_prefetch=1, grid=(slots, per),
        in_specs=[pl.BlockSpec(block, lambda s, i, c_ref: (s, c_ref[0] * per + i, 0)), plain],
        out_specs=[plain, plain])
    return pl.pallas_call(
        body, name="add_halves_" + tag, grid_spec=grid_spec,
        out_shape=[_sds((slots, half, width)), _sds((slots, half, width), BF16)],
        compiler_params=_cp("parallel", "parallel"))(c.reshape(1).astype(jnp.int32), packed, got)


def _add_quarters(chip_sum, others, chip, tag):
    _, rows, width = chip_sum.shape
    tr = _tile(rows, 408, 16)

    def body(chip_ref, own_ref, others_ref, o_ref):
        acc = own_ref[...]
        for j in range(3):
            acc = acc + others_ref[j].astype(F32)
        o_ref[...] = acc

    grid_spec = pltpu.PrefetchScalarGridSpec(
        num_scalar_prefetch=1, grid=(rows // tr,),
        in_specs=[pl.BlockSpec((None, tr, width), lambda i, chip_ref: (chip_ref[0], i, 0)),
                  pl.BlockSpec((3, tr, width), lambda i, chip_ref: (0, i, 0))],
        out_specs=pl.BlockSpec((tr, width), lambda i, chip_ref: (i, 0)))
    return pl.pallas_call(
        body, name="add_quarters_" + tag, grid_spec=grid_spec, out_shape=_sds((rows, width)),
        compiler_params=_cp("parallel"))(chip.reshape(1).astype(jnp.int32), chip_sum, others)


def _exchange_start(parts, tag):
    _, rows, width = parts.shape

    def body(x_ref, land_ref, send_sems, recv_sems, x_thru, land_thru, token):
        x, y, c = _place()
        for j, chip in enumerate(_other_chips(x, y)):
            _remote(x_ref.at[2 * chip[0] + chip[1]], land_ref.at[j], send_sems, recv_sems, j, (*chip, c)).start()
        token[...] = jnp.zeros_like(token)

    land = lax.empty((3, rows, width), parts.dtype)
    return pl.pallas_call(
        body, name="exchange_" + tag + "_start",
        out_shape=(pltpu.SemaphoreType.DMA((3,)), pltpu.SemaphoreType.DMA((3,)), pltpu.HBM(parts.shape, parts.dtype),
                   pltpu.HBM(land.shape, land.dtype), _sds((8, 128))),
        in_specs=(_HBM, _HBM), out_specs=(_SEMS, _SEMS, _HBM, _HBM, pl.BlockSpec(memory_space=pltpu.VMEM)),
        input_output_aliases={0: 2, 1: 3}, compiler_params=pltpu.CompilerParams(has_side_effects=_DATAFLOW),
    )(pltpu.with_memory_space_constraint(parts, pltpu.HBM), pltpu.with_memory_space_constraint(land, pltpu.HBM))


def _exchange_wait(send_sems, recv_sems, parts_thru, land_thru, after, tag):
    def body(x_ref, land_ref, send_sems, recv_sems, after_ref, x_dead, got_ref):
        x, y, c = _place()
        for j, chip in enumerate(_other_chips(x, y)):
            cp = _remote(x_ref.at[2 * chip[0] + chip[1]], land_ref.at[j], send_sems, recv_sems, j, (*chip, c))
            cp.wait_send()
            cp.wait_recv()

    return pl.pallas_call(
        body, name="exchange_" + tag + "_wait",
        out_shape=(pltpu.HBM(parts_thru.shape, parts_thru.dtype), pltpu.HBM(land_thru.shape, land_thru.dtype)),
        in_specs=(_HBM, _HBM, _SEMS, _SEMS, _ANY), out_specs=(_HBM, _HBM), input_output_aliases={0: 0, 1: 1},
        compiler_params=pltpu.CompilerParams(has_side_effects=_DATAFLOW),
    )(parts_thru, land_thru, send_sems, recv_sems, after)[1]


def _allreduce_small(vec):
    rows, width = vec.shape
    vmem = pl.BlockSpec(memory_space=pltpu.VMEM)

    def body(x_ref, o_ref, buf_ref, send_sems, recv_sems):
        x, y, c = _place()
        me = 4 * x + 2 * y + c
        buf_ref[me] = x_ref[...]
        copies = []
        for k in range(1, N_DEV):
            peer = (x ^ ((k >> 2) & 1), y ^ ((k >> 1) & 1), c ^ (k & 1))
            copies.append(_remote(x_ref, buf_ref.at[me], send_sems, recv_sems, k - 1, peer))
        for cp in copies:
            cp.start()
        for k in range(1, N_DEV):
            _remote(x_ref, buf_ref.at[me ^ k], send_sems, recv_sems, k - 1, (x, y, c)).wait_recv()
        for cp in copies:
            cp.wait_send()
        total = buf_ref[0]
        for d in range(1, N_DEV):
            total = total + buf_ref[d]
        o_ref[...] = total

    return pl.pallas_call(
        body, name="allreduce_small", in_specs=[vmem], out_specs=vmem, out_shape=_sds((rows, width)),
        scratch_shapes=[pltpu.VMEM((N_DEV, rows, width), F32), pltpu.SemaphoreType.DMA((N_DEV - 1,)),
                        pltpu.SemaphoreType.DMA((N_DEV - 1,))],
    )(vec)


def _rows1024(a):
    return a.reshape(-1, 1024)


def _pad_rows(a, rows):
    return jnp.concatenate([a, jnp.zeros((rows - a.shape[0], a.shape[1]), a.dtype)], axis=0)


_TRANSPOSED = ("w_in", "w_gate", "w_up")
_SMALL_SHARDED = ("w_up_f", "w_up_b", "a_up_f", "a_up_b", "g_up")
_BIG_SHARDED = ("w_in", "w_out", "w_gate", "w_up", "w_down")


def _pack_weight_shards(w):
    conv_bits = lax.bitcast_convert_type(w["conv_w"], BF16).reshape(1, -1)
    conv_row = jnp.concatenate([conv_bits, jnp.zeros((1, 1024 - conv_bits.shape[1]), BF16)], axis=1)

    def rows(name):
        a = w[name].astype(BF16)
        return a.T if name in _TRANSPOSED else _rows1024(a)

    early = _pad_rows(jnp.concatenate([rows(name) for name, _ in _EARLY_ROWS[:-1]] + [conv_row], axis=0), EARLY_R)
    return early, jnp.concatenate([rows(name) for name, _ in _FFN_ROWS], axis=0)


def _split_rows(gathered, layout):
    out, row = {}, 0
    for name, n in layout:
        out[name] = gathered[:, row:row + n]
        row += n
    return out


def _unpack_early(gathered):
    out = _split_rows(gathered, _EARLY_ROWS)
    cols = lambda a, k: jnp.concatenate([a[s].reshape(k, -1) for s in range(N_SHARD)], axis=1)
    conv = lax.bitcast_convert_type(out["conv_w"][:, 0, :768].reshape(N_SHARD, 3, 128, 2), F32)
    full = dict(w_in=out["w_in"].reshape(-1, 1024).T, w_out=out["w_out"].reshape(D_MODEL, D_MODEL),
                conv_w=jnp.concatenate([conv[s] for s in range(N_SHARD)], axis=1))
    full.update({name: cols(out[name], D_GATE if name == "g_up" else D_LORA) for name in _SMALL_SHARDED})
    return full


def _unpack_ffn(gathered):
    out = _split_rows(gathered, _FFN_ROWS)
    return dict(w_gate=out["w_gate"].reshape(-1, 1024).T, w_up=out["w_up"].reshape(-1, 1024).T,
                w_down=out["w_down"].reshape(D_FF, D_MODEL))


def _pack_grads(g, layout, rows):
    col_split = lambda a, s: a[:, s * (a.shape[1] // N_SHARD):(s + 1) * (a.shape[1] // N_SHARD)]
    row_split = lambda a, s: a[s * (a.shape[0] // N_SHARD):(s + 1) * (a.shape[0] // N_SHARD)]
    by_rows = {name: (g[name].T if name in _TRANSPOSED else g[name]) for name, _ in layout if name in _BIG_SHARDED}
    used = sum(n for _, n in layout)
    parts = []
    for s in range(N_SHARD):
        for name, _ in layout:
            if name in _BIG_SHARDED:
                parts.append(row_split(by_rows[name], s))
            elif name in _SMALL_SHARDED:
                parts.append(_rows1024(col_split(g[name], s)))
            else:
                conv = col_split(g["conv_w"], s).reshape(1, -1)
                parts.append(jnp.concatenate([conv, jnp.zeros((1, 1024 - conv.shape[1]), F32)], axis=1))
        if rows > used:
            parts.append(jnp.zeros((rows - used, 1024), F32))
    return jnp.concatenate(parts, axis=0).reshape(N_SHARD, rows, 1024)


def _unpack_grad_shard(pack, layout):
    small_shapes = {name: (D_GATE if name == "g_up" else D_LORA, 128) for name in _SMALL_SHARDED}
    out, row = {}, 0
    for name, n in layout:
        piece = pack[row:row + n]
        if name == "conv_w":
            out[name] = piece[0, :384].reshape(3, 128)
        else:
            out[name] = piece.T if name in _TRANSPOSED else piece.reshape(small_shapes.get(name, piece.shape))
        row += n
    return out


_SMALL_LAYOUT = (("norm1_w", 1024), ("mu_shift", D_SHIFT), ("w0_f", 512), ("w0_b", 512), ("a0_f", 512),
                 ("a0_b", 512), ("k_k", 512), ("k_a_f", 512), ("k_a_b", 512), ("r_k_f", 512), ("r_k_b", 512),
                 ("gn_w", 512), ("gn_b", 512), ("norm2_w", 1024), ("norm_f_w", 1024), ("loss", 1))


def _pack_small(vals):
    rows = []
    for name, n in _SMALL_LAYOUT:
        flat = vals[name].reshape(-1)
        n_rows = -(-n // 1024)
        rows.append(jnp.concatenate([flat, jnp.zeros((n_rows * 1024 - n,), F32)]).reshape(n_rows, 1024))
    return _pad_rows(jnp.concatenate(rows, axis=0), SMALL_ROWS)


def _unpack_small(pack):
    out, row = {}, 0
    for name, n in _SMALL_LAYOUT:
        n_rows = -(-n // 1024)
        out[name] = pack[row:row + n_rows].reshape(-1)[:n]
        row += n_rows
    return out


_WEIGHTS = ("norm1_w", "w_in", "mu_shift", "w_up_f", "w0_f", "w_up_b", "w0_b", "a_up_f", "a0_f", "a_up_b", "a0_b",
            "g_up", "k_k", "k_a_f", "k_a_b", "r_k_f", "r_k_b", "gn_w", "gn_b", "conv_w", "w_out", "norm2_w",
            "w_gate", "w_up", "w_down", "norm_f_w")


def _train_step(x, loss_target, w, m, v):
    batch, seq, _ = x.shape
    t = batch * seq
    tm = _tile(seq, 256, 8)
    xs = x.reshape(t, D_MODEL)
    target = loss_target.reshape(t, D_MODEL)
    vec = lambda name: w[name].reshape(1, -1)

    local = {name: w[name][0] for name, _ in _PACK_ROWS}
    c = lax.axis_index("c")
    chip = 2 * lax.axis_index("x") + lax.axis_index("y")
    early, ffn_pack = _pack_weight_shards(local)
    early_all = lax.dynamic_update_slice(_gather_weights(early), early[None], (chip, 0, 0))
    ffn_send, ffn_recv, ffn_pack, ffn_land, token = _fetch_start(ffn_pack, early_all)
    full = _unpack_early(early_all)
    w_in = full["w_in"]
    w_shift = jnp.concatenate([w_in[:, :D_SHIFT], jnp.zeros((D_MODEL, D_SHIFT_PAD - D_SHIFT), BF16)], axis=1)
    w_conv = w_in[:, D_SHIFT:]
    zeros_lora = jnp.zeros((D_LORA, D_RWKV), F32)
    lora = lambda name: full[name].astype(F32)
    mats = (jnp.concatenate([lora("w_up_f"), zeros_lora]), jnp.concatenate([zeros_lora, lora("a_up_f")]),
            jnp.concatenate([lora("w_up_b"), zeros_lora]), jnp.concatenate([zeros_lora, lora("a_up_b")]),
            jnp.concatenate([lora("g_up"), jnp.zeros((D_GATE_PAD - D_GATE, D_RWKV), F32)]))
    mu = jnp.concatenate([vec("mu_shift"), jnp.zeros((1, D_SHIFT_PAD - D_SHIFT), F32)], axis=1)
    mu = jnp.broadcast_to(mu, (GROUP, D_SHIFT_PAD))
    zero_row = jnp.zeros((1, D_RWKV), F32)
    pvec = jnp.concatenate([vec("k_k"), vec("w0_f"), vec("a0_f"), vec("k_a_f"), vec("w0_b"), vec("a0_b"),
                            vec("k_a_b"), zero_row], axis=0)
    qvec = jnp.concatenate([vec("gn_w"), vec("gn_b"), vec("r_k_f"), vec("r_k_b"), full["conv_w"], zero_row], axis=0)
    ones_blocks = _head_ones()

    h1, h1_t = _rmsnorm_fwd(xs, vec("norm1_w") + token[0, 0], tm, "norm1_fwd")
    p_shift = _matmul(h1, w_shift, mode="nn", name="in_proj_shift")
    pc = _matmul(h1, w_conv, mode="nn", name="in_proj_conv")
    ps = _shift_fwd(p_shift, mu, tm, seq)
    kk, w_f, kd_f, b_f, w_b, kd_b, b_b, gate = _prep_fwd(ps, pvec, mats, ones_blocks, tm)
    dirs = ((w_f, kd_f, b_f), (w_b, kd_b, b_b))
    y_f, y_b, hist_f, hist_b = _scan_fwd(ps, kk, dirs, batch, seq)
    mixed, mixed_t = _post_fwd(y_f, y_b, ps, kd_f, kd_b, gate, pc, qvec, ones_blocks, tm, seq)
    x1 = _matmul(mixed, full["w_out"], mode="nn", name="out_proj", add=xs)
    h2, h2_t = _rmsnorm_fwd(x1, vec("norm2_w"), tm, "norm2_fwd")
    ffn_all = _fetch_wait(ffn_send, ffn_recv, ffn_pack, ffn_land, h2)
    full.update(_unpack_ffn(lax.dynamic_update_slice(ffn_all, ffn_pack[None], (chip, 0, 0))))
    ff_gate, ff_up, act, act_t = _ffn_in(h2, full["w_gate"], full["w_up"])
    x2 = _matmul(act, full["w_down"], mode="nn", name="ffn_down", add=x1)
    d_x2, d_norm_f, loss_part = _loss_head(x2, w["norm_f_w"].reshape(1, -1), target, tm)

    g = {}
    g["w_down"] = _matmul(act_t, d_x2, mode="nn", name="ffn_down_dw")
    d_gate, d_up = _ffn_in_bwd(d_x2, full["w_down"], ff_gate, ff_up)
    d_h2 = _matmul(d_gate, full["w_gate"], mode="nt", name="ffn_gate_dx")
    d_h2 = _matmul(d_up, full["w_up"], mode="nt", name="ffn_up_dx", add=d_h2)
    g["w_gate"] = _matmul(h2_t, d_gate, mode="nn", name="ffn_gate_dw")
    g["w_up"] = _matmul(h2_t, d_up, mode="nn", name="ffn_up_dw")
    ffn_grads = _pack_grads(g, _FFN_ROWS, sum(n for _, n in _FFN_ROWS))
    sw_send, sw_recv, ffn_grads, sw_land, sw_token = _swap_half_start(ffn_grads)
    d_x1, d_norm2 = _rmsnorm_bwd(x1, vec("norm2_w") + sw_token[0, 0], d_h2, d_x2, tm, "norm2_bwd")
    d_mixed = _matmul(d_x1, full["w_out"], mode="nt", name="out_proj_dx")
    g["w_out"] = _matmul(mixed_t, d_x1, mode="nn", name="out_proj_dw")
    ffn_grads, sibling_half = _swap_half_wait(sw_send, sw_recv, ffn_grads, sw_land, g["w_out"])
    ffn_sum, ffn_sum_bf16 = _add_halves(ffn_grads, sibling_half, c, "ffn")
    ex_send, ex_recv, ffn_sum_bf16, ex_land, ex_token = _exchange_start(ffn_sum_bf16, "ffn")
    dy, dr_o, dkdf_o, dkdb_o, dv_o, d_gatev, d_pc, d_qvec = _post_bwd(
        d_mixed, y_f, y_b, ps, kd_f, kd_b, gate, pc, qvec + ex_token[0, 0], ones_blocks, tm, seq)
    (dr_f, dw_f, dkd_f, dv_f, dkk_f, db_f), (dr_b, dw_b, dkd_b, dv_b, dkk_b, db_b) = _scan_bwd(
        ps, kk, dirs, dy, hist_f, hist_b, batch, seq)
    cts = [[dr_f, dr_b, dr_o], [dv_f, dv_b, dv_o], [dkk_f, dkk_b], [dw_f], [dkd_f, dkdf_o], [db_f],
           [dw_b], [dkd_b, dkdb_o], [db_b], [d_gatev]]
    q, d_pvec, d_m0, d_m1, d_m2, d_m3, d_m4 = _prep_bwd(ps, pvec, mats, ones_blocks, cts, tm)
    d_pshift, d_mu = _shift_bwd(q, p_shift, mu, tm, seq)
    d_h1 = _matmul(d_pshift, w_shift, mode="nt", name="in_proj_shift_dx")
    d_h1 = _matmul(d_pc, w_conv, mode="nt", name="in_proj_conv_dx", add=d_h1)
    d_w_shift = _matmul(h1_t, d_pshift, mode="nn", name="in_proj_shift_dw")
    d_w_conv = _matmul(h1_t, d_pc, mode="nn", name="in_proj_conv_dw")
    g["w_in"] = jnp.concatenate([d_w_shift[:, :D_SHIFT], d_w_conv], axis=1)
    d_x, d_norm1 = _rmsnorm_bwd(xs, vec("norm1_w"), d_h1, d_x1, tm, "norm1_bwd")
    g["w_up_f"], g["a_up_f"] = d_m0[:D_LORA], d_m1[D_LORA:]
    g["w_up_b"], g["a_up_b"] = d_m2[:D_LORA], d_m3[D_LORA:]
    g["g_up"] = d_m4[:D_GATE]
    g["conv_w"] = d_qvec[4:7]

    def finish(chip_sum, others, tag, layout):
        eighth = _add_quarters(chip_sum, others, chip, tag)
        other_eighth = _swap_with_sibling(eighth, "swap_eighths_" + tag)
        return _unpack_grad_shard(jnp.concatenate([jnp.where(c == 0, eighth, other_eighth),
                                                   jnp.where(c == 0, other_eighth, eighth)], axis=0), layout)

    as2d = lambda name: (1, w[name].shape[0]) if w[name].ndim == 1 else w[name].shape
    operands = lambda name: tuple(a.reshape(as2d(name)) for a in (w[name], grads[name], m[name], v[name]))

    packed = _pack_grads(g, _EARLY_ROWS, EARLY_R)
    mix_sum, mix_sum_bf16 = _add_halves(packed, _swap_other_half(packed, "mixer"), c, "mixer")
    mx_send, mx_recv, mix_sum_bf16, mx_land, mx_token = _exchange_start(mix_sum_bf16, "mixer")
    grads = finish(ffn_sum, _exchange_wait(ex_send, ex_recv, ffn_sum_bf16, ex_land, mx_token, "ffn"), "ffn", _FFN_ROWS)
    updates = {name: _adamw(*operands(name), "adamw_" + name) for name in _FFN_NAMES}
    mix_others = _exchange_wait(mx_send, mx_recv, mix_sum_bf16, mx_land, updates["w_down"][2], "mixer")
    grads.update(finish(mix_sum, mix_others, "mixer", _EARLY_ROWS))

    small = dict(norm1_w=d_norm1, mu_shift=d_mu[:, :D_SHIFT], w0_f=d_pvec[1], w0_b=d_pvec[4], a0_f=d_pvec[2],
                 a0_b=d_pvec[5], k_k=d_pvec[0], k_a_f=d_pvec[3], k_a_b=d_pvec[6], r_k_f=d_qvec[2], r_k_b=d_qvec[3],
                 gn_w=d_qvec[0], gn_b=d_qvec[1], norm2_w=d_norm2, norm_f_w=d_norm_f, loss=loss_part)
    reduced = _unpack_small(_allreduce_small(_pack_small(small)))
    loss = reduced.pop("loss")[0]
    grads.update(reduced)

    outs = {}
    small = [name for name in _WEIGHTS if name not in _BIG_SHARDED]
    updates.update(zip(small, _adamw_small([operands(name) for name in small])))
    for name in ("w_in", "w_out"):
        updates[name] = _adamw(*operands(name), "adamw_" + name)
    for name in _WEIGHTS:
        shape = w[name].shape
        outs[name] = (grads[name].reshape(shape),) + tuple(a.reshape(shape) for a in updates[name])
    d_x = d_x.reshape(batch, seq, D_MODEL)
    return (loss, d_x) + tuple(outs[name][k] for k in range(4) for name in _WEIGHTS)


def kernel(x, norm1_w, w_in, mu_shift, w_up_f, w0_f, w_up_b, w0_b, a_up_f, a0_f, a_up_b, a0_b, g_up, k_k, k_a_f, k_a_b, r_k_f, r_k_b, gn_w, gn_b, conv_w, w_out, norm2_w, w_gate, w_up, w_down, norm_f_w, loss_target, m_norm1_w, m_w_in, m_mu_shift, m_w_up_f, m_w0_f, m_w_up_b, m_w0_b, m_a_up_f, m_a0_f, m_a_up_b, m_a0_b, m_g_up, m_k_k, m_k_a_f, m_k_a_b, m_r_k_f, m_r_k_b, m_gn_w, m_gn_b, m_conv_w, m_w_out, m_norm2_w, m_w_gate, m_w_up, m_w_down, m_norm_f_w, v_norm1_w, v_w_in, v_mu_shift, v_w_up_f, v_w0_f, v_w_up_b, v_w0_b, v_a_up_f, v_a0_f, v_a_up_b, v_a0_b, v_g_up, v_k_k, v_k_a_f, v_k_a_b, v_r_k_f, v_r_k_b, v_gn_w, v_gn_b, v_conv_w, v_w_out, v_norm2_w, v_w_gate, v_w_up, v_w_down, v_norm_f_w):
    args = locals()
    w = {name: args[name] for name in _WEIGHTS}
    m = {name: args["m_" + name] for name in _WEIGHTS}
    v = {name: args["v_" + name] for name in _WEIGHTS}
    return _train_step(x, loss_target, w, m, v)
```

```python
import functools

import jax
import jax.numpy as jnp
from jax import lax
from jax.experimental import pallas as pl
from jax.experimental.pallas import tpu as pltpu

F32 = jnp.float32
BF16 = jnp.bfloat16
MESH = pl.DeviceIdType.MESH

D_MODEL = 1024
D_RWKV = 512
HEAD = 64
N_PAIR = D_RWKV // (2 * HEAD)
D_LORA = 64
D_GATE = 160
D_GATE_PAD = 384
D_FF = 2816
D_SHIFT = 1824
D_SHIFT_PAD = 2048
D_CONV3 = 1536
LOG_DECAY_SCALE = 0.606531
RMS_EPS = 1e-6
GN_EPS = 64e-5
NORM_EPS = 1e-12
ADAM_LR, ADAM_B1, ADAM_B2, ADAM_EPS, ADAM_WD, ADAM_STEP = 0.001, 0.9, 0.999, 1e-08, 0.01, 10

N_SHARD = 4
N_DEV = 8
V7X_VMEM_LIMIT = 48 * 1024 * 1024
SCAN_CHUNK = 32
GROUP = 8

_PACK_ROWS = (("w_in", 840), ("w_out", 256), ("w_gate", 704), ("w_up", 704), ("w_down", 704),
              ("w_up_f", 8), ("w_up_b", 8), ("a_up_f", 8), ("a_up_b", 8), ("g_up", 20), ("conv_w", 1))
_FFN_NAMES = ("w_gate", "w_up", "w_down")
_EARLY_ROWS = tuple(item for item in _PACK_ROWS if item[0] not in _FFN_NAMES)
_FFN_ROWS = tuple(item for item in _PACK_ROWS if item[0] in _FFN_NAMES)
EARLY_R = 1152
SMALL_ROWS = 24


def _tile(n, cap, mult=128):
    best = None
    t = mult
    while t <= min(n, cap):
        if n % t == 0:
            best = t
        t += mult
    return best or n


def _cp(*sem):
    return pltpu.CompilerParams(dimension_semantics=sem or None, vmem_limit_bytes=V7X_VMEM_LIMIT)


def _sds(shape, dtype=F32):
    return jax.ShapeDtypeStruct(shape, dtype)


def _matmul(a, b, *, mode, name, out_dtype=F32, add=None):
    m, kdim = a.shape
    n = b.shape[1] if mode == "nn" else b.shape[0]
    tm, tn = _tile(m, 512, 8), _tile(n, 1536)
    tk = kdim if kdim <= 3584 else _tile(kdim, 1024)
    nk = kdim // tk
    a_spec = pl.BlockSpec((tm, tk), lambda i, j, k: (i, k))
    if mode == "nn":
        b_spec = pl.BlockSpec((tk, tn), lambda i, j, k: (k, j))
        dims = (((1,), (0,)), ((), ()))
    else:
        b_spec = pl.BlockSpec((tn, tk), lambda i, j, k: (j, k))
        dims = (((1,), (1,)), ((), ()))
    has_add = add is not None

    def body(*refs):
        a_ref, b_ref = refs[0], refs[1]
        add_ref = refs[2] if has_add else None
        o_ref = refs[3] if has_add else refs[2]
        part = lax.dot_general(a_ref[...].astype(BF16), b_ref[...].astype(BF16), dims,
                               preferred_element_type=F32)
        if nk == 1:
            if has_add:
                part = part + add_ref[...]
            o_ref[...] = part.astype(out_dtype)
        else:
            acc_ref = refs[-1]
            k = pl.program_id(2)

            @pl.when(k == 0)
            def _():
                acc_ref[...] = jnp.zeros_like(acc_ref)

            acc_ref[...] += part

            @pl.when(k == nk - 1)
            def _():
                res = acc_ref[...]
                if has_add:
                    res = res + add_ref[...]
                o_ref[...] = res.astype(out_dtype)

    o_spec = pl.BlockSpec((tm, tn), lambda i, j, k: (i, j))
    in_specs = [a_spec, b_spec] + ([o_spec] if has_add else [])
    args = (a, b) + ((add,) if has_add else ())
    return pl.pallas_call(
        body, name=name, grid=(m // tm, n // tn, nk), in_specs=in_specs, out_specs=o_spec,
        out_shape=_sds((m, n), out_dtype),
        scratch_shapes=[pltpu.VMEM((tm, tn), F32)] if nk > 1 else [],
        compiler_params=_cp("parallel", "parallel", "arbitrary"),
    )(*args)


def _row(tm, width):
    return pl.BlockSpec((tm, width), lambda i: (i, 0))


def _col(tm, height):
    return pl.BlockSpec((height, tm), lambda i: (0, i))


def _fixed(shape):
    return pl.BlockSpec(shape, lambda i: tuple(0 for _ in shape))


def _rmsnorm_fwd(x, w, tm, name):
    t, d = x.shape

    def body(x_ref, w_ref, o_ref, ot_ref):
        xv = x_ref[...]
        rstd = lax.rsqrt(jnp.mean(xv * xv, axis=-1, keepdims=True) + RMS_EPS)
        yv = xv * rstd * w_ref[...]
        o_ref[...] = yv.astype(BF16)
        ot_ref[...] = jnp.transpose(yv).astype(BF16)

    return pl.pallas_call(
        body, name=name, grid=(t // tm,), in_specs=[_row(tm, d), _fixed((1, d))],
        out_specs=[_row(tm, d), _col(tm, d)], out_shape=[_sds((t, d), BF16), _sds((d, t), BF16)],
        compiler_params=_cp("parallel"))(x, w)


def _rms_bwd_math(xv, wv, dyv):
    rstd = lax.rsqrt(jnp.mean(xv * xv, axis=-1, keepdims=True) + RMS_EPS)
    xhat = xv * rstd
    gv = dyv * wv
    dx = rstd * (gv - xhat * jnp.mean(gv * xhat, axis=-1, keepdims=True))
    return dx, jnp.sum(dyv * xhat, axis=0, keepdims=True)


def _proj_norm_bwd(dy_a, w_a, dy_b, w_b, x, w, dres, tm, name):
    t, d = x.shape
    ka, kb = dy_a.shape[1], dy_b.shape[1]
    nt = (((1,), (1,)), ((), ()))

    def body(dya_ref, wa_ref, dyb_ref, wb_ref, x_ref, w_ref, dres_ref, dx_ref, dw_ref):
        d_h = (lax.dot_general(dya_ref[...], wa_ref[...], nt, preferred_element_type=F32)
               + lax.dot_general(dyb_ref[...], wb_ref[...], nt, preferred_element_type=F32))
        dx, dw = _rms_bwd_math(x_ref[...], w_ref[...], d_h)
        dx_ref[...] = dres_ref[...] + dx

        @pl.when(pl.program_id(0) == 0)
        def _():
            dw_ref[...] = jnp.zeros_like(dw_ref)

        dw_ref[...] += dw

    return pl.pallas_call(
        body, name=name, grid=(t // tm,),
        in_specs=[_row(tm, ka), _fixed((d, ka)), _row(tm, kb), _fixed((d, kb)), _row(tm, d), _fixed((1, d)),
                  _row(tm, d)],
        out_specs=[_row(tm, d), _fixed((1, d))],
        out_shape=[_sds((t, d)), _sds((1, d))], compiler_params=_cp("arbitrary"))(dy_a, w_a, dy_b, w_b, x, w, dres)


def _ffn_out_loss(act, w_down, x1, w, target, tm):
    t, d = x1.shape
    f = act.shape[1]

    def body(a_ref, wd_ref, x_ref, w_ref, t_ref, dx_ref, dw_ref, loss_ref):
        xv = x_ref[...] + jnp.dot(a_ref[...], wd_ref[...], preferred_element_type=F32)
        wv = w_ref[...]
        rstd = lax.rsqrt(jnp.mean(xv * xv, axis=-1, keepdims=True) + RMS_EPS)
        err = xv * rstd * wv - t_ref[...]
        dx, dw = _rms_bwd_math(xv, wv, err * (1.0 / d))
        dx_ref[...] = dx

        @pl.when(pl.program_id(0) == 0)
        def _():
            dw_ref[...] = jnp.zeros_like(dw_ref)
            loss_ref[...] = jnp.zeros_like(loss_ref)

        dw_ref[...] += dw
        loss_ref[...] += 0.5 * jnp.sum(jnp.mean(err * err, axis=-1, keepdims=True), axis=0, keepdims=True)

    return pl.pallas_call(
        body, name="ffn_out_loss", grid=(t // tm,),
        in_specs=[_row(tm, f), _fixed((f, d)), _row(tm, d), _fixed((1, d)), _row(tm, d)],
        out_specs=[_row(tm, d), _fixed((1, d)), _fixed((1, 1))],
        out_shape=[_sds((t, d)), _sds((1, d)), _sds((1, 1))], compiler_params=_cp("arbitrary"))(act, w_down, x1, w, target)


def _ffn_in(h, w_gate, w_up):
    t, d = h.shape
    f = w_gate.shape[1]
    tm, tn = _tile(t, 512, 8), _tile(f, 1536)

    def body(h_ref, wg_ref, wu_ref, g_ref, u_ref, a_ref, at_ref):
        hv = h_ref[...]
        gv = jnp.dot(hv, wg_ref[...], preferred_element_type=F32)
        uv = jnp.dot(hv, wu_ref[...], preferred_element_type=F32)
        act = gv * jax.nn.sigmoid(gv) * uv
        g_ref[...] = gv.astype(BF16)
        u_ref[...] = uv.astype(BF16)
        a_ref[...] = act.astype(BF16)
        at_ref[...] = jnp.transpose(act).astype(BF16)

    w_spec = pl.BlockSpec((d, tn), lambda i, j: (0, j))
    o_spec = pl.BlockSpec((tm, tn), lambda i, j: (i, j))
    return pl.pallas_call(
        body, name="ffn_in", grid=(t // tm, f // tn),
        in_specs=[pl.BlockSpec((tm, d), lambda i, j: (i, 0)), w_spec, w_spec],
        out_specs=[o_spec, o_spec, o_spec, pl.BlockSpec((tn, tm), lambda i, j: (j, i))],
        out_shape=[_sds((t, f), BF16)] * 3 + [_sds((f, t), BF16)],
        compiler_params=_cp("parallel", "parallel"))(h, w_gate, w_up)


def _ffn_in_bwd(d_out, w_down, gate, up):
    t, d = d_out.shape
    f = w_down.shape[0]
    tm, tn = _tile(t, 512, 8), _tile(f, 1536)

    def body(do_ref, w_ref, g_ref, u_ref, dg_ref, du_ref):
        dv = lax.dot_general(do_ref[...].astype(BF16), w_ref[...], (((1,), (1,)), ((), ())),
                             preferred_element_type=F32)
        gv, uv = g_ref[...].astype(F32), u_ref[...].astype(F32)
        sg = jax.nn.sigmoid(gv)
        du_ref[...] = (dv * gv * sg).astype(BF16)
        dg_ref[...] = (dv * uv * (sg * (1.0 + gv * (1.0 - sg)))).astype(BF16)

    tile = pl.BlockSpec((tm, tn), lambda i, j: (i, j))
    return pl.pallas_call(
        body, name="ffn_in_bwd", grid=(t // tm, f // tn),
        in_specs=[pl.BlockSpec((tm, d), lambda i, j: (i, 0)), pl.BlockSpec((tn, d), lambda i, j: (j, 0)), tile, tile],
        out_specs=[tile, tile], out_shape=[_sds((t, f), BF16)] * 2,
        compiler_params=_cp("parallel", "parallel"))(d_out, w_down, gate, up)


def _halo_specs(tm, width, rows_total):
    per = tm // GROUP
    last = rows_total // GROUP - 1
    prev = pl.BlockSpec((GROUP, width), lambda i: (jnp.maximum(i * per - 1, 0), 0))
    nxt = pl.BlockSpec((GROUP, width), lambda i: (jnp.minimum((i + 1) * per, last), 0))
    return prev, nxt


def _edge_flags(tm, seq):
    i = pl.program_id(0)
    has_prev = jnp.where((i * tm) % seq == 0, 0.0, 1.0).astype(F32)
    has_next = jnp.where(((i + 1) * tm) % seq == 0, 0.0, 1.0).astype(F32)
    return has_prev, has_next


def _shifted(xv, prev_row, next_row):
    tm = xv.shape[0]
    row = lax.broadcasted_iota(jnp.int32, xv.shape, 0)
    down = jnp.where(row == 0, prev_row, pltpu.roll(xv, 1, axis=0))
    up = jnp.where(row == tm - 1, next_row, pltpu.roll(xv, tm - 1, axis=0))
    return down, up


def _shift_fwd(p, mu, tm, seq):
    t, w = p.shape
    prev_spec, next_spec = _halo_specs(tm, w, t)

    def body(p_ref, hp_ref, hn_ref, mu_ref, o_ref):
        has_prev, has_next = _edge_flags(tm, seq)
        xv = p_ref[...]
        down, up = _shifted(xv, hp_ref[GROUP - 1:GROUP, :] * has_prev, hn_ref[0:1, :] * has_next)
        o_ref[...] = xv + mu_ref[0:1, :] * (0.5 * (down + up) - xv)

    return pl.pallas_call(
        body, name="shift_fwd", grid=(t // tm,),
        in_specs=[_row(tm, w), prev_spec, next_spec, _fixed((GROUP, w))], out_specs=_row(tm, w),
        out_shape=_sds((t, w)), compiler_params=_cp("parallel"))(p, p, p, mu)


def _shift_bwd(q, p, mu, tm, seq):
    t, w = p.shape
    prev_spec, next_spec = _halo_specs(tm, w, t)

    def body(q_ref, qp_ref, qn_ref, p_ref, pp_ref, pn_ref, mu_ref, dp_ref, dmu_ref):
        has_prev, has_next = _edge_flags(tm, seq)
        muv = mu_ref[0:1, :]
        qv = q_ref[...]
        mq = muv * qv
        mq_down, mq_up = _shifted(mq, muv * qp_ref[GROUP - 1:GROUP, :] * has_prev,
                                  muv * qn_ref[0:1, :] * has_next)
        dp_ref[...] = (qv - mq + 0.5 * (mq_down + mq_up)).astype(BF16)
        pv = p_ref[...]
        p_down, p_up = _shifted(pv, pp_ref[GROUP - 1:GROUP, :] * has_prev, pn_ref[0:1, :] * has_next)

        @pl.when(pl.program_id(0) == 0)
        def _():
            dmu_ref[...] = jnp.zeros_like(dmu_ref)

        dmu_ref[...] += jnp.sum(qv * (0.5 * (p_down + p_up) - pv), axis=0, keepdims=True)

    return pl.pallas_call(
        body, name="shift_bwd", grid=(t // tm,),
        in_specs=[_row(tm, w), prev_spec, next_spec, _row(tm, w), prev_spec, next_spec, _fixed((GROUP, w))],
        out_specs=[_row(tm, w), _fixed((1, w))],
        out_shape=[_sds((t, w), BF16), _sds((1, w))], compiler_params=_cp("arbitrary"))(q, q, q, p, p, p, mu)


@jax.custom_vjp
def _bdot(a, b):
    return jnp.dot(a.astype(BF16), b.astype(BF16), preferred_element_type=F32)


def _bdot_fwd(a, b):
    return _bdot(a, b), (a, b)


def _bdot_bwd(res, g):
    a, b = res
    gb = g.astype(BF16)
    da = lax.dot_general(gb, b.astype(BF16), (((1,), (1,)), ((), ())), preferred_element_type=F32)
    db = lax.dot_general(a.astype(BF16), gb, (((0,), (0,)), ((), ())), preferred_element_type=F32)
    return da, db


_bdot.defvjp(_bdot_fwd, _bdot_bwd)


def _seg_raw(x, ones_blocks):
    hi = x.astype(BF16)
    lo = (x - hi.astype(F32)).astype(BF16)
    return (jnp.dot(hi, ones_blocks, preferred_element_type=F32)
            + jnp.dot(lo, ones_blocks, preferred_element_type=F32))


@jax.custom_vjp
def _seg(x, ones_blocks):
    return _seg_raw(x, ones_blocks)


def _seg_fwd(x, ones_blocks):
    return _seg_raw(x, ones_blocks), ones_blocks


def _seg_bwd(ones_blocks, g):
    return _seg_raw(g, ones_blocks), jnp.zeros_like(ones_blocks)


_seg.defvjp(_seg_fwd, _seg_bwd)


def _head_ones():
    h = jnp.arange(D_RWKV) // HEAD
    return (h[:, None] == h[None, :]).astype(BF16)


def _prep_math(ps, k_k, w0_f, a0_f, k_a_f, w0_b, a0_b, k_a_b, wup_f, aup_f, wup_b, aup_b, gup, ones_blocks):
    r = ps[:, 0:512]
    k = ps[:, 512:1024]
    v = ps[:, 1024:1536]
    xwa = ps[:, 1536:1664]
    xg = ps[:, 1664:D_SHIFT_PAD]
    kk_raw = k * k_k
    norm = jnp.sqrt(_seg(kk_raw * kk_raw, ones_blocks))
    kk = kk_raw / jnp.maximum(norm, NORM_EPS)
    t_xwa = jnp.tanh(xwa)
    outs = [r, v, kk]
    for w0, a0, k_a, wup, aup in ((w0_f, a0_f, k_a_f, wup_f, aup_f), (w0_b, a0_b, k_a_b, wup_b, aup_b)):
        decay = jnp.exp(-LOG_DECAY_SCALE * jax.nn.sigmoid(w0 + _bdot(t_xwa, wup)))
        rate = jax.nn.sigmoid(a0 + _bdot(xwa, aup))
        outs += [decay, k * (1.0 + (rate - 1.0) * k_a), kk * rate]
    outs.append(_bdot(jax.nn.sigmoid(xg), gup))
    return tuple(outs)


def _prep_args(tm, ps_ref, pv_ref, mat_refs, ones_ref):
    vecs = [jnp.broadcast_to(pv_ref[j:j + 1, :], (tm, D_RWKV)) for j in range(7)]
    return [ps_ref[...]] + vecs + [m[...] for m in mat_refs] + [ones_ref[...]]


_PREP_MAT_SHAPES = ((128, D_RWKV),) * 4 + ((D_GATE_PAD, D_RWKV),)


def _prep_fwd(ps, pvec, mats, ones_blocks, tm):
    t = ps.shape[0]

    def body(ps_ref, pv_ref, m0, m1, m2, m3, m4, ones_ref, *out_refs):
        outs = _prep_math(*_prep_args(tm, ps_ref, pv_ref, (m0, m1, m2, m3, m4), ones_ref))
        for o_ref, val in zip(out_refs, outs[2:]):
            o_ref[...] = val

    return pl.pallas_call(
        body, name="prep_fwd", grid=(t // tm,),
        in_specs=[_row(tm, D_SHIFT_PAD), _fixed((8, D_RWKV))] + [_fixed(s) for s in _PREP_MAT_SHAPES]
        + [_fixed((D_RWKV, D_RWKV))],
        out_specs=[_row(tm, D_RWKV)] * 8, out_shape=[_sds((t, D_RWKV))] * 8,
        compiler_params=_cp("parallel"))(ps, pvec, *mats, ones_blocks)


def _prep_bwd(ps, pvec, mats, ones_blocks, cts, tm):
    t = ps.shape[0]
    counts = [len(c) for c in cts]
    flat = [a for c in cts for a in c]

    def body(ps_ref, pv_ref, m0, m1, m2, m3, m4, ones_ref, *refs):
        ct_refs = refs[:len(flat)]
        q_ref, dpv_ref = refs[len(flat)], refs[len(flat) + 1]
        dmat_refs = refs[len(flat) + 2:]
        args = _prep_args(tm, ps_ref, pv_ref, (m0, m1, m2, m3, m4), ones_ref)
        _, vjp = jax.vjp(lambda *a: _prep_math(*a, args[-1]), *args[:-1])
        ct_vals, pos = [], 0
        for n in counts:
            val = ct_refs[pos][...]
            for extra in ct_refs[pos + 1:pos + n]:
                val = val + extra[...]
            ct_vals.append(val)
            pos += n
        grads = vjp(tuple(ct_vals))
        q_ref[...] = grads[0]

        @pl.when(pl.program_id(0) == 0)
        def _():
            dpv_ref[...] = jnp.zeros_like(dpv_ref)
            for d_ref in dmat_refs:
                d_ref[...] = jnp.zeros_like(d_ref)

        for j in range(7):
            dpv_ref[j:j + 1, :] += jnp.sum(grads[1 + j], axis=0, keepdims=True)
        for d_ref, gm in zip(dmat_refs, grads[8:13]):
            d_ref[...] += gm

    return pl.pallas_call(
        body, name="prep_bwd", grid=(t // tm,),
        in_specs=[_row(tm, D_SHIFT_PAD), _fixed((8, D_RWKV))] + [_fixed(s) for s in _PREP_MAT_SHAPES]
        + [_fixed((D_RWKV, D_RWKV))] + [_row(tm, D_RWKV)] * len(flat),
        out_specs=[_row(tm, D_SHIFT_PAD), _fixed((8, D_RWKV))] + [_fixed(s) for s in _PREP_MAT_SHAPES],
        out_shape=[_sds((t, D_SHIFT_PAD)), _sds((8, D_RWKV))] + [_sds(s) for s in _PREP_MAT_SHAPES],
        compiler_params=_cp("arbitrary"))(ps, pvec, *mats, ones_blocks, *flat)


def _pair_ones():
    h = jnp.arange(2 * HEAD) // HEAD
    block = (h[:, None] == h[None, :]).astype(BF16)
    return jnp.concatenate([block, block], axis=0)


def _diag_mask():
    lane = lax.broadcasted_iota(jnp.int32, (HEAD, 2 * HEAD), 1)
    sub = lax.broadcasted_iota(jnp.int32, (HEAD, 2 * HEAD), 0)
    return jnp.where((lane & (HEAD - 1)) == sub, 1.0, 0.0).astype(F32)


def _to_row(cols, dmask):
    return jnp.sum(cols * dmask, axis=0, keepdims=True)


def _seg_many(exact, rounded, ones_pair):
    out_exact, out_rounded = [], []
    if exact:
        parts = []
        for x in exact:
            hi = x.astype(BF16)
            parts.append(jnp.concatenate([hi, (x - hi.astype(F32)).astype(BF16)], axis=1))
        res = jnp.dot(jnp.concatenate(parts, axis=0), ones_pair, preferred_element_type=F32)
        out_exact = [res[HEAD * c:HEAD * (c + 1)] for c in range(len(exact))]
    if rounded:
        res = jnp.dot(jnp.concatenate([x.astype(BF16) for x in rounded], axis=0), ones_pair[0:2 * HEAD],
                      preferred_element_type=F32)
        out_rounded = [res[HEAD * c:HEAD * (c + 1)] for c in range(len(rounded))]
    return out_exact, out_rounded


N_CHAIN = 2 * N_PAIR


def _chain(c):
    d, p = divmod(c, N_PAIR)
    return d, slice(2 * HEAD * p, 2 * HEAD * (p + 1))


def _scan_specs(n_chunks, col_blocks, fwd_chunk, bwd_chunk):
    def spec(chunk_of, col):
        return pl.BlockSpec((SCAN_CHUNK, D_RWKV), lambda b, g: (b * n_chunks + chunk_of(g), col))
    return [spec(fwd_chunk, c) for c in col_blocks] + [spec(bwd_chunk, c) for c in col_blocks]


def _scan_fwd(ps, kk, dirs, batch, seq):
    t = batch * seq
    n = seq // SCAN_CHUNK
    groups = SCAN_CHUNK // GROUP
    up = lambda g: g
    down = lambda g: n - 1 - g
    col_blocks = (0, 2, 0, 0, 0, 0)

    def body(*refs):
        dir_refs = (refs[0:6], refs[6:12])
        ones_ref = refs[12]
        y_refs, hist_refs, st_ref = refs[13:15], refs[15:17], refs[17]

        @pl.when(pl.program_id(1) == 0)
        def _():
            st_ref[...] = jnp.zeros_like(st_ref)

        ones_pair = ones_ref[...]
        dmask = _diag_mask()
        dmask_b = dmask.astype(BF16)
        sub8 = lax.broadcasted_iota(jnp.int32, (GROUP, 2 * HEAD), 0)

        def group(gi, carry):
            off = (pl.multiple_of(gi * GROUP, GROUP), pl.multiple_of((groups - 1 - gi) * GROUP, GROUP))
            loaded = [tuple(ref[pl.ds(off[d], GROUP), :] for ref in dir_refs[d]) for d in range(2)]
            states = list(carry)
            y_acc = [jnp.zeros((GROUP, 2 * HEAD), F32) for _ in range(N_CHAIN)]
            for step in range(GROUP):
                rows, idx = [], []
                for c in range(N_CHAIN):
                    d, lanes = _chain(c)
                    i = step if d == 0 else GROUP - 1 - step
                    idx.append(i)
                    rows.append(tuple(x8[i:i + 1, lanes] for x8 in loaded[d]))
                    hist_refs[d][c % N_PAIR, gi * GROUP + step] = states[c]
                _, v_cols = _seg_many([], [dmask_b * rows[c][1].astype(BF16) for c in range(N_CHAIN)], ones_pair)
                sas, _ = _seg_many([states[c] * rows[c][2] for c in range(N_CHAIN)], [], ones_pair)
                for c in range(N_CHAIN):
                    _, _, _, w_row, kd_row, b_row = rows[c]
                    states[c] = states[c] * w_row - sas[c] * b_row + v_cols[c] * kd_row
                _, ys = _seg_many([], [states[c] * rows[c][0] for c in range(N_CHAIN)], ones_pair)
                for c in range(N_CHAIN):
                    y_acc[c] = jnp.where(sub8 == idx[c], _to_row(ys[c], dmask), y_acc[c])
            for c in range(N_CHAIN):
                d, lanes = _chain(c)
                y_refs[d][pl.ds(off[d], GROUP), lanes] = y_acc[c]
            return tuple(states)

        final = lax.fori_loop(0, groups, group, tuple(st_ref[c] for c in range(N_CHAIN)))
        for c in range(N_CHAIN):
            st_ref[c] = final[c]
            hist_refs[c // N_PAIR][c % N_PAIR, SCAN_CHUNK] = final[c]

    y_spec_f = pl.BlockSpec((SCAN_CHUNK, D_RWKV), lambda b, g: (b * n + up(g), 0))
    y_spec_b = pl.BlockSpec((SCAN_CHUNK, D_RWKV), lambda b, g: (b * n + down(g), 0))
    hist_shape = (batch, n, N_PAIR, SCAN_CHUNK + 1, HEAD, 2 * HEAD)
    hist_block = (None, None, N_PAIR, SCAN_CHUNK + 1, HEAD, 2 * HEAD)
    hist_spec_f = pl.BlockSpec(hist_block, lambda b, g: (b, up(g), 0, 0, 0, 0))
    hist_spec_b = pl.BlockSpec(hist_block, lambda b, g: (b, down(g), 0, 0, 0, 0))
    ones_spec = pl.BlockSpec((4 * HEAD, 2 * HEAD), lambda b, g: (0, 0))
    (wf, kdf, bf), (wb, kdb, bb) = dirs
    return pl.pallas_call(
        body, name="wkv_fwd", grid=(batch, n),
        in_specs=_scan_specs(n, col_blocks, up, down) + [ones_spec],
        out_specs=[y_spec_f, y_spec_b, hist_spec_f, hist_spec_b],
        out_shape=[_sds((t, D_RWKV)), _sds((t, D_RWKV)), _sds(hist_shape), _sds(hist_shape)],
        scratch_shapes=[pltpu.VMEM((N_CHAIN, HEAD, 2 * HEAD), F32)],
        compiler_params=_cp("parallel", "arbitrary"),
    )(ps, ps, kk, wf, kdf, bf, ps, ps, kk, wb, kdb, bb, _pair_ones())


def _scan_bwd(ps, kk, dirs, dy, hist_f, hist_b, batch, seq):
    t = batch * seq
    n = seq // SCAN_CHUNK
    groups = SCAN_CHUNK // GROUP
    fwd_chunk = lambda g: n - 1 - g
    bwd_chunk = lambda g: g
    col_blocks = (0, 2, 0, 0, 0, 0, 0)

    def undo_group(dir_refs, out_refs, hist_refs, gi, d_states, ones_pair, dmask, sub8):
        d_states = list(d_states)
        loaded, blocks = [], []
        for d in range(2):
            blk = groups - 1 - gi if d == 0 else gi
            blocks.append(pl.ds(pl.multiple_of(blk * GROUP, GROUP), GROUP))
            r8, v8, kk8, w8, kd8, b8, dy8 = (ref[blocks[d], :] for ref in dir_refs[d])
            loaded.append((r8, v8, kk8, w8, kd8, -b8, dy8))
        acc = [[jnp.zeros((GROUP, 2 * HEAD), F32) for _ in range(6)] for _ in range(N_CHAIN)]
        for step in range(GROUP):
            rows, idx, before, after = [], [], [], []
            for c in range(N_CHAIN):
                d, lanes = _chain(c)
                i = GROUP - 1 - step if d == 0 else step
                q = (groups - 1 - gi) * GROUP + i if d == 0 else SCAN_CHUNK - 1 - (gi * GROUP + i)
                idx.append(i)
                rows.append(tuple(x8[i:i + 1, lanes] for x8 in loaded[d]))
                before.append(hist_refs[d][c % N_PAIR, q])
                after.append(hist_refs[d][c % N_PAIR, q + 1])
            _, cols = _seg_many([], [dmask.astype(BF16) * rows[c][j].astype(BF16) for c in range(N_CHAIN) for j in (1, 6)],
                                ones_pair)
            v_cols, dy_cols = cols[0::2], cols[1::2]
            d_now = [d_states[c] + dy_cols[c] * rows[c][0] for c in range(N_CHAIN)]
            d_sas, _ = _seg_many([d_now[c] * rows[c][5] for c in range(N_CHAIN)], [], ones_pair)
            _, others = _seg_many(
                [], [x for c in range(N_CHAIN) for x in (before[c] * rows[c][2], d_now[c] * rows[c][4])], ones_pair)
            for c in range(N_CHAIN):
                sa, d_sa, dv_cols = others[2 * c], d_sas[c], others[2 * c + 1]
                rows_out = (
                    jnp.sum(after[c] * dy_cols[c], axis=0, keepdims=True),
                    jnp.sum(d_now[c] * before[c], axis=0, keepdims=True),
                    jnp.sum(d_now[c] * v_cols[c], axis=0, keepdims=True),
                    _to_row(dv_cols, dmask),
                    jnp.sum(before[c] * d_sa, axis=0, keepdims=True),
                    -jnp.sum(d_now[c] * sa, axis=0, keepdims=True),
                )
                acc[c] = [jnp.where(sub8 == idx[c], val, a) for val, a in zip(rows_out, acc[c])]
                d_states[c] = d_now[c] * rows[c][3] + d_sa * rows[c][2]
        for c in range(N_CHAIN):
            d, lanes = _chain(c)
            for o_ref, val in zip(out_refs[d], acc[c]):
                o_ref[blocks[d], lanes] = val
        return tuple(d_states)

    def body(*refs):
        dir_refs = (refs[0:7], refs[7:14])
        hist_refs, ones_ref = refs[14:16], refs[16]
        out_refs = (refs[17:23], refs[23:29])
        dst_ref = refs[29]

        @pl.when(pl.program_id(1) == 0)
        def _():
            dst_ref[...] = jnp.zeros_like(dst_ref)

        ones_pair = ones_ref[...]
        dmask = _diag_mask()
        sub8 = lax.broadcasted_iota(jnp.int32, (GROUP, 2 * HEAD), 0)

        def group(gi, carry):
            return undo_group(dir_refs, out_refs, hist_refs, gi, carry, ones_pair, dmask, sub8)

        final = lax.fori_loop(0, groups, group, tuple(dst_ref[c] for c in range(N_CHAIN)))
        for c in range(N_CHAIN):
            dst_ref[c] = final[c]

    blk = (SCAN_CHUNK, D_RWKV)
    out_f = pl.BlockSpec(blk, lambda b, g: (b * n + fwd_chunk(g), 0))
    out_b = pl.BlockSpec(blk, lambda b, g: (b * n + bwd_chunk(g), 0))
    hist_block = (None, None, N_PAIR, SCAN_CHUNK + 1, HEAD, 2 * HEAD)
    hist_spec_f = pl.BlockSpec(hist_block, lambda b, g: (b, fwd_chunk(g), 0, 0, 0, 0))
    hist_spec_b = pl.BlockSpec(hist_block, lambda b, g: (b, bwd_chunk(g), 0, 0, 0, 0))
    ones_spec = pl.BlockSpec((4 * HEAD, 2 * HEAD), lambda b, g: (0, 0))
    (wf, kdf, bf), (wb, kdb, bb) = dirs
    outs = pl.pallas_call(
        body, name="wkv_bwd", grid=(batch, n),
        in_specs=_scan_specs(n, col_blocks, fwd_chunk, bwd_chunk) + [hist_spec_f, hist_spec_b, ones_spec],
        out_specs=[out_f] * 6 + [out_b] * 6,
        out_shape=[_sds((t, D_RWKV))] * 12,
        scratch_shapes=[pltpu.VMEM((N_CHAIN, HEAD, 2 * HEAD), F32)],
        compiler_params=_cp("parallel", "arbitrary"),
    )(ps, ps, kk, wf, kdf, bf, dy, ps, ps, kk, wb, kdb, bb, dy, hist_f, hist_b, _pair_ones())
    return outs[0:6], outs[6:12]


def _post_math(y, r, kd_f, kd_b, v, gate, gn_w, gn_b, rk_f, rk_b, ones_blocks):
    mean = _seg(y, ones_blocks) * (1.0 / HEAD)
    yc = y - mean
    var = _seg(yc * yc, ones_blocks) * (1.0 / HEAD)
    yn = yc * lax.rsqrt(var + GN_EPS) * gn_w + gn_b
    bonus = _seg(r * kd_f * rk_f, ones_blocks) * v + _seg(r * kd_b * rk_b, ones_blocks) * v
    return (yn + bonus) * gate


def _conv_parts(pc, halo_prev, halo_next, has_prev, has_next):
    gate_b, gate_c, hid = pc[:, 0:512], pc[:, 512:1024], pc[:, 1024:1536]
    u = gate_c * hid
    u_prev_row = halo_prev[GROUP - 1:GROUP, 512:1024] * halo_prev[GROUP - 1:GROUP, 1024:1536] * has_prev
    u_next_row = halo_next[0:1, 512:1024] * halo_next[0:1, 1024:1536] * has_next
    u_down, u_up = _shifted(u, u_prev_row, u_next_row)
    return gate_b, gate_c, hid, u, u_down, u_up


def _post_specs(tm, t):
    pc_prev, pc_next = _halo_specs(tm, D_CONV3, t)
    col = lambda c: pl.BlockSpec((tm, D_RWKV), lambda i: (i, c))
    return ([col(0), col(0), col(0), col(0), col(0), col(2), col(0), _row(tm, D_CONV3), pc_prev, pc_next,
             _fixed((8, D_RWKV)), _fixed((D_RWKV, D_RWKV))])


def _post_fwd(y_f, y_b, ps, kd_f, kd_b, gate, pc, qvec, ones_blocks, tm, seq):
    t = ps.shape[0]

    def body(yf_ref, yb_ref, r_ref, kdf_ref, kdb_ref, v_ref, g_ref, pc_ref, hp_ref, hn_ref, qv_ref, ones_ref,
             o_ref, ot_ref):
        has_prev, has_next = _edge_flags(tm, seq)
        vec = [jnp.broadcast_to(qv_ref[j:j + 1, :], (tm, D_RWKV)) for j in range(7)]
        o_rwkv = _post_math(yf_ref[...] + yb_ref[...], r_ref[...], kdf_ref[...], kdb_ref[...], v_ref[...],
                            g_ref[...], vec[0], vec[1], vec[2], vec[3], ones_ref[...])
        gate_b, _, _, u, u_down, u_up = _conv_parts(pc_ref[...], hp_ref[...], hn_ref[...], has_prev, has_next)
        o_conv = gate_b * (vec[4] * u_down + vec[5] * u + vec[6] * u_up)
        for half, val in enumerate((o_rwkv, o_conv)):
            o_ref[:, D_RWKV * half:D_RWKV * (half + 1)] = val.astype(BF16)
            ot_ref[D_RWKV * half:D_RWKV * (half + 1), :] = jnp.transpose(val).astype(BF16)

    return pl.pallas_call(
        body, name="post_fwd", grid=(t // tm,), in_specs=_post_specs(tm, t),
        out_specs=[_row(tm, D_MODEL), _col(tm, D_MODEL)],
        out_shape=[_sds((t, D_MODEL), BF16), _sds((D_MODEL, t), BF16)], compiler_params=_cp("parallel"),
    )(y_f, y_b, ps, kd_f, kd_b, ps, gate, pc, pc, pc, qvec, ones_blocks)


def _post_bwd(d_out, y_f, y_b, ps, kd_f, kd_b, gate, pc, qvec, ones_blocks, tm, seq):
    t = ps.shape[0]
    do_prev, do_next = _halo_specs(tm, D_MODEL, t)

    def body(do_ref, dop_ref, don_ref, yf_ref, yb_ref, r_ref, kdf_ref, kdb_ref, v_ref, g_ref, pc_ref, hp_ref,
             hn_ref, qv_ref, ones_ref, dy_ref, dr_ref, dkdf_ref, dkdb_ref, dv_ref, dg_ref, dpc_ref, dqv_ref):
        has_prev, has_next = _edge_flags(tm, seq)
        vec = [jnp.broadcast_to(qv_ref[j:j + 1, :], (tm, D_RWKV)) for j in range(7)]
        ones_v = ones_ref[...]
        args = (yf_ref[...] + yb_ref[...], r_ref[...], kdf_ref[...], kdb_ref[...], v_ref[...], g_ref[...],
                vec[0], vec[1], vec[2], vec[3])
        _, vjp = jax.vjp(lambda *a: _post_math(*a, ones_v), *args)
        grads = vjp(do_ref[:, 0:D_RWKV])
        for o_ref, gval in zip((dy_ref, dr_ref, dkdf_ref, dkdb_ref, dv_ref, dg_ref), grads[0:6]):
            o_ref[...] = gval

        hp, hn = hp_ref[...], hn_ref[...]
        gate_b, gate_c, hid, u, u_down, u_up = _conv_parts(pc_ref[...], hp, hn, has_prev, has_next)
        d_oc = do_ref[:, D_RWKV:2 * D_RWKV]
        d_cu = d_oc * gate_b
        d_cu_prev = dop_ref[GROUP - 1:GROUP, D_RWKV:2 * D_RWKV] * hp[GROUP - 1:GROUP, 0:512] * has_prev
        d_cu_next = don_ref[0:1, D_RWKV:2 * D_RWKV] * hn[0:1, 0:512] * has_next
        d_cu_down, d_cu_up = _shifted(d_cu, d_cu_prev, d_cu_next)
        d_u = vec[5] * d_cu + vec[4] * d_cu_up + vec[6] * d_cu_down
        dpc_ref[:, 0:512] = (d_oc * (vec[4] * u_down + vec[5] * u + vec[6] * u_up)).astype(BF16)
        dpc_ref[:, 512:1024] = (d_u * hid).astype(BF16)
        dpc_ref[:, 1024:1536] = (d_u * gate_c).astype(BF16)

        @pl.when(pl.program_id(0) == 0)
        def _():
            dqv_ref[...] = jnp.zeros_like(dqv_ref)

        vec_grads = list(grads[6:10]) + [d_cu * u_down, d_cu * u, d_cu * u_up]
        for j, gval in enumerate(vec_grads):
            dqv_ref[j:j + 1, :] += jnp.sum(gval, axis=0, keepdims=True)

    return pl.pallas_call(
        body, name="post_bwd", grid=(t // tm,),
        in_specs=[_row(tm, D_MODEL), do_prev, do_next] + _post_specs(tm, t),
        out_specs=[_row(tm, D_RWKV)] * 6 + [_row(tm, D_CONV3), _fixed((8, D_RWKV))],
        out_shape=[_sds((t, D_RWKV))] * 6 + [_sds((t, D_CONV3), BF16), _sds((8, D_RWKV))],
        compiler_params=_cp("arbitrary"),
    )(d_out, d_out, d_out, y_f, y_b, ps, kd_f, kd_b, ps, gate, pc, pc, pc, qvec, ones_blocks)


def _adamw_math(wv, gv, mv, vv):
    m2 = ADAM_B1 * mv + (1.0 - ADAM_B1) * gv
    v2 = ADAM_B2 * vv + (1.0 - ADAM_B2) * (gv * gv)
    m_hat = m2 / (1.0 - ADAM_B1 ** ADAM_STEP)
    v_hat = v2 / (1.0 - ADAM_B2 ** ADAM_STEP)
    return -ADAM_LR * (m_hat / (jnp.sqrt(v_hat) + ADAM_EPS) + ADAM_WD * wv), m2, v2


def _adamw_small(items):
    n = len(items)

    def body(*refs):
        ins, outs = refs[:4 * n], refs[4 * n:]
        for k in range(n):
            w_ref, g_ref, m_ref, v_ref = ins[4 * k:4 * k + 4]
            for o_ref, val in zip(outs[3 * k:3 * k + 3], _adamw_math(w_ref[...], g_ref[...], m_ref[...], v_ref[...])):
                o_ref[...] = val

    flat = [a for item in items for a in item]
    outs = pl.pallas_call(
        body, name="adamw_small", out_shape=[_sds(item[0].shape) for item in items for _ in range(3)],
        compiler_params=_cp())(*flat)
    return [tuple(outs[3 * k:3 * k + 3]) for k in range(n)]


def _adamw(w, g, m, v, name):
    r, c = w.shape[-2:]
    tr = _tile(r, 256, 8)
    if w.ndim == 3:
        spec = pl.BlockSpec((None, tr, c), lambda i: (0, i, 0))
    else:
        spec = pl.BlockSpec((tr, c), lambda i: (i, 0))

    def body(w_ref, g_ref, m_ref, v_ref, d_ref, nm_ref, nv_ref):
        d_ref[...], nm_ref[...], nv_ref[...] = _adamw_math(w_ref[...], g_ref[...], m_ref[...], v_ref[...])

    return pl.pallas_call(
        body, name=name, grid=(r // tr,), in_specs=[spec] * 4, out_specs=[spec] * 3,
        out_shape=[_sds(w.shape)] * 3, compiler_params=_cp("parallel"))(w, g, m, v)


_ANY = pl.BlockSpec(memory_space=pl.ANY)


def _place():
    return lax.axis_index("x"), lax.axis_index("y"), lax.axis_index("c")


def _other_chips(x, y):
    return [(1 - x, y), (x, 1 - y), (1 - x, 1 - y)]


def _remote(src, dst, send_sems, recv_sems, k, to):
    return pltpu.make_async_remote_copy(src_ref=src, dst_ref=dst, send_sem=send_sems.at[k],
                                        recv_sem=recv_sems.at[k], device_id=to, device_id_type=MESH)


def _gather_weights(pack):
    rows, width = pack.shape
    half = rows // 2

    def body(x_ref, out_ref, send_sems, recv_sems):
        x, y, c = _place()
        sibling = (x, y, 1 - c)
        chips = _other_chips(x, y)

        def block(chip, part):
            return out_ref.at[2 * chip[0] + chip[1], pl.ds(part * half, half), :]

        first = [_remote(x_ref.at[pl.ds(c * half, half), :], block((x, y), c), send_sems, recv_sems, j, (*chip, c))
                 for j, chip in enumerate(chips)]
        for cp in first:
            cp.start()
        passed = [_remote(block(chip, c), block(chip, c), send_sems, recv_sems, 3 + j, sibling)
                  for j, chip in enumerate(chips)]
        for j, chip in enumerate(chips):
            _remote(block(chip, c), block(chip, c), send_sems, recv_sems, j, sibling).wait_recv()
            passed[j].start()
        for j, chip in enumerate(chips):
            _remote(block(chip, 1 - c), block(chip, 1 - c), send_sems, recv_sems, 3 + j, sibling).wait_recv()
        for cp in first + passed:
            cp.wait_send()

    return pl.pallas_call(
        body, name="gather_weights", in_specs=[_ANY], out_specs=_ANY,
        out_shape=_sds((N_SHARD, rows, width), pack.dtype),
        scratch_shapes=[pltpu.SemaphoreType.DMA((6,)), pltpu.SemaphoreType.DMA((6,))],
    )(pack)


_HBM = pl.BlockSpec(memory_space=pltpu.HBM)
_SEMS = pl.BlockSpec(memory_space=pltpu.SEMAPHORE)
_DATAFLOW = pltpu.SideEffectType.DATAFLOW_SIDE_EFFECTING


def _fetch_start(pack, after):
    def body(x_ref, land_ref, after_ref, send_sems, recv_sems, x_thru, land_thru, token):
        x, y, c = _place()
        for j, chip in enumerate(_other_chips(x, y)):
            _remote(x_ref, land_ref.at[2 * x + y], send_sems, recv_sems, j, (*chip, c)).start()
        token[...] = jnp.zeros_like(token)

    land = lax.empty((N_SHARD,) + pack.shape, pack.dtype)
    return pl.pallas_call(
        body, name="fetch_ffn_start",
        out_shape=(pltpu.SemaphoreType.DMA((3,)), pltpu.SemaphoreType.DMA((3,)), pltpu.HBM(pack.shape, pack.dtype),
                   pltpu.HBM(land.shape, land.dtype), _sds((8, 128))),
        in_specs=(_HBM, _HBM, _ANY), out_specs=(_SEMS, _SEMS, _HBM, _HBM, pl.BlockSpec(memory_space=pltpu.VMEM)),
        input_output_aliases={0: 2, 1: 3}, compiler_params=pltpu.CompilerParams(has_side_effects=_DATAFLOW),
    )(pltpu.with_memory_space_constraint(pack, pltpu.HBM), pltpu.with_memory_space_constraint(land, pltpu.HBM), after)


def _fetch_wait(send_sems, recv_sems, pack_thru, land_thru, after):
    def body(x_ref, land_ref, send_sems, recv_sems, after_ref, x_dead, got_ref):
        x, y, c = _place()
        for j, chip in enumerate(_other_chips(x, y)):
            cp = _remote(x_ref, land_ref.at[2 * chip[0] + chip[1]], send_sems, recv_sems, j, (*chip, c))
            cp.wait_send()
            cp.wait_recv()

    return pl.pallas_call(
        body, name="fetch_ffn_wait",
        out_shape=(pltpu.HBM(pack_thru.shape, pack_thru.dtype), pltpu.HBM(land_thru.shape, land_thru.dtype)),
        in_specs=(_HBM, _HBM, _SEMS, _SEMS, _ANY), out_specs=(_HBM, _HBM), input_output_aliases={0: 0, 1: 1},
        compiler_params=pltpu.CompilerParams(has_side_effects=_DATAFLOW),
    )(pack_thru, land_thru, send_sems, recv_sems, after)[1]


def _swap_with_sibling(block, name):
    def body(x_ref, out_ref, send_sems, recv_sems):
        x, y, c = _place()
        cp = _remote(x_ref, out_ref, send_sems, recv_sems, 0, (x, y, 1 - c))
        cp.start()
        cp.wait()

    return pl.pallas_call(
        body, name=name, in_specs=[_ANY], out_specs=_ANY, out_shape=_sds(block.shape, block.dtype),
        scratch_shapes=[pltpu.SemaphoreType.DMA((1,)), pltpu.SemaphoreType.DMA((1,))],
    )(block)


def _swap_other_half(packed, tag):
    slots, rows, width = packed.shape
    half = rows // 2

    def body(x_ref, out_ref, send_sems, recv_sems):
        x, y, c = _place()
        cp = _remote(x_ref.at[:, pl.ds((1 - c) * half, half), :], out_ref, send_sems, recv_sems, 0, (x, y, 1 - c))
        cp.start()
        cp.wait()

    return pl.pallas_call(
        body, name="swap_halves_" + tag, in_specs=[_ANY], out_specs=_ANY, out_shape=_sds((slots, half, width)),
        scratch_shapes=[pltpu.SemaphoreType.DMA((1,)), pltpu.SemaphoreType.DMA((1,))],
    )(packed)


def _add_halves(packed, got, c, tag):
    slots, rows, width = packed.shape
    half = rows // 2
    tr = _tile(half, 408, 16)
    per = half // tr
    block = (None, tr, width)

    def body(c_ref, mine_ref, got_ref, sum_ref, sum16_ref):
        acc = mine_ref[...] + got_ref[...]
        sum_ref[...] = acc
        sum16_ref[...] = acc.astype(BF16)

    plain = pl.BlockSpec(block, lambda s, i, c_ref: (s, i, 0))
    grid_spec = pltpu.PrefetchScalarGridSpec(
        num_scalar_prefetch=1, grid=(slots, per),
        in_specs=[pl.BlockSpec(block, lambda s, i, c_ref: (s, c_ref[0] * per + i, 0)), plain],
        out_specs=[plain, plain])
    return pl.pallas_call(
        body, name="add_halves_" + tag, grid_spec=grid_spec,
        out_shape=[_sds((slots, half, width)), _sds((slots, half, width), BF16)],
        compiler_params=_cp("parallel", "parallel"))(c.reshape(1).astype(jnp.int32), packed, got)


def _add_quarters(chip_sum, others, chip, tag):
    _, rows, width = chip_sum.shape
    tr = _tile(rows, 408, 16)

    def body(chip_ref, own_ref, others_ref, o_ref):
        acc = own_ref[...]
        for j in range(3):
            acc = acc + others_ref[j].astype(F32)
        o_ref[...] = acc

    grid_spec = pltpu.PrefetchScalarGridSpec(
        num_scalar_prefetch=1, grid=(rows // tr,),
        in_specs=[pl.BlockSpec((None, tr, width), lambda i, chip_ref: (chip_ref[0], i, 0)),
                  pl.BlockSpec((3, tr, width), lambda i, chip_ref: (0, i, 0))],
        out_specs=pl.BlockSpec((tr, width), lambda i, chip_ref: (i, 0)))
    return pl.pallas_call(
        body, name="add_quarters_" + tag, grid_spec=grid_spec, out_shape=_sds((rows, width)),
        compiler_params=_cp("parallel"))(chip.reshape(1).astype(jnp.int32), chip_sum, others)


def _exchange_start(parts, tag):
    _, rows, width = parts.shape

    def body(x_ref, land_ref, send_sems, recv_sems, x_thru, land_thru, token):
        x, y, c = _place()
        for j, chip in enumerate(_other_chips(x, y)):
            _remote(x_ref.at[2 * chip[0] + chip[1]], land_ref.at[j], send_sems, recv_sems, j, (*chip, c)).start()
        token[...] = jnp.zeros_like(token)

    land = lax.empty((3, rows, width), parts.dtype)
    return pl.pallas_call(
        body, name="exchange_" + tag + "_start",
        out_shape=(pltpu.SemaphoreType.DMA((3,)), pltpu.SemaphoreType.DMA((3,)), pltpu.HBM(parts.shape, parts.dtype),
                   pltpu.HBM(land.shape, land.dtype), _sds((8, 128))),
        in_specs=(_HBM, _HBM), out_specs=(_SEMS, _SEMS, _HBM, _HBM, pl.BlockSpec(memory_space=pltpu.VMEM)),
        input_output_aliases={0: 2, 1: 3}, compiler_params=pltpu.CompilerParams(has_side_effects=_DATAFLOW),
    )(pltpu.with_memory_space_constraint(parts, pltpu.HBM), pltpu.with_memory_space_constraint(land, pltpu.HBM))


def _exchange_wait(send_sems, recv_sems, parts_thru, land_thru, after, tag):
    def body(x_ref, land_ref, send_sems, recv_sems, after_ref, x_dead, got_ref):
        x, y, c = _place()
        for j, chip in enumerate(_other_chips(x, y)):
            cp = _remote(x_ref.at[2 * chip[0] + chip[1]], land_ref.at[j], send_sems, recv_sems, j, (*chip, c))
            cp.wait_send()
            cp.wait_recv()

    return pl.pallas_call(
        body, name="exchange_" + tag + "_wait",
        out_shape=(pltpu.HBM(parts_thru.shape, parts_thru.dtype), pltpu.HBM(land_thru.shape, land_thru.dtype)),
        in_specs=(_HBM, _HBM, _SEMS, _SEMS, _ANY), out_specs=(_HBM, _HBM), input_output_aliases={0: 0, 1: 1},
        compiler_params=pltpu.CompilerParams(has_side_effects=_DATAFLOW),
    )(parts_thru, land_thru, send_sems, recv_sems, after)[1]


def _allreduce_small(vec):
    rows, width = vec.shape
    vmem = pl.BlockSpec(memory_space=pltpu.VMEM)

    def body(x_ref, o_ref, buf_ref, send_sems, recv_sems):
        x, y, c = _place()
        me = 4 * x + 2 * y + c
        buf_ref[me] = x_ref[...]
        copies = []
        for k in range(1, N_DEV):
            peer = (x ^ ((k >> 2) & 1), y ^ ((k >> 1) & 1), c ^ (k & 1))
            copies.append(_remote(x_ref, buf_ref.at[me], send_sems, recv_sems, k - 1, peer))
        for cp in copies:
            cp.start()
        for k in range(1, N_DEV):
            _remote(x_ref, buf_ref.at[me ^ k], send_sems, recv_sems, k - 1, (x, y, c)).wait_recv()
        for cp in copies:
            cp.wait_send()
        total = buf_ref[0]
        for d in range(1, N_DEV):
            total = total + buf_ref[d]
        o_ref[...] = total

    return pl.pallas_call(
        body, name="allreduce_small", in_specs=[vmem], out_specs=vmem, out_shape=_sds((rows, width)),
        scratch_shapes=[pltpu.VMEM((N_DEV, rows, width), F32), pltpu.SemaphoreType.DMA((N_DEV - 1,)),
                        pltpu.SemaphoreType.DMA((N_DEV - 1,))],
    )(vec)


def _rows1024(a):
    return a.reshape(-1, 1024)


def _pad_rows(a, rows):
    return jnp.concatenate([a, jnp.zeros((rows - a.shape[0], a.shape[1]), a.dtype)], axis=0)


_TRANSPOSED = ("w_in", "w_gate", "w_up")
_SMALL_SHARDED = ("w_up_f", "w_up_b", "a_up_f", "a_up_b", "g_up")
_BIG_SHARDED = ("w_in", "w_out", "w_gate", "w_up", "w_down")


def _pack_weight_shards(w):
    conv_bits = lax.bitcast_convert_type(w["conv_w"], BF16).reshape(1, -1)
    conv_row = jnp.concatenate([conv_bits, jnp.zeros((1, 1024 - conv_bits.shape[1]), BF16)], axis=1)

    def rows(name):
        a = w[name].astype(BF16)
        return a.T if name in _TRANSPOSED else _rows1024(a)

    early = _pad_rows(jnp.concatenate([rows(name) for name, _ in _EARLY_ROWS[:-1]] + [conv_row], axis=0), EARLY_R)
    return early, jnp.concatenate([rows(name) for name, _ in _FFN_ROWS], axis=0)


def _split_rows(gathered, layout):
    out, row = {}, 0
    for name, n in layout:
        out[name] = gathered[:, row:row + n]
        row += n
    return out


def _unpack_early(gathered):
    out = _split_rows(gathered, _EARLY_ROWS)
    cols = lambda a, k: jnp.concatenate([a[s].reshape(k, -1) for s in range(N_SHARD)], axis=1)
    conv = lax.bitcast_convert_type(out["conv_w"][:, 0, :768].reshape(N_SHARD, 3, 128, 2), F32)
    full = dict(w_in=out["w_in"].reshape(-1, 1024).T, w_out=out["w_out"].reshape(D_MODEL, D_MODEL),
                conv_w=jnp.concatenate([conv[s] for s in range(N_SHARD)], axis=1))
    full.update({name: cols(out[name], D_GATE if name == "g_up" else D_LORA) for name in _SMALL_SHARDED})
    return full


def _unpack_ffn(gathered):
    out = _split_rows(gathered, _FFN_ROWS)
    return dict(w_gate=out["w_gate"].reshape(-1, 1024).T, w_up=out["w_up"].reshape(-1, 1024).T,
                w_down=out["w_down"].reshape(D_FF, D_MODEL))


def _pack_grads(g, layout, rows):
    col_split = lambda a, s: a[:, s * (a.shape[1] // N_SHARD):(s + 1) * (a.shape[1] // N_SHARD)]
    row_split = lambda a, s: a[s * (a.shape[0] // N_SHARD):(s + 1) * (a.shape[0] // N_SHARD)]
    by_rows = {name: (g[name].T if name in _TRANSPOSED else g[name]) for name, _ in layout if name in _BIG_SHARDED}
    used = sum(n for _, n in layout)
    parts = []
    for s in range(N_SHARD):
        for name, _ in layout:
            if name in _BIG_SHARDED:
                parts.append(row_split(by_rows[name], s))
            elif name in _SMALL_SHARDED:
                parts.append(_rows1024(col_split(g[name], s)))
            else:
                conv = col_split(g["conv_w"], s).reshape(1, -1)
                parts.append(jnp.concatenate([conv, jnp.zeros((1, 1024 - conv.shape[1]), F32)], axis=1))
        if rows > used:
            parts.append(jnp.zeros((rows - used, 1024), F32))
    return jnp.concatenate(parts, axis=0).reshape(N_SHARD, rows, 1024)


def _unpack_grad_shard(pack, layout):
    small_shapes = {name: (D_GATE if name == "g_up" else D_LORA, 128) for name in _SMALL_SHARDED}
    out, row = {}, 0
    for name, n in layout:
        piece = pack[row:row + n]
        if name == "conv_w":
            out[name] = piece[0, :384].reshape(3, 128)
        else:
            out[name] = piece.T if name in _TRANSPOSED else piece.reshape(small_shapes.get(name, piece.shape))
        row += n
    return out


_SMALL_LAYOUT = (("norm1_w", 1024), ("mu_shift", D_SHIFT), ("w0_f", 512), ("w0_b", 512), ("a0_f", 512),
                 ("a0_b", 512), ("k_k", 512), ("k_a_f", 512), ("k_a_b", 512), ("r_k_f", 512), ("r_k_b", 512),
                 ("gn_w", 512), ("gn_b", 512), ("norm2_w", 1024), ("norm_f_w", 1024), ("loss", 1))


def _pack_small(vals):
    rows = []
    for name, n in _SMALL_LAYOUT:
        flat = vals[name].reshape(-1)
        n_rows = -(-n // 1024)
        rows.append(jnp.concatenate([flat, jnp.zeros((n_rows * 1024 - n,), F32)]).reshape(n_rows, 1024))
    return _pad_rows(jnp.concatenate(rows, axis=0), SMALL_ROWS)


def _unpack_small(pack):
    out, row = {}, 0
    for name, n in _SMALL_LAYOUT:
        n_rows = -(-n // 1024)
        out[name] = pack[row:row + n_rows].reshape(-1)[:n]
        row += n_rows
    return out


_WEIGHTS = ("norm1_w", "w_in", "mu_shift", "w_up_f", "w0_f", "w_up_b", "w0_b", "a_up_f", "a0_f", "a_up_b", "a0_b",
            "g_up", "k_k", "k_a_f", "k_a_b", "r_k_f", "r_k_b", "gn_w", "gn_b", "conv_w", "w_out", "norm2_w",
            "w_gate", "w_up", "w_down", "norm_f_w")


def _train_step(x, loss_target, w, m, v):
    batch, seq, _ = x.shape
    t = batch * seq
    tm = _tile(seq, 256, 8)
    xs = x.reshape(t, D_MODEL)
    target = loss_target.reshape(t, D_MODEL)
    vec = lambda name: w[name].reshape(1, -1)

    local = {name: w[name][0] for name, _ in _PACK_ROWS}
    c = lax.axis_index("c")
    chip = 2 * lax.axis_index("x") + lax.axis_index("y")
    early, ffn_pack = _pack_weight_shards(local)
    early_all = lax.dynamic_update_slice(_gather_weights(early), early[None], (chip, 0, 0))
    ffn_send, ffn_recv, ffn_pack, ffn_land, token = _fetch_start(ffn_pack, early_all)
    full = _unpack_early(early_all)
    w_in = full["w_in"]
    w_shift = jnp.concatenate([w_in[:, :D_SHIFT], jnp.zeros((D_MODEL, D_SHIFT_PAD - D_SHIFT), BF16)], axis=1)
    w_conv = w_in[:, D_SHIFT:]
    zeros_lora = jnp.zeros((D_LORA, D_RWKV), F32)
    lora = lambda name: full[name].astype(F32)
    mats = (jnp.concatenate([lora("w_up_f"), zeros_lora]), jnp.concatenate([zeros_lora, lora("a_up_f")]),
            jnp.concatenate([lora("w_up_b"), zeros_lora]), jnp.concatenate([zeros_lora, lora("a_up_b")]),
            jnp.concatenate([lora("g_up"), jnp.zeros((D_GATE_PAD - D_GATE, D_RWKV), F32)]))
    mu = jnp.concatenate([vec("mu_shift"), jnp.zeros((1, D_SHIFT_PAD - D_SHIFT), F32)], axis=1)
    mu = jnp.broadcast_to(mu, (GROUP, D_SHIFT_PAD))
    zero_row = jnp.zeros((1, D_RWKV), F32)
    pvec = jnp.concatenate([vec("k_k"), vec("w0_f"), vec("a0_f"), vec("k_a_f"), vec("w0_b"), vec("a0_b"),
                            vec("k_a_b"), zero_row], axis=0)
    qvec = jnp.concatenate([vec("gn_w"), vec("gn_b"), vec("r_k_f"), vec("r_k_b"), full["conv_w"], zero_row], axis=0)
    ones_blocks = _head_ones()

    h1, h1_t = _rmsnorm_fwd(xs, vec("norm1_w") + token[0, 0], tm, "norm1_fwd")
    p_shift = _matmul(h1, w_shift, mode="nn", name="in_proj_shift")
    pc = _matmul(h1, w_conv, mode="nn", name="in_proj_conv")
    ps = _shift_fwd(p_shift, mu, tm, seq)
    kk, w_f, kd_f, b_f, w_b, kd_b, b_b, gate = _prep_fwd(ps, pvec, mats, ones_blocks, tm)
    dirs = ((w_f, kd_f, b_f), (w_b, kd_b, b_b))
    y_f, y_b, hist_f, hist_b = _scan_fwd(ps, kk, dirs, batch, seq)
    mixed, mixed_t = _post_fwd(y_f, y_b, ps, kd_f, kd_b, gate, pc, qvec, ones_blocks, tm, seq)
    x1 = _matmul(mixed, full["w_out"], mode="nn", name="out_proj", add=xs)
    h2, h2_t = _rmsnorm_fwd(x1, vec("norm2_w"), tm, "norm2_fwd")
    ffn_all = _fetch_wait(ffn_send, ffn_recv, ffn_pack, ffn_land, h2)
    full.update(_unpack_ffn(lax.dynamic_update_slice(ffn_all, ffn_pack[None], (chip, 0, 0))))
    ff_gate, ff_up, act, act_t = _ffn_in(h2, full["w_gate"], full["w_up"])
    d_x2, d_norm_f, loss_part = _ffn_out_loss(act, full["w_down"], x1, w["norm_f_w"].reshape(1, -1), target, tm)

    g = {}
    g["w_down"] = _matmul(act_t, d_x2, mode="nn", name="ffn_down_dw")
    d_gate, d_up = _ffn_in_bwd(d_x2, full["w_down"], ff_gate, ff_up)
    g["w_gate"] = _matmul(h2_t, d_gate, mode="nn", name="ffn_gate_dw")
    g["w_up"] = _matmul(h2_t, d_up, mode="nn", name="ffn_up_dw")
    ffn_grads = _pack_grads(g, _FFN_ROWS, sum(n for _, n in _FFN_ROWS))
    ffn_sum, ffn_sum_bf16 = _add_halves(ffn_grads, _swap_other_half(ffn_grads, "ffn"), c, "ffn")
    ex_send, ex_recv, ffn_sum_bf16, ex_land, ex_token = _exchange_start(ffn_sum_bf16, "ffn")
    d_x1, d_norm2 = _proj_norm_bwd(d_gate, full["w_gate"], d_up, full["w_up"], x1, vec("norm2_w") + ex_token[0, 0],
                                   d_x2, tm, "ffn_in_dx_norm2_bwd")
    d_mixed = _matmul(d_x1, full["w_out"], mode="nt", name="out_proj_dx")
    g["w_out"] = _matmul(mixed_t, d_x1, mode="nn", name="out_proj_dw")
    dy, dr_o, dkdf_o, dkdb_o, dv_o, d_gatev, d_pc, d_qvec = _post_bwd(
        d_mixed, y_f, y_b, ps, kd_f, kd_b, gate, pc, qvec, ones_blocks, tm, seq)
    (dr_f, dw_f, dkd_f, dv_f, dkk_f, db_f), (dr_b, dw_b, dkd_b, dv_b, dkk_b, db_b) = _scan_bwd(
        ps, kk, dirs, dy, hist_f, hist_b, batch, seq)
    cts = [[dr_f, dr_b, dr_o], [dv_f, dv_b, dv_o], [dkk_f, dkk_b], [dw_f], [dkd_f, dkdf_o], [db_f],
           [dw_b], [dkd_b, dkdb_o], [db_b], [d_gatev]]
    q, d_pvec, d_m0, d_m1, d_m2, d_m3, d_m4 = _prep_bwd(ps, pvec, mats, ones_blocks, cts, tm)
    d_pshift, d_mu = _shift_bwd(q, p_shift, mu, tm, seq)
    d_w_shift = _matmul(h1_t, d_pshift, mode="nn", name="in_proj_shift_dw")
    d_w_conv = _matmul(h1_t, d_pc, mode="nn", name="in_proj_conv_dw")
    g["w_in"] = jnp.concatenate([d_w_shift[:, :D_SHIFT], d_w_conv], axis=1)
    d_x, d_norm1 = _proj_norm_bwd(d_pshift, w_shift, d_pc, w_conv, xs, vec("norm1_w"), d_x1, tm,
                                  "in_proj_dx_norm1_bwd")
    g["w_up_f"], g["a_up_f"] = d_m0[:D_LORA], d_m1[D_LORA:]
    g["w_up_b"], g["a_up_b"] = d_m2[:D_LORA], d_m3[D_LORA:]
    g["g_up"] = d_m4[:D_GATE]
    g["conv_w"] = d_qvec[4:7]

    def finish(chip_sum, others, tag, layout):
        eighth = _add_quarters(chip_sum, others, chip, tag)
        other_eighth = _swap_with_sibling(eighth, "swap_eighths_" + tag)
        return _unpack_grad_shard(jnp.concatenate([jnp.where(c == 0, eighth, other_eighth),
                                                   jnp.where(c == 0, other_eighth, eighth)], axis=0), layout)

    as2d = lambda name: (1, w[name].shape[0]) if w[name].ndim == 1 else w[name].shape
    operands = lambda name: tuple(a.reshape(as2d(name)) for a in (w[name], grads[name], m[name], v[name]))

    packed = _pack_grads(g, _EARLY_ROWS, EARLY_R)
    mix_sum, mix_sum_bf16 = _add_halves(packed, _swap_other_half(packed, "mixer"), c, "mixer")
    mx_send, mx_recv, mix_sum_bf16, mx_land, mx_token = _exchange_start(mix_sum_bf16, "mixer")
    grads = finish(ffn_sum, _exchange_wait(ex_send, ex_recv, ffn_sum_bf16, ex_land, mx_token, "ffn"), "ffn", _FFN_ROWS)
    updates = {name: _adamw(*operands(name), "adamw_" + name) for name in _FFN_NAMES}
    mix_others = _exchange_wait(mx_send, mx_recv, mix_sum_bf16, mx_land, updates["w_down"][2], "mixer")
    grads.update(finish(mix_sum, mix_others, "mixer", _EARLY_ROWS))

    small = dict(norm1_w=d_norm1, mu_shift=d_mu[:, :D_SHIFT], w0_f=d_pvec[1], w0_b=d_pvec[4], a0_f=d_pvec[2],
                 a0_b=d_pvec[5], k_k=d_pvec[0], k_a_f=d_pvec[3], k_a_b=d_pvec[6], r_k_f=d_qvec[2], r_k_b=d_qvec[3],
                 gn_w=d_qvec[0], gn_b=d_qvec[1], norm2_w=d_norm2, norm_f_w=d_norm_f, loss=loss_part)
    reduced = _unpack_small(_allreduce_small(_pack_small(small)))
    loss = reduced.pop("loss")[0]
    grads.update(reduced)

    outs = {}
    small = [name for name in _WEIGHTS if name not in _BIG_SHARDED]
    updates.update(zip(small, _adamw_small([operands(name) for name in small])))
    for name in ("w_in", "w_out"):
        updates[name] = _adamw(*operands(name), "adamw_" + name)
    for name in _WEIGHTS:
        shape = w[name].shape
        outs[name] = (grads[name].reshape(shape),) + tuple(a.reshape(shape) for a in updates[name])
    d_x = d_x.reshape(batch, seq, D_MODEL)
    return (loss, d_x) + tuple(outs[name][k] for k in range(4) for name in _WEIGHTS)


def kernel(x, norm1_w, w_in, mu_shift, w_up_f, w0_f, w_up_b, w0_b, a_up_f, a0_f, a_up_b, a0_b, g_up, k_k, k_a_f, k_a_b, r_k_f, r_k_b, gn_w, gn_b, conv_w, w_out, norm2_w, w_gate, w_up, w_down, norm_f_w, loss_target, m_norm1_w, m_w_in, m_mu_shift, m_w_up_f, m_w0_f, m_w_up_b, m_w0_b, m_a_up_f, m_a0_f, m_a_up_b, m_a0_b, m_g_up, m_k_k, m_k_a_f, m_k_a_b, m_r_k_f, m_r_k_b, m_gn_w, m_gn_b, m_conv_w, m_w_out, m_norm2_w, m_w_gate, m_w_up, m_w_down, m_norm_f_w, v_norm1_w, v_w_in, v_mu_shift, v_w_up_f, v_w0_f, v_w_up_b, v_w0_b, v_a_up_f, v_a0_f, v_a_up_b, v_a0_b, v_g_up, v_k_k, v_k_a_f, v_k_a_b, v_r_k_f, v_r_k_b, v_gn_w, v_gn_b, v_conv_w, v_w_out, v_norm2_w, v_w_gate, v_w_up, v_w_down, v_norm_f_w):
    args = locals()
    w = {name: args[name] for name in _WEIGHTS}
    m = {name: args["m_" + name] for name in _WEIGHTS}
    v = {name: args["v_" + name] for name in _WEIGHTS}
    return _train_step(x, loss_target, w, m, v)
```

```python
import functools

import jax
import jax.numpy as jnp
from jax import lax
from jax.experimental import pallas as pl
from jax.experimental.pallas import tpu as pltpu

F32 = jnp.float32
BF16 = jnp.bfloat16
MESH = pl.DeviceIdType.MESH

D_MODEL = 1024
D_RWKV = 512
HEAD = 64
N_PAIR = D_RWKV // (2 * HEAD)
D_LORA = 64
D_GATE = 160
D_GATE_PAD = 384
D_FF = 2816
D_SHIFT = 1824
D_SHIFT_PAD = 2048
D_CONV3 = 1536
LOG_DECAY_SCALE = 0.606531
RMS_EPS = 1e-6
GN_EPS = 64e-5
NORM_EPS = 1e-12
ADAM_LR, ADAM_B1, ADAM_B2, ADAM_EPS, ADAM_WD, ADAM_STEP = 0.001, 0.9, 0.999, 1e-08, 0.01, 10

N_SHARD = 4
N_DEV = 8
V7X_VMEM_LIMIT = 48 * 1024 * 1024
SCAN_CHUNK = 32
GROUP = 8

_PACK_ROWS = (("w_in", 840), ("w_out", 256), ("w_gate", 704), ("w_up", 704), ("w_down", 704),
              ("w_up_f", 8), ("w_up_b", 8), ("a_up_f", 8), ("a_up_b", 8), ("g_up", 20), ("conv_w", 1))
_FFN_NAMES = ("w_gate", "w_up", "w_down")
_EARLY_ROWS = tuple(item for item in _PACK_ROWS if item[0] not in _FFN_NAMES)
_FFN_ROWS = tuple(item for item in _PACK_ROWS if item[0] in _FFN_NAMES)
EARLY_R = 1152
SMALL_ROWS = 24


def _tile(n, cap, mult=128):
    best = None
    t = mult
    while t <= min(n, cap):
        if n % t == 0:
            best = t
        t += mult
    return best or n


def _cp(*sem):
    return pltpu.CompilerParams(dimension_semantics=sem or None, vmem_limit_bytes=V7X_VMEM_LIMIT)


def _sds(shape, dtype=F32):
    return jax.ShapeDtypeStruct(shape, dtype)


def _matmul(a, b, *, mode, name, out_dtype=F32, add=None):
    m, kdim = a.shape
    n = b.shape[1] if mode == "nn" else b.shape[0]
    tm, tn = _tile(m, 512, 8), _tile(n, 1536)
    tk = kdim if kdim <= 3584 else _tile(kdim, 1024)
    nk = kdim // tk
    a_spec = pl.BlockSpec((tm, tk), lambda i, j, k: (i, k))
    if mode == "nn":
        b_spec = pl.BlockSpec((tk, tn), lambda i, j, k: (k, j))
        dims = (((1,), (0,)), ((), ()))
    else:
        b_spec = pl.BlockSpec((tn, tk), lambda i, j, k: (j, k))
        dims = (((1,), (1,)), ((), ()))
    has_add = add is not None

    def body(*refs):
        a_ref, b_ref = refs[0], refs[1]
        add_ref = refs[2] if has_add else None
        o_ref = refs[3] if has_add else refs[2]
        part = lax.dot_general(a_ref[...].astype(BF16), b_ref[...].astype(BF16), dims,
                               preferred_element_type=F32)
        if nk == 1:
            if has_add:
                part = part + add_ref[...]
            o_ref[...] = part.astype(out_dtype)
        else:
            acc_ref = refs[-1]
            k = pl.program_id(2)

            @pl.when(k == 0)
            def _():
                acc_ref[...] = jnp.zeros_like(acc_ref)

            acc_ref[...] += part

            @pl.when(k == nk - 1)
            def _():
                res = acc_ref[...]
                if has_add:
                    res = res + add_ref[...]
                o_ref[...] = res.astype(out_dtype)

    o_spec = pl.BlockSpec((tm, tn), lambda i, j, k: (i, j))
    in_specs = [a_spec, b_spec] + ([o_spec] if has_add else [])
    args = (a, b) + ((add,) if has_add else ())
    return pl.pallas_call(
        body, name=name, grid=(m // tm, n // tn, nk), in_specs=in_specs, out_specs=o_spec,
        out_shape=_sds((m, n), out_dtype),
        scratch_shapes=[pltpu.VMEM((tm, tn), F32)] if nk > 1 else [],
        compiler_params=_cp("parallel", "parallel", "arbitrary"),
    )(*args)


def _row(tm, width):
    return pl.BlockSpec((tm, width), lambda i: (i, 0))


def _col(tm, height):
    return pl.BlockSpec((height, tm), lambda i: (0, i))


def _fixed(shape):
    return pl.BlockSpec(shape, lambda i: tuple(0 for _ in shape))


def _rmsnorm_tile(xv, wv):
    return xv * lax.rsqrt(jnp.mean(xv * xv, axis=-1, keepdims=True) + RMS_EPS) * wv


def _norm_in_proj(x, w, w_shift, w_conv, tm):
    t, d = x.shape
    n_a, n_b = w_shift.shape[1], w_conv.shape[1]

    def body(x_ref, w_ref, wa_ref, wb_ref, ht_ref, pa_ref, pb_ref):
        hv = _rmsnorm_tile(x_ref[...], w_ref[...])
        ht_ref[...] = jnp.transpose(hv).astype(BF16)
        hb = hv.astype(BF16)
        pa_ref[...] = jnp.dot(hb, wa_ref[...], preferred_element_type=F32)
        pb_ref[...] = jnp.dot(hb, wb_ref[...], preferred_element_type=F32)

    return pl.pallas_call(
        body, name="norm1_in_proj", grid=(t // tm,),
        in_specs=[_row(tm, d), _fixed((1, d)), _fixed((d, n_a)), _fixed((d, n_b))],
        out_specs=[_col(tm, d), _row(tm, n_a), _row(tm, n_b)],
        out_shape=[_sds((d, t), BF16), _sds((t, n_a)), _sds((t, n_b))],
        compiler_params=_cp("parallel"))(x, w, w_shift, w_conv)


def _out_proj_norm(mixed, w_out, res, w, tm):
    t, d = res.shape

    def body(m_ref, wo_ref, r_ref, w_ref, x_ref, h_ref, ht_ref):
        xv = r_ref[...] + jnp.dot(m_ref[...], wo_ref[...], preferred_element_type=F32)
        x_ref[...] = xv
        hv = _rmsnorm_tile(xv, w_ref[...])
        h_ref[...] = hv.astype(BF16)
        ht_ref[...] = jnp.transpose(hv).astype(BF16)

    return pl.pallas_call(
        body, name="out_proj_norm2", grid=(t // tm,),
        in_specs=[_row(tm, d), _fixed((d, d)), _row(tm, d), _fixed((1, d))],
        out_specs=[_row(tm, d), _row(tm, d), _col(tm, d)],
        out_shape=[_sds((t, d)), _sds((t, d), BF16), _sds((d, t), BF16)],
        compiler_params=_cp("parallel"))(mixed, w_out, res, w)


def _rms_bwd_math(xv, wv, dyv):
    rstd = lax.rsqrt(jnp.mean(xv * xv, axis=-1, keepdims=True) + RMS_EPS)
    xhat = xv * rstd
    gv = dyv * wv
    dx = rstd * (gv - xhat * jnp.mean(gv * xhat, axis=-1, keepdims=True))
    return dx, jnp.sum(dyv * xhat, axis=0, keepdims=True)


def _proj_norm_bwd(dy_a, w_a, dy_b, w_b, x, w, dres, tm, name):
    t, d = x.shape
    ka, kb = dy_a.shape[1], dy_b.shape[1]
    nt = (((1,), (1,)), ((), ()))

    def body(dya_ref, wa_ref, dyb_ref, wb_ref, x_ref, w_ref, dres_ref, dx_ref, dw_ref):
        d_h = (lax.dot_general(dya_ref[...], wa_ref[...], nt, preferred_element_type=F32)
               + lax.dot_general(dyb_ref[...], wb_ref[...], nt, preferred_element_type=F32))
        dx, dw = _rms_bwd_math(x_ref[...], w_ref[...], d_h)
        dx_ref[...] = dres_ref[...] + dx

        @pl.when(pl.program_id(0) == 0)
        def _():
            dw_ref[...] = jnp.zeros_like(dw_ref)

        dw_ref[...] += dw

    return pl.pallas_call(
        body, name=name, grid=(t // tm,),
        in_specs=[_row(tm, ka), _fixed((d, ka)), _row(tm, kb), _fixed((d, kb)), _row(tm, d), _fixed((1, d)),
                  _row(tm, d)],
        out_specs=[_row(tm, d), _fixed((1, d))],
        out_shape=[_sds((t, d)), _sds((1, d))], compiler_params=_cp("arbitrary"))(dy_a, w_a, dy_b, w_b, x, w, dres)


def _ffn_out_loss(act, w_down, x1, w, target, tm):
    t, d = x1.shape
    f = act.shape[1]

    def body(a_ref, wd_ref, x_ref, w_ref, t_ref, dx_ref, dw_ref, loss_ref):
        xv = x_ref[...] + jnp.dot(a_ref[...], wd_ref[...], preferred_element_type=F32)
        wv = w_ref[...]
        rstd = lax.rsqrt(jnp.mean(xv * xv, axis=-1, keepdims=True) + RMS_EPS)
        err = xv * rstd * wv - t_ref[...]
        dx, dw = _rms_bwd_math(xv, wv, err * (1.0 / d))
        dx_ref[...] = dx

        @pl.when(pl.program_id(0) == 0)
        def _():
            dw_ref[...] = jnp.zeros_like(dw_ref)
            loss_ref[...] = jnp.zeros_like(loss_ref)

        dw_ref[...] += dw
        loss_ref[...] += 0.5 * jnp.sum(jnp.mean(err * err, axis=-1, keepdims=True), axis=0, keepdims=True)

    return pl.pallas_call(
        body, name="ffn_out_loss", grid=(t // tm,),
        in_specs=[_row(tm, f), _fixed((f, d)), _row(tm, d), _fixed((1, d)), _row(tm, d)],
        out_specs=[_row(tm, d), _fixed((1, d)), _fixed((1, 1))],
        out_shape=[_sds((t, d)), _sds((1, d)), _sds((1, 1))], compiler_params=_cp("arbitrary"))(act, w_down, x1, w, target)


def _ffn_in(h, w_gate, w_up):
    t, d = h.shape
    f = w_gate.shape[1]
    tm, tn = _tile(t, 512, 8), _tile(f, 1536)

    def body(h_ref, wg_ref, wu_ref, g_ref, u_ref, a_ref, at_ref):
        hv = h_ref[...]
        gv = jnp.dot(hv, wg_ref[...], preferred_element_type=F32)
        uv = jnp.dot(hv, wu_ref[...], preferred_element_type=F32)
        act = gv * jax.nn.sigmoid(gv) * uv
        g_ref[...] = gv.astype(BF16)
        u_ref[...] = uv.astype(BF16)
        a_ref[...] = act.astype(BF16)
        at_ref[...] = jnp.transpose(act).astype(BF16)

    w_spec = pl.BlockSpec((d, tn), lambda i, j: (0, j))
    o_spec = pl.BlockSpec((tm, tn), lambda i, j: (i, j))
    return pl.pallas_call(
        body, name="ffn_in", grid=(t // tm, f // tn),
        in_specs=[pl.BlockSpec((tm, d), lambda i, j: (i, 0)), w_spec, w_spec],
        out_specs=[o_spec, o_spec, o_spec, pl.BlockSpec((tn, tm), lambda i, j: (j, i))],
        out_shape=[_sds((t, f), BF16)] * 3 + [_sds((f, t), BF16)],
        compiler_params=_cp("parallel", "parallel"))(h, w_gate, w_up)


def _ffn_in_bwd(d_out, w_down, gate, up):
    t, d = d_out.shape
    f = w_down.shape[0]
    tm, tn = _tile(t, 512, 8), _tile(f, 1536)

    def body(do_ref, w_ref, g_ref, u_ref, dg_ref, du_ref):
        dv = lax.dot_general(do_ref[...].astype(BF16), w_ref[...], (((1,), (1,)), ((), ())),
                             preferred_element_type=F32)
        gv, uv = g_ref[...].astype(F32), u_ref[...].astype(F32)
        sg = jax.nn.sigmoid(gv)
        du_ref[...] = (dv * gv * sg).astype(BF16)
        dg_ref[...] = (dv * uv * (sg * (1.0 + gv * (1.0 - sg)))).astype(BF16)

    tile = pl.BlockSpec((tm, tn), lambda i, j: (i, j))
    return pl.pallas_call(
        body, name="ffn_in_bwd", grid=(t // tm, f // tn),
        in_specs=[pl.BlockSpec((tm, d), lambda i, j: (i, 0)), pl.BlockSpec((tn, d), lambda i, j: (j, 0)), tile, tile],
        out_specs=[tile, tile], out_shape=[_sds((t, f), BF16)] * 2,
        compiler_params=_cp("parallel", "parallel"))(d_out, w_down, gate, up)


def _halo_specs(tm, width, rows_total):
    per = tm // GROUP
    last = rows_total // GROUP - 1
    prev = pl.BlockSpec((GROUP, width), lambda i: (jnp.maximum(i * per - 1, 0), 0))
    nxt = pl.BlockSpec((GROUP, width), lambda i: (jnp.minimum((i + 1) * per, last), 0))
    return prev, nxt


def _edge_flags(tm, seq):
    i = pl.program_id(0)
    has_prev = jnp.where((i * tm) % seq == 0, 0.0, 1.0).astype(F32)
    has_next = jnp.where(((i + 1) * tm) % seq == 0, 0.0, 1.0).astype(F32)
    return has_prev, has_next


def _shifted(xv, prev_row, next_row):
    tm = xv.shape[0]
    row = lax.broadcasted_iota(jnp.int32, xv.shape, 0)
    down = jnp.where(row == 0, prev_row, pltpu.roll(xv, 1, axis=0))
    up = jnp.where(row == tm - 1, next_row, pltpu.roll(xv, tm - 1, axis=0))
    return down, up


def _shift_fwd(p, mu, tm, seq):
    t, w = p.shape
    prev_spec, next_spec = _halo_specs(tm, w, t)

    def body(p_ref, hp_ref, hn_ref, mu_ref, o_ref):
        has_prev, has_next = _edge_flags(tm, seq)
        xv = p_ref[...]
        down, up = _shifted(xv, hp_ref[GROUP - 1:GROUP, :] * has_prev, hn_ref[0:1, :] * has_next)
        o_ref[...] = xv + mu_ref[0:1, :] * (0.5 * (down + up) - xv)

    return pl.pallas_call(
        body, name="shift_fwd", grid=(t // tm,),
        in_specs=[_row(tm, w), prev_spec, next_spec, _fixed((GROUP, w))], out_specs=_row(tm, w),
        out_shape=_sds((t, w)), compiler_params=_cp("parallel"))(p, p, p, mu)


def _shift_bwd(q, p, mu, tm, seq):
    t, w = p.shape
    prev_spec, next_spec = _halo_specs(tm, w, t)

    def body(q_ref, qp_ref, qn_ref, p_ref, pp_ref, pn_ref, mu_ref, dp_ref, dmu_ref):
        has_prev, has_next = _edge_flags(tm, seq)
        muv = mu_ref[0:1, :]
        qv = q_ref[...]
        mq = muv * qv
        mq_down, mq_up = _shifted(mq, muv * qp_ref[GROUP - 1:GROUP, :] * has_prev,
                                  muv * qn_ref[0:1, :] * has_next)
        dp_ref[...] = (qv - mq + 0.5 * (mq_down + mq_up)).astype(BF16)
        pv = p_ref[...]
        p_down, p_up = _shifted(pv, pp_ref[GROUP - 1:GROUP, :] * has_prev, pn_ref[0:1, :] * has_next)

        @pl.when(pl.program_id(0) == 0)
        def _():
            dmu_ref[...] = jnp.zeros_like(dmu_ref)

        dmu_ref[...] += jnp.sum(qv * (0.5 * (p_down + p_up) - pv), axis=0, keepdims=True)

    return pl.pallas_call(
        body, name="shift_bwd", grid=(t // tm,),
        in_specs=[_row(tm, w), prev_spec, next_spec, _row(tm, w), prev_spec, next_spec, _fixed((GROUP, w))],
        out_specs=[_row(tm, w), _fixed((1, w))],
        out_shape=[_sds((t, w), BF16), _sds((1, w))], compiler_params=_cp("arbitrary"))(q, q, q, p, p, p, mu)


@jax.custom_vjp
def _bdot(a, b):
    return jnp.dot(a.astype(BF16), b.astype(BF16), preferred_element_type=F32)


def _bdot_fwd(a, b):
    return _bdot(a, b), (a, b)


def _bdot_bwd(res, g):
    a, b = res
    gb = g.astype(BF16)
    da = lax.dot_general(gb, b.astype(BF16), (((1,), (1,)), ((), ())), preferred_element_type=F32)
    db = lax.dot_general(a.astype(BF16), gb, (((0,), (0,)), ((), ())), preferred_element_type=F32)
    return da, db


_bdot.defvjp(_bdot_fwd, _bdot_bwd)


def _seg_raw(x, ones_blocks):
    hi = x.astype(BF16)
    lo = (x - hi.astype(F32)).astype(BF16)
    return (jnp.dot(hi, ones_blocks, preferred_element_type=F32)
            + jnp.dot(lo, ones_blocks, preferred_element_type=F32))


@jax.custom_vjp
def _seg(x, ones_blocks):
    return _seg_raw(x, ones_blocks)


def _seg_fwd(x, ones_blocks):
    return _seg_raw(x, ones_blocks), ones_blocks


def _seg_bwd(ones_blocks, g):
    return _seg_raw(g, ones_blocks), jnp.zeros_like(ones_blocks)


_seg.defvjp(_seg_fwd, _seg_bwd)


def _head_ones():
    h = jnp.arange(D_RWKV) // HEAD
    return (h[:, None] == h[None, :]).astype(BF16)


def _prep_math(ps, k_k, w0_f, a0_f, k_a_f, w0_b, a0_b, k_a_b, wup_f, aup_f, wup_b, aup_b, gup, ones_blocks):
    r = ps[:, 0:512]
    k = ps[:, 512:1024]
    v = ps[:, 1024:1536]
    xwa = ps[:, 1536:1664]
    xg = ps[:, 1664:D_SHIFT_PAD]
    kk_raw = k * k_k
    norm = jnp.sqrt(_seg(kk_raw * kk_raw, ones_blocks))
    kk = kk_raw / jnp.maximum(norm, NORM_EPS)
    t_xwa = jnp.tanh(xwa)
    outs = [r, v, kk]
    for w0, a0, k_a, wup, aup in ((w0_f, a0_f, k_a_f, wup_f, aup_f), (w0_b, a0_b, k_a_b, wup_b, aup_b)):
        decay = jnp.exp(-LOG_DECAY_SCALE * jax.nn.sigmoid(w0 + _bdot(t_xwa, wup)))
        rate = jax.nn.sigmoid(a0 + _bdot(xwa, aup))
        outs += [decay, k * (1.0 + (rate - 1.0) * k_a), kk * rate]
    outs.append(_bdot(jax.nn.sigmoid(xg), gup))
    return tuple(outs)


def _prep_args(tm, ps_ref, pv_ref, mat_refs, ones_ref):
    vecs = [jnp.broadcast_to(pv_ref[j:j + 1, :], (tm, D_RWKV)) for j in range(7)]
    return [ps_ref[...]] + vecs + [m[...] for m in mat_refs] + [ones_ref[...]]


_PREP_MAT_SHAPES = ((128, D_RWKV),) * 4 + ((D_GATE_PAD, D_RWKV),)


def _prep_fwd(ps, pvec, mats, ones_blocks, tm):
    t = ps.shape[0]

    def body(ps_ref, pv_ref, m0, m1, m2, m3, m4, ones_ref, *out_refs):
        outs = _prep_math(*_prep_args(tm, ps_ref, pv_ref, (m0, m1, m2, m3, m4), ones_ref))
        for o_ref, val in zip(out_refs, outs[2:]):
            o_ref[...] = val

    return pl.pallas_call(
        body, name="prep_fwd", grid=(t // tm,),
        in_specs=[_row(tm, D_SHIFT_PAD), _fixed((8, D_RWKV))] + [_fixed(s) for s in _PREP_MAT_SHAPES]
        + [_fixed((D_RWKV, D_RWKV))],
        out_specs=[_row(tm, D_RWKV)] * 8, out_shape=[_sds((t, D_RWKV))] * 8,
        compiler_params=_cp("parallel"))(ps, pvec, *mats, ones_blocks)


def _prep_bwd(ps, pvec, mats, ones_blocks, cts, tm):
    t = ps.shape[0]
    counts = [len(c) for c in cts]
    flat = [a for c in cts for a in c]

    def body(ps_ref, pv_ref, m0, m1, m2, m3, m4, ones_ref, *refs):
        ct_refs = refs[:len(flat)]
        q_ref, dpv_ref = refs[len(flat)], refs[len(flat) + 1]
        dmat_refs = refs[len(flat) + 2:]
        args = _prep_args(tm, ps_ref, pv_ref, (m0, m1, m2, m3, m4), ones_ref)
        _, vjp = jax.vjp(lambda *a: _prep_math(*a, args[-1]), *args[:-1])
        ct_vals, pos = [], 0
        for n in counts:
            val = ct_refs[pos][...]
            for extra in ct_refs[pos + 1:pos + n]:
                val = val + extra[...]
            ct_vals.append(val)
            pos += n
        grads = vjp(tuple(ct_vals))
        q_ref[...] = grads[0]

        @pl.when(pl.program_id(0) == 0)
        def _():
            dpv_ref[...] = jnp.zeros_like(dpv_ref)
            for d_ref in dmat_refs:
                d_ref[...] = jnp.zeros_like(d_ref)

        for j in range(7):
            dpv_ref[j:j + 1, :] += jnp.sum(grads[1 + j], axis=0, keepdims=True)
        for d_ref, gm in zip(dmat_refs, grads[8:13]):
            d_ref[...] += gm

    return pl.pallas_call(
        body, name="prep_bwd", grid=(t // tm,),
        in_specs=[_row(tm, D_SHIFT_PAD), _fixed((8, D_RWKV))] + [_fixed(s) for s in _PREP_MAT_SHAPES]
        + [_fixed((D_RWKV, D_RWKV))] + [_row(tm, D_RWKV)] * len(flat),
        out_specs=[_row(tm, D_SHIFT_PAD), _fixed((8, D_RWKV))] + [_fixed(s) for s in _PREP_MAT_SHAPES],
        out_shape=[_sds((t, D_SHIFT_PAD)), _sds((8, D_RWKV))] + [_sds(s) for s in _PREP_MAT_SHAPES],
        compiler_params=_cp("arbitrary"))(ps, pvec, *mats, ones_blocks, *flat)


def _pair_ones():
    h = jnp.arange(2 * HEAD) // HEAD
    block = (h[:, None] == h[None, :]).astype(BF16)
    return jnp.concatenate([block, block], axis=0)


def _diag_mask():
    lane = lax.broadcasted_iota(jnp.int32, (HEAD, 2 * HEAD), 1)
    sub = lax.broadcasted_iota(jnp.int32, (HEAD, 2 * HEAD), 0)
    return jnp.where((lane & (HEAD - 1)) == sub, 1.0, 0.0).astype(F32)


def _to_row(cols, dmask):
    return jnp.sum(cols * dmask, axis=0, keepdims=True)


def _seg_many(exact, rounded, ones_pair):
    out_exact, out_rounded = [], []
    if exact:
        parts = []
        for x in exact:
            hi = x.astype(BF16)
            parts.append(jnp.concatenate([hi, (x - hi.astype(F32)).astype(BF16)], axis=1))
        res = jnp.dot(jnp.concatenate(parts, axis=0), ones_pair, preferred_element_type=F32)
        out_exact = [res[HEAD * c:HEAD * (c + 1)] for c in range(len(exact))]
    if rounded:
        res = jnp.dot(jnp.concatenate([x.astype(BF16) for x in rounded], axis=0), ones_pair[0:2 * HEAD],
                      preferred_element_type=F32)
        out_rounded = [res[HEAD * c:HEAD * (c + 1)] for c in range(len(rounded))]
    return out_exact, out_rounded


N_CHAIN = 2 * N_PAIR


def _chain(c):
    d, p = divmod(c, N_PAIR)
    return d, slice(2 * HEAD * p, 2 * HEAD * (p + 1))


def _scan_specs(n_chunks, col_blocks, fwd_chunk, bwd_chunk):
    def spec(chunk_of, col):
        return pl.BlockSpec((SCAN_CHUNK, D_RWKV), lambda b, g: (b * n_chunks + chunk_of(g), col))
    return [spec(fwd_chunk, c) for c in col_blocks] + [spec(bwd_chunk, c) for c in col_blocks]


def _scan_fwd(ps, kk, dirs, batch, seq):
    t = batch * seq
    n = seq // SCAN_CHUNK
    groups = SCAN_CHUNK // GROUP
    up = lambda g: g
    down = lambda g: n - 1 - g
    col_blocks = (0, 2, 0, 0, 0, 0)

    def body(*refs):
        dir_refs = (refs[0:6], refs[6:12])
        ones_ref = refs[12]
        y_refs, hist_refs, st_ref = refs[13:15], refs[15:17], refs[17]

        @pl.when(pl.program_id(1) == 0)
        def _():
            st_ref[...] = jnp.zeros_like(st_ref)

        ones_pair = ones_ref[...]
        dmask = _diag_mask()
        dmask_b = dmask.astype(BF16)
        sub8 = lax.broadcasted_iota(jnp.int32, (GROUP, 2 * HEAD), 0)

        def group(gi, carry):
            off = (pl.multiple_of(gi * GROUP, GROUP), pl.multiple_of((groups - 1 - gi) * GROUP, GROUP))
            loaded = [tuple(ref[pl.ds(off[d], GROUP), :] for ref in dir_refs[d]) for d in range(2)]
            states = list(carry)
            y_acc = [jnp.zeros((GROUP, 2 * HEAD), F32) for _ in range(N_CHAIN)]
            for step in range(GROUP):
                rows, idx = [], []
                for c in range(N_CHAIN):
                    d, lanes = _chain(c)
                    i = step if d == 0 else GROUP - 1 - step
                    idx.append(i)
                    rows.append(tuple(x8[i:i + 1, lanes] for x8 in loaded[d]))
                    hist_refs[d][c % N_PAIR, gi * GROUP + step] = states[c]
                _, v_cols = _seg_many([], [dmask_b * rows[c][1].astype(BF16) for c in range(N_CHAIN)], ones_pair)
                sas, _ = _seg_many([states[c] * rows[c][2] for c in range(N_CHAIN)], [], ones_pair)
                for c in range(N_CHAIN):
                    _, _, _, w_row, kd_row, b_row = rows[c]
                    states[c] = states[c] * w_row - sas[c] * b_row + v_cols[c] * kd_row
                _, ys = _seg_many([], [states[c] * rows[c][0] for c in range(N_CHAIN)], ones_pair)
                for c in range(N_CHAIN):
                    y_acc[c] = jnp.where(sub8 == idx[c], _to_row(ys[c], dmask), y_acc[c])
            for c in range(N_CHAIN):
                d, lanes = _chain(c)
                y_refs[d][pl.ds(off[d], GROUP), lanes] = y_acc[c]
            return tuple(states)

        final = lax.fori_loop(0, groups, group, tuple(st_ref[c] for c in range(N_CHAIN)))
        for c in range(N_CHAIN):
            st_ref[c] = final[c]
            hist_refs[c // N_PAIR][c % N_PAIR, SCAN_CHUNK] = final[c]

    y_spec_f = pl.BlockSpec((SCAN_CHUNK, D_RWKV), lambda b, g: (b * n + up(g), 0))
    y_spec_b = pl.BlockSpec((SCAN_CHUNK, D_RWKV), lambda b, g: (b * n + down(g), 0))
    hist_shape = (batch, n, N_PAIR, SCAN_CHUNK + 1, HEAD, 2 * HEAD)
    hist_block = (None, None, N_PAIR, SCAN_CHUNK + 1, HEAD, 2 * HEAD)
    hist_spec_f = pl.BlockSpec(hist_block, lambda b, g: (b, up(g), 0, 0, 0, 0))
    hist_spec_b = pl.BlockSpec(hist_block, lambda b, g: (b, down(g), 0, 0, 0, 0))
    ones_spec = pl.BlockSpec((4 * HEAD, 2 * HEAD), lambda b, g: (0, 0))
    (wf, kdf, bf), (wb, kdb, bb) = dirs
    return pl.pallas_call(
        body, name="wkv_fwd", grid=(batch, n),
        in_specs=_scan_specs(n, col_blocks, up, down) + [ones_spec],
        out_specs=[y_spec_f, y_spec_b, hist_spec_f, hist_spec_b],
        out_shape=[_sds((t, D_RWKV)), _sds((t, D_RWKV)), _sds(hist_shape), _sds(hist_shape)],
        scratch_shapes=[pltpu.VMEM((N_CHAIN, HEAD, 2 * HEAD), F32)],
        compiler_params=_cp("parallel", "arbitrary"),
    )(ps, ps, kk, wf, kdf, bf, ps, ps, kk, wb, kdb, bb, _pair_ones())


def _scan_bwd(ps, kk, dirs, dy, hist_f, hist_b, batch, seq):
    t = batch * seq
    n = seq // SCAN_CHUNK
    groups = SCAN_CHUNK // GROUP
    fwd_chunk = lambda g: n - 1 - g
    bwd_chunk = lambda g: g
    col_blocks = (0, 2, 0, 0, 0, 0, 0)

    def undo_group(dir_refs, out_refs, hist_refs, gi, d_states, ones_pair, dmask, sub8):
        d_states = list(d_states)
        loaded, blocks = [], []
        for d in range(2):
            blk = groups - 1 - gi if d == 0 else gi
            blocks.append(pl.ds(pl.multiple_of(blk * GROUP, GROUP), GROUP))
            r8, v8, kk8, w8, kd8, b8, dy8 = (ref[blocks[d], :] for ref in dir_refs[d])
            loaded.append((r8, v8, kk8, w8, kd8, -b8, dy8))
        acc = [[jnp.zeros((GROUP, 2 * HEAD), F32) for _ in range(6)] for _ in range(N_CHAIN)]
        for step in range(GROUP):
            rows, idx, before, after = [], [], [], []
            for c in range(N_CHAIN):
                d, lanes = _chain(c)
                i = GROUP - 1 - step if d == 0 else step
                q = (groups - 1 - gi) * GROUP + i if d == 0 else SCAN_CHUNK - 1 - (gi * GROUP + i)
                idx.append(i)
                rows.append(tuple(x8[i:i + 1, lanes] for x8 in loaded[d]))
                before.append(hist_refs[d][c % N_PAIR, q])
                after.append(hist_refs[d][c % N_PAIR, q + 1])
            _, cols = _seg_many([], [dmask.astype(BF16) * rows[c][j].astype(BF16) for c in range(N_CHAIN) for j in (1, 6)],
                                ones_pair)
            v_cols, dy_cols = cols[0::2], cols[1::2]
            d_now = [d_states[c] + dy_cols[c] * rows[c][0] for c in range(N_CHAIN)]
            d_sas, _ = _seg_many([d_now[c] * rows[c][5] for c in range(N_CHAIN)], [], ones_pair)
            _, others = _seg_many(
                [], [x for c in range(N_CHAIN) for x in (before[c] * rows[c][2], d_now[c] * rows[c][4])], ones_pair)
            for c in range(N_CHAIN):
                sa, d_sa, dv_cols = others[2 * c], d_sas[c], others[2 * c + 1]
                rows_out = (
                    jnp.sum(after[c] * dy_cols[c], axis=0, keepdims=True),
                    jnp.sum(d_now[c] * before[c], axis=0, keepdims=True),
                    jnp.sum(d_now[c] * v_cols[c], axis=0, keepdims=True),
                    _to_row(dv_cols, dmask),
                    jnp.sum(before[c] * d_sa, axis=0, keepdims=True),
                    -jnp.sum(d_now[c] * sa, axis=0, keepdims=True),
                )
                acc[c] = [jnp.where(sub8 == idx[c], val, a) for val, a in zip(rows_out, acc[c])]
                d_states[c] = d_now[c] * rows[c][3] + d_sa * rows[c][2]
        for c in range(N_CHAIN):
            d, lanes = _chain(c)
            for o_ref, val in zip(out_refs[d], acc[c]):
                o_ref[blocks[d], lanes] = val
        return tuple(d_states)

    def body(*refs):
        dir_refs = (refs[0:7], refs[7:14])
        hist_refs, ones_ref = refs[14:16], refs[16]
        out_refs = (refs[17:23], refs[23:29])
        dst_ref = refs[29]

        @pl.when(pl.program_id(1) == 0)
        def _():
            dst_ref[...] = jnp.zeros_like(dst_ref)

        ones_pair = ones_ref[...]
        dmask = _diag_mask()
        sub8 = lax.broadcasted_iota(jnp.int32, (GROUP, 2 * HEAD), 0)

        def group(gi, carry):
            return undo_group(dir_refs, out_refs, hist_refs, gi, carry, ones_pair, dmask, sub8)

        final = lax.fori_loop(0, groups, group, tuple(dst_ref[c] for c in range(N_CHAIN)))
        for c in range(N_CHAIN):
            dst_ref[c] = final[c]

    blk = (SCAN_CHUNK, D_RWKV)
    out_f = pl.BlockSpec(blk, lambda b, g: (b * n + fwd_chunk(g), 0))
    out_b = pl.BlockSpec(blk, lambda b, g: (b * n + bwd_chunk(g), 0))
    hist_block = (None, None, N_PAIR, SCAN_CHUNK + 1, HEAD, 2 * HEAD)
    hist_spec_f = pl.BlockSpec(hist_block, lambda b, g: (b, fwd_chunk(g), 0, 0, 0, 0))
    hist_spec_b = pl.BlockSpec(hist_block, lambda b, g: (b, bwd_chunk(g), 0, 0, 0, 0))
    ones_spec = pl.BlockSpec((4 * HEAD, 2 * HEAD), lambda b, g: (0, 0))
    (wf, kdf, bf), (wb, kdb, bb) = dirs
    outs = pl.pallas_call(
        body, name="wkv_bwd", grid=(batch, n),
        in_specs=_scan_specs(n, col_blocks, fwd_chunk, bwd_chunk) + [hist_spec_f, hist_spec_b, ones_spec],
        out_specs=[out_f] * 6 + [out_b] * 6,
        out_shape=[_sds((t, D_RWKV))] * 12,
        scratch_shapes=[pltpu.VMEM((N_CHAIN, HEAD, 2 * HEAD), F32)],
        compiler_params=_cp("parallel", "arbitrary"),
    )(ps, ps, kk, wf, kdf, bf, dy, ps, ps, kk, wb, kdb, bb, dy, hist_f, hist_b, _pair_ones())
    return outs[0:6], outs[6:12]


def _post_math(y, r, kd_f, kd_b, v, gate, gn_w, gn_b, rk_f, rk_b, ones_blocks):
    mean = _seg(y, ones_blocks) * (1.0 / HEAD)
    yc = y - mean
    var = _seg(yc * yc, ones_blocks) * (1.0 / HEAD)
    yn = yc * lax.rsqrt(var + GN_EPS) * gn_w + gn_b
    bonus = _seg(r * kd_f * rk_f, ones_blocks) * v + _seg(r * kd_b * rk_b, ones_blocks) * v
    return (yn + bonus) * gate


def _conv_parts(pc, halo_prev, halo_next, has_prev, has_next):
    gate_b, gate_c, hid = pc[:, 0:512], pc[:, 512:1024], pc[:, 1024:1536]
    u = gate_c * hid
    u_prev_row = halo_prev[GROUP - 1:GROUP, 512:1024] * halo_prev[GROUP - 1:GROUP, 1024:1536] * has_prev
    u_next_row = halo_next[0:1, 512:1024] * halo_next[0:1, 1024:1536] * has_next
    u_down, u_up = _shifted(u, u_prev_row, u_next_row)
    return gate_b, gate_c, hid, u, u_down, u_up


def _post_specs(tm, t):
    pc_prev, pc_next = _halo_specs(tm, D_CONV3, t)
    col = lambda c: pl.BlockSpec((tm, D_RWKV), lambda i: (i, c))
    return ([col(0), col(0), col(0), col(0), col(0), col(2), col(0), _row(tm, D_CONV3), pc_prev, pc_next,
             _fixed((8, D_RWKV)), _fixed((D_RWKV, D_RWKV))])


def _post_fwd(y_f, y_b, ps, kd_f, kd_b, gate, pc, qvec, ones_blocks, tm, seq):
    t = ps.shape[0]

    def body(yf_ref, yb_ref, r_ref, kdf_ref, kdb_ref, v_ref, g_ref, pc_ref, hp_ref, hn_ref, qv_ref, ones_ref,
             o_ref, ot_ref):
        has_prev, has_next = _edge_flags(tm, seq)
        vec = [jnp.broadcast_to(qv_ref[j:j + 1, :], (tm, D_RWKV)) for j in range(7)]
        o_rwkv = _post_math(yf_ref[...] + yb_ref[...], r_ref[...], kdf_ref[...], kdb_ref[...], v_ref[...],
                            g_ref[...], vec[0], vec[1], vec[2], vec[3], ones_ref[...])
        gate_b, _, _, u, u_down, u_up = _conv_parts(pc_ref[...], hp_ref[...], hn_ref[...], has_prev, has_next)
        o_conv = gate_b * (vec[4] * u_down + vec[5] * u + vec[6] * u_up)
        for half, val in enumerate((o_rwkv, o_conv)):
            o_ref[:, D_RWKV * half:D_RWKV * (half + 1)] = val.astype(BF16)
            ot_ref[D_RWKV * half:D_RWKV * (half + 1), :] = jnp.transpose(val).astype(BF16)

    return pl.pallas_call(
        body, name="post_fwd", grid=(t // tm,), in_specs=_post_specs(tm, t),
        out_specs=[_row(tm, D_MODEL), _col(tm, D_MODEL)],
        out_shape=[_sds((t, D_MODEL), BF16), _sds((D_MODEL, t), BF16)], compiler_params=_cp("parallel"),
    )(y_f, y_b, ps, kd_f, kd_b, ps, gate, pc, pc, pc, qvec, ones_blocks)


def _post_bwd(d_out, y_f, y_b, ps, kd_f, kd_b, gate, pc, qvec, ones_blocks, tm, seq):
    t = ps.shape[0]
    do_prev, do_next = _halo_specs(tm, D_MODEL, t)

    def body(do_ref, dop_ref, don_ref, yf_ref, yb_ref, r_ref, kdf_ref, kdb_ref, v_ref, g_ref, pc_ref, hp_ref,
             hn_ref, qv_ref, ones_ref, dy_ref, dr_ref, dkdf_ref, dkdb_ref, dv_ref, dg_ref, dpc_ref, dqv_ref):
        has_prev, has_next = _edge_flags(tm, seq)
        vec = [jnp.broadcast_to(qv_ref[j:j + 1, :], (tm, D_RWKV)) for j in range(7)]
        ones_v = ones_ref[...]
        args = (yf_ref[...] + yb_ref[...], r_ref[...], kdf_ref[...], kdb_ref[...], v_ref[...], g_ref[...],
                vec[0], vec[1], vec[2], vec[3])
        _, vjp = jax.vjp(lambda *a: _post_math(*a, ones_v), *args)
        grads = vjp(do_ref[:, 0:D_RWKV])
        for o_ref, gval in zip((dy_ref, dr_ref, dkdf_ref, dkdb_ref, dv_ref, dg_ref), grads[0:6]):
            o_ref[...] = gval

        hp, hn = hp_ref[...], hn_ref[...]
        gate_b, gate_c, hid, u, u_down, u_up = _conv_parts(pc_ref[...], hp, hn, has_prev, has_next)
        d_oc = do_ref[:, D_RWKV:2 * D_RWKV]
        d_cu = d_oc * gate_b
        d_cu_prev = dop_ref[GROUP - 1:GROUP, D_RWKV:2 * D_RWKV] * hp[GROUP - 1:GROUP, 0:512] * has_prev
        d_cu_next = don_ref[0:1, D_RWKV:2 * D_RWKV] * hn[0:1, 0:512] * has_next
        d_cu_down, d_cu_up = _shifted(d_cu, d_cu_prev, d_cu_next)
        d_u = vec[5] * d_cu + vec[4] * d_cu_up + vec[6] * d_cu_down
        dpc_ref[:, 0:512] = (d_oc * (vec[4] * u_down + vec[5] * u + vec[6] * u_up)).astype(BF16)
        dpc_ref[:, 512:1024] = (d_u * hid).astype(BF16)
        dpc_ref[:, 1024:1536] = (d_u * gate_c).astype(BF16)

        @pl.when(pl.program_id(0) == 0)
        def _():
            dqv_ref[...] = jnp.zeros_like(dqv_ref)

        vec_grads = list(grads[6:10]) + [d_cu * u_down, d_cu * u, d_cu * u_up]
        for j, gval in enumerate(vec_grads):
            dqv_ref[j:j + 1, :] += jnp.sum(gval, axis=0, keepdims=True)

    return pl.pallas_call(
        body, name="post_bwd", grid=(t // tm,),
        in_specs=[_row(tm, D_MODEL), do_prev, do_next] + _post_specs(tm, t),
        out_specs=[_row(tm, D_RWKV)] * 6 + [_row(tm, D_CONV3), _fixed((8, D_RWKV))],
        out_shape=[_sds((t, D_RWKV))] * 6 + [_sds((t, D_CONV3), BF16), _sds((8, D_RWKV))],
        compiler_params=_cp("arbitrary"),
    )(d_out, d_out, d_out, y_f, y_b, ps, kd_f, kd_b, ps, gate, pc, pc, pc, qvec, ones_blocks)


def _adamw_math(wv, gv, mv, vv):
    m2 = ADAM_B1 * mv + (1.0 - ADAM_B1) * gv
    v2 = ADAM_B2 * vv + (1.0 - ADAM_B2) * (gv * gv)
    m_hat = m2 / (1.0 - ADAM_B1 ** ADAM_STEP)
    v_hat = v2 / (1.0 - ADAM_B2 ** ADAM_STEP)
    return -ADAM_LR * (m_hat / (jnp.sqrt(v_hat) + ADAM_EPS) + ADAM_WD * wv), m2, v2


def _adamw_small(items):
    n = len(items)

    def body(*refs):
        ins, outs = refs[:4 * n], refs[4 * n:]
        for k in range(n):
            w_ref, g_ref, m_ref, v_ref = ins[4 * k:4 * k + 4]
            for o_ref, val in zip(outs[3 * k:3 * k + 3], _adamw_math(w_ref[...], g_ref[...], m_ref[...], v_ref[...])):
                o_ref[...] = val

    flat = [a for item in items for a in item]
    outs = pl.pallas_call(
        body, name="adamw_small", out_shape=[_sds(item[0].shape) for item in items for _ in range(3)],
        compiler_params=_cp())(*flat)
    return [tuple(outs[3 * k:3 * k + 3]) for k in range(n)]


def _adamw(w, g, m, v, name):
    r, c = w.shape[-2:]
    tr = _tile(r, 256, 8)
    if w.ndim == 3:
        spec = pl.BlockSpec((None, tr, c), lambda i: (0, i, 0))
    else:
        spec = pl.BlockSpec((tr, c), lambda i: (i, 0))

    def body(w_ref, g_ref, m_ref, v_ref, d_ref, nm_ref, nv_ref):
        d_ref[...], nm_ref[...], nv_ref[...] = _adamw_math(w_ref[...], g_ref[...], m_ref[...], v_ref[...])

    return pl.pallas_call(
        body, name=name, grid=(r // tr,), in_specs=[spec] * 4, out_specs=[spec] * 3,
        out_shape=[_sds(w.shape)] * 3, compiler_params=_cp("parallel"))(w, g, m, v)


_ANY = pl.BlockSpec(memory_space=pl.ANY)


def _place():
    return lax.axis_index("x"), lax.axis_index("y"), lax.axis_index("c")


def _other_chips(x, y):
    return [(1 - x, y), (x, 1 - y), (1 - x, 1 - y)]


def _remote(src, dst, send_sems, recv_sems, k, to):
    return pltpu.make_async_remote_copy(src_ref=src, dst_ref=dst, send_sem=send_sems.at[k],
                                        recv_sem=recv_sems.at[k], device_id=to, device_id_type=MESH)


def _gather_weights(pack):
    rows, width = pack.shape
    half = rows // 2

    def body(x_ref, out_ref, send_sems, recv_sems):
        x, y, c = _place()
        sibling = (x, y, 1 - c)
        chips = _other_chips(x, y)

        def block(chip, part):
            return out_ref.at[2 * chip[0] + chip[1], pl.ds(part * half, half), :]

        first = [_remote(x_ref.at[pl.ds(c * half, half), :], block((x, y), c), send_sems, recv_sems, j, (*chip, c))
                 for j, chip in enumerate(chips)]
        for cp in first:
            cp.start()
        passed = [_remote(block(chip, c), block(chip, c), send_sems, recv_sems, 3 + j, sibling)
                  for j, chip in enumerate(chips)]
        for j, chip in enumerate(chips):
            _remote(block(chip, c), block(chip, c), send_sems, recv_sems, j, sibling).wait_recv()
            passed[j].start()
        for j, chip in enumerate(chips):
            _remote(block(chip, 1 - c), block(chip, 1 - c), send_sems, recv_sems, 3 + j, sibling).wait_recv()
        for cp in first + passed:
            cp.wait_send()

    return pl.pallas_call(
        body, name="gather_weights", in_specs=[_ANY], out_specs=_ANY,
        out_shape=_sds((N_SHARD, rows, width), pack.dtype),
        scratch_shapes=[pltpu.SemaphoreType.DMA((6,)), pltpu.SemaphoreType.DMA((6,))],
    )(pack)


_HBM = pl.BlockSpec(memory_space=pltpu.HBM)
_SEMS = pl.BlockSpec(memory_space=pltpu.SEMAPHORE)
_DATAFLOW = pltpu.SideEffectType.DATAFLOW_SIDE_EFFECTING


def _fetch_start(pack, after):
    def body(x_ref, land_ref, after_ref, send_sems, recv_sems, x_thru, land_thru, token):
        x, y, c = _place()
        for j, chip in enumerate(_other_chips(x, y)):
            _remote(x_ref, land_ref.at[2 * x + y], send_sems, recv_sems, j, (*chip, c)).start()
        token[...] = jnp.zeros_like(token)

    land = lax.empty((N_SHARD,) + pack.shape, pack.dtype)
    return pl.pallas_call(
        body, name="fetch_ffn_start",
        out_shape=(pltpu.SemaphoreType.DMA((3,)), pltpu.SemaphoreType.DMA((3,)), pltpu.HBM(pack.shape, pack.dtype),
                   pltpu.HBM(land.shape, land.dtype), _sds((8, 128))),
        in_specs=(_HBM, _HBM, _ANY), out_specs=(_SEMS, _SEMS, _HBM, _HBM, pl.BlockSpec(memory_space=pltpu.VMEM)),
        input_output_aliases={0: 2, 1: 3}, compiler_params=pltpu.CompilerParams(has_side_effects=_DATAFLOW),
    )(pltpu.with_memory_space_constraint(pack, pltpu.HBM), pltpu.with_memory_space_constraint(land, pltpu.HBM), after)


def _fetch_wait(send_sems, recv_sems, pack_thru, land_thru, after):
    def body(x_ref, land_ref, send_sems, recv_sems, after_ref, x_dead, got_ref):
        x, y, c = _place()
        for j, chip in enumerate(_other_chips(x, y)):
            cp = _remote(x_ref, land_ref.at[2 * chip[0] + chip[1]], send_sems, recv_sems, j, (*chip, c))
            cp.wait_send()
            cp.wait_recv()

    return pl.pallas_call(
        body, name="fetch_ffn_wait",
        out_shape=(pltpu.HBM(pack_thru.shape, pack_thru.dtype), pltpu.HBM(land_thru.shape, land_thru.dtype)),
        in_specs=(_HBM, _HBM, _SEMS, _SEMS, _ANY), out_specs=(_HBM, _HBM), input_output_aliases={0: 0, 1: 1},
        compiler_params=pltpu.CompilerParams(has_side_effects=_DATAFLOW),
    )(pack_thru, land_thru, send_sems, recv_sems, after)[1]


def _swap_with_sibling(block, name):
    def body(x_ref, out_ref, send_sems, recv_sems):
        x, y, c = _place()
        cp = _remote(x_ref, out_ref, send_sems, recv_sems, 0, (x, y, 1 - c))
        cp.start()
        cp.wait()

    return pl.pallas_call(
        body, name=name, in_specs=[_ANY], out_specs=_ANY, out_shape=_sds(block.shape, block.dtype),
        scratch_shapes=[pltpu.SemaphoreType.DMA((1,)), pltpu.SemaphoreType.DMA((1,))],
    )(block)


def _swap_other_half(packed, tag):
    slots, rows, width = packed.shape
    half = rows // 2

    def body(x_ref, out_ref, send_sems, recv_sems):
        x, y, c = _place()
        cp = _remote(x_ref.at[:, pl.ds((1 - c) * half, half), :], out_ref, send_sems, recv_sems, 0, (x, y, 1 - c))
        cp.start()
        cp.wait()

    return pl.pallas_call(
        body, name="swap_halves_" + tag, in_specs=[_ANY], out_specs=_ANY, out_shape=_sds((slots, half, width)),
        scratch_shapes=[pltpu.SemaphoreType.DMA((1,)), pltpu.SemaphoreType.DMA((1,))],
    )(packed)


def _add_halves(packed, got, c, tag):
    slots, rows, width = packed.shape
    half = rows // 2
    tr = _tile(half, 408, 16)
    per = half // tr
    block = (None, tr, width)

    def body(c_ref, mine_ref, got_ref, sum_ref, sum16_ref):
        acc = mine_ref[...] + got_ref[...]
        sum_ref[...] = acc
        sum16_ref[...] = acc.astype(BF16)

    plain = pl.BlockSpec(block, lambda s, i, c_ref: (s, i, 0))
    grid_spec = pltpu.PrefetchScalarGridSpec(
        num_scalar_prefetch=1, grid=(slots, per),
        in_specs=[pl.BlockSpec(block, lambda s, i, c_ref: (s, c_ref[0] * per + i, 0)), plain],
        out_specs=[plain, plain])
    return pl.pallas_call(
        body, name="add_halves_" + tag, grid_spec=grid_spec,
        out_shape=[_sds((slots, half, width)), _sds((slots, half, width), BF16)],
        compiler_params=_cp("parallel", "parallel"))(c.reshape(1).astype(jnp.int32), packed, got)


def _add_quarters(chip_sum, others, chip, tag):
    _, rows, width = chip_sum.shape
    tr = _tile(rows, 408, 16)

    def body(chip_ref, own_ref, others_ref, o_ref):
        acc = own_ref[...]
        for j in range(3):
            acc = acc + others_ref[j].astype(F32)
        o_ref[...] = acc

    grid_spec = pltpu.PrefetchScalarGridSpec(
        num_scalar_prefetch=1, grid=(rows // tr,),
        in_specs=[pl.BlockSpec((None, tr, width), lambda i, chip_ref: (chip_ref[0], i, 0)),
                  pl.BlockSpec((3, tr, width), lambda i, chip_ref: (0, i, 0))],
        out_specs=pl.BlockSpec((tr, width), lambda i, chip_ref: (i, 0)))
    return pl.pallas_call(
        body, name="add_quarters_" + tag, grid_spec=grid_spec, out_shape=_sds((rows, width)),
        compiler_params=_cp("parallel"))(chip.reshape(1).astype(jnp.int32), chip_sum, others)


def _exchange_start(parts, tag):
    _, rows, width = parts.shape

    def body(x_ref, land_ref, send_sems, recv_sems, x_thru, land_thru, token):
        x, y, c = _place()
        for j, chip in enumerate(_other_chips(x, y)):
            _remote(x_ref.at[2 * chip[0] + chip[1]], land_ref.at[j], send_sems, recv_sems, j, (*chip, c)).start()
        token[...] = jnp.zeros_like(token)

    land = lax.empty((3, rows, width), parts.dtype)
    return pl.pallas_call(
        body, name="exchange_" + tag + "_start",
        out_shape=(pltpu.SemaphoreType.DMA((3,)), pltpu.SemaphoreType.DMA((3,)), pltpu.HBM(parts.shape, parts.dtype),
                   pltpu.HBM(land.shape, land.dtype), _sds((8, 128))),
        in_specs=(_HBM, _HBM), out_specs=(_SEMS, _SEMS, _HBM, _HBM, pl.BlockSpec(memory_space=pltpu.VMEM)),
        input_output_aliases={0: 2, 1: 3}, compiler_params=pltpu.CompilerParams(has_side_effects=_DATAFLOW),
    )(pltpu.with_memory_space_constraint(parts, pltpu.HBM), pltpu.with_memory_space_constraint(land, pltpu.HBM))


def _exchange_wait(send_sems, recv_sems, parts_thru, land_thru, after, tag):
    def body(x_ref, land_ref, send_sems, recv_sems, after_ref, x_dead, got_ref):
        x, y, c = _place()
        for j, chip in enumerate(_other_chips(x, y)):
            cp = _remote(x_ref.at[2 * chip[0] + chip[1]], land_ref.at[j], send_sems, recv_sems, j, (*chip, c))
            cp.wait_send()
            cp.wait_recv()

    return pl.pallas_call(
        body, name="exchange_" + tag + "_wait",
        out_shape=(pltpu.HBM(parts_thru.shape, parts_thru.dtype), pltpu.HBM(land_thru.shape, land_thru.dtype)),
        in_specs=(_HBM, _HBM, _SEMS, _SEMS, _ANY), out_specs=(_HBM, _HBM), input_output_aliases={0: 0, 1: 1},
        compiler_params=pltpu.CompilerParams(has_side_effects=_DATAFLOW),
    )(parts_thru, land_thru, send_sems, recv_sems, after)[1]


def _allreduce_small(vec):
    rows, width = vec.shape
    vmem = pl.BlockSpec(memory_space=pltpu.VMEM)

    def body(x_ref, o_ref, buf_ref, send_sems, recv_sems):
        x, y, c = _place()
        me = 4 * x + 2 * y + c
        buf_ref[me] = x_ref[...]
        copies = []
        for k in range(1, N_DEV):
            peer = (x ^ ((k >> 2) & 1), y ^ ((k >> 1) & 1), c ^ (k & 1))
            copies.append(_remote(x_ref, buf_ref.at[me], send_sems, recv_sems, k - 1, peer))
        for cp in copies:
            cp.start()
        for k in range(1, N_DEV):
            _remote(x_ref, buf_ref.at[me ^ k], send_sems, recv_sems, k - 1, (x, y, c)).wait_recv()
        for cp in copies:
            cp.wait_send()
        total = buf_ref[0]
        for d in range(1, N_DEV):
            total = total + buf_ref[d]
        o_ref[...] = total

    return pl.pallas_call(
        body, name="allreduce_small", in_specs=[vmem], out_specs=vmem, out_shape=_sds((rows, width)),
        scratch_shapes=[pltpu.VMEM((N_DEV, rows, width), F32), pltpu.SemaphoreType.DMA((N_DEV - 1,)),
                        pltpu.SemaphoreType.DMA((N_DEV - 1,))],
    )(vec)


def _rows1024(a):
    return a.reshape(-1, 1024)


def _pad_rows(a, rows):
    return jnp.concatenate([a, jnp.zeros((rows - a.shape[0], a.shape[1]), a.dtype)], axis=0)


_TRANSPOSED = ("w_in", "w_gate", "w_up")
_SMALL_SHARDED = ("w_up_f", "w_up_b", "a_up_f", "a_up_b", "g_up")
_BIG_SHARDED = ("w_in", "w_out", "w_gate", "w_up", "w_down")


def _pack_weight_shards(w):
    conv_bits = lax.bitcast_convert_type(w["conv_w"], BF16).reshape(1, -1)
    conv_row = jnp.concatenate([conv_bits, jnp.zeros((1, 1024 - conv_bits.shape[1]), BF16)], axis=1)

    def rows(name):
        a = w[name].astype(BF16)
        return a.T if name in _TRANSPOSED else _rows1024(a)

    early = _pad_rows(jnp.concatenate([rows(name) for name, _ in _EARLY_ROWS[:-1]] + [conv_row], axis=0), EARLY_R)
    return early, jnp.concatenate([rows(name) for name, _ in _FFN_ROWS], axis=0)


def _split_rows(gathered, layout):
    out, row = {}, 0
    for name, n in layout:
        out[name] = gathered[:, row:row + n]
        row += n
    return out


def _unpack_early(gathered):
    out = _split_rows(gathered, _EARLY_ROWS)
    cols = lambda a, k: jnp.concatenate([a[s].reshape(k, -1) for s in range(N_SHARD)], axis=1)
    conv = lax.bitcast_convert_type(out["conv_w"][:, 0, :768].reshape(N_SHARD, 3, 128, 2), F32)
    full = dict(w_in=out["w_in"].reshape(-1, 1024).T, w_out=out["w_out"].reshape(D_MODEL, D_MODEL),
                conv_w=jnp.concatenate([conv[s] for s in range(N_SHARD)], axis=1))
    full.update({name: cols(out[name], D_GATE if name == "g_up" else D_LORA) for name in _SMALL_SHARDED})
    return full


def _unpack_ffn(gathered):
    out = _split_rows(gathered, _FFN_ROWS)
    return dict(w_gate=out["w_gate"].reshape(-1, 1024).T, w_up=out["w_up"].reshape(-1, 1024).T,
                w_down=out["w_down"].reshape(D_FF, D_MODEL))


def _pack_grads(g, layout, rows):
    col_split = lambda a, s: a[:, s * (a.shape[1] // N_SHARD):(s + 1) * (a.shape[1] // N_SHARD)]
    row_split = lambda a, s: a[s * (a.shape[0] // N_SHARD):(s + 1) * (a.shape[0] // N_SHARD)]
    by_rows = {name: (g[name].T if name in _TRANSPOSED else g[name]) for name, _ in layout if name in _BIG_SHARDED}
    used = sum(n for _, n in layout)
    parts = []
    for s in range(N_SHARD):
        for name, _ in layout:
            if name in _BIG_SHARDED:
                parts.append(row_split(by_rows[name], s))
            elif name in _SMALL_SHARDED:
                parts.append(_rows1024(col_split(g[name], s)))
            else:
                conv = col_split(g["conv_w"], s).reshape(1, -1)
                parts.append(jnp.concatenate([conv, jnp.zeros((1, 1024 - conv.shape[1]), F32)], axis=1))
        if rows > used:
            parts.append(jnp.zeros((rows - used, 1024), F32))
    return jnp.concatenate(parts, axis=0).reshape(N_SHARD, rows, 1024)


def _unpack_grad_shard(pack, layout):
    small_shapes = {name: (D_GATE if name == "g_up" else D_LORA, 128) for name in _SMALL_SHARDED}
    out, row = {}, 0
    for name, n in layout:
        piece = pack[row:row + n]
        if name == "conv_w":
            out[name] = piece[0, :384].reshape(3, 128)
        else:
            out[name] = piece.T if name in _TRANSPOSED else piece.reshape(small_shapes.get(name, piece.shape))
        row += n
    return out


_SMALL_LAYOUT = (("norm1_w", 1024), ("mu_shift", D_SHIFT), ("w0_f", 512), ("w0_b", 512), ("a0_f", 512),
                 ("a0_b", 512), ("k_k", 512), ("k_a_f", 512), ("k_a_b", 512), ("r_k_f", 512), ("r_k_b", 512),
                 ("gn_w", 512), ("gn_b", 512), ("norm2_w", 1024), ("norm_f_w", 1024), ("loss", 1))


def _pack_small(vals):
    rows = []
    for name, n in _SMALL_LAYOUT:
        flat = vals[name].reshape(-1)
        n_rows = -(-n // 1024)
        rows.append(jnp.concatenate([flat, jnp.zeros((n_rows * 1024 - n,), F32)]).reshape(n_rows, 1024))
    return _pad_rows(jnp.concatenate(rows, axis=0), SMALL_ROWS)


def _unpack_small(pack):
    out, row = {}, 0
    for name, n in _SMALL_LAYOUT:
        n_rows = -(-n // 1024)
        out[name] = pack[row:row + n_rows].reshape(-1)[:n]
        row += n_rows
    return out


_WEIGHTS = ("norm1_w", "w_in", "mu_shift", "w_up_f", "w0_f", "w_up_b", "w0_b", "a_up_f", "a0_f", "a_up_b", "a0_b",
            "g_up", "k_k", "k_a_f", "k_a_b", "r_k_f", "r_k_b", "gn_w", "gn_b", "conv_w", "w_out", "norm2_w",
            "w_gate", "w_up", "w_down", "norm_f_w")


def _train_step(x, loss_target, w, m, v):
    batch, seq, _ = x.shape
    t = batch * seq
    tm = _tile(seq, 256, 8)
    xs = x.reshape(t, D_MODEL)
    target = loss_target.reshape(t, D_MODEL)
    vec = lambda name: w[name].reshape(1, -1)

    local = {name: w[name][0] for name, _ in _PACK_ROWS}
    c = lax.axis_index("c")
    chip = 2 * lax.axis_index("x") + lax.axis_index("y")
    early, ffn_pack = _pack_weight_shards(local)
    early_all = lax.dynamic_update_slice(_gather_weights(early), early[None], (chip, 0, 0))
    ffn_send, ffn_recv, ffn_pack, ffn_land, token = _fetch_start(ffn_pack, early_all)
    full = _unpack_early(early_all)
    w_in = full["w_in"]
    w_shift = jnp.concatenate([w_in[:, :D_SHIFT], jnp.zeros((D_MODEL, D_SHIFT_PAD - D_SHIFT), BF16)], axis=1)
    w_conv = w_in[:, D_SHIFT:]
    zeros_lora = jnp.zeros((D_LORA, D_RWKV), F32)
    lora = lambda name: full[name].astype(F32)
    mats = (jnp.concatenate([lora("w_up_f"), zeros_lora]), jnp.concatenate([zeros_lora, lora("a_up_f")]),
            jnp.concatenate([lora("w_up_b"), zeros_lora]), jnp.concatenate([zeros_lora, lora("a_up_b")]),
            jnp.concatenate([lora("g_up"), jnp.zeros((D_GATE_PAD - D_GATE, D_RWKV), F32)]))
    mu = jnp.concatenate([vec("mu_shift"), jnp.zeros((1, D_SHIFT_PAD - D_SHIFT), F32)], axis=1)
    mu = jnp.broadcast_to(mu, (GROUP, D_SHIFT_PAD))
    zero_row = jnp.zeros((1, D_RWKV), F32)
    pvec = jnp.concatenate([vec("k_k"), vec("w0_f"), vec("a0_f"), vec("k_a_f"), vec("w0_b"), vec("a0_b"),
                            vec("k_a_b"), zero_row], axis=0)
    qvec = jnp.concatenate([vec("gn_w"), vec("gn_b"), vec("r_k_f"), vec("r_k_b"), full["conv_w"], zero_row], axis=0)
    ones_blocks = _head_ones()

    h1_t, p_shift, pc = _norm_in_proj(xs, vec("norm1_w") + token[0, 0], w_shift, w_conv, tm)
    ps = _shift_fwd(p_shift, mu, tm, seq)
    kk, w_f, kd_f, b_f, w_b, kd_b, b_b, gate = _prep_fwd(ps, pvec, mats, ones_blocks, tm)
    dirs = ((w_f, kd_f, b_f), (w_b, kd_b, b_b))
    y_f, y_b, hist_f, hist_b = _scan_fwd(ps, kk, dirs, batch, seq)
    mixed, mixed_t = _post_fwd(y_f, y_b, ps, kd_f, kd_b, gate, pc, qvec, ones_blocks, tm, seq)
    x1, h2, h2_t = _out_proj_norm(mixed, full["w_out"], xs, vec("norm2_w"), tm)
    ffn_all = _fetch_wait(ffn_send, ffn_recv, ffn_pack, ffn_land, h2)
    full.update(_unpack_ffn(lax.dynamic_update_slice(ffn_all, ffn_pack[None], (chip, 0, 0))))
    ff_gate, ff_up, act, act_t = _ffn_in(h2, full["w_gate"], full["w_up"])
    d_x2, d_norm_f, loss_part = _ffn_out_loss(act, full["w_down"], x1, w["norm_f_w"].reshape(1, -1), target, tm)

    g = {}
    g["w_down"] = _matmul(act_t, d_x2, mode="nn", name="ffn_down_dw")
    d_gate, d_up = _ffn_in_bwd(d_x2, full["w_down"], ff_gate, ff_up)
    g["w_gate"] = _matmul(h2_t, d_gate, mode="nn", name="ffn_gate_dw")
    g["w_up"] = _matmul(h2_t, d_up, mode="nn", name="ffn_up_dw")
    ffn_grads = _pack_grads(g, _FFN_ROWS, sum(n for _, n in _FFN_ROWS))
    ffn_sum, ffn_sum_bf16 = _add_halves(ffn_grads, _swap_other_half(ffn_grads, "ffn"), c, "ffn")
    ex_send, ex_recv, ffn_sum_bf16, ex_land, ex_token = _exchange_start(ffn_sum_bf16, "ffn")
    d_x1, d_norm2 = _proj_norm_bwd(d_gate, full["w_gate"], d_up, full["w_up"], x1, vec("norm2_w") + ex_token[0, 0],
                                   d_x2, tm, "ffn_in_dx_norm2_bwd")
    d_mixed = _matmul(d_x1, full["w_out"], mode="nt", name="out_proj_dx")
    g["w_out"] = _matmul(mixed_t, d_x1, mode="nn", name="out_proj_dw")
    dy, dr_o, dkdf_o, dkdb_o, dv_o, d_gatev, d_pc, d_qvec = _post_bwd(
        d_mixed, y_f, y_b, ps, kd_f, kd_b, gate, pc, qvec, ones_blocks, tm, seq)
    (dr_f, dw_f, dkd_f, dv_f, dkk_f, db_f), (dr_b, dw_b, dkd_b, dv_b, dkk_b, db_b) = _scan_bwd(
        ps, kk, dirs, dy, hist_f, hist_b, batch, seq)
    cts = [[dr_f, dr_b, dr_o], [dv_f, dv_b, dv_o], [dkk_f, dkk_b], [dw_f], [dkd_f, dkdf_o], [db_f],
           [dw_b], [dkd_b, dkdb_o], [db_b], [d_gatev]]
    q, d_pvec, d_m0, d_m1, d_m2, d_m3, d_m4 = _prep_bwd(ps, pvec, mats, ones_blocks, cts, tm)
    d_pshift, d_mu = _shift_bwd(q, p_shift, mu, tm, seq)
    d_w_shift = _matmul(h1_t, d_pshift, mode="nn", name="in_proj_shift_dw")
    d_w_conv = _matmul(h1_t, d_pc, mode="nn", name="in_proj_conv_dw")
    g["w_in"] = jnp.concatenate([d_w_shift[:, :D_SHIFT], d_w_conv], axis=1)
    d_x, d_norm1 = _proj_norm_bwd(d_pshift, w_shift, d_pc, w_conv, xs, vec("norm1_w"), d_x1, tm,
                                  "in_proj_dx_norm1_bwd")
    g["w_up_f"], g["a_up_f"] = d_m0[:D_LORA], d_m1[D_LORA:]
    g["w_up_b"], g["a_up_b"] = d_m2[:D_LORA], d_m3[D_LORA:]
    g["g_up"] = d_m4[:D_GATE]
    g["conv_w"] = d_qvec[4:7]

    def finish(chip_sum, others, tag, layout):
        eighth = _add_quarters(chip_sum, others, chip, tag)
        other_eighth = _swap_with_sibling(eighth, "swap_eighths_" + tag)
        return _unpack_grad_shard(jnp.concatenate([jnp.where(c == 0, eighth, other_eighth),
                                                   jnp.where(c == 0, other_eighth, eighth)], axis=0), layout)

    as2d = lambda name: (1, w[name].shape[0]) if w[name].ndim == 1 else w[name].shape
    operands = lambda name: tuple(a.reshape(as2d(name)) for a in (w[name], grads[name], m[name], v[name]))

    packed = _pack_grads(g, _EARLY_ROWS, EARLY_R)
    mix_sum, mix_sum_bf16 = _add_halves(packed, _swap_other_half(packed, "mixer"), c, "mixer")
    mx_send, mx_recv, mix_sum_bf16, mx_land, mx_token = _exchange_start(mix_sum_bf16, "mixer")
    grads = finish(ffn_sum, _exchange_wait(ex_send, ex_recv, ffn_sum_bf16, ex_land, mx_token, "ffn"), "ffn", _FFN_ROWS)
    updates = {name: _adamw(*operands(name), "adamw_" + name) for name in _FFN_NAMES}
    mix_others = _exchange_wait(mx_send, mx_recv, mix_sum_bf16, mx_land, updates["w_down"][2], "mixer")
    grads.update(finish(mix_sum, mix_others, "mixer", _EARLY_ROWS))

    small = dict(norm1_w=d_norm1, mu_shift=d_mu[:, :D_SHIFT], w0_f=d_pvec[1], w0_b=d_pvec[4], a0_f=d_pvec[2],
                 a0_b=d_pvec[5], k_k=d_pvec[0], k_a_f=d_pvec[3], k_a_b=d_pvec[6], r_k_f=d_qvec[2], r_k_b=d_qvec[3],
                 gn_w=d_qvec[0], gn_b=d_qvec[1], norm2_w=d_norm2, norm_f_w=d_norm_f, loss=loss_part)
    reduced = _unpack_small(_allreduce_small(_pack_small(small)))
    loss = reduced.pop("loss")[0]
    grads.update(reduced)

    outs = {}
    small = [name for name in _WEIGHTS if name not in _BIG_SHARDED]
    updates.update(zip(small, _adamw_small([operands(name) for name in small])))
    for name in ("w_in", "w_out"):
        updates[name] = _adamw(*operands(name), "adamw_" + name)
    for name in _WEIGHTS:
        shape = w[name].shape
        outs[name] = (grads[name].reshape(shape),) + tuple(a.reshape(shape) for a in updates[name])
    d_x = d_x.reshape(batch, seq, D_MODEL)
    return (loss, d_x) + tuple(outs[name][k] for k in range(4) for name in _WEIGHTS)


def kernel(x, norm1_w, w_in, mu_shift, w_up_f, w0_f, w_up_b, w0_b, a_up_f, a0_f, a_up_b, a0_b, g_up, k_k, k_a_f, k_a_b, r_k_f, r_k_b, gn_w, gn_b, conv_w, w_out, norm2_w, w_gate, w_up, w_down, norm_f_w, loss_target, m_norm1_w, m_w_in, m_mu_shift, m_w_up_f, m_w0_f, m_w_up_b, m_w0_b, m_a_up_f, m_a0_f, m_a_up_b, m_a0_b, m_g_up, m_k_k, m_k_a_f, m_k_a_b, m_r_k_f, m_r_k_b, m_gn_w, m_gn_b, m_conv_w, m_w_out, m_norm2_w, m_w_gate, m_w_up, m_w_down, m_norm_f_w, v_norm1_w, v_w_in, v_mu_shift, v_w_up_f, v_w0_f, v_w_up_b, v_w0_b, v_a_up_f, v_a0_f, v_a_up_b, v_a0_b, v_g_up, v_k_k, v_k_a_f, v_k_a_b, v_r_k_f, v_r_k_b, v_gn_w, v_gn_b, v_conv_w, v_w_out, v_norm2_w, v_w_gate, v_w_up, v_w_down, v_norm_f_w):
    args = locals()
    w = {name: args[name] for name in _WEIGHTS}
    m = {name: args["m_" + name] for name in _WEIGHTS}
    v = {name: args["v_" + name] for name in _WEIGHTS}
    return _train_step(x, loss_target, w, m, v)
```

```python
import functools

import jax
import jax.numpy as jnp
from jax import lax
from jax.experimental import pallas as pl
from jax.experimental.pallas import tpu as pltpu

F32 = jnp.float32
BF16 = jnp.bfloat16
MESH = pl.DeviceIdType.MESH

D_MODEL = 1024
D_RWKV = 512
HEAD = 64
N_PAIR = D_RWKV // (2 * HEAD)
D_LORA = 64
D_GATE = 160
D_GATE_PAD = 384
D_FF = 2816
D_SHIFT = 1824
D_SHIFT_PAD = 2048
D_CONV3 = 1536
LOG_DECAY_SCALE = 0.606531
RMS_EPS = 1e-6
GN_EPS = 64e-5
NORM_EPS = 1e-12
ADAM_LR, ADAM_B1, ADAM_B2, ADAM_EPS, ADAM_WD, ADAM_STEP = 0.001, 0.9, 0.999, 1e-08, 0.01, 10

N_SHARD = 4
N_DEV = 8
V7X_VMEM_LIMIT = 48 * 1024 * 1024
SCAN_CHUNK = 32
GROUP = 8

_PACK_ROWS = (("w_in", 840), ("w_out", 256), ("w_gate", 704), ("w_up", 704), ("w_down", 704),
              ("w_up_f", 8), ("w_up_b", 8), ("a_up_f", 8), ("a_up_b", 8), ("g_up", 20), ("conv_w", 1))
_FFN_NAMES = ("w_gate", "w_up", "w_down")
_EARLY_ROWS = tuple(item for item in _PACK_ROWS if item[0] not in _FFN_NAMES)
_FFN_ROWS = tuple(item for item in _PACK_ROWS if item[0] in _FFN_NAMES)
EARLY_R = 1152
SMALL_ROWS = 24


def _tile(n, cap, mult=128):
    best = None
    t = mult
    while t <= min(n, cap):
        if n % t == 0:
            best = t
        t += mult
    return best or n


def _cp(*sem):
    return pltpu.CompilerParams(dimension_semantics=sem or None, vmem_limit_bytes=V7X_VMEM_LIMIT)


def _sds(shape, dtype=F32):
    return jax.ShapeDtypeStruct(shape, dtype)


def _matmul(a, b, *, mode, name, out_dtype=F32, add=None):
    m, kdim = a.shape
    n = b.shape[1] if mode == "nn" else b.shape[0]
    tm, tn = _tile(m, 512, 8), _tile(n, 1536)
    tk = kdim if kdim <= 3584 else _tile(kdim, 1024)
    nk = kdim // tk
    a_spec = pl.BlockSpec((tm, tk), lambda i, j, k: (i, k))
    if mode == "nn":
        b_spec = pl.BlockSpec((tk, tn), lambda i, j, k: (k, j))
        dims = (((1,), (0,)), ((), ()))
    else:
        b_spec = pl.BlockSpec((tn, tk), lambda i, j, k: (j, k))
        dims = (((1,), (1,)), ((), ()))
    has_add = add is not None

    def body(*refs):
        a_ref, b_ref = refs[0], refs[1]
        add_ref = refs[2] if has_add else None
        o_ref = refs[3] if has_add else refs[2]
        part = lax.dot_general(a_ref[...].astype(BF16), b_ref[...].astype(BF16), dims,
                               preferred_element_type=F32)
        if nk == 1:
            if has_add:
                part = part + add_ref[...]
            o_ref[...] = part.astype(out_dtype)
        else:
            acc_ref = refs[-1]
            k = pl.program_id(2)

            @pl.when(k == 0)
            def _():
                acc_ref[...] = jnp.zeros_like(acc_ref)

            acc_ref[...] += part

            @pl.when(k == nk - 1)
            def _():
                res = acc_ref[...]
                if has_add:
                    res = res + add_ref[...]
                o_ref[...] = res.astype(out_dtype)

    o_spec = pl.BlockSpec((tm, tn), lambda i, j, k: (i, j))
    in_specs = [a_spec, b_spec] + ([o_spec] if has_add else [])
    args = (a, b) + ((add,) if has_add else ())
    return pl.pallas_call(
        body, name=name, grid=(m // tm, n // tn, nk), in_specs=in_specs, out_specs=o_spec,
        out_shape=_sds((m, n), out_dtype),
        scratch_shapes=[pltpu.VMEM((tm, tn), F32)] if nk > 1 else [],
        compiler_params=_cp("parallel", "parallel", "arbitrary"),
    )(*args)


def _row(tm, width):
    return pl.BlockSpec((tm, width), lambda i: (i, 0))


def _col(tm, height):
    return pl.BlockSpec((height, tm), lambda i: (0, i))


def _fixed(shape):
    return pl.BlockSpec(shape, lambda i: tuple(0 for _ in shape))


def _rmsnorm_tile(xv, wv):
    return xv * lax.rsqrt(jnp.mean(xv * xv, axis=-1, keepdims=True) + RMS_EPS) * wv


def _norm_in_proj(x, w, w_shift, w_conv, tm):
    t, d = x.shape
    n_a, n_b = w_shift.shape[1], w_conv.shape[1]

    def body(x_ref, w_ref, wa_ref, wb_ref, ht_ref, pa_ref, pb_ref):
        hv = _rmsnorm_tile(x_ref[...], w_ref[...])
        ht_ref[...] = jnp.transpose(hv).astype(BF16)
        hb = hv.astype(BF16)
        pa_ref[...] = jnp.dot(hb, wa_ref[...], preferred_element_type=F32)
        pb_ref[...] = jnp.dot(hb, wb_ref[...], preferred_element_type=F32)

    return pl.pallas_call(
        body, name="norm1_in_proj", grid=(t // tm,),
        in_specs=[_row(tm, d), _fixed((1, d)), _fixed((d, n_a)), _fixed((d, n_b))],
        out_specs=[_col(tm, d), _row(tm, n_a), _row(tm, n_b)],
        out_shape=[_sds((d, t), BF16), _sds((t, n_a)), _sds((t, n_b))],
        compiler_params=_cp("parallel"))(x, w, w_shift, w_conv)


def _out_proj_norm(mixed, w_out, res, w, tm):
    t, d = res.shape

    def body(m_ref, wo_ref, r_ref, w_ref, x_ref, h_ref, ht_ref):
        xv = r_ref[...] + jnp.dot(m_ref[...], wo_ref[...], preferred_element_type=F32)
        x_ref[...] = xv
        hv = _rmsnorm_tile(xv, w_ref[...])
        h_ref[...] = hv.astype(BF16)
        ht_ref[...] = jnp.transpose(hv).astype(BF16)

    return pl.pallas_call(
        body, name="out_proj_norm2", grid=(t // tm,),
        in_specs=[_row(tm, d), _fixed((d, d)), _row(tm, d), _fixed((1, d))],
        out_specs=[_row(tm, d), _row(tm, d), _col(tm, d)],
        out_shape=[_sds((t, d)), _sds((t, d), BF16), _sds((d, t), BF16)],
        compiler_params=_cp("parallel"))(mixed, w_out, res, w)


def _rms_bwd_math(xv, wv, dyv):
    rstd = lax.rsqrt(jnp.mean(xv * xv, axis=-1, keepdims=True) + RMS_EPS)
    xhat = xv * rstd
    gv = dyv * wv
    dx = rstd * (gv - xhat * jnp.mean(gv * xhat, axis=-1, keepdims=True))
    return dx, jnp.sum(dyv * xhat, axis=0, keepdims=True)


def _proj_norm_bwd(dy_a, w_a, dy_b, w_b, x, w, dres, tm, name):
    t, d = x.shape
    ka, kb = dy_a.shape[1], dy_b.shape[1]
    nt = (((1,), (1,)), ((), ()))

    def body(dya_ref, wa_ref, dyb_ref, wb_ref, x_ref, w_ref, dres_ref, dx_ref, dw_ref):
        d_h = (lax.dot_general(dya_ref[...], wa_ref[...], nt, preferred_element_type=F32)
               + lax.dot_general(dyb_ref[...], wb_ref[...], nt, preferred_element_type=F32))
        dx, dw = _rms_bwd_math(x_ref[...], w_ref[...], d_h)
        dx_ref[...] = dres_ref[...] + dx

        @pl.when(pl.program_id(0) == 0)
        def _():
            dw_ref[...] = jnp.zeros_like(dw_ref)

        dw_ref[...] += dw

    return pl.pallas_call(
        body, name=name, grid=(t // tm,),
        in_specs=[_row(tm, ka), _fixed((d, ka)), _row(tm, kb), _fixed((d, kb)), _row(tm, d), _fixed((1, d)),
                  _row(tm, d)],
        out_specs=[_row(tm, d), _fixed((1, d))],
        out_shape=[_sds((t, d)), _sds((1, d))], compiler_params=_cp("arbitrary"))(dy_a, w_a, dy_b, w_b, x, w, dres)


def _ffn_out_loss(act, w_down, x1, w, target, tm):
    t, d = x1.shape
    f = act.shape[1]

    def body(a_ref, wd_ref, x_ref, w_ref, t_ref, dx_ref, dxt_ref, dw_ref, loss_ref):
        xv = x_ref[...] + jnp.dot(a_ref[...], wd_ref[...], preferred_element_type=F32)
        wv = w_ref[...]
        rstd = lax.rsqrt(jnp.mean(xv * xv, axis=-1, keepdims=True) + RMS_EPS)
        err = xv * rstd * wv - t_ref[...]
        dx, dw = _rms_bwd_math(xv, wv, err * (1.0 / d))
        dx_ref[...] = dx
        dxt_ref[...] = jnp.transpose(dx).astype(BF16)

        @pl.when(pl.program_id(0) == 0)
        def _():
            dw_ref[...] = jnp.zeros_like(dw_ref)
            loss_ref[...] = jnp.zeros_like(loss_ref)

        dw_ref[...] += dw
        loss_ref[...] += 0.5 * jnp.sum(jnp.mean(err * err, axis=-1, keepdims=True), axis=0, keepdims=True)

    return pl.pallas_call(
        body, name="ffn_out_loss", grid=(t // tm,),
        in_specs=[_row(tm, f), _fixed((f, d)), _row(tm, d), _fixed((1, d)), _row(tm, d)],
        out_specs=[_row(tm, d), _col(tm, d), _fixed((1, d)), _fixed((1, 1))],
        out_shape=[_sds((t, d)), _sds((d, t), BF16), _sds((1, d)), _sds((1, 1))],
        compiler_params=_cp("arbitrary"))(act, w_down, x1, w, target)


def _ffn_in(h, w_gate, w_up):
    t, d = h.shape
    f = w_gate.shape[1]
    tm, tn = _tile(t, 512, 8), _tile(f, 1536)

    def body(h_ref, wg_ref, wu_ref, g_ref, u_ref, a_ref):
        hv = h_ref[...]
        gv = jnp.dot(hv, wg_ref[...], preferred_element_type=F32)
        uv = jnp.dot(hv, wu_ref[...], preferred_element_type=F32)
        g_ref[...] = gv.astype(BF16)
        u_ref[...] = uv.astype(BF16)
        a_ref[...] = (gv * jax.nn.sigmoid(gv) * uv).astype(BF16)

    w_spec = pl.BlockSpec((d, tn), lambda i, j: (0, j))
    o_spec = pl.BlockSpec((tm, tn), lambda i, j: (i, j))
    return pl.pallas_call(
        body, name="ffn_in", grid=(t // tm, f // tn),
        in_specs=[pl.BlockSpec((tm, d), lambda i, j: (i, 0)), w_spec, w_spec],
        out_specs=[o_spec, o_spec, o_spec], out_shape=[_sds((t, f), BF16)] * 3,
        compiler_params=_cp("parallel", "parallel"))(h, w_gate, w_up)


def _ffn_in_bwd(d_out, w_down, gate, up):
    t, d = d_out.shape
    f = w_down.shape[0]
    tm, tn = _tile(t, 512, 8), _tile(f, 1536)

    def body(do_ref, w_ref, g_ref, u_ref, dg_ref, du_ref):
        dv = lax.dot_general(do_ref[...].astype(BF16), w_ref[...], (((1,), (1,)), ((), ())),
                             preferred_element_type=F32)
        gv, uv = g_ref[...].astype(F32), u_ref[...].astype(F32)
        sg = jax.nn.sigmoid(gv)
        du_ref[...] = (dv * gv * sg).astype(BF16)
        dg_ref[...] = (dv * uv * (sg * (1.0 + gv * (1.0 - sg)))).astype(BF16)

    tile = pl.BlockSpec((tm, tn), lambda i, j: (i, j))
    return pl.pallas_call(
        body, name="ffn_in_bwd", grid=(t // tm, f // tn),
        in_specs=[pl.BlockSpec((tm, d), lambda i, j: (i, 0)), pl.BlockSpec((tn, d), lambda i, j: (j, 0)), tile, tile],
        out_specs=[tile, tile], out_shape=[_sds((t, f), BF16)] * 2,
        compiler_params=_cp("parallel", "parallel"))(d_out, w_down, gate, up)


def _halo_specs(tm, width, rows_total):
    per = tm // GROUP
    last = rows_total // GROUP - 1
    prev = pl.BlockSpec((GROUP, width), lambda i: (jnp.maximum(i * per - 1, 0), 0))
    nxt = pl.BlockSpec((GROUP, width), lambda i: (jnp.minimum((i + 1) * per, last), 0))
    return prev, nxt


def _edge_flags(tm, seq):
    i = pl.program_id(0)
    has_prev = jnp.where((i * tm) % seq == 0, 0.0, 1.0).astype(F32)
    has_next = jnp.where(((i + 1) * tm) % seq == 0, 0.0, 1.0).astype(F32)
    return has_prev, has_next


def _shifted(xv, prev_row, next_row):
    tm = xv.shape[0]
    row = lax.broadcasted_iota(jnp.int32, xv.shape, 0)
    down = jnp.where(row == 0, prev_row, pltpu.roll(xv, 1, axis=0))
    up = jnp.where(row == tm - 1, next_row, pltpu.roll(xv, tm - 1, axis=0))
    return down, up


def _shift_bwd(q, p, mu, tm, seq):
    t, w = p.shape
    prev_spec, next_spec = _halo_specs(tm, w, t)

    def body(q_ref, qp_ref, qn_ref, p_ref, pp_ref, pn_ref, mu_ref, dp_ref, dmu_ref):
        has_prev, has_next = _edge_flags(tm, seq)
        muv = mu_ref[0:1, :]
        qv = q_ref[...]
        mq = muv * qv
        mq_down, mq_up = _shifted(mq, muv * qp_ref[GROUP - 1:GROUP, :] * has_prev,
                                  muv * qn_ref[0:1, :] * has_next)
        dp_ref[...] = (qv - mq + 0.5 * (mq_down + mq_up)).astype(BF16)
        pv = p_ref[...]
        p_down, p_up = _shifted(pv, pp_ref[GROUP - 1:GROUP, :] * has_prev, pn_ref[0:1, :] * has_next)

        @pl.when(pl.program_id(0) == 0)
        def _():
            dmu_ref[...] = jnp.zeros_like(dmu_ref)

        dmu_ref[...] += jnp.sum(qv * (0.5 * (p_down + p_up) - pv), axis=0, keepdims=True)

    return pl.pallas_call(
        body, name="shift_bwd", grid=(t // tm,),
        in_specs=[_row(tm, w), prev_spec, next_spec, _row(tm, w), prev_spec, next_spec, _fixed((GROUP, w))],
        out_specs=[_row(tm, w), _fixed((1, w))],
        out_shape=[_sds((t, w), BF16), _sds((1, w))], compiler_params=_cp("arbitrary"))(q, q, q, p, p, p, mu)


@jax.custom_vjp
def _bdot(a, b):
    return jnp.dot(a.astype(BF16), b.astype(BF16), preferred_element_type=F32)


def _bdot_fwd(a, b):
    return _bdot(a, b), (a, b)


def _bdot_bwd(res, g):
    a, b = res
    gb = g.astype(BF16)
    da = lax.dot_general(gb, b.astype(BF16), (((1,), (1,)), ((), ())), preferred_element_type=F32)
    db = lax.dot_general(a.astype(BF16), gb, (((0,), (0,)), ((), ())), preferred_element_type=F32)
    return da, db


_bdot.defvjp(_bdot_fwd, _bdot_bwd)


def _seg_raw(x, ones_blocks):
    hi = x.astype(BF16)
    lo = (x - hi.astype(F32)).astype(BF16)
    return (jnp.dot(hi, ones_blocks, preferred_element_type=F32)
            + jnp.dot(lo, ones_blocks, preferred_element_type=F32))


@jax.custom_vjp
def _seg(x, ones_blocks):
    return _seg_raw(x, ones_blocks)


def _seg_fwd(x, ones_blocks):
    return _seg_raw(x, ones_blocks), ones_blocks


def _seg_bwd(ones_blocks, g):
    return _seg_raw(g, ones_blocks), jnp.zeros_like(ones_blocks)


_seg.defvjp(_seg_fwd, _seg_bwd)


def _head_ones():
    h = jnp.arange(D_RWKV) // HEAD
    return (h[:, None] == h[None, :]).astype(BF16)


def _prep_math(ps, k_k, w0_f, a0_f, k_a_f, w0_b, a0_b, k_a_b, wup_f, aup_f, wup_b, aup_b, gup, ones_blocks):
    r = ps[:, 0:512]
    k = ps[:, 512:1024]
    v = ps[:, 1024:1536]
    xwa = ps[:, 1536:1664]
    xg = ps[:, 1664:D_SHIFT_PAD]
    kk_raw = k * k_k
    norm = jnp.sqrt(_seg(kk_raw * kk_raw, ones_blocks))
    kk = kk_raw / jnp.maximum(norm, NORM_EPS)
    t_xwa = jnp.tanh(xwa)
    outs = [r, v, kk]
    for w0, a0, k_a, wup, aup in ((w0_f, a0_f, k_a_f, wup_f, aup_f), (w0_b, a0_b, k_a_b, wup_b, aup_b)):
        decay = jnp.exp(-LOG_DECAY_SCALE * jax.nn.sigmoid(w0 + _bdot(t_xwa, wup)))
        rate = jax.nn.sigmoid(a0 + _bdot(xwa, aup))
        outs += [decay, k * (1.0 + (rate - 1.0) * k_a), kk * rate]
    outs.append(_bdot(jax.nn.sigmoid(xg), gup))
    return tuple(outs)


def _prep_args(tm, ps_tile, pv_ref, mat_refs, ones_ref):
    vecs = [jnp.broadcast_to(pv_ref[j:j + 1, :], (tm, D_RWKV)) for j in range(7)]
    return [ps_tile] + vecs + [m[...] for m in mat_refs] + [ones_ref[...]]


_PREP_MAT_SHAPES = ((128, D_RWKV),) * 4 + ((D_GATE_PAD, D_RWKV),)


def _shift_prep_fwd(p, mu, pvec, mats, ones_blocks, tm, seq):
    t, w = p.shape
    prev_spec, next_spec = _halo_specs(tm, w, t)

    def body(p_ref, hp_ref, hn_ref, mu_ref, pv_ref, m0, m1, m2, m3, m4, ones_ref, ps_ref, *out_refs):
        has_prev, has_next = _edge_flags(tm, seq)
        xv = p_ref[...]
        down, up = _shifted(xv, hp_ref[GROUP - 1:GROUP, :] * has_prev, hn_ref[0:1, :] * has_next)
        ps_tile = xv + mu_ref[0:1, :] * (0.5 * (down + up) - xv)
        ps_ref[...] = ps_tile
        outs = _prep_math(*_prep_args(tm, ps_tile, pv_ref, (m0, m1, m2, m3, m4), ones_ref))
        for o_ref, val in zip(out_refs, outs[2:]):
            o_ref[...] = val

    return pl.pallas_call(
        body, name="shift_prep_fwd", grid=(t // tm,),
        in_specs=[_row(tm, w), prev_spec, next_spec, _fixed((GROUP, w)), _fixed((8, D_RWKV))]
        + [_fixed(s) for s in _PREP_MAT_SHAPES] + [_fixed((D_RWKV, D_RWKV))],
        out_specs=[_row(tm, w)] + [_row(tm, D_RWKV)] * 8, out_shape=[_sds((t, w))] + [_sds((t, D_RWKV))] * 8,
        compiler_params=_cp("parallel"))(p, p, p, mu, pvec, *mats, ones_blocks)


def _prep_bwd(ps, pvec, mats, ones_blocks, cts, tm):
    t = ps.shape[0]
    counts = [len(c) for c in cts]
    flat = [a for c in cts for a in c]

    def body(ps_ref, pv_ref, m0, m1, m2, m3, m4, ones_ref, *refs):
        ct_refs = refs[:len(flat)]
        q_ref, dpv_ref = refs[len(flat)], refs[len(flat) + 1]
        dmat_refs = refs[len(flat) + 2:]
        args = _prep_args(tm, ps_ref[...], pv_ref, (m0, m1, m2, m3, m4), ones_ref)
        _, vjp = jax.vjp(lambda *a: _prep_math(*a, args[-1]), *args[:-1])
        ct_vals, pos = [], 0
        for n in counts:
            val = ct_refs[pos][...]
            for extra in ct_refs[pos + 1:pos + n]:
                val = val + extra[...]
            ct_vals.append(val)
            pos += n
        grads = vjp(tuple(ct_vals))
        q_ref[...] = grads[0]

        @pl.when(pl.program_id(0) == 0)
        def _():
            dpv_ref[...] = jnp.zeros_like(dpv_ref)
            for d_ref in dmat_refs:
                d_ref[...] = jnp.zeros_like(d_ref)

        for j in range(7):
            dpv_ref[j:j + 1, :] += jnp.sum(grads[1 + j], axis=0, keepdims=True)
        for d_ref, gm in zip(dmat_refs, grads[8:13]):
            d_ref[...] += gm

    return pl.pallas_call(
        body, name="prep_bwd", grid=(t // tm,),
        in_specs=[_row(tm, D_SHIFT_PAD), _fixed((8, D_RWKV))] + [_fixed(s) for s in _PREP_MAT_SHAPES]
        + [_fixed((D_RWKV, D_RWKV))] + [_row(tm, D_RWKV)] * len(flat),
        out_specs=[_row(tm, D_SHIFT_PAD), _fixed((8, D_RWKV))] + [_fixed(s) for s in _PREP_MAT_SHAPES],
        out_shape=[_sds((t, D_SHIFT_PAD)), _sds((8, D_RWKV))] + [_sds(s) for s in _PREP_MAT_SHAPES],
        compiler_params=_cp("arbitrary"))(ps, pvec, *mats, ones_blocks, *flat)


def _pair_ones():
    h = jnp.arange(2 * HEAD) // HEAD
    block = (h[:, None] == h[None, :]).astype(BF16)
    return jnp.concatenate([block, block], axis=0)


def _diag_mask():
    lane = lax.broadcasted_iota(jnp.int32, (HEAD, 2 * HEAD), 1)
    sub = lax.broadcasted_iota(jnp.int32, (HEAD, 2 * HEAD), 0)
    return jnp.where((lane & (HEAD - 1)) == sub, 1.0, 0.0).astype(F32)


def _to_row(cols, dmask):
    return jnp.sum(cols * dmask, axis=0, keepdims=True)


def _seg_many(exact, rounded, ones_pair):
    out_exact, out_rounded = [], []
    if exact:
        parts = []
        for x in exact:
            hi = x.astype(BF16)
            parts.append(jnp.concatenate([hi, (x - hi.astype(F32)).astype(BF16)], axis=1))
        res = jnp.dot(jnp.concatenate(parts, axis=0), ones_pair, preferred_element_type=F32)
        out_exact = [res[HEAD * c:HEAD * (c + 1)] for c in range(len(exact))]
    if rounded:
        res = jnp.dot(jnp.concatenate([x.astype(BF16) for x in rounded], axis=0), ones_pair[0:2 * HEAD],
                      preferred_element_type=F32)
        out_rounded = [res[HEAD * c:HEAD * (c + 1)] for c in range(len(rounded))]
    return out_exact, out_rounded


N_CHAIN = 2 * N_PAIR


def _chain(c):
    d, p = divmod(c, N_PAIR)
    return d, slice(2 * HEAD * p, 2 * HEAD * (p + 1))


def _scan_specs(n_chunks, col_blocks, fwd_chunk, bwd_chunk):
    def spec(chunk_of, col):
        return pl.BlockSpec((SCAN_CHUNK, D_RWKV), lambda b, g: (b * n_chunks + chunk_of(g), col))
    return [spec(fwd_chunk, c) for c in col_blocks] + [spec(bwd_chunk, c) for c in col_blocks]


def _scan_fwd(ps, kk, dirs, batch, seq):
    t = batch * seq
    n = seq // SCAN_CHUNK
    groups = SCAN_CHUNK // GROUP
    up = lambda g: g
    down = lambda g: n - 1 - g
    col_blocks = (0, 2, 0, 0, 0, 0)

    def body(*refs):
        dir_refs = (refs[0:6], refs[6:12])
        ones_ref = refs[12]
        y_refs, hist_refs, st_ref = refs[13:15], refs[15:17], refs[17]

        @pl.when(pl.program_id(1) == 0)
        def _():
            st_ref[...] = jnp.zeros_like(st_ref)

        ones_pair = ones_ref[...]
        dmask = _diag_mask()
        dmask_b = dmask.astype(BF16)
        sub8 = lax.broadcasted_iota(jnp.int32, (GROUP, 2 * HEAD), 0)

        def group(gi, carry):
            off = (pl.multiple_of(gi * GROUP, GROUP), pl.multiple_of((groups - 1 - gi) * GROUP, GROUP))
            loaded = [tuple(ref[pl.ds(off[d], GROUP), :] for ref in dir_refs[d]) for d in range(2)]
            states = list(carry)
            y_acc = [jnp.zeros((GROUP, 2 * HEAD), F32) for _ in range(N_CHAIN)]
            for step in range(GROUP):
                rows, idx = [], []
                for c in range(N_CHAIN):
                    d, lanes = _chain(c)
                    i = step if d == 0 else GROUP - 1 - step
                    idx.append(i)
                    rows.append(tuple(x8[i:i + 1, lanes] for x8 in loaded[d]))
                    hist_refs[d][c % N_PAIR, gi * GROUP + step] = states[c]
                _, v_cols = _seg_many([], [dmask_b * rows[c][1].astype(BF16) for c in range(N_CHAIN)], ones_pair)
                sas, _ = _seg_many([states[c] * rows[c][2] for c in range(N_CHAIN)], [], ones_pair)
                for c in range(N_CHAIN):
                    _, _, _, w_row, kd_row, b_row = rows[c]
                    states[c] = states[c] * w_row - sas[c] * b_row + v_cols[c] * kd_row
                _, ys = _seg_many([], [states[c] * rows[c][0] for c in range(N_CHAIN)], ones_pair)
                for c in range(N_CHAIN):
                    y_acc[c] = jnp.where(sub8 == idx[c], _to_row(ys[c], dmask), y_acc[c])
            for c in range(N_CHAIN):
                d, lanes = _chain(c)
                y_refs[d][pl.ds(off[d], GROUP), lanes] = y_acc[c]
            return tuple(states)

        final = lax.fori_loop(0, groups, group, tuple(st_ref[c] for c in range(N_CHAIN)))
        for c in range(N_CHAIN):
            st_ref[c] = final[c]
            hist_refs[c // N_PAIR][c % N_PAIR, SCAN_CHUNK] = final[c]

    y_spec_f = pl.BlockSpec((SCAN_CHUNK, D_RWKV), lambda b, g: (b * n + up(g), 0))
    y_spec_b = pl.BlockSpec((SCAN_CHUNK, D_RWKV), lambda b, g: (b * n + down(g), 0))
    hist_shape = (batch, n, N_PAIR, SCAN_CHUNK + 1, HEAD, 2 * HEAD)
    hist_block = (None, None, N_PAIR, SCAN_CHUNK + 1, HEAD, 2 * HEAD)
    hist_spec_f = pl.BlockSpec(hist_block, lambda b, g: (b, up(g), 0, 0, 0, 0))
    hist_spec_b = pl.BlockSpec(hist_block, lambda b, g: (b, down(g), 0, 0, 0, 0))
    ones_spec = pl.BlockSpec((4 * HEAD, 2 * HEAD), lambda b, g: (0, 0))
    (wf, kdf, bf), (wb, kdb, bb) = dirs
    return pl.pallas_call(
        body, name="wkv_fwd", grid=(batch, n),
        in_specs=_scan_specs(n, col_blocks, up, down) + [ones_spec],
        out_specs=[y_spec_f, y_spec_b, hist_spec_f, hist_spec_b],
        out_shape=[_sds((t, D_RWKV)), _sds((t, D_RWKV)), _sds(hist_shape), _sds(hist_shape)],
        scratch_shapes=[pltpu.VMEM((N_CHAIN, HEAD, 2 * HEAD), F32)],
        compiler_params=_cp("parallel", "arbitrary"),
    )(ps, ps, kk, wf, kdf, bf, ps, ps, kk, wb, kdb, bb, _pair_ones())


def _scan_bwd(ps, kk, dirs, dy, hist_f, hist_b, batch, seq):
    t = batch * seq
    n = seq // SCAN_CHUNK
    groups = SCAN_CHUNK // GROUP
    fwd_chunk = lambda g: n - 1 - g
    bwd_chunk = lambda g: g
    col_blocks = (0, 2, 0, 0, 0, 0, 0)

    def undo_group(dir_refs, out_refs, hist_refs, gi, d_states, ones_pair, dmask, sub8):
        d_states = list(d_states)
        loaded, blocks = [], []
        for d in range(2):
            blk = groups - 1 - gi if d == 0 else gi
            blocks.append(pl.ds(pl.multiple_of(blk * GROUP, GROUP), GROUP))
            r8, v8, kk8, w8, kd8, b8, dy8 = (ref[blocks[d], :] for ref in dir_refs[d])
            loaded.append((r8, v8, kk8, w8, kd8, -b8, dy8))
        acc = [[jnp.zeros((GROUP, 2 * HEAD), F32) for _ in range(6)] for _ in range(N_CHAIN)]
        for step in range(GROUP):
            rows, idx, before, after = [], [], [], []
            for c in range(N_CHAIN):
                d, lanes = _chain(c)
                i = GROUP - 1 - step if d == 0 else step
                q = (groups - 1 - gi) * GROUP + i if d == 0 else SCAN_CHUNK - 1 - (gi * GROUP + i)
                idx.append(i)
                rows.append(tuple(x8[i:i + 1, lanes] for x8 in loaded[d]))
                before.append(hist_refs[d][c % N_PAIR, q])
                after.append(hist_refs[d][c % N_PAIR, q + 1])
            _, cols = _seg_many([], [dmask.astype(BF16) * rows[c][j].astype(BF16) for c in range(N_CHAIN) for j in (1, 6)],
                                ones_pair)
            v_cols, dy_cols = cols[0::2], cols[1::2]
            d_now = [d_states[c] + dy_cols[c] * rows[c][0] for c in range(N_CHAIN)]
            d_sas, _ = _seg_many([d_now[c] * rows[c][5] for c in range(N_CHAIN)], [], ones_pair)
            _, others = _seg_many(
                [], [x for c in range(N_CHAIN) for x in (before[c] * rows[c][2], d_now[c] * rows[c][4])], ones_pair)
            for c in range(N_CHAIN):
                sa, d_sa, dv_cols = others[2 * c], d_sas[c], others[2 * c + 1]
                rows_out = (
                    jnp.sum(after[c] * dy_cols[c], axis=0, keepdims=True),
                    jnp.sum(d_now[c] * before[c], axis=0, keepdims=True),
                    jnp.sum(d_now[c] * v_cols[c], axis=0, keepdims=True),
                    _to_row(dv_cols, dmask),
                    jnp.sum(before[c] * d_sa, axis=0, keepdims=True),
                    -jnp.sum(d_now[c] * sa, axis=0, keepdims=True),
                )
                acc[c] = [jnp.where(sub8 == idx[c], val, a) for val, a in zip(rows_out, acc[c])]
                d_states[c] = d_now[c] * rows[c][3] + d_sa * rows[c][2]
        for c in range(N_CHAIN):
            d, lanes = _chain(c)
            for o_ref, val in zip(out_refs[d], acc[c]):
                o_ref[blocks[d], lanes] = val
        return tuple(d_states)

    def body(*refs):
        dir_refs = (refs[0:7], refs[7:14])
        hist_refs, ones_ref = refs[14:16], refs[16]
        out_refs = (refs[17:23], refs[23:29])
        dst_ref = refs[29]

        @pl.when(pl.program_id(1) == 0)
        def _():
            dst_ref[...] = jnp.zeros_like(dst_ref)

        ones_pair = ones_ref[...]
        dmask = _diag_mask()
        sub8 = lax.broadcasted_iota(jnp.int32, (GROUP, 2 * HEAD), 0)

        def group(gi, carry):
            return undo_group(dir_refs, out_refs, hist_refs, gi, carry, ones_pair, dmask, sub8)

        final = lax.fori_loop(0, groups, group, tuple(dst_ref[c] for c in range(N_CHAIN)))
        for c in range(N_CHAIN):
            dst_ref[c] = final[c]

    blk = (SCAN_CHUNK, D_RWKV)
    out_f = pl.BlockSpec(blk, lambda b, g: (b * n + fwd_chunk(g), 0))
    out_b = pl.BlockSpec(blk, lambda b, g: (b * n + bwd_chunk(g), 0))
    hist_block = (None, None, N_PAIR, SCAN_CHUNK + 1, HEAD, 2 * HEAD)
    hist_spec_f = pl.BlockSpec(hist_block, lambda b, g: (b, fwd_chunk(g), 0, 0, 0, 0))
    hist_spec_b = pl.BlockSpec(hist_block, lambda b, g: (b, bwd_chunk(g), 0, 0, 0, 0))
    ones_spec = pl.BlockSpec((4 * HEAD, 2 * HEAD), lambda b, g: (0, 0))
    (wf, kdf, bf), (wb, kdb, bb) = dirs
    outs = pl.pallas_call(
        body, name="wkv_bwd", grid=(batch, n),
        in_specs=_scan_specs(n, col_blocks, fwd_chunk, bwd_chunk) + [hist_spec_f, hist_spec_b, ones_spec],
        out_specs=[out_f] * 6 + [out_b] * 6,
        out_shape=[_sds((t, D_RWKV))] * 12,
        scratch_shapes=[pltpu.VMEM((N_CHAIN, HEAD, 2 * HEAD), F32)],
        compiler_params=_cp("parallel", "arbitrary"),
    )(ps, ps, kk, wf, kdf, bf, dy, ps, ps, kk, wb, kdb, bb, dy, hist_f, hist_b, _pair_ones())
    return outs[0:6], outs[6:12]


def _post_math(y, r, kd_f, kd_b, v, gate, gn_w, gn_b, rk_f, rk_b, ones_blocks):
    mean = _seg(y, ones_blocks) * (1.0 / HEAD)
    yc = y - mean
    var = _seg(yc * yc, ones_blocks) * (1.0 / HEAD)
    yn = yc * lax.rsqrt(var + GN_EPS) * gn_w + gn_b
    bonus = _seg(r * kd_f * rk_f, ones_blocks) * v + _seg(r * kd_b * rk_b, ones_blocks) * v
    return (yn + bonus) * gate


def _conv_parts(pc, halo_prev, halo_next, has_prev, has_next):
    gate_b, gate_c, hid = pc[:, 0:512], pc[:, 512:1024], pc[:, 1024:1536]
    u = gate_c * hid
    u_prev_row = halo_prev[GROUP - 1:GROUP, 512:1024] * halo_prev[GROUP - 1:GROUP, 1024:1536] * has_prev
    u_next_row = halo_next[0:1, 512:1024] * halo_next[0:1, 1024:1536] * has_next
    u_down, u_up = _shifted(u, u_prev_row, u_next_row)
    return gate_b, gate_c, hid, u, u_down, u_up


def _post_specs(tm, t):
    pc_prev, pc_next = _halo_specs(tm, D_CONV3, t)
    col = lambda c: pl.BlockSpec((tm, D_RWKV), lambda i: (i, c))
    return ([col(0), col(0), col(0), col(0), col(0), col(2), col(0), _row(tm, D_CONV3), pc_prev, pc_next,
             _fixed((8, D_RWKV)), _fixed((D_RWKV, D_RWKV))])


def _post_fwd(y_f, y_b, ps, kd_f, kd_b, gate, pc, qvec, ones_blocks, tm, seq):
    t = ps.shape[0]

    def body(yf_ref, yb_ref, r_ref, kdf_ref, kdb_ref, v_ref, g_ref, pc_ref, hp_ref, hn_ref, qv_ref, ones_ref,
             o_ref, ot_ref):
        has_prev, has_next = _edge_flags(tm, seq)
        vec = [jnp.broadcast_to(qv_ref[j:j + 1, :], (tm, D_RWKV)) for j in range(7)]
        o_rwkv = _post_math(yf_ref[...] + yb_ref[...], r_ref[...], kdf_ref[...], kdb_ref[...], v_ref[...],
                            g_ref[...], vec[0], vec[1], vec[2], vec[3], ones_ref[...])
        gate_b, _, _, u, u_down, u_up = _conv_parts(pc_ref[...], hp_ref[...], hn_ref[...], has_prev, has_next)
        o_conv = gate_b * (vec[4] * u_down + vec[5] * u + vec[6] * u_up)
        for half, val in enumerate((o_rwkv, o_conv)):
            o_ref[:, D_RWKV * half:D_RWKV * (half + 1)] = val.astype(BF16)
            ot_ref[D_RWKV * half:D_RWKV * (half + 1), :] = jnp.transpose(val).astype(BF16)

    return pl.pallas_call(
        body, name="post_fwd", grid=(t // tm,), in_specs=_post_specs(tm, t),
        out_specs=[_row(tm, D_MODEL), _col(tm, D_MODEL)],
        out_shape=[_sds((t, D_MODEL), BF16), _sds((D_MODEL, t), BF16)], compiler_params=_cp("parallel"),
    )(y_f, y_b, ps, kd_f, kd_b, ps, gate, pc, pc, pc, qvec, ones_blocks)


def _post_bwd(d_out, y_f, y_b, ps, kd_f, kd_b, gate, pc, qvec, ones_blocks, tm, seq):
    t = ps.shape[0]
    do_prev, do_next = _halo_specs(tm, D_MODEL, t)

    def body(do_ref, dop_ref, don_ref, yf_ref, yb_ref, r_ref, kdf_ref, kdb_ref, v_ref, g_ref, pc_ref, hp_ref,
             hn_ref, qv_ref, ones_ref, dy_ref, dr_ref, dkdf_ref, dkdb_ref, dv_ref, dg_ref, dpc_ref, dqv_ref):
        has_prev, has_next = _edge_flags(tm, seq)
        vec = [jnp.broadcast_to(qv_ref[j:j + 1, :], (tm, D_RWKV)) for j in range(7)]
        ones_v = ones_ref[...]
        args = (yf_ref[...] + yb_ref[...], r_ref[...], kdf_ref[...], kdb_ref[...], v_ref[...], g_ref[...],
                vec[0], vec[1], vec[2], vec[3])
        _, vjp = jax.vjp(lambda *a: _post_math(*a, ones_v), *args)
        grads = vjp(do_ref[:, 0:D_RWKV])
        for o_ref, gval in zip((dy_ref, dr_ref, dkdf_ref, dkdb_ref, dv_ref, dg_ref), grads[0:6]):
            o_ref[...] = gval

        hp, hn = hp_ref[...], hn_ref[...]
        gate_b, gate_c, hid, u, u_down, u_up = _conv_parts(pc_ref[...], hp, hn, has_prev, has_next)
        d_oc = do_ref[:, D_RWKV:2 * D_RWKV]
        d_cu = d_oc * gate_b
        d_cu_prev = dop_ref[GROUP - 1:GROUP, D_RWKV:2 * D_RWKV] * hp[GROUP - 1:GROUP, 0:512] * has_prev
        d_cu_next = don_ref[0:1, D_RWKV:2 * D_RWKV] * hn[0:1, 0:512] * has_next
        d_cu_down, d_cu_up = _shifted(d_cu, d_cu_prev, d_cu_next)
        d_u = vec[5] * d_cu + vec[4] * d_cu_up + vec[6] * d_cu_down
        dpc_ref[:, 0:512] = (d_oc * (vec[4] * u_down + vec[5] * u + vec[6] * u_up)).astype(BF16)
        dpc_ref[:, 512:1024] = (d_u * hid).astype(BF16)
        dpc_ref[:, 1024:1536] = (d_u * gate_c).astype(BF16)

        @pl.when(pl.program_id(0) == 0)
        def _():
            dqv_ref[...] = jnp.zeros_like(dqv_ref)

        vec_grads = list(grads[6:10]) + [d_cu * u_down, d_cu * u, d_cu * u_up]
        for j, gval in enumerate(vec_grads):
            dqv_ref[j:j + 1, :] += jnp.sum(gval, axis=0, keepdims=True)

    return pl.pallas_call(
        body, name="post_bwd", grid=(t // tm,),
        in_specs=[_row(tm, D_MODEL), do_prev, do_next] + _post_specs(tm, t),
        out_specs=[_row(tm, D_RWKV)] * 6 + [_row(tm, D_CONV3), _fixed((8, D_RWKV))],
        out_shape=[_sds((t, D_RWKV))] * 6 + [_sds((t, D_CONV3), BF16), _sds((8, D_RWKV))],
        compiler_params=_cp("arbitrary"),
    )(d_out, d_out, d_out, y_f, y_b, ps, kd_f, kd_b, ps, gate, pc, pc, pc, qvec, ones_blocks)


def _adamw_math(wv, gv, mv, vv):
    m2 = ADAM_B1 * mv + (1.0 - ADAM_B1) * gv
    v2 = ADAM_B2 * vv + (1.0 - ADAM_B2) * (gv * gv)
    m_hat = m2 / (1.0 - ADAM_B1 ** ADAM_STEP)
    v_hat = v2 / (1.0 - ADAM_B2 ** ADAM_STEP)
    return -ADAM_LR * (m_hat / (jnp.sqrt(v_hat) + ADAM_EPS) + ADAM_WD * wv), m2, v2


def _adamw_small(items):
    n = len(items)

    def body(*refs):
        ins, outs = refs[:4 * n], refs[4 * n:]
        for k in range(n):
            w_ref, g_ref, m_ref, v_ref = ins[4 * k:4 * k + 4]
            for o_ref, val in zip(outs[3 * k:3 * k + 3], _adamw_math(w_ref[...], g_ref[...], m_ref[...], v_ref[...])):
                o_ref[...] = val

    flat = [a for item in items for a in item]
    outs = pl.pallas_call(
        body, name="adamw_small", out_shape=[_sds(item[0].shape) for item in items for _ in range(3)],
        compiler_params=_cp())(*flat)
    return [tuple(outs[3 * k:3 * k + 3]) for k in range(n)]


def _adamw(w, g, m, v, name):
    r, c = w.shape[-2:]
    tr = _tile(r, 256, 8)
    if w.ndim == 3:
        spec = pl.BlockSpec((None, tr, c), lambda i: (0, i, 0))
    else:
        spec = pl.BlockSpec((tr, c), lambda i: (i, 0))

    def body(w_ref, g_ref, m_ref, v_ref, d_ref, nm_ref, nv_ref):
        d_ref[...], nm_ref[...], nv_ref[...] = _adamw_math(w_ref[...], g_ref[...], m_ref[...], v_ref[...])

    return pl.pallas_call(
        body, name=name, grid=(r // tr,), in_specs=[spec] * 4, out_specs=[spec] * 3,
        out_shape=[_sds(w.shape)] * 3, compiler_params=_cp("parallel"))(w, g, m, v)


_ANY = pl.BlockSpec(memory_space=pl.ANY)


def _place():
    return lax.axis_index("x"), lax.axis_index("y"), lax.axis_index("c")


def _other_chips(x, y):
    return [(1 - x, y), (x, 1 - y), (1 - x, 1 - y)]


def _remote(src, dst, send_sems, recv_sems, k, to):
    return pltpu.make_async_remote_copy(src_ref=src, dst_ref=dst, send_sem=send_sems.at[k],
                                        recv_sem=recv_sems.at[k], device_id=to, device_id_type=MESH)


def _gather_weights(pack):
    rows, width = pack.shape
    half = rows // 2

    def body(x_ref, out_ref, send_sems, recv_sems):
        x, y, c = _place()
        sibling = (x, y, 1 - c)
        chips = _other_chips(x, y)

        def block(chip, part):
            return out_ref.at[2 * chip[0] + chip[1], pl.ds(part * half, half), :]

        first = [_remote(x_ref.at[pl.ds(c * half, half), :], block((x, y), c), send_sems, recv_sems, j, (*chip, c))
                 for j, chip in enumerate(chips)]
        for cp in first:
            cp.start()
        passed = [_remote(block(chip, c), block(chip, c), send_sems, recv_sems, 3 + j, sibling)
                  for j, chip in enumerate(chips)]
        for j, chip in enumerate(chips):
            _remote(block(chip, c), block(chip, c), send_sems, recv_sems, j, sibling).wait_recv()
            passed[j].start()
        for j, chip in enumerate(chips):
            _remote(block(chip, 1 - c), block(chip, 1 - c), send_sems, recv_sems, 3 + j, sibling).wait_recv()
        for cp in first + passed:
            cp.wait_send()

    return pl.pallas_call(
        body, name="gather_weights", in_specs=[_ANY], out_specs=_ANY,
        out_shape=_sds((N_SHARD, rows, width), pack.dtype),
        scratch_shapes=[pltpu.SemaphoreType.DMA((6,)), pltpu.SemaphoreType.DMA((6,))],
    )(pack)


_HBM = pl.BlockSpec(memory_space=pltpu.HBM)
_SEMS = pl.BlockSpec(memory_space=pltpu.SEMAPHORE)
_DATAFLOW = pltpu.SideEffectType.DATAFLOW_SIDE_EFFECTING


def _fetch_start(pack, after):
    def body(x_ref, land_ref, after_ref, send_sems, recv_sems, x_thru, land_thru, token):
        x, y, c = _place()
        for j, chip in enumerate(_other_chips(x, y)):
            _remote(x_ref, land_ref.at[2 * x + y], send_sems, recv_sems, j, (*chip, c)).start()
        token[...] = jnp.zeros_like(token)

    land = lax.empty((N_SHARD,) + pack.shape, pack.dtype)
    return pl.pallas_call(
        body, name="fetch_ffn_start",
        out_shape=(pltpu.SemaphoreType.DMA((3,)), pltpu.SemaphoreType.DMA((3,)), pltpu.HBM(pack.shape, pack.dtype),
                   pltpu.HBM(land.shape, land.dtype), _sds((8, 128))),
        in_specs=(_HBM, _HBM, _ANY), out_specs=(_SEMS, _SEMS, _HBM, _HBM, pl.BlockSpec(memory_space=pltpu.VMEM)),
        input_output_aliases={0: 2, 1: 3}, compiler_params=pltpu.CompilerParams(has_side_effects=_DATAFLOW),
    )(pltpu.with_memory_space_constraint(pack, pltpu.HBM), pltpu.with_memory_space_constraint(land, pltpu.HBM), after)


def _fetch_wait(send_sems, recv_sems, pack_thru, land_thru, after):
    def body(x_ref, land_ref, send_sems, recv_sems, after_ref, x_dead, got_ref):
        x, y, c = _place()
        for j, chip in enumerate(_other_chips(x, y)):
            cp = _remote(x_ref, land_ref.at[2 * chip[0] + chip[1]], send_sems, recv_sems, j, (*chip, c))
            cp.wait_send()
            cp.wait_recv()

    return pl.pallas_call(
        body, name="fetch_ffn_wait",
        out_shape=(pltpu.HBM(pack_thru.shape, pack_thru.dtype), pltpu.HBM(land_thru.shape, land_thru.dtype)),
        in_specs=(_HBM, _HBM, _SEMS, _SEMS, _ANY), out_specs=(_HBM, _HBM), input_output_aliases={0: 0, 1: 1},
        compiler_params=pltpu.CompilerParams(has_side_effects=_DATAFLOW),
    )(pack_thru, land_thru, send_sems, recv_sems, after)[1]


def _swap_with_sibling(block, name):
    def body(x_ref, out_ref, send_sems, recv_sems):
        x, y, c = _place()
        cp = _remote(x_ref, out_ref, send_sems, recv_sems, 0, (x, y, 1 - c))
        cp.start()
        cp.wait()

    return pl.pallas_call(
        body, name=name, in_specs=[_ANY], out_specs=_ANY, out_shape=_sds(block.shape, block.dtype),
        scratch_shapes=[pltpu.SemaphoreType.DMA((1,)), pltpu.SemaphoreType.DMA((1,))],
    )(block)


def _swap_other_half(packed, tag):
    slots, rows, width = packed.shape
    half = rows // 2

    def body(x_ref, out_ref, send_sems, recv_sems):
        x, y, c = _place()
        cp = _remote(x_ref.at[:, pl.ds((1 - c) * half, half), :], out_ref, send_sems, recv_sems, 0, (x, y, 1 - c))
        cp.start()
        cp.wait()

    return pl.pallas_call(
        body, name="swap_halves_" + tag, in_specs=[_ANY], out_specs=_ANY, out_shape=_sds((slots, half, width)),
        scratch_shapes=[pltpu.SemaphoreType.DMA((1,)), pltpu.SemaphoreType.DMA((1,))],
    )(packed)


def _add_halves(packed, got, c, tag):
    slots, rows, width = packed.shape
    half = rows // 2
    tr = _tile(half, 408, 16)
    per = half // tr
    block = (None, tr, width)

    def body(c_ref, mine_ref, got_ref, sum_ref, sum16_ref):
        acc = mine_ref[...] + got_ref[...]
        sum_ref[...] = acc
        sum16_ref[...] = acc.astype(BF16)

    plain = pl.BlockSpec(block, lambda s, i, c_ref: (s, i, 0))
    grid_spec = pltpu.PrefetchScalarGridSpec(
        num_scalar_prefetch=1, grid=(slots, per),
        in_specs=[pl.BlockSpec(block, lambda s, i, c_ref: (s, c_ref[0] * per + i, 0)), plain],
        out_specs=[plain, plain])
    return pl.pallas_call(
        body, name="add_halves_" + tag, grid_spec=grid_spec,
        out_shape=[_sds((slots, half, width)), _sds((slots, half, width), BF16)],
        compiler_params=_cp("parallel", "parallel"))(c.reshape(1).astype(jnp.int32), packed, got)


def _add_quarters(chip_sum, others, chip, tag):
    _, rows, width = chip_sum.shape
    tr = _tile(rows, 408, 16)

    def body(chip_ref, own_ref, others_ref, o_ref):
        acc = own_ref[...]
        for j in range(3):
            acc = acc + others_ref[j].astype(F32)
        o_ref[...] = acc

    grid_spec = pltpu.PrefetchScalarGridSpec(
        num_scalar_prefetch=1, grid=(rows // tr,),
        in_specs=[pl.BlockSpec((None, tr, width), lambda i, chip_ref: (chip_ref[0], i, 0)),
                  pl.BlockSpec((3, tr, width), lambda i, chip_ref: (0, i, 0))],
        out_specs=pl.BlockSpec((tr, width), lambda i, chip_ref: (i, 0)))
    return pl.pallas_call(
        body, name="add_quarters_" + tag, grid_spec=grid_spec, out_shape=_sds((rows, width)),
        compiler_params=_cp("parallel"))(chip.reshape(1).astype(jnp.int32), chip_sum, others)


def _exchange_start(parts, tag):
    _, rows, width = parts.shape

    def body(x_ref, land_ref, send_sems, recv_sems, x_thru, land_thru, token):
        x, y, c = _place()
        for j, chip in enumerate(_other_chips(x, y)):
            _remote(x_ref.at[2 * chip[0] + chip[1]], land_ref.at[j], send_sems, recv_sems, j, (*chip, c)).start()
        token[...] = jnp.zeros_like(token)

    land = lax.empty((3, rows, width), parts.dtype)
    return pl.pallas_call(
        body, name="exchange_" + tag + "_start",
        out_shape=(pltpu.SemaphoreType.DMA((3,)), pltpu.SemaphoreType.DMA((3,)), pltpu.HBM(parts.shape, parts.dtype),
                   pltpu.HBM(land.shape, land.dtype), _sds((8, 128))),
        in_specs=(_HBM, _HBM), out_specs=(_SEMS, _SEMS, _HBM, _HBM, pl.BlockSpec(memory_space=pltpu.VMEM)),
        input_output_aliases={0: 2, 1: 3}, compiler_params=pltpu.CompilerParams(has_side_effects=_DATAFLOW),
    )(pltpu.with_memory_space_constraint(parts, pltpu.HBM), pltpu.with_memory_space_constraint(land, pltpu.HBM))


def _exchange_wait(send_sems, recv_sems, parts_thru, land_thru, after, tag):
    def body(x_ref, land_ref, send_sems, recv_sems, after_ref, x_dead, got_ref):
        x, y, c = _place()
        for j, chip in enumerate(_other_chips(x, y)):
            cp = _remote(x_ref.at[2 * chip[0] + chip[1]], land_ref.at[j], send_sems, recv_sems, j, (*chip, c))
            cp.wait_send()
            cp.wait_recv()

    return pl.pallas_call(
        body, name="exchange_" + tag + "_wait",
        out_shape=(pltpu.HBM(parts_thru.shape, parts_thru.dtype), pltpu.HBM(land_thru.shape, land_thru.dtype)),
        in_specs=(_HBM, _HBM, _SEMS, _SEMS, _ANY), out_specs=(_HBM, _HBM), input_output_aliases={0: 0, 1: 1},
        compiler_params=pltpu.CompilerParams(has_side_effects=_DATAFLOW),
    )(parts_thru, land_thru, send_sems, recv_sems, after)[1]


def _allreduce_small(vec):
    rows, width = vec.shape
    vmem = pl.BlockSpec(memory_space=pltpu.VMEM)

    def body(x_ref, o_ref, buf_ref, send_sems, recv_sems):
        x, y, c = _place()
        me = 4 * x + 2 * y + c
        buf_ref[me] = x_ref[...]
        copies = []
        for k in range(1, N_DEV):
            peer = (x ^ ((k >> 2) & 1), y ^ ((k >> 1) & 1), c ^ (k & 1))
            copies.append(_remote(x_ref, buf_ref.at[me], send_sems, recv_sems, k - 1, peer))
        for cp in copies:
            cp.start()
        for k in range(1, N_DEV):
            _remote(x_ref, buf_ref.at[me ^ k], send_sems, recv_sems, k - 1, (x, y, c)).wait_recv()
        for cp in copies:
            cp.wait_send()
        total = buf_ref[0]
        for d in range(1, N_DEV):
            total = total + buf_ref[d]
        o_ref[...] = total

    return pl.pallas_call(
        body, name="allreduce_small", in_specs=[vmem], out_specs=vmem, out_shape=_sds((rows, width)),
        scratch_shapes=[pltpu.VMEM((N_DEV, rows, width), F32), pltpu.SemaphoreType.DMA((N_DEV - 1,)),
                        pltpu.SemaphoreType.DMA((N_DEV - 1,))],
    )(vec)


def _rows1024(a):
    return a.reshape(-1, 1024)


def _pad_rows(a, rows):
    return jnp.concatenate([a, jnp.zeros((rows - a.shape[0], a.shape[1]), a.dtype)], axis=0)


_TRANSPOSED = ("w_in", "w_gate", "w_up")
_SMALL_SHARDED = ("w_up_f", "w_up_b", "a_up_f", "a_up_b", "g_up")
_BIG_SHARDED = ("w_in", "w_out", "w_gate", "w_up", "w_down")


def _pack_weight_shards(w):
    conv_bits = lax.bitcast_convert_type(w["conv_w"], BF16).reshape(1, -1)
    conv_row = jnp.concatenate([conv_bits, jnp.zeros((1, 1024 - conv_bits.shape[1]), BF16)], axis=1)

    def rows(name):
        a = w[name].astype(BF16)
        return a.T if name in _TRANSPOSED else _rows1024(a)

    early = _pad_rows(jnp.concatenate([rows(name) for name, _ in _EARLY_ROWS[:-1]] + [conv_row], axis=0), EARLY_R)
    return early, jnp.concatenate([rows(name) for name, _ in _FFN_ROWS], axis=0)


def _split_rows(gathered, layout):
    out, row = {}, 0
    for name, n in layout:
        out[name] = gathered[:, row:row + n]
        row += n
    return out


def _unpack_early(gathered):
    out = _split_rows(gathered, _EARLY_ROWS)
    cols = lambda a, k: jnp.concatenate([a[s].reshape(k, -1) for s in range(N_SHARD)], axis=1)
    conv = lax.bitcast_convert_type(out["conv_w"][:, 0, :768].reshape(N_SHARD, 3, 128, 2), F32)
    full = dict(w_in=out["w_in"].reshape(-1, 1024).T, w_out=out["w_out"].reshape(D_MODEL, D_MODEL),
                conv_w=jnp.concatenate([conv[s] for s in range(N_SHARD)], axis=1))
    full.update({name: cols(out[name], D_GATE if name == "g_up" else D_LORA) for name in _SMALL_SHARDED})
    return full


def _unpack_ffn(gathered):
    out = _split_rows(gathered, _FFN_ROWS)
    return dict(w_gate=out["w_gate"].reshape(-1, 1024).T, w_up=out["w_up"].reshape(-1, 1024).T,
                w_down=out["w_down"].reshape(D_FF, D_MODEL))


def _pack_grads(g, layout, rows):
    col_split = lambda a, s: a[:, s * (a.shape[1] // N_SHARD):(s + 1) * (a.shape[1] // N_SHARD)]
    row_split = lambda a, s: a[s * (a.shape[0] // N_SHARD):(s + 1) * (a.shape[0] // N_SHARD)]
    by_rows = {name: (g[name].T if name in _TRANSPOSED else g[name]) for name, _ in layout if name in _BIG_SHARDED}
    used = sum(n for _, n in layout)
    parts = []
    for s in range(N_SHARD):
        for name, _ in layout:
            if name in _BIG_SHARDED:
                parts.append(row_split(by_rows[name], s))
            elif name in _SMALL_SHARDED:
                parts.append(_rows1024(col_split(g[name], s)))
            else:
                conv = col_split(g["conv_w"], s).reshape(1, -1)
                parts.append(jnp.concatenate([conv, jnp.zeros((1, 1024 - conv.shape[1]), F32)], axis=1))
        if rows > used:
            parts.append(jnp.zeros((rows - used, 1024), F32))
    return jnp.concatenate(parts, axis=0).reshape(N_SHARD, rows, 1024)


def _unpack_grad_shard(pack, layout):
    small_shapes = {name: (D_GATE if name == "g_up" else D_LORA, 128) for name in _SMALL_SHARDED}
    out, row = {}, 0
    for name, n in layout:
        piece = pack[row:row + n]
        if name == "conv_w":
            out[name] = piece[0, :384].reshape(3, 128)
        else:
            out[name] = piece.T if name in _TRANSPOSED else piece.reshape(small_shapes.get(name, piece.shape))
        row += n
    return out


_SMALL_LAYOUT = (("norm1_w", 1024), ("mu_shift", D_SHIFT), ("w0_f", 512), ("w0_b", 512), ("a0_f", 512),
                 ("a0_b", 512), ("k_k", 512), ("k_a_f", 512), ("k_a_b", 512), ("r_k_f", 512), ("r_k_b", 512),
                 ("gn_w", 512), ("gn_b", 512), ("norm2_w", 1024), ("norm_f_w", 1024), ("loss", 1))


def _pack_small(vals):
    rows = []
    for name, n in _SMALL_LAYOUT:
        flat = vals[name].reshape(-1)
        n_rows = -(-n // 1024)
        rows.append(jnp.concatenate([flat, jnp.zeros((n_rows * 1024 - n,), F32)]).reshape(n_rows, 1024))
    return _pad_rows(jnp.concatenate(rows, axis=0), SMALL_ROWS)


def _unpack_small(pack):
    out, row = {}, 0
    for name, n in _SMALL_LAYOUT:
        n_rows = -(-n // 1024)
        out[name] = pack[row:row + n_rows].reshape(-1)[:n]
        row += n_rows
    return out


_WEIGHTS = ("norm1_w", "w_in", "mu_shift", "w_up_f", "w0_f", "w_up_b", "w0_b", "a_up_f", "a0_f", "a_up_b", "a0_b",
            "g_up", "k_k", "k_a_f", "k_a_b", "r_k_f", "r_k_b", "gn_w", "gn_b", "conv_w", "w_out", "norm2_w",
            "w_gate", "w_up", "w_down", "norm_f_w")


def _train_step(x, loss_target, w, m, v):
    batch, seq, _ = x.shape
    t = batch * seq
    tm = _tile(seq, 256, 8)
    xs = x.reshape(t, D_MODEL)
    target = loss_target.reshape(t, D_MODEL)
    vec = lambda name: w[name].reshape(1, -1)

    local = {name: w[name][0] for name, _ in _PACK_ROWS}
    c = lax.axis_index("c")
    chip = 2 * lax.axis_index("x") + lax.axis_index("y")
    early, ffn_pack = _pack_weight_shards(local)
    early_all = lax.dynamic_update_slice(_gather_weights(early), early[None], (chip, 0, 0))
    ffn_send, ffn_recv, ffn_pack, ffn_land, token = _fetch_start(ffn_pack, early_all)
    full = _unpack_early(early_all)
    w_in = full["w_in"]
    w_shift = jnp.concatenate([w_in[:, :D_SHIFT], jnp.zeros((D_MODEL, D_SHIFT_PAD - D_SHIFT), BF16)], axis=1)
    w_conv = w_in[:, D_SHIFT:]
    zeros_lora = jnp.zeros((D_LORA, D_RWKV), F32)
    lora = lambda name: full[name].astype(F32)
    mats = (jnp.concatenate([lora("w_up_f"), zeros_lora]), jnp.concatenate([zeros_lora, lora("a_up_f")]),
            jnp.concatenate([lora("w_up_b"), zeros_lora]), jnp.concatenate([zeros_lora, lora("a_up_b")]),
            jnp.concatenate([lora("g_up"), jnp.zeros((D_GATE_PAD - D_GATE, D_RWKV), F32)]))
    mu = jnp.concatenate([vec("mu_shift"), jnp.zeros((1, D_SHIFT_PAD - D_SHIFT), F32)], axis=1)
    mu = jnp.broadcast_to(mu, (GROUP, D_SHIFT_PAD))
    zero_row = jnp.zeros((1, D_RWKV), F32)
    pvec = jnp.concatenate([vec("k_k"), vec("w0_f"), vec("a0_f"), vec("k_a_f"), vec("w0_b"), vec("a0_b"),
                            vec("k_a_b"), zero_row], axis=0)
    qvec = jnp.concatenate([vec("gn_w"), vec("gn_b"), vec("r_k_f"), vec("r_k_b"), full["conv_w"], zero_row], axis=0)
    ones_blocks = _head_ones()

    h1_t, p_shift, pc = _norm_in_proj(xs, vec("norm1_w") + token[0, 0], w_shift, w_conv, tm)
    ps, kk, w_f, kd_f, b_f, w_b, kd_b, b_b, gate = _shift_prep_fwd(p_shift, mu, pvec, mats, ones_blocks, tm, seq)
    dirs = ((w_f, kd_f, b_f), (w_b, kd_b, b_b))
    y_f, y_b, hist_f, hist_b = _scan_fwd(ps, kk, dirs, batch, seq)
    mixed, mixed_t = _post_fwd(y_f, y_b, ps, kd_f, kd_b, gate, pc, qvec, ones_blocks, tm, seq)
    x1, h2, h2_t = _out_proj_norm(mixed, full["w_out"], xs, vec("norm2_w"), tm)
    ffn_all = _fetch_wait(ffn_send, ffn_recv, ffn_pack, ffn_land, h2)
    full.update(_unpack_ffn(lax.dynamic_update_slice(ffn_all, ffn_pack[None], (chip, 0, 0))))
    ff_gate, ff_up, act = _ffn_in(h2, full["w_gate"], full["w_up"])
    d_x2, d_x2_t, d_norm_f, loss_part = _ffn_out_loss(act, full["w_down"], x1, w["norm_f_w"].reshape(1, -1), target, tm)

    g = {}
    g["w_down"] = _matmul(d_x2_t, act, mode="nn", name="ffn_down_dw").T
    d_gate, d_up = _ffn_in_bwd(d_x2, full["w_down"], ff_gate, ff_up)
    g["w_gate"] = _matmul(h2_t, d_gate, mode="nn", name="ffn_gate_dw")
    g["w_up"] = _matmul(h2_t, d_up, mode="nn", name="ffn_up_dw")
    ffn_grads = _pack_grads(g, _FFN_ROWS, sum(n for _, n in _FFN_ROWS))
    ffn_sum, ffn_sum_bf16 = _add_halves(ffn_grads, _swap_other_half(ffn_grads, "ffn"), c, "ffn")
    ex_send, ex_recv, ffn_sum_bf16, ex_land, ex_token = _exchange_start(ffn_sum_bf16, "ffn")
    d_x1, d_norm2 = _proj_norm_bwd(d_gate, full["w_gate"], d_up, full["w_up"], x1, vec("norm2_w") + ex_token[0, 0],
                                   d_x2, tm, "ffn_in_dx_norm2_bwd")
    d_mixed = _matmul(d_x1, full["w_out"], mode="nt", name="out_proj_dx")
    g["w_out"] = _matmul(mixed_t, d_x1, mode="nn", name="out_proj_dw")
    dy, dr_o, dkdf_o, dkdb_o, dv_o, d_gatev, d_pc, d_qvec = _post_bwd(
        d_mixed, y_f, y_b, ps, kd_f, kd_b, gate, pc, qvec, ones_blocks, tm, seq)
    (dr_f, dw_f, dkd_f, dv_f, dkk_f, db_f), (dr_b, dw_b, dkd_b, dv_b, dkk_b, db_b) = _scan_bwd(
        ps, kk, dirs, dy, hist_f, hist_b, batch, seq)
    cts = [[dr_f, dr_b, dr_o], [dv_f, dv_b, dv_o], [dkk_f, dkk_b], [dw_f], [dkd_f, dkdf_o], [db_f],
           [dw_b], [dkd_b, dkdb_o], [db_b], [d_gatev]]
    q, d_pvec, d_m0, d_m1, d_m2, d_m3, d_m4 = _prep_bwd(ps, pvec, mats, ones_blocks, cts, tm)
    d_pshift, d_mu = _shift_bwd(q, p_shift, mu, tm, seq)
    d_w_shift = _matmul(h1_t, d_pshift, mode="nn", name="in_proj_shift_dw")
    d_w_conv = _matmul(h1_t, d_pc, mode="nn", name="in_proj_conv_dw")
    g["w_in"] = jnp.concatenate([d_w_shift[:, :D_SHIFT], d_w_conv], axis=1)
    d_x, d_norm1 = _proj_norm_bwd(d_pshift, w_shift, d_pc, w_conv, xs, vec("norm1_w"), d_x1, tm,
                                  "in_proj_dx_norm1_bwd")
    g["w_up_f"], g["a_up_f"] = d_m0[:D_LORA], d_m1[D_LORA:]
    g["w_up_b"], g["a_up_b"] = d_m2[:D_LORA], d_m3[D_LORA:]
    g["g_up"] = d_m4[:D_GATE]
    g["conv_w"] = d_qvec[4:7]

    def finish(chip_sum, others, tag, layout):
        eighth = _add_quarters(chip_sum, others, chip, tag)
        other_eighth = _swap_with_sibling(eighth, "swap_eighths_" + tag)
        return _unpack_grad_shard(jnp.concatenate([jnp.where(c == 0, eighth, other_eighth),
                                                   jnp.where(c == 0, other_eighth, eighth)], axis=0), layout)

    as2d = lambda name: (1, w[name].shape[0]) if w[name].ndim == 1 else w[name].shape
    operands = lambda name: tuple(a.reshape(as2d(name)) for a in (w[name], grads[name], m[name], v[name]))

    packed = _pack_grads(g, _EARLY_ROWS, EARLY_R)
    mix_sum, mix_sum_bf16 = _add_halves(packed, _swap_other_half(packed, "mixer"), c, "mixer")
    mx_send, mx_recv, mix_sum_bf16, mx_land, mx_token = _exchange_start(mix_sum_bf16, "mixer")
    grads = finish(ffn_sum, _exchange_wait(ex_send, ex_recv, ffn_sum_bf16, ex_land, mx_token, "ffn"), "ffn", _FFN_ROWS)
    updates = {name: _adamw(*operands(name), "adamw_" + name) for name in _FFN_NAMES}
    mix_others = _exchange_wait(mx_send, mx_recv, mix_sum_bf16, mx_land, updates["w_down"][2], "mixer")
    grads.update(finish(mix_sum, mix_others, "mixer", _EARLY_ROWS))

    small = dict(norm1_w=d_norm1, mu_shift=d_mu[:, :D_SHIFT], w0_f=d_pvec[1], w0_b=d_pvec[4], a0_f=d_pvec[2],
                 a0_b=d_pvec[5], k_k=d_pvec[0], k_a_f=d_pvec[3], k_a_b=d_pvec[6], r_k_f=d_qvec[2], r_k_b=d_qvec[3],
                 gn_w=d_qvec[0], gn_b=d_qvec[1], norm2_w=d_norm2, norm_f_w=d_norm_f, loss=loss_part)
    reduced = _unpack_small(_allreduce_small(_pack_small(small)))
    loss = reduced.pop("loss")[0]
    grads.update(reduced)

    outs = {}
    small = [name for name in _WEIGHTS if name not in _BIG_SHARDED]
    updates.update(zip(small, _adamw_small([operands(name) for name in small])))
    for name in ("w_in", "w_out"):
        updates[name] = _adamw(*operands(name), "adamw_" + name)
    for name in _WEIGHTS:
        shape = w[name].shape
        outs[name] = (grads[name].reshape(shape),) + tuple(a.reshape(shape) for a in updates[name])
    d_x = d_x.reshape(batch, seq, D_MODEL)
    return (loss, d_x) + tuple(outs[name][k] for k in range(4) for name in _WEIGHTS)


def kernel(x, norm1_w, w_in, mu_shift, w_up_f, w0_f, w_up_b, w0_b, a_up_f, a0_f, a_up_b, a0_b, g_up, k_k, k_a_f, k_a_b, r_k_f, r_k_b, gn_w, gn_b, conv_w, w_out, norm2_w, w_gate, w_up, w_down, norm_f_w, loss_target, m_norm1_w, m_w_in, m_mu_shift, m_w_up_f, m_w0_f, m_w_up_b, m_w0_b, m_a_up_f, m_a0_f, m_a_up_b, m_a0_b, m_g_up, m_k_k, m_k_a_f, m_k_a_b, m_r_k_f, m_r_k_b, m_gn_w, m_gn_b, m_conv_w, m_w_out, m_norm2_w, m_w_gate, m_w_up, m_w_down, m_norm_f_w, v_norm1_w, v_w_in, v_mu_shift, v_w_up_f, v_w0_f, v_w_up_b, v_w0_b, v_a_up_f, v_a0_f, v_a_up_b, v_a0_b, v_g_up, v_k_k, v_k_a_f, v_k_a_b, v_r_k_f, v_r_k_b, v_gn_w, v_gn_b, v_conv_w, v_w_out, v_norm2_w, v_w_gate, v_w_up, v_w_down, v_norm_f_w):
    args = locals()
    w = {name: args[name] for name in _WEIGHTS}
    m = {name: args["m_" + name] for name in _WEIGHTS}
    v = {name: args["v_" + name] for name in _WEIGHTS}
    return _train_step(x, loss_target, w, m, v)
```

```python
import functools

import jax
import jax.numpy as jnp
from jax import lax
from jax.experimental import pallas as pl
from jax.experimental.pallas import tpu as pltpu

F32 = jnp.float32
BF16 = jnp.bfloat16
MESH = pl.DeviceIdType.MESH

D_MODEL = 1024
D_RWKV = 512
HEAD = 64
N_PAIR = D_RWKV // (2 * HEAD)
D_LORA = 64
D_GATE = 160
D_GATE_PAD = 384
D_FF = 2816
D_SHIFT = 1824
D_SHIFT_PAD = 2048
D_CONV3 = 1536
LOG_DECAY_SCALE = 0.606531
RMS_EPS = 1e-6
GN_EPS = 64e-5
NORM_EPS = 1e-12
ADAM_LR, ADAM_B1, ADAM_B2, ADAM_EPS, ADAM_WD, ADAM_STEP = 0.001, 0.9, 0.999, 1e-08, 0.01, 10

N_SHARD = 4
N_DEV = 8
V7X_VMEM_LIMIT = 48 * 1024 * 1024
SCAN_CHUNK = 32
GROUP = 8

_PACK_ROWS = (("w_in", 840), ("w_out", 256), ("w_gate", 704), ("w_up", 704), ("w_down", 704),
              ("w_up_f", 8), ("w_up_b", 8), ("a_up_f", 8), ("a_up_b", 8), ("g_up", 20), ("conv_w", 1))
_FFN_NAMES = ("w_gate", "w_up", "w_down")
_EARLY_ROWS = tuple(item for item in _PACK_ROWS if item[0] not in _FFN_NAMES)
_FFN_ROWS = tuple(item for item in _PACK_ROWS if item[0] in _FFN_NAMES)
EARLY_R = 1152
SMALL_ROWS = 24


def _tile(n, cap, mult=128):
    best = None
    t = mult
    while t <= min(n, cap):
        if n % t == 0:
            best = t
        t += mult
    return best or n


def _cp(*sem):
    return pltpu.CompilerParams(dimension_semantics=sem or None, vmem_limit_bytes=V7X_VMEM_LIMIT)


def _sds(shape, dtype=F32):
    return jax.ShapeDtypeStruct(shape, dtype)


def _matmul(a, b, *, mode, name, out_dtype=F32, add=None):
    m, kdim = a.shape
    n = b.shape[1] if mode == "nn" else b.shape[0]
    tm, tn = _tile(m, 512, 8), _tile(n, 1536)
    tk = kdim if kdim <= 3584 else _tile(kdim, 1024)
    nk = kdim // tk
    a_spec = pl.BlockSpec((tm, tk), lambda i, j, k: (i, k))
    if mode == "nn":
        b_spec = pl.BlockSpec((tk, tn), lambda i, j, k: (k, j))
        dims = (((1,), (0,)), ((), ()))
    else:
        b_spec = pl.BlockSpec((tn, tk), lambda i, j, k: (j, k))
        dims = (((1,), (1,)), ((), ()))
    has_add = add is not None

    def body(*refs):
        a_ref, b_ref = refs[0], refs[1]
        add_ref = refs[2] if has_add else None
        o_ref = refs[3] if has_add else refs[2]
        part = lax.dot_general(a_ref[...].astype(BF16), b_ref[...].astype(BF16), dims,
                               preferred_element_type=F32)
        if nk == 1:
            if has_add:
                part = part + add_ref[...]
            o_ref[...] = part.astype(out_dtype)
        else:
            acc_ref = refs[-1]
            k = pl.program_id(2)

            @pl.when(k == 0)
            def _():
                acc_ref[...] = jnp.zeros_like(acc_ref)

            acc_ref[...] += part

            @pl.when(k == nk - 1)
            def _():
                res = acc_ref[...]
                if has_add:
                    res = res + add_ref[...]
                o_ref[...] = res.astype(out_dtype)

    o_spec = pl.BlockSpec((tm, tn), lambda i, j, k: (i, j))
    in_specs = [a_spec, b_spec] + ([o_spec] if has_add else [])
    args = (a, b) + ((add,) if has_add else ())
    return pl.pallas_call(
        body, name=name, grid=(m // tm, n // tn, nk), in_specs=in_specs, out_specs=o_spec,
        out_shape=_sds((m, n), out_dtype),
        scratch_shapes=[pltpu.VMEM((tm, tn), F32)] if nk > 1 else [],
        compiler_params=_cp("parallel", "parallel", "arbitrary"),
    )(*args)


def _row(tm, width):
    return pl.BlockSpec((tm, width), lambda i: (i, 0))


def _col(tm, height):
    return pl.BlockSpec((height, tm), lambda i: (0, i))


def _fixed(shape):
    return pl.BlockSpec(shape, lambda i: tuple(0 for _ in shape))


def _rmsnorm_tile(xv, wv):
    return xv * lax.rsqrt(jnp.mean(xv * xv, axis=-1, keepdims=True) + RMS_EPS) * wv


def _norm_in_proj(x, w, w_shift, w_conv, tm):
    t, d = x.shape
    n_a, n_b = w_shift.shape[1], w_conv.shape[1]

    def body(x_ref, w_ref, wa_ref, wb_ref, ht_ref, pa_ref, pb_ref):
        hv = _rmsnorm_tile(x_ref[...], w_ref[...])
        ht_ref[...] = jnp.transpose(hv).astype(BF16)
        hb = hv.astype(BF16)
        pa_ref[...] = jnp.dot(hb, wa_ref[...], preferred_element_type=F32)
        pb_ref[...] = jnp.dot(hb, wb_ref[...], preferred_element_type=F32)

    return pl.pallas_call(
        body, name="norm1_in_proj", grid=(t // tm,),
        in_specs=[_row(tm, d), _fixed((1, d)), _fixed((d, n_a)), _fixed((d, n_b))],
        out_specs=[_col(tm, d), _row(tm, n_a), _row(tm, n_b)],
        out_shape=[_sds((d, t), BF16), _sds((t, n_a)), _sds((t, n_b))],
        compiler_params=_cp("parallel"))(x, w, w_shift, w_conv)


def _out_proj_norm(mixed, w_out, res, w, tm):
    t, d = res.shape

    def body(m_ref, wo_ref, r_ref, w_ref, x_ref, h_ref, ht_ref):
        xv = r_ref[...] + jnp.dot(m_ref[...], wo_ref[...], preferred_element_type=F32)
        x_ref[...] = xv
        hv = _rmsnorm_tile(xv, w_ref[...])
        h_ref[...] = hv.astype(BF16)
        ht_ref[...] = jnp.transpose(hv).astype(BF16)

    return pl.pallas_call(
        body, name="out_proj_norm2", grid=(t // tm,),
        in_specs=[_row(tm, d), _fixed((d, d)), _row(tm, d), _fixed((1, d))],
        out_specs=[_row(tm, d), _row(tm, d), _col(tm, d)],
        out_shape=[_sds((t, d)), _sds((t, d), BF16), _sds((d, t), BF16)],
        compiler_params=_cp("parallel"))(mixed, w_out, res, w)


def _rms_bwd_math(xv, wv, dyv):
    rstd = lax.rsqrt(jnp.mean(xv * xv, axis=-1, keepdims=True) + RMS_EPS)
    xhat = xv * rstd
    gv = dyv * wv
    dx = rstd * (gv - xhat * jnp.mean(gv * xhat, axis=-1, keepdims=True))
    return dx, jnp.sum(dyv * xhat, axis=0, keepdims=True)


def _proj_norm_bwd(dy_a, w_a, dy_b, w_b, x, w, dres, tm, name):
    t, d = x.shape
    ka, kb = dy_a.shape[1], dy_b.shape[1]
    nt = (((1,), (1,)), ((), ()))

    def body(dya_ref, wa_ref, dyb_ref, wb_ref, x_ref, w_ref, dres_ref, dx_ref, dw_ref):
        d_h = (lax.dot_general(dya_ref[...], wa_ref[...], nt, preferred_element_type=F32)
               + lax.dot_general(dyb_ref[...], wb_ref[...], nt, preferred_element_type=F32))
        dx, dw = _rms_bwd_math(x_ref[...], w_ref[...], d_h)
        dx_ref[...] = dres_ref[...] + dx

        @pl.when(pl.program_id(0) == 0)
        def _():
            dw_ref[...] = jnp.zeros_like(dw_ref)

        dw_ref[...] += dw

    return pl.pallas_call(
        body, name=name, grid=(t // tm,),
        in_specs=[_row(tm, ka), _fixed((d, ka)), _row(tm, kb), _fixed((d, kb)), _row(tm, d), _fixed((1, d)),
                  _row(tm, d)],
        out_specs=[_row(tm, d), _fixed((1, d))],
        out_shape=[_sds((t, d)), _sds((1, d))], compiler_params=_cp("arbitrary"))(dy_a, w_a, dy_b, w_b, x, w, dres)


def _ffn_out_loss(act, w_down, x1, w, target, tm):
    t, d = x1.shape
    f = act.shape[1]

    def body(a_ref, wd_ref, x_ref, w_ref, t_ref, dx_ref, dw_ref, loss_ref):
        xv = x_ref[...] + jnp.dot(a_ref[...], wd_ref[...], preferred_element_type=F32)
        wv = w_ref[...]
        rstd = lax.rsqrt(jnp.mean(xv * xv, axis=-1, keepdims=True) + RMS_EPS)
        err = xv * rstd * wv - t_ref[...]
        dx, dw = _rms_bwd_math(xv, wv, err * (1.0 / d))
        dx_ref[...] = dx

        @pl.when(pl.program_id(0) == 0)
        def _():
            dw_ref[...] = jnp.zeros_like(dw_ref)
            loss_ref[...] = jnp.zeros_like(loss_ref)

        dw_ref[...] += dw
        loss_ref[...] += 0.5 * jnp.sum(jnp.mean(err * err, axis=-1, keepdims=True), axis=0, keepdims=True)

    return pl.pallas_call(
        body, name="ffn_out_loss", grid=(t // tm,),
        in_specs=[_row(tm, f), _fixed((f, d)), _row(tm, d), _fixed((1, d)), _row(tm, d)],
        out_specs=[_row(tm, d), _fixed((1, d)), _fixed((1, 1))],
        out_shape=[_sds((t, d)), _sds((1, d)), _sds((1, 1))], compiler_params=_cp("arbitrary"))(act, w_down, x1, w, target)


def _ffn_in(h, w_gate, w_up):
    t, d = h.shape
    f = w_gate.shape[1]
    tm, tn = _tile(t, 512, 8), _tile(f, 1536)

    def body(h_ref, wg_ref, wu_ref, g_ref, u_ref, a_ref, at_ref):
        hv = h_ref[...]
        gv = jnp.dot(hv, wg_ref[...], preferred_element_type=F32)
        uv = jnp.dot(hv, wu_ref[...], preferred_element_type=F32)
        act = gv * jax.nn.sigmoid(gv) * uv
        g_ref[...] = gv.astype(BF16)
        u_ref[...] = uv.astype(BF16)
        a_ref[...] = act.astype(BF16)
        at_ref[...] = jnp.transpose(act).astype(BF16)

    w_spec = pl.BlockSpec((d, tn), lambda i, j: (0, j))
    o_spec = pl.BlockSpec((tm, tn), lambda i, j: (i, j))
    return pl.pallas_call(
        body, name="ffn_in", grid=(t // tm, f // tn),
        in_specs=[pl.BlockSpec((tm, d), lambda i, j: (i, 0)), w_spec, w_spec],
        out_specs=[o_spec, o_spec, o_spec, pl.BlockSpec((tn, tm), lambda i, j: (j, i))],
        out_shape=[_sds((t, f), BF16)] * 3 + [_sds((f, t), BF16)],
        compiler_params=_cp("parallel", "parallel"))(h, w_gate, w_up)


def _ffn_in_bwd(d_out, w_down, gate, up):
    t, d = d_out.shape
    f = w_down.shape[0]
    tm, tn = _tile(t, 512, 8), _tile(f, 1536)

    def body(do_ref, w_ref, g_ref, u_ref, dg_ref, du_ref):
        dv = lax.dot_general(do_ref[...].astype(BF16), w_ref[...], (((1,), (1,)), ((), ())),
                             preferred_element_type=F32)
        gv, uv = g_ref[...].astype(F32), u_ref[...].astype(F32)
        sg = jax.nn.sigmoid(gv)
        du_ref[...] = (dv * gv * sg).astype(BF16)
        dg_ref[...] = (dv * uv * (sg * (1.0 + gv * (1.0 - sg)))).astype(BF16)

    tile = pl.BlockSpec((tm, tn), lambda i, j: (i, j))
    return pl.pallas_call(
        body, name="ffn_in_bwd", grid=(t // tm, f // tn),
        in_specs=[pl.BlockSpec((tm, d), lambda i, j: (i, 0)), pl.BlockSpec((tn, d), lambda i, j: (j, 0)), tile, tile],
        out_specs=[tile, tile], out_shape=[_sds((t, f), BF16)] * 2,
        compiler_params=_cp("parallel", "parallel"))(d_out, w_down, gate, up)


def _halo_specs(tm, width, rows_total):
    per = tm // GROUP
    last = rows_total // GROUP - 1
    prev = pl.BlockSpec((GROUP, width), lambda i: (jnp.maximum(i * per - 1, 0), 0))
    nxt = pl.BlockSpec((GROUP, width), lambda i: (jnp.minimum((i + 1) * per, last), 0))
    return prev, nxt


def _edge_flags(tm, seq):
    i = pl.program_id(0)
    has_prev = jnp.where((i * tm) % seq == 0, 0.0, 1.0).astype(F32)
    has_next = jnp.where(((i + 1) * tm) % seq == 0, 0.0, 1.0).astype(F32)
    return has_prev, has_next


def _shifted(xv, prev_row, next_row):
    tm = xv.shape[0]
    row = lax.broadcasted_iota(jnp.int32, xv.shape, 0)
    down = jnp.where(row == 0, prev_row, pltpu.roll(xv, 1, axis=0))
    up = jnp.where(row == tm - 1, next_row, pltpu.roll(xv, tm - 1, axis=0))
    return down, up


def _shift_bwd(q, p, mu, tm, seq):
    t, w = p.shape
    prev_spec, next_spec = _halo_specs(tm, w, t)

    def body(q_ref, qp_ref, qn_ref, p_ref, pp_ref, pn_ref, mu_ref, dp_ref, dmu_ref):
        has_prev, has_next = _edge_flags(tm, seq)
        muv = mu_ref[0:1, :]
        qv = q_ref[...]
        mq = muv * qv
        mq_down, mq_up = _shifted(mq, muv * qp_ref[GROUP - 1:GROUP, :] * has_prev,
                                  muv * qn_ref[0:1, :] * has_next)
        dp_ref[...] = (qv - mq + 0.5 * (mq_down + mq_up)).astype(BF16)
        pv = p_ref[...]
        p_down, p_up = _shifted(pv, pp_ref[GROUP - 1:GROUP, :] * has_prev, pn_ref[0:1, :] * has_next)

        @pl.when(pl.program_id(0) == 0)
        def _():
            dmu_ref[...] = jnp.zeros_like(dmu_ref)

        dmu_ref[...] += jnp.sum(qv * (0.5 * (p_down + p_up) - pv), axis=0, keepdims=True)

    return pl.pallas_call(
        body, name="shift_bwd", grid=(t // tm,),
        in_specs=[_row(tm, w), prev_spec, next_spec, _row(tm, w), prev_spec, next_spec, _fixed((GROUP, w))],
        out_specs=[_row(tm, w), _fixed((1, w))],
        out_shape=[_sds((t, w), BF16), _sds((1, w))], compiler_params=_cp("arbitrary"))(q, q, q, p, p, p, mu)


@jax.custom_vjp
def _bdot(a, b):
    return jnp.dot(a.astype(BF16), b.astype(BF16), preferred_element_type=F32)


def _bdot_fwd(a, b):
    return _bdot(a, b), (a, b)


def _bdot_bwd(res, g):
    a, b = res
    gb = g.astype(BF16)
    da = lax.dot_general(gb, b.astype(BF16), (((1,), (1,)), ((), ())), preferred_element_type=F32)
    db = lax.dot_general(a.astype(BF16), gb, (((0,), (0,)), ((), ())), preferred_element_type=F32)
    return da, db


_bdot.defvjp(_bdot_fwd, _bdot_bwd)


def _seg_raw(x, ones_blocks):
    hi = x.astype(BF16)
    lo = (x - hi.astype(F32)).astype(BF16)
    return (jnp.dot(hi, ones_blocks, preferred_element_type=F32)
            + jnp.dot(lo, ones_blocks, preferred_element_type=F32))


@jax.custom_vjp
def _seg(x, ones_blocks):
    return _seg_raw(x, ones_blocks)


def _seg_fwd(x, ones_blocks):
    return _seg_raw(x, ones_blocks), ones_blocks


def _seg_bwd(ones_blocks, g):
    return _seg_raw(g, ones_blocks), jnp.zeros_like(ones_blocks)


_seg.defvjp(_seg_fwd, _seg_bwd)


def _head_ones():
    h = jnp.arange(D_RWKV) // HEAD
    return (h[:, None] == h[None, :]).astype(BF16)


def _prep_math(ps, k_k, w0_f, a0_f, k_a_f, w0_b, a0_b, k_a_b, wup_f, aup_f, wup_b, aup_b, gup, ones_blocks):
    r = ps[:, 0:512]
    k = ps[:, 512:1024]
    v = ps[:, 1024:1536]
    xwa = ps[:, 1536:1664]
    xg = ps[:, 1664:D_SHIFT_PAD]
    kk_raw = k * k_k
    norm = jnp.sqrt(_seg(kk_raw * kk_raw, ones_blocks))
    kk = kk_raw / jnp.maximum(norm, NORM_EPS)
    t_xwa = jnp.tanh(xwa)
    outs = [r, v, kk]
    for w0, a0, k_a, wup, aup in ((w0_f, a0_f, k_a_f, wup_f, aup_f), (w0_b, a0_b, k_a_b, wup_b, aup_b)):
        decay = jnp.exp(-LOG_DECAY_SCALE * jax.nn.sigmoid(w0 + _bdot(t_xwa, wup)))
        rate = jax.nn.sigmoid(a0 + _bdot(xwa, aup))
        outs += [decay, k * (1.0 + (rate - 1.0) * k_a), kk * rate]
    outs.append(_bdot(jax.nn.sigmoid(xg), gup))
    return tuple(outs)


def _prep_args(tm, ps_tile, pv_ref, mat_refs, ones_ref):
    vecs = [jnp.broadcast_to(pv_ref[j:j + 1, :], (tm, D_RWKV)) for j in range(7)]
    return [ps_tile] + vecs + [m[...] for m in mat_refs] + [ones_ref[...]]


_PREP_MAT_SHAPES = ((128, D_RWKV),) * 4 + ((D_GATE_PAD, D_RWKV),)


def _shift_prep_fwd(p, mu, pvec, mats, ones_blocks, tm, seq):
    t, w = p.shape
    prev_spec, next_spec = _halo_specs(tm, w, t)

    def body(p_ref, hp_ref, hn_ref, mu_ref, pv_ref, m0, m1, m2, m3, m4, ones_ref, ps_ref, *out_refs):
        has_prev, has_next = _edge_flags(tm, seq)
        xv = p_ref[...]
        down, up = _shifted(xv, hp_ref[GROUP - 1:GROUP, :] * has_prev, hn_ref[0:1, :] * has_next)
        ps_tile = xv + mu_ref[0:1, :] * (0.5 * (down + up) - xv)
        ps_ref[...] = ps_tile
        outs = _prep_math(*_prep_args(tm, ps_tile, pv_ref, (m0, m1, m2, m3, m4), ones_ref))
        for o_ref, val in zip(out_refs, outs[2:]):
            o_ref[...] = val

    return pl.pallas_call(
        body, name="shift_prep_fwd", grid=(t // tm,),
        in_specs=[_row(tm, w), prev_spec, next_spec, _fixed((GROUP, w)), _fixed((8, D_RWKV))]
        + [_fixed(s) for s in _PREP_MAT_SHAPES] + [_fixed((D_RWKV, D_RWKV))],
        out_specs=[_row(tm, w)] + [_row(tm, D_RWKV)] * 8, out_shape=[_sds((t, w))] + [_sds((t, D_RWKV))] * 8,
        compiler_params=_cp("parallel"))(p, p, p, mu, pvec, *mats, ones_blocks)


def _prep_bwd(ps, pvec, mats, ones_blocks, cts, tm):
    t = ps.shape[0]
    counts = [len(c) for c in cts]
    flat = [a for c in cts for a in c]

    def body(ps_ref, pv_ref, m0, m1, m2, m3, m4, ones_ref, *refs):
        ct_refs = refs[:len(flat)]
        q_ref, dpv_ref = refs[len(flat)], refs[len(flat) + 1]
        dmat_refs = refs[len(flat) + 2:]
        args = _prep_args(tm, ps_ref[...], pv_ref, (m0, m1, m2, m3, m4), ones_ref)
        _, vjp = jax.vjp(lambda *a: _prep_math(*a, args[-1]), *args[:-1])
        ct_vals, pos = [], 0
        for n in counts:
            val = ct_refs[pos][...]
            for extra in ct_refs[pos + 1:pos + n]:
                val = val + extra[...]
            ct_vals.append(val)
            pos += n
        grads = vjp(tuple(ct_vals))
        q_ref[...] = grads[0]

        @pl.when(pl.program_id(0) == 0)
        def _():
            dpv_ref[...] = jnp.zeros_like(dpv_ref)
            for d_ref in dmat_refs:
                d_ref[...] = jnp.zeros_like(d_ref)

        for j in range(7):
            dpv_ref[j:j + 1, :] += jnp.sum(grads[1 + j], axis=0, keepdims=True)
        for d_ref, gm in zip(dmat_refs, grads[8:13]):
            d_ref[...] += gm

    return pl.pallas_call(
        body, name="prep_bwd", grid=(t // tm,),
        in_specs=[_row(tm, D_SHIFT_PAD), _fixed((8, D_RWKV))] + [_fixed(s) for s in _PREP_MAT_SHAPES]
        + [_fixed((D_RWKV, D_RWKV))] + [_row(tm, D_RWKV)] * len(flat),
        out_specs=[_row(tm, D_SHIFT_PAD), _fixed((8, D_RWKV))] + [_fixed(s) for s in _PREP_MAT_SHAPES],
        out_shape=[_sds((t, D_SHIFT_PAD)), _sds((8, D_RWKV))] + [_sds(s) for s in _PREP_MAT_SHAPES],
        compiler_params=_cp("arbitrary"))(ps, pvec, *mats, ones_blocks, *flat)


def _pair_ones():
    h = jnp.arange(2 * HEAD) // HEAD
    block = (h[:, None] == h[None, :]).astype(BF16)
    return jnp.concatenate([block, block], axis=0)


def _diag_mask():
    lane = lax.broadcasted_iota(jnp.int32, (HEAD, 2 * HEAD), 1)
    sub = lax.broadcasted_iota(jnp.int32, (HEAD, 2 * HEAD), 0)
    return jnp.where((lane & (HEAD - 1)) == sub, 1.0, 0.0).astype(F32)


def _to_row(cols, dmask):
    return jnp.sum(cols * dmask, axis=0, keepdims=True)


def _seg_many(exact, rounded, ones_pair):
    out_exact, out_rounded = [], []
    if exact:
        parts = []
        for x in exact:
            hi = x.astype(BF16)
            parts.append(jnp.concatenate([hi, (x - hi.astype(F32)).astype(BF16)], axis=1))
        res = jnp.dot(jnp.concatenate(parts, axis=0), ones_pair, preferred_element_type=F32)
        out_exact = [res[HEAD * c:HEAD * (c + 1)] for c in range(len(exact))]
    if rounded:
        res = jnp.dot(jnp.concatenate([x.astype(BF16) for x in rounded], axis=0), ones_pair[0:2 * HEAD],
                      preferred_element_type=F32)
        out_rounded = [res[HEAD * c:HEAD * (c + 1)] for c in range(len(rounded))]
    return out_exact, out_rounded


N_CHAIN = 2 * N_PAIR


def _chain(c):
    d, p = divmod(c, N_PAIR)
    return d, slice(2 * HEAD * p, 2 * HEAD * (p + 1))


def _scan_specs(n_chunks, col_blocks, fwd_chunk, bwd_chunk):
    def spec(chunk_of, col):
        return pl.BlockSpec((SCAN_CHUNK, D_RWKV), lambda b, g: (b * n_chunks + chunk_of(g), col))
    return [spec(fwd_chunk, c) for c in col_blocks] + [spec(bwd_chunk, c) for c in col_blocks]


def _scan_fwd(ps, kk, dirs, batch, seq):
    t = batch * seq
    n = seq // SCAN_CHUNK
    groups = SCAN_CHUNK // GROUP
    up = lambda g: g
    down = lambda g: n - 1 - g
    col_blocks = (0, 2, 0, 0, 0, 0)

    def body(*refs):
        dir_refs = (refs[0:6], refs[6:12])
        ones_ref = refs[12]
        y_refs, hist_refs, st_ref = refs[13:15], refs[15:17], refs[17]

        @pl.when(pl.program_id(1) == 0)
        def _():
            st_ref[...] = jnp.zeros_like(st_ref)

        ones_pair = ones_ref[...]
        dmask = _diag_mask()
        dmask_b = dmask.astype(BF16)
        sub8 = lax.broadcasted_iota(jnp.int32, (GROUP, 2 * HEAD), 0)

        def group(gi, carry):
            off = (pl.multiple_of(gi * GROUP, GROUP), pl.multiple_of((groups - 1 - gi) * GROUP, GROUP))
            loaded = [tuple(ref[pl.ds(off[d], GROUP), :] for ref in dir_refs[d]) for d in range(2)]
            states = list(carry)
            y_acc = [jnp.zeros((GROUP, 2 * HEAD), F32) for _ in range(N_CHAIN)]
            for step in range(GROUP):
                rows, idx = [], []
                for c in range(N_CHAIN):
                    d, lanes = _chain(c)
                    i = step if d == 0 else GROUP - 1 - step
                    idx.append(i)
                    rows.append(tuple(x8[i:i + 1, lanes] for x8 in loaded[d]))
                    hist_refs[d][c % N_PAIR, gi * GROUP + step] = states[c]
                _, v_cols = _seg_many([], [dmask_b * rows[c][1].astype(BF16) for c in range(N_CHAIN)], ones_pair)
                sas, _ = _seg_many([states[c] * rows[c][2] for c in range(N_CHAIN)], [], ones_pair)
                for c in range(N_CHAIN):
                    _, _, _, w_row, kd_row, b_row = rows[c]
                    states[c] = states[c] * w_row - sas[c] * b_row + v_cols[c] * kd_row
                _, ys = _seg_many([], [states[c] * rows[c][0] for c in range(N_CHAIN)], ones_pair)
                for c in range(N_CHAIN):
                    y_acc[c] = jnp.where(sub8 == idx[c], _to_row(ys[c], dmask), y_acc[c])
            for c in range(N_CHAIN):
                d, lanes = _chain(c)
                y_refs[d][pl.ds(off[d], GROUP), lanes] = y_acc[c]
            return tuple(states)

        final = lax.fori_loop(0, groups, group, tuple(st_ref[c] for c in range(N_CHAIN)))
        for c in range(N_CHAIN):
            st_ref[c] = final[c]
            hist_refs[c // N_PAIR][c % N_PAIR, SCAN_CHUNK] = final[c]

    y_spec_f = pl.BlockSpec((SCAN_CHUNK, D_RWKV), lambda b, g: (b * n + up(g), 0))
    y_spec_b = pl.BlockSpec((SCAN_CHUNK, D_RWKV), lambda b, g: (b * n + down(g), 0))
    hist_shape = (batch, n, N_PAIR, SCAN_CHUNK + 1, HEAD, 2 * HEAD)
    hist_block = (None, None, N_PAIR, SCAN_CHUNK + 1, HEAD, 2 * HEAD)
    hist_spec_f = pl.BlockSpec(hist_block, lambda b, g: (b, up(g), 0, 0, 0, 0))
    hist_spec_b = pl.BlockSpec(hist_block, lambda b, g: (b, down(g), 0, 0, 0, 0))
    ones_spec = pl.BlockSpec((4 * HEAD, 2 * HEAD), lambda b, g: (0, 0))
    (wf, kdf, bf), (wb, kdb, bb) = dirs
    return pl.pallas_call(
        body, name="wkv_fwd", grid=(batch, n),
        in_specs=_scan_specs(n, col_blocks, up, down) + [ones_spec],
        out_specs=[y_spec_f, y_spec_b, hist_spec_f, hist_spec_b],
        out_shape=[_sds((t, D_RWKV)), _sds((t, D_RWKV)), _sds(hist_shape), _sds(hist_shape)],
        scratch_shapes=[pltpu.VMEM((N_CHAIN, HEAD, 2 * HEAD), F32)],
        compiler_params=_cp("parallel", "arbitrary"),
    )(ps, ps, kk, wf, kdf, bf, ps, ps, kk, wb, kdb, bb, _pair_ones())


def _scan_bwd(ps, kk, dirs, dy, hist_f, hist_b, batch, seq):
    t = batch * seq
    n = seq // SCAN_CHUNK
    groups = SCAN_CHUNK // GROUP
    fwd_chunk = lambda g: n - 1 - g
    bwd_chunk = lambda g: g
    col_blocks = (0, 2, 0, 0, 0, 0, 0)

    def undo_group(dir_refs, out_refs, hist_refs, gi, d_states, ones_pair, dmask, sub8):
        d_states = list(d_states)
        loaded, blocks = [], []
        for d in range(2):
            blk = groups - 1 - gi if d == 0 else gi
            blocks.append(pl.ds(pl.multiple_of(blk * GROUP, GROUP), GROUP))
            r8, v8, kk8, w8, kd8, b8, dy8 = (ref[blocks[d], :] for ref in dir_refs[d])
            loaded.append((r8, v8, kk8, w8, kd8, -b8, dy8))
        acc = [[jnp.zeros((GROUP, 2 * HEAD), F32) for _ in range(6)] for _ in range(N_CHAIN)]
        for step in range(GROUP):
            rows, idx, before, after = [], [], [], []
            for c in range(N_CHAIN):
                d, lanes = _chain(c)
                i = GROUP - 1 - step if d == 0 else step
                q = (groups - 1 - gi) * GROUP + i if d == 0 else SCAN_CHUNK - 1 - (gi * GROUP + i)
                idx.append(i)
                rows.append(tuple(x8[i:i + 1, lanes] for x8 in loaded[d]))
                before.append(hist_refs[d][c % N_PAIR, q])
                after.append(hist_refs[d][c % N_PAIR, q + 1])
            _, cols = _seg_many([], [dmask.astype(BF16) * rows[c][j].astype(BF16) for c in range(N_CHAIN) for j in (1, 6)],
                                ones_pair)
            v_cols, dy_cols = cols[0::2], cols[1::2]
            d_now = [d_states[c] + dy_cols[c] * rows[c][0] for c in range(N_CHAIN)]
            d_sas, _ = _seg_many([d_now[c] * rows[c][5] for c in range(N_CHAIN)], [], ones_pair)
            _, others = _seg_many(
                [], [x for c in range(N_CHAIN) for x in (before[c] * rows[c][2], d_now[c] * rows[c][4])], ones_pair)
            for c in range(N_CHAIN):
                sa, d_sa, dv_cols = others[2 * c], d_sas[c], others[2 * c + 1]
                rows_out = (
                    jnp.sum(after[c] * dy_cols[c], axis=0, keepdims=True),
                    jnp.sum(d_now[c] * before[c], axis=0, keepdims=True),
                    jnp.sum(d_now[c] * v_cols[c], axis=0, keepdims=True),
                    _to_row(dv_cols, dmask),
                    jnp.sum(before[c] * d_sa, axis=0, keepdims=True),
                    -jnp.sum(d_now[c] * sa, axis=0, keepdims=True),
                )
                acc[c] = [jnp.where(sub8 == idx[c], val, a) for val, a in zip(rows_out, acc[c])]
                d_states[c] = d_now[c] * rows[c][3] + d_sa * rows[c][2]
        for c in range(N_CHAIN):
            d, lanes = _chain(c)
            for o_ref, val in zip(out_refs[d], acc[c]):
                o_ref[blocks[d], lanes] = val
        return tuple(d_states)

    def body(*refs):
        dir_refs = (refs[0:7], refs[7:14])
        hist_refs, ones_ref = refs[14:16], refs[16]
        out_refs = (refs[17:23], refs[23:29])
        dst_ref = refs[29]

        @pl.when(pl.program_id(1) == 0)
        def _():
            dst_ref[...] = jnp.zeros_like(dst_ref)

        ones_pair = ones_ref[...]
        dmask = _diag_mask()
        sub8 = lax.broadcasted_iota(jnp.int32, (GROUP, 2 * HEAD), 0)

        def group(gi, carry):
            return undo_group(dir_refs, out_refs, hist_refs, gi, carry, ones_pair, dmask, sub8)

        final = lax.fori_loop(0, groups, group, tuple(dst_ref[c] for c in range(N_CHAIN)))
        for c in range(N_CHAIN):
            dst_ref[c] = final[c]

    blk = (SCAN_CHUNK, D_RWKV)
    out_f = pl.BlockSpec(blk, lambda b, g: (b * n + fwd_chunk(g), 0))
    out_b = pl.BlockSpec(blk, lambda b, g: (b * n + bwd_chunk(g), 0))
    hist_block = (None, None, N_PAIR, SCAN_CHUNK + 1, HEAD, 2 * HEAD)
    hist_spec_f = pl.BlockSpec(hist_block, lambda b, g: (b, fwd_chunk(g), 0, 0, 0, 0))
    hist_spec_b = pl.BlockSpec(hist_block, lambda b, g: (b, bwd_chunk(g), 0, 0, 0, 0))
    ones_spec = pl.BlockSpec((4 * HEAD, 2 * HEAD), lambda b, g: (0, 0))
    (wf, kdf, bf), (wb, kdb, bb) = dirs
    outs = pl.pallas_call(
        body, name="wkv_bwd", grid=(batch, n),
        in_specs=_scan_specs(n, col_blocks, fwd_chunk, bwd_chunk) + [hist_spec_f, hist_spec_b, ones_spec],
        out_specs=[out_f] * 6 + [out_b] * 6,
        out_shape=[_sds((t, D_RWKV))] * 12,
        scratch_shapes=[pltpu.VMEM((N_CHAIN, HEAD, 2 * HEAD), F32)],
        compiler_params=_cp("parallel", "arbitrary"),
    )(ps, ps, kk, wf, kdf, bf, dy, ps, ps, kk, wb, kdb, bb, dy, hist_f, hist_b, _pair_ones())
    return outs[0:6], outs[6:12]


def _post_math(y, r, kd_f, kd_b, v, gate, gn_w, gn_b, rk_f, rk_b, ones_blocks):
    mean = _seg(y, ones_blocks) * (1.0 / HEAD)
    yc = y - mean
    var = _seg(yc * yc, ones_blocks) * (1.0 / HEAD)
    yn = yc * lax.rsqrt(var + GN_EPS) * gn_w + gn_b
    bonus = _seg(r * kd_f * rk_f, ones_blocks) * v + _seg(r * kd_b * rk_b, ones_blocks) * v
    return (yn + bonus) * gate


def _conv_parts(pc, halo_prev, halo_next, has_prev, has_next):
    gate_b, gate_c, hid = pc[:, 0:512], pc[:, 512:1024], pc[:, 1024:1536]
    u = gate_c * hid
    u_prev_row = halo_prev[GROUP - 1:GROUP, 512:1024] * halo_prev[GROUP - 1:GROUP, 1024:1536] * has_prev
    u_next_row = halo_next[0:1, 512:1024] * halo_next[0:1, 1024:1536] * has_next
    u_down, u_up = _shifted(u, u_prev_row, u_next_row)
    return gate_b, gate_c, hid, u, u_down, u_up


def _post_specs(tm, t):
    pc_prev, pc_next = _halo_specs(tm, D_CONV3, t)
    col = lambda c: pl.BlockSpec((tm, D_RWKV), lambda i: (i, c))
    return ([col(0), col(0), col(0), col(0), col(0), col(2), col(0), _row(tm, D_CONV3), pc_prev, pc_next,
             _fixed((8, D_RWKV)), _fixed((D_RWKV, D_RWKV))])


def _post_fwd(y_f, y_b, ps, kd_f, kd_b, gate, pc, qvec, ones_blocks, tm, seq):
    t = ps.shape[0]

    def body(yf_ref, yb_ref, r_ref, kdf_ref, kdb_ref, v_ref, g_ref, pc_ref, hp_ref, hn_ref, qv_ref, ones_ref,
             o_ref, ot_ref):
        has_prev, has_next = _edge_flags(tm, seq)
        vec = [jnp.broadcast_to(qv_ref[j:j + 1, :], (tm, D_RWKV)) for j in range(7)]
        o_rwkv = _post_math(yf_ref[...] + yb_ref[...], r_ref[...], kdf_ref[...], kdb_ref[...], v_ref[...],
                            g_ref[...], vec[0], vec[1], vec[2], vec[3], ones_ref[...])
        gate_b, _, _, u, u_down, u_up = _conv_parts(pc_ref[...], hp_ref[...], hn_ref[...], has_prev, has_next)
        o_conv = gate_b * (vec[4] * u_down + vec[5] * u + vec[6] * u_up)
        for half, val in enumerate((o_rwkv, o_conv)):
            o_ref[:, D_RWKV * half:D_RWKV * (half + 1)] = val.astype(BF16)
            ot_ref[D_RWKV * half:D_RWKV * (half + 1), :] = jnp.transpose(val).astype(BF16)

    return pl.pallas_call(
        body, name="post_fwd", grid=(t // tm,), in_specs=_post_specs(tm, t),
        out_specs=[_row(tm, D_MODEL), _col(tm, D_MODEL)],
        out_shape=[_sds((t, D_MODEL), BF16), _sds((D_MODEL, t), BF16)], compiler_params=_cp("parallel"),
    )(y_f, y_b, ps, kd_f, kd_b, ps, gate, pc, pc, pc, qvec, ones_blocks)


def _post_bwd(d_out, y_f, y_b, ps, kd_f, kd_b, gate, pc, qvec, ones_blocks, tm, seq):
    t = ps.shape[0]
    do_prev, do_next = _halo_specs(tm, D_MODEL, t)

    def body(do_ref, dop_ref, don_ref, yf_ref, yb_ref, r_ref, kdf_ref, kdb_ref, v_ref, g_ref, pc_ref, hp_ref,
             hn_ref, qv_ref, ones_ref, dy_ref, dr_ref, dkdf_ref, dkdb_ref, dv_ref, dg_ref, dpc_ref, dqv_ref):
        has_prev, has_next = _edge_flags(tm, seq)
        vec = [jnp.broadcast_to(qv_ref[j:j + 1, :], (tm, D_RWKV)) for j in range(7)]
        ones_v = ones_ref[...]
        args = (yf_ref[...] + yb_ref[...], r_ref[...], kdf_ref[...], kdb_ref[...], v_ref[...], g_ref[...],
                vec[0], vec[1], vec[2], vec[3])
        _, vjp = jax.vjp(lambda *a: _post_math(*a, ones_v), *args)
        grads = vjp(do_ref[:, 0:D_RWKV])
        for o_ref, gval in zip((dy_ref, dr_ref, dkdf_ref, dkdb_ref, dv_ref, dg_ref), grads[0:6]):
            o_ref[...] = gval

        hp, hn = hp_ref[...], hn_ref[...]
        gate_b, gate_c, hid, u, u_down, u_up = _conv_parts(pc_ref[...], hp, hn, has_prev, has_next)
        d_oc = do_ref[:, D_RWKV:2 * D_RWKV]
        d_cu = d_oc * gate_b
        d_cu_prev = dop_ref[GROUP - 1:GROUP, D_RWKV:2 * D_RWKV] * hp[GROUP - 1:GROUP, 0:512] * has_prev
        d_cu_next = don_ref[0:1, D_RWKV:2 * D_RWKV] * hn[0:1, 0:512] * has_next
        d_cu_down, d_cu_up = _shifted(d_cu, d_cu_prev, d_cu_next)
        d_u = vec[5] * d_cu + vec[4] * d_cu_up + vec[6] * d_cu_down
        dpc_ref[:, 0:512] = (d_oc * (vec[4] * u_down + vec[5] * u + vec[6] * u_up)).astype(BF16)
        dpc_ref[:, 512:1024] = (d_u * hid).astype(BF16)
        dpc_ref[:, 1024:1536] = (d_u * gate_c).astype(BF16)

        @pl.when(pl.program_id(0) == 0)
        def _():
            dqv_ref[...] = jnp.zeros_like(dqv_ref)

        vec_grads = list(grads[6:10]) + [d_cu * u_down, d_cu * u, d_cu * u_up]
        for j, gval in enumerate(vec_grads):
            dqv_ref[j:j + 1, :] += jnp.sum(gval, axis=0, keepdims=True)

    return pl.pallas_call(
        body, name="post_bwd", grid=(t // tm,),
        in_specs=[_row(tm, D_MODEL), do_prev, do_next] + _post_specs(tm, t),
        out_specs=[_row(tm, D_RWKV)] * 6 + [_row(tm, D_CONV3), _fixed((8, D_RWKV))],
        out_shape=[_sds((t, D_RWKV))] * 6 + [_sds((t, D_CONV3), BF16), _sds((8, D_RWKV))],
        compiler_params=_cp("arbitrary"),
    )(d_out, d_out, d_out, y_f, y_b, ps, kd_f, kd_b, ps, gate, pc, pc, pc, qvec, ones_blocks)


def _adamw_math(wv, gv, mv, vv):
    m2 = ADAM_B1 * mv + (1.0 - ADAM_B1) * gv
    v2 = ADAM_B2 * vv + (1.0 - ADAM_B2) * (gv * gv)
    m_hat = m2 / (1.0 - ADAM_B1 ** ADAM_STEP)
    v_hat = v2 / (1.0 - ADAM_B2 ** ADAM_STEP)
    return -ADAM_LR * (m_hat / (jnp.sqrt(v_hat) + ADAM_EPS) + ADAM_WD * wv), m2, v2


def _adamw_small(items):
    n = len(items)

    def body(*refs):
        ins, outs = refs[:4 * n], refs[4 * n:]
        for k in range(n):
            w_ref, g_ref, m_ref, v_ref = ins[4 * k:4 * k + 4]
            for o_ref, val in zip(outs[3 * k:3 * k + 3], _adamw_math(w_ref[...], g_ref[...], m_ref[...], v_ref[...])):
                o_ref[...] = val

    flat = [a for item in items for a in item]
    outs = pl.pallas_call(
        body, name="adamw_small", out_shape=[_sds(item[0].shape) for item in items for _ in range(3)],
        compiler_params=_cp())(*flat)
    return [tuple(outs[3 * k:3 * k + 3]) for k in range(n)]


def _adamw(w, g, m, v, name):
    r, c = w.shape[-2:]
    tr = _tile(r, 256, 8)
    if w.ndim == 3:
        spec = pl.BlockSpec((None, tr, c), lambda i: (0, i, 0))
    else:
        spec = pl.BlockSpec((tr, c), lambda i: (i, 0))

    def body(w_ref, g_ref, m_ref, v_ref, d_ref, nm_ref, nv_ref):
        d_ref[...], nm_ref[...], nv_ref[...] = _adamw_math(w_ref[...], g_ref[...], m_ref[...], v_ref[...])

    return pl.pallas_call(
        body, name=name, grid=(r // tr,), in_specs=[spec] * 4, out_specs=[spec] * 3,
        out_shape=[_sds(w.shape)] * 3, compiler_params=_cp("parallel"))(w, g, m, v)


_ANY = pl.BlockSpec(memory_space=pl.ANY)


def _place():
    return lax.axis_index("x"), lax.axis_index("y"), lax.axis_index("c")


def _other_chips(x, y):
    return [(1 - x, y), (x, 1 - y), (1 - x, 1 - y)]


def _remote(src, dst, send_sems, recv_sems, k, to):
    return pltpu.make_async_remote_copy(src_ref=src, dst_ref=dst, send_sem=send_sems.at[k],
                                        recv_sem=recv_sems.at[k], device_id=to, device_id_type=MESH)


def _gather_weights(pack):
    rows, width = pack.shape
    half = rows // 2

    def body(x_ref, out_ref, send_sems, recv_sems):
        x, y, c = _place()
        sibling = (x, y, 1 - c)
        chips = _other_chips(x, y)

        def block(chip, part):
            return out_ref.at[2 * chip[0] + chip[1], pl.ds(part * half, half), :]

        first = [_remote(x_ref.at[pl.ds(c * half, half), :], block((x, y), c), send_sems, recv_sems, j, (*chip, c))
                 for j, chip in enumerate(chips)]
        for cp in first:
            cp.start()
        passed = [_remote(block(chip, c), block(chip, c), send_sems, recv_sems, 3 + j, sibling)
                  for j, chip in enumerate(chips)]
        for j, chip in enumerate(chips):
            _remote(block(chip, c), block(chip, c), send_sems, recv_sems, j, sibling).wait_recv()
            passed[j].start()
        for j, chip in enumerate(chips):
            _remote(block(chip, 1 - c), block(chip, 1 - c), send_sems, recv_sems, 3 + j, sibling).wait_recv()
        for cp in first + passed:
            cp.wait_send()

    return pl.pallas_call(
        body, name="gather_weights", in_specs=[_ANY], out_specs=_ANY,
        out_shape=_sds((N_SHARD, rows, width), pack.dtype),
        scratch_shapes=[pltpu.SemaphoreType.DMA((6,)), pltpu.SemaphoreType.DMA((6,))],
    )(pack)


_HBM = pl.BlockSpec(memory_space=pltpu.HBM)
_SEMS = pl.BlockSpec(memory_space=pltpu.SEMAPHORE)
_DATAFLOW = pltpu.SideEffectType.DATAFLOW_SIDE_EFFECTING


def _fetch_start(pack, after):
    def body(x_ref, land_ref, after_ref, send_sems, recv_sems, x_thru, land_thru, token):
        x, y, c = _place()
        for j, chip in enumerate(_other_chips(x, y)):
            _remote(x_ref, land_ref.at[2 * x + y], send_sems, recv_sems, j, (*chip, c)).start()
        token[...] = jnp.zeros_like(token)

    land = lax.empty((N_SHARD,) + pack.shape, pack.dtype)
    return pl.pallas_call(
        body, name="fetch_ffn_start",
        out_shape=(pltpu.SemaphoreType.DMA((3,)), pltpu.SemaphoreType.DMA((3,)), pltpu.HBM(pack.shape, pack.dtype),
                   pltpu.HBM(land.shape, land.dtype), _sds((8, 128))),
        in_specs=(_HBM, _HBM, _ANY), out_specs=(_SEMS, _SEMS, _HBM, _HBM, pl.BlockSpec(memory_space=pltpu.VMEM)),
        input_output_aliases={0: 2, 1: 3}, compiler_params=pltpu.CompilerParams(has_side_effects=_DATAFLOW),
    )(pltpu.with_memory_space_constraint(pack, pltpu.HBM), pltpu.with_memory_space_constraint(land, pltpu.HBM), after)


def _fetch_wait(send_sems, recv_sems, pack_thru, land_thru, after):
    def body(x_ref, land_ref, send_sems, recv_sems, after_ref, x_dead, got_ref):
        x, y, c = _place()
        for j, chip in enumerate(_other_chips(x, y)):
            cp = _remote(x_ref, land_ref.at[2 * chip[0] + chip[1]], send_sems, recv_sems, j, (*chip, c))
            cp.wait_send()
            cp.wait_recv()

    return pl.pallas_call(
        body, name="fetch_ffn_wait",
        out_shape=(pltpu.HBM(pack_thru.shape, pack_thru.dtype), pltpu.HBM(land_thru.shape, land_thru.dtype)),
        in_specs=(_HBM, _HBM, _SEMS, _SEMS, _ANY), out_specs=(_HBM, _HBM), input_output_aliases={0: 0, 1: 1},
        compiler_params=pltpu.CompilerParams(has_side_effects=_DATAFLOW),
    )(pack_thru, land_thru, send_sems, recv_sems, after)[1]


def _swap_with_sibling(block, name):
    def body(x_ref, out_ref, send_sems, recv_sems):
        x, y, c = _place()
        cp = _remote(x_ref, out_ref, send_sems, recv_sems, 0, (x, y, 1 - c))
        cp.start()
        cp.wait()

    return pl.pallas_call(
        body, name=name, in_specs=[_ANY], out_specs=_ANY, out_shape=_sds(block.shape, block.dtype),
        scratch_shapes=[pltpu.SemaphoreType.DMA((1,)), pltpu.SemaphoreType.DMA((1,))],
    )(block)


def _swap_other_half(packed, tag):
    slots, rows, width = packed.shape
    half = rows // 2

    def body(x_ref, out_ref, send_sems, recv_sems):
        x, y, c = _place()
        cp = _remote(x_ref.at[:, pl.ds((1 - c) * half, half), :], out_ref, send_sems, recv_sems, 0, (x, y, 1 - c))
        cp.start()
        cp.wait()

    return pl.pallas_call(
        body, name="swap_halves_" + tag, in_specs=[_ANY], out_specs=_ANY, out_shape=_sds((slots, half, width)),
        scratch_shapes=[pltpu.SemaphoreType.DMA((1,)), pltpu.SemaphoreType.DMA((1,))],
    )(packed)


def _add_halves(packed, got, c, tag):
    slots, rows, width = packed.shape
    half = rows // 2
    tr = _tile(half, 408, 16)
    per = half // tr
    block = (None, tr, width)

    def body(c_ref, mine_ref, got_ref, sum_ref, sum16_ref):
        acc = mine_ref[...] + got_ref[...]
        sum_ref[...] = acc
        sum16_ref[...] = acc.astype(BF16)

    plain = pl.BlockSpec(block, lambda s, i, c_ref: (s, i, 0))
    grid_spec = pltpu.PrefetchScalarGridSpec(
        num_scalar_prefetch=1, grid=(slots, per),
        in_specs=[pl.BlockSpec(block, lambda s, i, c_ref: (s, c_ref[0] * per + i, 0)), plain],
        out_specs=[plain, plain])
    return pl.pallas_call(
        body, name="add_halves_" + tag, grid_spec=grid_spec,
        out_shape=[_sds((slots, half, width)), _sds((slots, half, width), BF16)],
        compiler_params=_cp("parallel", "parallel"))(c.reshape(1).astype(jnp.int32), packed, got)


def _add_quarters(chip_sum, others, chip, tag):
    _, rows, width = chip_sum.shape
    tr = _tile(rows, 408, 16)

    def body(chip_ref, own_ref, others_ref, o_ref):
        acc = own_ref[...]
        for j in range(3):
            acc = acc + others_ref[j].astype(F32)
        o_ref[...] = acc

    grid_spec = pltpu.PrefetchScalarGridSpec(
        num_scalar_prefetch=1, grid=(rows // tr,),
        in_specs=[pl.BlockSpec((None, tr, width), lambda i, chip_ref: (chip_ref[0], i, 0)),
                  pl.BlockSpec((3, tr, width), lambda i, chip_ref: (0, i, 0))],
        out_specs=pl.BlockSpec((tr, width), lambda i, chip_ref: (i, 0)))
    return pl.pallas_call(
        body, name="add_quarters_" + tag, grid_spec=grid_spec, out_shape=_sds((rows, width)),
        compiler_params=_cp("parallel"))(chip.reshape(1).astype(jnp.int32), chip_sum, others)


def _exchange_start(parts, tag):
    _, rows, width = parts.shape

    def body(x_ref, land_ref, send_sems, recv_sems, x_thru, land_thru, token):
        x, y, c = _place()
        for j, chip in enumerate(_other_chips(x, y)):
            _remote(x_ref.at[2 * chip[0] + chip[1]], land_ref.at[j], send_sems, recv_sems, j, (*chip, c)).start()
        token[...] = jnp.zeros_like(token)

    land = lax.empty((3, rows, width), parts.dtype)
    return pl.pallas_call(
        body, name="exchange_" + tag + "_start",
        out_shape=(pltpu.SemaphoreType.DMA((3,)), pltpu.SemaphoreType.DMA((3,)), pltpu.HBM(parts.shape, parts.dtype),
                   pltpu.HBM(land.shape, land.dtype), _sds((8, 128))),
        in_specs=(_HBM, _HBM), out_specs=(_SEMS, _SEMS, _HBM, _HBM, pl.BlockSpec(memory_space=pltpu.VMEM)),
        input_output_aliases={0: 2, 1: 3}, compiler_params=pltpu.CompilerParams(has_side_effects=_DATAFLOW),
    )(pltpu.with_memory_space_constraint(parts, pltpu.HBM), pltpu.with_memory_space_constraint(land, pltpu.HBM))


def _exchange_wait(send_sems, recv_sems, parts_thru, land_thru, after, tag):
    def body(x_ref, land_ref, send_sems, recv_sems, after_ref, x_dead, got_ref):
        x, y, c = _place()
        for j, chip in enumerate(_other_chips(x, y)):
            cp = _remote(x_ref.at[2 * chip[0] + chip[1]], land_ref.at[j], send_sems, recv_sems, j, (*chip, c))
            cp.wait_send()
            cp.wait_recv()

    return pl.pallas_call(
        body, name="exchange_" + tag + "_wait",
        out_shape=(pltpu.HBM(parts_thru.shape, parts_thru.dtype), pltpu.HBM(land_thru.shape, land_thru.dtype)),
        in_specs=(_HBM, _HBM, _SEMS, _SEMS, _ANY), out_specs=(_HBM, _HBM), input_output_aliases={0: 0, 1: 1},
        compiler_params=pltpu.CompilerParams(has_side_effects=_DATAFLOW),
    )(parts_thru, land_thru, send_sems, recv_sems, after)[1]


def _allreduce_small(vec):
    rows, width = vec.shape
    vmem = pl.BlockSpec(memory_space=pltpu.VMEM)

    def body(x_ref, o_ref, buf_ref, send_sems, recv_sems):
        x, y, c = _place()
        me = 4 * x + 2 * y + c
        buf_ref[me] = x_ref[...]
        copies = []
        for k in range(1, N_DEV):
            peer = (x ^ ((k >> 2) & 1), y ^ ((k >> 1) & 1), c ^ (k & 1))
            copies.append(_remote(x_ref, buf_ref.at[me], send_sems, recv_sems, k - 1, peer))
        for cp in copies:
            cp.start()
        for k in range(1, N_DEV):
            _remote(x_ref, buf_ref.at[me ^ k], send_sems, recv_sems, k - 1, (x, y, c)).wait_recv()
        for cp in copies:
            cp.wait_send()
        total = buf_ref[0]
        for d in range(1, N_DEV):
            total = total + buf_ref[d]
        o_ref[...] = total

    return pl.pallas_call(
        body, name="allreduce_small", in_specs=[vmem], out_specs=vmem, out_shape=_sds((rows, width)),
        scratch_shapes=[pltpu.VMEM((N_DEV, rows, width), F32), pltpu.SemaphoreType.DMA((N_DEV - 1,)),
                        pltpu.SemaphoreType.DMA((N_DEV - 1,))],
    )(vec)


def _rows1024(a):
    return a.reshape(-1, 1024)


def _pad_rows(a, rows):
    return jnp.concatenate([a, jnp.zeros((rows - a.shape[0], a.shape[1]), a.dtype)], axis=0)


_TRANSPOSED = ("w_in", "w_gate", "w_up")
_SMALL_SHARDED = ("w_up_f", "w_up_b", "a_up_f", "a_up_b", "g_up")
_BIG_SHARDED = ("w_in", "w_out", "w_gate", "w_up", "w_down")


def _pack_weight_shards(w):
    conv_bits = lax.bitcast_convert_type(w["conv_w"], BF16).reshape(1, -1)
    conv_row = jnp.concatenate([conv_bits, jnp.zeros((1, 1024 - conv_bits.shape[1]), BF16)], axis=1)

    def rows(name):
        a = w[name].astype(BF16)
        return a.T if name in _TRANSPOSED else _rows1024(a)

    early = _pad_rows(jnp.concatenate([rows(name) for name, _ in _EARLY_ROWS[:-1]] + [conv_row], axis=0), EARLY_R)
    return early, jnp.concatenate([rows(name) for name, _ in _FFN_ROWS], axis=0)


def _split_rows(gathered, layout):
    out, row = {}, 0
    for name, n in layout:
        out[name] = gathered[:, row:row + n]
        row += n
    return out


def _unpack_early(gathered):
    out = _split_rows(gathered, _EARLY_ROWS)
    cols = lambda a, k: jnp.concatenate([a[s].reshape(k, -1) for s in range(N_SHARD)], axis=1)
    conv = lax.bitcast_convert_type(out["conv_w"][:, 0, :768].reshape(N_SHARD, 3, 128, 2), F32)
    full = dict(w_in=out["w_in"].reshape(-1, 1024).T, w_out=out["w_out"].reshape(D_MODEL, D_MODEL),
                conv_w=jnp.concatenate([conv[s] for s in range(N_SHARD)], axis=1))
    full.update({name: cols(out[name], D_GATE if name == "g_up" else D_LORA) for name in _SMALL_SHARDED})
    return full


def _unpack_ffn(gathered):
    out = _split_rows(gathered, _FFN_ROWS)
    return dict(w_gate=out["w_gate"].reshape(-1, 1024).T, w_up=out["w_up"].reshape(-1, 1024).T,
                w_down=out["w_down"].reshape(D_FF, D_MODEL))


def _pack_grads(g, layout, rows):
    col_split = lambda a, s: a[:, s * (a.shape[1] // N_SHARD):(s + 1) * (a.shape[1] // N_SHARD)]
    row_split = lambda a, s: a[s * (a.shape[0] // N_SHARD):(s + 1) * (a.shape[0] // N_SHARD)]
    by_rows = {name: (g[name].T if name in _TRANSPOSED else g[name]) for name, _ in layout if name in _BIG_SHARDED}
    used = sum(n for _, n in layout)
    parts = []
    for s in range(N_SHARD):
        for name, _ in layout:
            if name in _BIG_SHARDED:
                parts.append(row_split(by_rows[name], s))
            elif name in _SMALL_SHARDED:
                parts.append(_rows1024(col_split(g[name], s)))
            else:
                conv = col_split(g["conv_w"], s).reshape(1, -1)
                parts.append(jnp.concatenate([conv, jnp.zeros((1, 1024 - conv.shape[1]), F32)], axis=1))
        if rows > used:
            parts.append(jnp.zeros((rows - used, 1024), F32))
    return jnp.concatenate(parts, axis=0).reshape(N_SHARD, rows, 1024)


def _unpack_grad_shard(pack, layout):
    small_shapes = {name: (D_GATE if name == "g_up" else D_LORA, 128) for name in _SMALL_SHARDED}
    out, row = {}, 0
    for name, n in layout:
        piece = pack[row:row + n]
        if name == "conv_w":
            out[name] = piece[0, :384].reshape(3, 128)
        else:
            out[name] = piece.T if name in _TRANSPOSED else piece.reshape(small_shapes.get(name, piece.shape))
        row += n
    return out


_SMALL_LAYOUT = (("norm1_w", 1024), ("mu_shift", D_SHIFT), ("w0_f", 512), ("w0_b", 512), ("a0_f", 512),
                 ("a0_b", 512), ("k_k", 512), ("k_a_f", 512), ("k_a_b", 512), ("r_k_f", 512), ("r_k_b", 512),
                 ("gn_w", 512), ("gn_b", 512), ("norm2_w", 1024), ("norm_f_w", 1024), ("loss", 1))


def _pack_small(vals):
    rows = []
    for name, n in _SMALL_LAYOUT:
        flat = vals[name].reshape(-1)
        n_rows = -(-n // 1024)
        rows.append(jnp.concatenate([flat, jnp.zeros((n_rows * 1024 - n,), F32)]).reshape(n_rows, 1024))
    return _pad_rows(jnp.concatenate(rows, axis=0), SMALL_ROWS)


def _unpack_small(pack):
    out, row = {}, 0
    for name, n in _SMALL_LAYOUT:
        n_rows = -(-n // 1024)
        out[name] = pack[row:row + n_rows].reshape(-1)[:n]
        row += n_rows
    return out


_WEIGHTS = ("norm1_w", "w_in", "mu_shift", "w_up_f", "w0_f", "w_up_b", "w0_b", "a_up_f", "a0_f", "a_up_b", "a0_b",
            "g_up", "k_k", "k_a_f", "k_a_b", "r_k_f", "r_k_b", "gn_w", "gn_b", "conv_w", "w_out", "norm2_w",
            "w_gate", "w_up", "w_down", "norm_f_w")


def _train_step(x, loss_target, w, m, v):
    batch, seq, _ = x.shape
    t = batch * seq
    tm = _tile(seq, 256, 8)
    xs = x.reshape(t, D_MODEL)
    target = loss_target.reshape(t, D_MODEL)
    vec = lambda name: w[name].reshape(1, -1)

    local = {name: w[name][0] for name, _ in _PACK_ROWS}
    c = lax.axis_index("c")
    chip = 2 * lax.axis_index("x") + lax.axis_index("y")
    early, ffn_pack = _pack_weight_shards(local)
    early_all = lax.dynamic_update_slice(_gather_weights(early), early[None], (chip, 0, 0))
    ffn_send, ffn_recv, ffn_pack, ffn_land, token = _fetch_start(ffn_pack, early_all)
    full = _unpack_early(early_all)
    w_in = full["w_in"]
    w_shift = jnp.concatenate([w_in[:, :D_SHIFT], jnp.zeros((D_MODEL, D_SHIFT_PAD - D_SHIFT), BF16)], axis=1)
    w_conv = w_in[:, D_SHIFT:]
    zeros_lora = jnp.zeros((D_LORA, D_RWKV), F32)
    lora = lambda name: full[name].astype(F32)
    mats = (jnp.concatenate([lora("w_up_f"), zeros_lora]), jnp.concatenate([zeros_lora, lora("a_up_f")]),
            jnp.concatenate([lora("w_up_b"), zeros_lora]), jnp.concatenate([zeros_lora, lora("a_up_b")]),
            jnp.concatenate([lora("g_up"), jnp.zeros((D_GATE_PAD - D_GATE, D_RWKV), F32)]))
    mu = jnp.concatenate([vec("mu_shift"), jnp.zeros((1, D_SHIFT_PAD - D_SHIFT), F32)], axis=1)
    mu = jnp.broadcast_to(mu, (GROUP, D_SHIFT_PAD))
    zero_row = jnp.zeros((1, D_RWKV), F32)
    pvec = jnp.concatenate([vec("k_k"), vec("w0_f"), vec("a0_f"), vec("k_a_f"), vec("w0_b"), vec("a0_b"),
                            vec("k_a_b"), zero_row], axis=0)
    qvec = jnp.concatenate([vec("gn_w"), vec("gn_b"), vec("r_k_f"), vec("r_k_b"), full["conv_w"], zero_row], axis=0)
    ones_blocks = _head_ones()

    h1_t, p_shift, pc = _norm_in_proj(xs, vec("norm1_w") + token[0, 0], w_shift, w_conv, tm)
    ps, kk, w_f, kd_f, b_f, w_b, kd_b, b_b, gate = _shift_prep_fwd(p_shift, mu, pvec, mats, ones_blocks, tm, seq)
    dirs = ((w_f, kd_f, b_f), (w_b, kd_b, b_b))
    y_f, y_b, hist_f, hist_b = _scan_fwd(ps, kk, dirs, batch, seq)
    mixed, mixed_t = _post_fwd(y_f, y_b, ps, kd_f, kd_b, gate, pc, qvec, ones_blocks, tm, seq)
    x1, h2, h2_t = _out_proj_norm(mixed, full["w_out"], xs, vec("norm2_w"), tm)
    ffn_all = _fetch_wait(ffn_send, ffn_recv, ffn_pack, ffn_land, h2)
    full.update(_unpack_ffn(lax.dynamic_update_slice(ffn_all, ffn_pack[None], (chip, 0, 0))))
    ff_gate, ff_up, act, act_t = _ffn_in(h2, full["w_gate"], full["w_up"])
    d_x2, d_norm_f, loss_part = _ffn_out_loss(act, full["w_down"], x1, w["norm_f_w"].reshape(1, -1), target, tm)

    g = {}
    g["w_down"] = _matmul(act_t, d_x2, mode="nn", name="ffn_down_dw")
    d_gate, d_up = _ffn_in_bwd(d_x2, full["w_down"], ff_gate, ff_up)
    g["w_gate"] = _matmul(h2_t, d_gate, mode="nn", name="ffn_gate_dw")
    g["w_up"] = _matmul(h2_t, d_up, mode="nn", name="ffn_up_dw")
    ffn_grads = _pack_grads(g, _FFN_ROWS, sum(n for _, n in _FFN_ROWS))
    ffn_sum, ffn_sum_bf16 = _add_halves(ffn_grads, _swap_other_half(ffn_grads, "ffn"), c, "ffn")
    ex_send, ex_recv, ffn_sum_bf16, ex_land, ex_token = _exchange_start(ffn_sum_bf16, "ffn")
    d_x1, d_norm2 = _proj_norm_bwd(d_gate, full["w_gate"], d_up, full["w_up"], x1, vec("norm2_w") + ex_token[0, 0],
                                   d_x2, tm, "ffn_in_dx_norm2_bwd")
    d_mixed = _matmul(d_x1, full["w_out"], mode="nt", name="out_proj_dx")
    g["w_out"] = _matmul(mixed_t, d_x1, mode="nn", name="out_proj_dw")
    dy, dr_o, dkdf_o, dkdb_o, dv_o, d_gatev, d_pc, d_qvec = _post_bwd(
        d_mixed, y_f, y_b, ps, kd_f, kd_b, gate, pc, qvec, ones_blocks, tm, seq)
    (dr_f, dw_f, dkd_f, dv_f, dkk_f, db_f), (dr_b, dw_b, dkd_b, dv_b, dkk_b, db_b) = _scan_bwd(
        ps, kk, dirs, dy, hist_f, hist_b, batch, seq)
    cts = [[dr_f, dr_b, dr_o], [dv_f, dv_b, dv_o], [dkk_f, dkk_b], [dw_f], [dkd_f, dkdf_o], [db_f],
           [dw_b], [dkd_b, dkdb_o], [db_b], [d_gatev]]
    q, d_pvec, d_m0, d_m1, d_m2, d_m3, d_m4 = _prep_bwd(ps, pvec, mats, ones_blocks, cts, tm)
    d_pshift, d_mu = _shift_bwd(q, p_shift, mu, tm, seq)
    d_w_shift = _matmul(h1_t, d_pshift, mode="nn", name="in_proj_shift_dw")
    d_w_conv = _matmul(h1_t, d_pc, mode="nn", name="in_proj_conv_dw")
    g["w_in"] = jnp.concatenate([d_w_shift[:, :D_SHIFT], d_w_conv], axis=1)
    d_x, d_norm1 = _proj_norm_bwd(d_pshift, w_shift, d_pc, w_conv, xs, vec("norm1_w"), d_x1, tm,
                                  "in_proj_dx_norm1_bwd")
    g["w_up_f"], g["a_up_f"] = d_m0[:D_LORA], d_m1[D_LORA:]
    g["w_up_b"], g["a_up_b"] = d_m2[:D_LORA], d_m3[D_LORA:]
    g["g_up"] = d_m4[:D_GATE]
    g["conv_w"] = d_qvec[4:7]

    def finish(chip_sum, others, tag, layout):
        eighth = _add_quarters(chip_sum, others, chip, tag)
        other_eighth = _swap_with_sibling(eighth, "swap_eighths_" + tag)
        return _unpack_grad_shard(jnp.concatenate([jnp.where(c == 0, eighth, other_eighth),
                                                   jnp.where(c == 0, other_eighth, eighth)], axis=0), layout)

    as2d = lambda name: (1, w[name].shape[0]) if w[name].ndim == 1 else w[name].shape
    operands = lambda name: tuple(a.reshape(as2d(name)) for a in (w[name], grads[name], m[name], v[name]))

    packed = _pack_grads(g, _EARLY_ROWS, EARLY_R)
    mix_sum, mix_sum_bf16 = _add_halves(packed, _swap_other_half(packed, "mixer"), c, "mixer")
    mx_send, mx_recv, mix_sum_bf16, mx_land, mx_token = _exchange_start(mix_sum_bf16, "mixer")
    grads = finish(ffn_sum, _exchange_wait(ex_send, ex_recv, ffn_sum_bf16, ex_land, mx_token, "ffn"), "ffn", _FFN_ROWS)
    updates = {name: _adamw(*operands(name), "adamw_" + name) for name in _FFN_NAMES}
    mix_others = _exchange_wait(mx_send, mx_recv, mix_sum_bf16, mx_land, updates["w_down"][2], "mixer")
    grads.update(finish(mix_sum, mix_others, "mixer", _EARLY_ROWS))

    small = dict(norm1_w=d_norm1, mu_shift=d_mu[:, :D_SHIFT], w0_f=d_pvec[1], w0_b=d_pvec[4], a0_f=d_pvec[2],
                 a0_b=d_pvec[5], k_k=d_pvec[0], k_a_f=d_pvec[3], k_a_b=d_pvec[6], r_k_f=d_qvec[2], r_k_b=d_qvec[3],
                 gn_w=d_qvec[0], gn_b=d_qvec[1], norm2_w=d_norm2, norm_f_w=d_norm_f, loss=loss_part)
    reduced = _unpack_small(_allreduce_small(_pack_small(small)))
    loss = reduced.pop("loss")[0]
    grads.update(reduced)

    outs = {}
    small = [name for name in _WEIGHTS if name not in _BIG_SHARDED]
    updates.update(zip(small, _adamw_small([operands(name) for name in small])))
    for name in ("w_in", "w_out"):
        updates[name] = _adamw(*operands(name), "adamw_" + name)
    for name in _WEIGHTS:
        shape = w[name].shape
        outs[name] = (grads[name].reshape(shape),) + tuple(a.reshape(shape) for a in updates[name])
    d_x = d_x.reshape(batch, seq, D_MODEL)
    return (loss, d_x) + tuple(outs[name][k] for k in range(4) for name in _WEIGHTS)


def kernel(x, norm1_w, w_in, mu_shift, w_up_f, w0_f, w_up_b, w0_b, a_up_f, a0_f, a_up_b, a0_b, g_up, k_k, k_a_f, k_a_b, r_k_f, r_k_b, gn_w, gn_b, conv_w, w_out, norm2_w, w_gate, w_up, w_down, norm_f_w, loss_target, m_norm1_w, m_w_in, m_mu_shift, m_w_up_f, m_w0_f, m_w_up_b, m_w0_b, m_a_up_f, m_a0_f, m_a_up_b, m_a0_b, m_g_up, m_k_k, m_k_a_f, m_k_a_b, m_r_k_f, m_r_k_b, m_gn_w, m_gn_b, m_conv_w, m_w_out, m_norm2_w, m_w_gate, m_w_up, m_w_down, m_norm_f_w, v_norm1_w, v_w_in, v_mu_shift, v_w_up_f, v_w0_f, v_w_up_b, v_w0_b, v_a_up_f, v_a0_f, v_a_up_b, v_a0_b, v_g_up, v_k_k, v_k_a_f, v_k_a_b, v_r_k_f, v_r_k_b, v_gn_w, v_gn_b, v_conv_w, v_w_out, v_norm2_w, v_w_gate, v_w_up, v_w_down, v_norm_f_w):
    args = locals()
    w = {name: args[name] for name in _WEIGHTS}
    m = {name: args["m_" + name] for name in _WEIGHTS}
    v = {name: args["v_" + name] for name in _WEIGHTS}
    return _train_step(x, loss_target, w, m, v)
```

```python
import functools

import jax
import jax.numpy as jnp
from jax import lax
from jax.experimental import pallas as pl
from jax.experimental.pallas import tpu as pltpu

F32 = jnp.float32
BF16 = jnp.bfloat16
MESH = pl.DeviceIdType.MESH

D_MODEL = 1024
D_RWKV = 512
HEAD = 64
N_PAIR = D_RWKV // (2 * HEAD)
D_LORA = 64
D_GATE = 160
D_GATE_PAD = 384
D_FF = 2816
D_SHIFT = 1824
D_SHIFT_PAD = 2048
D_CONV3 = 1536
LOG_DECAY_SCALE = 0.606531
RMS_EPS = 1e-6
GN_EPS = 64e-5
NORM_EPS = 1e-12
ADAM_LR, ADAM_B1, ADAM_B2, ADAM_EPS, ADAM_WD, ADAM_STEP = 0.001, 0.9, 0.999, 1e-08, 0.01, 10

N_SHARD = 4
N_DEV = 8
V7X_VMEM_LIMIT = 48 * 1024 * 1024
SCAN_CHUNK = 64
GROUP = 8

_PACK_ROWS = (("w_in", 840), ("w_out", 256), ("w_gate", 704), ("w_up", 704), ("w_down", 704),
              ("w_up_f", 8), ("w_up_b", 8), ("a_up_f", 8), ("a_up_b", 8), ("g_up", 20), ("conv_w", 1))
_FFN_NAMES = ("w_gate", "w_up", "w_down")
_EARLY_ROWS = tuple(item for item in _PACK_ROWS if item[0] not in _FFN_NAMES)
_FFN_ROWS = tuple(item for item in _PACK_ROWS if item[0] in _FFN_NAMES)
EARLY_R = 1152
SMALL_ROWS = 24


def _tile(n, cap, mult=128):
    best = None
    t = mult
    while t <= min(n, cap):
        if n % t == 0:
            best = t
        t += mult
    return best or n


def _cp(*sem):
    return pltpu.CompilerParams(dimension_semantics=sem or None, vmem_limit_bytes=V7X_VMEM_LIMIT)


def _sds(shape, dtype=F32):
    return jax.ShapeDtypeStruct(shape, dtype)


def _matmul(a, b, *, mode, name, out_dtype=F32, add=None):
    m, kdim = a.shape
    n = b.shape[1] if mode == "nn" else b.shape[0]
    tm, tn = _tile(m, 512, 8), _tile(n, 1536)
    tk = kdim if kdim <= 3584 else _tile(kdim, 1024)
    nk = kdim // tk
    a_spec = pl.BlockSpec((tm, tk), lambda i, j, k: (i, k))
    if mode == "nn":
        b_spec = pl.BlockSpec((tk, tn), lambda i, j, k: (k, j))
        dims = (((1,), (0,)), ((), ()))
    else:
        b_spec = pl.BlockSpec((tn, tk), lambda i, j, k: (j, k))
        dims = (((1,), (1,)), ((), ()))
    has_add = add is not None

    def body(*refs):
        a_ref, b_ref = refs[0], refs[1]
        add_ref = refs[2] if has_add else None
        o_ref = refs[3] if has_add else refs[2]
        part = lax.dot_general(a_ref[...].astype(BF16), b_ref[...].astype(BF16), dims,
                               preferred_element_type=F32)
        if nk == 1:
            if has_add:
                part = part + add_ref[...]
            o_ref[...] = part.astype(out_dtype)
        else:
            acc_ref = refs[-1]
            k = pl.program_id(2)

            @pl.when(k == 0)
            def _():
                acc_ref[...] = jnp.zeros_like(acc_ref)

            acc_ref[...] += part

            @pl.when(k == nk - 1)
            def _():
                res = acc_ref[...]
                if has_add:
                    res = res + add_ref[...]
                o_ref[...] = res.astype(out_dtype)

    o_spec = pl.BlockSpec((tm, tn), lambda i, j, k: (i, j))
    in_specs = [a_spec, b_spec] + ([o_spec] if has_add else [])
    args = (a, b) + ((add,) if has_add else ())
    return pl.pallas_call(
        body, name=name, grid=(m // tm, n // tn, nk), in_specs=in_specs, out_specs=o_spec,
        out_shape=_sds((m, n), out_dtype),
        scratch_shapes=[pltpu.VMEM((tm, tn), F32)] if nk > 1 else [],
        compiler_params=_cp("parallel", "parallel", "arbitrary"),
    )(*args)


def _row(tm, width):
    return pl.BlockSpec((tm, width), lambda i: (i, 0))


def _col(tm, height):
    return pl.BlockSpec((height, tm), lambda i: (0, i))


def _fixed(shape):
    return pl.BlockSpec(shape, lambda i: tuple(0 for _ in shape))


def _rmsnorm_tile(xv, wv):
    return xv * lax.rsqrt(jnp.mean(xv * xv, axis=-1, keepdims=True) + RMS_EPS) * wv


def _norm_in_proj(x, w, w_shift, w_conv, tm):
    t, d = x.shape
    n_a, n_b = w_shift.shape[1], w_conv.shape[1]

    def body(x_ref, w_ref, wa_ref, wb_ref, ht_ref, pa_ref, pb_ref):
        hv = _rmsnorm_tile(x_ref[...], w_ref[...])
        ht_ref[...] = jnp.transpose(hv).astype(BF16)
        hb = hv.astype(BF16)
        pa_ref[...] = jnp.dot(hb, wa_ref[...], preferred_element_type=F32)
        pb_ref[...] = jnp.dot(hb, wb_ref[...], preferred_element_type=F32)

    return pl.pallas_call(
        body, name="norm1_in_proj", grid=(t // tm,),
        in_specs=[_row(tm, d), _fixed((1, d)), _fixed((d, n_a)), _fixed((d, n_b))],
        out_specs=[_col(tm, d), _row(tm, n_a), _row(tm, n_b)],
        out_shape=[_sds((d, t), BF16), _sds((t, n_a)), _sds((t, n_b))],
        compiler_params=_cp("parallel"))(x, w, w_shift, w_conv)


def _out_proj_norm(mixed, w_out, res, w, tm):
    t, d = res.shape

    def body(m_ref, wo_ref, r_ref, w_ref, x_ref, h_ref, ht_ref):
        xv = r_ref[...] + jnp.dot(m_ref[...], wo_ref[...], preferred_element_type=F32)
        x_ref[...] = xv
        hv = _rmsnorm_tile(xv, w_ref[...])
        h_ref[...] = hv.astype(BF16)
        ht_ref[...] = jnp.transpose(hv).astype(BF16)

    return pl.pallas_call(
        body, name="out_proj_norm2", grid=(t // tm,),
        in_specs=[_row(tm, d), _fixed((d, d)), _row(tm, d), _fixed((1, d))],
        out_specs=[_row(tm, d), _row(tm, d), _col(tm, d)],
        out_shape=[_sds((t, d)), _sds((t, d), BF16), _sds((d, t), BF16)],
        compiler_params=_cp("parallel"))(mixed, w_out, res, w)


def _rms_bwd_math(xv, wv, dyv):
    rstd = lax.rsqrt(jnp.mean(xv * xv, axis=-1, keepdims=True) + RMS_EPS)
    xhat = xv * rstd
    gv = dyv * wv
    dx = rstd * (gv - xhat * jnp.mean(gv * xhat, axis=-1, keepdims=True))
    return dx, jnp.sum(dyv * xhat, axis=0, keepdims=True)


def _proj_norm_bwd(dy_a, w_a, dy_b, w_b, x, w, dres, tm, name):
    t, d = x.shape
    ka, kb = dy_a.shape[1], dy_b.shape[1]
    nt = (((1,), (1,)), ((), ()))

    def body(dya_ref, wa_ref, dyb_ref, wb_ref, x_ref, w_ref, dres_ref, dx_ref, dw_ref):
        d_h = (lax.dot_general(dya_ref[...], wa_ref[...], nt, preferred_element_type=F32)
               + lax.dot_general(dyb_ref[...], wb_ref[...], nt, preferred_element_type=F32))
        dx, dw = _rms_bwd_math(x_ref[...], w_ref[...], d_h)
        dx_ref[...] = dres_ref[...] + dx

        @pl.when(pl.program_id(0) == 0)
        def _():
            dw_ref[...] = jnp.zeros_like(dw_ref)

        dw_ref[...] += dw

    return pl.pallas_call(
        body, name=name, grid=(t // tm,),
        in_specs=[_row(tm, ka), _fixed((d, ka)), _row(tm, kb), _fixed((d, kb)), _row(tm, d), _fixed((1, d)),
                  _row(tm, d)],
        out_specs=[_row(tm, d), _fixed((1, d))],
        out_shape=[_sds((t, d)), _sds((1, d))], compiler_params=_cp("arbitrary"))(dy_a, w_a, dy_b, w_b, x, w, dres)


def _ffn_out_loss(act, w_down, x1, w, target, tm):
    t, d = x1.shape
    f = act.shape[1]

    def body(a_ref, wd_ref, x_ref, w_ref, t_ref, dx_ref, dw_ref, loss_ref):
        xv = x_ref[...] + jnp.dot(a_ref[...], wd_ref[...], preferred_element_type=F32)
        wv = w_ref[...]
        rstd = lax.rsqrt(jnp.mean(xv * xv, axis=-1, keepdims=True) + RMS_EPS)
        err = xv * rstd * wv - t_ref[...]
        dx, dw = _rms_bwd_math(xv, wv, err * (1.0 / d))
        dx_ref[...] = dx

        @pl.when(pl.program_id(0) == 0)
        def _():
            dw_ref[...] = jnp.zeros_like(dw_ref)
            loss_ref[...] = jnp.zeros_like(loss_ref)

        dw_ref[...] += dw
        loss_ref[...] += 0.5 * jnp.sum(jnp.mean(err * err, axis=-1, keepdims=True), axis=0, keepdims=True)

    return pl.pallas_call(
        body, name="ffn_out_loss", grid=(t // tm,),
        in_specs=[_row(tm, f), _fixed((f, d)), _row(tm, d), _fixed((1, d)), _row(tm, d)],
        out_specs=[_row(tm, d), _fixed((1, d)), _fixed((1, 1))],
        out_shape=[_sds((t, d)), _sds((1, d)), _sds((1, 1))], compiler_params=_cp("arbitrary"))(act, w_down, x1, w, target)


def _ffn_in(h, w_gate, w_up):
    t, d = h.shape
    f = w_gate.shape[1]
    tm, tn = _tile(t, 512, 8), _tile(f, 1536)

    def body(h_ref, wg_ref, wu_ref, g_ref, u_ref, a_ref, at_ref):
        hv = h_ref[...]
        gv = jnp.dot(hv, wg_ref[...], preferred_element_type=F32)
        uv = jnp.dot(hv, wu_ref[...], preferred_element_type=F32)
        act = gv * jax.nn.sigmoid(gv) * uv
        g_ref[...] = gv.astype(BF16)
        u_ref[...] = uv.astype(BF16)
        a_ref[...] = act.astype(BF16)
        at_ref[...] = jnp.transpose(act).astype(BF16)

    w_spec = pl.BlockSpec((d, tn), lambda i, j: (0, j))
    o_spec = pl.BlockSpec((tm, tn), lambda i, j: (i, j))
    return pl.pallas_call(
        body, name="ffn_in", grid=(t // tm, f // tn),
        in_specs=[pl.BlockSpec((tm, d), lambda i, j: (i, 0)), w_spec, w_spec],
        out_specs=[o_spec, o_spec, o_spec, pl.BlockSpec((tn, tm), lambda i, j: (j, i))],
        out_shape=[_sds((t, f), BF16)] * 3 + [_sds((f, t), BF16)],
        compiler_params=_cp("parallel", "parallel"))(h, w_gate, w_up)


def _ffn_in_bwd(d_out, w_down, gate, up):
    t, d = d_out.shape
    f = w_down.shape[0]
    tm, tn = _tile(t, 512, 8), _tile(f, 1536)

    def body(do_ref, w_ref, g_ref, u_ref, dg_ref, du_ref):
        dv = lax.dot_general(do_ref[...].astype(BF16), w_ref[...], (((1,), (1,)), ((), ())),
                             preferred_element_type=F32)
        gv, uv = g_ref[...].astype(F32), u_ref[...].astype(F32)
        sg = jax.nn.sigmoid(gv)
        du_ref[...] = (dv * gv * sg).astype(BF16)
        dg_ref[...] = (dv * uv * (sg * (1.0 + gv * (1.0 - sg)))).astype(BF16)

    tile = pl.BlockSpec((tm, tn), lambda i, j: (i, j))
    return pl.pallas_call(
        body, name="ffn_in_bwd", grid=(t // tm, f // tn),
        in_specs=[pl.BlockSpec((tm, d), lambda i, j: (i, 0)), pl.BlockSpec((tn, d), lambda i, j: (j, 0)), tile, tile],
        out_specs=[tile, tile], out_shape=[_sds((t, f), BF16)] * 2,
        compiler_params=_cp("parallel", "parallel"))(d_out, w_down, gate, up)


def _halo_specs(tm, width, rows_total):
    per = tm // GROUP
    last = rows_total // GROUP - 1
    prev = pl.BlockSpec((GROUP, width), lambda i: (jnp.maximum(i * per - 1, 0), 0))
    nxt = pl.BlockSpec((GROUP, width), lambda i: (jnp.minimum((i + 1) * per, last), 0))
    return prev, nxt


def _edge_flags(tm, seq):
    i = pl.program_id(0)
    has_prev = jnp.where((i * tm) % seq == 0, 0.0, 1.0).astype(F32)
    has_next = jnp.where(((i + 1) * tm) % seq == 0, 0.0, 1.0).astype(F32)
    return has_prev, has_next


def _shifted(xv, prev_row, next_row):
    tm = xv.shape[0]
    row = lax.broadcasted_iota(jnp.int32, xv.shape, 0)
    down = jnp.where(row == 0, prev_row, pltpu.roll(xv, 1, axis=0))
    up = jnp.where(row == tm - 1, next_row, pltpu.roll(xv, tm - 1, axis=0))
    return down, up


def _shift_bwd(q, p, mu, tm, seq):
    t, w = p.shape
    prev_spec, next_spec = _halo_specs(tm, w, t)

    def body(q_ref, qp_ref, qn_ref, p_ref, pp_ref, pn_ref, mu_ref, dp_ref, dmu_ref):
        has_prev, has_next = _edge_flags(tm, seq)
        muv = mu_ref[0:1, :]
        qv = q_ref[...]
        mq = muv * qv
        mq_down, mq_up = _shifted(mq, muv * qp_ref[GROUP - 1:GROUP, :] * has_prev,
                                  muv * qn_ref[0:1, :] * has_next)
        dp_ref[...] = (qv - mq + 0.5 * (mq_down + mq_up)).astype(BF16)
        pv = p_ref[...]
        p_down, p_up = _shifted(pv, pp_ref[GROUP - 1:GROUP, :] * has_prev, pn_ref[0:1, :] * has_next)

        @pl.when(pl.program_id(0) == 0)
        def _():
            dmu_ref[...] = jnp.zeros_like(dmu_ref)

        dmu_ref[...] += jnp.sum(qv * (0.5 * (p_down + p_up) - pv), axis=0, keepdims=True)

    return pl.pallas_call(
        body, name="shift_bwd", grid=(t // tm,),
        in_specs=[_row(tm, w), prev_spec, next_spec, _row(tm, w), prev_spec, next_spec, _fixed((GROUP, w))],
        out_specs=[_row(tm, w), _fixed((1, w))],
        out_shape=[_sds((t, w), BF16), _sds((1, w))], compiler_params=_cp("arbitrary"))(q, q, q, p, p, p, mu)


@jax.custom_vjp
def _bdot(a, b):
    return jnp.dot(a.astype(BF16), b.astype(BF16), preferred_element_type=F32)


def _bdot_fwd(a, b):
    return _bdot(a, b), (a, b)


def _bdot_bwd(res, g):
    a, b = res
    gb = g.astype(BF16)
    da = lax.dot_general(gb, b.astype(BF16), (((1,), (1,)), ((), ())), preferred_element_type=F32)
    db = lax.dot_general(a.astype(BF16), gb, (((0,), (0,)), ((), ())), preferred_element_type=F32)
    return da, db


_bdot.defvjp(_bdot_fwd, _bdot_bwd)


def _seg_raw(x, ones_blocks):
    hi = x.astype(BF16)
    lo = (x - hi.astype(F32)).astype(BF16)
    return (jnp.dot(hi, ones_blocks, preferred_element_type=F32)
            + jnp.dot(lo, ones_blocks, preferred_element_type=F32))


@jax.custom_vjp
def _seg(x, ones_blocks):
    return _seg_raw(x, ones_blocks)


def _seg_fwd(x, ones_blocks):
    return _seg_raw(x, ones_blocks), ones_blocks


def _seg_bwd(ones_blocks, g):
    return _seg_raw(g, ones_blocks), jnp.zeros_like(ones_blocks)


_seg.defvjp(_seg_fwd, _seg_bwd)


def _head_ones():
    h = jnp.arange(D_RWKV) // HEAD
    return (h[:, None] == h[None, :]).astype(BF16)


def _prep_math(ps, k_k, w0_f, a0_f, k_a_f, w0_b, a0_b, k_a_b, wup_f, aup_f, wup_b, aup_b, gup, ones_blocks):
    r = ps[:, 0:512]
    k = ps[:, 512:1024]
    v = ps[:, 1024:1536]
    xwa = ps[:, 1536:1664]
    xg = ps[:, 1664:D_SHIFT_PAD]
    kk_raw = k * k_k
    norm = jnp.sqrt(_seg(kk_raw * kk_raw, ones_blocks))
    kk = kk_raw / jnp.maximum(norm, NORM_EPS)
    t_xwa = jnp.tanh(xwa)
    outs = [r, v, kk]
    for w0, a0, k_a, wup, aup in ((w0_f, a0_f, k_a_f, wup_f, aup_f), (w0_b, a0_b, k_a_b, wup_b, aup_b)):
        decay = jnp.exp(-LOG_DECAY_SCALE * jax.nn.sigmoid(w0 + _bdot(t_xwa, wup)))
        rate = jax.nn.sigmoid(a0 + _bdot(xwa, aup))
        outs += [decay, k * (1.0 + (rate - 1.0) * k_a), kk * rate]
    outs.append(_bdot(jax.nn.sigmoid(xg), gup))
    return tuple(outs)


def _prep_args(tm, ps_tile, pv_ref, mat_refs, ones_ref):
    vecs = [jnp.broadcast_to(pv_ref[j:j + 1, :], (tm, D_RWKV)) for j in range(7)]
    return [ps_tile] + vecs + [m[...] for m in mat_refs] + [ones_ref[...]]


_PREP_MAT_SHAPES = ((128, D_RWKV),) * 4 + ((D_GATE_PAD, D_RWKV),)


def _shift_prep_fwd(p, mu, pvec, mats, ones_blocks, tm, seq):
    t, w = p.shape
    prev_spec, next_spec = _halo_specs(tm, w, t)

    def body(p_ref, hp_ref, hn_ref, mu_ref, pv_ref, m0, m1, m2, m3, m4, ones_ref, ps_ref, *out_refs):
        has_prev, has_next = _edge_flags(tm, seq)
        xv = p_ref[...]
        down, up = _shifted(xv, hp_ref[GROUP - 1:GROUP, :] * has_prev, hn_ref[0:1, :] * has_next)
        ps_tile = xv + mu_ref[0:1, :] * (0.5 * (down + up) - xv)
        ps_ref[...] = ps_tile
        outs = _prep_math(*_prep_args(tm, ps_tile, pv_ref, (m0, m1, m2, m3, m4), ones_ref))
        for o_ref, val in zip(out_refs, outs[2:]):
            o_ref[...] = val

    return pl.pallas_call(
        body, name="shift_prep_fwd", grid=(t // tm,),
        in_specs=[_row(tm, w), prev_spec, next_spec, _fixed((GROUP, w)), _fixed((8, D_RWKV))]
        + [_fixed(s) for s in _PREP_MAT_SHAPES] + [_fixed((D_RWKV, D_RWKV))],
        out_specs=[_row(tm, w)] + [_row(tm, D_RWKV)] * 8, out_shape=[_sds((t, w))] + [_sds((t, D_RWKV))] * 8,
        compiler_params=_cp("parallel"))(p, p, p, mu, pvec, *mats, ones_blocks)


def _prep_bwd(ps, pvec, mats, ones_blocks, cts, tm):
    t = ps.shape[0]
    counts = [len(c) for c in cts]
    flat = [a for c in cts for a in c]

    def body(ps_ref, pv_ref, m0, m1, m2, m3, m4, ones_ref, *refs):
        ct_refs = refs[:len(flat)]
        q_ref, dpv_ref = refs[len(flat)], refs[len(flat) + 1]
        dmat_refs = refs[len(flat) + 2:]
        args = _prep_args(tm, ps_ref[...], pv_ref, (m0, m1, m2, m3, m4), ones_ref)
        _, vjp = jax.vjp(lambda *a: _prep_math(*a, args[-1]), *args[:-1])
        ct_vals, pos = [], 0
        for n in counts:
            val = ct_refs[pos][...]
            for extra in ct_refs[pos + 1:pos + n]:
                val = val + extra[...]
            ct_vals.append(val)
            pos += n
        grads = vjp(tuple(ct_vals))
        q_ref[...] = grads[0]

        @pl.when(pl.program_id(0) == 0)
        def _():
            dpv_ref[...] = jnp.zeros_like(dpv_ref)
            for d_ref in dmat_refs:
                d_ref[...] = jnp.zeros_like(d_ref)

        for j in range(7):
            dpv_ref[j:j + 1, :] += jnp.sum(grads[1 + j], axis=0, keepdims=True)
        for d_ref, gm in zip(dmat_refs, grads[8:13]):
            d_ref[...] += gm

    return pl.pallas_call(
        body, name="prep_bwd", grid=(t // tm,),
        in_specs=[_row(tm, D_SHIFT_PAD), _fixed((8, D_RWKV))] + [_fixed(s) for s in _PREP_MAT_SHAPES]
        + [_fixed((D_RWKV, D_RWKV))] + [_row(tm, D_RWKV)] * len(flat),
        out_specs=[_row(tm, D_SHIFT_PAD), _fixed((8, D_RWKV))] + [_fixed(s) for s in _PREP_MAT_SHAPES],
        out_shape=[_sds((t, D_SHIFT_PAD)), _sds((8, D_RWKV))] + [_sds(s) for s in _PREP_MAT_SHAPES],
        compiler_params=_cp("arbitrary"))(ps, pvec, *mats, ones_blocks, *flat)


def _pair_ones():
    h = jnp.arange(2 * HEAD) // HEAD
    block = (h[:, None] == h[None, :]).astype(BF16)
    return jnp.concatenate([block, block], axis=0)


def _diag_mask():
    lane = lax.broadcasted_iota(jnp.int32, (HEAD, 2 * HEAD), 1)
    sub = lax.broadcasted_iota(jnp.int32, (HEAD, 2 * HEAD), 0)
    return jnp.where((lane & (HEAD - 1)) == sub, 1.0, 0.0).astype(F32)


def _to_row(cols, dmask):
    return jnp.sum(cols * dmask, axis=0, keepdims=True)


def _seg_many(exact, rounded, ones_pair):
    out_exact, out_rounded = [], []
    if exact:
        parts = []
        for x in exact:
            hi = x.astype(BF16)
            parts.append(jnp.concatenate([hi, (x - hi.astype(F32)).astype(BF16)], axis=1))
        res = jnp.dot(jnp.concatenate(parts, axis=0), ones_pair, preferred_element_type=F32)
        out_exact = [res[HEAD * c:HEAD * (c + 1)] for c in range(len(exact))]
    if rounded:
        res = jnp.dot(jnp.concatenate([x.astype(BF16) for x in rounded], axis=0), ones_pair[0:2 * HEAD],
                      preferred_element_type=F32)
        out_rounded = [res[HEAD * c:HEAD * (c + 1)] for c in range(len(rounded))]
    return out_exact, out_rounded


N_CHAIN = 2 * N_PAIR


def _chain(c):
    d, p = divmod(c, N_PAIR)
    return d, slice(2 * HEAD * p, 2 * HEAD * (p + 1))


def _scan_specs(n_chunks, col_blocks, fwd_chunk, bwd_chunk):
    def spec(chunk_of, col):
        return pl.BlockSpec((SCAN_CHUNK, D_RWKV), lambda b, g: (b * n_chunks + chunk_of(g), col))
    return [spec(fwd_chunk, c) for c in col_blocks] + [spec(bwd_chunk, c) for c in col_blocks]


def _scan_fwd(ps, kk, dirs, batch, seq):
    t = batch * seq
    n = seq // SCAN_CHUNK
    groups = SCAN_CHUNK // GROUP
    up = lambda g: g
    down = lambda g: n - 1 - g
    col_blocks = (0, 2, 0, 0, 0, 0)

    def body(*refs):
        dir_refs = (refs[0:6], refs[6:12])
        ones_ref = refs[12]
        y_refs, hist_refs, st_ref = refs[13:15], refs[15:17], refs[17]

        @pl.when(pl.program_id(1) == 0)
        def _():
            st_ref[...] = jnp.zeros_like(st_ref)

        ones_pair = ones_ref[...]
        dmask = _diag_mask()
        dmask_b = dmask.astype(BF16)
        sub8 = lax.broadcasted_iota(jnp.int32, (GROUP, 2 * HEAD), 0)

        def group(gi, carry):
            off = (pl.multiple_of(gi * GROUP, GROUP), pl.multiple_of((groups - 1 - gi) * GROUP, GROUP))
            loaded = [tuple(ref[pl.ds(off[d], GROUP), :] for ref in dir_refs[d]) for d in range(2)]
            states = list(carry)
            y_acc = [jnp.zeros((GROUP, 2 * HEAD), F32) for _ in range(N_CHAIN)]
            for step in range(GROUP):
                rows, idx = [], []
                for c in range(N_CHAIN):
                    d, lanes = _chain(c)
                    i = step if d == 0 else GROUP - 1 - step
                    idx.append(i)
                    rows.append(tuple(x8[i:i + 1, lanes] for x8 in loaded[d]))
                    hist_refs[d][c % N_PAIR, gi * GROUP + step] = states[c]
                _, v_cols = _seg_many([], [dmask_b * rows[c][1].astype(BF16) for c in range(N_CHAIN)], ones_pair)
                sas, _ = _seg_many([states[c] * rows[c][2] for c in range(N_CHAIN)], [], ones_pair)
                for c in range(N_CHAIN):
                    _, _, _, w_row, kd_row, b_row = rows[c]
                    states[c] = states[c] * w_row - sas[c] * b_row + v_cols[c] * kd_row
                _, ys = _seg_many([], [states[c] * rows[c][0] for c in range(N_CHAIN)], ones_pair)
                for c in range(N_CHAIN):
                    y_acc[c] = jnp.where(sub8 == idx[c], _to_row(ys[c], dmask), y_acc[c])
            for c in range(N_CHAIN):
                d, lanes = _chain(c)
                y_refs[d][pl.ds(off[d], GROUP), lanes] = y_acc[c]
            return tuple(states)

        final = lax.fori_loop(0, groups, group, tuple(st_ref[c] for c in range(N_CHAIN)))
        for c in range(N_CHAIN):
            st_ref[c] = final[c]
            hist_refs[c // N_PAIR][c % N_PAIR, SCAN_CHUNK] = final[c]

    y_spec_f = pl.BlockSpec((SCAN_CHUNK, D_RWKV), lambda b, g: (b * n + up(g), 0))
    y_spec_b = pl.BlockSpec((SCAN_CHUNK, D_RWKV), lambda b, g: (b * n + down(g), 0))
    hist_shape = (batch, n, N_PAIR, SCAN_CHUNK + 1, HEAD, 2 * HEAD)
    hist_block = (None, None, N_PAIR, SCAN_CHUNK + 1, HEAD, 2 * HEAD)
    hist_spec_f = pl.BlockSpec(hist_block, lambda b, g: (b, up(g), 0, 0, 0, 0))
    hist_spec_b = pl.BlockSpec(hist_block, lambda b, g: (b, down(g), 0, 0, 0, 0))
    ones_spec = pl.BlockSpec((4 * HEAD, 2 * HEAD), lambda b, g: (0, 0))
    (wf, kdf, bf), (wb, kdb, bb) = dirs
    return pl.pallas_call(
        body, name="wkv_fwd", grid=(batch, n),
        in_specs=_scan_specs(n, col_blocks, up, down) + [ones_spec],
        out_specs=[y_spec_f, y_spec_b, hist_spec_f, hist_spec_b],
        out_shape=[_sds((t, D_RWKV)), _sds((t, D_RWKV)), _sds(hist_shape), _sds(hist_shape)],
        scratch_shapes=[pltpu.VMEM((N_CHAIN, HEAD, 2 * HEAD), F32)],
        compiler_params=_cp("parallel", "arbitrary"),
    )(ps, ps, kk, wf, kdf, bf, ps, ps, kk, wb, kdb, bb, _pair_ones())


def _scan_bwd(ps, kk, dirs, dy, hist_f, hist_b, batch, seq):
    t = batch * seq
    n = seq // SCAN_CHUNK
    groups = SCAN_CHUNK // GROUP
    fwd_chunk = lambda g: n - 1 - g
    bwd_chunk = lambda g: g
    col_blocks = (0, 2, 0, 0, 0, 0, 0)

    def undo_group(dir_refs, out_refs, hist_refs, gi, d_states, ones_pair, dmask, sub8):
        d_states = list(d_states)
        loaded, blocks = [], []
        for d in range(2):
            blk = groups - 1 - gi if d == 0 else gi
            blocks.append(pl.ds(pl.multiple_of(blk * GROUP, GROUP), GROUP))
            r8, v8, kk8, w8, kd8, b8, dy8 = (ref[blocks[d], :] for ref in dir_refs[d])
            loaded.append((r8, v8, kk8, w8, kd8, -b8, dy8))
        acc = [[jnp.zeros((GROUP, 2 * HEAD), F32) for _ in range(6)] for _ in range(N_CHAIN)]
        for step in range(GROUP):
            rows, idx, before, after = [], [], [], []
            for c in range(N_CHAIN):
                d, lanes = _chain(c)
                i = GROUP - 1 - step if d == 0 else step
                q = (groups - 1 - gi) * GROUP + i if d == 0 else SCAN_CHUNK - 1 - (gi * GROUP + i)
                idx.append(i)
                rows.append(tuple(x8[i:i + 1, lanes] for x8 in loaded[d]))
                before.append(hist_refs[d][c % N_PAIR, q])
                after.append(hist_refs[d][c % N_PAIR, q + 1])
            _, cols = _seg_many([], [dmask.astype(BF16) * rows[c][j].astype(BF16) for c in range(N_CHAIN) for j in (1, 6)],
                                ones_pair)
            v_cols, dy_cols = cols[0::2], cols[1::2]
            d_now = [d_states[c] + dy_cols[c] * rows[c][0] for c in range(N_CHAIN)]
            d_sas, _ = _seg_many([d_now[c] * rows[c][5] for c in range(N_CHAIN)], [], ones_pair)
            _, others = _seg_many(
                [], [x for c in range(N_CHAIN) for x in (before[c] * rows[c][2], d_now[c] * rows[c][4])], ones_pair)
            for c in range(N_CHAIN):
                sa, d_sa, dv_cols = others[2 * c], d_sas[c], others[2 * c + 1]
                rows_out = (
                    jnp.sum(after[c] * dy_cols[c], axis=0, keepdims=True),
                    jnp.sum(d_now[c] * before[c], axis=0, keepdims=True),
                    jnp.sum(d_now[c] * v_cols[c], axis=0, keepdims=True),
                    _to_row(dv_cols, dmask),
                    jnp.sum(before[c] * d_sa, axis=0, keepdims=True),
                    -jnp.sum(d_now[c] * sa, axis=0, keepdims=True),
                )
                acc[c] = [jnp.where(sub8 == idx[c], val, a) for val, a in zip(rows_out, acc[c])]
                d_states[c] = d_now[c] * rows[c][3] + d_sa * rows[c][2]
        for c in range(N_CHAIN):
            d, lanes = _chain(c)
            for o_ref, val in zip(out_refs[d], acc[c]):
                o_ref[blocks[d], lanes] = val
        return tuple(d_states)

    def body(*refs):
        dir_refs = (refs[0:7], refs[7:14])
        hist_refs, ones_ref = refs[14:16], refs[16]
        out_refs = (refs[17:23], refs[23:29])
        dst_ref = refs[29]

        @pl.when(pl.program_id(1) == 0)
        def _():
            dst_ref[...] = jnp.zeros_like(dst_ref)

        ones_pair = ones_ref[...]
        dmask = _diag_mask()
        sub8 = lax.broadcasted_iota(jnp.int32, (GROUP, 2 * HEAD), 0)

        def group(gi, carry):
            return undo_group(dir_refs, out_refs, hist_refs, gi, carry, ones_pair, dmask, sub8)

        final = lax.fori_loop(0, groups, group, tuple(dst_ref[c] for c in range(N_CHAIN)))
        for c in range(N_CHAIN):
            dst_ref[c] = final[c]

    blk = (SCAN_CHUNK, D_RWKV)
    out_f = pl.BlockSpec(blk, lambda b, g: (b * n + fwd_chunk(g), 0))
    out_b = pl.BlockSpec(blk, lambda b, g: (b * n + bwd_chunk(g), 0))
    hist_block = (None, None, N_PAIR, SCAN_CHUNK + 1, HEAD, 2 * HEAD)
    hist_spec_f = pl.BlockSpec(hist_block, lambda b, g: (b, fwd_chunk(g), 0, 0, 0, 0))
    hist_spec_b = pl.BlockSpec(hist_block, lambda b, g: (b, bwd_chunk(g), 0, 0, 0, 0))
    ones_spec = pl.BlockSpec((4 * HEAD, 2 * HEAD), lambda b, g: (0, 0))
    (wf, kdf, bf), (wb, kdb, bb) = dirs
    outs = pl.pallas_call(
        body, name="wkv_bwd", grid=(batch, n),
        in_specs=_scan_specs(n, col_blocks, fwd_chunk, bwd_chunk) + [hist_spec_f, hist_spec_b, ones_spec],
        out_specs=[out_f] * 6 + [out_b] * 6,
        out_shape=[_sds((t, D_RWKV))] * 12,
        scratch_shapes=[pltpu.VMEM((N_CHAIN, HEAD, 2 * HEAD), F32)],
        compiler_params=_cp("parallel", "arbitrary"),
    )(ps, ps, kk, wf, kdf, bf, dy, ps, ps, kk, wb, kdb, bb, dy, hist_f, hist_b, _pair_ones())
    return outs[0:6], outs[6:12]


def _post_math(y, r, kd_f, kd_b, v, gate, gn_w, gn_b, rk_f, rk_b, ones_blocks):
    mean = _seg(y, ones_blocks) * (1.0 / HEAD)
    yc = y - mean
    var = _seg(yc * yc, ones_blocks) * (1.0 / HEAD)
    yn = yc * lax.rsqrt(var + GN_EPS) * gn_w + gn_b
    bonus = _seg(r * kd_f * rk_f, ones_blocks) * v + _seg(r * kd_b * rk_b, ones_blocks) * v
    return (yn + bonus) * gate


def _conv_parts(pc, halo_prev, halo_next, has_prev, has_next):
    gate_b, gate_c, hid = pc[:, 0:512], pc[:, 512:1024], pc[:, 1024:1536]
    u = gate_c * hid
    u_prev_row = halo_prev[GROUP - 1:GROUP, 512:1024] * halo_prev[GROUP - 1:GROUP, 1024:1536] * has_prev
    u_next_row = halo_next[0:1, 512:1024] * halo_next[0:1, 1024:1536] * has_next
    u_down, u_up = _shifted(u, u_prev_row, u_next_row)
    return gate_b, gate_c, hid, u, u_down, u_up


def _post_specs(tm, t):
    pc_prev, pc_next = _halo_specs(tm, D_CONV3, t)
    col = lambda c: pl.BlockSpec((tm, D_RWKV), lambda i: (i, c))
    return ([col(0), col(0), col(0), col(0), col(0), col(2), col(0), _row(tm, D_CONV3), pc_prev, pc_next,
             _fixed((8, D_RWKV)), _fixed((D_RWKV, D_RWKV))])


def _post_fwd(y_f, y_b, ps, kd_f, kd_b, gate, pc, qvec, ones_blocks, tm, seq):
    t = ps.shape[0]

    def body(yf_ref, yb_ref, r_ref, kdf_ref, kdb_ref, v_ref, g_ref, pc_ref, hp_ref, hn_ref, qv_ref, ones_ref,
             o_ref, ot_ref):
        has_prev, has_next = _edge_flags(tm, seq)
        vec = [jnp.broadcast_to(qv_ref[j:j + 1, :], (tm, D_RWKV)) for j in range(7)]
        o_rwkv = _post_math(yf_ref[...] + yb_ref[...], r_ref[...], kdf_ref[...], kdb_ref[...], v_ref[...],
                            g_ref[...], vec[0], vec[1], vec[2], vec[3], ones_ref[...])
        gate_b, _, _, u, u_down, u_up = _conv_parts(pc_ref[...], hp_ref[...], hn_ref[...], has_prev, has_next)
        o_conv = gate_b * (vec[4] * u_down + vec[5] * u + vec[6] * u_up)
        for half, val in enumerate((o_rwkv, o_conv)):
            o_ref[:, D_RWKV * half:D_RWKV * (half + 1)] = val.astype(BF16)
            ot_ref[D_RWKV * half:D_RWKV * (half + 1), :] = jnp.transpose(val).astype(BF16)

    return pl.pallas_call(
        body, name="post_fwd", grid=(t // tm,), in_specs=_post_specs(tm, t),
        out_specs=[_row(tm, D_MODEL), _col(tm, D_MODEL)],
        out_shape=[_sds((t, D_MODEL), BF16), _sds((D_MODEL, t), BF16)], compiler_params=_cp("parallel"),
    )(y_f, y_b, ps, kd_f, kd_b, ps, gate, pc, pc, pc, qvec, ones_blocks)


def _post_bwd(d_out, y_f, y_b, ps, kd_f, kd_b, gate, pc, qvec, ones_blocks, tm, seq):
    t = ps.shape[0]
    do_prev, do_next = _halo_specs(tm, D_MODEL, t)

    def body(do_ref, dop_ref, don_ref, yf_ref, yb_ref, r_ref, kdf_ref, kdb_ref, v_ref, g_ref, pc_ref, hp_ref,
             hn_ref, qv_ref, ones_ref, dy_ref, dr_ref, dkdf_ref, dkdb_ref, dv_ref, dg_ref, dpc_ref, dqv_ref):
        has_prev, has_next = _edge_flags(tm, seq)
        vec = [jnp.broadcast_to(qv_ref[j:j + 1, :], (tm, D_RWKV)) for j in range(7)]
        ones_v = ones_ref[...]
        args = (yf_ref[...] + yb_ref[...], r_ref[...], kdf_ref[...], kdb_ref[...], v_ref[...], g_ref[...],
                vec[0], vec[1], vec[2], vec[3])
        _, vjp = jax.vjp(lambda *a: _post_math(*a, ones_v), *args)
        grads = vjp(do_ref[:, 0:D_RWKV])
        for o_ref, gval in zip((dy_ref, dr_ref, dkdf_ref, dkdb_ref, dv_ref, dg_ref), grads[0:6]):
            o_ref[...] = gval

        hp, hn = hp_ref[...], hn_ref[...]
        gate_b, gate_c, hid, u, u_down, u_up = _conv_parts(pc_ref[...], hp, hn, has_prev, has_next)
        d_oc = do_ref[:, D_RWKV:2 * D_RWKV]
        d_cu = d_oc * gate_b
        d_cu_prev = dop_ref[GROUP - 1:GROUP, D_RWKV:2 * D_RWKV] * hp[GROUP - 1:GROUP, 0:512] * has_prev
        d_cu_next = don_ref[0:1, D_RWKV:2 * D_RWKV] * hn[0:1, 0:512] * has_next
        d_cu_down, d_cu_up = _shifted(d_cu, d_cu_prev, d_cu_next)
        d_u = vec[5] * d_cu + vec[4] * d_cu_up + vec[6] * d_cu_down
        dpc_ref[:, 0:512] = (d_oc * (vec[4] * u_down + vec[5] * u + vec[6] * u_up)).astype(BF16)
        dpc_ref[:, 512:1024] = (d_u * hid).astype(BF16)
        dpc_ref[:, 1024:1536] = (d_u * gate_c).astype(BF16)

        @pl.when(pl.program_id(0) == 0)
        def _():
            dqv_ref[...] = jnp.zeros_like(dqv_ref)

        vec_grads = list(grads[6:10]) + [d_cu * u_down, d_cu * u, d_cu * u_up]
        for j, gval in enumerate(vec_grads):
            dqv_ref[j:j + 1, :] += jnp.sum(gval, axis=0, keepdims=True)

    return pl.pallas_call(
        body, name="post_bwd", grid=(t // tm,),
        in_specs=[_row(tm, D_MODEL), do_prev, do_next] + _post_specs(tm, t),
        out_specs=[_row(tm, D_RWKV)] * 6 + [_row(tm, D_CONV3), _fixed((8, D_RWKV))],
        out_shape=[_sds((t, D_RWKV))] * 6 + [_sds((t, D_CONV3), BF16), _sds((8, D_RWKV))],
        compiler_params=_cp("arbitrary"),
    )(d_out, d_out, d_out, y_f, y_b, ps, kd_f, kd_b, ps, gate, pc, pc, pc, qvec, ones_blocks)


def _adamw_math(wv, gv, mv, vv):
    m2 = ADAM_B1 * mv + (1.0 - ADAM_B1) * gv
    v2 = ADAM_B2 * vv + (1.0 - ADAM_B2) * (gv * gv)
    m_hat = m2 / (1.0 - ADAM_B1 ** ADAM_STEP)
    v_hat = v2 / (1.0 - ADAM_B2 ** ADAM_STEP)
    return -ADAM_LR * (m_hat / (jnp.sqrt(v_hat) + ADAM_EPS) + ADAM_WD * wv), m2, v2


def _adamw_small(items):
    n = len(items)

    def body(*refs):
        ins, outs = refs[:4 * n], refs[4 * n:]
        for k in range(n):
            w_ref, g_ref, m_ref, v_ref = ins[4 * k:4 * k + 4]
            for o_ref, val in zip(outs[3 * k:3 * k + 3], _adamw_math(w_ref[...], g_ref[...], m_ref[...], v_ref[...])):
                o_ref[...] = val

    flat = [a for item in items for a in item]
    outs = pl.pallas_call(
        body, name="adamw_small", out_shape=[_sds(item[0].shape) for item in items for _ in range(3)],
        compiler_params=_cp())(*flat)
    return [tuple(outs[3 * k:3 * k + 3]) for k in range(n)]


def _adamw(w, g, m, v, name):
    r, c = w.shape[-2:]
    tr = _tile(r, 256, 8)
    if w.ndim == 3:
        spec = pl.BlockSpec((None, tr, c), lambda i: (0, i, 0))
    else:
        spec = pl.BlockSpec((tr, c), lambda i: (i, 0))

    def body(w_ref, g_ref, m_ref, v_ref, d_ref, nm_ref, nv_ref):
        d_ref[...], nm_ref[...], nv_ref[...] = _adamw_math(w_ref[...], g_ref[...], m_ref[...], v_ref[...])

    return pl.pallas_call(
        body, name=name, grid=(r // tr,), in_specs=[spec] * 4, out_specs=[spec] * 3,
        out_shape=[_sds(w.shape)] * 3, compiler_params=_cp("parallel"))(w, g, m, v)


_ANY = pl.BlockSpec(memory_space=pl.ANY)


def _place():
    return lax.axis_index("x"), lax.axis_index("y"), lax.axis_index("c")


def _other_chips(x, y):
    return [(1 - x, y), (x, 1 - y), (1 - x, 1 - y)]


def _remote(src, dst, send_sems, recv_sems, k, to):
    return pltpu.make_async_remote_copy(src_ref=src, dst_ref=dst, send_sem=send_sems.at[k],
                                        recv_sem=recv_sems.at[k], device_id=to, device_id_type=MESH)


def _gather_weights(pack):
    rows, width = pack.shape
    half = rows // 2

    def body(x_ref, out_ref, send_sems, recv_sems):
        x, y, c = _place()
        sibling = (x, y, 1 - c)
        chips = _other_chips(x, y)

        def block(chip, part):
            return out_ref.at[2 * chip[0] + chip[1], pl.ds(part * half, half), :]

        first = [_remote(x_ref.at[pl.ds(c * half, half), :], block((x, y), c), send_sems, recv_sems, j, (*chip, c))
                 for j, chip in enumerate(chips)]
        for cp in first:
            cp.start()
        passed = [_remote(block(chip, c), block(chip, c), send_sems, recv_sems, 3 + j, sibling)
                  for j, chip in enumerate(chips)]
        for j, chip in enumerate(chips):
            _remote(block(chip, c), block(chip, c), send_sems, recv_sems, j, sibling).wait_recv()
            passed[j].start()
        for j, chip in enumerate(chips):
            _remote(block(chip, 1 - c), block(chip, 1 - c), send_sems, recv_sems, 3 + j, sibling).wait_recv()
        for cp in first + passed:
            cp.wait_send()

    return pl.pallas_call(
        body, name="gather_weights", in_specs=[_ANY], out_specs=_ANY,
        out_shape=_sds((N_SHARD, rows, width), pack.dtype),
        scratch_shapes=[pltpu.SemaphoreType.DMA((6,)), pltpu.SemaphoreType.DMA((6,))],
    )(pack)


_HBM = pl.BlockSpec(memory_space=pltpu.HBM)
_SEMS = pl.BlockSpec(memory_space=pltpu.SEMAPHORE)
_DATAFLOW = pltpu.SideEffectType.DATAFLOW_SIDE_EFFECTING


def _fetch_start(pack, after):
    def body(x_ref, land_ref, after_ref, send_sems, recv_sems, x_thru, land_thru, token):
        x, y, c = _place()
        for j, chip in enumerate(_other_chips(x, y)):
            _remote(x_ref, land_ref.at[2 * x + y], send_sems, recv_sems, j, (*chip, c)).start()
        token[...] = jnp.zeros_like(token)

    land = lax.empty((N_SHARD,) + pack.shape, pack.dtype)
    return pl.pallas_call(
        body, name="fetch_ffn_start",
        out_shape=(pltpu.SemaphoreType.DMA((3,)), pltpu.SemaphoreType.DMA((3,)), pltpu.HBM(pack.shape, pack.dtype),
                   pltpu.HBM(land.shape, land.dtype), _sds((8, 128))),
        in_specs=(_HBM, _HBM, _ANY), out_specs=(_SEMS, _SEMS, _HBM, _HBM, pl.BlockSpec(memory_space=pltpu.VMEM)),
        input_output_aliases={0: 2, 1: 3}, compiler_params=pltpu.CompilerParams(has_side_effects=_DATAFLOW),
    )(pltpu.with_memory_space_constraint(pack, pltpu.HBM), pltpu.with_memory_space_constraint(land, pltpu.HBM), after)


def _fetch_wait(send_sems, recv_sems, pack_thru, land_thru, after):
    def body(x_ref, land_ref, send_sems, recv_sems, after_ref, x_dead, got_ref):
        x, y, c = _place()
        for j, chip in enumerate(_other_chips(x, y)):
            cp = _remote(x_ref, land_ref.at[2 * chip[0] + chip[1]], send_sems, recv_sems, j, (*chip, c))
            cp.wait_send()
            cp.wait_recv()

    return pl.pallas_call(
        body, name="fetch_ffn_wait",
        out_shape=(pltpu.HBM(pack_thru.shape, pack_thru.dtype), pltpu.HBM(land_thru.shape, land_thru.dtype)),
        in_specs=(_HBM, _HBM, _SEMS, _SEMS, _ANY), out_specs=(_HBM, _HBM), input_output_aliases={0: 0, 1: 1},
        compiler_params=pltpu.CompilerParams(has_side_effects=_DATAFLOW),
    )(pack_thru, land_thru, send_sems, recv_sems, after)[1]


def _swap_with_sibling(block, name):
    def body(x_ref, out_ref, send_sems, recv_sems):
        x, y, c = _place()
        cp = _remote(x_ref, out_ref, send_sems, recv_sems, 0, (x, y, 1 - c))
        cp.start()
        cp.wait()

    return pl.pallas_call(
        body, name=name, in_specs=[_ANY], out_specs=_ANY, out_shape=_sds(block.shape, block.dtype),
        scratch_shapes=[pltpu.SemaphoreType.DMA((1,)), pltpu.SemaphoreType.DMA((1,))],
    )(block)


def _swap_other_half(packed, tag):
    slots, rows, width = packed.shape
    half = rows // 2

    def body(x_ref, out_ref, send_sems, recv_sems):
        x, y, c = _place()
        cp = _remote(x_ref.at[:, pl.ds((1 - c) * half, half), :], out_ref, send_sems, recv_sems, 0, (x, y, 1 - c))
        cp.start()
        cp.wait()

    return pl.pallas_call(
        body, name="swap_halves_" + tag, in_specs=[_ANY], out_specs=_ANY, out_shape=_sds((slots, half, width)),
        scratch_shapes=[pltpu.SemaphoreType.DMA((1,)), pltpu.SemaphoreType.DMA((1,))],
    )(packed)


def _add_halves(packed, got, c, tag):
    slots, rows, width = packed.shape
    half = rows // 2
    tr = _tile(half, 408, 16)
    per = half // tr
    block = (None, tr, width)

    def body(c_ref, mine_ref, got_ref, sum_ref, sum16_ref):
        acc = mine_ref[...] + got_ref[...]
        sum_ref[...] = acc
        sum16_ref[...] = acc.astype(BF16)

    plain = pl.BlockSpec(block, lambda s, i, c_ref: (s, i, 0))
    grid_spec = pltpu.PrefetchScalarGridSpec(
        num_scalar_prefetch=1, grid=(slots, per),
        in_specs=[pl.BlockSpec(block, lambda s, i, c_ref: (s, c_ref[0] * per + i, 0)), plain],
        out_specs=[plain, plain])
    return pl.pallas_call(
        body, name="add_halves_" + tag, grid_spec=grid_spec,
        out_shape=[_sds((slots, half, width)), _sds((slots, half, width), BF16)],
        compiler_params=_cp("parallel", "parallel"))(c.reshape(1).astype(jnp.int32), packed, got)


def _add_quarters(chip_sum, others, chip, tag):
    _, rows, width = chip_sum.shape
    tr = _tile(rows, 408, 16)

    def body(chip_ref, own_ref, others_ref, o_ref):
        acc = own_ref[...]
        for j in range(3):
            acc = acc + others_ref[j].astype(F32)
        o_ref[...] = acc

    grid_spec = pltpu.PrefetchScalarGridSpec(
        num_scalar_prefetch=1, grid=(rows // tr,),
        in_specs=[pl.BlockSpec((None, tr, width), lambda i, chip_ref: (chip_ref[0], i, 0)),
                  pl.BlockSpec((3, tr, width), lambda i, chip_ref: (0, i, 0))],
        out_specs=pl.BlockSpec((tr, width), lambda i, chip_ref: (i, 0)))
    return pl.pallas_call(
        body, name="add_quarters_" + tag, grid_spec=grid_spec, out_shape=_sds((rows, width)),
        compiler_params=_cp("parallel"))(chip.reshape(1).astype(jnp.int32), chip_sum, others)


def _exchange_start(parts, tag):
    _, rows, width = parts.shape

    def body(x_ref, land_ref, send_sems, recv_sems, x_thru, land_thru, token):
        x, y, c = _place()
        for j, chip in enumerate(_other_chips(x, y)):
            _remote(x_ref.at[2 * chip[0] + chip[1]], land_ref.at[j], send_sems, recv_sems, j, (*chip, c)).start()
        token[...] = jnp.zeros_like(token)

    land = lax.empty((3, rows, width), parts.dtype)
    return pl.pallas_call(
        body, name="exchange_" + tag + "_start",
        out_shape=(pltpu.SemaphoreType.DMA((3,)), pltpu.SemaphoreType.DMA((3,)), pltpu.HBM(parts.shape, parts.dtype),
                   pltpu.HBM(land.shape, land.dtype), _sds((8, 128))),
        in_specs=(_HBM, _HBM), out_specs=(_SEMS, _SEMS, _HBM, _HBM, pl.BlockSpec(memory_space=pltpu.VMEM)),
        input_output_aliases={0: 2, 1: 3}, compiler_params=pltpu.CompilerParams(has_side_effects=_DATAFLOW),
    )(pltpu.with_memory_space_constraint(parts, pltpu.HBM), pltpu.with_memory_space_constraint(land, pltpu.HBM))


def _exchange_wait(send_sems, recv_sems, parts_thru, land_thru, after, tag):
    def body(x_ref, land_ref, send_sems, recv_sems, after_ref, x_dead, got_ref):
        x, y, c = _place()
        for j, chip in enumerate(_other_chips(x, y)):
            cp = _remote(x_ref.at[2 * chip[0] + chip[1]], land_ref.at[j], send_sems, recv_sems, j, (*chip, c))
            cp.wait_send()
            cp.wait_recv()

    return pl.pallas_call(
        body, name="exchange_" + tag + "_wait",
        out_shape=(pltpu.HBM(parts_thru.shape, parts_thru.dtype), pltpu.HBM(land_thru.shape, land_thru.dtype)),
        in_specs=(_HBM, _HBM, _SEMS, _SEMS, _ANY), out_specs=(_HBM, _HBM), input_output_aliases={0: 0, 1: 1},
        compiler_params=pltpu.CompilerParams(has_side_effects=_DATAFLOW),
    )(parts_thru, land_thru, send_sems, recv_sems, after)[1]


def _allreduce_small(vec):
    rows, width = vec.shape
    vmem = pl.BlockSpec(memory_space=pltpu.VMEM)

    def body(x_ref, o_ref, buf_ref, send_sems, recv_sems):
        x, y, c = _place()
        me = 4 * x + 2 * y + c
        buf_ref[me] = x_ref[...]
        copies = []
        for k in range(1, N_DEV):
            peer = (x ^ ((k >> 2) & 1), y ^ ((k >> 1) & 1), c ^ (k & 1))
            copies.append(_remote(x_ref, buf_ref.at[me], send_sems, recv_sems, k - 1, peer))
        for cp in copies:
            cp.start()
        for k in range(1, N_DEV):
            _remote(x_ref, buf_ref.at[me ^ k], send_sems, recv_sems, k - 1, (x, y, c)).wait_recv()
        for cp in copies:
            cp.wait_send()
        total = buf_ref[0]
        for d in range(1, N_DEV):
            total = total + buf_ref[d]
        o_ref[...] = total

    return pl.pallas_call(
        body, name="allreduce_small", in_specs=[vmem], out_specs=vmem, out_shape=_sds((rows, width)),
        scratch_shapes=[pltpu.VMEM((N_DEV, rows, width), F32), pltpu.SemaphoreType.DMA((N_DEV - 1,)),
                        pltpu.SemaphoreType.DMA((N_DEV - 1,))],
    )(vec)


def _rows1024(a):
    return a.reshape(-1, 1024)


def _pad_rows(a, rows):
    return jnp.concatenate([a, jnp.zeros((rows - a.shape[0], a.shape[1]), a.dtype)], axis=0)


_TRANSPOSED = ("w_in", "w_gate", "w_up")
_SMALL_SHARDED = ("w_up_f", "w_up_b", "a_up_f", "a_up_b", "g_up")
_BIG_SHARDED = ("w_in", "w_out", "w_gate", "w_up", "w_down")


def _pack_weight_shards(w):
    conv_bits = lax.bitcast_convert_type(w["conv_w"], BF16).reshape(1, -1)
    conv_row = jnp.concatenate([conv_bits, jnp.zeros((1, 1024 - conv_bits.shape[1]), BF16)], axis=1)

    def rows(name):
        a = w[name].astype(BF16)
        return a.T if name in _TRANSPOSED else _rows1024(a)

    early = _pad_rows(jnp.concatenate([rows(name) for name, _ in _EARLY_ROWS[:-1]] + [conv_row], axis=0), EARLY_R)
    return early, jnp.concatenate([rows(name) for name, _ in _FFN_ROWS], axis=0)


def _split_rows(gathered, layout):
    out, row = {}, 0
    for name, n in layout:
        out[name] = gathered[:, row:row + n]
        row += n
    return out


def _unpack_early(gathered):
    out = _split_rows(gathered, _EARLY_ROWS)
    cols = lambda a, k: jnp.concatenate([a[s].reshape(k, -1) for s in range(N_SHARD)], axis=1)
    conv = lax.bitcast_convert_type(out["conv_w"][:, 0, :768].reshape(N_SHARD, 3, 128, 2), F32)
    full = dict(w_in=out["w_in"].reshape(-1, 1024).T, w_out=out["w_out"].reshape(D_MODEL, D_MODEL),
                conv_w=jnp.concatenate([conv[s] for s in range(N_SHARD)], axis=1))
    full.update({name: cols(out[name], D_GATE if name == "g_up" else D_LORA) for name in _SMALL_SHARDED})
    return full


def _unpack_ffn(gathered):
    out = _split_rows(gathered, _FFN_ROWS)
    return dict(w_gate=out["w_gate"].reshape(-1, 1024).T, w_up=out["w_up"].reshape(-1, 1024).T,
                w_down=out["w_down"].reshape(D_FF, D_MODEL))


def _pack_grads(g, layout, rows):
    col_split = lambda a, s: a[:, s * (a.shape[1] // N_SHARD):(s + 1) * (a.shape[1] // N_SHARD)]
    row_split = lambda a, s: a[s * (a.shape[0] // N_SHARD):(s + 1) * (a.shape[0] // N_SHARD)]
    by_rows = {name: (g[name].T if name in _TRANSPOSED else g[name]) for name, _ in layout if name in _BIG_SHARDED}
    used = sum(n for _, n in layout)
    parts = []
    for s in range(N_SHARD):
        for name, _ in layout:
            if name in _BIG_SHARDED:
                parts.append(row_split(by_rows[name], s))
            elif name in _SMALL_SHARDED:
                parts.append(_rows1024(col_split(g[name], s)))
            else:
                conv = col_split(g["conv_w"], s).reshape(1, -1)
                parts.append(jnp.concatenate([conv, jnp.zeros((1, 1024 - conv.shape[1]), F32)], axis=1))
        if rows > used:
            parts.append(jnp.zeros((rows - used, 1024), F32))
    return jnp.concatenate(parts, axis=0).reshape(N_SHARD, rows, 1024)


def _unpack_grad_shard(pack, layout):
    small_shapes = {name: (D_GATE if name == "g_up" else D_LORA, 128) for name in _SMALL_SHARDED}
    out, row = {}, 0
    for name, n in layout:
        piece = pack[row:row + n]
        if name == "conv_w":
            out[name] = piece[0, :384].reshape(3, 128)
        else:
            out[name] = piece.T if name in _TRANSPOSED else piece.reshape(small_shapes.get(name, piece.shape))
        row += n
    return out


_SMALL_LAYOUT = (("norm1_w", 1024), ("mu_shift", D_SHIFT), ("w0_f", 512), ("w0_b", 512), ("a0_f", 512),
                 ("a0_b", 512), ("k_k", 512), ("k_a_f", 512), ("k_a_b", 512), ("r_k_f", 512), ("r_k_b", 512),
                 ("gn_w", 512), ("gn_b", 512), ("norm2_w", 1024), ("norm_f_w", 1024), ("loss", 1))


def _pack_small(vals):
    rows = []
    for name, n in _SMALL_LAYOUT:
        flat = vals[name].reshape(-1)
        n_rows = -(-n // 1024)
        rows.append(jnp.concatenate([flat, jnp.zeros((n_rows * 1024 - n,), F32)]).reshape(n_rows, 1024))
    return _pad_rows(jnp.concatenate(rows, axis=0), SMALL_ROWS)


def _unpack_small(pack):
    out, row = {}, 0
    for name, n in _SMALL_LAYOUT:
        n_rows = -(-n // 1024)
        out[name] = pack[row:row + n_rows].reshape(-1)[:n]
        row += n_rows
    return out


_WEIGHTS = ("norm1_w", "w_in", "mu_shift", "w_up_f", "w0_f", "w_up_b", "w0_b", "a_up_f", "a0_f", "a_up_b", "a0_b",
            "g_up", "k_k", "k_a_f", "k_a_b", "r_k_f", "r_k_b", "gn_w", "gn_b", "conv_w", "w_out", "norm2_w",
            "w_gate", "w_up", "w_down", "norm_f_w")


def _train_step(x, loss_target, w, m, v):
    batch, seq, _ = x.shape
    t = batch * seq
    tm = _tile(seq, 256, 8)
    xs = x.reshape(t, D_MODEL)
    target = loss_target.reshape(t, D_MODEL)
    vec = lambda name: w[name].reshape(1, -1)

    local = {name: w[name][0] for name, _ in _PACK_ROWS}
    c = lax.axis_index("c")
    chip = 2 * lax.axis_index("x") + lax.axis_index("y")
    early, ffn_pack = _pack_weight_shards(local)
    early_all = lax.dynamic_update_slice(_gather_weights(early), early[None], (chip, 0, 0))
    ffn_send, ffn_recv, ffn_pack, ffn_land, token = _fetch_start(ffn_pack, early_all)
    full = _unpack_early(early_all)
    w_in = full["w_in"]
    w_shift = jnp.concatenate([w_in[:, :D_SHIFT], jnp.zeros((D_MODEL, D_SHIFT_PAD - D_SHIFT), BF16)], axis=1)
    w_conv = w_in[:, D_SHIFT:]
    zeros_lora = jnp.zeros((D_LORA, D_RWKV), F32)
    lora = lambda name: full[name].astype(F32)
    mats = (jnp.concatenate([lora("w_up_f"), zeros_lora]), jnp.concatenate([zeros_lora, lora("a_up_f")]),
            jnp.concatenate([lora("w_up_b"), zeros_lora]), jnp.concatenate([zeros_lora, lora("a_up_b")]),
            jnp.concatenate([lora("g_up"), jnp.zeros((D_GATE_PAD - D_GATE, D_RWKV), F32)]))
    mu = jnp.concatenate([vec("mu_shift"), jnp.zeros((1, D_SHIFT_PAD - D_SHIFT), F32)], axis=1)
    mu = jnp.broadcast_to(mu, (GROUP, D_SHIFT_PAD))
    zero_row = jnp.zeros((1, D_RWKV), F32)
    pvec = jnp.concatenate([vec("k_k"), vec("w0_f"), vec("a0_f"), vec("k_a_f"), vec("w0_b"), vec("a0_b"),
                            vec("k_a_b"), zero_row], axis=0)
    qvec = jnp.concatenate([vec("gn_w"), vec("gn_b"), vec("r_k_f"), vec("r_k_b"), full["conv_w"], zero_row], axis=0)
    ones_blocks = _head_ones()

    h1_t, p_shift, pc = _norm_in_proj(xs, vec("norm1_w") + token[0, 0], w_shift, w_conv, tm)
    ps, kk, w_f, kd_f, b_f, w_b, kd_b, b_b, gate = _shift_prep_fwd(p_shift, mu, pvec, mats, ones_blocks, tm, seq)
    dirs = ((w_f, kd_f, b_f), (w_b, kd_b, b_b))
    y_f, y_b, hist_f, hist_b = _scan_fwd(ps, kk, dirs, batch, seq)
    mixed, mixed_t = _post_fwd(y_f, y_b, ps, kd_f, kd_b, gate, pc, qvec, ones_blocks, tm, seq)
    x1, h2, h2_t = _out_proj_norm(mixed, full["w_out"], xs, vec("norm2_w"), tm)
    ffn_all = _fetch_wait(ffn_send, ffn_recv, ffn_pack, ffn_land, h2)
    full.update(_unpack_ffn(lax.dynamic_update_slice(ffn_all, ffn_pack[None], (chip, 0, 0))))
    ff_gate, ff_up, act, act_t = _ffn_in(h2, full["w_gate"], full["w_up"])
    d_x2, d_norm_f, loss_part = _ffn_out_loss(act, full["w_down"], x1, w["norm_f_w"].reshape(1, -1), target, tm)

    g = {}
    g["w_down"] = _matmul(act_t, d_x2, mode="nn", name="ffn_down_dw")
    d_gate, d_up = _ffn_in_bwd(d_x2, full["w_down"], ff_gate, ff_up)
    g["w_gate"] = _matmul(h2_t, d_gate, mode="nn", name="ffn_gate_dw")
    g["w_up"] = _matmul(h2_t, d_up, mode="nn", name="ffn_up_dw")
    ffn_grads = _pack_grads(g, _FFN_ROWS, sum(n for _, n in _FFN_ROWS))
    ffn_sum, ffn_sum_bf16 = _add_halves(ffn_grads, _swap_other_half(ffn_grads, "ffn"), c, "ffn")
    ex_send, ex_recv, ffn_sum_bf16, ex_land, ex_token = _exchange_start(ffn_sum_bf16, "ffn")
    d_x1, d_norm2 = _proj_norm_bwd(d_gate, full["w_gate"], d_up, full["w_up"], x1, vec("norm2_w") + ex_token[0, 0],
                                   d_x2, tm, "ffn_in_dx_norm2_bwd")
    d_mixed = _matmul(d_x1, full["w_out"], mode="nt", name="out_proj_dx")
    g["w_out"] = _matmul(mixed_t, d_x1, mode="nn", name="out_proj_dw")
    dy, dr_o, dkdf_o, dkdb_o, dv_o, d_gatev, d_pc, d_qvec = _post_bwd(
        d_mixed, y_f, y_b, ps, kd_f, kd_b, gate, pc, qvec, ones_blocks, tm, seq)
    (dr_f, dw_f, dkd_f, dv_f, dkk_f, db_f), (dr_b, dw_b, dkd_b, dv_b, dkk_b, db_b) = _scan_bwd(
        ps, kk, dirs, dy, hist_f, hist_b, batch, seq)
    cts = [[dr_f, dr_b, dr_o], [dv_f, dv_b, dv_o], [dkk_f, dkk_b], [dw_f], [dkd_f, dkdf_o], [db_f],
           [dw_b], [dkd_b, dkdb_o], [db_b], [d_gatev]]
    q, d_pvec, d_m0, d_m1, d_m2, d_m3, d_m4 = _prep_bwd(ps, pvec, mats, ones_blocks, cts, tm)
    d_pshift, d_mu = _shift_bwd(q, p_shift, mu, tm, seq)
    d_w_shift = _matmul(h1_t, d_pshift, mode="nn", name="in_proj_shift_dw")
    d_w_conv = _matmul(h1_t, d_pc, mode="nn", name="in_proj_conv_dw")
    g["w_in"] = jnp.concatenate([d_w_shift[:, :D_SHIFT], d_w_conv], axis=1)
    d_x, d_norm1 = _proj_norm_bwd(d_pshift, w_shift, d_pc, w_conv, xs, vec("norm1_w"), d_x1, tm,
                                  "in_proj_dx_norm1_bwd")
    g["w_up_f"], g["a_up_f"] = d_m0[:D_LORA], d_m1[D_LORA:]
    g["w_up_b"], g["a_up_b"] = d_m2[:D_LORA], d_m3[D_LORA:]
    g["g_up"] = d_m4[:D_GATE]
    g["conv_w"] = d_qvec[4:7]

    def finish(chip_sum, others, tag, layout):
        eighth = _add_quarters(chip_sum, others, chip, tag)
        other_eighth = _swap_with_sibling(eighth, "swap_eighths_" + tag)
        return _unpack_grad_shard(jnp.concatenate([jnp.where(c == 0, eighth, other_eighth),
                                                   jnp.where(c == 0, other_eighth, eighth)], axis=0), layout)

    as2d = lambda name: (1, w[name].shape[0]) if w[name].ndim == 1 else w[name].shape
    operands = lambda name: tuple(a.reshape(as2d(name)) for a in (w[name], grads[name], m[name], v[name]))

    packed = _pack_grads(g, _EARLY_ROWS, EARLY_R)
    mix_sum, mix_sum_bf16 = _add_halves(packed, _swap_other_half(packed, "mixer"), c, "mixer")
    mx_send, mx_recv, mix_sum_bf16, mx_land, mx_token = _exchange_start(mix_sum_bf16, "mixer")
    grads = finish(ffn_sum, _exchange_wait(ex_send, ex_recv, ffn_sum_bf16, ex_land, mx_token, "ffn"), "ffn", _FFN_ROWS)
    updates = {name: _adamw(*operands(name), "adamw_" + name) for name in _FFN_NAMES}
    mix_others = _exchange_wait(mx_send, mx_recv, mix_sum_bf16, mx_land, updates["w_down"][2], "mixer")
    grads.update(finish(mix_sum, mix_others, "mixer", _EARLY_ROWS))

    small = dict(norm1_w=d_norm1, mu_shift=d_mu[:, :D_SHIFT], w0_f=d_pvec[1], w0_b=d_pvec[4], a0_f=d_pvec[2],
                 a0_b=d_pvec[5], k_k=d_pvec[0], k_a_f=d_pvec[3], k_a_b=d_pvec[6], r_k_f=d_qvec[2], r_k_b=d_qvec[3],
                 gn_w=d_qvec[0], gn_b=d_qvec[1], norm2_w=d_norm2, norm_f_w=d_norm_f, loss=loss_part)
    reduced = _unpack_small(_allreduce_small(_pack_small(small)))
    loss = reduced.pop("loss")[0]
    grads.update(reduced)

    outs = {}
    small = [name for name in _WEIGHTS if name not in _BIG_SHARDED]
    updates.update(zip(small, _adamw_small([operands(name) for name in small])))
    for name in ("w_in", "w_out"):
        updates[name] = _adamw(*operands(name), "adamw_" + name)
    for name in _WEIGHTS:
        shape = w[name].shape
        outs[name] = (grads[name].reshape(shape),) + tuple(a.reshape(shape) for a in updates[name])
    d_x = d_x.reshape(batch, seq, D_MODEL)
    return (loss, d_x) + tuple(outs[name][k] for k in range(4) for name in _WEIGHTS)


def kernel(x, norm1_w, w_in, mu_shift, w_up_f, w0_f, w_up_b, w0_b, a_up_f, a0_f, a_up_b, a0_b, g_up, k_k, k_a_f, k_a_b, r_k_f, r_k_b, gn_w, gn_b, conv_w, w_out, norm2_w, w_gate, w_up, w_down, norm_f_w, loss_target, m_norm1_w, m_w_in, m_mu_shift, m_w_up_f, m_w0_f, m_w_up_b, m_w0_b, m_a_up_f, m_a0_f, m_a_up_b, m_a0_b, m_g_up, m_k_k, m_k_a_f, m_k_a_b, m_r_k_f, m_r_k_b, m_gn_w, m_gn_b, m_conv_w, m_w_out, m_norm2_w, m_w_gate, m_w_up, m_w_down, m_norm_f_w, v_norm1_w, v_w_in, v_mu_shift, v_w_up_f, v_w0_f, v_w_up_b, v_w0_b, v_a_up_f, v_a0_f, v_a_up_b, v_a0_b, v_g_up, v_k_k, v_k_a_f, v_k_a_b, v_r_k_f, v_r_k_b, v_gn_w, v_gn_b, v_conv_w, v_w_out, v_norm2_w, v_w_gate, v_w_up, v_w_down, v_norm_f_w):
    args = locals()
    w = {name: args[name] for name in _WEIGHTS}
    m = {name: args["m_" + name] for name in _WEIGHTS}
    v = {name: args["v_" + name] for name in _WEIGHTS}
    return _train_step(x, loss_target, w, m, v)
```

```python
import functools

import jax
import jax.numpy as jnp
from jax import lax
from jax.experimental import pallas as pl
from jax.experimental.pallas import tpu as pltpu

F32 = jnp.float32
BF16 = jnp.bfloat16
MESH = pl.DeviceIdType.MESH

D_MODEL = 1024
D_RWKV = 512
HEAD = 64
N_PAIR = D_RWKV // (2 * HEAD)
D_LORA = 64
D_GATE = 160
D_GATE_PAD = 384
D_FF = 2816
D_SHIFT = 1824
D_SHIFT_PAD = 2048
D_CONV3 = 1536
LOG_DECAY_SCALE = 0.606531
RMS_EPS = 1e-6
GN_EPS = 64e-5
NORM_EPS = 1e-12
ADAM_LR, ADAM_B1, ADAM_B2, ADAM_EPS, ADAM_WD, ADAM_STEP = 0.001, 0.9, 0.999, 1e-08, 0.01, 10

N_SHARD = 4
N_DEV = 8
V7X_VMEM_LIMIT = 48 * 1024 * 1024
SCAN_CHUNK = 64
GROUP = 8

_PACK_ROWS = (("w_in", 840), ("w_out", 256), ("w_gate", 704), ("w_up", 704), ("w_down", 704),
              ("w_up_f", 8), ("w_up_b", 8), ("a_up_f", 8), ("a_up_b", 8), ("g_up", 20), ("conv_w", 1))
_FFN_NAMES = ("w_gate", "w_up", "w_down")
_EARLY_ROWS = tuple(item for item in _PACK_ROWS if item[0] not in _FFN_NAMES)
_FFN_ROWS = tuple(item for item in _PACK_ROWS if item[0] in _FFN_NAMES)
EARLY_R = 1152
SMALL_ROWS = 24


def _tile(n, cap, mult=128):
    best = None
    t = mult
    while t <= min(n, cap):
        if n % t == 0:
            best = t
        t += mult
    return best or n


def _cp(*sem):
    return pltpu.CompilerParams(dimension_semantics=sem or None, vmem_limit_bytes=V7X_VMEM_LIMIT)


def _sds(shape, dtype=F32):
    return jax.ShapeDtypeStruct(shape, dtype)


def _matmul(a, b, *, mode, name, out_dtype=F32, add=None):
    m, kdim = a.shape
    n = b.shape[1] if mode == "nn" else b.shape[0]
    tm, tn = _tile(m, 768, 8), _tile(n, 1536)
    tk = kdim if kdim <= 3584 else _tile(kdim, 1024)
    nk = kdim // tk
    a_spec = pl.BlockSpec((tm, tk), lambda i, j, k: (i, k))
    if mode == "nn":
        b_spec = pl.BlockSpec((tk, tn), lambda i, j, k: (k, j))
        dims = (((1,), (0,)), ((), ()))
    else:
        b_spec = pl.BlockSpec((tn, tk), lambda i, j, k: (j, k))
        dims = (((1,), (1,)), ((), ()))
    has_add = add is not None

    def body(*refs):
        a_ref, b_ref = refs[0], refs[1]
        add_ref = refs[2] if has_add else None
        o_ref = refs[3] if has_add else refs[2]
        part = lax.dot_general(a_ref[...].astype(BF16), b_ref[...].astype(BF16), dims,
                               preferred_element_type=F32)
        if nk == 1:
            if has_add:
                part = part + add_ref[...]
            o_ref[...] = part.astype(out_dtype)
        else:
            acc_ref = refs[-1]
            k = pl.program_id(2)

            @pl.when(k == 0)
            def _():
                acc_ref[...] = jnp.zeros_like(acc_ref)

            acc_ref[...] += part

            @pl.when(k == nk - 1)
            def _():
                res = acc_ref[...]
                if has_add:
                    res = res + add_ref[...]
                o_ref[...] = res.astype(out_dtype)

    o_spec = pl.BlockSpec((tm, tn), lambda i, j, k: (i, j))
    in_specs = [a_spec, b_spec] + ([o_spec] if has_add else [])
    args = (a, b) + ((add,) if has_add else ())
    return pl.pallas_call(
        body, name=name, grid=(m // tm, n // tn, nk), in_specs=in_specs, out_specs=o_spec,
        out_shape=_sds((m, n), out_dtype),
        scratch_shapes=[pltpu.VMEM((tm, tn), F32)] if nk > 1 else [],
        compiler_params=_cp("parallel", "parallel", "arbitrary"),
    )(*args)


def _row(tm, width):
    return pl.BlockSpec((tm, width), lambda i: (i, 0))


def _col(tm, height):
    return pl.BlockSpec((height, tm), lambda i: (0, i))


def _fixed(shape):
    return pl.BlockSpec(shape, lambda i: tuple(0 for _ in shape))


def _rmsnorm_tile(xv, wv):
    return xv * lax.rsqrt(jnp.mean(xv * xv, axis=-1, keepdims=True) + RMS_EPS) * wv


def _norm_in_proj(x, w, w_shift, w_conv, tm):
    t, d = x.shape
    n_a, n_b = w_shift.shape[1], w_conv.shape[1]

    def body(x_ref, w_ref, wa_ref, wb_ref, ht_ref, pa_ref, pb_ref):
        hv = _rmsnorm_tile(x_ref[...], w_ref[...])
        ht_ref[...] = jnp.transpose(hv).astype(BF16)
        hb = hv.astype(BF16)
        pa_ref[...] = jnp.dot(hb, wa_ref[...], preferred_element_type=F32)
        pb_ref[...] = jnp.dot(hb, wb_ref[...], preferred_element_type=F32)

    return pl.pallas_call(
        body, name="norm1_in_proj", grid=(t // tm,),
        in_specs=[_row(tm, d), _fixed((1, d)), _fixed((d, n_a)), _fixed((d, n_b))],
        out_specs=[_col(tm, d), _row(tm, n_a), _row(tm, n_b)],
        out_shape=[_sds((d, t), BF16), _sds((t, n_a)), _sds((t, n_b))],
        compiler_params=_cp("parallel"))(x, w, w_shift, w_conv)


def _out_proj_norm(mixed, w_out, res, w, tm):
    t, d = res.shape

    def body(m_ref, wo_ref, r_ref, w_ref, x_ref, h_ref, ht_ref):
        xv = r_ref[...] + jnp.dot(m_ref[...], wo_ref[...], preferred_element_type=F32)
        x_ref[...] = xv
        hv = _rmsnorm_tile(xv, w_ref[...])
        h_ref[...] = hv.astype(BF16)
        ht_ref[...] = jnp.transpose(hv).astype(BF16)

    return pl.pallas_call(
        body, name="out_proj_norm2", grid=(t // tm,),
        in_specs=[_row(tm, d), _fixed((d, d)), _row(tm, d), _fixed((1, d))],
        out_specs=[_row(tm, d), _row(tm, d), _col(tm, d)],
        out_shape=[_sds((t, d)), _sds((t, d), BF16), _sds((d, t), BF16)],
        compiler_params=_cp("parallel"))(mixed, w_out, res, w)


def _rms_bwd_math(xv, wv, dyv):
    rstd = lax.rsqrt(jnp.mean(xv * xv, axis=-1, keepdims=True) + RMS_EPS)
    xhat = xv * rstd
    gv = dyv * wv
    dx = rstd * (gv - xhat * jnp.mean(gv * xhat, axis=-1, keepdims=True))
    return dx, jnp.sum(dyv * xhat, axis=0, keepdims=True)


def _proj_norm_bwd(dy_a, w_a, dy_b, w_b, x, w, dres, tm, name):
    t, d = x.shape
    ka, kb = dy_a.shape[1], dy_b.shape[1]
    nt = (((1,), (1,)), ((), ()))

    def body(dya_ref, wa_ref, dyb_ref, wb_ref, x_ref, w_ref, dres_ref, dx_ref, dw_ref):
        d_h = (lax.dot_general(dya_ref[...], wa_ref[...], nt, preferred_element_type=F32)
               + lax.dot_general(dyb_ref[...], wb_ref[...], nt, preferred_element_type=F32))
        dx, dw = _rms_bwd_math(x_ref[...], w_ref[...], d_h)
        dx_ref[...] = dres_ref[...] + dx

        @pl.when(pl.program_id(0) == 0)
        def _():
            dw_ref[...] = jnp.zeros_like(dw_ref)

        dw_ref[...] += dw

    return pl.pallas_call(
        body, name=name, grid=(t // tm,),
        in_specs=[_row(tm, ka), _fixed((d, ka)), _row(tm, kb), _fixed((d, kb)), _row(tm, d), _fixed((1, d)),
                  _row(tm, d)],
        out_specs=[_row(tm, d), _fixed((1, d))],
        out_shape=[_sds((t, d)), _sds((1, d))], compiler_params=_cp("arbitrary"))(dy_a, w_a, dy_b, w_b, x, w, dres)


def _ffn_out_loss(act, w_down, x1, w, target, tm):
    t, d = x1.shape
    f = act.shape[1]

    def body(a_ref, wd_ref, x_ref, w_ref, t_ref, dx_ref, dx16_ref, dw_ref, loss_ref):
        xv = x_ref[...] + jnp.dot(a_ref[...], wd_ref[...], preferred_element_type=F32)
        wv = w_ref[...]
        rstd = lax.rsqrt(jnp.mean(xv * xv, axis=-1, keepdims=True) + RMS_EPS)
        err = xv * rstd * wv - t_ref[...]
        dx, dw = _rms_bwd_math(xv, wv, err * (1.0 / d))
        dx_ref[...] = dx
        dx16_ref[...] = dx.astype(BF16)

        @pl.when(pl.program_id(0) == 0)
        def _():
            dw_ref[...] = jnp.zeros_like(dw_ref)
            loss_ref[...] = jnp.zeros_like(loss_ref)

        dw_ref[...] += dw
        loss_ref[...] += 0.5 * jnp.sum(jnp.mean(err * err, axis=-1, keepdims=True), axis=0, keepdims=True)

    return pl.pallas_call(
        body, name="ffn_out_loss", grid=(t // tm,),
        in_specs=[_row(tm, f), _fixed((f, d)), _row(tm, d), _fixed((1, d)), _row(tm, d)],
        out_specs=[_row(tm, d), _row(tm, d), _fixed((1, d)), _fixed((1, 1))],
        out_shape=[_sds((t, d)), _sds((t, d), BF16), _sds((1, d)), _sds((1, 1))],
        compiler_params=_cp("arbitrary"))(act, w_down, x1, w, target)


def _ffn_in(h, w_gate, w_up):
    t, d = h.shape
    f = w_gate.shape[1]
    tm, tn = _tile(t, 512, 8), _tile(f, 1536)

    def body(h_ref, wg_ref, wu_ref, g_ref, u_ref, a_ref, at_ref):
        hv = h_ref[...]
        gv = jnp.dot(hv, wg_ref[...], preferred_element_type=F32)
        uv = jnp.dot(hv, wu_ref[...], preferred_element_type=F32)
        act = gv * jax.nn.sigmoid(gv) * uv
        g_ref[...] = gv.astype(BF16)
        u_ref[...] = uv.astype(BF16)
        a_ref[...] = act.astype(BF16)
        at_ref[...] = jnp.transpose(act).astype(BF16)

    w_spec = pl.BlockSpec((d, tn), lambda i, j: (0, j))
    o_spec = pl.BlockSpec((tm, tn), lambda i, j: (i, j))
    return pl.pallas_call(
        body, name="ffn_in", grid=(t // tm, f // tn),
        in_specs=[pl.BlockSpec((tm, d), lambda i, j: (i, 0)), w_spec, w_spec],
        out_specs=[o_spec, o_spec, o_spec, pl.BlockSpec((tn, tm), lambda i, j: (j, i))],
        out_shape=[_sds((t, f), BF16)] * 3 + [_sds((f, t), BF16)],
        compiler_params=_cp("parallel", "parallel"))(h, w_gate, w_up)


def _ffn_in_bwd(d_out, w_down, gate, up):
    t, d = d_out.shape
    f = w_down.shape[0]
    tm, tn = _tile(t, 512, 8), _tile(f, 1536)

    def body(do_ref, w_ref, g_ref, u_ref, dg_ref, du_ref):
        dv = lax.dot_general(do_ref[...].astype(BF16), w_ref[...], (((1,), (1,)), ((), ())),
                             preferred_element_type=F32)
        gv, uv = g_ref[...].astype(F32), u_ref[...].astype(F32)
        sg = jax.nn.sigmoid(gv)
        du_ref[...] = (dv * gv * sg).astype(BF16)
        dg_ref[...] = (dv * uv * (sg * (1.0 + gv * (1.0 - sg)))).astype(BF16)

    tile = pl.BlockSpec((tm, tn), lambda i, j: (i, j))
    return pl.pallas_call(
        body, name="ffn_in_bwd", grid=(t // tm, f // tn),
        in_specs=[pl.BlockSpec((tm, d), lambda i, j: (i, 0)), pl.BlockSpec((tn, d), lambda i, j: (j, 0)), tile, tile],
        out_specs=[tile, tile], out_shape=[_sds((t, f), BF16)] * 2,
        compiler_params=_cp("parallel", "parallel"))(d_out, w_down, gate, up)


def _halo_specs(tm, width, rows_total):
    per = tm // GROUP
    last = rows_total // GROUP - 1
    prev = pl.BlockSpec((GROUP, width), lambda i: (jnp.maximum(i * per - 1, 0), 0))
    nxt = pl.BlockSpec((GROUP, width), lambda i: (jnp.minimum((i + 1) * per, last), 0))
    return prev, nxt


def _edge_flags(tm, seq):
    i = pl.program_id(0)
    has_prev = jnp.where((i * tm) % seq == 0, 0.0, 1.0).astype(F32)
    has_next = jnp.where(((i + 1) * tm) % seq == 0, 0.0, 1.0).astype(F32)
    return has_prev, has_next


def _shifted(xv, prev_row, next_row):
    tm = xv.shape[0]
    row = lax.broadcasted_iota(jnp.int32, xv.shape, 0)
    down = jnp.where(row == 0, prev_row, pltpu.roll(xv, 1, axis=0))
    up = jnp.where(row == tm - 1, next_row, pltpu.roll(xv, tm - 1, axis=0))
    return down, up


def _shift_bwd(q, p, mu, tm, seq):
    t, w = p.shape
    prev_spec, next_spec = _halo_specs(tm, w, t)

    def body(q_ref, qp_ref, qn_ref, p_ref, pp_ref, pn_ref, mu_ref, dp_ref, dmu_ref):
        has_prev, has_next = _edge_flags(tm, seq)
        muv = mu_ref[0:1, :]
        qv = q_ref[...]
        mq = muv * qv
        mq_down, mq_up = _shifted(mq, muv * qp_ref[GROUP - 1:GROUP, :] * has_prev,
                                  muv * qn_ref[0:1, :] * has_next)
        dp_ref[...] = (qv - mq + 0.5 * (mq_down + mq_up)).astype(BF16)
        pv = p_ref[...]
        p_down, p_up = _shifted(pv, pp_ref[GROUP - 1:GROUP, :] * has_prev, pn_ref[0:1, :] * has_next)

        @pl.when(pl.program_id(0) == 0)
        def _():
            dmu_ref[...] = jnp.zeros_like(dmu_ref)

        dmu_ref[...] += jnp.sum(qv * (0.5 * (p_down + p_up) - pv), axis=0, keepdims=True)

    return pl.pallas_call(
        body, name="shift_bwd", grid=(t // tm,),
        in_specs=[_row(tm, w), prev_spec, next_spec, _row(tm, w), prev_spec, next_spec, _fixed((GROUP, w))],
        out_specs=[_row(tm, w), _fixed((1, w))],
        out_shape=[_sds((t, w), BF16), _sds((1, w))], compiler_params=_cp("arbitrary"))(q, q, q, p, p, p, mu)


@jax.custom_vjp
def _bdot(a, b):
    return jnp.dot(a.astype(BF16), b.astype(BF16), preferred_element_type=F32)


def _bdot_fwd(a, b):
    return _bdot(a, b), (a, b)


def _bdot_bwd(res, g):
    a, b = res
    gb = g.astype(BF16)
    da = lax.dot_general(gb, b.astype(BF16), (((1,), (1,)), ((), ())), preferred_element_type=F32)
    db = lax.dot_general(a.astype(BF16), gb, (((0,), (0,)), ((), ())), preferred_element_type=F32)
    return da, db


_bdot.defvjp(_bdot_fwd, _bdot_bwd)


def _seg_raw(x, ones_blocks):
    hi = x.astype(BF16)
    lo = (x - hi.astype(F32)).astype(BF16)
    return (jnp.dot(hi, ones_blocks, preferred_element_type=F32)
            + jnp.dot(lo, ones_blocks, preferred_element_type=F32))


@jax.custom_vjp
def _seg(x, ones_blocks):
    return _seg_raw(x, ones_blocks)


def _seg_fwd(x, ones_blocks):
    return _seg_raw(x, ones_blocks), ones_blocks


def _seg_bwd(ones_blocks, g):
    return _seg_raw(g, ones_blocks), jnp.zeros_like(ones_blocks)


_seg.defvjp(_seg_fwd, _seg_bwd)


def _head_ones():
    h = jnp.arange(D_RWKV) // HEAD
    return (h[:, None] == h[None, :]).astype(BF16)


def _prep_math(ps, k_k, w0_f, a0_f, k_a_f, w0_b, a0_b, k_a_b, wup_f, aup_f, wup_b, aup_b, gup, ones_blocks):
    r = ps[:, 0:512]
    k = ps[:, 512:1024]
    v = ps[:, 1024:1536]
    xwa = ps[:, 1536:1664]
    xg = ps[:, 1664:D_SHIFT_PAD]
    kk_raw = k * k_k
    norm = jnp.sqrt(_seg(kk_raw * kk_raw, ones_blocks))
    kk = kk_raw / jnp.maximum(norm, NORM_EPS)
    t_xwa = jnp.tanh(xwa)
    outs = [r, v, kk]
    for w0, a0, k_a, wup, aup in ((w0_f, a0_f, k_a_f, wup_f, aup_f), (w0_b, a0_b, k_a_b, wup_b, aup_b)):
        decay = jnp.exp(-LOG_DECAY_SCALE * jax.nn.sigmoid(w0 + _bdot(t_xwa, wup)))
        rate = jax.nn.sigmoid(a0 + _bdot(xwa, aup))
        outs += [decay, k * (1.0 + (rate - 1.0) * k_a), kk * rate]
    outs.append(_bdot(jax.nn.sigmoid(xg), gup))
    return tuple(outs)


def _prep_args(tm, ps_tile, pv_ref, mat_refs, ones_ref):
    vecs = [jnp.broadcast_to(pv_ref[j:j + 1, :], (tm, D_RWKV)) for j in range(7)]
    return [ps_tile] + vecs + [m[...] for m in mat_refs] + [ones_ref[...]]


_PREP_MAT_SHAPES = ((128, D_RWKV),) * 4 + ((D_GATE_PAD, D_RWKV),)


def _shift_prep_fwd(p, mu, pvec, mats, ones_blocks, tm, seq):
    t, w = p.shape
    prev_spec, next_spec = _halo_specs(tm, w, t)

    def body(p_ref, hp_ref, hn_ref, mu_ref, pv_ref, m0, m1, m2, m3, m4, ones_ref, ps_ref, *out_refs):
        has_prev, has_next = _edge_flags(tm, seq)
        xv = p_ref[...]
        down, up = _shifted(xv, hp_ref[GROUP - 1:GROUP, :] * has_prev, hn_ref[0:1, :] * has_next)
        ps_tile = xv + mu_ref[0:1, :] * (0.5 * (down + up) - xv)
        ps_ref[...] = ps_tile
        outs = _prep_math(*_prep_args(tm, ps_tile, pv_ref, (m0, m1, m2, m3, m4), ones_ref))
        for o_ref, val in zip(out_refs, outs[2:]):
            o_ref[...] = val

    return pl.pallas_call(
        body, name="shift_prep_fwd", grid=(t // tm,),
        in_specs=[_row(tm, w), prev_spec, next_spec, _fixed((GROUP, w)), _fixed((8, D_RWKV))]
        + [_fixed(s) for s in _PREP_MAT_SHAPES] + [_fixed((D_RWKV, D_RWKV))],
        out_specs=[_row(tm, w)] + [_row(tm, D_RWKV)] * 8, out_shape=[_sds((t, w))] + [_sds((t, D_RWKV))] * 8,
        compiler_params=_cp("parallel"))(p, p, p, mu, pvec, *mats, ones_blocks)


def _prep_bwd(ps, pvec, mats, ones_blocks, cts, tm):
    t = ps.shape[0]
    counts = [len(c) for c in cts]
    flat = [a for c in cts for a in c]

    def body(ps_ref, pv_ref, m0, m1, m2, m3, m4, ones_ref, *refs):
        ct_refs = refs[:len(flat)]
        q_ref, dpv_ref = refs[len(flat)], refs[len(flat) + 1]
        dmat_refs = refs[len(flat) + 2:]
        args = _prep_args(tm, ps_ref[...], pv_ref, (m0, m1, m2, m3, m4), ones_ref)
        _, vjp = jax.vjp(lambda *a: _prep_math(*a, args[-1]), *args[:-1])
        ct_vals, pos = [], 0
        for n in counts:
            val = ct_refs[pos][...]
            for extra in ct_refs[pos + 1:pos + n]:
                val = val + extra[...]
            ct_vals.append(val)
            pos += n
        grads = vjp(tuple(ct_vals))
        q_ref[...] = grads[0]

        @pl.when(pl.program_id(0) == 0)
        def _():
            dpv_ref[...] = jnp.zeros_like(dpv_ref)
            for d_ref in dmat_refs:
                d_ref[...] = jnp.zeros_like(d_ref)

        for j in range(7):
            dpv_ref[j:j + 1, :] += jnp.sum(grads[1 + j], axis=0, keepdims=True)
        for d_ref, gm in zip(dmat_refs, grads[8:13]):
            d_ref[...] += gm

    return pl.pallas_call(
        body, name="prep_bwd", grid=(t // tm,),
        in_specs=[_row(tm, D_SHIFT_PAD), _fixed((8, D_RWKV))] + [_fixed(s) for s in _PREP_MAT_SHAPES]
        + [_fixed((D_RWKV, D_RWKV))] + [_row(tm, D_RWKV)] * len(flat),
        out_specs=[_row(tm, D_SHIFT_PAD), _fixed((8, D_RWKV))] + [_fixed(s) for s in _PREP_MAT_SHAPES],
        out_shape=[_sds((t, D_SHIFT_PAD)), _sds((8, D_RWKV))] + [_sds(s) for s in _PREP_MAT_SHAPES],
        compiler_params=_cp("arbitrary"))(ps, pvec, *mats, ones_blocks, *flat)


def _pair_ones():
    h = jnp.arange(2 * HEAD) // HEAD
    block = (h[:, None] == h[None, :]).astype(BF16)
    return jnp.concatenate([block, block], axis=0)


def _diag_mask():
    lane = lax.broadcasted_iota(jnp.int32, (HEAD, 2 * HEAD), 1)
    sub = lax.broadcasted_iota(jnp.int32, (HEAD, 2 * HEAD), 0)
    return jnp.where((lane & (HEAD - 1)) == sub, 1.0, 0.0).astype(F32)


def _to_row(cols, dmask):
    return jnp.sum(cols * dmask, axis=0, keepdims=True)


def _seg_many(exact, rounded, ones_pair):
    out_exact, out_rounded = [], []
    if exact:
        parts = []
        for x in exact:
            hi = x.astype(BF16)
            parts.append(jnp.concatenate([hi, (x - hi.astype(F32)).astype(BF16)], axis=1))
        res = jnp.dot(jnp.concatenate(parts, axis=0), ones_pair, preferred_element_type=F32)
        out_exact = [res[HEAD * c:HEAD * (c + 1)] for c in range(len(exact))]
    if rounded:
        res = jnp.dot(jnp.concatenate([x.astype(BF16) for x in rounded], axis=0), ones_pair[0:2 * HEAD],
                      preferred_element_type=F32)
        out_rounded = [res[HEAD * c:HEAD * (c + 1)] for c in range(len(rounded))]
    return out_exact, out_rounded


N_CHAIN = 2 * N_PAIR


def _chain(c):
    d, p = divmod(c, N_PAIR)
    return d, slice(2 * HEAD * p, 2 * HEAD * (p + 1))


def _scan_specs(n_chunks, col_blocks, fwd_chunk, bwd_chunk):
    def spec(chunk_of, col):
        return pl.BlockSpec((SCAN_CHUNK, D_RWKV), lambda b, g: (b * n_chunks + chunk_of(g), col))
    return [spec(fwd_chunk, c) for c in col_blocks] + [spec(bwd_chunk, c) for c in col_blocks]


def _scan_fwd(ps, kk, dirs, batch, seq):
    t = batch * seq
    n = seq // SCAN_CHUNK
    groups = SCAN_CHUNK // GROUP
    up = lambda g: g
    down = lambda g: n - 1 - g
    col_blocks = (0, 2, 0, 0, 0, 0)

    def body(*refs):
        dir_refs = (refs[0:6], refs[6:12])
        ones_ref = refs[12]
        y_refs, hist_refs, st_ref = refs[13:15], refs[15:17], refs[17]

        @pl.when(pl.program_id(1) == 0)
        def _():
            st_ref[...] = jnp.zeros_like(st_ref)

        ones_pair = ones_ref[...]
        dmask = _diag_mask()
        dmask_b = dmask.astype(BF16)
        sub8 = lax.broadcasted_iota(jnp.int32, (GROUP, 2 * HEAD), 0)

        def group(gi, carry):
            off = (pl.multiple_of(gi * GROUP, GROUP), pl.multiple_of((groups - 1 - gi) * GROUP, GROUP))
            loaded = [tuple(ref[pl.ds(off[d], GROUP), :] for ref in dir_refs[d]) for d in range(2)]
            states = list(carry)
            y_acc = [jnp.zeros((GROUP, 2 * HEAD), F32) for _ in range(N_CHAIN)]
            for step in range(GROUP):
                rows, idx = [], []
                for c in range(N_CHAIN):
                    d, lanes = _chain(c)
                    i = step if d == 0 else GROUP - 1 - step
                    idx.append(i)
                    rows.append(tuple(x8[i:i + 1, lanes] for x8 in loaded[d]))
                    hist_refs[d][c % N_PAIR, gi * GROUP + step] = states[c]
                _, v_cols = _seg_many([], [dmask_b * rows[c][1].astype(BF16) for c in range(N_CHAIN)], ones_pair)
                sas, _ = _seg_many([states[c] * rows[c][2] for c in range(N_CHAIN)], [], ones_pair)
                for c in range(N_CHAIN):
                    _, _, _, w_row, kd_row, b_row = rows[c]
                    states[c] = states[c] * w_row - sas[c] * b_row + v_cols[c] * kd_row
                _, ys = _seg_many([], [states[c] * rows[c][0] for c in range(N_CHAIN)], ones_pair)
                for c in range(N_CHAIN):
                    y_acc[c] = jnp.where(sub8 == idx[c], _to_row(ys[c], dmask), y_acc[c])
            for c in range(N_CHAIN):
                d, lanes = _chain(c)
                y_refs[d][pl.ds(off[d], GROUP), lanes] = y_acc[c]
            return tuple(states)

        final = lax.fori_loop(0, groups, group, tuple(st_ref[c] for c in range(N_CHAIN)))
        for c in range(N_CHAIN):
            st_ref[c] = final[c]
            hist_refs[c // N_PAIR][c % N_PAIR, SCAN_CHUNK] = final[c]

    y_spec_f = pl.BlockSpec((SCAN_CHUNK, D_RWKV), lambda b, g: (b * n + up(g), 0))
    y_spec_b = pl.BlockSpec((SCAN_CHUNK, D_RWKV), lambda b, g: (b * n + down(g), 0))
    hist_shape = (batch, n, N_PAIR, SCAN_CHUNK + 1, HEAD, 2 * HEAD)
    hist_block = (None, None, N_PAIR, SCAN_CHUNK + 1, HEAD, 2 * HEAD)
    hist_spec_f = pl.BlockSpec(hist_block, lambda b, g: (b, up(g), 0, 0, 0, 0))
    hist_spec_b = pl.BlockSpec(hist_block, lambda b, g: (b, down(g), 0, 0, 0, 0))
    ones_spec = pl.BlockSpec((4 * HEAD, 2 * HEAD), lambda b, g: (0, 0))
    (wf, kdf, bf), (wb, kdb, bb) = dirs
    return pl.pallas_call(
        body, name="wkv_fwd", grid=(batch, n),
        in_specs=_scan_specs(n, col_blocks, up, down) + [ones_spec],
        out_specs=[y_spec_f, y_spec_b, hist_spec_f, hist_spec_b],
        out_shape=[_sds((t, D_RWKV)), _sds((t, D_RWKV)), _sds(hist_shape), _sds(hist_shape)],
        scratch_shapes=[pltpu.VMEM((N_CHAIN, HEAD, 2 * HEAD), F32)],
        compiler_params=_cp("parallel", "arbitrary"),
    )(ps, ps, kk, wf, kdf, bf, ps, ps, kk, wb, kdb, bb, _pair_ones())


def _scan_bwd(ps, kk, dirs, dy, hist_f, hist_b, batch, seq):
    t = batch * seq
    n = seq // SCAN_CHUNK
    groups = SCAN_CHUNK // GROUP
    fwd_chunk = lambda g: n - 1 - g
    bwd_chunk = lambda g: g
    col_blocks = (0, 2, 0, 0, 0, 0, 0)

    def undo_group(dir_refs, out_refs, hist_refs, gi, d_states, ones_pair, dmask, sub8):
        d_states = list(d_states)
        loaded, blocks = [], []
        for d in range(2):
            blk = groups - 1 - gi if d == 0 else gi
            blocks.append(pl.ds(pl.multiple_of(blk * GROUP, GROUP), GROUP))
            r8, v8, kk8, w8, kd8, b8, dy8 = (ref[blocks[d], :] for ref in dir_refs[d])
            loaded.append((r8, v8, kk8, w8, kd8, -b8, dy8))
        acc = [[jnp.zeros((GROUP, 2 * HEAD), F32) for _ in range(6)] for _ in range(N_CHAIN)]
        for step in range(GROUP):
            rows, idx, before, after = [], [], [], []
            for c in range(N_CHAIN):
                d, lanes = _chain(c)
                i = GROUP - 1 - step if d == 0 else step
                q = (groups - 1 - gi) * GROUP + i if d == 0 else SCAN_CHUNK - 1 - (gi * GROUP + i)
                idx.append(i)
                rows.append(tuple(x8[i:i + 1, lanes] for x8 in loaded[d]))
                before.append(hist_refs[d][c % N_PAIR, q])
                after.append(hist_refs[d][c % N_PAIR, q + 1])
            _, cols = _seg_many([], [dmask.astype(BF16) * rows[c][j].astype(BF16) for c in range(N_CHAIN) for j in (1, 6)],
                                ones_pair)
            v_cols, dy_cols = cols[0::2], cols[1::2]
            d_now = [d_states[c] + dy_cols[c] * rows[c][0] for c in range(N_CHAIN)]
            d_sas, _ = _seg_many([d_now[c] * rows[c][5] for c in range(N_CHAIN)], [], ones_pair)
            _, others = _seg_many(
                [], [x for c in range(N_CHAIN) for x in (before[c] * rows[c][2], d_now[c] * rows[c][4])], ones_pair)
            for c in range(N_CHAIN):
                sa, d_sa, dv_cols = others[2 * c], d_sas[c], others[2 * c + 1]
                rows_out = (
                    jnp.sum(after[c] * dy_cols[c], axis=0, keepdims=True),
                    jnp.sum(d_now[c] * before[c], axis=0, keepdims=True),
                    jnp.sum(d_now[c] * v_cols[c], axis=0, keepdims=True),
                    _to_row(dv_cols, dmask),
                    jnp.sum(before[c] * d_sa, axis=0, keepdims=True),
                    -jnp.sum(d_now[c] * sa, axis=0, keepdims=True),
                )
                acc[c] = [jnp.where(sub8 == idx[c], val, a) for val, a in zip(rows_out, acc[c])]
                d_states[c] = d_now[c] * rows[c][3] + d_sa * rows[c][2]
        for c in range(N_CHAIN):
            d, lanes = _chain(c)
            for o_ref, val in zip(out_refs[d], acc[c]):
                o_ref[blocks[d], lanes] = val
        return tuple(d_states)

    def body(*refs):
        dir_refs = (refs[0:7], refs[7:14])
        hist_refs, ones_ref = refs[14:16], refs[16]
        out_refs = (refs[17:23], refs[23:29])
        dst_ref = refs[29]

        @pl.when(pl.program_id(1) == 0)
        def _():
            dst_ref[...] = jnp.zeros_like(dst_ref)

        ones_pair = ones_ref[...]
        dmask = _diag_mask()
        sub8 = lax.broadcasted_iota(jnp.int32, (GROUP, 2 * HEAD), 0)

        def group(gi, carry):
            return undo_group(dir_refs, out_refs, hist_refs, gi, carry, ones_pair, dmask, sub8)

        final = lax.fori_loop(0, groups, group, tuple(dst_ref[c] for c in range(N_CHAIN)))
        for c in range(N_CHAIN):
            dst_ref[c] = final[c]

    blk = (SCAN_CHUNK, D_RWKV)
    out_f = pl.BlockSpec(blk, lambda b, g: (b * n + fwd_chunk(g), 0))
    out_b = pl.BlockSpec(blk, lambda b, g: (b * n + bwd_chunk(g), 0))
    hist_block = (None, None, N_PAIR, SCAN_CHUNK + 1, HEAD, 2 * HEAD)
    hist_spec_f = pl.BlockSpec(hist_block, lambda b, g: (b, fwd_chunk(g), 0, 0, 0, 0))
    hist_spec_b = pl.BlockSpec(hist_block, lambda b, g: (b, bwd_chunk(g), 0, 0, 0, 0))
    ones_spec = pl.BlockSpec((4 * HEAD, 2 * HEAD), lambda b, g: (0, 0))
    (wf, kdf, bf), (wb, kdb, bb) = dirs
    outs = pl.pallas_call(
        body, name="wkv_bwd", grid=(batch, n),
        in_specs=_scan_specs(n, col_blocks, fwd_chunk, bwd_chunk) + [hist_spec_f, hist_spec_b, ones_spec],
        out_specs=[out_f] * 6 + [out_b] * 6,
        out_shape=[_sds((t, D_RWKV))] * 12,
        scratch_shapes=[pltpu.VMEM((N_CHAIN, HEAD, 2 * HEAD), F32)],
        compiler_params=_cp("parallel", "arbitrary"),
    )(ps, ps, kk, wf, kdf, bf, dy, ps, ps, kk, wb, kdb, bb, dy, hist_f, hist_b, _pair_ones())
    return outs[0:6], outs[6:12]


def _post_math(y, r, kd_f, kd_b, v, gate, gn_w, gn_b, rk_f, rk_b, ones_blocks):
    mean = _seg(y, ones_blocks) * (1.0 / HEAD)
    yc = y - mean
    var = _seg(yc * yc, ones_blocks) * (1.0 / HEAD)
    yn = yc * lax.rsqrt(var + GN_EPS) * gn_w + gn_b
    bonus = _seg(r * kd_f * rk_f, ones_blocks) * v + _seg(r * kd_b * rk_b, ones_blocks) * v
    return (yn + bonus) * gate


def _conv_parts(pc, halo_prev, halo_next, has_prev, has_next):
    gate_b, gate_c, hid = pc[:, 0:512], pc[:, 512:1024], pc[:, 1024:1536]
    u = gate_c * hid
    u_prev_row = halo_prev[GROUP - 1:GROUP, 512:1024] * halo_prev[GROUP - 1:GROUP, 1024:1536] * has_prev
    u_next_row = halo_next[0:1, 512:1024] * halo_next[0:1, 1024:1536] * has_next
    u_down, u_up = _shifted(u, u_prev_row, u_next_row)
    return gate_b, gate_c, hid, u, u_down, u_up


def _post_specs(tm, t):
    pc_prev, pc_next = _halo_specs(tm, D_CONV3, t)
    col = lambda c: pl.BlockSpec((tm, D_RWKV), lambda i: (i, c))
    return ([col(0), col(0), col(0), col(0), col(0), col(2), col(0), _row(tm, D_CONV3), pc_prev, pc_next,
             _fixed((8, D_RWKV)), _fixed((D_RWKV, D_RWKV))])


def _post_fwd(y_f, y_b, ps, kd_f, kd_b, gate, pc, qvec, ones_blocks, tm, seq):
    t = ps.shape[0]

    def body(yf_ref, yb_ref, r_ref, kdf_ref, kdb_ref, v_ref, g_ref, pc_ref, hp_ref, hn_ref, qv_ref, ones_ref,
             o_ref, ot_ref):
        has_prev, has_next = _edge_flags(tm, seq)
        vec = [jnp.broadcast_to(qv_ref[j:j + 1, :], (tm, D_RWKV)) for j in range(7)]
        o_rwkv = _post_math(yf_ref[...] + yb_ref[...], r_ref[...], kdf_ref[...], kdb_ref[...], v_ref[...],
                            g_ref[...], vec[0], vec[1], vec[2], vec[3], ones_ref[...])
        gate_b, _, _, u, u_down, u_up = _conv_parts(pc_ref[...], hp_ref[...], hn_ref[...], has_prev, has_next)
        o_conv = gate_b * (vec[4] * u_down + vec[5] * u + vec[6] * u_up)
        for half, val in enumerate((o_rwkv, o_conv)):
            o_ref[:, D_RWKV * half:D_RWKV * (half + 1)] = val.astype(BF16)
            ot_ref[D_RWKV * half:D_RWKV * (half + 1), :] = jnp.transpose(val).astype(BF16)

    return pl.pallas_call(
        body, name="post_fwd", grid=(t // tm,), in_specs=_post_specs(tm, t),
        out_specs=[_row(tm, D_MODEL), _col(tm, D_MODEL)],
        out_shape=[_sds((t, D_MODEL), BF16), _sds((D_MODEL, t), BF16)], compiler_params=_cp("parallel"),
    )(y_f, y_b, ps, kd_f, kd_b, ps, gate, pc, pc, pc, qvec, ones_blocks)


def _post_bwd(d_out, y_f, y_b, ps, kd_f, kd_b, gate, pc, qvec, ones_blocks, tm, seq):
    t = ps.shape[0]
    do_prev, do_next = _halo_specs(tm, D_MODEL, t)

    def body(do_ref, dop_ref, don_ref, yf_ref, yb_ref, r_ref, kdf_ref, kdb_ref, v_ref, g_ref, pc_ref, hp_ref,
             hn_ref, qv_ref, ones_ref, dy_ref, dr_ref, dkdf_ref, dkdb_ref, dv_ref, dg_ref, dpc_ref, dqv_ref):
        has_prev, has_next = _edge_flags(tm, seq)
        vec = [jnp.broadcast_to(qv_ref[j:j + 1, :], (tm, D_RWKV)) for j in range(7)]
        ones_v = ones_ref[...]
        args = (yf_ref[...] + yb_ref[...], r_ref[...], kdf_ref[...], kdb_ref[...], v_ref[...], g_ref[...],
                vec[0], vec[1], vec[2], vec[3])
        _, vjp = jax.vjp(lambda *a: _post_math(*a, ones_v), *args)
        grads = vjp(do_ref[:, 0:D_RWKV])
        for o_ref, gval in zip((dy_ref, dr_ref, dkdf_ref, dkdb_ref, dv_ref, dg_ref), grads[0:6]):
            o_ref[...] = gval

        hp, hn = hp_ref[...], hn_ref[...]
        gate_b, gate_c, hid, u, u_down, u_up = _conv_parts(pc_ref[...], hp, hn, has_prev, has_next)
        d_oc = do_ref[:, D_RWKV:2 * D_RWKV]
        d_cu = d_oc * gate_b
        d_cu_prev = dop_ref[GROUP - 1:GROUP, D_RWKV:2 * D_RWKV] * hp[GROUP - 1:GROUP, 0:512] * has_prev
        d_cu_next = don_ref[0:1, D_RWKV:2 * D_RWKV] * hn[0:1, 0:512] * has_next
        d_cu_down, d_cu_up = _shifted(d_cu, d_cu_prev, d_cu_next)
        d_u = vec[5] * d_cu + vec[4] * d_cu_up + vec[6] * d_cu_down
        dpc_ref[:, 0:512] = (d_oc * (vec[4] * u_down + vec[5] * u + vec[6] * u_up)).astype(BF16)
        dpc_ref[:, 512:1024] = (d_u * hid).astype(BF16)
        dpc_ref[:, 1024:1536] = (d_u * gate_c).astype(BF16)

        @pl.when(pl.program_id(0) == 0)
        def _():
            dqv_ref[...] = jnp.zeros_like(dqv_ref)

        vec_grads = list(grads[6:10]) + [d_cu * u_down, d_cu * u, d_cu * u_up]
        for j, gval in enumerate(vec_grads):
            dqv_ref[j:j + 1, :] += jnp.sum(gval, axis=0, keepdims=True)

    return pl.pallas_call(
        body, name="post_bwd", grid=(t // tm,),
        in_specs=[_row(tm, D_MODEL), do_prev, do_next] + _post_specs(tm, t),
        out_specs=[_row(tm, D_RWKV)] * 6 + [_row(tm, D_CONV3), _fixed((8, D_RWKV))],
        out_shape=[_sds((t, D_RWKV))] * 6 + [_sds((t, D_CONV3), BF16), _sds((8, D_RWKV))],
        compiler_params=_cp("arbitrary"),
    )(d_out, d_out, d_out, y_f, y_b, ps, kd_f, kd_b, ps, gate, pc, pc, pc, qvec, ones_blocks)


def _adamw_math(wv, gv, mv, vv):
    m2 = ADAM_B1 * mv + (1.0 - ADAM_B1) * gv
    v2 = ADAM_B2 * vv + (1.0 - ADAM_B2) * (gv * gv)
    m_hat = m2 / (1.0 - ADAM_B1 ** ADAM_STEP)
    v_hat = v2 / (1.0 - ADAM_B2 ** ADAM_STEP)
    return -ADAM_LR * (m_hat / (jnp.sqrt(v_hat) + ADAM_EPS) + ADAM_WD * wv), m2, v2


def _adamw_small(items):
    n = len(items)

    def body(*refs):
        ins, outs = refs[:4 * n], refs[4 * n:]
        for k in range(n):
            w_ref, g_ref, m_ref, v_ref = ins[4 * k:4 * k + 4]
            for o_ref, val in zip(outs[3 * k:3 * k + 3], _adamw_math(w_ref[...], g_ref[...], m_ref[...], v_ref[...])):
                o_ref[...] = val

    flat = [a for item in items for a in item]
    outs = pl.pallas_call(
        body, name="adamw_small", out_shape=[_sds(item[0].shape) for item in items for _ in range(3)],
        compiler_params=_cp())(*flat)
    return [tuple(outs[3 * k:3 * k + 3]) for k in range(n)]


def _adamw(w, g, m, v, name):
    r, c = w.shape[-2:]
    tr = _tile(r, 256, 8)
    if w.ndim == 3:
        spec = pl.BlockSpec((None, tr, c), lambda i: (0, i, 0))
    else:
        spec = pl.BlockSpec((tr, c), lambda i: (i, 0))

    def body(w_ref, g_ref, m_ref, v_ref, d_ref, nm_ref, nv_ref):
        d_ref[...], nm_ref[...], nv_ref[...] = _adamw_math(w_ref[...], g_ref[...], m_ref[...], v_ref[...])

    return pl.pallas_call(
        body, name=name, grid=(r // tr,), in_specs=[spec] * 4, out_specs=[spec] * 3,
        out_shape=[_sds(w.shape)] * 3, compiler_params=_cp("parallel"))(w, g, m, v)


_ANY = pl.BlockSpec(memory_space=pl.ANY)


def _place():
    return lax.axis_index("x"), lax.axis_index("y"), lax.axis_index("c")


def _other_chips(x, y):
    return [(1 - x, y), (x, 1 - y), (1 - x, 1 - y)]


def _remote(src, dst, send_sems, recv_sems, k, to):
    return pltpu.make_async_remote_copy(src_ref=src, dst_ref=dst, send_sem=send_sems.at[k],
                                        recv_sem=recv_sems.at[k], device_id=to, device_id_type=MESH)


def _gather_weights(pack):
    rows, width = pack.shape
    half = rows // 2

    def body(x_ref, out_ref, send_sems, recv_sems):
        x, y, c = _place()
        sibling = (x, y, 1 - c)
        chips = _other_chips(x, y)

        def block(chip, part):
            return out_ref.at[2 * chip[0] + chip[1], pl.ds(part * half, half), :]

        first = [_remote(x_ref.at[pl.ds(c * half, half), :], block((x, y), c), send_sems, recv_sems, j, (*chip, c))
                 for j, chip in enumerate(chips)]
        for cp in first:
            cp.start()
        passed = [_remote(block(chip, c), block(chip, c), send_sems, recv_sems, 3 + j, sibling)
                  for j, chip in enumerate(chips)]
        for j, chip in enumerate(chips):
            _remote(block(chip, c), block(chip, c), send_sems, recv_sems, j, sibling).wait_recv()
            passed[j].start()
        for j, chip in enumerate(chips):
            _remote(block(chip, 1 - c), block(chip, 1 - c), send_sems, recv_sems, 3 + j, sibling).wait_recv()
        for cp in first + passed:
            cp.wait_send()

    return pl.pallas_call(
        body, name="gather_weights", in_specs=[_ANY], out_specs=_ANY,
        out_shape=_sds((N_SHARD, rows, width), pack.dtype),
        scratch_shapes=[pltpu.SemaphoreType.DMA((6,)), pltpu.SemaphoreType.DMA((6,))],
    )(pack)


_HBM = pl.BlockSpec(memory_space=pltpu.HBM)
_SEMS = pl.BlockSpec(memory_space=pltpu.SEMAPHORE)
_DATAFLOW = pltpu.SideEffectType.DATAFLOW_SIDE_EFFECTING


def _fetch_start(pack, after):
    def body(x_ref, land_ref, after_ref, send_sems, recv_sems, x_thru, land_thru, token):
        x, y, c = _place()
        for j, chip in enumerate(_other_chips(x, y)):
            _remote(x_ref, land_ref.at[2 * x + y], send_sems, recv_sems, j, (*chip, c)).start()
        token[...] = jnp.zeros_like(token)

    land = lax.empty((N_SHARD,) + pack.shape, pack.dtype)
    return pl.pallas_call(
        body, name="fetch_ffn_start",
        out_shape=(pltpu.SemaphoreType.DMA((3,)), pltpu.SemaphoreType.DMA((3,)), pltpu.HBM(pack.shape, pack.dtype),
                   pltpu.HBM(land.shape, land.dtype), _sds((8, 128))),
        in_specs=(_HBM, _HBM, _ANY), out_specs=(_SEMS, _SEMS, _HBM, _HBM, pl.BlockSpec(memory_space=pltpu.VMEM)),
        input_output_aliases={0: 2, 1: 3}, compiler_params=pltpu.CompilerParams(has_side_effects=_DATAFLOW),
    )(pltpu.with_memory_space_constraint(pack, pltpu.HBM), pltpu.with_memory_space_constraint(land, pltpu.HBM), after)


def _fetch_wait(send_sems, recv_sems, pack_thru, land_thru, after):
    def body(x_ref, land_ref, send_sems, recv_sems, after_ref, x_dead, got_ref):
        x, y, c = _place()
        for j, chip in enumerate(_other_chips(x, y)):
            cp = _remote(x_ref, land_ref.at[2 * chip[0] + chip[1]], send_sems, recv_sems, j, (*chip, c))
            cp.wait_send()
            cp.wait_recv()

    return pl.pallas_call(
        body, name="fetch_ffn_wait",
        out_shape=(pltpu.HBM(pack_thru.shape, pack_thru.dtype), pltpu.HBM(land_thru.shape, land_thru.dtype)),
        in_specs=(_HBM, _HBM, _SEMS, _SEMS, _ANY), out_specs=(_HBM, _HBM), input_output_aliases={0: 0, 1: 1},
        compiler_params=pltpu.CompilerParams(has_side_effects=_DATAFLOW),
    )(pack_thru, land_thru, send_sems, recv_sems, after)[1]


def _swap_with_sibling(block, name):
    def body(x_ref, out_ref, send_sems, recv_sems):
        x, y, c = _place()
        cp = _remote(x_ref, out_ref, send_sems, recv_sems, 0, (x, y, 1 - c))
        cp.start()
        cp.wait()

    return pl.pallas_call(
        body, name=name, in_specs=[_ANY], out_specs=_ANY, out_shape=_sds(block.shape, block.dtype),
        scratch_shapes=[pltpu.SemaphoreType.DMA((1,)), pltpu.SemaphoreType.DMA((1,))],
    )(block)


def _swap_other_half(packed, tag):
    slots, rows, width = packed.shape
    half = rows // 2

    def body(x_ref, out_ref, send_sems, recv_sems):
        x, y, c = _place()
        cp = _remote(x_ref.at[:, pl.ds((1 - c) * half, half), :], out_ref, send_sems, recv_sems, 0, (x, y, 1 - c))
        cp.start()
        cp.wait()

    return pl.pallas_call(
        body, name="swap_halves_" + tag, in_specs=[_ANY], out_specs=_ANY, out_shape=_sds((slots, half, width)),
        scratch_shapes=[pltpu.SemaphoreType.DMA((1,)), pltpu.SemaphoreType.DMA((1,))],
    )(packed)


def _add_halves(packed, got, c, tag):
    slots, rows, width = packed.shape
    half = rows // 2
    tr = _tile(half, 408, 16)
    per = half // tr
    block = (None, tr, width)

    def body(c_ref, mine_ref, got_ref, sum_ref, sum16_ref):
        acc = mine_ref[...] + got_ref[...]
        sum_ref[...] = acc
        sum16_ref[...] = acc.astype(BF16)

    plain = pl.BlockSpec(block, lambda s, i, c_ref: (s, i, 0))
    grid_spec = pltpu.PrefetchScalarGridSpec(
        num_scalar_prefetch=1, grid=(slots, per),
        in_specs=[pl.BlockSpec(block, lambda s, i, c_ref: (s, c_ref[0] * per + i, 0)), plain],
        out_specs=[plain, plain])
    return pl.pallas_call(
        body, name="add_halves_" + tag, grid_spec=grid_spec,
        out_shape=[_sds((slots, half, width)), _sds((slots, half, width), BF16)],
        compiler_params=_cp("parallel", "parallel"))(c.reshape(1).astype(jnp.int32), packed, got)


def _add_quarters(chip_sum, others, chip, tag):
    _, rows, width = chip_sum.shape
    tr = _tile(rows, 408, 16)

    def body(chip_ref, own_ref, others_ref, o_ref):
        acc = own_ref[...]
        for j in range(3):
            acc = acc + others_ref[j].astype(F32)
        o_ref[...] = acc

    grid_spec = pltpu.PrefetchScalarGridSpec(
        num_scalar_prefetch=1, grid=(rows // tr,),
        in_specs=[pl.BlockSpec((None, tr, width), lambda i, chip_ref: (chip_ref[0], i, 0)),
                  pl.BlockSpec((3, tr, width), lambda i, chip_ref: (0, i, 0))],
        out_specs=pl.BlockSpec((tr, width), lambda i, chip_ref: (i, 0)))
    return pl.pallas_call(
        body, name="add_quarters_" + tag, grid_spec=grid_spec, out_shape=_sds((rows, width)),
        compiler_params=_cp("parallel"))(chip.reshape(1).astype(jnp.int32), chip_sum, others)


def _exchange_start(parts, tag):
    _, rows, width = parts.shape

    def body(x_ref, land_ref, send_sems, recv_sems, x_thru, land_thru, token):
        x, y, c = _place()
        for j, chip in enumerate(_other_chips(x, y)):
            _remote(x_ref.at[2 * chip[0] + chip[1]], land_ref.at[j], send_sems, recv_sems, j, (*chip, c)).start()
        token[...] = jnp.zeros_like(token)

    land = lax.empty((3, rows, width), parts.dtype)
    return pl.pallas_call(
        body, name="exchange_" + tag + "_start",
        out_shape=(pltpu.SemaphoreType.DMA((3,)), pltpu.SemaphoreType.DMA((3,)), pltpu.HBM(parts.shape, parts.dtype),
                   pltpu.HBM(land.shape, land.dtype), _sds((8, 128))),
        in_specs=(_HBM, _HBM), out_specs=(_SEMS, _SEMS, _HBM, _HBM, pl.BlockSpec(memory_space=pltpu.VMEM)),
        input_output_aliases={0: 2, 1: 3}, compiler_params=pltpu.CompilerParams(has_side_effects=_DATAFLOW),
    )(pltpu.with_memory_space_constraint(parts, pltpu.HBM), pltpu.with_memory_space_constraint(land, pltpu.HBM))


def _exchange_wait(send_sems, recv_sems, parts_thru, land_thru, after, tag):
    def body(x_ref, land_ref, send_sems, recv_sems, after_ref, x_dead, got_ref):
        x, y, c = _place()
        for j, chip in enumerate(_other_chips(x, y)):
            cp = _remote(x_ref.at[2 * chip[0] + chip[1]], land_ref.at[j], send_sems, recv_sems, j, (*chip, c))
            cp.wait_send()
            cp.wait_recv()

    return pl.pallas_call(
        body, name="exchange_" + tag + "_wait",
        out_shape=(pltpu.HBM(parts_thru.shape, parts_thru.dtype), pltpu.HBM(land_thru.shape, land_thru.dtype)),
        in_specs=(_HBM, _HBM, _SEMS, _SEMS, _ANY), out_specs=(_HBM, _HBM), input_output_aliases={0: 0, 1: 1},
        compiler_params=pltpu.CompilerParams(has_side_effects=_DATAFLOW),
    )(parts_thru, land_thru, send_sems, recv_sems, after)[1]


def _allreduce_small(vec):
    rows, width = vec.shape
    vmem = pl.BlockSpec(memory_space=pltpu.VMEM)

    def body(x_ref, o_ref, buf_ref, send_sems, recv_sems):
        x, y, c = _place()
        me = 4 * x + 2 * y + c
        buf_ref[me] = x_ref[...]
        copies = []
        for k in range(1, N_DEV):
            peer = (x ^ ((k >> 2) & 1), y ^ ((k >> 1) & 1), c ^ (k & 1))
            copies.append(_remote(x_ref, buf_ref.at[me], send_sems, recv_sems, k - 1, peer))
        for cp in copies:
            cp.start()
        for k in range(1, N_DEV):
            _remote(x_ref, buf_ref.at[me ^ k], send_sems, recv_sems, k - 1, (x, y, c)).wait_recv()
        for cp in copies:
            cp.wait_send()
        total = buf_ref[0]
        for d in range(1, N_DEV):
            total = total + buf_ref[d]
        o_ref[...] = total

    return pl.pallas_call(
        body, name="allreduce_small", in_specs=[vmem], out_specs=vmem, out_shape=_sds((rows, width)),
        scratch_shapes=[pltpu.VMEM((N_DEV, rows, width), F32), pltpu.SemaphoreType.DMA((N_DEV - 1,)),
                        pltpu.SemaphoreType.DMA((N_DEV - 1,))],
    )(vec)


def _rows1024(a):
    return a.reshape(-1, 1024)


def _pad_rows(a, rows):
    return jnp.concatenate([a, jnp.zeros((rows - a.shape[0], a.shape[1]), a.dtype)], axis=0)


_TRANSPOSED = ("w_in", "w_gate", "w_up")
_SMALL_SHARDED = ("w_up_f", "w_up_b", "a_up_f", "a_up_b", "g_up")
_BIG_SHARDED = ("w_in", "w_out", "w_gate", "w_up", "w_down")


def _pack_weight_shards(w):
    conv_bits = lax.bitcast_convert_type(w["conv_w"], BF16).reshape(1, -1)
    conv_row = jnp.concatenate([conv_bits, jnp.zeros((1, 1024 - conv_bits.shape[1]), BF16)], axis=1)

    def rows(name):
        a = w[name].astype(BF16)
        return a.T if name in _TRANSPOSED else _rows1024(a)

    early = _pad_rows(jnp.concatenate([rows(name) for name, _ in _EARLY_ROWS[:-1]] + [conv_row], axis=0), EARLY_R)
    return early, jnp.concatenate([rows(name) for name, _ in _FFN_ROWS], axis=0)


def _split_rows(gathered, layout):
    out, row = {}, 0
    for name, n in layout:
        out[name] = gathered[:, row:row + n]
        row += n
    return out


def _unpack_early(gathered):
    out = _split_rows(gathered, _EARLY_ROWS)
    cols = lambda a, k: jnp.concatenate([a[s].reshape(k, -1) for s in range(N_SHARD)], axis=1)
    conv = lax.bitcast_convert_type(out["conv_w"][:, 0, :768].reshape(N_SHARD, 3, 128, 2), F32)
    full = dict(w_in=out["w_in"].reshape(-1, 1024).T, w_out=out["w_out"].reshape(D_MODEL, D_MODEL),
                conv_w=jnp.concatenate([conv[s] for s in range(N_SHARD)], axis=1))
    full.update({name: cols(out[name], D_GATE if name == "g_up" else D_LORA) for name in _SMALL_SHARDED})
    return full


def _unpack_ffn(gathered):
    out = _split_rows(gathered, _FFN_ROWS)
    return dict(w_gate=out["w_gate"].reshape(-1, 1024).T, w_up=out["w_up"].reshape(-1, 1024).T,
                w_down=out["w_down"].reshape(D_FF, D_MODEL))


def _pack_grads(g, layout, rows):
    col_split = lambda a, s: a[:, s * (a.shape[1] // N_SHARD):(s + 1) * (a.shape[1] // N_SHARD)]
    row_split = lambda a, s: a[s * (a.shape[0] // N_SHARD):(s + 1) * (a.shape[0] // N_SHARD)]
    by_rows = {name: (g[name].T if name in _TRANSPOSED else g[name]) for name, _ in layout if name in _BIG_SHARDED}
    used = sum(n for _, n in layout)
    parts = []
    for s in range(N_SHARD):
        for name, _ in layout:
            if name in _BIG_SHARDED:
                parts.append(row_split(by_rows[name], s))
            elif name in _SMALL_SHARDED:
                parts.append(_rows1024(col_split(g[name], s)))
            else:
                conv = col_split(g["conv_w"], s).reshape(1, -1)
                parts.append(jnp.concatenate([conv, jnp.zeros((1, 1024 - conv.shape[1]), F32)], axis=1))
        if rows > used:
            parts.append(jnp.zeros((rows - used, 1024), F32))
    return jnp.concatenate(parts, axis=0).reshape(N_SHARD, rows, 1024)


def _unpack_grad_shard(pack, layout):
    small_shapes = {name: (D_GATE if name == "g_up" else D_LORA, 128) for name in _SMALL_SHARDED}
    out, row = {}, 0
    for name, n in layout:
        piece = pack[row:row + n]
        if name == "conv_w":
            out[name] = piece[0, :384].reshape(3, 128)
        else:
            out[name] = piece.T if name in _TRANSPOSED else piece.reshape(small_shapes.get(name, piece.shape))
        row += n
    return out


_SMALL_LAYOUT = (("norm1_w", 1024), ("mu_shift", D_SHIFT), ("w0_f", 512), ("w0_b", 512), ("a0_f", 512),
                 ("a0_b", 512), ("k_k", 512), ("k_a_f", 512), ("k_a_b", 512), ("r_k_f", 512), ("r_k_b", 512),
                 ("gn_w", 512), ("gn_b", 512), ("norm2_w", 1024), ("norm_f_w", 1024), ("loss", 1))


def _pack_small(vals):
    rows = []
    for name, n in _SMALL_LAYOUT:
        flat = vals[name].reshape(-1)
        n_rows = -(-n // 1024)
        rows.append(jnp.concatenate([flat, jnp.zeros((n_rows * 1024 - n,), F32)]).reshape(n_rows, 1024))
    return _pad_rows(jnp.concatenate(rows, axis=0), SMALL_ROWS)


def _unpack_small(pack):
    out, row = {}, 0
    for name, n in _SMALL_LAYOUT:
        n_rows = -(-n // 1024)
        out[name] = pack[row:row + n_rows].reshape(-1)[:n]
        row += n_rows
    return out


_WEIGHTS = ("norm1_w", "w_in", "mu_shift", "w_up_f", "w0_f", "w_up_b", "w0_b", "a_up_f", "a0_f", "a_up_b", "a0_b",
            "g_up", "k_k", "k_a_f", "k_a_b", "r_k_f", "r_k_b", "gn_w", "gn_b", "conv_w", "w_out", "norm2_w",
            "w_gate", "w_up", "w_down", "norm_f_w")


def _train_step(x, loss_target, w, m, v):
    batch, seq, _ = x.shape
    t = batch * seq
    tm = _tile(seq, 256, 8)
    xs = x.reshape(t, D_MODEL)
    target = loss_target.reshape(t, D_MODEL)
    vec = lambda name: w[name].reshape(1, -1)

    local = {name: w[name][0] for name, _ in _PACK_ROWS}
    c = lax.axis_index("c")
    chip = 2 * lax.axis_index("x") + lax.axis_index("y")
    early, ffn_pack = _pack_weight_shards(local)
    early_all = lax.dynamic_update_slice(_gather_weights(early), early[None], (chip, 0, 0))
    ffn_send, ffn_recv, ffn_pack, ffn_land, token = _fetch_start(ffn_pack, early_all)
    full = _unpack_early(early_all)
    w_in = full["w_in"]
    w_shift = jnp.concatenate([w_in[:, :D_SHIFT], jnp.zeros((D_MODEL, D_SHIFT_PAD - D_SHIFT), BF16)], axis=1)
    w_conv = w_in[:, D_SHIFT:]
    zeros_lora = jnp.zeros((D_LORA, D_RWKV), F32)
    lora = lambda name: full[name].astype(F32)
    mats = (jnp.concatenate([lora("w_up_f"), zeros_lora]), jnp.concatenate([zeros_lora, lora("a_up_f")]),
            jnp.concatenate([lora("w_up_b"), zeros_lora]), jnp.concatenate([zeros_lora, lora("a_up_b")]),
            jnp.concatenate([lora("g_up"), jnp.zeros((D_GATE_PAD - D_GATE, D_RWKV), F32)]))
    mu = jnp.concatenate([vec("mu_shift"), jnp.zeros((1, D_SHIFT_PAD - D_SHIFT), F32)], axis=1)
    mu = jnp.broadcast_to(mu, (GROUP, D_SHIFT_PAD))
    zero_row = jnp.zeros((1, D_RWKV), F32)
    pvec = jnp.concatenate([vec("k_k"), vec("w0_f"), vec("a0_f"), vec("k_a_f"), vec("w0_b"), vec("a0_b"),
                            vec("k_a_b"), zero_row], axis=0)
    qvec = jnp.concatenate([vec("gn_w"), vec("gn_b"), vec("r_k_f"), vec("r_k_b"), full["conv_w"], zero_row], axis=0)
    ones_blocks = _head_ones()

    h1_t, p_shift, pc = _norm_in_proj(xs, vec("norm1_w") + token[0, 0], w_shift, w_conv, tm)
    ps, kk, w_f, kd_f, b_f, w_b, kd_b, b_b, gate = _shift_prep_fwd(p_shift, mu, pvec, mats, ones_blocks, tm, seq)
    dirs = ((w_f, kd_f, b_f), (w_b, kd_b, b_b))
    y_f, y_b, hist_f, hist_b = _scan_fwd(ps, kk, dirs, batch, seq)
    mixed, mixed_t = _post_fwd(y_f, y_b, ps, kd_f, kd_b, gate, pc, qvec, ones_blocks, tm, seq)
    x1, h2, h2_t = _out_proj_norm(mixed, full["w_out"], xs, vec("norm2_w"), tm)
    ffn_all = _fetch_wait(ffn_send, ffn_recv, ffn_pack, ffn_land, h2)
    full.update(_unpack_ffn(lax.dynamic_update_slice(ffn_all, ffn_pack[None], (chip, 0, 0))))
    ff_gate, ff_up, act, act_t = _ffn_in(h2, full["w_gate"], full["w_up"])
    d_x2, d_x2_bf16, d_norm_f, loss_part = _ffn_out_loss(act, full["w_down"], x1, w["norm_f_w"].reshape(1, -1),
                                                         target, tm)

    g = {}
    g["w_down"] = _matmul(act_t, d_x2_bf16, mode="nn", name="ffn_down_dw")
    d_gate, d_up = _ffn_in_bwd(d_x2_bf16, full["w_down"], ff_gate, ff_up)
    g["w_gate"] = _matmul(h2_t, d_gate, mode="nn", name="ffn_gate_dw")
    g["w_up"] = _matmul(h2_t, d_up, mode="nn", name="ffn_up_dw")
    ffn_grads = _pack_grads(g, _FFN_ROWS, sum(n for _, n in _FFN_ROWS))
    ffn_sum, ffn_sum_bf16 = _add_halves(ffn_grads, _swap_other_half(ffn_grads, "ffn"), c, "ffn")
    ex_send, ex_recv, ffn_sum_bf16, ex_land, ex_token = _exchange_start(ffn_sum_bf16, "ffn")
    d_x1, d_norm2 = _proj_norm_bwd(d_gate, full["w_gate"], d_up, full["w_up"], x1, vec("norm2_w") + ex_token[0, 0],
                                   d_x2, tm, "ffn_in_dx_norm2_bwd")
    d_mixed = _matmul(d_x1, full["w_out"], mode="nt", name="out_proj_dx")
    g["w_out"] = _matmul(mixed_t, d_x1, mode="nn", name="out_proj_dw")
    dy, dr_o, dkdf_o, dkdb_o, dv_o, d_gatev, d_pc, d_qvec = _post_bwd(
        d_mixed, y_f, y_b, ps, kd_f, kd_b, gate, pc, qvec, ones_blocks, tm, seq)
    (dr_f, dw_f, dkd_f, dv_f, dkk_f, db_f), (dr_b, dw_b, dkd_b, dv_b, dkk_b, db_b) = _scan_bwd(
        ps, kk, dirs, dy, hist_f, hist_b, batch, seq)
    cts = [[dr_f, dr_b, dr_o], [dv_f, dv_b, dv_o], [dkk_f, dkk_b], [dw_f], [dkd_f, dkdf_o], [db_f],
           [dw_b], [dkd_b, dkdb_o], [db_b], [d_gatev]]
    q, d_pvec, d_m0, d_m1, d_m2, d_m3, d_m4 = _prep_bwd(ps, pvec, mats, ones_blocks, cts, tm)
    d_pshift, d_mu = _shift_bwd(q, p_shift, mu, tm, seq)
    d_w_shift = _matmul(h1_t, d_pshift, mode="nn", name="in_proj_shift_dw")
    d_w_conv = _matmul(h1_t, d_pc, mode="nn", name="in_proj_conv_dw")
    g["w_in"] = jnp.concatenate([d_w_shift[:, :D_SHIFT], d_w_conv], axis=1)
    d_x, d_norm1 = _proj_norm_bwd(d_pshift, w_shift, d_pc, w_conv, xs, vec("norm1_w"), d_x1, tm,
                                  "in_proj_dx_norm1_bwd")
    g["w_up_f"], g["a_up_f"] = d_m0[:D_LORA], d_m1[D_LORA:]
    g["w_up_b"], g["a_up_b"] = d_m2[:D_LORA], d_m3[D_LORA:]
    g["g_up"] = d_m4[:D_GATE]
    g["conv_w"] = d_qvec[4:7]

    def finish(chip_sum, others, tag, layout):
        eighth = _add_quarters(chip_sum, others, chip, tag)
        other_eighth = _swap_with_sibling(eighth, "swap_eighths_" + tag)
        return _unpack_grad_shard(jnp.concatenate([jnp.where(c == 0, eighth, other_eighth),
                                                   jnp.where(c == 0, other_eighth, eighth)], axis=0), layout)

    as2d = lambda name: (1, w[name].shape[0]) if w[name].ndim == 1 else w[name].shape
    operands = lambda name: tuple(a.reshape(as2d(name)) for a in (w[name], grads[name], m[name], v[name]))

    packed = _pack_grads(g, _EARLY_ROWS, EARLY_R)
    mix_sum, mix_sum_bf16 = _add_halves(packed, _swap_other_half(packed, "mixer"), c, "mixer")
    mx_send, mx_recv, mix_sum_bf16, mx_land, mx_token = _exchange_start(mix_sum_bf16, "mixer")
    grads = finish(ffn_sum, _exchange_wait(ex_send, ex_recv, ffn_sum_bf16, ex_land, mx_token, "ffn"), "ffn", _FFN_ROWS)
    updates = {name: _adamw(*operands(name), "adamw_" + name) for name in _FFN_NAMES}
    mix_others = _exchange_wait(mx_send, mx_recv, mix_sum_bf16, mx_land, updates["w_down"][2], "mixer")
    grads.update(finish(mix_sum, mix_others, "mixer", _EARLY_ROWS))

    small = dict(norm1_w=d_norm1, mu_shift=d_mu[:, :D_SHIFT], w0_f=d_pvec[1], w0_b=d_pvec[4], a0_f=d_pvec[2],
                 a0_b=d_pvec[5], k_k=d_pvec[0], k_a_f=d_pvec[3], k_a_b=d_pvec[6], r_k_f=d_qvec[2], r_k_b=d_qvec[3],
                 gn_w=d_qvec[0], gn_b=d_qvec[1], norm2_w=d_norm2, norm_f_w=d_norm_f, loss=loss_part)
    reduced = _unpack_small(_allreduce_small(_pack_small(small)))
    loss = reduced.pop("loss")[0]
    grads.update(reduced)

    outs = {}
    small = [name for name in _WEIGHTS if name not in _BIG_SHARDED]
    updates.update(zip(small, _adamw_small([operands(name) for name in small])))
    for name in ("w_in", "w_out"):
        updates[name] = _adamw(*operands(name), "adamw_" + name)
    for name in _WEIGHTS:
        shape = w[name].shape
        outs[name] = (grads[name].reshape(shape),) + tuple(a.reshape(shape) for a in updates[name])
    d_x = d_x.reshape(batch, seq, D_MODEL)
    return (loss, d_x) + tuple(outs[name][k] for k in range(4) for name in _WEIGHTS)


def kernel(x, norm1_w, w_in, mu_shift, w_up_f, w0_f, w_up_b, w0_b, a_up_f, a0_f, a_up_b, a0_b, g_up, k_k, k_a_f, k_a_b, r_k_f, r_k_b, gn_w, gn_b, conv_w, w_out, norm2_w, w_gate, w_up, w_down, norm_f_w, loss_target, m_norm1_w, m_w_in, m_mu_shift, m_w_up_f, m_w0_f, m_w_up_b, m_w0_b, m_a_up_f, m_a0_f, m_a_up_b, m_a0_b, m_g_up, m_k_k, m_k_a_f, m_k_a_b, m_r_k_f, m_r_k_b, m_gn_w, m_gn_b, m_conv_w, m_w_out, m_norm2_w, m_w_gate, m_w_up, m_w_down, m_norm_f_w, v_norm1_w, v_w_in, v_mu_shift, v_w_up_f, v_w0_f, v_w_up_b, v_w0_b, v_a_up_f, v_a0_f, v_a_up_b, v_a0_b, v_g_up, v_k_k, v_k_a_f, v_k_a_b, v_r_k_f, v_r_k_b, v_gn_w, v_gn_b, v_conv_w, v_w_out, v_norm2_w, v_w_gate, v_w_up, v_w_down, v_norm_f_w):
    args = locals()
    w = {name: args[name] for name in _WEIGHTS}
    m = {name: args["m_" + name] for name in _WEIGHTS}
    v = {name: args["v_" + name] for name in _WEIGHTS}
    return _train_step(x, loss_target, w, m, v)
```

```python
import functools

import jax
import jax.numpy as jnp
from jax import lax
from jax.experimental import pallas as pl
from jax.experimental.pallas import tpu as pltpu

F32 = jnp.float32
BF16 = jnp.bfloat16
MESH = pl.DeviceIdType.MESH

D_MODEL = 1024
D_RWKV = 512
HEAD = 64
N_PAIR = D_RWKV // (2 * HEAD)
D_LORA = 64
D_GATE = 160
D_GATE_PAD = 384
D_FF = 2816
D_SHIFT = 1824
D_SHIFT_PAD = 2048
D_CONV3 = 1536
LOG_DECAY_SCALE = 0.606531
RMS_EPS = 1e-6
GN_EPS = 64e-5
NORM_EPS = 1e-12
ADAM_LR, ADAM_B1, ADAM_B2, ADAM_EPS, ADAM_WD, ADAM_STEP = 0.001, 0.9, 0.999, 1e-08, 0.01, 10

N_SHARD = 4
N_DEV = 8
V7X_VMEM_LIMIT = 48 * 1024 * 1024
SCAN_CHUNK = 64
GROUP = 8

_PACK_ROWS = (("w_in", 840), ("w_out", 256), ("w_gate", 704), ("w_up", 704), ("w_down", 704),
              ("w_up_f", 8), ("w_up_b", 8), ("a_up_f", 8), ("a_up_b", 8), ("g_up", 20), ("conv_w", 1))
_FFN_NAMES = ("w_gate", "w_up", "w_down")
_EARLY_ROWS = tuple(item for item in _PACK_ROWS if item[0] not in _FFN_NAMES)
_FFN_ROWS = tuple(item for item in _PACK_ROWS if item[0] in _FFN_NAMES)
EARLY_R = 1152
SMALL_ROWS = 24


def _tile(n, cap, mult=128):
    best = None
    t = mult
    while t <= min(n, cap):
        if n % t == 0:
            best = t
        t += mult
    return best or n


def _cp(*sem):
    return pltpu.CompilerParams(dimension_semantics=sem or None, vmem_limit_bytes=V7X_VMEM_LIMIT)


def _sds(shape, dtype=F32):
    return jax.ShapeDtypeStruct(shape, dtype)


def _matmul(a, b, *, mode, name, out_dtype=F32, add=None):
    m, kdim = a.shape
    n = b.shape[1] if mode == "nn" else b.shape[0]
    tm, tn = _tile(m, 768, 8), _tile(n, 1536)
    tk = kdim if kdim <= 3584 else _tile(kdim, 2048)
    nk = kdim // tk
    a_spec = pl.BlockSpec((tm, tk), lambda i, j, k: (i, k))
    if mode == "nn":
        b_spec = pl.BlockSpec((tk, tn), lambda i, j, k: (k, j))
        dims = (((1,), (0,)), ((), ()))
    else:
        b_spec = pl.BlockSpec((tn, tk), lambda i, j, k: (j, k))
        dims = (((1,), (1,)), ((), ()))
    has_add = add is not None

    def body(*refs):
        a_ref, b_ref = refs[0], refs[1]
        add_ref = refs[2] if has_add else None
        o_ref = refs[3] if has_add else refs[2]
        part = lax.dot_general(a_ref[...].astype(BF16), b_ref[...].astype(BF16), dims,
                               preferred_element_type=F32)
        if nk == 1:
            if has_add:
                part = part + add_ref[...]
            o_ref[...] = part.astype(out_dtype)
        else:
            acc_ref = refs[-1]
            k = pl.program_id(2)

            @pl.when(k == 0)
            def _():
                acc_ref[...] = jnp.zeros_like(acc_ref)

            acc_ref[...] += part

            @pl.when(k == nk - 1)
            def _():
                res = acc_ref[...]
                if has_add:
                    res = res + add_ref[...]
                o_ref[...] = res.astype(out_dtype)

    o_spec = pl.BlockSpec((tm, tn), lambda i, j, k: (i, j))
    in_specs = [a_spec, b_spec] + ([o_spec] if has_add else [])
    args = (a, b) + ((add,) if has_add else ())
    return pl.pallas_call(
        body, name=name, grid=(m // tm, n // tn, nk), in_specs=in_specs, out_specs=o_spec,
        out_shape=_sds((m, n), out_dtype),
        scratch_shapes=[pltpu.VMEM((tm, tn), F32)] if nk > 1 else [],
        compiler_params=_cp("parallel", "parallel", "arbitrary"),
    )(*args)


def _row(tm, width):
    return pl.BlockSpec((tm, width), lambda i: (i, 0))


def _col(tm, height):
    return pl.BlockSpec((height, tm), lambda i: (0, i))


def _fixed(shape):
    return pl.BlockSpec(shape, lambda i: tuple(0 for _ in shape))


def _rmsnorm_tile(xv, wv):
    return xv * lax.rsqrt(jnp.mean(xv * xv, axis=-1, keepdims=True) + RMS_EPS) * wv


def _norm_in_proj(x, w, w_shift, w_conv, tm):
    t, d = x.shape
    n_a, n_b = w_shift.shape[1], w_conv.shape[1]

    def body(x_ref, w_ref, wa_ref, wb_ref, ht_ref, pa_ref, pb_ref):
        hv = _rmsnorm_tile(x_ref[...], w_ref[...])
        ht_ref[...] = jnp.transpose(hv).astype(BF16)
        hb = hv.astype(BF16)
        pa_ref[...] = jnp.dot(hb, wa_ref[...], preferred_element_type=F32)
        pb_ref[...] = jnp.dot(hb, wb_ref[...], preferred_element_type=F32)

    return pl.pallas_call(
        body, name="norm1_in_proj", grid=(t // tm,),
        in_specs=[_row(tm, d), _fixed((1, d)), _fixed((d, n_a)), _fixed((d, n_b))],
        out_specs=[_col(tm, d), _row(tm, n_a), _row(tm, n_b)],
        out_shape=[_sds((d, t), BF16), _sds((t, n_a)), _sds((t, n_b))],
        compiler_params=_cp("parallel"))(x, w, w_shift, w_conv)


def _out_proj_norm(mixed, w_out, res, w, tm):
    t, d = res.shape

    def body(m_ref, wo_ref, r_ref, w_ref, x_ref, h_ref, ht_ref):
        xv = r_ref[...] + jnp.dot(m_ref[...], wo_ref[...], preferred_element_type=F32)
        x_ref[...] = xv
        hv = _rmsnorm_tile(xv, w_ref[...])
        h_ref[...] = hv.astype(BF16)
        ht_ref[...] = jnp.transpose(hv).astype(BF16)

    return pl.pallas_call(
        body, name="out_proj_norm2", grid=(t // tm,),
        in_specs=[_row(tm, d), _fixed((d, d)), _row(tm, d), _fixed((1, d))],
        out_specs=[_row(tm, d), _row(tm, d), _col(tm, d)],
        out_shape=[_sds((t, d)), _sds((t, d), BF16), _sds((d, t), BF16)],
        compiler_params=_cp("parallel"))(mixed, w_out, res, w)


def _rms_bwd_math(xv, wv, dyv):
    rstd = lax.rsqrt(jnp.mean(xv * xv, axis=-1, keepdims=True) + RMS_EPS)
    xhat = xv * rstd
    gv = dyv * wv
    dx = rstd * (gv - xhat * jnp.mean(gv * xhat, axis=-1, keepdims=True))
    return dx, jnp.sum(dyv * xhat, axis=0, keepdims=True)


def _proj_norm_bwd(dy_a, w_a, dy_b, w_b, x, w, dres, tm, name):
    t, d = x.shape
    ka, kb = dy_a.shape[1], dy_b.shape[1]
    nt = (((1,), (1,)), ((), ()))

    def body(dya_ref, wa_ref, dyb_ref, wb_ref, x_ref, w_ref, dres_ref, dx_ref, dw_ref):
        d_h = (lax.dot_general(dya_ref[...], wa_ref[...], nt, preferred_element_type=F32)
               + lax.dot_general(dyb_ref[...], wb_ref[...], nt, preferred_element_type=F32))
        dx, dw = _rms_bwd_math(x_ref[...], w_ref[...], d_h)
        dx_ref[...] = dres_ref[...] + dx

        @pl.when(pl.program_id(0) == 0)
        def _():
            dw_ref[...] = jnp.zeros_like(dw_ref)

        dw_ref[...] += dw

    return pl.pallas_call(
        body, name=name, grid=(t // tm,),
        in_specs=[_row(tm, ka), _fixed((d, ka)), _row(tm, kb), _fixed((d, kb)), _row(tm, d), _fixed((1, d)),
                  _row(tm, d)],
        out_specs=[_row(tm, d), _fixed((1, d))],
        out_shape=[_sds((t, d)), _sds((1, d))], compiler_params=_cp("arbitrary"))(dy_a, w_a, dy_b, w_b, x, w, dres)


def _ffn_out_loss(act, w_down, x1, w, target, tm):
    t, d = x1.shape
    f = act.shape[1]

    def body(a_ref, wd_ref, x_ref, w_ref, t_ref, dx_ref, dx16_ref, dw_ref, loss_ref):
        xv = x_ref[...] + jnp.dot(a_ref[...], wd_ref[...], preferred_element_type=F32)
        wv = w_ref[...]
        rstd = lax.rsqrt(jnp.mean(xv * xv, axis=-1, keepdims=True) + RMS_EPS)
        err = xv * rstd * wv - t_ref[...]
        dx, dw = _rms_bwd_math(xv, wv, err * (1.0 / d))
        dx_ref[...] = dx
        dx16_ref[...] = dx.astype(BF16)

        @pl.when(pl.program_id(0) == 0)
        def _():
            dw_ref[...] = jnp.zeros_like(dw_ref)
            loss_ref[...] = jnp.zeros_like(loss_ref)

        dw_ref[...] += dw
        loss_ref[...] += 0.5 * jnp.sum(jnp.mean(err * err, axis=-1, keepdims=True), axis=0, keepdims=True)

    return pl.pallas_call(
        body, name="ffn_out_loss", grid=(t // tm,),
        in_specs=[_row(tm, f), _fixed((f, d)), _row(tm, d), _fixed((1, d)), _row(tm, d)],
        out_specs=[_row(tm, d), _row(tm, d), _fixed((1, d)), _fixed((1, 1))],
        out_shape=[_sds((t, d)), _sds((t, d), BF16), _sds((1, d)), _sds((1, 1))],
        compiler_params=_cp("arbitrary"))(act, w_down, x1, w, target)


def _ffn_in(h, w_gate, w_up):
    t, d = h.shape
    f = w_gate.shape[1]
    tm, tn = _tile(t, 512, 8), _tile(f, 1536)

    def body(h_ref, wg_ref, wu_ref, g_ref, u_ref, a_ref, at_ref):
        hv = h_ref[...]
        gv = jnp.dot(hv, wg_ref[...], preferred_element_type=F32)
        uv = jnp.dot(hv, wu_ref[...], preferred_element_type=F32)
        act = gv * jax.nn.sigmoid(gv) * uv
        g_ref[...] = gv.astype(BF16)
        u_ref[...] = uv.astype(BF16)
        a_ref[...] = act.astype(BF16)
        at_ref[...] = jnp.transpose(act).astype(BF16)

    w_spec = pl.BlockSpec((d, tn), lambda i, j: (0, j))
    o_spec = pl.BlockSpec((tm, tn), lambda i, j: (i, j))
    return pl.pallas_call(
        body, name="ffn_in", grid=(t // tm, f // tn),
        in_specs=[pl.BlockSpec((tm, d), lambda i, j: (i, 0)), w_spec, w_spec],
        out_specs=[o_spec, o_spec, o_spec, pl.BlockSpec((tn, tm), lambda i, j: (j, i))],
        out_shape=[_sds((t, f), BF16)] * 3 + [_sds((f, t), BF16)],
        compiler_params=_cp("parallel", "parallel"))(h, w_gate, w_up)


def _ffn_in_bwd(d_out, w_down, gate, up):
    t, d = d_out.shape
    f = w_down.shape[0]
    tm, tn = _tile(t, 512, 8), _tile(f, 1536)

    def body(do_ref, w_ref, g_ref, u_ref, dg_ref, du_ref):
        dv = lax.dot_general(do_ref[...].astype(BF16), w_ref[...], (((1,), (1,)), ((), ())),
                             preferred_element_type=F32)
        gv, uv = g_ref[...].astype(F32), u_ref[...].astype(F32)
        sg = jax.nn.sigmoid(gv)
        du_ref[...] = (dv * gv * sg).astype(BF16)
        dg_ref[...] = (dv * uv * (sg * (1.0 + gv * (1.0 - sg)))).astype(BF16)

    tile = pl.BlockSpec((tm, tn), lambda i, j: (i, j))
    return pl.pallas_call(
        body, name="ffn_in_bwd", grid=(t // tm, f // tn),
        in_specs=[pl.BlockSpec((tm, d), lambda i, j: (i, 0)), pl.BlockSpec((tn, d), lambda i, j: (j, 0)), tile, tile],
        out_specs=[tile, tile], out_shape=[_sds((t, f), BF16)] * 2,
        compiler_params=_cp("parallel", "parallel"))(d_out, w_down, gate, up)


def _halo_specs(tm, width, rows_total):
    per = tm // GROUP
    last = rows_total // GROUP - 1
    prev = pl.BlockSpec((GROUP, width), lambda i: (jnp.maximum(i * per - 1, 0), 0))
    nxt = pl.BlockSpec((GROUP, width), lambda i: (jnp.minimum((i + 1) * per, last), 0))
    return prev, nxt


def _edge_flags(tm, seq):
    i = pl.program_id(0)
    has_prev = jnp.where((i * tm) % seq == 0, 0.0, 1.0).astype(F32)
    has_next = jnp.where(((i + 1) * tm) % seq == 0, 0.0, 1.0).astype(F32)
    return has_prev, has_next


def _shifted(xv, prev_row, next_row):
    tm = xv.shape[0]
    row = lax.broadcasted_iota(jnp.int32, xv.shape, 0)
    down = jnp.where(row == 0, prev_row, pltpu.roll(xv, 1, axis=0))
    up = jnp.where(row == tm - 1, next_row, pltpu.roll(xv, tm - 1, axis=0))
    return down, up


def _shift_bwd(q, p, mu, tm, seq):
    t, w = p.shape
    prev_spec, next_spec = _halo_specs(tm, w, t)

    def body(q_ref, qp_ref, qn_ref, p_ref, pp_ref, pn_ref, mu_ref, dp_ref, dmu_ref):
        has_prev, has_next = _edge_flags(tm, seq)
        muv = mu_ref[0:1, :]
        qv = q_ref[...]
        mq = muv * qv
        mq_down, mq_up = _shifted(mq, muv * qp_ref[GROUP - 1:GROUP, :] * has_prev,
                                  muv * qn_ref[0:1, :] * has_next)
        dp_ref[...] = (qv - mq + 0.5 * (mq_down + mq_up)).astype(BF16)
        pv = p_ref[...]
        p_down, p_up = _shifted(pv, pp_ref[GROUP - 1:GROUP, :] * has_prev, pn_ref[0:1, :] * has_next)

        @pl.when(pl.program_id(0) == 0)
        def _():
            dmu_ref[...] = jnp.zeros_like(dmu_ref)

        dmu_ref[...] += jnp.sum(qv * (0.5 * (p_down + p_up) - pv), axis=0, keepdims=True)

    return pl.pallas_call(
        body, name="shift_bwd", grid=(t // tm,),
        in_specs=[_row(tm, w), prev_spec, next_spec, _row(tm, w), prev_spec, next_spec, _fixed((GROUP, w))],
        out_specs=[_row(tm, w), _fixed((1, w))],
        out_shape=[_sds((t, w), BF16), _sds((1, w))], compiler_params=_cp("arbitrary"))(q, q, q, p, p, p, mu)


@jax.custom_vjp
def _bdot(a, b):
    return jnp.dot(a.astype(BF16), b.astype(BF16), preferred_element_type=F32)


def _bdot_fwd(a, b):
    return _bdot(a, b), (a, b)


def _bdot_bwd(res, g):
    a, b = res
    gb = g.astype(BF16)
    da = lax.dot_general(gb, b.astype(BF16), (((1,), (1,)), ((), ())), preferred_element_type=F32)
    db = lax.dot_general(a.astype(BF16), gb, (((0,), (0,)), ((), ())), preferred_element_type=F32)
    return da, db


_bdot.defvjp(_bdot_fwd, _bdot_bwd)


def _seg_raw(x, ones_blocks):
    hi = x.astype(BF16)
    lo = (x - hi.astype(F32)).astype(BF16)
    return (jnp.dot(hi, ones_blocks, preferred_element_type=F32)
            + jnp.dot(lo, ones_blocks, preferred_element_type=F32))


@jax.custom_vjp
def _seg(x, ones_blocks):
    return _seg_raw(x, ones_blocks)


def _seg_fwd(x, ones_blocks):
    return _seg_raw(x, ones_blocks), ones_blocks


def _seg_bwd(ones_blocks, g):
    return _seg_raw(g, ones_blocks), jnp.zeros_like(ones_blocks)


_seg.defvjp(_seg_fwd, _seg_bwd)


def _head_ones():
    h = jnp.arange(D_RWKV) // HEAD
    return (h[:, None] == h[None, :]).astype(BF16)


def _prep_math(ps, k_k, w0_f, a0_f, k_a_f, w0_b, a0_b, k_a_b, wup_f, aup_f, wup_b, aup_b, gup, ones_blocks):
    r = ps[:, 0:512]
    k = ps[:, 512:1024]
    v = ps[:, 1024:1536]
    xwa = ps[:, 1536:1664]
    xg = ps[:, 1664:D_SHIFT_PAD]
    kk_raw = k * k_k
    norm = jnp.sqrt(_seg(kk_raw * kk_raw, ones_blocks))
    kk = kk_raw / jnp.maximum(norm, NORM_EPS)
    t_xwa = jnp.tanh(xwa)
    outs = [r, v, kk]
    for w0, a0, k_a, wup, aup in ((w0_f, a0_f, k_a_f, wup_f, aup_f), (w0_b, a0_b, k_a_b, wup_b, aup_b)):
        decay = jnp.exp(-LOG_DECAY_SCALE * jax.nn.sigmoid(w0 + _bdot(t_xwa, wup)))
        rate = jax.nn.sigmoid(a0 + _bdot(xwa, aup))
        outs += [decay, k * (1.0 + (rate - 1.0) * k_a), kk * rate]
    outs.append(_bdot(jax.nn.sigmoid(xg), gup))
    return tuple(outs)


def _prep_args(tm, ps_tile, pv_ref, mat_refs, ones_ref):
    vecs = [jnp.broadcast_to(pv_ref[j:j + 1, :], (tm, D_RWKV)) for j in range(7)]
    return [ps_tile] + vecs + [m[...] for m in mat_refs] + [ones_ref[...]]


_PREP_MAT_SHAPES = ((128, D_RWKV),) * 4 + ((D_GATE_PAD, D_RWKV),)


def _shift_prep_fwd(p, mu, pvec, mats, ones_blocks, tm, seq):
    t, w = p.shape
    prev_spec, next_spec = _halo_specs(tm, w, t)

    def body(p_ref, hp_ref, hn_ref, mu_ref, pv_ref, m0, m1, m2, m3, m4, ones_ref, ps_ref, *out_refs):
        has_prev, has_next = _edge_flags(tm, seq)
        xv = p_ref[...]
        down, up = _shifted(xv, hp_ref[GROUP - 1:GROUP, :] * has_prev, hn_ref[0:1, :] * has_next)
        ps_tile = xv + mu_ref[0:1, :] * (0.5 * (down + up) - xv)
        ps_ref[...] = ps_tile
        outs = _prep_math(*_prep_args(tm, ps_tile, pv_ref, (m0, m1, m2, m3, m4), ones_ref))
        for o_ref, val in zip(out_refs, outs[2:]):
            o_ref[...] = val

    return pl.pallas_call(
        body, name="shift_prep_fwd", grid=(t // tm,),
        in_specs=[_row(tm, w), prev_spec, next_spec, _fixed((GROUP, w)), _fixed((8, D_RWKV))]
        + [_fixed(s) for s in _PREP_MAT_SHAPES] + [_fixed((D_RWKV, D_RWKV))],
        out_specs=[_row(tm, w)] + [_row(tm, D_RWKV)] * 8, out_shape=[_sds((t, w))] + [_sds((t, D_RWKV))] * 8,
        compiler_params=_cp("parallel"))(p, p, p, mu, pvec, *mats, ones_blocks)


def _prep_bwd(ps, pvec, mats, ones_blocks, cts, tm):
    t = ps.shape[0]
    counts = [len(c) for c in cts]
    flat = [a for c in cts for a in c]

    def body(ps_ref, pv_ref, m0, m1, m2, m3, m4, ones_ref, *refs):
        ct_refs = refs[:len(flat)]
        q_ref, dpv_ref = refs[len(flat)], refs[len(flat) + 1]
        dmat_refs = refs[len(flat) + 2:]
        args = _prep_args(tm, ps_ref[...], pv_ref, (m0, m1, m2, m3, m4), ones_ref)
        _, vjp = jax.vjp(lambda *a: _prep_math(*a, args[-1]), *args[:-1])
        ct_vals, pos = [], 0
        for n in counts:
            val = ct_refs[pos][...]
            for extra in ct_refs[pos + 1:pos + n]:
                val = val + extra[...]
            ct_vals.append(val)
            pos += n
        grads = vjp(tuple(ct_vals))
        q_ref[...] = grads[0]

        @pl.when(pl.program_id(0) == 0)
        def _():
            dpv_ref[...] = jnp.zeros_like(dpv_ref)
            for d_ref in dmat_refs:
                d_ref[...] = jnp.zeros_like(d_ref)

        for j in range(7):
            dpv_ref[j:j + 1, :] += jnp.sum(grads[1 + j], axis=0, keepdims=True)
        for d_ref, gm in zip(dmat_refs, grads[8:13]):
            d_ref[...] += gm

    return pl.pallas_call(
        body, name="prep_bwd", grid=(t // tm,),
        in_specs=[_row(tm, D_SHIFT_PAD), _fixed((8, D_RWKV))] + [_fixed(s) for s in _PREP_MAT_SHAPES]
        + [_fixed((D_RWKV, D_RWKV))] + [_row(tm, D_RWKV)] * len(flat),
        out_specs=[_row(tm, D_SHIFT_PAD), _fixed((8, D_RWKV))] + [_fixed(s) for s in _PREP_MAT_SHAPES],
        out_shape=[_sds((t, D_SHIFT_PAD)), _sds((8, D_RWKV))] + [_sds(s) for s in _PREP_MAT_SHAPES],
        compiler_params=_cp("arbitrary"))(ps, pvec, *mats, ones_blocks, *flat)


def _pair_ones():
    h = jnp.arange(2 * HEAD) // HEAD
    block = (h[:, None] == h[None, :]).astype(BF16)
    return jnp.concatenate([block, block], axis=0)


def _diag_mask():
    lane = lax.broadcasted_iota(jnp.int32, (HEAD, 2 * HEAD), 1)
    sub = lax.broadcasted_iota(jnp.int32, (HEAD, 2 * HEAD), 0)
    return jnp.where((lane & (HEAD - 1)) == sub, 1.0, 0.0).astype(F32)


def _to_row(cols, dmask):
    return jnp.sum(cols * dmask, axis=0, keepdims=True)


def _seg_many(exact, rounded, ones_pair):
    out_exact, out_rounded = [], []
    if exact:
        parts = []
        for x in exact:
            hi = x.astype(BF16)
            parts.append(jnp.concatenate([hi, (x - hi.astype(F32)).astype(BF16)], axis=1))
        res = jnp.dot(jnp.concatenate(parts, axis=0), ones_pair, preferred_element_type=F32)
        out_exact = [res[HEAD * c:HEAD * (c + 1)] for c in range(len(exact))]
    if rounded:
        res = jnp.dot(jnp.concatenate([x.astype(BF16) for x in rounded], axis=0), ones_pair[0:2 * HEAD],
                      preferred_element_type=F32)
        out_rounded = [res[HEAD * c:HEAD * (c + 1)] for c in range(len(rounded))]
    return out_exact, out_rounded


N_CHAIN = 2 * N_PAIR


def _chain(c):
    d, p = divmod(c, N_PAIR)
    return d, slice(2 * HEAD * p, 2 * HEAD * (p + 1))


def _scan_specs(n_chunks, col_blocks, fwd_chunk, bwd_chunk):
    def spec(chunk_of, col):
        return pl.BlockSpec((SCAN_CHUNK, D_RWKV), lambda b, g: (b * n_chunks + chunk_of(g), col))
    return [spec(fwd_chunk, c) for c in col_blocks] + [spec(bwd_chunk, c) for c in col_blocks]


def _scan_fwd(ps, kk, dirs, batch, seq):
    t = batch * seq
    n = seq // SCAN_CHUNK
    groups = SCAN_CHUNK // GROUP
    up = lambda g: g
    down = lambda g: n - 1 - g
    col_blocks = (0, 2, 0, 0, 0, 0)

    def body(*refs):
        dir_refs = (refs[0:6], refs[6:12])
        ones_ref = refs[12]
        y_refs, hist_refs, st_ref = refs[13:15], refs[15:17], refs[17]

        @pl.when(pl.program_id(1) == 0)
        def _():
            st_ref[...] = jnp.zeros_like(st_ref)

        ones_pair = ones_ref[...]
        dmask = _diag_mask()
        dmask_b = dmask.astype(BF16)
        sub8 = lax.broadcasted_iota(jnp.int32, (GROUP, 2 * HEAD), 0)

        def group(gi, carry):
            off = (pl.multiple_of(gi * GROUP, GROUP), pl.multiple_of((groups - 1 - gi) * GROUP, GROUP))
            loaded = [tuple(ref[pl.ds(off[d], GROUP), :] for ref in dir_refs[d]) for d in range(2)]
            states = list(carry)
            y_acc = [jnp.zeros((GROUP, 2 * HEAD), F32) for _ in range(N_CHAIN)]
            for step in range(GROUP):
                rows, idx = [], []
                for c in range(N_CHAIN):
                    d, lanes = _chain(c)
                    i = step if d == 0 else GROUP - 1 - step
                    idx.append(i)
                    rows.append(tuple(x8[i:i + 1, lanes] for x8 in loaded[d]))
                    hist_refs[d][c % N_PAIR, gi * GROUP + step] = states[c]
                _, v_cols = _seg_many([], [dmask_b * rows[c][1].astype(BF16) for c in range(N_CHAIN)], ones_pair)
                sas, _ = _seg_many([states[c] * rows[c][2] for c in range(N_CHAIN)], [], ones_pair)
                for c in range(N_CHAIN):
                    _, _, _, w_row, kd_row, b_row = rows[c]
                    states[c] = states[c] * w_row - sas[c] * b_row + v_cols[c] * kd_row
                _, ys = _seg_many([], [states[c] * rows[c][0] for c in range(N_CHAIN)], ones_pair)
                for c in range(N_CHAIN):
                    y_acc[c] = jnp.where(sub8 == idx[c], _to_row(ys[c], dmask), y_acc[c])
            for c in range(N_CHAIN):
                d, lanes = _chain(c)
                y_refs[d][pl.ds(off[d], GROUP), lanes] = y_acc[c]
            return tuple(states)

        final = lax.fori_loop(0, groups, group, tuple(st_ref[c] for c in range(N_CHAIN)))
        for c in range(N_CHAIN):
            st_ref[c] = final[c]
            hist_refs[c // N_PAIR][c % N_PAIR, SCAN_CHUNK] = final[c]

    y_spec_f = pl.BlockSpec((SCAN_CHUNK, D_RWKV), lambda b, g: (b * n + up(g), 0))
    y_spec_b = pl.BlockSpec((SCAN_CHUNK, D_RWKV), lambda b, g: (b * n + down(g), 0))
    hist_shape = (batch, n, N_PAIR, SCAN_CHUNK + 1, HEAD, 2 * HEAD)
    hist_block = (None, None, N_PAIR, SCAN_CHUNK + 1, HEAD, 2 * HEAD)
    hist_spec_f = pl.BlockSpec(hist_block, lambda b, g: (b, up(g), 0, 0, 0, 0))
    hist_spec_b = pl.BlockSpec(hist_block, lambda b, g: (b, down(g), 0, 0, 0, 0))
    ones_spec = pl.BlockSpec((4 * HEAD, 2 * HEAD), lambda b, g: (0, 0))
    (wf, kdf, bf), (wb, kdb, bb) = dirs
    return pl.pallas_call(
        body, name="wkv_fwd", grid=(batch, n),
        in_specs=_scan_specs(n, col_blocks, up, down) + [ones_spec],
        out_specs=[y_spec_f, y_spec_b, hist_spec_f, hist_spec_b],
        out_shape=[_sds((t, D_RWKV)), _sds((t, D_RWKV)), _sds(hist_shape), _sds(hist_shape)],
        scratch_shapes=[pltpu.VMEM((N_CHAIN, HEAD, 2 * HEAD), F32)],
        compiler_params=_cp("parallel", "arbitrary"),
    )(ps, ps, kk, wf, kdf, bf, ps, ps, kk, wb, kdb, bb, _pair_ones())


def _scan_bwd(ps, kk, dirs, dy, hist_f, hist_b, batch, seq):
    t = batch * seq
    n = seq // SCAN_CHUNK
    groups = SCAN_CHUNK // GROUP
    fwd_chunk = lambda g: n - 1 - g
    bwd_chunk = lambda g: g
    col_blocks = (0, 2, 0, 0, 0, 0, 0)

    def undo_group(dir_refs, out_refs, hist_refs, gi, d_states, ones_pair, dmask, sub8):
        d_states = list(d_states)
        loaded, blocks = [], []
        for d in range(2):
            blk = groups - 1 - gi if d == 0 else gi
            blocks.append(pl.ds(pl.multiple_of(blk * GROUP, GROUP), GROUP))
            r8, v8, kk8, w8, kd8, b8, dy8 = (ref[blocks[d], :] for ref in dir_refs[d])
            loaded.append((r8, v8, kk8, w8, kd8, -b8, dy8))
        acc = [[jnp.zeros((GROUP, 2 * HEAD), F32) for _ in range(6)] for _ in range(N_CHAIN)]
        for step in range(GROUP):
            rows, idx, before, after = [], [], [], []
            for c in range(N_CHAIN):
                d, lanes = _chain(c)
                i = GROUP - 1 - step if d == 0 else step
                q = (groups - 1 - gi) * GROUP + i if d == 0 else SCAN_CHUNK - 1 - (gi * GROUP + i)
                idx.append(i)
                rows.append(tuple(x8[i:i + 1, lanes] for x8 in loaded[d]))
                before.append(hist_refs[d][c % N_PAIR, q])
                after.append(hist_refs[d][c % N_PAIR, q + 1])
            _, cols = _seg_many([], [dmask.astype(BF16) * rows[c][j].astype(BF16) for c in range(N_CHAIN) for j in (1, 6)],
                                ones_pair)
            v_cols, dy_cols = cols[0::2], cols[1::2]
            d_now = [d_states[c] + dy_cols[c] * rows[c][0] for c in range(N_CHAIN)]
            d_sas, _ = _seg_many([d_now[c] * rows[c][5] for c in range(N_CHAIN)], [], ones_pair)
            _, others = _seg_many(
                [], [x for c in range(N_CHAIN) for x in (before[c] * rows[c][2], d_now[c] * rows[c][4])], ones_pair)
            for c in range(N_CHAIN):
                sa, d_sa, dv_cols = others[2 * c], d_sas[c], others[2 * c + 1]
                rows_out = (
                    jnp.sum(after[c] * dy_cols[c], axis=0, keepdims=True),
                    jnp.sum(d_now[c] * before[c], axis=0, keepdims=True),
                    jnp.sum(d_now[c] * v_cols[c], axis=0, keepdims=True),
                    _to_row(dv_cols, dmask),
                    jnp.sum(before[c] * d_sa, axis=0, keepdims=True),
                    -jnp.sum(d_now[c] * sa, axis=0, keepdims=True),
                )
                acc[c] = [jnp.where(sub8 == idx[c], val, a) for val, a in zip(rows_out, acc[c])]
                d_states[c] = d_now[c] * rows[c][3] + d_sa * rows[c][2]
        for c in range(N_CHAIN):
            d, lanes = _chain(c)
            for o_ref, val in zip(out_refs[d], acc[c]):
                o_ref[blocks[d], lanes] = val
        return tuple(d_states)

    def body(*refs):
        dir_refs = (refs[0:7], refs[7:14])
        hist_refs, ones_ref = refs[14:16], refs[16]
        out_refs = (refs[17:23], refs[23:29])
        dst_ref = refs[29]

        @pl.when(pl.program_id(1) == 0)
        def _():
            dst_ref[...] = jnp.zeros_like(dst_ref)

        ones_pair = ones_ref[...]
        dmask = _diag_mask()
        sub8 = lax.broadcasted_iota(jnp.int32, (GROUP, 2 * HEAD), 0)

        def group(gi, carry):
            return undo_group(dir_refs, out_refs, hist_refs, gi, carry, ones_pair, dmask, sub8)

        final = lax.fori_loop(0, groups, group, tuple(dst_ref[c] for c in range(N_CHAIN)))
        for c in range(N_CHAIN):
            dst_ref[c] = final[c]

    blk = (SCAN_CHUNK, D_RWKV)
    out_f = pl.BlockSpec(blk, lambda b, g: (b * n + fwd_chunk(g), 0))
    out_b = pl.BlockSpec(blk, lambda b, g: (b * n + bwd_chunk(g), 0))
    hist_block = (None, None, N_PAIR, SCAN_CHUNK + 1, HEAD, 2 * HEAD)
    hist_spec_f = pl.BlockSpec(hist_block, lambda b, g: (b, fwd_chunk(g), 0, 0, 0, 0))
    hist_spec_b = pl.BlockSpec(hist_block, lambda b, g: (b, bwd_chunk(g), 0, 0, 0, 0))
    ones_spec = pl.BlockSpec((4 * HEAD, 2 * HEAD), lambda b, g: (0, 0))
    (wf, kdf, bf), (wb, kdb, bb) = dirs
    outs = pl.pallas_call(
        body, name="wkv_bwd", grid=(batch, n),
        in_specs=_scan_specs(n, col_blocks, fwd_chunk, bwd_chunk) + [hist_spec_f, hist_spec_b, ones_spec],
        out_specs=[out_f] * 6 + [out_b] * 6,
        out_shape=[_sds((t, D_RWKV))] * 12,
        scratch_shapes=[pltpu.VMEM((N_CHAIN, HEAD, 2 * HEAD), F32)],
        compiler_params=_cp("parallel", "arbitrary"),
    )(ps, ps, kk, wf, kdf, bf, dy, ps, ps, kk, wb, kdb, bb, dy, hist_f, hist_b, _pair_ones())
    return outs[0:6], outs[6:12]


def _post_math(y, r, kd_f, kd_b, v, gate, gn_w, gn_b, rk_f, rk_b, ones_blocks):
    mean = _seg(y, ones_blocks) * (1.0 / HEAD)
    yc = y - mean
    var = _seg(yc * yc, ones_blocks) * (1.0 / HEAD)
    yn = yc * lax.rsqrt(var + GN_EPS) * gn_w + gn_b
    bonus = _seg(r * kd_f * rk_f, ones_blocks) * v + _seg(r * kd_b * rk_b, ones_blocks) * v
    return (yn + bonus) * gate


def _conv_parts(pc, halo_prev, halo_next, has_prev, has_next):
    gate_b, gate_c, hid = pc[:, 0:512], pc[:, 512:1024], pc[:, 1024:1536]
    u = gate_c * hid
    u_prev_row = halo_prev[GROUP - 1:GROUP, 512:1024] * halo_prev[GROUP - 1:GROUP, 1024:1536] * has_prev
    u_next_row = halo_next[0:1, 512:1024] * halo_next[0:1, 1024:1536] * has_next
    u_down, u_up = _shifted(u, u_prev_row, u_next_row)
    return gate_b, gate_c, hid, u, u_down, u_up


def _post_specs(tm, t):
    pc_prev, pc_next = _halo_specs(tm, D_CONV3, t)
    col = lambda c: pl.BlockSpec((tm, D_RWKV), lambda i: (i, c))
    return ([col(0), col(0), col(0), col(0), col(0), col(2), col(0), _row(tm, D_CONV3), pc_prev, pc_next,
             _fixed((8, D_RWKV)), _fixed((D_RWKV, D_RWKV))])


def _post_fwd(y_f, y_b, ps, kd_f, kd_b, gate, pc, qvec, ones_blocks, tm, seq):
    t = ps.shape[0]

    def body(yf_ref, yb_ref, r_ref, kdf_ref, kdb_ref, v_ref, g_ref, pc_ref, hp_ref, hn_ref, qv_ref, ones_ref,
             o_ref, ot_ref):
        has_prev, has_next = _edge_flags(tm, seq)
        vec = [jnp.broadcast_to(qv_ref[j:j + 1, :], (tm, D_RWKV)) for j in range(7)]
        o_rwkv = _post_math(yf_ref[...] + yb_ref[...], r_ref[...], kdf_ref[...], kdb_ref[...], v_ref[...],
                            g_ref[...], vec[0], vec[1], vec[2], vec[3], ones_ref[...])
        gate_b, _, _, u, u_down, u_up = _conv_parts(pc_ref[...], hp_ref[...], hn_ref[...], has_prev, has_next)
        o_conv = gate_b * (vec[4] * u_down + vec[5] * u + vec[6] * u_up)
        for half, val in enumerate((o_rwkv, o_conv)):
            o_ref[:, D_RWKV * half:D_RWKV * (half + 1)] = val.astype(BF16)
            ot_ref[D_RWKV * half:D_RWKV * (half + 1), :] = jnp.transpose(val).astype(BF16)

    return pl.pallas_call(
        body, name="post_fwd", grid=(t // tm,), in_specs=_post_specs(tm, t),
        out_specs=[_row(tm, D_MODEL), _col(tm, D_MODEL)],
        out_shape=[_sds((t, D_MODEL), BF16), _sds((D_MODEL, t), BF16)], compiler_params=_cp("parallel"),
    )(y_f, y_b, ps, kd_f, kd_b, ps, gate, pc, pc, pc, qvec, ones_blocks)


def _post_bwd(d_out, y_f, y_b, ps, kd_f, kd_b, gate, pc, qvec, ones_blocks, tm, seq):
    t = ps.shape[0]
    do_prev, do_next = _halo_specs(tm, D_MODEL, t)

    def body(do_ref, dop_ref, don_ref, yf_ref, yb_ref, r_ref, kdf_ref, kdb_ref, v_ref, g_ref, pc_ref, hp_ref,
             hn_ref, qv_ref, ones_ref, dy_ref, dr_ref, dkdf_ref, dkdb_ref, dv_ref, dg_ref, dpc_ref, dqv_ref):
        has_prev, has_next = _edge_flags(tm, seq)
        vec = [jnp.broadcast_to(qv_ref[j:j + 1, :], (tm, D_RWKV)) for j in range(7)]
        ones_v = ones_ref[...]
        args = (yf_ref[...] + yb_ref[...], r_ref[...], kdf_ref[...], kdb_ref[...], v_ref[...], g_ref[...],
                vec[0], vec[1], vec[2], vec[3])
        _, vjp = jax.vjp(lambda *a: _post_math(*a, ones_v), *args)
        grads = vjp(do_ref[:, 0:D_RWKV])
        for o_ref, gval in zip((dy_ref, dr_ref, dkdf_ref, dkdb_ref, dv_ref, dg_ref), grads[0:6]):
            o_ref[...] = gval

        hp, hn = hp_ref[...], hn_ref[...]
        gate_b, gate_c, hid, u, u_down, u_up = _conv_parts(pc_ref[...], hp, hn, has_prev, has_next)
        d_oc = do_ref[:, D_RWKV:2 * D_RWKV]
        d_cu = d_oc * gate_b
        d_cu_prev = dop_ref[GROUP - 1:GROUP, D_RWKV:2 * D_RWKV] * hp[GROUP - 1:GROUP, 0:512] * has_prev
        d_cu_next = don_ref[0:1, D_RWKV:2 * D_RWKV] * hn[0:1, 0:512] * has_next
        d_cu_down, d_cu_up = _shifted(d_cu, d_cu_prev, d_cu_next)
        d_u = vec[5] * d_cu + vec[4] * d_cu_up + vec[6] * d_cu_down
        dpc_ref[:, 0:512] = (d_oc * (vec[4] * u_down + vec[5] * u + vec[6] * u_up)).astype(BF16)
        dpc_ref[:, 512:1024] = (d_u * hid).astype(BF16)
        dpc_ref[:, 1024:1536] = (d_u * gate_c).astype(BF16)

        @pl.when(pl.program_id(0) == 0)
        def _():
            dqv_ref[...] = jnp.zeros_like(dqv_ref)

        vec_grads = list(grads[6:10]) + [d_cu * u_down, d_cu * u, d_cu * u_up]
        for j, gval in enumerate(vec_grads):
            dqv_ref[j:j + 1, :] += jnp.sum(gval, axis=0, keepdims=True)

    return pl.pallas_call(
        body, name="post_bwd", grid=(t // tm,),
        in_specs=[_row(tm, D_MODEL), do_prev, do_next] + _post_specs(tm, t),
        out_specs=[_row(tm, D_RWKV)] * 6 + [_row(tm, D_CONV3), _fixed((8, D_RWKV))],
        out_shape=[_sds((t, D_RWKV))] * 6 + [_sds((t, D_CONV3), BF16), _sds((8, D_RWKV))],
        compiler_params=_cp("arbitrary"),
    )(d_out, d_out, d_out, y_f, y_b, ps, kd_f, kd_b, ps, gate, pc, pc, pc, qvec, ones_blocks)


def _adamw_math(wv, gv, mv, vv):
    m2 = ADAM_B1 * mv + (1.0 - ADAM_B1) * gv
    v2 = ADAM_B2 * vv + (1.0 - ADAM_B2) * (gv * gv)
    m_hat = m2 / (1.0 - ADAM_B1 ** ADAM_STEP)
    v_hat = v2 / (1.0 - ADAM_B2 ** ADAM_STEP)
    return -ADAM_LR * (m_hat / (jnp.sqrt(v_hat) + ADAM_EPS) + ADAM_WD * wv), m2, v2


def _adamw_small(items):
    n = len(items)

    def body(*refs):
        ins, outs = refs[:4 * n], refs[4 * n:]
        for k in range(n):
            w_ref, g_ref, m_ref, v_ref = ins[4 * k:4 * k + 4]
            for o_ref, val in zip(outs[3 * k:3 * k + 3], _adamw_math(w_ref[...], g_ref[...], m_ref[...], v_ref[...])):
                o_ref[...] = val

    flat = [a for item in items for a in item]
    outs = pl.pallas_call(
        body, name="adamw_small", out_shape=[_sds(item[0].shape) for item in items for _ in range(3)],
        compiler_params=_cp())(*flat)
    return [tuple(outs[3 * k:3 * k + 3]) for k in range(n)]


def _adamw(w, g, m, v, name):
    r, c = w.shape[-2:]
    tr = _tile(r, 256, 8)
    if w.ndim == 3:
        spec = pl.BlockSpec((None, tr, c), lambda i: (0, i, 0))
    else:
        spec = pl.BlockSpec((tr, c), lambda i: (i, 0))

    def body(w_ref, g_ref, m_ref, v_ref, d_ref, nm_ref, nv_ref):
        d_ref[...], nm_ref[...], nv_ref[...] = _adamw_math(w_ref[...], g_ref[...], m_ref[...], v_ref[...])

    return pl.pallas_call(
        body, name=name, grid=(r // tr,), in_specs=[spec] * 4, out_specs=[spec] * 3,
        out_shape=[_sds(w.shape)] * 3, compiler_params=_cp("parallel"))(w, g, m, v)


_ANY = pl.BlockSpec(memory_space=pl.ANY)


def _place():
    return lax.axis_index("x"), lax.axis_index("y"), lax.axis_index("c")


def _other_chips(x, y):
    return [(1 - x, y), (x, 1 - y), (1 - x, 1 - y)]


def _remote(src, dst, send_sems, recv_sems, k, to):
    return pltpu.make_async_remote_copy(src_ref=src, dst_ref=dst, send_sem=send_sems.at[k],
                                        recv_sem=recv_sems.at[k], device_id=to, device_id_type=MESH)


def _gather_weights(pack):
    rows, width = pack.shape
    half = rows // 2

    def body(x_ref, out_ref, send_sems, recv_sems):
        x, y, c = _place()
        sibling = (x, y, 1 - c)
        chips = _other_chips(x, y)

        def block(chip, part):
            return out_ref.at[2 * chip[0] + chip[1], pl.ds(part * half, half), :]

        first = [_remote(x_ref.at[pl.ds(c * half, half), :], block((x, y), c), send_sems, recv_sems, j, (*chip, c))
                 for j, chip in enumerate(chips)]
        for cp in first:
            cp.start()
        passed = [_remote(block(chip, c), block(chip, c), send_sems, recv_sems, 3 + j, sibling)
                  for j, chip in enumerate(chips)]
        for j, chip in enumerate(chips):
            _remote(block(chip, c), block(chip, c), send_sems, recv_sems, j, sibling).wait_recv()
            passed[j].start()
        for j, chip in enumerate(chips):
            _remote(block(chip, 1 - c), block(chip, 1 - c), send_sems, recv_sems, 3 + j, sibling).wait_recv()
        for cp in first + passed:
            cp.wait_send()

    return pl.pallas_call(
        body, name="gather_weights", in_specs=[_ANY], out_specs=_ANY,
        out_shape=_sds((N_SHARD, rows, width), pack.dtype),
        scratch_shapes=[pltpu.SemaphoreType.DMA((6,)), pltpu.SemaphoreType.DMA((6,))],
    )(pack)


_HBM = pl.BlockSpec(memory_space=pltpu.HBM)
_SEMS = pl.BlockSpec(memory_space=pltpu.SEMAPHORE)
_DATAFLOW = pltpu.SideEffectType.DATAFLOW_SIDE_EFFECTING


def _fetch_start(pack, after):
    def body(x_ref, land_ref, after_ref, send_sems, recv_sems, x_thru, land_thru, token):
        x, y, c = _place()
        for j, chip in enumerate(_other_chips(x, y)):
            _remote(x_ref, land_ref.at[2 * x + y], send_sems, recv_sems, j, (*chip, c)).start()
        token[...] = jnp.zeros_like(token)

    land = lax.empty((N_SHARD,) + pack.shape, pack.dtype)
    return pl.pallas_call(
        body, name="fetch_ffn_start",
        out_shape=(pltpu.SemaphoreType.DMA((3,)), pltpu.SemaphoreType.DMA((3,)), pltpu.HBM(pack.shape, pack.dtype),
                   pltpu.HBM(land.shape, land.dtype), _sds((8, 128))),
        in_specs=(_HBM, _HBM, _ANY), out_specs=(_SEMS, _SEMS, _HBM, _HBM, pl.BlockSpec(memory_space=pltpu.VMEM)),
        input_output_aliases={0: 2, 1: 3}, compiler_params=pltpu.CompilerParams(has_side_effects=_DATAFLOW),
    )(pltpu.with_memory_space_constraint(pack, pltpu.HBM), pltpu.with_memory_space_constraint(land, pltpu.HBM), after)


def _fetch_wait(send_sems, recv_sems, pack_thru, land_thru, after):
    def body(x_ref, land_ref, send_sems, recv_sems, after_ref, x_dead, got_ref):
        x, y, c = _place()
        for j, chip in enumerate(_other_chips(x, y)):
            cp = _remote(x_ref, land_ref.at[2 * chip[0] + chip[1]], send_sems, recv_sems, j, (*chip, c))
            cp.wait_send()
            cp.wait_recv()

    return pl.pallas_call(
        body, name="fetch_ffn_wait",
        out_shape=(pltpu.HBM(pack_thru.shape, pack_thru.dtype), pltpu.HBM(land_thru.shape, land_thru.dtype)),
        in_specs=(_HBM, _HBM, _SEMS, _SEMS, _ANY), out_specs=(_HBM, _HBM), input_output_aliases={0: 0, 1: 1},
        compiler_params=pltpu.CompilerParams(has_side_effects=_DATAFLOW),
    )(pack_thru, land_thru, send_sems, recv_sems, after)[1]


def _swap_with_sibling(block, name):
    def body(x_ref, out_ref, send_sems, recv_sems):
        x, y, c = _place()
        cp = _remote(x_ref, out_ref, send_sems, recv_sems, 0, (x, y, 1 - c))
        cp.start()
        cp.wait()

    return pl.pallas_call(
        body, name=name, in_specs=[_ANY], out_specs=_ANY, out_shape=_sds(block.shape, block.dtype),
        scratch_shapes=[pltpu.SemaphoreType.DMA((1,)), pltpu.SemaphoreType.DMA((1,))],
    )(block)


def _swap_other_half(packed, tag):
    slots, rows, width = packed.shape
    half = rows // 2

    def body(x_ref, out_ref, send_sems, recv_sems):
        x, y, c = _place()
        cp = _remote(x_ref.at[:, pl.ds((1 - c) * half, half), :], out_ref, send_sems, recv_sems, 0, (x, y, 1 - c))
        cp.start()
        cp.wait()

    return pl.pallas_call(
        body, name="swap_halves_" + tag, in_specs=[_ANY], out_specs=_ANY, out_shape=_sds((slots, half, width)),
        scratch_shapes=[pltpu.SemaphoreType.DMA((1,)), pltpu.SemaphoreType.DMA((1,))],
    )(packed)


def _add_halves(packed, got, c, tag):
    slots, rows, width = packed.shape
    half = rows // 2
    tr = _tile(half, 408, 16)
    per = half // tr
    block = (None, tr, width)

    def body(c_ref, mine_ref, got_ref, sum_ref, sum16_ref):
        acc = mine_ref[...] + got_ref[...]
        sum_ref[...] = acc
        sum16_ref[...] = acc.astype(BF16)

    plain = pl.BlockSpec(block, lambda s, i, c_ref: (s, i, 0))
    grid_spec = pltpu.PrefetchScalarGridSpec(
        num_scalar_prefetch=1, grid=(slots, per),
        in_specs=[pl.BlockSpec(block, lambda s, i, c_ref: (s, c_ref[0] * per + i, 0)), plain],
        out_specs=[plain, plain])
    return pl.pallas_call(
        body, name="add_halves_" + tag, grid_spec=grid_spec,
        out_shape=[_sds((slots, half, width)), _sds((slots, half, width), BF16)],
        compiler_params=_cp("parallel", "parallel"))(c.reshape(1).astype(jnp.int32), packed, got)


def _add_quarters(chip_sum, others, chip, tag):
    _, rows, width = chip_sum.shape
    tr = _tile(rows, 408, 16)

    def body(chip_ref, own_ref, others_ref, o_ref):
        acc = own_ref[...]
        for j in range(3):
            acc = acc + others_ref[j].astype(F32)
        o_ref[...] = acc

    grid_spec = pltpu.PrefetchScalarGridSpec(
        num_scalar_prefetch=1, grid=(rows // tr,),
        in_specs=[pl.BlockSpec((None, tr, width), lambda i, chip_ref: (chip_ref[0], i, 0)),
                  pl.BlockSpec((3, tr, width), lambda i, chip_ref: (0, i, 0))],
        out_specs=pl.BlockSpec((tr, width), lambda i, chip_ref: (i, 0)))
    return pl.pallas_call(
        body, name="add_quarters_" + tag, grid_spec=grid_spec, out_shape=_sds((rows, width)),
        compiler_params=_cp("parallel"))(chip.reshape(1).astype(jnp.int32), chip_sum, others)


def _exchange_start(parts, tag):
    _, rows, width = parts.shape

    def body(x_ref, land_ref, send_sems, recv_sems, x_thru, land_thru, token):
        x, y, c = _place()
        for j, chip in enumerate(_other_chips(x, y)):
            _remote(x_ref.at[2 * chip[0] + chip[1]], land_ref.at[j], send_sems, recv_sems, j, (*chip, c)).start()
        token[...] = jnp.zeros_like(token)

    land = lax.empty((3, rows, width), parts.dtype)
    return pl.pallas_call(
        body, name="exchange_" + tag + "_start",
        out_shape=(pltpu.SemaphoreType.DMA((3,)), pltpu.SemaphoreType.DMA((3,)), pltpu.HBM(parts.shape, parts.dtype),
                   pltpu.HBM(land.shape, land.dtype), _sds((8, 128))),
        in_specs=(_HBM, _HBM), out_specs=(_SEMS, _SEMS, _HBM, _HBM, pl.BlockSpec(memory_space=pltpu.VMEM)),
        input_output_aliases={0: 2, 1: 3}, compiler_params=pltpu.CompilerParams(has_side_effects=_DATAFLOW),
    )(pltpu.with_memory_space_constraint(parts, pltpu.HBM), pltpu.with_memory_space_constraint(land, pltpu.HBM))


def _exchange_wait(send_sems, recv_sems, parts_thru, land_thru, after, tag):
    def body(x_ref, land_ref, send_sems, recv_sems, after_ref, x_dead, got_ref):
        x, y, c = _place()
        for j, chip in enumerate(_other_chips(x, y)):
            cp = _remote(x_ref.at[2 * chip[0] + chip[1]], land_ref.at[j], send_sems, recv_sems, j, (*chip, c))
            cp.wait_send()
            cp.wait_recv()

    return pl.pallas_call(
        body, name="exchange_" + tag + "_wait",
        out_shape=(pltpu.HBM(parts_thru.shape, parts_thru.dtype), pltpu.HBM(land_thru.shape, land_thru.dtype)),
        in_specs=(_HBM, _HBM, _SEMS, _SEMS, _ANY), out_specs=(_HBM, _HBM), input_output_aliases={0: 0, 1: 1},
        compiler_params=pltpu.CompilerParams(has_side_effects=_DATAFLOW),
    )(parts_thru, land_thru, send_sems, recv_sems, after)[1]


def _allreduce_small(vec):
    rows, width = vec.shape
    vmem = pl.BlockSpec(memory_space=pltpu.VMEM)

    def body(x_ref, o_ref, buf_ref, send_sems, recv_sems):
        x, y, c = _place()
        me = 4 * x + 2 * y + c
        buf_ref[me] = x_ref[...]
        copies = []
        for k in range(1, N_DEV):
            peer = (x ^ ((k >> 2) & 1), y ^ ((k >> 1) & 1), c ^ (k & 1))
            copies.append(_remote(x_ref, buf_ref.at[me], send_sems, recv_sems, k - 1, peer))
        for cp in copies:
            cp.start()
        for k in range(1, N_DEV):
            _remote(x_ref, buf_ref.at[me ^ k], send_sems, recv_sems, k - 1, (x, y, c)).wait_recv()
        for cp in copies:
            cp.wait_send()
        total = buf_ref[0]
        for d in range(1, N_DEV):
            total = total + buf_ref[d]
        o_ref[...] = total

    return pl.pallas_call(
        body, name="allreduce_small", in_specs=[vmem], out_specs=vmem, out_shape=_sds((rows, width)),
        scratch_shapes=[pltpu.VMEM((N_DEV, rows, width), F32), pltpu.SemaphoreType.DMA((N_DEV - 1,)),
                        pltpu.SemaphoreType.DMA((N_DEV - 1,))],
    )(vec)


def _rows1024(a):
    return a.reshape(-1, 1024)


def _pad_rows(a, rows):
    return jnp.concatenate([a, jnp.zeros((rows - a.shape[0], a.shape[1]), a.dtype)], axis=0)


_TRANSPOSED = ("w_in", "w_gate", "w_up")
_SMALL_SHARDED = ("w_up_f", "w_up_b", "a_up_f", "a_up_b", "g_up")
_BIG_SHARDED = ("w_in", "w_out", "w_gate", "w_up", "w_down")


def _pack_weight_shards(w):
    conv_bits = lax.bitcast_convert_type(w["conv_w"], BF16).reshape(1, -1)
    conv_row = jnp.concatenate([conv_bits, jnp.zeros((1, 1024 - conv_bits.shape[1]), BF16)], axis=1)

    def rows(name):
        a = w[name].astype(BF16)
        return a.T if name in _TRANSPOSED else _rows1024(a)

    early = _pad_rows(jnp.concatenate([rows(name) for name, _ in _EARLY_ROWS[:-1]] + [conv_row], axis=0), EARLY_R)
    return early, jnp.concatenate([rows(name) for name, _ in _FFN_ROWS], axis=0)


def _split_rows(gathered, layout):
    out, row = {}, 0
    for name, n in layout:
        out[name] = gathered[:, row:row + n]
        row += n
    return out


def _unpack_early(gathered):
    out = _split_rows(gathered, _EARLY_ROWS)
    cols = lambda a, k: jnp.concatenate([a[s].reshape(k, -1) for s in range(N_SHARD)], axis=1)
    conv = lax.bitcast_convert_type(out["conv_w"][:, 0, :768].reshape(N_SHARD, 3, 128, 2), F32)
    full = dict(w_in=out["w_in"].reshape(-1, 1024).T, w_out=out["w_out"].reshape(D_MODEL, D_MODEL),
                conv_w=jnp.concatenate([conv[s] for s in range(N_SHARD)], axis=1))
    full.update({name: cols(out[name], D_GATE if name == "g_up" else D_LORA) for name in _SMALL_SHARDED})
    return full


def _unpack_ffn(gathered):
    out = _split_rows(gathered, _FFN_ROWS)
    return dict(w_gate=out["w_gate"].reshape(-1, 1024).T, w_up=out["w_up"].reshape(-1, 1024).T,
                w_down=out["w_down"].reshape(D_FF, D_MODEL))


def _pack_grads(g, layout, rows):
    col_split = lambda a, s: a[:, s * (a.shape[1] // N_SHARD):(s + 1) * (a.shape[1] // N_SHARD)]
    row_split = lambda a, s: a[s * (a.shape[0] // N_SHARD):(s + 1) * (a.shape[0] // N_SHARD)]
    by_rows = {name: (g[name].T if name in _TRANSPOSED else g[name]) for name, _ in layout if name in _BIG_SHARDED}
    used = sum(n for _, n in layout)
    parts = []
    for s in range(N_SHARD):
        for name, _ in layout:
            if name in _BIG_SHARDED:
                parts.append(row_split(by_rows[name], s))
            elif name in _SMALL_SHARDED:
                parts.append(_rows1024(col_split(g[name], s)))
            else:
                conv = col_split(g["conv_w"], s).reshape(1, -1)
                parts.append(jnp.concatenate([conv, jnp.zeros((1, 1024 - conv.shape[1]), F32)], axis=1))
        if rows > used:
            parts.append(jnp.zeros((rows - used, 1024), F32))
    return jnp.concatenate(parts, axis=0).reshape(N_SHARD, rows, 1024)


def _unpack_grad_shard(pack, layout):
    small_shapes = {name: (D_GATE if name == "g_up" else D_LORA, 128) for name in _SMALL_SHARDED}
    out, row = {}, 0
    for name, n in layout:
        piece = pack[row:row + n]
        if name == "conv_w":
            out[name] = piece[0, :384].reshape(3, 128)
        else:
            out[name] = piece.T if name in _TRANSPOSED else piece.reshape(small_shapes.get(name, piece.shape))
        row += n
    return out


_SMALL_LAYOUT = (("norm1_w", 1024), ("mu_shift", D_SHIFT), ("w0_f", 512), ("w0_b", 512), ("a0_f", 512),
                 ("a0_b", 512), ("k_k", 512), ("k_a_f", 512), ("k_a_b", 512), ("r_k_f", 512), ("r_k_b", 512),
                 ("gn_w", 512), ("gn_b", 512), ("norm2_w", 1024), ("norm_f_w", 1024), ("loss", 1))


def _pack_small(vals):
    rows = []
    for name, n in _SMALL_LAYOUT:
        flat = vals[name].reshape(-1)
        n_rows = -(-n // 1024)
        rows.append(jnp.concatenate([flat, jnp.zeros((n_rows * 1024 - n,), F32)]).reshape(n_rows, 1024))
    return _pad_rows(jnp.concatenate(rows, axis=0), SMALL_ROWS)


def _unpack_small(pack):
    out, row = {}, 0
    for name, n in _SMALL_LAYOUT:
        n_rows = -(-n // 1024)
        out[name] = pack[row:row + n_rows].reshape(-1)[:n]
        row += n_rows
    return out


_WEIGHTS = ("norm1_w", "w_in", "mu_shift", "w_up_f", "w0_f", "w_up_b", "w0_b", "a_up_f", "a0_f", "a_up_b", "a0_b",
            "g_up", "k_k", "k_a_f", "k_a_b", "r_k_f", "r_k_b", "gn_w", "gn_b", "conv_w", "w_out", "norm2_w",
            "w_gate", "w_up", "w_down", "norm_f_w")


def _train_step(x, loss_target, w, m, v):
    batch, seq, _ = x.shape
    t = batch * seq
    tm = _tile(seq, 256, 8)
    xs = x.reshape(t, D_MODEL)
    target = loss_target.reshape(t, D_MODEL)
    vec = lambda name: w[name].reshape(1, -1)

    local = {name: w[name][0] for name, _ in _PACK_ROWS}
    c = lax.axis_index("c")
    chip = 2 * lax.axis_index("x") + lax.axis_index("y")
    early, ffn_pack = _pack_weight_shards(local)
    early_all = lax.dynamic_update_slice(_gather_weights(early), early[None], (chip, 0, 0))
    ffn_send, ffn_recv, ffn_pack, ffn_land, token = _fetch_start(ffn_pack, early_all)
    full = _unpack_early(early_all)
    w_in = full["w_in"]
    w_shift = jnp.concatenate([w_in[:, :D_SHIFT], jnp.zeros((D_MODEL, D_SHIFT_PAD - D_SHIFT), BF16)], axis=1)
    w_conv = w_in[:, D_SHIFT:]
    zeros_lora = jnp.zeros((D_LORA, D_RWKV), F32)
    lora = lambda name: full[name].astype(F32)
    mats = (jnp.concatenate([lora("w_up_f"), zeros_lora]), jnp.concatenate([zeros_lora, lora("a_up_f")]),
            jnp.concatenate([lora("w_up_b"), zeros_lora]), jnp.concatenate([zeros_lora, lora("a_up_b")]),
            jnp.concatenate([lora("g_up"), jnp.zeros((D_GATE_PAD - D_GATE, D_RWKV), F32)]))
    mu = jnp.concatenate([vec("mu_shift"), jnp.zeros((1, D_SHIFT_PAD - D_SHIFT), F32)], axis=1)
    mu = jnp.broadcast_to(mu, (GROUP, D_SHIFT_PAD))
    zero_row = jnp.zeros((1, D_RWKV), F32)
    pvec = jnp.concatenate([vec("k_k"), vec("w0_f"), vec("a0_f"), vec("k_a_f"), vec("w0_b"), vec("a0_b"),
                            vec("k_a_b"), zero_row], axis=0)
    qvec = jnp.concatenate([vec("gn_w"), vec("gn_b"), vec("r_k_f"), vec("r_k_b"), full["conv_w"], zero_row], axis=0)
    ones_blocks = _head_ones()

    h1_t, p_shift, pc = _norm_in_proj(xs, vec("norm1_w") + token[0, 0], w_shift, w_conv, tm)
    ps, kk, w_f, kd_f, b_f, w_b, kd_b, b_b, gate = _shift_prep_fwd(p_shift, mu, pvec, mats, ones_blocks, tm, seq)
    dirs = ((w_f, kd_f, b_f), (w_b, kd_b, b_b))
    y_f, y_b, hist_f, hist_b = _scan_fwd(ps, kk, dirs, batch, seq)
    mixed, mixed_t = _post_fwd(y_f, y_b, ps, kd_f, kd_b, gate, pc, qvec, ones_blocks, tm, seq)
    x1, h2, h2_t = _out_proj_norm(mixed, full["w_out"], xs, vec("norm2_w"), tm)
    ffn_all = _fetch_wait(ffn_send, ffn_recv, ffn_pack, ffn_land, h2)
    full.update(_unpack_ffn(lax.dynamic_update_slice(ffn_all, ffn_pack[None], (chip, 0, 0))))
    ff_gate, ff_up, act, act_t = _ffn_in(h2, full["w_gate"], full["w_up"])
    d_x2, d_x2_bf16, d_norm_f, loss_part = _ffn_out_loss(act, full["w_down"], x1, w["norm_f_w"].reshape(1, -1),
                                                         target, tm)

    g = {}
    g["w_down"] = _matmul(act_t, d_x2_bf16, mode="nn", name="ffn_down_dw")
    d_gate, d_up = _ffn_in_bwd(d_x2_bf16, full["w_down"], ff_gate, ff_up)
    g["w_gate"] = _matmul(h2_t, d_gate, mode="nn", name="ffn_gate_dw")
    g["w_up"] = _matmul(h2_t, d_up, mode="nn", name="ffn_up_dw")
    ffn_grads = _pack_grads(g, _FFN_ROWS, sum(n for _, n in _FFN_ROWS))
    ffn_sum, ffn_sum_bf16 = _add_halves(ffn_grads, _swap_other_half(ffn_grads, "ffn"), c, "ffn")
    ex_send, ex_recv, ffn_sum_bf16, ex_land, ex_token = _exchange_start(ffn_sum_bf16, "ffn")
    d_x1, d_norm2 = _proj_norm_bwd(d_gate, full["w_gate"], d_up, full["w_up"], x1, vec("norm2_w") + ex_token[0, 0],
                                   d_x2, tm, "ffn_in_dx_norm2_bwd")
    d_mixed = _matmul(d_x1, full["w_out"], mode="nt", name="out_proj_dx")
    g["w_out"] = _matmul(mixed_t, d_x1, mode="nn", name="out_proj_dw")
    dy, dr_o, dkdf_o, dkdb_o, dv_o, d_gatev, d_pc, d_qvec = _post_bwd(
        d_mixed, y_f, y_b, ps, kd_f, kd_b, gate, pc, qvec, ones_blocks, tm, seq)
    (dr_f, dw_f, dkd_f, dv_f, dkk_f, db_f), (dr_b, dw_b, dkd_b, dv_b, dkk_b, db_b) = _scan_bwd(
        ps, kk, dirs, dy, hist_f, hist_b, batch, seq)
    cts = [[dr_f, dr_b, dr_o], [dv_f, dv_b, dv_o], [dkk_f, dkk_b], [dw_f], [dkd_f, dkdf_o], [db_f],
           [dw_b], [dkd_b, dkdb_o], [db_b], [d_gatev]]
    q, d_pvec, d_m0, d_m1, d_m2, d_m3, d_m4 = _prep_bwd(ps, pvec, mats, ones_blocks, cts, tm)
    d_pshift, d_mu = _shift_bwd(q, p_shift, mu, tm, seq)
    d_w_shift = _matmul(h1_t, d_pshift, mode="nn", name="in_proj_shift_dw")
    d_w_conv = _matmul(h1_t, d_pc, mode="nn", name="in_proj_conv_dw")
    g["w_in"] = jnp.concatenate([d_w_shift[:, :D_SHIFT], d_w_conv], axis=1)
    d_x, d_norm1 = _proj_norm_bwd(d_pshift, w_shift, d_pc, w_conv, xs, vec("norm1_w"), d_x1, tm,
                                  "in_proj_dx_norm1_bwd")
    g["w_up_f"], g["a_up_f"] = d_m0[:D_LORA], d_m1[D_LORA:]
    g["w_up_b"], g["a_up_b"] = d_m2[:D_LORA], d_m3[D_LORA:]
    g["g_up"] = d_m4[:D_GATE]
    g["conv_w"] = d_qvec[4:7]

    def finish(chip_sum, others, tag, layout):
        eighth = _add_quarters(chip_sum, others, chip, tag)
        other_eighth = _swap_with_sibling(eighth, "swap_eighths_" + tag)
        return _unpack_grad_shard(jnp.concatenate([jnp.where(c == 0, eighth, other_eighth),
                                                   jnp.where(c == 0, other_eighth, eighth)], axis=0), layout)

    as2d = lambda name: (1, w[name].shape[0]) if w[name].ndim == 1 else w[name].shape
    operands = lambda name: tuple(a.reshape(as2d(name)) for a in (w[name], grads[name], m[name], v[name]))

    packed = _pack_grads(g, _EARLY_ROWS, EARLY_R)
    mix_sum, mix_sum_bf16 = _add_halves(packed, _swap_other_half(packed, "mixer"), c, "mixer")
    mx_send, mx_recv, mix_sum_bf16, mx_land, mx_token = _exchange_start(mix_sum_bf16, "mixer")
    grads = finish(ffn_sum, _exchange_wait(ex_send, ex_recv, ffn_sum_bf16, ex_land, mx_token, "ffn"), "ffn", _FFN_ROWS)
    updates = {name: _adamw(*operands(name), "adamw_" + name) for name in _FFN_NAMES}
    mix_others = _exchange_wait(mx_send, mx_recv, mix_sum_bf16, mx_land, updates["w_down"][2], "mixer")
    grads.update(finish(mix_sum, mix_others, "mixer", _EARLY_ROWS))

    small = dict(norm1_w=d_norm1, mu_shift=d_mu[:, :D_SHIFT], w0_f=d_pvec[1], w0_b=d_pvec[4], a0_f=d_pvec[2],
                 a0_b=d_pvec[5], k_k=d_pvec[0], k_a_f=d_pvec[3], k_a_b=d_pvec[6], r_k_f=d_qvec[2], r_k_b=d_qvec[3],
                 gn_w=d_qvec[0], gn_b=d_qvec[1], norm2_w=d_norm2, norm_f_w=d_norm_f, loss=loss_part)
    reduced = _unpack_small(_allreduce_small(_pack_small(small)))
    loss = reduced.pop("loss")[0]
    grads.update(reduced)

    outs = {}
    small = [name for name in _WEIGHTS if name not in _BIG_SHARDED]
    updates.update(zip(small, _adamw_small([operands(name) for name in small])))
    for name in ("w_in", "w_out"):
        updates[name] = _adamw(*operands(name), "adamw_" + name)
    for name in _WEIGHTS:
        shape = w[name].shape
        outs[name] = (grads[name].reshape(shape),) + tuple(a.reshape(shape) for a in updates[name])
    d_x = d_x.reshape(batch, seq, D_MODEL)
    return (loss, d_x) + tuple(outs[name][k] for k in range(4) for name in _WEIGHTS)


def kernel(x, norm1_w, w_in, mu_shift, w_up_f, w0_f, w_up_b, w0_b, a_up_f, a0_f, a_up_b, a0_b, g_up, k_k, k_a_f, k_a_b, r_k_f, r_k_b, gn_w, gn_b, conv_w, w_out, norm2_w, w_gate, w_up, w_down, norm_f_w, loss_target, m_norm1_w, m_w_in, m_mu_shift, m_w_up_f, m_w0_f, m_w_up_b, m_w0_b, m_a_up_f, m_a0_f, m_a_up_b, m_a0_b, m_g_up, m_k_k, m_k_a_f, m_k_a_b, m_r_k_f, m_r_k_b, m_gn_w, m_gn_b, m_conv_w, m_w_out, m_norm2_w, m_w_gate, m_w_up, m_w_down, m_norm_f_w, v_norm1_w, v_w_in, v_mu_shift, v_w_up_f, v_w0_f, v_w_up_b, v_w0_b, v_a_up_f, v_a0_f, v_a_up_b, v_a0_b, v_g_up, v_k_k, v_k_a_f, v_k_a_b, v_r_k_f, v_r_k_b, v_gn_w, v_gn_b, v_conv_w, v_w_out, v_norm2_w, v_w_gate, v_w_up, v_w_down, v_norm_f_w):
    args = locals()
    w = {name: args[name] for name in _WEIGHTS}
    m = {name: args["m_" + name] for name in _WEIGHTS}
    v = {name: args["v_" + name] for name in _WEIGHTS}
    return _train_step(x, loss_target, w, m, v)
```

```python
import functools

import jax
import jax.numpy as jnp
from jax import lax
from jax.experimental import pallas as pl
from jax.experimental.pallas import tpu as pltpu

F32 = jnp.float32
BF16 = jnp.bfloat16
MESH = pl.DeviceIdType.MESH

D_MODEL = 1024
D_RWKV = 512
HEAD = 64
N_PAIR = D_RWKV // (2 * HEAD)
D_LORA = 64
D_GATE = 160
D_GATE_PAD = 384
D_FF = 2816
D_SHIFT = 1824
D_SHIFT_PAD = 2048
D_CONV3 = 1536
LOG_DECAY_SCALE = 0.606531
RMS_EPS = 1e-6
GN_EPS = 64e-5
NORM_EPS = 1e-12
ADAM_LR, ADAM_B1, ADAM_B2, ADAM_EPS, ADAM_WD, ADAM_STEP = 0.001, 0.9, 0.999, 1e-08, 0.01, 10

N_SHARD = 4
N_DEV = 8
V7X_VMEM_LIMIT = 48 * 1024 * 1024
SCAN_CHUNK = 64
GROUP = 8

_PACK_ROWS = (("w_in", 840), ("w_out", 256), ("w_gate", 704), ("w_up", 704), ("w_down", 704),
              ("w_up_f", 8), ("w_up_b", 8), ("a_up_f", 8), ("a_up_b", 8), ("g_up", 20), ("conv_w", 1))
_FFN_NAMES = ("w_gate", "w_up", "w_down")
_EARLY_ROWS = tuple(item for item in _PACK_ROWS if item[0] not in _FFN_NAMES)
_FFN_ROWS = tuple(item for item in _PACK_ROWS if item[0] in _FFN_NAMES)
EARLY_R = 1152
SMALL_ROWS = 24


def _tile(n, cap, mult=128):
    best = None
    t = mult
    while t <= min(n, cap):
        if n % t == 0:
            best = t
        t += mult
    return best or n


def _cp(*sem):
    return pltpu.CompilerParams(dimension_semantics=sem or None, vmem_limit_bytes=V7X_VMEM_LIMIT)


def _sds(shape, dtype=F32):
    return jax.ShapeDtypeStruct(shape, dtype)


def _matmul(a, b, *, mode, name, out_dtype=F32, add=None):
    m, kdim = a.shape
    n = b.shape[1] if mode == "nn" else b.shape[0]
    tm, tn = _tile(m, 768, 8), _tile(n, 1536)
    tk = kdim if kdim <= 4096 else _tile(kdim, 2048)
    nk = kdim // tk
    a_spec = pl.BlockSpec((tm, tk), lambda i, j, k: (i, k))
    if mode == "nn":
        b_spec = pl.BlockSpec((tk, tn), lambda i, j, k: (k, j))
        dims = (((1,), (0,)), ((), ()))
    else:
        b_spec = pl.BlockSpec((tn, tk), lambda i, j, k: (j, k))
        dims = (((1,), (1,)), ((), ()))
    has_add = add is not None

    def body(*refs):
        a_ref, b_ref = refs[0], refs[1]
        add_ref = refs[2] if has_add else None
        o_ref = refs[3] if has_add else refs[2]
        part = lax.dot_general(a_ref[...].astype(BF16), b_ref[...].astype(BF16), dims,
                               preferred_element_type=F32)
        if nk == 1:
            if has_add:
                part = part + add_ref[...]
            o_ref[...] = part.astype(out_dtype)
        else:
            acc_ref = refs[-1]
            k = pl.program_id(2)

            @pl.when(k == 0)
            def _():
                acc_ref[...] = jnp.zeros_like(acc_ref)

            acc_ref[...] += part

            @pl.when(k == nk - 1)
            def _():
                res = acc_ref[...]
                if has_add:
                    res = res + add_ref[...]
                o_ref[...] = res.astype(out_dtype)

    o_spec = pl.BlockSpec((tm, tn), lambda i, j, k: (i, j))
    in_specs = [a_spec, b_spec] + ([o_spec] if has_add else [])
    args = (a, b) + ((add,) if has_add else ())
    return pl.pallas_call(
        body, name=name, grid=(m // tm, n // tn, nk), in_specs=in_specs, out_specs=o_spec,
        out_shape=_sds((m, n), out_dtype),
        scratch_shapes=[pltpu.VMEM((tm, tn), F32)] if nk > 1 else [],
        compiler_params=_cp("parallel", "parallel", "arbitrary"),
    )(*args)


def _row(tm, width):
    return pl.BlockSpec((tm, width), lambda i: (i, 0))


def _col(tm, height):
    return pl.BlockSpec((height, tm), lambda i: (0, i))


def _fixed(shape):
    return pl.BlockSpec(shape, lambda i: tuple(0 for _ in shape))


def _rmsnorm_tile(xv, wv):
    return xv * lax.rsqrt(jnp.mean(xv * xv, axis=-1, keepdims=True) + RMS_EPS) * wv


def _norm_in_proj(x, w, w_shift, w_conv, tm):
    t, d = x.shape
    n_a, n_b = w_shift.shape[1], w_conv.shape[1]

    def body(x_ref, w_ref, wa_ref, wb_ref, ht_ref, pa_ref, pb_ref):
        hv = _rmsnorm_tile(x_ref[...], w_ref[...])
        ht_ref[...] = jnp.transpose(hv).astype(BF16)
        hb = hv.astype(BF16)
        pa_ref[...] = jnp.dot(hb, wa_ref[...], preferred_element_type=F32)
        pb_ref[...] = jnp.dot(hb, wb_ref[...], preferred_element_type=F32)

    return pl.pallas_call(
        body, name="norm1_in_proj", grid=(t // tm,),
        in_specs=[_row(tm, d), _fixed((1, d)), _fixed((d, n_a)), _fixed((d, n_b))],
        out_specs=[_col(tm, d), _row(tm, n_a), _row(tm, n_b)],
        out_shape=[_sds((d, t), BF16), _sds((t, n_a)), _sds((t, n_b))],
        compiler_params=_cp("parallel"))(x, w, w_shift, w_conv)


def _out_proj_norm(mixed, w_out, res, w, tm):
    t, d = res.shape

    def body(m_ref, wo_ref, r_ref, w_ref, x_ref, h_ref, ht_ref):
        xv = r_ref[...] + jnp.dot(m_ref[...], wo_ref[...], preferred_element_type=F32)
        x_ref[...] = xv
        hv = _rmsnorm_tile(xv, w_ref[...])
        h_ref[...] = hv.astype(BF16)
        ht_ref[...] = jnp.transpose(hv).astype(BF16)

    return pl.pallas_call(
        body, name="out_proj_norm2", grid=(t // tm,),
        in_specs=[_row(tm, d), _fixed((d, d)), _row(tm, d), _fixed((1, d))],
        out_specs=[_row(tm, d), _row(tm, d), _col(tm, d)],
        out_shape=[_sds((t, d)), _sds((t, d), BF16), _sds((d, t), BF16)],
        compiler_params=_cp("parallel"))(mixed, w_out, res, w)


def _rms_bwd_math(xv, wv, dyv):
    rstd = lax.rsqrt(jnp.mean(xv * xv, axis=-1, keepdims=True) + RMS_EPS)
    xhat = xv * rstd
    gv = dyv * wv
    dx = rstd * (gv - xhat * jnp.mean(gv * xhat, axis=-1, keepdims=True))
    return dx, jnp.sum(dyv * xhat, axis=0, keepdims=True)


def _proj_norm_bwd(dy_a, w_a, dy_b, w_b, x, w, dres, tm, name):
    t, d = x.shape
    ka, kb = dy_a.shape[1], dy_b.shape[1]
    nt = (((1,), (1,)), ((), ()))

    def body(dya_ref, wa_ref, dyb_ref, wb_ref, x_ref, w_ref, dres_ref, dx_ref, dw_ref):
        d_h = (lax.dot_general(dya_ref[...], wa_ref[...], nt, preferred_element_type=F32)
               + lax.dot_general(dyb_ref[...], wb_ref[...], nt, preferred_element_type=F32))
        dx, dw = _rms_bwd_math(x_ref[...], w_ref[...], d_h)
        dx_ref[...] = dres_ref[...] + dx

        @pl.when(pl.program_id(0) == 0)
        def _():
            dw_ref[...] = jnp.zeros_like(dw_ref)

        dw_ref[...] += dw

    return pl.pallas_call(
        body, name=name, grid=(t // tm,),
        in_specs=[_row(tm, ka), _fixed((d, ka)), _row(tm, kb), _fixed((d, kb)), _row(tm, d), _fixed((1, d)),
                  _row(tm, d)],
        out_specs=[_row(tm, d), _fixed((1, d))],
        out_shape=[_sds((t, d)), _sds((1, d))], compiler_params=_cp("arbitrary"))(dy_a, w_a, dy_b, w_b, x, w, dres)


def _ffn_out_loss(act, w_down, x1, w, target, tm):
    t, d = x1.shape
    f = act.shape[1]

    def body(a_ref, wd_ref, x_ref, w_ref, t_ref, dx_ref, dx16_ref, dw_ref, loss_ref):
        xv = x_ref[...] + jnp.dot(a_ref[...], wd_ref[...], preferred_element_type=F32)
        wv = w_ref[...]
        rstd = lax.rsqrt(jnp.mean(xv * xv, axis=-1, keepdims=True) + RMS_EPS)
        err = xv * rstd * wv - t_ref[...]
        dx, dw = _rms_bwd_math(xv, wv, err * (1.0 / d))
        dx_ref[...] = dx
        dx16_ref[...] = dx.astype(BF16)

        @pl.when(pl.program_id(0) == 0)
        def _():
            dw_ref[...] = jnp.zeros_like(dw_ref)
            loss_ref[...] = jnp.zeros_like(loss_ref)

        dw_ref[...] += dw
        loss_ref[...] += 0.5 * jnp.sum(jnp.mean(err * err, axis=-1, keepdims=True), axis=0, keepdims=True)

    return pl.pallas_call(
        body, name="ffn_out_loss", grid=(t // tm,),
        in_specs=[_row(tm, f), _fixed((f, d)), _row(tm, d), _fixed((1, d)), _row(tm, d)],
        out_specs=[_row(tm, d), _row(tm, d), _fixed((1, d)), _fixed((1, 1))],
        out_shape=[_sds((t, d)), _sds((t, d), BF16), _sds((1, d)), _sds((1, 1))],
        compiler_params=_cp("arbitrary"))(act, w_down, x1, w, target)


def _ffn_in(h, w_gate, w_up):
    t, d = h.shape
    f = w_gate.shape[1]
    tm, tn = _tile(t, 512, 8), _tile(f, 1536)

    def body(h_ref, wg_ref, wu_ref, g_ref, u_ref, a_ref, at_ref):
        hv = h_ref[...]
        gv = jnp.dot(hv, wg_ref[...], preferred_element_type=F32)
        uv = jnp.dot(hv, wu_ref[...], preferred_element_type=F32)
        act = gv * jax.nn.sigmoid(gv) * uv
        g_ref[...] = gv.astype(BF16)
        u_ref[...] = uv.astype(BF16)
        a_ref[...] = act.astype(BF16)
        at_ref[...] = jnp.transpose(act).astype(BF16)

    w_spec = pl.BlockSpec((d, tn), lambda i, j: (0, j))
    o_spec = pl.BlockSpec((tm, tn), lambda i, j: (i, j))
    return pl.pallas_call(
        body, name="ffn_in", grid=(t // tm, f // tn),
        in_specs=[pl.BlockSpec((tm, d), lambda i, j: (i, 0)), w_spec, w_spec],
        out_specs=[o_spec, o_spec, o_spec, pl.BlockSpec((tn, tm), lambda i, j: (j, i))],
        out_shape=[_sds((t, f), BF16)] * 3 + [_sds((f, t), BF16)],
        compiler_params=_cp("parallel", "parallel"))(h, w_gate, w_up)


def _ffn_in_bwd(d_out, w_down, gate, up):
    t, d = d_out.shape
    f = w_down.shape[0]
    tm, tn = _tile(t, 512, 8), _tile(f, 1536)

    def body(do_ref, w_ref, g_ref, u_ref, dg_ref, du_ref):
        dv = lax.dot_general(do_ref[...].astype(BF16), w_ref[...], (((1,), (1,)), ((), ())),
                             preferred_element_type=F32)
        gv, uv = g_ref[...].astype(F32), u_ref[...].astype(F32)
        sg = jax.nn.sigmoid(gv)
        du_ref[...] = (dv * gv * sg).astype(BF16)
        dg_ref[...] = (dv * uv * (sg * (1.0 + gv * (1.0 - sg)))).astype(BF16)

    tile = pl.BlockSpec((tm, tn), lambda i, j: (i, j))
    return pl.pallas_call(
        body, name="ffn_in_bwd", grid=(t // tm, f // tn),
        in_specs=[pl.BlockSpec((tm, d), lambda i, j: (i, 0)), pl.BlockSpec((tn, d), lambda i, j: (j, 0)), tile, tile],
        out_specs=[tile, tile], out_shape=[_sds((t, f), BF16)] * 2,
        compiler_params=_cp("parallel", "parallel"))(d_out, w_down, gate, up)


def _halo_specs(tm, width, rows_total):
    per = tm // GROUP
    last = rows_total // GROUP - 1
    prev = pl.BlockSpec((GROUP, width), lambda i: (jnp.maximum(i * per - 1, 0), 0))
    nxt = pl.BlockSpec((GROUP, width), lambda i: (jnp.minimum((i + 1) * per, last), 0))
    return prev, nxt


def _edge_flags(tm, seq):
    i = pl.program_id(0)
    has_prev = jnp.where((i * tm) % seq == 0, 0.0, 1.0).astype(F32)
    has_next = jnp.where(((i + 1) * tm) % seq == 0, 0.0, 1.0).astype(F32)
    return has_prev, has_next


def _shifted(xv, prev_row, next_row):
    tm = xv.shape[0]
    row = lax.broadcasted_iota(jnp.int32, xv.shape, 0)
    down = jnp.where(row == 0, prev_row, pltpu.roll(xv, 1, axis=0))
    up = jnp.where(row == tm - 1, next_row, pltpu.roll(xv, tm - 1, axis=0))
    return down, up


def _shift_bwd(q, p, mu, tm, seq):
    t, w = p.shape
    prev_spec, next_spec = _halo_specs(tm, w, t)

    def body(q_ref, qp_ref, qn_ref, p_ref, pp_ref, pn_ref, mu_ref, dp_ref, dmu_ref):
        has_prev, has_next = _edge_flags(tm, seq)
        muv = mu_ref[0:1, :]
        qv = q_ref[...]
        mq = muv * qv
        mq_down, mq_up = _shifted(mq, muv * qp_ref[GROUP - 1:GROUP, :] * has_prev,
                                  muv * qn_ref[0:1, :] * has_next)
        dp_ref[...] = (qv - mq + 0.5 * (mq_down + mq_up)).astype(BF16)
        pv = p_ref[...]
        p_down, p_up = _shifted(pv, pp_ref[GROUP - 1:GROUP, :] * has_prev, pn_ref[0:1, :] * has_next)

        @pl.when(pl.program_id(0) == 0)
        def _():
            dmu_ref[...] = jnp.zeros_like(dmu_ref)

        dmu_ref[...] += jnp.sum(qv * (0.5 * (p_down + p_up) - pv), axis=0, keepdims=True)

    return pl.pallas_call(
        body, name="shift_bwd", grid=(t // tm,),
        in_specs=[_row(tm, w), prev_spec, next_spec, _row(tm, w), prev_spec, next_spec, _fixed((GROUP, w))],
        out_specs=[_row(tm, w), _fixed((1, w))],
        out_shape=[_sds((t, w), BF16), _sds((1, w))], compiler_params=_cp("arbitrary"))(q, q, q, p, p, p, mu)


@jax.custom_vjp
def _bdot(a, b):
    return jnp.dot(a.astype(BF16), b.astype(BF16), preferred_element_type=F32)


def _bdot_fwd(a, b):
    return _bdot(a, b), (a, b)


def _bdot_bwd(res, g):
    a, b = res
    gb = g.astype(BF16)
    da = lax.dot_general(gb, b.astype(BF16), (((1,), (1,)), ((), ())), preferred_element_type=F32)
    db = lax.dot_general(a.astype(BF16), gb, (((0,), (0,)), ((), ())), preferred_element_type=F32)
    return da, db


_bdot.defvjp(_bdot_fwd, _bdot_bwd)


def _seg_raw(x, ones_blocks):
    hi = x.astype(BF16)
    lo = (x - hi.astype(F32)).astype(BF16)
    return (jnp.dot(hi, ones_blocks, preferred_element_type=F32)
            + jnp.dot(lo, ones_blocks, preferred_element_type=F32))


@jax.custom_vjp
def _seg(x, ones_blocks):
    return _seg_raw(x, ones_blocks)


def _seg_fwd(x, ones_blocks):
    return _seg_raw(x, ones_blocks), ones_blocks


def _seg_bwd(ones_blocks, g):
    return _seg_raw(g, ones_blocks), jnp.zeros_like(ones_blocks)


_seg.defvjp(_seg_fwd, _seg_bwd)


def _head_ones():
    h = jnp.arange(D_RWKV) // HEAD
    return (h[:, None] == h[None, :]).astype(BF16)


def _prep_math(ps, k_k, w0_f, a0_f, k_a_f, w0_b, a0_b, k_a_b, wup_f, aup_f, wup_b, aup_b, gup, ones_blocks):
    r = ps[:, 0:512]
    k = ps[:, 512:1024]
    v = ps[:, 1024:1536]
    xwa = ps[:, 1536:1664]
    xg = ps[:, 1664:D_SHIFT_PAD]
    kk_raw = k * k_k
    norm = jnp.sqrt(_seg(kk_raw * kk_raw, ones_blocks))
    kk = kk_raw / jnp.maximum(norm, NORM_EPS)
    t_xwa = jnp.tanh(xwa)
    outs = [r, v, kk]
    for w0, a0, k_a, wup, aup in ((w0_f, a0_f, k_a_f, wup_f, aup_f), (w0_b, a0_b, k_a_b, wup_b, aup_b)):
        decay = jnp.exp(-LOG_DECAY_SCALE * jax.nn.sigmoid(w0 + _bdot(t_xwa, wup)))
        rate = jax.nn.sigmoid(a0 + _bdot(xwa, aup))
        outs += [decay, k * (1.0 + (rate - 1.0) * k_a), kk * rate]
    outs.append(_bdot(jax.nn.sigmoid(xg), gup))
    return tuple(outs)


def _prep_args(tm, ps_tile, pv_ref, mat_refs, ones_ref):
    vecs = [jnp.broadcast_to(pv_ref[j:j + 1, :], (tm, D_RWKV)) for j in range(7)]
    return [ps_tile] + vecs + [m[...] for m in mat_refs] + [ones_ref[...]]


_PREP_MAT_SHAPES = ((128, D_RWKV),) * 4 + ((D_GATE_PAD, D_RWKV),)


def _shift_prep_fwd(p, mu, pvec, mats, ones_blocks, tm, seq):
    t, w = p.shape
    prev_spec, next_spec = _halo_specs(tm, w, t)

    def body(p_ref, hp_ref, hn_ref, mu_ref, pv_ref, m0, m1, m2, m3, m4, ones_ref, ps_ref, *out_refs):
        has_prev, has_next = _edge_flags(tm, seq)
        xv = p_ref[...]
        down, up = _shifted(xv, hp_ref[GROUP - 1:GROUP, :] * has_prev, hn_ref[0:1, :] * has_next)
        ps_tile = xv + mu_ref[0:1, :] * (0.5 * (down + up) - xv)
        ps_ref[...] = ps_tile
        outs = _prep_math(*_prep_args(tm, ps_tile, pv_ref, (m0, m1, m2, m3, m4), ones_ref))
        for o_ref, val in zip(out_refs, outs[2:]):
            o_ref[...] = val

    return pl.pallas_call(
        body, name="shift_prep_fwd", grid=(t // tm,),
        in_specs=[_row(tm, w), prev_spec, next_spec, _fixed((GROUP, w)), _fixed((8, D_RWKV))]
        + [_fixed(s) for s in _PREP_MAT_SHAPES] + [_fixed((D_RWKV, D_RWKV))],
        out_specs=[_row(tm, w)] + [_row(tm, D_RWKV)] * 8, out_shape=[_sds((t, w))] + [_sds((t, D_RWKV))] * 8,
        compiler_params=_cp("parallel"))(p, p, p, mu, pvec, *mats, ones_blocks)


def _prep_bwd(ps, pvec, mats, ones_blocks, cts, tm):
    t = ps.shape[0]
    counts = [len(c) for c in cts]
    flat = [a for c in cts for a in c]

    def body(ps_ref, pv_ref, m0, m1, m2, m3, m4, ones_ref, *refs):
        ct_refs = refs[:len(flat)]
        q_ref, dpv_ref = refs[len(flat)], refs[len(flat) + 1]
        dmat_refs = refs[len(flat) + 2:]
        args = _prep_args(tm, ps_ref[...], pv_ref, (m0, m1, m2, m3, m4), ones_ref)
        _, vjp = jax.vjp(lambda *a: _prep_math(*a, args[-1]), *args[:-1])
        ct_vals, pos = [], 0
        for n in counts:
            val = ct_refs[pos][...]
            for extra in ct_refs[pos + 1:pos + n]:
                val = val + extra[...]
            ct_vals.append(val)
            pos += n
        grads = vjp(tuple(ct_vals))
        q_ref[...] = grads[0]

        @pl.when(pl.program_id(0) == 0)
        def _():
            dpv_ref[...] = jnp.zeros_like(dpv_ref)
            for d_ref in dmat_refs:
                d_ref[...] = jnp.zeros_like(d_ref)

        for j in range(7):
            dpv_ref[j:j + 1, :] += jnp.sum(grads[1 + j], axis=0, keepdims=True)
        for d_ref, gm in zip(dmat_refs, grads[8:13]):
            d_ref[...] += gm

    return pl.pallas_call(
        body, name="prep_bwd", grid=(t // tm,),
        in_specs=[_row(tm, D_SHIFT_PAD), _fixed((8, D_RWKV))] + [_fixed(s) for s in _PREP_MAT_SHAPES]
        + [_fixed((D_RWKV, D_RWKV))] + [_row(tm, D_RWKV)] * len(flat),
        out_specs=[_row(tm, D_SHIFT_PAD), _fixed((8, D_RWKV))] + [_fixed(s) for s in _PREP_MAT_SHAPES],
        out_shape=[_sds((t, D_SHIFT_PAD)), _sds((8, D_RWKV))] + [_sds(s) for s in _PREP_MAT_SHAPES],
        compiler_params=_cp("arbitrary"))(ps, pvec, *mats, ones_blocks, *flat)


def _pair_ones():
    h = jnp.arange(2 * HEAD) // HEAD
    block = (h[:, None] == h[None, :]).astype(BF16)
    return jnp.concatenate([block, block], axis=0)


def _diag_mask():
    lane = lax.broadcasted_iota(jnp.int32, (HEAD, 2 * HEAD), 1)
    sub = lax.broadcasted_iota(jnp.int32, (HEAD, 2 * HEAD), 0)
    return jnp.where((lane & (HEAD - 1)) == sub, 1.0, 0.0).astype(F32)


def _to_row(cols, dmask):
    return jnp.sum(cols * dmask, axis=0, keepdims=True)


def _seg_many(exact, rounded, ones_pair):
    out_exact, out_rounded = [], []
    if exact:
        parts = []
        for x in exact:
            hi = x.astype(BF16)
            parts.append(jnp.concatenate([hi, (x - hi.astype(F32)).astype(BF16)], axis=1))
        res = jnp.dot(jnp.concatenate(parts, axis=0), ones_pair, preferred_element_type=F32)
        out_exact = [res[HEAD * c:HEAD * (c + 1)] for c in range(len(exact))]
    if rounded:
        res = jnp.dot(jnp.concatenate([x.astype(BF16) for x in rounded], axis=0), ones_pair[0:2 * HEAD],
                      preferred_element_type=F32)
        out_rounded = [res[HEAD * c:HEAD * (c + 1)] for c in range(len(rounded))]
    return out_exact, out_rounded


N_CHAIN = 2 * N_PAIR


def _chain(c):
    d, p = divmod(c, N_PAIR)
    return d, slice(2 * HEAD * p, 2 * HEAD * (p + 1))


def _scan_specs(n_chunks, col_blocks, fwd_chunk, bwd_chunk):
    def spec(chunk_of, col):
        return pl.BlockSpec((SCAN_CHUNK, D_RWKV), lambda b, g: (b * n_chunks + chunk_of(g), col))
    return [spec(fwd_chunk, c) for c in col_blocks] + [spec(bwd_chunk, c) for c in col_blocks]


def _scan_fwd(ps, kk, dirs, batch, seq):
    t = batch * seq
    n = seq // SCAN_CHUNK
    groups = SCAN_CHUNK // GROUP
    up = lambda g: g
    down = lambda g: n - 1 - g
    col_blocks = (0, 2, 0, 0, 0, 0)

    def body(*refs):
        dir_refs = (refs[0:6], refs[6:12])
        ones_ref = refs[12]
        y_refs, hist_refs, st_ref = refs[13:15], refs[15:17], refs[17]

        @pl.when(pl.program_id(1) == 0)
        def _():
            st_ref[...] = jnp.zeros_like(st_ref)

        ones_pair = ones_ref[...]
        dmask = _diag_mask()
        dmask_b = dmask.astype(BF16)
        sub8 = lax.broadcasted_iota(jnp.int32, (GROUP, 2 * HEAD), 0)

        def group(gi, carry):
            off = (pl.multiple_of(gi * GROUP, GROUP), pl.multiple_of((groups - 1 - gi) * GROUP, GROUP))
            loaded = [tuple(ref[pl.ds(off[d], GROUP), :] for ref in dir_refs[d]) for d in range(2)]
            states = list(carry)
            y_acc = [jnp.zeros((GROUP, 2 * HEAD), F32) for _ in range(N_CHAIN)]
            for step in range(GROUP):
                rows, idx = [], []
                for c in range(N_CHAIN):
                    d, lanes = _chain(c)
                    i = step if d == 0 else GROUP - 1 - step
                    idx.append(i)
                    rows.append(tuple(x8[i:i + 1, lanes] for x8 in loaded[d]))
                    hist_refs[d][c % N_PAIR, gi * GROUP + step] = states[c]
                _, v_cols = _seg_many([], [dmask_b * rows[c][1].astype(BF16) for c in range(N_CHAIN)], ones_pair)
                sas, _ = _seg_many([states[c] * rows[c][2] for c in range(N_CHAIN)], [], ones_pair)
                for c in range(N_CHAIN):
                    _, _, _, w_row, kd_row, b_row = rows[c]
                    states[c] = states[c] * w_row - sas[c] * b_row + v_cols[c] * kd_row
                _, ys = _seg_many([], [states[c] * rows[c][0] for c in range(N_CHAIN)], ones_pair)
                for c in range(N_CHAIN):
                    y_acc[c] = jnp.where(sub8 == idx[c], _to_row(ys[c], dmask), y_acc[c])
            for c in range(N_CHAIN):
                d, lanes = _chain(c)
                y_refs[d][pl.ds(off[d], GROUP), lanes] = y_acc[c]
            return tuple(states)

        final = lax.fori_loop(0, groups, group, tuple(st_ref[c] for c in range(N_CHAIN)))
        for c in range(N_CHAIN):
            st_ref[c] = final[c]
            hist_refs[c // N_PAIR][c % N_PAIR, SCAN_CHUNK] = final[c]

    y_spec_f = pl.BlockSpec((SCAN_CHUNK, D_RWKV), lambda b, g: (b * n + up(g), 0))
    y_spec_b = pl.BlockSpec((SCAN_CHUNK, D_RWKV), lambda b, g: (b * n + down(g), 0))
    hist_shape = (batch, n, N_PAIR, SCAN_CHUNK + 1, HEAD, 2 * HEAD)
    hist_block = (None, None, N_PAIR, SCAN_CHUNK + 1, HEAD, 2 * HEAD)
    hist_spec_f = pl.BlockSpec(hist_block, lambda b, g: (b, up(g), 0, 0, 0, 0))
    hist_spec_b = pl.BlockSpec(hist_block, lambda b, g: (b, down(g), 0, 0, 0, 0))
    ones_spec = pl.BlockSpec((4 * HEAD, 2 * HEAD), lambda b, g: (0, 0))
    (wf, kdf, bf), (wb, kdb, bb) = dirs
    return pl.pallas_call(
        body, name="wkv_fwd", grid=(batch, n),
        in_specs=_scan_specs(n, col_blocks, up, down) + [ones_spec],
        out_specs=[y_spec_f, y_spec_b, hist_spec_f, hist_spec_b],
        out_shape=[_sds((t, D_RWKV)), _sds((t, D_RWKV)), _sds(hist_shape), _sds(hist_shape)],
        scratch_shapes=[pltpu.VMEM((N_CHAIN, HEAD, 2 * HEAD), F32)],
        compiler_params=_cp("parallel", "arbitrary"),
    )(ps, ps, kk, wf, kdf, bf, ps, ps, kk, wb, kdb, bb, _pair_ones())


def _scan_bwd(ps, kk, dirs, dy, hist_f, hist_b, batch, seq):
    t = batch * seq
    n = seq // SCAN_CHUNK
    groups = SCAN_CHUNK // GROUP
    fwd_chunk = lambda g: n - 1 - g
    bwd_chunk = lambda g: g
    col_blocks = (0, 2, 0, 0, 0, 0, 0)

    def undo_group(dir_refs, out_refs, hist_refs, gi, d_states, ones_pair, dmask, sub8):
        d_states = list(d_states)
        loaded, blocks = [], []
        for d in range(2):
            blk = groups - 1 - gi if d == 0 else gi
            blocks.append(pl.ds(pl.multiple_of(blk * GROUP, GROUP), GROUP))
            r8, v8, kk8, w8, kd8, b8, dy8 = (ref[blocks[d], :] for ref in dir_refs[d])
            loaded.append((r8, v8, kk8, w8, kd8, -b8, dy8))
        acc = [[jnp.zeros((GROUP, 2 * HEAD), F32) for _ in range(6)] for _ in range(N_CHAIN)]
        for step in range(GROUP):
            rows, idx, before, after = [], [], [], []
            for c in range(N_CHAIN):
                d, lanes = _chain(c)
                i = GROUP - 1 - step if d == 0 else step
                q = (groups - 1 - gi) * GROUP + i if d == 0 else SCAN_CHUNK - 1 - (gi * GROUP + i)
                idx.append(i)
                rows.append(tuple(x8[i:i + 1, lanes] for x8 in loaded[d]))
                before.append(hist_refs[d][c % N_PAIR, q])
                after.append(hist_refs[d][c % N_PAIR, q + 1])
            _, cols = _seg_many([], [dmask.astype(BF16) * rows[c][j].astype(BF16) for c in range(N_CHAIN) for j in (1, 6)],
                                ones_pair)
            v_cols, dy_cols = cols[0::2], cols[1::2]
            d_now = [d_states[c] + dy_cols[c] * rows[c][0] for c in range(N_CHAIN)]
            d_sas, _ = _seg_many([d_now[c] * rows[c][5] for c in range(N_CHAIN)], [], ones_pair)
            _, others = _seg_many(
                [], [x for c in range(N_CHAIN) for x in (before[c] * rows[c][2], d_now[c] * rows[c][4])], ones_pair)
            for c in range(N_CHAIN):
                sa, d_sa, dv_cols = others[2 * c], d_sas[c], others[2 * c + 1]
                rows_out = (
                    jnp.sum(after[c] * dy_cols[c], axis=0, keepdims=True),
                    jnp.sum(d_now[c] * before[c], axis=0, keepdims=True),
                    jnp.sum(d_now[c] * v_cols[c], axis=0, keepdims=True),
                    _to_row(dv_cols, dmask),
                    jnp.sum(before[c] * d_sa, axis=0, keepdims=True),
                    -jnp.sum(d_now[c] * sa, axis=0, keepdims=True),
                )
                acc[c] = [jnp.where(sub8 == idx[c], val, a) for val, a in zip(rows_out, acc[c])]
                d_states[c] = d_now[c] * rows[c][3] + d_sa * rows[c][2]
        for c in range(N_CHAIN):
            d, lanes = _chain(c)
            for o_ref, val in zip(out_refs[d], acc[c]):
                o_ref[blocks[d], lanes] = val
        return tuple(d_states)

    def body(*refs):
        dir_refs = (refs[0:7], refs[7:14])
        hist_refs, ones_ref = refs[14:16], refs[16]
        out_refs = (refs[17:23], refs[23:29])
        dst_ref = refs[29]

        @pl.when(pl.program_id(1) == 0)
        def _():
            dst_ref[...] = jnp.zeros_like(dst_ref)

        ones_pair = ones_ref[...]
        dmask = _diag_mask()
        sub8 = lax.broadcasted_iota(jnp.int32, (GROUP, 2 * HEAD), 0)

        def group(gi, carry):
            return undo_group(dir_refs, out_refs, hist_refs, gi, carry, ones_pair, dmask, sub8)

        final = lax.fori_loop(0, groups, group, tuple(dst_ref[c] for c in range(N_CHAIN)))
        for c in range(N_CHAIN):
            dst_ref[c] = final[c]

    blk = (SCAN_CHUNK, D_RWKV)
    out_f = pl.BlockSpec(blk, lambda b, g: (b * n + fwd_chunk(g), 0))
    out_b = pl.BlockSpec(blk, lambda b, g: (b * n + bwd_chunk(g), 0))
    hist_block = (None, None, N_PAIR, SCAN_CHUNK + 1, HEAD, 2 * HEAD)
    hist_spec_f = pl.BlockSpec(hist_block, lambda b, g: (b, fwd_chunk(g), 0, 0, 0, 0))
    hist_spec_b = pl.BlockSpec(hist_block, lambda b, g: (b, bwd_chunk(g), 0, 0, 0, 0))
    ones_spec = pl.BlockSpec((4 * HEAD, 2 * HEAD), lambda b, g: (0, 0))
    (wf, kdf, bf), (wb, kdb, bb) = dirs
    outs = pl.pallas_call(
        body, name="wkv_bwd", grid=(batch, n),
        in_specs=_scan_specs(n, col_blocks, fwd_chunk, bwd_chunk) + [hist_spec_f, hist_spec_b, ones_spec],
        out_specs=[out_f] * 6 + [out_b] * 6,
        out_shape=[_sds((t, D_RWKV))] * 12,
        scratch_shapes=[pltpu.VMEM((N_CHAIN, HEAD, 2 * HEAD), F32)],
        compiler_params=_cp("parallel", "arbitrary"),
    )(ps, ps, kk, wf, kdf, bf, dy, ps, ps, kk, wb, kdb, bb, dy, hist_f, hist_b, _pair_ones())
    return outs[0:6], outs[6:12]


def _post_math(y, r, kd_f, kd_b, v, gate, gn_w, gn_b, rk_f, rk_b, ones_blocks):
    mean = _seg(y, ones_blocks) * (1.0 / HEAD)
    yc = y - mean
    var = _seg(yc * yc, ones_blocks) * (1.0 / HEAD)
    yn = yc * lax.rsqrt(var + GN_EPS) * gn_w + gn_b
    bonus = _seg(r * kd_f * rk_f, ones_blocks) * v + _seg(r * kd_b * rk_b, ones_blocks) * v
    return (yn + bonus) * gate


def _conv_parts(pc, halo_prev, halo_next, has_prev, has_next):
    gate_b, gate_c, hid = pc[:, 0:512], pc[:, 512:1024], pc[:, 1024:1536]
    u = gate_c * hid
    u_prev_row = halo_prev[GROUP - 1:GROUP, 512:1024] * halo_prev[GROUP - 1:GROUP, 1024:1536] * has_prev
    u_next_row = halo_next[0:1, 512:1024] * halo_next[0:1, 1024:1536] * has_next
    u_down, u_up = _shifted(u, u_prev_row, u_next_row)
    return gate_b, gate_c, hid, u, u_down, u_up


def _post_specs(tm, t):
    pc_prev, pc_next = _halo_specs(tm, D_CONV3, t)
    col = lambda c: pl.BlockSpec((tm, D_RWKV), lambda i: (i, c))
    return ([col(0), col(0), col(0), col(0), col(0), col(2), col(0), _row(tm, D_CONV3), pc_prev, pc_next,
             _fixed((8, D_RWKV)), _fixed((D_RWKV, D_RWKV))])


def _post_fwd(y_f, y_b, ps, kd_f, kd_b, gate, pc, qvec, ones_blocks, tm, seq):
    t = ps.shape[0]

    def body(yf_ref, yb_ref, r_ref, kdf_ref, kdb_ref, v_ref, g_ref, pc_ref, hp_ref, hn_ref, qv_ref, ones_ref,
             o_ref, ot_ref):
        has_prev, has_next = _edge_flags(tm, seq)
        vec = [jnp.broadcast_to(qv_ref[j:j + 1, :], (tm, D_RWKV)) for j in range(7)]
        o_rwkv = _post_math(yf_ref[...] + yb_ref[...], r_ref[...], kdf_ref[...], kdb_ref[...], v_ref[...],
                            g_ref[...], vec[0], vec[1], vec[2], vec[3], ones_ref[...])
        gate_b, _, _, u, u_down, u_up = _conv_parts(pc_ref[...], hp_ref[...], hn_ref[...], has_prev, has_next)
        o_conv = gate_b * (vec[4] * u_down + vec[5] * u + vec[6] * u_up)
        for half, val in enumerate((o_rwkv, o_conv)):
            o_ref[:, D_RWKV * half:D_RWKV * (half + 1)] = val.astype(BF16)
            ot_ref[D_RWKV * half:D_RWKV * (half + 1), :] = jnp.transpose(val).astype(BF16)

    return pl.pallas_call(
        body, name="post_fwd", grid=(t // tm,), in_specs=_post_specs(tm, t),
        out_specs=[_row(tm, D_MODEL), _col(tm, D_MODEL)],
        out_shape=[_sds((t, D_MODEL), BF16), _sds((D_MODEL, t), BF16)], compiler_params=_cp("parallel"),
    )(y_f, y_b, ps, kd_f, kd_b, ps, gate, pc, pc, pc, qvec, ones_blocks)


def _post_bwd(d_out, y_f, y_b, ps, kd_f, kd_b, gate, pc, qvec, ones_blocks, tm, seq):
    t = ps.shape[0]
    do_prev, do_next = _halo_specs(tm, D_MODEL, t)

    def body(do_ref, dop_ref, don_ref, yf_ref, yb_ref, r_ref, kdf_ref, kdb_ref, v_ref, g_ref, pc_ref, hp_ref,
             hn_ref, qv_ref, ones_ref, dy_ref, dr_ref, dkdf_ref, dkdb_ref, dv_ref, dg_ref, dpc_ref, dqv_ref):
        has_prev, has_next = _edge_flags(tm, seq)
        vec = [jnp.broadcast_to(qv_ref[j:j + 1, :], (tm, D_RWKV)) for j in range(7)]
        ones_v = ones_ref[...]
        args = (yf_ref[...] + yb_ref[...], r_ref[...], kdf_ref[...], kdb_ref[...], v_ref[...], g_ref[...],
                vec[0], vec[1], vec[2], vec[3])
        _, vjp = jax.vjp(lambda *a: _post_math(*a, ones_v), *args)
        grads = vjp(do_ref[:, 0:D_RWKV])
        for o_ref, gval in zip((dy_ref, dr_ref, dkdf_ref, dkdb_ref, dv_ref, dg_ref), grads[0:6]):
            o_ref[...] = gval

        hp, hn = hp_ref[...], hn_ref[...]
        gate_b, gate_c, hid, u, u_down, u_up = _conv_parts(pc_ref[...], hp, hn, has_prev, has_next)
        d_oc = do_ref[:, D_RWKV:2 * D_RWKV]
        d_cu = d_oc * gate_b
        d_cu_prev = dop_ref[GROUP - 1:GROUP, D_RWKV:2 * D_RWKV] * hp[GROUP - 1:GROUP, 0:512] * has_prev
        d_cu_next = don_ref[0:1, D_RWKV:2 * D_RWKV] * hn[0:1, 0:512] * has_next
        d_cu_down, d_cu_up = _shifted(d_cu, d_cu_prev, d_cu_next)
        d_u = vec[5] * d_cu + vec[4] * d_cu_up + vec[6] * d_cu_down
        dpc_ref[:, 0:512] = (d_oc * (vec[4] * u_down + vec[5] * u + vec[6] * u_up)).astype(BF16)
        dpc_ref[:, 512:1024] = (d_u * hid).astype(BF16)
        dpc_ref[:, 1024:1536] = (d_u * gate_c).astype(BF16)

        @pl.when(pl.program_id(0) == 0)
        def _():
            dqv_ref[...] = jnp.zeros_like(dqv_ref)

        vec_grads = list(grads[6:10]) + [d_cu * u_down, d_cu * u, d_cu * u_up]
        for j, gval in enumerate(vec_grads):
            dqv_ref[j:j + 1, :] += jnp.sum(gval, axis=0, keepdims=True)

    return pl.pallas_call(
        body, name="post_bwd", grid=(t // tm,),
        in_specs=[_row(tm, D_MODEL), do_prev, do_next] + _post_specs(tm, t),
        out_specs=[_row(tm, D_RWKV)] * 6 + [_row(tm, D_CONV3), _fixed((8, D_RWKV))],
        out_shape=[_sds((t, D_RWKV))] * 6 + [_sds((t, D_CONV3), BF16), _sds((8, D_RWKV))],
        compiler_params=_cp("arbitrary"),
    )(d_out, d_out, d_out, y_f, y_b, ps, kd_f, kd_b, ps, gate, pc, pc, pc, qvec, ones_blocks)


def _adamw_math(wv, gv, mv, vv):
    m2 = ADAM_B1 * mv + (1.0 - ADAM_B1) * gv
    v2 = ADAM_B2 * vv + (1.0 - ADAM_B2) * (gv * gv)
    m_hat = m2 / (1.0 - ADAM_B1 ** ADAM_STEP)
    v_hat = v2 / (1.0 - ADAM_B2 ** ADAM_STEP)
    return -ADAM_LR * (m_hat / (jnp.sqrt(v_hat) + ADAM_EPS) + ADAM_WD * wv), m2, v2


def _adamw_small(items):
    n = len(items)

    def body(*refs):
        ins, outs = refs[:4 * n], refs[4 * n:]
        for k in range(n):
            w_ref, g_ref, m_ref, v_ref = ins[4 * k:4 * k + 4]
            for o_ref, val in zip(outs[3 * k:3 * k + 3], _adamw_math(w_ref[...], g_ref[...], m_ref[...], v_ref[...])):
                o_ref[...] = val

    flat = [a for item in items for a in item]
    outs = pl.pallas_call(
        body, name="adamw_small", out_shape=[_sds(item[0].shape) for item in items for _ in range(3)],
        compiler_params=_cp())(*flat)
    return [tuple(outs[3 * k:3 * k + 3]) for k in range(n)]


def _adamw(w, g, m, v, name):
    r, c = w.shape[-2:]
    tr = _tile(r, 256, 8)
    if w.ndim == 3:
        spec = pl.BlockSpec((None, tr, c), lambda i: (0, i, 0))
    else:
        spec = pl.BlockSpec((tr, c), lambda i: (i, 0))

    def body(w_ref, g_ref, m_ref, v_ref, d_ref, nm_ref, nv_ref):
        d_ref[...], nm_ref[...], nv_ref[...] = _adamw_math(w_ref[...], g_ref[...], m_ref[...], v_ref[...])

    return pl.pallas_call(
        body, name=name, grid=(r // tr,), in_specs=[spec] * 4, out_specs=[spec] * 3,
        out_shape=[_sds(w.shape)] * 3, compiler_params=_cp("parallel"))(w, g, m, v)


_ANY = pl.BlockSpec(memory_space=pl.ANY)


def _place():
    return lax.axis_index("x"), lax.axis_index("y"), lax.axis_index("c")


def _other_chips(x, y):
    return [(1 - x, y), (x, 1 - y), (1 - x, 1 - y)]


def _remote(src, dst, send_sems, recv_sems, k, to):
    return pltpu.make_async_remote_copy(src_ref=src, dst_ref=dst, send_sem=send_sems.at[k],
                                        recv_sem=recv_sems.at[k], device_id=to, device_id_type=MESH)


def _gather_weights(pack):
    rows, width = pack.shape
    half = rows // 2

    def body(x_ref, out_ref, send_sems, recv_sems):
        x, y, c = _place()
        sibling = (x, y, 1 - c)
        chips = _other_chips(x, y)

        def block(chip, part):
            return out_ref.at[2 * chip[0] + chip[1], pl.ds(part * half, half), :]

        first = [_remote(x_ref.at[pl.ds(c * half, half), :], block((x, y), c), send_sems, recv_sems, j, (*chip, c))
                 for j, chip in enumerate(chips)]
        for cp in first:
            cp.start()
        passed = [_remote(block(chip, c), block(chip, c), send_sems, recv_sems, 3 + j, sibling)
                  for j, chip in enumerate(chips)]
        for j, chip in enumerate(chips):
            _remote(block(chip, c), block(chip, c), send_sems, recv_sems, j, sibling).wait_recv()
            passed[j].start()
        for j, chip in enumerate(chips):
            _remote(block(chip, 1 - c), block(chip, 1 - c), send_sems, recv_sems, 3 + j, sibling).wait_recv()
        for cp in first + passed:
            cp.wait_send()

    return pl.pallas_call(
        body, name="gather_weights", in_specs=[_ANY], out_specs=_ANY,
        out_shape=_sds((N_SHARD, rows, width), pack.dtype),
        scratch_shapes=[pltpu.SemaphoreType.DMA((6,)), pltpu.SemaphoreType.DMA((6,))],
    )(pack)


_HBM = pl.BlockSpec(memory_space=pltpu.HBM)
_SEMS = pl.BlockSpec(memory_space=pltpu.SEMAPHORE)
_DATAFLOW = pltpu.SideEffectType.DATAFLOW_SIDE_EFFECTING


def _fetch_start(pack, after):
    def body(x_ref, land_ref, after_ref, send_sems, recv_sems, x_thru, land_thru, token):
        x, y, c = _place()
        for j, chip in enumerate(_other_chips(x, y)):
            _remote(x_ref, land_ref.at[2 * x + y], send_sems, recv_sems, j, (*chip, c)).start()
        token[...] = jnp.zeros_like(token)

    land = lax.empty((N_SHARD,) + pack.shape, pack.dtype)
    return pl.pallas_call(
        body, name="fetch_ffn_start",
        out_shape=(pltpu.SemaphoreType.DMA((3,)), pltpu.SemaphoreType.DMA((3,)), pltpu.HBM(pack.shape, pack.dtype),
                   pltpu.HBM(land.shape, land.dtype), _sds((8, 128))),
        in_specs=(_HBM, _HBM, _ANY), out_specs=(_SEMS, _SEMS, _HBM, _HBM, pl.BlockSpec(memory_space=pltpu.VMEM)),
        input_output_aliases={0: 2, 1: 3}, compiler_params=pltpu.CompilerParams(has_side_effects=_DATAFLOW),
    )(pltpu.with_memory_space_constraint(pack, pltpu.HBM), pltpu.with_memory_space_constraint(land, pltpu.HBM), after)


def _fetch_wait(send_sems, recv_sems, pack_thru, land_thru, after):
    def body(x_ref, land_ref, send_sems, recv_sems, after_ref, x_dead, got_ref):
        x, y, c = _place()
        for j, chip in enumerate(_other_chips(x, y)):
            cp = _remote(x_ref, land_ref.at[2 * chip[0] + chip[1]], send_sems, recv_sems, j, (*chip, c))
            cp.wait_send()
            cp.wait_recv()

    return pl.pallas_call(
        body, name="fetch_ffn_wait",
        out_shape=(pltpu.HBM(pack_thru.shape, pack_thru.dtype), pltpu.HBM(land_thru.shape, land_thru.dtype)),
        in_specs=(_HBM, _HBM, _SEMS, _SEMS, _ANY), out_specs=(_HBM, _HBM), input_output_aliases={0: 0, 1: 1},
        compiler_params=pltpu.CompilerParams(has_side_effects=_DATAFLOW),
    )(pack_thru, land_thru, send_sems, recv_sems, after)[1]


def _swap_with_sibling(block, name):
    def body(x_ref, out_ref, send_sems, recv_sems):
        x, y, c = _place()
        cp = _remote(x_ref, out_ref, send_sems, recv_sems, 0, (x, y, 1 - c))
        cp.start()
        cp.wait()

    return pl.pallas_call(
        body, name=name, in_specs=[_ANY], out_specs=_ANY, out_shape=_sds(block.shape, block.dtype),
        scratch_shapes=[pltpu.SemaphoreType.DMA((1,)), pltpu.SemaphoreType.DMA((1,))],
    )(block)


def _swap_other_half(packed, tag):
    slots, rows, width = packed.shape
    half = rows // 2

    def body(x_ref, out_ref, send_sems, recv_sems):
        x, y, c = _place()
        cp = _remote(x_ref.at[:, pl.ds((1 - c) * half, half), :], out_ref, send_sems, recv_sems, 0, (x, y, 1 - c))
        cp.start()
        cp.wait()

    return pl.pallas_call(
        body, name="swap_halves_" + tag, in_specs=[_ANY], out_specs=_ANY, out_shape=_sds((slots, half, width)),
        scratch_shapes=[pltpu.SemaphoreType.DMA((1,)), pltpu.SemaphoreType.DMA((1,))],
    )(packed)


def _add_halves(packed, got, c, tag):
    slots, rows, width = packed.shape
    half = rows // 2
    tr = _tile(half, 408, 16)
    per = half // tr
    block = (None, tr, width)

    def body(c_ref, mine_ref, got_ref, sum_ref, sum16_ref):
        acc = mine_ref[...] + got_ref[...]
        sum_ref[...] = acc
        sum16_ref[...] = acc.astype(BF16)

    plain = pl.BlockSpec(block, lambda s, i, c_ref: (s, i, 0))
    grid_spec = pltpu.PrefetchScalarGridSpec(
        num_scalar_prefetch=1, grid=(slots, per),
        in_specs=[pl.BlockSpec(block, lambda s, i, c_ref: (s, c_ref[0] * per + i, 0)), plain],
        out_specs=[plain, plain])
    return pl.pallas_call(
        body, name="add_halves_" + tag, grid_spec=grid_spec,
        out_shape=[_sds((slots, half, width)), _sds((slots, half, width), BF16)],
        compiler_params=_cp("parallel", "parallel"))(c.reshape(1).astype(jnp.int32), packed, got)


def _add_quarters(chip_sum, others, chip, tag):
    _, rows, width = chip_sum.shape
    tr = _tile(rows, 408, 16)

    def body(chip_ref, own_ref, others_ref, o_ref):
        acc = own_ref[...]
        for j in range(3):
            acc = acc + others_ref[j].astype(F32)
        o_ref[...] = acc

    grid_spec = pltpu.PrefetchScalarGridSpec(
        num_scalar_prefetch=1, grid=(rows // tr,),
        in_specs=[pl.BlockSpec((None, tr, width), lambda i, chip_ref: (chip_ref[0], i, 0)),
                  pl.BlockSpec((3, tr, width), lambda i, chip_ref: (0, i, 0))],
        out_specs=pl.BlockSpec((tr, width), lambda i, chip_ref: (i, 0)))
    return pl.pallas_call(
        body, name="add_quarters_" + tag, grid_spec=grid_spec, out_shape=_sds((rows, width)),
        compiler_params=_cp("parallel"))(chip.reshape(1).astype(jnp.int32), chip_sum, others)


def _exchange_start(parts, tag):
    _, rows, width = parts.shape

    def body(x_ref, land_ref, send_sems, recv_sems, x_thru, land_thru, token):
        x, y, c = _place()
        for j, chip in enumerate(_other_chips(x, y)):
            _remote(x_ref.at[2 * chip[0] + chip[1]], land_ref.at[j], send_sems, recv_sems, j, (*chip, c)).start()
        token[...] = jnp.zeros_like(token)

    land = lax.empty((3, rows, width), parts.dtype)
    return pl.pallas_call(
        body, name="exchange_" + tag + "_start",
        out_shape=(pltpu.SemaphoreType.DMA((3,)), pltpu.SemaphoreType.DMA((3,)), pltpu.HBM(parts.shape, parts.dtype),
                   pltpu.HBM(land.shape, land.dtype), _sds((8, 128))),
        in_specs=(_HBM, _HBM), out_specs=(_SEMS, _SEMS, _HBM, _HBM, pl.BlockSpec(memory_space=pltpu.VMEM)),
        input_output_aliases={0: 2, 1: 3}, compiler_params=pltpu.CompilerParams(has_side_effects=_DATAFLOW),
    )(pltpu.with_memory_space_constraint(parts, pltpu.HBM), pltpu.with_memory_space_constraint(land, pltpu.HBM))


def _exchange_wait(send_sems, recv_sems, parts_thru, land_thru, after, tag):
    def body(x_ref, land_ref, send_sems, recv_sems, after_ref, x_dead, got_ref):
        x, y, c = _place()
        for j, chip in enumerate(_other_chips(x, y)):
            cp = _remote(x_ref.at[2 * chip[0] + chip[1]], land_ref.at[j], send_sems, recv_sems, j, (*chip, c))
            cp.wait_send()
            cp.wait_recv()

    return pl.pallas_call(
        body, name="exchange_" + tag + "_wait",
        out_shape=(pltpu.HBM(parts_thru.shape, parts_thru.dtype), pltpu.HBM(land_thru.shape, land_thru.dtype)),
        in_specs=(_HBM, _HBM, _SEMS, _SEMS, _ANY), out_specs=(_HBM, _HBM), input_output_aliases={0: 0, 1: 1},
        compiler_params=pltpu.CompilerParams(has_side_effects=_DATAFLOW),
    )(parts_thru, land_thru, send_sems, recv_sems, after)[1]


def _allreduce_small(vec):
    rows, width = vec.shape
    vmem = pl.BlockSpec(memory_space=pltpu.VMEM)

    def body(x_ref, o_ref, buf_ref, send_sems, recv_sems):
        x, y, c = _place()
        me = 4 * x + 2 * y + c
        buf_ref[me] = x_ref[...]
        copies = []
        for k in range(1, N_DEV):
            peer = (x ^ ((k >> 2) & 1), y ^ ((k >> 1) & 1), c ^ (k & 1))
            copies.append(_remote(x_ref, buf_ref.at[me], send_sems, recv_sems, k - 1, peer))
        for cp in copies:
            cp.start()
        for k in range(1, N_DEV):
            _remote(x_ref, buf_ref.at[me ^ k], send_sems, recv_sems, k - 1, (x, y, c)).wait_recv()
        for cp in copies:
            cp.wait_send()
        total = buf_ref[0]
        for d in range(1, N_DEV):
            total = total + buf_ref[d]
        o_ref[...] = total

    return pl.pallas_call(
        body, name="allreduce_small", in_specs=[vmem], out_specs=vmem, out_shape=_sds((rows, width)),
        scratch_shapes=[pltpu.VMEM((N_DEV, rows, width), F32), pltpu.SemaphoreType.DMA((N_DEV - 1,)),
                        pltpu.SemaphoreType.DMA((N_DEV - 1,))],
    )(vec)


def _rows1024(a):
    return a.reshape(-1, 1024)


def _pad_rows(a, rows):
    return jnp.concatenate([a, jnp.zeros((rows - a.shape[0], a.shape[1]), a.dtype)], axis=0)


_TRANSPOSED = ("w_in", "w_gate", "w_up")
_SMALL_SHARDED = ("w_up_f", "w_up_b", "a_up_f", "a_up_b", "g_up")
_BIG_SHARDED = ("w_in", "w_out", "w_gate", "w_up", "w_down")


def _pack_weight_shards(w):
    conv_bits = lax.bitcast_convert_type(w["conv_w"], BF16).reshape(1, -1)
    conv_row = jnp.concatenate([conv_bits, jnp.zeros((1, 1024 - conv_bits.shape[1]), BF16)], axis=1)

    def rows(name):
        a = w[name].astype(BF16)
        return a.T if name in _TRANSPOSED else _rows1024(a)

    early = _pad_rows(jnp.concatenate([rows(name) for name, _ in _EARLY_ROWS[:-1]] + [conv_row], axis=0), EARLY_R)
    return early, jnp.concatenate([rows(name) for name, _ in _FFN_ROWS], axis=0)


def _split_rows(gathered, layout):
    out, row = {}, 0
    for name, n in layout:
        out[name] = gathered[:, row:row + n]
        row += n
    return out


def _unpack_early(gathered):
    out = _split_rows(gathered, _EARLY_ROWS)
    cols = lambda a, k: jnp.concatenate([a[s].reshape(k, -1) for s in range(N_SHARD)], axis=1)
    conv = lax.bitcast_convert_type(out["conv_w"][:, 0, :768].reshape(N_SHARD, 3, 128, 2), F32)
    full = dict(w_in=out["w_in"].reshape(-1, 1024).T, w_out=out["w_out"].reshape(D_MODEL, D_MODEL),
                conv_w=jnp.concatenate([conv[s] for s in range(N_SHARD)], axis=1))
    full.update({name: cols(out[name], D_GATE if name == "g_up" else D_LORA) for name in _SMALL_SHARDED})
    return full


def _unpack_ffn(gathered):
    out = _split_rows(gathered, _FFN_ROWS)
    return dict(w_gate=out["w_gate"].reshape(-1, 1024).T, w_up=out["w_up"].reshape(-1, 1024).T,
                w_down=out["w_down"].reshape(D_FF, D_MODEL))


def _pack_grads(g, layout, rows):
    col_split = lambda a, s: a[:, s * (a.shape[1] // N_SHARD):(s + 1) * (a.shape[1] // N_SHARD)]
    row_split = lambda a, s: a[s * (a.shape[0] // N_SHARD):(s + 1) * (a.shape[0] // N_SHARD)]
    by_rows = {name: (g[name].T if name in _TRANSPOSED else g[name]) for name, _ in layout if name in _BIG_SHARDED}
    used = sum(n for _, n in layout)
    parts = []
    for s in range(N_SHARD):
        for name, _ in layout:
            if name in _BIG_SHARDED:
                parts.append(row_split(by_rows[name], s))
            elif name in _SMALL_SHARDED:
                parts.append(_rows1024(col_split(g[name], s)))
            else:
                conv = col_split(g["conv_w"], s).reshape(1, -1)
                parts.append(jnp.concatenate([conv, jnp.zeros((1, 1024 - conv.shape[1]), F32)], axis=1))
        if rows > used:
            parts.append(jnp.zeros((rows - used, 1024), F32))
    return jnp.concatenate(parts, axis=0).reshape(N_SHARD, rows, 1024)


def _unpack_grad_shard(pack, layout):
    small_shapes = {name: (D_GATE if name == "g_up" else D_LORA, 128) for name in _SMALL_SHARDED}
    out, row = {}, 0
    for name, n in layout:
        piece = pack[row:row + n]
        if name == "conv_w":
            out[name] = piece[0, :384].reshape(3, 128)
        else:
            out[name] = piece.T if name in _TRANSPOSED else piece.reshape(small_shapes.get(name, piece.shape))
        row += n
    return out


_SMALL_LAYOUT = (("norm1_w", 1024), ("mu_shift", D_SHIFT), ("w0_f", 512), ("w0_b", 512), ("a0_f", 512),
                 ("a0_b", 512), ("k_k", 512), ("k_a_f", 512), ("k_a_b", 512), ("r_k_f", 512), ("r_k_b", 512),
                 ("gn_w", 512), ("gn_b", 512), ("norm2_w", 1024), ("norm_f_w", 1024), ("loss", 1))


def _pack_small(vals):
    rows = []
    for name, n in _SMALL_LAYOUT:
        flat = vals[name].reshape(-1)
        n_rows = -(-n // 1024)
        rows.append(jnp.concatenate([flat, jnp.zeros((n_rows * 1024 - n,), F32)]).reshape(n_rows, 1024))
    return _pad_rows(jnp.concatenate(rows, axis=0), SMALL_ROWS)


def _unpack_small(pack):
    out, row = {}, 0
    for name, n in _SMALL_LAYOUT:
        n_rows = -(-n // 1024)
        out[name] = pack[row:row + n_rows].reshape(-1)[:n]
        row += n_rows
    return out


_WEIGHTS = ("norm1_w", "w_in", "mu_shift", "w_up_f", "w0_f", "w_up_b", "w0_b", "a_up_f", "a0_f", "a_up_b", "a0_b",
            "g_up", "k_k", "k_a_f", "k_a_b", "r_k_f", "r_k_b", "gn_w", "gn_b", "conv_w", "w_out", "norm2_w",
            "w_gate", "w_up", "w_down", "norm_f_w")


def _train_step(x, loss_target, w, m, v):
    batch, seq, _ = x.shape
    t = batch * seq
    tm = _tile(seq, 256, 8)
    xs = x.reshape(t, D_MODEL)
    target = loss_target.reshape(t, D_MODEL)
    vec = lambda name: w[name].reshape(1, -1)

    local = {name: w[name][0] for name, _ in _PACK_ROWS}
    c = lax.axis_index("c")
    chip = 2 * lax.axis_index("x") + lax.axis_index("y")
    early, ffn_pack = _pack_weight_shards(local)
    early_all = lax.dynamic_update_slice(_gather_weights(early), early[None], (chip, 0, 0))
    ffn_send, ffn_recv, ffn_pack, ffn_land, token = _fetch_start(ffn_pack, early_all)
    full = _unpack_early(early_all)
    w_in = full["w_in"]
    w_shift = jnp.concatenate([w_in[:, :D_SHIFT], jnp.zeros((D_MODEL, D_SHIFT_PAD - D_SHIFT), BF16)], axis=1)
    w_conv = w_in[:, D_SHIFT:]
    zeros_lora = jnp.zeros((D_LORA, D_RWKV), F32)
    lora = lambda name: full[name].astype(F32)
    mats = (jnp.concatenate([lora("w_up_f"), zeros_lora]), jnp.concatenate([zeros_lora, lora("a_up_f")]),
            jnp.concatenate([lora("w_up_b"), zeros_lora]), jnp.concatenate([zeros_lora, lora("a_up_b")]),
            jnp.concatenate([lora("g_up"), jnp.zeros((D_GATE_PAD - D_GATE, D_RWKV), F32)]))
    mu = jnp.concatenate([vec("mu_shift"), jnp.zeros((1, D_SHIFT_PAD - D_SHIFT), F32)], axis=1)
    mu = jnp.broadcast_to(mu, (GROUP, D_SHIFT_PAD))
    zero_row = jnp.zeros((1, D_RWKV), F32)
    pvec = jnp.concatenate([vec("k_k"), vec("w0_f"), vec("a0_f"), vec("k_a_f"), vec("w0_b"), vec("a0_b"),
                            vec("k_a_b"), zero_row], axis=0)
    qvec = jnp.concatenate([vec("gn_w"), vec("gn_b"), vec("r_k_f"), vec("r_k_b"), full["conv_w"], zero_row], axis=0)
    ones_blocks = _head_ones()

    h1_t, p_shift, pc = _norm_in_proj(xs, vec("norm1_w") + token[0, 0], w_shift, w_conv, tm)
    ps, kk, w_f, kd_f, b_f, w_b, kd_b, b_b, gate = _shift_prep_fwd(p_shift, mu, pvec, mats, ones_blocks, tm, seq)
    dirs = ((w_f, kd_f, b_f), (w_b, kd_b, b_b))
    y_f, y_b, hist_f, hist_b = _scan_fwd(ps, kk, dirs, batch, seq)
    mixed, mixed_t = _post_fwd(y_f, y_b, ps, kd_f, kd_b, gate, pc, qvec, ones_blocks, tm, seq)
    x1, h2, h2_t = _out_proj_norm(mixed, full["w_out"], xs, vec("norm2_w"), tm)
    ffn_all = _fetch_wait(ffn_send, ffn_recv, ffn_pack, ffn_land, h2)
    full.update(_unpack_ffn(lax.dynamic_update_slice(ffn_all, ffn_pack[None], (chip, 0, 0))))
    ff_gate, ff_up, act, act_t = _ffn_in(h2, full["w_gate"], full["w_up"])
    d_x2, d_x2_bf16, d_norm_f, loss_part = _ffn_out_loss(act, full["w_down"], x1, w["norm_f_w"].reshape(1, -1),
                                                         target, tm)

    g = {}
    g["w_down"] = _matmul(act_t, d_x2_bf16, mode="nn", name="ffn_down_dw")
    d_gate, d_up = _ffn_in_bwd(d_x2_bf16, full["w_down"], ff_gate, ff_up)
    g["w_gate"] = _matmul(h2_t, d_gate, mode="nn", name="ffn_gate_dw")
    g["w_up"] = _matmul(h2_t, d_up, mode="nn", name="ffn_up_dw")
    ffn_grads = _pack_grads(g, _FFN_ROWS, sum(n for _, n in _FFN_ROWS))
    ffn_sum, ffn_sum_bf16 = _add_halves(ffn_grads, _swap_other_half(ffn_grads, "ffn"), c, "ffn")
    ex_send, ex_recv, ffn_sum_bf16, ex_land, ex_token = _exchange_start(ffn_sum_bf16, "ffn")
    d_x1, d_norm2 = _proj_norm_bwd(d_gate, full["w_gate"], d_up, full["w_up"], x1, vec("norm2_w") + ex_token[0, 0],
                                   d_x2, tm, "ffn_in_dx_norm2_bwd")
    d_mixed = _matmul(d_x1, full["w_out"], mode="nt", name="out_proj_dx")
    g["w_out"] = _matmul(mixed_t, d_x1, mode="nn", name="out_proj_dw")
    dy, dr_o, dkdf_o, dkdb_o, dv_o, d_gatev, d_pc, d_qvec = _post_bwd(
        d_mixed, y_f, y_b, ps, kd_f, kd_b, gate, pc, qvec, ones_blocks, tm, seq)
    (dr_f, dw_f, dkd_f, dv_f, dkk_f, db_f), (dr_b, dw_b, dkd_b, dv_b, dkk_b, db_b) = _scan_bwd(
        ps, kk, dirs, dy, hist_f, hist_b, batch, seq)
    cts = [[dr_f, dr_b, dr_o], [dv_f, dv_b, dv_o], [dkk_f, dkk_b], [dw_f], [dkd_f, dkdf_o], [db_f],
           [dw_b], [dkd_b, dkdb_o], [db_b], [d_gatev]]
    q, d_pvec, d_m0, d_m1, d_m2, d_m3, d_m4 = _prep_bwd(ps, pvec, mats, ones_blocks, cts, tm)
    d_pshift, d_mu = _shift_bwd(q, p_shift, mu, tm, seq)
    d_w_shift = _matmul(h1_t, d_pshift, mode="nn", name="in_proj_shift_dw")
    d_w_conv = _matmul(h1_t, d_pc, mode="nn", name="in_proj_conv_dw")
    g["w_in"] = jnp.concatenate([d_w_shift[:, :D_SHIFT], d_w_conv], axis=1)
    d_x, d_norm1 = _proj_norm_bwd(d_pshift, w_shift, d_pc, w_conv, xs, vec("norm1_w"), d_x1, tm,
                                  "in_proj_dx_norm1_bwd")
    g["w_up_f"], g["a_up_f"] = d_m0[:D_LORA], d_m1[D_LORA:]
    g["w_up_b"], g["a_up_b"] = d_m2[:D_LORA], d_m3[D_LORA:]
    g["g_up"] = d_m4[:D_GATE]
    g["conv_w"] = d_qvec[4:7]

    def finish(chip_sum, others, tag, layout):
        eighth = _add_quarters(chip_sum, others, chip, tag)
        other_eighth = _swap_with_sibling(eighth, "swap_eighths_" + tag)
        return _unpack_grad_shard(jnp.concatenate([jnp.where(c == 0, eighth, other_eighth),
                                                   jnp.where(c == 0, other_eighth, eighth)], axis=0), layout)

    as2d = lambda name: (1, w[name].shape[0]) if w[name].ndim == 1 else w[name].shape
    operands = lambda name: tuple(a.reshape(as2d(name)) for a in (w[name], grads[name], m[name], v[name]))

    packed = _pack_grads(g, _EARLY_ROWS, EARLY_R)
    mix_sum, mix_sum_bf16 = _add_halves(packed, _swap_other_half(packed, "mixer"), c, "mixer")
    mx_send, mx_recv, mix_sum_bf16, mx_land, mx_token = _exchange_start(mix_sum_bf16, "mixer")
    grads = finish(ffn_sum, _exchange_wait(ex_send, ex_recv, ffn_sum_bf16, ex_land, mx_token, "ffn"), "ffn", _FFN_ROWS)
    updates = {name: _adamw(*operands(name), "adamw_" + name) for name in _FFN_NAMES}
    mix_others = _exchange_wait(mx_send, mx_recv, mix_sum_bf16, mx_land, updates["w_down"][2], "mixer")
    grads.update(finish(mix_sum, mix_others, "mixer", _EARLY_ROWS))

    small = dict(norm1_w=d_norm1, mu_shift=d_mu[:, :D_SHIFT], w0_f=d_pvec[1], w0_b=d_pvec[4], a0_f=d_pvec[2],
                 a0_b=d_pvec[5], k_k=d_pvec[0], k_a_f=d_pvec[3], k_a_b=d_pvec[6], r_k_f=d_qvec[2], r_k_b=d_qvec[3],
                 gn_w=d_qvec[0], gn_b=d_qvec[1], norm2_w=d_norm2, norm_f_w=d_norm_f, loss=loss_part)
    reduced = _unpack_small(_allreduce_small(_pack_small(small)))
    loss = reduced.pop("loss")[0]
    grads.update(reduced)

    outs = {}
    small = [name for name in _WEIGHTS if name not in _BIG_SHARDED]
    updates.update(zip(small, _adamw_small([operands(name) for name in small])))
    for name in ("w_in", "w_out"):
        updates[name] = _adamw(*operands(name), "adamw_" + name)
    for name in _WEIGHTS:
        shape = w[name].shape
        outs[name] = (grads[name].reshape(shape),) + tuple(a.reshape(shape) for a in updates[name])
    d_x = d_x.reshape(batch, seq, D_MODEL)
    return (loss, d_x) + tuple(outs[name][k] for k in range(4) for name in _WEIGHTS)


def kernel(x, norm1_w, w_in, mu_shift, w_up_f, w0_f, w_up_b, w0_b, a_up_f, a0_f, a_up_b, a0_b, g_up, k_k, k_a_f, k_a_b, r_k_f, r_k_b, gn_w, gn_b, conv_w, w_out, norm2_w, w_gate, w_up, w_down, norm_f_w, loss_target, m_norm1_w, m_w_in, m_mu_shift, m_w_up_f, m_w0_f, m_w_up_b, m_w0_b, m_a_up_f, m_a0_f, m_a_up_b, m_a0_b, m_g_up, m_k_k, m_k_a_f, m_k_a_b, m_r_k_f, m_r_k_b, m_gn_w, m_gn_b, m_conv_w, m_w_out, m_norm2_w, m_w_gate, m_w_up, m_w_down, m_norm_f_w, v_norm1_w, v_w_in, v_mu_shift, v_w_up_f, v_w0_f, v_w_up_b, v_w0_b, v_a_up_f, v_a0_f, v_a_up_b, v_a0_b, v_g_up, v_k_k, v_k_a_f, v_k_a_b, v_r_k_f, v_r_k_b, v_gn_w, v_gn_b, v_conv_w, v_w_out, v_norm2_w, v_w_gate, v_w_up, v_w_down, v_norm_f_w):
    args = locals()
    w = {name: args[name] for name in _WEIGHTS}
    m = {name: args["m_" + name] for name in _WEIGHTS}
    v = {name: args["v_" + name] for name in _WEIGHTS}
    return _train_step(x, loss_target, w, m, v)
```

```python
import functools

import jax
import jax.numpy as jnp
from jax import lax
from jax.experimental import pallas as pl
from jax.experimental.pallas import tpu as pltpu

F32 = jnp.float32
BF16 = jnp.bfloat16
MESH = pl.DeviceIdType.MESH

D_MODEL = 1024
D_RWKV = 512
HEAD = 64
N_PAIR = D_RWKV // (2 * HEAD)
D_LORA = 64
D_GATE = 160
D_GATE_PAD = 384
D_FF = 2816
D_SHIFT = 1824
D_SHIFT_PAD = 2048
D_CONV3 = 1536
LOG_DECAY_SCALE = 0.606531
RMS_EPS = 1e-6
GN_EPS = 64e-5
NORM_EPS = 1e-12
ADAM_LR, ADAM_B1, ADAM_B2, ADAM_EPS, ADAM_WD, ADAM_STEP = 0.001, 0.9, 0.999, 1e-08, 0.01, 10

N_SHARD = 4
N_DEV = 8
V7X_VMEM_LIMIT = 48 * 1024 * 1024
SCAN_CHUNK = 64
GROUP = 8

_PACK_ROWS = (("w_in", 840), ("w_out", 256), ("w_gate", 704), ("w_up", 704), ("w_down", 704),
              ("w_up_f", 8), ("w_up_b", 8), ("a_up_f", 8), ("a_up_b", 8), ("g_up", 20), ("conv_w", 1))
_FFN_NAMES = ("w_gate", "w_up", "w_down")
_EARLY_ROWS = tuple(item for item in _PACK_ROWS if item[0] not in _FFN_NAMES)
_FFN_ROWS = tuple(item for item in _PACK_ROWS if item[0] in _FFN_NAMES)
EARLY_R = 1152
SMALL_ROWS = 24


def _tile(n, cap, mult=128):
    best = None
    t = mult
    while t <= min(n, cap):
        if n % t == 0:
            best = t
        t += mult
    return best or n


def _cp(*sem):
    return pltpu.CompilerParams(dimension_semantics=sem or None, vmem_limit_bytes=V7X_VMEM_LIMIT)


def _sds(shape, dtype=F32):
    return jax.ShapeDtypeStruct(shape, dtype)


def _matmul(a, b, *, mode, name, out_dtype=F32, add=None):
    m, kdim = a.shape
    n = b.shape[1] if mode == "nn" else b.shape[0]
    tm, tn = _tile(m, 768, 8), _tile(n, 1536)
    tk = kdim if kdim <= 4096 else _tile(kdim, 2048)
    nk = kdim // tk
    a_spec = pl.BlockSpec((tm, tk), lambda i, j, k: (i, k))
    if mode == "nn":
        b_spec = pl.BlockSpec((tk, tn), lambda i, j, k: (k, j))
        dims = (((1,), (0,)), ((), ()))
    else:
        b_spec = pl.BlockSpec((tn, tk), lambda i, j, k: (j, k))
        dims = (((1,), (1,)), ((), ()))
    has_add = add is not None

    def body(*refs):
        a_ref, b_ref = refs[0], refs[1]
        add_ref = refs[2] if has_add else None
        o_ref = refs[3] if has_add else refs[2]
        part = lax.dot_general(a_ref[...].astype(BF16), b_ref[...].astype(BF16), dims,
                               preferred_element_type=F32)
        if nk == 1:
            if has_add:
                part = part + add_ref[...]
            o_ref[...] = part.astype(out_dtype)
        else:
            acc_ref = refs[-1]
            k = pl.program_id(2)

            @pl.when(k == 0)
            def _():
                acc_ref[...] = jnp.zeros_like(acc_ref)

            acc_ref[...] += part

            @pl.when(k == nk - 1)
            def _():
                res = acc_ref[...]
                if has_add:
                    res = res + add_ref[...]
                o_ref[...] = res.astype(out_dtype)

    o_spec = pl.BlockSpec((tm, tn), lambda i, j, k: (i, j))
    in_specs = [a_spec, b_spec] + ([o_spec] if has_add else [])
    args = (a, b) + ((add,) if has_add else ())
    return pl.pallas_call(
        body, name=name, grid=(m // tm, n // tn, nk), in_specs=in_specs, out_specs=o_spec,
        out_shape=_sds((m, n), out_dtype),
        scratch_shapes=[pltpu.VMEM((tm, tn), F32)] if nk > 1 else [],
        compiler_params=_cp("parallel", "parallel", "arbitrary"),
    )(*args)


def _row(tm, width):
    return pl.BlockSpec((tm, width), lambda i: (i, 0))


def _col(tm, height):
    return pl.BlockSpec((height, tm), lambda i: (0, i))


def _fixed(shape):
    return pl.BlockSpec(shape, lambda i: tuple(0 for _ in shape))


def _rmsnorm_tile(xv, wv):
    return xv * lax.rsqrt(jnp.mean(xv * xv, axis=-1, keepdims=True) + RMS_EPS) * wv


def _norm_in_proj(x, w, w_shift, w_conv, tm):
    t, d = x.shape
    n_a, n_b = w_shift.shape[1], w_conv.shape[1]

    def body(x_ref, w_ref, wa_ref, wb_ref, ht_ref, pa_ref, pb_ref):
        hv = _rmsnorm_tile(x_ref[...], w_ref[...])
        ht_ref[...] = jnp.transpose(hv).astype(BF16)
        hb = hv.astype(BF16)
        pa_ref[...] = jnp.dot(hb, wa_ref[...], preferred_element_type=F32)
        pb_ref[...] = jnp.dot(hb, wb_ref[...], preferred_element_type=F32)

    return pl.pallas_call(
        body, name="norm1_in_proj", grid=(t // tm,),
        in_specs=[_row(tm, d), _fixed((1, d)), _fixed((d, n_a)), _fixed((d, n_b))],
        out_specs=[_col(tm, d), _row(tm, n_a), _row(tm, n_b)],
        out_shape=[_sds((d, t), BF16), _sds((t, n_a)), _sds((t, n_b))],
        compiler_params=_cp("parallel"))(x, w, w_shift, w_conv)


def _out_proj_norm(mixed, w_out, res, w, tm):
    t, d = res.shape

    def body(m_ref, wo_ref, r_ref, w_ref, x_ref, h_ref, ht_ref):
        xv = r_ref[...] + jnp.dot(m_ref[...], wo_ref[...], preferred_element_type=F32)
        x_ref[...] = xv
        hv = _rmsnorm_tile(xv, w_ref[...])
        h_ref[...] = hv.astype(BF16)
        ht_ref[...] = jnp.transpose(hv).astype(BF16)

    return pl.pallas_call(
        body, name="out_proj_norm2", grid=(t // tm,),
        in_specs=[_row(tm, d), _fixed((d, d)), _row(tm, d), _fixed((1, d))],
        out_specs=[_row(tm, d), _row(tm, d), _col(tm, d)],
        out_shape=[_sds((t, d)), _sds((t, d), BF16), _sds((d, t), BF16)],
        compiler_params=_cp("parallel"))(mixed, w_out, res, w)


def _rms_bwd_math(xv, wv, dyv):
    rstd = lax.rsqrt(jnp.mean(xv * xv, axis=-1, keepdims=True) + RMS_EPS)
    xhat = xv * rstd
    gv = dyv * wv
    dx = rstd * (gv - xhat * jnp.mean(gv * xhat, axis=-1, keepdims=True))
    return dx, jnp.sum(dyv * xhat, axis=0, keepdims=True)


def _proj_norm_bwd(dy_a, w_a, dy_b, w_b, x, w, dres, tm, name):
    t, d = x.shape
    ka, kb = dy_a.shape[1], dy_b.shape[1]
    nt = (((1,), (1,)), ((), ()))

    def body(dya_ref, wa_ref, dyb_ref, wb_ref, x_ref, w_ref, dres_ref, dx_ref, dw_ref):
        d_h = (lax.dot_general(dya_ref[...], wa_ref[...], nt, preferred_element_type=F32)
               + lax.dot_general(dyb_ref[...], wb_ref[...], nt, preferred_element_type=F32))
        dx, dw = _rms_bwd_math(x_ref[...], w_ref[...], d_h)
        dx_ref[...] = dres_ref[...] + dx

        @pl.when(pl.program_id(0) == 0)
        def _():
            dw_ref[...] = jnp.zeros_like(dw_ref)

        dw_ref[...] += dw

    return pl.pallas_call(
        body, name=name, grid=(t // tm,),
        in_specs=[_row(tm, ka), _fixed((d, ka)), _row(tm, kb), _fixed((d, kb)), _row(tm, d), _fixed((1, d)),
                  _row(tm, d)],
        out_specs=[_row(tm, d), _fixed((1, d))],
        out_shape=[_sds((t, d)), _sds((1, d))], compiler_params=_cp("arbitrary"))(dy_a, w_a, dy_b, w_b, x, w, dres)


def _ffn_out_loss(act, w_down, x1, w, target, tm):
    t, d = x1.shape
    f = act.shape[1]

    def body(a_ref, wd_ref, x_ref, w_ref, t_ref, dx_ref, dx16_ref, dw_ref, loss_ref):
        xv = x_ref[...] + jnp.dot(a_ref[...], wd_ref[...], preferred_element_type=F32)
        wv = w_ref[...]
        rstd = lax.rsqrt(jnp.mean(xv * xv, axis=-1, keepdims=True) + RMS_EPS)
        err = xv * rstd * wv - t_ref[...]
        dx, dw = _rms_bwd_math(xv, wv, err * (1.0 / d))
        dx_ref[...] = dx
        dx16_ref[...] = dx.astype(BF16)

        @pl.when(pl.program_id(0) == 0)
        def _():
            dw_ref[...] = jnp.zeros_like(dw_ref)
            loss_ref[...] = jnp.zeros_like(loss_ref)

        dw_ref[...] += dw
        loss_ref[...] += 0.5 * jnp.sum(jnp.mean(err * err, axis=-1, keepdims=True), axis=0, keepdims=True)

    return pl.pallas_call(
        body, name="ffn_out_loss", grid=(t // tm,),
        in_specs=[_row(tm, f), _fixed((f, d)), _row(tm, d), _fixed((1, d)), _row(tm, d)],
        out_specs=[_row(tm, d), _row(tm, d), _fixed((1, d)), _fixed((1, 1))],
        out_shape=[_sds((t, d)), _sds((t, d), BF16), _sds((1, d)), _sds((1, 1))],
        compiler_params=_cp("arbitrary"))(act, w_down, x1, w, target)


def _ffn_in(h, w_gate, w_up):
    t, d = h.shape
    f = w_gate.shape[1]
    tm, tn = _tile(t, 512, 8), _tile(f, 1536)

    def body(h_ref, wg_ref, wu_ref, g_ref, u_ref, a_ref, at_ref):
        hv = h_ref[...]
        gv = jnp.dot(hv, wg_ref[...], preferred_element_type=F32)
        uv = jnp.dot(hv, wu_ref[...], preferred_element_type=F32)
        act = gv * jax.nn.sigmoid(gv) * uv
        g_ref[...] = gv.astype(BF16)
        u_ref[...] = uv.astype(BF16)
        a_ref[...] = act.astype(BF16)
        at_ref[...] = jnp.transpose(act).astype(BF16)

    w_spec = pl.BlockSpec((d, tn), lambda i, j: (0, j))
    o_spec = pl.BlockSpec((tm, tn), lambda i, j: (i, j))
    return pl.pallas_call(
        body, name="ffn_in", grid=(t // tm, f // tn),
        in_specs=[pl.BlockSpec((tm, d), lambda i, j: (i, 0)), w_spec, w_spec],
        out_specs=[o_spec, o_spec, o_spec, pl.BlockSpec((tn, tm), lambda i, j: (j, i))],
        out_shape=[_sds((t, f), BF16)] * 3 + [_sds((f, t), BF16)],
        compiler_params=_cp("parallel", "parallel"))(h, w_gate, w_up)


def _ffn_in_bwd(d_out, w_down, gate, up):
    t, d = d_out.shape
    f = w_down.shape[0]
    tm, tn = _tile(t, 512, 8), _tile(f, 1536)

    def body(do_ref, w_ref, g_ref, u_ref, dg_ref, du_ref):
        dv = lax.dot_general(do_ref[...].astype(BF16), w_ref[...], (((1,), (1,)), ((), ())),
                             preferred_element_type=F32)
        gv, uv = g_ref[...].astype(F32), u_ref[...].astype(F32)
        sg = jax.nn.sigmoid(gv)
        du_ref[...] = (dv * gv * sg).astype(BF16)
        dg_ref[...] = (dv * uv * (sg * (1.0 + gv * (1.0 - sg)))).astype(BF16)

    tile = pl.BlockSpec((tm, tn), lambda i, j: (i, j))
    return pl.pallas_call(
        body, name="ffn_in_bwd", grid=(t // tm, f // tn),
        in_specs=[pl.BlockSpec((tm, d), lambda i, j: (i, 0)), pl.BlockSpec((tn, d), lambda i, j: (j, 0)), tile, tile],
        out_specs=[tile, tile], out_shape=[_sds((t, f), BF16)] * 2,
        compiler_params=_cp("parallel", "parallel"))(d_out, w_down, gate, up)


def _halo_specs(tm, width, rows_total):
    per = tm // GROUP
    last = rows_total // GROUP - 1
    prev = pl.BlockSpec((GROUP, width), lambda i: (jnp.maximum(i * per - 1, 0), 0))
    nxt = pl.BlockSpec((GROUP, width), lambda i: (jnp.minimum((i + 1) * per, last), 0))
    return prev, nxt


def _edge_flags(tm, seq):
    i = pl.program_id(0)
    has_prev = jnp.where((i * tm) % seq == 0, 0.0, 1.0).astype(F32)
    has_next = jnp.where(((i + 1) * tm) % seq == 0, 0.0, 1.0).astype(F32)
    return has_prev, has_next


def _shifted(xv, prev_row, next_row):
    tm = xv.shape[0]
    row = lax.broadcasted_iota(jnp.int32, xv.shape, 0)
    down = jnp.where(row == 0, prev_row, pltpu.roll(xv, 1, axis=0))
    up = jnp.where(row == tm - 1, next_row, pltpu.roll(xv, tm - 1, axis=0))
    return down, up


def _shift_bwd(q, p, mu, tm, seq):
    t, w = p.shape
    prev_spec, next_spec = _halo_specs(tm, w, t)

    def body(q_ref, qp_ref, qn_ref, p_ref, pp_ref, pn_ref, mu_ref, dp_ref, dmu_ref):
        has_prev, has_next = _edge_flags(tm, seq)
        muv = mu_ref[0:1, :]
        qv = q_ref[...]
        mq = muv * qv
        mq_down, mq_up = _shifted(mq, muv * qp_ref[GROUP - 1:GROUP, :] * has_prev,
                                  muv * qn_ref[0:1, :] * has_next)
        dp_ref[...] = (qv - mq + 0.5 * (mq_down + mq_up)).astype(BF16)
        pv = p_ref[...]
        p_down, p_up = _shifted(pv, pp_ref[GROUP - 1:GROUP, :] * has_prev, pn_ref[0:1, :] * has_next)

        @pl.when(pl.program_id(0) == 0)
        def _():
            dmu_ref[...] = jnp.zeros_like(dmu_ref)

        dmu_ref[...] += jnp.sum(qv * (0.5 * (p_down + p_up) - pv), axis=0, keepdims=True)

    return pl.pallas_call(
        body, name="shift_bwd", grid=(t // tm,),
        in_specs=[_row(tm, w), prev_spec, next_spec, _row(tm, w), prev_spec, next_spec, _fixed((GROUP, w))],
        out_specs=[_row(tm, w), _fixed((1, w))],
        out_shape=[_sds((t, w), BF16), _sds((1, w))], compiler_params=_cp("arbitrary"))(q, q, q, p, p, p, mu)


@jax.custom_vjp
def _bdot(a, b):
    return jnp.dot(a.astype(BF16), b.astype(BF16), preferred_element_type=F32)


def _bdot_fwd(a, b):
    return _bdot(a, b), (a, b)


def _bdot_bwd(res, g):
    a, b = res
    gb = g.astype(BF16)
    da = lax.dot_general(gb, b.astype(BF16), (((1,), (1,)), ((), ())), preferred_element_type=F32)
    db = lax.dot_general(a.astype(BF16), gb, (((0,), (0,)), ((), ())), preferred_element_type=F32)
    return da, db


_bdot.defvjp(_bdot_fwd, _bdot_bwd)


def _seg_raw(x, ones_blocks):
    hi = x.astype(BF16)
    lo = (x - hi.astype(F32)).astype(BF16)
    return (jnp.dot(hi, ones_blocks, preferred_element_type=F32)
            + jnp.dot(lo, ones_blocks, preferred_element_type=F32))


@jax.custom_vjp
def _seg(x, ones_blocks):
    return _seg_raw(x, ones_blocks)


def _seg_fwd(x, ones_blocks):
    return _seg_raw(x, ones_blocks), ones_blocks


def _seg_bwd(ones_blocks, g):
    return _seg_raw(g, ones_blocks), jnp.zeros_like(ones_blocks)


_seg.defvjp(_seg_fwd, _seg_bwd)


def _head_ones():
    h = jnp.arange(D_RWKV) // HEAD
    return (h[:, None] == h[None, :]).astype(BF16)


def _prep_math(ps, k_k, w0_f, a0_f, k_a_f, w0_b, a0_b, k_a_b, wup_f, aup_f, wup_b, aup_b, gup, ones_blocks):
    r = ps[:, 0:512]
    k = ps[:, 512:1024]
    v = ps[:, 1024:1536]
    xwa = ps[:, 1536:1664]
    xg = ps[:, 1664:D_SHIFT_PAD]
    kk_raw = k * k_k
    norm = jnp.sqrt(_seg(kk_raw * kk_raw, ones_blocks))
    kk = kk_raw / jnp.maximum(norm, NORM_EPS)
    t_xwa = jnp.tanh(xwa)
    outs = [r, v, kk]
    for w0, a0, k_a, wup, aup in ((w0_f, a0_f, k_a_f, wup_f, aup_f), (w0_b, a0_b, k_a_b, wup_b, aup_b)):
        decay = jnp.exp(-LOG_DECAY_SCALE * jax.nn.sigmoid(w0 + _bdot(t_xwa, wup)))
        rate = jax.nn.sigmoid(a0 + _bdot(xwa, aup))
        outs += [decay, k * (1.0 + (rate - 1.0) * k_a), kk * rate]
    outs.append(_bdot(jax.nn.sigmoid(xg), gup))
    return tuple(outs)


def _prep_args(tm, ps_tile, pv_ref, mat_refs, ones_ref):
    vecs = [jnp.broadcast_to(pv_ref[j:j + 1, :], (tm, D_RWKV)) for j in range(7)]
    return [ps_tile] + vecs + [m[...] for m in mat_refs] + [ones_ref[...]]


_PREP_MAT_SHAPES = ((128, D_RWKV),) * 4 + ((D_GATE_PAD, D_RWKV),)


def _shift_prep_fwd(p, mu, pvec, mats, ones_blocks, tm, seq):
    t, w = p.shape
    prev_spec, next_spec = _halo_specs(tm, w, t)

    def body(p_ref, hp_ref, hn_ref, mu_ref, pv_ref, m0, m1, m2, m3, m4, ones_ref, ps_ref, *out_refs):
        has_prev, has_next = _edge_flags(tm, seq)
        xv = p_ref[...]
        down, up = _shifted(xv, hp_ref[GROUP - 1:GROUP, :] * has_prev, hn_ref[0:1, :] * has_next)
        ps_tile = xv + mu_ref[0:1, :] * (0.5 * (down + up) - xv)
        ps_ref[...] = ps_tile
        outs = _prep_math(*_prep_args(tm, ps_tile, pv_ref, (m0, m1, m2, m3, m4), ones_ref))
        for o_ref, val in zip(out_refs, outs[2:]):
            o_ref[...] = val

    return pl.pallas_call(
        body, name="shift_prep_fwd", grid=(t // tm,),
        in_specs=[_row(tm, w), prev_spec, next_spec, _fixed((GROUP, w)), _fixed((8, D_RWKV))]
        + [_fixed(s) for s in _PREP_MAT_SHAPES] + [_fixed((D_RWKV, D_RWKV))],
        out_specs=[_row(tm, w)] + [_row(tm, D_RWKV)] * 8, out_shape=[_sds((t, w))] + [_sds((t, D_RWKV))] * 8,
        compiler_params=_cp("parallel"))(p, p, p, mu, pvec, *mats, ones_blocks)


def _prep_bwd(ps, pvec, mats, ones_blocks, cts, tm):
    t = ps.shape[0]
    counts = [len(c) for c in cts]
    flat = [a for c in cts for a in c]

    def body(ps_ref, pv_ref, m0, m1, m2, m3, m4, ones_ref, *refs):
        ct_refs = refs[:len(flat)]
        q_ref, dpv_ref = refs[len(flat)], refs[len(flat) + 1]
        dmat_refs = refs[len(flat) + 2:]
        args = _prep_args(tm, ps_ref[...], pv_ref, (m0, m1, m2, m3, m4), ones_ref)
        _, vjp = jax.vjp(lambda *a: _prep_math(*a, args[-1]), *args[:-1])
        ct_vals, pos = [], 0
        for n in counts:
            val = ct_refs[pos][...]
            for extra in ct_refs[pos + 1:pos + n]:
                val = val + extra[...]
            ct_vals.append(val)
            pos += n
        grads = vjp(tuple(ct_vals))
        q_ref[...] = grads[0]

        @pl.when(pl.program_id(0) == 0)
        def _():
            dpv_ref[...] = jnp.zeros_like(dpv_ref)
            for d_ref in dmat_refs:
                d_ref[...] = jnp.zeros_like(d_ref)

        for j in range(7):
            dpv_ref[j:j + 1, :] += jnp.sum(grads[1 + j], axis=0, keepdims=True)
        for d_ref, gm in zip(dmat_refs, grads[8:13]):
            d_ref[...] += gm

    return pl.pallas_call(
        body, name="prep_bwd", grid=(t // tm,),
        in_specs=[_row(tm, D_SHIFT_PAD), _fixed((8, D_RWKV))] + [_fixed(s) for s in _PREP_MAT_SHAPES]
        + [_fixed((D_RWKV, D_RWKV))] + [_row(tm, D_RWKV)] * len(flat),
        out_specs=[_row(tm, D_SHIFT_PAD), _fixed((8, D_RWKV))] + [_fixed(s) for s in _PREP_MAT_SHAPES],
        out_shape=[_sds((t, D_SHIFT_PAD)), _sds((8, D_RWKV))] + [_sds(s) for s in _PREP_MAT_SHAPES],
        compiler_params=_cp("arbitrary"))(ps, pvec, *mats, ones_blocks, *flat)


def _pair_ones():
    h = jnp.arange(2 * HEAD) // HEAD
    block = (h[:, None] == h[None, :]).astype(BF16)
    return jnp.concatenate([block, block], axis=0)


def _diag_mask():
    lane = lax.broadcasted_iota(jnp.int32, (HEAD, 2 * HEAD), 1)
    sub = lax.broadcasted_iota(jnp.int32, (HEAD, 2 * HEAD), 0)
    return jnp.where((lane & (HEAD - 1)) == sub, 1.0, 0.0).astype(F32)


def _to_row(cols, dmask):
    return jnp.sum(cols * dmask, axis=0, keepdims=True)


def _seg_many(exact, rounded, ones_pair):
    out_exact, out_rounded = [], []
    if exact:
        parts = []
        for x in exact:
            hi = x.astype(BF16)
            parts.append(jnp.concatenate([hi, (x - hi.astype(F32)).astype(BF16)], axis=1))
        res = jnp.dot(jnp.concatenate(parts, axis=0), ones_pair, preferred_element_type=F32)
        out_exact = [res[HEAD * c:HEAD * (c + 1)] for c in range(len(exact))]
    if rounded:
        res = jnp.dot(jnp.concatenate([x.astype(BF16) for x in rounded], axis=0), ones_pair[0:2 * HEAD],
                      preferred_element_type=F32)
        out_rounded = [res[HEAD * c:HEAD * (c + 1)] for c in range(len(rounded))]
    return out_exact, out_rounded


def _seg_rows(pieces, ones_pair):
    parts = []
    for x in pieces:
        hi = x.astype(BF16)
        parts.append(jnp.concatenate([hi, (x - hi.astype(F32)).astype(BF16)], axis=1))
    res = jnp.dot(jnp.concatenate(parts, axis=0), ones_pair, preferred_element_type=F32)
    return [res[2 * GROUP * c:2 * GROUP * (c + 1)] for c in range(len(pieces))]


N_CHAIN = 2 * N_PAIR


def _chain(c):
    d, p = divmod(c, N_PAIR)
    return d, slice(2 * HEAD * p, 2 * HEAD * (p + 1))


def _scan_specs(n_chunks, col_blocks, fwd_chunk, bwd_chunk):
    def spec(chunk_of, col):
        return pl.BlockSpec((SCAN_CHUNK, D_RWKV), lambda b, g: (b * n_chunks + chunk_of(g), col))
    return [spec(fwd_chunk, c) for c in col_blocks] + [spec(bwd_chunk, c) for c in col_blocks]


def _scan_fwd(ps, kk, dirs, batch, seq):
    t = batch * seq
    n = seq // SCAN_CHUNK
    groups = SCAN_CHUNK // GROUP
    up = lambda g: g
    down = lambda g: n - 1 - g
    col_blocks = (0, 2, 0, 0, 0, 0)

    def body(*refs):
        dir_refs = (refs[0:6], refs[6:12])
        ones_ref = refs[12]
        y_refs, hist_refs, st_ref = refs[13:15], refs[15:17], refs[17]

        @pl.when(pl.program_id(1) == 0)
        def _():
            st_ref[...] = jnp.zeros_like(st_ref)

        ones_pair = ones_ref[...]
        dmask = _diag_mask()
        dmask_b = dmask.astype(BF16)
        sub8 = lax.broadcasted_iota(jnp.int32, (GROUP, 2 * HEAD), 0)

        def group(gi, carry):
            off = (pl.multiple_of(gi * GROUP, GROUP), pl.multiple_of((groups - 1 - gi) * GROUP, GROUP))
            loaded = [tuple(ref[pl.ds(off[d], GROUP), :] for ref in dir_refs[d]) for d in range(2)]
            states = list(carry)
            y_acc = [jnp.zeros((GROUP, 2 * HEAD), F32) for _ in range(N_CHAIN)]
            with_next = []
            for d in range(2):
                _, _, kk8, w8, kd8, b8 = loaded[d]
                kk_next = pltpu.roll(kk8, GROUP - 1 if d == 0 else 1, axis=0)
                with_next.append((b8 * kk_next, kd8 * kk_next, w8 * kk_next))
            row_dots = _seg_rows([jnp.concatenate([with_next[c // N_PAIR][0][:, _chain(c)[1]],
                                                   with_next[c // N_PAIR][1][:, _chain(c)[1]]], axis=0)
                                  for c in range(N_CHAIN)], ones_pair)
            for step in range(0, GROUP, 2):
                rows0, rows1, first, second, w_kk = [], [], [], [], []
                for c in range(N_CHAIN):
                    d, lanes = _chain(c)
                    i0, i1 = (step, step + 1) if d == 0 else (GROUP - 1 - step, GROUP - 2 - step)
                    first.append(i0)
                    second.append(i1)
                    rows0.append(tuple(x8[i0:i0 + 1, lanes] for x8 in loaded[d]))
                    rows1.append(tuple(x8[i1:i1 + 1, lanes] for x8 in loaded[d]))
                    w_kk.append(with_next[d][2][i0:i0 + 1, lanes])
                    hist_refs[d][c % N_PAIR, gi * GROUP + step] = states[c]
                _, v_cols = _seg_many([], [dmask_b * r[c][1].astype(BF16) for r in (rows0, rows1)
                                           for c in range(N_CHAIN)], ones_pair)
                sums, _ = _seg_many([x for c in range(N_CHAIN)
                                     for x in (states[c] * rows0[c][2], states[c] * w_kk[c])], [], ones_pair)
                mid = []
                for c in range(N_CHAIN):
                    d = c // N_PAIR
                    sa0, carried = sums[2 * c], sums[2 * c + 1]
                    b_dot = row_dots[c][first[c]:first[c] + 1]
                    kd_dot = row_dots[c][GROUP + first[c]:GROUP + first[c] + 1]
                    sa1 = carried - sa0 * b_dot + v_cols[c] * kd_dot
                    _, _, _, w0, kd0, b0 = rows0[c]
                    _, _, _, w1, kd1, b1 = rows1[c]
                    state0 = states[c] * w0 - sa0 * b0 + v_cols[c] * kd0
                    hist_refs[d][c % N_PAIR, gi * GROUP + step + 1] = state0
                    states[c] = state0 * w1 - sa1 * b1 + v_cols[N_CHAIN + c] * kd1
                    mid.append(state0)
                _, ys = _seg_many([], [mid[c] * rows0[c][0] for c in range(N_CHAIN)]
                                  + [states[c] * rows1[c][0] for c in range(N_CHAIN)], ones_pair)
                for c in range(N_CHAIN):
                    y_acc[c] = jnp.where(sub8 == first[c], _to_row(ys[c], dmask), y_acc[c])
                    y_acc[c] = jnp.where(sub8 == second[c], _to_row(ys[N_CHAIN + c], dmask), y_acc[c])
            for c in range(N_CHAIN):
                d, lanes = _chain(c)
                y_refs[d][pl.ds(off[d], GROUP), lanes] = y_acc[c]
            return tuple(states)

        final = lax.fori_loop(0, groups, group, tuple(st_ref[c] for c in range(N_CHAIN)))
        for c in range(N_CHAIN):
            st_ref[c] = final[c]
            hist_refs[c // N_PAIR][c % N_PAIR, SCAN_CHUNK] = final[c]

    y_spec_f = pl.BlockSpec((SCAN_CHUNK, D_RWKV), lambda b, g: (b * n + up(g), 0))
    y_spec_b = pl.BlockSpec((SCAN_CHUNK, D_RWKV), lambda b, g: (b * n + down(g), 0))
    hist_shape = (batch, n, N_PAIR, SCAN_CHUNK + 1, HEAD, 2 * HEAD)
    hist_block = (None, None, N_PAIR, SCAN_CHUNK + 1, HEAD, 2 * HEAD)
    hist_spec_f = pl.BlockSpec(hist_block, lambda b, g: (b, up(g), 0, 0, 0, 0))
    hist_spec_b = pl.BlockSpec(hist_block, lambda b, g: (b, down(g), 0, 0, 0, 0))
    ones_spec = pl.BlockSpec((4 * HEAD, 2 * HEAD), lambda b, g: (0, 0))
    (wf, kdf, bf), (wb, kdb, bb) = dirs
    return pl.pallas_call(
        body, name="wkv_fwd", grid=(batch, n),
        in_specs=_scan_specs(n, col_blocks, up, down) + [ones_spec],
        out_specs=[y_spec_f, y_spec_b, hist_spec_f, hist_spec_b],
        out_shape=[_sds((t, D_RWKV)), _sds((t, D_RWKV)), _sds(hist_shape), _sds(hist_shape)],
        scratch_shapes=[pltpu.VMEM((N_CHAIN, HEAD, 2 * HEAD), F32)],
        compiler_params=_cp("parallel", "arbitrary"),
    )(ps, ps, kk, wf, kdf, bf, ps, ps, kk, wb, kdb, bb, _pair_ones())


def _scan_bwd(ps, kk, dirs, dy, hist_f, hist_b, batch, seq):
    t = batch * seq
    n = seq // SCAN_CHUNK
    groups = SCAN_CHUNK // GROUP
    fwd_chunk = lambda g: n - 1 - g
    bwd_chunk = lambda g: g
    col_blocks = (0, 2, 0, 0, 0, 0, 0)

    def undo_group(dir_refs, out_refs, hist_refs, gi, d_states, ones_pair, dmask, sub8):
        d_states = list(d_states)
        loaded, blocks = [], []
        for d in range(2):
            blk = groups - 1 - gi if d == 0 else gi
            blocks.append(pl.ds(pl.multiple_of(blk * GROUP, GROUP), GROUP))
            r8, v8, kk8, w8, kd8, b8, dy8 = (ref[blocks[d], :] for ref in dir_refs[d])
            loaded.append((r8, v8, kk8, w8, kd8, -b8, dy8))
        acc = [[jnp.zeros((GROUP, 2 * HEAD), F32) for _ in range(6)] for _ in range(N_CHAIN)]
        for step in range(GROUP):
            rows, idx, before, after = [], [], [], []
            for c in range(N_CHAIN):
                d, lanes = _chain(c)
                i = GROUP - 1 - step if d == 0 else step
                q = (groups - 1 - gi) * GROUP + i if d == 0 else SCAN_CHUNK - 1 - (gi * GROUP + i)
                idx.append(i)
                rows.append(tuple(x8[i:i + 1, lanes] for x8 in loaded[d]))
                before.append(hist_refs[d][c % N_PAIR, q])
                after.append(hist_refs[d][c % N_PAIR, q + 1])
            _, cols = _seg_many([], [dmask.astype(BF16) * rows[c][j].astype(BF16) for c in range(N_CHAIN) for j in (1, 6)],
                                ones_pair)
            v_cols, dy_cols = cols[0::2], cols[1::2]
            d_now = [d_states[c] + dy_cols[c] * rows[c][0] for c in range(N_CHAIN)]
            d_sas, _ = _seg_many([d_now[c] * rows[c][5] for c in range(N_CHAIN)], [], ones_pair)
            _, others = _seg_many(
                [], [x for c in range(N_CHAIN) for x in (before[c] * rows[c][2], d_now[c] * rows[c][4])], ones_pair)
            for c in range(N_CHAIN):
                sa, d_sa, dv_cols = others[2 * c], d_sas[c], others[2 * c + 1]
                rows_out = (
                    jnp.sum(after[c] * dy_cols[c], axis=0, keepdims=True),
                    jnp.sum(d_now[c] * before[c], axis=0, keepdims=True),
                    jnp.sum(d_now[c] * v_cols[c], axis=0, keepdims=True),
                    _to_row(dv_cols, dmask),
                    jnp.sum(before[c] * d_sa, axis=0, keepdims=True),
                    -jnp.sum(d_now[c] * sa, axis=0, keepdims=True),
                )
                acc[c] = [jnp.where(sub8 == idx[c], val, a) for val, a in zip(rows_out, acc[c])]
                d_states[c] = d_now[c] * rows[c][3] + d_sa * rows[c][2]
        for c in range(N_CHAIN):
            d, lanes = _chain(c)
            for o_ref, val in zip(out_refs[d], acc[c]):
                o_ref[blocks[d], lanes] = val
        return tuple(d_states)

    def body(*refs):
        dir_refs = (refs[0:7], refs[7:14])
        hist_refs, ones_ref = refs[14:16], refs[16]
        out_refs = (refs[17:23], refs[23:29])
        dst_ref = refs[29]

        @pl.when(pl.program_id(1) == 0)
        def _():
            dst_ref[...] = jnp.zeros_like(dst_ref)

        ones_pair = ones_ref[...]
        dmask = _diag_mask()
        sub8 = lax.broadcasted_iota(jnp.int32, (GROUP, 2 * HEAD), 0)

        def group(gi, carry):
            return undo_group(dir_refs, out_refs, hist_refs, gi, carry, ones_pair, dmask, sub8)

        final = lax.fori_loop(0, groups, group, tuple(dst_ref[c] for c in range(N_CHAIN)))
        for c in range(N_CHAIN):
            dst_ref[c] = final[c]

    blk = (SCAN_CHUNK, D_RWKV)
    out_f = pl.BlockSpec(blk, lambda b, g: (b * n + fwd_chunk(g), 0))
    out_b = pl.BlockSpec(blk, lambda b, g: (b * n + bwd_chunk(g), 0))
    hist_block = (None, None, N_PAIR, SCAN_CHUNK + 1, HEAD, 2 * HEAD)
    hist_spec_f = pl.BlockSpec(hist_block, lambda b, g: (b, fwd_chunk(g), 0, 0, 0, 0))
    hist_spec_b = pl.BlockSpec(hist_block, lambda b, g: (b, bwd_chunk(g), 0, 0, 0, 0))
    ones_spec = pl.BlockSpec((4 * HEAD, 2 * HEAD), lambda b, g: (0, 0))
    (wf, kdf, bf), (wb, kdb, bb) = dirs
    outs = pl.pallas_call(
        body, name="wkv_bwd", grid=(batch, n),
        in_specs=_scan_specs(n, col_blocks, fwd_chunk, bwd_chunk) + [hist_spec_f, hist_spec_b, ones_spec],
        out_specs=[out_f] * 6 + [out_b] * 6,
        out_shape=[_sds((t, D_RWKV))] * 12,
        scratch_shapes=[pltpu.VMEM((N_CHAIN, HEAD, 2 * HEAD), F32)],
        compiler_params=_cp("parallel", "arbitrary"),
    )(ps, ps, kk, wf, kdf, bf, dy, ps, ps, kk, wb, kdb, bb, dy, hist_f, hist_b, _pair_ones())
    return outs[0:6], outs[6:12]


def _post_math(y, r, kd_f, kd_b, v, gate, gn_w, gn_b, rk_f, rk_b, ones_blocks):
    mean = _seg(y, ones_blocks) * (1.0 / HEAD)
    yc = y - mean
    var = _seg(yc * yc, ones_blocks) * (1.0 / HEAD)
    yn = yc * lax.rsqrt(var + GN_EPS) * gn_w + gn_b
    bonus = _seg(r * kd_f * rk_f, ones_blocks) * v + _seg(r * kd_b * rk_b, ones_blocks) * v
    return (yn + bonus) * gate


def _conv_parts(pc, halo_prev, halo_next, has_prev, has_next):
    gate_b, gate_c, hid = pc[:, 0:512], pc[:, 512:1024], pc[:, 1024:1536]
    u = gate_c * hid
    u_prev_row = halo_prev[GROUP - 1:GROUP, 512:1024] * halo_prev[GROUP - 1:GROUP, 1024:1536] * has_prev
    u_next_row = halo_next[0:1, 512:1024] * halo_next[0:1, 1024:1536] * has_next
    u_down, u_up = _shifted(u, u_prev_row, u_next_row)
    return gate_b, gate_c, hid, u, u_down, u_up


def _post_specs(tm, t):
    pc_prev, pc_next = _halo_specs(tm, D_CONV3, t)
    col = lambda c: pl.BlockSpec((tm, D_RWKV), lambda i: (i, c))
    return ([col(0), col(0), col(0), col(0), col(0), col(2), col(0), _row(tm, D_CONV3), pc_prev, pc_next,
             _fixed((8, D_RWKV)), _fixed((D_RWKV, D_RWKV))])


def _post_fwd(y_f, y_b, ps, kd_f, kd_b, gate, pc, qvec, ones_blocks, tm, seq):
    t = ps.shape[0]

    def body(yf_ref, yb_ref, r_ref, kdf_ref, kdb_ref, v_ref, g_ref, pc_ref, hp_ref, hn_ref, qv_ref, ones_ref,
             o_ref, ot_ref):
        has_prev, has_next = _edge_flags(tm, seq)
        vec = [jnp.broadcast_to(qv_ref[j:j + 1, :], (tm, D_RWKV)) for j in range(7)]
        o_rwkv = _post_math(yf_ref[...] + yb_ref[...], r_ref[...], kdf_ref[...], kdb_ref[...], v_ref[...],
                            g_ref[...], vec[0], vec[1], vec[2], vec[3], ones_ref[...])
        gate_b, _, _, u, u_down, u_up = _conv_parts(pc_ref[...], hp_ref[...], hn_ref[...], has_prev, has_next)
        o_conv = gate_b * (vec[4] * u_down + vec[5] * u + vec[6] * u_up)
        for half, val in enumerate((o_rwkv, o_conv)):
            o_ref[:, D_RWKV * half:D_RWKV * (half + 1)] = val.astype(BF16)
            ot_ref[D_RWKV * half:D_RWKV * (half + 1), :] = jnp.transpose(val).astype(BF16)

    return pl.pallas_call(
        body, name="post_fwd", grid=(t // tm,), in_specs=_post_specs(tm, t),
        out_specs=[_row(tm, D_MODEL), _col(tm, D_MODEL)],
        out_shape=[_sds((t, D_MODEL), BF16), _sds((D_MODEL, t), BF16)], compiler_params=_cp("parallel"),
    )(y_f, y_b, ps, kd_f, kd_b, ps, gate, pc, pc, pc, qvec, ones_blocks)


def _post_bwd(d_out, y_f, y_b, ps, kd_f, kd_b, gate, pc, qvec, ones_blocks, tm, seq):
    t = ps.shape[0]
    do_prev, do_next = _halo_specs(tm, D_MODEL, t)

    def body(do_ref, dop_ref, don_ref, yf_ref, yb_ref, r_ref, kdf_ref, kdb_ref, v_ref, g_ref, pc_ref, hp_ref,
             hn_ref, qv_ref, ones_ref, dy_ref, dr_ref, dkdf_ref, dkdb_ref, dv_ref, dg_ref, dpc_ref, dqv_ref):
        has_prev, has_next = _edge_flags(tm, seq)
        vec = [jnp.broadcast_to(qv_ref[j:j + 1, :], (tm, D_RWKV)) for j in range(7)]
        ones_v = ones_ref[...]
        args = (yf_ref[...] + yb_ref[...], r_ref[...], kdf_ref[...], kdb_ref[...], v_ref[...], g_ref[...],
                vec[0], vec[1], vec[2], vec[3])
        _, vjp = jax.vjp(lambda *a: _post_math(*a, ones_v), *args)
        grads = vjp(do_ref[:, 0:D_RWKV])
        for o_ref, gval in zip((dy_ref, dr_ref, dkdf_ref, dkdb_ref, dv_ref, dg_ref), grads[0:6]):
            o_ref[...] = gval

        hp, hn = hp_ref[...], hn_ref[...]
        gate_b, gate_c, hid, u, u_down, u_up = _conv_parts(pc_ref[...], hp, hn, has_prev, has_next)
        d_oc = do_ref[:, D_RWKV:2 * D_RWKV]
        d_cu = d_oc * gate_b
        d_cu_prev = dop_ref[GROUP - 1:GROUP, D_RWKV:2 * D_RWKV] * hp[GROUP - 1:GROUP, 0:512] * has_prev
        d_cu_next = don_ref[0:1, D_RWKV:2 * D_RWKV] * hn[0:1, 0:512] * has_next
        d_cu_down, d_cu_up = _shifted(d_cu, d_cu_prev, d_cu_next)
        d_u = vec[5] * d_cu + vec[4] * d_cu_up + vec[6] * d_cu_down
        dpc_ref[:, 0:512] = (d_oc * (vec[4] * u_down + vec[5] * u + vec[6] * u_up)).astype(BF16)
        dpc_ref[:, 512:1024] = (d_u * hid).astype(BF16)
        dpc_ref[:, 1024:1536] = (d_u * gate_c).astype(BF16)

        @pl.when(pl.program_id(0) == 0)
        def _():
            dqv_ref[...] = jnp.zeros_like(dqv_ref)

        vec_grads = list(grads[6:10]) + [d_cu * u_down, d_cu * u, d_cu * u_up]
        for j, gval in enumerate(vec_grads):
            dqv_ref[j:j + 1, :] += jnp.sum(gval, axis=0, keepdims=True)

    return pl.pallas_call(
        body, name="post_bwd", grid=(t // tm,),
        in_specs=[_row(tm, D_MODEL), do_prev, do_next] + _post_specs(tm, t),
        out_specs=[_row(tm, D_RWKV)] * 6 + [_row(tm, D_CONV3), _fixed((8, D_RWKV))],
        out_shape=[_sds((t, D_RWKV))] * 6 + [_sds((t, D_CONV3), BF16), _sds((8, D_RWKV))],
        compiler_params=_cp("arbitrary"),
    )(d_out, d_out, d_out, y_f, y_b, ps, kd_f, kd_b, ps, gate, pc, pc, pc, qvec, ones_blocks)


def _adamw_math(wv, gv, mv, vv):
    m2 = ADAM_B1 * mv + (1.0 - ADAM_B1) * gv
    v2 = ADAM_B2 * vv + (1.0 - ADAM_B2) * (gv * gv)
    m_hat = m2 / (1.0 - ADAM_B1 ** ADAM_STEP)
    v_hat = v2 / (1.0 - ADAM_B2 ** ADAM_STEP)
    return -ADAM_LR * (m_hat / (jnp.sqrt(v_hat) + ADAM_EPS) + ADAM_WD * wv), m2, v2


def _adamw_small(items):
    n = len(items)

    def body(*refs):
        ins, outs = refs[:4 * n], refs[4 * n:]
        for k in range(n):
            w_ref, g_ref, m_ref, v_ref = ins[4 * k:4 * k + 4]
            for o_ref, val in zip(outs[3 * k:3 * k + 3], _adamw_math(w_ref[...], g_ref[...], m_ref[...], v_ref[...])):
                o_ref[...] = val

    flat = [a for item in items for a in item]
    outs = pl.pallas_call(
        body, name="adamw_small", out_shape=[_sds(item[0].shape) for item in items for _ in range(3)],
        compiler_params=_cp())(*flat)
    return [tuple(outs[3 * k:3 * k + 3]) for k in range(n)]


def _adamw(w, g, m, v, name):
    r, c = w.shape[-2:]
    tr = _tile(r, 256, 8)
    if w.ndim == 3:
        spec = pl.BlockSpec((None, tr, c), lambda i: (0, i, 0))
    else:
        spec = pl.BlockSpec((tr, c), lambda i: (i, 0))

    def body(w_ref, g_ref, m_ref, v_ref, d_ref, nm_ref, nv_ref):
        d_ref[...], nm_ref[...], nv_ref[...] = _adamw_math(w_ref[...], g_ref[...], m_ref[...], v_ref[...])

    return pl.pallas_call(
        body, name=name, grid=(r // tr,), in_specs=[spec] * 4, out_specs=[spec] * 3,
        out_shape=[_sds(w.shape)] * 3, compiler_params=_cp("parallel"))(w, g, m, v)


_ANY = pl.BlockSpec(memory_space=pl.ANY)


def _place():
    return lax.axis_index("x"), lax.axis_index("y"), lax.axis_index("c")


def _other_chips(x, y):
    return [(1 - x, y), (x, 1 - y), (1 - x, 1 - y)]


def _remote(src, dst, send_sems, recv_sems, k, to):
    return pltpu.make_async_remote_copy(src_ref=src, dst_ref=dst, send_sem=send_sems.at[k],
                                        recv_sem=recv_sems.at[k], device_id=to, device_id_type=MESH)


def _gather_weights(pack):
    rows, width = pack.shape
    half = rows // 2

    def body(x_ref, out_ref, send_sems, recv_sems):
        x, y, c = _place()
        sibling = (x, y, 1 - c)
        chips = _other_chips(x, y)

        def block(chip, part):
            return out_ref.at[2 * chip[0] + chip[1], pl.ds(part * half, half), :]

        first = [_remote(x_ref.at[pl.ds(c * half, half), :], block((x, y), c), send_sems, recv_sems, j, (*chip, c))
                 for j, chip in enumerate(chips)]
        for cp in first:
            cp.start()
        passed = [_remote(block(chip, c), block(chip, c), send_sems, recv_sems, 3 + j, sibling)
                  for j, chip in enumerate(chips)]
        for j, chip in enumerate(chips):
            _remote(block(chip, c), block(chip, c), send_sems, recv_sems, j, sibling).wait_recv()
            passed[j].start()
        for j, chip in enumerate(chips):
            _remote(block(chip, 1 - c), block(chip, 1 - c), send_sems, recv_sems, 3 + j, sibling).wait_recv()
        for cp in first + passed:
            cp.wait_send()

    return pl.pallas_call(
        body, name="gather_weights", in_specs=[_ANY], out_specs=_ANY,
        out_shape=_sds((N_SHARD, rows, width), pack.dtype),
        scratch_shapes=[pltpu.SemaphoreType.DMA((6,)), pltpu.SemaphoreType.DMA((6,))],
    )(pack)


_HBM = pl.BlockSpec(memory_space=pltpu.HBM)
_SEMS = pl.BlockSpec(memory_space=pltpu.SEMAPHORE)
_DATAFLOW = pltpu.SideEffectType.DATAFLOW_SIDE_EFFECTING


def _fetch_start(pack, after):
    def body(x_ref, land_ref, after_ref, send_sems, recv_sems, x_thru, land_thru, token):
        x, y, c = _place()
        for j, chip in enumerate(_other_chips(x, y)):
            _remote(x_ref, land_ref.at[2 * x + y], send_sems, recv_sems, j, (*chip, c)).start()
        token[...] = jnp.zeros_like(token)

    land = lax.empty((N_SHARD,) + pack.shape, pack.dtype)
    return pl.pallas_call(
        body, name="fetch_ffn_start",
        out_shape=(pltpu.SemaphoreType.DMA((3,)), pltpu.SemaphoreType.DMA((3,)), pltpu.HBM(pack.shape, pack.dtype),
                   pltpu.HBM(land.shape, land.dtype), _sds((8, 128))),
        in_specs=(_HBM, _HBM, _ANY), out_specs=(_SEMS, _SEMS, _HBM, _HBM, pl.BlockSpec(memory_space=pltpu.VMEM)),
        input_output_aliases={0: 2, 1: 3}, compiler_params=pltpu.CompilerParams(has_side_effects=_DATAFLOW),
    )(pltpu.with_memory_space_constraint(pack, pltpu.HBM), pltpu.with_memory_space_constraint(land, pltpu.HBM), after)


def _fetch_wait(send_sems, recv_sems, pack_thru, land_thru, after):
    def body(x_ref, land_ref, send_sems, recv_sems, after_ref, x_dead, got_ref):
        x, y, c = _place()
        for j, chip in enumerate(_other_chips(x, y)):
            cp = _remote(x_ref, land_ref.at[2 * chip[0] + chip[1]], send_sems, recv_sems, j, (*chip, c))
            cp.wait_send()
            cp.wait_recv()

    return pl.pallas_call(
        body, name="fetch_ffn_wait",
        out_shape=(pltpu.HBM(pack_thru.shape, pack_thru.dtype), pltpu.HBM(land_thru.shape, land_thru.dtype)),
        in_specs=(_HBM, _HBM, _SEMS, _SEMS, _ANY), out_specs=(_HBM, _HBM), input_output_aliases={0: 0, 1: 1},
        compiler_params=pltpu.CompilerParams(has_side_effects=_DATAFLOW),
    )(pack_thru, land_thru, send_sems, recv_sems, after)[1]


def _swap_with_sibling(block, name):
    def body(x_ref, out_ref, send_sems, recv_sems):
        x, y, c = _place()
        cp = _remote(x_ref, out_ref, send_sems, recv_sems, 0, (x, y, 1 - c))
        cp.start()
        cp.wait()

    return pl.pallas_call(
        body, name=name, in_specs=[_ANY], out_specs=_ANY, out_shape=_sds(block.shape, block.dtype),
        scratch_shapes=[pltpu.SemaphoreType.DMA((1,)), pltpu.SemaphoreType.DMA((1,))],
    )(block)


def _swap_other_half(packed, tag):
    slots, rows, width = packed.shape
    half = rows // 2

    def body(x_ref, out_ref, send_sems, recv_sems):
        x, y, c = _place()
        cp = _remote(x_ref.at[:, pl.ds((1 - c) * half, half), :], out_ref, send_sems, recv_sems, 0, (x, y, 1 - c))
        cp.start()
        cp.wait()

    return pl.pallas_call(
        body, name="swap_halves_" + tag, in_specs=[_ANY], out_specs=_ANY, out_shape=_sds((slots, half, width)),
        scratch_shapes=[pltpu.SemaphoreType.DMA((1,)), pltpu.SemaphoreType.DMA((1,))],
    )(packed)


def _add_halves(packed, got, c, tag):
    slots, rows, width = packed.shape
    half = rows // 2
    tr = _tile(half, 408, 16)
    per = half // tr
    block = (None, tr, width)

    def body(c_ref, mine_ref, got_ref, sum_ref, sum16_ref):
        acc = mine_ref[...] + got_ref[...]
        sum_ref[...] = acc
        sum16_ref[...] = acc.astype(BF16)

    plain = pl.BlockSpec(block, lambda s, i, c_ref: (s, i, 0))
    grid_spec = pltpu.PrefetchScalarGridSpec(
        num_scalar_prefetch=1, grid=(slots, per),
        in_specs=[pl.BlockSpec(block, lambda s, i, c_ref: (s, c_ref[0] * per + i, 0)), plain],
        out_specs=[plain, plain])
    return pl.pallas_call(
        body, name="add_halves_" + tag, grid_spec=grid_spec,
        out_shape=[_sds((slots, half, width)), _sds((slots, half, width), BF16)],
        compiler_params=_cp("parallel", "parallel"))(c.reshape(1).astype(jnp.int32), packed, got)


def _add_quarters(chip_sum, others, chip, tag):
    _, rows, width = chip_sum.shape
    tr = _tile(rows, 408, 16)

    def body(chip_ref, own_ref, others_ref, o_ref):
        acc = own_ref[...]
        for j in range(3):
            acc = acc + others_ref[j].astype(F32)
        o_ref[...] = acc

    grid_spec = pltpu.PrefetchScalarGridSpec(
        num_scalar_prefetch=1, grid=(rows // tr,),
        in_specs=[pl.BlockSpec((None, tr, width), lambda i, chip_ref: (chip_ref[0], i, 0)),
                  pl.BlockSpec((3, tr, width), lambda i, chip_ref: (0, i, 0))],
        out_specs=pl.BlockSpec((tr, width), lambda i, chip_ref: (i, 0)))
    return pl.pallas_call(
        body, name="add_quarters_" + tag, grid_spec=grid_spec, out_shape=_sds((rows, width)),
        compiler_params=_cp("parallel"))(chip.reshape(1).astype(jnp.int32), chip_sum, others)


def _exchange_start(parts, tag):
    _, rows, width = parts.shape

    def body(x_ref, land_ref, send_sems, recv_sems, x_thru, land_thru, token):
        x, y, c = _place()
        for j, chip in enumerate(_other_chips(x, y)):
            _remote(x_ref.at[2 * chip[0] + chip[1]], land_ref.at[j], send_sems, recv_sems, j, (*chip, c)).start()
        token[...] = jnp.zeros_like(token)

    land = lax.empty((3, rows, width), parts.dtype)
    return pl.pallas_call(
        body, name="exchange_" + tag + "_start",
        out_shape=(pltpu.SemaphoreType.DMA((3,)), pltpu.SemaphoreType.DMA((3,)), pltpu.HBM(parts.shape, parts.dtype),
                   pltpu.HBM(land.shape, land.dtype), _sds((8, 128))),
        in_specs=(_HBM, _HBM), out_specs=(_SEMS, _SEMS, _HBM, _HBM, pl.BlockSpec(memory_space=pltpu.VMEM)),
        input_output_aliases={0: 2, 1: 3}, compiler_params=pltpu.CompilerParams(has_side_effects=_DATAFLOW),
    )(pltpu.with_memory_space_constraint(parts, pltpu.HBM), pltpu.with_memory_space_constraint(land, pltpu.HBM))


def _exchange_wait(send_sems, recv_sems, parts_thru, land_thru, after, tag):
    def body(x_ref, land_ref, send_sems, recv_sems, after_ref, x_dead, got_ref):
        x, y, c = _place()
        for j, chip in enumerate(_other_chips(x, y)):
            cp = _remote(x_ref.at[2 * chip[0] + chip[1]], land_ref.at[j], send_sems, recv_sems, j, (*chip, c))
            cp.wait_send()
            cp.wait_recv()

    return pl.pallas_call(
        body, name="exchange_" + tag + "_wait",
        out_shape=(pltpu.HBM(parts_thru.shape, parts_thru.dtype), pltpu.HBM(land_thru.shape, land_thru.dtype)),
        in_specs=(_HBM, _HBM, _SEMS, _SEMS, _ANY), out_specs=(_HBM, _HBM), input_output_aliases={0: 0, 1: 1},
        compiler_params=pltpu.CompilerParams(has_side_effects=_DATAFLOW),
    )(parts_thru, land_thru, send_sems, recv_sems, after)[1]


def _allreduce_small(vec):
    rows, width = vec.shape
    vmem = pl.BlockSpec(memory_space=pltpu.VMEM)

    def body(x_ref, o_ref, buf_ref, send_sems, recv_sems):
        x, y, c = _place()
        me = 4 * x + 2 * y + c
        buf_ref[me] = x_ref[...]
        copies = []
        for k in range(1, N_DEV):
            peer = (x ^ ((k >> 2) & 1), y ^ ((k >> 1) & 1), c ^ (k & 1))
            copies.append(_remote(x_ref, buf_ref.at[me], send_sems, recv_sems, k - 1, peer))
        for cp in copies:
            cp.start()
        for k in range(1, N_DEV):
            _remote(x_ref, buf_ref.at[me ^ k], send_sems, recv_sems, k - 1, (x, y, c)).wait_recv()
        for cp in copies:
            cp.wait_send()
        total = buf_ref[0]
        for d in range(1, N_DEV):
            total = total + buf_ref[d]
        o_ref[...] = total

    return pl.pallas_call(
        body, name="allreduce_small", in_specs=[vmem], out_specs=vmem, out_shape=_sds((rows, width)),
        scratch_shapes=[pltpu.VMEM((N_DEV, rows, width), F32), pltpu.SemaphoreType.DMA((N_DEV - 1,)),
                        pltpu.SemaphoreType.DMA((N_DEV - 1,))],
    )(vec)


def _rows1024(a):
    return a.reshape(-1, 1024)


def _pad_rows(a, rows):
    return jnp.concatenate([a, jnp.zeros((rows - a.shape[0], a.shape[1]), a.dtype)], axis=0)


_TRANSPOSED = ("w_in", "w_gate", "w_up")
_SMALL_SHARDED = ("w_up_f", "w_up_b", "a_up_f", "a_up_b", "g_up")
_BIG_SHARDED = ("w_in", "w_out", "w_gate", "w_up", "w_down")


def _pack_weight_shards(w):
    conv_bits = lax.bitcast_convert_type(w["conv_w"], BF16).reshape(1, -1)
    conv_row = jnp.concatenate([conv_bits, jnp.zeros((1, 1024 - conv_bits.shape[1]), BF16)], axis=1)

    def rows(name):
        a = w[name].astype(BF16)
        return a.T if name in _TRANSPOSED else _rows1024(a)

    early = _pad_rows(jnp.concatenate([rows(name) for name, _ in _EARLY_ROWS[:-1]] + [conv_row], axis=0), EARLY_R)
    return early, jnp.concatenate([rows(name) for name, _ in _FFN_ROWS], axis=0)


def _split_rows(gathered, layout):
    out, row = {}, 0
    for name, n in layout:
        out[name] = gathered[:, row:row + n]
        row += n
    return out


def _unpack_early(gathered):
    out = _split_rows(gathered, _EARLY_ROWS)
    cols = lambda a, k: jnp.concatenate([a[s].reshape(k, -1) for s in range(N_SHARD)], axis=1)
    conv = lax.bitcast_convert_type(out["conv_w"][:, 0, :768].reshape(N_SHARD, 3, 128, 2), F32)
    full = dict(w_in=out["w_in"].reshape(-1, 1024).T, w_out=out["w_out"].reshape(D_MODEL, D_MODEL),
                conv_w=jnp.concatenate([conv[s] for s in range(N_SHARD)], axis=1))
    full.update({name: cols(out[name], D_GATE if name == "g_up" else D_LORA) for name in _SMALL_SHARDED})
    return full


def _unpack_ffn(gathered):
    out = _split_rows(gathered, _FFN_ROWS)
    return dict(w_gate=out["w_gate"].reshape(-1, 1024).T, w_up=out["w_up"].reshape(-1, 1024).T,
                w_down=out["w_down"].reshape(D_FF, D_MODEL))


def _pack_grads(g, layout, rows):
    col_split = lambda a, s: a[:, s * (a.shape[1] // N_SHARD):(s + 1) * (a.shape[1] // N_SHARD)]
    row_split = lambda a, s: a[s * (a.shape[0] // N_SHARD):(s + 1) * (a.shape[0] // N_SHARD)]
    by_rows = {name: (g[name].T if name in _TRANSPOSED else g[name]) for name, _ in layout if name in _BIG_SHARDED}
    used = sum(n for _, n in layout)
    parts = []
    for s in range(N_SHARD):
        for name, _ in layout:
            if name in _BIG_SHARDED:
                parts.append(row_split(by_rows[name], s))
            elif name in _SMALL_SHARDED:
                parts.append(_rows1024(col_split(g[name], s)))
            else:
                conv = col_split(g["conv_w"], s).reshape(1, -1)
                parts.append(jnp.concatenate([conv, jnp.zeros((1, 1024 - conv.shape[1]), F32)], axis=1))
        if rows > used:
            parts.append(jnp.zeros((rows - used, 1024), F32))
    return jnp.concatenate(parts, axis=0).reshape(N_SHARD, rows, 1024)


def _unpack_grad_shard(pack, layout):
    small_shapes = {name: (D_GATE if name == "g_up" else D_LORA, 128) for name in _SMALL_SHARDED}
    out, row = {}, 0
    for name, n in layout:
        piece = pack[row:row + n]
        if name == "conv_w":
            out[name] = piece[0, :384].reshape(3, 128)
        else:
            out[name] = piece.T if name in _TRANSPOSED else piece.reshape(small_shapes.get(name, piece.shape))
        row += n
    return out


_SMALL_LAYOUT = (("norm1_w", 1024), ("mu_shift", D_SHIFT), ("w0_f", 512), ("w0_b", 512), ("a0_f", 512),
                 ("a0_b", 512), ("k_k", 512), ("k_a_f", 512), ("k_a_b", 512), ("r_k_f", 512), ("r_k_b", 512),
                 ("gn_w", 512), ("gn_b", 512), ("norm2_w", 1024), ("norm_f_w", 1024), ("loss", 1))


def _pack_small(vals):
    rows = []
    for name, n in _SMALL_LAYOUT:
        flat = vals[name].reshape(-1)
        n_rows = -(-n // 1024)
        rows.append(jnp.concatenate([flat, jnp.zeros((n_rows * 1024 - n,), F32)]).reshape(n_rows, 1024))
    return _pad_rows(jnp.concatenate(rows, axis=0), SMALL_ROWS)


def _unpack_small(pack):
    out, row = {}, 0
    for name, n in _SMALL_LAYOUT:
        n_rows = -(-n // 1024)
        out[name] = pack[row:row + n_rows].reshape(-1)[:n]
        row += n_rows
    return out


_WEIGHTS = ("norm1_w", "w_in", "mu_shift", "w_up_f", "w0_f", "w_up_b", "w0_b", "a_up_f", "a0_f", "a_up_b", "a0_b",
            "g_up", "k_k", "k_a_f", "k_a_b", "r_k_f", "r_k_b", "gn_w", "gn_b", "conv_w", "w_out", "norm2_w",
            "w_gate", "w_up", "w_down", "norm_f_w")


def _train_step(x, loss_target, w, m, v):
    batch, seq, _ = x.shape
    t = batch * seq
    tm = _tile(seq, 256, 8)
    xs = x.reshape(t, D_MODEL)
    target = loss_target.reshape(t, D_MODEL)
    vec = lambda name: w[name].reshape(1, -1)

    local = {name: w[name][0] for name, _ in _PACK_ROWS}
    c = lax.axis_index("c")
    chip = 2 * lax.axis_index("x") + lax.axis_index("y")
    early, ffn_pack = _pack_weight_shards(local)
    early_all = lax.dynamic_update_slice(_gather_weights(early), early[None], (chip, 0, 0))
    ffn_send, ffn_recv, ffn_pack, ffn_land, token = _fetch_start(ffn_pack, early_all)
    full = _unpack_early(early_all)
    w_in = full["w_in"]
    w_shift = jnp.concatenate([w_in[:, :D_SHIFT], jnp.zeros((D_MODEL, D_SHIFT_PAD - D_SHIFT), BF16)], axis=1)
    w_conv = w_in[:, D_SHIFT:]
    zeros_lora = jnp.zeros((D_LORA, D_RWKV), F32)
    lora = lambda name: full[name].astype(F32)
    mats = (jnp.concatenate([lora("w_up_f"), zeros_lora]), jnp.concatenate([zeros_lora, lora("a_up_f")]),
            jnp.concatenate([lora("w_up_b"), zeros_lora]), jnp.concatenate([zeros_lora, lora("a_up_b")]),
            jnp.concatenate([lora("g_up"), jnp.zeros((D_GATE_PAD - D_GATE, D_RWKV), F32)]))
    mu = jnp.concatenate([vec("mu_shift"), jnp.zeros((1, D_SHIFT_PAD - D_SHIFT), F32)], axis=1)
    mu = jnp.broadcast_to(mu, (GROUP, D_SHIFT_PAD))
    zero_row = jnp.zeros((1, D_RWKV), F32)
    pvec = jnp.concatenate([vec("k_k"), vec("w0_f"), vec("a0_f"), vec("k_a_f"), vec("w0_b"), vec("a0_b"),
                            vec("k_a_b"), zero_row], axis=0)
    qvec = jnp.concatenate([vec("gn_w"), vec("gn_b"), vec("r_k_f"), vec("r_k_b"), full["conv_w"], zero_row], axis=0)
    ones_blocks = _head_ones()

    h1_t, p_shift, pc = _norm_in_proj(xs, vec("norm1_w") + token[0, 0], w_shift, w_conv, tm)
    ps, kk, w_f, kd_f, b_f, w_b, kd_b, b_b, gate = _shift_prep_fwd(p_shift, mu, pvec, mats, ones_blocks, tm, seq)
    dirs = ((w_f, kd_f, b_f), (w_b, kd_b, b_b))
    y_f, y_b, hist_f, hist_b = _scan_fwd(ps, kk, dirs, batch, seq)
    mixed, mixed_t = _post_fwd(y_f, y_b, ps, kd_f, kd_b, gate, pc, qvec, ones_blocks, tm, seq)
    x1, h2, h2_t = _out_proj_norm(mixed, full["w_out"], xs, vec("norm2_w"), tm)
    ffn_all = _fetch_wait(ffn_send, ffn_recv, ffn_pack, ffn_land, h2)
    full.update(_unpack_ffn(lax.dynamic_update_slice(ffn_all, ffn_pack[None], (chip, 0, 0))))
    ff_gate, ff_up, act, act_t = _ffn_in(h2, full["w_gate"], full["w_up"])
    d_x2, d_x2_bf16, d_norm_f, loss_part = _ffn_out_loss(act, full["w_down"], x1, w["norm_f_w"].reshape(1, -1),
                                                         target, tm)

    g = {}
    g["w_down"] = _matmul(act_t, d_x2_bf16, mode="nn", name="ffn_down_dw")
    d_gate, d_up = _ffn_in_bwd(d_x2_bf16, full["w_down"], ff_gate, ff_up)
    g["w_gate"] = _matmul(h2_t, d_gate, mode="nn", name="ffn_gate_dw")
    g["w_up"] = _matmul(h2_t, d_up, mode="nn", name="ffn_up_dw")
    ffn_grads = _pack_grads(g, _FFN_ROWS, sum(n for _, n in _FFN_ROWS))
    ffn_sum, ffn_sum_bf16 = _add_halves(ffn_grads, _swap_other_half(ffn_grads, "ffn"), c, "ffn")
    ex_send, ex_recv, ffn_sum_bf16, ex_land, ex_token = _exchange_start(ffn_sum_bf16, "ffn")
    d_x1, d_norm2 = _proj_norm_bwd(d_gate, full["w_gate"], d_up, full["w_up"], x1, vec("norm2_w") + ex_token[0, 0],
                                   d_x2, tm, "ffn_in_dx_norm2_bwd")
    d_mixed = _matmul(d_x1, full["w_out"], mode="nt", name="out_proj_dx")
    g["w_out"] = _matmul(mixed_t, d_x1, mode="nn", name="out_proj_dw")
    dy, dr_o, dkdf_o, dkdb_o, dv_o, d_gatev, d_pc, d_qvec = _post_bwd(
        d_mixed, y_f, y_b, ps, kd_f, kd_b, gate, pc, qvec, ones_blocks, tm, seq)
    (dr_f, dw_f, dkd_f, dv_f, dkk_f, db_f), (dr_b, dw_b, dkd_b, dv_b, dkk_b, db_b) = _scan_bwd(
        ps, kk, dirs, dy, hist_f, hist_b, batch, seq)
    cts = [[dr_f, dr_b, dr_o], [dv_f, dv_b, dv_o], [dkk_f, dkk_b], [dw_f], [dkd_f, dkdf_o], [db_f],
           [dw_b], [dkd_b, dkdb_o], [db_b], [d_gatev]]
    q, d_pvec, d_m0, d_m1, d_m2, d_m3, d_m4 = _prep_bwd(ps, pvec, mats, ones_blocks, cts, tm)
    d_pshift, d_mu = _shift_bwd(q, p_shift, mu, tm, seq)
    d_w_shift = _matmul(h1_t, d_pshift, mode="nn", name="in_proj_shift_dw")
    d_w_conv = _matmul(h1_t, d_pc, mode="nn", name="in_proj_conv_dw")
    g["w_in"] = jnp.concatenate([d_w_shift[:, :D_SHIFT], d_w_conv], axis=1)
    d_x, d_norm1 = _proj_norm_bwd(d_pshift, w_shift, d_pc, w_conv, xs, vec("norm1_w"), d_x1, tm,
                                  "in_proj_dx_norm1_bwd")
    g["w_up_f"], g["a_up_f"] = d_m0[:D_LORA], d_m1[D_LORA:]
    g["w_up_b"], g["a_up_b"] = d_m2[:D_LORA], d_m3[D_LORA:]
    g["g_up"] = d_m4[:D_GATE]
    g["conv_w"] = d_qvec[4:7]

    def finish(chip_sum, others, tag, layout):
        eighth = _add_quarters(chip_sum, others, chip, tag)
        other_eighth = _swap_with_sibling(eighth, "swap_eighths_" + tag)
        return _unpack_grad_shard(jnp.concatenate([jnp.where(c == 0, eighth, other_eighth),
                                                   jnp.where(c == 0, other_eighth, eighth)], axis=0), layout)

    as2d = lambda name: (1, w[name].shape[0]) if w[name].ndim == 1 else w[name].shape
    operands = lambda name: tuple(a.reshape(as2d(name)) for a in (w[name], grads[name], m[name], v[name]))

    packed = _pack_grads(g, _EARLY_ROWS, EARLY_R)
    mix_sum, mix_sum_bf16 = _add_halves(packed, _swap_other_half(packed, "mixer"), c, "mixer")
    mx_send, mx_recv, mix_sum_bf16, mx_land, mx_token = _exchange_start(mix_sum_bf16, "mixer")
    grads = finish(ffn_sum, _exchange_wait(ex_send, ex_recv, ffn_sum_bf16, ex_land, mx_token, "ffn"), "ffn", _FFN_ROWS)
    updates = {name: _adamw(*operands(name), "adamw_" + name) for name in _FFN_NAMES}
    mix_others = _exchange_wait(mx_send, mx_recv, mix_sum_bf16, mx_land, updates["w_down"][2], "mixer")
    grads.update(finish(mix_sum, mix_others, "mixer", _EARLY_ROWS))

    small = dict(norm1_w=d_norm1, mu_shift=d_mu[:, :D_SHIFT], w0_f=d_pvec[1], w0_b=d_pvec[4], a0_f=d_pvec[2],
                 a0_b=d_pvec[5], k_k=d_pvec[0], k_a_f=d_pvec[3], k_a_b=d_pvec[6], r_k_f=d_qvec[2], r_k_b=d_qvec[3],
                 gn_w=d_qvec[0], gn_b=d_qvec[1], norm2_w=d_norm2, norm_f_w=d_norm_f, loss=loss_part)
    reduced = _unpack_small(_allreduce_small(_pack_small(small)))
    loss = reduced.pop("loss")[0]
    grads.update(reduced)

    outs = {}
    small = [name for name in _WEIGHTS if name not in _BIG_SHARDED]
    updates.update(zip(small, _adamw_small([operands(name) for name in small])))
    for name in ("w_in", "w_out"):
        updates[name] = _adamw(*operands(name), "adamw_" + name)
    for name in _WEIGHTS:
        shape = w[name].shape
        outs[name] = (grads[name].reshape(shape),) + tuple(a.reshape(shape) for a in updates[name])
    d_x = d_x.reshape(batch, seq, D_MODEL)
    return (loss, d_x) + tuple(outs[name][k] for k in range(4) for name in _WEIGHTS)


def kernel(x, norm1_w, w_in, mu_shift, w_up_f, w0_f, w_up_b, w0_b, a_up_f, a0_f, a_up_b, a0_b, g_up, k_k, k_a_f, k_a_b, r_k_f, r_k_b, gn_w, gn_b, conv_w, w_out, norm2_w, w_gate, w_up, w_down, norm_f_w, loss_target, m_norm1_w, m_w_in, m_mu_shift, m_w_up_f, m_w0_f, m_w_up_b, m_w0_b, m_a_up_f, m_a0_f, m_a_up_b, m_a0_b, m_g_up, m_k_k, m_k_a_f, m_k_a_b, m_r_k_f, m_r_k_b, m_gn_w, m_gn_b, m_conv_w, m_w_out, m_norm2_w, m_w_gate, m_w_up, m_w_down, m_norm_f_w, v_norm1_w, v_w_in, v_mu_shift, v_w_up_f, v_w0_f, v_w_up_b, v_w0_b, v_a_up_f, v_a0_f, v_a_up_b, v_a0_b, v_g_up, v_k_k, v_k_a_f, v_k_a_b, v_r_k_f, v_r_k_b, v_gn_w, v_gn_b, v_conv_w, v_w_out, v_norm2_w, v_w_gate, v_w_up, v_w_down, v_norm_f_w):
    args = locals()
    w = {name: args[name] for name in _WEIGHTS}
    m = {name: args["m_" + name] for name in _WEIGHTS}
    v = {name: args["v_" + name] for name in _WEIGHTS}
    return _train_step(x, loss_target, w, m, v)
```

```python
import functools

import jax
import jax.numpy as jnp
from jax import lax
from jax.experimental import pallas as pl
from jax.experimental.pallas import tpu as pltpu

F32 = jnp.float32
BF16 = jnp.bfloat16
MESH = pl.DeviceIdType.MESH

D_MODEL = 1024
D_RWKV = 512
HEAD = 64
N_PAIR = D_RWKV // (2 * HEAD)
D_LORA = 64
D_GATE = 160
D_GATE_PAD = 384
D_FF = 2816
D_SHIFT = 1824
D_SHIFT_PAD = 2048
D_CONV3 = 1536
LOG_DECAY_SCALE = 0.606531
RMS_EPS = 1e-6
GN_EPS = 64e-5
NORM_EPS = 1e-12
ADAM_LR, ADAM_B1, ADAM_B2, ADAM_EPS, ADAM_WD, ADAM_STEP = 0.001, 0.9, 0.999, 1e-08, 0.01, 10

N_SHARD = 4
N_DEV = 8
V7X_VMEM_LIMIT = 48 * 1024 * 1024
SCAN_CHUNK = 64
GROUP = 8

_PACK_ROWS = (("w_in", 840), ("w_out", 256), ("w_gate", 704), ("w_up", 704), ("w_down", 704),
              ("w_up_f", 8), ("w_up_b", 8), ("a_up_f", 8), ("a_up_b", 8), ("g_up", 20), ("conv_w", 1))
_FFN_NAMES = ("w_gate", "w_up", "w_down")
_EARLY_ROWS = tuple(item for item in _PACK_ROWS if item[0] not in _FFN_NAMES)
_FFN_ROWS = tuple(item for item in _PACK_ROWS if item[0] in _FFN_NAMES)
EARLY_R = 1152
SMALL_ROWS = 24


def _tile(n, cap, mult=128):
    best = None
    t = mult
    while t <= min(n, cap):
        if n % t == 0:
            best = t
        t += mult
    return best or n


def _cp(*sem):
    return pltpu.CompilerParams(dimension_semantics=sem or None, vmem_limit_bytes=V7X_VMEM_LIMIT)


def _sds(shape, dtype=F32):
    return jax.ShapeDtypeStruct(shape, dtype)


def _matmul(a, b, *, mode, name, out_dtype=F32, add=None):
    m, kdim = a.shape
    n = b.shape[1] if mode == "nn" else b.shape[0]
    tm, tn = _tile(m, 768, 8), _tile(n, 1536)
    tk = kdim if kdim <= 4096 else _tile(kdim, 2048)
    nk = kdim // tk
    a_spec = pl.BlockSpec((tm, tk), lambda i, j, k: (i, k))
    if mode == "nn":
        b_spec = pl.BlockSpec((tk, tn), lambda i, j, k: (k, j))
        dims = (((1,), (0,)), ((), ()))
    else:
        b_spec = pl.BlockSpec((tn, tk), lambda i, j, k: (j, k))
        dims = (((1,), (1,)), ((), ()))
    has_add = add is not None

    def body(*refs):
        a_ref, b_ref = refs[0], refs[1]
        add_ref = refs[2] if has_add else None
        o_ref = refs[3] if has_add else refs[2]
        part = lax.dot_general(a_ref[...].astype(BF16), b_ref[...].astype(BF16), dims,
                               preferred_element_type=F32)
        if nk == 1:
            if has_add:
                part = part + add_ref[...]
            o_ref[...] = part.astype(out_dtype)
        else:
            acc_ref = refs[-1]
            k = pl.program_id(2)

            @pl.when(k == 0)
            def _():
                acc_ref[...] = jnp.zeros_like(acc_ref)

            acc_ref[...] += part

            @pl.when(k == nk - 1)
            def _():
                res = acc_ref[...]
                if has_add:
                    res = res + add_ref[...]
                o_ref[...] = res.astype(out_dtype)

    o_spec = pl.BlockSpec((tm, tn), lambda i, j, k: (i, j))
    in_specs = [a_spec, b_spec] + ([o_spec] if has_add else [])
    args = (a, b) + ((add,) if has_add else ())
    return pl.pallas_call(
        body, name=name, grid=(m // tm, n // tn, nk), in_specs=in_specs, out_specs=o_spec,
        out_shape=_sds((m, n), out_dtype),
        scratch_shapes=[pltpu.VMEM((tm, tn), F32)] if nk > 1 else [],
        compiler_params=_cp("parallel", "parallel", "arbitrary"),
    )(*args)


def _row(tm, width):
    return pl.BlockSpec((tm, width), lambda i: (i, 0))


def _col(tm, height):
    return pl.BlockSpec((height, tm), lambda i: (0, i))


def _fixed(shape):
    return pl.BlockSpec(shape, lambda i: tuple(0 for _ in shape))


def _rmsnorm_tile(xv, wv):
    return xv * lax.rsqrt(jnp.mean(xv * xv, axis=-1, keepdims=True) + RMS_EPS) * wv


def _norm_in_proj(x, w, w_shift, w_conv, tm):
    t, d = x.shape
    n_a, n_b = w_shift.shape[1], w_conv.shape[1]

    def body(x_ref, w_ref, wa_ref, wb_ref, ht_ref, pa_ref, pb_ref):
        hv = _rmsnorm_tile(x_ref[...], w_ref[...])
        ht_ref[...] = jnp.transpose(hv).astype(BF16)
        hb = hv.astype(BF16)
        pa_ref[...] = jnp.dot(hb, wa_ref[...], preferred_element_type=F32)
        pb_ref[...] = jnp.dot(hb, wb_ref[...], preferred_element_type=F32)

    return pl.pallas_call(
        body, name="norm1_in_proj", grid=(t // tm,),
        in_specs=[_row(tm, d), _fixed((1, d)), _fixed((d, n_a)), _fixed((d, n_b))],
        out_specs=[_col(tm, d), _row(tm, n_a), _row(tm, n_b)],
        out_shape=[_sds((d, t), BF16), _sds((t, n_a)), _sds((t, n_b))],
        compiler_params=_cp("parallel"))(x, w, w_shift, w_conv)


def _out_proj_norm(mixed, w_out, res, w, tm):
    t, d = res.shape

    def body(m_ref, wo_ref, r_ref, w_ref, x_ref, h_ref, ht_ref):
        xv = r_ref[...] + jnp.dot(m_ref[...], wo_ref[...], preferred_element_type=F32)
        x_ref[...] = xv
        hv = _rmsnorm_tile(xv, w_ref[...])
        h_ref[...] = hv.astype(BF16)
        ht_ref[...] = jnp.transpose(hv).astype(BF16)

    return pl.pallas_call(
        body, name="out_proj_norm2", grid=(t // tm,),
        in_specs=[_row(tm, d), _fixed((d, d)), _row(tm, d), _fixed((1, d))],
        out_specs=[_row(tm, d), _row(tm, d), _col(tm, d)],
        out_shape=[_sds((t, d)), _sds((t, d), BF16), _sds((d, t), BF16)],
        compiler_params=_cp("parallel"))(mixed, w_out, res, w)


def _rms_bwd_math(xv, wv, dyv):
    rstd = lax.rsqrt(jnp.mean(xv * xv, axis=-1, keepdims=True) + RMS_EPS)
    xhat = xv * rstd
    gv = dyv * wv
    dx = rstd * (gv - xhat * jnp.mean(gv * xhat, axis=-1, keepdims=True))
    return dx, jnp.sum(dyv * xhat, axis=0, keepdims=True)


def _proj_norm_bwd(dy_a, w_a, dy_b, w_b, x, w, dres, tm, name, w_before=None):
    t, d = x.shape
    ka, kb = dy_a.shape[1], dy_b.shape[1]
    nt = (((1,), (1,)), ((), ()))
    chained = w_before is not None

    def body(dya_ref, wa_ref, dyb_ref, wb_ref, x_ref, w_ref, dres_ref, *refs):
        dx_ref, dw_ref = refs[-3:-1] if chained else refs[-2:]
        d_h = (lax.dot_general(dya_ref[...], wa_ref[...], nt, preferred_element_type=F32)
               + lax.dot_general(dyb_ref[...], wb_ref[...], nt, preferred_element_type=F32))
        dx, dw = _rms_bwd_math(x_ref[...], w_ref[...], d_h)
        dx = dres_ref[...] + dx
        dx_ref[...] = dx
        if chained:
            refs[-1][...] = lax.dot_general(dx.astype(BF16), refs[0][...], nt, preferred_element_type=F32)

        @pl.when(pl.program_id(0) == 0)
        def _():
            dw_ref[...] = jnp.zeros_like(dw_ref)

        dw_ref[...] += dw

    extra_in = [_fixed(w_before.shape)] if chained else []
    extra_out = [_row(tm, w_before.shape[0])] if chained else []
    return pl.pallas_call(
        body, name=name, grid=(t // tm,),
        in_specs=[_row(tm, ka), _fixed((d, ka)), _row(tm, kb), _fixed((d, kb)), _row(tm, d), _fixed((1, d)),
                  _row(tm, d)] + extra_in,
        out_specs=[_row(tm, d), _fixed((1, d))] + extra_out,
        out_shape=[_sds((t, d)), _sds((1, d))] + ([_sds((t, w_before.shape[0]))] if chained else []),
        compiler_params=_cp("arbitrary"))(dy_a, w_a, dy_b, w_b, x, w, dres, *([w_before] if chained else []))


def _ffn_out_loss(act, w_down, x1, w, target, tm):
    t, d = x1.shape
    f = act.shape[1]

    def body(a_ref, wd_ref, x_ref, w_ref, t_ref, dx_ref, dx16_ref, dw_ref, loss_ref):
        xv = x_ref[...] + jnp.dot(a_ref[...], wd_ref[...], preferred_element_type=F32)
        wv = w_ref[...]
        rstd = lax.rsqrt(jnp.mean(xv * xv, axis=-1, keepdims=True) + RMS_EPS)
        err = xv * rstd * wv - t_ref[...]
        dx, dw = _rms_bwd_math(xv, wv, err * (1.0 / d))
        dx_ref[...] = dx
        dx16_ref[...] = dx.astype(BF16)

        @pl.when(pl.program_id(0) == 0)
        def _():
            dw_ref[...] = jnp.zeros_like(dw_ref)
            loss_ref[...] = jnp.zeros_like(loss_ref)

        dw_ref[...] += dw
        loss_ref[...] += 0.5 * jnp.sum(jnp.mean(err * err, axis=-1, keepdims=True), axis=0, keepdims=True)

    return pl.pallas_call(
        body, name="ffn_out_loss", grid=(t // tm,),
        in_specs=[_row(tm, f), _fixed((f, d)), _row(tm, d), _fixed((1, d)), _row(tm, d)],
        out_specs=[_row(tm, d), _row(tm, d), _fixed((1, d)), _fixed((1, 1))],
        out_shape=[_sds((t, d)), _sds((t, d), BF16), _sds((1, d)), _sds((1, 1))],
        compiler_params=_cp("arbitrary"))(act, w_down, x1, w, target)


def _ffn_in(h, w_gate, w_up):
    t, d = h.shape
    f = w_gate.shape[1]
    tm, tn = _tile(t, 512, 8), _tile(f, 1536)

    def body(h_ref, wg_ref, wu_ref, g_ref, u_ref, a_ref, at_ref):
        hv = h_ref[...]
        gv = jnp.dot(hv, wg_ref[...], preferred_element_type=F32)
        uv = jnp.dot(hv, wu_ref[...], preferred_element_type=F32)
        act = gv * jax.nn.sigmoid(gv) * uv
        g_ref[...] = gv.astype(BF16)
        u_ref[...] = uv.astype(BF16)
        a_ref[...] = act.astype(BF16)
        at_ref[...] = jnp.transpose(act).astype(BF16)

    w_spec = pl.BlockSpec((d, tn), lambda i, j: (0, j))
    o_spec = pl.BlockSpec((tm, tn), lambda i, j: (i, j))
    return pl.pallas_call(
        body, name="ffn_in", grid=(t // tm, f // tn),
        in_specs=[pl.BlockSpec((tm, d), lambda i, j: (i, 0)), w_spec, w_spec],
        out_specs=[o_spec, o_spec, o_spec, pl.BlockSpec((tn, tm), lambda i, j: (j, i))],
        out_shape=[_sds((t, f), BF16)] * 3 + [_sds((f, t), BF16)],
        compiler_params=_cp("parallel", "parallel"))(h, w_gate, w_up)


def _ffn_in_bwd(d_out, w_down, gate, up):
    t, d = d_out.shape
    f = w_down.shape[0]
    tm, tn = _tile(t, 512, 8), _tile(f, 1536)

    def body(do_ref, w_ref, g_ref, u_ref, dg_ref, du_ref):
        dv = lax.dot_general(do_ref[...].astype(BF16), w_ref[...], (((1,), (1,)), ((), ())),
                             preferred_element_type=F32)
        gv, uv = g_ref[...].astype(F32), u_ref[...].astype(F32)
        sg = jax.nn.sigmoid(gv)
        du_ref[...] = (dv * gv * sg).astype(BF16)
        dg_ref[...] = (dv * uv * (sg * (1.0 + gv * (1.0 - sg)))).astype(BF16)

    tile = pl.BlockSpec((tm, tn), lambda i, j: (i, j))
    return pl.pallas_call(
        body, name="ffn_in_bwd", grid=(t // tm, f // tn),
        in_specs=[pl.BlockSpec((tm, d), lambda i, j: (i, 0)), pl.BlockSpec((tn, d), lambda i, j: (j, 0)), tile, tile],
        out_specs=[tile, tile], out_shape=[_sds((t, f), BF16)] * 2,
        compiler_params=_cp("parallel", "parallel"))(d_out, w_down, gate, up)


def _halo_specs(tm, width, rows_total):
    per = tm // GROUP
    last = rows_total // GROUP - 1
    prev = pl.BlockSpec((GROUP, width), lambda i: (jnp.maximum(i * per - 1, 0), 0))
    nxt = pl.BlockSpec((GROUP, width), lambda i: (jnp.minimum((i + 1) * per, last), 0))
    return prev, nxt


def _edge_flags(tm, seq):
    i = pl.program_id(0)
    has_prev = jnp.where((i * tm) % seq == 0, 0.0, 1.0).astype(F32)
    has_next = jnp.where(((i + 1) * tm) % seq == 0, 0.0, 1.0).astype(F32)
    return has_prev, has_next


def _shifted(xv, prev_row, next_row):
    tm = xv.shape[0]
    row = lax.broadcasted_iota(jnp.int32, xv.shape, 0)
    down = jnp.where(row == 0, prev_row, pltpu.roll(xv, 1, axis=0))
    up = jnp.where(row == tm - 1, next_row, pltpu.roll(xv, tm - 1, axis=0))
    return down, up


def _shift_bwd(q, p, mu, tm, seq):
    t, w = p.shape
    prev_spec, next_spec = _halo_specs(tm, w, t)

    def body(q_ref, qp_ref, qn_ref, p_ref, pp_ref, pn_ref, mu_ref, dp_ref, dmu_ref):
        has_prev, has_next = _edge_flags(tm, seq)
        muv = mu_ref[0:1, :]
        qv = q_ref[...]
        mq = muv * qv
        mq_down, mq_up = _shifted(mq, muv * qp_ref[GROUP - 1:GROUP, :] * has_prev,
                                  muv * qn_ref[0:1, :] * has_next)
        dp_ref[...] = (qv - mq + 0.5 * (mq_down + mq_up)).astype(BF16)
        pv = p_ref[...]
        p_down, p_up = _shifted(pv, pp_ref[GROUP - 1:GROUP, :] * has_prev, pn_ref[0:1, :] * has_next)

        @pl.when(pl.program_id(0) == 0)
        def _():
            dmu_ref[...] = jnp.zeros_like(dmu_ref)

        dmu_ref[...] += jnp.sum(qv * (0.5 * (p_down + p_up) - pv), axis=0, keepdims=True)

    return pl.pallas_call(
        body, name="shift_bwd", grid=(t // tm,),
        in_specs=[_row(tm, w), prev_spec, next_spec, _row(tm, w), prev_spec, next_spec, _fixed((GROUP, w))],
        out_specs=[_row(tm, w), _fixed((1, w))],
        out_shape=[_sds((t, w), BF16), _sds((1, w))], compiler_params=_cp("arbitrary"))(q, q, q, p, p, p, mu)


@jax.custom_vjp
def _bdot(a, b):
    return jnp.dot(a.astype(BF16), b.astype(BF16), preferred_element_type=F32)


def _bdot_fwd(a, b):
    return _bdot(a, b), (a, b)


def _bdot_bwd(res, g):
    a, b = res
    gb = g.astype(BF16)
    da = lax.dot_general(gb, b.astype(BF16), (((1,), (1,)), ((), ())), preferred_element_type=F32)
    db = lax.dot_general(a.astype(BF16), gb, (((0,), (0,)), ((), ())), preferred_element_type=F32)
    return da, db


_bdot.defvjp(_bdot_fwd, _bdot_bwd)


def _seg_raw(x, ones_blocks):
    hi = x.astype(BF16)
    lo = (x - hi.astype(F32)).astype(BF16)
    return (jnp.dot(hi, ones_blocks, preferred_element_type=F32)
            + jnp.dot(lo, ones_blocks, preferred_element_type=F32))


@jax.custom_vjp
def _seg(x, ones_blocks):
    return _seg_raw(x, ones_blocks)


def _seg_fwd(x, ones_blocks):
    return _seg_raw(x, ones_blocks), ones_blocks


def _seg_bwd(ones_blocks, g):
    return _seg_raw(g, ones_blocks), jnp.zeros_like(ones_blocks)


_seg.defvjp(_seg_fwd, _seg_bwd)


def _head_ones():
    h = jnp.arange(D_RWKV) // HEAD
    return (h[:, None] == h[None, :]).astype(BF16)


def _prep_math(ps, k_k, w0_f, a0_f, k_a_f, w0_b, a0_b, k_a_b, wup_f, aup_f, wup_b, aup_b, gup, ones_blocks):
    r = ps[:, 0:512]
    k = ps[:, 512:1024]
    v = ps[:, 1024:1536]
    xwa = ps[:, 1536:1664]
    xg = ps[:, 1664:D_SHIFT_PAD]
    kk_raw = k * k_k
    norm = jnp.sqrt(_seg(kk_raw * kk_raw, ones_blocks))
    kk = kk_raw / jnp.maximum(norm, NORM_EPS)
    t_xwa = jnp.tanh(xwa)
    outs = [r, v, kk]
    for w0, a0, k_a, wup, aup in ((w0_f, a0_f, k_a_f, wup_f, aup_f), (w0_b, a0_b, k_a_b, wup_b, aup_b)):
        decay = jnp.exp(-LOG_DECAY_SCALE * jax.nn.sigmoid(w0 + _bdot(t_xwa, wup)))
        rate = jax.nn.sigmoid(a0 + _bdot(xwa, aup))
        outs += [decay, k * (1.0 + (rate - 1.0) * k_a), kk * rate]
    outs.append(_bdot(jax.nn.sigmoid(xg), gup))
    return tuple(outs)


def _prep_args(tm, ps_tile, pv_ref, mat_refs, ones_ref):
    vecs = [jnp.broadcast_to(pv_ref[j:j + 1, :], (tm, D_RWKV)) for j in range(7)]
    return [ps_tile] + vecs + [m[...] for m in mat_refs] + [ones_ref[...]]


_PREP_MAT_SHAPES = ((128, D_RWKV),) * 4 + ((D_GATE_PAD, D_RWKV),)


def _shift_prep_fwd(p, mu, pvec, mats, ones_blocks, tm, seq):
    t, w = p.shape
    prev_spec, next_spec = _halo_specs(tm, w, t)

    def body(p_ref, hp_ref, hn_ref, mu_ref, pv_ref, m0, m1, m2, m3, m4, ones_ref, ps_ref, *out_refs):
        has_prev, has_next = _edge_flags(tm, seq)
        xv = p_ref[...]
        down, up = _shifted(xv, hp_ref[GROUP - 1:GROUP, :] * has_prev, hn_ref[0:1, :] * has_next)
        ps_tile = xv + mu_ref[0:1, :] * (0.5 * (down + up) - xv)
        ps_ref[...] = ps_tile
        outs = _prep_math(*_prep_args(tm, ps_tile, pv_ref, (m0, m1, m2, m3, m4), ones_ref))
        for o_ref, val in zip(out_refs, outs[2:]):
            o_ref[...] = val

    return pl.pallas_call(
        body, name="shift_prep_fwd", grid=(t // tm,),
        in_specs=[_row(tm, w), prev_spec, next_spec, _fixed((GROUP, w)), _fixed((8, D_RWKV))]
        + [_fixed(s) for s in _PREP_MAT_SHAPES] + [_fixed((D_RWKV, D_RWKV))],
        out_specs=[_row(tm, w)] + [_row(tm, D_RWKV)] * 8, out_shape=[_sds((t, w))] + [_sds((t, D_RWKV))] * 8,
        compiler_params=_cp("parallel"))(p, p, p, mu, pvec, *mats, ones_blocks)


def _prep_bwd(ps, pvec, mats, ones_blocks, cts, tm):
    t = ps.shape[0]
    counts = [len(c) for c in cts]
    flat = [a for c in cts for a in c]

    def body(ps_ref, pv_ref, m0, m1, m2, m3, m4, ones_ref, *refs):
        ct_refs = refs[:len(flat)]
        q_ref, dpv_ref = refs[len(flat)], refs[len(flat) + 1]
        dmat_refs = refs[len(flat) + 2:]
        args = _prep_args(tm, ps_ref[...], pv_ref, (m0, m1, m2, m3, m4), ones_ref)
        _, vjp = jax.vjp(lambda *a: _prep_math(*a, args[-1]), *args[:-1])
        ct_vals, pos = [], 0
        for n in counts:
            val = ct_refs[pos][...]
            for extra in ct_refs[pos + 1:pos + n]:
                val = val + extra[...]
            ct_vals.append(val)
            pos += n
        grads = vjp(tuple(ct_vals))
        q_ref[...] = grads[0]

        @pl.when(pl.program_id(0) == 0)
        def _():
            dpv_ref[...] = jnp.zeros_like(dpv_ref)
            for d_ref in dmat_refs:
                d_ref[...] = jnp.zeros_like(d_ref)

        for j in range(7):
            dpv_ref[j:j + 1, :] += jnp.sum(grads[1 + j], axis=0, keepdims=True)
        for d_ref, gm in zip(dmat_refs, grads[8:13]):
            d_ref[...] += gm

    return pl.pallas_call(
        body, name="prep_bwd", grid=(t // tm,),
        in_specs=[_row(tm, D_SHIFT_PAD), _fixed((8, D_RWKV))] + [_fixed(s) for s in _PREP_MAT_SHAPES]
        + [_fixed((D_RWKV, D_RWKV))] + [_row(tm, D_RWKV)] * len(flat),
        out_specs=[_row(tm, D_SHIFT_PAD), _fixed((8, D_RWKV))] + [_fixed(s) for s in _PREP_MAT_SHAPES],
        out_shape=[_sds((t, D_SHIFT_PAD)), _sds((8, D_RWKV))] + [_sds(s) for s in _PREP_MAT_SHAPES],
        compiler_params=_cp("arbitrary"))(ps, pvec, *mats, ones_blocks, *flat)


def _pair_ones():
    h = jnp.arange(2 * HEAD) // HEAD
    block = (h[:, None] == h[None, :]).astype(BF16)
    return jnp.concatenate([block, block], axis=0)


def _diag_mask():
    lane = lax.broadcasted_iota(jnp.int32, (HEAD, 2 * HEAD), 1)
    sub = lax.broadcasted_iota(jnp.int32, (HEAD, 2 * HEAD), 0)
    return jnp.where((lane & (HEAD - 1)) == sub, 1.0, 0.0).astype(F32)


def _to_row(cols, dmask):
    return jnp.sum(cols * dmask, axis=0, keepdims=True)


def _seg_many(exact, rounded, ones_pair):
    out_exact, out_rounded = [], []
    if exact:
        parts = []
        for x in exact:
            hi = x.astype(BF16)
            parts.append(jnp.concatenate([hi, (x - hi.astype(F32)).astype(BF16)], axis=1))
        res = jnp.dot(jnp.concatenate(parts, axis=0), ones_pair, preferred_element_type=F32)
        out_exact = [res[HEAD * c:HEAD * (c + 1)] for c in range(len(exact))]
    if rounded:
        res = jnp.dot(jnp.concatenate([x.astype(BF16) for x in rounded], axis=0), ones_pair[0:2 * HEAD],
                      preferred_element_type=F32)
        out_rounded = [res[HEAD * c:HEAD * (c + 1)] for c in range(len(rounded))]
    return out_exact, out_rounded


def _seg_rows(pieces, ones_pair):
    parts = []
    for x in pieces:
        hi = x.astype(BF16)
        parts.append(jnp.concatenate([hi, (x - hi.astype(F32)).astype(BF16)], axis=1))
    res = jnp.dot(jnp.concatenate(parts, axis=0), ones_pair, preferred_element_type=F32)
    return [res[2 * GROUP * c:2 * GROUP * (c + 1)] for c in range(len(pieces))]


N_CHAIN = 2 * N_PAIR


def _chain(c):
    d, p = divmod(c, N_PAIR)
    return d, slice(2 * HEAD * p, 2 * HEAD * (p + 1))


def _scan_specs(n_chunks, col_blocks, fwd_chunk, bwd_chunk):
    def spec(chunk_of, col):
        return pl.BlockSpec((SCAN_CHUNK, D_RWKV), lambda b, g: (b * n_chunks + chunk_of(g), col))
    return [spec(fwd_chunk, c) for c in col_blocks] + [spec(bwd_chunk, c) for c in col_blocks]


def _scan_fwd(ps, kk, dirs, batch, seq):
    t = batch * seq
    n = seq // SCAN_CHUNK
    groups = SCAN_CHUNK // GROUP
    up = lambda g: g
    down = lambda g: n - 1 - g
    col_blocks = (0, 2, 0, 0, 0, 0)

    def body(*refs):
        dir_refs = (refs[0:6], refs[6:12])
        ones_ref = refs[12]
        y_refs, hist_refs, st_ref = refs[13:15], refs[15:17], refs[17]

        @pl.when(pl.program_id(1) == 0)
        def _():
            st_ref[...] = jnp.zeros_like(st_ref)

        ones_pair = ones_ref[...]
        dmask = _diag_mask()
        dmask_b = dmask.astype(BF16)
        sub8 = lax.broadcasted_iota(jnp.int32, (GROUP, 2 * HEAD), 0)

        def group(gi, carry):
            off = (pl.multiple_of(gi * GROUP, GROUP), pl.multiple_of((groups - 1 - gi) * GROUP, GROUP))
            loaded = [tuple(ref[pl.ds(off[d], GROUP), :] for ref in dir_refs[d]) for d in range(2)]
            states = list(carry)
            y_acc = [jnp.zeros((GROUP, 2 * HEAD), F32) for _ in range(N_CHAIN)]
            with_next = []
            for d in range(2):
                _, _, kk8, w8, kd8, b8 = loaded[d]
                kk_next = pltpu.roll(kk8, GROUP - 1 if d == 0 else 1, axis=0)
                with_next.append((b8 * kk_next, kd8 * kk_next, w8 * kk_next))
            row_dots = _seg_rows([jnp.concatenate([with_next[c // N_PAIR][0][:, _chain(c)[1]],
                                                   with_next[c // N_PAIR][1][:, _chain(c)[1]]], axis=0)
                                  for c in range(N_CHAIN)], ones_pair)
            for step in range(0, GROUP, 2):
                rows0, rows1, first, second, w_kk = [], [], [], [], []
                for c in range(N_CHAIN):
                    d, lanes = _chain(c)
                    i0, i1 = (step, step + 1) if d == 0 else (GROUP - 1 - step, GROUP - 2 - step)
                    first.append(i0)
                    second.append(i1)
                    rows0.append(tuple(x8[i0:i0 + 1, lanes] for x8 in loaded[d]))
                    rows1.append(tuple(x8[i1:i1 + 1, lanes] for x8 in loaded[d]))
                    w_kk.append(with_next[d][2][i0:i0 + 1, lanes])
                    hist_refs[d][c % N_PAIR, gi * GROUP + step] = states[c]
                _, v_cols = _seg_many([], [dmask_b * r[c][1].astype(BF16) for r in (rows0, rows1)
                                           for c in range(N_CHAIN)], ones_pair)
                sums, _ = _seg_many([x for c in range(N_CHAIN)
                                     for x in (states[c] * rows0[c][2], states[c] * w_kk[c])], [], ones_pair)
                mid = []
                for c in range(N_CHAIN):
                    d = c // N_PAIR
                    sa0, carried = sums[2 * c], sums[2 * c + 1]
                    b_dot = row_dots[c][first[c]:first[c] + 1]
                    kd_dot = row_dots[c][GROUP + first[c]:GROUP + first[c] + 1]
                    sa1 = carried - sa0 * b_dot + v_cols[c] * kd_dot
                    _, _, _, w0, kd0, b0 = rows0[c]
                    _, _, _, w1, kd1, b1 = rows1[c]
                    state0 = states[c] * w0 - sa0 * b0 + v_cols[c] * kd0
                    hist_refs[d][c % N_PAIR, gi * GROUP + step + 1] = state0
                    states[c] = state0 * w1 - sa1 * b1 + v_cols[N_CHAIN + c] * kd1
                    mid.append(state0)
                _, ys = _seg_many([], [mid[c] * rows0[c][0] for c in range(N_CHAIN)]
                                  + [states[c] * rows1[c][0] for c in range(N_CHAIN)], ones_pair)
                for c in range(N_CHAIN):
                    y_acc[c] = jnp.where(sub8 == first[c], _to_row(ys[c], dmask), y_acc[c])
                    y_acc[c] = jnp.where(sub8 == second[c], _to_row(ys[N_CHAIN + c], dmask), y_acc[c])
            for c in range(N_CHAIN):
                d, lanes = _chain(c)
                y_refs[d][pl.ds(off[d], GROUP), lanes] = y_acc[c]
            return tuple(states)

        final = lax.fori_loop(0, groups, group, tuple(st_ref[c] for c in range(N_CHAIN)))
        for c in range(N_CHAIN):
            st_ref[c] = final[c]
            hist_refs[c // N_PAIR][c % N_PAIR, SCAN_CHUNK] = final[c]

    y_spec_f = pl.BlockSpec((SCAN_CHUNK, D_RWKV), lambda b, g: (b * n + up(g), 0))
    y_spec_b = pl.BlockSpec((SCAN_CHUNK, D_RWKV), lambda b, g: (b * n + down(g), 0))
    hist_shape = (batch, n, N_PAIR, SCAN_CHUNK + 1, HEAD, 2 * HEAD)
    hist_block = (None, None, N_PAIR, SCAN_CHUNK + 1, HEAD, 2 * HEAD)
    hist_spec_f = pl.BlockSpec(hist_block, lambda b, g: (b, up(g), 0, 0, 0, 0))
    hist_spec_b = pl.BlockSpec(hist_block, lambda b, g: (b, down(g), 0, 0, 0, 0))
    ones_spec = pl.BlockSpec((4 * HEAD, 2 * HEAD), lambda b, g: (0, 0))
    (wf, kdf, bf), (wb, kdb, bb) = dirs
    return pl.pallas_call(
        body, name="wkv_fwd", grid=(batch, n),
        in_specs=_scan_specs(n, col_blocks, up, down) + [ones_spec],
        out_specs=[y_spec_f, y_spec_b, hist_spec_f, hist_spec_b],
        out_shape=[_sds((t, D_RWKV)), _sds((t, D_RWKV)), _sds(hist_shape), _sds(hist_shape)],
        scratch_shapes=[pltpu.VMEM((N_CHAIN, HEAD, 2 * HEAD), F32)],
        compiler_params=_cp("parallel", "arbitrary"),
    )(ps, ps, kk, wf, kdf, bf, ps, ps, kk, wb, kdb, bb, _pair_ones())


def _scan_bwd(ps, kk, dirs, dy, hist_f, hist_b, batch, seq):
    t = batch * seq
    n = seq // SCAN_CHUNK
    groups = SCAN_CHUNK // GROUP
    fwd_chunk = lambda g: n - 1 - g
    bwd_chunk = lambda g: g
    col_blocks = (0, 2, 0, 0, 0, 0, 0)

    def undo_group(dir_refs, out_refs, hist_refs, gi, d_states, ones_pair, dmask, sub8):
        d_states = list(d_states)
        loaded, blocks = [], []
        for d in range(2):
            blk = groups - 1 - gi if d == 0 else gi
            blocks.append(pl.ds(pl.multiple_of(blk * GROUP, GROUP), GROUP))
            r8, v8, kk8, w8, kd8, b8, dy8 = (ref[blocks[d], :] for ref in dir_refs[d])
            loaded.append((r8, v8, kk8, w8, kd8, -b8, dy8))
        acc = [[jnp.zeros((GROUP, 2 * HEAD), F32) for _ in range(6)] for _ in range(N_CHAIN)]
        for step in range(GROUP):
            rows, idx, before, after = [], [], [], []
            for c in range(N_CHAIN):
                d, lanes = _chain(c)
                i = GROUP - 1 - step if d == 0 else step
                q = (groups - 1 - gi) * GROUP + i if d == 0 else SCAN_CHUNK - 1 - (gi * GROUP + i)
                idx.append(i)
                rows.append(tuple(x8[i:i + 1, lanes] for x8 in loaded[d]))
                before.append(hist_refs[d][c % N_PAIR, q])
                after.append(hist_refs[d][c % N_PAIR, q + 1])
            _, cols = _seg_many([], [dmask.astype(BF16) * rows[c][j].astype(BF16) for c in range(N_CHAIN) for j in (1, 6)],
                                ones_pair)
            v_cols, dy_cols = cols[0::2], cols[1::2]
            d_now = [d_states[c] + dy_cols[c] * rows[c][0] for c in range(N_CHAIN)]
            d_sas, _ = _seg_many([d_now[c] * rows[c][5] for c in range(N_CHAIN)], [], ones_pair)
            _, others = _seg_many(
                [], [x for c in range(N_CHAIN) for x in (before[c] * rows[c][2], d_now[c] * rows[c][4])], ones_pair)
            for c in range(N_CHAIN):
                sa, d_sa, dv_cols = others[2 * c], d_sas[c], others[2 * c + 1]
                rows_out = (
                    jnp.sum(after[c] * dy_cols[c], axis=0, keepdims=True),
                    jnp.sum(d_now[c] * before[c], axis=0, keepdims=True),
                    jnp.sum(d_now[c] * v_cols[c], axis=0, keepdims=True),
                    _to_row(dv_cols, dmask),
                    jnp.sum(before[c] * d_sa, axis=0, keepdims=True),
                    -jnp.sum(d_now[c] * sa, axis=0, keepdims=True),
                )
                acc[c] = [jnp.where(sub8 == idx[c], val, a) for val, a in zip(rows_out, acc[c])]
                d_states[c] = d_now[c] * rows[c][3] + d_sa * rows[c][2]
        for c in range(N_CHAIN):
            d, lanes = _chain(c)
            for o_ref, val in zip(out_refs[d], acc[c]):
                o_ref[blocks[d], lanes] = val
        return tuple(d_states)

    def body(*refs):
        dir_refs = (refs[0:7], refs[7:14])
        hist_refs, ones_ref = refs[14:16], refs[16]
        out_refs = (refs[17:23], refs[23:29])
        dst_ref = refs[29]

        @pl.when(pl.program_id(1) == 0)
        def _():
            dst_ref[...] = jnp.zeros_like(dst_ref)

        ones_pair = ones_ref[...]
        dmask = _diag_mask()
        sub8 = lax.broadcasted_iota(jnp.int32, (GROUP, 2 * HEAD), 0)

        def group(gi, carry):
            return undo_group(dir_refs, out_refs, hist_refs, gi, carry, ones_pair, dmask, sub8)

        final = lax.fori_loop(0, groups, group, tuple(dst_ref[c] for c in range(N_CHAIN)))
        for c in range(N_CHAIN):
            dst_ref[c] = final[c]

    blk = (SCAN_CHUNK, D_RWKV)
    out_f = pl.BlockSpec(blk, lambda b, g: (b * n + fwd_chunk(g), 0))
    out_b = pl.BlockSpec(blk, lambda b, g: (b * n + bwd_chunk(g), 0))
    hist_block = (None, None, N_PAIR, SCAN_CHUNK + 1, HEAD, 2 * HEAD)
    hist_spec_f = pl.BlockSpec(hist_block, lambda b, g: (b, fwd_chunk(g), 0, 0, 0, 0))
    hist_spec_b = pl.BlockSpec(hist_block, lambda b, g: (b, bwd_chunk(g), 0, 0, 0, 0))
    ones_spec = pl.BlockSpec((4 * HEAD, 2 * HEAD), lambda b, g: (0, 0))
    (wf, kdf, bf), (wb, kdb, bb) = dirs
    outs = pl.pallas_call(
        body, name="wkv_bwd", grid=(batch, n),
        in_specs=_scan_specs(n, col_blocks, fwd_chunk, bwd_chunk) + [hist_spec_f, hist_spec_b, ones_spec],
        out_specs=[out_f] * 6 + [out_b] * 6,
        out_shape=[_sds((t, D_RWKV))] * 12,
        scratch_shapes=[pltpu.VMEM((N_CHAIN, HEAD, 2 * HEAD), F32)],
        compiler_params=_cp("parallel", "arbitrary"),
    )(ps, ps, kk, wf, kdf, bf, dy, ps, ps, kk, wb, kdb, bb, dy, hist_f, hist_b, _pair_ones())
    return outs[0:6], outs[6:12]


def _post_math(y, r, kd_f, kd_b, v, gate, gn_w, gn_b, rk_f, rk_b, ones_blocks):
    mean = _seg(y, ones_blocks) * (1.0 / HEAD)
    yc = y - mean
    var = _seg(yc * yc, ones_blocks) * (1.0 / HEAD)
    yn = yc * lax.rsqrt(var + GN_EPS) * gn_w + gn_b
    bonus = _seg(r * kd_f * rk_f, ones_blocks) * v + _seg(r * kd_b * rk_b, ones_blocks) * v
    return (yn + bonus) * gate


def _conv_parts(pc, halo_prev, halo_next, has_prev, has_next):
    gate_b, gate_c, hid = pc[:, 0:512], pc[:, 512:1024], pc[:, 1024:1536]
    u = gate_c * hid
    u_prev_row = halo_prev[GROUP - 1:GROUP, 512:1024] * halo_prev[GROUP - 1:GROUP, 1024:1536] * has_prev
    u_next_row = halo_next[0:1, 512:1024] * halo_next[0:1, 1024:1536] * has_next
    u_down, u_up = _shifted(u, u_prev_row, u_next_row)
    return gate_b, gate_c, hid, u, u_down, u_up


def _post_specs(tm, t):
    pc_prev, pc_next = _halo_specs(tm, D_CONV3, t)
    col = lambda c: pl.BlockSpec((tm, D_RWKV), lambda i: (i, c))
    return ([col(0), col(0), col(0), col(0), col(0), col(2), col(0), _row(tm, D_CONV3), pc_prev, pc_next,
             _fixed((8, D_RWKV)), _fixed((D_RWKV, D_RWKV))])


def _post_fwd(y_f, y_b, ps, kd_f, kd_b, gate, pc, qvec, ones_blocks, tm, seq):
    t = ps.shape[0]

    def body(yf_ref, yb_ref, r_ref, kdf_ref, kdb_ref, v_ref, g_ref, pc_ref, hp_ref, hn_ref, qv_ref, ones_ref,
             o_ref, ot_ref):
        has_prev, has_next = _edge_flags(tm, seq)
        vec = [jnp.broadcast_to(qv_ref[j:j + 1, :], (tm, D_RWKV)) for j in range(7)]
        o_rwkv = _post_math(yf_ref[...] + yb_ref[...], r_ref[...], kdf_ref[...], kdb_ref[...], v_ref[...],
                            g_ref[...], vec[0], vec[1], vec[2], vec[3], ones_ref[...])
        gate_b, _, _, u, u_down, u_up = _conv_parts(pc_ref[...], hp_ref[...], hn_ref[...], has_prev, has_next)
        o_conv = gate_b * (vec[4] * u_down + vec[5] * u + vec[6] * u_up)
        for half, val in enumerate((o_rwkv, o_conv)):
            o_ref[:, D_RWKV * half:D_RWKV * (half + 1)] = val.astype(BF16)
            ot_ref[D_RWKV * half:D_RWKV * (half + 1), :] = jnp.transpose(val).astype(BF16)

    return pl.pallas_call(
        body, name="post_fwd", grid=(t // tm,), in_specs=_post_specs(tm, t),
        out_specs=[_row(tm, D_MODEL), _col(tm, D_MODEL)],
        out_shape=[_sds((t, D_MODEL), BF16), _sds((D_MODEL, t), BF16)], compiler_params=_cp("parallel"),
    )(y_f, y_b, ps, kd_f, kd_b, ps, gate, pc, pc, pc, qvec, ones_blocks)


def _post_bwd(d_out, y_f, y_b, ps, kd_f, kd_b, gate, pc, qvec, ones_blocks, tm, seq):
    t = ps.shape[0]
    do_prev, do_next = _halo_specs(tm, D_MODEL, t)

    def body(do_ref, dop_ref, don_ref, yf_ref, yb_ref, r_ref, kdf_ref, kdb_ref, v_ref, g_ref, pc_ref, hp_ref,
             hn_ref, qv_ref, ones_ref, dy_ref, dr_ref, dkdf_ref, dkdb_ref, dv_ref, dg_ref, dpc_ref, dqv_ref):
        has_prev, has_next = _edge_flags(tm, seq)
        vec = [jnp.broadcast_to(qv_ref[j:j + 1, :], (tm, D_RWKV)) for j in range(7)]
        ones_v = ones_ref[...]
        args = (yf_ref[...] + yb_ref[...], r_ref[...], kdf_ref[...], kdb_ref[...], v_ref[...], g_ref[...],
                vec[0], vec[1], vec[2], vec[3])
        _, vjp = jax.vjp(lambda *a: _post_math(*a, ones_v), *args)
        grads = vjp(do_ref[:, 0:D_RWKV])
        for o_ref, gval in zip((dy_ref, dr_ref, dkdf_ref, dkdb_ref, dv_ref, dg_ref), grads[0:6]):
            o_ref[...] = gval

        hp, hn = hp_ref[...], hn_ref[...]
        gate_b, gate_c, hid, u, u_down, u_up = _conv_parts(pc_ref[...], hp, hn, has_prev, has_next)
        d_oc = do_ref[:, D_RWKV:2 * D_RWKV]
        d_cu = d_oc * gate_b
        d_cu_prev = dop_ref[GROUP - 1:GROUP, D_RWKV:2 * D_RWKV] * hp[GROUP - 1:GROUP, 0:512] * has_prev
        d_cu_next = don_ref[0:1, D_RWKV:2 * D_RWKV] * hn[0:1, 0:512] * has_next
        d_cu_down, d_cu_up = _shifted(d_cu, d_cu_prev, d_cu_next)
        d_u = vec[5] * d_cu + vec[4] * d_cu_up + vec[6] * d_cu_down
        dpc_ref[:, 0:512] = (d_oc * (vec[4] * u_down + vec[5] * u + vec[6] * u_up)).astype(BF16)
        dpc_ref[:, 512:1024] = (d_u * hid).astype(BF16)
        dpc_ref[:, 1024:1536] = (d_u * gate_c).astype(BF16)

        @pl.when(pl.program_id(0) == 0)
        def _():
            dqv_ref[...] = jnp.zeros_like(dqv_ref)

        vec_grads = list(grads[6:10]) + [d_cu * u_down, d_cu * u, d_cu * u_up]
        for j, gval in enumerate(vec_grads):
            dqv_ref[j:j + 1, :] += jnp.sum(gval, axis=0, keepdims=True)

    return pl.pallas_call(
        body, name="post_bwd", grid=(t // tm,),
        in_specs=[_row(tm, D_MODEL), do_prev, do_next] + _post_specs(tm, t),
        out_specs=[_row(tm, D_RWKV)] * 6 + [_row(tm, D_CONV3), _fixed((8, D_RWKV))],
        out_shape=[_sds((t, D_RWKV))] * 6 + [_sds((t, D_CONV3), BF16), _sds((8, D_RWKV))],
        compiler_params=_cp("arbitrary"),
    )(d_out, d_out, d_out, y_f, y_b, ps, kd_f, kd_b, ps, gate, pc, pc, pc, qvec, ones_blocks)


def _adamw_math(wv, gv, mv, vv):
    m2 = ADAM_B1 * mv + (1.0 - ADAM_B1) * gv
    v2 = ADAM_B2 * vv + (1.0 - ADAM_B2) * (gv * gv)
    m_hat = m2 / (1.0 - ADAM_B1 ** ADAM_STEP)
    v_hat = v2 / (1.0 - ADAM_B2 ** ADAM_STEP)
    return -ADAM_LR * (m_hat / (jnp.sqrt(v_hat) + ADAM_EPS) + ADAM_WD * wv), m2, v2


def _adamw_small(items):
    n = len(items)

    def body(*refs):
        ins, outs = refs[:4 * n], refs[4 * n:]
        for k in range(n):
            w_ref, g_ref, m_ref, v_ref = ins[4 * k:4 * k + 4]
            for o_ref, val in zip(outs[3 * k:3 * k + 3], _adamw_math(w_ref[...], g_ref[...], m_ref[...], v_ref[...])):
                o_ref[...] = val

    flat = [a for item in items for a in item]
    outs = pl.pallas_call(
        body, name="adamw_small", out_shape=[_sds(item[0].shape) for item in items for _ in range(3)],
        compiler_params=_cp())(*flat)
    return [tuple(outs[3 * k:3 * k + 3]) for k in range(n)]


def _adamw(w, g, m, v, name):
    r, c = w.shape[-2:]
    tr = _tile(r, 256, 8)
    if w.ndim == 3:
        spec = pl.BlockSpec((None, tr, c), lambda i: (0, i, 0))
    else:
        spec = pl.BlockSpec((tr, c), lambda i: (i, 0))

    def body(w_ref, g_ref, m_ref, v_ref, d_ref, nm_ref, nv_ref):
        d_ref[...], nm_ref[...], nv_ref[...] = _adamw_math(w_ref[...], g_ref[...], m_ref[...], v_ref[...])

    return pl.pallas_call(
        body, name=name, grid=(r // tr,), in_specs=[spec] * 4, out_specs=[spec] * 3,
        out_shape=[_sds(w.shape)] * 3, compiler_params=_cp("parallel"))(w, g, m, v)


_ANY = pl.BlockSpec(memory_space=pl.ANY)


def _place():
    return lax.axis_index("x"), lax.axis_index("y"), lax.axis_index("c")


def _other_chips(x, y):
    return [(1 - x, y), (x, 1 - y), (1 - x, 1 - y)]


def _remote(src, dst, send_sems, recv_sems, k, to):
    return pltpu.make_async_remote_copy(src_ref=src, dst_ref=dst, send_sem=send_sems.at[k],
                                        recv_sem=recv_sems.at[k], device_id=to, device_id_type=MESH)


def _gather_weights(pack):
    rows, width = pack.shape
    half = rows // 2

    def body(x_ref, out_ref, send_sems, recv_sems):
        x, y, c = _place()
        sibling = (x, y, 1 - c)
        chips = _other_chips(x, y)

        def block(chip, part):
            return out_ref.at[2 * chip[0] + chip[1], pl.ds(part * half, half), :]

        first = [_remote(x_ref.at[pl.ds(c * half, half), :], block((x, y), c), send_sems, recv_sems, j, (*chip, c))
                 for j, chip in enumerate(chips)]
        for cp in first:
            cp.start()
        passed = [_remote(block(chip, c), block(chip, c), send_sems, recv_sems, 3 + j, sibling)
                  for j, chip in enumerate(chips)]
        for j, chip in enumerate(chips):
            _remote(block(chip, c), block(chip, c), send_sems, recv_sems, j, sibling).wait_recv()
            passed[j].start()
        for j, chip in enumerate(chips):
            _remote(block(chip, 1 - c), block(chip, 1 - c), send_sems, recv_sems, 3 + j, sibling).wait_recv()
        for cp in first + passed:
            cp.wait_send()

    return pl.pallas_call(
        body, name="gather_weights", in_specs=[_ANY], out_specs=_ANY,
        out_shape=_sds((N_SHARD, rows, width), pack.dtype),
        scratch_shapes=[pltpu.SemaphoreType.DMA((6,)), pltpu.SemaphoreType.DMA((6,))],
    )(pack)


_HBM = pl.BlockSpec(memory_space=pltpu.HBM)
_SEMS = pl.BlockSpec(memory_space=pltpu.SEMAPHORE)
_DATAFLOW = pltpu.SideEffectType.DATAFLOW_SIDE_EFFECTING


def _fetch_start(pack, after):
    def body(x_ref, land_ref, after_ref, send_sems, recv_sems, x_thru, land_thru, token):
        x, y, c = _place()
        for j, chip in enumerate(_other_chips(x, y)):
            _remote(x_ref, land_ref.at[2 * x + y], send_sems, recv_sems, j, (*chip, c)).start()
        token[...] = jnp.zeros_like(token)

    land = lax.empty((N_SHARD,) + pack.shape, pack.dtype)
    return pl.pallas_call(
        body, name="fetch_ffn_start",
        out_shape=(pltpu.SemaphoreType.DMA((3,)), pltpu.SemaphoreType.DMA((3,)), pltpu.HBM(pack.shape, pack.dtype),
                   pltpu.HBM(land.shape, land.dtype), _sds((8, 128))),
        in_specs=(_HBM, _HBM, _ANY), out_specs=(_SEMS, _SEMS, _HBM, _HBM, pl.BlockSpec(memory_space=pltpu.VMEM)),
        input_output_aliases={0: 2, 1: 3}, compiler_params=pltpu.CompilerParams(has_side_effects=_DATAFLOW),
    )(pltpu.with_memory_space_constraint(pack, pltpu.HBM), pltpu.with_memory_space_constraint(land, pltpu.HBM), after)


def _fetch_wait(send_sems, recv_sems, pack_thru, land_thru, after):
    def body(x_ref, land_ref, send_sems, recv_sems, after_ref, x_dead, got_ref):
        x, y, c = _place()
        for j, chip in enumerate(_other_chips(x, y)):
            cp = _remote(x_ref, land_ref.at[2 * chip[0] + chip[1]], send_sems, recv_sems, j, (*chip, c))
            cp.wait_send()
            cp.wait_recv()

    return pl.pallas_call(
        body, name="fetch_ffn_wait",
        out_shape=(pltpu.HBM(pack_thru.shape, pack_thru.dtype), pltpu.HBM(land_thru.shape, land_thru.dtype)),
        in_specs=(_HBM, _HBM, _SEMS, _SEMS, _ANY), out_specs=(_HBM, _HBM), input_output_aliases={0: 0, 1: 1},
        compiler_params=pltpu.CompilerParams(has_side_effects=_DATAFLOW),
    )(pack_thru, land_thru, send_sems, recv_sems, after)[1]


def _swap_with_sibling(block, name):
    def body(x_ref, out_ref, send_sems, recv_sems):
        x, y, c = _place()
        cp = _remote(x_ref, out_ref, send_sems, recv_sems, 0, (x, y, 1 - c))
        cp.start()
        cp.wait()

    return pl.pallas_call(
        body, name=name, in_specs=[_ANY], out_specs=_ANY, out_shape=_sds(block.shape, block.dtype),
        scratch_shapes=[pltpu.SemaphoreType.DMA((1,)), pltpu.SemaphoreType.DMA((1,))],
    )(block)


def _swap_other_half(packed, tag):
    slots, rows, width = packed.shape
    half = rows // 2

    def body(x_ref, out_ref, send_sems, recv_sems):
        x, y, c = _place()
        cp = _remote(x_ref.at[:, pl.ds((1 - c) * half, half), :], out_ref, send_sems, recv_sems, 0, (x, y, 1 - c))
        cp.start()
        cp.wait()

    return pl.pallas_call(
        body, name="swap_halves_" + tag, in_specs=[_ANY], out_specs=_ANY, out_shape=_sds((slots, half, width)),
        scratch_shapes=[pltpu.SemaphoreType.DMA((1,)), pltpu.SemaphoreType.DMA((1,))],
    )(packed)


def _add_halves(packed, got, c, tag):
    slots, rows, width = packed.shape
    half = rows // 2
    tr = _tile(half, 408, 16)
    per = half // tr
    block = (None, tr, width)

    def body(c_ref, mine_ref, got_ref, sum_ref, sum16_ref):
        acc = mine_ref[...] + got_ref[...]
        sum_ref[...] = acc
        sum16_ref[...] = acc.astype(BF16)

    plain = pl.BlockSpec(block, lambda s, i, c_ref: (s, i, 0))
    grid_spec = pltpu.PrefetchScalarGridSpec(
        num_scalar_prefetch=1, grid=(slots, per),
        in_specs=[pl.BlockSpec(block, lambda s, i, c_ref: (s, c_ref[0] * per + i, 0)), plain],
        out_specs=[plain, plain])
    return pl.pallas_call(
        body, name="add_halves_" + tag, grid_spec=grid_spec,
        out_shape=[_sds((slots, half, width)), _sds((slots, half, width), BF16)],
        compiler_params=_cp("parallel", "parallel"))(c.reshape(1).astype(jnp.int32), packed, got)


def _add_quarters(chip_sum, others, chip, tag):
    _, rows, width = chip_sum.shape
    tr = _tile(rows, 408, 16)

    def body(chip_ref, own_ref, others_ref, o_ref):
        acc = own_ref[...]
        for j in range(3):
            acc = acc + others_ref[j].astype(F32)
        o_ref[...] = acc

    grid_spec = pltpu.PrefetchScalarGridSpec(
        num_scalar_prefetch=1, grid=(rows // tr,),
        in_specs=[pl.BlockSpec((None, tr, width), lambda i, chip_ref: (chip_ref[0], i, 0)),
                  pl.BlockSpec((3, tr, width), lambda i, chip_ref: (0, i, 0))],
        out_specs=pl.BlockSpec((tr, width), lambda i, chip_ref: (i, 0)))
    return pl.pallas_call(
        body, name="add_quarters_" + tag, grid_spec=grid_spec, out_shape=_sds((rows, width)),
        compiler_params=_cp("parallel"))(chip.reshape(1).astype(jnp.int32), chip_sum, others)


def _exchange_start(parts, tag):
    _, rows, width = parts.shape

    def body(x_ref, land_ref, send_sems, recv_sems, x_thru, land_thru, token):
        x, y, c = _place()
        for j, chip in enumerate(_other_chips(x, y)):
            _remote(x_ref.at[2 * chip[0] + chip[1]], land_ref.at[j], send_sems, recv_sems, j, (*chip, c)).start()
        token[...] = jnp.zeros_like(token)

    land = lax.empty((3, rows, width), parts.dtype)
    return pl.pallas_call(
        body, name="exchange_" + tag + "_start",
        out_shape=(pltpu.SemaphoreType.DMA((3,)), pltpu.SemaphoreType.DMA((3,)), pltpu.HBM(parts.shape, parts.dtype),
                   pltpu.HBM(land.shape, land.dtype), _sds((8, 128))),
        in_specs=(_HBM, _HBM), out_specs=(_SEMS, _SEMS, _HBM, _HBM, pl.BlockSpec(memory_space=pltpu.VMEM)),
        input_output_aliases={0: 2, 1: 3}, compiler_params=pltpu.CompilerParams(has_side_effects=_DATAFLOW),
    )(pltpu.with_memory_space_constraint(parts, pltpu.HBM), pltpu.with_memory_space_constraint(land, pltpu.HBM))


def _exchange_wait(send_sems, recv_sems, parts_thru, land_thru, after, tag):
    def body(x_ref, land_ref, send_sems, recv_sems, after_ref, x_dead, got_ref):
        x, y, c = _place()
        for j, chip in enumerate(_other_chips(x, y)):
            cp = _remote(x_ref.at[2 * chip[0] + chip[1]], land_ref.at[j], send_sems, recv_sems, j, (*chip, c))
            cp.wait_send()
            cp.wait_recv()

    return pl.pallas_call(
        body, name="exchange_" + tag + "_wait",
        out_shape=(pltpu.HBM(parts_thru.shape, parts_thru.dtype), pltpu.HBM(land_thru.shape, land_thru.dtype)),
        in_specs=(_HBM, _HBM, _SEMS, _SEMS, _ANY), out_specs=(_HBM, _HBM), input_output_aliases={0: 0, 1: 1},
        compiler_params=pltpu.CompilerParams(has_side_effects=_DATAFLOW),
    )(parts_thru, land_thru, send_sems, recv_sems, after)[1]


def _allreduce_small(vec):
    rows, width = vec.shape
    vmem = pl.BlockSpec(memory_space=pltpu.VMEM)

    def body(x_ref, o_ref, buf_ref, send_sems, recv_sems):
        x, y, c = _place()
        me = 4 * x + 2 * y + c
        buf_ref[me] = x_ref[...]
        copies = []
        for k in range(1, N_DEV):
            peer = (x ^ ((k >> 2) & 1), y ^ ((k >> 1) & 1), c ^ (k & 1))
            copies.append(_remote(x_ref, buf_ref.at[me], send_sems, recv_sems, k - 1, peer))
        for cp in copies:
            cp.start()
        for k in range(1, N_DEV):
            _remote(x_ref, buf_ref.at[me ^ k], send_sems, recv_sems, k - 1, (x, y, c)).wait_recv()
        for cp in copies:
            cp.wait_send()
        total = buf_ref[0]
        for d in range(1, N_DEV):
            total = total + buf_ref[d]
        o_ref[...] = total

    return pl.pallas_call(
        body, name="allreduce_small", in_specs=[vmem], out_specs=vmem, out_shape=_sds((rows, width)),
        scratch_shapes=[pltpu.VMEM((N_DEV, rows, width), F32), pltpu.SemaphoreType.DMA((N_DEV - 1,)),
                        pltpu.SemaphoreType.DMA((N_DEV - 1,))],
    )(vec)


def _rows1024(a):
    return a.reshape(-1, 1024)


def _pad_rows(a, rows):
    return jnp.concatenate([a, jnp.zeros((rows - a.shape[0], a.shape[1]), a.dtype)], axis=0)


_TRANSPOSED = ("w_in", "w_gate", "w_up")
_SMALL_SHARDED = ("w_up_f", "w_up_b", "a_up_f", "a_up_b", "g_up")
_BIG_SHARDED = ("w_in", "w_out", "w_gate", "w_up", "w_down")


def _pack_weight_shards(w):
    conv_bits = lax.bitcast_convert_type(w["conv_w"], BF16).reshape(1, -1)
    conv_row = jnp.concatenate([conv_bits, jnp.zeros((1, 1024 - conv_bits.shape[1]), BF16)], axis=1)

    def rows(name):
        a = w[name].astype(BF16)
        return a.T if name in _TRANSPOSED else _rows1024(a)

    early = _pad_rows(jnp.concatenate([rows(name) for name, _ in _EARLY_ROWS[:-1]] + [conv_row], axis=0), EARLY_R)
    return early, jnp.concatenate([rows(name) for name, _ in _FFN_ROWS], axis=0)


def _split_rows(gathered, layout):
    out, row = {}, 0
    for name, n in layout:
        out[name] = gathered[:, row:row + n]
        row += n
    return out


def _unpack_early(gathered):
    out = _split_rows(gathered, _EARLY_ROWS)
    cols = lambda a, k: jnp.concatenate([a[s].reshape(k, -1) for s in range(N_SHARD)], axis=1)
    conv = lax.bitcast_convert_type(out["conv_w"][:, 0, :768].reshape(N_SHARD, 3, 128, 2), F32)
    full = dict(w_in=out["w_in"].reshape(-1, 1024).T, w_out=out["w_out"].reshape(D_MODEL, D_MODEL),
                conv_w=jnp.concatenate([conv[s] for s in range(N_SHARD)], axis=1))
    full.update({name: cols(out[name], D_GATE if name == "g_up" else D_LORA) for name in _SMALL_SHARDED})
    return full


def _unpack_ffn(gathered):
    out = _split_rows(gathered, _FFN_ROWS)
    return dict(w_gate=out["w_gate"].reshape(-1, 1024).T, w_up=out["w_up"].reshape(-1, 1024).T,
                w_down=out["w_down"].reshape(D_FF, D_MODEL))


def _pack_grads(g, layout, rows):
    col_split = lambda a, s: a[:, s * (a.shape[1] // N_SHARD):(s + 1) * (a.shape[1] // N_SHARD)]
    row_split = lambda a, s: a[s * (a.shape[0] // N_SHARD):(s + 1) * (a.shape[0] // N_SHARD)]
    by_rows = {name: (g[name].T if name in _TRANSPOSED else g[name]) for name, _ in layout if name in _BIG_SHARDED}
    used = sum(n for _, n in layout)
    parts = []
    for s in range(N_SHARD):
        for name, _ in layout:
            if name in _BIG_SHARDED:
                parts.append(row_split(by_rows[name], s))
            elif name in _SMALL_SHARDED:
                parts.append(_rows1024(col_split(g[name], s)))
            else:
                conv = col_split(g["conv_w"], s).reshape(1, -1)
                parts.append(jnp.concatenate([conv, jnp.zeros((1, 1024 - conv.shape[1]), F32)], axis=1))
        if rows > used:
            parts.append(jnp.zeros((rows - used, 1024), F32))
    return jnp.concatenate(parts, axis=0).reshape(N_SHARD, rows, 1024)


def _unpack_grad_shard(pack, layout):
    small_shapes = {name: (D_GATE if name == "g_up" else D_LORA, 128) for name in _SMALL_SHARDED}
    out, row = {}, 0
    for name, n in layout:
        piece = pack[row:row + n]
        if name == "conv_w":
            out[name] = piece[0, :384].reshape(3, 128)
        else:
            out[name] = piece.T if name in _TRANSPOSED else piece.reshape(small_shapes.get(name, piece.shape))
        row += n
    return out


_SMALL_LAYOUT = (("norm1_w", 1024), ("mu_shift", D_SHIFT), ("w0_f", 512), ("w0_b", 512), ("a0_f", 512),
                 ("a0_b", 512), ("k_k", 512), ("k_a_f", 512), ("k_a_b", 512), ("r_k_f", 512), ("r_k_b", 512),
                 ("gn_w", 512), ("gn_b", 512), ("norm2_w", 1024), ("norm_f_w", 1024), ("loss", 1))


def _pack_small(vals):
    rows = []
    for name, n in _SMALL_LAYOUT:
        flat = vals[name].reshape(-1)
        n_rows = -(-n // 1024)
        rows.append(jnp.concatenate([flat, jnp.zeros((n_rows * 1024 - n,), F32)]).reshape(n_rows, 1024))
    return _pad_rows(jnp.concatenate(rows, axis=0), SMALL_ROWS)


def _unpack_small(pack):
    out, row = {}, 0
    for name, n in _SMALL_LAYOUT:
        n_rows = -(-n // 1024)
        out[name] = pack[row:row + n_rows].reshape(-1)[:n]
        row += n_rows
    return out


_WEIGHTS = ("norm1_w", "w_in", "mu_shift", "w_up_f", "w0_f", "w_up_b", "w0_b", "a_up_f", "a0_f", "a_up_b", "a0_b",
            "g_up", "k_k", "k_a_f", "k_a_b", "r_k_f", "r_k_b", "gn_w", "gn_b", "conv_w", "w_out", "norm2_w",
            "w_gate", "w_up", "w_down", "norm_f_w")


def _train_step(x, loss_target, w, m, v):
    batch, seq, _ = x.shape
    t = batch * seq
    tm = _tile(seq, 256, 8)
    xs = x.reshape(t, D_MODEL)
    target = loss_target.reshape(t, D_MODEL)
    vec = lambda name: w[name].reshape(1, -1)

    local = {name: w[name][0] for name, _ in _PACK_ROWS}
    c = lax.axis_index("c")
    chip = 2 * lax.axis_index("x") + lax.axis_index("y")
    early, ffn_pack = _pack_weight_shards(local)
    early_all = lax.dynamic_update_slice(_gather_weights(early), early[None], (chip, 0, 0))
    ffn_send, ffn_recv, ffn_pack, ffn_land, token = _fetch_start(ffn_pack, early_all)
    full = _unpack_early(early_all)
    w_in = full["w_in"]
    w_shift = jnp.concatenate([w_in[:, :D_SHIFT], jnp.zeros((D_MODEL, D_SHIFT_PAD - D_SHIFT), BF16)], axis=1)
    w_conv = w_in[:, D_SHIFT:]
    zeros_lora = jnp.zeros((D_LORA, D_RWKV), F32)
    lora = lambda name: full[name].astype(F32)
    mats = (jnp.concatenate([lora("w_up_f"), zeros_lora]), jnp.concatenate([zeros_lora, lora("a_up_f")]),
            jnp.concatenate([lora("w_up_b"), zeros_lora]), jnp.concatenate([zeros_lora, lora("a_up_b")]),
            jnp.concatenate([lora("g_up"), jnp.zeros((D_GATE_PAD - D_GATE, D_RWKV), F32)]))
    mu = jnp.concatenate([vec("mu_shift"), jnp.zeros((1, D_SHIFT_PAD - D_SHIFT), F32)], axis=1)
    mu = jnp.broadcast_to(mu, (GROUP, D_SHIFT_PAD))
    zero_row = jnp.zeros((1, D_RWKV), F32)
    pvec = jnp.concatenate([vec("k_k"), vec("w0_f"), vec("a0_f"), vec("k_a_f"), vec("w0_b"), vec("a0_b"),
                            vec("k_a_b"), zero_row], axis=0)
    qvec = jnp.concatenate([vec("gn_w"), vec("gn_b"), vec("r_k_f"), vec("r_k_b"), full["conv_w"], zero_row], axis=0)
    ones_blocks = _head_ones()

    h1_t, p_shift, pc = _norm_in_proj(xs, vec("norm1_w") + token[0, 0], w_shift, w_conv, tm)
    ps, kk, w_f, kd_f, b_f, w_b, kd_b, b_b, gate = _shift_prep_fwd(p_shift, mu, pvec, mats, ones_blocks, tm, seq)
    dirs = ((w_f, kd_f, b_f), (w_b, kd_b, b_b))
    y_f, y_b, hist_f, hist_b = _scan_fwd(ps, kk, dirs, batch, seq)
    mixed, mixed_t = _post_fwd(y_f, y_b, ps, kd_f, kd_b, gate, pc, qvec, ones_blocks, tm, seq)
    x1, h2, h2_t = _out_proj_norm(mixed, full["w_out"], xs, vec("norm2_w"), tm)
    ffn_all = _fetch_wait(ffn_send, ffn_recv, ffn_pack, ffn_land, h2)
    full.update(_unpack_ffn(lax.dynamic_update_slice(ffn_all, ffn_pack[None], (chip, 0, 0))))
    ff_gate, ff_up, act, act_t = _ffn_in(h2, full["w_gate"], full["w_up"])
    d_x2, d_x2_bf16, d_norm_f, loss_part = _ffn_out_loss(act, full["w_down"], x1, w["norm_f_w"].reshape(1, -1),
                                                         target, tm)

    g = {}
    g["w_down"] = _matmul(act_t, d_x2_bf16, mode="nn", name="ffn_down_dw")
    d_gate, d_up = _ffn_in_bwd(d_x2_bf16, full["w_down"], ff_gate, ff_up)
    g["w_gate"] = _matmul(h2_t, d_gate, mode="nn", name="ffn_gate_dw")
    g["w_up"] = _matmul(h2_t, d_up, mode="nn", name="ffn_up_dw")
    ffn_grads = _pack_grads(g, _FFN_ROWS, sum(n for _, n in _FFN_ROWS))
    ffn_sum, ffn_sum_bf16 = _add_halves(ffn_grads, _swap_other_half(ffn_grads, "ffn"), c, "ffn")
    ex_send, ex_recv, ffn_sum_bf16, ex_land, ex_token = _exchange_start(ffn_sum_bf16, "ffn")
    d_x1, d_norm2, d_mixed = _proj_norm_bwd(d_gate, full["w_gate"], d_up, full["w_up"], x1,
                                            vec("norm2_w") + ex_token[0, 0], d_x2, tm, "ffn_in_dx_norm2_bwd",
                                            w_before=full["w_out"])
    g["w_out"] = _matmul(mixed_t, d_x1, mode="nn", name="out_proj_dw")
    dy, dr_o, dkdf_o, dkdb_o, dv_o, d_gatev, d_pc, d_qvec = _post_bwd(
        d_mixed, y_f, y_b, ps, kd_f, kd_b, gate, pc, qvec, ones_blocks, tm, seq)
    (dr_f, dw_f, dkd_f, dv_f, dkk_f, db_f), (dr_b, dw_b, dkd_b, dv_b, dkk_b, db_b) = _scan_bwd(
        ps, kk, dirs, dy, hist_f, hist_b, batch, seq)
    cts = [[dr_f, dr_b, dr_o], [dv_f, dv_b, dv_o], [dkk_f, dkk_b], [dw_f], [dkd_f, dkdf_o], [db_f],
           [dw_b], [dkd_b, dkdb_o], [db_b], [d_gatev]]
    q, d_pvec, d_m0, d_m1, d_m2, d_m3, d_m4 = _prep_bwd(ps, pvec, mats, ones_blocks, cts, tm)
    d_pshift, d_mu = _shift_bwd(q, p_shift, mu, tm, seq)
    d_w_shift = _matmul(h1_t, d_pshift, mode="nn", name="in_proj_shift_dw")
    d_w_conv = _matmul(h1_t, d_pc, mode="nn", name="in_proj_conv_dw")
    g["w_in"] = jnp.concatenate([d_w_shift[:, :D_SHIFT], d_w_conv], axis=1)
    d_x, d_norm1 = _proj_norm_bwd(d_pshift, w_shift, d_pc, w_conv, xs, vec("norm1_w"), d_x1, tm,
                                  "in_proj_dx_norm1_bwd")
    g["w_up_f"], g["a_up_f"] = d_m0[:D_LORA], d_m1[D_LORA:]
    g["w_up_b"], g["a_up_b"] = d_m2[:D_LORA], d_m3[D_LORA:]
    g["g_up"] = d_m4[:D_GATE]
    g["conv_w"] = d_qvec[4:7]

    def finish(chip_sum, others, tag, layout):
        eighth = _add_quarters(chip_sum, others, chip, tag)
        other_eighth = _swap_with_sibling(eighth, "swap_eighths_" + tag)
        return _unpack_grad_shard(jnp.concatenate([jnp.where(c == 0, eighth, other_eighth),
                                                   jnp.where(c == 0, other_eighth, eighth)], axis=0), layout)

    as2d = lambda name: (1, w[name].shape[0]) if w[name].ndim == 1 else w[name].shape
    operands = lambda name: tuple(a.reshape(as2d(name)) for a in (w[name], grads[name], m[name], v[name]))

    packed = _pack_grads(g, _EARLY_ROWS, EARLY_R)
    mix_sum, mix_sum_bf16 = _add_halves(packed, _swap_other_half(packed, "mixer"), c, "mixer")
    mx_send, mx_recv, mix_sum_bf16, mx_land, mx_token = _exchange_start(mix_sum_bf16, "mixer")
    grads = finish(ffn_sum, _exchange_wait(ex_send, ex_recv, ffn_sum_bf16, ex_land, mx_token, "ffn"), "ffn", _FFN_ROWS)
    updates = {name: _adamw(*operands(name), "adamw_" + name) for name in _FFN_NAMES}
    mix_others = _exchange_wait(mx_send, mx_recv, mix_sum_bf16, mx_land, updates["w_down"][2], "mixer")
    grads.update(finish(mix_sum, mix_others, "mixer", _EARLY_ROWS))

    small = dict(norm1_w=d_norm1, mu_shift=d_mu[:, :D_SHIFT], w0_f=d_pvec[1], w0_b=d_pvec[4], a0_f=d_pvec[2],
                 a0_b=d_pvec[5], k_k=d_pvec[0], k_a_f=d_pvec[3], k_a_b=d_pvec[6], r_k_f=d_qvec[2], r_k_b=d_qvec[3],
                 gn_w=d_qvec[0], gn_b=d_qvec[1], norm2_w=d_norm2, norm_f_w=d_norm_f, loss=loss_part)
    reduced = _unpack_small(_allreduce_small(_pack_small(small)))
    loss = reduced.pop("loss")[0]
    grads.update(reduced)

    outs = {}
    small = [name for name in _WEIGHTS if name not in _BIG_SHARDED]
    updates.update(zip(small, _adamw_small([operands(name) for name in small])))
    for name in ("w_in", "w_out"):
        updates[name] = _adamw(*operands(name), "adamw_" + name)
    for name in _WEIGHTS:
        shape = w[name].shape
        outs[name] = (grads[name].reshape(shape),) + tuple(a.reshape(shape) for a in updates[name])
    d_x = d_x.reshape(batch, seq, D_MODEL)
    return (loss, d_x) + tuple(outs[name][k] for k in range(4) for name in _WEIGHTS)


def kernel(x, norm1_w, w_in, mu_shift, w_up_f, w0_f, w_up_b, w0_b, a_up_f, a0_f, a_up_b, a0_b, g_up, k_k, k_a_f, k_a_b, r_k_f, r_k_b, gn_w, gn_b, conv_w, w_out, norm2_w, w_gate, w_up, w_down, norm_f_w, loss_target, m_norm1_w, m_w_in, m_mu_shift, m_w_up_f, m_w0_f, m_w_up_b, m_w0_b, m_a_up_f, m_a0_f, m_a_up_b, m_a0_b, m_g_up, m_k_k, m_k_a_f, m_k_a_b, m_r_k_f, m_r_k_b, m_gn_w, m_gn_b, m_conv_w, m_w_out, m_norm2_w, m_w_gate, m_w_up, m_w_down, m_norm_f_w, v_norm1_w, v_w_in, v_mu_shift, v_w_up_f, v_w0_f, v_w_up_b, v_w0_b, v_a_up_f, v_a0_f, v_a_up_b, v_a0_b, v_g_up, v_k_k, v_k_a_f, v_k_a_b, v_r_k_f, v_r_k_b, v_gn_w, v_gn_b, v_conv_w, v_w_out, v_norm2_w, v_w_gate, v_w_up, v_w_down, v_norm_f_w):
    args = locals()
    w = {name: args[name] for name in _WEIGHTS}
    m = {name: args["m_" + name] for name in _WEIGHTS}
    v = {name: args["v_" + name] for name in _WEIGHTS}
    return _train_step(x, loss_target, w, m, v)
```
